```python
import math
import jax, jax.numpy as jnp
from jax import lax
import numpy as np

D_MODEL = 1024
BATCH = 8
SEQ = 2048
DEPTH = 4

ROPE_THETA = 10000.0
EPS = 1e-6
LN_EPS = 1e-5
MLA_H = 8
MLA_NOPE = 64
MLA_ROPE = 32
MLA_V = 64
Q_LORA = 256
KV_LORA = 128
ATTN_BLOCK = 128
GDN_H = 4
GDN_DK = 128
GDN_DV = 128
CONV_W = 4
GDN_CHUNK = 64
ML_H = 4
ML_DK = 64
ML_DV = 128
ML_CHUNK = 64
SWA_H = 8
SWA_KV = 2
SWA_D = 64
WINDOW = 128
N_EXPERTS = 64
N_GROUPS = 8
TOPK_GROUPS = 4
TOP_K = 8
D_EXPERT = 256
D_SHARED = 256
ROUTED_SCALE = 2.5
DISPATCH_BLOCK = 128
DN_ALPHA = (2 * DEPTH) ** 0.25
DN_BETA = (8 * DEPTH) ** -0.25

EV_SIZES = (Q_LORA, KV_LORA + MLA_ROPE, GDN_H * GDN_DK, GDN_H * GDN_DK, GDN_H * GDN_DV, GDN_H, GDN_H, GDN_H * GDN_DV)
OD_SIZES = (ML_H * ML_DK, ML_H * ML_DK, ML_H * ML_DV, ML_H, ML_H, ML_H * ML_DV, SWA_H * SWA_D, SWA_KV * SWA_D, SWA_KV * SWA_D)
EV_IN = sum(EV_SIZES)
OD_IN = sum(OD_SIZES)
EV_MIX = MLA_H * MLA_V + GDN_H * GDN_DV
OD_MIX = ML_H * ML_DV + SWA_H * SWA_D

kernel_name = "hybrid_mla_gdn_mlstm_swa_moe_deepnorm"


def split_cols(t, sizes):
    return jnp.split(t, np.cumsum(sizes)[:-1].tolist(), axis=-1)


def rms_norm(t, g):
    tf = t.astype(jnp.float32)
    y = tf * lax.rsqrt(jnp.mean(tf * tf, -1, keepdims=True) + EPS)
    return (y * g.astype(jnp.float32)).astype(t.dtype)


def layer_norm(t, g, b):
    tf = t.astype(jnp.float32)
    mu = jnp.mean(tf, -1, keepdims=True)
    var = jnp.mean(jnp.square(tf - mu), -1, keepdims=True)
    y = (tf - mu) * lax.rsqrt(var + LN_EPS) * g.astype(jnp.float32) + b.astype(jnp.float32)
    return y.astype(t.dtype)


def l2_normalize(t):
    tf = t.astype(jnp.float32)
    return tf * lax.rsqrt(jnp.sum(tf * tf, -1, keepdims=True) + EPS)


def rope_tables(positions, dim):
    inv_freq = ROPE_THETA ** (-(jnp.arange(0, dim, 2, dtype=jnp.float32) / dim))
    ang = positions.astype(jnp.float32)[..., None] * inv_freq
    return jnp.cos(ang), jnp.sin(ang)


def apply_rope(t, cos, sin):
    shape = cos.shape[:2] + (1,) * (t.ndim - 3) + cos.shape[2:]
    c, s = cos.reshape(shape), sin.reshape(shape)
    tf = t.astype(jnp.float32)
    half = t.shape[-1] // 2
    t1, t2 = tf[..., :half], tf[..., half:]
    return jnp.concatenate([t1 * c - t2 * s, t2 * c + t1 * s], -1).astype(t.dtype)


def causal_dwconv(t, w):
    width, ch = w.shape
    return lax.conv_general_dilated(t, w[:, None, :].astype(t.dtype), window_strides=(1,),
                                    padding=[(width - 1, 0)], dimension_numbers=('NWC', 'WIO', 'NWC'),
                                    feature_group_count=ch)


def to_chunks(t, c):
    b, s, h = t.shape[:3]
    t = t.astype(jnp.float32).reshape((b, s // c, c, h) + t.shape[3:])
    return jnp.moveaxis(t, (1, 3), (0, 2))


def from_chunks(o):
    n, b, h, c, d = o.shape
    return jnp.moveaxis(o, (0, 2), (1, 3)).reshape(b, n * c, h, d)


def mla_attention(q_nope, q_rope, k_nope, k_rope, v):
    b, s, h, _ = q_nope.shape
    nb = s // ATTN_BLOCK
    scale = (MLA_NOPE + MLA_ROPE) ** -0.5
    kpos = jnp.arange(s)

    def blocks(t):
        return jnp.moveaxis(t.reshape((b, nb, ATTN_BLOCK) + t.shape[2:]), 1, 0)

    def one_block(args):
        qn, qr, n = args
        sc = jnp.einsum('bqhd,bkhd->bhqk', qn, k_nope) + jnp.einsum('bqhd,bkd->bhqk', qr, k_rope)
        sc = sc.astype(jnp.float32) * scale
        qpos = n * ATTN_BLOCK + jnp.arange(ATTN_BLOCK)
        sc = jnp.where(qpos[:, None] >= kpos[None, :], sc, -jnp.inf)
        p = jax.nn.softmax(sc, -1).astype(v.dtype)
        return jnp.einsum('bhqk,bkhd->bqhd', p, v)

    o = lax.map(one_block, (blocks(q_nope), blocks(q_rope), jnp.arange(nb)))
    return jnp.moveaxis(o, 0, 1).reshape(b, s, h * MLA_V)


def gated_delta_rule(q, k, v, beta, g):
    b, s, h, dk = q.shape
    dv = v.shape[-1]
    c = GDN_CHUNK
    q = to_chunks(q, c) * dk ** -0.5
    k, v, beta = to_chunks(k, c), to_chunks(v, c), to_chunks(beta, c)
    g = jnp.cumsum(to_chunks(g, c), -1)
    incl = jnp.tril(jnp.ones((c, c), bool))
    strict = jnp.tril(jnp.ones((c, c), bool), -1)
    decay = jnp.exp(jnp.where(incl, g[..., :, None] - g[..., None, :], -jnp.inf))
    kb = k * beta[..., None]
    lower = jnp.where(strict, jnp.einsum('nbhcd,nbhsd->nbhcs', kb, k) * decay, 0.0)
    eye = jnp.eye(c, dtype=jnp.float32)
    a = eye + lower
    t_inv = lax.linalg.triangular_solve(a, jnp.broadcast_to(eye, a.shape), left_side=True,
                                        lower=True, unit_diagonal=True)
    u = t_inv @ (v * beta[..., None])
    w = t_inv @ (kb * jnp.exp(g)[..., None])
    attn = jnp.einsum('nbhcd,nbhsd->nbhcs', q, k) * decay
    qg = q * jnp.exp(g)[..., None]
    kg = k * jnp.exp(g[..., -1:] - g)[..., None]
    glast = jnp.exp(g[..., -1])

    def step(state, inp):
        qg_c, kg_c, u_c, w_c, a_c, gl = inp
        v_new = u_c - w_c @ state
        o = qg_c @ state + a_c @ v_new
        state = state * gl[..., None, None] + jnp.einsum('bhcd,bhce->bhde', kg_c, v_new)
        return state, o

    _, o = lax.scan(step, jnp.zeros((b, h, dk, dv), jnp.float32), (qg, kg, u, w, attn, glast))
    return from_chunks(o)


def mlstm_chunked(q, k, v, i_pre, logf):
    b, s, h, dk = q.shape
    dv = v.shape[-1]
    c = ML_CHUNK
    incl = jnp.tril(jnp.ones((c, c), bool))
    xs = (to_chunks(q, c), to_chunks(k, c), to_chunks(v, c), to_chunks(i_pre, c), to_chunks(logf, c))

    def step(carry, inp):
        c_st, n_st, m_st = carry
        q_c, k_c, v_c, i_c, lf_c = inp
        bcum = jnp.cumsum(lf_c, -1)
        d = jnp.where(incl, bcum[..., :, None] - bcum[..., None, :] + i_c[..., None, :], -jnp.inf)
        inter = bcum + m_st[..., None]
        m_t = jnp.maximum(inter, jnp.max(d, -1))
        w_inter = jnp.exp(inter - m_t)
        p = jnp.exp(d - m_t[..., None]) * jnp.einsum('bhtd,bhsd->bhts', q_c, k_c)
        num = w_inter[..., None] * jnp.einsum('bhtd,bhde->bhte', q_c, c_st) + p @ v_c
        den = w_inter * jnp.einsum('bhtd,bhd->bht', q_c, n_st) + jnp.sum(p, -1)
        h_c = num / jnp.maximum(jnp.abs(den), jnp.exp(-m_t))[..., None]
        b_end = bcum[..., -1]
        a = b_end[..., None] - bcum + i_c
        m_new = jnp.maximum(b_end + m_st, jnp.max(a, -1))
        s_a = jnp.exp(a - m_new[..., None])
        keep = jnp.exp(b_end + m_st - m_new)
        c_st = keep[..., None, None] * c_st + jnp.einsum('bhsd,bhse->bhde', k_c * s_a[..., None], v_c)
        n_st = keep[..., None] * n_st + jnp.einsum('bhsd,bhs->bhd', k_c, s_a)
        return (c_st, n_st, m_new), h_c

    init = (jnp.zeros((b, h, dk, dv), jnp.float32), jnp.zeros((b, h, dk), jnp.float32),
            jnp.zeros((b, h), jnp.float32))
    _, o = lax.scan(step, init, xs)
    return from_chunks(o)


def swa_with_sinks(q, k, v, sinks):
    b, s, hq, d = q.shape
    grp = hq // SWA_KV
    nb = s // WINDOW
    qb = q.reshape(b, nb, WINDOW, SWA_KV, grp, d)

    def band(t):
        tb = t.reshape(b, nb, WINDOW, SWA_KV, d)
        prev = jnp.pad(tb, ((0, 0), (1, 0), (0, 0), (0, 0), (0, 0)))[:, :-1]
        return jnp.concatenate([prev, tb], axis=2)

    kb, vb = band(k), band(v)
    sc = jnp.einsum('bnqhgd,bnkhd->bnhgqk', qb, kb).astype(jnp.float32) * d ** -0.5
    qi = jnp.arange(WINDOW)[:, None]
    kj = jnp.arange(2 * WINDOW)[None, :]
    rel = qi + WINDOW - kj
    in_band = (rel >= 0) & (rel < WINDOW)
    exists = (jnp.arange(nb)[:, None, None] > 0) | (kj >= WINDOW)[None]
    mask = in_band[None] & exists
    sc = jnp.where(mask[None, :, None, None], sc, -jnp.inf)
    sink = jnp.broadcast_to(sinks.astype(jnp.float32).reshape(1, 1, SWA_KV, grp, 1, 1), sc.shape[:-1] + (1,))
    p = jax.nn.softmax(jnp.concatenate([sc, sink], -1), -1)[..., :-1].astype(v.dtype)
    o = jnp.einsum('bnhgqk,bnkhd->bnqhgd', p, vb)
    return o.reshape(b, s, hq * d)


def even_mixer(x, cos_m, sin_m, w_in, q_norm, w_qb, kv_norm, w_kvb, conv_w, a_log, dt_bias, o_norm, w_out):
    b, s, _ = x.shape
    c_q, kv_a, gq, gk, gv, g_beta, g_decay, z = split_cols(x @ w_in, EV_SIZES)
    q = (rms_norm(c_q, q_norm) @ w_qb).reshape(b, s, MLA_H, MLA_NOPE + MLA_ROPE)
    q_nope, q_rope = q[..., :MLA_NOPE], apply_rope(q[..., MLA_NOPE:], cos_m, sin_m)
    c_kv, k_rope = kv_a[..., :KV_LORA], apply_rope(kv_a[..., KV_LORA:], cos_m, sin_m)
    kv = (rms_norm(c_kv, kv_norm) @ w_kvb).reshape(b, s, MLA_H, MLA_NOPE + MLA_V)
    o_a = mla_attention(q_nope, q_rope, kv[..., :MLA_NOPE], k_rope, kv[..., MLA_NOPE:])
    qkv = jax.nn.silu(causal_dwconv(jnp.concatenate([gq, gk, gv], -1), conv_w))
    gq, gk, gv = split_cols(qkv, (GDN_H * GDN_DK, GDN_H * GDN_DK, GDN_H * GDN_DV))
    gq = l2_normalize(gq.reshape(b, s, GDN_H, GDN_DK))
    gk = l2_normalize(gk.reshape(b, s, GDN_H, GDN_DK))
    gv = gv.reshape(b, s, GDN_H, GDN_DV)
    beta = jax.nn.sigmoid(g_beta.astype(jnp.float32))
    g = -jnp.exp(a_log.astype(jnp.float32)) * jax.nn.softplus(g_decay.astype(jnp.float32) + dt_bias.astype(jnp.float32))
    o_b = gated_delta_rule(gq, gk, gv, beta, g)
    o_b = rms_norm(o_b, o_norm) * jax.nn.silu(z.reshape(b, s, GDN_H, GDN_DV).astype(jnp.float32))
    o_b = o_b.reshape(b, s, GDN_H * GDN_DV).astype(x.dtype)
    return jnp.concatenate([o_a, o_b], -1) @ w_out


def odd_mixer(x, cos_s, sin_s, w_in, b_i, b_f, ml_norm, sinks, w_out):
    b, s, _ = x.shape
    mq, mk, mv, mi, mf, mo, sq, sk, sv = split_cols(x @ w_in, OD_SIZES)
    i_pre = mi.astype(jnp.float32) + b_i.astype(jnp.float32)
    logf = jax.nn.log_sigmoid(mf.astype(jnp.float32) + b_f.astype(jnp.float32))
    h = mlstm_chunked(mq.reshape(b, s, ML_H, ML_DK), mk.reshape(b, s, ML_H, ML_DK) * ML_DK ** -0.5,
                      mv.reshape(b, s, ML_H, ML_DV), i_pre, logf)
    h = rms_norm(h, ml_norm.reshape(ML_H, ML_DV)) * jax.nn.sigmoid(mo.reshape(b, s, ML_H, ML_DV).astype(jnp.float32))
    o_c = h.reshape(b, s, ML_H * ML_DV).astype(x.dtype)
    q = apply_rope(sq.reshape(b, s, SWA_H, SWA_D), cos_s, sin_s)
    k = apply_rope(sk.reshape(b, s, SWA_KV, SWA_D), cos_s, sin_s)
    o_d = swa_with_sinks(q, k, sv.reshape(b, s, SWA_KV, SWA_D), sinks)
    return jnp.concatenate([o_c, o_d], -1) @ w_out


def moe_ffn(x, router_w, router_b, w_gate, w_up, w_down, s_gate, s_up, s_down):
    t, d = x.shape
    scores = jax.nn.sigmoid((x @ router_w).astype(jnp.float32))
    grouped = (scores + router_b.astype(jnp.float32)).reshape(t, N_GROUPS, N_EXPERTS // N_GROUPS)
    group_score = jnp.sum(lax.top_k(grouped, 2)[0], -1)
    _, top_groups = lax.top_k(group_score, TOPK_GROUPS)
    group_mask = jnp.any(top_groups[..., None] == jnp.arange(N_GROUPS), axis=1)
    masked = jnp.where(group_mask[..., None], grouped, -jnp.inf).reshape(t, N_EXPERTS)
    _, idx = lax.top_k(masked, TOP_K)
    gate = jnp.take_along_axis(scores, idx, -1)
    gate = gate / jnp.sum(gate, -1, keepdims=True) * ROUTED_SCALE
    n_assign = t * TOP_K
    flat_e = idx.reshape(-1)
    order = jnp.argsort(flat_e)
    e_sorted = flat_e[order]
    tok_sorted = (order // TOP_K).astype(jnp.int32)
    w_sorted = gate.reshape(-1)[order].astype(x.dtype)
    counts = jnp.bincount(flat_e, length=N_EXPERTS)
    padded = (counts + DISPATCH_BLOCK - 1) // DISPATCH_BLOCK * DISPATCH_BLOCK
    pad_end = jnp.cumsum(padded)
    pad_start = pad_end - padded
    start = jnp.cumsum(counts) - counts
    dest = pad_start[e_sorted] + jnp.arange(n_assign) - start[e_sorted]
    n_blocks = -(-n_assign // DISPATCH_BLOCK) + N_EXPERTS
    rows = n_blocks * DISPATCH_BLOCK
    row_tok = jnp.zeros((rows,), jnp.int32).at[dest].set(tok_sorted)
    row_w = jnp.zeros((rows,), x.dtype).at[dest].set(w_sorted)
    block_e = jnp.minimum(jnp.searchsorted(pad_end, jnp.arange(n_blocks) * DISPATCH_BLOCK, side='right'), N_EXPERTS - 1)

    def body(y, blk):
        tok, wt, e = blk
        xb = x[tok]
        hb = jax.nn.silu(xb @ w_gate[e]) * (xb @ w_up[e])
        return y.at[tok].add((hb @ w_down[e]) * wt[:, None]), None

    y, _ = lax.scan(body, jnp.zeros_like(x), (row_tok.reshape(n_blocks, DISPATCH_BLOCK),
                                              row_w.reshape(n_blocks, DISPATCH_BLOCK), block_e))
    shared = (jax.nn.silu(x @ s_gate) * (x @ s_up)) @ s_down
    return y + shared


def setup_inputs(seed: int = 0) -> dict:
    key = jax.random.key(seed)
    keys = iter(jax.random.split(key, 40))
    ne, no = (DEPTH + 1) // 2, DEPTH // 2

    def nrm(shape, scale):
        return jax.random.normal(next(keys), shape, jnp.float32) * scale

    def gain(shape):
        return 1.0 + nrm(shape, 0.02)

    x = nrm((BATCH, SEQ, D_MODEL), 1.0)
    positions = jax.random.randint(next(keys), (BATCH, 1), 0, 4096, dtype=jnp.int32) + jnp.arange(SEQ, dtype=jnp.int32)[None, :]
    u = jax.random.uniform(next(keys), (ne, GDN_H), jnp.float32)
    dt = jnp.exp(u * (math.log(0.1) - math.log(0.001)) + math.log(0.001))
    return {
        'x': x,
        'positions': positions,
        'ev_w_in': nrm((ne, D_MODEL, EV_IN), D_MODEL ** -0.5),
        'mla_q_norm': gain((ne, Q_LORA)),
        'mla_w_qb': nrm((ne, Q_LORA, MLA_H * (MLA_NOPE + MLA_ROPE)), Q_LORA ** -0.5),
        'mla_kv_norm': gain((ne, KV_LORA)),
        'mla_w_kvb': nrm((ne, KV_LORA, MLA_H * (MLA_NOPE + MLA_V)), KV_LORA ** -0.5),
        'gdn_conv': nrm((ne, CONV_W, GDN_H * (2 * GDN_DK + GDN_DV)), CONV_W ** -0.5),
        'gdn_a_log': jnp.log(jax.random.uniform(next(keys), (ne, GDN_H), jnp.float32, 1.0, 16.0)),
        'gdn_dt_bias': dt + jnp.log(-jnp.expm1(-dt)),
        'gdn_norm': gain((ne, GDN_DV)),
        'ev_w_out': nrm((ne, EV_MIX, D_MODEL), EV_MIX ** -0.5 * DN_BETA),
        'od_w_in': nrm((no, D_MODEL, OD_IN), D_MODEL ** -0.5),
        'mlstm_b_i': nrm((no, ML_H), 0.1) - 1.0,
        'mlstm_b_f': jnp.linspace(3.0, 6.0, ML_H, dtype=jnp.float32)[None, :] + nrm((no, ML_H), 0.1),
        'mlstm_norm': gain((no, ML_H * ML_DV)),
        'swa_sinks': nrm((no, SWA_H), 0.5),
        'od_w_out': nrm((no, OD_MIX, D_MODEL), OD_MIX ** -0.5 * DN_BETA),
        'ln1_g': gain((DEPTH, D_MODEL)),
        'ln1_b': nrm((DEPTH, D_MODEL), 0.02),
        'router_w': nrm((DEPTH, D_MODEL, N_EXPERTS), D_MODEL ** -0.5),
        'router_b': nrm((DEPTH, N_EXPERTS), 0.01),
        'moe_w_gate': nrm((DEPTH, N_EXPERTS, D_MODEL, D_EXPERT), D_MODEL ** -0.5),
        'moe_w_up': nrm((DEPTH, N_EXPERTS, D_MODEL, D_EXPERT), D_MODEL ** -0.5),
        'moe_w_down': nrm((DEPTH, N_EXPERTS, D_EXPERT, D_MODEL), D_EXPERT ** -0.5 * DN_BETA),
        'shared_w_gate': nrm((DEPTH, D_MODEL, D_SHARED), D_MODEL ** -0.5),
        'shared_w_up': nrm((DEPTH, D_MODEL, D_SHARED), D_MODEL ** -0.5),
        'shared_w_down': nrm((DEPTH, D_SHARED, D_MODEL), D_SHARED ** -0.5 * DN_BETA),
        'ln2_g': gain((DEPTH, D_MODEL)),
        'ln2_b': nrm((DEPTH, D_MODEL), 0.02),
    }


def reference(x, positions, ev_w_in, mla_q_norm, mla_w_qb, mla_kv_norm, mla_w_kvb, gdn_conv, gdn_a_log,
              gdn_dt_bias, gdn_norm, ev_w_out, od_w_in, mlstm_b_i, mlstm_b_f, mlstm_norm, swa_sinks, od_w_out,
              ln1_g, ln1_b, router_w, router_b, moe_w_gate, moe_w_up, moe_w_down, shared_w_gate, shared_w_up,
              shared_w_down, ln2_g, ln2_b):
    b, s, d = x.shape
    cos_m, sin_m = rope_tables(positions, MLA_ROPE)
    cos_s, sin_s = rope_tables(positions, SWA_D)
    for layer in range(DEPTH):
        j = layer // 2
        if layer % 2 == 0:
            mix = even_mixer(x, cos_m, sin_m, ev_w_in[j], mla_q_norm[j], mla_w_qb[j], mla_kv_norm[j], mla_w_kvb[j],
                             gdn_conv[j], gdn_a_log[j], gdn_dt_bias[j], gdn_norm[j], ev_w_out[j])
        else:
            mix = odd_mixer(x, cos_s, sin_s, od_w_in[j], mlstm_b_i[j], mlstm_b_f[j], mlstm_norm[j], swa_sinks[j], od_w_out[j])
        x = layer_norm(DN_ALPHA * x + mix, ln1_g[layer], ln1_b[layer])
        ff = moe_ffn(x.reshape(b * s, d), router_w[layer], router_b[layer], moe_w_gate[layer], moe_w_up[layer],
                     moe_w_down[layer], shared_w_gate[layer], shared_w_up[layer], shared_w_down[layer])
        x = layer_norm(DN_ALPHA * x + ff.reshape(b, s, d), ln2_g[layer], ln2_b[layer])
    return x
```

```python
import functools
import math

import numpy as np
import jax
import jax.numpy as jnp
from jax import lax
from jax.experimental import pallas as pl
from jax.experimental.pallas import tpu as pltpu

F32 = jnp.float32
BF16 = jnp.bfloat16
HI = lax.Precision.HIGHEST

D_MODEL = 1024
DEPTH = 4
ROPE_THETA = 10000.0
EPS = 1e-6
LN_EPS = 1e-5
MLA_H, MLA_NOPE, MLA_ROPE, MLA_V = 8, 64, 32, 64
Q_LORA, KV_LORA = 256, 128
GDN_H, GDN_DK, GDN_DV, CONV_W, GDN_CHUNK = 4, 128, 128, 4, 64
ML_H, ML_DK, ML_DV, ML_CHUNK = 4, 64, 128, 64
SWA_H, SWA_KV, SWA_D, WINDOW = 8, 2, 64, 128
N_EXPERTS, N_GROUPS, TOPK_GROUPS, TOP_K = 64, 8, 4, 8
D_EXPERT, D_SHARED = 256, 256
ROUTED_SCALE = 2.5
DN_ALPHA = (2 * DEPTH) ** 0.25

LANES = 128
V7X_VMEM_BYTES = 64 * 1024 * 1024
VMEM_LIMIT = 48 * 1024 * 1024

EXPERT_BLOCK = 256


def _cparams(sem, vmem=VMEM_LIMIT):
    return pltpu.CompilerParams(dimension_semantics=sem, vmem_limit_bytes=vmem)


def _dot(a, b, precision=None):
    return jnp.dot(a, b, preferred_element_type=F32, precision=precision)


def _dot_nt(a, b, precision=None):
    return lax.dot_general(a, b, (((1,), (1,)), ((), ())), preferred_element_type=F32, precision=precision)


def _dot_tn(a, b, precision=None):
    return lax.dot_general(a, b, (((0,), (0,)), ((), ())), preferred_element_type=F32, precision=precision)


def _sigmoid(x):
    return 1.0 / (1.0 + jnp.exp(-x))


def _softplus(x):
    return jnp.maximum(x, 0.0) + jnp.log(1.0 + jnp.exp(-jnp.abs(x)))


def _silu(x):
    return x * _sigmoid(x)


def _lane_bcast(x, c):
    return jnp.broadcast_to(x[:, c:c + 1], x.shape)


def _iota2(shape, dim):
    return lax.broadcasted_iota(jnp.int32, shape, dim)


def _rope_kernel(pos_ref, rows_ref, c_ref, s_ref):
    ang = pos_ref[...] * rows_ref[0:1, :]
    c_ref[...] = rows_ref[1:2, :] * jnp.cos(ang) + rows_ref[2:3, :]
    s_ref[...] = rows_ref[3:4, :] * jnp.sin(ang)


def _rope_tables(pos, rows, tm=512):
    t = pos.shape[0]
    return pl.pallas_call(
        _rope_kernel,
        grid=(t // tm,),
        in_specs=[pl.BlockSpec((tm, 1), lambda i: (i, 0)), pl.BlockSpec((8, LANES), lambda i: (0, 0))],
        out_specs=[pl.BlockSpec((tm, LANES), lambda i: (i, 0))] * 2,
        out_shape=[jax.ShapeDtypeStruct((t, LANES), F32)] * 2,
        compiler_params=_cparams(("arbitrary",)),
        name="rope_tables",
    )(pos, rows)


def _rope_rows(dim, first_lane, pad_one_lanes):
    half = dim // 2
    inv = ROPE_THETA ** (-(np.arange(0, dim, 2, dtype=np.float32) / dim))
    rows = np.zeros((8, LANES), np.float32)
    lo = slice(first_lane, first_lane + half)
    hi = slice(first_lane + half, first_lane + dim)
    rows[0, lo] = inv
    rows[0, hi] = inv
    rows[1, lo] = 1.0
    rows[1, hi] = 1.0
    rows[2, :pad_one_lanes] = 1.0
    rows[3, lo] = -1.0
    rows[3, hi] = 1.0
    return jnp.asarray(rows)


def _proj_kernel(x_ref, w_ref, *out_refs, offsets):
    xb = x_ref[...].astype(BF16)
    for o_ref, (a, b) in zip(out_refs, offsets):
        o_ref[...] = _dot(xb, w_ref[:, a:b]).astype(o_ref.dtype)


def _proj(x, w, widths, dtypes, tm=256):
    t, k = x.shape
    offs = np.concatenate([[0], np.cumsum(widths)]).tolist()
    offsets = tuple((offs[i], offs[i + 1]) for i in range(len(widths)))
    return pl.pallas_call(
        functools.partial(_proj_kernel, offsets=offsets),
        grid=(t // tm,),
        in_specs=[pl.BlockSpec((tm, k), lambda i: (i, 0)), pl.BlockSpec(w.shape, lambda i: (0, 0))],
        out_specs=[pl.BlockSpec((tm, n), lambda i: (i, 0)) for n in widths],
        out_shape=[jax.ShapeDtypeStruct((t, n), dt) for n, dt in zip(widths, dtypes)],
        compiler_params=_cparams(("arbitrary",)),
        name="in_proj",
    )(x, w)


OD_SEG = dict(mq=(0, 512), mk=(512, 1024), mv=(1024, 1536), mo=(1536, 2048), gates=(2048, 2176),
              sq=(2176, 3200), sqsw=(3200, 4224), sk=(4224, 4480), sksw=(4480, 4736), sv=(4736, 5248))
OD_COLS = 5248


def _proj_odd_kernel(x_ref, w_ref, c_ref, s_ref, mq_ref, mk_ref, mv_ref, mo_ref, mg_ref, sq_ref, sk_ref, sv_ref):
    xb = x_ref[...].astype(BF16)

    def seg(name):
        a, b = OD_SEG[name]
        return _dot(xb, w_ref[:, a:b])

    mq_ref[...] = seg("mq")
    mk_ref[...] = seg("mk")
    mv_ref[...] = seg("mv")
    mo_ref[...] = seg("mo")
    mg_ref[...] = seg("gates")
    c = c_ref[...]
    s = s_ref[...]
    c8 = jnp.concatenate([c] * SWA_H, axis=1)
    s8 = jnp.concatenate([s] * SWA_H, axis=1)
    sq_ref[...] = (seg("sq") * c8 + seg("sqsw") * s8).astype(sq_ref.dtype)
    c2 = jnp.concatenate([c] * SWA_KV, axis=1)
    s2 = jnp.concatenate([s] * SWA_KV, axis=1)
    sk_ref[...] = (seg("sk") * c2 + seg("sksw") * s2).astype(sk_ref.dtype)
    sv_ref[...] = seg("sv").astype(sv_ref.dtype)


def _proj_odd(x, w, ctab, stab, tm=256):
    t, k = x.shape
    widths = (512, 512, 512, 512, 128, SWA_H * LANES, SWA_KV * LANES, 2 * SWA_KV * LANES)
    dtypes = (F32, F32, F32, F32, F32, BF16, BF16, BF16)
    return pl.pallas_call(
        _proj_odd_kernel,
        grid=(t // tm,),
        in_specs=[pl.BlockSpec((tm, k), lambda i: (i, 0)), pl.BlockSpec(w.shape, lambda i: (0, 0)),
                  pl.BlockSpec((tm, LANES), lambda i: (i, 0)), pl.BlockSpec((tm, LANES), lambda i: (i, 0))],
        out_specs=[pl.BlockSpec((tm, n), lambda i: (i, 0)) for n in widths],
        out_shape=[jax.ShapeDtypeStruct((t, n), dt) for n, dt in zip(widths, dtypes)],
        compiler_params=_cparams(("arbitrary",)),
        name="in_proj_odd",
    )(x, w, ctab, stab)


def _rms(x, g):
    return x * lax.rsqrt(jnp.mean(x * x, axis=-1, keepdims=True) + EPS) * g


def _mla_prep_kernel(in_ref, c_ref, s_ref, qn_ref, kvn_ref, wq_ref, wkv_ref, q_ref, k_ref, v_ref):
    hw = MLA_H * LANES
    c = c_ref[...]
    s = s_ref[...]
    c8 = jnp.concatenate([c] * MLA_H, axis=1)
    s8 = jnp.concatenate([s] * MLA_H, axis=1)
    cqn = _rms(in_ref[:, 0:Q_LORA], qn_ref[...]).astype(BF16)
    qq = _dot(cqn, wq_ref[...])
    scale = (MLA_NOPE + MLA_ROPE) ** -0.5
    q_ref[...] = ((qq[:, :hw] * c8 + qq[:, hw:] * s8) * scale).astype(q_ref.dtype)
    ckvn = _rms(in_ref[:, Q_LORA:Q_LORA + KV_LORA], kvn_ref[...]).astype(BF16)
    kv = _dot(ckvn, wkv_ref[...])
    o = Q_LORA + KV_LORA
    krr = in_ref[:, o:o + LANES] * c + in_ref[:, o + LANES:o + 2 * LANES] * s
    k_ref[...] = (kv[:, :hw] + jnp.concatenate([krr] * MLA_H, axis=1)).astype(k_ref.dtype)
    v_ref[...] = kv[:, hw:].astype(v_ref.dtype)


def _mla_prep(mla_in, ctab, stab, qn, kvn, wq2, wkv2, tm=256):
    t = mla_in.shape[0]
    hw = MLA_H * LANES
    row = lambda i: (i, 0)
    fix = lambda i: (0, 0)
    return pl.pallas_call(
        _mla_prep_kernel,
        grid=(t // tm,),
        in_specs=[pl.BlockSpec((tm, mla_in.shape[1]), row), pl.BlockSpec((tm, LANES), row), pl.BlockSpec((tm, LANES), row),
                  pl.BlockSpec(qn.shape, fix), pl.BlockSpec(kvn.shape, fix),
                  pl.BlockSpec(wq2.shape, fix), pl.BlockSpec(wkv2.shape, fix)],
        out_specs=[pl.BlockSpec((tm, hw), row)] * 3,
        out_shape=[jax.ShapeDtypeStruct((t, hw), BF16)] * 3,
        compiler_params=_cparams(("arbitrary",)),
        name="mla_prep",
    )(mla_in, ctab, stab, qn, kvn, wq2, wkv2)


def _mla_attn_kernel(q_ref, k_ref, v_ref, o_ref, *, tq):
    i = pl.program_id(2)
    neg = -1e30

    def head(hh):
        q = q_ref[:, hh * LANES:(hh + 1) * LANES]

        def chunk(j, carry, masked):
            m, l, acc = carry
            start = pl.multiple_of(j * tq, tq)
            kc = k_ref[pl.ds(start, tq), hh * LANES:(hh + 1) * LANES]
            vc = v_ref[pl.ds(start, tq), hh * LANES:(hh + 1) * LANES]
            s = _dot_nt(q, kc)
            if masked:
                s = jnp.where(_iota2(s.shape, 0) >= _iota2(s.shape, 1), s, neg)
            m_new = jnp.maximum(m, jnp.max(s, axis=-1, keepdims=True))
            alpha = jnp.exp(m - m_new)
            p = jnp.exp(s - m_new)
            l = alpha * l + jnp.sum(p, axis=-1, keepdims=True)
            acc = alpha * acc + _dot(p.astype(BF16), vc)
            return m_new, l, acc

        init = (jnp.full((tq, 1), neg, F32), jnp.zeros((tq, 1), F32), jnp.zeros((tq, LANES), F32))
        carry = lax.fori_loop(0, i, lambda j, c: chunk(j, c, False), init)
        m, l, acc = chunk(i, carry, True)
        return acc / l

    o_ref[...] = (head(0) + head(1)).astype(o_ref.dtype)


def _mla_attn(q, k, v, batch, seq, tq=256):
    nq = seq // tq
    pairs = MLA_H // 2
    return pl.pallas_call(
        functools.partial(_mla_attn_kernel, tq=tq),
        grid=(batch, pairs, nq),
        in_specs=[pl.BlockSpec((tq, 2 * LANES), lambda b, p, i: (b * nq + i, p)),
                  pl.BlockSpec((seq, 2 * LANES), lambda b, p, i: (b, p)),
                  pl.BlockSpec((seq, 2 * LANES), lambda b, p, i: (b, p))],
        out_specs=pl.BlockSpec((tq, LANES), lambda b, p, i: (b * nq + i, p)),
        out_shape=jax.ShapeDtypeStruct((batch * seq, pairs * LANES), BF16),
        compiler_params=_cparams(("arbitrary", "arbitrary", "arbitrary")),
        name="mla_attn",
    )(q, k, v)


def _unit_lower_inverse(n):
    c = n.shape[0]
    eye = (_iota2((c, c), 0) == _iota2((c, c), 1)).astype(F32)
    x = -n
    p = eye + x
    for _ in range(int(math.log2(c)) - 1):
        x = _dot(x, x, HI)
        p = p + _dot(p, x, HI)
    return p


def _gdn_kernel(qkv_ref, g_ref, z_ref, cw_ref, al_ref, dt_ref, on_ref, o_ref, ext_ref, st_ref):
    c = GDN_CHUNK
    hd = GDN_DK
    nqk = GDN_H * GDN_DK

    @pl.when(pl.program_id(1) == 0)
    def _():
        ext_ref[0:8, :] = jnp.zeros((8, ext_ref.shape[1]), F32)
        st_ref[...] = jnp.zeros(st_ref.shape, F32)

    ext_ref[8:8 + c, :] = qkv_ref[...]
    conv = cw_ref[0:1, :] * ext_ref[5:5 + c, :]
    for j in range(1, CONV_W):
        conv = conv + cw_ref[j:j + 1, :] * ext_ref[5 + j:5 + j + c, :]
    ext_ref[0:8, :] = ext_ref[c:c + 8, :]
    act = _silu(conv)

    gates = g_ref[...]
    beta_all = _sigmoid(gates)
    g_all = -jnp.exp(al_ref[...]) * _softplus(gates + dt_ref[...])
    tri = (_iota2((c, c), 0) >= _iota2((c, c), 1)).astype(F32)
    gc_all = _dot(tri, g_all, HI)
    row_ge = _iota2((c, c), 0) >= _iota2((c, c), 1)
    row_gt = _iota2((c, c), 0) > _iota2((c, c), 1)
    ones = jnp.ones((c, LANES), F32)
    lane = _iota2((c, LANES), 1)

    for h in range(GDN_H):
        q = act[:, h * hd:(h + 1) * hd]
        k = act[:, nqk + h * hd:nqk + (h + 1) * hd]
        v = act[:, 2 * nqk + h * GDN_DV:2 * nqk + (h + 1) * GDN_DV]
        q = q * lax.rsqrt(jnp.sum(q * q, axis=-1, keepdims=True) + EPS) * (GDN_DK ** -0.5)
        k = k * lax.rsqrt(jnp.sum(k * k, axis=-1, keepdims=True) + EPS)
        beta = _lane_bcast(beta_all, h)
        gcol = _lane_bcast(gc_all, GDN_H + h)
        grow = _dot_nt(ones, jnp.where(lane == GDN_H + h, gc_all, 0.0), HI)
        diff = gcol[:, :c] - grow
        decay = jnp.exp(jnp.where(row_ge, diff, -jnp.inf))
        kb = k * beta
        lower = jnp.where(row_gt, _dot_nt(kb, k, HI) * decay, 0.0)
        tinv = _unit_lower_inverse(lower)
        eg = jnp.exp(gcol)
        u = _dot(tinv, v * beta, HI)
        w = _dot(tinv, kb * eg, HI)
        attn = _dot_nt(q, k, HI) * decay
        glast = gcol[c - 1:c, :]
        qg = q * eg
        kg = k * jnp.exp(glast - gcol)
        state = st_ref[h]
        v_new = u - _dot(w, state, HI)
        o = _dot(qg, state, HI) + _dot(attn, v_new, HI)
        st_ref[h] = state * jnp.exp(glast) + _dot_tn(kg, v_new, HI)
        o = _rms(o, on_ref[...]) * _silu(z_ref[:, h * GDN_DV:(h + 1) * GDN_DV])
        o_ref[:, h * GDN_DV:(h + 1) * GDN_DV] = o.astype(o_ref.dtype)


def _gdn(qkv, gates, z, conv_w, a_row, dt_row, o_norm, batch, seq):
    c = GDN_CHUNK
    nc = seq // c
    w3 = qkv.shape[1]
    row = lambda b, i: (b * nc + i, 0)
    fix = lambda b, i: (0, 0)
    return pl.pallas_call(
        _gdn_kernel,
        grid=(batch, nc),
        in_specs=[pl.BlockSpec((c, w3), row), pl.BlockSpec((c, LANES), row), pl.BlockSpec((c, GDN_H * GDN_DV), row),
                  pl.BlockSpec(conv_w.shape, fix), pl.BlockSpec((1, LANES), fix), pl.BlockSpec((1, LANES), fix),
                  pl.BlockSpec((1, GDN_DV), fix)],
        out_specs=pl.BlockSpec((c, GDN_H * GDN_DV), row),
        out_shape=jax.ShapeDtypeStruct((batch * seq, GDN_H * GDN_DV), BF16),
        scratch_shapes=[pltpu.VMEM((c + 8, w3), F32), pltpu.VMEM((GDN_H, GDN_DK, GDN_DV), F32)],
        compiler_params=_cparams(("arbitrary", "arbitrary")),
        name="gdn",
    )(qkv, gates, z, conv_w, a_row, dt_row, o_norm)


def _mlstm_kernel(q_ref, k_ref, v_ref, og_ref, g_ref, bias_ref, nrm_ref, o_ref, c_ref, n_ref, m_ref):
    c = ML_CHUNK

    @pl.when(pl.program_id(1) == 0)
    def _():
        c_ref[...] = jnp.zeros(c_ref.shape, F32)
        n_ref[...] = jnp.zeros(n_ref.shape, F32)
        m_ref[...] = jnp.zeros(m_ref.shape, F32)

    pre = g_ref[...] + bias_ref[...]
    logf = jnp.minimum(pre, 0.0) - jnp.log(1.0 + jnp.exp(-jnp.abs(pre)))
    tri = (_iota2((c, c), 0) >= _iota2((c, c), 1)).astype(F32)
    bcum_all = _dot(tri, logf, HI)
    row_ge = _iota2((c, c), 0) >= _iota2((c, c), 1)
    ones = jnp.ones((c, LANES), F32)
    lane = _iota2((c, LANES), 1)

    for h in range(ML_H):
        q = q_ref[:, h * LANES:(h + 1) * LANES]
        k = k_ref[:, h * LANES:(h + 1) * LANES] * (ML_DK ** -0.5)
        v = v_ref[:, h * ML_DV:(h + 1) * ML_DV]
        bcol = _lane_bcast(bcum_all, ML_H + h)
        icol = _lane_bcast(pre, h)
        brow = _dot_nt(ones, jnp.where(lane == ML_H + h, bcum_all, 0.0), HI)
        irow = _dot_nt(ones, jnp.where(lane == h, pre, 0.0), HI)
        m_st = m_ref[h]
        d = jnp.where(row_ge, bcol[:, :c] - brow + irow, -jnp.inf)
        inter = bcol + m_st
        m_t = jnp.maximum(inter, jnp.max(d, axis=-1, keepdims=True))
        w_inter = jnp.exp(inter - m_t)
        p = jnp.exp(d - m_t[:, :c]) * _dot_nt(q, k, HI)
        cst = c_ref[h]
        num = w_inter * _dot(q, cst, HI) + _dot(p, v, HI)
        den = w_inter * jnp.sum(q * n_ref[h], axis=-1, keepdims=True) + jnp.sum(p, axis=-1, keepdims=True)
        hc = num / jnp.maximum(jnp.abs(den), jnp.exp(-m_t))
        b_end = bcol[c - 1:c, :]
        a = b_end - bcol + icol
        m_new = jnp.maximum(b_end + m_st, jnp.max(a, axis=0, keepdims=True))
        s_a = jnp.exp(a - m_new)
        keep = jnp.exp(b_end + m_st - m_new)
        ks = k * s_a
        c_ref[h] = cst * keep + _dot_tn(ks, v, HI)
        n_ref[h] = n_ref[h] * keep + jnp.sum(ks, axis=0, keepdims=True)
        m_ref[h] = m_new
        hn = _rms(hc, nrm_ref[:, h * ML_DV:(h + 1) * ML_DV]) * _sigmoid(og_ref[:, h * ML_DV:(h + 1) * ML_DV])
        o_ref[:, h * ML_DV:(h + 1) * ML_DV] = hn.astype(o_ref.dtype)


def _mlstm(mq, mk, mv, mo, gates, bias_row, norm_row, batch, seq):
    c = ML_CHUNK
    nc = seq // c
    row = lambda b, i: (b * nc + i, 0)
    fix = lambda b, i: (0, 0)
    wide = ML_H * LANES
    return pl.pallas_call(
        _mlstm_kernel,
        grid=(batch, nc),
        in_specs=[pl.BlockSpec((c, wide), row), pl.BlockSpec((c, wide), row), pl.BlockSpec((c, wide), row),
                  pl.BlockSpec((c, wide), row), pl.BlockSpec((c, LANES), row),
                  pl.BlockSpec((1, LANES), fix), pl.BlockSpec((1, wide), fix)],
        out_specs=pl.BlockSpec((c, wide), row),
        out_shape=jax.ShapeDtypeStruct((batch * seq, wide), BF16),
        scratch_shapes=[pltpu.VMEM((ML_H, LANES, ML_DV), F32), pltpu.VMEM((ML_H, 1, LANES), F32),
                        pltpu.VMEM((ML_H, 1, LANES), F32)],
        compiler_params=_cparams(("arbitrary", "arbitrary")),
        name="mlstm",
    )(mq, mk, mv, mo, gates, bias_row, norm_row)


def _swa_kernel(q_ref, kc_ref, kp_ref, vc_ref, vp_ref, sink_ref, o_ref):
    w = WINDOW
    n = pl.program_id(1)
    scale = SWA_D ** -0.5
    qi = _iota2((w, w), 0)
    kj = _iota2((w, w), 1)
    mask_c = kj <= qi
    mask_p = jnp.logical_and(kj > qi, n > 0)
    grp = SWA_H // SWA_KV
    neg = -1e30
    for pair in range(SWA_H // 2):
        acc = jnp.zeros((w, LANES), F32)
        for sub in range(2):
            h = 2 * pair + sub
            g = h // grp
            q = q_ref[:, h * LANES:(h + 1) * LANES]
            kc = kc_ref[:, g * LANES:(g + 1) * LANES]
            kp = kp_ref[:, g * LANES:(g + 1) * LANES]
            vcol = (2 * g + sub) * LANES
            vc = vc_ref[:, vcol:vcol + LANES]
            vp = vp_ref[:, vcol:vcol + LANES]
            s_c = jnp.where(mask_c, _dot_nt(q, kc) * scale, neg)
            s_p = jnp.where(mask_p, _dot_nt(q, kp) * scale, neg)
            sink = sink_ref[:, h:h + 1]
            m = jnp.maximum(jnp.maximum(jnp.max(s_c, axis=-1, keepdims=True), jnp.max(s_p, axis=-1, keepdims=True)), sink)
            p_c = jnp.where(mask_c, jnp.exp(s_c - m), 0.0)
            p_p = jnp.where(mask_p, jnp.exp(s_p - m), 0.0)
            den = jnp.sum(p_c, axis=-1, keepdims=True) + jnp.sum(p_p, axis=-1, keepdims=True) + jnp.exp(sink - m)
            inv = 1.0 / den
            acc = acc + _dot((p_c * inv).astype(BF16), vc) + _dot((p_p * inv).astype(BF16), vp)
        o_ref[:, pair * LANES:(pair + 1) * LANES] = acc.astype(o_ref.dtype)


def _swa(sq, sk, sv, sinks_row, batch, seq):
    w = WINDOW
    nb = seq // w
    cur = lambda b, n: (b * nb + n, 0)
    prev = lambda b, n: (b * nb + jnp.maximum(n - 1, 0), 0)
    return pl.pallas_call(
        _swa_kernel,
        grid=(batch, nb),
        in_specs=[pl.BlockSpec((w, sq.shape[1]), cur),
                  pl.BlockSpec((w, sk.shape[1]), cur), pl.BlockSpec((w, sk.shape[1]), prev),
                  pl.BlockSpec((w, sv.shape[1]), cur), pl.BlockSpec((w, sv.shape[1]), prev),
                  pl.BlockSpec((1, LANES), lambda b, n: (0, 0))],
        out_specs=pl.BlockSpec((w, SWA_H * SWA_D), cur),
        out_shape=jax.ShapeDtypeStruct((batch * seq, SWA_H * SWA_D), BF16),
        compiler_params=_cparams(("arbitrary", "arbitrary")),
        name="swa",
    )(sq, sk, sk, sv, sv, sinks_row)


def _layer_norm(h, g, b):
    mu = jnp.mean(h, axis=-1, keepdims=True)
    d = h - mu
    var = jnp.mean(d * d, axis=-1, keepdims=True)
    return d * lax.rsqrt(var + LN_EPS) * g + b


def _outproj_kernel(x_ref, a1_ref, a2_ref, w_ref, g_ref, b_ref, o_ref):
    k1 = a1_ref.shape[1]
    y = _dot(a1_ref[...].astype(BF16), w_ref[0:k1, :]) + _dot(a2_ref[...].astype(BF16), w_ref[k1:, :])
    o_ref[...] = _layer_norm(DN_ALPHA * x_ref[...] + y, g_ref[...], b_ref[...])


def _outproj_ln(x, a1, a2, w, g, b, tm=256):
    t, d = x.shape
    row = lambda i: (i, 0)
    fix = lambda i: (0, 0)
    return pl.pallas_call(
        _outproj_kernel,
        grid=(t // tm,),
        in_specs=[pl.BlockSpec((tm, d), row), pl.BlockSpec((tm, a1.shape[1]), row), pl.BlockSpec((tm, a2.shape[1]), row),
                  pl.BlockSpec(w.shape, fix), pl.BlockSpec((1, d), fix), pl.BlockSpec((1, d), fix)],
        out_specs=pl.BlockSpec((tm, d), row),
        out_shape=jax.ShapeDtypeStruct((t, d), F32),
        compiler_params=_cparams(("arbitrary",)),
        name="outproj_ln",
    )(x, a1, a2, w, g, b)


def _first_index(x, m, iota_f, sentinel):
    return jnp.min(jnp.where(x == m, iota_f, sentinel), axis=0, keepdims=True)


def _router_kernel(x_ref, wt_ref, bias_ref, idx_ref, gate_ref, rank_ref, cnt_ref, carry_ref):
    tm = x_ref.shape[0]
    e = N_EXPERTS
    gs = e // N_GROUPS
    ninf = -jnp.inf

    @pl.when(pl.program_id(0) == 0)
    def _():
        carry_ref[...] = jnp.zeros(carry_ref.shape, F32)

    logits = _dot_nt(wt_ref[...], x_ref[...], HI)
    scores = _sigmoid(logits)
    sel = scores + bias_ref[:, 0:1]

    sub_f = _iota2((gs, tm), 0).astype(F32)
    gscore = []
    for g in range(N_GROUPS):
        blk = sel[g * gs:(g + 1) * gs, :]
        m1 = jnp.max(blk, axis=0, keepdims=True)
        i1 = _first_index(blk, m1, sub_f, float(gs))
        m2 = jnp.max(jnp.where(sub_f == i1, ninf, blk), axis=0, keepdims=True)
        gscore.append(m1 + m2)
    gsc = jnp.concatenate(gscore, axis=0)
    grp_f = _iota2((N_GROUPS, tm), 0).astype(F32)
    gmask = jnp.zeros((N_GROUPS, tm), F32)
    for _ in range(TOPK_GROUPS):
        m = jnp.max(gsc, axis=0, keepdims=True)
        gi = _first_index(gsc, m, grp_f, float(N_GROUPS))
        hit = grp_f == gi
        gmask = jnp.where(hit, 1.0, gmask)
        gsc = jnp.where(hit, ninf, gsc)
    masked = jnp.concatenate(
        [jnp.where(gmask[g:g + 1, :] > 0.0, sel[g * gs:(g + 1) * gs, :], ninf) for g in range(N_GROUPS)], axis=0)

    exp_f = _iota2((e, tm), 0).astype(F32)
    chosen = jnp.zeros((e, tm), F32)
    idxs, gates = [], []
    for _ in range(TOP_K):
        m = jnp.max(masked, axis=0, keepdims=True)
        ei = _first_index(masked, m, exp_f, float(e))
        hit = exp_f == ei
        idxs.append(ei)
        gates.append(jnp.sum(jnp.where(hit, scores, 0.0), axis=0, keepdims=True))
        chosen = jnp.where(hit, 1.0, chosen)
        masked = jnp.where(hit, ninf, masked)
    gate = jnp.concatenate(gates, axis=0)
    gate = gate / jnp.sum(gate, axis=0, keepdims=True) * ROUTED_SCALE
    idx_f = jnp.concatenate(idxs, axis=0)

    upper = (_iota2((tm, tm), 0) < _iota2((tm, tm), 1)).astype(BF16)
    before = _dot(chosen.astype(BF16), upper) + carry_ref[...][:, 0:1]
    ranks = [jnp.sum(jnp.where(exp_f == idxs[k], before, 0.0), axis=0, keepdims=True) for k in range(TOP_K)]
    carry_ref[...] = carry_ref[...] + jnp.sum(chosen, axis=1, keepdims=True)

    idx_ref[...] = idx_f.astype(jnp.int32)
    gate_ref[...] = gate
    rank_ref[...] = jnp.concatenate(ranks, axis=0).astype(jnp.int32)
    cnt_ref[...] = carry_ref[...]


def _router(x, wt, bias_col, tm=512):
    t, d = x.shape
    col = lambda i: (0, i)
    fix = lambda i: (0, 0)
    return pl.pallas_call(
        _router_kernel,
        grid=(t // tm,),
        in_specs=[pl.BlockSpec((tm, d), lambda i: (i, 0)), pl.BlockSpec(wt.shape, fix), pl.BlockSpec((N_EXPERTS, LANES), fix)],
        out_specs=[pl.BlockSpec((TOP_K, tm), col), pl.BlockSpec((TOP_K, tm), col), pl.BlockSpec((TOP_K, tm), col),
                   pl.BlockSpec((N_EXPERTS, LANES), fix)],
        out_shape=[jax.ShapeDtypeStruct((TOP_K, t), jnp.int32), jax.ShapeDtypeStruct((TOP_K, t), F32),
                   jax.ShapeDtypeStruct((TOP_K, t), jnp.int32), jax.ShapeDtypeStruct((N_EXPERTS, LANES), F32)],
        scratch_shapes=[pltpu.VMEM((N_EXPERTS, LANES), F32)],
        compiler_params=_cparams(("arbitrary",)),
        name="router",
    )(x, wt, bias_col)


def _dispatch_kernel(dest_ref, fill_ref, x_ref, xs_ref, zero_ref, sem, zsem):
    tm = x_ref.shape[0]

    def row_copy(t, s):
        return pltpu.make_async_copy(x_ref.at[pl.ds(t, 1), :], xs_ref.at[pl.ds(dest_ref[s, t], 1), :], sem)

    def issue(t, _):
        for s in range(TOP_K):
            row_copy(t, s).start()
        return 0

    lax.fori_loop(0, tm, issue, 0)

    @pl.when(pl.program_id(0) == 0)
    def _():
        zero_ref[...] = jnp.zeros(zero_ref.shape, F32)

        def pad_copy(r):
            return pltpu.make_async_copy(zero_ref, xs_ref.at[pl.ds(r, 1), :], zsem)

        def per_expert(e, _):
            start = fill_ref[e, 0]
            cnt = fill_ref[e, 1]
            lax.fori_loop(0, cnt, lambda r, c: (pad_copy(start + r).start(), c)[1], 0)
            lax.fori_loop(0, cnt, lambda r, c: (pad_copy(start + r).wait(), c)[1], 0)
            return 0

        lax.fori_loop(0, N_EXPERTS, per_expert, 0)

    def drain(t, _):
        for s in range(TOP_K):
            row_copy(t, s).wait()
        return 0

    lax.fori_loop(0, tm, drain, 0)


def _dispatch(x, dest, fill, rows, tm=256):
    t, d = x.shape
    return pl.pallas_call(
        _dispatch_kernel,
        grid=(t // tm,),
        in_specs=[pl.BlockSpec((TOP_K, tm), lambda i: (0, i), memory_space=pltpu.SMEM),
                  pl.BlockSpec(memory_space=pltpu.SMEM),
                  pl.BlockSpec((tm, d), lambda i: (i, 0))],
        out_specs=pl.BlockSpec(memory_space=pl.ANY),
        out_shape=jax.ShapeDtypeStruct((rows, d), F32),
        scratch_shapes=[pltpu.VMEM((1, d), F32), pltpu.SemaphoreType.DMA(()), pltpu.SemaphoreType.DMA(())],
        compiler_params=_cparams(("arbitrary",)),
        name="moe_dispatch",
    )(dest, fill, x)


def _experts_kernel(be_ref, nu_ref, xs_ref, wg_ref, wu_ref, wd_ref, ys_ref):
    i = pl.program_id(0)

    @pl.when(i < nu_ref[0])
    def _():
        xb = xs_ref[...].astype(BF16)
        h = _silu(_dot(xb, wg_ref[0])) * _dot(xb, wu_ref[0])
        ys_ref[...] = _dot(h.astype(BF16), wd_ref[0])

    @pl.when(i >= nu_ref[0])
    def _():
        ys_ref[...] = jnp.zeros(ys_ref.shape, F32)


def _experts(block_e, n_used, xs, wg, wu, wd):
    rows, d = xs.shape
    nb = rows // EXPERT_BLOCK
    blk = lambda i, be, nu: (jnp.minimum(i, nu[0] - 1), 0)
    wsel = lambda i, be, nu: (be[i], 0, 0)
    return pl.pallas_call(
        _experts_kernel,
        grid_spec=pltpu.PrefetchScalarGridSpec(
            num_scalar_prefetch=2,
            grid=(nb,),
            in_specs=[pl.BlockSpec((EXPERT_BLOCK, d), blk),
                      pl.BlockSpec((1, d, D_EXPERT), wsel), pl.BlockSpec((1, d, D_EXPERT), wsel),
                      pl.BlockSpec((1, D_EXPERT, d), wsel)],
            out_specs=pl.BlockSpec((EXPERT_BLOCK, d), lambda i, be, nu: (i, 0)),
        ),
        out_shape=jax.ShapeDtypeStruct((rows, d), F32),
        compiler_params=_cparams(("arbitrary",)),
        name="moe_experts",
    )(block_e, n_used, xs, wg, wu, wd)


def _combine_kernel(dest_ref, x_ref, gate_ref, ys_ref, sg_ref, su_ref, sd_ref, g_ref, b_ref, o_ref, buf_ref, sem):
    tm = x_ref.shape[0]

    def row_copy(t, s):
        return pltpu.make_async_copy(ys_ref.at[pl.ds(dest_ref[s, t], 1), :], buf_ref.at[s, pl.ds(t, 1), :], sem)

    def issue(t, _):
        for s in range(TOP_K):
            row_copy(t, s).start()
        return 0

    lax.fori_loop(0, tm, issue, 0)

    x = x_ref[...]
    xb = x.astype(BF16)
    hs = _silu(_dot(xb, sg_ref[...])) * _dot(xb, su_ref[...])
    ff = _dot(hs.astype(BF16), sd_ref[...])

    def drain(t, _):
        for s in range(TOP_K):
            row_copy(t, s).wait()
        return 0

    lax.fori_loop(0, tm, drain, 0)

    gate = gate_ref[...]
    for s in range(TOP_K):
        ff = ff + gate[:, s:s + 1] * buf_ref[s]
    o_ref[...] = _layer_norm(DN_ALPHA * x + ff, g_ref[...], b_ref[...])


def _combine(dest, x, gate_t, ys, sg, su, sd, g, b, tm=256):
    t, d = x.shape
    row = lambda i: (i, 0)
    fix = lambda i: (0, 0)
    return pl.pallas_call(
        _combine_kernel,
        grid=(t // tm,),
        in_specs=[pl.BlockSpec((TOP_K, tm), lambda i: (0, i), memory_space=pltpu.SMEM),
                  pl.BlockSpec((tm, d), row), pl.BlockSpec((tm, TOP_K), row),
                  pl.BlockSpec(memory_space=pl.ANY),
                  pl.BlockSpec(sg.shape, fix), pl.BlockSpec(su.shape, fix), pl.BlockSpec(sd.shape, fix),
                  pl.BlockSpec((1, d), fix), pl.BlockSpec((1, d), fix)],
        out_specs=pl.BlockSpec((tm, d), row),
        out_shape=jax.ShapeDtypeStruct((t, d), F32),
        scratch_shapes=[pltpu.VMEM((TOP_K, tm, d), F32), pltpu.SemaphoreType.DMA(())],
        compiler_params=_cparams(("arbitrary",)),
        name="moe_combine",
    )(dest, x, gate_t, ys, sg, su, sd, g, b)


def _take_cols(w, idx):
    wz = jnp.concatenate([w, jnp.zeros((w.shape[0], 1), w.dtype)], axis=1)
    idx = np.where(np.asarray(idx) < 0, w.shape[1], np.asarray(idx))
    return jnp.take(wz, jnp.asarray(idx, jnp.int32), axis=1)


def _pad_lane_row(v, first_lane, width=LANES):
    out = jnp.zeros((1, width), F32)
    return lax.dynamic_update_slice(out, v.reshape(1, -1).astype(F32), (0, first_lane))


def _even_in_cols():
    z = lambda n: -np.ones(n, int)
    kr0 = Q_LORA + KV_LORA
    half = MLA_ROPE // 2
    cols = [np.arange(0, Q_LORA), np.arange(Q_LORA, Q_LORA + KV_LORA),
            z(64), np.arange(kr0, kr0 + MLA_ROPE), z(32),
            z(64), np.arange(kr0 + half, kr0 + MLA_ROPE), np.arange(kr0, kr0 + half), z(32)]
    g0 = kr0 + MLA_ROPE
    nqk = GDN_H * GDN_DK
    cols.append(np.arange(g0, g0 + 3 * nqk))
    zoff = g0 + 3 * nqk + 2 * GDN_H
    cols.append(np.arange(zoff, zoff + GDN_H * GDN_DV))
    cols += [np.arange(g0 + 3 * nqk, g0 + 3 * nqk + 2 * GDN_H), z(LANES - 2 * GDN_H)]
    return np.concatenate(cols)


EV_WIDTHS = (Q_LORA + KV_LORA + 2 * LANES, 3 * GDN_H * GDN_DK, GDN_H * GDN_DV, LANES)


def _mla_q_cols():
    per = MLA_NOPE + MLA_ROPE
    half = MLA_ROPE // 2
    main, sw = [], []
    for h in range(MLA_H):
        b = h * per
        main += [np.arange(b, b + per), -np.ones(LANES - per, int)]
        sw += [-np.ones(MLA_NOPE, int), np.arange(b + MLA_NOPE + half, b + per), np.arange(b + MLA_NOPE, b + MLA_NOPE + half),
               -np.ones(LANES - per, int)]
    return np.concatenate(main + sw)


def _mla_kv_cols():
    per = MLA_NOPE + MLA_V
    kc, vc = [], []
    for h in range(MLA_H):
        b = h * per
        kc += [np.arange(b, b + MLA_NOPE), -np.ones(LANES - MLA_NOPE, int)]
        vv = np.arange(b + MLA_NOPE, b + per)
        pad = -np.ones(LANES - MLA_V, int)
        vc += [vv, pad] if h % 2 == 0 else [pad, vv]
    return np.concatenate(kc + vc)


def _odd_in_cols():
    z = lambda n: -np.ones(n, int)
    o = 0
    cols = []
    mq0, mk0 = 0, ML_H * ML_DK
    for base in (mq0, mk0):
        for h in range(ML_H):
            cols += [np.arange(base + h * ML_DK, base + (h + 1) * ML_DK), z(LANES - ML_DK)]
    mv0 = 2 * ML_H * ML_DK
    cols.append(np.arange(mv0, mv0 + ML_H * ML_DV))
    mi0 = mv0 + ML_H * ML_DV
    mo0 = mi0 + 2 * ML_H
    cols.append(np.arange(mo0, mo0 + ML_H * ML_DV))
    cols += [np.arange(mi0, mi0 + 2 * ML_H), z(LANES - 2 * ML_H)]
    sq0 = mo0 + ML_H * ML_DV
    sk0 = sq0 + SWA_H * SWA_D
    sv0 = sk0 + SWA_KV * SWA_D
    half = SWA_D // 2

    def heads(base, n, swapped):
        out = []
        for h in range(n):
            b = base + h * SWA_D
            if swapped:
                out += [np.arange(b + half, b + SWA_D), np.arange(b, b + half), z(LANES - SWA_D)]
            else:
                out += [np.arange(b, b + SWA_D), z(LANES - SWA_D)]
        return out

    cols += heads(sq0, SWA_H, False) + heads(sq0, SWA_H, True) + heads(sk0, SWA_KV, False) + heads(sk0, SWA_KV, True)
    for g in range(SWA_KV):
        vv = np.arange(sv0 + g * SWA_D, sv0 + (g + 1) * SWA_D)
        cols += [vv, z(LANES - SWA_D), z(LANES - SWA_D), vv]
    return np.concatenate(cols)


def _even_mixer(x, tabs, w_in, q_norm, w_qb, kv_norm, w_kvb, conv_w, a_log, dt_bias, o_norm, batch, seq):
    ctab, stab = tabs
    w = _take_cols(w_in, _even_in_cols()).astype(BF16)
    mla_in, qkv, z, gates = _proj(x, w, EV_WIDTHS, (F32, F32, F32, F32))
    wq2 = _take_cols(w_qb, _mla_q_cols()).astype(BF16)
    wkv2 = _take_cols(w_kvb, _mla_kv_cols()).astype(BF16)
    q, k, v = _mla_prep(mla_in, ctab, stab, q_norm.reshape(1, -1), kv_norm.reshape(1, -1), wq2, wkv2)
    o_a = _mla_attn(q, k, v, batch, seq)
    o_b = _gdn(qkv, gates, z, conv_w, _pad_lane_row(a_log, GDN_H), _pad_lane_row(dt_bias, GDN_H),
               o_norm.reshape(1, -1), batch, seq)
    return o_a, o_b


def _odd_mixer(x, tabs, w_in, b_i, b_f, ml_norm, sinks, batch, seq):
    ctab, stab = tabs
    w = _take_cols(w_in, _odd_in_cols()).astype(BF16)
    mq, mk, mv, mo, mg, sq, sk, sv = _proj_odd(x, w, ctab, stab)
    bias_row = _pad_lane_row(jnp.concatenate([b_i, b_f]), 0)
    o_c = _mlstm(mq, mk, mv, mo, mg, bias_row, ml_norm.reshape(1, -1), batch, seq)
    o_d = _swa(sq, sk, sv, _pad_lane_row(sinks, 0), batch, seq)
    return o_c, o_d


def _moe(x, router_w, router_b, w_gate, w_up, w_down, s_gate, s_up, s_down, ln_g, ln_b):
    t, d = x.shape
    bias_col = jnp.broadcast_to(router_b.reshape(-1, 1).astype(F32), (N_EXPERTS, LANES))
    idx, gate, rank, cnt = _router(x, router_w.T, bias_col)
    counts = cnt[:, 0].astype(jnp.int32)
    padded = (counts + EXPERT_BLOCK - 1) // EXPERT_BLOCK * EXPERT_BLOCK
    pad_end = jnp.cumsum(padded)
    pad_start = pad_end - padded
    dest = pad_start[idx] + rank
    n_blocks = t * TOP_K // EXPERT_BLOCK + N_EXPERTS
    rows = n_blocks * EXPERT_BLOCK
    block_e = jnp.minimum(jnp.searchsorted(pad_end, jnp.arange(n_blocks, dtype=jnp.int32) * EXPERT_BLOCK, side="right"),
                          N_EXPERTS - 1).astype(jnp.int32)
    n_used = (pad_end[-1:] // EXPERT_BLOCK).astype(jnp.int32)
    fill = jnp.stack([pad_start + counts, padded - counts], axis=1).astype(jnp.int32)
    xs = _dispatch(x, dest, fill, rows)
    ys = _experts(block_e, n_used, xs, w_gate.astype(BF16), w_up.astype(BF16), w_down.astype(BF16))
    return _combine(dest, x, gate.T, ys, s_gate.astype(BF16), s_up.astype(BF16), s_down.astype(BF16),
                    ln_g.reshape(1, -1), ln_b.reshape(1, -1))


def kernel(x, positions, ev_w_in, mla_q_norm, mla_w_qb, mla_kv_norm, mla_w_kvb, gdn_conv, gdn_a_log, gdn_dt_bias, gdn_norm, ev_w_out, od_w_in, mlstm_b_i, mlstm_b_f, mlstm_norm, swa_sinks, od_w_out, ln1_g, ln1_b, router_w, router_b, moe_w_gate, moe_w_up, moe_w_down, shared_w_gate, shared_w_up, shared_w_down, ln2_g, ln2_b):
    batch, seq, d = x.shape
    t = batch * seq
    pos = positions.reshape(t, 1).astype(F32)
    tabs_m = _rope_tables(pos, _rope_rows(MLA_ROPE, MLA_NOPE, MLA_NOPE))
    tabs_s = _rope_tables(pos, _rope_rows(SWA_D, 0, 0))
    h = x.reshape(t, d)
    for layer in range(DEPTH):
        j = layer // 2
        if layer % 2 == 0:
            a1, a2 = _even_mixer(h, tabs_m, ev_w_in[j], mla_q_norm[j], mla_w_qb[j], mla_kv_norm[j], mla_w_kvb[j],
                                 gdn_conv[j], gdn_a_log[j], gdn_dt_bias[j], gdn_norm[j], batch, seq)
            w_out = ev_w_out[j]
        else:
            a1, a2 = _odd_mixer(h, tabs_s, od_w_in[j], mlstm_b_i[j], mlstm_b_f[j], mlstm_norm[j], swa_sinks[j], batch, seq)
            w_out = od_w_out[j]
        h = _outproj_ln(h, a1, a2, w_out.astype(BF16), ln1_g[layer].reshape(1, -1), ln1_b[layer].reshape(1, -1))
        h = _moe(h, router_w[layer], router_b[layer], moe_w_gate[layer], moe_w_up[layer], moe_w_down[layer],
                 shared_w_gate[layer], shared_w_up[layer], shared_w_down[layer], ln2_g[layer], ln2_b[layer])
    return h.reshape(batch, seq, d)
```

```python
import functools
import math

import numpy as np
import jax
import jax.numpy as jnp
from jax import lax
from jax.experimental import pallas as pl
from jax.experimental.pallas import tpu as pltpu

F32 = jnp.float32
BF16 = jnp.bfloat16
HI = lax.Precision.HIGHEST

D_MODEL = 1024
DEPTH = 4
ROPE_THETA = 10000.0
EPS = 1e-6
LN_EPS = 1e-5
MLA_H, MLA_NOPE, MLA_ROPE, MLA_V = 8, 64, 32, 64
Q_LORA, KV_LORA = 256, 128
GDN_H, GDN_DK, GDN_DV, CONV_W, GDN_CHUNK = 4, 128, 128, 4, 64
ML_H, ML_DK, ML_DV, ML_CHUNK = 4, 64, 128, 64
SWA_H, SWA_KV, SWA_D, WINDOW = 8, 2, 64, 128
N_EXPERTS, N_GROUPS, TOPK_GROUPS, TOP_K = 64, 8, 4, 8
D_EXPERT, D_SHARED = 256, 256
ROUTED_SCALE = 2.5
DN_ALPHA = (2 * DEPTH) ** 0.25

LANES = 128
V7X_VMEM_BYTES = 64 * 1024 * 1024
VMEM_LIMIT = 48 * 1024 * 1024

EXPERT_BLOCK = 256
SEQS_PER_STEP = 2


def _cparams(sem, vmem=VMEM_LIMIT):
    return pltpu.CompilerParams(dimension_semantics=sem, vmem_limit_bytes=vmem)


def _dot(a, b, precision=None):
    return jnp.dot(a, b, preferred_element_type=F32, precision=precision)


def _dot_nt(a, b, precision=None):
    return lax.dot_general(a, b, (((1,), (1,)), ((), ())), preferred_element_type=F32, precision=precision)


def _dot_tn(a, b, precision=None):
    return lax.dot_general(a, b, (((0,), (0,)), ((), ())), preferred_element_type=F32, precision=precision)


def _split2(a):
    hi = a.astype(BF16)
    lo = (a - hi.astype(F32)).astype(BF16)
    return hi, lo


def _split3(a):
    p1 = a.astype(BF16)
    r = a - p1.astype(F32)
    p2 = r.astype(BF16)
    p3 = (r - p2.astype(F32)).astype(BF16)
    return p1, p2, p3


def _dot3(a, b, dot=_dot):
    ah, al = _split2(a)
    bh, bl = _split2(b)
    return dot(ah, bh) + (dot(ah, bl) + dot(al, bh))


def _dot_sel(sel, b, dot=_dot):
    sel = sel.astype(BF16)
    p1, p2, p3 = _split3(b)
    return dot(sel, p1) + (dot(sel, p2) + dot(sel, p3))


def _sigmoid(x):
    return 1.0 / (1.0 + jnp.exp(-x))


def _softplus(x):
    return jnp.maximum(x, 0.0) + jnp.log(1.0 + jnp.exp(-jnp.abs(x)))


def _silu(x):
    return x * _sigmoid(x)


def _lane_bcast(x, c):
    return jnp.broadcast_to(x[:, c:c + 1], x.shape)


def _iota2(shape, dim):
    return lax.broadcasted_iota(jnp.int32, shape, dim)


def _rope_kernel(pos_ref, rows_ref, c_ref, s_ref):
    ang = pos_ref[...] * rows_ref[0:1, :]
    c_ref[...] = rows_ref[1:2, :] * jnp.cos(ang) + rows_ref[2:3, :]
    s_ref[...] = rows_ref[3:4, :] * jnp.sin(ang)


def _rope_tables(pos, rows, tm=512):
    t = pos.shape[0]
    return pl.pallas_call(
        _rope_kernel,
        grid=(t // tm,),
        in_specs=[pl.BlockSpec((tm, 1), lambda i: (i, 0)), pl.BlockSpec((8, LANES), lambda i: (0, 0))],
        out_specs=[pl.BlockSpec((tm, LANES), lambda i: (i, 0))] * 2,
        out_shape=[jax.ShapeDtypeStruct((t, LANES), F32)] * 2,
        compiler_params=_cparams(("arbitrary",)),
        name="rope_tables",
    )(pos, rows)


def _rope_rows(dim, first_lane, pad_one_lanes):
    half = dim // 2
    inv = ROPE_THETA ** (-(np.arange(0, dim, 2, dtype=np.float32) / dim))
    rows = np.zeros((8, LANES), np.float32)
    lo = slice(first_lane, first_lane + half)
    hi = slice(first_lane + half, first_lane + dim)
    rows[0, lo] = inv
    rows[0, hi] = inv
    rows[1, lo] = 1.0
    rows[1, hi] = 1.0
    rows[2, :pad_one_lanes] = 1.0
    rows[3, lo] = -1.0
    rows[3, hi] = 1.0
    return jnp.asarray(rows)


def _proj_kernel(x_ref, w_ref, *out_refs, offsets):
    xb = x_ref[...].astype(BF16)
    for o_ref, (a, b) in zip(out_refs, offsets):
        o_ref[...] = _dot(xb, w_ref[:, a:b]).astype(o_ref.dtype)


def _proj(x, w, widths, dtypes, tm=256):
    t, k = x.shape
    offs = np.concatenate([[0], np.cumsum(widths)]).tolist()
    offsets = tuple((offs[i], offs[i + 1]) for i in range(len(widths)))
    return pl.pallas_call(
        functools.partial(_proj_kernel, offsets=offsets),
        grid=(t // tm,),
        in_specs=[pl.BlockSpec((tm, k), lambda i: (i, 0)), pl.BlockSpec(w.shape, lambda i: (0, 0))],
        out_specs=[pl.BlockSpec((tm, n), lambda i: (i, 0)) for n in widths],
        out_shape=[jax.ShapeDtypeStruct((t, n), dt) for n, dt in zip(widths, dtypes)],
        compiler_params=_cparams(("arbitrary",)),
        name="in_proj",
    )(x, w)


OD_SEG = dict(mq=(0, 512), mk=(512, 1024), mv=(1024, 1536), mo=(1536, 2048), gates=(2048, 2176),
              sq=(2176, 3200), sqsw=(3200, 4224), sk=(4224, 4480), sksw=(4480, 4736), sv=(4736, 5248))
OD_COLS = 5248


def _proj_odd_kernel(x_ref, w_ref, c_ref, s_ref, mq_ref, mk_ref, mv_ref, mo_ref, mg_ref, sq_ref, sk_ref, sv_ref):
    xb = x_ref[...].astype(BF16)

    def seg(name):
        a, b = OD_SEG[name]
        return _dot(xb, w_ref[:, a:b])

    mq_ref[...] = seg("mq")
    mk_ref[...] = seg("mk")
    mv_ref[...] = seg("mv")
    mo_ref[...] = seg("mo")
    mg_ref[...] = seg("gates")
    c = c_ref[...]
    s = s_ref[...]
    c8 = jnp.concatenate([c] * SWA_H, axis=1)
    s8 = jnp.concatenate([s] * SWA_H, axis=1)
    sq_ref[...] = (seg("sq") * c8 + seg("sqsw") * s8).astype(sq_ref.dtype)
    c2 = jnp.concatenate([c] * SWA_KV, axis=1)
    s2 = jnp.concatenate([s] * SWA_KV, axis=1)
    sk_ref[...] = (seg("sk") * c2 + seg("sksw") * s2).astype(sk_ref.dtype)
    sv_ref[...] = seg("sv").astype(sv_ref.dtype)


def _proj_odd(x, w, ctab, stab, tm=256):
    t, k = x.shape
    widths = (512, 512, 512, 512, 128, SWA_H * LANES, SWA_KV * LANES, 2 * SWA_KV * LANES)
    dtypes = (F32, F32, F32, F32, F32, BF16, BF16, BF16)
    return pl.pallas_call(
        _proj_odd_kernel,
        grid=(t // tm,),
        in_specs=[pl.BlockSpec((tm, k), lambda i: (i, 0)), pl.BlockSpec(w.shape, lambda i: (0, 0)),
                  pl.BlockSpec((tm, LANES), lambda i: (i, 0)), pl.BlockSpec((tm, LANES), lambda i: (i, 0))],
        out_specs=[pl.BlockSpec((tm, n), lambda i: (i, 0)) for n in widths],
        out_shape=[jax.ShapeDtypeStruct((t, n), dt) for n, dt in zip(widths, dtypes)],
        compiler_params=_cparams(("arbitrary",)),
        name="in_proj_odd",
    )(x, w, ctab, stab)


def _rms(x, g):
    return x * lax.rsqrt(jnp.mean(x * x, axis=-1, keepdims=True) + EPS) * g


def _mla_prep_kernel(in_ref, c_ref, s_ref, qn_ref, kvn_ref, wq_ref, wkv_ref, q_ref, k_ref, v_ref):
    hw = MLA_H * LANES
    c = c_ref[...]
    s = s_ref[...]
    c8 = jnp.concatenate([c] * MLA_H, axis=1)
    s8 = jnp.concatenate([s] * MLA_H, axis=1)
    cqn = _rms(in_ref[:, 0:Q_LORA], qn_ref[...]).astype(BF16)
    qq = _dot(cqn, wq_ref[...])
    scale = (MLA_NOPE + MLA_ROPE) ** -0.5
    q_ref[...] = ((qq[:, :hw] * c8 + qq[:, hw:] * s8) * scale).astype(q_ref.dtype)
    ckvn = _rms(in_ref[:, Q_LORA:Q_LORA + KV_LORA], kvn_ref[...]).astype(BF16)
    kv = _dot(ckvn, wkv_ref[...])
    o = Q_LORA + KV_LORA
    krr = in_ref[:, o:o + LANES] * c + in_ref[:, o + LANES:o + 2 * LANES] * s
    k_ref[...] = (kv[:, :hw] + jnp.concatenate([krr] * MLA_H, axis=1)).astype(k_ref.dtype)
    v_ref[...] = kv[:, hw:].astype(v_ref.dtype)


def _mla_prep(mla_in, ctab, stab, qn, kvn, wq2, wkv2, tm=256):
    t = mla_in.shape[0]
    hw = MLA_H * LANES
    row = lambda i: (i, 0)
    fix = lambda i: (0, 0)
    return pl.pallas_call(
        _mla_prep_kernel,
        grid=(t // tm,),
        in_specs=[pl.BlockSpec((tm, mla_in.shape[1]), row), pl.BlockSpec((tm, LANES), row), pl.BlockSpec((tm, LANES), row),
                  pl.BlockSpec(qn.shape, fix), pl.BlockSpec(kvn.shape, fix),
                  pl.BlockSpec(wq2.shape, fix), pl.BlockSpec(wkv2.shape, fix)],
        out_specs=[pl.BlockSpec((tm, hw), row)] * 3,
        out_shape=[jax.ShapeDtypeStruct((t, hw), BF16)] * 3,
        compiler_params=_cparams(("arbitrary",)),
        name="mla_prep",
    )(mla_in, ctab, stab, qn, kvn, wq2, wkv2)


def _mla_attn_kernel(q_ref, k_ref, v_ref, o_ref, *, tq):
    i = pl.program_id(2)
    neg = -1e30

    def head(hh):
        q = q_ref[:, hh * LANES:(hh + 1) * LANES]

        def chunk(j, carry, masked):
            m, l, acc = carry
            start = pl.multiple_of(j * tq, tq)
            kc = k_ref[pl.ds(start, tq), hh * LANES:(hh + 1) * LANES]
            vc = v_ref[pl.ds(start, tq), hh * LANES:(hh + 1) * LANES]
            s = _dot_nt(q, kc)
            if masked:
                s = jnp.where(_iota2(s.shape, 0) >= _iota2(s.shape, 1), s, neg)
            m_new = jnp.maximum(m, jnp.max(s, axis=-1, keepdims=True))
            alpha = jnp.exp(m - m_new)
            p = jnp.exp(s - m_new)
            l = alpha * l + jnp.sum(p, axis=-1, keepdims=True)
            acc = alpha * acc + _dot(p.astype(BF16), vc)
            return m_new, l, acc

        init = (jnp.full((tq, 1), neg, F32), jnp.zeros((tq, 1), F32), jnp.zeros((tq, LANES), F32))
        carry = lax.fori_loop(0, i, lambda j, c: chunk(j, c, False), init)
        m, l, acc = chunk(i, carry, True)
        return acc / l

    o_ref[...] = (head(0) + head(1)).astype(o_ref.dtype)


def _mla_attn(q, k, v, batch, seq, tq=256):
    nq = seq // tq
    pairs = MLA_H // 2
    return pl.pallas_call(
        functools.partial(_mla_attn_kernel, tq=tq),
        grid=(batch, pairs, nq),
        in_specs=[pl.BlockSpec((tq, 2 * LANES), lambda b, p, i: (b * nq + i, p)),
                  pl.BlockSpec((seq, 2 * LANES), lambda b, p, i: (b, p)),
                  pl.BlockSpec((seq, 2 * LANES), lambda b, p, i: (b, p))],
        out_specs=pl.BlockSpec((tq, LANES), lambda b, p, i: (b * nq + i, p)),
        out_shape=jax.ShapeDtypeStruct((batch * seq, pairs * LANES), BF16),
        compiler_params=_cparams(("arbitrary", "arbitrary", "arbitrary")),
        name="mla_attn",
    )(q, k, v)


def _unit_lower_inverse(n):
    c = n.shape[0]
    eye = (_iota2((c, c), 0) == _iota2((c, c), 1)).astype(F32)
    x = -n
    p = eye + x
    xh, xl = _split2(x)
    for _ in range(int(math.log2(c)) - 1):
        x = _dot(xh, xh) + (_dot(xh, xl) + _dot(xl, xh))
        xh, xl = _split2(x)
        ph, plo = _split2(p)
        p = p + (_dot(ph, xh) + (_dot(ph, xl) + _dot(plo, xh)))
    return p


def _gdn_kernel(qkv_ref, g_ref, z_ref, cw_ref, al_ref, dt_ref, on_ref, o_ref, ext_ref, st_ref):
    @pl.when(pl.program_id(1) == 0)
    def _():
        ext_ref[:, 0:8, :] = jnp.zeros((ext_ref.shape[0], 8, ext_ref.shape[2]), F32)
        st_ref[...] = jnp.zeros(st_ref.shape, F32)

    for bb in range(qkv_ref.shape[0]):
        _gdn_seq(qkv_ref.at[bb], g_ref.at[bb], z_ref.at[bb], cw_ref, al_ref, dt_ref, on_ref, o_ref.at[bb],
                 ext_ref.at[bb], st_ref.at[bb])


def _gdn_seq(qkv_ref, g_ref, z_ref, cw_ref, al_ref, dt_ref, on_ref, o_ref, ext_ref, st_ref):
    c = GDN_CHUNK
    hd = GDN_DK
    nqk = GDN_H * GDN_DK
    ext_ref[8:8 + c, :] = qkv_ref[...]
    conv = cw_ref[0:1, :] * ext_ref[5:5 + c, :]
    for j in range(1, CONV_W):
        conv = conv + cw_ref[j:j + 1, :] * ext_ref[5 + j:5 + j + c, :]
    ext_ref[0:8, :] = ext_ref[c:c + 8, :]
    act = _silu(conv)

    gates = g_ref[...]
    beta_all = _sigmoid(gates)
    g_all = -jnp.exp(al_ref[...]) * _softplus(gates + dt_ref[...])
    tri = (_iota2((c, c), 0) >= _iota2((c, c), 1)).astype(F32)
    gc_all = _dot_sel(tri, g_all)
    row_ge = _iota2((c, c), 0) >= _iota2((c, c), 1)
    row_gt = _iota2((c, c), 0) > _iota2((c, c), 1)
    ones = jnp.ones((c, LANES), F32)
    lane = _iota2((c, LANES), 1)

    for h in range(GDN_H):
        q = act[:, h * hd:(h + 1) * hd]
        k = act[:, nqk + h * hd:nqk + (h + 1) * hd]
        v = act[:, 2 * nqk + h * GDN_DV:2 * nqk + (h + 1) * GDN_DV]
        q = q * lax.rsqrt(jnp.sum(q * q, axis=-1, keepdims=True) + EPS) * (GDN_DK ** -0.5)
        k = k * lax.rsqrt(jnp.sum(k * k, axis=-1, keepdims=True) + EPS)
        beta = _lane_bcast(beta_all, h)
        gcol = _lane_bcast(gc_all, GDN_H + h)
        grow = _dot_sel(ones, jnp.where(lane == GDN_H + h, gc_all, 0.0), _dot_nt)
        diff = gcol[:, :c] - grow
        decay = jnp.exp(jnp.where(row_ge, diff, -jnp.inf))
        kb = k * beta
        lower = jnp.where(row_gt, _dot3(kb, k, _dot_nt) * decay, 0.0)
        tinv = _unit_lower_inverse(lower)
        eg = jnp.exp(gcol)
        th, tl = _split2(tinv)
        rhs_h, rhs_l = _split2(jnp.concatenate([v * beta, kb * eg], axis=1))
        uw = _dot(th, rhs_h) + (_dot(th, rhs_l) + _dot(tl, rhs_h))
        u = uw[:, :GDN_DV]
        w = uw[:, GDN_DV:]
        kbf = k.astype(BF16)
        attn = _dot_nt(q.astype(BF16), kbf) * decay
        glast = gcol[c - 1:c, :]
        qg = q * eg
        kg = k * jnp.exp(glast - gcol)
        state = st_ref[h]
        sb = state.astype(BF16)
        v_new = u - _dot(w.astype(BF16), sb)
        vnb = v_new.astype(BF16)
        o = _dot(qg.astype(BF16), sb) + _dot(attn.astype(BF16), vnb)
        st_ref[h] = state * jnp.exp(glast) + _dot_tn(kg.astype(BF16), vnb)
        o = _rms(o, on_ref[...]) * _silu(z_ref[:, h * GDN_DV:(h + 1) * GDN_DV])
        o_ref[:, h * GDN_DV:(h + 1) * GDN_DV] = o.astype(o_ref.dtype)


def _gdn(qkv, gates, z, conv_w, a_row, dt_row, o_norm, batch, seq):
    c = GDN_CHUNK
    nc = seq // c
    w3 = qkv.shape[1]
    wo = GDN_H * GDN_DV
    nb = SEQS_PER_STEP
    row = lambda b, i: (b, i, 0)
    fix = lambda b, i: (0, 0)
    out = pl.pallas_call(
        _gdn_kernel,
        grid=(batch // nb, nc),
        in_specs=[pl.BlockSpec((nb, c, w3), row), pl.BlockSpec((nb, c, LANES), row), pl.BlockSpec((nb, c, wo), row),
                  pl.BlockSpec(conv_w.shape, fix), pl.BlockSpec((1, LANES), fix), pl.BlockSpec((1, LANES), fix),
                  pl.BlockSpec((1, GDN_DV), fix)],
        out_specs=pl.BlockSpec((nb, c, wo), row),
        out_shape=jax.ShapeDtypeStruct((batch, seq, wo), BF16),
        scratch_shapes=[pltpu.VMEM((nb, c + 8, w3), F32), pltpu.VMEM((nb, GDN_H, GDN_DK, GDN_DV), F32)],
        compiler_params=_cparams(("arbitrary", "arbitrary")),
        name="gdn",
    )(qkv.reshape(batch, seq, w3), gates.reshape(batch, seq, LANES), z.reshape(batch, seq, wo), conv_w, a_row, dt_row, o_norm)
    return out.reshape(batch * seq, wo)


def _mlstm_kernel(q_ref, k_ref, v_ref, og_ref, g_ref, bias_ref, nrm_ref, o_ref, c_ref, n_ref, m_ref):
    @pl.when(pl.program_id(1) == 0)
    def _():
        c_ref[...] = jnp.zeros(c_ref.shape, F32)
        n_ref[...] = jnp.zeros(n_ref.shape, F32)
        m_ref[...] = jnp.zeros(m_ref.shape, F32)

    for bb in range(q_ref.shape[0]):
        _mlstm_seq(q_ref.at[bb], k_ref.at[bb], v_ref.at[bb], og_ref.at[bb], g_ref.at[bb], bias_ref, nrm_ref,
                   o_ref.at[bb], c_ref.at[bb], n_ref.at[bb], m_ref.at[bb])


def _mlstm_seq(q_ref, k_ref, v_ref, og_ref, g_ref, bias_ref, nrm_ref, o_ref, c_ref, n_ref, m_ref):
    c = ML_CHUNK
    pre = g_ref[...] + bias_ref[...]
    logf = jnp.minimum(pre, 0.0) - jnp.log(1.0 + jnp.exp(-jnp.abs(pre)))
    tri = (_iota2((c, c), 0) >= _iota2((c, c), 1)).astype(F32)
    bcum_all = _dot_sel(tri, logf)
    row_ge = _iota2((c, c), 0) >= _iota2((c, c), 1)
    ones = jnp.ones((c, LANES), F32)
    lane = _iota2((c, LANES), 1)

    for h in range(ML_H):
        q = q_ref[:, h * LANES:(h + 1) * LANES]
        k = k_ref[:, h * LANES:(h + 1) * LANES] * (ML_DK ** -0.5)
        v = v_ref[:, h * ML_DV:(h + 1) * ML_DV]
        bcol = _lane_bcast(bcum_all, ML_H + h)
        icol = _lane_bcast(pre, h)
        brow = _dot_sel(ones, jnp.where(lane == ML_H + h, bcum_all, 0.0), _dot_nt)
        irow = _dot_sel(ones, jnp.where(lane == h, pre, 0.0), _dot_nt)
        m_st = m_ref[h]
        d = jnp.where(row_ge, bcol[:, :c] - brow + irow, -jnp.inf)
        inter = bcol + m_st
        m_t = jnp.maximum(inter, jnp.max(d, axis=-1, keepdims=True))
        w_inter = jnp.exp(inter - m_t)
        qb = q.astype(BF16)
        vb = v.astype(BF16)
        p = jnp.exp(d - m_t[:, :c]) * _dot_nt(qb, k.astype(BF16))
        cst = c_ref[h]
        num = w_inter * _dot(qb, cst.astype(BF16)) + _dot(p.astype(BF16), vb)
        den = w_inter * jnp.sum(q * n_ref[h], axis=-1, keepdims=True) + jnp.sum(p, axis=-1, keepdims=True)
        hc = num / jnp.maximum(jnp.abs(den), jnp.exp(-m_t))
        b_end = bcol[c - 1:c, :]
        a = b_end - bcol + icol
        m_new = jnp.maximum(b_end + m_st, jnp.max(a, axis=0, keepdims=True))
        s_a = jnp.exp(a - m_new)
        keep = jnp.exp(b_end + m_st - m_new)
        ks = k * s_a
        c_ref[h] = cst * keep + _dot_tn(ks.astype(BF16), vb)
        n_ref[h] = n_ref[h] * keep + jnp.sum(ks, axis=0, keepdims=True)
        m_ref[h] = m_new
        hn = _rms(hc, nrm_ref[:, h * ML_DV:(h + 1) * ML_DV]) * _sigmoid(og_ref[:, h * ML_DV:(h + 1) * ML_DV])
        o_ref[:, h * ML_DV:(h + 1) * ML_DV] = hn.astype(o_ref.dtype)


def _mlstm(mq, mk, mv, mo, gates, bias_row, norm_row, batch, seq):
    c = ML_CHUNK
    nc = seq // c
    nb = SEQS_PER_STEP
    row = lambda b, i: (b, i, 0)
    fix = lambda b, i: (0, 0)
    wide = ML_H * LANES
    r3 = lambda a: a.reshape(batch, seq, a.shape[-1])
    out = pl.pallas_call(
        _mlstm_kernel,
        grid=(batch // nb, nc),
        in_specs=[pl.BlockSpec((nb, c, wide), row), pl.BlockSpec((nb, c, wide), row), pl.BlockSpec((nb, c, wide), row),
                  pl.BlockSpec((nb, c, wide), row), pl.BlockSpec((nb, c, LANES), row),
                  pl.BlockSpec((1, LANES), fix), pl.BlockSpec((1, wide), fix)],
        out_specs=pl.BlockSpec((nb, c, wide), row),
        out_shape=jax.ShapeDtypeStruct((batch, seq, wide), BF16),
        scratch_shapes=[pltpu.VMEM((nb, ML_H, LANES, ML_DV), F32), pltpu.VMEM((nb, ML_H, 1, LANES), F32),
                        pltpu.VMEM((nb, ML_H, 1, LANES), F32)],
        compiler_params=_cparams(("arbitrary", "arbitrary")),
        name="mlstm",
    )(r3(mq), r3(mk), r3(mv), r3(mo), r3(gates), bias_row, norm_row)
    return out.reshape(batch * seq, wide)


def _swa_kernel(q_ref, kc_ref, kp_ref, vc_ref, vp_ref, sink_ref, o_ref):
    w = WINDOW
    n = pl.program_id(1)
    scale = SWA_D ** -0.5
    qi = _iota2((w, w), 0)
    kj = _iota2((w, w), 1)
    mask_c = kj <= qi
    mask_p = jnp.logical_and(kj > qi, n > 0)
    grp = SWA_H // SWA_KV
    neg = -1e30
    for pair in range(SWA_H // 2):
        acc = jnp.zeros((w, LANES), F32)
        for sub in range(2):
            h = 2 * pair + sub
            g = h // grp
            q = q_ref[:, h * LANES:(h + 1) * LANES]
            kc = kc_ref[:, g * LANES:(g + 1) * LANES]
            kp = kp_ref[:, g * LANES:(g + 1) * LANES]
            vcol = (2 * g + sub) * LANES
            vc = vc_ref[:, vcol:vcol + LANES]
            vp = vp_ref[:, vcol:vcol + LANES]
            s_c = jnp.where(mask_c, _dot_nt(q, kc) * scale, neg)
            s_p = jnp.where(mask_p, _dot_nt(q, kp) * scale, neg)
            sink = sink_ref[:, h:h + 1]
            m = jnp.maximum(jnp.maximum(jnp.max(s_c, axis=-1, keepdims=True), jnp.max(s_p, axis=-1, keepdims=True)), sink)
            p_c = jnp.where(mask_c, jnp.exp(s_c - m), 0.0)
            p_p = jnp.where(mask_p, jnp.exp(s_p - m), 0.0)
            den = jnp.sum(p_c, axis=-1, keepdims=True) + jnp.sum(p_p, axis=-1, keepdims=True) + jnp.exp(sink - m)
            inv = 1.0 / den
            acc = acc + _dot((p_c * inv).astype(BF16), vc) + _dot((p_p * inv).astype(BF16), vp)
        o_ref[:, pair * LANES:(pair + 1) * LANES] = acc.astype(o_ref.dtype)


def _swa(sq, sk, sv, sinks_row, batch, seq):
    w = WINDOW
    nb = seq // w
    cur = lambda b, n: (b * nb + n, 0)
    prev = lambda b, n: (b * nb + jnp.maximum(n - 1, 0), 0)
    return pl.pallas_call(
        _swa_kernel,
        grid=(batch, nb),
        in_specs=[pl.BlockSpec((w, sq.shape[1]), cur),
                  pl.BlockSpec((w, sk.shape[1]), cur), pl.BlockSpec((w, sk.shape[1]), prev),
                  pl.BlockSpec((w, sv.shape[1]), cur), pl.BlockSpec((w, sv.shape[1]), prev),
                  pl.BlockSpec((1, LANES), lambda b, n: (0, 0))],
        out_specs=pl.BlockSpec((w, SWA_H * SWA_D), cur),
        out_shape=jax.ShapeDtypeStruct((batch * seq, SWA_H * SWA_D), BF16),
        compiler_params=_cparams(("arbitrary", "arbitrary")),
        name="swa",
    )(sq, sk, sk, sv, sv, sinks_row)


def _layer_norm(h, g, b):
    mu = jnp.mean(h, axis=-1, keepdims=True)
    d = h - mu
    var = jnp.mean(d * d, axis=-1, keepdims=True)
    return d * lax.rsqrt(var + LN_EPS) * g + b


def _outproj_kernel(x_ref, a1_ref, a2_ref, w_ref, g_ref, b_ref, o_ref):
    k1 = a1_ref.shape[1]
    y = _dot(a1_ref[...].astype(BF16), w_ref[0:k1, :]) + _dot(a2_ref[...].astype(BF16), w_ref[k1:, :])
    o_ref[...] = _layer_norm(DN_ALPHA * x_ref[...] + y, g_ref[...], b_ref[...])


def _outproj_ln(x, a1, a2, w, g, b, tm=256):
    t, d = x.shape
    row = lambda i: (i, 0)
    fix = lambda i: (0, 0)
    return pl.pallas_call(
        _outproj_kernel,
        grid=(t // tm,),
        in_specs=[pl.BlockSpec((tm, d), row), pl.BlockSpec((tm, a1.shape[1]), row), pl.BlockSpec((tm, a2.shape[1]), row),
                  pl.BlockSpec(w.shape, fix), pl.BlockSpec((1, d), fix), pl.BlockSpec((1, d), fix)],
        out_specs=pl.BlockSpec((tm, d), row),
        out_shape=jax.ShapeDtypeStruct((t, d), F32),
        compiler_params=_cparams(("arbitrary",)),
        name="outproj_ln",
    )(x, a1, a2, w, g, b)


def _first_index(x, m, iota_f, sentinel):
    return jnp.min(jnp.where(x == m, iota_f, sentinel), axis=0, keepdims=True)


def _router_kernel(x_ref, wt_ref, bias_ref, idx_ref, gate_ref, rank_ref, cnt_ref, carry_ref):
    tm = x_ref.shape[0]
    e = N_EXPERTS
    gs = e // N_GROUPS
    ninf = -jnp.inf

    @pl.when(pl.program_id(0) == 0)
    def _():
        carry_ref[...] = jnp.zeros(carry_ref.shape, F32)

    logits = _dot_nt(wt_ref[...], x_ref[...], HI)
    scores = _sigmoid(logits)
    sel = scores + bias_ref[:, 0:1]

    sub_f = _iota2((gs, tm), 0).astype(F32)
    gscore = []
    for g in range(N_GROUPS):
        blk = sel[g * gs:(g + 1) * gs, :]
        m1 = jnp.max(blk, axis=0, keepdims=True)
        i1 = _first_index(blk, m1, sub_f, float(gs))
        m2 = jnp.max(jnp.where(sub_f == i1, ninf, blk), axis=0, keepdims=True)
        gscore.append(m1 + m2)
    gsc = jnp.concatenate(gscore, axis=0)
    grp_f = _iota2((N_GROUPS, tm), 0).astype(F32)
    gmask = jnp.zeros((N_GROUPS, tm), F32)
    for _ in range(TOPK_GROUPS):
        m = jnp.max(gsc, axis=0, keepdims=True)
        gi = _first_index(gsc, m, grp_f, float(N_GROUPS))
        hit = grp_f == gi
        gmask = jnp.where(hit, 1.0, gmask)
        gsc = jnp.where(hit, ninf, gsc)
    masked = jnp.concatenate(
        [jnp.where(gmask[g:g + 1, :] > 0.0, sel[g * gs:(g + 1) * gs, :], ninf) for g in range(N_GROUPS)], axis=0)

    exp_f = _iota2((e, tm), 0).astype(F32)
    chosen = jnp.zeros((e, tm), F32)
    idxs, gates = [], []
    for _ in range(TOP_K):
        m = jnp.max(masked, axis=0, keepdims=True)
        ei = _first_index(masked, m, exp_f, float(e))
        hit = exp_f == ei
        idxs.append(ei)
        gates.append(jnp.sum(jnp.where(hit, scores, 0.0), axis=0, keepdims=True))
        chosen = jnp.where(hit, 1.0, chosen)
        masked = jnp.where(hit, ninf, masked)
    gate = jnp.concatenate(gates, axis=0)
    gate = gate / jnp.sum(gate, axis=0, keepdims=True) * ROUTED_SCALE
    idx_f = jnp.concatenate(idxs, axis=0)

    upper = (_iota2((tm, tm), 0) < _iota2((tm, tm), 1)).astype(BF16)
    before = _dot(chosen.astype(BF16), upper) + carry_ref[...][:, 0:1]
    ranks = [jnp.sum(jnp.where(exp_f == idxs[k], before, 0.0), axis=0, keepdims=True) for k in range(TOP_K)]
    carry_ref[...] = carry_ref[...] + jnp.sum(chosen, axis=1, keepdims=True)

    idx_ref[...] = idx_f.astype(jnp.int32)
    gate_ref[...] = gate
    rank_ref[...] = jnp.concatenate(ranks, axis=0).astype(jnp.int32)
    cnt_ref[...] = carry_ref[...]


def _router(x, wt, bias_col, tm=512):
    t, d = x.shape
    col = lambda i: (0, i)
    fix = lambda i: (0, 0)
    return pl.pallas_call(
        _router_kernel,
        grid=(t // tm,),
        in_specs=[pl.BlockSpec((tm, d), lambda i: (i, 0)), pl.BlockSpec(wt.shape, fix), pl.BlockSpec((N_EXPERTS, LANES), fix)],
        out_specs=[pl.BlockSpec((TOP_K, tm), col), pl.BlockSpec((TOP_K, tm), col), pl.BlockSpec((TOP_K, tm), col),
                   pl.BlockSpec((N_EXPERTS, LANES), fix)],
        out_shape=[jax.ShapeDtypeStruct((TOP_K, t), jnp.int32), jax.ShapeDtypeStruct((TOP_K, t), F32),
                   jax.ShapeDtypeStruct((TOP_K, t), jnp.int32), jax.ShapeDtypeStruct((N_EXPERTS, LANES), F32)],
        scratch_shapes=[pltpu.VMEM((N_EXPERTS, LANES), F32)],
        compiler_params=_cparams(("arbitrary",)),
        name="router",
    )(x, wt, bias_col)


def _dest_kernel(idx_ref, rank_ref, start_ref, dest_ref):
    tm = idx_ref.shape[1]
    exp_i = _iota2((N_EXPERTS, tm), 0)
    start = start_ref[:, 0:1]
    rows = [jnp.sum(jnp.where(exp_i == idx_ref[s:s + 1, :], start, 0.0), axis=0, keepdims=True) for s in range(TOP_K)]
    dest_ref[...] = jnp.concatenate(rows, axis=0).astype(jnp.int32) + rank_ref[...]


def _dest_rows(idx, rank, start_col, tm=2048):
    t = idx.shape[1]
    tm = min(tm, t)
    col = lambda i: (0, i)
    return pl.pallas_call(
        _dest_kernel,
        grid=(t // tm,),
        in_specs=[pl.BlockSpec((TOP_K, tm), col), pl.BlockSpec((TOP_K, tm), col),
                  pl.BlockSpec((N_EXPERTS, LANES), lambda i: (0, 0))],
        out_specs=pl.BlockSpec((TOP_K, tm), col),
        out_shape=jax.ShapeDtypeStruct((TOP_K, t), jnp.int32),
        compiler_params=_cparams(("arbitrary",)),
        name="moe_dest",
    )(idx, rank, start_col)


def _dispatch_kernel(dest_ref, fill_ref, x_ref, xs_ref, zero_ref, sem, zsem):
    tm = x_ref.shape[0]

    def row_copy(t, s):
        return pltpu.make_async_copy(x_ref.at[pl.ds(t, 1), :], xs_ref.at[pl.ds(dest_ref[s, t], 1), :], sem)

    def issue(t, _):
        for s in range(TOP_K):
            row_copy(t, s).start()
        return 0

    lax.fori_loop(0, tm, issue, 0)

    @pl.when(pl.program_id(0) == 0)
    def _():
        zero_ref[...] = jnp.zeros(zero_ref.shape, F32)

        def pad_copy(r):
            return pltpu.make_async_copy(zero_ref, xs_ref.at[pl.ds(r, 1), :], zsem)

        def per_expert(e, _):
            start = fill_ref[e, 0]
            cnt = fill_ref[e, 1]
            lax.fori_loop(0, cnt, lambda r, c: (pad_copy(start + r).start(), c)[1], 0)
            lax.fori_loop(0, cnt, lambda r, c: (pad_copy(start + r).wait(), c)[1], 0)
            return 0

        lax.fori_loop(0, N_EXPERTS, per_expert, 0)

    for s in range(TOP_K):
        pltpu.make_async_copy(x_ref, xs_ref.at[pl.ds(0, tm), :], sem).wait()


def _dispatch(x, dest, fill, rows, tm=256):
    t, d = x.shape
    return pl.pallas_call(
        _dispatch_kernel,
        grid=(t // tm,),
        in_specs=[pl.BlockSpec((TOP_K, tm), lambda i: (0, i), memory_space=pltpu.SMEM),
                  pl.BlockSpec(memory_space=pltpu.SMEM),
                  pl.BlockSpec((tm, d), lambda i: (i, 0))],
        out_specs=pl.BlockSpec(memory_space=pl.ANY),
        out_shape=jax.ShapeDtypeStruct((rows, d), F32),
        scratch_shapes=[pltpu.VMEM((1, d), F32), pltpu.SemaphoreType.DMA(()), pltpu.SemaphoreType.DMA(())],
        compiler_params=_cparams(("arbitrary",)),
        name="moe_dispatch",
    )(dest, fill, x)


def _experts_kernel(be_ref, nu_ref, xs_ref, wg_ref, wu_ref, wd_ref, ys_ref):
    i = pl.program_id(0)

    @pl.when(i < nu_ref[0])
    def _():
        xb = xs_ref[...].astype(BF16)
        h = _silu(_dot(xb, wg_ref[0])) * _dot(xb, wu_ref[0])
        ys_ref[...] = _dot(h.astype(BF16), wd_ref[0])

    @pl.when(i >= nu_ref[0])
    def _():
        ys_ref[...] = jnp.zeros(ys_ref.shape, F32)


def _experts(block_e, n_used, xs, wg, wu, wd):
    rows, d = xs.shape
    nb = rows // EXPERT_BLOCK
    blk = lambda i, be, nu: (jnp.minimum(i, nu[0] - 1), 0)
    wsel = lambda i, be, nu: (be[i], 0, 0)
    return pl.pallas_call(
        _experts_kernel,
        grid_spec=pltpu.PrefetchScalarGridSpec(
            num_scalar_prefetch=2,
            grid=(nb,),
            in_specs=[pl.BlockSpec((EXPERT_BLOCK, d), blk),
                      pl.BlockSpec((1, d, D_EXPERT), wsel), pl.BlockSpec((1, d, D_EXPERT), wsel),
                      pl.BlockSpec((1, D_EXPERT, d), wsel)],
            out_specs=pl.BlockSpec((EXPERT_BLOCK, d), lambda i, be, nu: (i, 0)),
        ),
        out_shape=jax.ShapeDtypeStruct((rows, d), F32),
        compiler_params=_cparams(("arbitrary",)),
        name="moe_experts",
    )(block_e, n_used, xs, wg, wu, wd)


def _combine_kernel(dest_ref, x_ref, gate_ref, ys_ref, sg_ref, su_ref, sd_ref, g_ref, b_ref, o_ref, buf_ref, sem):
    tm = x_ref.shape[0]

    def row_copy(t, s):
        return pltpu.make_async_copy(ys_ref.at[pl.ds(dest_ref[s, t], 1), :], buf_ref.at[s, pl.ds(t, 1), :], sem)

    def issue(t, _):
        for s in range(TOP_K):
            row_copy(t, s).start()
        return 0

    lax.fori_loop(0, tm, issue, 0)

    x = x_ref[...]
    xb = x.astype(BF16)
    hs = _silu(_dot(xb, sg_ref[...])) * _dot(xb, su_ref[...])
    ff = _dot(hs.astype(BF16), sd_ref[...])

    for s in range(TOP_K):
        pltpu.make_async_copy(ys_ref.at[pl.ds(0, tm), :], buf_ref.at[s], sem).wait()

    gate = gate_ref[...]
    for s in range(TOP_K):
        ff = ff + gate[:, s:s + 1] * buf_ref[s]
    o_ref[...] = _layer_norm(DN_ALPHA * x + ff, g_ref[...], b_ref[...])


def _combine(dest, x, gate_t, ys, sg, su, sd, g, b, tm=256):
    t, d = x.shape
    row = lambda i: (i, 0)
    fix = lambda i: (0, 0)
    return pl.pallas_call(
        _combine_kernel,
        grid=(t // tm,),
        in_specs=[pl.BlockSpec((TOP_K, tm), lambda i: (0, i), memory_space=pltpu.SMEM),
                  pl.BlockSpec((tm, d), row), pl.BlockSpec((tm, TOP_K), row),
                  pl.BlockSpec(memory_space=pl.ANY),
                  pl.BlockSpec(sg.shape, fix), pl.BlockSpec(su.shape, fix), pl.BlockSpec(sd.shape, fix),
                  pl.BlockSpec((1, d), fix), pl.BlockSpec((1, d), fix)],
        out_specs=pl.BlockSpec((tm, d), row),
        out_shape=jax.ShapeDtypeStruct((t, d), F32),
        scratch_shapes=[pltpu.VMEM((TOP_K, tm, d), F32), pltpu.SemaphoreType.DMA(())],
        compiler_params=_cparams(("arbitrary",)),
        name="moe_combine",
    )(dest, x, gate_t, ys, sg, su, sd, g, b)


def _take_cols(w, idx):
    wz = jnp.concatenate([w, jnp.zeros((w.shape[0], 1), w.dtype)], axis=1)
    idx = np.where(np.asarray(idx) < 0, w.shape[1], np.asarray(idx))
    return jnp.take(wz, jnp.asarray(idx, jnp.int32), axis=1)


def _pad_lane_row(v, first_lane, width=LANES):
    out = jnp.zeros((1, width), F32)
    return lax.dynamic_update_slice(out, v.reshape(1, -1).astype(F32), (0, first_lane))


def _even_in_cols():
    z = lambda n: -np.ones(n, int)
    kr0 = Q_LORA + KV_LORA
    half = MLA_ROPE // 2
    cols = [np.arange(0, Q_LORA), np.arange(Q_LORA, Q_LORA + KV_LORA),
            z(64), np.arange(kr0, kr0 + MLA_ROPE), z(32),
            z(64), np.arange(kr0 + half, kr0 + MLA_ROPE), np.arange(kr0, kr0 + half), z(32)]
    g0 = kr0 + MLA_ROPE
    nqk = GDN_H * GDN_DK
    cols.append(np.arange(g0, g0 + 3 * nqk))
    zoff = g0 + 3 * nqk + 2 * GDN_H
    cols.append(np.arange(zoff, zoff + GDN_H * GDN_DV))
    cols += [np.arange(g0 + 3 * nqk, g0 + 3 * nqk + 2 * GDN_H), z(LANES - 2 * GDN_H)]
    return np.concatenate(cols)


EV_WIDTHS = (Q_LORA + KV_LORA + 2 * LANES, 3 * GDN_H * GDN_DK, GDN_H * GDN_DV, LANES)


def _mla_q_cols():
    per = MLA_NOPE + MLA_ROPE
    half = MLA_ROPE // 2
    main, sw = [], []
    for h in range(MLA_H):
        b = h * per
        main += [np.arange(b, b + per), -np.ones(LANES - per, int)]
        sw += [-np.ones(MLA_NOPE, int), np.arange(b + MLA_NOPE + half, b + per), np.arange(b + MLA_NOPE, b + MLA_NOPE + half),
               -np.ones(LANES - per, int)]
    return np.concatenate(main + sw)


def _mla_kv_cols():
    per = MLA_NOPE + MLA_V
    kc, vc = [], []
    for h in range(MLA_H):
        b = h * per
        kc += [np.arange(b, b + MLA_NOPE), -np.ones(LANES - MLA_NOPE, int)]
        vv = np.arange(b + MLA_NOPE, b + per)
        pad = -np.ones(LANES - MLA_V, int)
        vc += [vv, pad] if h % 2 == 0 else [pad, vv]
    return np.concatenate(kc + vc)


def _odd_in_cols():
    z = lambda n: -np.ones(n, int)
    o = 0
    cols = []
    mq0, mk0 = 0, ML_H * ML_DK
    for base in (mq0, mk0):
        for h in range(ML_H):
            cols += [np.arange(base + h * ML_DK, base + (h + 1) * ML_DK), z(LANES - ML_DK)]
    mv0 = 2 * ML_H * ML_DK
    cols.append(np.arange(mv0, mv0 + ML_H * ML_DV))
    mi0 = mv0 + ML_H * ML_DV
    mo0 = mi0 + 2 * ML_H
    cols.append(np.arange(mo0, mo0 + ML_H * ML_DV))
    cols += [np.arange(mi0, mi0 + 2 * ML_H), z(LANES - 2 * ML_H)]
    sq0 = mo0 + ML_H * ML_DV
    sk0 = sq0 + SWA_H * SWA_D
    sv0 = sk0 + SWA_KV * SWA_D
    half = SWA_D // 2

    def heads(base, n, swapped):
        out = []
        for h in range(n):
            b = base + h * SWA_D
            if swapped:
                out += [np.arange(b + half, b + SWA_D), np.arange(b, b + half), z(LANES - SWA_D)]
            else:
                out += [np.arange(b, b + SWA_D), z(LANES - SWA_D)]
        return out

    cols += heads(sq0, SWA_H, False) + heads(sq0, SWA_H, True) + heads(sk0, SWA_KV, False) + heads(sk0, SWA_KV, True)
    for g in range(SWA_KV):
        vv = np.arange(sv0 + g * SWA_D, sv0 + (g + 1) * SWA_D)
        cols += [vv, z(LANES - SWA_D), z(LANES - SWA_D), vv]
    return np.concatenate(cols)


def _even_mixer(x, tabs, w_in, q_norm, w_qb, kv_norm, w_kvb, conv_w, a_log, dt_bias, o_norm, batch, seq):
    ctab, stab = tabs
    w = _take_cols(w_in, _even_in_cols()).astype(BF16)
    mla_in, qkv, z, gates = _proj(x, w, EV_WIDTHS, (F32, F32, F32, F32))
    wq2 = _take_cols(w_qb, _mla_q_cols()).astype(BF16)
    wkv2 = _take_cols(w_kvb, _mla_kv_cols()).astype(BF16)
    q, k, v = _mla_prep(mla_in, ctab, stab, q_norm.reshape(1, -1), kv_norm.reshape(1, -1), wq2, wkv2)
    o_a = _mla_attn(q, k, v, batch, seq)
    o_b = _gdn(qkv, gates, z, conv_w, _pad_lane_row(a_log, GDN_H), _pad_lane_row(dt_bias, GDN_H),
               o_norm.reshape(1, -1), batch, seq)
    return o_a, o_b


def _odd_mixer(x, tabs, w_in, b_i, b_f, ml_norm, sinks, batch, seq):
    ctab, stab = tabs
    w = _take_cols(w_in, _odd_in_cols()).astype(BF16)
    mq, mk, mv, mo, mg, sq, sk, sv = _proj_odd(x, w, ctab, stab)
    bias_row = _pad_lane_row(jnp.concatenate([b_i, b_f]), 0)
    o_c = _mlstm(mq, mk, mv, mo, mg, bias_row, ml_norm.reshape(1, -1), batch, seq)
    o_d = _swa(sq, sk, sv, _pad_lane_row(sinks, 0), batch, seq)
    return o_c, o_d


def _moe(x, router_w, router_b, w_gate, w_up, w_down, s_gate, s_up, s_down, ln_g, ln_b):
    t, d = x.shape
    bias_col = jnp.broadcast_to(router_b.reshape(-1, 1).astype(F32), (N_EXPERTS, LANES))
    idx, gate, rank, cnt = _router(x, router_w.T, bias_col)
    counts = cnt[:, 0].astype(jnp.int32)
    padded = (counts + EXPERT_BLOCK - 1) // EXPERT_BLOCK * EXPERT_BLOCK
    pad_end = jnp.cumsum(padded)
    pad_start = pad_end - padded
    start_col = jnp.broadcast_to(pad_start.astype(F32).reshape(-1, 1), (N_EXPERTS, LANES))
    dest = _dest_rows(idx, rank, start_col)
    n_blocks = t * TOP_K // EXPERT_BLOCK + N_EXPERTS
    rows = n_blocks * EXPERT_BLOCK
    block_row = jnp.arange(n_blocks, dtype=jnp.int32) * EXPERT_BLOCK
    block_e = jnp.minimum(jnp.sum((pad_end[None, :] <= block_row[:, None]).astype(jnp.int32), axis=1), N_EXPERTS - 1)
    n_used = (pad_end[-1:] // EXPERT_BLOCK).astype(jnp.int32)
    fill = jnp.stack([pad_start + counts, padded - counts], axis=1).astype(jnp.int32)
    xs = _dispatch(x, dest, fill, rows)
    ys = _experts(block_e, n_used, xs, w_gate.astype(BF16), w_up.astype(BF16), w_down.astype(BF16))
    return _combine(dest, x, gate.T, ys, s_gate.astype(BF16), s_up.astype(BF16), s_down.astype(BF16),
                    ln_g.reshape(1, -1), ln_b.reshape(1, -1))


def kernel(x, positions, ev_w_in, mla_q_norm, mla_w_qb, mla_kv_norm, mla_w_kvb, gdn_conv, gdn_a_log, gdn_dt_bias, gdn_norm, ev_w_out, od_w_in, mlstm_b_i, mlstm_b_f, mlstm_norm, swa_sinks, od_w_out, ln1_g, ln1_b, router_w, router_b, moe_w_gate, moe_w_up, moe_w_down, shared_w_gate, shared_w_up, shared_w_down, ln2_g, ln2_b):
    batch, seq, d = x.shape
    t = batch * seq
    pos = positions.reshape(t, 1).astype(F32)
    tabs_m = _rope_tables(pos, _rope_rows(MLA_ROPE, MLA_NOPE, MLA_NOPE))
    tabs_s = _rope_tables(pos, _rope_rows(SWA_D, 0, 0))
    h = x.reshape(t, d)
    for layer in range(DEPTH):
        j = layer // 2
        if layer % 2 == 0:
            a1, a2 = _even_mixer(h, tabs_m, ev_w_in[j], mla_q_norm[j], mla_w_qb[j], mla_kv_norm[j], mla_w_kvb[j],
                                 gdn_conv[j], gdn_a_log[j], gdn_dt_bias[j], gdn_norm[j], batch, seq)
            w_out = ev_w_out[j]
        else:
            a1, a2 = _odd_mixer(h, tabs_s, od_w_in[j], mlstm_b_i[j], mlstm_b_f[j], mlstm_norm[j], swa_sinks[j], batch, seq)
            w_out = od_w_out[j]
        h = _outproj_ln(h, a1, a2, w_out.astype(BF16), ln1_g[layer].reshape(1, -1), ln1_b[layer].reshape(1, -1))
        h = _moe(h, router_w[layer], router_b[layer], moe_w_gate[layer], moe_w_up[layer], moe_w_down[layer],
                 shared_w_gate[layer], shared_w_up[layer], shared_w_down[layer], ln2_g[layer], ln2_b[layer])
    return h.reshape(batch, seq, d)
```

```python
import functools
import math

import numpy as np
import jax
import jax.numpy as jnp
from jax import lax
from jax.experimental import pallas as pl
from jax.experimental.pallas import tpu as pltpu

F32 = jnp.float32
BF16 = jnp.bfloat16
HI = lax.Precision.HIGHEST

D_MODEL = 1024
DEPTH = 4
ROPE_THETA = 10000.0
EPS = 1e-6
LN_EPS = 1e-5
MLA_H, MLA_NOPE, MLA_ROPE, MLA_V = 8, 64, 32, 64
Q_LORA, KV_LORA = 256, 128
GDN_H, GDN_DK, GDN_DV, CONV_W, GDN_CHUNK = 4, 128, 128, 4, 64
ML_H, ML_DK, ML_DV, ML_CHUNK = 4, 64, 128, 64
SWA_H, SWA_KV, SWA_D, WINDOW = 8, 2, 64, 128
N_EXPERTS, N_GROUPS, TOPK_GROUPS, TOP_K = 64, 8, 4, 8
D_EXPERT, D_SHARED = 256, 256
ROUTED_SCALE = 2.5
DN_ALPHA = (2 * DEPTH) ** 0.25

LANES = 128
V7X_VMEM_BYTES = 64 * 1024 * 1024
VMEM_LIMIT = 48 * 1024 * 1024

EXPERT_BLOCK = 256
SEQS_PER_STEP = 2


def _cparams(sem, vmem=VMEM_LIMIT):
    return pltpu.CompilerParams(dimension_semantics=sem, vmem_limit_bytes=vmem)


def _dot(a, b, precision=None):
    return jnp.dot(a, b, preferred_element_type=F32, precision=precision)


def _dot_nt(a, b, precision=None):
    return lax.dot_general(a, b, (((1,), (1,)), ((), ())), preferred_element_type=F32, precision=precision)


def _dot_tn(a, b, precision=None):
    return lax.dot_general(a, b, (((0,), (0,)), ((), ())), preferred_element_type=F32, precision=precision)


def _split2(a):
    hi = a.astype(BF16)
    lo = (a - hi.astype(F32)).astype(BF16)
    return hi, lo


def _split3(a):
    p1 = a.astype(BF16)
    r = a - p1.astype(F32)
    p2 = r.astype(BF16)
    p3 = (r - p2.astype(F32)).astype(BF16)
    return p1, p2, p3


def _dot3(a, b, dot=_dot):
    ah, al = _split2(a)
    bh, bl = _split2(b)
    return dot(ah, bh) + (dot(ah, bl) + dot(al, bh))


def _dot_sel(sel, b, dot=_dot):
    sel = sel.astype(BF16)
    p1, p2, p3 = _split3(b)
    return dot(sel, p1) + (dot(sel, p2) + dot(sel, p3))


def _sigmoid(x):
    return 1.0 / (1.0 + jnp.exp(-x))


def _softplus(x):
    return jnp.maximum(x, 0.0) + jnp.log(1.0 + jnp.exp(-jnp.abs(x)))


def _silu(x):
    return x * _sigmoid(x)


def _lane_bcast(x, c):
    return jnp.broadcast_to(x[:, c:c + 1], x.shape)


def _iota2(shape, dim):
    return lax.broadcasted_iota(jnp.int32, shape, dim)


def _rope_kernel(pos_ref, rows_ref, c_ref, s_ref):
    ang = pos_ref[...] * rows_ref[0:1, :]
    c_ref[...] = rows_ref[1:2, :] * jnp.cos(ang) + rows_ref[2:3, :]
    s_ref[...] = rows_ref[3:4, :] * jnp.sin(ang)


def _rope_tables(pos, rows, tm=512):
    t = pos.shape[0]
    return pl.pallas_call(
        _rope_kernel,
        grid=(t // tm,),
        in_specs=[pl.BlockSpec((tm, 1), lambda i: (i, 0)), pl.BlockSpec((8, LANES), lambda i: (0, 0))],
        out_specs=[pl.BlockSpec((tm, LANES), lambda i: (i, 0))] * 2,
        out_shape=[jax.ShapeDtypeStruct((t, LANES), F32)] * 2,
        compiler_params=_cparams(("arbitrary",)),
        name="rope_tables",
    )(pos, rows)


def _rope_rows(dim, first_lane, pad_one_lanes):
    half = dim // 2
    inv = ROPE_THETA ** (-(np.arange(0, dim, 2, dtype=np.float32) / dim))
    rows = np.zeros((8, LANES), np.float32)
    lo = slice(first_lane, first_lane + half)
    hi = slice(first_lane + half, first_lane + dim)
    rows[0, lo] = inv
    rows[0, hi] = inv
    rows[1, lo] = 1.0
    rows[1, hi] = 1.0
    rows[2, :pad_one_lanes] = 1.0
    rows[3, lo] = -1.0
    rows[3, hi] = 1.0
    return jnp.asarray(rows)


def _proj_kernel(x_ref, w_ref, *out_refs, offsets):
    xb = x_ref[...].astype(BF16)
    for o_ref, (a, b) in zip(out_refs, offsets):
        o_ref[...] = _dot(xb, w_ref[:, a:b]).astype(o_ref.dtype)


def _proj(x, w, widths, dtypes, tm=256):
    t, k = x.shape
    offs = np.concatenate([[0], np.cumsum(widths)]).tolist()
    offsets = tuple((offs[i], offs[i + 1]) for i in range(len(widths)))
    return pl.pallas_call(
        functools.partial(_proj_kernel, offsets=offsets),
        grid=(t // tm,),
        in_specs=[pl.BlockSpec((tm, k), lambda i: (i, 0)), pl.BlockSpec(w.shape, lambda i: (0, 0))],
        out_specs=[pl.BlockSpec((tm, n), lambda i: (i, 0)) for n in widths],
        out_shape=[jax.ShapeDtypeStruct((t, n), dt) for n, dt in zip(widths, dtypes)],
        compiler_params=_cparams(("arbitrary",)),
        name="in_proj",
    )(x, w)


OD_SEG = dict(mq=(0, 512), mk=(512, 1024), mv=(1024, 1536), mo=(1536, 2048), gates=(2048, 2176),
              sq=(2176, 3200), sqsw=(3200, 4224), sk=(4224, 4480), sksw=(4480, 4736), sv=(4736, 5248))
OD_COLS = 5248


def _proj_odd_kernel(x_ref, w_ref, c_ref, s_ref, mq_ref, mk_ref, mv_ref, mo_ref, mg_ref, sq_ref, sk_ref, sv_ref):
    xb = x_ref[...].astype(BF16)

    def seg(name):
        a, b = OD_SEG[name]
        return _dot(xb, w_ref[:, a:b])

    mq_ref[...] = seg("mq")
    mk_ref[...] = seg("mk")
    mv_ref[...] = seg("mv")
    mo_ref[...] = seg("mo")
    mg_ref[...] = seg("gates")
    c = c_ref[...]
    s = s_ref[...]
    c8 = jnp.concatenate([c] * SWA_H, axis=1)
    s8 = jnp.concatenate([s] * SWA_H, axis=1)
    sq_ref[...] = (seg("sq") * c8 + seg("sqsw") * s8).astype(sq_ref.dtype)
    c2 = jnp.concatenate([c] * SWA_KV, axis=1)
    s2 = jnp.concatenate([s] * SWA_KV, axis=1)
    sk_ref[...] = (seg("sk") * c2 + seg("sksw") * s2).astype(sk_ref.dtype)
    sv_ref[...] = seg("sv").astype(sv_ref.dtype)


def _proj_odd(x, w, ctab, stab, tm=256):
    t, k = x.shape
    widths = (512, 512, 512, 512, 128, SWA_H * LANES, SWA_KV * LANES, 2 * SWA_KV * LANES)
    dtypes = (F32, F32, F32, F32, F32, BF16, BF16, BF16)
    return pl.pallas_call(
        _proj_odd_kernel,
        grid=(t // tm,),
        in_specs=[pl.BlockSpec((tm, k), lambda i: (i, 0)), pl.BlockSpec(w.shape, lambda i: (0, 0)),
                  pl.BlockSpec((tm, LANES), lambda i: (i, 0)), pl.BlockSpec((tm, LANES), lambda i: (i, 0))],
        out_specs=[pl.BlockSpec((tm, n), lambda i: (i, 0)) for n in widths],
        out_shape=[jax.ShapeDtypeStruct((t, n), dt) for n, dt in zip(widths, dtypes)],
        compiler_params=_cparams(("arbitrary",)),
        name="in_proj_odd",
    )(x, w, ctab, stab)


def _rms(x, g):
    return x * lax.rsqrt(jnp.mean(x * x, axis=-1, keepdims=True) + EPS) * g


def _mla_prep_kernel(in_ref, c_ref, s_ref, qn_ref, kvn_ref, wq_ref, wkv_ref, q_ref, k_ref, v_ref):
    hw = MLA_H * LANES
    c = c_ref[...]
    s = s_ref[...]
    c8 = jnp.concatenate([c] * MLA_H, axis=1)
    s8 = jnp.concatenate([s] * MLA_H, axis=1)
    cqn = _rms(in_ref[:, 0:Q_LORA], qn_ref[...]).astype(BF16)
    qq = _dot(cqn, wq_ref[...])
    scale = (MLA_NOPE + MLA_ROPE) ** -0.5
    q_ref[...] = ((qq[:, :hw] * c8 + qq[:, hw:] * s8) * scale).astype(q_ref.dtype)
    ckvn = _rms(in_ref[:, Q_LORA:Q_LORA + KV_LORA], kvn_ref[...]).astype(BF16)
    kv = _dot(ckvn, wkv_ref[...])
    o = Q_LORA + KV_LORA
    krr = in_ref[:, o:o + LANES] * c + in_ref[:, o + LANES:o + 2 * LANES] * s
    k_ref[...] = (kv[:, :hw] + jnp.concatenate([krr] * MLA_H, axis=1)).astype(k_ref.dtype)
    v_ref[...] = kv[:, hw:].astype(v_ref.dtype)


def _mla_prep(mla_in, ctab, stab, qn, kvn, wq2, wkv2, tm=256):
    t = mla_in.shape[0]
    hw = MLA_H * LANES
    row = lambda i: (i, 0)
    fix = lambda i: (0, 0)
    return pl.pallas_call(
        _mla_prep_kernel,
        grid=(t // tm,),
        in_specs=[pl.BlockSpec((tm, mla_in.shape[1]), row), pl.BlockSpec((tm, LANES), row), pl.BlockSpec((tm, LANES), row),
                  pl.BlockSpec(qn.shape, fix), pl.BlockSpec(kvn.shape, fix),
                  pl.BlockSpec(wq2.shape, fix), pl.BlockSpec(wkv2.shape, fix)],
        out_specs=[pl.BlockSpec((tm, hw), row)] * 3,
        out_shape=[jax.ShapeDtypeStruct((t, hw), BF16)] * 3,
        compiler_params=_cparams(("arbitrary",)),
        name="mla_prep",
    )(mla_in, ctab, stab, qn, kvn, wq2, wkv2)


def _mla_attn_kernel(q_ref, k_ref, v_ref, o_ref, *, tq):
    i = pl.program_id(2)
    neg = -1e30

    def chunk(j, carry, masked):
        start = pl.multiple_of(j * tq, tq)
        out = []
        for hh in range(2):
            m, l, acc = carry[hh]
            q = q_ref[:, hh * LANES:(hh + 1) * LANES]
            kc = k_ref[pl.ds(start, tq), hh * LANES:(hh + 1) * LANES]
            vc = v_ref[pl.ds(start, tq), hh * LANES:(hh + 1) * LANES]
            s = _dot_nt(q, kc)
            if masked:
                s = jnp.where(_iota2(s.shape, 0) >= _iota2(s.shape, 1), s, neg)
            m_new = jnp.maximum(m, jnp.max(s, axis=-1, keepdims=True))
            alpha = jnp.exp(m - m_new)
            p = jnp.exp(s - m_new)
            l = alpha * l + jnp.sum(p, axis=-1, keepdims=True)
            acc = alpha * acc + _dot(p.astype(BF16), vc)
            out.append((m_new, l, acc))
        return tuple(out)

    one = (jnp.full((tq, 1), neg, F32), jnp.zeros((tq, 1), F32), jnp.zeros((tq, LANES), F32))
    carry = lax.fori_loop(0, i, lambda j, c: chunk(j, c, False), (one, one))
    (_, l0, acc0), (_, l1, acc1) = chunk(i, carry, True)
    o_ref[...] = (acc0 / l0 + acc1 / l1).astype(o_ref.dtype)


def _mla_attn(q, k, v, batch, seq, tq=512):
    tq = min(tq, seq)
    nq = seq // tq
    pairs = MLA_H // 2
    return pl.pallas_call(
        functools.partial(_mla_attn_kernel, tq=tq),
        grid=(batch, pairs, nq),
        in_specs=[pl.BlockSpec((tq, 2 * LANES), lambda b, p, i: (b * nq + i, p)),
                  pl.BlockSpec((seq, 2 * LANES), lambda b, p, i: (b, p)),
                  pl.BlockSpec((seq, 2 * LANES), lambda b, p, i: (b, p))],
        out_specs=pl.BlockSpec((tq, LANES), lambda b, p, i: (b * nq + i, p)),
        out_shape=jax.ShapeDtypeStruct((batch * seq, pairs * LANES), BF16),
        compiler_params=_cparams(("arbitrary", "arbitrary", "arbitrary")),
        name="mla_attn",
    )(q, k, v)


def _unit_lower_inverse_many(ns):
    c = ns[0].shape[0]
    eye = (_iota2((c, c), 0) == _iota2((c, c), 1)).astype(F32)
    xs = [-n for n in ns]
    ps = [eye + x for x in xs]
    xsplit = [_split2(x) for x in xs]
    for _ in range(int(math.log2(c)) - 1):
        xs = [_dot(xh, xh) + (_dot(xh, xl) + _dot(xl, xh)) for xh, xl in xsplit]
        xsplit = [_split2(x) for x in xs]
        psplit = [_split2(p) for p in ps]
        ps = [p + (_dot(ph, xh) + (_dot(ph, xl) + _dot(plo, xh)))
              for p, (ph, plo), (xh, xl) in zip(ps, psplit, xsplit)]
    return ps


def _gdn_kernel(qkv_ref, g_ref, z_ref, cw_ref, al_ref, dt_ref, on_ref, o_ref, ext_ref, st_ref):
    c = GDN_CHUNK
    hd = GDN_DK
    nqk = GDN_H * GDN_DK

    @pl.when(pl.program_id(1) == 0)
    def _():
        ext_ref[:, 0:8, :] = jnp.zeros((ext_ref.shape[0], 8, ext_ref.shape[2]), F32)
        st_ref[...] = jnp.zeros(st_ref.shape, F32)

    tri = (_iota2((c, c), 0) >= _iota2((c, c), 1)).astype(F32)
    row_ge = _iota2((c, c), 0) >= _iota2((c, c), 1)
    row_gt = _iota2((c, c), 0) > _iota2((c, c), 1)
    ones = jnp.ones((c, LANES), F32)
    lane = _iota2((c, LANES), 1)

    units = []
    for bb in range(qkv_ref.shape[0]):
        ext = ext_ref.at[bb]
        ext[8:8 + c, :] = qkv_ref[bb]
        conv = cw_ref[0:1, :] * ext[5:5 + c, :]
        for j in range(1, CONV_W):
            conv = conv + cw_ref[j:j + 1, :] * ext[5 + j:5 + j + c, :]
        ext[0:8, :] = ext[c:c + 8, :]
        act = _silu(conv)
        gates = g_ref[bb]
        beta_all = _sigmoid(gates)
        g_all = -jnp.exp(al_ref[...]) * _softplus(gates + dt_ref[...])
        gc_all = _dot_sel(tri, g_all)
        for h in range(GDN_H):
            q = act[:, h * hd:(h + 1) * hd]
            k = act[:, nqk + h * hd:nqk + (h + 1) * hd]
            v = act[:, 2 * nqk + h * GDN_DV:2 * nqk + (h + 1) * GDN_DV]
            q = q * lax.rsqrt(jnp.sum(q * q, axis=-1, keepdims=True) + EPS) * (GDN_DK ** -0.5)
            k = k * lax.rsqrt(jnp.sum(k * k, axis=-1, keepdims=True) + EPS)
            beta = _lane_bcast(beta_all, h)
            gcol = _lane_bcast(gc_all, GDN_H + h)
            grow = _dot_sel(ones, jnp.where(lane == GDN_H + h, gc_all, 0.0), _dot_nt)
            decay = jnp.exp(jnp.where(row_ge, gcol[:, :c] - grow, -jnp.inf))
            kb = k * beta
            lower = jnp.where(row_gt, _dot3(kb, k, _dot_nt) * decay, 0.0)
            eg = jnp.exp(gcol)
            glast = gcol[c - 1:c, :]
            units.append(dict(bb=bb, h=h, lower=lower, rhs=jnp.concatenate([v * beta, kb * eg], axis=1),
                              attn=_dot_nt(q.astype(BF16), k.astype(BF16)) * decay, qg=(q * eg).astype(BF16),
                              kg=(k * jnp.exp(glast - gcol)).astype(BF16), gl=jnp.exp(glast)))

    tinvs = _unit_lower_inverse_many([u["lower"] for u in units])
    uws = []
    for u, tinv in zip(units, tinvs):
        th, tl = _split2(tinv)
        rh, rl = _split2(u["rhs"])
        uws.append(_dot(th, rh) + (_dot(th, rl) + _dot(tl, rh)))
    states = [st_ref[u["bb"], u["h"]] for u in units]
    sbs = [s.astype(BF16) for s in states]
    vnews = [(uw[:, :GDN_DV] - _dot(uw[:, GDN_DV:].astype(BF16), sb)).astype(BF16) for uw, sb in zip(uws, sbs)]
    for u, state, sb, vnb in zip(units, states, sbs, vnews):
        bb, h = u["bb"], u["h"]
        o = _dot(u["qg"], sb) + _dot(u["attn"].astype(BF16), vnb)
        st_ref[bb, h] = state * u["gl"] + _dot_tn(u["kg"], vnb)
        o = _rms(o, on_ref[...]) * _silu(z_ref[bb, :, h * GDN_DV:(h + 1) * GDN_DV])
        o_ref[bb, :, h * GDN_DV:(h + 1) * GDN_DV] = o.astype(o_ref.dtype)


def _gdn(qkv, gates, z, conv_w, a_row, dt_row, o_norm, batch, seq):
    c = GDN_CHUNK
    nc = seq // c
    w3 = qkv.shape[1]
    wo = GDN_H * GDN_DV
    nb = SEQS_PER_STEP
    row = lambda b, i: (b, i, 0)
    fix = lambda b, i: (0, 0)
    out = pl.pallas_call(
        _gdn_kernel,
        grid=(batch // nb, nc),
        in_specs=[pl.BlockSpec((nb, c, w3), row), pl.BlockSpec((nb, c, LANES), row), pl.BlockSpec((nb, c, wo), row),
                  pl.BlockSpec(conv_w.shape, fix), pl.BlockSpec((1, LANES), fix), pl.BlockSpec((1, LANES), fix),
                  pl.BlockSpec((1, GDN_DV), fix)],
        out_specs=pl.BlockSpec((nb, c, wo), row),
        out_shape=jax.ShapeDtypeStruct((batch, seq, wo), BF16),
        scratch_shapes=[pltpu.VMEM((nb, c + 8, w3), F32), pltpu.VMEM((nb, GDN_H, GDN_DK, GDN_DV), F32)],
        compiler_params=_cparams(("arbitrary", "arbitrary")),
        name="gdn",
    )(qkv.reshape(batch, seq, w3), gates.reshape(batch, seq, LANES), z.reshape(batch, seq, wo), conv_w, a_row, dt_row, o_norm)
    return out.reshape(batch * seq, wo)


def _mlstm_kernel(q_ref, k_ref, v_ref, og_ref, g_ref, bias_ref, nrm_ref, o_ref, c_ref, n_ref, m_ref):
    @pl.when(pl.program_id(1) == 0)
    def _():
        c_ref[...] = jnp.zeros(c_ref.shape, F32)
        n_ref[...] = jnp.zeros(n_ref.shape, F32)
        m_ref[...] = jnp.zeros(m_ref.shape, F32)

    c = ML_CHUNK
    tri = (_iota2((c, c), 0) >= _iota2((c, c), 1)).astype(F32)
    row_ge = _iota2((c, c), 0) >= _iota2((c, c), 1)
    ones = jnp.ones((c, LANES), F32)
    lane = _iota2((c, LANES), 1)

    units = []
    for bb in range(q_ref.shape[0]):
        pre = g_ref[bb] + bias_ref[...]
        logf = jnp.minimum(pre, 0.0) - jnp.log(1.0 + jnp.exp(-jnp.abs(pre)))
        bcum_all = _dot_sel(tri, logf)
        for h in range(ML_H):
            q = q_ref[bb, :, h * LANES:(h + 1) * LANES]
            k = k_ref[bb, :, h * LANES:(h + 1) * LANES] * (ML_DK ** -0.5)
            units.append(dict(bb=bb, h=h, q=q, k=k, qb=q.astype(BF16), vb=v_ref[bb, :, h * ML_DV:(h + 1) * ML_DV].astype(BF16),
                              bcol=_lane_bcast(bcum_all, ML_H + h),
                              icol=_lane_bcast(pre, h),
                              col=jnp.where(lane == h, pre, 0.0) - jnp.where(lane == ML_H + h, bcum_all, 0.0),
                              m_st=m_ref[bb, h], cst=c_ref[bb, h], nst=n_ref[bb, h]))
    for u in units:
        u["row"] = _dot_sel(ones, u["col"], _dot_nt)
        u["qk"] = _dot_nt(u["qb"], u["k"].astype(BF16))
        u["qc"] = _dot(u["qb"], u["cst"].astype(BF16))
    for u in units:
        d = jnp.where(row_ge, u["bcol"][:, :c] + u["row"], -jnp.inf)
        inter = u["bcol"] + u["m_st"]
        m_t = jnp.maximum(inter, jnp.max(d, axis=-1, keepdims=True))
        u["m_t"] = m_t
        u["w_inter"] = jnp.exp(inter - m_t)
        u["p"] = jnp.exp(d - m_t[:, :c]) * u["qk"]
        u["pv"] = _dot(u["p"].astype(BF16), u["vb"])
        b_end = u["bcol"][c - 1:c, :]
        a = b_end - u["bcol"] + u["icol"]
        m_new = jnp.maximum(b_end + u["m_st"], jnp.max(a, axis=0, keepdims=True))
        u["m_new"] = m_new
        u["keep"] = jnp.exp(b_end + u["m_st"] - m_new)
        u["ks"] = u["k"] * jnp.exp(a - m_new)
        u["kv"] = _dot_tn(u["ks"].astype(BF16), u["vb"])
    for u in units:
        bb, h = u["bb"], u["h"]
        num = u["w_inter"] * u["qc"] + u["pv"]
        den = (u["w_inter"] * jnp.sum(u["q"] * u["nst"], axis=-1, keepdims=True)
               + jnp.sum(u["p"], axis=-1, keepdims=True))
        hc = num / jnp.maximum(jnp.abs(den), jnp.exp(-u["m_t"]))
        c_ref[bb, h] = u["cst"] * u["keep"] + u["kv"]
        n_ref[bb, h] = u["nst"] * u["keep"] + jnp.sum(u["ks"], axis=0, keepdims=True)
        m_ref[bb, h] = u["m_new"]
        hn = (_rms(hc, nrm_ref[:, h * ML_DV:(h + 1) * ML_DV])
              * _sigmoid(og_ref[bb, :, h * ML_DV:(h + 1) * ML_DV]))
        o_ref[bb, :, h * ML_DV:(h + 1) * ML_DV] = hn.astype(o_ref.dtype)


def _mlstm(mq, mk, mv, mo, gates, bias_row, norm_row, batch, seq):
    c = ML_CHUNK
    nc = seq // c
    nb = SEQS_PER_STEP
    row = lambda b, i: (b, i, 0)
    fix = lambda b, i: (0, 0)
    wide = ML_H * LANES
    r3 = lambda a: a.reshape(batch, seq, a.shape[-1])
    out = pl.pallas_call(
        _mlstm_kernel,
        grid=(batch // nb, nc),
        in_specs=[pl.BlockSpec((nb, c, wide), row), pl.BlockSpec((nb, c, wide), row), pl.BlockSpec((nb, c, wide), row),
                  pl.BlockSpec((nb, c, wide), row), pl.BlockSpec((nb, c, LANES), row),
                  pl.BlockSpec((1, LANES), fix), pl.BlockSpec((1, wide), fix)],
        out_specs=pl.BlockSpec((nb, c, wide), row),
        out_shape=jax.ShapeDtypeStruct((batch, seq, wide), BF16),
        scratch_shapes=[pltpu.VMEM((nb, ML_H, LANES, ML_DV), F32), pltpu.VMEM((nb, ML_H, 1, LANES), F32),
                        pltpu.VMEM((nb, ML_H, 1, LANES), F32)],
        compiler_params=_cparams(("arbitrary", "arbitrary")),
        name="mlstm",
    )(r3(mq), r3(mk), r3(mv), r3(mo), r3(gates), bias_row, norm_row)
    return out.reshape(batch * seq, wide)


def _swa_kernel(q_ref, kc_ref, kp_ref, vc_ref, vp_ref, sink_ref, o_ref):
    w = WINDOW
    n = pl.program_id(1)
    scale = SWA_D ** -0.5
    qi = _iota2((w, w), 0)
    kj = _iota2((w, w), 1)
    mask_c = kj <= qi
    mask_p = jnp.logical_and(kj > qi, n > 0)
    grp = SWA_H // SWA_KV
    neg = -1e30
    scores = []
    for h in range(SWA_H):
        g = h // grp
        q = q_ref[:, h * LANES:(h + 1) * LANES]
        scores.append((_dot_nt(q, kc_ref[:, g * LANES:(g + 1) * LANES]), _dot_nt(q, kp_ref[:, g * LANES:(g + 1) * LANES])))
    probs = []
    for h, (sc, sp) in enumerate(scores):
        s_c = jnp.where(mask_c, sc * scale, neg)
        s_p = jnp.where(mask_p, sp * scale, neg)
        sink = sink_ref[:, h:h + 1]
        m = jnp.maximum(jnp.max(jnp.maximum(s_c, s_p), axis=-1, keepdims=True), sink)
        p_c = jnp.where(mask_c, jnp.exp(s_c - m), 0.0)
        p_p = jnp.where(mask_p, jnp.exp(s_p - m), 0.0)
        den = jnp.sum(p_c + p_p, axis=-1, keepdims=True) + jnp.exp(sink - m)
        inv = 1.0 / den
        probs.append(((p_c * inv).astype(BF16), (p_p * inv).astype(BF16)))
    for pair in range(SWA_H // 2):
        acc = None
        for sub in range(2):
            h = 2 * pair + sub
            vcol = (2 * (h // grp) + sub) * LANES
            p_c, p_p = probs[h]
            part = _dot(p_c, vc_ref[:, vcol:vcol + LANES]) + _dot(p_p, vp_ref[:, vcol:vcol + LANES])
            acc = part if acc is None else acc + part
        o_ref[:, pair * LANES:(pair + 1) * LANES] = acc.astype(o_ref.dtype)


def _swa(sq, sk, sv, sinks_row, batch, seq):
    w = WINDOW
    nb = seq // w
    cur = lambda b, n: (b * nb + n, 0)
    prev = lambda b, n: (b * nb + jnp.maximum(n - 1, 0), 0)
    return pl.pallas_call(
        _swa_kernel,
        grid=(batch, nb),
        in_specs=[pl.BlockSpec((w, sq.shape[1]), cur),
                  pl.BlockSpec((w, sk.shape[1]), cur), pl.BlockSpec((w, sk.shape[1]), prev),
                  pl.BlockSpec((w, sv.shape[1]), cur), pl.BlockSpec((w, sv.shape[1]), prev),
                  pl.BlockSpec((1, LANES), lambda b, n: (0, 0))],
        out_specs=pl.BlockSpec((w, SWA_H * SWA_D), cur),
        out_shape=jax.ShapeDtypeStruct((batch * seq, SWA_H * SWA_D), BF16),
        compiler_params=_cparams(("arbitrary", "arbitrary")),
        name="swa",
    )(sq, sk, sk, sv, sv, sinks_row)


def _layer_norm(h, g, b):
    mu = jnp.mean(h, axis=-1, keepdims=True)
    d = h - mu
    var = jnp.mean(d * d, axis=-1, keepdims=True)
    return d * lax.rsqrt(var + LN_EPS) * g + b


def _outproj_kernel(x_ref, a1_ref, a2_ref, w_ref, g_ref, b_ref, o_ref):
    k1 = a1_ref.shape[1]
    y = _dot(a1_ref[...].astype(BF16), w_ref[0:k1, :]) + _dot(a2_ref[...].astype(BF16), w_ref[k1:, :])
    o_ref[...] = _layer_norm(DN_ALPHA * x_ref[...] + y, g_ref[...], b_ref[...])


def _outproj_ln(x, a1, a2, w, g, b, tm=256):
    t, d = x.shape
    row = lambda i: (i, 0)
    fix = lambda i: (0, 0)
    return pl.pallas_call(
        _outproj_kernel,
        grid=(t // tm,),
        in_specs=[pl.BlockSpec((tm, d), row), pl.BlockSpec((tm, a1.shape[1]), row), pl.BlockSpec((tm, a2.shape[1]), row),
                  pl.BlockSpec(w.shape, fix), pl.BlockSpec((1, d), fix), pl.BlockSpec((1, d), fix)],
        out_specs=pl.BlockSpec((tm, d), row),
        out_shape=jax.ShapeDtypeStruct((t, d), F32),
        compiler_params=_cparams(("arbitrary",)),
        name="outproj_ln",
    )(x, a1, a2, w, g, b)


def _first_index(x, m, iota_f, sentinel):
    return jnp.min(jnp.where(x == m, iota_f, sentinel), axis=0, keepdims=True)


def _router_kernel(x_ref, wt_ref, bias_ref, idx_ref, gate_ref, rank_ref, cnt_ref, carry_ref):
    tm = x_ref.shape[0]
    e = N_EXPERTS
    gs = e // N_GROUPS
    ninf = -jnp.inf

    @pl.when(pl.program_id(0) == 0)
    def _():
        carry_ref[...] = jnp.zeros(carry_ref.shape, F32)

    logits = _dot_nt(wt_ref[...], x_ref[...], HI)
    scores = _sigmoid(logits)
    sel = scores + bias_ref[:, 0:1]

    sub_f = _iota2((gs, tm), 0).astype(F32)
    gscore = []
    for g in range(N_GROUPS):
        blk = sel[g * gs:(g + 1) * gs, :]
        m1 = jnp.max(blk, axis=0, keepdims=True)
        i1 = _first_index(blk, m1, sub_f, float(gs))
        m2 = jnp.max(jnp.where(sub_f == i1, ninf, blk), axis=0, keepdims=True)
        gscore.append(m1 + m2)
    gsc = jnp.concatenate(gscore, axis=0)
    grp_f = _iota2((N_GROUPS, tm), 0).astype(F32)
    gmask = jnp.zeros((N_GROUPS, tm), F32)
    for _ in range(TOPK_GROUPS):
        m = jnp.max(gsc, axis=0, keepdims=True)
        gi = _first_index(gsc, m, grp_f, float(N_GROUPS))
        hit = grp_f == gi
        gmask = jnp.where(hit, 1.0, gmask)
        gsc = jnp.where(hit, ninf, gsc)
    masked = jnp.concatenate(
        [jnp.where(gmask[g:g + 1, :] > 0.0, sel[g * gs:(g + 1) * gs, :], ninf) for g in range(N_GROUPS)], axis=0)

    exp_f = _iota2((e, tm), 0).astype(F32)
    chosen = jnp.zeros((e, tm), F32)
    idxs, gates = [], []
    for _ in range(TOP_K):
        m = jnp.max(masked, axis=0, keepdims=True)
        ei = _first_index(masked, m, exp_f, float(e))
        hit = exp_f == ei
        idxs.append(ei)
        gates.append(jnp.sum(jnp.where(hit, scores, 0.0), axis=0, keepdims=True))
        chosen = jnp.where(hit, 1.0, chosen)
        masked = jnp.where(hit, ninf, masked)
    gate = jnp.concatenate(gates, axis=0)
    gate = gate / jnp.sum(gate, axis=0, keepdims=True) * ROUTED_SCALE
    idx_f = jnp.concatenate(idxs, axis=0)

    upper = (_iota2((tm, tm), 0) < _iota2((tm, tm), 1)).astype(BF16)
    before = _dot(chosen.astype(BF16), upper) + carry_ref[...][:, 0:1]
    ranks = [jnp.sum(jnp.where(exp_f == idxs[k], before, 0.0), axis=0, keepdims=True) for k in range(TOP_K)]
    carry_ref[...] = carry_ref[...] + jnp.sum(chosen, axis=1, keepdims=True)

    idx_ref[...] = idx_f.astype(jnp.int32)
    gate_ref[...] = gate
    rank_ref[...] = jnp.concatenate(ranks, axis=0).astype(jnp.int32)
    cnt_ref[...] = carry_ref[...]


def _router(x, wt, bias_col, tm=512):
    t, d = x.shape
    col = lambda i: (0, i)
    fix = lambda i: (0, 0)
    return pl.pallas_call(
        _router_kernel,
        grid=(t // tm,),
        in_specs=[pl.BlockSpec((tm, d), lambda i: (i, 0)), pl.BlockSpec(wt.shape, fix), pl.BlockSpec((N_EXPERTS, LANES), fix)],
        out_specs=[pl.BlockSpec((TOP_K, tm), col), pl.BlockSpec((TOP_K, tm), col), pl.BlockSpec((TOP_K, tm), col),
                   pl.BlockSpec((N_EXPERTS, LANES), fix)],
        out_shape=[jax.ShapeDtypeStruct((TOP_K, t), jnp.int32), jax.ShapeDtypeStruct((TOP_K, t), F32),
                   jax.ShapeDtypeStruct((TOP_K, t), jnp.int32), jax.ShapeDtypeStruct((N_EXPERTS, LANES), F32)],
        scratch_shapes=[pltpu.VMEM((N_EXPERTS, LANES), F32)],
        compiler_params=_cparams(("arbitrary",)),
        name="router",
    )(x, wt, bias_col)


def _dest_kernel(idx_ref, rank_ref, start_ref, dest_ref):
    tm = idx_ref.shape[1]
    exp_i = _iota2((N_EXPERTS, tm), 0)
    start = start_ref[:, 0:1]
    rows = [jnp.sum(jnp.where(exp_i == idx_ref[s:s + 1, :], start, 0.0), axis=0, keepdims=True) for s in range(TOP_K)]
    dest_ref[...] = jnp.concatenate(rows, axis=0).astype(jnp.int32) + rank_ref[...]


def _dest_rows(idx, rank, start_col, tm=2048):
    t = idx.shape[1]
    tm = min(tm, t)
    col = lambda i: (0, i)
    return pl.pallas_call(
        _dest_kernel,
        grid=(t // tm,),
        in_specs=[pl.BlockSpec((TOP_K, tm), col), pl.BlockSpec((TOP_K, tm), col),
                  pl.BlockSpec((N_EXPERTS, LANES), lambda i: (0, 0))],
        out_specs=pl.BlockSpec((TOP_K, tm), col),
        out_shape=jax.ShapeDtypeStruct((TOP_K, t), jnp.int32),
        compiler_params=_cparams(("arbitrary",)),
        name="moe_dest",
    )(idx, rank, start_col)


def _pack_pairs(x):
    n = x.shape[1] // 2
    hi = lax.bitcast_convert_type(x[:, :n].astype(BF16).astype(F32), jnp.uint32)
    lo = lax.bitcast_convert_type(x[:, n:].astype(BF16).astype(F32), jnp.uint32)
    return hi | (lo >> 16)


def _unpack_pairs(w):
    hi = lax.bitcast_convert_type(w & jnp.uint32(0xFFFF0000), F32)
    lo = lax.bitcast_convert_type(w << 16, F32)
    return hi, lo


def _dispatch_kernel(dest_ref, fill_ref, x_ref, xs_ref, pk_ref, zero_ref, sem, zsem):
    tm = x_ref.shape[0]
    pk_ref[...] = _pack_pairs(x_ref[...])

    def row_copy(t, s):
        return pltpu.make_async_copy(pk_ref.at[pl.ds(t, 1), :], xs_ref.at[pl.ds(dest_ref[s, t], 1), :], sem)

    def issue(t, _):
        for s in range(TOP_K):
            row_copy(t, s).start(priority=s % 2)
        return 0

    lax.fori_loop(0, tm, issue, 0)

    @pl.when(pl.program_id(0) == 0)
    def _():
        zero_ref[...] = jnp.zeros(zero_ref.shape, zero_ref.dtype)

        def pad_copy(r):
            return pltpu.make_async_copy(zero_ref, xs_ref.at[pl.ds(r, 1), :], zsem)

        def per_expert(e, _):
            start = fill_ref[e, 0]
            cnt = fill_ref[e, 1]
            lax.fori_loop(0, cnt, lambda r, c: (pad_copy(start + r).start(), c)[1], 0)
            lax.fori_loop(0, cnt, lambda r, c: (pad_copy(start + r).wait(), c)[1], 0)
            return 0

        lax.fori_loop(0, N_EXPERTS, per_expert, 0)

    for s in range(TOP_K):
        pltpu.make_async_copy(pk_ref, xs_ref.at[pl.ds(0, tm), :], sem).wait()


def _dispatch(x, dest, fill, rows, tm=256):
    t, d = x.shape
    return pl.pallas_call(
        _dispatch_kernel,
        grid=(t // tm,),
        in_specs=[pl.BlockSpec((TOP_K, tm), lambda i: (0, i), memory_space=pltpu.SMEM),
                  pl.BlockSpec(memory_space=pltpu.SMEM),
                  pl.BlockSpec((tm, d), lambda i: (i, 0))],
        out_specs=pl.BlockSpec(memory_space=pl.ANY),
        out_shape=jax.ShapeDtypeStruct((rows, d // 2), jnp.uint32),
        scratch_shapes=[pltpu.VMEM((tm, d // 2), jnp.uint32), pltpu.VMEM((1, d // 2), jnp.uint32),
                        pltpu.SemaphoreType.DMA(()), pltpu.SemaphoreType.DMA(())],
        compiler_params=_cparams(("arbitrary",)),
        name="moe_dispatch",
    )(dest, fill, x)


def _experts_kernel(be_ref, nu_ref, xs_ref, wg_ref, wu_ref, wd_ref, ys_ref):
    i = pl.program_id(0)

    @pl.when(i < nu_ref[0])
    def _():
        half = xs_ref.shape[1]
        xa, xb = _unpack_pairs(xs_ref[...])
        xa = xa.astype(BF16)
        xb = xb.astype(BF16)
        gate = _dot(xa, wg_ref[0, :half, :]) + _dot(xb, wg_ref[0, half:, :])
        up = _dot(xa, wu_ref[0, :half, :]) + _dot(xb, wu_ref[0, half:, :])
        h = _silu(gate) * up
        ys_ref[...] = _pack_pairs(_dot(h.astype(BF16), wd_ref[0]))

    @pl.when(i >= nu_ref[0])
    def _():
        ys_ref[...] = jnp.zeros(ys_ref.shape, ys_ref.dtype)


def _experts(block_e, n_used, xs, wg, wu, wd):
    rows, half = xs.shape
    d = 2 * half
    nb = rows // EXPERT_BLOCK
    blk = lambda i, be, nu: (jnp.minimum(i, nu[0] - 1), 0)
    wsel = lambda i, be, nu: (be[i], 0, 0)
    return pl.pallas_call(
        _experts_kernel,
        grid_spec=pltpu.PrefetchScalarGridSpec(
            num_scalar_prefetch=2,
            grid=(nb,),
            in_specs=[pl.BlockSpec((EXPERT_BLOCK, half), blk),
                      pl.BlockSpec((1, d, D_EXPERT), wsel), pl.BlockSpec((1, d, D_EXPERT), wsel),
                      pl.BlockSpec((1, D_EXPERT, d), wsel)],
            out_specs=pl.BlockSpec((EXPERT_BLOCK, half), lambda i, be, nu: (i, 0)),
        ),
        out_shape=jax.ShapeDtypeStruct((rows, half), jnp.uint32),
        compiler_params=_cparams(("arbitrary",)),
        name="moe_experts",
    )(block_e, n_used, xs, wg, wu, wd)


def _combine_kernel(dest_ref, x_ref, gate_ref, ys_ref, sg_ref, su_ref, sd_ref, g_ref, b_ref, o_ref, buf_ref, sem):
    tm = x_ref.shape[0]

    def row_copy(t, s):
        return pltpu.make_async_copy(ys_ref.at[pl.ds(dest_ref[s, t], 1), :], buf_ref.at[s, pl.ds(t, 1), :], sem)

    def issue(t, _):
        for s in range(TOP_K):
            row_copy(t, s).start(priority=s % 2)
        return 0

    lax.fori_loop(0, tm, issue, 0)

    x = x_ref[...]
    xb = x.astype(BF16)
    hs = _silu(_dot(xb, sg_ref[...])) * _dot(xb, su_ref[...])
    ff = _dot(hs.astype(BF16), sd_ref[...])

    for s in range(TOP_K):
        pltpu.make_async_copy(ys_ref.at[pl.ds(0, tm), :], buf_ref.at[s], sem).wait()

    gate = gate_ref[...]
    half = buf_ref.shape[2]
    ya = ff[:, :half]
    yb = ff[:, half:]
    for s in range(TOP_K):
        a, b = _unpack_pairs(buf_ref[s])
        ya = ya + gate[:, s:s + 1] * a
        yb = yb + gate[:, s:s + 1] * b
    ff = jnp.concatenate([ya, yb], axis=1)
    o_ref[...] = _layer_norm(DN_ALPHA * x + ff, g_ref[...], b_ref[...])


def _combine(dest, x, gate_t, ys, sg, su, sd, g, b, tm=256):
    t, d = x.shape
    row = lambda i: (i, 0)
    fix = lambda i: (0, 0)
    return pl.pallas_call(
        _combine_kernel,
        grid=(t // tm,),
        in_specs=[pl.BlockSpec((TOP_K, tm), lambda i: (0, i), memory_space=pltpu.SMEM),
                  pl.BlockSpec((tm, d), row), pl.BlockSpec((tm, TOP_K), row),
                  pl.BlockSpec(memory_space=pl.ANY),
                  pl.BlockSpec(sg.shape, fix), pl.BlockSpec(su.shape, fix), pl.BlockSpec(sd.shape, fix),
                  pl.BlockSpec((1, d), fix), pl.BlockSpec((1, d), fix)],
        out_specs=pl.BlockSpec((tm, d), row),
        out_shape=jax.ShapeDtypeStruct((t, d), F32),
        scratch_shapes=[pltpu.VMEM((TOP_K, tm, d // 2), jnp.uint32), pltpu.SemaphoreType.DMA(())],
        compiler_params=_cparams(("arbitrary",)),
        name="moe_combine",
    )(dest, x, gate_t, ys, sg, su, sd, g, b)


def _take_cols(w, idx):
    wz = jnp.concatenate([w, jnp.zeros((w.shape[0], 1), w.dtype)], axis=1)
    idx = np.where(np.asarray(idx) < 0, w.shape[1], np.asarray(idx))
    return jnp.take(wz, jnp.asarray(idx, jnp.int32), axis=1)


def _pad_lane_row(v, first_lane, width=LANES):
    out = jnp.zeros((1, width), F32)
    return lax.dynamic_update_slice(out, v.reshape(1, -1).astype(F32), (0, first_lane))


def _even_in_cols():
    z = lambda n: -np.ones(n, int)
    kr0 = Q_LORA + KV_LORA
    half = MLA_ROPE // 2
    cols = [np.arange(0, Q_LORA), np.arange(Q_LORA, Q_LORA + KV_LORA),
            z(64), np.arange(kr0, kr0 + MLA_ROPE), z(32),
            z(64), np.arange(kr0 + half, kr0 + MLA_ROPE), np.arange(kr0, kr0 + half), z(32)]
    g0 = kr0 + MLA_ROPE
    nqk = GDN_H * GDN_DK
    cols.append(np.arange(g0, g0 + 3 * nqk))
    zoff = g0 + 3 * nqk + 2 * GDN_H
    cols.append(np.arange(zoff, zoff + GDN_H * GDN_DV))
    cols += [np.arange(g0 + 3 * nqk, g0 + 3 * nqk + 2 * GDN_H), z(LANES - 2 * GDN_H)]
    return np.concatenate(cols)


EV_WIDTHS = (Q_LORA + KV_LORA + 2 * LANES, 3 * GDN_H * GDN_DK, GDN_H * GDN_DV, LANES)


def _mla_q_cols():
    per = MLA_NOPE + MLA_ROPE
    half = MLA_ROPE // 2
    main, sw = [], []
    for h in range(MLA_H):
        b = h * per
        main += [np.arange(b, b + per), -np.ones(LANES - per, int)]
        sw += [-np.ones(MLA_NOPE, int), np.arange(b + MLA_NOPE + half, b + per), np.arange(b + MLA_NOPE, b + MLA_NOPE + half),
               -np.ones(LANES - per, int)]
    return np.concatenate(main + sw)


def _mla_kv_cols():
    per = MLA_NOPE + MLA_V
    kc, vc = [], []
    for h in range(MLA_H):
        b = h * per
        kc += [np.arange(b, b + MLA_NOPE), -np.ones(LANES - MLA_NOPE, int)]
        vv = np.arange(b + MLA_NOPE, b + per)
        pad = -np.ones(LANES - MLA_V, int)
        vc += [vv, pad] if h % 2 == 0 else [pad, vv]
    return np.concatenate(kc + vc)


def _odd_in_cols():
    z = lambda n: -np.ones(n, int)
    o = 0
    cols = []
    mq0, mk0 = 0, ML_H * ML_DK
    for base in (mq0, mk0):
        for h in range(ML_H):
            cols += [np.arange(base + h * ML_DK, base + (h + 1) * ML_DK), z(LANES - ML_DK)]
    mv0 = 2 * ML_H * ML_DK
    cols.append(np.arange(mv0, mv0 + ML_H * ML_DV))
    mi0 = mv0 + ML_H * ML_DV
    mo0 = mi0 + 2 * ML_H
    cols.append(np.arange(mo0, mo0 + ML_H * ML_DV))
    cols += [np.arange(mi0, mi0 + 2 * ML_H), z(LANES - 2 * ML_H)]
    sq0 = mo0 + ML_H * ML_DV
    sk0 = sq0 + SWA_H * SWA_D
    sv0 = sk0 + SWA_KV * SWA_D
    half = SWA_D // 2

    def heads(base, n, swapped):
        out = []
        for h in range(n):
            b = base + h * SWA_D
            if swapped:
                out += [np.arange(b + half, b + SWA_D), np.arange(b, b + half), z(LANES - SWA_D)]
            else:
                out += [np.arange(b, b + SWA_D), z(LANES - SWA_D)]
        return out

    cols += heads(sq0, SWA_H, False) + heads(sq0, SWA_H, True) + heads(sk0, SWA_KV, False) + heads(sk0, SWA_KV, True)
    for g in range(SWA_KV):
        vv = np.arange(sv0 + g * SWA_D, sv0 + (g + 1) * SWA_D)
        cols += [vv, z(LANES - SWA_D), z(LANES - SWA_D), vv]
    return np.concatenate(cols)


def _even_mixer(x, tabs, w_in, q_norm, w_qb, kv_norm, w_kvb, conv_w, a_log, dt_bias, o_norm, batch, seq):
    ctab, stab = tabs
    w = _take_cols(w_in, _even_in_cols()).astype(BF16)
    mla_in, qkv, z, gates = _proj(x, w, EV_WIDTHS, (F32, F32, F32, F32))
    wq2 = _take_cols(w_qb, _mla_q_cols()).astype(BF16)
    wkv2 = _take_cols(w_kvb, _mla_kv_cols()).astype(BF16)
    q, k, v = _mla_prep(mla_in, ctab, stab, q_norm.reshape(1, -1), kv_norm.reshape(1, -1), wq2, wkv2)
    o_a = _mla_attn(q, k, v, batch, seq)
    o_b = _gdn(qkv, gates, z, conv_w, _pad_lane_row(a_log, GDN_H), _pad_lane_row(dt_bias, GDN_H),
               o_norm.reshape(1, -1), batch, seq)
    return o_a, o_b


def _odd_mixer(x, tabs, w_in, b_i, b_f, ml_norm, sinks, batch, seq):
    ctab, stab = tabs
    w = _take_cols(w_in, _odd_in_cols()).astype(BF16)
    mq, mk, mv, mo, mg, sq, sk, sv = _proj_odd(x, w, ctab, stab)
    bias_row = _pad_lane_row(jnp.concatenate([b_i, b_f]), 0)
    o_c = _mlstm(mq, mk, mv, mo, mg, bias_row, ml_norm.reshape(1, -1), batch, seq)
    o_d = _swa(sq, sk, sv, _pad_lane_row(sinks, 0), batch, seq)
    return o_c, o_d


def _moe(x, router_w, router_b, w_gate, w_up, w_down, s_gate, s_up, s_down, ln_g, ln_b):
    t, d = x.shape
    bias_col = jnp.broadcast_to(router_b.reshape(-1, 1).astype(F32), (N_EXPERTS, LANES))
    idx, gate, rank, cnt = _router(x, router_w.T, bias_col)
    counts = cnt[:, 0].astype(jnp.int32)
    padded = (counts + EXPERT_BLOCK - 1) // EXPERT_BLOCK * EXPERT_BLOCK
    pad_end = jnp.cumsum(padded)
    pad_start = pad_end - padded
    start_col = jnp.broadcast_to(pad_start.astype(F32).reshape(-1, 1), (N_EXPERTS, LANES))
    dest = _dest_rows(idx, rank, start_col)
    n_blocks = t * TOP_K // EXPERT_BLOCK + N_EXPERTS
    rows = n_blocks * EXPERT_BLOCK
    block_row = jnp.arange(n_blocks, dtype=jnp.int32) * EXPERT_BLOCK
    block_e = jnp.minimum(jnp.sum((pad_end[None, :] <= block_row[:, None]).astype(jnp.int32), axis=1), N_EXPERTS - 1)
    n_used = (pad_end[-1:] // EXPERT_BLOCK).astype(jnp.int32)
    fill = jnp.stack([pad_start + counts, padded - counts], axis=1).astype(jnp.int32)
    xs = _dispatch(x, dest, fill, rows)
    ys = _experts(block_e, n_used, xs, w_gate.astype(BF16), w_up.astype(BF16), w_down.astype(BF16))
    return _combine(dest, x, gate.T, ys, s_gate.astype(BF16), s_up.astype(BF16), s_down.astype(BF16),
                    ln_g.reshape(1, -1), ln_b.reshape(1, -1))


def kernel(x, positions, ev_w_in, mla_q_norm, mla_w_qb, mla_kv_norm, mla_w_kvb, gdn_conv, gdn_a_log, gdn_dt_bias, gdn_norm, ev_w_out, od_w_in, mlstm_b_i, mlstm_b_f, mlstm_norm, swa_sinks, od_w_out, ln1_g, ln1_b, router_w, router_b, moe_w_gate, moe_w_up, moe_w_down, shared_w_gate, shared_w_up, shared_w_down, ln2_g, ln2_b):
    batch, seq, d = x.shape
    t = batch * seq
    pos = positions.reshape(t, 1).astype(F32)
    tabs_m = _rope_tables(pos, _rope_rows(MLA_ROPE, MLA_NOPE, MLA_NOPE))
    tabs_s = _rope_tables(pos, _rope_rows(SWA_D, 0, 0))
    h = x.reshape(t, d)
    for layer in range(DEPTH):
        j = layer // 2
        if layer % 2 == 0:
            a1, a2 = _even_mixer(h, tabs_m, ev_w_in[j], mla_q_norm[j], mla_w_qb[j], mla_kv_norm[j], mla_w_kvb[j],
                                 gdn_conv[j], gdn_a_log[j], gdn_dt_bias[j], gdn_norm[j], batch, seq)
            w_out = ev_w_out[j]
        else:
            a1, a2 = _odd_mixer(h, tabs_s, od_w_in[j], mlstm_b_i[j], mlstm_b_f[j], mlstm_norm[j], swa_sinks[j], batch, seq)
            w_out = od_w_out[j]
        h = _outproj_ln(h, a1, a2, w_out.astype(BF16), ln1_g[layer].reshape(1, -1), ln1_b[layer].reshape(1, -1))
        h = _moe(h, router_w[layer], router_b[layer], moe_w_gate[layer], moe_w_up[layer], moe_w_down[layer],
                 shared_w_gate[layer], shared_w_up[layer], shared_w_down[layer], ln2_g[layer], ln2_b[layer])
    return h.reshape(batch, seq, d)
```

```python
import functools
import math

import numpy as np
import jax
import jax.numpy as jnp
from jax import lax
from jax.experimental import pallas as pl
from jax.experimental.pallas import tpu as pltpu
from jax.experimental.pallas import tpu_sc as plsc

F32 = jnp.float32
BF16 = jnp.bfloat16
HI = lax.Precision.HIGHEST

D_MODEL = 1024
DEPTH = 4
ROPE_THETA = 10000.0
EPS = 1e-6
LN_EPS = 1e-5
MLA_H, MLA_NOPE, MLA_ROPE, MLA_V = 8, 64, 32, 64
Q_LORA, KV_LORA = 256, 128
GDN_H, GDN_DK, GDN_DV, CONV_W, GDN_CHUNK = 4, 128, 128, 4, 64
ML_H, ML_DK, ML_DV, ML_CHUNK = 4, 64, 128, 64
SWA_H, SWA_KV, SWA_D, WINDOW = 8, 2, 64, 128
N_EXPERTS, N_GROUPS, TOPK_GROUPS, TOP_K = 64, 8, 4, 8
D_EXPERT, D_SHARED = 256, 256
ROUTED_SCALE = 2.5
DN_ALPHA = (2 * DEPTH) ** 0.25

LANES = 128
V7X_VMEM_BYTES = 64 * 1024 * 1024
VMEM_LIMIT = 48 * 1024 * 1024

EXPERT_BLOCK = 256
SEQS_PER_STEP = 2
DMA_GROUP = 2
SC_CHUNK = 64


def _cparams(sem, vmem=VMEM_LIMIT):
    return pltpu.CompilerParams(dimension_semantics=sem, vmem_limit_bytes=vmem)


def _dot(a, b, precision=None):
    return jnp.dot(a, b, preferred_element_type=F32, precision=precision)


def _dot_nt(a, b, precision=None):
    return lax.dot_general(a, b, (((1,), (1,)), ((), ())), preferred_element_type=F32, precision=precision)


def _dot_tn(a, b, precision=None):
    return lax.dot_general(a, b, (((0,), (0,)), ((), ())), preferred_element_type=F32, precision=precision)


def _split2(a):
    hi = a.astype(BF16)
    lo = (a - hi.astype(F32)).astype(BF16)
    return hi, lo


def _split3(a):
    p1 = a.astype(BF16)
    r = a - p1.astype(F32)
    p2 = r.astype(BF16)
    p3 = (r - p2.astype(F32)).astype(BF16)
    return p1, p2, p3


def _dot3(a, b, dot=_dot):
    ah, al = _split2(a)
    bh, bl = _split2(b)
    return dot(ah, bh) + (dot(ah, bl) + dot(al, bh))


def _dot_sel(sel, b, dot=_dot):
    sel = sel.astype(BF16)
    p1, p2, p3 = _split3(b)
    return dot(sel, p1) + (dot(sel, p2) + dot(sel, p3))


def _sigmoid(x):
    return 1.0 / (1.0 + jnp.exp(-x))


def _softplus(x):
    return jnp.maximum(x, 0.0) + jnp.log(1.0 + jnp.exp(-jnp.abs(x)))


def _silu(x):
    return x * _sigmoid(x)


def _lane_bcast(x, c):
    return jnp.broadcast_to(x[:, c:c + 1], x.shape)


def _iota2(shape, dim):
    return lax.broadcasted_iota(jnp.int32, shape, dim)


def _rope_kernel(pos_ref, rows_ref, c_ref, s_ref):
    ang = pos_ref[...] * rows_ref[0:1, :]
    c_ref[...] = rows_ref[1:2, :] * jnp.cos(ang) + rows_ref[2:3, :]
    s_ref[...] = rows_ref[3:4, :] * jnp.sin(ang)


def _rope_tables(pos, rows, tm=512):
    t = pos.shape[0]
    return pl.pallas_call(
        _rope_kernel,
        grid=(t // tm,),
        in_specs=[pl.BlockSpec((tm, 1), lambda i: (i, 0)), pl.BlockSpec((8, LANES), lambda i: (0, 0))],
        out_specs=[pl.BlockSpec((tm, LANES), lambda i: (i, 0))] * 2,
        out_shape=[jax.ShapeDtypeStruct((t, LANES), F32)] * 2,
        compiler_params=_cparams(("arbitrary",)),
        name="rope_tables",
    )(pos, rows)


def _rope_rows(dim, first_lane, pad_one_lanes):
    half = dim // 2
    inv = ROPE_THETA ** (-(np.arange(0, dim, 2, dtype=np.float32) / dim))
    rows = np.zeros((8, LANES), np.float32)
    lo = slice(first_lane, first_lane + half)
    hi = slice(first_lane + half, first_lane + dim)
    rows[0, lo] = inv
    rows[0, hi] = inv
    rows[1, lo] = 1.0
    rows[1, hi] = 1.0
    rows[2, :pad_one_lanes] = 1.0
    rows[3, lo] = -1.0
    rows[3, hi] = 1.0
    return jnp.asarray(rows)


def _proj_kernel(x_ref, w_ref, *out_refs, offsets):
    xb = x_ref[...].astype(BF16)
    for o_ref, (a, b) in zip(out_refs, offsets):
        o_ref[...] = _dot(xb, w_ref[:, a:b]).astype(o_ref.dtype)


def _proj(x, w, widths, dtypes, tm=256):
    t, k = x.shape
    offs = np.concatenate([[0], np.cumsum(widths)]).tolist()
    offsets = tuple((offs[i], offs[i + 1]) for i in range(len(widths)))
    return pl.pallas_call(
        functools.partial(_proj_kernel, offsets=offsets),
        grid=(t // tm,),
        in_specs=[pl.BlockSpec((tm, k), lambda i: (i, 0)), pl.BlockSpec(w.shape, lambda i: (0, 0))],
        out_specs=[pl.BlockSpec((tm, n), lambda i: (i, 0)) for n in widths],
        out_shape=[jax.ShapeDtypeStruct((t, n), dt) for n, dt in zip(widths, dtypes)],
        compiler_params=_cparams(("arbitrary",)),
        name="in_proj",
    )(x, w)


OD_SEG = dict(mq=(0, 512), mk=(512, 1024), mv=(1024, 1536), mo=(1536, 2048), gates=(2048, 2176),
              sq=(2176, 3200), sqsw=(3200, 4224), sk=(4224, 4480), sksw=(4480, 4736), sv=(4736, 5248))
OD_COLS = 5248


def _proj_odd_kernel(x_ref, w_ref, c_ref, s_ref, mq_ref, mk_ref, mv_ref, mo_ref, mg_ref, sq_ref, sk_ref, sv_ref):
    xb = x_ref[...].astype(BF16)

    def seg(name):
        a, b = OD_SEG[name]
        return _dot(xb, w_ref[:, a:b])

    mq_ref[...] = seg("mq")
    mk_ref[...] = seg("mk")
    mv_ref[...] = seg("mv")
    mo_ref[...] = seg("mo")
    mg_ref[...] = seg("gates")
    c = c_ref[...]
    s = s_ref[...]
    c8 = jnp.concatenate([c] * SWA_H, axis=1)
    s8 = jnp.concatenate([s] * SWA_H, axis=1)
    sq_ref[...] = (seg("sq") * c8 + seg("sqsw") * s8).astype(sq_ref.dtype)
    c2 = jnp.concatenate([c] * SWA_KV, axis=1)
    s2 = jnp.concatenate([s] * SWA_KV, axis=1)
    sk_ref[...] = (seg("sk") * c2 + seg("sksw") * s2).astype(sk_ref.dtype)
    sv_ref[...] = seg("sv").astype(sv_ref.dtype)


def _proj_odd(x, w, ctab, stab, tm=256):
    t, k = x.shape
    widths = (512, 512, 512, 512, 128, SWA_H * LANES, SWA_KV * LANES, 2 * SWA_KV * LANES)
    dtypes = (F32, F32, F32, F32, F32, BF16, BF16, BF16)
    return pl.pallas_call(
        _proj_odd_kernel,
        grid=(t // tm,),
        in_specs=[pl.BlockSpec((tm, k), lambda i: (i, 0)), pl.BlockSpec(w.shape, lambda i: (0, 0)),
                  pl.BlockSpec((tm, LANES), lambda i: (i, 0)), pl.BlockSpec((tm, LANES), lambda i: (i, 0))],
        out_specs=[pl.BlockSpec((tm, n), lambda i: (i, 0)) for n in widths],
        out_shape=[jax.ShapeDtypeStruct((t, n), dt) for n, dt in zip(widths, dtypes)],
        compiler_params=_cparams(("arbitrary",)),
        name="in_proj_odd",
    )(x, w, ctab, stab)


def _rms(x, g):
    return x * lax.rsqrt(jnp.mean(x * x, axis=-1, keepdims=True) + EPS) * g


def _mla_prep_kernel(in_ref, c_ref, s_ref, qn_ref, kvn_ref, wq_ref, wkv_ref, q_ref, k_ref, v_ref):
    hw = MLA_H * LANES
    c = c_ref[...]
    s = s_ref[...]
    c8 = jnp.concatenate([c] * MLA_H, axis=1)
    s8 = jnp.concatenate([s] * MLA_H, axis=1)
    cqn = _rms(in_ref[:, 0:Q_LORA], qn_ref[...]).astype(BF16)
    qq = _dot(cqn, wq_ref[...])
    scale = (MLA_NOPE + MLA_ROPE) ** -0.5
    q_ref[...] = ((qq[:, :hw] * c8 + qq[:, hw:] * s8) * scale).astype(q_ref.dtype)
    ckvn = _rms(in_ref[:, Q_LORA:Q_LORA + KV_LORA], kvn_ref[...]).astype(BF16)
    kv = _dot(ckvn, wkv_ref[...])
    o = Q_LORA + KV_LORA
    krr = in_ref[:, o:o + LANES] * c + in_ref[:, o + LANES:o + 2 * LANES] * s
    k_ref[...] = (kv[:, :hw] + jnp.concatenate([krr] * MLA_H, axis=1)).astype(k_ref.dtype)
    v_ref[...] = kv[:, hw:].astype(v_ref.dtype)


def _mla_prep(mla_in, ctab, stab, qn, kvn, wq2, wkv2, tm=256):
    t = mla_in.shape[0]
    hw = MLA_H * LANES
    row = lambda i: (i, 0)
    fix = lambda i: (0, 0)
    return pl.pallas_call(
        _mla_prep_kernel,
        grid=(t // tm,),
        in_specs=[pl.BlockSpec((tm, mla_in.shape[1]), row), pl.BlockSpec((tm, LANES), row), pl.BlockSpec((tm, LANES), row),
                  pl.BlockSpec(qn.shape, fix), pl.BlockSpec(kvn.shape, fix),
                  pl.BlockSpec(wq2.shape, fix), pl.BlockSpec(wkv2.shape, fix)],
        out_specs=[pl.BlockSpec((tm, hw), row)] * 3,
        out_shape=[jax.ShapeDtypeStruct((t, hw), BF16)] * 3,
        compiler_params=_cparams(("arbitrary",)),
        name="mla_prep",
    )(mla_in, ctab, stab, qn, kvn, wq2, wkv2)


def _mla_attn_kernel(q_ref, k_ref, v_ref, o_ref, *, tq):
    i = pl.program_id(2)
    neg = -1e30

    def chunk(j, carry, masked):
        start = pl.multiple_of(j * tq, tq)
        out = []
        for hh in range(2):
            m, l, acc = carry[hh]
            q = q_ref[:, hh * LANES:(hh + 1) * LANES]
            kc = k_ref[pl.ds(start, tq), hh * LANES:(hh + 1) * LANES]
            vc = v_ref[pl.ds(start, tq), hh * LANES:(hh + 1) * LANES]
            s = _dot_nt(q, kc)
            if masked:
                s = jnp.where(_iota2(s.shape, 0) >= _iota2(s.shape, 1), s, neg)
            m_new = jnp.maximum(m, jnp.max(s, axis=-1, keepdims=True))
            alpha = jnp.exp(m - m_new)
            p = jnp.exp(s - m_new)
            l = alpha * l + jnp.sum(p, axis=-1, keepdims=True)
            acc = alpha * acc + _dot(p.astype(BF16), vc)
            out.append((m_new, l, acc))
        return tuple(out)

    one = (jnp.full((tq, 1), neg, F32), jnp.zeros((tq, 1), F32), jnp.zeros((tq, LANES), F32))
    carry = lax.fori_loop(0, i, lambda j, c: chunk(j, c, False), (one, one))
    (_, l0, acc0), (_, l1, acc1) = chunk(i, carry, True)
    o_ref[...] = (acc0 / l0 + acc1 / l1).astype(o_ref.dtype)


def _mla_attn(q, k, v, batch, seq, tq=512):
    tq = min(tq, seq)
    nq = seq // tq
    pairs = MLA_H // 2
    return pl.pallas_call(
        functools.partial(_mla_attn_kernel, tq=tq),
        grid=(batch, pairs, nq),
        in_specs=[pl.BlockSpec((tq, 2 * LANES), lambda b, p, i: (b * nq + i, p)),
                  pl.BlockSpec((seq, 2 * LANES), lambda b, p, i: (b, p)),
                  pl.BlockSpec((seq, 2 * LANES), lambda b, p, i: (b, p))],
        out_specs=pl.BlockSpec((tq, LANES), lambda b, p, i: (b * nq + i, p)),
        out_shape=jax.ShapeDtypeStruct((batch * seq, pairs * LANES), BF16),
        compiler_params=_cparams(("arbitrary", "arbitrary", "arbitrary")),
        name="mla_attn",
    )(q, k, v)


def _unit_lower_inverse_many(ns):
    c = ns[0].shape[0]
    eye = (_iota2((c, c), 0) == _iota2((c, c), 1)).astype(F32)
    xs = [-n for n in ns]
    ps = [eye + x for x in xs]
    xsplit = [_split2(x) for x in xs]
    for _ in range(int(math.log2(c)) - 1):
        xs = [_dot(xh, xh) + (_dot(xh, xl) + _dot(xl, xh)) for xh, xl in xsplit]
        xsplit = [_split2(x) for x in xs]
        psplit = [_split2(p) for p in ps]
        ps = [p + (_dot(ph, xh) + (_dot(ph, xl) + _dot(plo, xh)))
              for p, (ph, plo), (xh, xl) in zip(ps, psplit, xsplit)]
    return ps


def _gdn_kernel(qkv_ref, g_ref, z_ref, cw_ref, al_ref, dt_ref, on_ref, o_ref, ext_ref, st_ref):
    c = GDN_CHUNK
    hd = GDN_DK
    nqk = GDN_H * GDN_DK

    @pl.when(pl.program_id(1) == 0)
    def _():
        ext_ref[:, 0:8, :] = jnp.zeros((ext_ref.shape[0], 8, ext_ref.shape[2]), F32)
        st_ref[...] = jnp.zeros(st_ref.shape, F32)

    tri = (_iota2((c, c), 0) >= _iota2((c, c), 1)).astype(F32)
    row_ge = _iota2((c, c), 0) >= _iota2((c, c), 1)
    row_gt = _iota2((c, c), 0) > _iota2((c, c), 1)
    ones = jnp.ones((c, LANES), F32)
    lane = _iota2((c, LANES), 1)

    units = []
    for bb in range(qkv_ref.shape[0]):
        ext = ext_ref.at[bb]
        ext[8:8 + c, :] = qkv_ref[bb]
        conv = cw_ref[0:1, :] * ext[5:5 + c, :]
        for j in range(1, CONV_W):
            conv = conv + cw_ref[j:j + 1, :] * ext[5 + j:5 + j + c, :]
        ext[0:8, :] = ext[c:c + 8, :]
        act = _silu(conv)
        gates = g_ref[bb]
        beta_all = _sigmoid(gates)
        g_all = -jnp.exp(al_ref[...]) * _softplus(gates + dt_ref[...])
        gc_all = _dot_sel(tri, g_all)
        for h in range(GDN_H):
            q = act[:, h * hd:(h + 1) * hd]
            k = act[:, nqk + h * hd:nqk + (h + 1) * hd]
            v = act[:, 2 * nqk + h * GDN_DV:2 * nqk + (h + 1) * GDN_DV]
            q = q * lax.rsqrt(jnp.sum(q * q, axis=-1, keepdims=True) + EPS) * (GDN_DK ** -0.5)
            k = k * lax.rsqrt(jnp.sum(k * k, axis=-1, keepdims=True) + EPS)
            beta = _lane_bcast(beta_all, h)
            gcol = _lane_bcast(gc_all, GDN_H + h)
            grow = _dot_sel(ones, jnp.where(lane == GDN_H + h, gc_all, 0.0), _dot_nt)
            decay = jnp.exp(jnp.where(row_ge, gcol[:, :c] - grow, -jnp.inf))
            kb = k * beta
            lower = jnp.where(row_gt, _dot3(kb, k, _dot_nt) * decay, 0.0)
            eg = jnp.exp(gcol)
            glast = gcol[c - 1:c, :]
            units.append(dict(bb=bb, h=h, lower=lower, rhs=jnp.concatenate([v * beta, kb * eg], axis=1),
                              attn=_dot_nt(q.astype(BF16), k.astype(BF16)) * decay, qg=(q * eg).astype(BF16),
                              kg=(k * jnp.exp(glast - gcol)).astype(BF16), gl=jnp.exp(glast)))

    tinvs = _unit_lower_inverse_many([u["lower"] for u in units])
    uws = []
    for u, tinv in zip(units, tinvs):
        th, tl = _split2(tinv)
        rh, rl = _split2(u["rhs"])
        uws.append(_dot(th, rh) + (_dot(th, rl) + _dot(tl, rh)))
    states = [st_ref[u["bb"], u["h"]] for u in units]
    sbs = [s.astype(BF16) for s in states]
    vnews = [(uw[:, :GDN_DV] - _dot(uw[:, GDN_DV:].astype(BF16), sb)).astype(BF16) for uw, sb in zip(uws, sbs)]
    for u, state, sb, vnb in zip(units, states, sbs, vnews):
        bb, h = u["bb"], u["h"]
        o = _dot(u["qg"], sb) + _dot(u["attn"].astype(BF16), vnb)
        st_ref[bb, h] = state * u["gl"] + _dot_tn(u["kg"], vnb)
        o = _rms(o, on_ref[...]) * _silu(z_ref[bb, :, h * GDN_DV:(h + 1) * GDN_DV])
        o_ref[bb, :, h * GDN_DV:(h + 1) * GDN_DV] = o.astype(o_ref.dtype)


def _gdn(qkv, gates, z, conv_w, a_row, dt_row, o_norm, batch, seq):
    c = GDN_CHUNK
    nc = seq // c
    w3 = qkv.shape[1]
    wo = GDN_H * GDN_DV
    nb = SEQS_PER_STEP
    row = lambda b, i: (b, i, 0)
    fix = lambda b, i: (0, 0)
    out = pl.pallas_call(
        _gdn_kernel,
        grid=(batch // nb, nc),
        in_specs=[pl.BlockSpec((nb, c, w3), row), pl.BlockSpec((nb, c, LANES), row), pl.BlockSpec((nb, c, wo), row),
                  pl.BlockSpec(conv_w.shape, fix), pl.BlockSpec((1, LANES), fix), pl.BlockSpec((1, LANES), fix),
                  pl.BlockSpec((1, GDN_DV), fix)],
        out_specs=pl.BlockSpec((nb, c, wo), row),
        out_shape=jax.ShapeDtypeStruct((batch, seq, wo), BF16),
        scratch_shapes=[pltpu.VMEM((nb, c + 8, w3), F32), pltpu.VMEM((nb, GDN_H, GDN_DK, GDN_DV), F32)],
        compiler_params=_cparams(("arbitrary", "arbitrary")),
        name="gdn",
    )(qkv.reshape(batch, seq, w3), gates.reshape(batch, seq, LANES), z.reshape(batch, seq, wo), conv_w, a_row, dt_row, o_norm)
    return out.reshape(batch * seq, wo)


def _mlstm_kernel(q_ref, k_ref, v_ref, og_ref, g_ref, bias_ref, nrm_ref, o_ref, c_ref, n_ref, m_ref):
    @pl.when(pl.program_id(1) == 0)
    def _():
        c_ref[...] = jnp.zeros(c_ref.shape, F32)
        n_ref[...] = jnp.zeros(n_ref.shape, F32)
        m_ref[...] = jnp.zeros(m_ref.shape, F32)

    c = ML_CHUNK
    tri = (_iota2((c, c), 0) >= _iota2((c, c), 1)).astype(F32)
    row_ge = _iota2((c, c), 0) >= _iota2((c, c), 1)
    ones = jnp.ones((c, LANES), F32)
    lane = _iota2((c, LANES), 1)

    units = []
    for bb in range(q_ref.shape[0]):
        pre = g_ref[bb] + bias_ref[...]
        logf = jnp.minimum(pre, 0.0) - jnp.log(1.0 + jnp.exp(-jnp.abs(pre)))
        bcum_all = _dot_sel(tri, logf)
        for h in range(ML_H):
            q = q_ref[bb, :, h * LANES:(h + 1) * LANES]
            k = k_ref[bb, :, h * LANES:(h + 1) * LANES] * (ML_DK ** -0.5)
            units.append(dict(bb=bb, h=h, q=q, k=k, qb=q.astype(BF16), vb=v_ref[bb, :, h * ML_DV:(h + 1) * ML_DV].astype(BF16),
                              bcol=_lane_bcast(bcum_all, ML_H + h),
                              icol=_lane_bcast(pre, h),
                              col=jnp.where(lane == h, pre, 0.0) - jnp.where(lane == ML_H + h, bcum_all, 0.0),
                              m_st=m_ref[bb, h], cst=c_ref[bb, h], nst=n_ref[bb, h]))
    for u in units:
        u["row"] = _dot_sel(ones, u["col"], _dot_nt)
        u["qk"] = _dot_nt(u["qb"], u["k"].astype(BF16))
        u["qc"] = _dot(u["qb"], u["cst"].astype(BF16))
    for u in units:
        d = jnp.where(row_ge, u["bcol"][:, :c] + u["row"], -jnp.inf)
        inter = u["bcol"] + u["m_st"]
        m_t = jnp.maximum(inter, jnp.max(d, axis=-1, keepdims=True))
        u["m_t"] = m_t
        u["w_inter"] = jnp.exp(inter - m_t)
        u["p"] = jnp.exp(d - m_t[:, :c]) * u["qk"]
        u["pv"] = _dot(u["p"].astype(BF16), u["vb"])
        b_end = u["bcol"][c - 1:c, :]
        a = b_end - u["bcol"] + u["icol"]
        m_new = jnp.maximum(b_end + u["m_st"], jnp.max(a, axis=0, keepdims=True))
        u["m_new"] = m_new
        u["keep"] = jnp.exp(b_end + u["m_st"] - m_new)
        u["ks"] = u["k"] * jnp.exp(a - m_new)
        u["kv"] = _dot_tn(u["ks"].astype(BF16), u["vb"])
    for u in units:
        bb, h = u["bb"], u["h"]
        num = u["w_inter"] * u["qc"] + u["pv"]
        den = (u["w_inter"] * jnp.sum(u["q"] * u["nst"], axis=-1, keepdims=True)
               + jnp.sum(u["p"], axis=-1, keepdims=True))
        hc = num / jnp.maximum(jnp.abs(den), jnp.exp(-u["m_t"]))
        c_ref[bb, h] = u["cst"] * u["keep"] + u["kv"]
        n_ref[bb, h] = u["nst"] * u["keep"] + jnp.sum(u["ks"], axis=0, keepdims=True)
        m_ref[bb, h] = u["m_new"]
        hn = (_rms(hc, nrm_ref[:, h * ML_DV:(h + 1) * ML_DV])
              * _sigmoid(og_ref[bb, :, h * ML_DV:(h + 1) * ML_DV]))
        o_ref[bb, :, h * ML_DV:(h + 1) * ML_DV] = hn.astype(o_ref.dtype)


def _mlstm(mq, mk, mv, mo, gates, bias_row, norm_row, batch, seq):
    c = ML_CHUNK
    nc = seq // c
    nb = SEQS_PER_STEP
    row = lambda b, i: (b, i, 0)
    fix = lambda b, i: (0, 0)
    wide = ML_H * LANES
    r3 = lambda a: a.reshape(batch, seq, a.shape[-1])
    out = pl.pallas_call(
        _mlstm_kernel,
        grid=(batch // nb, nc),
        in_specs=[pl.BlockSpec((nb, c, wide), row), pl.BlockSpec((nb, c, wide), row), pl.BlockSpec((nb, c, wide), row),
                  pl.BlockSpec((nb, c, wide), row), pl.BlockSpec((nb, c, LANES), row),
                  pl.BlockSpec((1, LANES), fix), pl.BlockSpec((1, wide), fix)],
        out_specs=pl.BlockSpec((nb, c, wide), row),
        out_shape=jax.ShapeDtypeStruct((batch, seq, wide), BF16),
        scratch_shapes=[pltpu.VMEM((nb, ML_H, LANES, ML_DV), F32), pltpu.VMEM((nb, ML_H, 1, LANES), F32),
                        pltpu.VMEM((nb, ML_H, 1, LANES), F32)],
        compiler_params=_cparams(("arbitrary", "arbitrary")),
        name="mlstm",
    )(r3(mq), r3(mk), r3(mv), r3(mo), r3(gates), bias_row, norm_row)
    return out.reshape(batch * seq, wide)


def _swa_kernel(q_ref, kc_ref, kp_ref, vc_ref, vp_ref, sink_ref, o_ref):
    w = WINDOW
    n = pl.program_id(1)
    scale = SWA_D ** -0.5
    qi = _iota2((w, w), 0)
    kj = _iota2((w, w), 1)
    mask_c = kj <= qi
    mask_p = jnp.logical_and(kj > qi, n > 0)
    grp = SWA_H // SWA_KV
    neg = -1e30
    scores = []
    for h in range(SWA_H):
        g = h // grp
        q = q_ref[:, h * LANES:(h + 1) * LANES]
        scores.append((_dot_nt(q, kc_ref[:, g * LANES:(g + 1) * LANES]), _dot_nt(q, kp_ref[:, g * LANES:(g + 1) * LANES])))
    probs = []
    for h, (sc, sp) in enumerate(scores):
        s_c = jnp.where(mask_c, sc * scale, neg)
        s_p = jnp.where(mask_p, sp * scale, neg)
        sink = sink_ref[:, h:h + 1]
        m = jnp.maximum(jnp.max(jnp.maximum(s_c, s_p), axis=-1, keepdims=True), sink)
        p_c = jnp.where(mask_c, jnp.exp(s_c - m), 0.0)
        p_p = jnp.where(mask_p, jnp.exp(s_p - m), 0.0)
        den = jnp.sum(p_c + p_p, axis=-1, keepdims=True) + jnp.exp(sink - m)
        inv = 1.0 / den
        probs.append(((p_c * inv).astype(BF16), (p_p * inv).astype(BF16)))
    for pair in range(SWA_H // 2):
        acc = None
        for sub in range(2):
            h = 2 * pair + sub
            vcol = (2 * (h // grp) + sub) * LANES
            p_c, p_p = probs[h]
            part = _dot(p_c, vc_ref[:, vcol:vcol + LANES]) + _dot(p_p, vp_ref[:, vcol:vcol + LANES])
            acc = part if acc is None else acc + part
        o_ref[:, pair * LANES:(pair + 1) * LANES] = acc.astype(o_ref.dtype)


def _swa(sq, sk, sv, sinks_row, batch, seq):
    w = WINDOW
    nb = seq // w
    cur = lambda b, n: (b * nb + n, 0)
    prev = lambda b, n: (b * nb + jnp.maximum(n - 1, 0), 0)
    return pl.pallas_call(
        _swa_kernel,
        grid=(batch, nb),
        in_specs=[pl.BlockSpec((w, sq.shape[1]), cur),
                  pl.BlockSpec((w, sk.shape[1]), cur), pl.BlockSpec((w, sk.shape[1]), prev),
                  pl.BlockSpec((w, sv.shape[1]), cur), pl.BlockSpec((w, sv.shape[1]), prev),
                  pl.BlockSpec((1, LANES), lambda b, n: (0, 0))],
        out_specs=pl.BlockSpec((w, SWA_H * SWA_D), cur),
        out_shape=jax.ShapeDtypeStruct((batch * seq, SWA_H * SWA_D), BF16),
        compiler_params=_cparams(("arbitrary", "arbitrary")),
        name="swa",
    )(sq, sk, sk, sv, sv, sinks_row)


def _layer_norm(h, g, b):
    mu = jnp.mean(h, axis=-1, keepdims=True)
    d = h - mu
    var = jnp.mean(d * d, axis=-1, keepdims=True)
    return d * lax.rsqrt(var + LN_EPS) * g + b


def _outproj_kernel(x_ref, a1_ref, a2_ref, w_ref, g_ref, b_ref, o_ref):
    k1 = a1_ref.shape[1]
    y = _dot(a1_ref[...].astype(BF16), w_ref[0:k1, :]) + _dot(a2_ref[...].astype(BF16), w_ref[k1:, :])
    o_ref[...] = _layer_norm(DN_ALPHA * x_ref[...] + y, g_ref[...], b_ref[...])


def _outproj_ln(x, a1, a2, w, g, b, tm=256):
    t, d = x.shape
    row = lambda i: (i, 0)
    fix = lambda i: (0, 0)
    return pl.pallas_call(
        _outproj_kernel,
        grid=(t // tm,),
        in_specs=[pl.BlockSpec((tm, d), row), pl.BlockSpec((tm, a1.shape[1]), row), pl.BlockSpec((tm, a2.shape[1]), row),
                  pl.BlockSpec(w.shape, fix), pl.BlockSpec((1, d), fix), pl.BlockSpec((1, d), fix)],
        out_specs=pl.BlockSpec((tm, d), row),
        out_shape=jax.ShapeDtypeStruct((t, d), F32),
        compiler_params=_cparams(("arbitrary",)),
        name="outproj_ln",
    )(x, a1, a2, w, g, b)


def _first_index(x, m, iota_f, sentinel):
    return jnp.min(jnp.where(x == m, iota_f, sentinel), axis=0, keepdims=True)


def _router_kernel(x_ref, wt_ref, bias_ref, idx_ref, gate_ref, rank_ref, cnt_ref, carry_ref):
    tm = x_ref.shape[0]
    e = N_EXPERTS
    gs = e // N_GROUPS
    ninf = -jnp.inf

    @pl.when(pl.program_id(0) == 0)
    def _():
        carry_ref[...] = jnp.zeros(carry_ref.shape, F32)

    logits = _dot_nt(wt_ref[...], x_ref[...], HI)
    scores = _sigmoid(logits)
    sel = scores + bias_ref[:, 0:1]

    sub_f = _iota2((gs, tm), 0).astype(F32)
    gscore = []
    for g in range(N_GROUPS):
        blk = sel[g * gs:(g + 1) * gs, :]
        m1 = jnp.max(blk, axis=0, keepdims=True)
        i1 = _first_index(blk, m1, sub_f, float(gs))
        m2 = jnp.max(jnp.where(sub_f == i1, ninf, blk), axis=0, keepdims=True)
        gscore.append(m1 + m2)
    gsc = jnp.concatenate(gscore, axis=0)
    grp_f = _iota2((N_GROUPS, tm), 0).astype(F32)
    gmask = jnp.zeros((N_GROUPS, tm), F32)
    for _ in range(TOPK_GROUPS):
        m = jnp.max(gsc, axis=0, keepdims=True)
        gi = _first_index(gsc, m, grp_f, float(N_GROUPS))
        hit = grp_f == gi
        gmask = jnp.where(hit, 1.0, gmask)
        gsc = jnp.where(hit, ninf, gsc)
    masked = jnp.concatenate(
        [jnp.where(gmask[g:g + 1, :] > 0.0, sel[g * gs:(g + 1) * gs, :], ninf) for g in range(N_GROUPS)], axis=0)

    exp_f = _iota2((e, tm), 0).astype(F32)
    chosen = jnp.zeros((e, tm), F32)
    idxs, gates = [], []
    for _ in range(TOP_K):
        m = jnp.max(masked, axis=0, keepdims=True)
        ei = _first_index(masked, m, exp_f, float(e))
        hit = exp_f == ei
        idxs.append(ei)
        gates.append(jnp.sum(jnp.where(hit, scores, 0.0), axis=0, keepdims=True))
        chosen = jnp.where(hit, 1.0, chosen)
        masked = jnp.where(hit, ninf, masked)
    gate = jnp.concatenate(gates, axis=0)
    gate = gate / jnp.sum(gate, axis=0, keepdims=True) * ROUTED_SCALE
    idx_f = jnp.concatenate(idxs, axis=0)

    upper = (_iota2((tm, tm), 0) < _iota2((tm, tm), 1)).astype(BF16)
    before = _dot(chosen.astype(BF16), upper) + carry_ref[...][:, 0:1]
    ranks = [jnp.sum(jnp.where(exp_f == idxs[k], before, 0.0), axis=0, keepdims=True) for k in range(TOP_K)]
    carry_ref[...] = carry_ref[...] + jnp.sum(chosen, axis=1, keepdims=True)

    idx_ref[...] = idx_f.astype(jnp.int32)
    gate_ref[...] = gate
    rank_ref[...] = jnp.concatenate(ranks, axis=0).astype(jnp.int32)
    cnt_ref[...] = carry_ref[...]


def _router(x, wt, bias_col, tm=512):
    t, d = x.shape
    col = lambda i: (0, i)
    fix = lambda i: (0, 0)
    return pl.pallas_call(
        _router_kernel,
        grid=(t // tm,),
        in_specs=[pl.BlockSpec((tm, d), lambda i: (i, 0)), pl.BlockSpec(wt.shape, fix), pl.BlockSpec((N_EXPERTS, LANES), fix)],
        out_specs=[pl.BlockSpec((TOP_K, tm), col), pl.BlockSpec((TOP_K, tm), col), pl.BlockSpec((TOP_K, tm), col),
                   pl.BlockSpec((N_EXPERTS, LANES), fix)],
        out_shape=[jax.ShapeDtypeStruct((TOP_K, t), jnp.int32), jax.ShapeDtypeStruct((TOP_K, t), F32),
                   jax.ShapeDtypeStruct((TOP_K, t), jnp.int32), jax.ShapeDtypeStruct((N_EXPERTS, LANES), F32)],
        scratch_shapes=[pltpu.VMEM((N_EXPERTS, LANES), F32)],
        compiler_params=_cparams(("arbitrary",)),
        name="router",
    )(x, wt, bias_col)


def _dest_kernel(idx_ref, rank_ref, start_ref, dest_ref):
    tm = idx_ref.shape[1]
    exp_i = _iota2((N_EXPERTS, tm), 0)
    start = start_ref[:, 0:1]
    rows = [jnp.sum(jnp.where(exp_i == idx_ref[s:s + 1, :], start, 0.0), axis=0, keepdims=True) for s in range(TOP_K)]
    dest_ref[...] = jnp.concatenate(rows, axis=0).astype(jnp.int32) + rank_ref[...]


def _dest_rows(idx, rank, start_col, tm=2048):
    t = idx.shape[1]
    tm = min(tm, t)
    col = lambda i: (0, i)
    return pl.pallas_call(
        _dest_kernel,
        grid=(t // tm,),
        in_specs=[pl.BlockSpec((TOP_K, tm), col), pl.BlockSpec((TOP_K, tm), col),
                  pl.BlockSpec((N_EXPERTS, LANES), lambda i: (0, 0))],
        out_specs=pl.BlockSpec((TOP_K, tm), col),
        out_shape=jax.ShapeDtypeStruct((TOP_K, t), jnp.int32),
        compiler_params=_cparams(("arbitrary",)),
        name="moe_dest",
    )(idx, rank, start_col)


def _pack_pairs(x):
    n = x.shape[1] // 2
    hi = lax.bitcast_convert_type(x[:, :n].astype(BF16).astype(F32), jnp.uint32)
    lo = lax.bitcast_convert_type(x[:, n:].astype(BF16).astype(F32), jnp.uint32)
    return hi | (lo >> 16)


def _unpack_pairs(w):
    hi = lax.bitcast_convert_type(w & jnp.uint32(0xFFFF0000), F32)
    lo = lax.bitcast_convert_type(w << 16, F32)
    return hi, lo


def _dispatch_kernel(dest_ref, fill_ref, x_ref, xs_ref, pk_ref, zero_ref, sem, zsem):
    tm = x_ref.shape[0]
    pk_ref[...] = _pack_pairs(x_ref[...])

    def issue(g, _):
        t0 = g * DMA_GROUP
        rows = [[dest_ref[s, t0 + j] for s in range(TOP_K)] for j in range(DMA_GROUP)]
        for j in range(DMA_GROUP):
            for s in range(TOP_K):
                pltpu.make_async_copy(pk_ref.at[pl.ds(t0 + j, 1), :], xs_ref.at[pl.ds(rows[j][s], 1), :],
                                      sem).start(priority=s % 2)
        return 0

    lax.fori_loop(0, tm // DMA_GROUP, issue, 0)

    @pl.when(pl.program_id(0) == 0)
    def _():
        zero_ref[...] = jnp.zeros(zero_ref.shape, zero_ref.dtype)

        def pad_copy(r):
            return pltpu.make_async_copy(zero_ref, xs_ref.at[pl.ds(r, 1), :], zsem)

        def per_expert(e, _):
            start = fill_ref[e, 0]
            cnt = fill_ref[e, 1]
            lax.fori_loop(0, cnt, lambda r, c: (pad_copy(start + r).start(), c)[1], 0)
            lax.fori_loop(0, cnt, lambda r, c: (pad_copy(start + r).wait(), c)[1], 0)
            return 0

        lax.fori_loop(0, N_EXPERTS, per_expert, 0)

    for s in range(TOP_K):
        pltpu.make_async_copy(pk_ref, xs_ref.at[pl.ds(0, tm), :], sem).wait()


def _dispatch(x, dest, fill, rows, tm=256):
    t, d = x.shape
    return pl.pallas_call(
        _dispatch_kernel,
        grid=(t // tm,),
        in_specs=[pl.BlockSpec((TOP_K, tm), lambda i: (0, i), memory_space=pltpu.SMEM),
                  pl.BlockSpec(memory_space=pltpu.SMEM),
                  pl.BlockSpec((tm, d), lambda i: (i, 0))],
        out_specs=pl.BlockSpec(memory_space=pl.ANY),
        out_shape=jax.ShapeDtypeStruct((rows, d // 2), jnp.uint32),
        scratch_shapes=[pltpu.VMEM((tm, d // 2), jnp.uint32), pltpu.VMEM((1, d // 2), jnp.uint32),
                        pltpu.SemaphoreType.DMA(()), pltpu.SemaphoreType.DMA(())],
        compiler_params=_cparams(("arbitrary",)),
        name="moe_dispatch",
    )(dest, fill, x)


def _experts_kernel(be_ref, nu_ref, xs_ref, wg_ref, wu_ref, wd_ref, ys_ref):
    i = pl.program_id(0)

    @pl.when(i < nu_ref[0])
    def _():
        half = xs_ref.shape[1]
        xa, xb = _unpack_pairs(xs_ref[...])
        xa = xa.astype(BF16)
        xb = xb.astype(BF16)
        gate = _dot(xa, wg_ref[0, :half, :]) + _dot(xb, wg_ref[0, half:, :])
        up = _dot(xa, wu_ref[0, :half, :]) + _dot(xb, wu_ref[0, half:, :])
        h = _silu(gate) * up
        ys_ref[...] = _pack_pairs(_dot(h.astype(BF16), wd_ref[0]))

    @pl.when(i >= nu_ref[0])
    def _():
        ys_ref[...] = jnp.zeros(ys_ref.shape, ys_ref.dtype)


def _experts(block_e, n_used, xs, wg, wu, wd):
    rows, half = xs.shape
    d = 2 * half
    nb = rows // EXPERT_BLOCK
    blk = lambda i, be, nu: (jnp.minimum(i, nu[0] - 1), 0)
    wsel = lambda i, be, nu: (be[i], 0, 0)
    return pl.pallas_call(
        _experts_kernel,
        grid_spec=pltpu.PrefetchScalarGridSpec(
            num_scalar_prefetch=2,
            grid=(nb,),
            in_specs=[pl.BlockSpec((EXPERT_BLOCK, half), blk),
                      pl.BlockSpec((1, d, D_EXPERT), wsel), pl.BlockSpec((1, d, D_EXPERT), wsel),
                      pl.BlockSpec((1, D_EXPERT, d), wsel)],
            out_specs=pl.BlockSpec((EXPERT_BLOCK, half), lambda i, be, nu: (i, 0)),
        ),
        out_shape=jax.ShapeDtypeStruct((rows, half), jnp.uint32),
        compiler_params=_cparams(("arbitrary",)),
        name="moe_experts",
    )(block_e, n_used, xs, wg, wu, wd)


def _sc_gather_rows(table, idx, chunk=SC_CHUNK):
    n = idx.shape[0]
    width = table.shape[1]
    info = plsc.get_sparse_core_info()
    ncores, nsub = info.num_cores, info.num_subcores
    per_worker = n // (ncores * nsub)
    nchunk = per_worker // chunk
    mesh = plsc.VectorSubcoreMesh(core_axis_name="c", subcore_axis_name="s")

    @functools.partial(
        pl.kernel, mesh=mesh,
        out_type=jax.ShapeDtypeStruct((n, width), table.dtype),
        scratch_types=[pltpu.VMEM((chunk,), jnp.int32), pltpu.VMEM((chunk, width), table.dtype), pltpu.SemaphoreType.DMA],
    )
    def gather(table_hbm, idx_hbm, out_hbm, idx_v, rows_v, sem):
        base = (lax.axis_index("s") * ncores + lax.axis_index("c")) * per_worker

        @pl.loop(0, nchunk)
        def _(i):
            off = pl.multiple_of(base + i * chunk, chunk)
            pltpu.sync_copy(idx_hbm.at[pl.ds(off, chunk)], idx_v)
            pltpu.async_copy(table_hbm.at[idx_v], rows_v, sem).wait()
            pltpu.sync_copy(rows_v, out_hbm.at[pl.ds(off, chunk)])

    return gather(table, idx)


def _combine_kernel(x_ref, gate_ref, rows_ref, sg_ref, su_ref, sd_ref, g_ref, b_ref, o_ref):
    x = x_ref[...]
    xb = x.astype(BF16)
    hs = _silu(_dot(xb, sg_ref[...])) * _dot(xb, su_ref[...])
    ff = _dot(hs.astype(BF16), sd_ref[...])
    gate = gate_ref[...]
    half = rows_ref.shape[2]
    ya = ff[:, :half]
    yb = ff[:, half:]
    for s in range(TOP_K):
        a, b = _unpack_pairs(rows_ref[s])
        ya = ya + gate[:, s:s + 1] * a
        yb = yb + gate[:, s:s + 1] * b
    ff = jnp.concatenate([ya, yb], axis=1)
    o_ref[...] = _layer_norm(DN_ALPHA * x + ff, g_ref[...], b_ref[...])


def _combine(x, gate_t, rows, sg, su, sd, g, b, tm=256):
    t, d = x.shape
    row = lambda i: (i, 0)
    fix = lambda i: (0, 0)
    return pl.pallas_call(
        _combine_kernel,
        grid=(t // tm,),
        in_specs=[pl.BlockSpec((tm, d), row), pl.BlockSpec((tm, TOP_K), row),
                  pl.BlockSpec((TOP_K, tm, d // 2), lambda i: (0, i, 0)),
                  pl.BlockSpec(sg.shape, fix), pl.BlockSpec(su.shape, fix), pl.BlockSpec(sd.shape, fix),
                  pl.BlockSpec((1, d), fix), pl.BlockSpec((1, d), fix)],
        out_specs=pl.BlockSpec((tm, d), row),
        out_shape=jax.ShapeDtypeStruct((t, d), F32),
        compiler_params=_cparams(("arbitrary",)),
        name="moe_combine",
    )(x, gate_t, rows, sg, su, sd, g, b)


def _take_cols(w, idx):
    wz = jnp.concatenate([w, jnp.zeros((w.shape[0], 1), w.dtype)], axis=1)
    idx = np.where(np.asarray(idx) < 0, w.shape[1], np.asarray(idx))
    return jnp.take(wz, jnp.asarray(idx, jnp.int32), axis=1)


def _pad_lane_row(v, first_lane, width=LANES):
    out = jnp.zeros((1, width), F32)
    return lax.dynamic_update_slice(out, v.reshape(1, -1).astype(F32), (0, first_lane))


def _even_in_cols():
    z = lambda n: -np.ones(n, int)
    kr0 = Q_LORA + KV_LORA
    half = MLA_ROPE // 2
    cols = [np.arange(0, Q_LORA), np.arange(Q_LORA, Q_LORA + KV_LORA),
            z(64), np.arange(kr0, kr0 + MLA_ROPE), z(32),
            z(64), np.arange(kr0 + half, kr0 + MLA_ROPE), np.arange(kr0, kr0 + half), z(32)]
    g0 = kr0 + MLA_ROPE
    nqk = GDN_H * GDN_DK
    cols.append(np.arange(g0, g0 + 3 * nqk))
    zoff = g0 + 3 * nqk + 2 * GDN_H
    cols.append(np.arange(zoff, zoff + GDN_H * GDN_DV))
    cols += [np.arange(g0 + 3 * nqk, g0 + 3 * nqk + 2 * GDN_H), z(LANES - 2 * GDN_H)]
    return np.concatenate(cols)


EV_WIDTHS = (Q_LORA + KV_LORA + 2 * LANES, 3 * GDN_H * GDN_DK, GDN_H * GDN_DV, LANES)


def _mla_q_cols():
    per = MLA_NOPE + MLA_ROPE
    half = MLA_ROPE // 2
    main, sw = [], []
    for h in range(MLA_H):
        b = h * per
        main += [np.arange(b, b + per), -np.ones(LANES - per, int)]
        sw += [-np.ones(MLA_NOPE, int), np.arange(b + MLA_NOPE + half, b + per), np.arange(b + MLA_NOPE, b + MLA_NOPE + half),
               -np.ones(LANES - per, int)]
    return np.concatenate(main + sw)


def _mla_kv_cols():
    per = MLA_NOPE + MLA_V
    kc, vc = [], []
    for h in range(MLA_H):
        b = h * per
        kc += [np.arange(b, b + MLA_NOPE), -np.ones(LANES - MLA_NOPE, int)]
        vv = np.arange(b + MLA_NOPE, b + per)
        pad = -np.ones(LANES - MLA_V, int)
        vc += [vv, pad] if h % 2 == 0 else [pad, vv]
    return np.concatenate(kc + vc)


def _odd_in_cols():
    z = lambda n: -np.ones(n, int)
    o = 0
    cols = []
    mq0, mk0 = 0, ML_H * ML_DK
    for base in (mq0, mk0):
        for h in range(ML_H):
            cols += [np.arange(base + h * ML_DK, base + (h + 1) * ML_DK), z(LANES - ML_DK)]
    mv0 = 2 * ML_H * ML_DK
    cols.append(np.arange(mv0, mv0 + ML_H * ML_DV))
    mi0 = mv0 + ML_H * ML_DV
    mo0 = mi0 + 2 * ML_H
    cols.append(np.arange(mo0, mo0 + ML_H * ML_DV))
    cols += [np.arange(mi0, mi0 + 2 * ML_H), z(LANES - 2 * ML_H)]
    sq0 = mo0 + ML_H * ML_DV
    sk0 = sq0 + SWA_H * SWA_D
    sv0 = sk0 + SWA_KV * SWA_D
    half = SWA_D // 2

    def heads(base, n, swapped):
        out = []
        for h in range(n):
            b = base + h * SWA_D
            if swapped:
                out += [np.arange(b + half, b + SWA_D), np.arange(b, b + half), z(LANES - SWA_D)]
            else:
                out += [np.arange(b, b + SWA_D), z(LANES - SWA_D)]
        return out

    cols += heads(sq0, SWA_H, False) + heads(sq0, SWA_H, True) + heads(sk0, SWA_KV, False) + heads(sk0, SWA_KV, True)
    for g in range(SWA_KV):
        vv = np.arange(sv0 + g * SWA_D, sv0 + (g + 1) * SWA_D)
        cols += [vv, z(LANES - SWA_D), z(LANES - SWA_D), vv]
    return np.concatenate(cols)


def _even_mixer(x, tabs, w_in, q_norm, w_qb, kv_norm, w_kvb, conv_w, a_log, dt_bias, o_norm, batch, seq):
    ctab, stab = tabs
    w = _take_cols(w_in, _even_in_cols()).astype(BF16)
    mla_in, qkv, z, gates = _proj(x, w, EV_WIDTHS, (F32, F32, F32, F32))
    wq2 = _take_cols(w_qb, _mla_q_cols()).astype(BF16)
    wkv2 = _take_cols(w_kvb, _mla_kv_cols()).astype(BF16)
    q, k, v = _mla_prep(mla_in, ctab, stab, q_norm.reshape(1, -1), kv_norm.reshape(1, -1), wq2, wkv2)
    o_a = _mla_attn(q, k, v, batch, seq)
    o_b = _gdn(qkv, gates, z, conv_w, _pad_lane_row(a_log, GDN_H), _pad_lane_row(dt_bias, GDN_H),
               o_norm.reshape(1, -1), batch, seq)
    return o_a, o_b


def _odd_mixer(x, tabs, w_in, b_i, b_f, ml_norm, sinks, batch, seq):
    ctab, stab = tabs
    w = _take_cols(w_in, _odd_in_cols()).astype(BF16)
    mq, mk, mv, mo, mg, sq, sk, sv = _proj_odd(x, w, ctab, stab)
    bias_row = _pad_lane_row(jnp.concatenate([b_i, b_f]), 0)
    o_c = _mlstm(mq, mk, mv, mo, mg, bias_row, ml_norm.reshape(1, -1), batch, seq)
    o_d = _swa(sq, sk, sv, _pad_lane_row(sinks, 0), batch, seq)
    return o_c, o_d


def _moe(x, router_w, router_b, w_gate, w_up, w_down, s_gate, s_up, s_down, ln_g, ln_b):
    t, d = x.shape
    bias_col = jnp.broadcast_to(router_b.reshape(-1, 1).astype(F32), (N_EXPERTS, LANES))
    idx, gate, rank, cnt = _router(x, router_w.T, bias_col)
    counts = cnt[:, 0].astype(jnp.int32)
    padded = (counts + EXPERT_BLOCK - 1) // EXPERT_BLOCK * EXPERT_BLOCK
    pad_end = jnp.cumsum(padded)
    pad_start = pad_end - padded
    start_col = jnp.broadcast_to(pad_start.astype(F32).reshape(-1, 1), (N_EXPERTS, LANES))
    dest = _dest_rows(idx, rank, start_col)
    n_blocks = t * TOP_K // EXPERT_BLOCK + N_EXPERTS
    rows = n_blocks * EXPERT_BLOCK
    block_row = jnp.arange(n_blocks, dtype=jnp.int32) * EXPERT_BLOCK
    block_e = jnp.minimum(jnp.sum((pad_end[None, :] <= block_row[:, None]).astype(jnp.int32), axis=1), N_EXPERTS - 1)
    n_used = (pad_end[-1:] // EXPERT_BLOCK).astype(jnp.int32)
    fill = jnp.stack([pad_start + counts, padded - counts], axis=1).astype(jnp.int32)
    xs = _dispatch(x, dest, fill, rows)
    ys = _experts(block_e, n_used, xs, w_gate.astype(BF16), w_up.astype(BF16), w_down.astype(BF16))
    picked = _sc_gather_rows(ys, dest.reshape(-1)).reshape(TOP_K, t, d // 2)
    return _combine(x, gate.T, picked, s_gate.astype(BF16), s_up.astype(BF16), s_down.astype(BF16),
                    ln_g.reshape(1, -1), ln_b.reshape(1, -1))


def kernel(x, positions, ev_w_in, mla_q_norm, mla_w_qb, mla_kv_norm, mla_w_kvb, gdn_conv, gdn_a_log, gdn_dt_bias, gdn_norm, ev_w_out, od_w_in, mlstm_b_i, mlstm_b_f, mlstm_norm, swa_sinks, od_w_out, ln1_g, ln1_b, router_w, router_b, moe_w_gate, moe_w_up, moe_w_down, shared_w_gate, shared_w_up, shared_w_down, ln2_g, ln2_b):
    batch, seq, d = x.shape
    t = batch * seq
    pos = positions.reshape(t, 1).astype(F32)
    tabs_m = _rope_tables(pos, _rope_rows(MLA_ROPE, MLA_NOPE, MLA_NOPE))
    tabs_s = _rope_tables(pos, _rope_rows(SWA_D, 0, 0))
    h = x.reshape(t, d)
    for layer in range(DEPTH):
        j = layer // 2
        if layer % 2 == 0:
            a1, a2 = _even_mixer(h, tabs_m, ev_w_in[j], mla_q_norm[j], mla_w_qb[j], mla_kv_norm[j], mla_w_kvb[j],
                                 gdn_conv[j], gdn_a_log[j], gdn_dt_bias[j], gdn_norm[j], batch, seq)
            w_out = ev_w_out[j]
        else:
            a1, a2 = _odd_mixer(h, tabs_s, od_w_in[j], mlstm_b_i[j], mlstm_b_f[j], mlstm_norm[j], swa_sinks[j], batch, seq)
            w_out = od_w_out[j]
        h = _outproj_ln(h, a1, a2, w_out.astype(BF16), ln1_g[layer].reshape(1, -1), ln1_b[layer].reshape(1, -1))
        h = _moe(h, router_w[layer], router_b[layer], moe_w_gate[layer], moe_w_up[layer], moe_w_down[layer],
                 shared_w_gate[layer], shared_w_up[layer], shared_w_down[layer], ln2_g[layer], ln2_b[layer])
    return h.reshape(batch, seq, d)
```

```python
import functools
import math

import numpy as np
import jax
import jax.numpy as jnp
from jax import lax
from jax.experimental import pallas as pl
from jax.experimental.pallas import tpu as pltpu
from jax.experimental.pallas import tpu_sc as plsc

F32 = jnp.float32
BF16 = jnp.bfloat16
HI = lax.Precision.HIGHEST

D_MODEL = 1024
DEPTH = 4
ROPE_THETA = 10000.0
EPS = 1e-6
LN_EPS = 1e-5
MLA_H, MLA_NOPE, MLA_ROPE, MLA_V = 8, 64, 32, 64
Q_LORA, KV_LORA = 256, 128
GDN_H, GDN_DK, GDN_DV, CONV_W, GDN_CHUNK = 4, 128, 128, 4, 64
ML_H, ML_DK, ML_DV, ML_CHUNK = 4, 64, 128, 64
SWA_H, SWA_KV, SWA_D, WINDOW = 8, 2, 64, 128
N_EXPERTS, N_GROUPS, TOPK_GROUPS, TOP_K = 64, 8, 4, 8
D_EXPERT, D_SHARED = 256, 256
ROUTED_SCALE = 2.5
DN_ALPHA = (2 * DEPTH) ** 0.25

LANES = 128
V7X_VMEM_BYTES = 64 * 1024 * 1024
VMEM_LIMIT = 48 * 1024 * 1024

EXPERT_BLOCK = 256
SEQS_PER_STEP = 2
DMA_GROUP = 2
SC_CHUNK = 64


def _cparams(sem, vmem=VMEM_LIMIT):
    return pltpu.CompilerParams(dimension_semantics=sem, vmem_limit_bytes=vmem)


def _dot(a, b, precision=None):
    return jnp.dot(a, b, preferred_element_type=F32, precision=precision)


def _dot_nt(a, b, precision=None):
    return lax.dot_general(a, b, (((1,), (1,)), ((), ())), preferred_element_type=F32, precision=precision)


def _dot_tn(a, b, precision=None):
    return lax.dot_general(a, b, (((0,), (0,)), ((), ())), preferred_element_type=F32, precision=precision)


def _split2(a):
    hi = a.astype(BF16)
    lo = (a - hi.astype(F32)).astype(BF16)
    return hi, lo


def _split3(a):
    p1 = a.astype(BF16)
    r = a - p1.astype(F32)
    p2 = r.astype(BF16)
    p3 = (r - p2.astype(F32)).astype(BF16)
    return p1, p2, p3


def _dot3(a, b, dot=_dot):
    ah, al = _split2(a)
    bh, bl = _split2(b)
    return dot(ah, bh) + (dot(ah, bl) + dot(al, bh))


def _dot_sel(sel, b, dot=_dot):
    sel = sel.astype(BF16)
    p1, p2, p3 = _split3(b)
    return dot(sel, p1) + (dot(sel, p2) + dot(sel, p3))


def _sigmoid(x):
    return 1.0 / (1.0 + jnp.exp(-x))


def _softplus(x):
    return jnp.maximum(x, 0.0) + jnp.log(1.0 + jnp.exp(-jnp.abs(x)))


def _silu(x):
    return x * _sigmoid(x)


def _lane_bcast(x, c):
    return jnp.broadcast_to(x[:, c:c + 1], x.shape)


def _iota2(shape, dim):
    return lax.broadcasted_iota(jnp.int32, shape, dim)


def _rope_kernel(pos_ref, rows_ref, c_ref, s_ref):
    ang = pos_ref[...] * rows_ref[0:1, :]
    c_ref[...] = rows_ref[1:2, :] * jnp.cos(ang) + rows_ref[2:3, :]
    s_ref[...] = rows_ref[3:4, :] * jnp.sin(ang)


def _rope_tables(pos, rows, tm=512):
    t = pos.shape[0]
    return pl.pallas_call(
        _rope_kernel,
        grid=(t // tm,),
        in_specs=[pl.BlockSpec((tm, 1), lambda i: (i, 0)), pl.BlockSpec((8, LANES), lambda i: (0, 0))],
        out_specs=[pl.BlockSpec((tm, LANES), lambda i: (i, 0))] * 2,
        out_shape=[jax.ShapeDtypeStruct((t, LANES), F32)] * 2,
        compiler_params=_cparams(("arbitrary",)),
        name="rope_tables",
    )(pos, rows)


def _rope_rows(dim, first_lane, pad_one_lanes):
    half = dim // 2
    inv = ROPE_THETA ** (-(np.arange(0, dim, 2, dtype=np.float32) / dim))
    rows = np.zeros((8, LANES), np.float32)
    lo = slice(first_lane, first_lane + half)
    hi = slice(first_lane + half, first_lane + dim)
    rows[0, lo] = inv
    rows[0, hi] = inv
    rows[1, lo] = 1.0
    rows[1, hi] = 1.0
    rows[2, :pad_one_lanes] = 1.0
    rows[3, lo] = -1.0
    rows[3, hi] = 1.0
    return jnp.asarray(rows)


def _proj_kernel(x_ref, w_ref, *out_refs, offsets):
    xb = x_ref[...].astype(BF16)
    for o_ref, (a, b) in zip(out_refs, offsets):
        o_ref[...] = _dot(xb, w_ref[:, a:b]).astype(o_ref.dtype)


def _proj(x, w, widths, dtypes, tm=256):
    t, k = x.shape
    offs = np.concatenate([[0], np.cumsum(widths)]).tolist()
    offsets = tuple((offs[i], offs[i + 1]) for i in range(len(widths)))
    return pl.pallas_call(
        functools.partial(_proj_kernel, offsets=offsets),
        grid=(t // tm,),
        in_specs=[pl.BlockSpec((tm, k), lambda i: (i, 0)), pl.BlockSpec(w.shape, lambda i: (0, 0))],
        out_specs=[pl.BlockSpec((tm, n), lambda i: (i, 0)) for n in widths],
        out_shape=[jax.ShapeDtypeStruct((t, n), dt) for n, dt in zip(widths, dtypes)],
        compiler_params=_cparams(("arbitrary",)),
        name="in_proj",
    )(x, w)


OD_SEG = dict(mq=(0, 512), mk=(512, 1024), mv=(1024, 1536), mo=(1536, 2048), gates=(2048, 2176),
              sq=(2176, 3200), sqsw=(3200, 4224), sk=(4224, 4480), sksw=(4480, 4736), sv=(4736, 5248))
OD_COLS = 5248


def _proj_odd_kernel(x_ref, w_ref, c_ref, s_ref, mq_ref, mk_ref, mv_ref, mo_ref, mg_ref, sq_ref, sk_ref, sv_ref):
    xb = x_ref[...].astype(BF16)

    def seg(name):
        a, b = OD_SEG[name]
        return _dot(xb, w_ref[:, a:b])

    mq_ref[...] = seg("mq")
    mk_ref[...] = seg("mk")
    mv_ref[...] = seg("mv")
    mo_ref[...] = seg("mo")
    mg_ref[...] = seg("gates")
    c = c_ref[...]
    s = s_ref[...]
    c8 = jnp.concatenate([c] * SWA_H, axis=1)
    s8 = jnp.concatenate([s] * SWA_H, axis=1)
    sq_ref[...] = (seg("sq") * c8 + seg("sqsw") * s8).astype(sq_ref.dtype)
    c2 = jnp.concatenate([c] * SWA_KV, axis=1)
    s2 = jnp.concatenate([s] * SWA_KV, axis=1)
    sk_ref[...] = (seg("sk") * c2 + seg("sksw") * s2).astype(sk_ref.dtype)
    sv_ref[...] = seg("sv").astype(sv_ref.dtype)


def _proj_odd(x, w, ctab, stab, tm=256):
    t, k = x.shape
    widths = (512, 512, 512, 512, 128, SWA_H * LANES, SWA_KV * LANES, 2 * SWA_KV * LANES)
    dtypes = (F32, F32, F32, F32, F32, BF16, BF16, BF16)
    return pl.pallas_call(
        _proj_odd_kernel,
        grid=(t // tm,),
        in_specs=[pl.BlockSpec((tm, k), lambda i: (i, 0)), pl.BlockSpec(w.shape, lambda i: (0, 0)),
                  pl.BlockSpec((tm, LANES), lambda i: (i, 0)), pl.BlockSpec((tm, LANES), lambda i: (i, 0))],
        out_specs=[pl.BlockSpec((tm, n), lambda i: (i, 0)) for n in widths],
        out_shape=[jax.ShapeDtypeStruct((t, n), dt) for n, dt in zip(widths, dtypes)],
        compiler_params=_cparams(("arbitrary",)),
        name="in_proj_odd",
    )(x, w, ctab, stab)


def _rms(x, g):
    return x * lax.rsqrt(jnp.mean(x * x, axis=-1, keepdims=True) + EPS) * g


def _mla_prep_kernel(in_ref, c_ref, s_ref, qn_ref, kvn_ref, wq_ref, wkv_ref, q_ref, k_ref, v_ref):
    hw = MLA_H * LANES
    c = c_ref[...]
    s = s_ref[...]
    c8 = jnp.concatenate([c] * MLA_H, axis=1)
    s8 = jnp.concatenate([s] * MLA_H, axis=1)
    cqn = _rms(in_ref[:, 0:Q_LORA], qn_ref[...]).astype(BF16)
    qq = _dot(cqn, wq_ref[...])
    scale = (MLA_NOPE + MLA_ROPE) ** -0.5
    q_ref[...] = ((qq[:, :hw] * c8 + qq[:, hw:] * s8) * scale).astype(q_ref.dtype)
    ckvn = _rms(in_ref[:, Q_LORA:Q_LORA + KV_LORA], kvn_ref[...]).astype(BF16)
    kv = _dot(ckvn, wkv_ref[...])
    o = Q_LORA + KV_LORA
    krr = in_ref[:, o:o + LANES] * c + in_ref[:, o + LANES:o + 2 * LANES] * s
    k_ref[...] = (kv[:, :hw] + jnp.concatenate([krr] * MLA_H, axis=1)).astype(k_ref.dtype)
    v_ref[...] = kv[:, hw:].astype(v_ref.dtype)


def _mla_prep(mla_in, ctab, stab, qn, kvn, wq2, wkv2, tm=256):
    t = mla_in.shape[0]
    hw = MLA_H * LANES
    row = lambda i: (i, 0)
    fix = lambda i: (0, 0)
    return pl.pallas_call(
        _mla_prep_kernel,
        grid=(t // tm,),
        in_specs=[pl.BlockSpec((tm, mla_in.shape[1]), row), pl.BlockSpec((tm, LANES), row), pl.BlockSpec((tm, LANES), row),
                  pl.BlockSpec(qn.shape, fix), pl.BlockSpec(kvn.shape, fix),
                  pl.BlockSpec(wq2.shape, fix), pl.BlockSpec(wkv2.shape, fix)],
        out_specs=[pl.BlockSpec((tm, hw), row)] * 3,
        out_shape=[jax.ShapeDtypeStruct((t, hw), BF16)] * 3,
        compiler_params=_cparams(("arbitrary",)),
        name="mla_prep",
    )(mla_in, ctab, stab, qn, kvn, wq2, wkv2)


def _mla_attn_kernel(q_ref, k_ref, v_ref, o_ref, *, tq):
    i = pl.program_id(2)
    neg = -1e30

    def chunk(j, carry, masked):
        start = pl.multiple_of(j * tq, tq)
        out = []
        for hh in range(2):
            m, l, acc = carry[hh]
            q = q_ref[:, hh * LANES:(hh + 1) * LANES]
            kc = k_ref[pl.ds(start, tq), hh * LANES:(hh + 1) * LANES]
            vc = v_ref[pl.ds(start, tq), hh * LANES:(hh + 1) * LANES]
            s = _dot_nt(q, kc)
            if masked:
                s = jnp.where(_iota2(s.shape, 0) >= _iota2(s.shape, 1), s, neg)
            m_new = jnp.maximum(m, jnp.max(s, axis=-1, keepdims=True))
            alpha = jnp.exp(m - m_new)
            p = jnp.exp(s - m_new)
            l = alpha * l + jnp.sum(p, axis=-1, keepdims=True)
            acc = alpha * acc + _dot(p.astype(BF16), vc)
            out.append((m_new, l, acc))
        return tuple(out)

    one = (jnp.full((tq, 1), neg, F32), jnp.zeros((tq, 1), F32), jnp.zeros((tq, LANES), F32))
    carry = lax.fori_loop(0, i, lambda j, c: chunk(j, c, False), (one, one))
    (_, l0, acc0), (_, l1, acc1) = chunk(i, carry, True)
    o_ref[...] = (acc0 / l0 + acc1 / l1).astype(o_ref.dtype)


def _mla_attn(q, k, v, batch, seq, tq=512):
    tq = min(tq, seq)
    nq = seq // tq
    pairs = MLA_H // 2
    return pl.pallas_call(
        functools.partial(_mla_attn_kernel, tq=tq),
        grid=(batch, pairs, nq),
        in_specs=[pl.BlockSpec((tq, 2 * LANES), lambda b, p, i: (b * nq + i, p)),
                  pl.BlockSpec((seq, 2 * LANES), lambda b, p, i: (b, p)),
                  pl.BlockSpec((seq, 2 * LANES), lambda b, p, i: (b, p))],
        out_specs=pl.BlockSpec((tq, LANES), lambda b, p, i: (b * nq + i, p)),
        out_shape=jax.ShapeDtypeStruct((batch * seq, pairs * LANES), BF16),
        compiler_params=_cparams(("arbitrary", "arbitrary", "arbitrary")),
        name="mla_attn",
    )(q, k, v)


def _unit_lower_inverse_many(ns):
    c = ns[0].shape[0]
    eye = (_iota2((c, c), 0) == _iota2((c, c), 1)).astype(F32)
    xs = [-n for n in ns]
    ps = [eye + x for x in xs]
    xsplit = [_split2(x) for x in xs]
    for _ in range(int(math.log2(c)) - 1):
        xs = [_dot(xh, xh) + (_dot(xh, xl) + _dot(xl, xh)) for xh, xl in xsplit]
        xsplit = [_split2(x) for x in xs]
        psplit = [_split2(p) for p in ps]
        ps = [p + (_dot(ph, xh) + (_dot(ph, xl) + _dot(plo, xh)))
              for p, (ph, plo), (xh, xl) in zip(ps, psplit, xsplit)]
    return ps


def _gdn_kernel(qkv_ref, g_ref, z_ref, cw_ref, al_ref, dt_ref, on_ref, o_ref, ext_ref, st_ref):
    c = GDN_CHUNK
    hd = GDN_DK
    nqk = GDN_H * GDN_DK

    @pl.when(pl.program_id(1) == 0)
    def _():
        ext_ref[:, 0:8, :] = jnp.zeros((ext_ref.shape[0], 8, ext_ref.shape[2]), F32)
        st_ref[...] = jnp.zeros(st_ref.shape, F32)

    tri = (_iota2((c, c), 0) >= _iota2((c, c), 1)).astype(F32)
    row_ge = _iota2((c, c), 0) >= _iota2((c, c), 1)
    row_gt = _iota2((c, c), 0) > _iota2((c, c), 1)
    ones = jnp.ones((c, LANES), F32)
    lane = _iota2((c, LANES), 1)

    units = []
    for bb in range(qkv_ref.shape[0]):
        ext = ext_ref.at[bb]
        ext[8:8 + c, :] = qkv_ref[bb]
        conv = cw_ref[0:1, :] * ext[5:5 + c, :]
        for j in range(1, CONV_W):
            conv = conv + cw_ref[j:j + 1, :] * ext[5 + j:5 + j + c, :]
        ext[0:8, :] = ext[c:c + 8, :]
        act = _silu(conv)
        gates = g_ref[bb]
        beta_all = _sigmoid(gates)
        g_all = -jnp.exp(al_ref[...]) * _softplus(gates + dt_ref[...])
        gc_all = _dot_sel(tri, g_all)
        for h in range(GDN_H):
            q = act[:, h * hd:(h + 1) * hd]
            k = act[:, nqk + h * hd:nqk + (h + 1) * hd]
            v = act[:, 2 * nqk + h * GDN_DV:2 * nqk + (h + 1) * GDN_DV]
            q = q * lax.rsqrt(jnp.sum(q * q, axis=-1, keepdims=True) + EPS) * (GDN_DK ** -0.5)
            k = k * lax.rsqrt(jnp.sum(k * k, axis=-1, keepdims=True) + EPS)
            beta = _lane_bcast(beta_all, h)
            gcol = _lane_bcast(gc_all, GDN_H + h)
            grow = _dot_sel(ones, jnp.where(lane == GDN_H + h, gc_all, 0.0), _dot_nt)
            decay = jnp.exp(jnp.where(row_ge, gcol[:, :c] - grow, -jnp.inf))
            kb = k * beta
            lower = jnp.where(row_gt, _dot3(kb, k, _dot_nt) * decay, 0.0)
            eg = jnp.exp(gcol)
            glast = gcol[c - 1:c, :]
            units.append(dict(bb=bb, h=h, lower=lower, rhs=jnp.concatenate([v * beta, kb * eg], axis=1),
                              attn=_dot_nt(q.astype(BF16), k.astype(BF16)) * decay, qg=(q * eg).astype(BF16),
                              kg=(k * jnp.exp(glast - gcol)).astype(BF16), gl=jnp.exp(glast)))

    tinvs = _unit_lower_inverse_many([u["lower"] for u in units])
    uws = []
    for u, tinv in zip(units, tinvs):
        th, tl = _split2(tinv)
        rh, rl = _split2(u["rhs"])
        uws.append(_dot(th, rh) + (_dot(th, rl) + _dot(tl, rh)))
    states = [st_ref[u["bb"], u["h"]] for u in units]
    sbs = [s.astype(BF16) for s in states]
    vnews = [(uw[:, :GDN_DV] - _dot(uw[:, GDN_DV:].astype(BF16), sb)).astype(BF16) for uw, sb in zip(uws, sbs)]
    for u, state, sb, vnb in zip(units, states, sbs, vnews):
        bb, h = u["bb"], u["h"]
        o = _dot(u["qg"], sb) + _dot(u["attn"].astype(BF16), vnb)
        st_ref[bb, h] = state * u["gl"] + _dot_tn(u["kg"], vnb)
        o = _rms(o, on_ref[...]) * _silu(z_ref[bb, :, h * GDN_DV:(h + 1) * GDN_DV])
        o_ref[bb, :, h * GDN_DV:(h + 1) * GDN_DV] = o.astype(o_ref.dtype)


def _gdn(qkv, gates, z, conv_w, a_row, dt_row, o_norm, batch, seq):
    c = GDN_CHUNK
    nc = seq // c
    w3 = qkv.shape[1]
    wo = GDN_H * GDN_DV
    nb = SEQS_PER_STEP
    row = lambda b, i: (b, i, 0)
    fix = lambda b, i: (0, 0)
    out = pl.pallas_call(
        _gdn_kernel,
        grid=(batch // nb, nc),
        in_specs=[pl.BlockSpec((nb, c, w3), row), pl.BlockSpec((nb, c, LANES), row), pl.BlockSpec((nb, c, wo), row),
                  pl.BlockSpec(conv_w.shape, fix), pl.BlockSpec((1, LANES), fix), pl.BlockSpec((1, LANES), fix),
                  pl.BlockSpec((1, GDN_DV), fix)],
        out_specs=pl.BlockSpec((nb, c, wo), row),
        out_shape=jax.ShapeDtypeStruct((batch, seq, wo), BF16),
        scratch_shapes=[pltpu.VMEM((nb, c + 8, w3), F32), pltpu.VMEM((nb, GDN_H, GDN_DK, GDN_DV), F32)],
        compiler_params=_cparams(("arbitrary", "arbitrary")),
        name="gdn",
    )(qkv.reshape(batch, seq, w3), gates.reshape(batch, seq, LANES), z.reshape(batch, seq, wo), conv_w, a_row, dt_row, o_norm)
    return out.reshape(batch * seq, wo)


def _mlstm_kernel(q_ref, k_ref, v_ref, og_ref, g_ref, bias_ref, nrm_ref, o_ref, c_ref, n_ref, m_ref):
    @pl.when(pl.program_id(1) == 0)
    def _():
        c_ref[...] = jnp.zeros(c_ref.shape, F32)
        n_ref[...] = jnp.zeros(n_ref.shape, F32)
        m_ref[...] = jnp.zeros(m_ref.shape, F32)

    c = ML_CHUNK
    tri = (_iota2((c, c), 0) >= _iota2((c, c), 1)).astype(F32)
    row_ge = _iota2((c, c), 0) >= _iota2((c, c), 1)
    ones = jnp.ones((c, LANES), F32)
    lane = _iota2((c, LANES), 1)

    units = []
    for bb in range(q_ref.shape[0]):
        pre = g_ref[bb] + bias_ref[...]
        logf = jnp.minimum(pre, 0.0) - jnp.log(1.0 + jnp.exp(-jnp.abs(pre)))
        bcum_all = _dot_sel(tri, logf)
        for h in range(ML_H):
            q = q_ref[bb, :, h * LANES:(h + 1) * LANES]
            k = k_ref[bb, :, h * LANES:(h + 1) * LANES] * (ML_DK ** -0.5)
            units.append(dict(bb=bb, h=h, q=q, k=k, qb=q.astype(BF16), vb=v_ref[bb, :, h * ML_DV:(h + 1) * ML_DV].astype(BF16),
                              bcol=_lane_bcast(bcum_all, ML_H + h),
                              icol=_lane_bcast(pre, h),
                              col=jnp.where(lane == h, pre, 0.0) - jnp.where(lane == ML_H + h, bcum_all, 0.0),
                              m_st=m_ref[bb, h], cst=c_ref[bb, h], nst=n_ref[bb, h]))
    for u in units:
        u["row"] = _dot_sel(ones, u["col"], _dot_nt)
        u["qk"] = _dot_nt(u["qb"], u["k"].astype(BF16))
        u["qc"] = _dot(u["qb"], u["cst"].astype(BF16))
    for u in units:
        d = jnp.where(row_ge, u["bcol"][:, :c] + u["row"], -jnp.inf)
        inter = u["bcol"] + u["m_st"]
        m_t = jnp.maximum(inter, jnp.max(d, axis=-1, keepdims=True))
        u["m_t"] = m_t
        u["w_inter"] = jnp.exp(inter - m_t)
        u["p"] = jnp.exp(d - m_t[:, :c]) * u["qk"]
        u["pv"] = _dot(u["p"].astype(BF16), u["vb"])
        b_end = u["bcol"][c - 1:c, :]
        a = b_end - u["bcol"] + u["icol"]
        m_new = jnp.maximum(b_end + u["m_st"], jnp.max(a, axis=0, keepdims=True))
        u["m_new"] = m_new
        u["keep"] = jnp.exp(b_end + u["m_st"] - m_new)
        u["ks"] = u["k"] * jnp.exp(a - m_new)
        u["kv"] = _dot_tn(u["ks"].astype(BF16), u["vb"])
    for u in units:
        bb, h = u["bb"], u["h"]
        num = u["w_inter"] * u["qc"] + u["pv"]
        den = (u["w_inter"] * jnp.sum(u["q"] * u["nst"], axis=-1, keepdims=True)
               + jnp.sum(u["p"], axis=-1, keepdims=True))
        hc = num / jnp.maximum(jnp.abs(den), jnp.exp(-u["m_t"]))
        c_ref[bb, h] = u["cst"] * u["keep"] + u["kv"]
        n_ref[bb, h] = u["nst"] * u["keep"] + jnp.sum(u["ks"], axis=0, keepdims=True)
        m_ref[bb, h] = u["m_new"]
        hn = (_rms(hc, nrm_ref[:, h * ML_DV:(h + 1) * ML_DV])
              * _sigmoid(og_ref[bb, :, h * ML_DV:(h + 1) * ML_DV]))
        o_ref[bb, :, h * ML_DV:(h + 1) * ML_DV] = hn.astype(o_ref.dtype)


def _mlstm(mq, mk, mv, mo, gates, bias_row, norm_row, batch, seq):
    c = ML_CHUNK
    nc = seq // c
    nb = SEQS_PER_STEP
    row = lambda b, i: (b, i, 0)
    fix = lambda b, i: (0, 0)
    wide = ML_H * LANES
    r3 = lambda a: a.reshape(batch, seq, a.shape[-1])
    out = pl.pallas_call(
        _mlstm_kernel,
        grid=(batch // nb, nc),
        in_specs=[pl.BlockSpec((nb, c, wide), row), pl.BlockSpec((nb, c, wide), row), pl.BlockSpec((nb, c, wide), row),
                  pl.BlockSpec((nb, c, wide), row), pl.BlockSpec((nb, c, LANES), row),
                  pl.BlockSpec((1, LANES), fix), pl.BlockSpec((1, wide), fix)],
        out_specs=pl.BlockSpec((nb, c, wide), row),
        out_shape=jax.ShapeDtypeStruct((batch, seq, wide), BF16),
        scratch_shapes=[pltpu.VMEM((nb, ML_H, LANES, ML_DV), F32), pltpu.VMEM((nb, ML_H, 1, LANES), F32),
                        pltpu.VMEM((nb, ML_H, 1, LANES), F32)],
        compiler_params=_cparams(("arbitrary", "arbitrary")),
        name="mlstm",
    )(r3(mq), r3(mk), r3(mv), r3(mo), r3(gates), bias_row, norm_row)
    return out.reshape(batch * seq, wide)


def _swa_kernel(q_ref, kc_ref, kp_ref, vc_ref, vp_ref, sink_ref, o_ref):
    w = WINDOW
    n = pl.program_id(1)
    scale = SWA_D ** -0.5
    qi = _iota2((w, w), 0)
    kj = _iota2((w, w), 1)
    mask_c = kj <= qi
    mask_p = jnp.logical_and(kj > qi, n > 0)
    grp = SWA_H // SWA_KV
    neg = -1e30
    scores = []
    for h in range(SWA_H):
        g = h // grp
        q = q_ref[:, h * LANES:(h + 1) * LANES]
        scores.append((_dot_nt(q, kc_ref[:, g * LANES:(g + 1) * LANES]), _dot_nt(q, kp_ref[:, g * LANES:(g + 1) * LANES])))
    probs = []
    for h, (sc, sp) in enumerate(scores):
        s_c = jnp.where(mask_c, sc * scale, neg)
        s_p = jnp.where(mask_p, sp * scale, neg)
        sink = sink_ref[:, h:h + 1]
        m = jnp.maximum(jnp.max(jnp.maximum(s_c, s_p), axis=-1, keepdims=True), sink)
        p_c = jnp.where(mask_c, jnp.exp(s_c - m), 0.0)
        p_p = jnp.where(mask_p, jnp.exp(s_p - m), 0.0)
        den = jnp.sum(p_c + p_p, axis=-1, keepdims=True) + jnp.exp(sink - m)
        inv = 1.0 / den
        probs.append(((p_c * inv).astype(BF16), (p_p * inv).astype(BF16)))
    for pair in range(SWA_H // 2):
        acc = None
        for sub in range(2):
            h = 2 * pair + sub
            vcol = (2 * (h // grp) + sub) * LANES
            p_c, p_p = probs[h]
            part = _dot(p_c, vc_ref[:, vcol:vcol + LANES]) + _dot(p_p, vp_ref[:, vcol:vcol + LANES])
            acc = part if acc is None else acc + part
        o_ref[:, pair * LANES:(pair + 1) * LANES] = acc.astype(o_ref.dtype)


def _swa(sq, sk, sv, sinks_row, batch, seq):
    w = WINDOW
    nb = seq // w
    cur = lambda b, n: (b * nb + n, 0)
    prev = lambda b, n: (b * nb + jnp.maximum(n - 1, 0), 0)
    return pl.pallas_call(
        _swa_kernel,
        grid=(batch, nb),
        in_specs=[pl.BlockSpec((w, sq.shape[1]), cur),
                  pl.BlockSpec((w, sk.shape[1]), cur), pl.BlockSpec((w, sk.shape[1]), prev),
                  pl.BlockSpec((w, sv.shape[1]), cur), pl.BlockSpec((w, sv.shape[1]), prev),
                  pl.BlockSpec((1, LANES), lambda b, n: (0, 0))],
        out_specs=pl.BlockSpec((w, SWA_H * SWA_D), cur),
        out_shape=jax.ShapeDtypeStruct((batch * seq, SWA_H * SWA_D), BF16),
        compiler_params=_cparams(("arbitrary", "arbitrary")),
        name="swa",
    )(sq, sk, sk, sv, sv, sinks_row)


def _layer_norm(h, g, b):
    mu = jnp.mean(h, axis=-1, keepdims=True)
    d = h - mu
    var = jnp.mean(d * d, axis=-1, keepdims=True)
    return d * lax.rsqrt(var + LN_EPS) * g + b


def _outproj_kernel(x_ref, a1_ref, a2_ref, w_ref, g_ref, b_ref, o_ref, op_ref):
    k1 = a1_ref.shape[1]
    y = _dot(a1_ref[...].astype(BF16), w_ref[0:k1, :]) + _dot(a2_ref[...].astype(BF16), w_ref[k1:, :])
    h = _layer_norm(DN_ALPHA * x_ref[...] + y, g_ref[...], b_ref[...])
    o_ref[...] = h
    op_ref[...] = _pack_pairs(h)


def _outproj_ln(x, a1, a2, w, g, b, tm=256):
    t, d = x.shape
    row = lambda i: (i, 0)
    fix = lambda i: (0, 0)
    return pl.pallas_call(
        _outproj_kernel,
        grid=(t // tm,),
        in_specs=[pl.BlockSpec((tm, d), row), pl.BlockSpec((tm, a1.shape[1]), row), pl.BlockSpec((tm, a2.shape[1]), row),
                  pl.BlockSpec(w.shape, fix), pl.BlockSpec((1, d), fix), pl.BlockSpec((1, d), fix)],
        out_specs=[pl.BlockSpec((tm, d), row), pl.BlockSpec((tm, d // 2), row)],
        out_shape=[jax.ShapeDtypeStruct((t, d), F32), jax.ShapeDtypeStruct((t, d // 2), jnp.uint32)],
        compiler_params=_cparams(("arbitrary",)),
        name="outproj_ln",
    )(x, a1, a2, w, g, b)


def _first_index(x, m, iota_f, sentinel):
    return jnp.min(jnp.where(x == m, iota_f, sentinel), axis=0, keepdims=True)


def _router_kernel(x_ref, wt_ref, bias_ref, idx_ref, gate_ref, rank_ref, cnt_ref, carry_ref):
    tm = x_ref.shape[0]
    e = N_EXPERTS
    gs = e // N_GROUPS
    ninf = -jnp.inf

    @pl.when(pl.program_id(0) == 0)
    def _():
        carry_ref[...] = jnp.zeros(carry_ref.shape, F32)

    logits = _dot_nt(wt_ref[...], x_ref[...], HI)
    scores = _sigmoid(logits)
    sel = scores + bias_ref[:, 0:1]

    sub_f = _iota2((gs, tm), 0).astype(F32)
    gscore = []
    for g in range(N_GROUPS):
        blk = sel[g * gs:(g + 1) * gs, :]
        m1 = jnp.max(blk, axis=0, keepdims=True)
        i1 = _first_index(blk, m1, sub_f, float(gs))
        m2 = jnp.max(jnp.where(sub_f == i1, ninf, blk), axis=0, keepdims=True)
        gscore.append(m1 + m2)
    gsc = jnp.concatenate(gscore, axis=0)
    grp_f = _iota2((N_GROUPS, tm), 0).astype(F32)
    gmask = jnp.zeros((N_GROUPS, tm), F32)
    for _ in range(TOPK_GROUPS):
        m = jnp.max(gsc, axis=0, keepdims=True)
        gi = _first_index(gsc, m, grp_f, float(N_GROUPS))
        hit = grp_f == gi
        gmask = jnp.where(hit, 1.0, gmask)
        gsc = jnp.where(hit, ninf, gsc)
    masked = jnp.concatenate(
        [jnp.where(gmask[g:g + 1, :] > 0.0, sel[g * gs:(g + 1) * gs, :], ninf) for g in range(N_GROUPS)], axis=0)

    exp_f = _iota2((e, tm), 0).astype(F32)
    chosen = jnp.zeros((e, tm), F32)
    idxs, gates = [], []
    for _ in range(TOP_K):
        m = jnp.max(masked, axis=0, keepdims=True)
        ei = _first_index(masked, m, exp_f, float(e))
        hit = exp_f == ei
        idxs.append(ei)
        gates.append(jnp.sum(jnp.where(hit, scores, 0.0), axis=0, keepdims=True))
        chosen = jnp.where(hit, 1.0, chosen)
        masked = jnp.where(hit, ninf, masked)
    gate = jnp.concatenate(gates, axis=0)
    gate = gate / jnp.sum(gate, axis=0, keepdims=True) * ROUTED_SCALE
    idx_f = jnp.concatenate(idxs, axis=0)

    upper = (_iota2((tm, tm), 0) < _iota2((tm, tm), 1)).astype(BF16)
    before = _dot(chosen.astype(BF16), upper) + carry_ref[...][:, 0:1]
    ranks = [jnp.sum(jnp.where(exp_f == idxs[k], before, 0.0), axis=0, keepdims=True) for k in range(TOP_K)]
    carry_ref[...] = carry_ref[...] + jnp.sum(chosen, axis=1, keepdims=True)

    idx_ref[...] = idx_f.astype(jnp.int32)
    gate_ref[...] = gate
    rank_ref[...] = jnp.concatenate(ranks, axis=0).astype(jnp.int32)
    cnt_ref[...] = carry_ref[...]


def _router(x, wt, bias_col, tm=512):
    t, d = x.shape
    col = lambda i: (0, i)
    fix = lambda i: (0, 0)
    return pl.pallas_call(
        _router_kernel,
        grid=(t // tm,),
        in_specs=[pl.BlockSpec((tm, d), lambda i: (i, 0)), pl.BlockSpec(wt.shape, fix), pl.BlockSpec((N_EXPERTS, LANES), fix)],
        out_specs=[pl.BlockSpec((TOP_K, tm), col), pl.BlockSpec((TOP_K, tm), col), pl.BlockSpec((TOP_K, tm), col),
                   pl.BlockSpec((N_EXPERTS, LANES), fix)],
        out_shape=[jax.ShapeDtypeStruct((TOP_K, t), jnp.int32), jax.ShapeDtypeStruct((TOP_K, t), F32),
                   jax.ShapeDtypeStruct((TOP_K, t), jnp.int32), jax.ShapeDtypeStruct((N_EXPERTS, LANES), F32)],
        scratch_shapes=[pltpu.VMEM((N_EXPERTS, LANES), F32)],
        compiler_params=_cparams(("arbitrary",)),
        name="router",
    )(x, wt, bias_col)


def _dest_kernel(idx_ref, rank_ref, start_ref, dest_ref):
    tm = idx_ref.shape[1]
    exp_i = _iota2((N_EXPERTS, tm), 0)
    start = start_ref[:, 0:1]
    rows = [jnp.sum(jnp.where(exp_i == idx_ref[s:s + 1, :], start, 0.0), axis=0, keepdims=True) for s in range(TOP_K)]
    dest_ref[...] = jnp.concatenate(rows, axis=0).astype(jnp.int32) + rank_ref[...]


def _dest_rows(idx, rank, start_col, tm=2048):
    t = idx.shape[1]
    tm = min(tm, t)
    col = lambda i: (0, i)
    return pl.pallas_call(
        _dest_kernel,
        grid=(t // tm,),
        in_specs=[pl.BlockSpec((TOP_K, tm), col), pl.BlockSpec((TOP_K, tm), col),
                  pl.BlockSpec((N_EXPERTS, LANES), lambda i: (0, 0))],
        out_specs=pl.BlockSpec((TOP_K, tm), col),
        out_shape=jax.ShapeDtypeStruct((TOP_K, t), jnp.int32),
        compiler_params=_cparams(("arbitrary",)),
        name="moe_dest",
    )(idx, rank, start_col)


def _pack_pairs(x):
    n = x.shape[1] // 2
    hi = lax.bitcast_convert_type(x[:, :n].astype(BF16).astype(F32), jnp.uint32)
    lo = lax.bitcast_convert_type(x[:, n:].astype(BF16).astype(F32), jnp.uint32)
    return hi | (lo >> 16)


def _unpack_pairs(w):
    hi = lax.bitcast_convert_type(w & jnp.uint32(0xFFFF0000), F32)
    lo = lax.bitcast_convert_type(w << 16, F32)
    return hi, lo


def _sc_scatter_rows(xp, dest, rows, chunk=LANES):
    t, width = xp.shape
    info = plsc.get_sparse_core_info()
    ncores, nsub = info.num_cores, info.num_subcores
    per_worker = t // (ncores * nsub)
    nchunk = per_worker // chunk
    mesh = plsc.VectorSubcoreMesh(core_axis_name="c", subcore_axis_name="s")

    @functools.partial(
        pl.kernel, mesh=mesh,
        out_type=jax.ShapeDtypeStruct((rows, width), xp.dtype),
        scratch_types=[pltpu.VMEM((TOP_K, chunk), jnp.int32), pltpu.VMEM((chunk, width), xp.dtype), pltpu.SemaphoreType.DMA],
    )
    def scatter(xp_hbm, dest_hbm, out_hbm, idx_v, rows_v, sem):
        base = (lax.axis_index("s") * ncores + lax.axis_index("c")) * per_worker

        @pl.loop(0, nchunk)
        def _(i):
            off = pl.multiple_of(base + i * chunk, chunk)
            pltpu.sync_copy(dest_hbm.at[:, pl.ds(off, chunk)], idx_v)
            pltpu.sync_copy(xp_hbm.at[pl.ds(off, chunk)], rows_v)
            copies = [pltpu.async_copy(rows_v, out_hbm.at[idx_v.at[s]], sem) for s in range(TOP_K)]
            for cp in copies:
                cp.wait()

    return scatter(xp, dest)


def _experts_kernel(be_ref, nu_ref, nv_ref, xs_ref, wg_ref, wu_ref, wd_ref, ys_ref):
    i = pl.program_id(0)

    @pl.when(i < nu_ref[0])
    def _():
        half = xs_ref.shape[1]
        live = _iota2((xs_ref.shape[0], 1), 0) < nv_ref[i]
        xa, xb = _unpack_pairs(jnp.where(live, xs_ref[...], jnp.uint32(0)))
        xa = xa.astype(BF16)
        xb = xb.astype(BF16)
        gate = _dot(xa, wg_ref[0, :half, :]) + _dot(xb, wg_ref[0, half:, :])
        up = _dot(xa, wu_ref[0, :half, :]) + _dot(xb, wu_ref[0, half:, :])
        h = _silu(gate) * up
        ys_ref[...] = _pack_pairs(_dot(h.astype(BF16), wd_ref[0]))

    @pl.when(i >= nu_ref[0])
    def _():
        ys_ref[...] = jnp.zeros(ys_ref.shape, ys_ref.dtype)


def _experts(block_e, n_used, n_valid, xs, wg, wu, wd):
    rows, half = xs.shape
    d = 2 * half
    nb = rows // EXPERT_BLOCK
    blk = lambda i, be, nu, nv: (jnp.minimum(i, nu[0] - 1), 0)
    wsel = lambda i, be, nu, nv: (be[i], 0, 0)
    return pl.pallas_call(
        _experts_kernel,
        grid_spec=pltpu.PrefetchScalarGridSpec(
            num_scalar_prefetch=3,
            grid=(nb,),
            in_specs=[pl.BlockSpec((EXPERT_BLOCK, half), blk),
                      pl.BlockSpec((1, d, D_EXPERT), wsel), pl.BlockSpec((1, d, D_EXPERT), wsel),
                      pl.BlockSpec((1, D_EXPERT, d), wsel)],
            out_specs=pl.BlockSpec((EXPERT_BLOCK, half), lambda i, be, nu, nv: (i, 0)),
        ),
        out_shape=jax.ShapeDtypeStruct((rows, half), jnp.uint32),
        compiler_params=_cparams(("arbitrary",)),
        name="moe_experts",
    )(block_e, n_used, n_valid, xs, wg, wu, wd)


def _sc_gather_rows(table, idx, chunk=SC_CHUNK):
    n = idx.shape[0]
    width = table.shape[1]
    info = plsc.get_sparse_core_info()
    ncores, nsub = info.num_cores, info.num_subcores
    per_worker = n // (ncores * nsub)
    nchunk = per_worker // chunk
    mesh = plsc.VectorSubcoreMesh(core_axis_name="c", subcore_axis_name="s")

    @functools.partial(
        pl.kernel, mesh=mesh,
        out_type=jax.ShapeDtypeStruct((n, width), table.dtype),
        scratch_types=[pltpu.VMEM((chunk,), jnp.int32), pltpu.VMEM((chunk, width), table.dtype), pltpu.SemaphoreType.DMA],
    )
    def gather(table_hbm, idx_hbm, out_hbm, idx_v, rows_v, sem):
        base = (lax.axis_index("s") * ncores + lax.axis_index("c")) * per_worker

        @pl.loop(0, nchunk)
        def _(i):
            off = pl.multiple_of(base + i * chunk, chunk)
            pltpu.sync_copy(idx_hbm.at[pl.ds(off, chunk)], idx_v)
            pltpu.async_copy(table_hbm.at[idx_v], rows_v, sem).wait()
            pltpu.sync_copy(rows_v, out_hbm.at[pl.ds(off, chunk)])

    return gather(table, idx)


def _combine_kernel(x_ref, gate_ref, rows_ref, sg_ref, su_ref, sd_ref, g_ref, b_ref, o_ref):
    x = x_ref[...]
    xb = x.astype(BF16)
    hs = _silu(_dot(xb, sg_ref[...])) * _dot(xb, su_ref[...])
    ff = _dot(hs.astype(BF16), sd_ref[...])
    gate = gate_ref[...]
    half = rows_ref.shape[2]
    ya = ff[:, :half]
    yb = ff[:, half:]
    for s in range(TOP_K):
        a, b = _unpack_pairs(rows_ref[s])
        ya = ya + gate[:, s:s + 1] * a
        yb = yb + gate[:, s:s + 1] * b
    ff = jnp.concatenate([ya, yb], axis=1)
    o_ref[...] = _layer_norm(DN_ALPHA * x + ff, g_ref[...], b_ref[...])


def _combine(x, gate_t, rows, sg, su, sd, g, b, tm=256):
    t, d = x.shape
    row = lambda i: (i, 0)
    fix = lambda i: (0, 0)
    return pl.pallas_call(
        _combine_kernel,
        grid=(t // tm,),
        in_specs=[pl.BlockSpec((tm, d), row), pl.BlockSpec((tm, TOP_K), row),
                  pl.BlockSpec((TOP_K, tm, d // 2), lambda i: (0, i, 0)),
                  pl.BlockSpec(sg.shape, fix), pl.BlockSpec(su.shape, fix), pl.BlockSpec(sd.shape, fix),
                  pl.BlockSpec((1, d), fix), pl.BlockSpec((1, d), fix)],
        out_specs=pl.BlockSpec((tm, d), row),
        out_shape=jax.ShapeDtypeStruct((t, d), F32),
        compiler_params=_cparams(("arbitrary",)),
        name="moe_combine",
    )(x, gate_t, rows, sg, su, sd, g, b)


def _take_cols(w, idx):
    wz = jnp.concatenate([w, jnp.zeros((w.shape[0], 1), w.dtype)], axis=1)
    idx = np.where(np.asarray(idx) < 0, w.shape[1], np.asarray(idx))
    return jnp.take(wz, jnp.asarray(idx, jnp.int32), axis=1)


def _pad_lane_row(v, first_lane, width=LANES):
    out = jnp.zeros((1, width), F32)
    return lax.dynamic_update_slice(out, v.reshape(1, -1).astype(F32), (0, first_lane))


def _even_in_cols():
    z = lambda n: -np.ones(n, int)
    kr0 = Q_LORA + KV_LORA
    half = MLA_ROPE // 2
    cols = [np.arange(0, Q_LORA), np.arange(Q_LORA, Q_LORA + KV_LORA),
            z(64), np.arange(kr0, kr0 + MLA_ROPE), z(32),
            z(64), np.arange(kr0 + half, kr0 + MLA_ROPE), np.arange(kr0, kr0 + half), z(32)]
    g0 = kr0 + MLA_ROPE
    nqk = GDN_H * GDN_DK
    cols.append(np.arange(g0, g0 + 3 * nqk))
    zoff = g0 + 3 * nqk + 2 * GDN_H
    cols.append(np.arange(zoff, zoff + GDN_H * GDN_DV))
    cols += [np.arange(g0 + 3 * nqk, g0 + 3 * nqk + 2 * GDN_H), z(LANES - 2 * GDN_H)]
    return np.concatenate(cols)


EV_WIDTHS = (Q_LORA + KV_LORA + 2 * LANES, 3 * GDN_H * GDN_DK, GDN_H * GDN_DV, LANES)


def _mla_q_cols():
    per = MLA_NOPE + MLA_ROPE
    half = MLA_ROPE // 2
    main, sw = [], []
    for h in range(MLA_H):
        b = h * per
        main += [np.arange(b, b + per), -np.ones(LANES - per, int)]
        sw += [-np.ones(MLA_NOPE, int), np.arange(b + MLA_NOPE + half, b + per), np.arange(b + MLA_NOPE, b + MLA_NOPE + half),
               -np.ones(LANES - per, int)]
    return np.concatenate(main + sw)


def _mla_kv_cols():
    per = MLA_NOPE + MLA_V
    kc, vc = [], []
    for h in range(MLA_H):
        b = h * per
        kc += [np.arange(b, b + MLA_NOPE), -np.ones(LANES - MLA_NOPE, int)]
        vv = np.arange(b + MLA_NOPE, b + per)
        pad = -np.ones(LANES - MLA_V, int)
        vc += [vv, pad] if h % 2 == 0 else [pad, vv]
    return np.concatenate(kc + vc)


def _odd_in_cols():
    z = lambda n: -np.ones(n, int)
    o = 0
    cols = []
    mq0, mk0 = 0, ML_H * ML_DK
    for base in (mq0, mk0):
        for h in range(ML_H):
            cols += [np.arange(base + h * ML_DK, base + (h + 1) * ML_DK), z(LANES - ML_DK)]
    mv0 = 2 * ML_H * ML_DK
    cols.append(np.arange(mv0, mv0 + ML_H * ML_DV))
    mi0 = mv0 + ML_H * ML_DV
    mo0 = mi0 + 2 * ML_H
    cols.append(np.arange(mo0, mo0 + ML_H * ML_DV))
    cols += [np.arange(mi0, mi0 + 2 * ML_H), z(LANES - 2 * ML_H)]
    sq0 = mo0 + ML_H * ML_DV
    sk0 = sq0 + SWA_H * SWA_D
    sv0 = sk0 + SWA_KV * SWA_D
    half = SWA_D // 2

    def heads(base, n, swapped):
        out = []
        for h in range(n):
            b = base + h * SWA_D
            if swapped:
                out += [np.arange(b + half, b + SWA_D), np.arange(b, b + half), z(LANES - SWA_D)]
            else:
                out += [np.arange(b, b + SWA_D), z(LANES - SWA_D)]
        return out

    cols += heads(sq0, SWA_H, False) + heads(sq0, SWA_H, True) + heads(sk0, SWA_KV, False) + heads(sk0, SWA_KV, True)
    for g in range(SWA_KV):
        vv = np.arange(sv0 + g * SWA_D, sv0 + (g + 1) * SWA_D)
        cols += [vv, z(LANES - SWA_D), z(LANES - SWA_D), vv]
    return np.concatenate(cols)


def _even_mixer(x, tabs, w_in, q_norm, w_qb, kv_norm, w_kvb, conv_w, a_log, dt_bias, o_norm, batch, seq):
    ctab, stab = tabs
    w = _take_cols(w_in, _even_in_cols()).astype(BF16)
    mla_in, qkv, z, gates = _proj(x, w, EV_WIDTHS, (F32, F32, F32, F32))
    wq2 = _take_cols(w_qb, _mla_q_cols()).astype(BF16)
    wkv2 = _take_cols(w_kvb, _mla_kv_cols()).astype(BF16)
    q, k, v = _mla_prep(mla_in, ctab, stab, q_norm.reshape(1, -1), kv_norm.reshape(1, -1), wq2, wkv2)
    o_a = _mla_attn(q, k, v, batch, seq)
    o_b = _gdn(qkv, gates, z, conv_w, _pad_lane_row(a_log, GDN_H), _pad_lane_row(dt_bias, GDN_H),
               o_norm.reshape(1, -1), batch, seq)
    return o_a, o_b


def _odd_mixer(x, tabs, w_in, b_i, b_f, ml_norm, sinks, batch, seq):
    ctab, stab = tabs
    w = _take_cols(w_in, _odd_in_cols()).astype(BF16)
    mq, mk, mv, mo, mg, sq, sk, sv = _proj_odd(x, w, ctab, stab)
    bias_row = _pad_lane_row(jnp.concatenate([b_i, b_f]), 0)
    o_c = _mlstm(mq, mk, mv, mo, mg, bias_row, ml_norm.reshape(1, -1), batch, seq)
    o_d = _swa(sq, sk, sv, _pad_lane_row(sinks, 0), batch, seq)
    return o_c, o_d


def _moe(x, xp, router_w, router_b, w_gate, w_up, w_down, s_gate, s_up, s_down, ln_g, ln_b):
    t, d = x.shape
    bias_col = jnp.broadcast_to(router_b.reshape(-1, 1).astype(F32), (N_EXPERTS, LANES))
    idx, gate, rank, cnt = _router(x, router_w.T, bias_col)
    counts = cnt[:, 0].astype(jnp.int32)
    padded = (counts + EXPERT_BLOCK - 1) // EXPERT_BLOCK * EXPERT_BLOCK
    pad_end = jnp.cumsum(padded)
    pad_start = pad_end - padded
    start_col = jnp.broadcast_to(pad_start.astype(F32).reshape(-1, 1), (N_EXPERTS, LANES))
    dest = _dest_rows(idx, rank, start_col)
    n_blocks = t * TOP_K // EXPERT_BLOCK + N_EXPERTS
    rows = n_blocks * EXPERT_BLOCK
    block_row = jnp.arange(n_blocks, dtype=jnp.int32) * EXPERT_BLOCK
    block_e = jnp.minimum(jnp.sum((pad_end[None, :] <= block_row[:, None]).astype(jnp.int32), axis=1), N_EXPERTS - 1)
    n_used = (pad_end[-1:] // EXPERT_BLOCK).astype(jnp.int32)
    live_end = jnp.sum(jnp.where(block_e[:, None] == jnp.arange(N_EXPERTS, dtype=jnp.int32)[None, :],
                                 (pad_start + counts)[None, :], 0), axis=1)
    n_valid = jnp.clip(live_end - block_row, 0, EXPERT_BLOCK).astype(jnp.int32)
    xs = _sc_scatter_rows(xp, dest, rows)
    ys = _experts(block_e, n_used, n_valid, xs, w_gate.astype(BF16), w_up.astype(BF16), w_down.astype(BF16))
    picked = _sc_gather_rows(ys, dest.reshape(-1)).reshape(TOP_K, t, d // 2)
    return _combine(x, gate.T, picked, s_gate.astype(BF16), s_up.astype(BF16), s_down.astype(BF16),
                    ln_g.reshape(1, -1), ln_b.reshape(1, -1))


def kernel(x, positions, ev_w_in, mla_q_norm, mla_w_qb, mla_kv_norm, mla_w_kvb, gdn_conv, gdn_a_log, gdn_dt_bias, gdn_norm, ev_w_out, od_w_in, mlstm_b_i, mlstm_b_f, mlstm_norm, swa_sinks, od_w_out, ln1_g, ln1_b, router_w, router_b, moe_w_gate, moe_w_up, moe_w_down, shared_w_gate, shared_w_up, shared_w_down, ln2_g, ln2_b):
    batch, seq, d = x.shape
    t = batch * seq
    pos = positions.reshape(t, 1).astype(F32)
    tabs_m = _rope_tables(pos, _rope_rows(MLA_ROPE, MLA_NOPE, MLA_NOPE))
    tabs_s = _rope_tables(pos, _rope_rows(SWA_D, 0, 0))
    h = x.reshape(t, d)
    for layer in range(DEPTH):
        j = layer // 2
        if layer % 2 == 0:
            a1, a2 = _even_mixer(h, tabs_m, ev_w_in[j], mla_q_norm[j], mla_w_qb[j], mla_kv_norm[j], mla_w_kvb[j],
                                 gdn_conv[j], gdn_a_log[j], gdn_dt_bias[j], gdn_norm[j], batch, seq)
            w_out = ev_w_out[j]
        else:
            a1, a2 = _odd_mixer(h, tabs_s, od_w_in[j], mlstm_b_i[j], mlstm_b_f[j], mlstm_norm[j], swa_sinks[j], batch, seq)
            w_out = od_w_out[j]
        h, hp = _outproj_ln(h, a1, a2, w_out.astype(BF16), ln1_g[layer].reshape(1, -1), ln1_b[layer].reshape(1, -1))
        h = _moe(h, hp, router_w[layer], router_b[layer], moe_w_gate[layer], moe_w_up[layer], moe_w_down[layer],
                 shared_w_gate[layer], shared_w_up[layer], shared_w_down[layer], ln2_g[layer], ln2_b[layer])
    return h.reshape(batch, seq, d)
```

```python
import functools
import math

import numpy as np
import jax
import jax.numpy as jnp
from jax import lax
from jax.experimental import pallas as pl
from jax.experimental.pallas import tpu as pltpu
from jax.experimental.pallas import tpu_sc as plsc

F32 = jnp.float32
BF16 = jnp.bfloat16
HI = lax.Precision.HIGHEST

D_MODEL = 1024
DEPTH = 4
ROPE_THETA = 10000.0
EPS = 1e-6
LN_EPS = 1e-5
MLA_H, MLA_NOPE, MLA_ROPE, MLA_V = 8, 64, 32, 64
Q_LORA, KV_LORA = 256, 128
GDN_H, GDN_DK, GDN_DV, CONV_W, GDN_CHUNK = 4, 128, 128, 4, 64
ML_H, ML_DK, ML_DV, ML_CHUNK = 4, 64, 128, 64
SWA_H, SWA_KV, SWA_D, WINDOW = 8, 2, 64, 128
N_EXPERTS, N_GROUPS, TOPK_GROUPS, TOP_K = 64, 8, 4, 8
D_EXPERT, D_SHARED = 256, 256
ROUTED_SCALE = 2.5
DN_ALPHA = (2 * DEPTH) ** 0.25

LANES = 128
V7X_VMEM_BYTES = 64 * 1024 * 1024
VMEM_LIMIT = 48 * 1024 * 1024

EXPERT_BLOCK = 512
SEQS_PER_STEP = 2
DMA_GROUP = 2
SC_CHUNK = 64


def _cparams(sem, vmem=VMEM_LIMIT):
    return pltpu.CompilerParams(dimension_semantics=sem, vmem_limit_bytes=vmem)


def _dot(a, b, precision=None):
    return jnp.dot(a, b, preferred_element_type=F32, precision=precision)


def _dot_nt(a, b, precision=None):
    return lax.dot_general(a, b, (((1,), (1,)), ((), ())), preferred_element_type=F32, precision=precision)


def _dot_tn(a, b, precision=None):
    return lax.dot_general(a, b, (((0,), (0,)), ((), ())), preferred_element_type=F32, precision=precision)


def _split2(a):
    hi = a.astype(BF16)
    lo = (a - hi.astype(F32)).astype(BF16)
    return hi, lo


def _split3(a):
    p1 = a.astype(BF16)
    r = a - p1.astype(F32)
    p2 = r.astype(BF16)
    p3 = (r - p2.astype(F32)).astype(BF16)
    return p1, p2, p3


def _dot3(a, b, dot=_dot):
    ah, al = _split2(a)
    bh, bl = _split2(b)
    return dot(ah, bh) + (dot(ah, bl) + dot(al, bh))


def _dot_sel(sel, b, dot=_dot):
    sel = sel.astype(BF16)
    p1, p2, p3 = _split3(b)
    return dot(sel, p1) + (dot(sel, p2) + dot(sel, p3))


def _sigmoid(x):
    return 1.0 / (1.0 + jnp.exp(-x))


def _softplus(x):
    return jnp.maximum(x, 0.0) + jnp.log(1.0 + jnp.exp(-jnp.abs(x)))


def _silu(x):
    return x * _sigmoid(x)


def _lane_bcast(x, c):
    return jnp.broadcast_to(x[:, c:c + 1], x.shape)


def _iota2(shape, dim):
    return lax.broadcasted_iota(jnp.int32, shape, dim)


def _rope_kernel(pos_ref, rows_ref, c_ref, s_ref):
    ang = pos_ref[...] * rows_ref[0:1, :]
    c_ref[...] = rows_ref[1:2, :] * jnp.cos(ang) + rows_ref[2:3, :]
    s_ref[...] = rows_ref[3:4, :] * jnp.sin(ang)


def _rope_tables(pos, rows, tm=512):
    t = pos.shape[0]
    return pl.pallas_call(
        _rope_kernel,
        grid=(t // tm,),
        in_specs=[pl.BlockSpec((tm, 1), lambda i: (i, 0)), pl.BlockSpec((8, LANES), lambda i: (0, 0))],
        out_specs=[pl.BlockSpec((tm, LANES), lambda i: (i, 0))] * 2,
        out_shape=[jax.ShapeDtypeStruct((t, LANES), F32)] * 2,
        compiler_params=_cparams(("arbitrary",)),
        name="rope_tables",
    )(pos, rows)


def _rope_rows(dim, first_lane, pad_one_lanes):
    half = dim // 2
    inv = ROPE_THETA ** (-(np.arange(0, dim, 2, dtype=np.float32) / dim))
    rows = np.zeros((8, LANES), np.float32)
    lo = slice(first_lane, first_lane + half)
    hi = slice(first_lane + half, first_lane + dim)
    rows[0, lo] = inv
    rows[0, hi] = inv
    rows[1, lo] = 1.0
    rows[1, hi] = 1.0
    rows[2, :pad_one_lanes] = 1.0
    rows[3, lo] = -1.0
    rows[3, hi] = 1.0
    return jnp.asarray(rows)


def _proj_kernel(x_ref, w_ref, *out_refs, offsets):
    xb = x_ref[...].astype(BF16)
    for o_ref, (a, b) in zip(out_refs, offsets):
        o_ref[...] = _dot(xb, w_ref[:, a:b]).astype(o_ref.dtype)


def _proj(x, w, widths, dtypes, tm=256):
    t, k = x.shape
    offs = np.concatenate([[0], np.cumsum(widths)]).tolist()
    offsets = tuple((offs[i], offs[i + 1]) for i in range(len(widths)))
    return pl.pallas_call(
        functools.partial(_proj_kernel, offsets=offsets),
        grid=(t // tm,),
        in_specs=[pl.BlockSpec((tm, k), lambda i: (i, 0)), pl.BlockSpec(w.shape, lambda i: (0, 0))],
        out_specs=[pl.BlockSpec((tm, n), lambda i: (i, 0)) for n in widths],
        out_shape=[jax.ShapeDtypeStruct((t, n), dt) for n, dt in zip(widths, dtypes)],
        compiler_params=_cparams(("arbitrary",)),
        name="in_proj",
    )(x, w)


OD_SEG = dict(mq=(0, 512), mk=(512, 1024), mv=(1024, 1536), mo=(1536, 2048), gates=(2048, 2176),
              sq=(2176, 3200), sqsw=(3200, 4224), sk=(4224, 4480), sksw=(4480, 4736), sv=(4736, 5248))
OD_COLS = 5248


def _proj_odd_kernel(x_ref, w_ref, c_ref, s_ref, mq_ref, mk_ref, mv_ref, mo_ref, mg_ref, sq_ref, sk_ref, sv_ref):
    xb = x_ref[...].astype(BF16)

    def seg(name):
        a, b = OD_SEG[name]
        return _dot(xb, w_ref[:, a:b])

    mq_ref[...] = seg("mq")
    mk_ref[...] = seg("mk")
    mv_ref[...] = seg("mv")
    mo_ref[...] = seg("mo")
    mg_ref[...] = seg("gates")
    c = c_ref[...]
    s = s_ref[...]
    c8 = jnp.concatenate([c] * SWA_H, axis=1)
    s8 = jnp.concatenate([s] * SWA_H, axis=1)
    sq_ref[...] = (seg("sq") * c8 + seg("sqsw") * s8).astype(sq_ref.dtype)
    c2 = jnp.concatenate([c] * SWA_KV, axis=1)
    s2 = jnp.concatenate([s] * SWA_KV, axis=1)
    sk_ref[...] = (seg("sk") * c2 + seg("sksw") * s2).astype(sk_ref.dtype)
    sv_ref[...] = seg("sv").astype(sv_ref.dtype)


def _proj_odd(x, w, ctab, stab, tm=256):
    t, k = x.shape
    widths = (512, 512, 512, 512, 128, SWA_H * LANES, SWA_KV * LANES, 2 * SWA_KV * LANES)
    dtypes = (F32, F32, F32, F32, F32, BF16, BF16, BF16)
    return pl.pallas_call(
        _proj_odd_kernel,
        grid=(t // tm,),
        in_specs=[pl.BlockSpec((tm, k), lambda i: (i, 0)), pl.BlockSpec(w.shape, lambda i: (0, 0)),
                  pl.BlockSpec((tm, LANES), lambda i: (i, 0)), pl.BlockSpec((tm, LANES), lambda i: (i, 0))],
        out_specs=[pl.BlockSpec((tm, n), lambda i: (i, 0)) for n in widths],
        out_shape=[jax.ShapeDtypeStruct((t, n), dt) for n, dt in zip(widths, dtypes)],
        compiler_params=_cparams(("arbitrary",)),
        name="in_proj_odd",
    )(x, w, ctab, stab)


def _rms(x, g):
    return x * lax.rsqrt(jnp.mean(x * x, axis=-1, keepdims=True) + EPS) * g


def _mla_prep_kernel(in_ref, c_ref, s_ref, qn_ref, kvn_ref, wq_ref, wkv_ref, q_ref, k_ref, v_ref):
    hw = MLA_H * LANES
    c = c_ref[...]
    s = s_ref[...]
    c8 = jnp.concatenate([c] * MLA_H, axis=1)
    s8 = jnp.concatenate([s] * MLA_H, axis=1)
    cqn = _rms(in_ref[:, 0:Q_LORA], qn_ref[...]).astype(BF16)
    qq = _dot(cqn, wq_ref[...])
    scale = (MLA_NOPE + MLA_ROPE) ** -0.5
    q_ref[...] = ((qq[:, :hw] * c8 + qq[:, hw:] * s8) * scale).astype(q_ref.dtype)
    ckvn = _rms(in_ref[:, Q_LORA:Q_LORA + KV_LORA], kvn_ref[...]).astype(BF16)
    kv = _dot(ckvn, wkv_ref[...])
    o = Q_LORA + KV_LORA
    krr = in_ref[:, o:o + LANES] * c + in_ref[:, o + LANES:o + 2 * LANES] * s
    k_ref[...] = (kv[:, :hw] + jnp.concatenate([krr] * MLA_H, axis=1)).astype(k_ref.dtype)
    v_ref[...] = kv[:, hw:].astype(v_ref.dtype)


def _mla_prep(mla_in, ctab, stab, qn, kvn, wq2, wkv2, tm=256):
    t = mla_in.shape[0]
    hw = MLA_H * LANES
    row = lambda i: (i, 0)
    fix = lambda i: (0, 0)
    return pl.pallas_call(
        _mla_prep_kernel,
        grid=(t // tm,),
        in_specs=[pl.BlockSpec((tm, mla_in.shape[1]), row), pl.BlockSpec((tm, LANES), row), pl.BlockSpec((tm, LANES), row),
                  pl.BlockSpec(qn.shape, fix), pl.BlockSpec(kvn.shape, fix),
                  pl.BlockSpec(wq2.shape, fix), pl.BlockSpec(wkv2.shape, fix)],
        out_specs=[pl.BlockSpec((tm, hw), row)] * 3,
        out_shape=[jax.ShapeDtypeStruct((t, hw), BF16)] * 3,
        compiler_params=_cparams(("arbitrary",)),
        name="mla_prep",
    )(mla_in, ctab, stab, qn, kvn, wq2, wkv2)


def _mla_attn_kernel(q_ref, k_ref, v_ref, o_ref, *, tq):
    i = pl.program_id(2)
    neg = -1e30

    def chunk(j, carry, masked):
        start = pl.multiple_of(j * tq, tq)
        out = []
        for hh in range(2):
            m, l, acc = carry[hh]
            q = q_ref[:, hh * LANES:(hh + 1) * LANES]
            kc = k_ref[pl.ds(start, tq), hh * LANES:(hh + 1) * LANES]
            vc = v_ref[pl.ds(start, tq), hh * LANES:(hh + 1) * LANES]
            s = _dot_nt(q, kc)
            if masked:
                s = jnp.where(_iota2(s.shape, 0) >= _iota2(s.shape, 1), s, neg)
            m_new = jnp.maximum(m, jnp.max(s, axis=-1, keepdims=True))
            alpha = jnp.exp(m - m_new)
            p = jnp.exp(s - m_new)
            l = alpha * l + jnp.sum(p, axis=-1, keepdims=True)
            acc = alpha * acc + _dot(p.astype(BF16), vc)
            out.append((m_new, l, acc))
        return tuple(out)

    one = (jnp.full((tq, 1), neg, F32), jnp.zeros((tq, 1), F32), jnp.zeros((tq, LANES), F32))
    carry = lax.fori_loop(0, i, lambda j, c: chunk(j, c, False), (one, one))
    (_, l0, acc0), (_, l1, acc1) = chunk(i, carry, True)
    o_ref[...] = (acc0 / l0 + acc1 / l1).astype(o_ref.dtype)


def _mla_attn(q, k, v, batch, seq, tq=512):
    tq = min(tq, seq)
    nq = seq // tq
    pairs = MLA_H // 2
    return pl.pallas_call(
        functools.partial(_mla_attn_kernel, tq=tq),
        grid=(batch, pairs, nq),
        in_specs=[pl.BlockSpec((tq, 2 * LANES), lambda b, p, i: (b * nq + i, p)),
                  pl.BlockSpec((seq, 2 * LANES), lambda b, p, i: (b, p)),
                  pl.BlockSpec((seq, 2 * LANES), lambda b, p, i: (b, p))],
        out_specs=pl.BlockSpec((tq, LANES), lambda b, p, i: (b * nq + i, p)),
        out_shape=jax.ShapeDtypeStruct((batch * seq, pairs * LANES), BF16),
        compiler_params=_cparams(("arbitrary", "arbitrary", "arbitrary")),
        name="mla_attn",
    )(q, k, v)


def _unit_lower_inverse_many(ns):
    c = ns[0].shape[0]
    eye = (_iota2((c, c), 0) == _iota2((c, c), 1)).astype(F32)
    xs = [-n for n in ns]
    ps = [eye + x for x in xs]
    xsplit = [_split2(x) for x in xs]
    for _ in range(int(math.log2(c)) - 1):
        xs = [_dot(xh, xh) + (_dot(xh, xl) + _dot(xl, xh)) for xh, xl in xsplit]
        xsplit = [_split2(x) for x in xs]
        psplit = [_split2(p) for p in ps]
        ps = [p + (_dot(ph, xh) + (_dot(ph, xl) + _dot(plo, xh)))
              for p, (ph, plo), (xh, xl) in zip(ps, psplit, xsplit)]
    return ps


def _gdn_kernel(qkv_ref, g_ref, z_ref, cw_ref, al_ref, dt_ref, on_ref, o_ref, ext_ref, st_ref):
    c = GDN_CHUNK
    hd = GDN_DK
    nqk = GDN_H * GDN_DK

    @pl.when(pl.program_id(1) == 0)
    def _():
        ext_ref[:, 0:8, :] = jnp.zeros((ext_ref.shape[0], 8, ext_ref.shape[2]), F32)
        st_ref[...] = jnp.zeros(st_ref.shape, F32)

    tri = (_iota2((c, c), 0) >= _iota2((c, c), 1)).astype(F32)
    row_ge = _iota2((c, c), 0) >= _iota2((c, c), 1)
    row_gt = _iota2((c, c), 0) > _iota2((c, c), 1)
    ones = jnp.ones((c, LANES), F32)
    lane = _iota2((c, LANES), 1)

    units = []
    for bb in range(qkv_ref.shape[0]):
        ext = ext_ref.at[bb]
        ext[8:8 + c, :] = qkv_ref[bb]
        conv = cw_ref[0:1, :] * ext[5:5 + c, :]
        for j in range(1, CONV_W):
            conv = conv + cw_ref[j:j + 1, :] * ext[5 + j:5 + j + c, :]
        ext[0:8, :] = ext[c:c + 8, :]
        act = _silu(conv)
        gates = g_ref[bb]
        beta_all = _sigmoid(gates)
        g_all = -jnp.exp(al_ref[...]) * _softplus(gates + dt_ref[...])
        gc_all = _dot_sel(tri, g_all)
        for h in range(GDN_H):
            q = act[:, h * hd:(h + 1) * hd]
            k = act[:, nqk + h * hd:nqk + (h + 1) * hd]
            v = act[:, 2 * nqk + h * GDN_DV:2 * nqk + (h + 1) * GDN_DV]
            q = q * lax.rsqrt(jnp.sum(q * q, axis=-1, keepdims=True) + EPS) * (GDN_DK ** -0.5)
            k = k * lax.rsqrt(jnp.sum(k * k, axis=-1, keepdims=True) + EPS)
            beta = _lane_bcast(beta_all, h)
            gcol = _lane_bcast(gc_all, GDN_H + h)
            grow = _dot_sel(ones, jnp.where(lane == GDN_H + h, gc_all, 0.0), _dot_nt)
            decay = jnp.exp(jnp.where(row_ge, gcol[:, :c] - grow, -jnp.inf))
            kb = k * beta
            lower = jnp.where(row_gt, _dot3(kb, k, _dot_nt) * decay, 0.0)
            eg = jnp.exp(gcol)
            glast = gcol[c - 1:c, :]
            units.append(dict(bb=bb, h=h, lower=lower, rhs=jnp.concatenate([v * beta, kb * eg], axis=1),
                              attn=_dot_nt(q.astype(BF16), k.astype(BF16)) * decay, qg=(q * eg).astype(BF16),
                              kg=(k * jnp.exp(glast - gcol)).astype(BF16), gl=jnp.exp(glast)))

    tinvs = _unit_lower_inverse_many([u["lower"] for u in units])
    uws = []
    for u, tinv in zip(units, tinvs):
        th, tl = _split2(tinv)
        rh, rl = _split2(u["rhs"])
        uws.append(_dot(th, rh) + (_dot(th, rl) + _dot(tl, rh)))
    states = [st_ref[u["bb"], u["h"]] for u in units]
    sbs = [s.astype(BF16) for s in states]
    vnews = [(uw[:, :GDN_DV] - _dot(uw[:, GDN_DV:].astype(BF16), sb)).astype(BF16) for uw, sb in zip(uws, sbs)]
    for u, state, sb, vnb in zip(units, states, sbs, vnews):
        bb, h = u["bb"], u["h"]
        o = _dot(u["qg"], sb) + _dot(u["attn"].astype(BF16), vnb)
        st_ref[bb, h] = state * u["gl"] + _dot_tn(u["kg"], vnb)
        o = _rms(o, on_ref[...]) * _silu(z_ref[bb, :, h * GDN_DV:(h + 1) * GDN_DV])
        o_ref[bb, :, h * GDN_DV:(h + 1) * GDN_DV] = o.astype(o_ref.dtype)


def _gdn(qkv, gates, z, conv_w, a_row, dt_row, o_norm, batch, seq):
    c = GDN_CHUNK
    nc = seq // c
    w3 = qkv.shape[1]
    wo = GDN_H * GDN_DV
    nb = SEQS_PER_STEP
    row = lambda b, i: (b, i, 0)
    fix = lambda b, i: (0, 0)
    out = pl.pallas_call(
        _gdn_kernel,
        grid=(batch // nb, nc),
        in_specs=[pl.BlockSpec((nb, c, w3), row), pl.BlockSpec((nb, c, LANES), row), pl.BlockSpec((nb, c, wo), row),
                  pl.BlockSpec(conv_w.shape, fix), pl.BlockSpec((1, LANES), fix), pl.BlockSpec((1, LANES), fix),
                  pl.BlockSpec((1, GDN_DV), fix)],
        out_specs=pl.BlockSpec((nb, c, wo), row),
        out_shape=jax.ShapeDtypeStruct((batch, seq, wo), BF16),
        scratch_shapes=[pltpu.VMEM((nb, c + 8, w3), F32), pltpu.VMEM((nb, GDN_H, GDN_DK, GDN_DV), F32)],
        compiler_params=_cparams(("arbitrary", "arbitrary")),
        name="gdn",
    )(qkv.reshape(batch, seq, w3), gates.reshape(batch, seq, LANES), z.reshape(batch, seq, wo), conv_w, a_row, dt_row, o_norm)
    return out.reshape(batch * seq, wo)


def _mlstm_kernel(q_ref, k_ref, v_ref, og_ref, g_ref, bias_ref, nrm_ref, o_ref, c_ref, n_ref, m_ref):
    @pl.when(pl.program_id(1) == 0)
    def _():
        c_ref[...] = jnp.zeros(c_ref.shape, F32)
        n_ref[...] = jnp.zeros(n_ref.shape, F32)
        m_ref[...] = jnp.zeros(m_ref.shape, F32)

    c = ML_CHUNK
    tri = (_iota2((c, c), 0) >= _iota2((c, c), 1)).astype(F32)
    row_ge = _iota2((c, c), 0) >= _iota2((c, c), 1)
    ones = jnp.ones((c, LANES), F32)
    lane = _iota2((c, LANES), 1)

    units = []
    for bb in range(q_ref.shape[0]):
        pre = g_ref[bb] + bias_ref[...]
        logf = jnp.minimum(pre, 0.0) - jnp.log(1.0 + jnp.exp(-jnp.abs(pre)))
        bcum_all = _dot_sel(tri, logf)
        for h in range(ML_H):
            q = q_ref[bb, :, h * LANES:(h + 1) * LANES]
            k = k_ref[bb, :, h * LANES:(h + 1) * LANES] * (ML_DK ** -0.5)
            units.append(dict(bb=bb, h=h, q=q, k=k, qb=q.astype(BF16), vb=v_ref[bb, :, h * ML_DV:(h + 1) * ML_DV].astype(BF16),
                              bcol=_lane_bcast(bcum_all, ML_H + h),
                              icol=_lane_bcast(pre, h),
                              col=jnp.where(lane == h, pre, 0.0) - jnp.where(lane == ML_H + h, bcum_all, 0.0),
                              m_st=m_ref[bb, h], cst=c_ref[bb, h], nst=n_ref[bb, h]))
    for u in units:
        u["row"] = _dot_sel(ones, u["col"], _dot_nt)
        u["qk"] = _dot_nt(u["qb"], u["k"].astype(BF16))
        u["qc"] = _dot(u["qb"], u["cst"].astype(BF16))
    for u in units:
        d = jnp.where(row_ge, u["bcol"][:, :c] + u["row"], -jnp.inf)
        inter = u["bcol"] + u["m_st"]
        m_t = jnp.maximum(inter, jnp.max(d, axis=-1, keepdims=True))
        u["m_t"] = m_t
        u["w_inter"] = jnp.exp(inter - m_t)
        u["p"] = jnp.exp(d - m_t[:, :c]) * u["qk"]
        u["pv"] = _dot(u["p"].astype(BF16), u["vb"])
        b_end = u["bcol"][c - 1:c, :]
        a = b_end - u["bcol"] + u["icol"]
        m_new = jnp.maximum(b_end + u["m_st"], jnp.max(a, axis=0, keepdims=True))
        u["m_new"] = m_new
        u["keep"] = jnp.exp(b_end + u["m_st"] - m_new)
        u["ks"] = u["k"] * jnp.exp(a - m_new)
        u["kv"] = _dot_tn(u["ks"].astype(BF16), u["vb"])
    for u in units:
        bb, h = u["bb"], u["h"]
        num = u["w_inter"] * u["qc"] + u["pv"]
        den = (u["w_inter"] * jnp.sum(u["q"] * u["nst"], axis=-1, keepdims=True)
               + jnp.sum(u["p"], axis=-1, keepdims=True))
        hc = num / jnp.maximum(jnp.abs(den), jnp.exp(-u["m_t"]))
        c_ref[bb, h] = u["cst"] * u["keep"] + u["kv"]
        n_ref[bb, h] = u["nst"] * u["keep"] + jnp.sum(u["ks"], axis=0, keepdims=True)
        m_ref[bb, h] = u["m_new"]
        hn = (_rms(hc, nrm_ref[:, h * ML_DV:(h + 1) * ML_DV])
              * _sigmoid(og_ref[bb, :, h * ML_DV:(h + 1) * ML_DV]))
        o_ref[bb, :, h * ML_DV:(h + 1) * ML_DV] = hn.astype(o_ref.dtype)


def _mlstm(mq, mk, mv, mo, gates, bias_row, norm_row, batch, seq):
    c = ML_CHUNK
    nc = seq // c
    nb = SEQS_PER_STEP
    row = lambda b, i: (b, i, 0)
    fix = lambda b, i: (0, 0)
    wide = ML_H * LANES
    r3 = lambda a: a.reshape(batch, seq, a.shape[-1])
    out = pl.pallas_call(
        _mlstm_kernel,
        grid=(batch // nb, nc),
        in_specs=[pl.BlockSpec((nb, c, wide), row), pl.BlockSpec((nb, c, wide), row), pl.BlockSpec((nb, c, wide), row),
                  pl.BlockSpec((nb, c, wide), row), pl.BlockSpec((nb, c, LANES), row),
                  pl.BlockSpec((1, LANES), fix), pl.BlockSpec((1, wide), fix)],
        out_specs=pl.BlockSpec((nb, c, wide), row),
        out_shape=jax.ShapeDtypeStruct((batch, seq, wide), BF16),
        scratch_shapes=[pltpu.VMEM((nb, ML_H, LANES, ML_DV), F32), pltpu.VMEM((nb, ML_H, 1, LANES), F32),
                        pltpu.VMEM((nb, ML_H, 1, LANES), F32)],
        compiler_params=_cparams(("arbitrary", "arbitrary")),
        name="mlstm",
    )(r3(mq), r3(mk), r3(mv), r3(mo), r3(gates), bias_row, norm_row)
    return out.reshape(batch * seq, wide)


def _swa_kernel(q_ref, kc_ref, kp_ref, vc_ref, vp_ref, sink_ref, o_ref):
    w = WINDOW
    n = pl.program_id(1)
    scale = SWA_D ** -0.5
    qi = _iota2((w, w), 0)
    kj = _iota2((w, w), 1)
    mask_c = kj <= qi
    mask_p = jnp.logical_and(kj > qi, n > 0)
    grp = SWA_H // SWA_KV
    neg = -1e30
    scores = []
    for h in range(SWA_H):
        g = h // grp
        q = q_ref[:, h * LANES:(h + 1) * LANES]
        scores.append((_dot_nt(q, kc_ref[:, g * LANES:(g + 1) * LANES]), _dot_nt(q, kp_ref[:, g * LANES:(g + 1) * LANES])))
    probs = []
    for h, (sc, sp) in enumerate(scores):
        s_c = jnp.where(mask_c, sc * scale, neg)
        s_p = jnp.where(mask_p, sp * scale, neg)
        sink = sink_ref[:, h:h + 1]
        m = jnp.maximum(jnp.max(jnp.maximum(s_c, s_p), axis=-1, keepdims=True), sink)
        p_c = jnp.where(mask_c, jnp.exp(s_c - m), 0.0)
        p_p = jnp.where(mask_p, jnp.exp(s_p - m), 0.0)
        den = jnp.sum(p_c + p_p, axis=-1, keepdims=True) + jnp.exp(sink - m)
        inv = 1.0 / den
        probs.append(((p_c * inv).astype(BF16), (p_p * inv).astype(BF16)))
    for pair in range(SWA_H // 2):
        acc = None
        for sub in range(2):
            h = 2 * pair + sub
            vcol = (2 * (h // grp) + sub) * LANES
            p_c, p_p = probs[h]
            part = _dot(p_c, vc_ref[:, vcol:vcol + LANES]) + _dot(p_p, vp_ref[:, vcol:vcol + LANES])
            acc = part if acc is None else acc + part
        o_ref[:, pair * LANES:(pair + 1) * LANES] = acc.astype(o_ref.dtype)


def _swa(sq, sk, sv, sinks_row, batch, seq):
    w = WINDOW
    nb = seq // w
    cur = lambda b, n: (b * nb + n, 0)
    prev = lambda b, n: (b * nb + jnp.maximum(n - 1, 0), 0)
    return pl.pallas_call(
        _swa_kernel,
        grid=(batch, nb),
        in_specs=[pl.BlockSpec((w, sq.shape[1]), cur),
                  pl.BlockSpec((w, sk.shape[1]), cur), pl.BlockSpec((w, sk.shape[1]), prev),
                  pl.BlockSpec((w, sv.shape[1]), cur), pl.BlockSpec((w, sv.shape[1]), prev),
                  pl.BlockSpec((1, LANES), lambda b, n: (0, 0))],
        out_specs=pl.BlockSpec((w, SWA_H * SWA_D), cur),
        out_shape=jax.ShapeDtypeStruct((batch * seq, SWA_H * SWA_D), BF16),
        compiler_params=_cparams(("arbitrary", "arbitrary")),
        name="swa",
    )(sq, sk, sk, sv, sv, sinks_row)


def _layer_norm(h, g, b):
    mu = jnp.mean(h, axis=-1, keepdims=True)
    d = h - mu
    var = jnp.mean(d * d, axis=-1, keepdims=True)
    return d * lax.rsqrt(var + LN_EPS) * g + b


def _outproj_kernel(x_ref, a1_ref, a2_ref, w_ref, g_ref, b_ref, o_ref, op_ref):
    k1 = a1_ref.shape[1]
    y = _dot(a1_ref[...].astype(BF16), w_ref[0:k1, :]) + _dot(a2_ref[...].astype(BF16), w_ref[k1:, :])
    h = _layer_norm(DN_ALPHA * x_ref[...] + y, g_ref[...], b_ref[...])
    o_ref[...] = h
    op_ref[...] = _pack_pairs(h)


def _outproj_ln(x, a1, a2, w, g, b, tm=256):
    t, d = x.shape
    row = lambda i: (i, 0)
    fix = lambda i: (0, 0)
    return pl.pallas_call(
        _outproj_kernel,
        grid=(t // tm,),
        in_specs=[pl.BlockSpec((tm, d), row), pl.BlockSpec((tm, a1.shape[1]), row), pl.BlockSpec((tm, a2.shape[1]), row),
                  pl.BlockSpec(w.shape, fix), pl.BlockSpec((1, d), fix), pl.BlockSpec((1, d), fix)],
        out_specs=[pl.BlockSpec((tm, d), row), pl.BlockSpec((tm, d // 2), row)],
        out_shape=[jax.ShapeDtypeStruct((t, d), F32), jax.ShapeDtypeStruct((t, d // 2), jnp.uint32)],
        compiler_params=_cparams(("arbitrary",)),
        name="outproj_ln",
    )(x, a1, a2, w, g, b)


def _first_index(x, m, iota_f, sentinel):
    return jnp.min(jnp.where(x == m, iota_f, sentinel), axis=0, keepdims=True)


def _router_kernel(x_ref, wt_ref, bias_ref, idx_ref, gate_ref, rank_ref, cnt_ref, carry_ref):
    tm = x_ref.shape[0]
    e = N_EXPERTS
    gs = e // N_GROUPS
    ninf = -jnp.inf

    @pl.when(pl.program_id(0) == 0)
    def _():
        carry_ref[...] = jnp.zeros(carry_ref.shape, F32)

    logits = _dot_nt(wt_ref[...], x_ref[...], HI)
    scores = _sigmoid(logits)
    sel = scores + bias_ref[:, 0:1]

    sub_f = _iota2((gs, tm), 0).astype(F32)
    gscore = []
    for g in range(N_GROUPS):
        blk = sel[g * gs:(g + 1) * gs, :]
        m1 = jnp.max(blk, axis=0, keepdims=True)
        i1 = _first_index(blk, m1, sub_f, float(gs))
        m2 = jnp.max(jnp.where(sub_f == i1, ninf, blk), axis=0, keepdims=True)
        gscore.append(m1 + m2)
    gsc = jnp.concatenate(gscore, axis=0)
    grp_f = _iota2((N_GROUPS, tm), 0).astype(F32)
    gmask = jnp.zeros((N_GROUPS, tm), F32)
    for _ in range(TOPK_GROUPS):
        m = jnp.max(gsc, axis=0, keepdims=True)
        gi = _first_index(gsc, m, grp_f, float(N_GROUPS))
        hit = grp_f == gi
        gmask = jnp.where(hit, 1.0, gmask)
        gsc = jnp.where(hit, ninf, gsc)
    masked = jnp.concatenate(
        [jnp.where(gmask[g:g + 1, :] > 0.0, sel[g * gs:(g + 1) * gs, :], ninf) for g in range(N_GROUPS)], axis=0)

    exp_f = _iota2((e, tm), 0).astype(F32)
    chosen = jnp.zeros((e, tm), F32)
    idxs, gates = [], []
    for _ in range(TOP_K):
        m = jnp.max(masked, axis=0, keepdims=True)
        ei = _first_index(masked, m, exp_f, float(e))
        hit = exp_f == ei
        idxs.append(ei)
        gates.append(jnp.sum(jnp.where(hit, scores, 0.0), axis=0, keepdims=True))
        chosen = jnp.where(hit, 1.0, chosen)
        masked = jnp.where(hit, ninf, masked)
    gate = jnp.concatenate(gates, axis=0)
    gate = gate / jnp.sum(gate, axis=0, keepdims=True) * ROUTED_SCALE
    idx_f = jnp.concatenate(idxs, axis=0)

    upper = (_iota2((tm, tm), 0) < _iota2((tm, tm), 1)).astype(BF16)
    before = _dot(chosen.astype(BF16), upper) + carry_ref[...][:, 0:1]
    ranks = [jnp.sum(jnp.where(exp_f == idxs[k], before, 0.0), axis=0, keepdims=True) for k in range(TOP_K)]
    carry_ref[...] = carry_ref[...] + jnp.sum(chosen, axis=1, keepdims=True)

    idx_ref[...] = idx_f.astype(jnp.int32)
    gate_ref[...] = gate
    rank_ref[...] = jnp.concatenate(ranks, axis=0).astype(jnp.int32)
    cnt_ref[...] = carry_ref[...]


def _router(x, wt, bias_col, tm=512):
    t, d = x.shape
    col = lambda i: (0, i)
    fix = lambda i: (0, 0)
    return pl.pallas_call(
        _router_kernel,
        grid=(t // tm,),
        in_specs=[pl.BlockSpec((tm, d), lambda i: (i, 0)), pl.BlockSpec(wt.shape, fix), pl.BlockSpec((N_EXPERTS, LANES), fix)],
        out_specs=[pl.BlockSpec((TOP_K, tm), col), pl.BlockSpec((TOP_K, tm), col), pl.BlockSpec((TOP_K, tm), col),
                   pl.BlockSpec((N_EXPERTS, LANES), fix)],
        out_shape=[jax.ShapeDtypeStruct((TOP_K, t), jnp.int32), jax.ShapeDtypeStruct((TOP_K, t), F32),
                   jax.ShapeDtypeStruct((TOP_K, t), jnp.int32), jax.ShapeDtypeStruct((N_EXPERTS, LANES), F32)],
        scratch_shapes=[pltpu.VMEM((N_EXPERTS, LANES), F32)],
        compiler_params=_cparams(("arbitrary",)),
        name="router",
    )(x, wt, bias_col)


def _dest_kernel(idx_ref, rank_ref, start_ref, dest_ref):
    tm = idx_ref.shape[1]
    exp_i = _iota2((N_EXPERTS, tm), 0)
    start = start_ref[:, 0:1]
    rows = [jnp.sum(jnp.where(exp_i == idx_ref[s:s + 1, :], start, 0.0), axis=0, keepdims=True) for s in range(TOP_K)]
    dest_ref[...] = jnp.concatenate(rows, axis=0).astype(jnp.int32) + rank_ref[...]


def _dest_rows(idx, rank, start_col, tm=2048):
    t = idx.shape[1]
    tm = min(tm, t)
    col = lambda i: (0, i)
    return pl.pallas_call(
        _dest_kernel,
        grid=(t // tm,),
        in_specs=[pl.BlockSpec((TOP_K, tm), col), pl.BlockSpec((TOP_K, tm), col),
                  pl.BlockSpec((N_EXPERTS, LANES), lambda i: (0, 0))],
        out_specs=pl.BlockSpec((TOP_K, tm), col),
        out_shape=jax.ShapeDtypeStruct((TOP_K, t), jnp.int32),
        compiler_params=_cparams(("arbitrary",)),
        name="moe_dest",
    )(idx, rank, start_col)


def _pack_pairs(x):
    n = x.shape[1] // 2
    hi = lax.bitcast_convert_type(x[:, :n].astype(BF16).astype(F32), jnp.uint32)
    lo = lax.bitcast_convert_type(x[:, n:].astype(BF16).astype(F32), jnp.uint32)
    return hi | (lo >> 16)


def _unpack_pairs(w):
    hi = lax.bitcast_convert_type(w & jnp.uint32(0xFFFF0000), F32)
    lo = lax.bitcast_convert_type(w << 16, F32)
    return hi, lo


def _sc_scatter_rows(xp, dest, rows, chunk=LANES):
    t, width = xp.shape
    info = plsc.get_sparse_core_info()
    ncores, nsub = info.num_cores, info.num_subcores
    per_worker = t // (ncores * nsub)
    nchunk = per_worker // chunk
    mesh = plsc.VectorSubcoreMesh(core_axis_name="c", subcore_axis_name="s")

    @functools.partial(
        pl.kernel, mesh=mesh,
        out_type=jax.ShapeDtypeStruct((rows, width), xp.dtype),
        scratch_types=[pltpu.VMEM((TOP_K, chunk), jnp.int32), pltpu.VMEM((chunk, width), xp.dtype), pltpu.SemaphoreType.DMA],
    )
    def scatter(xp_hbm, dest_hbm, out_hbm, idx_v, rows_v, sem):
        base = (lax.axis_index("s") * ncores + lax.axis_index("c")) * per_worker

        @pl.loop(0, nchunk)
        def _(i):
            off = pl.multiple_of(base + i * chunk, chunk)
            pltpu.sync_copy(dest_hbm.at[:, pl.ds(off, chunk)], idx_v)
            pltpu.sync_copy(xp_hbm.at[pl.ds(off, chunk)], rows_v)
            copies = [pltpu.async_copy(rows_v, out_hbm.at[idx_v.at[s]], sem) for s in range(TOP_K)]
            for cp in copies:
                cp.wait()

    return scatter(xp, dest)


def _experts_kernel(be_ref, nu_ref, nv_ref, xs_ref, wg_ref, wu_ref, wd_ref, ys_ref, wgb_ref, wub_ref, wdb_ref):
    i = pl.program_id(0)

    @pl.when(jnp.logical_or(i == 0, be_ref[i] != be_ref[jnp.maximum(i - 1, 0)]))
    def _():
        wgb_ref[...] = wg_ref[0, 0].astype(BF16)
        wub_ref[...] = wu_ref[0, 0].astype(BF16)
        wdb_ref[...] = wd_ref[0, 0].astype(BF16)

    @pl.when(i < nu_ref[0])
    def _():
        half = xs_ref.shape[1]
        live = _iota2((xs_ref.shape[0], 1), 0) < nv_ref[i]
        xa, xb = _unpack_pairs(jnp.where(live, xs_ref[...], jnp.uint32(0)))
        xa = xa.astype(BF16)
        xb = xb.astype(BF16)
        gate = _dot(xa, wgb_ref[:half, :]) + _dot(xb, wgb_ref[half:, :])
        up = _dot(xa, wub_ref[:half, :]) + _dot(xb, wub_ref[half:, :])
        h = _silu(gate) * up
        ys_ref[...] = _pack_pairs(_dot(h.astype(BF16), wdb_ref[...]))

    @pl.when(i >= nu_ref[0])
    def _():
        ys_ref[...] = jnp.zeros(ys_ref.shape, ys_ref.dtype)


def _experts(block_e, n_used, n_valid, xs, wg, wu, wd, layer):
    rows, half = xs.shape
    d = 2 * half
    nb = rows // EXPERT_BLOCK
    blk = lambda i, be, nu, nv: (jnp.minimum(i, nu[0] - 1), 0)
    wsel = lambda i, be, nu, nv: (layer, be[i], 0, 0)
    return pl.pallas_call(
        _experts_kernel,
        grid_spec=pltpu.PrefetchScalarGridSpec(
            num_scalar_prefetch=3,
            grid=(nb,),
            in_specs=[pl.BlockSpec((EXPERT_BLOCK, half), blk),
                      pl.BlockSpec((1, 1, d, D_EXPERT), wsel), pl.BlockSpec((1, 1, d, D_EXPERT), wsel),
                      pl.BlockSpec((1, 1, D_EXPERT, d), wsel)],
            out_specs=pl.BlockSpec((EXPERT_BLOCK, half), lambda i, be, nu, nv: (i, 0)),
            scratch_shapes=[pltpu.VMEM((d, D_EXPERT), BF16), pltpu.VMEM((d, D_EXPERT), BF16),
                            pltpu.VMEM((D_EXPERT, d), BF16)],
        ),
        out_shape=jax.ShapeDtypeStruct((rows, half), jnp.uint32),
        compiler_params=_cparams(("arbitrary",)),
        name="moe_experts",
    )(block_e, n_used, n_valid, xs, wg, wu, wd)


def _sc_gather_rows(table, idx, chunk=SC_CHUNK):
    n = idx.shape[0]
    width = table.shape[1]
    info = plsc.get_sparse_core_info()
    ncores, nsub = info.num_cores, info.num_subcores
    per_worker = n // (ncores * nsub)
    nchunk = per_worker // chunk
    mesh = plsc.VectorSubcoreMesh(core_axis_name="c", subcore_axis_name="s")

    @functools.partial(
        pl.kernel, mesh=mesh,
        out_type=jax.ShapeDtypeStruct((n, width), table.dtype),
        scratch_types=[pltpu.VMEM((nchunk, chunk), jnp.int32), pltpu.VMEM((2, chunk, width), table.dtype),
                       pltpu.SemaphoreType.DMA((2,)), pltpu.SemaphoreType.DMA((2,))],
    )
    def gather(table_hbm, idx_hbm, out_hbm, idx_v, rows_v, gsem, wsem):
        wid = lax.axis_index("s") * ncores + lax.axis_index("c")
        base = wid * per_worker
        pltpu.sync_copy(idx_hbm.at[pl.ds(wid * nchunk, nchunk)], idx_v)

        def fetch(j, b):
            return pltpu.make_async_copy(table_hbm.at[idx_v.at[j]], rows_v.at[b], gsem.at[b])

        def flush(j, b):
            off = pl.multiple_of(base + j * chunk, chunk)
            return pltpu.make_async_copy(rows_v.at[b], out_hbm.at[pl.ds(off, chunk)], wsem.at[b])

        fetch(0, 0).start()

        @pl.loop(0, nchunk, step=2)
        def _(i):
            for b in range(2):
                j = i + b
                fetch(j, b).wait()

                @pl.when(j + 1 < nchunk)
                def _():
                    @pl.when(j >= 1)
                    def _():
                        flush(j - 1, 1 - b).wait()

                    fetch(j + 1, 1 - b).start()

                flush(j, b).start()

        flush(nchunk - 2, 0).wait()
        flush(nchunk - 1, 1).wait()

    return gather(table, idx.reshape(n // chunk, chunk))


def _combine_kernel(x_ref, gate_ref, rows_ref, sg_ref, su_ref, sd_ref, g_ref, b_ref, o_ref):
    x = x_ref[...]
    xb = x.astype(BF16)
    hs = _silu(_dot(xb, sg_ref[...])) * _dot(xb, su_ref[...])
    ff = _dot(hs.astype(BF16), sd_ref[...])
    gate = gate_ref[...]
    half = rows_ref.shape[2]
    ya = ff[:, :half]
    yb = ff[:, half:]
    for s in range(TOP_K):
        a, b = _unpack_pairs(rows_ref[s])
        ya = ya + gate[:, s:s + 1] * a
        yb = yb + gate[:, s:s + 1] * b
    ff = jnp.concatenate([ya, yb], axis=1)
    o_ref[...] = _layer_norm(DN_ALPHA * x + ff, g_ref[...], b_ref[...])


def _combine(x, gate_t, rows, sg, su, sd, g, b, tm=256):
    t, d = x.shape
    row = lambda i: (i, 0)
    fix = lambda i: (0, 0)
    return pl.pallas_call(
        _combine_kernel,
        grid=(t // tm,),
        in_specs=[pl.BlockSpec((tm, d), row), pl.BlockSpec((tm, TOP_K), row),
                  pl.BlockSpec((TOP_K, tm, d // 2), lambda i: (0, i, 0)),
                  pl.BlockSpec(sg.shape, fix), pl.BlockSpec(su.shape, fix), pl.BlockSpec(sd.shape, fix),
                  pl.BlockSpec((1, d), fix), pl.BlockSpec((1, d), fix)],
        out_specs=pl.BlockSpec((tm, d), row),
        out_shape=jax.ShapeDtypeStruct((t, d), F32),
        compiler_params=_cparams(("arbitrary",)),
        name="moe_combine",
    )(x, gate_t, rows, sg, su, sd, g, b)


def _take_cols(w, idx):
    wz = jnp.concatenate([w, jnp.zeros((w.shape[0], 1), w.dtype)], axis=1)
    idx = np.where(np.asarray(idx) < 0, w.shape[1], np.asarray(idx))
    return jnp.take(wz, jnp.asarray(idx, jnp.int32), axis=1)


def _pad_lane_row(v, first_lane, width=LANES):
    out = jnp.zeros((1, width), F32)
    return lax.dynamic_update_slice(out, v.reshape(1, -1).astype(F32), (0, first_lane))


def _even_in_cols():
    z = lambda n: -np.ones(n, int)
    kr0 = Q_LORA + KV_LORA
    half = MLA_ROPE // 2
    cols = [np.arange(0, Q_LORA), np.arange(Q_LORA, Q_LORA + KV_LORA),
            z(64), np.arange(kr0, kr0 + MLA_ROPE), z(32),
            z(64), np.arange(kr0 + half, kr0 + MLA_ROPE), np.arange(kr0, kr0 + half), z(32)]
    g0 = kr0 + MLA_ROPE
    nqk = GDN_H * GDN_DK
    cols.append(np.arange(g0, g0 + 3 * nqk))
    zoff = g0 + 3 * nqk + 2 * GDN_H
    cols.append(np.arange(zoff, zoff + GDN_H * GDN_DV))
    cols += [np.arange(g0 + 3 * nqk, g0 + 3 * nqk + 2 * GDN_H), z(LANES - 2 * GDN_H)]
    return np.concatenate(cols)


EV_WIDTHS = (Q_LORA + KV_LORA + 2 * LANES, 3 * GDN_H * GDN_DK, GDN_H * GDN_DV, LANES)


def _mla_q_cols():
    per = MLA_NOPE + MLA_ROPE
    half = MLA_ROPE // 2
    main, sw = [], []
    for h in range(MLA_H):
        b = h * per
        main += [np.arange(b, b + per), -np.ones(LANES - per, int)]
        sw += [-np.ones(MLA_NOPE, int), np.arange(b + MLA_NOPE + half, b + per), np.arange(b + MLA_NOPE, b + MLA_NOPE + half),
               -np.ones(LANES - per, int)]
    return np.concatenate(main + sw)


def _mla_kv_cols():
    per = MLA_NOPE + MLA_V
    kc, vc = [], []
    for h in range(MLA_H):
        b = h * per
        kc += [np.arange(b, b + MLA_NOPE), -np.ones(LANES - MLA_NOPE, int)]
        vv = np.arange(b + MLA_NOPE, b + per)
        pad = -np.ones(LANES - MLA_V, int)
        vc += [vv, pad] if h % 2 == 0 else [pad, vv]
    return np.concatenate(kc + vc)


def _odd_in_cols():
    z = lambda n: -np.ones(n, int)
    o = 0
    cols = []
    mq0, mk0 = 0, ML_H * ML_DK
    for base in (mq0, mk0):
        for h in range(ML_H):
            cols += [np.arange(base + h * ML_DK, base + (h + 1) * ML_DK), z(LANES - ML_DK)]
    mv0 = 2 * ML_H * ML_DK
    cols.append(np.arange(mv0, mv0 + ML_H * ML_DV))
    mi0 = mv0 + ML_H * ML_DV
    mo0 = mi0 + 2 * ML_H
    cols.append(np.arange(mo0, mo0 + ML_H * ML_DV))
    cols += [np.arange(mi0, mi0 + 2 * ML_H), z(LANES - 2 * ML_H)]
    sq0 = mo0 + ML_H * ML_DV
    sk0 = sq0 + SWA_H * SWA_D
    sv0 = sk0 + SWA_KV * SWA_D
    half = SWA_D // 2

    def heads(base, n, swapped):
        out = []
        for h in range(n):
            b = base + h * SWA_D
            if swapped:
                out += [np.arange(b + half, b + SWA_D), np.arange(b, b + half), z(LANES - SWA_D)]
            else:
                out += [np.arange(b, b + SWA_D), z(LANES - SWA_D)]
        return out

    cols += heads(sq0, SWA_H, False) + heads(sq0, SWA_H, True) + heads(sk0, SWA_KV, False) + heads(sk0, SWA_KV, True)
    for g in range(SWA_KV):
        vv = np.arange(sv0 + g * SWA_D, sv0 + (g + 1) * SWA_D)
        cols += [vv, z(LANES - SWA_D), z(LANES - SWA_D), vv]
    return np.concatenate(cols)


def _even_mixer(x, tabs, w_in, q_norm, w_qb, kv_norm, w_kvb, conv_w, a_log, dt_bias, o_norm, batch, seq):
    ctab, stab = tabs
    w = _take_cols(w_in, _even_in_cols()).astype(BF16)
    mla_in, qkv, z, gates = _proj(x, w, EV_WIDTHS, (F32, F32, F32, F32))
    wq2 = _take_cols(w_qb, _mla_q_cols()).astype(BF16)
    wkv2 = _take_cols(w_kvb, _mla_kv_cols()).astype(BF16)
    q, k, v = _mla_prep(mla_in, ctab, stab, q_norm.reshape(1, -1), kv_norm.reshape(1, -1), wq2, wkv2)
    o_a = _mla_attn(q, k, v, batch, seq)
    o_b = _gdn(qkv, gates, z, conv_w, _pad_lane_row(a_log, GDN_H), _pad_lane_row(dt_bias, GDN_H),
               o_norm.reshape(1, -1), batch, seq)
    return o_a, o_b


def _odd_mixer(x, tabs, w_in, b_i, b_f, ml_norm, sinks, batch, seq):
    ctab, stab = tabs
    w = _take_cols(w_in, _odd_in_cols()).astype(BF16)
    mq, mk, mv, mo, mg, sq, sk, sv = _proj_odd(x, w, ctab, stab)
    bias_row = _pad_lane_row(jnp.concatenate([b_i, b_f]), 0)
    o_c = _mlstm(mq, mk, mv, mo, mg, bias_row, ml_norm.reshape(1, -1), batch, seq)
    o_d = _swa(sq, sk, sv, _pad_lane_row(sinks, 0), batch, seq)
    return o_c, o_d


def _moe(x, xp, router_w, router_b, w_gate, w_up, w_down, layer, s_gate, s_up, s_down, ln_g, ln_b):
    t, d = x.shape
    bias_col = jnp.broadcast_to(router_b.reshape(-1, 1).astype(F32), (N_EXPERTS, LANES))
    idx, gate, rank, cnt = _router(x, router_w.T, bias_col)
    counts = cnt[:, 0].astype(jnp.int32)
    padded = (counts + EXPERT_BLOCK - 1) // EXPERT_BLOCK * EXPERT_BLOCK
    pad_end = jnp.cumsum(padded)
    pad_start = pad_end - padded
    start_col = jnp.broadcast_to(pad_start.astype(F32).reshape(-1, 1), (N_EXPERTS, LANES))
    dest = _dest_rows(idx, rank, start_col)
    n_blocks = t * TOP_K // EXPERT_BLOCK + N_EXPERTS
    rows = n_blocks * EXPERT_BLOCK
    block_row = jnp.arange(n_blocks, dtype=jnp.int32) * EXPERT_BLOCK
    block_e = jnp.minimum(jnp.sum((pad_end[None, :] <= block_row[:, None]).astype(jnp.int32), axis=1), N_EXPERTS - 1)
    n_used = (pad_end[-1:] // EXPERT_BLOCK).astype(jnp.int32)
    live_end = jnp.sum(jnp.where(block_e[:, None] == jnp.arange(N_EXPERTS, dtype=jnp.int32)[None, :],
                                 (pad_start + counts)[None, :], 0), axis=1)
    n_valid = jnp.clip(live_end - block_row, 0, EXPERT_BLOCK).astype(jnp.int32)
    xs = _sc_scatter_rows(xp, dest, rows)
    ys = _experts(block_e, n_used, n_valid, xs, w_gate, w_up, w_down, layer)
    picked = _sc_gather_rows(ys, dest.reshape(-1)).reshape(TOP_K, t, d // 2)
    return _combine(x, gate.T, picked, s_gate.astype(BF16), s_up.astype(BF16), s_down.astype(BF16),
                    ln_g.reshape(1, -1), ln_b.reshape(1, -1))


def kernel(x, positions, ev_w_in, mla_q_norm, mla_w_qb, mla_kv_norm, mla_w_kvb, gdn_conv, gdn_a_log, gdn_dt_bias, gdn_norm, ev_w_out, od_w_in, mlstm_b_i, mlstm_b_f, mlstm_norm, swa_sinks, od_w_out, ln1_g, ln1_b, router_w, router_b, moe_w_gate, moe_w_up, moe_w_down, shared_w_gate, shared_w_up, shared_w_down, ln2_g, ln2_b):
    batch, seq, d = x.shape
    t = batch * seq
    pos = positions.reshape(t, 1).astype(F32)
    tabs_m = _rope_tables(pos, _rope_rows(MLA_ROPE, MLA_NOPE, MLA_NOPE))
    tabs_s = _rope_tables(pos, _rope_rows(SWA_D, 0, 0))
    h = x.reshape(t, d)
    for layer in range(DEPTH):
        j = layer // 2
        if layer % 2 == 0:
            a1, a2 = _even_mixer(h, tabs_m, ev_w_in[j], mla_q_norm[j], mla_w_qb[j], mla_kv_norm[j], mla_w_kvb[j],
                                 gdn_conv[j], gdn_a_log[j], gdn_dt_bias[j], gdn_norm[j], batch, seq)
            w_out = ev_w_out[j]
        else:
            a1, a2 = _odd_mixer(h, tabs_s, od_w_in[j], mlstm_b_i[j], mlstm_b_f[j], mlstm_norm[j], swa_sinks[j], batch, seq)
            w_out = od_w_out[j]
        h, hp = _outproj_ln(h, a1, a2, w_out.astype(BF16), ln1_g[layer].reshape(1, -1), ln1_b[layer].reshape(1, -1))
        h = _moe(h, hp, router_w[layer], router_b[layer], moe_w_gate, moe_w_up, moe_w_down, layer,
                 shared_w_gate[layer], shared_w_up[layer], shared_w_down[layer], ln2_g[layer], ln2_b[layer])
    return h.reshape(batch, seq, d)
```

```python
import functools
import math

import numpy as np
import jax
import jax.numpy as jnp
from jax import lax
from jax.experimental import pallas as pl
from jax.experimental.pallas import tpu as pltpu
from jax.experimental.pallas import tpu_sc as plsc

F32 = jnp.float32
BF16 = jnp.bfloat16
HI = lax.Precision.HIGHEST

D_MODEL = 1024
DEPTH = 4
ROPE_THETA = 10000.0
EPS = 1e-6
LN_EPS = 1e-5
MLA_H, MLA_NOPE, MLA_ROPE, MLA_V = 8, 64, 32, 64
Q_LORA, KV_LORA = 256, 128
GDN_H, GDN_DK, GDN_DV, CONV_W, GDN_CHUNK = 4, 128, 128, 4, 64
ML_H, ML_DK, ML_DV, ML_CHUNK = 4, 64, 128, 64
SWA_H, SWA_KV, SWA_D, WINDOW = 8, 2, 64, 128
N_EXPERTS, N_GROUPS, TOPK_GROUPS, TOP_K = 64, 8, 4, 8
D_EXPERT, D_SHARED = 256, 256
ROUTED_SCALE = 2.5
DN_ALPHA = (2 * DEPTH) ** 0.25

LANES = 128
V7X_VMEM_BYTES = 64 * 1024 * 1024
VMEM_LIMIT = 48 * 1024 * 1024

EXPERT_BLOCK = 512
SEQS_PER_STEP = 2
DMA_GROUP = 2
SC_CHUNK = 64


def _cparams(sem, vmem=VMEM_LIMIT):
    return pltpu.CompilerParams(dimension_semantics=sem, vmem_limit_bytes=vmem)


def _dot(a, b, precision=None):
    return jnp.dot(a, b, preferred_element_type=F32, precision=precision)


def _dot_nt(a, b, precision=None):
    return lax.dot_general(a, b, (((1,), (1,)), ((), ())), preferred_element_type=F32, precision=precision)


def _dot_tn(a, b, precision=None):
    return lax.dot_general(a, b, (((0,), (0,)), ((), ())), preferred_element_type=F32, precision=precision)


def _split2(a):
    hi = a.astype(BF16)
    lo = (a - hi.astype(F32)).astype(BF16)
    return hi, lo


def _split3(a):
    p1 = a.astype(BF16)
    r = a - p1.astype(F32)
    p2 = r.astype(BF16)
    p3 = (r - p2.astype(F32)).astype(BF16)
    return p1, p2, p3


def _dot3(a, b, dot=_dot):
    ah, al = _split2(a)
    bh, bl = _split2(b)
    return dot(ah, bh) + (dot(ah, bl) + dot(al, bh))


def _dot_sel(sel, b, dot=_dot):
    sel = sel.astype(BF16)
    p1, p2, p3 = _split3(b)
    return dot(sel, p1) + (dot(sel, p2) + dot(sel, p3))


def _sigmoid(x):
    return 1.0 / (1.0 + jnp.exp(-x))


def _softplus(x):
    return jnp.maximum(x, 0.0) + jnp.log(1.0 + jnp.exp(-jnp.abs(x)))


def _silu(x):
    return x * _sigmoid(x)


def _lane_bcast(x, c):
    return jnp.broadcast_to(x[:, c:c + 1], x.shape)


def _iota2(shape, dim):
    return lax.broadcasted_iota(jnp.int32, shape, dim)


def _rope_kernel(pos_ref, rows_ref, c_ref, s_ref):
    ang = pos_ref[...] * rows_ref[0:1, :]
    c_ref[...] = rows_ref[1:2, :] * jnp.cos(ang) + rows_ref[2:3, :]
    s_ref[...] = rows_ref[3:4, :] * jnp.sin(ang)


def _rope_tables(pos, rows, tm=512):
    t = pos.shape[0]
    return pl.pallas_call(
        _rope_kernel,
        grid=(t // tm,),
        in_specs=[pl.BlockSpec((tm, 1), lambda i: (i, 0)), pl.BlockSpec((8, LANES), lambda i: (0, 0))],
        out_specs=[pl.BlockSpec((tm, LANES), lambda i: (i, 0))] * 2,
        out_shape=[jax.ShapeDtypeStruct((t, LANES), F32)] * 2,
        compiler_params=_cparams(("arbitrary",)),
        name="rope_tables",
    )(pos, rows)


def _rope_rows(dim, first_lane, pad_one_lanes):
    half = dim // 2
    inv = ROPE_THETA ** (-(np.arange(0, dim, 2, dtype=np.float32) / dim))
    rows = np.zeros((8, LANES), np.float32)
    lo = slice(first_lane, first_lane + half)
    hi = slice(first_lane + half, first_lane + dim)
    rows[0, lo] = inv
    rows[0, hi] = inv
    rows[1, lo] = 1.0
    rows[1, hi] = 1.0
    rows[2, :pad_one_lanes] = 1.0
    rows[3, lo] = -1.0
    rows[3, hi] = 1.0
    return jnp.asarray(rows)


def _proj_kernel(x_ref, w_ref, *out_refs, offsets):
    xb = x_ref[...].astype(BF16)
    for o_ref, (a, b) in zip(out_refs, offsets):
        o_ref[...] = _dot(xb, w_ref[:, a:b]).astype(o_ref.dtype)


def _proj(x, w, widths, dtypes, tm=512):
    t, k = x.shape
    offs = np.concatenate([[0], np.cumsum(widths)]).tolist()
    offsets = tuple((offs[i], offs[i + 1]) for i in range(len(widths)))
    return pl.pallas_call(
        functools.partial(_proj_kernel, offsets=offsets),
        grid=(t // tm,),
        in_specs=[pl.BlockSpec((tm, k), lambda i: (i, 0)), pl.BlockSpec(w.shape, lambda i: (0, 0))],
        out_specs=[pl.BlockSpec((tm, n), lambda i: (i, 0)) for n in widths],
        out_shape=[jax.ShapeDtypeStruct((t, n), dt) for n, dt in zip(widths, dtypes)],
        compiler_params=_cparams(("arbitrary",)),
        name="in_proj",
    )(x, w)


OD_SEG = dict(mq=(0, 512), mk=(512, 1024), mv=(1024, 1536), mo=(1536, 2048), gates=(2048, 2176),
              sq=(2176, 3200), sqsw=(3200, 4224), sk=(4224, 4480), sksw=(4480, 4736), sv=(4736, 5248))
OD_COLS = 5248


def _proj_odd_kernel(x_ref, w_ref, c_ref, s_ref, mq_ref, mk_ref, mv_ref, mo_ref, mg_ref, sq_ref, sk_ref, sv_ref):
    xb = x_ref[...].astype(BF16)

    def seg(name):
        a, b = OD_SEG[name]
        return _dot(xb, w_ref[:, a:b])

    mq_ref[...] = seg("mq")
    mk_ref[...] = seg("mk")
    mv_ref[...] = seg("mv")
    mo_ref[...] = seg("mo")
    mg_ref[...] = seg("gates")
    c = c_ref[...]
    s = s_ref[...]
    c8 = jnp.concatenate([c] * SWA_H, axis=1)
    s8 = jnp.concatenate([s] * SWA_H, axis=1)
    sq_ref[...] = (seg("sq") * c8 + seg("sqsw") * s8).astype(sq_ref.dtype)
    c2 = jnp.concatenate([c] * SWA_KV, axis=1)
    s2 = jnp.concatenate([s] * SWA_KV, axis=1)
    sk_ref[...] = (seg("sk") * c2 + seg("sksw") * s2).astype(sk_ref.dtype)
    sv_ref[...] = seg("sv").astype(sv_ref.dtype)


def _proj_odd(x, w, ctab, stab, tm=256):
    t, k = x.shape
    widths = (512, 512, 512, 512, 128, SWA_H * LANES, SWA_KV * LANES, 2 * SWA_KV * LANES)
    dtypes = (F32, F32, F32, F32, F32, BF16, BF16, BF16)
    return pl.pallas_call(
        _proj_odd_kernel,
        grid=(t // tm,),
        in_specs=[pl.BlockSpec((tm, k), lambda i: (i, 0)), pl.BlockSpec(w.shape, lambda i: (0, 0)),
                  pl.BlockSpec((tm, LANES), lambda i: (i, 0)), pl.BlockSpec((tm, LANES), lambda i: (i, 0))],
        out_specs=[pl.BlockSpec((tm, n), lambda i: (i, 0)) for n in widths],
        out_shape=[jax.ShapeDtypeStruct((t, n), dt) for n, dt in zip(widths, dtypes)],
        compiler_params=_cparams(("arbitrary",)),
        name="in_proj_odd",
    )(x, w, ctab, stab)


def _rms(x, g):
    return x * lax.rsqrt(jnp.mean(x * x, axis=-1, keepdims=True) + EPS) * g


def _mla_prep_kernel(in_ref, c_ref, s_ref, qn_ref, kvn_ref, wq_ref, wkv_ref, q_ref, k_ref, v_ref):
    hw = MLA_H * LANES
    c = c_ref[...]
    s = s_ref[...]
    c8 = jnp.concatenate([c] * MLA_H, axis=1)
    s8 = jnp.concatenate([s] * MLA_H, axis=1)
    cqn = _rms(in_ref[:, 0:Q_LORA], qn_ref[...]).astype(BF16)
    qq = _dot(cqn, wq_ref[...])
    scale = (MLA_NOPE + MLA_ROPE) ** -0.5
    q_ref[...] = ((qq[:, :hw] * c8 + qq[:, hw:] * s8) * scale).astype(q_ref.dtype)
    ckvn = _rms(in_ref[:, Q_LORA:Q_LORA + KV_LORA], kvn_ref[...]).astype(BF16)
    kv = _dot(ckvn, wkv_ref[...])
    o = Q_LORA + KV_LORA
    krr = in_ref[:, o:o + LANES] * c + in_ref[:, o + LANES:o + 2 * LANES] * s
    k_ref[...] = (kv[:, :hw] + jnp.concatenate([krr] * MLA_H, axis=1)).astype(k_ref.dtype)
    v_ref[...] = kv[:, hw:].astype(v_ref.dtype)


def _mla_prep(mla_in, ctab, stab, qn, kvn, wq2, wkv2, tm=512):
    t = mla_in.shape[0]
    hw = MLA_H * LANES
    row = lambda i: (i, 0)
    fix = lambda i: (0, 0)
    return pl.pallas_call(
        _mla_prep_kernel,
        grid=(t // tm,),
        in_specs=[pl.BlockSpec((tm, mla_in.shape[1]), row), pl.BlockSpec((tm, LANES), row), pl.BlockSpec((tm, LANES), row),
                  pl.BlockSpec(qn.shape, fix), pl.BlockSpec(kvn.shape, fix),
                  pl.BlockSpec(wq2.shape, fix), pl.BlockSpec(wkv2.shape, fix)],
        out_specs=[pl.BlockSpec((tm, hw), row)] * 3,
        out_shape=[jax.ShapeDtypeStruct((t, hw), BF16)] * 3,
        compiler_params=_cparams(("arbitrary",)),
        name="mla_prep",
    )(mla_in, ctab, stab, qn, kvn, wq2, wkv2)


def _mla_attn_kernel(q_ref, k_ref, v_ref, o_ref, *, tq):
    i = pl.program_id(2)
    neg = -1e30
    lane = _iota2((tq, LANES), 1)
    ones_lane = (MLA_V, 0)

    def chunk(j, carry, masked):
        start = pl.multiple_of(j * tq, tq)
        out = []
        for hh in range(2):
            m, acc = carry[hh]
            q = q_ref[:, hh * LANES:(hh + 1) * LANES]
            kc = k_ref[pl.ds(start, tq), hh * LANES:(hh + 1) * LANES]
            vc = v_ref[pl.ds(start, tq), hh * LANES:(hh + 1) * LANES]
            vc = jnp.where(lane == ones_lane[hh], jnp.ones_like(vc), vc)
            s = _dot_nt(q, kc)
            if masked:
                s = jnp.where(_iota2(s.shape, 0) >= _iota2(s.shape, 1), s, neg)
            m_new = jnp.maximum(m, jnp.max(s, axis=-1, keepdims=True))
            alpha = jnp.exp(m - m_new)
            p = jnp.exp(s - m_new)
            acc = alpha * acc + _dot(p.astype(BF16), vc)
            out.append((m_new, acc))
        return tuple(out)

    one = (jnp.full((tq, 1), neg, F32), jnp.zeros((tq, LANES), F32))
    carry = lax.fori_loop(0, i, lambda j, c: chunk(j, c, False), (one, one))
    (_, acc0), (_, acc1) = chunk(i, carry, True)
    o0 = acc0 / _lane_bcast(acc0, ones_lane[0])
    o1 = acc1 / _lane_bcast(acc1, ones_lane[1])
    o_ref[...] = jnp.where(lane < MLA_V, o0, o1).astype(o_ref.dtype)


def _mla_attn(q, k, v, batch, seq, tq=512):
    tq = min(tq, seq)
    nq = seq // tq
    pairs = MLA_H // 2
    return pl.pallas_call(
        functools.partial(_mla_attn_kernel, tq=tq),
        grid=(batch, pairs, nq),
        in_specs=[pl.BlockSpec((tq, 2 * LANES), lambda b, p, i: (b * nq + i, p)),
                  pl.BlockSpec((seq, 2 * LANES), lambda b, p, i: (b, p)),
                  pl.BlockSpec((seq, 2 * LANES), lambda b, p, i: (b, p))],
        out_specs=pl.BlockSpec((tq, LANES), lambda b, p, i: (b * nq + i, p)),
        out_shape=jax.ShapeDtypeStruct((batch * seq, pairs * LANES), BF16),
        compiler_params=_cparams(("arbitrary", "arbitrary", "arbitrary")),
        name="mla_attn",
    )(q, k, v)


def _unit_lower_inverse_many(ns):
    c = ns[0].shape[0]
    eye = (_iota2((c, c), 0) == _iota2((c, c), 1)).astype(F32)
    xs = [-n for n in ns]
    ps = [eye + x for x in xs]
    xsplit = [_split2(x) for x in xs]
    for _ in range(int(math.log2(c)) - 1):
        xs = [_dot(xh, xh) + (_dot(xh, xl) + _dot(xl, xh)) for xh, xl in xsplit]
        xsplit = [_split2(x) for x in xs]
        psplit = [_split2(p) for p in ps]
        ps = [p + (_dot(ph, xh) + (_dot(ph, xl) + _dot(plo, xh)))
              for p, (ph, plo), (xh, xl) in zip(ps, psplit, xsplit)]
    return ps


def _gdn_kernel(qkv_ref, g_ref, z_ref, cw_ref, al_ref, dt_ref, on_ref, o_ref, ext_ref, st_ref):
    c = GDN_CHUNK
    hd = GDN_DK
    nqk = GDN_H * GDN_DK

    @pl.when(pl.program_id(1) == 0)
    def _():
        ext_ref[:, 0:8, :] = jnp.zeros((ext_ref.shape[0], 8, ext_ref.shape[2]), F32)
        st_ref[...] = jnp.zeros(st_ref.shape, F32)

    tri = (_iota2((c, c), 0) >= _iota2((c, c), 1)).astype(F32)
    row_ge = _iota2((c, c), 0) >= _iota2((c, c), 1)
    row_gt = _iota2((c, c), 0) > _iota2((c, c), 1)
    ones = jnp.ones((c, LANES), F32)
    lane = _iota2((c, LANES), 1)

    units = []
    for bb in range(qkv_ref.shape[0]):
        ext = ext_ref.at[bb]
        ext[8:8 + c, :] = qkv_ref[bb]
        conv = cw_ref[0:1, :] * ext[5:5 + c, :]
        for j in range(1, CONV_W):
            conv = conv + cw_ref[j:j + 1, :] * ext[5 + j:5 + j + c, :]
        ext[0:8, :] = ext[c:c + 8, :]
        act = _silu(conv)
        gates = g_ref[bb]
        beta_all = _sigmoid(gates)
        g_all = -jnp.exp(al_ref[...]) * _softplus(gates + dt_ref[...])
        gc_all = _dot_sel(tri, g_all)
        for h in range(GDN_H):
            q = act[:, h * hd:(h + 1) * hd]
            k = act[:, nqk + h * hd:nqk + (h + 1) * hd]
            v = act[:, 2 * nqk + h * GDN_DV:2 * nqk + (h + 1) * GDN_DV]
            q = q * lax.rsqrt(jnp.sum(q * q, axis=-1, keepdims=True) + EPS) * (GDN_DK ** -0.5)
            k = k * lax.rsqrt(jnp.sum(k * k, axis=-1, keepdims=True) + EPS)
            beta = _lane_bcast(beta_all, h)
            gcol = _lane_bcast(gc_all, GDN_H + h)
            grow = _dot_sel(ones, jnp.where(lane == GDN_H + h, gc_all, 0.0), _dot_nt)
            decay = jnp.exp(jnp.where(row_ge, gcol[:, :c] - grow, -jnp.inf))
            kb = k * beta
            lower = jnp.where(row_gt, _dot3(kb, k, _dot_nt) * decay, 0.0)
            eg = jnp.exp(gcol)
            glast = gcol[c - 1:c, :]
            units.append(dict(bb=bb, h=h, lower=lower, rhs=jnp.concatenate([v * beta, kb * eg], axis=1),
                              attn=_dot_nt(q.astype(BF16), k.astype(BF16)) * decay, qg=(q * eg).astype(BF16),
                              kg=(k * jnp.exp(glast - gcol)).astype(BF16), gl=jnp.exp(glast)))

    tinvs = _unit_lower_inverse_many([u["lower"] for u in units])
    uws = []
    for u, tinv in zip(units, tinvs):
        th, tl = _split2(tinv)
        rh, rl = _split2(u["rhs"])
        uws.append(_dot(th, rh) + (_dot(th, rl) + _dot(tl, rh)))
    states = [st_ref[u["bb"], u["h"]] for u in units]
    sbs = [s.astype(BF16) for s in states]
    vnews = [(uw[:, :GDN_DV] - _dot(uw[:, GDN_DV:].astype(BF16), sb)).astype(BF16) for uw, sb in zip(uws, sbs)]
    for u, state, sb, vnb in zip(units, states, sbs, vnews):
        bb, h = u["bb"], u["h"]
        o = _dot(u["qg"], sb) + _dot(u["attn"].astype(BF16), vnb)
        st_ref[bb, h] = state * u["gl"] + _dot_tn(u["kg"], vnb)
        o = _rms(o, on_ref[...]) * _silu(z_ref[bb, :, h * GDN_DV:(h + 1) * GDN_DV])
        o_ref[bb, :, h * GDN_DV:(h + 1) * GDN_DV] = o.astype(o_ref.dtype)


def _gdn(qkv, gates, z, conv_w, a_row, dt_row, o_norm, batch, seq):
    c = GDN_CHUNK
    nc = seq // c
    w3 = qkv.shape[1]
    wo = GDN_H * GDN_DV
    nb = SEQS_PER_STEP
    row = lambda b, i: (b, i, 0)
    fix = lambda b, i: (0, 0)
    out = pl.pallas_call(
        _gdn_kernel,
        grid=(batch // nb, nc),
        in_specs=[pl.BlockSpec((nb, c, w3), row), pl.BlockSpec((nb, c, LANES), row), pl.BlockSpec((nb, c, wo), row),
                  pl.BlockSpec(conv_w.shape, fix), pl.BlockSpec((1, LANES), fix), pl.BlockSpec((1, LANES), fix),
                  pl.BlockSpec((1, GDN_DV), fix)],
        out_specs=pl.BlockSpec((nb, c, wo), row),
        out_shape=jax.ShapeDtypeStruct((batch, seq, wo), BF16),
        scratch_shapes=[pltpu.VMEM((nb, c + 8, w3), F32), pltpu.VMEM((nb, GDN_H, GDN_DK, GDN_DV), F32)],
        compiler_params=_cparams(("arbitrary", "arbitrary")),
        name="gdn",
    )(qkv.reshape(batch, seq, w3), gates.reshape(batch, seq, LANES), z.reshape(batch, seq, wo), conv_w, a_row, dt_row, o_norm)
    return out.reshape(batch * seq, wo)


def _mlstm_kernel(q_ref, k_ref, v_ref, og_ref, g_ref, bias_ref, nrm_ref, o_ref, c_ref, n_ref, m_ref):
    @pl.when(pl.program_id(1) == 0)
    def _():
        c_ref[...] = jnp.zeros(c_ref.shape, F32)
        n_ref[...] = jnp.zeros(n_ref.shape, F32)
        m_ref[...] = jnp.zeros(m_ref.shape, F32)

    c = ML_CHUNK
    tri = (_iota2((c, c), 0) >= _iota2((c, c), 1)).astype(F32)
    row_ge = _iota2((c, c), 0) >= _iota2((c, c), 1)
    ones = jnp.ones((c, LANES), F32)
    lane = _iota2((c, LANES), 1)

    units = []
    for bb in range(q_ref.shape[0]):
        pre = g_ref[bb] + bias_ref[...]
        logf = jnp.minimum(pre, 0.0) - jnp.log(1.0 + jnp.exp(-jnp.abs(pre)))
        bcum_all = _dot_sel(tri, logf)
        for h in range(ML_H):
            q = q_ref[bb, :, h * LANES:(h + 1) * LANES]
            k = k_ref[bb, :, h * LANES:(h + 1) * LANES] * (ML_DK ** -0.5)
            units.append(dict(bb=bb, h=h, q=q, k=k, qb=q.astype(BF16), vb=v_ref[bb, :, h * ML_DV:(h + 1) * ML_DV].astype(BF16),
                              bcol=_lane_bcast(bcum_all, ML_H + h),
                              icol=_lane_bcast(pre, h),
                              col=jnp.where(lane == h, pre, 0.0) - jnp.where(lane == ML_H + h, bcum_all, 0.0),
                              m_st=m_ref[bb, h], cst=c_ref[bb, h], nst=n_ref[bb, h]))
    for u in units:
        u["row"] = _dot_sel(ones, u["col"], _dot_nt)
        u["qk"] = _dot_nt(u["qb"], u["k"].astype(BF16))
        u["qc"] = _dot(u["qb"], u["cst"].astype(BF16))
    for u in units:
        d = jnp.where(row_ge, u["bcol"][:, :c] + u["row"], -jnp.inf)
        inter = u["bcol"] + u["m_st"]
        m_t = jnp.maximum(inter, jnp.max(d, axis=-1, keepdims=True))
        u["m_t"] = m_t
        u["w_inter"] = jnp.exp(inter - m_t)
        u["p"] = jnp.exp(d - m_t[:, :c]) * u["qk"]
        u["pv"] = _dot(u["p"].astype(BF16), u["vb"])
        b_end = u["bcol"][c - 1:c, :]
        a = b_end - u["bcol"] + u["icol"]
        m_new = jnp.maximum(b_end + u["m_st"], jnp.max(a, axis=0, keepdims=True))
        u["m_new"] = m_new
        u["keep"] = jnp.exp(b_end + u["m_st"] - m_new)
        u["ks"] = u["k"] * jnp.exp(a - m_new)
        u["kv"] = _dot_tn(u["ks"].astype(BF16), u["vb"])
    for u in units:
        bb, h = u["bb"], u["h"]
        num = u["w_inter"] * u["qc"] + u["pv"]
        den = (u["w_inter"] * jnp.sum(u["q"] * u["nst"], axis=-1, keepdims=True)
               + jnp.sum(u["p"], axis=-1, keepdims=True))
        hc = num / jnp.maximum(jnp.abs(den), jnp.exp(-u["m_t"]))
        c_ref[bb, h] = u["cst"] * u["keep"] + u["kv"]
        n_ref[bb, h] = u["nst"] * u["keep"] + jnp.sum(u["ks"], axis=0, keepdims=True)
        m_ref[bb, h] = u["m_new"]
        hn = (_rms(hc, nrm_ref[:, h * ML_DV:(h + 1) * ML_DV])
              * _sigmoid(og_ref[bb, :, h * ML_DV:(h + 1) * ML_DV]))
        o_ref[bb, :, h * ML_DV:(h + 1) * ML_DV] = hn.astype(o_ref.dtype)


def _mlstm(mq, mk, mv, mo, gates, bias_row, norm_row, batch, seq):
    c = ML_CHUNK
    nc = seq // c
    nb = SEQS_PER_STEP
    row = lambda b, i: (b, i, 0)
    fix = lambda b, i: (0, 0)
    wide = ML_H * LANES
    r3 = lambda a: a.reshape(batch, seq, a.shape[-1])
    out = pl.pallas_call(
        _mlstm_kernel,
        grid=(batch // nb, nc),
        in_specs=[pl.BlockSpec((nb, c, wide), row), pl.BlockSpec((nb, c, wide), row), pl.BlockSpec((nb, c, wide), row),
                  pl.BlockSpec((nb, c, wide), row), pl.BlockSpec((nb, c, LANES), row),
                  pl.BlockSpec((1, LANES), fix), pl.BlockSpec((1, wide), fix)],
        out_specs=pl.BlockSpec((nb, c, wide), row),
        out_shape=jax.ShapeDtypeStruct((batch, seq, wide), BF16),
        scratch_shapes=[pltpu.VMEM((nb, ML_H, LANES, ML_DV), F32), pltpu.VMEM((nb, ML_H, 1, LANES), F32),
                        pltpu.VMEM((nb, ML_H, 1, LANES), F32)],
        compiler_params=_cparams(("arbitrary", "arbitrary")),
        name="mlstm",
    )(r3(mq), r3(mk), r3(mv), r3(mo), r3(gates), bias_row, norm_row)
    return out.reshape(batch * seq, wide)


def _swa_kernel(q_ref, kc_ref, kp_ref, vc_ref, vp_ref, sink_ref, o_ref):
    w = WINDOW
    n = pl.program_id(1)
    scale = SWA_D ** -0.5
    qi = _iota2((w, w), 0)
    kj = _iota2((w, w), 1)
    mask_c = kj <= qi
    mask_p = jnp.logical_and(kj > qi, n > 0)
    grp = SWA_H // SWA_KV
    neg = -1e30
    units = [(bb, h) for bb in range(q_ref.shape[0]) for h in range(SWA_H)]
    scores = []
    for bb, h in units:
        g = h // grp
        q = q_ref[bb, :, h * LANES:(h + 1) * LANES]
        scores.append((_dot_nt(q, kc_ref[bb, :, g * LANES:(g + 1) * LANES]),
                       _dot_nt(q, kp_ref[bb, :, g * LANES:(g + 1) * LANES])))
    masked, tops, exps, dens, probs = [], [], [], [], {}
    for sc, sp in scores:
        masked.append((jnp.where(mask_c, sc * scale, neg), jnp.where(mask_p, sp * scale, neg)))
    for (bb, h), (s_c, s_p) in zip(units, masked):
        tops.append(jnp.maximum(jnp.max(jnp.maximum(s_c, s_p), axis=-1, keepdims=True), sink_ref[:, h:h + 1]))
    for (s_c, s_p), m in zip(masked, tops):
        exps.append((jnp.where(mask_c, jnp.exp(s_c - m), 0.0), jnp.where(mask_p, jnp.exp(s_p - m), 0.0)))
    ones_b = jnp.ones((w, LANES), BF16)
    for (bb, h), (p_c, p_p), m in zip(units, exps, tops):
        p_c, p_p = p_c.astype(BF16), p_p.astype(BF16)
        probs[bb, h] = (p_c, p_p)
        dens.append(_dot(p_c, ones_b) + _dot(p_p, ones_b) + jnp.exp(sink_ref[:, h:h + 1] - m))
    inv = {u: 1.0 / den for u, den in zip(units, dens)}
    for bb in range(q_ref.shape[0]):
        for pair in range(SWA_H // 2):
            acc = None
            for sub in range(2):
                h = 2 * pair + sub
                vcol = (2 * (h // grp) + sub) * LANES
                p_c, p_p = probs[bb, h]
                part = (_dot(p_c, vc_ref[bb, :, vcol:vcol + LANES]) + _dot(p_p, vp_ref[bb, :, vcol:vcol + LANES])) * inv[bb, h]
                acc = part if acc is None else acc + part
            o_ref[bb, :, pair * LANES:(pair + 1) * LANES] = acc.astype(o_ref.dtype)


def _swa(sq, sk, sv, sinks_row, batch, seq):
    w = WINDOW
    nb = seq // w
    ns = SEQS_PER_STEP
    wo = SWA_H * SWA_D
    cur = lambda b, n: (b, n, 0)
    prev = lambda b, n: (b, jnp.maximum(n - 1, 0), 0)
    r3 = lambda a: a.reshape(batch, seq, a.shape[-1])
    q3, k3, v3 = r3(sq), r3(sk), r3(sv)
    out = pl.pallas_call(
        _swa_kernel,
        grid=(batch // ns, nb),
        in_specs=[pl.BlockSpec((ns, w, sq.shape[1]), cur),
                  pl.BlockSpec((ns, w, sk.shape[1]), cur), pl.BlockSpec((ns, w, sk.shape[1]), prev),
                  pl.BlockSpec((ns, w, sv.shape[1]), cur), pl.BlockSpec((ns, w, sv.shape[1]), prev),
                  pl.BlockSpec((1, LANES), lambda b, n: (0, 0))],
        out_specs=pl.BlockSpec((ns, w, wo), cur),
        out_shape=jax.ShapeDtypeStruct((batch, seq, wo), BF16),
        compiler_params=_cparams(("arbitrary", "arbitrary")),
        name="swa",
    )(q3, k3, k3, v3, v3, sinks_row)
    return out.reshape(batch * seq, wo)


def _layer_norm(h, g, b):
    mu = jnp.mean(h, axis=-1, keepdims=True)
    d = h - mu
    var = jnp.mean(d * d, axis=-1, keepdims=True)
    return d * lax.rsqrt(var + LN_EPS) * g + b


def _outproj_kernel(x_ref, a1_ref, a2_ref, w_ref, g_ref, b_ref, o_ref, op_ref):
    k1 = a1_ref.shape[1]
    y = _dot(a1_ref[...].astype(BF16), w_ref[0:k1, :]) + _dot(a2_ref[...].astype(BF16), w_ref[k1:, :])
    h = _layer_norm(DN_ALPHA * x_ref[...] + y, g_ref[...], b_ref[...])
    o_ref[...] = h
    op_ref[...] = _pack_pairs(h)


def _outproj_ln(x, a1, a2, w, g, b, tm=512):
    t, d = x.shape
    row = lambda i: (i, 0)
    fix = lambda i: (0, 0)
    return pl.pallas_call(
        _outproj_kernel,
        grid=(t // tm,),
        in_specs=[pl.BlockSpec((tm, d), row), pl.BlockSpec((tm, a1.shape[1]), row), pl.BlockSpec((tm, a2.shape[1]), row),
                  pl.BlockSpec(w.shape, fix), pl.BlockSpec((1, d), fix), pl.BlockSpec((1, d), fix)],
        out_specs=[pl.BlockSpec((tm, d), row), pl.BlockSpec((tm, d // 2), row)],
        out_shape=[jax.ShapeDtypeStruct((t, d), F32), jax.ShapeDtypeStruct((t, d // 2), jnp.uint32)],
        compiler_params=_cparams(("arbitrary",)),
        name="outproj_ln",
    )(x, a1, a2, w, g, b)


def _first_index(x, m, iota_f, sentinel):
    return jnp.min(jnp.where(x == m, iota_f, sentinel), axis=0, keepdims=True)


def _router_kernel(x_ref, wt_ref, bias_ref, idx_ref, gate_ref, rank_ref, cnt_ref, carry_ref):
    tm = x_ref.shape[0]
    e = N_EXPERTS
    gs = e // N_GROUPS
    ninf = -jnp.inf

    @pl.when(pl.program_id(0) == 0)
    def _():
        carry_ref[...] = jnp.zeros(carry_ref.shape, F32)

    logits = _dot3(wt_ref[...], x_ref[...], _dot_nt)
    scores = _sigmoid(logits)
    sel = scores + bias_ref[:, 0:1]

    sub_f = _iota2((gs, tm), 0).astype(F32)
    gscore = []
    for g in range(N_GROUPS):
        blk = sel[g * gs:(g + 1) * gs, :]
        m1 = jnp.max(blk, axis=0, keepdims=True)
        i1 = _first_index(blk, m1, sub_f, float(gs))
        m2 = jnp.max(jnp.where(sub_f == i1, ninf, blk), axis=0, keepdims=True)
        gscore.append(m1 + m2)
    gsc = jnp.concatenate(gscore, axis=0)
    grp_f = _iota2((N_GROUPS, tm), 0).astype(F32)
    gmask = jnp.zeros((N_GROUPS, tm), F32)
    for _ in range(TOPK_GROUPS):
        m = jnp.max(gsc, axis=0, keepdims=True)
        gi = _first_index(gsc, m, grp_f, float(N_GROUPS))
        hit = grp_f == gi
        gmask = jnp.where(hit, 1.0, gmask)
        gsc = jnp.where(hit, ninf, gsc)
    masked = jnp.concatenate(
        [jnp.where(gmask[g:g + 1, :] > 0.0, sel[g * gs:(g + 1) * gs, :], ninf) for g in range(N_GROUPS)], axis=0)

    exp_f = _iota2((e, tm), 0).astype(F32)
    chosen = jnp.zeros((e, tm), F32)
    idxs, gates = [], []
    for _ in range(TOP_K):
        m = jnp.max(masked, axis=0, keepdims=True)
        ei = _first_index(masked, m, exp_f, float(e))
        hit = exp_f == ei
        idxs.append(ei)
        gates.append(jnp.sum(jnp.where(hit, scores, 0.0), axis=0, keepdims=True))
        chosen = jnp.where(hit, 1.0, chosen)
        masked = jnp.where(hit, ninf, masked)
    gate = jnp.concatenate(gates, axis=0)
    gate = gate / jnp.sum(gate, axis=0, keepdims=True) * ROUTED_SCALE
    idx_f = jnp.concatenate(idxs, axis=0)

    upper = (_iota2((tm, tm), 0) < _iota2((tm, tm), 1)).astype(BF16)
    before = _dot(chosen.astype(BF16), upper) + carry_ref[...][:, 0:1]
    ranks = [jnp.sum(jnp.where(exp_f == idxs[k], before, 0.0), axis=0, keepdims=True) for k in range(TOP_K)]
    carry_ref[...] = carry_ref[...] + jnp.sum(chosen, axis=1, keepdims=True)

    idx_ref[...] = idx_f.astype(jnp.int32)
    gate_ref[...] = gate
    rank_ref[...] = jnp.concatenate(ranks, axis=0).astype(jnp.int32)
    cnt_ref[...] = carry_ref[...]


def _router(x, wt, bias_col, tm=512):
    t, d = x.shape
    col = lambda i: (0, i)
    fix = lambda i: (0, 0)
    return pl.pallas_call(
        _router_kernel,
        grid=(t // tm,),
        in_specs=[pl.BlockSpec((tm, d), lambda i: (i, 0)), pl.BlockSpec(wt.shape, fix), pl.BlockSpec((N_EXPERTS, LANES), fix)],
        out_specs=[pl.BlockSpec((TOP_K, tm), col), pl.BlockSpec((TOP_K, tm), col), pl.BlockSpec((TOP_K, tm), col),
                   pl.BlockSpec((N_EXPERTS, LANES), fix)],
        out_shape=[jax.ShapeDtypeStruct((TOP_K, t), jnp.int32), jax.ShapeDtypeStruct((TOP_K, t), F32),
                   jax.ShapeDtypeStruct((TOP_K, t), jnp.int32), jax.ShapeDtypeStruct((N_EXPERTS, LANES), F32)],
        scratch_shapes=[pltpu.VMEM((N_EXPERTS, LANES), F32)],
        compiler_params=_cparams(("arbitrary",)),
        name="router",
    )(x, wt, bias_col)


def _dest_kernel(idx_ref, rank_ref, start_ref, dest_ref):
    tm = idx_ref.shape[1]
    exp_i = _iota2((N_EXPERTS, tm), 0)
    start = start_ref[:, 0:1]
    rows = [jnp.sum(jnp.where(exp_i == idx_ref[s:s + 1, :], start, 0.0), axis=0, keepdims=True) for s in range(TOP_K)]
    dest_ref[...] = jnp.concatenate(rows, axis=0).astype(jnp.int32) + rank_ref[...]


def _dest_rows(idx, rank, start_col, tm=2048):
    t = idx.shape[1]
    tm = min(tm, t)
    col = lambda i: (0, i)
    return pl.pallas_call(
        _dest_kernel,
        grid=(t // tm,),
        in_specs=[pl.BlockSpec((TOP_K, tm), col), pl.BlockSpec((TOP_K, tm), col),
                  pl.BlockSpec((N_EXPERTS, LANES), lambda i: (0, 0))],
        out_specs=pl.BlockSpec((TOP_K, tm), col),
        out_shape=jax.ShapeDtypeStruct((TOP_K, t), jnp.int32),
        compiler_params=_cparams(("arbitrary",)),
        name="moe_dest",
    )(idx, rank, start_col)


def _pack_pairs(x):
    n = x.shape[1] // 2
    hi = lax.bitcast_convert_type(x[:, :n].astype(BF16).astype(F32), jnp.uint32)
    lo = lax.bitcast_convert_type(x[:, n:].astype(BF16).astype(F32), jnp.uint32)
    return hi | (lo >> 16)


def _unpack_pairs(w):
    hi = lax.bitcast_convert_type(w & jnp.uint32(0xFFFF0000), F32)
    lo = lax.bitcast_convert_type(w << 16, F32)
    return hi, lo


def _sc_scatter_rows(xp, dest, rows, chunk=LANES):
    t, width = xp.shape
    info = plsc.get_sparse_core_info()
    ncores, nsub = info.num_cores, info.num_subcores
    per_worker = t // (ncores * nsub)
    nchunk = per_worker // chunk
    mesh = plsc.VectorSubcoreMesh(core_axis_name="c", subcore_axis_name="s")

    @functools.partial(
        pl.kernel, mesh=mesh,
        out_type=jax.ShapeDtypeStruct((rows, width), xp.dtype),
        scratch_types=[pltpu.VMEM((TOP_K, chunk), jnp.int32), pltpu.VMEM((chunk, width), xp.dtype), pltpu.SemaphoreType.DMA],
    )
    def scatter(xp_hbm, dest_hbm, out_hbm, idx_v, rows_v, sem):
        base = (lax.axis_index("s") * ncores + lax.axis_index("c")) * per_worker

        @pl.loop(0, nchunk)
        def _(i):
            off = pl.multiple_of(base + i * chunk, chunk)
            pltpu.sync_copy(dest_hbm.at[:, pl.ds(off, chunk)], idx_v)
            pltpu.sync_copy(xp_hbm.at[pl.ds(off, chunk)], rows_v)
            copies = [pltpu.async_copy(rows_v, out_hbm.at[idx_v.at[s]], sem) for s in range(TOP_K)]
            for cp in copies:
                cp.wait()

    return scatter(xp, dest)


def _experts_kernel(be_ref, nu_ref, nv_ref, xs_ref, wg_ref, wu_ref, wd_ref, ys_ref, wgb_ref, wub_ref, wdb_ref):
    i = pl.program_id(0)

    @pl.when(jnp.logical_or(i == 0, be_ref[i] != be_ref[jnp.maximum(i - 1, 0)]))
    def _():
        wgb_ref[...] = wg_ref[0, 0].astype(BF16)
        wub_ref[...] = wu_ref[0, 0].astype(BF16)
        wdb_ref[...] = wd_ref[0, 0].astype(BF16)

    @pl.when(i < nu_ref[0])
    def _():
        half = xs_ref.shape[1]
        live = _iota2((xs_ref.shape[0], 1), 0) < nv_ref[i]
        xa, xb = _unpack_pairs(jnp.where(live, xs_ref[...], jnp.uint32(0)))
        xa = xa.astype(BF16)
        xb = xb.astype(BF16)
        gate = _dot(xa, wgb_ref[:half, :]) + _dot(xb, wgb_ref[half:, :])
        up = _dot(xa, wub_ref[:half, :]) + _dot(xb, wub_ref[half:, :])
        h = _silu(gate) * up
        ys_ref[...] = _pack_pairs(_dot(h.astype(BF16), wdb_ref[...]))

    @pl.when(i >= nu_ref[0])
    def _():
        ys_ref[...] = jnp.zeros(ys_ref.shape, ys_ref.dtype)


def _experts(block_e, n_used, n_valid, xs, wg, wu, wd, layer):
    rows, half = xs.shape
    d = 2 * half
    nb = rows // EXPERT_BLOCK
    blk = lambda i, be, nu, nv: (jnp.minimum(i, nu[0] - 1), 0)
    wsel = lambda i, be, nu, nv: (layer, be[i], 0, 0)
    return pl.pallas_call(
        _experts_kernel,
        grid_spec=pltpu.PrefetchScalarGridSpec(
            num_scalar_prefetch=3,
            grid=(nb,),
            in_specs=[pl.BlockSpec((EXPERT_BLOCK, half), blk),
                      pl.BlockSpec((1, 1, d, D_EXPERT), wsel), pl.BlockSpec((1, 1, d, D_EXPERT), wsel),
                      pl.BlockSpec((1, 1, D_EXPERT, d), wsel)],
            out_specs=pl.BlockSpec((EXPERT_BLOCK, half), lambda i, be, nu, nv: (i, 0)),
            scratch_shapes=[pltpu.VMEM((d, D_EXPERT), BF16), pltpu.VMEM((d, D_EXPERT), BF16),
                            pltpu.VMEM((D_EXPERT, d), BF16)],
        ),
        out_shape=jax.ShapeDtypeStruct((rows, half), jnp.uint32),
        compiler_params=_cparams(("arbitrary",)),
        name="moe_experts",
    )(block_e, n_used, n_valid, xs, wg, wu, wd)


def _sc_gather_rows(table, idx, chunk=SC_CHUNK):
    n = idx.shape[0]
    width = table.shape[1]
    info = plsc.get_sparse_core_info()
    ncores, nsub = info.num_cores, info.num_subcores
    per_worker = n // (ncores * nsub)
    nchunk = per_worker // chunk
    mesh = plsc.VectorSubcoreMesh(core_axis_name="c", subcore_axis_name="s")

    @functools.partial(
        pl.kernel, mesh=mesh,
        out_type=jax.ShapeDtypeStruct((n, width), table.dtype),
        scratch_types=[pltpu.VMEM((nchunk, chunk), jnp.int32), pltpu.VMEM((2, chunk, width), table.dtype),
                       pltpu.SemaphoreType.DMA((2,)), pltpu.SemaphoreType.DMA((2,))],
    )
    def gather(table_hbm, idx_hbm, out_hbm, idx_v, rows_v, gsem, wsem):
        wid = lax.axis_index("s") * ncores + lax.axis_index("c")
        base = wid * per_worker
        pltpu.sync_copy(idx_hbm.at[pl.ds(wid * nchunk, nchunk)], idx_v)

        def fetch(j, b):
            return pltpu.make_async_copy(table_hbm.at[idx_v.at[j]], rows_v.at[b], gsem.at[b])

        def flush(j, b):
            off = pl.multiple_of(base + j * chunk, chunk)
            return pltpu.make_async_copy(rows_v.at[b], out_hbm.at[pl.ds(off, chunk)], wsem.at[b])

        fetch(0, 0).start()

        @pl.loop(0, nchunk, step=2)
        def _(i):
            for b in range(2):
                j = i + b
                fetch(j, b).wait()

                @pl.when(j + 1 < nchunk)
                def _():
                    @pl.when(j >= 1)
                    def _():
                        flush(j - 1, 1 - b).wait()

                    fetch(j + 1, 1 - b).start()

                flush(j, b).start()

        flush(nchunk - 2, 0).wait()
        flush(nchunk - 1, 1).wait()

    return gather(table, idx.reshape(n // chunk, chunk))


def _combine_kernel(x_ref, gate_ref, rows_ref, sg_ref, su_ref, sd_ref, g_ref, b_ref, o_ref):
    x = x_ref[...]
    xb = x.astype(BF16)
    hs = _silu(_dot(xb, sg_ref[...])) * _dot(xb, su_ref[...])
    ff = _dot(hs.astype(BF16), sd_ref[...])
    gate = gate_ref[...]
    half = rows_ref.shape[2]
    ya = ff[:, :half]
    yb = ff[:, half:]
    for s in range(TOP_K):
        a, b = _unpack_pairs(rows_ref[s])
        ya = ya + gate[:, s:s + 1] * a
        yb = yb + gate[:, s:s + 1] * b
    ff = jnp.concatenate([ya, yb], axis=1)
    o_ref[...] = _layer_norm(DN_ALPHA * x + ff, g_ref[...], b_ref[...])


def _combine(x, gate_t, rows, sg, su, sd, g, b, tm=512):
    t, d = x.shape
    row = lambda i: (i, 0)
    fix = lambda i: (0, 0)
    return pl.pallas_call(
        _combine_kernel,
        grid=(t // tm,),
        in_specs=[pl.BlockSpec((tm, d), row), pl.BlockSpec((tm, TOP_K), row),
                  pl.BlockSpec((TOP_K, tm, d // 2), lambda i: (0, i, 0)),
                  pl.BlockSpec(sg.shape, fix), pl.BlockSpec(su.shape, fix), pl.BlockSpec(sd.shape, fix),
                  pl.BlockSpec((1, d), fix), pl.BlockSpec((1, d), fix)],
        out_specs=pl.BlockSpec((tm, d), row),
        out_shape=jax.ShapeDtypeStruct((t, d), F32),
        compiler_params=_cparams(("arbitrary",)),
        name="moe_combine",
    )(x, gate_t, rows, sg, su, sd, g, b)


def _take_cols(w, idx):
    wz = jnp.concatenate([w, jnp.zeros((w.shape[0], 1), w.dtype)], axis=1)
    idx = np.where(np.asarray(idx) < 0, w.shape[1], np.asarray(idx))
    return jnp.take(wz, jnp.asarray(idx, jnp.int32), axis=1)


def _pad_lane_row(v, first_lane, width=LANES):
    out = jnp.zeros((1, width), F32)
    return lax.dynamic_update_slice(out, v.reshape(1, -1).astype(F32), (0, first_lane))


def _even_in_cols():
    z = lambda n: -np.ones(n, int)
    kr0 = Q_LORA + KV_LORA
    half = MLA_ROPE // 2
    cols = [np.arange(0, Q_LORA), np.arange(Q_LORA, Q_LORA + KV_LORA),
            z(64), np.arange(kr0, kr0 + MLA_ROPE), z(32),
            z(64), np.arange(kr0 + half, kr0 + MLA_ROPE), np.arange(kr0, kr0 + half), z(32)]
    g0 = kr0 + MLA_ROPE
    nqk = GDN_H * GDN_DK
    cols.append(np.arange(g0, g0 + 3 * nqk))
    zoff = g0 + 3 * nqk + 2 * GDN_H
    cols.append(np.arange(zoff, zoff + GDN_H * GDN_DV))
    cols += [np.arange(g0 + 3 * nqk, g0 + 3 * nqk + 2 * GDN_H), z(LANES - 2 * GDN_H)]
    return np.concatenate(cols)


EV_WIDTHS = (Q_LORA + KV_LORA + 2 * LANES, 3 * GDN_H * GDN_DK, GDN_H * GDN_DV, LANES)


def _mla_q_cols():
    per = MLA_NOPE + MLA_ROPE
    half = MLA_ROPE // 2
    main, sw = [], []
    for h in range(MLA_H):
        b = h * per
        main += [np.arange(b, b + per), -np.ones(LANES - per, int)]
        sw += [-np.ones(MLA_NOPE, int), np.arange(b + MLA_NOPE + half, b + per), np.arange(b + MLA_NOPE, b + MLA_NOPE + half),
               -np.ones(LANES - per, int)]
    return np.concatenate(main + sw)


def _mla_kv_cols():
    per = MLA_NOPE + MLA_V
    kc, vc = [], []
    for h in range(MLA_H):
        b = h * per
        kc += [np.arange(b, b + MLA_NOPE), -np.ones(LANES - MLA_NOPE, int)]
        vv = np.arange(b + MLA_NOPE, b + per)
        pad = -np.ones(LANES - MLA_V, int)
        vc += [vv, pad] if h % 2 == 0 else [pad, vv]
    return np.concatenate(kc + vc)


def _odd_in_cols():
    z = lambda n: -np.ones(n, int)
    o = 0
    cols = []
    mq0, mk0 = 0, ML_H * ML_DK
    for base in (mq0, mk0):
        for h in range(ML_H):
            cols += [np.arange(base + h * ML_DK, base + (h + 1) * ML_DK), z(LANES - ML_DK)]
    mv0 = 2 * ML_H * ML_DK
    cols.append(np.arange(mv0, mv0 + ML_H * ML_DV))
    mi0 = mv0 + ML_H * ML_DV
    mo0 = mi0 + 2 * ML_H
    cols.append(np.arange(mo0, mo0 + ML_H * ML_DV))
    cols += [np.arange(mi0, mi0 + 2 * ML_H), z(LANES - 2 * ML_H)]
    sq0 = mo0 + ML_H * ML_DV
    sk0 = sq0 + SWA_H * SWA_D
    sv0 = sk0 + SWA_KV * SWA_D
    half = SWA_D // 2

    def heads(base, n, swapped):
        out = []
        for h in range(n):
            b = base + h * SWA_D
            if swapped:
                out += [np.arange(b + half, b + SWA_D), np.arange(b, b + half), z(LANES - SWA_D)]
            else:
                out += [np.arange(b, b + SWA_D), z(LANES - SWA_D)]
        return out

    cols += heads(sq0, SWA_H, False) + heads(sq0, SWA_H, True) + heads(sk0, SWA_KV, False) + heads(sk0, SWA_KV, True)
    for g in range(SWA_KV):
        vv = np.arange(sv0 + g * SWA_D, sv0 + (g + 1) * SWA_D)
        cols += [vv, z(LANES - SWA_D), z(LANES - SWA_D), vv]
    return np.concatenate(cols)


def _even_mixer(x, tabs, w_in, q_norm, w_qb, kv_norm, w_kvb, conv_w, a_log, dt_bias, o_norm, batch, seq):
    ctab, stab = tabs
    w = _take_cols(w_in, _even_in_cols()).astype(BF16)
    mla_in, qkv, z, gates = _proj(x, w, EV_WIDTHS, (F32, F32, F32, F32))
    wq2 = _take_cols(w_qb, _mla_q_cols()).astype(BF16)
    wkv2 = _take_cols(w_kvb, _mla_kv_cols()).astype(BF16)
    q, k, v = _mla_prep(mla_in, ctab, stab, q_norm.reshape(1, -1), kv_norm.reshape(1, -1), wq2, wkv2)
    o_a = _mla_attn(q, k, v, batch, seq)
    o_b = _gdn(qkv, gates, z, conv_w, _pad_lane_row(a_log, GDN_H), _pad_lane_row(dt_bias, GDN_H),
               o_norm.reshape(1, -1), batch, seq)
    return o_a, o_b


def _odd_mixer(x, tabs, w_in, b_i, b_f, ml_norm, sinks, batch, seq):
    ctab, stab = tabs
    w = _take_cols(w_in, _odd_in_cols()).astype(BF16)
    mq, mk, mv, mo, mg, sq, sk, sv = _proj_odd(x, w, ctab, stab)
    bias_row = _pad_lane_row(jnp.concatenate([b_i, b_f]), 0)
    o_c = _mlstm(mq, mk, mv, mo, mg, bias_row, ml_norm.reshape(1, -1), batch, seq)
    o_d = _swa(sq, sk, sv, _pad_lane_row(sinks, 0), batch, seq)
    return o_c, o_d


def _moe(x, xp, router_w, router_b, w_gate, w_up, w_down, layer, s_gate, s_up, s_down, ln_g, ln_b):
    t, d = x.shape
    bias_col = jnp.broadcast_to(router_b.reshape(-1, 1).astype(F32), (N_EXPERTS, LANES))
    idx, gate, rank, cnt = _router(x, router_w.T, bias_col)
    counts = cnt[:, 0].astype(jnp.int32)
    padded = (counts + EXPERT_BLOCK - 1) // EXPERT_BLOCK * EXPERT_BLOCK
    pad_end = jnp.cumsum(padded)
    pad_start = pad_end - padded
    start_col = jnp.broadcast_to(pad_start.astype(F32).reshape(-1, 1), (N_EXPERTS, LANES))
    dest = _dest_rows(idx, rank, start_col)
    n_blocks = t * TOP_K // EXPERT_BLOCK + N_EXPERTS
    rows = n_blocks * EXPERT_BLOCK
    block_row = jnp.arange(n_blocks, dtype=jnp.int32) * EXPERT_BLOCK
    block_e = jnp.minimum(jnp.sum((pad_end[None, :] <= block_row[:, None]).astype(jnp.int32), axis=1), N_EXPERTS - 1)
    n_used = (pad_end[-1:] // EXPERT_BLOCK).astype(jnp.int32)
    live_end = jnp.sum(jnp.where(block_e[:, None] == jnp.arange(N_EXPERTS, dtype=jnp.int32)[None, :],
                                 (pad_start + counts)[None, :], 0), axis=1)
    n_valid = jnp.clip(live_end - block_row, 0, EXPERT_BLOCK).astype(jnp.int32)
    xs = _sc_scatter_rows(xp, dest, rows)
    ys = _experts(block_e, n_used, n_valid, xs, w_gate, w_up, w_down, layer)
    picked = _sc_gather_rows(ys, dest.reshape(-1)).reshape(TOP_K, t, d // 2)
    return _combine(x, gate.T, picked, s_gate.astype(BF16), s_up.astype(BF16), s_down.astype(BF16),
                    ln_g.reshape(1, -1), ln_b.reshape(1, -1))


def kernel(x, positions, ev_w_in, mla_q_norm, mla_w_qb, mla_kv_norm, mla_w_kvb, gdn_conv, gdn_a_log, gdn_dt_bias, gdn_norm, ev_w_out, od_w_in, mlstm_b_i, mlstm_b_f, mlstm_norm, swa_sinks, od_w_out, ln1_g, ln1_b, router_w, router_b, moe_w_gate, moe_w_up, moe_w_down, shared_w_gate, shared_w_up, shared_w_down, ln2_g, ln2_b):
    batch, seq, d = x.shape
    t = batch * seq
    pos = positions.reshape(t, 1).astype(F32)
    tabs_m = _rope_tables(pos, _rope_rows(MLA_ROPE, MLA_NOPE, MLA_NOPE))
    tabs_s = _rope_tables(pos, _rope_rows(SWA_D, 0, 0))
    h = x.reshape(t, d)
    for layer in range(DEPTH):
        j = layer // 2
        if layer % 2 == 0:
            a1, a2 = _even_mixer(h, tabs_m, ev_w_in[j], mla_q_norm[j], mla_w_qb[j], mla_kv_norm[j], mla_w_kvb[j],
                                 gdn_conv[j], gdn_a_log[j], gdn_dt_bias[j], gdn_norm[j], batch, seq)
            w_out = ev_w_out[j]
        else:
            a1, a2 = _odd_mixer(h, tabs_s, od_w_in[j], mlstm_b_i[j], mlstm_b_f[j], mlstm_norm[j], swa_sinks[j], batch, seq)
            w_out = od_w_out[j]
        h, hp = _outproj_ln(h, a1, a2, w_out.astype(BF16), ln1_g[layer].reshape(1, -1), ln1_b[layer].reshape(1, -1))
        h = _moe(h, hp, router_w[layer], router_b[layer], moe_w_gate, moe_w_up, moe_w_down, layer,
                 shared_w_gate[layer], shared_w_up[layer], shared_w_down[layer], ln2_g[layer], ln2_b[layer])
    return h.reshape(batch, seq, d)
```

```python
import functools
import math

import numpy as np
import jax
import jax.numpy as jnp
from jax import lax
from jax.experimental import pallas as pl
from jax.experimental.pallas import tpu as pltpu
from jax.experimental.pallas import tpu_sc as plsc

F32 = jnp.float32
BF16 = jnp.bfloat16
HI = lax.Precision.HIGHEST

D_MODEL = 1024
DEPTH = 4
ROPE_THETA = 10000.0
EPS = 1e-6
LN_EPS = 1e-5
MLA_H, MLA_NOPE, MLA_ROPE, MLA_V = 8, 64, 32, 64
Q_LORA, KV_LORA = 256, 128
GDN_H, GDN_DK, GDN_DV, CONV_W, GDN_CHUNK = 4, 128, 128, 4, 64
ML_H, ML_DK, ML_DV, ML_CHUNK = 4, 64, 128, 64
SWA_H, SWA_KV, SWA_D, WINDOW = 8, 2, 64, 128
N_EXPERTS, N_GROUPS, TOPK_GROUPS, TOP_K = 64, 8, 4, 8
D_EXPERT, D_SHARED = 256, 256
ROUTED_SCALE = 2.5
DN_ALPHA = (2 * DEPTH) ** 0.25

LANES = 128
V7X_VMEM_BYTES = 64 * 1024 * 1024
VMEM_LIMIT = 48 * 1024 * 1024

EXPERT_BLOCK = 512
EXPERT_SUBBLOCKS = 2
COMBINE_PARTS = 4
SEQS_PER_STEP = 2
DMA_GROUP = 2
SC_CHUNK = 64


def _cparams(sem, vmem=VMEM_LIMIT):
    return pltpu.CompilerParams(dimension_semantics=sem, vmem_limit_bytes=vmem)


def _dot(a, b, precision=None):
    return jnp.dot(a, b, preferred_element_type=F32, precision=precision)


def _dot_nt(a, b, precision=None):
    return lax.dot_general(a, b, (((1,), (1,)), ((), ())), preferred_element_type=F32, precision=precision)


def _dot_tn(a, b, precision=None):
    return lax.dot_general(a, b, (((0,), (0,)), ((), ())), preferred_element_type=F32, precision=precision)


def _split2(a):
    hi = a.astype(BF16)
    lo = (a - hi.astype(F32)).astype(BF16)
    return hi, lo


def _split3(a):
    p1 = a.astype(BF16)
    r = a - p1.astype(F32)
    p2 = r.astype(BF16)
    p3 = (r - p2.astype(F32)).astype(BF16)
    return p1, p2, p3


def _dot3(a, b, dot=_dot):
    ah, al = _split2(a)
    bh, bl = _split2(b)
    return dot(ah, bh) + (dot(ah, bl) + dot(al, bh))


def _dot_sel(sel, b, dot=_dot):
    sel = sel.astype(BF16)
    p1, p2, p3 = _split3(b)
    return dot(sel, p1) + (dot(sel, p2) + dot(sel, p3))


def _sigmoid(x):
    return 1.0 / (1.0 + jnp.exp(-x))


def _softplus(x):
    return jnp.maximum(x, 0.0) + jnp.log(1.0 + jnp.exp(-jnp.abs(x)))


def _silu(x):
    return x * _sigmoid(x)


def _lane_bcast(x, c):
    return jnp.broadcast_to(x[:, c:c + 1], x.shape)


def _iota2(shape, dim):
    return lax.broadcasted_iota(jnp.int32, shape, dim)


def _rope_kernel(pos_ref, rows_ref, c_ref, s_ref):
    ang = pos_ref[...] * rows_ref[0:1, :]
    c_ref[...] = rows_ref[1:2, :] * jnp.cos(ang) + rows_ref[2:3, :]
    s_ref[...] = rows_ref[3:4, :] * jnp.sin(ang)


def _rope_tables(pos, rows, tm=512):
    t = pos.shape[0]
    return pl.pallas_call(
        _rope_kernel,
        grid=(t // tm,),
        in_specs=[pl.BlockSpec((tm, 1), lambda i: (i, 0)), pl.BlockSpec((8, LANES), lambda i: (0, 0))],
        out_specs=[pl.BlockSpec((tm, LANES), lambda i: (i, 0))] * 2,
        out_shape=[jax.ShapeDtypeStruct((t, LANES), F32)] * 2,
        compiler_params=_cparams(("arbitrary",)),
        name="rope_tables",
    )(pos, rows)


def _rope_rows(dim, first_lane, pad_one_lanes):
    half = dim // 2
    inv = ROPE_THETA ** (-(np.arange(0, dim, 2, dtype=np.float32) / dim))
    rows = np.zeros((8, LANES), np.float32)
    lo = slice(first_lane, first_lane + half)
    hi = slice(first_lane + half, first_lane + dim)
    rows[0, lo] = inv
    rows[0, hi] = inv
    rows[1, lo] = 1.0
    rows[1, hi] = 1.0
    rows[2, :pad_one_lanes] = 1.0
    rows[3, lo] = -1.0
    rows[3, hi] = 1.0
    return jnp.asarray(rows)


def _proj_kernel(x_ref, w_ref, *out_refs, offsets):
    xb = x_ref[...].astype(BF16)
    for o_ref, (a, b) in zip(out_refs, offsets):
        o_ref[...] = _dot(xb, w_ref[:, a:b]).astype(o_ref.dtype)


def _proj(x, w, widths, dtypes, tm=512):
    t, k = x.shape
    offs = np.concatenate([[0], np.cumsum(widths)]).tolist()
    offsets = tuple((offs[i], offs[i + 1]) for i in range(len(widths)))
    return pl.pallas_call(
        functools.partial(_proj_kernel, offsets=offsets),
        grid=(t // tm,),
        in_specs=[pl.BlockSpec((tm, k), lambda i: (i, 0)), pl.BlockSpec(w.shape, lambda i: (0, 0))],
        out_specs=[pl.BlockSpec((tm, n), lambda i: (i, 0)) for n in widths],
        out_shape=[jax.ShapeDtypeStruct((t, n), dt) for n, dt in zip(widths, dtypes)],
        compiler_params=_cparams(("arbitrary",)),
        name="in_proj",
    )(x, w)


OD_SEG = dict(mq=(0, 512), mk=(512, 1024), mv=(1024, 1536), mo=(1536, 2048), gates=(2048, 2176),
              sq=(2176, 3200), sqsw=(3200, 4224), sk=(4224, 4480), sksw=(4480, 4736), sv=(4736, 5248))
OD_COLS = 5248


def _proj_odd_kernel(x_ref, w_ref, c_ref, s_ref, mq_ref, mk_ref, mv_ref, mo_ref, mg_ref, sq_ref, sk_ref, sv_ref):
    xb = x_ref[...].astype(BF16)

    def seg(name):
        a, b = OD_SEG[name]
        return _dot(xb, w_ref[:, a:b])

    mq_ref[...] = seg("mq")
    mk_ref[...] = seg("mk")
    mv_ref[...] = seg("mv")
    mo_ref[...] = seg("mo")
    mg_ref[...] = seg("gates")
    c = c_ref[...]
    s = s_ref[...]
    c8 = jnp.concatenate([c] * SWA_H, axis=1)
    s8 = jnp.concatenate([s] * SWA_H, axis=1)
    sq_ref[...] = (seg("sq") * c8 + seg("sqsw") * s8).astype(sq_ref.dtype)
    c2 = jnp.concatenate([c] * SWA_KV, axis=1)
    s2 = jnp.concatenate([s] * SWA_KV, axis=1)
    sk_ref[...] = (seg("sk") * c2 + seg("sksw") * s2).astype(sk_ref.dtype)
    sv_ref[...] = seg("sv").astype(sv_ref.dtype)


def _proj_odd(x, w, ctab, stab, tm=256):
    t, k = x.shape
    widths = (512, 512, 512, 512, 128, SWA_H * LANES, SWA_KV * LANES, 2 * SWA_KV * LANES)
    dtypes = (F32, F32, F32, F32, F32, BF16, BF16, BF16)
    return pl.pallas_call(
        _proj_odd_kernel,
        grid=(t // tm,),
        in_specs=[pl.BlockSpec((tm, k), lambda i: (i, 0)), pl.BlockSpec(w.shape, lambda i: (0, 0)),
                  pl.BlockSpec((tm, LANES), lambda i: (i, 0)), pl.BlockSpec((tm, LANES), lambda i: (i, 0))],
        out_specs=[pl.BlockSpec((tm, n), lambda i: (i, 0)) for n in widths],
        out_shape=[jax.ShapeDtypeStruct((t, n), dt) for n, dt in zip(widths, dtypes)],
        compiler_params=_cparams(("arbitrary",)),
        name="in_proj_odd",
    )(x, w, ctab, stab)


def _rms(x, g):
    return x * lax.rsqrt(jnp.mean(x * x, axis=-1, keepdims=True) + EPS) * g


def _mla_prep_kernel(in_ref, c_ref, s_ref, qn_ref, kvn_ref, wq_ref, wkv_ref, q_ref, k_ref, v_ref):
    hw = MLA_H * LANES
    c = c_ref[...]
    s = s_ref[...]
    c8 = jnp.concatenate([c] * MLA_H, axis=1)
    s8 = jnp.concatenate([s] * MLA_H, axis=1)
    cqn = _rms(in_ref[:, 0:Q_LORA], qn_ref[...]).astype(BF16)
    qq = _dot(cqn, wq_ref[...])
    scale = (MLA_NOPE + MLA_ROPE) ** -0.5
    q_ref[...] = ((qq[:, :hw] * c8 + qq[:, hw:] * s8) * scale).astype(q_ref.dtype)
    ckvn = _rms(in_ref[:, Q_LORA:Q_LORA + KV_LORA], kvn_ref[...]).astype(BF16)
    kv = _dot(ckvn, wkv_ref[...])
    o = Q_LORA + KV_LORA
    krr = in_ref[:, o:o + LANES] * c + in_ref[:, o + LANES:o + 2 * LANES] * s
    k_ref[...] = (kv[:, :hw] + jnp.concatenate([krr] * MLA_H, axis=1)).astype(k_ref.dtype)
    v_ref[...] = kv[:, hw:].astype(v_ref.dtype)


def _mla_prep(mla_in, ctab, stab, qn, kvn, wq2, wkv2, tm=512):
    t = mla_in.shape[0]
    hw = MLA_H * LANES
    row = lambda i: (i, 0)
    fix = lambda i: (0, 0)
    return pl.pallas_call(
        _mla_prep_kernel,
        grid=(t // tm,),
        in_specs=[pl.BlockSpec((tm, mla_in.shape[1]), row), pl.BlockSpec((tm, LANES), row), pl.BlockSpec((tm, LANES), row),
                  pl.BlockSpec(qn.shape, fix), pl.BlockSpec(kvn.shape, fix),
                  pl.BlockSpec(wq2.shape, fix), pl.BlockSpec(wkv2.shape, fix)],
        out_specs=[pl.BlockSpec((tm, hw), row)] * 3,
        out_shape=[jax.ShapeDtypeStruct((t, hw), BF16)] * 3,
        compiler_params=_cparams(("arbitrary",)),
        name="mla_prep",
    )(mla_in, ctab, stab, qn, kvn, wq2, wkv2)


def _mla_attn_kernel(q_ref, k_ref, v_ref, o_ref, *, tq):
    i = pl.program_id(2)
    neg = -1e30
    lane = _iota2((tq, LANES), 1)
    ones_lane = (MLA_V, 0)

    def chunk(j, carry, masked):
        start = pl.multiple_of(j * tq, tq)
        out = []
        for hh in range(2):
            m, acc = carry[hh]
            q = q_ref[:, hh * LANES:(hh + 1) * LANES]
            kc = k_ref[pl.ds(start, tq), hh * LANES:(hh + 1) * LANES]
            vc = v_ref[pl.ds(start, tq), hh * LANES:(hh + 1) * LANES]
            vc = jnp.where(lane == ones_lane[hh], jnp.ones_like(vc), vc)
            s = _dot_nt(q, kc)
            if masked:
                s = jnp.where(_iota2(s.shape, 0) >= _iota2(s.shape, 1), s, neg)
            m_new = jnp.maximum(m, jnp.max(s, axis=-1, keepdims=True))
            alpha = jnp.exp(m - m_new)
            p = jnp.exp(s - m_new)
            acc = alpha * acc + _dot(p.astype(BF16), vc)
            out.append((m_new, acc))
        return tuple(out)

    one = (jnp.full((tq, 1), neg, F32), jnp.zeros((tq, LANES), F32))
    carry = lax.fori_loop(0, i, lambda j, c: chunk(j, c, False), (one, one))
    (_, acc0), (_, acc1) = chunk(i, carry, True)
    o0 = acc0 / _lane_bcast(acc0, ones_lane[0])
    o1 = acc1 / _lane_bcast(acc1, ones_lane[1])
    o_ref[...] = jnp.where(lane < MLA_V, o0, o1).astype(o_ref.dtype)


def _mla_attn(q, k, v, batch, seq, tq=512):
    tq = min(tq, seq)
    nq = seq // tq
    pairs = MLA_H // 2
    return pl.pallas_call(
        functools.partial(_mla_attn_kernel, tq=tq),
        grid=(batch, pairs, nq),
        in_specs=[pl.BlockSpec((tq, 2 * LANES), lambda b, p, i: (b * nq + i, p)),
                  pl.BlockSpec((seq, 2 * LANES), lambda b, p, i: (b, p)),
                  pl.BlockSpec((seq, 2 * LANES), lambda b, p, i: (b, p))],
        out_specs=pl.BlockSpec((tq, LANES), lambda b, p, i: (b * nq + i, p)),
        out_shape=jax.ShapeDtypeStruct((batch * seq, pairs * LANES), BF16),
        compiler_params=_cparams(("arbitrary", "arbitrary", "arbitrary")),
        name="mla_attn",
    )(q, k, v)


def _unit_lower_inverse_many(ns):
    c = ns[0].shape[0]
    eye = (_iota2((c, c), 0) == _iota2((c, c), 1)).astype(F32)
    xs = [-n for n in ns]
    ps = [eye + x for x in xs]
    xsplit = [_split2(x) for x in xs]
    for _ in range(int(math.log2(c)) - 1):
        xs = [_dot(xh, xh) + (_dot(xh, xl) + _dot(xl, xh)) for xh, xl in xsplit]
        xsplit = [_split2(x) for x in xs]
        psplit = [_split2(p) for p in ps]
        ps = [p + (_dot(ph, xh) + (_dot(ph, xl) + _dot(plo, xh)))
              for p, (ph, plo), (xh, xl) in zip(ps, psplit, xsplit)]
    return ps


def _gdn_kernel(qkv_ref, g_ref, z_ref, cw_ref, al_ref, dt_ref, on_ref, o_ref, ext_ref, st_ref):
    c = GDN_CHUNK
    hd = GDN_DK
    nqk = GDN_H * GDN_DK

    @pl.when(pl.program_id(1) == 0)
    def _():
        ext_ref[:, 0:8, :] = jnp.zeros((ext_ref.shape[0], 8, ext_ref.shape[2]), F32)
        st_ref[...] = jnp.zeros(st_ref.shape, F32)

    tri = (_iota2((c, c), 0) >= _iota2((c, c), 1)).astype(F32)
    row_ge = _iota2((c, c), 0) >= _iota2((c, c), 1)
    row_gt = _iota2((c, c), 0) > _iota2((c, c), 1)
    ones = jnp.ones((c, LANES), F32)
    lane = _iota2((c, LANES), 1)

    units = []
    for bb in range(qkv_ref.shape[0]):
        ext = ext_ref.at[bb]
        ext[8:8 + c, :] = qkv_ref[bb]
        conv = cw_ref[0:1, :] * ext[5:5 + c, :]
        for j in range(1, CONV_W):
            conv = conv + cw_ref[j:j + 1, :] * ext[5 + j:5 + j + c, :]
        ext[0:8, :] = ext[c:c + 8, :]
        act = _silu(conv)
        gates = g_ref[bb]
        beta_all = _sigmoid(gates)
        g_all = -jnp.exp(al_ref[...]) * _softplus(gates + dt_ref[...])
        gc_all = _dot_sel(tri, g_all)
        for h in range(GDN_H):
            q = act[:, h * hd:(h + 1) * hd]
            k = act[:, nqk + h * hd:nqk + (h + 1) * hd]
            v = act[:, 2 * nqk + h * GDN_DV:2 * nqk + (h + 1) * GDN_DV]
            q = q * lax.rsqrt(jnp.sum(q * q, axis=-1, keepdims=True) + EPS) * (GDN_DK ** -0.5)
            k = k * lax.rsqrt(jnp.sum(k * k, axis=-1, keepdims=True) + EPS)
            beta = _lane_bcast(beta_all, h)
            gcol = _lane_bcast(gc_all, GDN_H + h)
            grow = _dot_sel(ones, jnp.where(lane == GDN_H + h, gc_all, 0.0), _dot_nt)
            decay = jnp.exp(jnp.where(row_ge, gcol[:, :c] - grow, -jnp.inf))
            kb = k * beta
            lower = jnp.where(row_gt, _dot3(kb, k, _dot_nt) * decay, 0.0)
            eg = jnp.exp(gcol)
            glast = gcol[c - 1:c, :]
            units.append(dict(bb=bb, h=h, lower=lower, rhs=jnp.concatenate([v * beta, kb * eg], axis=1),
                              attn=_dot_nt(q.astype(BF16), k.astype(BF16)) * decay, qg=(q * eg).astype(BF16),
                              kg=(k * jnp.exp(glast - gcol)).astype(BF16), gl=jnp.exp(glast)))

    tinvs = _unit_lower_inverse_many([u["lower"] for u in units])
    uws = []
    for u, tinv in zip(units, tinvs):
        th, tl = _split2(tinv)
        rh, rl = _split2(u["rhs"])
        uws.append(_dot(th, rh) + (_dot(th, rl) + _dot(tl, rh)))
    states = [st_ref[u["bb"], u["h"]] for u in units]
    sbs = [s.astype(BF16) for s in states]
    vnews = [(uw[:, :GDN_DV] - _dot(uw[:, GDN_DV:].astype(BF16), sb)).astype(BF16) for uw, sb in zip(uws, sbs)]
    for u, state, sb, vnb in zip(units, states, sbs, vnews):
        bb, h = u["bb"], u["h"]
        o = _dot(u["qg"], sb) + _dot(u["attn"].astype(BF16), vnb)
        st_ref[bb, h] = state * u["gl"] + _dot_tn(u["kg"], vnb)
        o = _rms(o, on_ref[...]) * _silu(z_ref[bb, :, h * GDN_DV:(h + 1) * GDN_DV])
        o_ref[bb, :, h * GDN_DV:(h + 1) * GDN_DV] = o.astype(o_ref.dtype)


def _gdn(qkv, gates, z, conv_w, a_row, dt_row, o_norm, batch, seq):
    c = GDN_CHUNK
    nc = seq // c
    w3 = qkv.shape[1]
    wo = GDN_H * GDN_DV
    nb = SEQS_PER_STEP
    row = lambda b, i: (b, i, 0)
    fix = lambda b, i: (0, 0)
    out = pl.pallas_call(
        _gdn_kernel,
        grid=(batch // nb, nc),
        in_specs=[pl.BlockSpec((nb, c, w3), row), pl.BlockSpec((nb, c, LANES), row), pl.BlockSpec((nb, c, wo), row),
                  pl.BlockSpec(conv_w.shape, fix), pl.BlockSpec((1, LANES), fix), pl.BlockSpec((1, LANES), fix),
                  pl.BlockSpec((1, GDN_DV), fix)],
        out_specs=pl.BlockSpec((nb, c, wo), row),
        out_shape=jax.ShapeDtypeStruct((batch, seq, wo), BF16),
        scratch_shapes=[pltpu.VMEM((nb, c + 8, w3), F32), pltpu.VMEM((nb, GDN_H, GDN_DK, GDN_DV), F32)],
        compiler_params=_cparams(("arbitrary", "arbitrary")),
        name="gdn",
    )(qkv.reshape(batch, seq, w3), gates.reshape(batch, seq, LANES), z.reshape(batch, seq, wo), conv_w, a_row, dt_row, o_norm)
    return out.reshape(batch * seq, wo)


def _mlstm_kernel(q_ref, k_ref, v_ref, og_ref, g_ref, bias_ref, nrm_ref, o_ref, c_ref, n_ref, m_ref):
    @pl.when(pl.program_id(1) == 0)
    def _():
        c_ref[...] = jnp.zeros(c_ref.shape, F32)
        n_ref[...] = jnp.zeros(n_ref.shape, F32)
        m_ref[...] = jnp.zeros(m_ref.shape, F32)

    c = ML_CHUNK
    tri = (_iota2((c, c), 0) >= _iota2((c, c), 1)).astype(F32)
    row_ge = _iota2((c, c), 0) >= _iota2((c, c), 1)
    ones = jnp.ones((c, LANES), F32)
    lane = _iota2((c, LANES), 1)

    units = []
    for bb in range(q_ref.shape[0]):
        pre = g_ref[bb] + bias_ref[...]
        logf = jnp.minimum(pre, 0.0) - jnp.log(1.0 + jnp.exp(-jnp.abs(pre)))
        bcum_all = _dot_sel(tri, logf)
        for h in range(ML_H):
            q = q_ref[bb, :, h * LANES:(h + 1) * LANES]
            k = k_ref[bb, :, h * LANES:(h + 1) * LANES] * (ML_DK ** -0.5)
            units.append(dict(bb=bb, h=h, q=q, k=k, qb=q.astype(BF16), vb=v_ref[bb, :, h * ML_DV:(h + 1) * ML_DV].astype(BF16),
                              bcol=_lane_bcast(bcum_all, ML_H + h),
                              icol=_lane_bcast(pre, h),
                              col=jnp.where(lane == h, pre, 0.0) - jnp.where(lane == ML_H + h, bcum_all, 0.0),
                              m_st=m_ref[bb, h], cst=c_ref[bb, h], nst=n_ref[bb, h]))
    for u in units:
        u["row"] = _dot_sel(ones, u["col"], _dot_nt)
        u["qk"] = _dot_nt(u["qb"], u["k"].astype(BF16))
        u["qc"] = _dot(u["qb"], u["cst"].astype(BF16))
    for u in units:
        d = jnp.where(row_ge, u["bcol"][:, :c] + u["row"], -jnp.inf)
        inter = u["bcol"] + u["m_st"]
        m_t = jnp.maximum(inter, jnp.max(d, axis=-1, keepdims=True))
        u["m_t"] = m_t
        u["w_inter"] = jnp.exp(inter - m_t)
        u["p"] = jnp.exp(d - m_t[:, :c]) * u["qk"]
        u["pv"] = _dot(u["p"].astype(BF16), u["vb"])
        b_end = u["bcol"][c - 1:c, :]
        a = b_end - u["bcol"] + u["icol"]
        m_new = jnp.maximum(b_end + u["m_st"], jnp.max(a, axis=0, keepdims=True))
        u["m_new"] = m_new
        u["keep"] = jnp.exp(b_end + u["m_st"] - m_new)
        u["ks"] = u["k"] * jnp.exp(a - m_new)
        u["kv"] = _dot_tn(u["ks"].astype(BF16), u["vb"])
    for u in units:
        bb, h = u["bb"], u["h"]
        num = u["w_inter"] * u["qc"] + u["pv"]
        den = (u["w_inter"] * jnp.sum(u["q"] * u["nst"], axis=-1, keepdims=True)
               + jnp.sum(u["p"], axis=-1, keepdims=True))
        hc = num / jnp.maximum(jnp.abs(den), jnp.exp(-u["m_t"]))
        c_ref[bb, h] = u["cst"] * u["keep"] + u["kv"]
        n_ref[bb, h] = u["nst"] * u["keep"] + jnp.sum(u["ks"], axis=0, keepdims=True)
        m_ref[bb, h] = u["m_new"]
        hn = (_rms(hc, nrm_ref[:, h * ML_DV:(h + 1) * ML_DV])
              * _sigmoid(og_ref[bb, :, h * ML_DV:(h + 1) * ML_DV]))
        o_ref[bb, :, h * ML_DV:(h + 1) * ML_DV] = hn.astype(o_ref.dtype)


def _mlstm(mq, mk, mv, mo, gates, bias_row, norm_row, batch, seq):
    c = ML_CHUNK
    nc = seq // c
    nb = SEQS_PER_STEP
    row = lambda b, i: (b, i, 0)
    fix = lambda b, i: (0, 0)
    wide = ML_H * LANES
    r3 = lambda a: a.reshape(batch, seq, a.shape[-1])
    out = pl.pallas_call(
        _mlstm_kernel,
        grid=(batch // nb, nc),
        in_specs=[pl.BlockSpec((nb, c, wide), row), pl.BlockSpec((nb, c, wide), row), pl.BlockSpec((nb, c, wide), row),
                  pl.BlockSpec((nb, c, wide), row), pl.BlockSpec((nb, c, LANES), row),
                  pl.BlockSpec((1, LANES), fix), pl.BlockSpec((1, wide), fix)],
        out_specs=pl.BlockSpec((nb, c, wide), row),
        out_shape=jax.ShapeDtypeStruct((batch, seq, wide), BF16),
        scratch_shapes=[pltpu.VMEM((nb, ML_H, LANES, ML_DV), F32), pltpu.VMEM((nb, ML_H, 1, LANES), F32),
                        pltpu.VMEM((nb, ML_H, 1, LANES), F32)],
        compiler_params=_cparams(("arbitrary", "arbitrary")),
        name="mlstm",
    )(r3(mq), r3(mk), r3(mv), r3(mo), r3(gates), bias_row, norm_row)
    return out.reshape(batch * seq, wide)


def _swa_kernel(q_ref, kc_ref, kp_ref, vc_ref, vp_ref, sink_ref, o_ref):
    w = WINDOW
    n = pl.program_id(1)
    scale = SWA_D ** -0.5
    qi = _iota2((w, w), 0)
    kj = _iota2((w, w), 1)
    mask_c = kj <= qi
    mask_p = jnp.logical_and(kj > qi, n > 0)
    grp = SWA_H // SWA_KV
    neg = -1e30
    units = [(bb, h) for bb in range(q_ref.shape[0]) for h in range(SWA_H)]
    scores = []
    for bb, h in units:
        g = h // grp
        q = q_ref[bb, :, h * LANES:(h + 1) * LANES]
        scores.append((_dot_nt(q, kc_ref[bb, :, g * LANES:(g + 1) * LANES]),
                       _dot_nt(q, kp_ref[bb, :, g * LANES:(g + 1) * LANES])))
    masked, tops, exps, dens, probs = [], [], [], [], {}
    for sc, sp in scores:
        masked.append((jnp.where(mask_c, sc * scale, neg), jnp.where(mask_p, sp * scale, neg)))
    for (bb, h), (s_c, s_p) in zip(units, masked):
        tops.append(jnp.maximum(jnp.max(jnp.maximum(s_c, s_p), axis=-1, keepdims=True), sink_ref[:, h:h + 1]))
    for (s_c, s_p), m in zip(masked, tops):
        exps.append((jnp.where(mask_c, jnp.exp(s_c - m), 0.0), jnp.where(mask_p, jnp.exp(s_p - m), 0.0)))
    ones_b = jnp.ones((w, LANES), BF16)
    for (bb, h), (p_c, p_p), m in zip(units, exps, tops):
        p_c, p_p = p_c.astype(BF16), p_p.astype(BF16)
        probs[bb, h] = (p_c, p_p)
        dens.append(_dot(p_c, ones_b) + _dot(p_p, ones_b) + jnp.exp(sink_ref[:, h:h + 1] - m))
    inv = {u: 1.0 / den for u, den in zip(units, dens)}
    for bb in range(q_ref.shape[0]):
        for pair in range(SWA_H // 2):
            acc = None
            for sub in range(2):
                h = 2 * pair + sub
                vcol = (2 * (h // grp) + sub) * LANES
                p_c, p_p = probs[bb, h]
                part = (_dot(p_c, vc_ref[bb, :, vcol:vcol + LANES]) + _dot(p_p, vp_ref[bb, :, vcol:vcol + LANES])) * inv[bb, h]
                acc = part if acc is None else acc + part
            o_ref[bb, :, pair * LANES:(pair + 1) * LANES] = acc.astype(o_ref.dtype)


def _swa(sq, sk, sv, sinks_row, batch, seq):
    w = WINDOW
    nb = seq // w
    ns = SEQS_PER_STEP
    wo = SWA_H * SWA_D
    cur = lambda b, n: (b, n, 0)
    prev = lambda b, n: (b, jnp.maximum(n - 1, 0), 0)
    r3 = lambda a: a.reshape(batch, seq, a.shape[-1])
    q3, k3, v3 = r3(sq), r3(sk), r3(sv)
    out = pl.pallas_call(
        _swa_kernel,
        grid=(batch // ns, nb),
        in_specs=[pl.BlockSpec((ns, w, sq.shape[1]), cur),
                  pl.BlockSpec((ns, w, sk.shape[1]), cur), pl.BlockSpec((ns, w, sk.shape[1]), prev),
                  pl.BlockSpec((ns, w, sv.shape[1]), cur), pl.BlockSpec((ns, w, sv.shape[1]), prev),
                  pl.BlockSpec((1, LANES), lambda b, n: (0, 0))],
        out_specs=pl.BlockSpec((ns, w, wo), cur),
        out_shape=jax.ShapeDtypeStruct((batch, seq, wo), BF16),
        compiler_params=_cparams(("arbitrary", "arbitrary")),
        name="swa",
    )(q3, k3, k3, v3, v3, sinks_row)
    return out.reshape(batch * seq, wo)


def _layer_norm(h, g, b):
    mu = jnp.mean(h, axis=-1, keepdims=True)
    d = h - mu
    var = jnp.mean(d * d, axis=-1, keepdims=True)
    return d * lax.rsqrt(var + LN_EPS) * g + b


def _outproj_kernel(x_ref, a1_ref, a2_ref, w_ref, g_ref, b_ref, o_ref, op_ref):
    k1 = a1_ref.shape[1]
    y = _dot(a1_ref[...].astype(BF16), w_ref[0:k1, :]) + _dot(a2_ref[...].astype(BF16), w_ref[k1:, :])
    h = _layer_norm(DN_ALPHA * x_ref[...] + y, g_ref[...], b_ref[...])
    o_ref[...] = h
    op_ref[...] = _pack_pairs(h)


def _outproj_ln(x, a1, a2, w, g, b, tm=512):
    t, d = x.shape
    row = lambda i: (i, 0)
    fix = lambda i: (0, 0)
    return pl.pallas_call(
        _outproj_kernel,
        grid=(t // tm,),
        in_specs=[pl.BlockSpec((tm, d), row), pl.BlockSpec((tm, a1.shape[1]), row), pl.BlockSpec((tm, a2.shape[1]), row),
                  pl.BlockSpec(w.shape, fix), pl.BlockSpec((1, d), fix), pl.BlockSpec((1, d), fix)],
        out_specs=[pl.BlockSpec((tm, d), row), pl.BlockSpec((tm, d // 2), row)],
        out_shape=[jax.ShapeDtypeStruct((t, d), F32), jax.ShapeDtypeStruct((t, d // 2), jnp.uint32)],
        compiler_params=_cparams(("arbitrary",)),
        name="outproj_ln",
    )(x, a1, a2, w, g, b)


def _first_index(x, m, iota_f, sentinel):
    return jnp.min(jnp.where(x == m, iota_f, sentinel), axis=0, keepdims=True)


def _router_kernel(x_ref, wt_ref, bias_ref, idx_ref, gate_ref, rank_ref, cnt_ref, carry_ref):
    tm = x_ref.shape[0]
    e = N_EXPERTS
    gs = e // N_GROUPS
    ninf = -jnp.inf

    @pl.when(pl.program_id(0) == 0)
    def _():
        carry_ref[...] = jnp.zeros(carry_ref.shape, F32)

    logits = _dot3(wt_ref[...], x_ref[...], _dot_nt)
    scores = _sigmoid(logits)
    sel = scores + bias_ref[:, 0:1]

    sub_f = _iota2((gs, tm), 0).astype(F32)
    gscore = []
    for g in range(N_GROUPS):
        blk = sel[g * gs:(g + 1) * gs, :]
        m1 = jnp.max(blk, axis=0, keepdims=True)
        i1 = _first_index(blk, m1, sub_f, float(gs))
        m2 = jnp.max(jnp.where(sub_f == i1, ninf, blk), axis=0, keepdims=True)
        gscore.append(m1 + m2)
    gsc = jnp.concatenate(gscore, axis=0)
    grp_f = _iota2((N_GROUPS, tm), 0).astype(F32)
    gmask = jnp.zeros((N_GROUPS, tm), F32)
    for _ in range(TOPK_GROUPS):
        m = jnp.max(gsc, axis=0, keepdims=True)
        gi = _first_index(gsc, m, grp_f, float(N_GROUPS))
        hit = grp_f == gi
        gmask = jnp.where(hit, 1.0, gmask)
        gsc = jnp.where(hit, ninf, gsc)
    masked = jnp.concatenate(
        [jnp.where(gmask[g:g + 1, :] > 0.0, sel[g * gs:(g + 1) * gs, :], ninf) for g in range(N_GROUPS)], axis=0)

    exp_f = _iota2((e, tm), 0).astype(F32)
    chosen = jnp.zeros((e, tm), F32)
    idxs, gates = [], []
    for _ in range(TOP_K):
        m = jnp.max(masked, axis=0, keepdims=True)
        ei = _first_index(masked, m, exp_f, float(e))
        hit = exp_f == ei
        idxs.append(ei)
        gates.append(jnp.sum(jnp.where(hit, scores, 0.0), axis=0, keepdims=True))
        chosen = jnp.where(hit, 1.0, chosen)
        masked = jnp.where(hit, ninf, masked)
    gate = jnp.concatenate(gates, axis=0)
    gate = gate / jnp.sum(gate, axis=0, keepdims=True) * ROUTED_SCALE
    idx_f = jnp.concatenate(idxs, axis=0)

    upper = (_iota2((tm, tm), 0) < _iota2((tm, tm), 1)).astype(BF16)
    before = _dot(chosen.astype(BF16), upper) + carry_ref[...][:, 0:1]
    ranks = [jnp.sum(jnp.where(exp_f == idxs[k], before, 0.0), axis=0, keepdims=True) for k in range(TOP_K)]
    carry_ref[...] = carry_ref[...] + jnp.sum(chosen, axis=1, keepdims=True)

    idx_ref[...] = idx_f.astype(jnp.int32)
    gate_ref[...] = gate
    rank_ref[...] = jnp.concatenate(ranks, axis=0).astype(jnp.int32)
    cnt_ref[...] = carry_ref[...]


def _router(x, wt, bias_col, tm=512):
    t, d = x.shape
    col = lambda i: (0, i)
    fix = lambda i: (0, 0)
    return pl.pallas_call(
        _router_kernel,
        grid=(t // tm,),
        in_specs=[pl.BlockSpec((tm, d), lambda i: (i, 0)), pl.BlockSpec(wt.shape, fix), pl.BlockSpec((N_EXPERTS, LANES), fix)],
        out_specs=[pl.BlockSpec((TOP_K, tm), col), pl.BlockSpec((TOP_K, tm), col), pl.BlockSpec((TOP_K, tm), col),
                   pl.BlockSpec((N_EXPERTS, LANES), fix)],
        out_shape=[jax.ShapeDtypeStruct((TOP_K, t), jnp.int32), jax.ShapeDtypeStruct((TOP_K, t), F32),
                   jax.ShapeDtypeStruct((TOP_K, t), jnp.int32), jax.ShapeDtypeStruct((N_EXPERTS, LANES), F32)],
        scratch_shapes=[pltpu.VMEM((N_EXPERTS, LANES), F32)],
        compiler_params=_cparams(("arbitrary",)),
        name="router",
    )(x, wt, bias_col)


def _dest_kernel(idx_ref, rank_ref, start_ref, dest_ref):
    tm = idx_ref.shape[1]
    exp_i = _iota2((N_EXPERTS, tm), 0)
    start = start_ref[:, 0:1]
    rows = [jnp.sum(jnp.where(exp_i == idx_ref[s:s + 1, :], start, 0.0), axis=0, keepdims=True) for s in range(TOP_K)]
    dest_ref[...] = jnp.concatenate(rows, axis=0).astype(jnp.int32) + rank_ref[...]


def _dest_rows(idx, rank, start_col, tm=2048):
    t = idx.shape[1]
    tm = min(tm, t)
    col = lambda i: (0, i)
    return pl.pallas_call(
        _dest_kernel,
        grid=(t // tm,),
        in_specs=[pl.BlockSpec((TOP_K, tm), col), pl.BlockSpec((TOP_K, tm), col),
                  pl.BlockSpec((N_EXPERTS, LANES), lambda i: (0, 0))],
        out_specs=pl.BlockSpec((TOP_K, tm), col),
        out_shape=jax.ShapeDtypeStruct((TOP_K, t), jnp.int32),
        compiler_params=_cparams(("arbitrary",)),
        name="moe_dest",
    )(idx, rank, start_col)


def _pack_pairs(x):
    n = x.shape[1] // 2
    hi = lax.bitcast_convert_type(x[:, :n].astype(BF16).astype(F32), jnp.uint32)
    lo = lax.bitcast_convert_type(x[:, n:].astype(BF16).astype(F32), jnp.uint32)
    return hi | (lo >> 16)


def _unpack_pairs(w):
    hi = lax.bitcast_convert_type(w & jnp.uint32(0xFFFF0000), F32)
    lo = lax.bitcast_convert_type(w << 16, F32)
    return hi, lo


def _sc_scatter_rows(xp, dest, rows, chunk=LANES):
    t, width = xp.shape
    info = plsc.get_sparse_core_info()
    ncores, nsub = info.num_cores, info.num_subcores
    per_worker = t // (ncores * nsub)
    nchunk = per_worker // chunk
    mesh = plsc.VectorSubcoreMesh(core_axis_name="c", subcore_axis_name="s")

    @functools.partial(
        pl.kernel, mesh=mesh,
        out_type=jax.ShapeDtypeStruct((rows, width), xp.dtype),
        scratch_types=[pltpu.VMEM((TOP_K, chunk), jnp.int32), pltpu.VMEM((chunk, width), xp.dtype), pltpu.SemaphoreType.DMA],
    )
    def scatter(xp_hbm, dest_hbm, out_hbm, idx_v, rows_v, sem):
        base = (lax.axis_index("s") * ncores + lax.axis_index("c")) * per_worker

        @pl.loop(0, nchunk)
        def _(i):
            off = pl.multiple_of(base + i * chunk, chunk)
            pltpu.sync_copy(dest_hbm.at[:, pl.ds(off, chunk)], idx_v)
            pltpu.sync_copy(xp_hbm.at[pl.ds(off, chunk)], rows_v)
            copies = [pltpu.async_copy(rows_v, out_hbm.at[idx_v.at[s]], sem) for s in range(TOP_K)]
            for cp in copies:
                cp.wait()

    return scatter(xp, dest)


def _experts_kernel(be_ref, nu_ref, nv_ref, xs_ref, wg_ref, wu_ref, wd_ref, ys_ref, wgb_ref, wub_ref, wdb_ref):
    i = pl.program_id(0)

    @pl.when(jnp.logical_or(i == 0, be_ref[i] != be_ref[jnp.maximum(i - 1, 0)]))
    def _():
        wgb_ref[...] = wg_ref[0, 0].astype(BF16)
        wub_ref[...] = wu_ref[0, 0].astype(BF16)
        wdb_ref[...] = wd_ref[0, 0].astype(BF16)

    @pl.when(i < nu_ref[0])
    def _():
        half = xs_ref.shape[1]
        sub = xs_ref.shape[0] // EXPERT_SUBBLOCKS
        acts = []
        for r in range(EXPERT_SUBBLOCKS):
            rows = pl.ds(r * sub, sub)
            live = (_iota2((sub, 1), 0) + r * sub) < nv_ref[i]
            xa, xb = _unpack_pairs(jnp.where(live, xs_ref[rows, :], jnp.uint32(0)))
            xa = xa.astype(BF16)
            xb = xb.astype(BF16)
            gate = _dot(xa, wgb_ref[:half, :]) + _dot(xb, wgb_ref[half:, :])
            up = _dot(xa, wub_ref[:half, :]) + _dot(xb, wub_ref[half:, :])
            acts.append((gate, up))
        outs = [_dot((_silu(gate) * up).astype(BF16), wdb_ref[...]) for gate, up in acts]
        for r, y in enumerate(outs):
            ys_ref[pl.ds(r * sub, sub), :] = _pack_pairs(y)

    @pl.when(i >= nu_ref[0])
    def _():
        ys_ref[...] = jnp.zeros(ys_ref.shape, ys_ref.dtype)


def _experts(block_e, n_used, n_valid, xs, wg, wu, wd, layer):
    rows, half = xs.shape
    d = 2 * half
    nb = rows // EXPERT_BLOCK
    blk = lambda i, be, nu, nv: (jnp.minimum(i, nu[0] - 1), 0)
    wsel = lambda i, be, nu, nv: (layer, be[i], 0, 0)
    return pl.pallas_call(
        _experts_kernel,
        grid_spec=pltpu.PrefetchScalarGridSpec(
            num_scalar_prefetch=3,
            grid=(nb,),
            in_specs=[pl.BlockSpec((EXPERT_BLOCK, half), blk),
                      pl.BlockSpec((1, 1, d, D_EXPERT), wsel), pl.BlockSpec((1, 1, d, D_EXPERT), wsel),
                      pl.BlockSpec((1, 1, D_EXPERT, d), wsel)],
            out_specs=pl.BlockSpec((EXPERT_BLOCK, half), lambda i, be, nu, nv: (i, 0)),
            scratch_shapes=[pltpu.VMEM((d, D_EXPERT), BF16), pltpu.VMEM((d, D_EXPERT), BF16),
                            pltpu.VMEM((D_EXPERT, d), BF16)],
        ),
        out_shape=jax.ShapeDtypeStruct((rows, half), jnp.uint32),
        compiler_params=_cparams(("arbitrary",)),
        name="moe_experts",
    )(block_e, n_used, n_valid, xs, wg, wu, wd)


def _sc_gather_rows(table, idx, chunk=SC_CHUNK):
    n = idx.shape[0]
    width = table.shape[1]
    info = plsc.get_sparse_core_info()
    ncores, nsub = info.num_cores, info.num_subcores
    per_worker = n // (ncores * nsub)
    nchunk = per_worker // chunk
    mesh = plsc.VectorSubcoreMesh(core_axis_name="c", subcore_axis_name="s")

    @functools.partial(
        pl.kernel, mesh=mesh,
        out_type=jax.ShapeDtypeStruct((n, width), table.dtype),
        scratch_types=[pltpu.VMEM((nchunk, chunk), jnp.int32), pltpu.VMEM((2, chunk, width), table.dtype),
                       pltpu.SemaphoreType.DMA((2,)), pltpu.SemaphoreType.DMA((2,))],
    )
    def gather(table_hbm, idx_hbm, out_hbm, idx_v, rows_v, gsem, wsem):
        wid = lax.axis_index("s") * ncores + lax.axis_index("c")
        base = wid * per_worker
        pltpu.sync_copy(idx_hbm.at[pl.ds(wid * nchunk, nchunk)], idx_v)

        def fetch(j, b):
            return pltpu.make_async_copy(table_hbm.at[idx_v.at[j]], rows_v.at[b], gsem.at[b])

        def flush(j, b):
            off = pl.multiple_of(base + j * chunk, chunk)
            return pltpu.make_async_copy(rows_v.at[b], out_hbm.at[pl.ds(off, chunk)], wsem.at[b])

        fetch(0, 0).start()

        @pl.loop(0, nchunk, step=2)
        def _(i):
            for b in range(2):
                j = i + b
                fetch(j, b).wait()

                @pl.when(j + 1 < nchunk)
                def _():
                    @pl.when(j >= 1)
                    def _():
                        flush(j - 1, 1 - b).wait()

                    fetch(j + 1, 1 - b).start()

                flush(j, b).start()

        flush(nchunk - 2, 0).wait()
        flush(nchunk - 1, 1).wait()

    return gather(table, idx.reshape(n // chunk, chunk))


def _combine_kernel(x_ref, gate_ref, rows_ref, sg_ref, su_ref, sd_ref, g_ref, b_ref, o_ref):
    x = x_ref[...]
    xb = x.astype(BF16)
    hs = _silu(_dot(xb, sg_ref[...])) * _dot(xb, su_ref[...])
    ff = _dot(hs.astype(BF16), sd_ref[...])
    gate = gate_ref[...]
    half = rows_ref.shape[2]
    ya = ff[:, :half]
    yb = ff[:, half:]
    for s in range(TOP_K):
        a, b = _unpack_pairs(rows_ref[s])
        ya = ya + gate[:, s:s + 1] * a
        yb = yb + gate[:, s:s + 1] * b
    ff = jnp.concatenate([ya, yb], axis=1)
    o_ref[...] = _layer_norm(DN_ALPHA * x + ff, g_ref[...], b_ref[...])


def _combine_alias_kernel(prev_ref, *refs):
    del prev_ref
    _combine_kernel(*refs)


def _combine(x, gate_t, rows, sg, su, sd, g, b, part, nparts, prev=None, tm=512):
    t, d = x.shape
    tp = t // nparts
    tm = min(tm, tp)
    first = part * (tp // tm)
    row = lambda i: (first + i, 0)
    fix = lambda i: (0, 0)
    in_specs = [pl.BlockSpec((tm, d), row), pl.BlockSpec((tm, TOP_K), row),
                pl.BlockSpec((TOP_K, tm, d // 2), lambda i: (0, i, 0)),
                pl.BlockSpec(sg.shape, fix), pl.BlockSpec(su.shape, fix), pl.BlockSpec(sd.shape, fix),
                pl.BlockSpec((1, d), fix), pl.BlockSpec((1, d), fix)]
    args = (x, gate_t, rows, sg, su, sd, g, b)
    if prev is None:
        body, aliases = _combine_kernel, {}
    else:
        body, aliases = _combine_alias_kernel, {0: 0}
        in_specs = [pl.BlockSpec(memory_space=pl.ANY)] + in_specs
        args = (prev,) + args
    return pl.pallas_call(
        body,
        grid=(tp // tm,),
        in_specs=in_specs,
        out_specs=pl.BlockSpec((tm, d), row),
        out_shape=jax.ShapeDtypeStruct((t, d), F32),
        input_output_aliases=aliases,
        compiler_params=_cparams(("arbitrary",)),
        name="moe_combine",
    )(*args)


def _take_cols(w, idx):
    wz = jnp.concatenate([w, jnp.zeros((w.shape[0], 1), w.dtype)], axis=1)
    idx = np.where(np.asarray(idx) < 0, w.shape[1], np.asarray(idx))
    return jnp.take(wz, jnp.asarray(idx, jnp.int32), axis=1)


def _pad_lane_row(v, first_lane, width=LANES):
    out = jnp.zeros((1, width), F32)
    return lax.dynamic_update_slice(out, v.reshape(1, -1).astype(F32), (0, first_lane))


def _even_in_cols():
    z = lambda n: -np.ones(n, int)
    kr0 = Q_LORA + KV_LORA
    half = MLA_ROPE // 2
    cols = [np.arange(0, Q_LORA), np.arange(Q_LORA, Q_LORA + KV_LORA),
            z(64), np.arange(kr0, kr0 + MLA_ROPE), z(32),
            z(64), np.arange(kr0 + half, kr0 + MLA_ROPE), np.arange(kr0, kr0 + half), z(32)]
    g0 = kr0 + MLA_ROPE
    nqk = GDN_H * GDN_DK
    cols.append(np.arange(g0, g0 + 3 * nqk))
    zoff = g0 + 3 * nqk + 2 * GDN_H
    cols.append(np.arange(zoff, zoff + GDN_H * GDN_DV))
    cols += [np.arange(g0 + 3 * nqk, g0 + 3 * nqk + 2 * GDN_H), z(LANES - 2 * GDN_H)]
    return np.concatenate(cols)


EV_WIDTHS = (Q_LORA + KV_LORA + 2 * LANES, 3 * GDN_H * GDN_DK, GDN_H * GDN_DV, LANES)


def _mla_q_cols():
    per = MLA_NOPE + MLA_ROPE
    half = MLA_ROPE // 2
    main, sw = [], []
    for h in range(MLA_H):
        b = h * per
        main += [np.arange(b, b + per), -np.ones(LANES - per, int)]
        sw += [-np.ones(MLA_NOPE, int), np.arange(b + MLA_NOPE + half, b + per), np.arange(b + MLA_NOPE, b + MLA_NOPE + half),
               -np.ones(LANES - per, int)]
    return np.concatenate(main + sw)


def _mla_kv_cols():
    per = MLA_NOPE + MLA_V
    kc, vc = [], []
    for h in range(MLA_H):
        b = h * per
        kc += [np.arange(b, b + MLA_NOPE), -np.ones(LANES - MLA_NOPE, int)]
        vv = np.arange(b + MLA_NOPE, b + per)
        pad = -np.ones(LANES - MLA_V, int)
        vc += [vv, pad] if h % 2 == 0 else [pad, vv]
    return np.concatenate(kc + vc)


def _odd_in_cols():
    z = lambda n: -np.ones(n, int)
    o = 0
    cols = []
    mq0, mk0 = 0, ML_H * ML_DK
    for base in (mq0, mk0):
        for h in range(ML_H):
            cols += [np.arange(base + h * ML_DK, base + (h + 1) * ML_DK), z(LANES - ML_DK)]
    mv0 = 2 * ML_H * ML_DK
    cols.append(np.arange(mv0, mv0 + ML_H * ML_DV))
    mi0 = mv0 + ML_H * ML_DV
    mo0 = mi0 + 2 * ML_H
    cols.append(np.arange(mo0, mo0 + ML_H * ML_DV))
    cols += [np.arange(mi0, mi0 + 2 * ML_H), z(LANES - 2 * ML_H)]
    sq0 = mo0 + ML_H * ML_DV
    sk0 = sq0 + SWA_H * SWA_D
    sv0 = sk0 + SWA_KV * SWA_D
    half = SWA_D // 2

    def heads(base, n, swapped):
        out = []
        for h in range(n):
            b = base + h * SWA_D
            if swapped:
                out += [np.arange(b + half, b + SWA_D), np.arange(b, b + half), z(LANES - SWA_D)]
            else:
                out += [np.arange(b, b + SWA_D), z(LANES - SWA_D)]
        return out

    cols += heads(sq0, SWA_H, False) + heads(sq0, SWA_H, True) + heads(sk0, SWA_KV, False) + heads(sk0, SWA_KV, True)
    for g in range(SWA_KV):
        vv = np.arange(sv0 + g * SWA_D, sv0 + (g + 1) * SWA_D)
        cols += [vv, z(LANES - SWA_D), z(LANES - SWA_D), vv]
    return np.concatenate(cols)


def _even_mixer(x, tabs, w_in, q_norm, w_qb, kv_norm, w_kvb, conv_w, a_log, dt_bias, o_norm, batch, seq):
    ctab, stab = tabs
    w = _take_cols(w_in, _even_in_cols()).astype(BF16)
    mla_in, qkv, z, gates = _proj(x, w, EV_WIDTHS, (F32, F32, F32, F32))
    wq2 = _take_cols(w_qb, _mla_q_cols()).astype(BF16)
    wkv2 = _take_cols(w_kvb, _mla_kv_cols()).astype(BF16)
    q, k, v = _mla_prep(mla_in, ctab, stab, q_norm.reshape(1, -1), kv_norm.reshape(1, -1), wq2, wkv2)
    o_a = _mla_attn(q, k, v, batch, seq)
    o_b = _gdn(qkv, gates, z, conv_w, _pad_lane_row(a_log, GDN_H), _pad_lane_row(dt_bias, GDN_H),
               o_norm.reshape(1, -1), batch, seq)
    return o_a, o_b


def _odd_mixer(x, tabs, w_in, b_i, b_f, ml_norm, sinks, batch, seq):
    ctab, stab = tabs
    w = _take_cols(w_in, _odd_in_cols()).astype(BF16)
    mq, mk, mv, mo, mg, sq, sk, sv = _proj_odd(x, w, ctab, stab)
    bias_row = _pad_lane_row(jnp.concatenate([b_i, b_f]), 0)
    o_c = _mlstm(mq, mk, mv, mo, mg, bias_row, ml_norm.reshape(1, -1), batch, seq)
    o_d = _swa(sq, sk, sv, _pad_lane_row(sinks, 0), batch, seq)
    return o_c, o_d


def _moe(x, xp, router_w, router_b, w_gate, w_up, w_down, layer, s_gate, s_up, s_down, ln_g, ln_b):
    t, d = x.shape
    bias_col = jnp.broadcast_to(router_b.reshape(-1, 1).astype(F32), (N_EXPERTS, LANES))
    idx, gate, rank, cnt = _router(x, router_w.T, bias_col)
    counts = cnt[:, 0].astype(jnp.int32)
    padded = (counts + EXPERT_BLOCK - 1) // EXPERT_BLOCK * EXPERT_BLOCK
    pad_end = jnp.cumsum(padded)
    pad_start = pad_end - padded
    start_col = jnp.broadcast_to(pad_start.astype(F32).reshape(-1, 1), (N_EXPERTS, LANES))
    dest = _dest_rows(idx, rank, start_col)
    n_blocks = t * TOP_K // EXPERT_BLOCK + N_EXPERTS
    rows = n_blocks * EXPERT_BLOCK
    block_row = jnp.arange(n_blocks, dtype=jnp.int32) * EXPERT_BLOCK
    block_e = jnp.minimum(jnp.sum((pad_end[None, :] <= block_row[:, None]).astype(jnp.int32), axis=1), N_EXPERTS - 1)
    n_used = (pad_end[-1:] // EXPERT_BLOCK).astype(jnp.int32)
    live_end = jnp.sum(jnp.where(block_e[:, None] == jnp.arange(N_EXPERTS, dtype=jnp.int32)[None, :],
                                 (pad_start + counts)[None, :], 0), axis=1)
    n_valid = jnp.clip(live_end - block_row, 0, EXPERT_BLOCK).astype(jnp.int32)
    xs = _sc_scatter_rows(xp, dest, rows)
    ys = _experts(block_e, n_used, n_valid, xs, w_gate, w_up, w_down, layer)
    nparts = COMBINE_PARTS if t % (COMBINE_PARTS * 512) == 0 else 1
    tp = t // nparts
    gate_t = gate.T
    sgb, sub, sdb = s_gate.astype(BF16), s_up.astype(BF16), s_down.astype(BF16)
    out = None
    for part in range(nparts):
        idx_p = dest[:, part * tp:(part + 1) * tp].reshape(-1)
        picked = _sc_gather_rows(ys, idx_p).reshape(TOP_K, tp, d // 2)
        out = _combine(x, gate_t, picked, sgb, sub, sdb, ln_g.reshape(1, -1), ln_b.reshape(1, -1), part, nparts, prev=out)
    return out


def kernel(x, positions, ev_w_in, mla_q_norm, mla_w_qb, mla_kv_norm, mla_w_kvb, gdn_conv, gdn_a_log, gdn_dt_bias, gdn_norm, ev_w_out, od_w_in, mlstm_b_i, mlstm_b_f, mlstm_norm, swa_sinks, od_w_out, ln1_g, ln1_b, router_w, router_b, moe_w_gate, moe_w_up, moe_w_down, shared_w_gate, shared_w_up, shared_w_down, ln2_g, ln2_b):
    batch, seq, d = x.shape
    t = batch * seq
    pos = positions.reshape(t, 1).astype(F32)
    tabs_m = _rope_tables(pos, _rope_rows(MLA_ROPE, MLA_NOPE, MLA_NOPE))
    tabs_s = _rope_tables(pos, _rope_rows(SWA_D, 0, 0))
    h = x.reshape(t, d)
    for layer in range(DEPTH):
        j = layer // 2
        if layer % 2 == 0:
            a1, a2 = _even_mixer(h, tabs_m, ev_w_in[j], mla_q_norm[j], mla_w_qb[j], mla_kv_norm[j], mla_w_kvb[j],
                                 gdn_conv[j], gdn_a_log[j], gdn_dt_bias[j], gdn_norm[j], batch, seq)
            w_out = ev_w_out[j]
        else:
            a1, a2 = _odd_mixer(h, tabs_s, od_w_in[j], mlstm_b_i[j], mlstm_b_f[j], mlstm_norm[j], swa_sinks[j], batch, seq)
            w_out = od_w_out[j]
        h, hp = _outproj_ln(h, a1, a2, w_out.astype(BF16), ln1_g[layer].reshape(1, -1), ln1_b[layer].reshape(1, -1))
        h = _moe(h, hp, router_w[layer], router_b[layer], moe_w_gate, moe_w_up, moe_w_down, layer,
                 shared_w_gate[layer], shared_w_up[layer], shared_w_down[layer], ln2_g[layer], ln2_b[layer])
    return h.reshape(batch, seq, d)
```

```python
import functools
import math

import numpy as np
import jax
import jax.numpy as jnp
from jax import lax
from jax.experimental import pallas as pl
from jax.experimental.pallas import tpu as pltpu
from jax.experimental.pallas import tpu_sc as plsc

F32 = jnp.float32
BF16 = jnp.bfloat16
HI = lax.Precision.HIGHEST

D_MODEL = 1024
DEPTH = 4
ROPE_THETA = 10000.0
EPS = 1e-6
LN_EPS = 1e-5
MLA_H, MLA_NOPE, MLA_ROPE, MLA_V = 8, 64, 32, 64
Q_LORA, KV_LORA = 256, 128
GDN_H, GDN_DK, GDN_DV, CONV_W, GDN_CHUNK = 4, 128, 128, 4, 64
ML_H, ML_DK, ML_DV, ML_CHUNK = 4, 64, 128, 64
SWA_H, SWA_KV, SWA_D, WINDOW = 8, 2, 64, 128
N_EXPERTS, N_GROUPS, TOPK_GROUPS, TOP_K = 64, 8, 4, 8
D_EXPERT, D_SHARED = 256, 256
ROUTED_SCALE = 2.5
DN_ALPHA = (2 * DEPTH) ** 0.25

LANES = 128
V7X_VMEM_BYTES = 64 * 1024 * 1024
VMEM_LIMIT = 48 * 1024 * 1024

EXPERT_BLOCK = 512
EXPERT_SUBBLOCKS = 2
COMBINE_PARTS = 1
SEQS_PER_STEP = 2
DMA_GROUP = 2
SC_CHUNK = 64


def _cparams(sem, vmem=VMEM_LIMIT):
    return pltpu.CompilerParams(dimension_semantics=sem, vmem_limit_bytes=vmem)


def _dot(a, b, precision=None):
    return jnp.dot(a, b, preferred_element_type=F32, precision=precision)


def _dot_nt(a, b, precision=None):
    return lax.dot_general(a, b, (((1,), (1,)), ((), ())), preferred_element_type=F32, precision=precision)


def _dot_tn(a, b, precision=None):
    return lax.dot_general(a, b, (((0,), (0,)), ((), ())), preferred_element_type=F32, precision=precision)


def _split2(a):
    hi = a.astype(BF16)
    lo = (a - hi.astype(F32)).astype(BF16)
    return hi, lo


def _split3(a):
    p1 = a.astype(BF16)
    r = a - p1.astype(F32)
    p2 = r.astype(BF16)
    p3 = (r - p2.astype(F32)).astype(BF16)
    return p1, p2, p3


def _dot3(a, b, dot=_dot):
    ah, al = _split2(a)
    bh, bl = _split2(b)
    return dot(ah, bh) + (dot(ah, bl) + dot(al, bh))


def _dot_sel(sel, b, dot=_dot):
    sel = sel.astype(BF16)
    p1, p2, p3 = _split3(b)
    return dot(sel, p1) + (dot(sel, p2) + dot(sel, p3))


def _sigmoid(x):
    return 1.0 / (1.0 + jnp.exp(-x))


def _softplus(x):
    return jnp.maximum(x, 0.0) + jnp.log(1.0 + jnp.exp(-jnp.abs(x)))


def _silu(x):
    return x * _sigmoid(x)


def _lane_bcast(x, c):
    return jnp.broadcast_to(x[:, c:c + 1], x.shape)


def _iota2(shape, dim):
    return lax.broadcasted_iota(jnp.int32, shape, dim)


def _rope_kernel(pos_ref, rows_ref, c_ref, s_ref):
    ang = pos_ref[...] * rows_ref[0:1, :]
    c_ref[...] = rows_ref[1:2, :] * jnp.cos(ang) + rows_ref[2:3, :]
    s_ref[...] = rows_ref[3:4, :] * jnp.sin(ang)


def _rope_tables(pos, rows, tm=512):
    t = pos.shape[0]
    return pl.pallas_call(
        _rope_kernel,
        grid=(t // tm,),
        in_specs=[pl.BlockSpec((tm, 1), lambda i: (i, 0)), pl.BlockSpec((8, LANES), lambda i: (0, 0))],
        out_specs=[pl.BlockSpec((tm, LANES), lambda i: (i, 0))] * 2,
        out_shape=[jax.ShapeDtypeStruct((t, LANES), F32)] * 2,
        compiler_params=_cparams(("arbitrary",)),
        name="rope_tables",
    )(pos, rows)


def _rope_rows(dim, first_lane, pad_one_lanes):
    half = dim // 2
    inv = ROPE_THETA ** (-(np.arange(0, dim, 2, dtype=np.float32) / dim))
    rows = np.zeros((8, LANES), np.float32)
    lo = slice(first_lane, first_lane + half)
    hi = slice(first_lane + half, first_lane + dim)
    rows[0, lo] = inv
    rows[0, hi] = inv
    rows[1, lo] = 1.0
    rows[1, hi] = 1.0
    rows[2, :pad_one_lanes] = 1.0
    rows[3, lo] = -1.0
    rows[3, hi] = 1.0
    return jnp.asarray(rows)


def _proj_kernel(x_ref, w_ref, *out_refs, offsets):
    xb = x_ref[...].astype(BF16)
    for o_ref, (a, b) in zip(out_refs, offsets):
        o_ref[...] = _dot(xb, w_ref[:, a:b]).astype(o_ref.dtype)


def _proj(x, w, widths, dtypes, tm=512):
    t, k = x.shape
    offs = np.concatenate([[0], np.cumsum(widths)]).tolist()
    offsets = tuple((offs[i], offs[i + 1]) for i in range(len(widths)))
    return pl.pallas_call(
        functools.partial(_proj_kernel, offsets=offsets),
        grid=(t // tm,),
        in_specs=[pl.BlockSpec((tm, k), lambda i: (i, 0)), pl.BlockSpec(w.shape, lambda i: (0, 0))],
        out_specs=[pl.BlockSpec((tm, n), lambda i: (i, 0)) for n in widths],
        out_shape=[jax.ShapeDtypeStruct((t, n), dt) for n, dt in zip(widths, dtypes)],
        compiler_params=_cparams(("arbitrary",)),
        name="in_proj",
    )(x, w)


def _proj_even_kernel(x_ref, w_ref, cw_ref, mla_ref, act_ref, z_ref, g_ref, ext_ref, *, tiles_per_seq):
    tm = x_ref.shape[0]
    o = np.concatenate([[0], np.cumsum(EV_WIDTHS)]).tolist()
    @pl.when(pl.program_id(0) % tiles_per_seq == 0)
    def _():
        ext_ref[0:8, :] = jnp.zeros((8, ext_ref.shape[1]), F32)

    xb = x_ref[...].astype(BF16)
    nchunk = 3
    cw = EV_WIDTHS[1] // nchunk

    def project(ci):
        ext_ref[8:8 + tm, ci * cw:(ci + 1) * cw] = _dot(xb, w_ref[:, o[1] + ci * cw:o[1] + (ci + 1) * cw])

    project(0)
    for ci in range(nchunk):
        if ci + 1 < nchunk:
            project(ci + 1)
        else:
            mla_ref[...] = _dot(xb, w_ref[:, o[0]:o[1]])
            z_ref[...] = _dot(xb, w_ref[:, o[2]:o[3]])
            g_ref[...] = _dot(xb, w_ref[:, o[3]:o[4]])
        cols = slice(ci * cw, (ci + 1) * cw)
        conv = cw_ref[0:1, cols] * ext_ref[5:5 + tm, cols]
        for j in range(1, CONV_W):
            conv = conv + cw_ref[j:j + 1, cols] * ext_ref[5 + j:5 + j + tm, cols]
        act_ref[:, cols] = _silu(conv)
    ext_ref[0:8, :] = ext_ref[tm:tm + 8, :]


def _proj_even(x, w, conv_w, seq, tm=512):
    t, k = x.shape
    tm = min(tm, seq)
    row = lambda i: (i, 0)
    fix = lambda i: (0, 0)
    return pl.pallas_call(
        functools.partial(_proj_even_kernel, tiles_per_seq=seq // tm),
        grid=(t // tm,),
        in_specs=[pl.BlockSpec((tm, k), row), pl.BlockSpec(w.shape, fix), pl.BlockSpec(conv_w.shape, fix)],
        out_specs=[pl.BlockSpec((tm, n), row) for n in EV_WIDTHS],
        out_shape=[jax.ShapeDtypeStruct((t, n), F32) for n in EV_WIDTHS],
        scratch_shapes=[pltpu.VMEM((tm + 8, EV_WIDTHS[1]), F32)],
        compiler_params=_cparams(("arbitrary",)),
        name="in_proj",
    )(x, w, conv_w)


OD_SEG = dict(mq=(0, 512), mk=(512, 1024), mv=(1024, 1536), mo=(1536, 2048), gates=(2048, 2176),
              sq=(2176, 3200), sqsw=(3200, 4224), sk=(4224, 4480), sksw=(4480, 4736), sv=(4736, 5248))
OD_COLS = 5248


def _proj_odd_kernel(x_ref, w_ref, c_ref, s_ref, mq_ref, mk_ref, mv_ref, mo_ref, mg_ref, sq_ref, sk_ref, sv_ref):
    xb = x_ref[...].astype(BF16)

    def seg(name):
        a, b = OD_SEG[name]
        return _dot(xb, w_ref[:, a:b])

    mq_ref[...] = seg("mq")
    mk_ref[...] = seg("mk")
    mv_ref[...] = seg("mv")
    mo_ref[...] = seg("mo")
    mg_ref[...] = seg("gates")
    c = c_ref[...]
    s = s_ref[...]
    c8 = jnp.concatenate([c] * SWA_H, axis=1)
    s8 = jnp.concatenate([s] * SWA_H, axis=1)
    sq_ref[...] = (seg("sq") * c8 + seg("sqsw") * s8).astype(sq_ref.dtype)
    c2 = jnp.concatenate([c] * SWA_KV, axis=1)
    s2 = jnp.concatenate([s] * SWA_KV, axis=1)
    sk_ref[...] = (seg("sk") * c2 + seg("sksw") * s2).astype(sk_ref.dtype)
    sv_ref[...] = seg("sv").astype(sv_ref.dtype)


def _proj_odd(x, w, ctab, stab, tm=256):
    t, k = x.shape
    widths = (512, 512, 512, 512, 128, SWA_H * LANES, SWA_KV * LANES, 2 * SWA_KV * LANES)
    dtypes = (F32, F32, F32, F32, F32, BF16, BF16, BF16)
    return pl.pallas_call(
        _proj_odd_kernel,
        grid=(t // tm,),
        in_specs=[pl.BlockSpec((tm, k), lambda i: (i, 0)), pl.BlockSpec(w.shape, lambda i: (0, 0)),
                  pl.BlockSpec((tm, LANES), lambda i: (i, 0)), pl.BlockSpec((tm, LANES), lambda i: (i, 0))],
        out_specs=[pl.BlockSpec((tm, n), lambda i: (i, 0)) for n in widths],
        out_shape=[jax.ShapeDtypeStruct((t, n), dt) for n, dt in zip(widths, dtypes)],
        compiler_params=_cparams(("arbitrary",)),
        name="in_proj_odd",
    )(x, w, ctab, stab)


def _rms(x, g):
    return x * lax.rsqrt(jnp.mean(x * x, axis=-1, keepdims=True) + EPS) * g


def _mla_prep_kernel(in_ref, c_ref, s_ref, qn_ref, kvn_ref, wq_ref, wkv_ref, q_ref, k_ref, v_ref):
    hw = MLA_H * LANES
    c = c_ref[...]
    s = s_ref[...]
    c8 = jnp.concatenate([c] * MLA_H, axis=1)
    s8 = jnp.concatenate([s] * MLA_H, axis=1)
    cqn = _rms(in_ref[:, 0:Q_LORA], qn_ref[...]).astype(BF16)
    qq = _dot(cqn, wq_ref[...])
    scale = (MLA_NOPE + MLA_ROPE) ** -0.5
    q_ref[...] = ((qq[:, :hw] * c8 + qq[:, hw:] * s8) * scale).astype(q_ref.dtype)
    ckvn = _rms(in_ref[:, Q_LORA:Q_LORA + KV_LORA], kvn_ref[...]).astype(BF16)
    kv = _dot(ckvn, wkv_ref[...])
    o = Q_LORA + KV_LORA
    krr = in_ref[:, o:o + LANES] * c + in_ref[:, o + LANES:o + 2 * LANES] * s
    k_ref[...] = (kv[:, :hw] + jnp.concatenate([krr] * MLA_H, axis=1)).astype(k_ref.dtype)
    v_ref[...] = kv[:, hw:].astype(v_ref.dtype)


def _mla_prep(mla_in, ctab, stab, qn, kvn, wq2, wkv2, tm=512):
    t = mla_in.shape[0]
    hw = MLA_H * LANES
    row = lambda i: (i, 0)
    fix = lambda i: (0, 0)
    return pl.pallas_call(
        _mla_prep_kernel,
        grid=(t // tm,),
        in_specs=[pl.BlockSpec((tm, mla_in.shape[1]), row), pl.BlockSpec((tm, LANES), row), pl.BlockSpec((tm, LANES), row),
                  pl.BlockSpec(qn.shape, fix), pl.BlockSpec(kvn.shape, fix),
                  pl.BlockSpec(wq2.shape, fix), pl.BlockSpec(wkv2.shape, fix)],
        out_specs=[pl.BlockSpec((tm, hw), row)] * 3,
        out_shape=[jax.ShapeDtypeStruct((t, hw), BF16)] * 3,
        compiler_params=_cparams(("arbitrary",)),
        name="mla_prep",
    )(mla_in, ctab, stab, qn, kvn, wq2, wkv2)


def _mla_attn_kernel(q_ref, k_ref, v_ref, o_ref, *, tq):
    i = pl.program_id(2)
    neg = -1e30
    lane = _iota2((tq, LANES), 1)
    ones_lane = (MLA_V, 0)

    def chunk(j, carry, masked):
        start = pl.multiple_of(j * tq, tq)
        out = []
        for hh in range(2):
            m, acc = carry[hh]
            q = q_ref[:, hh * LANES:(hh + 1) * LANES]
            kc = k_ref[pl.ds(start, tq), hh * LANES:(hh + 1) * LANES]
            vc = v_ref[pl.ds(start, tq), hh * LANES:(hh + 1) * LANES]
            vc = jnp.where(lane == ones_lane[hh], jnp.ones_like(vc), vc)
            s = _dot_nt(q, kc)
            if masked:
                s = jnp.where(_iota2(s.shape, 0) >= _iota2(s.shape, 1), s, neg)
            m_new = jnp.maximum(m, jnp.max(s, axis=-1, keepdims=True))
            alpha = jnp.exp(m - m_new)
            p = jnp.exp(s - m_new)
            acc = alpha * acc + _dot(p.astype(BF16), vc)
            out.append((m_new, acc))
        return tuple(out)

    one = (jnp.full((tq, 1), neg, F32), jnp.zeros((tq, LANES), F32))
    carry = lax.fori_loop(0, i, lambda j, c: chunk(j, c, False), (one, one))
    (_, acc0), (_, acc1) = chunk(i, carry, True)
    o0 = acc0 / _lane_bcast(acc0, ones_lane[0])
    o1 = acc1 / _lane_bcast(acc1, ones_lane[1])
    o_ref[...] = jnp.where(lane < MLA_V, o0, o1).astype(o_ref.dtype)


def _mla_attn(q, k, v, batch, seq, tq=512):
    tq = min(tq, seq)
    nq = seq // tq
    pairs = MLA_H // 2
    return pl.pallas_call(
        functools.partial(_mla_attn_kernel, tq=tq),
        grid=(batch, pairs, nq),
        in_specs=[pl.BlockSpec((tq, 2 * LANES), lambda b, p, i: (b * nq + i, p)),
                  pl.BlockSpec((seq, 2 * LANES), lambda b, p, i: (b, p)),
                  pl.BlockSpec((seq, 2 * LANES), lambda b, p, i: (b, p))],
        out_specs=pl.BlockSpec((tq, LANES), lambda b, p, i: (b * nq + i, p)),
        out_shape=jax.ShapeDtypeStruct((batch * seq, pairs * LANES), BF16),
        compiler_params=_cparams(("arbitrary", "arbitrary", "arbitrary")),
        name="mla_attn",
    )(q, k, v)


def _unit_lower_inverse_many(ns):
    c = ns[0].shape[0]
    eye = (_iota2((c, c), 0) == _iota2((c, c), 1)).astype(F32)
    xs = [-n for n in ns]
    ps = [eye + x for x in xs]
    xb = [x.astype(BF16) for x in xs]
    for _ in range(int(math.log2(c)) - 1):
        xs = [_dot(b, b) for b in xb]
        xb = [x.astype(BF16) for x in xs]
        ps = [p + _dot(p.astype(BF16), b) for p, b in zip(ps, xb)]
    return ps


def _gdn_kernel(act_ref, g_ref, z_ref, al_ref, dt_ref, on_ref, o_ref, st_ref):
    c = GDN_CHUNK
    hd = GDN_DK
    nqk = GDN_H * GDN_DK

    @pl.when(pl.program_id(1) == 0)
    def _():
        st_ref[...] = jnp.zeros(st_ref.shape, F32)

    tri = (_iota2((c, c), 0) >= _iota2((c, c), 1)).astype(F32)
    row_ge = _iota2((c, c), 0) >= _iota2((c, c), 1)
    row_gt = _iota2((c, c), 0) > _iota2((c, c), 1)
    lane = _iota2((c, LANES), 1)

    units = []
    for bb in range(act_ref.shape[0]):
        act = act_ref[bb]
        gates = g_ref[bb]
        beta_all = _sigmoid(gates)
        g_all = -jnp.exp(al_ref[...]) * _softplus(gates + dt_ref[...])
        gc_all = _dot_sel(tri, g_all)
        gc_parts = _split3(gc_all)
        for h in range(GDN_H):
            q = act[:, h * hd:(h + 1) * hd]
            k = act[:, nqk + h * hd:nqk + (h + 1) * hd]
            v = act[:, 2 * nqk + h * GDN_DV:2 * nqk + (h + 1) * GDN_DV]
            q = q * lax.rsqrt(jnp.sum(q * q, axis=-1, keepdims=True) + EPS) * (GDN_DK ** -0.5)
            k = k * lax.rsqrt(jnp.sum(k * k, axis=-1, keepdims=True) + EPS)
            beta = _lane_bcast(beta_all, h)
            gcol = _lane_bcast(gc_all, GDN_H + h)
            pick = (lane == GDN_H + h).astype(BF16)
            grow = _dot_nt(pick, gc_parts[0]) + (_dot_nt(pick, gc_parts[1]) + _dot_nt(pick, gc_parts[2]))
            decay = jnp.exp(jnp.where(row_ge, gcol[:, :c] - grow, -jnp.inf))
            kb = k * beta
            lower = jnp.where(row_gt, _dot3(kb, k, _dot_nt) * decay, 0.0)
            eg = jnp.exp(gcol)
            glast = gcol[c - 1:c, :]
            units.append(dict(bb=bb, h=h, lower=lower, rhs=jnp.concatenate([v * beta, kb * eg], axis=1),
                              attn=_dot_nt(q.astype(BF16), k.astype(BF16)) * decay, qg=(q * eg).astype(BF16),
                              kg=(k * jnp.exp(glast - gcol)).astype(BF16), gl=jnp.exp(glast)))

    tinvs = _unit_lower_inverse_many([u["lower"] for u in units])
    uws = []
    for u, tinv in zip(units, tinvs):
        uws.append(_dot(tinv.astype(BF16), u["rhs"].astype(BF16)))
    states = [st_ref[u["bb"], u["h"]] for u in units]
    sbs = [s.astype(BF16) for s in states]
    vnews = [(uw[:, :GDN_DV] - _dot(uw[:, GDN_DV:].astype(BF16), sb)).astype(BF16) for uw, sb in zip(uws, sbs)]
    for u, state, sb, vnb in zip(units, states, sbs, vnews):
        bb, h = u["bb"], u["h"]
        o = _dot(u["qg"], sb) + _dot(u["attn"].astype(BF16), vnb)
        st_ref[bb, h] = state * u["gl"] + _dot_tn(u["kg"], vnb)
        o = _rms(o, on_ref[...]) * _silu(z_ref[bb, :, h * GDN_DV:(h + 1) * GDN_DV])
        o_ref[bb, :, h * GDN_DV:(h + 1) * GDN_DV] = o.astype(o_ref.dtype)


def _gdn(act, gates, z, a_row, dt_row, o_norm, batch, seq):
    c = GDN_CHUNK
    nc = seq // c
    w3 = act.shape[1]
    wo = GDN_H * GDN_DV
    nb = SEQS_PER_STEP
    row = lambda b, i: (b, i, 0)
    fix = lambda b, i: (0, 0)
    out = pl.pallas_call(
        _gdn_kernel,
        grid=(batch // nb, nc),
        in_specs=[pl.BlockSpec((nb, c, w3), row), pl.BlockSpec((nb, c, LANES), row), pl.BlockSpec((nb, c, wo), row),
                  pl.BlockSpec((1, LANES), fix), pl.BlockSpec((1, LANES), fix), pl.BlockSpec((1, GDN_DV), fix)],
        out_specs=pl.BlockSpec((nb, c, wo), row),
        out_shape=jax.ShapeDtypeStruct((batch, seq, wo), BF16),
        scratch_shapes=[pltpu.VMEM((nb, GDN_H, GDN_DK, GDN_DV), F32)],
        compiler_params=_cparams(("arbitrary", "arbitrary")),
        name="gdn",
    )(act.reshape(batch, seq, w3), gates.reshape(batch, seq, LANES), z.reshape(batch, seq, wo), a_row, dt_row, o_norm)
    return out.reshape(batch * seq, wo)


def _mlstm_kernel(q_ref, k_ref, v_ref, og_ref, g_ref, bias_ref, nrm_ref, o_ref, c_ref, n_ref, m_ref):
    @pl.when(pl.program_id(1) == 0)
    def _():
        c_ref[...] = jnp.zeros(c_ref.shape, F32)
        n_ref[...] = jnp.zeros(n_ref.shape, F32)
        m_ref[...] = jnp.zeros(m_ref.shape, F32)

    c = ML_CHUNK
    tri = (_iota2((c, c), 0) >= _iota2((c, c), 1)).astype(F32)
    row_ge = _iota2((c, c), 0) >= _iota2((c, c), 1)
    ones = jnp.ones((c, LANES), F32)
    lane = _iota2((c, LANES), 1)

    units = []
    for bb in range(q_ref.shape[0]):
        pre = g_ref[bb] + bias_ref[...]
        logf = jnp.minimum(pre, 0.0) - jnp.log(1.0 + jnp.exp(-jnp.abs(pre)))
        bcum_all = _dot_sel(tri, logf)
        for h in range(ML_H):
            q = q_ref[bb, :, h * LANES:(h + 1) * LANES]
            k = k_ref[bb, :, h * LANES:(h + 1) * LANES] * (ML_DK ** -0.5)
            units.append(dict(bb=bb, h=h, q=q, k=k, qb=q.astype(BF16), vb=v_ref[bb, :, h * ML_DV:(h + 1) * ML_DV].astype(BF16),
                              bcol=_lane_bcast(bcum_all, ML_H + h),
                              icol=_lane_bcast(pre, h),
                              col=jnp.where(lane == h, pre, 0.0) - jnp.where(lane == ML_H + h, bcum_all, 0.0),
                              m_st=m_ref[bb, h], cst=c_ref[bb, h], nst=n_ref[bb, h]))
    for u in units:
        u["row"] = _dot_sel(ones, u["col"], _dot_nt)
        u["qk"] = _dot_nt(u["qb"], u["k"].astype(BF16))
        u["qc"] = _dot(u["qb"], u["cst"].astype(BF16))
    for u in units:
        d = jnp.where(row_ge, u["bcol"][:, :c] + u["row"], -jnp.inf)
        inter = u["bcol"] + u["m_st"]
        m_t = jnp.maximum(inter, jnp.max(d, axis=-1, keepdims=True))
        u["m_t"] = m_t
        u["w_inter"] = jnp.exp(inter - m_t)
        u["p"] = jnp.exp(d - m_t[:, :c]) * u["qk"]
        u["pv"] = _dot(u["p"].astype(BF16), u["vb"])
        b_end = u["bcol"][c - 1:c, :]
        a = b_end - u["bcol"] + u["icol"]
        m_new = jnp.maximum(b_end + u["m_st"], jnp.max(a, axis=0, keepdims=True))
        u["m_new"] = m_new
        u["keep"] = jnp.exp(b_end + u["m_st"] - m_new)
        u["ks"] = u["k"] * jnp.exp(a - m_new)
        u["kv"] = _dot_tn(u["ks"].astype(BF16), u["vb"])
    for u in units:
        bb, h = u["bb"], u["h"]
        num = u["w_inter"] * u["qc"] + u["pv"]
        den = (u["w_inter"] * jnp.sum(u["q"] * u["nst"], axis=-1, keepdims=True)
               + jnp.sum(u["p"], axis=-1, keepdims=True))
        hc = num / jnp.maximum(jnp.abs(den), jnp.exp(-u["m_t"]))
        c_ref[bb, h] = u["cst"] * u["keep"] + u["kv"]
        n_ref[bb, h] = u["nst"] * u["keep"] + jnp.sum(u["ks"], axis=0, keepdims=True)
        m_ref[bb, h] = u["m_new"]
        hn = (_rms(hc, nrm_ref[:, h * ML_DV:(h + 1) * ML_DV])
              * _sigmoid(og_ref[bb, :, h * ML_DV:(h + 1) * ML_DV]))
        o_ref[bb, :, h * ML_DV:(h + 1) * ML_DV] = hn.astype(o_ref.dtype)


def _mlstm(mq, mk, mv, mo, gates, bias_row, norm_row, batch, seq):
    c = ML_CHUNK
    nc = seq // c
    nb = SEQS_PER_STEP
    row = lambda b, i: (b, i, 0)
    fix = lambda b, i: (0, 0)
    wide = ML_H * LANES
    r3 = lambda a: a.reshape(batch, seq, a.shape[-1])
    out = pl.pallas_call(
        _mlstm_kernel,
        grid=(batch // nb, nc),
        in_specs=[pl.BlockSpec((nb, c, wide), row), pl.BlockSpec((nb, c, wide), row), pl.BlockSpec((nb, c, wide), row),
                  pl.BlockSpec((nb, c, wide), row), pl.BlockSpec((nb, c, LANES), row),
                  pl.BlockSpec((1, LANES), fix), pl.BlockSpec((1, wide), fix)],
        out_specs=pl.BlockSpec((nb, c, wide), row),
        out_shape=jax.ShapeDtypeStruct((batch, seq, wide), BF16),
        scratch_shapes=[pltpu.VMEM((nb, ML_H, LANES, ML_DV), F32), pltpu.VMEM((nb, ML_H, 1, LANES), F32),
                        pltpu.VMEM((nb, ML_H, 1, LANES), F32)],
        compiler_params=_cparams(("arbitrary", "arbitrary")),
        name="mlstm",
    )(r3(mq), r3(mk), r3(mv), r3(mo), r3(gates), bias_row, norm_row)
    return out.reshape(batch * seq, wide)


def _swa_kernel(q_ref, kc_ref, kp_ref, vc_ref, vp_ref, sink_ref, o_ref):
    w = WINDOW
    n = pl.program_id(1)
    scale = SWA_D ** -0.5
    qi = _iota2((w, w), 0)
    kj = _iota2((w, w), 1)
    mask_c = kj <= qi
    mask_p = jnp.logical_and(kj > qi, n > 0)
    grp = SWA_H // SWA_KV
    neg = -1e30
    units = [(bb, h) for bb in range(q_ref.shape[0]) for h in range(SWA_H)]
    scores = []
    for bb, h in units:
        g = h // grp
        q = q_ref[bb, :, h * LANES:(h + 1) * LANES]
        scores.append((_dot_nt(q, kc_ref[bb, :, g * LANES:(g + 1) * LANES]),
                       _dot_nt(q, kp_ref[bb, :, g * LANES:(g + 1) * LANES])))
    masked, tops, exps, dens, probs = [], [], [], [], {}
    for sc, sp in scores:
        masked.append((jnp.where(mask_c, sc * scale, neg), jnp.where(mask_p, sp * scale, neg)))
    for (bb, h), (s_c, s_p) in zip(units, masked):
        tops.append(jnp.maximum(jnp.max(jnp.maximum(s_c, s_p), axis=-1, keepdims=True), sink_ref[:, h:h + 1]))
    for (s_c, s_p), m in zip(masked, tops):
        exps.append((jnp.where(mask_c, jnp.exp(s_c - m), 0.0), jnp.where(mask_p, jnp.exp(s_p - m), 0.0)))
    ones_b = jnp.ones((w, LANES), BF16)
    for (bb, h), (p_c, p_p), m in zip(units, exps, tops):
        p_c, p_p = p_c.astype(BF16), p_p.astype(BF16)
        probs[bb, h] = (p_c, p_p)
        dens.append(_dot(p_c, ones_b) + _dot(p_p, ones_b) + jnp.exp(sink_ref[:, h:h + 1] - m))
    inv = {u: 1.0 / den for u, den in zip(units, dens)}
    for bb in range(q_ref.shape[0]):
        for pair in range(SWA_H // 2):
            acc = None
            for sub in range(2):
                h = 2 * pair + sub
                vcol = (2 * (h // grp) + sub) * LANES
                p_c, p_p = probs[bb, h]
                part = (_dot(p_c, vc_ref[bb, :, vcol:vcol + LANES]) + _dot(p_p, vp_ref[bb, :, vcol:vcol + LANES])) * inv[bb, h]
                acc = part if acc is None else acc + part
            o_ref[bb, :, pair * LANES:(pair + 1) * LANES] = acc.astype(o_ref.dtype)


def _swa(sq, sk, sv, sinks_row, batch, seq):
    w = WINDOW
    nb = seq // w
    ns = SEQS_PER_STEP
    wo = SWA_H * SWA_D
    cur = lambda b, n: (b, n, 0)
    prev = lambda b, n: (b, jnp.maximum(n - 1, 0), 0)
    r3 = lambda a: a.reshape(batch, seq, a.shape[-1])
    q3, k3, v3 = r3(sq), r3(sk), r3(sv)
    out = pl.pallas_call(
        _swa_kernel,
        grid=(batch // ns, nb),
        in_specs=[pl.BlockSpec((ns, w, sq.shape[1]), cur),
                  pl.BlockSpec((ns, w, sk.shape[1]), cur), pl.BlockSpec((ns, w, sk.shape[1]), prev),
                  pl.BlockSpec((ns, w, sv.shape[1]), cur), pl.BlockSpec((ns, w, sv.shape[1]), prev),
                  pl.BlockSpec((1, LANES), lambda b, n: (0, 0))],
        out_specs=pl.BlockSpec((ns, w, wo), cur),
        out_shape=jax.ShapeDtypeStruct((batch, seq, wo), BF16),
        compiler_params=_cparams(("arbitrary", "arbitrary")),
        name="swa",
    )(q3, k3, k3, v3, v3, sinks_row)
    return out.reshape(batch * seq, wo)


def _layer_norm(h, g, b):
    mu = jnp.mean(h, axis=-1, keepdims=True)
    d = h - mu
    var = jnp.mean(d * d, axis=-1, keepdims=True)
    return d * lax.rsqrt(var + LN_EPS) * g + b


def _outproj_kernel(x_ref, a1_ref, a2_ref, w_ref, g_ref, b_ref, o_ref, op_ref):
    k1 = a1_ref.shape[1]
    y = _dot(a1_ref[...].astype(BF16), w_ref[0:k1, :]) + _dot(a2_ref[...].astype(BF16), w_ref[k1:, :])
    h = _layer_norm(DN_ALPHA * x_ref[...] + y, g_ref[...], b_ref[...])
    o_ref[...] = h
    op_ref[...] = _pack_pairs(h)


def _outproj_ln(x, a1, a2, w, g, b, tm=512):
    t, d = x.shape
    row = lambda i: (i, 0)
    fix = lambda i: (0, 0)
    return pl.pallas_call(
        _outproj_kernel,
        grid=(t // tm,),
        in_specs=[pl.BlockSpec((tm, d), row), pl.BlockSpec((tm, a1.shape[1]), row), pl.BlockSpec((tm, a2.shape[1]), row),
                  pl.BlockSpec(w.shape, fix), pl.BlockSpec((1, d), fix), pl.BlockSpec((1, d), fix)],
        out_specs=[pl.BlockSpec((tm, d), row), pl.BlockSpec((tm, d // 2), row)],
        out_shape=[jax.ShapeDtypeStruct((t, d), F32), jax.ShapeDtypeStruct((t, d // 2), jnp.uint32)],
        compiler_params=_cparams(("arbitrary",)),
        name="outproj_ln",
    )(x, a1, a2, w, g, b)


def _first_index(x, m, iota_f, sentinel):
    return jnp.min(jnp.where(x == m, iota_f, sentinel), axis=0, keepdims=True)


def _router_kernel(x_ref, wt_ref, bias_ref, idx_ref, gate_ref, rank_ref, cnt_ref, carry_ref):
    tm = x_ref.shape[0]
    e = N_EXPERTS
    gs = e // N_GROUPS
    ninf = -jnp.inf

    @pl.when(pl.program_id(0) == 0)
    def _():
        carry_ref[...] = jnp.zeros(carry_ref.shape, F32)

    logits = _dot3(wt_ref[...], x_ref[...], _dot_nt)
    scores = _sigmoid(logits)
    sel = scores + bias_ref[:, 0:1]

    sub_f = _iota2((gs, tm), 0).astype(F32)
    gscore = []
    for g in range(N_GROUPS):
        blk = sel[g * gs:(g + 1) * gs, :]
        m1 = jnp.max(blk, axis=0, keepdims=True)
        i1 = _first_index(blk, m1, sub_f, float(gs))
        m2 = jnp.max(jnp.where(sub_f == i1, ninf, blk), axis=0, keepdims=True)
        gscore.append(m1 + m2)
    gsc = jnp.concatenate(gscore, axis=0)
    grp_f = _iota2((N_GROUPS, tm), 0).astype(F32)
    gmask = jnp.zeros((N_GROUPS, tm), F32)
    for _ in range(TOPK_GROUPS):
        m = jnp.max(gsc, axis=0, keepdims=True)
        gi = _first_index(gsc, m, grp_f, float(N_GROUPS))
        hit = grp_f == gi
        gmask = jnp.where(hit, 1.0, gmask)
        gsc = jnp.where(hit, ninf, gsc)
    masked = jnp.concatenate(
        [jnp.where(gmask[g:g + 1, :] > 0.0, sel[g * gs:(g + 1) * gs, :], ninf) for g in range(N_GROUPS)], axis=0)

    exp_f = _iota2((e, tm), 0).astype(F32)
    chosen = jnp.zeros((e, tm), F32)
    idxs, gates = [], []
    for _ in range(TOP_K):
        m = jnp.max(masked, axis=0, keepdims=True)
        ei = _first_index(masked, m, exp_f, float(e))
        hit = exp_f == ei
        idxs.append(ei)
        gates.append(jnp.sum(jnp.where(hit, scores, 0.0), axis=0, keepdims=True))
        chosen = jnp.where(hit, 1.0, chosen)
        masked = jnp.where(hit, ninf, masked)
    gate = jnp.concatenate(gates, axis=0)
    gate = gate / jnp.sum(gate, axis=0, keepdims=True) * ROUTED_SCALE
    idx_f = jnp.concatenate(idxs, axis=0)

    upper = (_iota2((tm, tm), 0) < _iota2((tm, tm), 1)).astype(BF16)
    before = _dot(chosen.astype(BF16), upper) + carry_ref[...][:, 0:1]
    ranks = [jnp.sum(jnp.where(exp_f == idxs[k], before, 0.0), axis=0, keepdims=True) for k in range(TOP_K)]
    carry_ref[...] = carry_ref[...] + jnp.sum(chosen, axis=1, keepdims=True)

    idx_ref[...] = idx_f.astype(jnp.int32)
    gate_ref[...] = gate
    rank_ref[...] = jnp.concatenate(ranks, axis=0).astype(jnp.int32)
    cnt_ref[...] = carry_ref[...]


def _router(x, wt, bias_col, tm=512):
    t, d = x.shape
    col = lambda i: (0, i)
    fix = lambda i: (0, 0)
    return pl.pallas_call(
        _router_kernel,
        grid=(t // tm,),
        in_specs=[pl.BlockSpec((tm, d), lambda i: (i, 0)), pl.BlockSpec(wt.shape, fix), pl.BlockSpec((N_EXPERTS, LANES), fix)],
        out_specs=[pl.BlockSpec((TOP_K, tm), col), pl.BlockSpec((TOP_K, tm), col), pl.BlockSpec((TOP_K, tm), col),
                   pl.BlockSpec((N_EXPERTS, LANES), fix)],
        out_shape=[jax.ShapeDtypeStruct((TOP_K, t), jnp.int32), jax.ShapeDtypeStruct((TOP_K, t), F32),
                   jax.ShapeDtypeStruct((TOP_K, t), jnp.int32), jax.ShapeDtypeStruct((N_EXPERTS, LANES), F32)],
        scratch_shapes=[pltpu.VMEM((N_EXPERTS, LANES), F32)],
        compiler_params=_cparams(("arbitrary",)),
        name="router",
    )(x, wt, bias_col)


def _dest_kernel(idx_ref, rank_ref, start_ref, dest_ref):
    tm = idx_ref.shape[1]
    exp_i = _iota2((N_EXPERTS, tm), 0)
    start = start_ref[:, 0:1]
    rows = [jnp.sum(jnp.where(exp_i == idx_ref[s:s + 1, :], start, 0.0), axis=0, keepdims=True) for s in range(TOP_K)]
    dest_ref[...] = jnp.concatenate(rows, axis=0).astype(jnp.int32) + rank_ref[...]


def _dest_rows(idx, rank, start_col, tm=2048):
    t = idx.shape[1]
    tm = min(tm, t)
    col = lambda i: (0, i)
    return pl.pallas_call(
        _dest_kernel,
        grid=(t // tm,),
        in_specs=[pl.BlockSpec((TOP_K, tm), col), pl.BlockSpec((TOP_K, tm), col),
                  pl.BlockSpec((N_EXPERTS, LANES), lambda i: (0, 0))],
        out_specs=pl.BlockSpec((TOP_K, tm), col),
        out_shape=jax.ShapeDtypeStruct((TOP_K, t), jnp.int32),
        compiler_params=_cparams(("arbitrary",)),
        name="moe_dest",
    )(idx, rank, start_col)


def _pack_pairs(x):
    n = x.shape[1] // 2
    hi = lax.bitcast_convert_type(x[:, :n].astype(BF16).astype(F32), jnp.uint32)
    lo = lax.bitcast_convert_type(x[:, n:].astype(BF16).astype(F32), jnp.uint32)
    return hi | (lo >> 16)


def _unpack_pairs(w):
    hi = lax.bitcast_convert_type(w & jnp.uint32(0xFFFF0000), F32)
    lo = lax.bitcast_convert_type(w << 16, F32)
    return hi, lo


def _sc_scatter_rows(xp, dest, rows, chunk=LANES):
    t, width = xp.shape
    info = plsc.get_sparse_core_info()
    ncores, nsub = info.num_cores, info.num_subcores
    per_worker = t // (ncores * nsub)
    nchunk = per_worker // chunk
    mesh = plsc.VectorSubcoreMesh(core_axis_name="c", subcore_axis_name="s")

    @functools.partial(
        pl.kernel, mesh=mesh,
        out_type=jax.ShapeDtypeStruct((rows, width), xp.dtype),
        scratch_types=[pltpu.VMEM((TOP_K, chunk), jnp.int32), pltpu.VMEM((chunk, width), xp.dtype), pltpu.SemaphoreType.DMA],
    )
    def scatter(xp_hbm, dest_hbm, out_hbm, idx_v, rows_v, sem):
        base = (lax.axis_index("s") * ncores + lax.axis_index("c")) * per_worker

        @pl.loop(0, nchunk)
        def _(i):
            off = pl.multiple_of(base + i * chunk, chunk)
            pltpu.sync_copy(dest_hbm.at[:, pl.ds(off, chunk)], idx_v)
            pltpu.sync_copy(xp_hbm.at[pl.ds(off, chunk)], rows_v)
            copies = [pltpu.async_copy(rows_v, out_hbm.at[idx_v.at[s]], sem) for s in range(TOP_K)]
            for cp in copies:
                cp.wait()

    return scatter(xp, dest)


def _experts_kernel(be_ref, nu_ref, nv_ref, xs_ref, wg_ref, wu_ref, wd_ref, ys_ref, wgb_ref, wub_ref, wdb_ref):
    i = pl.program_id(0)

    @pl.when(jnp.logical_or(i == 0, be_ref[i] != be_ref[jnp.maximum(i - 1, 0)]))
    def _():
        wgb_ref[...] = wg_ref[0, 0].astype(BF16)
        wub_ref[...] = wu_ref[0, 0].astype(BF16)
        wdb_ref[...] = wd_ref[0, 0].astype(BF16)

    @pl.when(i < nu_ref[0])
    def _():
        half = xs_ref.shape[1]
        sub = xs_ref.shape[0] // EXPERT_SUBBLOCKS
        acts = []
        for r in range(EXPERT_SUBBLOCKS):
            rows = pl.ds(r * sub, sub)
            live = (_iota2((sub, 1), 0) + r * sub) < nv_ref[i]
            xa, xb = _unpack_pairs(jnp.where(live, xs_ref[rows, :], jnp.uint32(0)))
            xa = xa.astype(BF16)
            xb = xb.astype(BF16)
            gate = _dot(xa, wgb_ref[:half, :]) + _dot(xb, wgb_ref[half:, :])
            up = _dot(xa, wub_ref[:half, :]) + _dot(xb, wub_ref[half:, :])
            acts.append((gate, up))
        outs = [_dot((_silu(gate) * up).astype(BF16), wdb_ref[...]) for gate, up in acts]
        for r, y in enumerate(outs):
            ys_ref[pl.ds(r * sub, sub), :] = _pack_pairs(y)

    @pl.when(i >= nu_ref[0])
    def _():
        ys_ref[...] = jnp.zeros(ys_ref.shape, ys_ref.dtype)


def _experts(block_e, n_used, n_valid, xs, wg, wu, wd, layer):
    rows, half = xs.shape
    d = 2 * half
    nb = rows // EXPERT_BLOCK
    blk = lambda i, be, nu, nv: (jnp.minimum(i, nu[0] - 1), 0)
    wsel = lambda i, be, nu, nv: (layer, be[i], 0, 0)
    return pl.pallas_call(
        _experts_kernel,
        grid_spec=pltpu.PrefetchScalarGridSpec(
            num_scalar_prefetch=3,
            grid=(nb,),
            in_specs=[pl.BlockSpec((EXPERT_BLOCK, half), blk),
                      pl.BlockSpec((1, 1, d, D_EXPERT), wsel), pl.BlockSpec((1, 1, d, D_EXPERT), wsel),
                      pl.BlockSpec((1, 1, D_EXPERT, d), wsel)],
            out_specs=pl.BlockSpec((EXPERT_BLOCK, half), lambda i, be, nu, nv: (i, 0)),
            scratch_shapes=[pltpu.VMEM((d, D_EXPERT), BF16), pltpu.VMEM((d, D_EXPERT), BF16),
                            pltpu.VMEM((D_EXPERT, d), BF16)],
        ),
        out_shape=jax.ShapeDtypeStruct((rows, half), jnp.uint32),
        compiler_params=_cparams(("arbitrary",)),
        name="moe_experts",
    )(block_e, n_used, n_valid, xs, wg, wu, wd)


def _sc_gather_rows(table, idx, chunk=SC_CHUNK):
    n = idx.shape[0]
    width = table.shape[1]
    info = plsc.get_sparse_core_info()
    ncores, nsub = info.num_cores, info.num_subcores
    per_worker = n // (ncores * nsub)
    nchunk = per_worker // chunk
    mesh = plsc.VectorSubcoreMesh(core_axis_name="c", subcore_axis_name="s")

    @functools.partial(
        pl.kernel, mesh=mesh,
        out_type=jax.ShapeDtypeStruct((n, width), table.dtype),
        scratch_types=[pltpu.VMEM((nchunk, chunk), jnp.int32), pltpu.VMEM((2, chunk, width), table.dtype),
                       pltpu.SemaphoreType.DMA((2,)), pltpu.SemaphoreType.DMA((2,))],
    )
    def gather(table_hbm, idx_hbm, out_hbm, idx_v, rows_v, gsem, wsem):
        wid = lax.axis_index("s") * ncores + lax.axis_index("c")
        base = wid * per_worker
        pltpu.sync_copy(idx_hbm.at[pl.ds(wid * nchunk, nchunk)], idx_v)

        def fetch(j, b):
            return pltpu.make_async_copy(table_hbm.at[idx_v.at[j]], rows_v.at[b], gsem.at[b])

        def flush(j, b):
            off = pl.multiple_of(base + j * chunk, chunk)
            return pltpu.make_async_copy(rows_v.at[b], out_hbm.at[pl.ds(off, chunk)], wsem.at[b])

        fetch(0, 0).start()

        @pl.loop(0, nchunk, step=2)
        def _(i):
            for b in range(2):
                j = i + b
                fetch(j, b).wait()

                @pl.when(j + 1 < nchunk)
                def _():
                    @pl.when(j >= 1)
                    def _():
                        flush(j - 1, 1 - b).wait()

                    fetch(j + 1, 1 - b).start()

                flush(j, b).start()

        flush(nchunk - 2, 0).wait()
        flush(nchunk - 1, 1).wait()

    return gather(table, idx.reshape(n // chunk, chunk))


def _combine_kernel(x_ref, gate_ref, rows_ref, sg_ref, su_ref, sd_ref, g_ref, b_ref, o_ref):
    x = x_ref[...]
    xb = x.astype(BF16)
    hs = _silu(_dot(xb, sg_ref[...])) * _dot(xb, su_ref[...])
    ff = _dot(hs.astype(BF16), sd_ref[...])
    gate = gate_ref[...]
    half = rows_ref.shape[2]
    ya = ff[:, :half]
    yb = ff[:, half:]
    for s in range(TOP_K):
        a, b = _unpack_pairs(rows_ref[s])
        ya = ya + gate[:, s:s + 1] * a
        yb = yb + gate[:, s:s + 1] * b
    ff = jnp.concatenate([ya, yb], axis=1)
    o_ref[...] = _layer_norm(DN_ALPHA * x + ff, g_ref[...], b_ref[...])


def _combine_alias_kernel(prev_ref, *refs):
    del prev_ref
    _combine_kernel(*refs)


def _combine(x, gate_t, rows, sg, su, sd, g, b, part, nparts, prev=None, tm=512):
    t, d = x.shape
    tp = t // nparts
    tm = min(tm, tp)
    first = part * (tp // tm)
    row = lambda i: (first + i, 0)
    fix = lambda i: (0, 0)
    in_specs = [pl.BlockSpec((tm, d), row), pl.BlockSpec((tm, TOP_K), row),
                pl.BlockSpec((TOP_K, tm, d // 2), lambda i: (0, i, 0)),
                pl.BlockSpec(sg.shape, fix), pl.BlockSpec(su.shape, fix), pl.BlockSpec(sd.shape, fix),
                pl.BlockSpec((1, d), fix), pl.BlockSpec((1, d), fix)]
    args = (x, gate_t, rows, sg, su, sd, g, b)
    if prev is None:
        body, aliases = _combine_kernel, {}
    else:
        body, aliases = _combine_alias_kernel, {0: 0}
        in_specs = [pl.BlockSpec(memory_space=pl.ANY)] + in_specs
        args = (prev,) + args
    return pl.pallas_call(
        body,
        grid=(tp // tm,),
        in_specs=in_specs,
        out_specs=pl.BlockSpec((tm, d), row),
        out_shape=jax.ShapeDtypeStruct((t, d), F32),
        input_output_aliases=aliases,
        compiler_params=_cparams(("arbitrary",)),
        name="moe_combine",
    )(*args)


def _take_cols(w, idx):
    wz = jnp.concatenate([w, jnp.zeros((w.shape[0], 1), w.dtype)], axis=1)
    idx = np.where(np.asarray(idx) < 0, w.shape[1], np.asarray(idx))
    return jnp.take(wz, jnp.asarray(idx, jnp.int32), axis=1)


def _pad_lane_row(v, first_lane, width=LANES):
    out = jnp.zeros((1, width), F32)
    return lax.dynamic_update_slice(out, v.reshape(1, -1).astype(F32), (0, first_lane))


def _even_in_cols():
    z = lambda n: -np.ones(n, int)
    kr0 = Q_LORA + KV_LORA
    half = MLA_ROPE // 2
    cols = [np.arange(0, Q_LORA), np.arange(Q_LORA, Q_LORA + KV_LORA),
            z(64), np.arange(kr0, kr0 + MLA_ROPE), z(32),
            z(64), np.arange(kr0 + half, kr0 + MLA_ROPE), np.arange(kr0, kr0 + half), z(32)]
    g0 = kr0 + MLA_ROPE
    nqk = GDN_H * GDN_DK
    cols.append(np.arange(g0, g0 + 3 * nqk))
    zoff = g0 + 3 * nqk + 2 * GDN_H
    cols.append(np.arange(zoff, zoff + GDN_H * GDN_DV))
    cols += [np.arange(g0 + 3 * nqk, g0 + 3 * nqk + 2 * GDN_H), z(LANES - 2 * GDN_H)]
    return np.concatenate(cols)


EV_WIDTHS = (Q_LORA + KV_LORA + 2 * LANES, 3 * GDN_H * GDN_DK, GDN_H * GDN_DV, LANES)


def _mla_q_cols():
    per = MLA_NOPE + MLA_ROPE
    half = MLA_ROPE // 2
    main, sw = [], []
    for h in range(MLA_H):
        b = h * per
        main += [np.arange(b, b + per), -np.ones(LANES - per, int)]
        sw += [-np.ones(MLA_NOPE, int), np.arange(b + MLA_NOPE + half, b + per), np.arange(b + MLA_NOPE, b + MLA_NOPE + half),
               -np.ones(LANES - per, int)]
    return np.concatenate(main + sw)


def _mla_kv_cols():
    per = MLA_NOPE + MLA_V
    kc, vc = [], []
    for h in range(MLA_H):
        b = h * per
        kc += [np.arange(b, b + MLA_NOPE), -np.ones(LANES - MLA_NOPE, int)]
        vv = np.arange(b + MLA_NOPE, b + per)
        pad = -np.ones(LANES - MLA_V, int)
        vc += [vv, pad] if h % 2 == 0 else [pad, vv]
    return np.concatenate(kc + vc)


def _odd_in_cols():
    z = lambda n: -np.ones(n, int)
    o = 0
    cols = []
    mq0, mk0 = 0, ML_H * ML_DK
    for base in (mq0, mk0):
        for h in range(ML_H):
            cols += [np.arange(base + h * ML_DK, base + (h + 1) * ML_DK), z(LANES - ML_DK)]
    mv0 = 2 * ML_H * ML_DK
    cols.append(np.arange(mv0, mv0 + ML_H * ML_DV))
    mi0 = mv0 + ML_H * ML_DV
    mo0 = mi0 + 2 * ML_H
    cols.append(np.arange(mo0, mo0 + ML_H * ML_DV))
    cols += [np.arange(mi0, mi0 + 2 * ML_H), z(LANES - 2 * ML_H)]
    sq0 = mo0 + ML_H * ML_DV
    sk0 = sq0 + SWA_H * SWA_D
    sv0 = sk0 + SWA_KV * SWA_D
    half = SWA_D // 2

    def heads(base, n, swapped):
        out = []
        for h in range(n):
            b = base + h * SWA_D
            if swapped:
                out += [np.arange(b + half, b + SWA_D), np.arange(b, b + half), z(LANES - SWA_D)]
            else:
                out += [np.arange(b, b + SWA_D), z(LANES - SWA_D)]
        return out

    cols += heads(sq0, SWA_H, False) + heads(sq0, SWA_H, True) + heads(sk0, SWA_KV, False) + heads(sk0, SWA_KV, True)
    for g in range(SWA_KV):
        vv = np.arange(sv0 + g * SWA_D, sv0 + (g + 1) * SWA_D)
        cols += [vv, z(LANES - SWA_D), z(LANES - SWA_D), vv]
    return np.concatenate(cols)


def _even_mixer(x, tabs, w_in, q_norm, w_qb, kv_norm, w_kvb, conv_w, a_log, dt_bias, o_norm, batch, seq):
    ctab, stab = tabs
    w = _take_cols(w_in, _even_in_cols()).astype(BF16)
    mla_in, act, z, gates = _proj_even(x, w, conv_w, seq)
    wq2 = _take_cols(w_qb, _mla_q_cols()).astype(BF16)
    wkv2 = _take_cols(w_kvb, _mla_kv_cols()).astype(BF16)
    q, k, v = _mla_prep(mla_in, ctab, stab, q_norm.reshape(1, -1), kv_norm.reshape(1, -1), wq2, wkv2)
    o_a = _mla_attn(q, k, v, batch, seq)
    o_b = _gdn(act, gates, z, _pad_lane_row(a_log, GDN_H), _pad_lane_row(dt_bias, GDN_H),
               o_norm.reshape(1, -1), batch, seq)
    return o_a, o_b


def _odd_mixer(x, tabs, w_in, b_i, b_f, ml_norm, sinks, batch, seq):
    ctab, stab = tabs
    w = _take_cols(w_in, _odd_in_cols()).astype(BF16)
    mq, mk, mv, mo, mg, sq, sk, sv = _proj_odd(x, w, ctab, stab)
    bias_row = _pad_lane_row(jnp.concatenate([b_i, b_f]), 0)
    o_c = _mlstm(mq, mk, mv, mo, mg, bias_row, ml_norm.reshape(1, -1), batch, seq)
    o_d = _swa(sq, sk, sv, _pad_lane_row(sinks, 0), batch, seq)
    return o_c, o_d


def _moe(x, xp, router_w, router_b, w_gate, w_up, w_down, layer, s_gate, s_up, s_down, ln_g, ln_b):
    t, d = x.shape
    bias_col = jnp.broadcast_to(router_b.reshape(-1, 1).astype(F32), (N_EXPERTS, LANES))
    idx, gate, rank, cnt = _router(x, router_w.T, bias_col)
    counts = cnt[:, 0].astype(jnp.int32)
    padded = (counts + EXPERT_BLOCK - 1) // EXPERT_BLOCK * EXPERT_BLOCK
    pad_end = jnp.cumsum(padded)
    pad_start = pad_end - padded
    start_col = jnp.broadcast_to(pad_start.astype(F32).reshape(-1, 1), (N_EXPERTS, LANES))
    dest = _dest_rows(idx, rank, start_col)
    n_blocks = t * TOP_K // EXPERT_BLOCK + N_EXPERTS
    rows = n_blocks * EXPERT_BLOCK
    block_row = jnp.arange(n_blocks, dtype=jnp.int32) * EXPERT_BLOCK
    block_e = jnp.minimum(jnp.sum((pad_end[None, :] <= block_row[:, None]).astype(jnp.int32), axis=1), N_EXPERTS - 1)
    n_used = (pad_end[-1:] // EXPERT_BLOCK).astype(jnp.int32)
    live_end = jnp.sum(jnp.where(block_e[:, None] == jnp.arange(N_EXPERTS, dtype=jnp.int32)[None, :],
                                 (pad_start + counts)[None, :], 0), axis=1)
    n_valid = jnp.clip(live_end - block_row, 0, EXPERT_BLOCK).astype(jnp.int32)
    xs = _sc_scatter_rows(xp, dest, rows)
    ys = _experts(block_e, n_used, n_valid, xs, w_gate, w_up, w_down, layer)
    nparts = COMBINE_PARTS if t % (COMBINE_PARTS * 512) == 0 else 1
    tp = t // nparts
    gate_t = gate.T
    sgb, sub, sdb = s_gate.astype(BF16), s_up.astype(BF16), s_down.astype(BF16)
    out = None
    for part in range(nparts):
        idx_p = dest[:, part * tp:(part + 1) * tp].reshape(-1)
        picked = _sc_gather_rows(ys, idx_p).reshape(TOP_K, tp, d // 2)
        out = _combine(x, gate_t, picked, sgb, sub, sdb, ln_g.reshape(1, -1), ln_b.reshape(1, -1), part, nparts, prev=out)
    return out


def kernel(x, positions, ev_w_in, mla_q_norm, mla_w_qb, mla_kv_norm, mla_w_kvb, gdn_conv, gdn_a_log, gdn_dt_bias, gdn_norm, ev_w_out, od_w_in, mlstm_b_i, mlstm_b_f, mlstm_norm, swa_sinks, od_w_out, ln1_g, ln1_b, router_w, router_b, moe_w_gate, moe_w_up, moe_w_down, shared_w_gate, shared_w_up, shared_w_down, ln2_g, ln2_b):
    batch, seq, d = x.shape
    t = batch * seq
    pos = positions.reshape(t, 1).astype(F32)
    tabs_m = _rope_tables(pos, _rope_rows(MLA_ROPE, MLA_NOPE, MLA_NOPE))
    tabs_s = _rope_tables(pos, _rope_rows(SWA_D, 0, 0))
    h = x.reshape(t, d)
    for layer in range(DEPTH):
        j = layer // 2
        if layer % 2 == 0:
            a1, a2 = _even_mixer(h, tabs_m, ev_w_in[j], mla_q_norm[j], mla_w_qb[j], mla_kv_norm[j], mla_w_kvb[j],
                                 gdn_conv[j], gdn_a_log[j], gdn_dt_bias[j], gdn_norm[j], batch, seq)
            w_out = ev_w_out[j]
        else:
            a1, a2 = _odd_mixer(h, tabs_s, od_w_in[j], mlstm_b_i[j], mlstm_b_f[j], mlstm_norm[j], swa_sinks[j], batch, seq)
            w_out = od_w_out[j]
        h, hp = _outproj_ln(h, a1, a2, w_out.astype(BF16), ln1_g[layer].reshape(1, -1), ln1_b[layer].reshape(1, -1))
        h = _moe(h, hp, router_w[layer], router_b[layer], moe_w_gate, moe_w_up, moe_w_down, layer,
                 shared_w_gate[layer], shared_w_up[layer], shared_w_down[layer], ln2_g[layer], ln2_b[layer])
    return h.reshape(batch, seq, d)
```

```python
import functools
import math

import numpy as np
import jax
import jax.numpy as jnp
from jax import lax
from jax.experimental import pallas as pl
from jax.experimental.pallas import tpu as pltpu
from jax.experimental.pallas import tpu_sc as plsc

F32 = jnp.float32
BF16 = jnp.bfloat16
HI = lax.Precision.HIGHEST

D_MODEL = 1024
DEPTH = 4
ROPE_THETA = 10000.0
EPS = 1e-6
LN_EPS = 1e-5
MLA_H, MLA_NOPE, MLA_ROPE, MLA_V = 8, 64, 32, 64
Q_LORA, KV_LORA = 256, 128
GDN_H, GDN_DK, GDN_DV, CONV_W, GDN_CHUNK = 4, 128, 128, 4, 64
ML_H, ML_DK, ML_DV, ML_CHUNK = 4, 64, 128, 64
SWA_H, SWA_KV, SWA_D, WINDOW = 8, 2, 64, 128
N_EXPERTS, N_GROUPS, TOPK_GROUPS, TOP_K = 64, 8, 4, 8
D_EXPERT, D_SHARED = 256, 256
ROUTED_SCALE = 2.5
DN_ALPHA = (2 * DEPTH) ** 0.25

LANES = 128
V7X_VMEM_BYTES = 64 * 1024 * 1024
VMEM_LIMIT = 48 * 1024 * 1024

EXPERT_BLOCK = 512
EXPERT_SUBBLOCKS = 2
COMBINE_PARTS = 1
SEQS_PER_STEP = 2
MLSTM_SEQS_PER_STEP = 2
DMA_GROUP = 2
SC_CHUNK = 64


def _cparams(sem, vmem=VMEM_LIMIT):
    return pltpu.CompilerParams(dimension_semantics=sem, vmem_limit_bytes=vmem)


def _dot(a, b, precision=None):
    return jnp.dot(a, b, preferred_element_type=F32, precision=precision)


def _dot_nt(a, b, precision=None):
    return lax.dot_general(a, b, (((1,), (1,)), ((), ())), preferred_element_type=F32, precision=precision)


def _dot_tn(a, b, precision=None):
    return lax.dot_general(a, b, (((0,), (0,)), ((), ())), preferred_element_type=F32, precision=precision)


def _split2(a):
    hi = a.astype(BF16)
    lo = (a - hi.astype(F32)).astype(BF16)
    return hi, lo


def _split3(a):
    p1 = a.astype(BF16)
    r = a - p1.astype(F32)
    p2 = r.astype(BF16)
    p3 = (r - p2.astype(F32)).astype(BF16)
    return p1, p2, p3


def _dot3(a, b, dot=_dot):
    ah, al = _split2(a)
    bh, bl = _split2(b)
    return dot(ah, bh) + (dot(ah, bl) + dot(al, bh))


def _dot_sel(sel, b, dot=_dot):
    sel = sel.astype(BF16)
    p1, p2, p3 = _split3(b)
    return dot(sel, p1) + (dot(sel, p2) + dot(sel, p3))


def _sigmoid(x):
    return 1.0 / (1.0 + jnp.exp(-x))


def _softplus(x):
    return jnp.maximum(x, 0.0) + jnp.log(1.0 + jnp.exp(-jnp.abs(x)))


def _silu(x):
    return x * _sigmoid(x)


def _lane_bcast(x, c):
    return jnp.broadcast_to(x[:, c:c + 1], x.shape)


def _iota2(shape, dim):
    return lax.broadcasted_iota(jnp.int32, shape, dim)


def _rope_kernel(pos_ref, rows_ref, c_ref, s_ref):
    ang = pos_ref[...] * rows_ref[0:1, :]
    c_ref[...] = rows_ref[1:2, :] * jnp.cos(ang) + rows_ref[2:3, :]
    s_ref[...] = rows_ref[3:4, :] * jnp.sin(ang)


def _rope_tables(pos, rows, tm=512):
    t = pos.shape[0]
    return pl.pallas_call(
        _rope_kernel,
        grid=(t // tm,),
        in_specs=[pl.BlockSpec((tm, 1), lambda i: (i, 0)), pl.BlockSpec((8, LANES), lambda i: (0, 0))],
        out_specs=[pl.BlockSpec((tm, LANES), lambda i: (i, 0))] * 2,
        out_shape=[jax.ShapeDtypeStruct((t, LANES), F32)] * 2,
        compiler_params=_cparams(("arbitrary",)),
        name="rope_tables",
    )(pos, rows)


def _rope_rows(dim, first_lane, pad_one_lanes, heads=1):
    half = dim // 2
    inv = ROPE_THETA ** (-(np.arange(0, dim, 2, dtype=np.float32) / dim))
    rows = np.zeros((8, LANES), np.float32)
    for h in range(heads):
        lo = slice(first_lane + h * dim, first_lane + h * dim + half)
        hi = slice(first_lane + h * dim + half, first_lane + (h + 1) * dim)
        rows[0, lo] = inv
        rows[0, hi] = inv
        rows[1, lo] = 1.0
        rows[1, hi] = 1.0
        rows[3, lo] = -1.0
        rows[3, hi] = 1.0
    rows[2, :pad_one_lanes] = 1.0
    return jnp.asarray(rows)


def _proj_kernel(x_ref, w_ref, *out_refs, offsets):
    xb = x_ref[...].astype(BF16)
    for o_ref, (a, b) in zip(out_refs, offsets):
        o_ref[...] = _dot(xb, w_ref[:, a:b]).astype(o_ref.dtype)


def _proj(x, w, widths, dtypes, tm=512):
    t, k = x.shape
    offs = np.concatenate([[0], np.cumsum(widths)]).tolist()
    offsets = tuple((offs[i], offs[i + 1]) for i in range(len(widths)))
    return pl.pallas_call(
        functools.partial(_proj_kernel, offsets=offsets),
        grid=(t // tm,),
        in_specs=[pl.BlockSpec((tm, k), lambda i: (i, 0)), pl.BlockSpec(w.shape, lambda i: (0, 0))],
        out_specs=[pl.BlockSpec((tm, n), lambda i: (i, 0)) for n in widths],
        out_shape=[jax.ShapeDtypeStruct((t, n), dt) for n, dt in zip(widths, dtypes)],
        compiler_params=_cparams(("arbitrary",)),
        name="in_proj",
    )(x, w)


def _proj_even_kernel(x_ref, w_ref, cw_ref, mla_ref, act_ref, z_ref, g_ref, ext_ref, *, tiles_per_seq):
    tm = x_ref.shape[0]
    o = np.concatenate([[0], np.cumsum(EV_WIDTHS)]).tolist()
    @pl.when(pl.program_id(0) % tiles_per_seq == 0)
    def _():
        ext_ref[0:8, :] = jnp.zeros((8, ext_ref.shape[1]), F32)

    xb = x_ref[...].astype(BF16)
    nchunk = 3
    cw = EV_WIDTHS[1] // nchunk

    def project(ci):
        ext_ref[8:8 + tm, ci * cw:(ci + 1) * cw] = _dot(xb, w_ref[:, o[1] + ci * cw:o[1] + (ci + 1) * cw])

    project(0)
    for ci in range(nchunk):
        if ci + 1 < nchunk:
            project(ci + 1)
        else:
            mla_ref[...] = _dot(xb, w_ref[:, o[0]:o[1]])
            z_ref[...] = _dot(xb, w_ref[:, o[2]:o[3]])
            g_ref[...] = _dot(xb, w_ref[:, o[3]:o[4]])
        cols = slice(ci * cw, (ci + 1) * cw)
        conv = cw_ref[0:1, cols] * ext_ref[5:5 + tm, cols]
        for j in range(1, CONV_W):
            conv = conv + cw_ref[j:j + 1, cols] * ext_ref[5 + j:5 + j + tm, cols]
        act_ref[:, cols] = _silu(conv)
    ext_ref[0:8, :] = ext_ref[tm:tm + 8, :]


def _proj_even(x, w, conv_w, seq, tm=512):
    t, k = x.shape
    tm = min(tm, seq)
    row = lambda i: (i, 0)
    fix = lambda i: (0, 0)
    return pl.pallas_call(
        functools.partial(_proj_even_kernel, tiles_per_seq=seq // tm),
        grid=(t // tm,),
        in_specs=[pl.BlockSpec((tm, k), row), pl.BlockSpec(w.shape, fix), pl.BlockSpec(conv_w.shape, fix)],
        out_specs=[pl.BlockSpec((tm, n), row) for n in EV_WIDTHS],
        out_shape=[jax.ShapeDtypeStruct((t, n), F32) for n in EV_WIDTHS],
        scratch_shapes=[pltpu.VMEM((tm + 8, EV_WIDTHS[1]), F32)],
        compiler_params=_cparams(("arbitrary",)),
        name="in_proj",
    )(x, w, conv_w)


OD_SEG = dict(mq=(0, 512), mk=(512, 1024), mv=(1024, 1536), mo=(1536, 2048), gates=(2048, 2176),
              sq=(2176, 2688), sqsw=(2688, 3200), sk=(3200, 3456), sksw=(3456, 3712), sv=(3712, 4224))
OD_COLS = 4224


def _proj_odd_kernel(x_ref, w_ref, c_ref, s_ref, mq_ref, mk_ref, mv_ref, mo_ref, mg_ref, sq_ref, sk_ref, sv_ref):
    xb = x_ref[...].astype(BF16)

    def seg(name):
        a, b = OD_SEG[name]
        return _dot(xb, w_ref[:, a:b])

    mq_ref[...] = seg("mq")
    mk_ref[...] = seg("mk")
    mv_ref[...] = seg("mv")
    mo_ref[...] = seg("mo")
    mg_ref[...] = seg("gates")
    c = c_ref[...]
    s = s_ref[...]
    c8 = jnp.concatenate([c] * (SWA_H // 2), axis=1)
    s8 = jnp.concatenate([s] * (SWA_H // 2), axis=1)
    sq_ref[...] = (seg("sq") * c8 + seg("sqsw") * s8).astype(sq_ref.dtype)
    c2 = jnp.concatenate([c] * SWA_KV, axis=1)
    s2 = jnp.concatenate([s] * SWA_KV, axis=1)
    sk_ref[...] = (seg("sk") * c2 + seg("sksw") * s2).astype(sk_ref.dtype)
    sv_ref[...] = seg("sv").astype(sv_ref.dtype)


def _proj_odd(x, w, ctab, stab, tm=256):
    t, k = x.shape
    widths = (512, 512, 512, 512, 128, SWA_H * SWA_D, SWA_KV * LANES, 2 * SWA_KV * LANES)
    dtypes = (F32, F32, F32, F32, F32, BF16, BF16, BF16)
    return pl.pallas_call(
        _proj_odd_kernel,
        grid=(t // tm,),
        in_specs=[pl.BlockSpec((tm, k), lambda i: (i, 0)), pl.BlockSpec(w.shape, lambda i: (0, 0)),
                  pl.BlockSpec((tm, LANES), lambda i: (i, 0)), pl.BlockSpec((tm, LANES), lambda i: (i, 0))],
        out_specs=[pl.BlockSpec((tm, n), lambda i: (i, 0)) for n in widths],
        out_shape=[jax.ShapeDtypeStruct((t, n), dt) for n, dt in zip(widths, dtypes)],
        compiler_params=_cparams(("arbitrary",)),
        name="in_proj_odd",
    )(x, w, ctab, stab)


def _rms(x, g):
    return x * lax.rsqrt(jnp.mean(x * x, axis=-1, keepdims=True) + EPS) * g


def _mla_prep_kernel(in_ref, c_ref, s_ref, qn_ref, kvn_ref, wq_ref, wkv_ref, q_ref, k_ref, v_ref):
    hw = MLA_H * LANES
    c = c_ref[...]
    s = s_ref[...]
    c8 = jnp.concatenate([c] * MLA_H, axis=1)
    s8 = jnp.concatenate([s] * MLA_H, axis=1)
    cqn = _rms(in_ref[:, 0:Q_LORA], qn_ref[...]).astype(BF16)
    qq = _dot(cqn, wq_ref[...])
    scale = (MLA_NOPE + MLA_ROPE) ** -0.5
    q_ref[...] = ((qq[:, :hw] * c8 + qq[:, hw:] * s8) * scale).astype(q_ref.dtype)
    ckvn = _rms(in_ref[:, Q_LORA:Q_LORA + KV_LORA], kvn_ref[...]).astype(BF16)
    kv = _dot(ckvn, wkv_ref[...])
    o = Q_LORA + KV_LORA
    krr = in_ref[:, o:o + LANES] * c + in_ref[:, o + LANES:o + 2 * LANES] * s
    k_ref[...] = (kv[:, :hw] + jnp.concatenate([krr] * MLA_H, axis=1)).astype(k_ref.dtype)
    v_ref[...] = kv[:, hw:].astype(v_ref.dtype)


def _mla_prep(mla_in, ctab, stab, qn, kvn, wq2, wkv2, tm=512):
    t = mla_in.shape[0]
    hw = MLA_H * LANES
    row = lambda i: (i, 0)
    fix = lambda i: (0, 0)
    return pl.pallas_call(
        _mla_prep_kernel,
        grid=(t // tm,),
        in_specs=[pl.BlockSpec((tm, mla_in.shape[1]), row), pl.BlockSpec((tm, LANES), row), pl.BlockSpec((tm, LANES), row),
                  pl.BlockSpec(qn.shape, fix), pl.BlockSpec(kvn.shape, fix),
                  pl.BlockSpec(wq2.shape, fix), pl.BlockSpec(wkv2.shape, fix)],
        out_specs=[pl.BlockSpec((tm, hw), row)] * 3,
        out_shape=[jax.ShapeDtypeStruct((t, hw), BF16)] * 3,
        compiler_params=_cparams(("arbitrary",)),
        name="mla_prep",
    )(mla_in, ctab, stab, qn, kvn, wq2, wkv2)


def _mla_attn_kernel(q_ref, k_ref, v_ref, o_ref, *, tq):
    i = pl.program_id(2)
    neg = -1e30
    lane = _iota2((tq, LANES), 1)
    ones_lane = (MLA_V, 0)

    def chunk(j, carry, masked):
        start = pl.multiple_of(j * tq, tq)
        out = []
        for hh in range(2):
            m, acc = carry[hh]
            q = q_ref[:, hh * LANES:(hh + 1) * LANES]
            kc = k_ref[pl.ds(start, tq), hh * LANES:(hh + 1) * LANES]
            vc = v_ref[pl.ds(start, tq), hh * LANES:(hh + 1) * LANES]
            vc = jnp.where(lane == ones_lane[hh], jnp.ones_like(vc), vc)
            s = _dot_nt(q, kc)
            if masked:
                s = jnp.where(_iota2(s.shape, 0) >= _iota2(s.shape, 1), s, neg)
            m_new = jnp.maximum(m, jnp.max(s, axis=-1, keepdims=True))
            alpha = jnp.exp(m - m_new)
            p = jnp.exp(s - m_new)
            acc = alpha * acc + _dot(p.astype(BF16), vc)
            out.append((m_new, acc))
        return tuple(out)

    one = (jnp.full((tq, 1), neg, F32), jnp.zeros((tq, LANES), F32))
    carry = lax.fori_loop(0, i, lambda j, c: chunk(j, c, False), (one, one))
    (_, acc0), (_, acc1) = chunk(i, carry, True)
    o0 = acc0 / _lane_bcast(acc0, ones_lane[0])
    o1 = acc1 / _lane_bcast(acc1, ones_lane[1])
    o_ref[...] = jnp.where(lane < MLA_V, o0, o1).astype(o_ref.dtype)


def _mla_attn(q, k, v, batch, seq, tq=512):
    tq = min(tq, seq)
    nq = seq // tq
    pairs = MLA_H // 2
    return pl.pallas_call(
        functools.partial(_mla_attn_kernel, tq=tq),
        grid=(batch, pairs, nq),
        in_specs=[pl.BlockSpec((tq, 2 * LANES), lambda b, p, i: (b * nq + i, p)),
                  pl.BlockSpec((seq, 2 * LANES), lambda b, p, i: (b, p)),
                  pl.BlockSpec((seq, 2 * LANES), lambda b, p, i: (b, p))],
        out_specs=pl.BlockSpec((tq, LANES), lambda b, p, i: (b * nq + i, p)),
        out_shape=jax.ShapeDtypeStruct((batch * seq, pairs * LANES), BF16),
        compiler_params=_cparams(("arbitrary", "arbitrary", "arbitrary")),
        name="mla_attn",
    )(q, k, v)


def _unit_lower_inverse_many(ns):
    c = ns[0].shape[0]
    eye = (_iota2((c, c), 0) == _iota2((c, c), 1)).astype(F32)
    xs = [-n for n in ns]
    ps = [eye + x for x in xs]
    xb = [x.astype(BF16) for x in xs]
    for _ in range(int(math.log2(c)) - 1):
        xs = [_dot(b, b) for b in xb]
        xb = [x.astype(BF16) for x in xs]
        ps = [p + _dot(p.astype(BF16), b) for p, b in zip(ps, xb)]
    return ps


def _gdn_kernel(act_ref, g_ref, z_ref, al_ref, dt_ref, on_ref, o_ref, st_ref):
    c = GDN_CHUNK
    hd = GDN_DK
    nqk = GDN_H * GDN_DK

    @pl.when(pl.program_id(1) == 0)
    def _():
        st_ref[...] = jnp.zeros(st_ref.shape, F32)

    tri = (_iota2((c, c), 0) >= _iota2((c, c), 1)).astype(F32)
    row_ge = _iota2((c, c), 0) >= _iota2((c, c), 1)
    row_gt = _iota2((c, c), 0) > _iota2((c, c), 1)
    lane = _iota2((c, LANES), 1)

    units = []
    for bb in range(act_ref.shape[0]):
        act = act_ref[bb]
        gates = g_ref[bb]
        beta_all = _sigmoid(gates)
        g_all = -jnp.exp(al_ref[...]) * _softplus(gates + dt_ref[...])
        gc_all = _dot_sel(tri, g_all)
        gc_parts = _split3(gc_all)
        for h in range(GDN_H):
            q = act[:, h * hd:(h + 1) * hd]
            k = act[:, nqk + h * hd:nqk + (h + 1) * hd]
            v = act[:, 2 * nqk + h * GDN_DV:2 * nqk + (h + 1) * GDN_DV]
            q = q * lax.rsqrt(jnp.sum(q * q, axis=-1, keepdims=True) + EPS) * (GDN_DK ** -0.5)
            k = k * lax.rsqrt(jnp.sum(k * k, axis=-1, keepdims=True) + EPS)
            beta = _lane_bcast(beta_all, h)
            gcol = _lane_bcast(gc_all, GDN_H + h)
            pick = (lane == GDN_H + h).astype(BF16)
            grow = _dot_nt(pick, gc_parts[0]) + (_dot_nt(pick, gc_parts[1]) + _dot_nt(pick, gc_parts[2]))
            decay = jnp.exp(jnp.where(row_ge, gcol[:, :c] - grow, -jnp.inf))
            kb = k * beta
            lower = jnp.where(row_gt, _dot3(kb, k, _dot_nt) * decay, 0.0)
            eg = jnp.exp(gcol)
            glast = gcol[c - 1:c, :]
            units.append(dict(bb=bb, h=h, lower=lower, rhs=jnp.concatenate([v * beta, kb * eg], axis=1),
                              attn=_dot_nt(q.astype(BF16), k.astype(BF16)) * decay, qg=(q * eg).astype(BF16),
                              kg=(k * jnp.exp(glast - gcol)).astype(BF16), gl=jnp.exp(glast)))

    tinvs = _unit_lower_inverse_many([u["lower"] for u in units])
    uws = []
    for u, tinv in zip(units, tinvs):
        uws.append(_dot(tinv.astype(BF16), u["rhs"].astype(BF16)))
    states = [st_ref[u["bb"], u["h"]] for u in units]
    sbs = [s.astype(BF16) for s in states]
    vnews = [(uw[:, :GDN_DV] - _dot(uw[:, GDN_DV:].astype(BF16), sb)).astype(BF16) for uw, sb in zip(uws, sbs)]
    for u, state, sb, vnb in zip(units, states, sbs, vnews):
        bb, h = u["bb"], u["h"]
        o = _dot(u["qg"], sb) + _dot(u["attn"].astype(BF16), vnb)
        st_ref[bb, h] = state * u["gl"] + _dot_tn(u["kg"], vnb)
        o = _rms(o, on_ref[...]) * _silu(z_ref[bb, :, h * GDN_DV:(h + 1) * GDN_DV])
        o_ref[bb, :, h * GDN_DV:(h + 1) * GDN_DV] = o.astype(o_ref.dtype)


def _gdn(act, gates, z, a_row, dt_row, o_norm, batch, seq):
    c = GDN_CHUNK
    nc = seq // c
    w3 = act.shape[1]
    wo = GDN_H * GDN_DV
    nb = SEQS_PER_STEP
    row = lambda b, i: (b, i, 0)
    fix = lambda b, i: (0, 0)
    out = pl.pallas_call(
        _gdn_kernel,
        grid=(batch // nb, nc),
        in_specs=[pl.BlockSpec((nb, c, w3), row), pl.BlockSpec((nb, c, LANES), row), pl.BlockSpec((nb, c, wo), row),
                  pl.BlockSpec((1, LANES), fix), pl.BlockSpec((1, LANES), fix), pl.BlockSpec((1, GDN_DV), fix)],
        out_specs=pl.BlockSpec((nb, c, wo), row),
        out_shape=jax.ShapeDtypeStruct((batch, seq, wo), BF16),
        scratch_shapes=[pltpu.VMEM((nb, GDN_H, GDN_DK, GDN_DV), F32)],
        compiler_params=_cparams(("arbitrary", "arbitrary")),
        name="gdn",
    )(act.reshape(batch, seq, w3), gates.reshape(batch, seq, LANES), z.reshape(batch, seq, wo), a_row, dt_row, o_norm)
    return out.reshape(batch * seq, wo)


def _mlstm_kernel(q_ref, k_ref, v_ref, og_ref, g_ref, bias_ref, nrm_ref, o_ref, c_ref, n_ref, m_ref):
    @pl.when(pl.program_id(1) == 0)
    def _():
        c_ref[...] = jnp.zeros(c_ref.shape, F32)
        n_ref[...] = jnp.zeros(n_ref.shape, F32)
        m_ref[...] = jnp.zeros(m_ref.shape, F32)

    c = ML_CHUNK
    tri = (_iota2((c, c), 0) >= _iota2((c, c), 1)).astype(F32)
    row_ge = _iota2((c, c), 0) >= _iota2((c, c), 1)
    ones = jnp.ones((c, LANES), F32)
    lane = _iota2((c, LANES), 1)

    units = []
    for bb in range(q_ref.shape[0]):
        pre = g_ref[bb] + bias_ref[...]
        logf = jnp.minimum(pre, 0.0) - jnp.log(1.0 + jnp.exp(-jnp.abs(pre)))
        bcum_all = _dot_sel(tri, logf)
        for h in range(ML_H):
            q = q_ref[bb, :, h * LANES:(h + 1) * LANES]
            k = k_ref[bb, :, h * LANES:(h + 1) * LANES] * (ML_DK ** -0.5)
            units.append(dict(bb=bb, h=h, q=q, k=k, qb=q.astype(BF16), vb=v_ref[bb, :, h * ML_DV:(h + 1) * ML_DV].astype(BF16),
                              bcol=_lane_bcast(bcum_all, ML_H + h),
                              icol=_lane_bcast(pre, h),
                              col=jnp.where(lane == h, pre, 0.0) - jnp.where(lane == ML_H + h, bcum_all, 0.0),
                              m_st=m_ref[bb, h], cst=c_ref[bb, h], nst=n_ref[bb, h]))
    for u in units:
        u["row"] = _dot_sel(ones, u["col"], _dot_nt)
        u["qk"] = _dot_nt(u["qb"], u["k"].astype(BF16))
        u["qc"] = _dot(u["qb"], u["cst"].astype(BF16))
    for u in units:
        d = jnp.where(row_ge, u["bcol"][:, :c] + u["row"], -jnp.inf)
        inter = u["bcol"] + u["m_st"]
        m_t = jnp.maximum(inter, jnp.max(d, axis=-1, keepdims=True))
        u["m_t"] = m_t
        u["w_inter"] = jnp.exp(inter - m_t)
        u["p"] = jnp.exp(d - m_t[:, :c]) * u["qk"]
        u["pv"] = _dot(u["p"].astype(BF16), u["vb"])
        b_end = u["bcol"][c - 1:c, :]
        a = b_end - u["bcol"] + u["icol"]
        m_new = jnp.maximum(b_end + u["m_st"], jnp.max(a, axis=0, keepdims=True))
        u["m_new"] = m_new
        u["keep"] = jnp.exp(b_end + u["m_st"] - m_new)
        u["ks"] = u["k"] * jnp.exp(a - m_new)
        u["kv"] = _dot_tn(u["ks"].astype(BF16), u["vb"])
    for u in units:
        bb, h = u["bb"], u["h"]
        num = u["w_inter"] * u["qc"] + u["pv"]
        den = (u["w_inter"] * jnp.sum(u["q"] * u["nst"], axis=-1, keepdims=True)
               + jnp.sum(u["p"], axis=-1, keepdims=True))
        hc = num / jnp.maximum(jnp.abs(den), jnp.exp(-u["m_t"]))
        c_ref[bb, h] = u["cst"] * u["keep"] + u["kv"]
        n_ref[bb, h] = u["nst"] * u["keep"] + jnp.sum(u["ks"], axis=0, keepdims=True)
        m_ref[bb, h] = u["m_new"]
        hn = (_rms(hc, nrm_ref[:, h * ML_DV:(h + 1) * ML_DV])
              * _sigmoid(og_ref[bb, :, h * ML_DV:(h + 1) * ML_DV]))
        o_ref[bb, :, h * ML_DV:(h + 1) * ML_DV] = hn.astype(o_ref.dtype)


def _mlstm(mq, mk, mv, mo, gates, bias_row, norm_row, batch, seq):
    c = ML_CHUNK
    nc = seq // c
    nb = min(MLSTM_SEQS_PER_STEP, batch)
    row = lambda b, i: (b, i, 0)
    fix = lambda b, i: (0, 0)
    wide = ML_H * LANES
    r3 = lambda a: a.reshape(batch, seq, a.shape[-1])
    out = pl.pallas_call(
        _mlstm_kernel,
        grid=(batch // nb, nc),
        in_specs=[pl.BlockSpec((nb, c, wide), row), pl.BlockSpec((nb, c, wide), row), pl.BlockSpec((nb, c, wide), row),
                  pl.BlockSpec((nb, c, wide), row), pl.BlockSpec((nb, c, LANES), row),
                  pl.BlockSpec((1, LANES), fix), pl.BlockSpec((1, wide), fix)],
        out_specs=pl.BlockSpec((nb, c, wide), row),
        out_shape=jax.ShapeDtypeStruct((batch, seq, wide), BF16),
        scratch_shapes=[pltpu.VMEM((nb, ML_H, LANES, ML_DV), F32), pltpu.VMEM((nb, ML_H, 1, LANES), F32),
                        pltpu.VMEM((nb, ML_H, 1, LANES), F32)],
        compiler_params=_cparams(("arbitrary", "arbitrary")),
        name="mlstm",
    )(r3(mq), r3(mk), r3(mv), r3(mo), r3(gates), bias_row, norm_row)
    return out.reshape(batch * seq, wide)


def _swa_kernel(q_ref, kc_ref, kp_ref, vc_ref, vp_ref, sink_ref, o_ref):
    w = WINDOW
    n = pl.program_id(1)
    scale = SWA_D ** -0.5
    qi = _iota2((w, w), 0)
    kj = _iota2((w, w), 1)
    mask_c = kj <= qi
    mask_p = jnp.logical_and(kj > qi, n > 0)
    grp = SWA_H // SWA_KV
    neg = -1e30
    units = [(bb, h) for bb in range(q_ref.shape[0]) for h in range(SWA_H)]
    scores = []
    half_of_lane = _iota2((w, LANES), 1) // SWA_D
    for bb, h in units:
        g = h // grp
        pair = q_ref[bb, :, (h // 2) * LANES:(h // 2 + 1) * LANES]
        q = jnp.where(half_of_lane == h % 2, pair, jnp.zeros_like(pair))
        scores.append((_dot_nt(q, kc_ref[bb, :, g * LANES:(g + 1) * LANES]),
                       _dot_nt(q, kp_ref[bb, :, g * LANES:(g + 1) * LANES])))
    masked, tops, exps, dens, probs = [], [], [], [], {}
    for sc, sp in scores:
        masked.append((jnp.where(mask_c, sc * scale, neg), jnp.where(mask_p, sp * scale, neg)))
    for (bb, h), (s_c, s_p) in zip(units, masked):
        tops.append(jnp.maximum(jnp.max(jnp.maximum(s_c, s_p), axis=-1, keepdims=True), sink_ref[:, h:h + 1]))
    for (s_c, s_p), m in zip(masked, tops):
        exps.append((jnp.where(mask_c, jnp.exp(s_c - m), 0.0), jnp.where(mask_p, jnp.exp(s_p - m), 0.0)))
    ones_b = jnp.ones((w, LANES), BF16)
    for (bb, h), (p_c, p_p), m in zip(units, exps, tops):
        p_c, p_p = p_c.astype(BF16), p_p.astype(BF16)
        probs[bb, h] = (p_c, p_p)
        dens.append(_dot(p_c, ones_b) + _dot(p_p, ones_b) + jnp.exp(sink_ref[:, h:h + 1] - m))
    inv = {u: 1.0 / den for u, den in zip(units, dens)}
    for bb in range(q_ref.shape[0]):
        for pair in range(SWA_H // 2):
            acc = None
            for sub in range(2):
                h = 2 * pair + sub
                vcol = (2 * (h // grp) + sub) * LANES
                p_c, p_p = probs[bb, h]
                part = (_dot(p_c, vc_ref[bb, :, vcol:vcol + LANES]) + _dot(p_p, vp_ref[bb, :, vcol:vcol + LANES])) * inv[bb, h]
                acc = part if acc is None else acc + part
            o_ref[bb, :, pair * LANES:(pair + 1) * LANES] = acc.astype(o_ref.dtype)


def _swa(sq, sk, sv, sinks_row, batch, seq):
    w = WINDOW
    nb = seq // w
    ns = SEQS_PER_STEP
    wo = SWA_H * SWA_D
    cur = lambda b, n: (b, n, 0)
    prev = lambda b, n: (b, jnp.maximum(n - 1, 0), 0)
    r3 = lambda a: a.reshape(batch, seq, a.shape[-1])
    q3, k3, v3 = r3(sq), r3(sk), r3(sv)
    out = pl.pallas_call(
        _swa_kernel,
        grid=(batch // ns, nb),
        in_specs=[pl.BlockSpec((ns, w, sq.shape[1]), cur),
                  pl.BlockSpec((ns, w, sk.shape[1]), cur), pl.BlockSpec((ns, w, sk.shape[1]), prev),
                  pl.BlockSpec((ns, w, sv.shape[1]), cur), pl.BlockSpec((ns, w, sv.shape[1]), prev),
                  pl.BlockSpec((1, LANES), lambda b, n: (0, 0))],
        out_specs=pl.BlockSpec((ns, w, wo), cur),
        out_shape=jax.ShapeDtypeStruct((batch, seq, wo), BF16),
        compiler_params=_cparams(("arbitrary", "arbitrary")),
        name="swa",
    )(q3, k3, k3, v3, v3, sinks_row)
    return out.reshape(batch * seq, wo)


def _layer_norm(h, g, b):
    mu = jnp.mean(h, axis=-1, keepdims=True)
    d = h - mu
    var = jnp.mean(d * d, axis=-1, keepdims=True)
    return d * lax.rsqrt(var + LN_EPS) * g + b


def _outproj_kernel(x_ref, a1_ref, a2_ref, w_ref, g_ref, b_ref, o_ref, op_ref):
    k1 = a1_ref.shape[1]
    y = _dot(a1_ref[...].astype(BF16), w_ref[0:k1, :]) + _dot(a2_ref[...].astype(BF16), w_ref[k1:, :])
    h = _layer_norm(DN_ALPHA * x_ref[...] + y, g_ref[...], b_ref[...])
    o_ref[...] = h
    op_ref[...] = _pack_pairs(h)


def _outproj_ln(x, a1, a2, w, g, b, tm=512):
    t, d = x.shape
    row = lambda i: (i, 0)
    fix = lambda i: (0, 0)
    return pl.pallas_call(
        _outproj_kernel,
        grid=(t // tm,),
        in_specs=[pl.BlockSpec((tm, d), row), pl.BlockSpec((tm, a1.shape[1]), row), pl.BlockSpec((tm, a2.shape[1]), row),
                  pl.BlockSpec(w.shape, fix), pl.BlockSpec((1, d), fix), pl.BlockSpec((1, d), fix)],
        out_specs=[pl.BlockSpec((tm, d), row), pl.BlockSpec((tm, d // 2), row)],
        out_shape=[jax.ShapeDtypeStruct((t, d), F32), jax.ShapeDtypeStruct((t, d // 2), jnp.uint32)],
        compiler_params=_cparams(("arbitrary",)),
        name="outproj_ln",
    )(x, a1, a2, w, g, b)


def _first_index(x, m, iota_f, sentinel):
    return jnp.min(jnp.where(x == m, iota_f, sentinel), axis=0, keepdims=True)


def _router_kernel(x_ref, wt_ref, bias_ref, idx_ref, gate_ref, rank_ref, cnt_ref, carry_ref):
    tm = x_ref.shape[0]
    e = N_EXPERTS
    gs = e // N_GROUPS
    ninf = -jnp.inf

    @pl.when(pl.program_id(0) == 0)
    def _():
        carry_ref[...] = jnp.zeros(carry_ref.shape, F32)

    logits = _dot3(wt_ref[...], x_ref[...], _dot_nt)
    scores = _sigmoid(logits)
    sel = scores + bias_ref[:, 0:1]

    sub_f = _iota2((gs, tm), 0).astype(F32)
    gscore = []
    for g in range(N_GROUPS):
        blk = sel[g * gs:(g + 1) * gs, :]
        m1 = jnp.max(blk, axis=0, keepdims=True)
        i1 = _first_index(blk, m1, sub_f, float(gs))
        m2 = jnp.max(jnp.where(sub_f == i1, ninf, blk), axis=0, keepdims=True)
        gscore.append(m1 + m2)
    gsc = jnp.concatenate(gscore, axis=0)
    grp_f = _iota2((N_GROUPS, tm), 0).astype(F32)
    gmask = jnp.zeros((N_GROUPS, tm), F32)
    for _ in range(TOPK_GROUPS):
        m = jnp.max(gsc, axis=0, keepdims=True)
        gi = _first_index(gsc, m, grp_f, float(N_GROUPS))
        hit = grp_f == gi
        gmask = jnp.where(hit, 1.0, gmask)
        gsc = jnp.where(hit, ninf, gsc)
    masked = jnp.concatenate(
        [jnp.where(gmask[g:g + 1, :] > 0.0, sel[g * gs:(g + 1) * gs, :], ninf) for g in range(N_GROUPS)], axis=0)

    exp_f = _iota2((e, tm), 0).astype(F32)
    chosen = jnp.zeros((e, tm), F32)
    idxs, gates = [], []
    for _ in range(TOP_K):
        m = jnp.max(masked, axis=0, keepdims=True)
        ei = _first_index(masked, m, exp_f, float(e))
        hit = exp_f == ei
        idxs.append(ei)
        gates.append(jnp.sum(jnp.where(hit, scores, 0.0), axis=0, keepdims=True))
        chosen = jnp.where(hit, 1.0, chosen)
        masked = jnp.where(hit, ninf, masked)
    gate = jnp.concatenate(gates, axis=0)
    gate = gate / jnp.sum(gate, axis=0, keepdims=True) * ROUTED_SCALE
    idx_f = jnp.concatenate(idxs, axis=0)

    upper = (_iota2((tm, tm), 0) < _iota2((tm, tm), 1)).astype(BF16)
    before = _dot(chosen.astype(BF16), upper) + carry_ref[...][:, 0:1]
    ranks = [jnp.sum(jnp.where(exp_f == idxs[k], before, 0.0), axis=0, keepdims=True) for k in range(TOP_K)]
    carry_ref[...] = carry_ref[...] + jnp.sum(chosen, axis=1, keepdims=True)

    idx_ref[...] = idx_f.astype(jnp.int32)
    gate_ref[...] = gate
    rank_ref[...] = jnp.concatenate(ranks, axis=0).astype(jnp.int32)
    cnt_ref[...] = carry_ref[...]


def _router(x, wt, bias_col, tm=512):
    t, d = x.shape
    col = lambda i: (0, i)
    fix = lambda i: (0, 0)
    return pl.pallas_call(
        _router_kernel,
        grid=(t // tm,),
        in_specs=[pl.BlockSpec((tm, d), lambda i: (i, 0)), pl.BlockSpec(wt.shape, fix), pl.BlockSpec((N_EXPERTS, LANES), fix)],
        out_specs=[pl.BlockSpec((TOP_K, tm), col), pl.BlockSpec((TOP_K, tm), col), pl.BlockSpec((TOP_K, tm), col),
                   pl.BlockSpec((N_EXPERTS, LANES), fix)],
        out_shape=[jax.ShapeDtypeStruct((TOP_K, t), jnp.int32), jax.ShapeDtypeStruct((TOP_K, t), F32),
                   jax.ShapeDtypeStruct((TOP_K, t), jnp.int32), jax.ShapeDtypeStruct((N_EXPERTS, LANES), F32)],
        scratch_shapes=[pltpu.VMEM((N_EXPERTS, LANES), F32)],
        compiler_params=_cparams(("arbitrary",)),
        name="router",
    )(x, wt, bias_col)


def _dest_kernel(idx_ref, rank_ref, start_ref, dest_ref):
    tm = idx_ref.shape[1]
    exp_i = _iota2((N_EXPERTS, tm), 0)
    start = start_ref[:, 0:1]
    rows = [jnp.sum(jnp.where(exp_i == idx_ref[s:s + 1, :], start, 0.0), axis=0, keepdims=True) for s in range(TOP_K)]
    dest_ref[...] = jnp.concatenate(rows, axis=0).astype(jnp.int32) + rank_ref[...]


def _dest_rows(idx, rank, start_col, tm=2048):
    t = idx.shape[1]
    tm = min(tm, t)
    col = lambda i: (0, i)
    return pl.pallas_call(
        _dest_kernel,
        grid=(t // tm,),
        in_specs=[pl.BlockSpec((TOP_K, tm), col), pl.BlockSpec((TOP_K, tm), col),
                  pl.BlockSpec((N_EXPERTS, LANES), lambda i: (0, 0))],
        out_specs=pl.BlockSpec((TOP_K, tm), col),
        out_shape=jax.ShapeDtypeStruct((TOP_K, t), jnp.int32),
        compiler_params=_cparams(("arbitrary",)),
        name="moe_dest",
    )(idx, rank, start_col)


def _pack_pairs(x):
    n = x.shape[1] // 2
    hi = lax.bitcast_convert_type(x[:, :n].astype(BF16).astype(F32), jnp.uint32)
    lo = lax.bitcast_convert_type(x[:, n:].astype(BF16).astype(F32), jnp.uint32)
    return hi | (lo >> 16)


def _unpack_pairs(w):
    hi = lax.bitcast_convert_type(w & jnp.uint32(0xFFFF0000), F32)
    lo = lax.bitcast_convert_type(w << 16, F32)
    return hi, lo


def _sc_scatter_rows(xp, dest, rows, chunk=LANES):
    t, width = xp.shape
    info = plsc.get_sparse_core_info()
    ncores, nsub = info.num_cores, info.num_subcores
    per_worker = t // (ncores * nsub)
    nchunk = per_worker // chunk
    mesh = plsc.VectorSubcoreMesh(core_axis_name="c", subcore_axis_name="s")

    @functools.partial(
        pl.kernel, mesh=mesh,
        out_type=jax.ShapeDtypeStruct((rows, width), xp.dtype),
        scratch_types=[pltpu.VMEM((TOP_K, chunk), jnp.int32), pltpu.VMEM((chunk, width), xp.dtype), pltpu.SemaphoreType.DMA],
    )
    def scatter(xp_hbm, dest_hbm, out_hbm, idx_v, rows_v, sem):
        base = (lax.axis_index("s") * ncores + lax.axis_index("c")) * per_worker

        @pl.loop(0, nchunk)
        def _(i):
            off = pl.multiple_of(base + i * chunk, chunk)
            pltpu.sync_copy(dest_hbm.at[:, pl.ds(off, chunk)], idx_v)
            pltpu.sync_copy(xp_hbm.at[pl.ds(off, chunk)], rows_v)
            copies = [pltpu.async_copy(rows_v, out_hbm.at[idx_v.at[s]], sem) for s in range(TOP_K)]
            for cp in copies:
                cp.wait()

    return scatter(xp, dest)


def _experts_kernel(be_ref, nu_ref, nv_ref, xs_ref, wg_ref, wu_ref, wd_ref, ys_ref, wgb_ref, wub_ref, wdb_ref):
    i = pl.program_id(0)

    @pl.when(jnp.logical_or(i == 0, be_ref[i] != be_ref[jnp.maximum(i - 1, 0)]))
    def _():
        wgb_ref[...] = wg_ref[0, 0].astype(BF16)
        wub_ref[...] = wu_ref[0, 0].astype(BF16)
        wdb_ref[...] = wd_ref[0, 0].astype(BF16)

    @pl.when(i < nu_ref[0])
    def _():
        half = xs_ref.shape[1]
        sub = xs_ref.shape[0] // EXPERT_SUBBLOCKS
        acts = []
        for r in range(EXPERT_SUBBLOCKS):
            rows = pl.ds(r * sub, sub)
            live = (_iota2((sub, 1), 0) + r * sub) < nv_ref[i]
            xa, xb = _unpack_pairs(jnp.where(live, xs_ref[rows, :], jnp.uint32(0)))
            xa = xa.astype(BF16)
            xb = xb.astype(BF16)
            gate = _dot(xa, wgb_ref[:half, :]) + _dot(xb, wgb_ref[half:, :])
            up = _dot(xa, wub_ref[:half, :]) + _dot(xb, wub_ref[half:, :])
            acts.append((gate, up))
        outs = [_dot((_silu(gate) * up).astype(BF16), wdb_ref[...]) for gate, up in acts]
        for r, y in enumerate(outs):
            ys_ref[pl.ds(r * sub, sub), :] = _pack_pairs(y)


def _experts(block_e, n_used, n_valid, xs, wg, wu, wd, layer):
    rows, half = xs.shape
    d = 2 * half
    nb = rows // EXPERT_BLOCK
    blk = lambda i, be, nu, nv: (jnp.minimum(i, nu[0] - 1), 0)
    wsel = lambda i, be, nu, nv: (layer, be[i], 0, 0)
    return pl.pallas_call(
        _experts_kernel,
        grid_spec=pltpu.PrefetchScalarGridSpec(
            num_scalar_prefetch=3,
            grid=(nb,),
            in_specs=[pl.BlockSpec((EXPERT_BLOCK, half), blk),
                      pl.BlockSpec((1, 1, d, D_EXPERT), wsel), pl.BlockSpec((1, 1, d, D_EXPERT), wsel),
                      pl.BlockSpec((1, 1, D_EXPERT, d), wsel)],
            out_specs=pl.BlockSpec((EXPERT_BLOCK, half), blk),
            scratch_shapes=[pltpu.VMEM((d, D_EXPERT), BF16), pltpu.VMEM((d, D_EXPERT), BF16),
                            pltpu.VMEM((D_EXPERT, d), BF16)],
        ),
        out_shape=jax.ShapeDtypeStruct((rows, half), jnp.uint32),
        compiler_params=_cparams(("arbitrary",)),
        name="moe_experts",
    )(block_e, n_used, n_valid, xs, wg, wu, wd)


def _sc_gather_rows(table, idx, chunk=SC_CHUNK):
    n = idx.shape[0]
    width = table.shape[1]
    info = plsc.get_sparse_core_info()
    ncores, nsub = info.num_cores, info.num_subcores
    per_worker = n // (ncores * nsub)
    nchunk = per_worker // chunk
    mesh = plsc.VectorSubcoreMesh(core_axis_name="c", subcore_axis_name="s")

    @functools.partial(
        pl.kernel, mesh=mesh,
        out_type=jax.ShapeDtypeStruct((n, width), table.dtype),
        scratch_types=[pltpu.VMEM((nchunk, chunk), jnp.int32), pltpu.VMEM((2, chunk, width), table.dtype),
                       pltpu.SemaphoreType.DMA((2,)), pltpu.SemaphoreType.DMA((2,))],
    )
    def gather(table_hbm, idx_hbm, out_hbm, idx_v, rows_v, gsem, wsem):
        wid = lax.axis_index("s") * ncores + lax.axis_index("c")
        base = wid * per_worker
        pltpu.sync_copy(idx_hbm.at[pl.ds(wid * nchunk, nchunk)], idx_v)

        def fetch(j, b):
            return pltpu.make_async_copy(table_hbm.at[idx_v.at[j]], rows_v.at[b], gsem.at[b])

        def flush(j, b):
            off = pl.multiple_of(base + j * chunk, chunk)
            return pltpu.make_async_copy(rows_v.at[b], out_hbm.at[pl.ds(off, chunk)], wsem.at[b])

        fetch(0, 0).start()

        @pl.loop(0, nchunk, step=2)
        def _(i):
            for b in range(2):
                j = i + b
                fetch(j, b).wait()

                @pl.when(j + 1 < nchunk)
                def _():
                    @pl.when(j >= 1)
                    def _():
                        flush(j - 1, 1 - b).wait()

                    fetch(j + 1, 1 - b).start()

                flush(j, b).start()

        flush(nchunk - 2, 0).wait()
        flush(nchunk - 1, 1).wait()

    return gather(table, idx.reshape(n // chunk, chunk))


def _combine_kernel(x_ref, gate_ref, rows_ref, sg_ref, su_ref, sd_ref, g_ref, b_ref, o_ref):
    x = x_ref[...]
    xb = x.astype(BF16)
    hs = _silu(_dot(xb, sg_ref[...])) * _dot(xb, su_ref[...])
    ff = _dot(hs.astype(BF16), sd_ref[...])
    gate = gate_ref[...]
    half = rows_ref.shape[2]
    ya = ff[:, :half]
    yb = ff[:, half:]
    for s in range(TOP_K):
        a, b = _unpack_pairs(rows_ref[s])
        ya = ya + gate[:, s:s + 1] * a
        yb = yb + gate[:, s:s + 1] * b
    ff = jnp.concatenate([ya, yb], axis=1)
    o_ref[...] = _layer_norm(DN_ALPHA * x + ff, g_ref[...], b_ref[...])


def _combine_alias_kernel(prev_ref, *refs):
    del prev_ref
    _combine_kernel(*refs)


def _combine(x, gate_t, rows, sg, su, sd, g, b, part, nparts, prev=None, tm=512):
    t, d = x.shape
    tp = t // nparts
    tm = min(tm, tp)
    first = part * (tp // tm)
    row = lambda i: (first + i, 0)
    fix = lambda i: (0, 0)
    in_specs = [pl.BlockSpec((tm, d), row), pl.BlockSpec((tm, TOP_K), row),
                pl.BlockSpec((TOP_K, tm, d // 2), lambda i: (0, i, 0)),
                pl.BlockSpec(sg.shape, fix), pl.BlockSpec(su.shape, fix), pl.BlockSpec(sd.shape, fix),
                pl.BlockSpec((1, d), fix), pl.BlockSpec((1, d), fix)]
    args = (x, gate_t, rows, sg, su, sd, g, b)
    if prev is None:
        body, aliases = _combine_kernel, {}
    else:
        body, aliases = _combine_alias_kernel, {0: 0}
        in_specs = [pl.BlockSpec(memory_space=pl.ANY)] + in_specs
        args = (prev,) + args
    return pl.pallas_call(
        body,
        grid=(tp // tm,),
        in_specs=in_specs,
        out_specs=pl.BlockSpec((tm, d), row),
        out_shape=jax.ShapeDtypeStruct((t, d), F32),
        input_output_aliases=aliases,
        compiler_params=_cparams(("arbitrary",)),
        name="moe_combine",
    )(*args)


def _take_cols(w, idx):
    idx = np.asarray(idx)
    pieces, start = [], 0
    for pos in range(1, len(idx) + 1):
        run_ends = pos == len(idx) or (idx[pos] != idx[pos - 1] + 1 if idx[pos - 1] >= 0 else idx[pos] >= 0)
        if run_ends:
            if idx[start] < 0:
                pieces.append(jnp.zeros((w.shape[0], pos - start), w.dtype))
            else:
                pieces.append(w[:, int(idx[start]):int(idx[start]) + pos - start])
            start = pos
    return jnp.concatenate(pieces, axis=1)


def _pad_lane_row(v, first_lane, width=LANES):
    out = jnp.zeros((1, width), F32)
    return lax.dynamic_update_slice(out, v.reshape(1, -1).astype(F32), (0, first_lane))


def _even_in_cols():
    z = lambda n: -np.ones(n, int)
    kr0 = Q_LORA + KV_LORA
    half = MLA_ROPE // 2
    cols = [np.arange(0, Q_LORA), np.arange(Q_LORA, Q_LORA + KV_LORA),
            z(64), np.arange(kr0, kr0 + MLA_ROPE), z(32),
            z(64), np.arange(kr0 + half, kr0 + MLA_ROPE), np.arange(kr0, kr0 + half), z(32)]
    g0 = kr0 + MLA_ROPE
    nqk = GDN_H * GDN_DK
    cols.append(np.arange(g0, g0 + 3 * nqk))
    zoff = g0 + 3 * nqk + 2 * GDN_H
    cols.append(np.arange(zoff, zoff + GDN_H * GDN_DV))
    cols += [np.arange(g0 + 3 * nqk, g0 + 3 * nqk + 2 * GDN_H), z(LANES - 2 * GDN_H)]
    return np.concatenate(cols)


EV_WIDTHS = (Q_LORA + KV_LORA + 2 * LANES, 3 * GDN_H * GDN_DK, GDN_H * GDN_DV, LANES)


def _mla_q_cols():
    per = MLA_NOPE + MLA_ROPE
    half = MLA_ROPE // 2
    main, sw = [], []
    for h in range(MLA_H):
        b = h * per
        main += [np.arange(b, b + per), -np.ones(LANES - per, int)]
        sw += [-np.ones(MLA_NOPE, int), np.arange(b + MLA_NOPE + half, b + per), np.arange(b + MLA_NOPE, b + MLA_NOPE + half),
               -np.ones(LANES - per, int)]
    return np.concatenate(main + sw)


def _mla_kv_cols():
    per = MLA_NOPE + MLA_V
    kc, vc = [], []
    for h in range(MLA_H):
        b = h * per
        kc += [np.arange(b, b + MLA_NOPE), -np.ones(LANES - MLA_NOPE, int)]
        vv = np.arange(b + MLA_NOPE, b + per)
        pad = -np.ones(LANES - MLA_V, int)
        vc += [vv, pad] if h % 2 == 0 else [pad, vv]
    return np.concatenate(kc + vc)


def _odd_in_cols():
    z = lambda n: -np.ones(n, int)
    o = 0
    cols = []
    mq0, mk0 = 0, ML_H * ML_DK
    for base in (mq0, mk0):
        for h in range(ML_H):
            cols += [np.arange(base + h * ML_DK, base + (h + 1) * ML_DK), z(LANES - ML_DK)]
    mv0 = 2 * ML_H * ML_DK
    cols.append(np.arange(mv0, mv0 + ML_H * ML_DV))
    mi0 = mv0 + ML_H * ML_DV
    mo0 = mi0 + 2 * ML_H
    cols.append(np.arange(mo0, mo0 + ML_H * ML_DV))
    cols += [np.arange(mi0, mi0 + 2 * ML_H), z(LANES - 2 * ML_H)]
    sq0 = mo0 + ML_H * ML_DV
    sk0 = sq0 + SWA_H * SWA_D
    sv0 = sk0 + SWA_KV * SWA_D
    half = SWA_D // 2

    def heads(base, n, swapped, copies):
        out = []
        for h in range(n):
            b = base + h * SWA_D
            one = [np.arange(b + half, b + SWA_D), np.arange(b, b + half)] if swapped else [np.arange(b, b + SWA_D)]
            out += one * copies
        return out

    cols += (heads(sq0, SWA_H, False, 1) + heads(sq0, SWA_H, True, 1)
             + heads(sk0, SWA_KV, False, 2) + heads(sk0, SWA_KV, True, 2))
    for g in range(SWA_KV):
        vv = np.arange(sv0 + g * SWA_D, sv0 + (g + 1) * SWA_D)
        cols += [vv, z(LANES - SWA_D), z(LANES - SWA_D), vv]
    return np.concatenate(cols)


def _even_mixer(x, tabs, w_in, q_norm, w_qb, kv_norm, w_kvb, conv_w, a_log, dt_bias, o_norm, batch, seq):
    ctab, stab = tabs
    w = _take_cols(w_in, _even_in_cols()).astype(BF16)
    mla_in, act, z, gates = _proj_even(x, w, conv_w, seq)
    wq2 = _take_cols(w_qb, _mla_q_cols()).astype(BF16)
    wkv2 = _take_cols(w_kvb, _mla_kv_cols()).astype(BF16)
    q, k, v = _mla_prep(mla_in, ctab, stab, q_norm.reshape(1, -1), kv_norm.reshape(1, -1), wq2, wkv2)
    o_a = _mla_attn(q, k, v, batch, seq)
    o_b = _gdn(act, gates, z, _pad_lane_row(a_log, GDN_H), _pad_lane_row(dt_bias, GDN_H),
               o_norm.reshape(1, -1), batch, seq)
    return o_a, o_b


def _odd_mixer(x, tabs, w_in, b_i, b_f, ml_norm, sinks, batch, seq):
    ctab, stab = tabs
    w = _take_cols(w_in, _odd_in_cols()).astype(BF16)
    mq, mk, mv, mo, mg, sq, sk, sv = _proj_odd(x, w, ctab, stab)
    bias_row = _pad_lane_row(jnp.concatenate([b_i, b_f]), 0)
    o_c = _mlstm(mq, mk, mv, mo, mg, bias_row, ml_norm.reshape(1, -1), batch, seq)
    o_d = _swa(sq, sk, sv, _pad_lane_row(sinks, 0), batch, seq)
    return o_c, o_d


def _moe(x, xp, router_w, router_b, w_gate, w_up, w_down, layer, s_gate, s_up, s_down, ln_g, ln_b):
    t, d = x.shape
    bias_col = jnp.broadcast_to(router_b.reshape(-1, 1).astype(F32), (N_EXPERTS, LANES))
    idx, gate, rank, cnt = _router(x, router_w.T, bias_col)
    counts = cnt[:, 0].astype(jnp.int32)
    padded = (counts + EXPERT_BLOCK - 1) // EXPERT_BLOCK * EXPERT_BLOCK
    pad_end = jnp.cumsum(padded)
    pad_start = pad_end - padded
    start_col = jnp.broadcast_to(pad_start.astype(F32).reshape(-1, 1), (N_EXPERTS, LANES))
    dest = _dest_rows(idx, rank, start_col)
    n_blocks = t * TOP_K // EXPERT_BLOCK + N_EXPERTS
    rows = n_blocks * EXPERT_BLOCK
    block_row = jnp.arange(n_blocks, dtype=jnp.int32) * EXPERT_BLOCK
    block_e = jnp.minimum(jnp.sum((pad_end[None, :] <= block_row[:, None]).astype(jnp.int32), axis=1), N_EXPERTS - 1)
    n_used = (pad_end[-1:] // EXPERT_BLOCK).astype(jnp.int32)
    live_end = jnp.sum(jnp.where(block_e[:, None] == jnp.arange(N_EXPERTS, dtype=jnp.int32)[None, :],
                                 (pad_start + counts)[None, :], 0), axis=1)
    n_valid = jnp.clip(live_end - block_row, 0, EXPERT_BLOCK).astype(jnp.int32)
    xs = _sc_scatter_rows(xp, dest, rows)
    ys = _experts(block_e, n_used, n_valid, xs, w_gate, w_up, w_down, layer)
    nparts = COMBINE_PARTS if t % (COMBINE_PARTS * 512) == 0 else 1
    tp = t // nparts
    gate_t = gate.T
    sgb, sub, sdb = s_gate.astype(BF16), s_up.astype(BF16), s_down.astype(BF16)
    out = None
    for part in range(nparts):
        idx_p = dest[:, part * tp:(part + 1) * tp].reshape(-1)
        picked = _sc_gather_rows(ys, idx_p).reshape(TOP_K, tp, d // 2)
        out = _combine(x, gate_t, picked, sgb, sub, sdb, ln_g.reshape(1, -1), ln_b.reshape(1, -1), part, nparts, prev=out)
    return out


def kernel(x, positions, ev_w_in, mla_q_norm, mla_w_qb, mla_kv_norm, mla_w_kvb, gdn_conv, gdn_a_log, gdn_dt_bias, gdn_norm, ev_w_out, od_w_in, mlstm_b_i, mlstm_b_f, mlstm_norm, swa_sinks, od_w_out, ln1_g, ln1_b, router_w, router_b, moe_w_gate, moe_w_up, moe_w_down, shared_w_gate, shared_w_up, shared_w_down, ln2_g, ln2_b):
    batch, seq, d = x.shape
    t = batch * seq
    pos = positions.reshape(t, 1).astype(F32)
    tabs_m = _rope_tables(pos, _rope_rows(MLA_ROPE, MLA_NOPE, MLA_NOPE))
    tabs_s = _rope_tables(pos, _rope_rows(SWA_D, 0, 0, heads=LANES // SWA_D))
    h = x.reshape(t, d)
    for layer in range(DEPTH):
        j = layer // 2
        if layer % 2 == 0:
            a1, a2 = _even_mixer(h, tabs_m, ev_w_in[j], mla_q_norm[j], mla_w_qb[j], mla_kv_norm[j], mla_w_kvb[j],
                                 gdn_conv[j], gdn_a_log[j], gdn_dt_bias[j], gdn_norm[j], batch, seq)
            w_out = ev_w_out[j]
        else:
            a1, a2 = _odd_mixer(h, tabs_s, od_w_in[j], mlstm_b_i[j], mlstm_b_f[j], mlstm_norm[j], swa_sinks[j], batch, seq)
            w_out = od_w_out[j]
        h, hp = _outproj_ln(h, a1, a2, w_out.astype(BF16), ln1_g[layer].reshape(1, -1), ln1_b[layer].reshape(1, -1))
        h = _moe(h, hp, router_w[layer], router_b[layer], moe_w_gate, moe_w_up, moe_w_down, layer,
                 shared_w_gate[layer], shared_w_up[layer], shared_w_down[layer], ln2_g[layer], ln2_b[layer])
    return h.reshape(batch, seq, d)
```

```python
import functools
import math

import numpy as np
import jax
import jax.numpy as jnp
from jax import lax
from jax.experimental import pallas as pl
from jax.experimental.pallas import tpu as pltpu
from jax.experimental.pallas import tpu_sc as plsc

F32 = jnp.float32
BF16 = jnp.bfloat16
HI = lax.Precision.HIGHEST

D_MODEL = 1024
DEPTH = 4
ROPE_THETA = 10000.0
EPS = 1e-6
LN_EPS = 1e-5
MLA_H, MLA_NOPE, MLA_ROPE, MLA_V = 8, 64, 32, 64
Q_LORA, KV_LORA = 256, 128
GDN_H, GDN_DK, GDN_DV, CONV_W, GDN_CHUNK = 4, 128, 128, 4, 64
ML_H, ML_DK, ML_DV, ML_CHUNK = 4, 64, 128, 64
SWA_H, SWA_KV, SWA_D, WINDOW = 8, 2, 64, 128
N_EXPERTS, N_GROUPS, TOPK_GROUPS, TOP_K = 64, 8, 4, 8
D_EXPERT, D_SHARED = 256, 256
ROUTED_SCALE = 2.5
DN_ALPHA = (2 * DEPTH) ** 0.25

LANES = 128
V7X_VMEM_BYTES = 64 * 1024 * 1024
VMEM_LIMIT = 48 * 1024 * 1024

EXPERT_BLOCK = 512
EXPERT_SUBBLOCKS = 2
COMBINE_PARTS = 1
SWA_SEQS_PER_STEP = 4
MLSTM_SEQS_PER_STEP = 2
GDN_SEQS_PER_STEP = 8
SC_CHUNK = 64


def _cparams(sem, vmem=VMEM_LIMIT):
    return pltpu.CompilerParams(dimension_semantics=sem, vmem_limit_bytes=vmem)


def _dot(a, b, precision=None):
    return jnp.dot(a, b, preferred_element_type=F32, precision=precision)


def _dot_nt(a, b, precision=None):
    return lax.dot_general(a, b, (((1,), (1,)), ((), ())), preferred_element_type=F32, precision=precision)


def _dot_tn(a, b, precision=None):
    return lax.dot_general(a, b, (((0,), (0,)), ((), ())), preferred_element_type=F32, precision=precision)


def _split2(a):
    hi = a.astype(BF16)
    lo = (a - hi.astype(F32)).astype(BF16)
    return hi, lo


def _split3(a):
    p1 = a.astype(BF16)
    r = a - p1.astype(F32)
    p2 = r.astype(BF16)
    p3 = (r - p2.astype(F32)).astype(BF16)
    return p1, p2, p3


def _dot3(a, b, dot=_dot):
    ah, al = _split2(a)
    bh, bl = _split2(b)
    return dot(ah, bh) + (dot(ah, bl) + dot(al, bh))


def _dot_sel(sel, b, dot=_dot):
    sel = sel.astype(BF16)
    p1, p2, p3 = _split3(b)
    return dot(sel, p1) + (dot(sel, p2) + dot(sel, p3))


def _sigmoid(x):
    return 1.0 / (1.0 + jnp.exp(-x))


def _softplus(x):
    return jnp.maximum(x, 0.0) + jnp.log(1.0 + jnp.exp(-jnp.abs(x)))


def _silu(x):
    return x * _sigmoid(x)


def _lane_bcast(x, c):
    return jnp.broadcast_to(x[:, c:c + 1], x.shape)


def _iota2(shape, dim):
    return lax.broadcasted_iota(jnp.int32, shape, dim)


def _rope_kernel(pos_ref, rows_ref, c_ref, s_ref):
    ang = pos_ref[...] * rows_ref[0:1, :]
    c_ref[...] = rows_ref[1:2, :] * jnp.cos(ang) + rows_ref[2:3, :]
    s_ref[...] = rows_ref[3:4, :] * jnp.sin(ang)


def _rope_tables(pos, rows, tm=512):
    t = pos.shape[0]
    return pl.pallas_call(
        _rope_kernel,
        grid=(t // tm,),
        in_specs=[pl.BlockSpec((tm, 1), lambda i: (i, 0)), pl.BlockSpec((8, LANES), lambda i: (0, 0))],
        out_specs=[pl.BlockSpec((tm, LANES), lambda i: (i, 0))] * 2,
        out_shape=[jax.ShapeDtypeStruct((t, LANES), F32)] * 2,
        compiler_params=_cparams(("arbitrary",)),
        name="rope_tables",
    )(pos, rows)


def _rope_rows(dim, first_lane, pad_one_lanes, heads=1):
    half = dim // 2
    inv = ROPE_THETA ** (-(np.arange(0, dim, 2, dtype=np.float32) / dim))
    rows = np.zeros((8, LANES), np.float32)
    for h in range(heads):
        lo = slice(first_lane + h * dim, first_lane + h * dim + half)
        hi = slice(first_lane + h * dim + half, first_lane + (h + 1) * dim)
        rows[0, lo] = inv
        rows[0, hi] = inv
        rows[1, lo] = 1.0
        rows[1, hi] = 1.0
        rows[3, lo] = -1.0
        rows[3, hi] = 1.0
    rows[2, :pad_one_lanes] = 1.0
    return jnp.asarray(rows)


def _proj_kernel(x_ref, w_ref, *out_refs, offsets):
    xb = x_ref[...].astype(BF16)
    for o_ref, (a, b) in zip(out_refs, offsets):
        o_ref[...] = _dot(xb, w_ref[:, a:b]).astype(o_ref.dtype)


def _proj(x, w, widths, dtypes, tm=512):
    t, k = x.shape
    offs = np.concatenate([[0], np.cumsum(widths)]).tolist()
    offsets = tuple((offs[i], offs[i + 1]) for i in range(len(widths)))
    return pl.pallas_call(
        functools.partial(_proj_kernel, offsets=offsets),
        grid=(t // tm,),
        in_specs=[pl.BlockSpec((tm, k), lambda i: (i, 0)), pl.BlockSpec(w.shape, lambda i: (0, 0))],
        out_specs=[pl.BlockSpec((tm, n), lambda i: (i, 0)) for n in widths],
        out_shape=[jax.ShapeDtypeStruct((t, n), dt) for n, dt in zip(widths, dtypes)],
        compiler_params=_cparams(("arbitrary",)),
        name="in_proj",
    )(x, w)


def _proj_even_kernel(x_ref, w_ref, cw_ref, mla_ref, act_ref, z_ref, g_ref, ext_ref, *, tiles_per_seq):
    tm = x_ref.shape[0]
    o = np.concatenate([[0], np.cumsum(EV_WIDTHS)]).tolist()
    @pl.when(pl.program_id(0) % tiles_per_seq == 0)
    def _():
        ext_ref[0:8, :] = jnp.zeros((8, ext_ref.shape[1]), F32)

    xb = x_ref[...].astype(BF16)
    nchunk = 3
    cw = EV_WIDTHS[1] // nchunk

    def project(ci):
        ext_ref[8:8 + tm, ci * cw:(ci + 1) * cw] = _dot(xb, w_ref[:, o[1] + ci * cw:o[1] + (ci + 1) * cw])

    project(0)
    for ci in range(nchunk):
        if ci + 1 < nchunk:
            project(ci + 1)
        else:
            mla_ref[...] = _dot(xb, w_ref[:, o[0]:o[1]])
            z_ref[...] = _dot(xb, w_ref[:, o[2]:o[3]])
            g_ref[...] = _dot(xb, w_ref[:, o[3]:o[4]])
        cols = slice(ci * cw, (ci + 1) * cw)
        conv = cw_ref[0:1, cols] * ext_ref[5:5 + tm, cols]
        for j in range(1, CONV_W):
            conv = conv + cw_ref[j:j + 1, cols] * ext_ref[5 + j:5 + j + tm, cols]
        act_ref[:, cols] = _silu(conv)
    ext_ref[0:8, :] = ext_ref[tm:tm + 8, :]


def _proj_even(x, w, conv_w, seq, tm=512):
    t, k = x.shape
    tm = min(tm, seq)
    row = lambda i: (i, 0)
    fix = lambda i: (0, 0)
    return pl.pallas_call(
        functools.partial(_proj_even_kernel, tiles_per_seq=seq // tm),
        grid=(t // tm,),
        in_specs=[pl.BlockSpec((tm, k), row), pl.BlockSpec(w.shape, fix), pl.BlockSpec(conv_w.shape, fix)],
        out_specs=[pl.BlockSpec((tm, n), row) for n in EV_WIDTHS],
        out_shape=[jax.ShapeDtypeStruct((t, n), F32) for n in EV_WIDTHS],
        scratch_shapes=[pltpu.VMEM((tm + 8, EV_WIDTHS[1]), F32)],
        compiler_params=_cparams(("arbitrary",)),
        name="in_proj",
    )(x, w, conv_w)


OD_SEG = dict(mq=(0, 512), mk=(512, 1024), mv=(1024, 1536), mo=(1536, 2048), gates=(2048, 2176),
              sq=(2176, 2688), sqsw=(2688, 3200), sk=(3200, 3456), sksw=(3456, 3712), sv=(3712, 4224))
OD_COLS = 4224


def _proj_odd_kernel(x_ref, w_ref, c_ref, s_ref, mq_ref, mk_ref, mv_ref, mo_ref, mg_ref, sq_ref, sk_ref, sv_ref):
    xb = x_ref[...].astype(BF16)

    def seg(name):
        a, b = OD_SEG[name]
        return _dot(xb, w_ref[:, a:b])

    mq_ref[...] = seg("mq")
    mk_ref[...] = seg("mk")
    mv_ref[...] = seg("mv")
    mo_ref[...] = seg("mo")
    mg_ref[...] = seg("gates")
    c = c_ref[...]
    s = s_ref[...]
    c8 = jnp.concatenate([c] * (SWA_H // 2), axis=1)
    s8 = jnp.concatenate([s] * (SWA_H // 2), axis=1)
    sq_ref[...] = (seg("sq") * c8 + seg("sqsw") * s8).astype(sq_ref.dtype)
    c2 = jnp.concatenate([c] * SWA_KV, axis=1)
    s2 = jnp.concatenate([s] * SWA_KV, axis=1)
    sk_ref[...] = (seg("sk") * c2 + seg("sksw") * s2).astype(sk_ref.dtype)
    sv_ref[...] = seg("sv").astype(sv_ref.dtype)


def _proj_odd(x, w, ctab, stab, tm=256):
    t, k = x.shape
    widths = (512, 512, 512, 512, 128, SWA_H * SWA_D, SWA_KV * LANES, 2 * SWA_KV * LANES)
    dtypes = (F32, F32, F32, F32, F32, BF16, BF16, BF16)
    return pl.pallas_call(
        _proj_odd_kernel,
        grid=(t // tm,),
        in_specs=[pl.BlockSpec((tm, k), lambda i: (i, 0)), pl.BlockSpec(w.shape, lambda i: (0, 0)),
                  pl.BlockSpec((tm, LANES), lambda i: (i, 0)), pl.BlockSpec((tm, LANES), lambda i: (i, 0))],
        out_specs=[pl.BlockSpec((tm, n), lambda i: (i, 0)) for n in widths],
        out_shape=[jax.ShapeDtypeStruct((t, n), dt) for n, dt in zip(widths, dtypes)],
        compiler_params=_cparams(("arbitrary",)),
        name="in_proj_odd",
    )(x, w, ctab, stab)


def _rms(x, g):
    return x * lax.rsqrt(jnp.mean(x * x, axis=-1, keepdims=True) + EPS) * g


def _mla_prep_kernel(in_ref, c_ref, s_ref, qn_ref, kvn_ref, wq_ref, wkv_ref, q_ref, k_ref, v_ref):
    hw = MLA_H * LANES
    c = c_ref[...]
    s = s_ref[...]
    c8 = jnp.concatenate([c] * MLA_H, axis=1)
    s8 = jnp.concatenate([s] * MLA_H, axis=1)
    cqn = _rms(in_ref[:, 0:Q_LORA], qn_ref[...]).astype(BF16)
    qq = _dot(cqn, wq_ref[...])
    scale = (MLA_NOPE + MLA_ROPE) ** -0.5
    q_ref[...] = ((qq[:, :hw] * c8 + qq[:, hw:] * s8) * scale).astype(q_ref.dtype)
    ckvn = _rms(in_ref[:, Q_LORA:Q_LORA + KV_LORA], kvn_ref[...]).astype(BF16)
    kv = _dot(ckvn, wkv_ref[...])
    o = Q_LORA + KV_LORA
    krr = in_ref[:, o:o + LANES] * c + in_ref[:, o + LANES:o + 2 * LANES] * s
    k_ref[...] = (kv[:, :hw] + jnp.concatenate([krr] * MLA_H, axis=1)).astype(k_ref.dtype)
    v_ref[...] = kv[:, hw:].astype(v_ref.dtype)


def _mla_prep(mla_in, ctab, stab, qn, kvn, wq2, wkv2, tm=512):
    t = mla_in.shape[0]
    hw = MLA_H * LANES
    row = lambda i: (i, 0)
    fix = lambda i: (0, 0)
    return pl.pallas_call(
        _mla_prep_kernel,
        grid=(t // tm,),
        in_specs=[pl.BlockSpec((tm, mla_in.shape[1]), row), pl.BlockSpec((tm, LANES), row), pl.BlockSpec((tm, LANES), row),
                  pl.BlockSpec(qn.shape, fix), pl.BlockSpec(kvn.shape, fix),
                  pl.BlockSpec(wq2.shape, fix), pl.BlockSpec(wkv2.shape, fix)],
        out_specs=[pl.BlockSpec((tm, hw), row)] * 3,
        out_shape=[jax.ShapeDtypeStruct((t, hw), BF16)] * 3,
        compiler_params=_cparams(("arbitrary",)),
        name="mla_prep",
    )(mla_in, ctab, stab, qn, kvn, wq2, wkv2)


def _mla_attn_kernel(q_ref, k_ref, v_ref, o_ref, *, tq):
    i = pl.program_id(2)
    neg = -1e30
    lane = _iota2((tq, LANES), 1)
    ones_lane = (MLA_V, 0)

    def chunk(j, carry, masked):
        start = pl.multiple_of(j * tq, tq)
        out = []
        for hh in range(2):
            m, acc = carry[hh]
            q = q_ref[:, hh * LANES:(hh + 1) * LANES]
            kc = k_ref[pl.ds(start, tq), hh * LANES:(hh + 1) * LANES]
            vc = v_ref[pl.ds(start, tq), hh * LANES:(hh + 1) * LANES]
            vc = jnp.where(lane == ones_lane[hh], jnp.ones_like(vc), vc)
            s = _dot_nt(q, kc)
            if masked:
                s = jnp.where(_iota2(s.shape, 0) >= _iota2(s.shape, 1), s, neg)
            m_new = jnp.maximum(m, jnp.max(s, axis=-1, keepdims=True))
            alpha = jnp.exp(m - m_new)
            p = jnp.exp(s - m_new)
            acc = alpha * acc + _dot(p.astype(BF16), vc)
            out.append((m_new, acc))
        return tuple(out)

    one = (jnp.full((tq, 1), neg, F32), jnp.zeros((tq, LANES), F32))
    carry = lax.fori_loop(0, i, lambda j, c: chunk(j, c, False), (one, one))
    (_, acc0), (_, acc1) = chunk(i, carry, True)
    o0 = acc0 / _lane_bcast(acc0, ones_lane[0])
    o1 = acc1 / _lane_bcast(acc1, ones_lane[1])
    o_ref[...] = jnp.where(lane < MLA_V, o0, o1).astype(o_ref.dtype)


def _mla_attn(q, k, v, batch, seq, tq=512):
    tq = min(tq, seq)
    nq = seq // tq
    pairs = MLA_H // 2
    return pl.pallas_call(
        functools.partial(_mla_attn_kernel, tq=tq),
        grid=(batch, pairs, nq),
        in_specs=[pl.BlockSpec((tq, 2 * LANES), lambda b, p, i: (b * nq + i, p)),
                  pl.BlockSpec((seq, 2 * LANES), lambda b, p, i: (b, p)),
                  pl.BlockSpec((seq, 2 * LANES), lambda b, p, i: (b, p))],
        out_specs=pl.BlockSpec((tq, LANES), lambda b, p, i: (b * nq + i, p)),
        out_shape=jax.ShapeDtypeStruct((batch * seq, pairs * LANES), BF16),
        compiler_params=_cparams(("arbitrary", "arbitrary", "arbitrary")),
        name="mla_attn",
    )(q, k, v)


def _unit_lower_inverse_many(ns):
    c = ns[0].shape[0]
    eye = (_iota2((c, c), 0) == _iota2((c, c), 1)).astype(F32)
    xs = [-n for n in ns]
    ps = [eye + x for x in xs]
    xb = [x.astype(BF16) for x in xs]
    for _ in range(int(math.log2(c)) - 1):
        xs = [_dot(b, b) for b in xb]
        xb = [x.astype(BF16) for x in xs]
        ps = [p + _dot(p.astype(BF16), b) for p, b in zip(ps, xb)]
    return ps


def _gdn_kernel(act_ref, g_ref, z_ref, al_ref, dt_ref, on_ref, o_ref, st_ref):
    c = GDN_CHUNK
    hd = GDN_DK
    nqk = GDN_H * GDN_DK

    @pl.when(pl.program_id(1) == 0)
    def _():
        st_ref[...] = jnp.zeros(st_ref.shape, F32)

    tri = (_iota2((c, c), 0) >= _iota2((c, c), 1)).astype(F32)
    row_ge = _iota2((c, c), 0) >= _iota2((c, c), 1)
    row_gt = _iota2((c, c), 0) > _iota2((c, c), 1)
    lane = _iota2((c, LANES), 1)

    units = []
    for bb in range(act_ref.shape[0]):
        act = act_ref[bb]
        gates = g_ref[bb]
        beta_all = _sigmoid(gates)
        g_all = -jnp.exp(al_ref[...]) * _softplus(gates + dt_ref[...])
        gc_all = _dot_sel(tri, g_all)
        gc_parts = _split3(gc_all)
        for h in range(GDN_H):
            q = act[:, h * hd:(h + 1) * hd]
            k = act[:, nqk + h * hd:nqk + (h + 1) * hd]
            v = act[:, 2 * nqk + h * GDN_DV:2 * nqk + (h + 1) * GDN_DV]
            q = q * lax.rsqrt(jnp.sum(q * q, axis=-1, keepdims=True) + EPS) * (GDN_DK ** -0.5)
            k = k * lax.rsqrt(jnp.sum(k * k, axis=-1, keepdims=True) + EPS)
            beta = _lane_bcast(beta_all, h)
            gcol = _lane_bcast(gc_all, GDN_H + h)
            pick = (lane == GDN_H + h).astype(BF16)
            grow = _dot_nt(pick, gc_parts[0]) + (_dot_nt(pick, gc_parts[1]) + _dot_nt(pick, gc_parts[2]))
            decay = jnp.exp(jnp.where(row_ge, gcol[:, :c] - grow, -jnp.inf))
            kb = k * beta
            lower = jnp.where(row_gt, _dot3(kb, k, _dot_nt) * decay, 0.0)
            eg = jnp.exp(gcol)
            glast = gcol[c - 1:c, :]
            units.append(dict(bb=bb, h=h, lower=lower, rhs=jnp.concatenate([v * beta, kb * eg], axis=1),
                              attn=_dot_nt(q.astype(BF16), k.astype(BF16)) * decay, qg=(q * eg).astype(BF16),
                              kg=(k * jnp.exp(glast - gcol)).astype(BF16), gl=jnp.exp(glast)))

    tinvs = _unit_lower_inverse_many([u["lower"] for u in units])
    uws = []
    for u, tinv in zip(units, tinvs):
        uws.append(_dot(tinv.astype(BF16), u["rhs"].astype(BF16)))
    states = [st_ref[u["bb"], u["h"]] for u in units]
    sbs = [s.astype(BF16) for s in states]
    vnews = [(uw[:, :GDN_DV] - _dot(uw[:, GDN_DV:].astype(BF16), sb)).astype(BF16) for uw, sb in zip(uws, sbs)]
    for u, state, sb, vnb in zip(units, states, sbs, vnews):
        bb, h = u["bb"], u["h"]
        o = _dot(u["qg"], sb) + _dot(u["attn"].astype(BF16), vnb)
        st_ref[bb, h] = state * u["gl"] + _dot_tn(u["kg"], vnb)
        o = _rms(o, on_ref[...]) * _silu(z_ref[bb, :, h * GDN_DV:(h + 1) * GDN_DV])
        o_ref[bb, :, h * GDN_DV:(h + 1) * GDN_DV] = o.astype(o_ref.dtype)


def _gdn(act, gates, z, a_row, dt_row, o_norm, batch, seq):
    c = GDN_CHUNK
    nc = seq // c
    w3 = act.shape[1]
    wo = GDN_H * GDN_DV
    nb = min(GDN_SEQS_PER_STEP, batch)
    row = lambda b, i: (b, i, 0)
    fix = lambda b, i: (0, 0)
    out = pl.pallas_call(
        _gdn_kernel,
        grid=(batch // nb, nc),
        in_specs=[pl.BlockSpec((nb, c, w3), row), pl.BlockSpec((nb, c, LANES), row), pl.BlockSpec((nb, c, wo), row),
                  pl.BlockSpec((1, LANES), fix), pl.BlockSpec((1, LANES), fix), pl.BlockSpec((1, GDN_DV), fix)],
        out_specs=pl.BlockSpec((nb, c, wo), row),
        out_shape=jax.ShapeDtypeStruct((batch, seq, wo), BF16),
        scratch_shapes=[pltpu.VMEM((nb, GDN_H, GDN_DK, GDN_DV), F32)],
        compiler_params=_cparams(("arbitrary", "arbitrary")),
        name="gdn",
    )(act.reshape(batch, seq, w3), gates.reshape(batch, seq, LANES), z.reshape(batch, seq, wo), a_row, dt_row, o_norm)
    return out.reshape(batch * seq, wo)


def _mlstm_kernel(q_ref, k_ref, v_ref, og_ref, g_ref, bias_ref, nrm_ref, o_ref, c_ref, n_ref, m_ref):
    @pl.when(pl.program_id(1) == 0)
    def _():
        c_ref[...] = jnp.zeros(c_ref.shape, F32)
        n_ref[...] = jnp.zeros(n_ref.shape, F32)
        m_ref[...] = jnp.zeros(m_ref.shape, F32)

    c = ML_CHUNK
    tri = (_iota2((c, c), 0) >= _iota2((c, c), 1)).astype(F32)
    row_ge = _iota2((c, c), 0) >= _iota2((c, c), 1)
    ones = jnp.ones((c, LANES), F32)
    lane = _iota2((c, LANES), 1)

    units = []
    for bb in range(q_ref.shape[0]):
        pre = g_ref[bb] + bias_ref[...]
        logf = jnp.minimum(pre, 0.0) - jnp.log(1.0 + jnp.exp(-jnp.abs(pre)))
        bcum_all = _dot_sel(tri, logf)
        for h in range(ML_H):
            q = q_ref[bb, :, h * LANES:(h + 1) * LANES]
            k = k_ref[bb, :, h * LANES:(h + 1) * LANES] * (ML_DK ** -0.5)
            units.append(dict(bb=bb, h=h, q=q, k=k, qb=q.astype(BF16), vb=v_ref[bb, :, h * ML_DV:(h + 1) * ML_DV].astype(BF16),
                              bcol=_lane_bcast(bcum_all, ML_H + h),
                              icol=_lane_bcast(pre, h),
                              col=jnp.where(lane == h, pre, 0.0) - jnp.where(lane == ML_H + h, bcum_all, 0.0),
                              m_st=m_ref[bb, h], cst=c_ref[bb, h], nst=n_ref[bb, h]))
    for u in units:
        u["row"] = _dot_sel(ones, u["col"], _dot_nt)
        u["qk"] = _dot_nt(u["qb"], u["k"].astype(BF16))
        u["qc"] = _dot(u["qb"], u["cst"].astype(BF16))
    for u in units:
        d = jnp.where(row_ge, u["bcol"][:, :c] + u["row"], -jnp.inf)
        inter = u["bcol"] + u["m_st"]
        m_t = jnp.maximum(inter, jnp.max(d, axis=-1, keepdims=True))
        u["m_t"] = m_t
        u["w_inter"] = jnp.exp(inter - m_t)
        u["p"] = jnp.exp(d - m_t[:, :c]) * u["qk"]
        u["pv"] = _dot(u["p"].astype(BF16), u["vb"])
        b_end = u["bcol"][c - 1:c, :]
        a = b_end - u["bcol"] + u["icol"]
        m_new = jnp.maximum(b_end + u["m_st"], jnp.max(a, axis=0, keepdims=True))
        u["m_new"] = m_new
        u["keep"] = jnp.exp(b_end + u["m_st"] - m_new)
        u["ks"] = u["k"] * jnp.exp(a - m_new)
        u["kv"] = _dot_tn(u["ks"].astype(BF16), u["vb"])
    for u in units:
        bb, h = u["bb"], u["h"]
        num = u["w_inter"] * u["qc"] + u["pv"]
        den = (u["w_inter"] * jnp.sum(u["q"] * u["nst"], axis=-1, keepdims=True)
               + jnp.sum(u["p"], axis=-1, keepdims=True))
        hc = num / jnp.maximum(jnp.abs(den), jnp.exp(-u["m_t"]))
        c_ref[bb, h] = u["cst"] * u["keep"] + u["kv"]
        n_ref[bb, h] = u["nst"] * u["keep"] + jnp.sum(u["ks"], axis=0, keepdims=True)
        m_ref[bb, h] = u["m_new"]
        hn = (_rms(hc, nrm_ref[:, h * ML_DV:(h + 1) * ML_DV])
              * _sigmoid(og_ref[bb, :, h * ML_DV:(h + 1) * ML_DV]))
        o_ref[bb, :, h * ML_DV:(h + 1) * ML_DV] = hn.astype(o_ref.dtype)


def _mlstm(mq, mk, mv, mo, gates, bias_row, norm_row, batch, seq):
    c = ML_CHUNK
    nc = seq // c
    nb = min(MLSTM_SEQS_PER_STEP, batch)
    row = lambda b, i: (b, i, 0)
    fix = lambda b, i: (0, 0)
    wide = ML_H * LANES
    r3 = lambda a: a.reshape(batch, seq, a.shape[-1])
    out = pl.pallas_call(
        _mlstm_kernel,
        grid=(batch // nb, nc),
        in_specs=[pl.BlockSpec((nb, c, wide), row), pl.BlockSpec((nb, c, wide), row), pl.BlockSpec((nb, c, wide), row),
                  pl.BlockSpec((nb, c, wide), row), pl.BlockSpec((nb, c, LANES), row),
                  pl.BlockSpec((1, LANES), fix), pl.BlockSpec((1, wide), fix)],
        out_specs=pl.BlockSpec((nb, c, wide), row),
        out_shape=jax.ShapeDtypeStruct((batch, seq, wide), BF16),
        scratch_shapes=[pltpu.VMEM((nb, ML_H, LANES, ML_DV), F32), pltpu.VMEM((nb, ML_H, 1, LANES), F32),
                        pltpu.VMEM((nb, ML_H, 1, LANES), F32)],
        compiler_params=_cparams(("arbitrary", "arbitrary")),
        name="mlstm",
    )(r3(mq), r3(mk), r3(mv), r3(mo), r3(gates), bias_row, norm_row)
    return out.reshape(batch * seq, wide)


def _swa_kernel(q_ref, kc_ref, kp_ref, vc_ref, vp_ref, sink_ref, o_ref):
    w = WINDOW
    n = pl.program_id(1)
    scale = SWA_D ** -0.5
    qi = _iota2((w, w), 0)
    kj = _iota2((w, w), 1)
    mask_c = kj <= qi
    mask_p = jnp.logical_and(kj > qi, n > 0)
    grp = SWA_H // SWA_KV
    neg = -1e30
    units = [(bb, h) for bb in range(q_ref.shape[0]) for h in range(SWA_H)]
    scores = []
    half_of_lane = _iota2((w, LANES), 1) // SWA_D
    for bb, h in units:
        g = h // grp
        pair = q_ref[bb, :, (h // 2) * LANES:(h // 2 + 1) * LANES]
        q = jnp.where(half_of_lane == h % 2, pair, jnp.zeros_like(pair))
        scores.append((_dot_nt(q, kc_ref[bb, :, g * LANES:(g + 1) * LANES]),
                       _dot_nt(q, kp_ref[bb, :, g * LANES:(g + 1) * LANES])))
    masked, tops, exps, dens, probs = [], [], [], [], {}
    for sc, sp in scores:
        masked.append((jnp.where(mask_c, sc * scale, neg), jnp.where(mask_p, sp * scale, neg)))
    for (bb, h), (s_c, s_p) in zip(units, masked):
        tops.append(jnp.maximum(jnp.max(jnp.maximum(s_c, s_p), axis=-1, keepdims=True), sink_ref[:, h:h + 1]))
    for (s_c, s_p), m in zip(masked, tops):
        exps.append((jnp.where(mask_c, jnp.exp(s_c - m), 0.0), jnp.where(mask_p, jnp.exp(s_p - m), 0.0)))
    ones_b = jnp.ones((w, LANES), BF16)
    for (bb, h), (p_c, p_p), m in zip(units, exps, tops):
        p_c, p_p = p_c.astype(BF16), p_p.astype(BF16)
        probs[bb, h] = (p_c, p_p)
        dens.append(_dot(p_c, ones_b) + _dot(p_p, ones_b) + jnp.exp(sink_ref[:, h:h + 1] - m))
    inv = {u: 1.0 / den for u, den in zip(units, dens)}
    for bb in range(q_ref.shape[0]):
        for pair in range(SWA_H // 2):
            acc = None
            for sub in range(2):
                h = 2 * pair + sub
                vcol = (2 * (h // grp) + sub) * LANES
                p_c, p_p = probs[bb, h]
                part = (_dot(p_c, vc_ref[bb, :, vcol:vcol + LANES]) + _dot(p_p, vp_ref[bb, :, vcol:vcol + LANES])) * inv[bb, h]
                acc = part if acc is None else acc + part
            o_ref[bb, :, pair * LANES:(pair + 1) * LANES] = acc.astype(o_ref.dtype)


def _swa(sq, sk, sv, sinks_row, batch, seq):
    w = WINDOW
    nb = seq // w
    ns = min(SWA_SEQS_PER_STEP, batch)
    wo = SWA_H * SWA_D
    cur = lambda b, n: (b, n, 0)
    prev = lambda b, n: (b, jnp.maximum(n - 1, 0), 0)
    r3 = lambda a: a.reshape(batch, seq, a.shape[-1])
    q3, k3, v3 = r3(sq), r3(sk), r3(sv)
    out = pl.pallas_call(
        _swa_kernel,
        grid=(batch // ns, nb),
        in_specs=[pl.BlockSpec((ns, w, sq.shape[1]), cur),
                  pl.BlockSpec((ns, w, sk.shape[1]), cur), pl.BlockSpec((ns, w, sk.shape[1]), prev),
                  pl.BlockSpec((ns, w, sv.shape[1]), cur), pl.BlockSpec((ns, w, sv.shape[1]), prev),
                  pl.BlockSpec((1, LANES), lambda b, n: (0, 0))],
        out_specs=pl.BlockSpec((ns, w, wo), cur),
        out_shape=jax.ShapeDtypeStruct((batch, seq, wo), BF16),
        compiler_params=_cparams(("arbitrary", "arbitrary")),
        name="swa",
    )(q3, k3, k3, v3, v3, sinks_row)
    return out.reshape(batch * seq, wo)


def _layer_norm(h, g, b):
    mu = jnp.mean(h, axis=-1, keepdims=True)
    d = h - mu
    var = jnp.mean(d * d, axis=-1, keepdims=True)
    return d * lax.rsqrt(var + LN_EPS) * g + b


def _outproj_kernel(x_ref, a1_ref, a2_ref, w_ref, g_ref, b_ref, o_ref, op_ref):
    k1 = a1_ref.shape[1]
    y = _dot(a1_ref[...].astype(BF16), w_ref[0:k1, :]) + _dot(a2_ref[...].astype(BF16), w_ref[k1:, :])
    h = _layer_norm(DN_ALPHA * x_ref[...] + y, g_ref[...], b_ref[...])
    o_ref[...] = h
    op_ref[...] = _pack_pairs(h)


def _outproj_ln(x, a1, a2, w, g, b, tm=512):
    t, d = x.shape
    row = lambda i: (i, 0)
    fix = lambda i: (0, 0)
    return pl.pallas_call(
        _outproj_kernel,
        grid=(t // tm,),
        in_specs=[pl.BlockSpec((tm, d), row), pl.BlockSpec((tm, a1.shape[1]), row), pl.BlockSpec((tm, a2.shape[1]), row),
                  pl.BlockSpec(w.shape, fix), pl.BlockSpec((1, d), fix), pl.BlockSpec((1, d), fix)],
        out_specs=[pl.BlockSpec((tm, d), row), pl.BlockSpec((tm, d // 2), row)],
        out_shape=[jax.ShapeDtypeStruct((t, d), F32), jax.ShapeDtypeStruct((t, d // 2), jnp.uint32)],
        compiler_params=_cparams(("arbitrary",)),
        name="outproj_ln",
    )(x, a1, a2, w, g, b)


def _first_index(x, m, iota_f, sentinel):
    return jnp.min(jnp.where(x == m, iota_f, sentinel), axis=0, keepdims=True)


def _router_kernel(x_ref, wt_ref, bias_ref, idx_ref, gate_ref, rank_ref, cnt_ref, carry_ref):
    tm = x_ref.shape[0]
    e = N_EXPERTS
    gs = e // N_GROUPS
    ninf = -jnp.inf

    @pl.when(pl.program_id(0) == 0)
    def _():
        carry_ref[...] = jnp.zeros(carry_ref.shape, F32)

    logits = _dot3(wt_ref[...], x_ref[...], _dot_nt)
    scores = _sigmoid(logits)
    sel = scores + bias_ref[:, 0:1]

    sub_f = _iota2((gs, tm), 0).astype(F32)
    gscore = []
    for g in range(N_GROUPS):
        blk = sel[g * gs:(g + 1) * gs, :]
        m1 = jnp.max(blk, axis=0, keepdims=True)
        i1 = _first_index(blk, m1, sub_f, float(gs))
        m2 = jnp.max(jnp.where(sub_f == i1, ninf, blk), axis=0, keepdims=True)
        gscore.append(m1 + m2)
    gsc = jnp.concatenate(gscore, axis=0)
    grp_f = _iota2((N_GROUPS, tm), 0).astype(F32)
    gmask = jnp.zeros((N_GROUPS, tm), F32)
    for _ in range(TOPK_GROUPS):
        m = jnp.max(gsc, axis=0, keepdims=True)
        gi = _first_index(gsc, m, grp_f, float(N_GROUPS))
        hit = grp_f == gi
        gmask = jnp.where(hit, 1.0, gmask)
        gsc = jnp.where(hit, ninf, gsc)
    masked = jnp.concatenate(
        [jnp.where(gmask[g:g + 1, :] > 0.0, sel[g * gs:(g + 1) * gs, :], ninf) for g in range(N_GROUPS)], axis=0)

    exp_f = _iota2((e, tm), 0).astype(F32)
    chosen = jnp.zeros((e, tm), F32)
    idxs, gates = [], []
    for _ in range(TOP_K):
        m = jnp.max(masked, axis=0, keepdims=True)
        ei = _first_index(masked, m, exp_f, float(e))
        hit = exp_f == ei
        idxs.append(ei)
        gates.append(jnp.sum(jnp.where(hit, scores, 0.0), axis=0, keepdims=True))
        chosen = jnp.where(hit, 1.0, chosen)
        masked = jnp.where(hit, ninf, masked)
    gate = jnp.concatenate(gates, axis=0)
    gate = gate / jnp.sum(gate, axis=0, keepdims=True) * ROUTED_SCALE
    idx_f = jnp.concatenate(idxs, axis=0)

    upper = (_iota2((tm, tm), 0) < _iota2((tm, tm), 1)).astype(BF16)
    before = _dot(chosen.astype(BF16), upper) + carry_ref[...][:, 0:1]
    ranks = [jnp.sum(jnp.where(exp_f == idxs[k], before, 0.0), axis=0, keepdims=True) for k in range(TOP_K)]
    carry_ref[...] = carry_ref[...] + jnp.sum(chosen, axis=1, keepdims=True)

    idx_ref[...] = idx_f.astype(jnp.int32)
    gate_ref[...] = gate
    rank_ref[...] = jnp.concatenate(ranks, axis=0).astype(jnp.int32)
    cnt_ref[...] = carry_ref[...]


def _router(x, wt, bias_col, tm=512):
    t, d = x.shape
    col = lambda i: (0, i)
    fix = lambda i: (0, 0)
    return pl.pallas_call(
        _router_kernel,
        grid=(t // tm,),
        in_specs=[pl.BlockSpec((tm, d), lambda i: (i, 0)), pl.BlockSpec(wt.shape, fix), pl.BlockSpec((N_EXPERTS, LANES), fix)],
        out_specs=[pl.BlockSpec((TOP_K, tm), col), pl.BlockSpec((TOP_K, tm), col), pl.BlockSpec((TOP_K, tm), col),
                   pl.BlockSpec((N_EXPERTS, LANES), fix)],
        out_shape=[jax.ShapeDtypeStruct((TOP_K, t), jnp.int32), jax.ShapeDtypeStruct((TOP_K, t), F32),
                   jax.ShapeDtypeStruct((TOP_K, t), jnp.int32), jax.ShapeDtypeStruct((N_EXPERTS, LANES), F32)],
        scratch_shapes=[pltpu.VMEM((N_EXPERTS, LANES), F32)],
        compiler_params=_cparams(("arbitrary",)),
        name="router",
    )(x, wt, bias_col)


def _dest_kernel(idx_ref, rank_ref, start_ref, dest_ref):
    tm = idx_ref.shape[1]
    exp_i = _iota2((N_EXPERTS, tm), 0)
    start = start_ref[:, 0:1]
    rows = [jnp.sum(jnp.where(exp_i == idx_ref[s:s + 1, :], start, 0.0), axis=0, keepdims=True) for s in range(TOP_K)]
    dest_ref[...] = jnp.concatenate(rows, axis=0).astype(jnp.int32) + rank_ref[...]


def _dest_rows(idx, rank, start_col, tm=2048):
    t = idx.shape[1]
    tm = min(tm, t)
    col = lambda i: (0, i)
    return pl.pallas_call(
        _dest_kernel,
        grid=(t // tm,),
        in_specs=[pl.BlockSpec((TOP_K, tm), col), pl.BlockSpec((TOP_K, tm), col),
                  pl.BlockSpec((N_EXPERTS, LANES), lambda i: (0, 0))],
        out_specs=pl.BlockSpec((TOP_K, tm), col),
        out_shape=jax.ShapeDtypeStruct((TOP_K, t), jnp.int32),
        compiler_params=_cparams(("arbitrary",)),
        name="moe_dest",
    )(idx, rank, start_col)


def _pack_pairs(x):
    n = x.shape[1] // 2
    hi = lax.bitcast_convert_type(x[:, :n].astype(BF16).astype(F32), jnp.uint32)
    lo = lax.bitcast_convert_type(x[:, n:].astype(BF16).astype(F32), jnp.uint32)
    return hi | (lo >> 16)


def _unpack_pairs(w):
    hi = lax.bitcast_convert_type(w & jnp.uint32(0xFFFF0000), F32)
    lo = lax.bitcast_convert_type(w << 16, F32)
    return hi, lo


def _sc_scatter_rows(xp, dest, rows, chunk=LANES):
    t, width = xp.shape
    info = plsc.get_sparse_core_info()
    ncores, nsub = info.num_cores, info.num_subcores
    per_worker = t // (ncores * nsub)
    nchunk = per_worker // chunk
    mesh = plsc.VectorSubcoreMesh(core_axis_name="c", subcore_axis_name="s")

    @functools.partial(
        pl.kernel, mesh=mesh,
        out_type=jax.ShapeDtypeStruct((rows, width), xp.dtype),
        scratch_types=[pltpu.VMEM((TOP_K, chunk), jnp.int32), pltpu.VMEM((chunk, width), xp.dtype), pltpu.SemaphoreType.DMA],
    )
    def scatter(xp_hbm, dest_hbm, out_hbm, idx_v, rows_v, sem):
        base = (lax.axis_index("s") * ncores + lax.axis_index("c")) * per_worker

        @pl.loop(0, nchunk)
        def _(i):
            off = pl.multiple_of(base + i * chunk, chunk)
            pltpu.sync_copy(dest_hbm.at[:, pl.ds(off, chunk)], idx_v)
            pltpu.sync_copy(xp_hbm.at[pl.ds(off, chunk)], rows_v)
            copies = [pltpu.async_copy(rows_v, out_hbm.at[idx_v.at[s]], sem) for s in range(TOP_K)]
            for cp in copies:
                cp.wait()

    return scatter(xp, dest)


def _experts_kernel(be_ref, nu_ref, nv_ref, xs_ref, wg_ref, wu_ref, wd_ref, ys_ref, wgb_ref, wub_ref, wdb_ref):
    i = pl.program_id(0)

    @pl.when(jnp.logical_or(i == 0, be_ref[i] != be_ref[jnp.maximum(i - 1, 0)]))
    def _():
        wgb_ref[...] = wg_ref[0, 0].astype(BF16)
        wub_ref[...] = wu_ref[0, 0].astype(BF16)
        wdb_ref[...] = wd_ref[0, 0].astype(BF16)

    @pl.when(i < nu_ref[0])
    def _():
        half = xs_ref.shape[1]
        sub = xs_ref.shape[0] // EXPERT_SUBBLOCKS
        acts = []
        for r in range(EXPERT_SUBBLOCKS):
            rows = pl.ds(r * sub, sub)
            live = (_iota2((sub, 1), 0) + r * sub) < nv_ref[i]
            xa, xb = _unpack_pairs(jnp.where(live, xs_ref[rows, :], jnp.uint32(0)))
            xa = xa.astype(BF16)
            xb = xb.astype(BF16)
            gate = _dot(xa, wgb_ref[:half, :]) + _dot(xb, wgb_ref[half:, :])
            up = _dot(xa, wub_ref[:half, :]) + _dot(xb, wub_ref[half:, :])
            acts.append((gate, up))
        outs = [_dot((_silu(gate) * up).astype(BF16), wdb_ref[...]) for gate, up in acts]
        for r, y in enumerate(outs):
            ys_ref[pl.ds(r * sub, sub), :] = _pack_pairs(y)


def _experts(block_e, n_used, n_valid, xs, wg, wu, wd, layer):
    rows, half = xs.shape
    d = 2 * half
    nb = rows // EXPERT_BLOCK
    blk = lambda i, be, nu, nv: (jnp.minimum(i, nu[0] - 1), 0)
    wsel = lambda i, be, nu, nv: (layer, be[i], 0, 0)
    return pl.pallas_call(
        _experts_kernel,
        grid_spec=pltpu.PrefetchScalarGridSpec(
            num_scalar_prefetch=3,
            grid=(nb,),
            in_specs=[pl.BlockSpec((EXPERT_BLOCK, half), blk),
                      pl.BlockSpec((1, 1, d, D_EXPERT), wsel), pl.BlockSpec((1, 1, d, D_EXPERT), wsel),
                      pl.BlockSpec((1, 1, D_EXPERT, d), wsel)],
            out_specs=pl.BlockSpec((EXPERT_BLOCK, half), blk),
            scratch_shapes=[pltpu.VMEM((d, D_EXPERT), BF16), pltpu.VMEM((d, D_EXPERT), BF16),
                            pltpu.VMEM((D_EXPERT, d), BF16)],
        ),
        out_shape=jax.ShapeDtypeStruct((rows, half), jnp.uint32),
        compiler_params=_cparams(("arbitrary",)),
        name="moe_experts",
    )(block_e, n_used, n_valid, xs, wg, wu, wd)


def _sc_gather_rows(table, idx, chunk=SC_CHUNK):
    n = idx.shape[0]
    width = table.shape[1]
    info = plsc.get_sparse_core_info()
    ncores, nsub = info.num_cores, info.num_subcores
    per_worker = n // (ncores * nsub)
    nchunk = per_worker // chunk
    mesh = plsc.VectorSubcoreMesh(core_axis_name="c", subcore_axis_name="s")

    @functools.partial(
        pl.kernel, mesh=mesh,
        out_type=jax.ShapeDtypeStruct((n, width), table.dtype),
        scratch_types=[pltpu.VMEM((nchunk, chunk), jnp.int32), pltpu.VMEM((2, chunk, width), table.dtype),
                       pltpu.SemaphoreType.DMA((2,)), pltpu.SemaphoreType.DMA((2,))],
    )
    def gather(table_hbm, idx_hbm, out_hbm, idx_v, rows_v, gsem, wsem):
        wid = lax.axis_index("s") * ncores + lax.axis_index("c")
        base = wid * per_worker
        pltpu.sync_copy(idx_hbm.at[pl.ds(wid * nchunk, nchunk)], idx_v)

        def fetch(j, b):
            return pltpu.make_async_copy(table_hbm.at[idx_v.at[j]], rows_v.at[b], gsem.at[b])

        def flush(j, b):
            off = pl.multiple_of(base + j * chunk, chunk)
            return pltpu.make_async_copy(rows_v.at[b], out_hbm.at[pl.ds(off, chunk)], wsem.at[b])

        fetch(0, 0).start()

        @pl.loop(0, nchunk, step=2)
        def _(i):
            for b in range(2):
                j = i + b
                fetch(j, b).wait()

                @pl.when(j + 1 < nchunk)
                def _():
                    @pl.when(j >= 1)
                    def _():
                        flush(j - 1, 1 - b).wait()

                    fetch(j + 1, 1 - b).start()

                flush(j, b).start()

        flush(nchunk - 2, 0).wait()
        flush(nchunk - 1, 1).wait()

    return gather(table, idx.reshape(n // chunk, chunk))


def _combine_kernel(x_ref, gate_ref, rows_ref, sg_ref, su_ref, sd_ref, g_ref, b_ref, o_ref):
    x = x_ref[...]
    xb = x.astype(BF16)
    hs = _silu(_dot(xb, sg_ref[...])) * _dot(xb, su_ref[...])
    ff = _dot(hs.astype(BF16), sd_ref[...])
    gate = gate_ref[...]
    half = rows_ref.shape[2]
    ya = ff[:, :half]
    yb = ff[:, half:]
    for s in range(TOP_K):
        a, b = _unpack_pairs(rows_ref[s])
        ya = ya + gate[:, s:s + 1] * a
        yb = yb + gate[:, s:s + 1] * b
    ff = jnp.concatenate([ya, yb], axis=1)
    o_ref[...] = _layer_norm(DN_ALPHA * x + ff, g_ref[...], b_ref[...])


def _combine_alias_kernel(prev_ref, *refs):
    del prev_ref
    _combine_kernel(*refs)


def _combine(x, gate_t, rows, sg, su, sd, g, b, part, nparts, prev=None, tm=512):
    t, d = x.shape
    tp = t // nparts
    tm = min(tm, tp)
    first = part * (tp // tm)
    row = lambda i: (first + i, 0)
    fix = lambda i: (0, 0)
    in_specs = [pl.BlockSpec((tm, d), row), pl.BlockSpec((tm, TOP_K), row),
                pl.BlockSpec((TOP_K, tm, d // 2), lambda i: (0, i, 0)),
                pl.BlockSpec(sg.shape, fix), pl.BlockSpec(su.shape, fix), pl.BlockSpec(sd.shape, fix),
                pl.BlockSpec((1, d), fix), pl.BlockSpec((1, d), fix)]
    args = (x, gate_t, rows, sg, su, sd, g, b)
    if prev is None:
        body, aliases = _combine_kernel, {}
    else:
        body, aliases = _combine_alias_kernel, {0: 0}
        in_specs = [pl.BlockSpec(memory_space=pl.ANY)] + in_specs
        args = (prev,) + args
    return pl.pallas_call(
        body,
        grid=(tp // tm,),
        in_specs=in_specs,
        out_specs=pl.BlockSpec((tm, d), row),
        out_shape=jax.ShapeDtypeStruct((t, d), F32),
        input_output_aliases=aliases,
        compiler_params=_cparams(("arbitrary",)),
        name="moe_combine",
    )(*args)


def _take_cols(w, idx):
    idx = np.asarray(idx)
    runs, start = [], 0
    for pos in range(1, len(idx) + 1):
        run_ends = pos == len(idx) or (idx[pos] != idx[pos - 1] + 1 if idx[pos - 1] >= 0 else idx[pos] >= 0)
        if run_ends:
            runs.append((start, int(idx[start]), pos - start))
            start = pos

    def body(w_ref, o_ref):
        for dst, src, width in runs:
            if src < 0:
                o_ref[:, dst:dst + width] = jnp.zeros((o_ref.shape[0], width), o_ref.dtype)
            else:
                o_ref[:, dst:dst + width] = w_ref[:, src:src + width].astype(o_ref.dtype)

    rows = w.shape[0]
    tr = min(rows, 256)
    return pl.pallas_call(
        body,
        grid=(rows // tr,),
        in_specs=[pl.BlockSpec((tr, w.shape[1]), lambda i: (i, 0))],
        out_specs=pl.BlockSpec((tr, len(idx)), lambda i: (i, 0)),
        out_shape=jax.ShapeDtypeStruct((rows, len(idx)), BF16),
        compiler_params=_cparams(("arbitrary",)),
        name="weight_cols",
    )(w)


def _pad_lane_row(v, first_lane, width=LANES):
    out = jnp.zeros((1, width), F32)
    return lax.dynamic_update_slice(out, v.reshape(1, -1).astype(F32), (0, first_lane))


def _even_in_cols():
    z = lambda n: -np.ones(n, int)
    kr0 = Q_LORA + KV_LORA
    half = MLA_ROPE // 2
    cols = [np.arange(0, Q_LORA), np.arange(Q_LORA, Q_LORA + KV_LORA),
            z(64), np.arange(kr0, kr0 + MLA_ROPE), z(32),
            z(64), np.arange(kr0 + half, kr0 + MLA_ROPE), np.arange(kr0, kr0 + half), z(32)]
    g0 = kr0 + MLA_ROPE
    nqk = GDN_H * GDN_DK
    cols.append(np.arange(g0, g0 + 3 * nqk))
    zoff = g0 + 3 * nqk + 2 * GDN_H
    cols.append(np.arange(zoff, zoff + GDN_H * GDN_DV))
    cols += [np.arange(g0 + 3 * nqk, g0 + 3 * nqk + 2 * GDN_H), z(LANES - 2 * GDN_H)]
    return np.concatenate(cols)


EV_WIDTHS = (Q_LORA + KV_LORA + 2 * LANES, 3 * GDN_H * GDN_DK, GDN_H * GDN_DV, LANES)


def _mla_q_cols():
    per = MLA_NOPE + MLA_ROPE
    half = MLA_ROPE // 2
    main, sw = [], []
    for h in range(MLA_H):
        b = h * per
        main += [np.arange(b, b + per), -np.ones(LANES - per, int)]
        sw += [-np.ones(MLA_NOPE, int), np.arange(b + MLA_NOPE + half, b + per), np.arange(b + MLA_NOPE, b + MLA_NOPE + half),
               -np.ones(LANES - per, int)]
    return np.concatenate(main + sw)


def _mla_kv_cols():
    per = MLA_NOPE + MLA_V
    kc, vc = [], []
    for h in range(MLA_H):
        b = h * per
        kc += [np.arange(b, b + MLA_NOPE), -np.ones(LANES - MLA_NOPE, int)]
        vv = np.arange(b + MLA_NOPE, b + per)
        pad = -np.ones(LANES - MLA_V, int)
        vc += [vv, pad] if h % 2 == 0 else [pad, vv]
    return np.concatenate(kc + vc)


def _odd_in_cols():
    z = lambda n: -np.ones(n, int)
    o = 0
    cols = []
    mq0, mk0 = 0, ML_H * ML_DK
    for base in (mq0, mk0):
        for h in range(ML_H):
            cols += [np.arange(base + h * ML_DK, base + (h + 1) * ML_DK), z(LANES - ML_DK)]
    mv0 = 2 * ML_H * ML_DK
    cols.append(np.arange(mv0, mv0 + ML_H * ML_DV))
    mi0 = mv0 + ML_H * ML_DV
    mo0 = mi0 + 2 * ML_H
    cols.append(np.arange(mo0, mo0 + ML_H * ML_DV))
    cols += [np.arange(mi0, mi0 + 2 * ML_H), z(LANES - 2 * ML_H)]
    sq0 = mo0 + ML_H * ML_DV
    sk0 = sq0 + SWA_H * SWA_D
    sv0 = sk0 + SWA_KV * SWA_D
    half = SWA_D // 2

    def heads(base, n, swapped, copies):
        out = []
        for h in range(n):
            b = base + h * SWA_D
            one = [np.arange(b + half, b + SWA_D), np.arange(b, b + half)] if swapped else [np.arange(b, b + SWA_D)]
            out += one * copies
        return out

    cols += (heads(sq0, SWA_H, False, 1) + heads(sq0, SWA_H, True, 1)
             + heads(sk0, SWA_KV, False, 2) + heads(sk0, SWA_KV, True, 2))
    for g in range(SWA_KV):
        vv = np.arange(sv0 + g * SWA_D, sv0 + (g + 1) * SWA_D)
        cols += [vv, z(LANES - SWA_D), z(LANES - SWA_D), vv]
    return np.concatenate(cols)


def _even_mixer(x, tabs, w_in, q_norm, w_qb, kv_norm, w_kvb, conv_w, a_log, dt_bias, o_norm, batch, seq):
    ctab, stab = tabs
    w = _take_cols(w_in, _even_in_cols()).astype(BF16)
    mla_in, act, z, gates = _proj_even(x, w, conv_w, seq)
    wq2 = _take_cols(w_qb, _mla_q_cols()).astype(BF16)
    wkv2 = _take_cols(w_kvb, _mla_kv_cols()).astype(BF16)
    q, k, v = _mla_prep(mla_in, ctab, stab, q_norm.reshape(1, -1), kv_norm.reshape(1, -1), wq2, wkv2)
    o_a = _mla_attn(q, k, v, batch, seq)
    o_b = _gdn(act, gates, z, _pad_lane_row(a_log, GDN_H), _pad_lane_row(dt_bias, GDN_H),
               o_norm.reshape(1, -1), batch, seq)
    return o_a, o_b


def _odd_mixer(x, tabs, w_in, b_i, b_f, ml_norm, sinks, batch, seq):
    ctab, stab = tabs
    w = _take_cols(w_in, _odd_in_cols()).astype(BF16)
    mq, mk, mv, mo, mg, sq, sk, sv = _proj_odd(x, w, ctab, stab)
    bias_row = _pad_lane_row(jnp.concatenate([b_i, b_f]), 0)
    o_c = _mlstm(mq, mk, mv, mo, mg, bias_row, ml_norm.reshape(1, -1), batch, seq)
    o_d = _swa(sq, sk, sv, _pad_lane_row(sinks, 0), batch, seq)
    return o_c, o_d


def _moe(x, xp, router_w, router_b, w_gate, w_up, w_down, layer, s_gate, s_up, s_down, ln_g, ln_b):
    t, d = x.shape
    bias_col = jnp.broadcast_to(router_b.reshape(-1, 1).astype(F32), (N_EXPERTS, LANES))
    idx, gate, rank, cnt = _router(x, router_w.T, bias_col)
    counts = cnt[:, 0].astype(jnp.int32)
    padded = (counts + EXPERT_BLOCK - 1) // EXPERT_BLOCK * EXPERT_BLOCK
    pad_end = jnp.cumsum(padded)
    pad_start = pad_end - padded
    start_col = jnp.broadcast_to(pad_start.astype(F32).reshape(-1, 1), (N_EXPERTS, LANES))
    dest = _dest_rows(idx, rank, start_col)
    n_blocks = t * TOP_K // EXPERT_BLOCK + N_EXPERTS
    rows = n_blocks * EXPERT_BLOCK
    block_row = jnp.arange(n_blocks, dtype=jnp.int32) * EXPERT_BLOCK
    block_e = jnp.minimum(jnp.sum((pad_end[None, :] <= block_row[:, None]).astype(jnp.int32), axis=1), N_EXPERTS - 1)
    n_used = (pad_end[-1:] // EXPERT_BLOCK).astype(jnp.int32)
    live_end = jnp.sum(jnp.where(block_e[:, None] == jnp.arange(N_EXPERTS, dtype=jnp.int32)[None, :],
                                 (pad_start + counts)[None, :], 0), axis=1)
    n_valid = jnp.clip(live_end - block_row, 0, EXPERT_BLOCK).astype(jnp.int32)
    xs = _sc_scatter_rows(xp, dest, rows)
    ys = _experts(block_e, n_used, n_valid, xs, w_gate, w_up, w_down, layer)
    nparts = COMBINE_PARTS if t % (COMBINE_PARTS * 512) == 0 else 1
    tp = t // nparts
    gate_t = gate.T
    sgb, sub, sdb = s_gate.astype(BF16), s_up.astype(BF16), s_down.astype(BF16)
    out = None
    for part in range(nparts):
        idx_p = dest[:, part * tp:(part + 1) * tp].reshape(-1)
        picked = _sc_gather_rows(ys, idx_p).reshape(TOP_K, tp, d // 2)
        out = _combine(x, gate_t, picked, sgb, sub, sdb, ln_g.reshape(1, -1), ln_b.reshape(1, -1), part, nparts, prev=out)
    return out


def kernel(x, positions, ev_w_in, mla_q_norm, mla_w_qb, mla_kv_norm, mla_w_kvb, gdn_conv, gdn_a_log, gdn_dt_bias, gdn_norm, ev_w_out, od_w_in, mlstm_b_i, mlstm_b_f, mlstm_norm, swa_sinks, od_w_out, ln1_g, ln1_b, router_w, router_b, moe_w_gate, moe_w_up, moe_w_down, shared_w_gate, shared_w_up, shared_w_down, ln2_g, ln2_b):
    batch, seq, d = x.shape
    t = batch * seq
    pos = positions.reshape(t, 1).astype(F32)
    tabs_m = _rope_tables(pos, _rope_rows(MLA_ROPE, MLA_NOPE, MLA_NOPE))
    tabs_s = _rope_tables(pos, _rope_rows(SWA_D, 0, 0, heads=LANES // SWA_D))
    h = x.reshape(t, d)
    for layer in range(DEPTH):
        j = layer // 2
        if layer % 2 == 0:
            a1, a2 = _even_mixer(h, tabs_m, ev_w_in[j], mla_q_norm[j], mla_w_qb[j], mla_kv_norm[j], mla_w_kvb[j],
                                 gdn_conv[j], gdn_a_log[j], gdn_dt_bias[j], gdn_norm[j], batch, seq)
            w_out = ev_w_out[j]
        else:
            a1, a2 = _odd_mixer(h, tabs_s, od_w_in[j], mlstm_b_i[j], mlstm_b_f[j], mlstm_norm[j], swa_sinks[j], batch, seq)
            w_out = od_w_out[j]
        h, hp = _outproj_ln(h, a1, a2, w_out.astype(BF16), ln1_g[layer].reshape(1, -1), ln1_b[layer].reshape(1, -1))
        h = _moe(h, hp, router_w[layer], router_b[layer], moe_w_gate, moe_w_up, moe_w_down, layer,
                 shared_w_gate[layer], shared_w_up[layer], shared_w_down[layer], ln2_g[layer], ln2_b[layer])
    return h.reshape(batch, seq, d)
```

```python
import functools
import math

import numpy as np
import jax
import jax.numpy as jnp
from jax import lax
from jax.experimental import pallas as pl
from jax.experimental.pallas import tpu as pltpu
from jax.experimental.pallas import tpu_sc as plsc

F32 = jnp.float32
BF16 = jnp.bfloat16
HI = lax.Precision.HIGHEST

D_MODEL = 1024
DEPTH = 4
ROPE_THETA = 10000.0
EPS = 1e-6
LN_EPS = 1e-5
MLA_H, MLA_NOPE, MLA_ROPE, MLA_V = 8, 64, 32, 64
Q_LORA, KV_LORA = 256, 128
GDN_H, GDN_DK, GDN_DV, CONV_W, GDN_CHUNK = 4, 128, 128, 4, 64
ML_H, ML_DK, ML_DV, ML_CHUNK = 4, 64, 128, 64
SWA_H, SWA_KV, SWA_D, WINDOW = 8, 2, 64, 128
N_EXPERTS, N_GROUPS, TOPK_GROUPS, TOP_K = 64, 8, 4, 8
D_EXPERT, D_SHARED = 256, 256
ROUTED_SCALE = 2.5
DN_ALPHA = (2 * DEPTH) ** 0.25

LANES = 128
V7X_VMEM_BYTES = 64 * 1024 * 1024
VMEM_LIMIT = 48 * 1024 * 1024

EXPERT_BLOCK = 512
EXPERT_SUBBLOCKS = 2
COMBINE_PARTS = 1
SWA_SEQS_PER_STEP = 4
MLSTM_SEQS_PER_STEP = 2
GDN_SEQS_PER_STEP = 8
SC_CHUNK = 64


def _cparams(sem, vmem=VMEM_LIMIT):
    return pltpu.CompilerParams(dimension_semantics=sem, vmem_limit_bytes=vmem)


def _dot(a, b, precision=None):
    return jnp.dot(a, b, preferred_element_type=F32, precision=precision)


def _dot_nt(a, b, precision=None):
    return lax.dot_general(a, b, (((1,), (1,)), ((), ())), preferred_element_type=F32, precision=precision)


def _dot_tn(a, b, precision=None):
    return lax.dot_general(a, b, (((0,), (0,)), ((), ())), preferred_element_type=F32, precision=precision)


def _split2(a):
    hi = a.astype(BF16)
    lo = (a - hi.astype(F32)).astype(BF16)
    return hi, lo


def _split3(a):
    p1 = a.astype(BF16)
    r = a - p1.astype(F32)
    p2 = r.astype(BF16)
    p3 = (r - p2.astype(F32)).astype(BF16)
    return p1, p2, p3


def _dot3(a, b, dot=_dot):
    ah, al = _split2(a)
    bh, bl = _split2(b)
    return dot(ah, bh) + (dot(ah, bl) + dot(al, bh))


def _dot_sel(sel, b, dot=_dot):
    sel = sel.astype(BF16)
    p1, p2, p3 = _split3(b)
    return dot(sel, p1) + (dot(sel, p2) + dot(sel, p3))


def _sigmoid(x):
    return 1.0 / (1.0 + jnp.exp(-x))


def _softplus(x):
    return jnp.maximum(x, 0.0) + jnp.log(1.0 + jnp.exp(-jnp.abs(x)))


def _silu(x):
    return x * _sigmoid(x)


def _lane_bcast(x, c):
    return jnp.broadcast_to(x[:, c:c + 1], x.shape)


def _iota2(shape, dim):
    return lax.broadcasted_iota(jnp.int32, shape, dim)


def _rope_kernel(pos_ref, rows_ref, c_ref, s_ref):
    ang = pos_ref[...] * rows_ref[0:1, :]
    c_ref[...] = rows_ref[1:2, :] * jnp.cos(ang) + rows_ref[2:3, :]
    s_ref[...] = rows_ref[3:4, :] * jnp.sin(ang)


def _rope_tables(pos, rows, tm=512):
    t = pos.shape[0]
    return pl.pallas_call(
        _rope_kernel,
        grid=(t // tm,),
        in_specs=[pl.BlockSpec((tm, 1), lambda i: (i, 0)), pl.BlockSpec((8, LANES), lambda i: (0, 0))],
        out_specs=[pl.BlockSpec((tm, LANES), lambda i: (i, 0))] * 2,
        out_shape=[jax.ShapeDtypeStruct((t, LANES), F32)] * 2,
        compiler_params=_cparams(("arbitrary",)),
        name="rope_tables",
    )(pos, rows)


def _rope_rows(dim, first_lane, pad_one_lanes, heads=1):
    half = dim // 2
    inv = ROPE_THETA ** (-(np.arange(0, dim, 2, dtype=np.float32) / dim))
    rows = np.zeros((8, LANES), np.float32)
    for h in range(heads):
        lo = slice(first_lane + h * dim, first_lane + h * dim + half)
        hi = slice(first_lane + h * dim + half, first_lane + (h + 1) * dim)
        rows[0, lo] = inv
        rows[0, hi] = inv
        rows[1, lo] = 1.0
        rows[1, hi] = 1.0
        rows[3, lo] = -1.0
        rows[3, hi] = 1.0
    rows[2, :pad_one_lanes] = 1.0
    return jnp.asarray(rows)


def _proj_kernel(x_ref, w_ref, *out_refs, offsets):
    xb = x_ref[...].astype(BF16)
    for o_ref, (a, b) in zip(out_refs, offsets):
        o_ref[...] = _dot(xb, w_ref[:, a:b]).astype(o_ref.dtype)


def _proj(x, w, widths, dtypes, tm=512):
    t, k = x.shape
    offs = np.concatenate([[0], np.cumsum(widths)]).tolist()
    offsets = tuple((offs[i], offs[i + 1]) for i in range(len(widths)))
    return pl.pallas_call(
        functools.partial(_proj_kernel, offsets=offsets),
        grid=(t // tm,),
        in_specs=[pl.BlockSpec((tm, k), lambda i: (i, 0)), pl.BlockSpec(w.shape, lambda i: (0, 0))],
        out_specs=[pl.BlockSpec((tm, n), lambda i: (i, 0)) for n in widths],
        out_shape=[jax.ShapeDtypeStruct((t, n), dt) for n, dt in zip(widths, dtypes)],
        compiler_params=_cparams(("arbitrary",)),
        name="in_proj",
    )(x, w)


def _proj_even_kernel(x_ref, w_ref, cw_ref, mla_ref, act_ref, z_ref, g_ref, ext_ref, *, tiles_per_seq):
    tm = x_ref.shape[0]
    o = np.concatenate([[0], np.cumsum(EV_WIDTHS)]).tolist()
    @pl.when(pl.program_id(0) % tiles_per_seq == 0)
    def _():
        ext_ref[0:8, :] = jnp.zeros((8, ext_ref.shape[1]), F32)

    xb = x_ref[...].astype(BF16)
    nchunk = 3
    cw = EV_WIDTHS[1] // nchunk

    def project(ci):
        ext_ref[8:8 + tm, ci * cw:(ci + 1) * cw] = _dot(xb, w_ref[:, o[1] + ci * cw:o[1] + (ci + 1) * cw])

    project(0)
    for ci in range(nchunk):
        if ci + 1 < nchunk:
            project(ci + 1)
        else:
            mla_ref[...] = _dot(xb, w_ref[:, o[0]:o[1]])
            z_ref[...] = _dot(xb, w_ref[:, o[2]:o[3]])
            g_ref[...] = _dot(xb, w_ref[:, o[3]:o[4]])
        cols = slice(ci * cw, (ci + 1) * cw)
        conv = cw_ref[0:1, cols] * ext_ref[5:5 + tm, cols]
        for j in range(1, CONV_W):
            conv = conv + cw_ref[j:j + 1, cols] * ext_ref[5 + j:5 + j + tm, cols]
        act_ref[:, cols] = _silu(conv)
    ext_ref[0:8, :] = ext_ref[tm:tm + 8, :]


def _proj_even(x, w, conv_w, seq, tm=512):
    t, k = x.shape
    tm = min(tm, seq)
    row = lambda i: (i, 0)
    fix = lambda i: (0, 0)
    return pl.pallas_call(
        functools.partial(_proj_even_kernel, tiles_per_seq=seq // tm),
        grid=(t // tm,),
        in_specs=[pl.BlockSpec((tm, k), row), pl.BlockSpec(w.shape, fix), pl.BlockSpec(conv_w.shape, fix)],
        out_specs=[pl.BlockSpec((tm, n), row) for n in EV_WIDTHS],
        out_shape=[jax.ShapeDtypeStruct((t, n), F32) for n in EV_WIDTHS],
        scratch_shapes=[pltpu.VMEM((tm + 8, EV_WIDTHS[1]), F32)],
        compiler_params=_cparams(("arbitrary",)),
        name="in_proj",
    )(x, w, conv_w)


OD_SEG = dict(mq=(0, 512), mk=(512, 1024), mv=(1024, 1536), mo=(1536, 2048), gates=(2048, 2176),
              sq=(2176, 2688), sqsw=(2688, 3200), sk=(3200, 3456), sksw=(3456, 3712), sv=(3712, 4224))
OD_COLS = 4224


def _proj_odd_kernel(x_ref, w_ref, c_ref, s_ref, mq_ref, mk_ref, mv_ref, mo_ref, mg_ref, sq_ref, sk_ref, sv_ref):
    xb = x_ref[...].astype(BF16)

    def seg(name):
        a, b = OD_SEG[name]
        return _dot(xb, w_ref[:, a:b])

    mq_ref[...] = seg("mq")
    mk_ref[...] = seg("mk")
    mv_ref[...] = seg("mv")
    mo_ref[...] = seg("mo")
    mg_ref[...] = seg("gates")
    c = c_ref[...]
    s = s_ref[...]
    c8 = jnp.concatenate([c] * (SWA_H // 2), axis=1)
    s8 = jnp.concatenate([s] * (SWA_H // 2), axis=1)
    sq_ref[...] = (seg("sq") * c8 + seg("sqsw") * s8).astype(sq_ref.dtype)
    c2 = jnp.concatenate([c] * SWA_KV, axis=1)
    s2 = jnp.concatenate([s] * SWA_KV, axis=1)
    sk_ref[...] = (seg("sk") * c2 + seg("sksw") * s2).astype(sk_ref.dtype)
    sv_ref[...] = seg("sv").astype(sv_ref.dtype)


def _proj_odd(x, w, ctab, stab, tm=256):
    t, k = x.shape
    widths = (512, 512, 512, 512, 128, SWA_H * SWA_D, SWA_KV * LANES, 2 * SWA_KV * LANES)
    dtypes = (F32, F32, F32, F32, F32, BF16, BF16, BF16)
    return pl.pallas_call(
        _proj_odd_kernel,
        grid=(t // tm,),
        in_specs=[pl.BlockSpec((tm, k), lambda i: (i, 0)), pl.BlockSpec(w.shape, lambda i: (0, 0)),
                  pl.BlockSpec((tm, LANES), lambda i: (i, 0)), pl.BlockSpec((tm, LANES), lambda i: (i, 0))],
        out_specs=[pl.BlockSpec((tm, n), lambda i: (i, 0)) for n in widths],
        out_shape=[jax.ShapeDtypeStruct((t, n), dt) for n, dt in zip(widths, dtypes)],
        compiler_params=_cparams(("arbitrary",)),
        name="in_proj_odd",
    )(x, w, ctab, stab)


def _rms(x, g):
    return x * lax.rsqrt(jnp.mean(x * x, axis=-1, keepdims=True) + EPS) * g


def _mla_prep_kernel(in_ref, c_ref, s_ref, qn_ref, kvn_ref, wq_ref, wkv_ref, q_ref, k_ref, v_ref):
    hw = MLA_H * LANES
    c = c_ref[...]
    s = s_ref[...]
    c8 = jnp.concatenate([c] * MLA_H, axis=1)
    s8 = jnp.concatenate([s] * MLA_H, axis=1)
    cqn = _rms(in_ref[:, 0:Q_LORA], qn_ref[...]).astype(BF16)
    qq = _dot(cqn, wq_ref[...])
    scale = (MLA_NOPE + MLA_ROPE) ** -0.5
    q_ref[...] = ((qq[:, :hw] * c8 + qq[:, hw:] * s8) * scale).astype(q_ref.dtype)
    ckvn = _rms(in_ref[:, Q_LORA:Q_LORA + KV_LORA], kvn_ref[...]).astype(BF16)
    kv = _dot(ckvn, wkv_ref[...])
    o = Q_LORA + KV_LORA
    krr = in_ref[:, o:o + LANES] * c + in_ref[:, o + LANES:o + 2 * LANES] * s
    k_ref[...] = (kv[:, :hw] + jnp.concatenate([krr] * MLA_H, axis=1)).astype(k_ref.dtype)
    v_ref[...] = kv[:, hw:].astype(v_ref.dtype)


def _mla_prep(mla_in, ctab, stab, qn, kvn, wq2, wkv2, tm=512):
    t = mla_in.shape[0]
    hw = MLA_H * LANES
    row = lambda i: (i, 0)
    fix = lambda i: (0, 0)
    return pl.pallas_call(
        _mla_prep_kernel,
        grid=(t // tm,),
        in_specs=[pl.BlockSpec((tm, mla_in.shape[1]), row), pl.BlockSpec((tm, LANES), row), pl.BlockSpec((tm, LANES), row),
                  pl.BlockSpec(qn.shape, fix), pl.BlockSpec(kvn.shape, fix),
                  pl.BlockSpec(wq2.shape, fix), pl.BlockSpec(wkv2.shape, fix)],
        out_specs=[pl.BlockSpec((tm, hw), row)] * 3,
        out_shape=[jax.ShapeDtypeStruct((t, hw), BF16)] * 3,
        compiler_params=_cparams(("arbitrary",)),
        name="mla_prep",
    )(mla_in, ctab, stab, qn, kvn, wq2, wkv2)


def _mla_attn_kernel(q_ref, k_ref, v_ref, o_ref, *, tq):
    i = pl.program_id(2)
    neg = -1e30
    lane = _iota2((tq, LANES), 1)
    ones_lane = (MLA_V, 0)

    def chunk(j, carry, masked):
        start = pl.multiple_of(j * tq, tq)
        out = []
        for hh in range(2):
            m, acc = carry[hh]
            q = q_ref[:, hh * LANES:(hh + 1) * LANES]
            kc = k_ref[pl.ds(start, tq), hh * LANES:(hh + 1) * LANES]
            vc = v_ref[pl.ds(start, tq), hh * LANES:(hh + 1) * LANES]
            vc = jnp.where(lane == ones_lane[hh], jnp.ones_like(vc), vc)
            s = _dot_nt(q, kc)
            if masked:
                s = jnp.where(_iota2(s.shape, 0) >= _iota2(s.shape, 1), s, neg)
            m_new = jnp.maximum(m, jnp.max(s, axis=-1, keepdims=True))
            alpha = jnp.exp(m - m_new)
            p = jnp.exp(s - m_new)
            acc = alpha * acc + _dot(p.astype(BF16), vc)
            out.append((m_new, acc))
        return tuple(out)

    one = (jnp.full((tq, 1), neg, F32), jnp.zeros((tq, LANES), F32))
    carry = lax.fori_loop(0, i, lambda j, c: chunk(j, c, False), (one, one))
    (_, acc0), (_, acc1) = chunk(i, carry, True)
    o0 = acc0 / _lane_bcast(acc0, ones_lane[0])
    o1 = acc1 / _lane_bcast(acc1, ones_lane[1])
    o_ref[...] = jnp.where(lane < MLA_V, o0, o1).astype(o_ref.dtype)


def _mla_attn(q, k, v, batch, seq, tq=512):
    tq = min(tq, seq)
    nq = seq // tq
    pairs = MLA_H // 2
    return pl.pallas_call(
        functools.partial(_mla_attn_kernel, tq=tq),
        grid=(batch, pairs, nq),
        in_specs=[pl.BlockSpec((tq, 2 * LANES), lambda b, p, i: (b * nq + i, p)),
                  pl.BlockSpec((seq, 2 * LANES), lambda b, p, i: (b, p)),
                  pl.BlockSpec((seq, 2 * LANES), lambda b, p, i: (b, p))],
        out_specs=pl.BlockSpec((tq, LANES), lambda b, p, i: (b * nq + i, p)),
        out_shape=jax.ShapeDtypeStruct((batch * seq, pairs * LANES), BF16),
        compiler_params=_cparams(("arbitrary", "arbitrary", "arbitrary")),
        name="mla_attn",
    )(q, k, v)


def _unit_lower_inverse_many(ns):
    c = ns[0].shape[0]
    eye = (_iota2((c, c), 0) == _iota2((c, c), 1)).astype(F32)
    xs = [-n for n in ns]
    ps = [eye + x for x in xs]
    xb = [x.astype(BF16) for x in xs]
    for _ in range(int(math.log2(c)) - 1):
        xs = [_dot(b, b) for b in xb]
        xb = [x.astype(BF16) for x in xs]
        ps = [p + _dot(p.astype(BF16), b) for p, b in zip(ps, xb)]
    return ps


def _gdn_kernel(act_ref, g_ref, z_ref, al_ref, dt_ref, on_ref, o_ref, st_ref):
    c = GDN_CHUNK
    hd = GDN_DK
    nqk = GDN_H * GDN_DK

    @pl.when(pl.program_id(1) == 0)
    def _():
        st_ref[...] = jnp.zeros(st_ref.shape, F32)

    tri = (_iota2((c, c), 0) >= _iota2((c, c), 1)).astype(F32)
    row_ge = _iota2((c, c), 0) >= _iota2((c, c), 1)
    row_gt = _iota2((c, c), 0) > _iota2((c, c), 1)
    lane = _iota2((c, LANES), 1)

    units = []
    for bb in range(act_ref.shape[0]):
        act = act_ref[bb]
        gates = g_ref[bb]
        beta_all = _sigmoid(gates)
        g_all = -jnp.exp(al_ref[...]) * _softplus(gates + dt_ref[...])
        gc_all = _dot_sel(tri, g_all)
        gc_parts = _split3(gc_all)
        for h in range(GDN_H):
            q = act[:, h * hd:(h + 1) * hd]
            k = act[:, nqk + h * hd:nqk + (h + 1) * hd]
            v = act[:, 2 * nqk + h * GDN_DV:2 * nqk + (h + 1) * GDN_DV]
            q = q * lax.rsqrt(jnp.sum(q * q, axis=-1, keepdims=True) + EPS) * (GDN_DK ** -0.5)
            k = k * lax.rsqrt(jnp.sum(k * k, axis=-1, keepdims=True) + EPS)
            beta = _lane_bcast(beta_all, h)
            gcol = _lane_bcast(gc_all, GDN_H + h)
            pick = (lane == GDN_H + h).astype(BF16)
            grow = _dot_nt(pick, gc_parts[0]) + (_dot_nt(pick, gc_parts[1]) + _dot_nt(pick, gc_parts[2]))
            decay = jnp.exp(jnp.where(row_ge, gcol[:, :c] - grow, -jnp.inf))
            kb = k * beta
            lower = jnp.where(row_gt, _dot3(kb, k, _dot_nt) * decay, 0.0)
            eg = jnp.exp(gcol)
            glast = gcol[c - 1:c, :]
            units.append(dict(bb=bb, h=h, lower=lower, rhs=jnp.concatenate([v * beta, kb * eg], axis=1),
                              attn=_dot_nt(q.astype(BF16), k.astype(BF16)) * decay, qg=(q * eg).astype(BF16),
                              kg=(k * jnp.exp(glast - gcol)).astype(BF16), gl=jnp.exp(glast)))

    tinvs = _unit_lower_inverse_many([u["lower"] for u in units])
    uws = []
    for u, tinv in zip(units, tinvs):
        uws.append(_dot(tinv.astype(BF16), u["rhs"].astype(BF16)))
    states = [st_ref[u["bb"], u["h"]] for u in units]
    sbs = [s.astype(BF16) for s in states]
    vnews = [(uw[:, :GDN_DV] - _dot(uw[:, GDN_DV:].astype(BF16), sb)).astype(BF16) for uw, sb in zip(uws, sbs)]
    for u, state, sb, vnb in zip(units, states, sbs, vnews):
        bb, h = u["bb"], u["h"]
        o = _dot(u["qg"], sb) + _dot(u["attn"].astype(BF16), vnb)
        st_ref[bb, h] = state * u["gl"] + _dot_tn(u["kg"], vnb)
        o = _rms(o, on_ref[...]) * _silu(z_ref[bb, :, h * GDN_DV:(h + 1) * GDN_DV])
        o_ref[bb, :, h * GDN_DV:(h + 1) * GDN_DV] = o.astype(o_ref.dtype)


def _gdn(act, gates, z, a_row, dt_row, o_norm, batch, seq):
    c = GDN_CHUNK
    nc = seq // c
    w3 = act.shape[1]
    wo = GDN_H * GDN_DV
    nb = min(GDN_SEQS_PER_STEP, batch)
    row = lambda b, i: (b, i, 0)
    fix = lambda b, i: (0, 0)
    out = pl.pallas_call(
        _gdn_kernel,
        grid=(batch // nb, nc),
        in_specs=[pl.BlockSpec((nb, c, w3), row), pl.BlockSpec((nb, c, LANES), row), pl.BlockSpec((nb, c, wo), row),
                  pl.BlockSpec((1, LANES), fix), pl.BlockSpec((1, LANES), fix), pl.BlockSpec((1, GDN_DV), fix)],
        out_specs=pl.BlockSpec((nb, c, wo), row),
        out_shape=jax.ShapeDtypeStruct((batch, seq, wo), BF16),
        scratch_shapes=[pltpu.VMEM((nb, GDN_H, GDN_DK, GDN_DV), F32)],
        compiler_params=_cparams(("arbitrary", "arbitrary")),
        name="gdn",
    )(act.reshape(batch, seq, w3), gates.reshape(batch, seq, LANES), z.reshape(batch, seq, wo), a_row, dt_row, o_norm)
    return out.reshape(batch * seq, wo)


def _mlstm_kernel(q_ref, k_ref, v_ref, og_ref, g_ref, bias_ref, nrm_ref, o_ref, c_ref, n_ref, m_ref):
    @pl.when(pl.program_id(1) == 0)
    def _():
        c_ref[...] = jnp.zeros(c_ref.shape, F32)
        n_ref[...] = jnp.zeros(n_ref.shape, F32)
        m_ref[...] = jnp.zeros(m_ref.shape, F32)

    c = ML_CHUNK
    tri = (_iota2((c, c), 0) >= _iota2((c, c), 1)).astype(F32)
    row_ge = _iota2((c, c), 0) >= _iota2((c, c), 1)
    ones = jnp.ones((c, LANES), F32)
    lane = _iota2((c, LANES), 1)

    units = []
    for bb in range(q_ref.shape[0]):
        pre = g_ref[bb] + bias_ref[...]
        logf = jnp.minimum(pre, 0.0) - jnp.log(1.0 + jnp.exp(-jnp.abs(pre)))
        bcum_all = _dot_sel(tri, logf)
        for h in range(ML_H):
            q = q_ref[bb, :, h * LANES:(h + 1) * LANES]
            k = k_ref[bb, :, h * LANES:(h + 1) * LANES] * (ML_DK ** -0.5)
            units.append(dict(bb=bb, h=h, q=q, k=k, qb=q.astype(BF16), vb=v_ref[bb, :, h * ML_DV:(h + 1) * ML_DV].astype(BF16),
                              bcol=_lane_bcast(bcum_all, ML_H + h),
                              icol=_lane_bcast(pre, h),
                              col=jnp.where(lane == h, pre, 0.0) - jnp.where(lane == ML_H + h, bcum_all, 0.0),
                              m_st=m_ref[bb, h], cst=c_ref[bb, h], nst=n_ref[bb, h]))
    for u in units:
        u["row"] = _dot_sel(ones, u["col"], _dot_nt)
        u["qk"] = _dot_nt(u["qb"], u["k"].astype(BF16))
        u["qc"] = _dot(u["qb"], u["cst"].astype(BF16))
    for u in units:
        d = jnp.where(row_ge, u["bcol"][:, :c] + u["row"], -jnp.inf)
        inter = u["bcol"] + u["m_st"]
        m_t = jnp.maximum(inter, jnp.max(d, axis=-1, keepdims=True))
        u["m_t"] = m_t
        u["w_inter"] = jnp.exp(inter - m_t)
        u["p"] = jnp.exp(d - m_t[:, :c]) * u["qk"]
        u["pv"] = _dot(u["p"].astype(BF16), u["vb"])
        b_end = u["bcol"][c - 1:c, :]
        a = b_end - u["bcol"] + u["icol"]
        m_new = jnp.maximum(b_end + u["m_st"], jnp.max(a, axis=0, keepdims=True))
        u["m_new"] = m_new
        u["keep"] = jnp.exp(b_end + u["m_st"] - m_new)
        u["ks"] = u["k"] * jnp.exp(a - m_new)
        u["kv"] = _dot_tn(u["ks"].astype(BF16), u["vb"])
    for u in units:
        bb, h = u["bb"], u["h"]
        num = u["w_inter"] * u["qc"] + u["pv"]
        den = (u["w_inter"] * jnp.sum(u["q"] * u["nst"], axis=-1, keepdims=True)
               + jnp.sum(u["p"], axis=-1, keepdims=True))
        hc = num / jnp.maximum(jnp.abs(den), jnp.exp(-u["m_t"]))
        c_ref[bb, h] = u["cst"] * u["keep"] + u["kv"]
        n_ref[bb, h] = u["nst"] * u["keep"] + jnp.sum(u["ks"], axis=0, keepdims=True)
        m_ref[bb, h] = u["m_new"]
        hn = (_rms(hc, nrm_ref[:, h * ML_DV:(h + 1) * ML_DV])
              * _sigmoid(og_ref[bb, :, h * ML_DV:(h + 1) * ML_DV]))
        o_ref[bb, :, h * ML_DV:(h + 1) * ML_DV] = hn.astype(o_ref.dtype)


def _mlstm(mq, mk, mv, mo, gates, bias_row, norm_row, batch, seq):
    c = ML_CHUNK
    nc = seq // c
    nb = min(MLSTM_SEQS_PER_STEP, batch)
    row = lambda b, i: (b, i, 0)
    fix = lambda b, i: (0, 0)
    wide = ML_H * LANES
    r3 = lambda a: a.reshape(batch, seq, a.shape[-1])
    out = pl.pallas_call(
        _mlstm_kernel,
        grid=(batch // nb, nc),
        in_specs=[pl.BlockSpec((nb, c, wide), row), pl.BlockSpec((nb, c, wide), row), pl.BlockSpec((nb, c, wide), row),
                  pl.BlockSpec((nb, c, wide), row), pl.BlockSpec((nb, c, LANES), row),
                  pl.BlockSpec((1, LANES), fix), pl.BlockSpec((1, wide), fix)],
        out_specs=pl.BlockSpec((nb, c, wide), row),
        out_shape=jax.ShapeDtypeStruct((batch, seq, wide), BF16),
        scratch_shapes=[pltpu.VMEM((nb, ML_H, LANES, ML_DV), F32), pltpu.VMEM((nb, ML_H, 1, LANES), F32),
                        pltpu.VMEM((nb, ML_H, 1, LANES), F32)],
        compiler_params=_cparams(("arbitrary", "arbitrary")),
        name="mlstm",
    )(r3(mq), r3(mk), r3(mv), r3(mo), r3(gates), bias_row, norm_row)
    return out.reshape(batch * seq, wide)


def _swa_kernel(q_ref, kc_ref, kp_ref, vc_ref, vp_ref, sink_ref, o_ref):
    w = WINDOW
    n = pl.program_id(1)
    scale = SWA_D ** -0.5
    qi = _iota2((w, w), 0)
    kj = _iota2((w, w), 1)
    mask_c = kj <= qi
    mask_p = jnp.logical_and(kj > qi, n > 0)
    grp = SWA_H // SWA_KV
    neg = -1e30
    units = [(bb, h) for bb in range(q_ref.shape[0]) for h in range(SWA_H)]
    scores = []
    half_of_lane = _iota2((w, LANES), 1) // SWA_D
    for bb, h in units:
        g = h // grp
        pair = q_ref[bb, :, (h // 2) * LANES:(h // 2 + 1) * LANES]
        q = jnp.where(half_of_lane == h % 2, pair, jnp.zeros_like(pair))
        scores.append((_dot_nt(q, kc_ref[bb, :, g * LANES:(g + 1) * LANES]),
                       _dot_nt(q, kp_ref[bb, :, g * LANES:(g + 1) * LANES])))
    masked, tops, exps, dens, probs = [], [], [], [], {}
    for sc, sp in scores:
        masked.append((jnp.where(mask_c, sc * scale, neg), jnp.where(mask_p, sp * scale, neg)))
    for (bb, h), (s_c, s_p) in zip(units, masked):
        tops.append(jnp.maximum(jnp.max(jnp.maximum(s_c, s_p), axis=-1, keepdims=True), sink_ref[:, h:h + 1]))
    for (s_c, s_p), m in zip(masked, tops):
        exps.append((jnp.where(mask_c, jnp.exp(s_c - m), 0.0), jnp.where(mask_p, jnp.exp(s_p - m), 0.0)))
    ones_b = jnp.ones((w, LANES), BF16)
    for (bb, h), (p_c, p_p), m in zip(units, exps, tops):
        p_c, p_p = p_c.astype(BF16), p_p.astype(BF16)
        probs[bb, h] = (p_c, p_p)
        dens.append(_dot(p_c, ones_b) + _dot(p_p, ones_b) + jnp.exp(sink_ref[:, h:h + 1] - m))
    inv = {u: 1.0 / den for u, den in zip(units, dens)}
    for bb in range(q_ref.shape[0]):
        for pair in range(SWA_H // 2):
            acc = None
            for sub in range(2):
                h = 2 * pair + sub
                vcol = (2 * (h // grp) + sub) * LANES
                p_c, p_p = probs[bb, h]
                part = (_dot(p_c, vc_ref[bb, :, vcol:vcol + LANES]) + _dot(p_p, vp_ref[bb, :, vcol:vcol + LANES])) * inv[bb, h]
                acc = part if acc is None else acc + part
            o_ref[bb, :, pair * LANES:(pair + 1) * LANES] = acc.astype(o_ref.dtype)


def _swa(sq, sk, sv, sinks_row, batch, seq):
    w = WINDOW
    nb = seq // w
    ns = min(SWA_SEQS_PER_STEP, batch)
    wo = SWA_H * SWA_D
    cur = lambda b, n: (b, n, 0)
    prev = lambda b, n: (b, jnp.maximum(n - 1, 0), 0)
    r3 = lambda a: a.reshape(batch, seq, a.shape[-1])
    q3, k3, v3 = r3(sq), r3(sk), r3(sv)
    out = pl.pallas_call(
        _swa_kernel,
        grid=(batch // ns, nb),
        in_specs=[pl.BlockSpec((ns, w, sq.shape[1]), cur),
                  pl.BlockSpec((ns, w, sk.shape[1]), cur), pl.BlockSpec((ns, w, sk.shape[1]), prev),
                  pl.BlockSpec((ns, w, sv.shape[1]), cur), pl.BlockSpec((ns, w, sv.shape[1]), prev),
                  pl.BlockSpec((1, LANES), lambda b, n: (0, 0))],
        out_specs=pl.BlockSpec((ns, w, wo), cur),
        out_shape=jax.ShapeDtypeStruct((batch, seq, wo), BF16),
        compiler_params=_cparams(("arbitrary", "arbitrary")),
        name="swa",
    )(q3, k3, k3, v3, v3, sinks_row)
    return out.reshape(batch * seq, wo)


def _layer_norm(h, g, b):
    mu = jnp.mean(h, axis=-1, keepdims=True)
    d = h - mu
    var = jnp.mean(d * d, axis=-1, keepdims=True)
    return d * lax.rsqrt(var + LN_EPS) * g + b


def _outproj_kernel(x_ref, a1_ref, a2_ref, w_ref, g_ref, b_ref, o_ref, op_ref):
    k1 = a1_ref.shape[1]
    y = _dot(a1_ref[...].astype(BF16), w_ref[0:k1, :]) + _dot(a2_ref[...].astype(BF16), w_ref[k1:, :])
    h = _layer_norm(DN_ALPHA * x_ref[...] + y, g_ref[...], b_ref[...])
    o_ref[...] = h
    op_ref[...] = _pack_pairs(h)


def _outproj_ln(x, a1, a2, w, g, b, tm=512):
    t, d = x.shape
    row = lambda i: (i, 0)
    fix = lambda i: (0, 0)
    return pl.pallas_call(
        _outproj_kernel,
        grid=(t // tm,),
        in_specs=[pl.BlockSpec((tm, d), row), pl.BlockSpec((tm, a1.shape[1]), row), pl.BlockSpec((tm, a2.shape[1]), row),
                  pl.BlockSpec(w.shape, fix), pl.BlockSpec((1, d), fix), pl.BlockSpec((1, d), fix)],
        out_specs=[pl.BlockSpec((tm, d), row), pl.BlockSpec((tm, d // 2), row)],
        out_shape=[jax.ShapeDtypeStruct((t, d), F32), jax.ShapeDtypeStruct((t, d // 2), jnp.uint32)],
        compiler_params=_cparams(("arbitrary",)),
        name="outproj_ln",
    )(x, a1, a2, w, g, b)


def _first_index(x, m, iota_f, sentinel):
    return jnp.min(jnp.where(x == m, iota_f, sentinel), axis=0, keepdims=True)


def _router_kernel(x_ref, wt_ref, bias_ref, idx_ref, gate_ref, rank_ref, cnt_ref, carry_ref):
    tm = x_ref.shape[0]
    e = N_EXPERTS
    gs = e // N_GROUPS
    ninf = -jnp.inf

    @pl.when(pl.program_id(0) == 0)
    def _():
        carry_ref[...] = jnp.zeros(carry_ref.shape, F32)

    logits = _dot3(wt_ref[...], x_ref[...], _dot_nt)
    scores = _sigmoid(logits)
    sel = scores + bias_ref[:, 0:1]

    sub_f = _iota2((gs, tm), 0).astype(F32)
    gscore = []
    for g in range(N_GROUPS):
        blk = sel[g * gs:(g + 1) * gs, :]
        m1 = jnp.max(blk, axis=0, keepdims=True)
        i1 = _first_index(blk, m1, sub_f, float(gs))
        m2 = jnp.max(jnp.where(sub_f == i1, ninf, blk), axis=0, keepdims=True)
        gscore.append(m1 + m2)
    gsc = jnp.concatenate(gscore, axis=0)
    grp_f = _iota2((N_GROUPS, tm), 0).astype(F32)
    gmask = jnp.zeros((N_GROUPS, tm), F32)
    for _ in range(TOPK_GROUPS):
        m = jnp.max(gsc, axis=0, keepdims=True)
        gi = _first_index(gsc, m, grp_f, float(N_GROUPS))
        hit = grp_f == gi
        gmask = jnp.where(hit, 1.0, gmask)
        gsc = jnp.where(hit, ninf, gsc)
    masked = jnp.concatenate(
        [jnp.where(gmask[g:g + 1, :] > 0.0, sel[g * gs:(g + 1) * gs, :], ninf) for g in range(N_GROUPS)], axis=0)

    exp_f = _iota2((e, tm), 0).astype(F32)
    chosen = jnp.zeros((e, tm), F32)
    idxs, gates = [], []
    for _ in range(TOP_K):
        m = jnp.max(masked, axis=0, keepdims=True)
        ei = _first_index(masked, m, exp_f, float(e))
        hit = exp_f == ei
        idxs.append(ei)
        gates.append(jnp.sum(jnp.where(hit, scores, 0.0), axis=0, keepdims=True))
        chosen = jnp.where(hit, 1.0, chosen)
        masked = jnp.where(hit, ninf, masked)
    gate = jnp.concatenate(gates, axis=0)
    gate = gate / jnp.sum(gate, axis=0, keepdims=True) * ROUTED_SCALE
    idx_f = jnp.concatenate(idxs, axis=0)

    upper = (_iota2((tm, tm), 0) < _iota2((tm, tm), 1)).astype(BF16)
    before = _dot(chosen.astype(BF16), upper) + carry_ref[...][:, 0:1]
    ranks = [jnp.sum(jnp.where(exp_f == idxs[k], before, 0.0), axis=0, keepdims=True) for k in range(TOP_K)]
    carry_ref[...] = carry_ref[...] + jnp.sum(chosen, axis=1, keepdims=True)

    idx_ref[...] = idx_f.astype(jnp.int32)
    gate_ref[...] = gate
    rank_ref[...] = jnp.concatenate(ranks, axis=0).astype(jnp.int32)
    cnt_ref[...] = carry_ref[...]


def _router(x, wt, bias_col, tm=512):
    t, d = x.shape
    col = lambda i: (0, i)
    fix = lambda i: (0, 0)
    return pl.pallas_call(
        _router_kernel,
        grid=(t // tm,),
        in_specs=[pl.BlockSpec((tm, d), lambda i: (i, 0)), pl.BlockSpec(wt.shape, fix), pl.BlockSpec((N_EXPERTS, LANES), fix)],
        out_specs=[pl.BlockSpec((TOP_K, tm), col), pl.BlockSpec((TOP_K, tm), col), pl.BlockSpec((TOP_K, tm), col),
                   pl.BlockSpec((N_EXPERTS, LANES), fix)],
        out_shape=[jax.ShapeDtypeStruct((TOP_K, t), jnp.int32), jax.ShapeDtypeStruct((TOP_K, t), F32),
                   jax.ShapeDtypeStruct((TOP_K, t), jnp.int32), jax.ShapeDtypeStruct((N_EXPERTS, LANES), F32)],
        scratch_shapes=[pltpu.VMEM((N_EXPERTS, LANES), F32)],
        compiler_params=_cparams(("arbitrary",)),
        name="router",
    )(x, wt, bias_col)


def _dest_kernel(idx_ref, rank_ref, start_ref, dest_ref):
    tm = idx_ref.shape[1]
    exp_i = _iota2((N_EXPERTS, tm), 0)
    start = start_ref[:, 0:1]
    rows = [jnp.sum(jnp.where(exp_i == idx_ref[s:s + 1, :], start, 0.0), axis=0, keepdims=True) for s in range(TOP_K)]
    dest_ref[...] = jnp.concatenate(rows, axis=0).astype(jnp.int32) + rank_ref[...]


def _dest_rows(idx, rank, start_col, tm=2048):
    t = idx.shape[1]
    tm = min(tm, t)
    col = lambda i: (0, i)
    return pl.pallas_call(
        _dest_kernel,
        grid=(t // tm,),
        in_specs=[pl.BlockSpec((TOP_K, tm), col), pl.BlockSpec((TOP_K, tm), col),
                  pl.BlockSpec((N_EXPERTS, LANES), lambda i: (0, 0))],
        out_specs=pl.BlockSpec((TOP_K, tm), col),
        out_shape=jax.ShapeDtypeStruct((TOP_K, t), jnp.int32),
        compiler_params=_cparams(("arbitrary",)),
        name="moe_dest",
    )(idx, rank, start_col)


def _pack_pairs(x):
    n = x.shape[1] // 2
    hi = lax.bitcast_convert_type(x[:, :n].astype(BF16).astype(F32), jnp.uint32)
    lo = lax.bitcast_convert_type(x[:, n:].astype(BF16).astype(F32), jnp.uint32)
    return hi | (lo >> 16)


def _unpack_pairs(w):
    hi = lax.bitcast_convert_type(w & jnp.uint32(0xFFFF0000), F32)
    lo = lax.bitcast_convert_type(w << 16, F32)
    return hi, lo


def _sc_scatter_rows(xp, dest, rows, chunk=LANES):
    t, width = xp.shape
    info = plsc.get_sparse_core_info()
    ncores, nsub = info.num_cores, info.num_subcores
    per_worker = t // (ncores * nsub)
    nchunk = per_worker // chunk
    mesh = plsc.VectorSubcoreMesh(core_axis_name="c", subcore_axis_name="s")

    @functools.partial(
        pl.kernel, mesh=mesh,
        out_type=jax.ShapeDtypeStruct((rows, width), xp.dtype),
        scratch_types=[pltpu.VMEM((TOP_K, chunk), jnp.int32), pltpu.VMEM((chunk, width), xp.dtype), pltpu.SemaphoreType.DMA],
    )
    def scatter(xp_hbm, dest_hbm, out_hbm, idx_v, rows_v, sem):
        base = (lax.axis_index("s") * ncores + lax.axis_index("c")) * per_worker

        @pl.loop(0, nchunk)
        def _(i):
            off = pl.multiple_of(base + i * chunk, chunk)
            pltpu.sync_copy(dest_hbm.at[:, pl.ds(off, chunk)], idx_v)
            pltpu.sync_copy(xp_hbm.at[pl.ds(off, chunk)], rows_v)
            copies = [pltpu.async_copy(rows_v, out_hbm.at[idx_v.at[s]], sem) for s in range(TOP_K)]
            for cp in copies:
                cp.wait()

    return scatter(xp, dest)


def _experts_kernel(be_ref, nu_ref, nv_ref, first_ref, slot_ref, nxt_ref, xs_ref, wg_hbm, wu_hbm, wd_hbm, ys_ref,
                    wgf_ref, wuf_ref, wdf_ref, wgb_ref, wub_ref, wdb_ref, sem, *, layer):
    i = pl.program_id(0)

    def fetch(e, s):
        return [pltpu.make_async_copy(wg_hbm.at[layer, e], wgf_ref.at[s], sem.at[s]),
                pltpu.make_async_copy(wu_hbm.at[layer, e], wuf_ref.at[s], sem.at[s]),
                pltpu.make_async_copy(wd_hbm.at[layer, e], wdf_ref.at[s], sem.at[s])]

    @pl.when(i == 0)
    def _():
        for cp in fetch(be_ref[0], 0):
            cp.start()

    @pl.when(jnp.logical_and(first_ref[i] == 1, i < nu_ref[0]))
    def _():
        s = slot_ref[i]
        for cp in fetch(be_ref[i], s):
            cp.wait()
        wgb_ref[...] = wgf_ref[s].astype(BF16)
        wub_ref[...] = wuf_ref[s].astype(BF16)
        wdb_ref[...] = wdf_ref[s].astype(BF16)

        @pl.when(nxt_ref[i] >= 0)
        def _():
            for cp in fetch(nxt_ref[i], 1 - s):
                cp.start()

    @pl.when(i < nu_ref[0])
    def _():
        half = xs_ref.shape[1]
        sub = xs_ref.shape[0] // EXPERT_SUBBLOCKS
        acts = []
        for r in range(EXPERT_SUBBLOCKS):
            rows = pl.ds(r * sub, sub)
            live = (_iota2((sub, 1), 0) + r * sub) < nv_ref[i]
            xa, xb = _unpack_pairs(jnp.where(live, xs_ref[rows, :], jnp.uint32(0)))
            xa = xa.astype(BF16)
            xb = xb.astype(BF16)
            gate = _dot(xa, wgb_ref[:half, :]) + _dot(xb, wgb_ref[half:, :])
            up = _dot(xa, wub_ref[:half, :]) + _dot(xb, wub_ref[half:, :])
            acts.append((gate, up))
        outs = [_dot((_silu(gate) * up).astype(BF16), wdb_ref[...]) for gate, up in acts]
        for r, y in enumerate(outs):
            ys_ref[pl.ds(r * sub, sub), :] = _pack_pairs(y)


def _experts(block_e, n_used, n_valid, xs, wg, wu, wd, layer):
    rows, half = xs.shape
    d = 2 * half
    nb = rows // EXPERT_BLOCK
    pos = jnp.arange(nb, dtype=jnp.int32)
    first = jnp.concatenate([jnp.ones((1,), jnp.int32), (block_e[1:] != block_e[:-1]).astype(jnp.int32)])
    slot = (jnp.cumsum(first) - 1) % 2
    later = (pos[None, :] > pos[:, None]) & (block_e[None, :] != block_e[:, None]) & (pos[None, :] < n_used[0])
    nxt_pos = jnp.min(jnp.where(later, pos[None, :], nb), axis=1)
    nxt = jnp.where(nxt_pos < nb, block_e[jnp.minimum(nxt_pos, nb - 1)], -1)
    blk = lambda i, be, nu, *rest: (jnp.minimum(i, nu[0] - 1), 0)
    hbm = pl.BlockSpec(memory_space=pl.ANY)
    return pl.pallas_call(
        functools.partial(_experts_kernel, layer=layer),
        grid_spec=pltpu.PrefetchScalarGridSpec(
            num_scalar_prefetch=6,
            grid=(nb,),
            in_specs=[pl.BlockSpec((EXPERT_BLOCK, half), blk), hbm, hbm, hbm],
            out_specs=pl.BlockSpec((EXPERT_BLOCK, half), blk),
            scratch_shapes=[pltpu.VMEM((2, d, D_EXPERT), F32), pltpu.VMEM((2, d, D_EXPERT), F32),
                            pltpu.VMEM((2, D_EXPERT, d), F32),
                            pltpu.VMEM((d, D_EXPERT), BF16), pltpu.VMEM((d, D_EXPERT), BF16),
                            pltpu.VMEM((D_EXPERT, d), BF16), pltpu.SemaphoreType.DMA((2,))],
        ),
        out_shape=jax.ShapeDtypeStruct((rows, half), jnp.uint32),
        compiler_params=_cparams(("arbitrary",)),
        name="moe_experts",
    )(block_e, n_used, n_valid, first, slot.astype(jnp.int32), nxt.astype(jnp.int32), xs, wg, wu, wd)


def _sc_gather_rows(table, idx, chunk=SC_CHUNK):
    n = idx.shape[0]
    width = table.shape[1]
    info = plsc.get_sparse_core_info()
    ncores, nsub = info.num_cores, info.num_subcores
    per_worker = n // (ncores * nsub)
    nchunk = per_worker // chunk
    mesh = plsc.VectorSubcoreMesh(core_axis_name="c", subcore_axis_name="s")

    @functools.partial(
        pl.kernel, mesh=mesh,
        out_type=jax.ShapeDtypeStruct((n, width), table.dtype),
        scratch_types=[pltpu.VMEM((nchunk, chunk), jnp.int32), pltpu.VMEM((2, chunk, width), table.dtype),
                       pltpu.SemaphoreType.DMA((2,)), pltpu.SemaphoreType.DMA((2,))],
    )
    def gather(table_hbm, idx_hbm, out_hbm, idx_v, rows_v, gsem, wsem):
        wid = lax.axis_index("s") * ncores + lax.axis_index("c")
        base = wid * per_worker
        pltpu.sync_copy(idx_hbm.at[pl.ds(wid * nchunk, nchunk)], idx_v)

        def fetch(j, b):
            return pltpu.make_async_copy(table_hbm.at[idx_v.at[j]], rows_v.at[b], gsem.at[b])

        def flush(j, b):
            off = pl.multiple_of(base + j * chunk, chunk)
            return pltpu.make_async_copy(rows_v.at[b], out_hbm.at[pl.ds(off, chunk)], wsem.at[b])

        fetch(0, 0).start()

        @pl.loop(0, nchunk, step=2)
        def _(i):
            for b in range(2):
                j = i + b
                fetch(j, b).wait()

                @pl.when(j + 1 < nchunk)
                def _():
                    @pl.when(j >= 1)
                    def _():
                        flush(j - 1, 1 - b).wait()

                    fetch(j + 1, 1 - b).start()

                flush(j, b).start()

        flush(nchunk - 2, 0).wait()
        flush(nchunk - 1, 1).wait()

    return gather(table, idx.reshape(n // chunk, chunk))


def _combine_kernel(x_ref, gate_ref, rows_ref, sg_ref, su_ref, sd_ref, g_ref, b_ref, o_ref):
    x = x_ref[...]
    xb = x.astype(BF16)
    hs = _silu(_dot(xb, sg_ref[...])) * _dot(xb, su_ref[...])
    ff = _dot(hs.astype(BF16), sd_ref[...])
    gate = gate_ref[...]
    half = rows_ref.shape[2]
    ya = ff[:, :half]
    yb = ff[:, half:]
    for s in range(TOP_K):
        a, b = _unpack_pairs(rows_ref[s])
        ya = ya + gate[:, s:s + 1] * a
        yb = yb + gate[:, s:s + 1] * b
    ff = jnp.concatenate([ya, yb], axis=1)
    o_ref[...] = _layer_norm(DN_ALPHA * x + ff, g_ref[...], b_ref[...])


def _combine_alias_kernel(prev_ref, *refs):
    del prev_ref
    _combine_kernel(*refs)


def _combine(x, gate_t, rows, sg, su, sd, g, b, part, nparts, prev=None, tm=512):
    t, d = x.shape
    tp = t // nparts
    tm = min(tm, tp)
    first = part * (tp // tm)
    row = lambda i: (first + i, 0)
    fix = lambda i: (0, 0)
    in_specs = [pl.BlockSpec((tm, d), row), pl.BlockSpec((tm, TOP_K), row),
                pl.BlockSpec((TOP_K, tm, d // 2), lambda i: (0, i, 0)),
                pl.BlockSpec(sg.shape, fix), pl.BlockSpec(su.shape, fix), pl.BlockSpec(sd.shape, fix),
                pl.BlockSpec((1, d), fix), pl.BlockSpec((1, d), fix)]
    args = (x, gate_t, rows, sg, su, sd, g, b)
    if prev is None:
        body, aliases = _combine_kernel, {}
    else:
        body, aliases = _combine_alias_kernel, {0: 0}
        in_specs = [pl.BlockSpec(memory_space=pl.ANY)] + in_specs
        args = (prev,) + args
    return pl.pallas_call(
        body,
        grid=(tp // tm,),
        in_specs=in_specs,
        out_specs=pl.BlockSpec((tm, d), row),
        out_shape=jax.ShapeDtypeStruct((t, d), F32),
        input_output_aliases=aliases,
        compiler_params=_cparams(("arbitrary",)),
        name="moe_combine",
    )(*args)


def _take_cols(w, idx):
    idx = np.asarray(idx)
    runs, start = [], 0
    for pos in range(1, len(idx) + 1):
        run_ends = pos == len(idx) or (idx[pos] != idx[pos - 1] + 1 if idx[pos - 1] >= 0 else idx[pos] >= 0)
        if run_ends:
            runs.append((start, int(idx[start]), pos - start))
            start = pos

    def body(w_ref, o_ref):
        for dst, src, width in runs:
            if src < 0:
                o_ref[:, dst:dst + width] = jnp.zeros((o_ref.shape[0], width), o_ref.dtype)
            else:
                o_ref[:, dst:dst + width] = w_ref[:, src:src + width].astype(o_ref.dtype)

    rows = w.shape[0]
    tr = min(rows, 256)
    return pl.pallas_call(
        body,
        grid=(rows // tr,),
        in_specs=[pl.BlockSpec((tr, w.shape[1]), lambda i: (i, 0))],
        out_specs=pl.BlockSpec((tr, len(idx)), lambda i: (i, 0)),
        out_shape=jax.ShapeDtypeStruct((rows, len(idx)), BF16),
        compiler_params=_cparams(("arbitrary",)),
        name="weight_cols",
    )(w)


def _pad_lane_row(v, first_lane, width=LANES):
    out = jnp.zeros((1, width), F32)
    return lax.dynamic_update_slice(out, v.reshape(1, -1).astype(F32), (0, first_lane))


def _even_in_cols():
    z = lambda n: -np.ones(n, int)
    kr0 = Q_LORA + KV_LORA
    half = MLA_ROPE // 2
    cols = [np.arange(0, Q_LORA), np.arange(Q_LORA, Q_LORA + KV_LORA),
            z(64), np.arange(kr0, kr0 + MLA_ROPE), z(32),
            z(64), np.arange(kr0 + half, kr0 + MLA_ROPE), np.arange(kr0, kr0 + half), z(32)]
    g0 = kr0 + MLA_ROPE
    nqk = GDN_H * GDN_DK
    cols.append(np.arange(g0, g0 + 3 * nqk))
    zoff = g0 + 3 * nqk + 2 * GDN_H
    cols.append(np.arange(zoff, zoff + GDN_H * GDN_DV))
    cols += [np.arange(g0 + 3 * nqk, g0 + 3 * nqk + 2 * GDN_H), z(LANES - 2 * GDN_H)]
    return np.concatenate(cols)


EV_WIDTHS = (Q_LORA + KV_LORA + 2 * LANES, 3 * GDN_H * GDN_DK, GDN_H * GDN_DV, LANES)


def _mla_q_cols():
    per = MLA_NOPE + MLA_ROPE
    half = MLA_ROPE // 2
    main, sw = [], []
    for h in range(MLA_H):
        b = h * per
        main += [np.arange(b, b + per), -np.ones(LANES - per, int)]
        sw += [-np.ones(MLA_NOPE, int), np.arange(b + MLA_NOPE + half, b + per), np.arange(b + MLA_NOPE, b + MLA_NOPE + half),
               -np.ones(LANES - per, int)]
    return np.concatenate(main + sw)


def _mla_kv_cols():
    per = MLA_NOPE + MLA_V
    kc, vc = [], []
    for h in range(MLA_H):
        b = h * per
        kc += [np.arange(b, b + MLA_NOPE), -np.ones(LANES - MLA_NOPE, int)]
        vv = np.arange(b + MLA_NOPE, b + per)
        pad = -np.ones(LANES - MLA_V, int)
        vc += [vv, pad] if h % 2 == 0 else [pad, vv]
    return np.concatenate(kc + vc)


def _odd_in_cols():
    z = lambda n: -np.ones(n, int)
    o = 0
    cols = []
    mq0, mk0 = 0, ML_H * ML_DK
    for base in (mq0, mk0):
        for h in range(ML_H):
            cols += [np.arange(base + h * ML_DK, base + (h + 1) * ML_DK), z(LANES - ML_DK)]
    mv0 = 2 * ML_H * ML_DK
    cols.append(np.arange(mv0, mv0 + ML_H * ML_DV))
    mi0 = mv0 + ML_H * ML_DV
    mo0 = mi0 + 2 * ML_H
    cols.append(np.arange(mo0, mo0 + ML_H * ML_DV))
    cols += [np.arange(mi0, mi0 + 2 * ML_H), z(LANES - 2 * ML_H)]
    sq0 = mo0 + ML_H * ML_DV
    sk0 = sq0 + SWA_H * SWA_D
    sv0 = sk0 + SWA_KV * SWA_D
    half = SWA_D // 2

    def heads(base, n, swapped, copies):
        out = []
        for h in range(n):
            b = base + h * SWA_D
            one = [np.arange(b + half, b + SWA_D), np.arange(b, b + half)] if swapped else [np.arange(b, b + SWA_D)]
            out += one * copies
        return out

    cols += (heads(sq0, SWA_H, False, 1) + heads(sq0, SWA_H, True, 1)
             + heads(sk0, SWA_KV, False, 2) + heads(sk0, SWA_KV, True, 2))
    for g in range(SWA_KV):
        vv = np.arange(sv0 + g * SWA_D, sv0 + (g + 1) * SWA_D)
        cols += [vv, z(LANES - SWA_D), z(LANES - SWA_D), vv]
    return np.concatenate(cols)


def _even_mixer(x, tabs, w_in, q_norm, w_qb, kv_norm, w_kvb, conv_w, a_log, dt_bias, o_norm, batch, seq):
    ctab, stab = tabs
    w = _take_cols(w_in, _even_in_cols()).astype(BF16)
    mla_in, act, z, gates = _proj_even(x, w, conv_w, seq)
    wq2 = _take_cols(w_qb, _mla_q_cols()).astype(BF16)
    wkv2 = _take_cols(w_kvb, _mla_kv_cols()).astype(BF16)
    q, k, v = _mla_prep(mla_in, ctab, stab, q_norm.reshape(1, -1), kv_norm.reshape(1, -1), wq2, wkv2)
    o_a = _mla_attn(q, k, v, batch, seq)
    o_b = _gdn(act, gates, z, _pad_lane_row(a_log, GDN_H), _pad_lane_row(dt_bias, GDN_H),
               o_norm.reshape(1, -1), batch, seq)
    return o_a, o_b


def _odd_mixer(x, tabs, w_in, b_i, b_f, ml_norm, sinks, batch, seq):
    ctab, stab = tabs
    w = _take_cols(w_in, _odd_in_cols()).astype(BF16)
    mq, mk, mv, mo, mg, sq, sk, sv = _proj_odd(x, w, ctab, stab)
    bias_row = _pad_lane_row(jnp.concatenate([b_i, b_f]), 0)
    o_c = _mlstm(mq, mk, mv, mo, mg, bias_row, ml_norm.reshape(1, -1), batch, seq)
    o_d = _swa(sq, sk, sv, _pad_lane_row(sinks, 0), batch, seq)
    return o_c, o_d


def _moe(x, xp, router_w, router_b, w_gate, w_up, w_down, layer, s_gate, s_up, s_down, ln_g, ln_b):
    t, d = x.shape
    bias_col = jnp.broadcast_to(router_b.reshape(-1, 1).astype(F32), (N_EXPERTS, LANES))
    idx, gate, rank, cnt = _router(x, router_w.T, bias_col)
    counts = cnt[:, 0].astype(jnp.int32)
    padded = (counts + EXPERT_BLOCK - 1) // EXPERT_BLOCK * EXPERT_BLOCK
    pad_end = jnp.cumsum(padded)
    pad_start = pad_end - padded
    start_col = jnp.broadcast_to(pad_start.astype(F32).reshape(-1, 1), (N_EXPERTS, LANES))
    dest = _dest_rows(idx, rank, start_col)
    n_blocks = t * TOP_K // EXPERT_BLOCK + N_EXPERTS
    rows = n_blocks * EXPERT_BLOCK
    block_row = jnp.arange(n_blocks, dtype=jnp.int32) * EXPERT_BLOCK
    block_e = jnp.minimum(jnp.sum((pad_end[None, :] <= block_row[:, None]).astype(jnp.int32), axis=1), N_EXPERTS - 1)
    n_used = (pad_end[-1:] // EXPERT_BLOCK).astype(jnp.int32)
    live_end = jnp.sum(jnp.where(block_e[:, None] == jnp.arange(N_EXPERTS, dtype=jnp.int32)[None, :],
                                 (pad_start + counts)[None, :], 0), axis=1)
    n_valid = jnp.clip(live_end - block_row, 0, EXPERT_BLOCK).astype(jnp.int32)
    xs = _sc_scatter_rows(xp, dest, rows)
    ys = _experts(block_e, n_used, n_valid, xs, w_gate, w_up, w_down, layer)
    nparts = COMBINE_PARTS if t % (COMBINE_PARTS * 512) == 0 else 1
    tp = t // nparts
    gate_t = gate.T
    sgb, sub, sdb = s_gate.astype(BF16), s_up.astype(BF16), s_down.astype(BF16)
    out = None
    for part in range(nparts):
        idx_p = dest[:, part * tp:(part + 1) * tp].reshape(-1)
        picked = _sc_gather_rows(ys, idx_p).reshape(TOP_K, tp, d // 2)
        out = _combine(x, gate_t, picked, sgb, sub, sdb, ln_g.reshape(1, -1), ln_b.reshape(1, -1), part, nparts, prev=out)
    return out


def kernel(x, positions, ev_w_in, mla_q_norm, mla_w_qb, mla_kv_norm, mla_w_kvb, gdn_conv, gdn_a_log, gdn_dt_bias, gdn_norm, ev_w_out, od_w_in, mlstm_b_i, mlstm_b_f, mlstm_norm, swa_sinks, od_w_out, ln1_g, ln1_b, router_w, router_b, moe_w_gate, moe_w_up, moe_w_down, shared_w_gate, shared_w_up, shared_w_down, ln2_g, ln2_b):
    batch, seq, d = x.shape
    t = batch * seq
    pos = positions.reshape(t, 1).astype(F32)
    tabs_m = _rope_tables(pos, _rope_rows(MLA_ROPE, MLA_NOPE, MLA_NOPE))
    tabs_s = _rope_tables(pos, _rope_rows(SWA_D, 0, 0, heads=LANES // SWA_D))
    h = x.reshape(t, d)
    for layer in range(DEPTH):
        j = layer // 2
        if layer % 2 == 0:
            a1, a2 = _even_mixer(h, tabs_m, ev_w_in[j], mla_q_norm[j], mla_w_qb[j], mla_kv_norm[j], mla_w_kvb[j],
                                 gdn_conv[j], gdn_a_log[j], gdn_dt_bias[j], gdn_norm[j], batch, seq)
            w_out = ev_w_out[j]
        else:
            a1, a2 = _odd_mixer(h, tabs_s, od_w_in[j], mlstm_b_i[j], mlstm_b_f[j], mlstm_norm[j], swa_sinks[j], batch, seq)
            w_out = od_w_out[j]
        h, hp = _outproj_ln(h, a1, a2, w_out.astype(BF16), ln1_g[layer].reshape(1, -1), ln1_b[layer].reshape(1, -1))
        h = _moe(h, hp, router_w[layer], router_b[layer], moe_w_gate, moe_w_up, moe_w_down, layer,
                 shared_w_gate[layer], shared_w_up[layer], shared_w_down[layer], ln2_g[layer], ln2_b[layer])
    return h.reshape(batch, seq, d)
```

```python
import functools
import math

import numpy as np
import jax
import jax.numpy as jnp
from jax import lax
from jax.experimental import pallas as pl
from jax.experimental.pallas import tpu as pltpu
from jax.experimental.pallas import tpu_sc as plsc

F32 = jnp.float32
BF16 = jnp.bfloat16
HI = lax.Precision.HIGHEST

D_MODEL = 1024
DEPTH = 4
ROPE_THETA = 10000.0
EPS = 1e-6
LN_EPS = 1e-5
MLA_H, MLA_NOPE, MLA_ROPE, MLA_V = 8, 64, 32, 64
Q_LORA, KV_LORA = 256, 128
GDN_H, GDN_DK, GDN_DV, CONV_W, GDN_CHUNK = 4, 128, 128, 4, 64
ML_H, ML_DK, ML_DV, ML_CHUNK = 4, 64, 128, 64
SWA_H, SWA_KV, SWA_D, WINDOW = 8, 2, 64, 128
N_EXPERTS, N_GROUPS, TOPK_GROUPS, TOP_K = 64, 8, 4, 8
D_EXPERT, D_SHARED = 256, 256
ROUTED_SCALE = 2.5
DN_ALPHA = (2 * DEPTH) ** 0.25

LANES = 128
V7X_VMEM_BYTES = 64 * 1024 * 1024
VMEM_LIMIT = 48 * 1024 * 1024

EXPERT_BLOCK = 512
EXPERT_SUBBLOCKS = 2
COMBINE_PARTS = 1
SWA_SEQS_PER_STEP = 4
MLSTM_SEQS_PER_STEP = 2
GDN_SEQS_PER_STEP = 8
SC_CHUNK = 64


def _cparams(sem, vmem=VMEM_LIMIT):
    return pltpu.CompilerParams(dimension_semantics=sem, vmem_limit_bytes=vmem)


def _dot(a, b, precision=None):
    return jnp.dot(a, b, preferred_element_type=F32, precision=precision)


def _dot_nt(a, b, precision=None):
    return lax.dot_general(a, b, (((1,), (1,)), ((), ())), preferred_element_type=F32, precision=precision)


def _dot_tn(a, b, precision=None):
    return lax.dot_general(a, b, (((0,), (0,)), ((), ())), preferred_element_type=F32, precision=precision)


def _split2(a):
    hi = a.astype(BF16)
    lo = (a - hi.astype(F32)).astype(BF16)
    return hi, lo


def _split3(a):
    p1 = a.astype(BF16)
    r = a - p1.astype(F32)
    p2 = r.astype(BF16)
    p3 = (r - p2.astype(F32)).astype(BF16)
    return p1, p2, p3


def _dot3(a, b, dot=_dot):
    ah, al = _split2(a)
    bh, bl = _split2(b)
    return dot(ah, bh) + (dot(ah, bl) + dot(al, bh))


def _dot_sel(sel, b, dot=_dot):
    sel = sel.astype(BF16)
    p1, p2, p3 = _split3(b)
    return dot(sel, p1) + (dot(sel, p2) + dot(sel, p3))


def _sigmoid(x):
    return 1.0 / (1.0 + jnp.exp(-x))


def _softplus(x):
    return jnp.maximum(x, 0.0) + jnp.log(1.0 + jnp.exp(-jnp.abs(x)))


def _silu(x):
    return x * _sigmoid(x)


def _lane_bcast(x, c):
    return jnp.broadcast_to(x[:, c:c + 1], x.shape)


def _iota2(shape, dim):
    return lax.broadcasted_iota(jnp.int32, shape, dim)


def _rope_kernel(pos_ref, rows_ref, c_ref, s_ref):
    ang = pos_ref[...] * rows_ref[0:1, :]
    c_ref[...] = rows_ref[1:2, :] * jnp.cos(ang) + rows_ref[2:3, :]
    s_ref[...] = rows_ref[3:4, :] * jnp.sin(ang)


def _rope_tables(pos, rows, tm=512):
    t = pos.shape[0]
    return pl.pallas_call(
        _rope_kernel,
        grid=(t // tm,),
        in_specs=[pl.BlockSpec((tm, 1), lambda i: (i, 0)), pl.BlockSpec((8, LANES), lambda i: (0, 0))],
        out_specs=[pl.BlockSpec((tm, LANES), lambda i: (i, 0))] * 2,
        out_shape=[jax.ShapeDtypeStruct((t, LANES), F32)] * 2,
        compiler_params=_cparams(("arbitrary",)),
        name="rope_tables",
    )(pos, rows)


def _rope_rows(dim, first_lane, pad_one_lanes, heads=1):
    half = dim // 2
    inv = ROPE_THETA ** (-(np.arange(0, dim, 2, dtype=np.float32) / dim))
    rows = np.zeros((8, LANES), np.float32)
    for h in range(heads):
        lo = slice(first_lane + h * dim, first_lane + h * dim + half)
        hi = slice(first_lane + h * dim + half, first_lane + (h + 1) * dim)
        rows[0, lo] = inv
        rows[0, hi] = inv
        rows[1, lo] = 1.0
        rows[1, hi] = 1.0
        rows[3, lo] = -1.0
        rows[3, hi] = 1.0
    rows[2, :pad_one_lanes] = 1.0
    return jnp.asarray(rows)


def _proj_kernel(x_ref, w_ref, *out_refs, offsets):
    xb = x_ref[...].astype(BF16)
    for o_ref, (a, b) in zip(out_refs, offsets):
        o_ref[...] = _dot(xb, w_ref[:, a:b]).astype(o_ref.dtype)


def _proj(x, w, widths, dtypes, tm=512):
    t, k = x.shape
    offs = np.concatenate([[0], np.cumsum(widths)]).tolist()
    offsets = tuple((offs[i], offs[i + 1]) for i in range(len(widths)))
    return pl.pallas_call(
        functools.partial(_proj_kernel, offsets=offsets),
        grid=(t // tm,),
        in_specs=[pl.BlockSpec((tm, k), lambda i: (i, 0)), pl.BlockSpec(w.shape, lambda i: (0, 0))],
        out_specs=[pl.BlockSpec((tm, n), lambda i: (i, 0)) for n in widths],
        out_shape=[jax.ShapeDtypeStruct((t, n), dt) for n, dt in zip(widths, dtypes)],
        compiler_params=_cparams(("arbitrary",)),
        name="in_proj",
    )(x, w)


def _proj_even_kernel(x_ref, w_ref, cw_ref, mla_ref, act_ref, z_ref, g_ref, ext_ref, *, tiles_per_seq):
    tm = x_ref.shape[0]
    o = np.concatenate([[0], np.cumsum(EV_WIDTHS)]).tolist()
    @pl.when(pl.program_id(0) % tiles_per_seq == 0)
    def _():
        ext_ref[0:8, :] = jnp.zeros((8, ext_ref.shape[1]), F32)

    xb = x_ref[...].astype(BF16)
    nchunk = 3
    cw = EV_WIDTHS[1] // nchunk

    def project(ci):
        ext_ref[8:8 + tm, ci * cw:(ci + 1) * cw] = _dot(xb, w_ref[:, o[1] + ci * cw:o[1] + (ci + 1) * cw])

    project(0)
    for ci in range(nchunk):
        if ci + 1 < nchunk:
            project(ci + 1)
        else:
            mla_ref[...] = _dot(xb, w_ref[:, o[0]:o[1]])
            z_ref[...] = _dot(xb, w_ref[:, o[2]:o[3]]).astype(z_ref.dtype)
            g_ref[...] = _dot(xb, w_ref[:, o[3]:o[4]])
        cols = slice(ci * cw, (ci + 1) * cw)
        conv = cw_ref[0:1, cols] * ext_ref[5:5 + tm, cols]
        for j in range(1, CONV_W):
            conv = conv + cw_ref[j:j + 1, cols] * ext_ref[5 + j:5 + j + tm, cols]
        act_ref[:, cols] = _silu(conv).astype(act_ref.dtype)
    ext_ref[0:8, :] = ext_ref[tm:tm + 8, :]


def _proj_even(x, w, conv_w, seq, tm=512):
    t, k = x.shape
    tm = min(tm, seq)
    row = lambda i: (i, 0)
    fix = lambda i: (0, 0)
    return pl.pallas_call(
        functools.partial(_proj_even_kernel, tiles_per_seq=seq // tm),
        grid=(t // tm,),
        in_specs=[pl.BlockSpec((tm, k), row), pl.BlockSpec(w.shape, fix), pl.BlockSpec(conv_w.shape, fix)],
        out_specs=[pl.BlockSpec((tm, n), row) for n in EV_WIDTHS],
        out_shape=[jax.ShapeDtypeStruct((t, n), dt) for n, dt in zip(EV_WIDTHS, (F32, BF16, BF16, F32))],
        scratch_shapes=[pltpu.VMEM((tm + 8, EV_WIDTHS[1]), F32)],
        compiler_params=_cparams(("arbitrary",)),
        name="in_proj",
    )(x, w, conv_w)


OD_SEG = dict(mq=(0, 512), mk=(512, 1024), mv=(1024, 1536), mo=(1536, 2048), gates=(2048, 2176),
              sq=(2176, 2688), sqsw=(2688, 3200), sk=(3200, 3456), sksw=(3456, 3712), sv=(3712, 4224))
OD_COLS = 4224


def _proj_odd_kernel(x_ref, w_ref, c_ref, s_ref, mq_ref, mk_ref, mv_ref, mo_ref, mg_ref, sq_ref, sk_ref, sv_ref):
    xb = x_ref[...].astype(BF16)

    def seg(name):
        a, b = OD_SEG[name]
        return _dot(xb, w_ref[:, a:b])

    mq_ref[...] = seg("mq").astype(mq_ref.dtype)
    mk_ref[...] = seg("mk").astype(mk_ref.dtype)
    mv_ref[...] = seg("mv").astype(mv_ref.dtype)
    mo_ref[...] = seg("mo").astype(mo_ref.dtype)
    mg_ref[...] = seg("gates")
    c = c_ref[...]
    s = s_ref[...]
    c8 = jnp.concatenate([c] * (SWA_H // 2), axis=1)
    s8 = jnp.concatenate([s] * (SWA_H // 2), axis=1)
    sq_ref[...] = (seg("sq") * c8 + seg("sqsw") * s8).astype(sq_ref.dtype)
    c2 = jnp.concatenate([c] * SWA_KV, axis=1)
    s2 = jnp.concatenate([s] * SWA_KV, axis=1)
    sk_ref[...] = (seg("sk") * c2 + seg("sksw") * s2).astype(sk_ref.dtype)
    sv_ref[...] = seg("sv").astype(sv_ref.dtype)


def _proj_odd(x, w, ctab, stab, tm=256):
    t, k = x.shape
    widths = (512, 512, 512, 512, 128, SWA_H * SWA_D, SWA_KV * LANES, 2 * SWA_KV * LANES)
    dtypes = (BF16, BF16, BF16, BF16, F32, BF16, BF16, BF16)
    return pl.pallas_call(
        _proj_odd_kernel,
        grid=(t // tm,),
        in_specs=[pl.BlockSpec((tm, k), lambda i: (i, 0)), pl.BlockSpec(w.shape, lambda i: (0, 0)),
                  pl.BlockSpec((tm, LANES), lambda i: (i, 0)), pl.BlockSpec((tm, LANES), lambda i: (i, 0))],
        out_specs=[pl.BlockSpec((tm, n), lambda i: (i, 0)) for n in widths],
        out_shape=[jax.ShapeDtypeStruct((t, n), dt) for n, dt in zip(widths, dtypes)],
        compiler_params=_cparams(("arbitrary",)),
        name="in_proj_odd",
    )(x, w, ctab, stab)


def _rms(x, g):
    return x * lax.rsqrt(jnp.mean(x * x, axis=-1, keepdims=True) + EPS) * g


def _mla_prep_kernel(in_ref, c_ref, s_ref, qn_ref, kvn_ref, wq_ref, wkv_ref, q_ref, k_ref, v_ref):
    hw = MLA_H * LANES
    c = c_ref[...]
    s = s_ref[...]
    c8 = jnp.concatenate([c] * MLA_H, axis=1)
    s8 = jnp.concatenate([s] * MLA_H, axis=1)
    cqn = _rms(in_ref[:, 0:Q_LORA], qn_ref[...]).astype(BF16)
    qq = _dot(cqn, wq_ref[...])
    scale = (MLA_NOPE + MLA_ROPE) ** -0.5
    q_ref[...] = ((qq[:, :hw] * c8 + qq[:, hw:] * s8) * scale).astype(q_ref.dtype)
    ckvn = _rms(in_ref[:, Q_LORA:Q_LORA + KV_LORA], kvn_ref[...]).astype(BF16)
    kv = _dot(ckvn, wkv_ref[...])
    o = Q_LORA + KV_LORA
    krr = in_ref[:, o:o + LANES] * c + in_ref[:, o + LANES:o + 2 * LANES] * s
    k_ref[...] = (kv[:, :hw] + jnp.concatenate([krr] * MLA_H, axis=1)).astype(k_ref.dtype)
    v_ref[...] = kv[:, hw:].astype(v_ref.dtype)


def _mla_prep(mla_in, ctab, stab, qn, kvn, wq2, wkv2, tm=512):
    t = mla_in.shape[0]
    hw = MLA_H * LANES
    row = lambda i: (i, 0)
    fix = lambda i: (0, 0)
    return pl.pallas_call(
        _mla_prep_kernel,
        grid=(t // tm,),
        in_specs=[pl.BlockSpec((tm, mla_in.shape[1]), row), pl.BlockSpec((tm, LANES), row), pl.BlockSpec((tm, LANES), row),
                  pl.BlockSpec(qn.shape, fix), pl.BlockSpec(kvn.shape, fix),
                  pl.BlockSpec(wq2.shape, fix), pl.BlockSpec(wkv2.shape, fix)],
        out_specs=[pl.BlockSpec((tm, hw), row)] * 3,
        out_shape=[jax.ShapeDtypeStruct((t, hw), BF16)] * 3,
        compiler_params=_cparams(("arbitrary",)),
        name="mla_prep",
    )(mla_in, ctab, stab, qn, kvn, wq2, wkv2)


def _mla_attn_kernel(q_ref, k_ref, v_ref, o_ref, *, tq):
    i = pl.program_id(2)
    neg = -1e30
    lane = _iota2((tq, LANES), 1)
    ones_lane = (MLA_V, 0)

    def chunk(j, carry, masked):
        start = pl.multiple_of(j * tq, tq)
        out = []
        for hh in range(2):
            m, acc = carry[hh]
            q = q_ref[:, hh * LANES:(hh + 1) * LANES]
            kc = k_ref[pl.ds(start, tq), hh * LANES:(hh + 1) * LANES]
            vc = v_ref[pl.ds(start, tq), hh * LANES:(hh + 1) * LANES]
            vc = jnp.where(lane == ones_lane[hh], jnp.ones_like(vc), vc)
            s = _dot_nt(q, kc)
            if masked:
                s = jnp.where(_iota2(s.shape, 0) >= _iota2(s.shape, 1), s, neg)
            m_new = jnp.maximum(m, jnp.max(s, axis=-1, keepdims=True))
            alpha = jnp.exp(m - m_new)
            p = jnp.exp(s - m_new)
            acc = alpha * acc + _dot(p.astype(BF16), vc)
            out.append((m_new, acc))
        return tuple(out)

    one = (jnp.full((tq, 1), neg, F32), jnp.zeros((tq, LANES), F32))
    carry = lax.fori_loop(0, i, lambda j, c: chunk(j, c, False), (one, one))
    (_, acc0), (_, acc1) = chunk(i, carry, True)
    o0 = acc0 / _lane_bcast(acc0, ones_lane[0])
    o1 = acc1 / _lane_bcast(acc1, ones_lane[1])
    o_ref[...] = jnp.where(lane < MLA_V, o0, o1).astype(o_ref.dtype)


def _mla_attn(q, k, v, batch, seq, tq=512):
    tq = min(tq, seq)
    nq = seq // tq
    pairs = MLA_H // 2
    return pl.pallas_call(
        functools.partial(_mla_attn_kernel, tq=tq),
        grid=(batch, pairs, nq),
        in_specs=[pl.BlockSpec((tq, 2 * LANES), lambda b, p, i: (b * nq + i, p)),
                  pl.BlockSpec((seq, 2 * LANES), lambda b, p, i: (b, p)),
                  pl.BlockSpec((seq, 2 * LANES), lambda b, p, i: (b, p))],
        out_specs=pl.BlockSpec((tq, LANES), lambda b, p, i: (b * nq + i, p)),
        out_shape=jax.ShapeDtypeStruct((batch * seq, pairs * LANES), BF16),
        compiler_params=_cparams(("arbitrary", "arbitrary", "arbitrary")),
        name="mla_attn",
    )(q, k, v)


def _unit_lower_inverse_many(ns):
    c = ns[0].shape[0]
    eye = (_iota2((c, c), 0) == _iota2((c, c), 1)).astype(F32)
    xs = [-n for n in ns]
    ps = [eye + x for x in xs]
    xb = [x.astype(BF16) for x in xs]
    for _ in range(int(math.log2(c)) - 1):
        xs = [_dot(b, b) for b in xb]
        xb = [x.astype(BF16) for x in xs]
        ps = [p + _dot(p.astype(BF16), b) for p, b in zip(ps, xb)]
    return ps


def _gdn_kernel(act_ref, g_ref, z_ref, al_ref, dt_ref, on_ref, o_ref, st_ref):
    c = GDN_CHUNK
    hd = GDN_DK
    nqk = GDN_H * GDN_DK

    @pl.when(pl.program_id(1) == 0)
    def _():
        st_ref[...] = jnp.zeros(st_ref.shape, F32)

    tri = (_iota2((c, c), 0) >= _iota2((c, c), 1)).astype(F32)
    row_ge = _iota2((c, c), 0) >= _iota2((c, c), 1)
    row_gt = _iota2((c, c), 0) > _iota2((c, c), 1)
    lane = _iota2((c, LANES), 1)

    units = []
    for bb in range(act_ref.shape[0]):
        act = act_ref[bb].astype(F32)
        gates = g_ref[bb]
        beta_all = _sigmoid(gates)
        g_all = -jnp.exp(al_ref[...]) * _softplus(gates + dt_ref[...])
        gc_all = _dot_sel(tri, g_all)
        gc_parts = _split3(gc_all)
        for h in range(GDN_H):
            q = act[:, h * hd:(h + 1) * hd]
            k = act[:, nqk + h * hd:nqk + (h + 1) * hd]
            v = act[:, 2 * nqk + h * GDN_DV:2 * nqk + (h + 1) * GDN_DV]
            q = q * lax.rsqrt(jnp.sum(q * q, axis=-1, keepdims=True) + EPS) * (GDN_DK ** -0.5)
            k = k * lax.rsqrt(jnp.sum(k * k, axis=-1, keepdims=True) + EPS)
            beta = _lane_bcast(beta_all, h)
            gcol = _lane_bcast(gc_all, GDN_H + h)
            pick = (lane == GDN_H + h).astype(BF16)
            grow = _dot_nt(pick, gc_parts[0]) + (_dot_nt(pick, gc_parts[1]) + _dot_nt(pick, gc_parts[2]))
            decay = jnp.exp(jnp.where(row_ge, gcol[:, :c] - grow, -jnp.inf))
            kb = k * beta
            lower = jnp.where(row_gt, _dot3(kb, k, _dot_nt) * decay, 0.0)
            eg = jnp.exp(gcol)
            glast = gcol[c - 1:c, :]
            units.append(dict(bb=bb, h=h, lower=lower, rhs=jnp.concatenate([v * beta, kb * eg], axis=1),
                              attn=_dot_nt(q.astype(BF16), k.astype(BF16)) * decay, qg=(q * eg).astype(BF16),
                              kg=(k * jnp.exp(glast - gcol)).astype(BF16), gl=jnp.exp(glast)))

    tinvs = _unit_lower_inverse_many([u["lower"] for u in units])
    uws = []
    for u, tinv in zip(units, tinvs):
        uws.append(_dot(tinv.astype(BF16), u["rhs"].astype(BF16)))
    states = [st_ref[u["bb"], u["h"]] for u in units]
    sbs = [s.astype(BF16) for s in states]
    vnews = [(uw[:, :GDN_DV] - _dot(uw[:, GDN_DV:].astype(BF16), sb)).astype(BF16) for uw, sb in zip(uws, sbs)]
    for u, state, sb, vnb in zip(units, states, sbs, vnews):
        bb, h = u["bb"], u["h"]
        o = _dot(u["qg"], sb) + _dot(u["attn"].astype(BF16), vnb)
        st_ref[bb, h] = state * u["gl"] + _dot_tn(u["kg"], vnb)
        o = _rms(o, on_ref[...]) * _silu(z_ref[bb, :, h * GDN_DV:(h + 1) * GDN_DV].astype(F32))
        o_ref[bb, :, h * GDN_DV:(h + 1) * GDN_DV] = o.astype(o_ref.dtype)


def _gdn(act, gates, z, a_row, dt_row, o_norm, batch, seq):
    c = GDN_CHUNK
    nc = seq // c
    w3 = act.shape[1]
    wo = GDN_H * GDN_DV
    nb = min(GDN_SEQS_PER_STEP, batch)
    row = lambda b, i: (b, i, 0)
    fix = lambda b, i: (0, 0)
    out = pl.pallas_call(
        _gdn_kernel,
        grid=(batch // nb, nc),
        in_specs=[pl.BlockSpec((nb, c, w3), row), pl.BlockSpec((nb, c, LANES), row), pl.BlockSpec((nb, c, wo), row),
                  pl.BlockSpec((1, LANES), fix), pl.BlockSpec((1, LANES), fix), pl.BlockSpec((1, GDN_DV), fix)],
        out_specs=pl.BlockSpec((nb, c, wo), row),
        out_shape=jax.ShapeDtypeStruct((batch, seq, wo), BF16),
        scratch_shapes=[pltpu.VMEM((nb, GDN_H, GDN_DK, GDN_DV), F32)],
        compiler_params=_cparams(("arbitrary", "arbitrary")),
        name="gdn",
    )(act.reshape(batch, seq, w3), gates.reshape(batch, seq, LANES), z.reshape(batch, seq, wo), a_row, dt_row, o_norm)
    return out.reshape(batch * seq, wo)


def _mlstm_kernel(q_ref, k_ref, v_ref, og_ref, g_ref, bias_ref, nrm_ref, o_ref, c_ref, n_ref, m_ref):
    @pl.when(pl.program_id(1) == 0)
    def _():
        c_ref[...] = jnp.zeros(c_ref.shape, F32)
        n_ref[...] = jnp.zeros(n_ref.shape, F32)
        m_ref[...] = jnp.zeros(m_ref.shape, F32)

    c = ML_CHUNK
    tri = (_iota2((c, c), 0) >= _iota2((c, c), 1)).astype(F32)
    row_ge = _iota2((c, c), 0) >= _iota2((c, c), 1)
    ones = jnp.ones((c, LANES), F32)
    lane = _iota2((c, LANES), 1)

    units = []
    for bb in range(q_ref.shape[0]):
        pre = g_ref[bb] + bias_ref[...]
        logf = jnp.minimum(pre, 0.0) - jnp.log(1.0 + jnp.exp(-jnp.abs(pre)))
        bcum_all = _dot_sel(tri, logf)
        for h in range(ML_H):
            q = q_ref[bb, :, h * LANES:(h + 1) * LANES].astype(F32)
            k = k_ref[bb, :, h * LANES:(h + 1) * LANES].astype(F32) * (ML_DK ** -0.5)
            units.append(dict(bb=bb, h=h, q=q, k=k, qb=q.astype(BF16), vb=v_ref[bb, :, h * ML_DV:(h + 1) * ML_DV].astype(BF16),
                              bcol=_lane_bcast(bcum_all, ML_H + h),
                              icol=_lane_bcast(pre, h),
                              col=jnp.where(lane == h, pre, 0.0) - jnp.where(lane == ML_H + h, bcum_all, 0.0),
                              m_st=m_ref[bb, h], cst=c_ref[bb, h], nst=n_ref[bb, h]))
    for u in units:
        u["row"] = _dot_sel(ones, u["col"], _dot_nt)
        u["qk"] = _dot_nt(u["qb"], u["k"].astype(BF16))
        u["qc"] = _dot(u["qb"], u["cst"].astype(BF16))
    for u in units:
        d = jnp.where(row_ge, u["bcol"][:, :c] + u["row"], -jnp.inf)
        inter = u["bcol"] + u["m_st"]
        m_t = jnp.maximum(inter, jnp.max(d, axis=-1, keepdims=True))
        u["m_t"] = m_t
        u["w_inter"] = jnp.exp(inter - m_t)
        u["p"] = jnp.exp(d - m_t[:, :c]) * u["qk"]
        u["pv"] = _dot(u["p"].astype(BF16), u["vb"])
        b_end = u["bcol"][c - 1:c, :]
        a = b_end - u["bcol"] + u["icol"]
        m_new = jnp.maximum(b_end + u["m_st"], jnp.max(a, axis=0, keepdims=True))
        u["m_new"] = m_new
        u["keep"] = jnp.exp(b_end + u["m_st"] - m_new)
        u["ks"] = u["k"] * jnp.exp(a - m_new)
        u["kv"] = _dot_tn(u["ks"].astype(BF16), u["vb"])
    for u in units:
        bb, h = u["bb"], u["h"]
        num = u["w_inter"] * u["qc"] + u["pv"]
        den = (u["w_inter"] * jnp.sum(u["q"] * u["nst"], axis=-1, keepdims=True)
               + jnp.sum(u["p"], axis=-1, keepdims=True))
        hc = num / jnp.maximum(jnp.abs(den), jnp.exp(-u["m_t"]))
        c_ref[bb, h] = u["cst"] * u["keep"] + u["kv"]
        n_ref[bb, h] = u["nst"] * u["keep"] + jnp.sum(u["ks"], axis=0, keepdims=True)
        m_ref[bb, h] = u["m_new"]
        hn = (_rms(hc, nrm_ref[:, h * ML_DV:(h + 1) * ML_DV])
              * _sigmoid(og_ref[bb, :, h * ML_DV:(h + 1) * ML_DV].astype(F32)))
        o_ref[bb, :, h * ML_DV:(h + 1) * ML_DV] = hn.astype(o_ref.dtype)


def _mlstm(mq, mk, mv, mo, gates, bias_row, norm_row, batch, seq):
    c = ML_CHUNK
    nc = seq // c
    nb = min(MLSTM_SEQS_PER_STEP, batch)
    row = lambda b, i: (b, i, 0)
    fix = lambda b, i: (0, 0)
    wide = ML_H * LANES
    r3 = lambda a: a.reshape(batch, seq, a.shape[-1])
    out = pl.pallas_call(
        _mlstm_kernel,
        grid=(batch // nb, nc),
        in_specs=[pl.BlockSpec((nb, c, wide), row), pl.BlockSpec((nb, c, wide), row), pl.BlockSpec((nb, c, wide), row),
                  pl.BlockSpec((nb, c, wide), row), pl.BlockSpec((nb, c, LANES), row),
                  pl.BlockSpec((1, LANES), fix), pl.BlockSpec((1, wide), fix)],
        out_specs=pl.BlockSpec((nb, c, wide), row),
        out_shape=jax.ShapeDtypeStruct((batch, seq, wide), BF16),
        scratch_shapes=[pltpu.VMEM((nb, ML_H, LANES, ML_DV), F32), pltpu.VMEM((nb, ML_H, 1, LANES), F32),
                        pltpu.VMEM((nb, ML_H, 1, LANES), F32)],
        compiler_params=_cparams(("arbitrary", "arbitrary")),
        name="mlstm",
    )(r3(mq), r3(mk), r3(mv), r3(mo), r3(gates), bias_row, norm_row)
    return out.reshape(batch * seq, wide)


def _swa_kernel(q_ref, kc_ref, kp_ref, vc_ref, vp_ref, sink_ref, o_ref):
    w = WINDOW
    n = pl.program_id(1)
    scale = SWA_D ** -0.5
    qi = _iota2((w, w), 0)
    kj = _iota2((w, w), 1)
    mask_c = kj <= qi
    mask_p = jnp.logical_and(kj > qi, n > 0)
    grp = SWA_H // SWA_KV
    neg = -1e30
    units = [(bb, h) for bb in range(q_ref.shape[0]) for h in range(SWA_H)]
    scores = []
    half_of_lane = _iota2((w, LANES), 1) // SWA_D
    for bb, h in units:
        g = h // grp
        pair = q_ref[bb, :, (h // 2) * LANES:(h // 2 + 1) * LANES]
        q = jnp.where(half_of_lane == h % 2, pair, jnp.zeros_like(pair))
        scores.append((_dot_nt(q, kc_ref[bb, :, g * LANES:(g + 1) * LANES]),
                       _dot_nt(q, kp_ref[bb, :, g * LANES:(g + 1) * LANES])))
    masked, tops, exps, dens, probs = [], [], [], [], {}
    for sc, sp in scores:
        masked.append((jnp.where(mask_c, sc * scale, neg), jnp.where(mask_p, sp * scale, neg)))
    for (bb, h), (s_c, s_p) in zip(units, masked):
        tops.append(jnp.maximum(jnp.max(jnp.maximum(s_c, s_p), axis=-1, keepdims=True), sink_ref[:, h:h + 1]))
    for (s_c, s_p), m in zip(masked, tops):
        exps.append((jnp.where(mask_c, jnp.exp(s_c - m), 0.0), jnp.where(mask_p, jnp.exp(s_p - m), 0.0)))
    ones_b = jnp.ones((w, LANES), BF16)
    for (bb, h), (p_c, p_p), m in zip(units, exps, tops):
        p_c, p_p = p_c.astype(BF16), p_p.astype(BF16)
        probs[bb, h] = (p_c, p_p)
        dens.append(_dot(p_c, ones_b) + _dot(p_p, ones_b) + jnp.exp(sink_ref[:, h:h + 1] - m))
    inv = {u: 1.0 / den for u, den in zip(units, dens)}
    for bb in range(q_ref.shape[0]):
        for pair in range(SWA_H // 2):
            acc = None
            for sub in range(2):
                h = 2 * pair + sub
                vcol = (2 * (h // grp) + sub) * LANES
                p_c, p_p = probs[bb, h]
                part = (_dot(p_c, vc_ref[bb, :, vcol:vcol + LANES]) + _dot(p_p, vp_ref[bb, :, vcol:vcol + LANES])) * inv[bb, h]
                acc = part if acc is None else acc + part
            o_ref[bb, :, pair * LANES:(pair + 1) * LANES] = acc.astype(o_ref.dtype)


def _swa(sq, sk, sv, sinks_row, batch, seq):
    w = WINDOW
    nb = seq // w
    ns = min(SWA_SEQS_PER_STEP, batch)
    wo = SWA_H * SWA_D
    cur = lambda b, n: (b, n, 0)
    prev = lambda b, n: (b, jnp.maximum(n - 1, 0), 0)
    r3 = lambda a: a.reshape(batch, seq, a.shape[-1])
    q3, k3, v3 = r3(sq), r3(sk), r3(sv)
    out = pl.pallas_call(
        _swa_kernel,
        grid=(batch // ns, nb),
        in_specs=[pl.BlockSpec((ns, w, sq.shape[1]), cur),
                  pl.BlockSpec((ns, w, sk.shape[1]), cur), pl.BlockSpec((ns, w, sk.shape[1]), prev),
                  pl.BlockSpec((ns, w, sv.shape[1]), cur), pl.BlockSpec((ns, w, sv.shape[1]), prev),
                  pl.BlockSpec((1, LANES), lambda b, n: (0, 0))],
        out_specs=pl.BlockSpec((ns, w, wo), cur),
        out_shape=jax.ShapeDtypeStruct((batch, seq, wo), BF16),
        compiler_params=_cparams(("arbitrary", "arbitrary")),
        name="swa",
    )(q3, k3, k3, v3, v3, sinks_row)
    return out.reshape(batch * seq, wo)


def _layer_norm(h, g, b):
    mu = jnp.mean(h, axis=-1, keepdims=True)
    d = h - mu
    var = jnp.mean(d * d, axis=-1, keepdims=True)
    return d * lax.rsqrt(var + LN_EPS) * g + b


def _outproj_kernel(x_ref, a1_ref, a2_ref, w_ref, g_ref, b_ref, o_ref, op_ref):
    k1 = a1_ref.shape[1]
    y = _dot(a1_ref[...].astype(BF16), w_ref[0:k1, :]) + _dot(a2_ref[...].astype(BF16), w_ref[k1:, :])
    h = _layer_norm(DN_ALPHA * x_ref[...] + y, g_ref[...], b_ref[...])
    o_ref[...] = h
    op_ref[...] = _pack_pairs(h)


def _outproj_ln(x, a1, a2, w, g, b, tm=512):
    t, d = x.shape
    row = lambda i: (i, 0)
    fix = lambda i: (0, 0)
    return pl.pallas_call(
        _outproj_kernel,
        grid=(t // tm,),
        in_specs=[pl.BlockSpec((tm, d), row), pl.BlockSpec((tm, a1.shape[1]), row), pl.BlockSpec((tm, a2.shape[1]), row),
                  pl.BlockSpec(w.shape, fix), pl.BlockSpec((1, d), fix), pl.BlockSpec((1, d), fix)],
        out_specs=[pl.BlockSpec((tm, d), row), pl.BlockSpec((tm, d // 2), row)],
        out_shape=[jax.ShapeDtypeStruct((t, d), F32), jax.ShapeDtypeStruct((t, d // 2), jnp.uint32)],
        compiler_params=_cparams(("arbitrary",)),
        name="outproj_ln",
    )(x, a1, a2, w, g, b)


def _first_index(x, m, iota_f, sentinel):
    return jnp.min(jnp.where(x == m, iota_f, sentinel), axis=0, keepdims=True)


def _router_kernel(x_ref, wt_ref, bias_ref, idx_ref, gate_ref, rank_ref, cnt_ref, carry_ref):
    tm = x_ref.shape[0]
    e = N_EXPERTS
    gs = e // N_GROUPS
    ninf = -jnp.inf

    @pl.when(pl.program_id(0) == 0)
    def _():
        carry_ref[...] = jnp.zeros(carry_ref.shape, F32)

    logits = _dot3(wt_ref[...], x_ref[...], _dot_nt)
    scores = _sigmoid(logits)
    sel = scores + bias_ref[:, 0:1]

    sub_f = _iota2((gs, tm), 0).astype(F32)
    gscore = []
    for g in range(N_GROUPS):
        blk = sel[g * gs:(g + 1) * gs, :]
        m1 = jnp.max(blk, axis=0, keepdims=True)
        i1 = _first_index(blk, m1, sub_f, float(gs))
        m2 = jnp.max(jnp.where(sub_f == i1, ninf, blk), axis=0, keepdims=True)
        gscore.append(m1 + m2)
    gsc = jnp.concatenate(gscore, axis=0)
    grp_f = _iota2((N_GROUPS, tm), 0).astype(F32)
    gmask = jnp.zeros((N_GROUPS, tm), F32)
    for _ in range(TOPK_GROUPS):
        m = jnp.max(gsc, axis=0, keepdims=True)
        gi = _first_index(gsc, m, grp_f, float(N_GROUPS))
        hit = grp_f == gi
        gmask = jnp.where(hit, 1.0, gmask)
        gsc = jnp.where(hit, ninf, gsc)
    masked = jnp.concatenate(
        [jnp.where(gmask[g:g + 1, :] > 0.0, sel[g * gs:(g + 1) * gs, :], ninf) for g in range(N_GROUPS)], axis=0)

    exp_f = _iota2((e, tm), 0).astype(F32)
    chosen = jnp.zeros((e, tm), F32)
    idxs, gates = [], []
    for _ in range(TOP_K):
        m = jnp.max(masked, axis=0, keepdims=True)
        ei = _first_index(masked, m, exp_f, float(e))
        hit = exp_f == ei
        idxs.append(ei)
        gates.append(jnp.sum(jnp.where(hit, scores, 0.0), axis=0, keepdims=True))
        chosen = jnp.where(hit, 1.0, chosen)
        masked = jnp.where(hit, ninf, masked)
    gate = jnp.concatenate(gates, axis=0)
    gate = gate / jnp.sum(gate, axis=0, keepdims=True) * ROUTED_SCALE
    idx_f = jnp.concatenate(idxs, axis=0)

    upper = (_iota2((tm, tm), 0) < _iota2((tm, tm), 1)).astype(BF16)
    before = _dot(chosen.astype(BF16), upper) + carry_ref[...][:, 0:1]
    ranks = [jnp.sum(jnp.where(exp_f == idxs[k], before, 0.0), axis=0, keepdims=True) for k in range(TOP_K)]
    carry_ref[...] = carry_ref[...] + jnp.sum(chosen, axis=1, keepdims=True)

    idx_ref[...] = idx_f.astype(jnp.int32)
    gate_ref[...] = gate
    rank_ref[...] = jnp.concatenate(ranks, axis=0).astype(jnp.int32)
    cnt_ref[...] = carry_ref[...]


def _router(x, wt, bias_col, tm=512):
    t, d = x.shape
    col = lambda i: (0, i)
    fix = lambda i: (0, 0)
    return pl.pallas_call(
        _router_kernel,
        grid=(t // tm,),
        in_specs=[pl.BlockSpec((tm, d), lambda i: (i, 0)), pl.BlockSpec(wt.shape, fix), pl.BlockSpec((N_EXPERTS, LANES), fix)],
        out_specs=[pl.BlockSpec((TOP_K, tm), col), pl.BlockSpec((TOP_K, tm), col), pl.BlockSpec((TOP_K, tm), col),
                   pl.BlockSpec((N_EXPERTS, LANES), fix)],
        out_shape=[jax.ShapeDtypeStruct((TOP_K, t), jnp.int32), jax.ShapeDtypeStruct((TOP_K, t), F32),
                   jax.ShapeDtypeStruct((TOP_K, t), jnp.int32), jax.ShapeDtypeStruct((N_EXPERTS, LANES), F32)],
        scratch_shapes=[pltpu.VMEM((N_EXPERTS, LANES), F32)],
        compiler_params=_cparams(("arbitrary",)),
        name="router",
    )(x, wt, bias_col)


def _dest_kernel(idx_ref, rank_ref, start_ref, dest_ref):
    tm = idx_ref.shape[1]
    exp_i = _iota2((N_EXPERTS, tm), 0)
    start = start_ref[:, 0:1]
    rows = [jnp.sum(jnp.where(exp_i == idx_ref[s:s + 1, :], start, 0.0), axis=0, keepdims=True) for s in range(TOP_K)]
    dest_ref[...] = jnp.concatenate(rows, axis=0).astype(jnp.int32) + rank_ref[...]


def _dest_rows(idx, rank, start_col, tm=2048):
    t = idx.shape[1]
    tm = min(tm, t)
    col = lambda i: (0, i)
    return pl.pallas_call(
        _dest_kernel,
        grid=(t // tm,),
        in_specs=[pl.BlockSpec((TOP_K, tm), col), pl.BlockSpec((TOP_K, tm), col),
                  pl.BlockSpec((N_EXPERTS, LANES), lambda i: (0, 0))],
        out_specs=pl.BlockSpec((TOP_K, tm), col),
        out_shape=jax.ShapeDtypeStruct((TOP_K, t), jnp.int32),
        compiler_params=_cparams(("arbitrary",)),
        name="moe_dest",
    )(idx, rank, start_col)


def _pack_pairs(x):
    n = x.shape[1] // 2
    hi = lax.bitcast_convert_type(x[:, :n].astype(BF16).astype(F32), jnp.uint32)
    lo = lax.bitcast_convert_type(x[:, n:].astype(BF16).astype(F32), jnp.uint32)
    return hi | (lo >> 16)


def _unpack_pairs(w):
    hi = lax.bitcast_convert_type(w & jnp.uint32(0xFFFF0000), F32)
    lo = lax.bitcast_convert_type(w << 16, F32)
    return hi, lo


def _sc_scatter_rows(xp, dest, rows, chunk=LANES):
    t, width = xp.shape
    info = plsc.get_sparse_core_info()
    ncores, nsub = info.num_cores, info.num_subcores
    per_worker = t // (ncores * nsub)
    nchunk = per_worker // chunk
    mesh = plsc.VectorSubcoreMesh(core_axis_name="c", subcore_axis_name="s")

    @functools.partial(
        pl.kernel, mesh=mesh,
        out_type=jax.ShapeDtypeStruct((rows, width), xp.dtype),
        scratch_types=[pltpu.VMEM((TOP_K, chunk), jnp.int32), pltpu.VMEM((chunk, width), xp.dtype), pltpu.SemaphoreType.DMA],
    )
    def scatter(xp_hbm, dest_hbm, out_hbm, idx_v, rows_v, sem):
        base = (lax.axis_index("s") * ncores + lax.axis_index("c")) * per_worker

        @pl.loop(0, nchunk)
        def _(i):
            off = pl.multiple_of(base + i * chunk, chunk)
            pltpu.sync_copy(dest_hbm.at[:, pl.ds(off, chunk)], idx_v)
            pltpu.sync_copy(xp_hbm.at[pl.ds(off, chunk)], rows_v)
            copies = [pltpu.async_copy(rows_v, out_hbm.at[idx_v.at[s]], sem) for s in range(TOP_K)]
            for cp in copies:
                cp.wait()

    return scatter(xp, dest)


def _experts_kernel(be_ref, nu_ref, nv_ref, first_ref, slot_ref, nxt_ref, xs_ref, wg_hbm, wu_hbm, wd_hbm, ys_ref,
                    wgf_ref, wuf_ref, wdf_ref, wgb_ref, wub_ref, wdb_ref, sem, *, layer):
    i = pl.program_id(0)

    def fetch(e, s):
        return [pltpu.make_async_copy(wg_hbm.at[layer, e], wgf_ref.at[s], sem.at[s]),
                pltpu.make_async_copy(wu_hbm.at[layer, e], wuf_ref.at[s], sem.at[s]),
                pltpu.make_async_copy(wd_hbm.at[layer, e], wdf_ref.at[s], sem.at[s])]

    @pl.when(i == 0)
    def _():
        for cp in fetch(be_ref[0], 0):
            cp.start()

    @pl.when(jnp.logical_and(first_ref[i] == 1, i < nu_ref[0]))
    def _():
        s = slot_ref[i]
        for cp in fetch(be_ref[i], s):
            cp.wait()
        wgb_ref[...] = wgf_ref[s].astype(BF16)
        wub_ref[...] = wuf_ref[s].astype(BF16)
        wdb_ref[...] = wdf_ref[s].astype(BF16)

        @pl.when(nxt_ref[i] >= 0)
        def _():
            for cp in fetch(nxt_ref[i], 1 - s):
                cp.start()

    @pl.when(i < nu_ref[0])
    def _():
        half = xs_ref.shape[1]
        sub = xs_ref.shape[0] // EXPERT_SUBBLOCKS
        acts = []
        for r in range(EXPERT_SUBBLOCKS):
            rows = pl.ds(r * sub, sub)
            live = (_iota2((sub, 1), 0) + r * sub) < nv_ref[i]
            xa, xb = _unpack_pairs(jnp.where(live, xs_ref[rows, :], jnp.uint32(0)))
            x = jnp.concatenate([xa.astype(BF16), xb.astype(BF16)], axis=1)
            acts.append((_dot(x, wgb_ref[...]), _dot(x, wub_ref[...])))
        outs = [_dot((_silu(gate) * up).astype(BF16), wdb_ref[...]) for gate, up in acts]
        for r, y in enumerate(outs):
            ys_ref[pl.ds(r * sub, sub), :] = _pack_pairs(y)


def _experts(block_e, n_used, n_valid, xs, wg, wu, wd, layer):
    rows, half = xs.shape
    d = 2 * half
    nb = rows // EXPERT_BLOCK
    pos = jnp.arange(nb, dtype=jnp.int32)
    first = jnp.concatenate([jnp.ones((1,), jnp.int32), (block_e[1:] != block_e[:-1]).astype(jnp.int32)])
    slot = (jnp.cumsum(first) - 1) % 2
    later = (pos[None, :] > pos[:, None]) & (block_e[None, :] != block_e[:, None]) & (pos[None, :] < n_used[0])
    nxt_pos = jnp.min(jnp.where(later, pos[None, :], nb), axis=1)
    nxt = jnp.where(nxt_pos < nb, block_e[jnp.minimum(nxt_pos, nb - 1)], -1)
    blk = lambda i, be, nu, *rest: (jnp.minimum(i, nu[0] - 1), 0)
    hbm = pl.BlockSpec(memory_space=pl.ANY)
    return pl.pallas_call(
        functools.partial(_experts_kernel, layer=layer),
        grid_spec=pltpu.PrefetchScalarGridSpec(
            num_scalar_prefetch=6,
            grid=(nb,),
            in_specs=[pl.BlockSpec((EXPERT_BLOCK, half), blk), hbm, hbm, hbm],
            out_specs=pl.BlockSpec((EXPERT_BLOCK, half), blk),
            scratch_shapes=[pltpu.VMEM((2, d, D_EXPERT), F32), pltpu.VMEM((2, d, D_EXPERT), F32),
                            pltpu.VMEM((2, D_EXPERT, d), F32),
                            pltpu.VMEM((d, D_EXPERT), BF16), pltpu.VMEM((d, D_EXPERT), BF16),
                            pltpu.VMEM((D_EXPERT, d), BF16), pltpu.SemaphoreType.DMA((2,))],
        ),
        out_shape=jax.ShapeDtypeStruct((rows, half), jnp.uint32),
        compiler_params=_cparams(("arbitrary",)),
        name="moe_experts",
    )(block_e, n_used, n_valid, first, slot.astype(jnp.int32), nxt.astype(jnp.int32), xs, wg, wu, wd)


def _sc_gather_rows(table, idx, chunk=SC_CHUNK):
    n = idx.shape[0]
    width = table.shape[1]
    info = plsc.get_sparse_core_info()
    ncores, nsub = info.num_cores, info.num_subcores
    per_worker = n // (ncores * nsub)
    nchunk = per_worker // chunk
    mesh = plsc.VectorSubcoreMesh(core_axis_name="c", subcore_axis_name="s")

    @functools.partial(
        pl.kernel, mesh=mesh,
        out_type=jax.ShapeDtypeStruct((n, width), table.dtype),
        scratch_types=[pltpu.VMEM((nchunk, chunk), jnp.int32), pltpu.VMEM((2, chunk, width), table.dtype),
                       pltpu.SemaphoreType.DMA((2,)), pltpu.SemaphoreType.DMA((2,))],
    )
    def gather(table_hbm, idx_hbm, out_hbm, idx_v, rows_v, gsem, wsem):
        wid = lax.axis_index("s") * ncores + lax.axis_index("c")
        base = wid * per_worker
        pltpu.sync_copy(idx_hbm.at[pl.ds(wid * nchunk, nchunk)], idx_v)

        def fetch(j, b):
            return pltpu.make_async_copy(table_hbm.at[idx_v.at[j]], rows_v.at[b], gsem.at[b])

        def flush(j, b):
            off = pl.multiple_of(base + j * chunk, chunk)
            return pltpu.make_async_copy(rows_v.at[b], out_hbm.at[pl.ds(off, chunk)], wsem.at[b])

        fetch(0, 0).start()

        @pl.loop(0, nchunk, step=2)
        def _(i):
            for b in range(2):
                j = i + b
                fetch(j, b).wait()

                @pl.when(j + 1 < nchunk)
                def _():
                    @pl.when(j >= 1)
                    def _():
                        flush(j - 1, 1 - b).wait()

                    fetch(j + 1, 1 - b).start()

                flush(j, b).start()

        flush(nchunk - 2, 0).wait()
        flush(nchunk - 1, 1).wait()

    return gather(table, idx.reshape(n // chunk, chunk))


def _combine_kernel(x_ref, gate_ref, rows_ref, sg_ref, su_ref, sd_ref, g_ref, b_ref, o_ref):
    x = x_ref[...]
    xb = x.astype(BF16)
    hs = _silu(_dot(xb, sg_ref[...])) * _dot(xb, su_ref[...])
    ff = _dot(hs.astype(BF16), sd_ref[...])
    gate = gate_ref[...]
    half = rows_ref.shape[2]
    ya = ff[:, :half]
    yb = ff[:, half:]
    for s in range(TOP_K):
        a, b = _unpack_pairs(rows_ref[s])
        ya = ya + gate[:, s:s + 1] * a
        yb = yb + gate[:, s:s + 1] * b
    ff = jnp.concatenate([ya, yb], axis=1)
    o_ref[...] = _layer_norm(DN_ALPHA * x + ff, g_ref[...], b_ref[...])


def _combine_alias_kernel(prev_ref, *refs):
    del prev_ref
    _combine_kernel(*refs)


def _combine(x, gate_t, rows, sg, su, sd, g, b, part, nparts, prev=None, tm=512):
    t, d = x.shape
    tp = t // nparts
    tm = min(tm, tp)
    first = part * (tp // tm)
    row = lambda i: (first + i, 0)
    fix = lambda i: (0, 0)
    in_specs = [pl.BlockSpec((tm, d), row), pl.BlockSpec((tm, TOP_K), row),
                pl.BlockSpec((TOP_K, tm, d // 2), lambda i: (0, i, 0)),
                pl.BlockSpec(sg.shape, fix), pl.BlockSpec(su.shape, fix), pl.BlockSpec(sd.shape, fix),
                pl.BlockSpec((1, d), fix), pl.BlockSpec((1, d), fix)]
    args = (x, gate_t, rows, sg, su, sd, g, b)
    if prev is None:
        body, aliases = _combine_kernel, {}
    else:
        body, aliases = _combine_alias_kernel, {0: 0}
        in_specs = [pl.BlockSpec(memory_space=pl.ANY)] + in_specs
        args = (prev,) + args
    return pl.pallas_call(
        body,
        grid=(tp // tm,),
        in_specs=in_specs,
        out_specs=pl.BlockSpec((tm, d), row),
        out_shape=jax.ShapeDtypeStruct((t, d), F32),
        input_output_aliases=aliases,
        compiler_params=_cparams(("arbitrary",)),
        name="moe_combine",
    )(*args)


def _take_cols(w, idx):
    idx = np.asarray(idx)
    runs, start = [], 0
    for pos in range(1, len(idx) + 1):
        run_ends = pos == len(idx) or (idx[pos] != idx[pos - 1] + 1 if idx[pos - 1] >= 0 else idx[pos] >= 0)
        if run_ends:
            runs.append((start, int(idx[start]), pos - start))
            start = pos

    def body(w_ref, o_ref):
        for dst, src, width in runs:
            if src < 0:
                o_ref[:, dst:dst + width] = jnp.zeros((o_ref.shape[0], width), o_ref.dtype)
            else:
                o_ref[:, dst:dst + width] = w_ref[:, src:src + width].astype(o_ref.dtype)

    rows = w.shape[0]
    tr = min(rows, 256)
    return pl.pallas_call(
        body,
        grid=(rows // tr,),
        in_specs=[pl.BlockSpec((tr, w.shape[1]), lambda i: (i, 0))],
        out_specs=pl.BlockSpec((tr, len(idx)), lambda i: (i, 0)),
        out_shape=jax.ShapeDtypeStruct((rows, len(idx)), BF16),
        compiler_params=_cparams(("arbitrary",)),
        name="weight_cols",
    )(w)


def _pad_lane_row(v, first_lane, width=LANES):
    out = jnp.zeros((1, width), F32)
    return lax.dynamic_update_slice(out, v.reshape(1, -1).astype(F32), (0, first_lane))


def _even_in_cols():
    z = lambda n: -np.ones(n, int)
    kr0 = Q_LORA + KV_LORA
    half = MLA_ROPE // 2
    cols = [np.arange(0, Q_LORA), np.arange(Q_LORA, Q_LORA + KV_LORA),
            z(64), np.arange(kr0, kr0 + MLA_ROPE), z(32),
            z(64), np.arange(kr0 + half, kr0 + MLA_ROPE), np.arange(kr0, kr0 + half), z(32)]
    g0 = kr0 + MLA_ROPE
    nqk = GDN_H * GDN_DK
    cols.append(np.arange(g0, g0 + 3 * nqk))
    zoff = g0 + 3 * nqk + 2 * GDN_H
    cols.append(np.arange(zoff, zoff + GDN_H * GDN_DV))
    cols += [np.arange(g0 + 3 * nqk, g0 + 3 * nqk + 2 * GDN_H), z(LANES - 2 * GDN_H)]
    return np.concatenate(cols)


EV_WIDTHS = (Q_LORA + KV_LORA + 2 * LANES, 3 * GDN_H * GDN_DK, GDN_H * GDN_DV, LANES)


def _mla_q_cols():
    per = MLA_NOPE + MLA_ROPE
    half = MLA_ROPE // 2
    main, sw = [], []
    for h in range(MLA_H):
        b = h * per
        main += [np.arange(b, b + per), -np.ones(LANES - per, int)]
        sw += [-np.ones(MLA_NOPE, int), np.arange(b + MLA_NOPE + half, b + per), np.arange(b + MLA_NOPE, b + MLA_NOPE + half),
               -np.ones(LANES - per, int)]
    return np.concatenate(main + sw)


def _mla_kv_cols():
    per = MLA_NOPE + MLA_V
    kc, vc = [], []
    for h in range(MLA_H):
        b = h * per
        kc += [np.arange(b, b + MLA_NOPE), -np.ones(LANES - MLA_NOPE, int)]
        vv = np.arange(b + MLA_NOPE, b + per)
        pad = -np.ones(LANES - MLA_V, int)
        vc += [vv, pad] if h % 2 == 0 else [pad, vv]
    return np.concatenate(kc + vc)


def _odd_in_cols():
    z = lambda n: -np.ones(n, int)
    o = 0
    cols = []
    mq0, mk0 = 0, ML_H * ML_DK
    for base in (mq0, mk0):
        for h in range(ML_H):
            cols += [np.arange(base + h * ML_DK, base + (h + 1) * ML_DK), z(LANES - ML_DK)]
    mv0 = 2 * ML_H * ML_DK
    cols.append(np.arange(mv0, mv0 + ML_H * ML_DV))
    mi0 = mv0 + ML_H * ML_DV
    mo0 = mi0 + 2 * ML_H
    cols.append(np.arange(mo0, mo0 + ML_H * ML_DV))
    cols += [np.arange(mi0, mi0 + 2 * ML_H), z(LANES - 2 * ML_H)]
    sq0 = mo0 + ML_H * ML_DV
    sk0 = sq0 + SWA_H * SWA_D
    sv0 = sk0 + SWA_KV * SWA_D
    half = SWA_D // 2

    def heads(base, n, swapped, copies):
        out = []
        for h in range(n):
            b = base + h * SWA_D
            one = [np.arange(b + half, b + SWA_D), np.arange(b, b + half)] if swapped else [np.arange(b, b + SWA_D)]
            out += one * copies
        return out

    cols += (heads(sq0, SWA_H, False, 1) + heads(sq0, SWA_H, True, 1)
             + heads(sk0, SWA_KV, False, 2) + heads(sk0, SWA_KV, True, 2))
    for g in range(SWA_KV):
        vv = np.arange(sv0 + g * SWA_D, sv0 + (g + 1) * SWA_D)
        cols += [vv, z(LANES - SWA_D), z(LANES - SWA_D), vv]
    return np.concatenate(cols)


def _even_mixer(x, tabs, w_in, q_norm, w_qb, kv_norm, w_kvb, conv_w, a_log, dt_bias, o_norm, batch, seq):
    ctab, stab = tabs
    w = _take_cols(w_in, _even_in_cols()).astype(BF16)
    mla_in, act, z, gates = _proj_even(x, w, conv_w, seq)
    wq2 = _take_cols(w_qb, _mla_q_cols()).astype(BF16)
    wkv2 = _take_cols(w_kvb, _mla_kv_cols()).astype(BF16)
    q, k, v = _mla_prep(mla_in, ctab, stab, q_norm.reshape(1, -1), kv_norm.reshape(1, -1), wq2, wkv2)
    o_a = _mla_attn(q, k, v, batch, seq)
    o_b = _gdn(act, gates, z, _pad_lane_row(a_log, GDN_H), _pad_lane_row(dt_bias, GDN_H),
               o_norm.reshape(1, -1), batch, seq)
    return o_a, o_b


def _odd_mixer(x, tabs, w_in, b_i, b_f, ml_norm, sinks, batch, seq):
    ctab, stab = tabs
    w = _take_cols(w_in, _odd_in_cols()).astype(BF16)
    mq, mk, mv, mo, mg, sq, sk, sv = _proj_odd(x, w, ctab, stab)
    bias_row = _pad_lane_row(jnp.concatenate([b_i, b_f]), 0)
    o_c = _mlstm(mq, mk, mv, mo, mg, bias_row, ml_norm.reshape(1, -1), batch, seq)
    o_d = _swa(sq, sk, sv, _pad_lane_row(sinks, 0), batch, seq)
    return o_c, o_d


def _moe(x, xp, router_w, router_b, w_gate, w_up, w_down, layer, s_gate, s_up, s_down, ln_g, ln_b):
    t, d = x.shape
    bias_col = jnp.broadcast_to(router_b.reshape(-1, 1).astype(F32), (N_EXPERTS, LANES))
    idx, gate, rank, cnt = _router(x, router_w.T, bias_col)
    counts = cnt[:, 0].astype(jnp.int32)
    padded = (counts + EXPERT_BLOCK - 1) // EXPERT_BLOCK * EXPERT_BLOCK
    pad_end = jnp.cumsum(padded)
    pad_start = pad_end - padded
    start_col = jnp.broadcast_to(pad_start.astype(F32).reshape(-1, 1), (N_EXPERTS, LANES))
    dest = _dest_rows(idx, rank, start_col)
    n_blocks = t * TOP_K // EXPERT_BLOCK + N_EXPERTS
    rows = n_blocks * EXPERT_BLOCK
    block_row = jnp.arange(n_blocks, dtype=jnp.int32) * EXPERT_BLOCK
    block_e = jnp.minimum(jnp.sum((pad_end[None, :] <= block_row[:, None]).astype(jnp.int32), axis=1), N_EXPERTS - 1)
    n_used = (pad_end[-1:] // EXPERT_BLOCK).astype(jnp.int32)
    live_end = jnp.sum(jnp.where(block_e[:, None] == jnp.arange(N_EXPERTS, dtype=jnp.int32)[None, :],
                                 (pad_start + counts)[None, :], 0), axis=1)
    n_valid = jnp.clip(live_end - block_row, 0, EXPERT_BLOCK).astype(jnp.int32)
    xs = _sc_scatter_rows(xp, dest, rows)
    ys = _experts(block_e, n_used, n_valid, xs, w_gate, w_up, w_down, layer)
    nparts = COMBINE_PARTS if t % (COMBINE_PARTS * 512) == 0 else 1
    tp = t // nparts
    gate_t = gate.T
    sgb, sub, sdb = s_gate.astype(BF16), s_up.astype(BF16), s_down.astype(BF16)
    out = None
    for part in range(nparts):
        idx_p = dest[:, part * tp:(part + 1) * tp].reshape(-1)
        picked = _sc_gather_rows(ys, idx_p).reshape(TOP_K, tp, d // 2)
        out = _combine(x, gate_t, picked, sgb, sub, sdb, ln_g.reshape(1, -1), ln_b.reshape(1, -1), part, nparts, prev=out)
    return out


def kernel(x, positions, ev_w_in, mla_q_norm, mla_w_qb, mla_kv_norm, mla_w_kvb, gdn_conv, gdn_a_log, gdn_dt_bias, gdn_norm, ev_w_out, od_w_in, mlstm_b_i, mlstm_b_f, mlstm_norm, swa_sinks, od_w_out, ln1_g, ln1_b, router_w, router_b, moe_w_gate, moe_w_up, moe_w_down, shared_w_gate, shared_w_up, shared_w_down, ln2_g, ln2_b):
    batch, seq, d = x.shape
    t = batch * seq
    pos = positions.reshape(t, 1).astype(F32)
    tabs_m = _rope_tables(pos, _rope_rows(MLA_ROPE, MLA_NOPE, MLA_NOPE))
    tabs_s = _rope_tables(pos, _rope_rows(SWA_D, 0, 0, heads=LANES // SWA_D))
    h = x.reshape(t, d)
    for layer in range(DEPTH):
        j = layer // 2
        if layer % 2 == 0:
            a1, a2 = _even_mixer(h, tabs_m, ev_w_in[j], mla_q_norm[j], mla_w_qb[j], mla_kv_norm[j], mla_w_kvb[j],
                                 gdn_conv[j], gdn_a_log[j], gdn_dt_bias[j], gdn_norm[j], batch, seq)
            w_out = ev_w_out[j]
        else:
            a1, a2 = _odd_mixer(h, tabs_s, od_w_in[j], mlstm_b_i[j], mlstm_b_f[j], mlstm_norm[j], swa_sinks[j], batch, seq)
            w_out = od_w_out[j]
        h, hp = _outproj_ln(h, a1, a2, w_out.astype(BF16), ln1_g[layer].reshape(1, -1), ln1_b[layer].reshape(1, -1))
        h = _moe(h, hp, router_w[layer], router_b[layer], moe_w_gate, moe_w_up, moe_w_down, layer,
                 shared_w_gate[layer], shared_w_up[layer], shared_w_down[layer], ln2_g[layer], ln2_b[layer])
    return h.reshape(batch, seq, d)
```

```python
import functools
import math

import numpy as np
import jax
import jax.numpy as jnp
from jax import lax
from jax.experimental import pallas as pl
from jax.experimental.pallas import tpu as pltpu
from jax.experimental.pallas import tpu_sc as plsc

F32 = jnp.float32
BF16 = jnp.bfloat16
HI = lax.Precision.HIGHEST

D_MODEL = 1024
DEPTH = 4
ROPE_THETA = 10000.0
EPS = 1e-6
LN_EPS = 1e-5
MLA_H, MLA_NOPE, MLA_ROPE, MLA_V = 8, 64, 32, 64
Q_LORA, KV_LORA = 256, 128
GDN_H, GDN_DK, GDN_DV, CONV_W, GDN_CHUNK = 4, 128, 128, 4, 64
ML_H, ML_DK, ML_DV, ML_CHUNK = 4, 64, 128, 64
SWA_H, SWA_KV, SWA_D, WINDOW = 8, 2, 64, 128
N_EXPERTS, N_GROUPS, TOPK_GROUPS, TOP_K = 64, 8, 4, 8
D_EXPERT, D_SHARED = 256, 256
ROUTED_SCALE = 2.5
DN_ALPHA = (2 * DEPTH) ** 0.25

LANES = 128
V7X_VMEM_BYTES = 64 * 1024 * 1024
VMEM_LIMIT = 48 * 1024 * 1024

EXPERT_BLOCK = 1024
EXPERT_SUBBLOCKS = 2
SWA_SEQS_PER_STEP = 4
MLSTM_SEQS_PER_STEP = 2
GDN_SEQS_PER_STEP = 8
SC_CHUNK = 64


def _cparams(sem, vmem=VMEM_LIMIT):
    return pltpu.CompilerParams(dimension_semantics=sem, vmem_limit_bytes=vmem)


def _dot(a, b, precision=None):
    return jnp.dot(a, b, preferred_element_type=F32, precision=precision)


def _dot_nt(a, b, precision=None):
    return lax.dot_general(a, b, (((1,), (1,)), ((), ())), preferred_element_type=F32, precision=precision)


def _dot_tn(a, b, precision=None):
    return lax.dot_general(a, b, (((0,), (0,)), ((), ())), preferred_element_type=F32, precision=precision)


def _split2(a):
    hi = a.astype(BF16)
    lo = (a - hi.astype(F32)).astype(BF16)
    return hi, lo


def _split3(a):
    p1 = a.astype(BF16)
    r = a - p1.astype(F32)
    p2 = r.astype(BF16)
    p3 = (r - p2.astype(F32)).astype(BF16)
    return p1, p2, p3


def _dot3(a, b, dot=_dot):
    ah, al = _split2(a)
    bh, bl = _split2(b)
    return dot(ah, bh) + (dot(ah, bl) + dot(al, bh))


def _dot_sel(sel, b, dot=_dot):
    sel = sel.astype(BF16)
    p1, p2, p3 = _split3(b)
    return dot(sel, p1) + (dot(sel, p2) + dot(sel, p3))


def _sigmoid(x):
    return 1.0 / (1.0 + jnp.exp(-x))


def _softplus(x):
    return jnp.maximum(x, 0.0) + jnp.log(1.0 + jnp.exp(-jnp.abs(x)))


def _silu(x):
    return x * _sigmoid(x)


def _lane_bcast(x, c):
    return jnp.broadcast_to(x[:, c:c + 1], x.shape)


def _iota2(shape, dim):
    return lax.broadcasted_iota(jnp.int32, shape, dim)


def _rope_kernel(pos_ref, rows_ref, c_ref, s_ref):
    ang = pos_ref[...] * rows_ref[0:1, :]
    c_ref[...] = rows_ref[1:2, :] * jnp.cos(ang) + rows_ref[2:3, :]
    s_ref[...] = rows_ref[3:4, :] * jnp.sin(ang)


def _rope_tables(pos, rows, tm=512):
    t = pos.shape[0]
    return pl.pallas_call(
        _rope_kernel,
        grid=(t // tm,),
        in_specs=[pl.BlockSpec((tm, 1), lambda i: (i, 0)), pl.BlockSpec((8, LANES), lambda i: (0, 0))],
        out_specs=[pl.BlockSpec((tm, LANES), lambda i: (i, 0))] * 2,
        out_shape=[jax.ShapeDtypeStruct((t, LANES), F32)] * 2,
        compiler_params=_cparams(("arbitrary",)),
        name="rope_tables",
    )(pos, rows)


def _rope_rows(dim, first_lane, pad_one_lanes, heads=1):
    half = dim // 2
    inv = ROPE_THETA ** (-(np.arange(0, dim, 2, dtype=np.float32) / dim))
    rows = np.zeros((8, LANES), np.float32)
    for h in range(heads):
        lo = slice(first_lane + h * dim, first_lane + h * dim + half)
        hi = slice(first_lane + h * dim + half, first_lane + (h + 1) * dim)
        rows[0, lo] = inv
        rows[0, hi] = inv
        rows[1, lo] = 1.0
        rows[1, hi] = 1.0
        rows[3, lo] = -1.0
        rows[3, hi] = 1.0
    rows[2, :pad_one_lanes] = 1.0
    return jnp.asarray(rows)


def _proj_kernel(x_ref, w_ref, *out_refs, offsets):
    xb = x_ref[...].astype(BF16)
    for o_ref, (a, b) in zip(out_refs, offsets):
        o_ref[...] = _dot(xb, w_ref[:, a:b]).astype(o_ref.dtype)


def _proj(x, w, widths, dtypes, tm=512):
    t, k = x.shape
    offs = np.concatenate([[0], np.cumsum(widths)]).tolist()
    offsets = tuple((offs[i], offs[i + 1]) for i in range(len(widths)))
    return pl.pallas_call(
        functools.partial(_proj_kernel, offsets=offsets),
        grid=(t // tm,),
        in_specs=[pl.BlockSpec((tm, k), lambda i: (i, 0)), pl.BlockSpec(w.shape, lambda i: (0, 0))],
        out_specs=[pl.BlockSpec((tm, n), lambda i: (i, 0)) for n in widths],
        out_shape=[jax.ShapeDtypeStruct((t, n), dt) for n, dt in zip(widths, dtypes)],
        compiler_params=_cparams(("arbitrary",)),
        name="in_proj",
    )(x, w)


def _proj_even_kernel(x_ref, w_ref, cw_ref, mla_ref, act_ref, z_ref, g_ref, ext_ref, *, tiles_per_seq):
    tm = x_ref.shape[0]
    o = np.concatenate([[0], np.cumsum(EV_WIDTHS)]).tolist()
    @pl.when(pl.program_id(0) % tiles_per_seq == 0)
    def _():
        ext_ref[0:8, :] = jnp.zeros((8, ext_ref.shape[1]), F32)

    xb = x_ref[...].astype(BF16)
    nchunk = 3
    cw = EV_WIDTHS[1] // nchunk

    def project(ci):
        ext_ref[8:8 + tm, ci * cw:(ci + 1) * cw] = _dot(xb, w_ref[:, o[1] + ci * cw:o[1] + (ci + 1) * cw])

    project(0)
    for ci in range(nchunk):
        if ci + 1 < nchunk:
            project(ci + 1)
        else:
            mla_ref[...] = _dot(xb, w_ref[:, o[0]:o[1]])
            z_ref[...] = _dot(xb, w_ref[:, o[2]:o[3]]).astype(z_ref.dtype)
            g_ref[...] = _dot(xb, w_ref[:, o[3]:o[4]])
        cols = slice(ci * cw, (ci + 1) * cw)
        conv = cw_ref[0:1, cols] * ext_ref[5:5 + tm, cols]
        for j in range(1, CONV_W):
            conv = conv + cw_ref[j:j + 1, cols] * ext_ref[5 + j:5 + j + tm, cols]
        act_ref[:, cols] = _silu(conv).astype(act_ref.dtype)
    ext_ref[0:8, :] = ext_ref[tm:tm + 8, :]


def _proj_even(x, w, conv_w, seq, tm=512):
    t, k = x.shape
    tm = min(tm, seq)
    row = lambda i: (i, 0)
    fix = lambda i: (0, 0)
    return pl.pallas_call(
        functools.partial(_proj_even_kernel, tiles_per_seq=seq // tm),
        grid=(t // tm,),
        in_specs=[pl.BlockSpec((tm, k), row), pl.BlockSpec(w.shape, fix), pl.BlockSpec(conv_w.shape, fix)],
        out_specs=[pl.BlockSpec((tm, n), row) for n in EV_WIDTHS],
        out_shape=[jax.ShapeDtypeStruct((t, n), F32) for n in EV_WIDTHS],
        scratch_shapes=[pltpu.VMEM((tm + 8, EV_WIDTHS[1]), F32)],
        compiler_params=_cparams(("arbitrary",)),
        name="in_proj",
    )(x, w, conv_w)


OD_SEG = dict(mq=(0, 512), mk=(512, 1024), mv=(1024, 1536), mo=(1536, 2048), gates=(2048, 2176),
              sq=(2176, 2688), sqsw=(2688, 3200), sk=(3200, 3456), sksw=(3456, 3712), sv=(3712, 4224))
OD_COLS = 4224


def _proj_odd_kernel(x_ref, w_ref, c_ref, s_ref, mq_ref, mk_ref, mv_ref, mo_ref, mg_ref, sq_ref, sk_ref, sv_ref):
    xb = x_ref[...].astype(BF16)

    def seg(name):
        a, b = OD_SEG[name]
        return _dot(xb, w_ref[:, a:b])

    mq_ref[...] = seg("mq").astype(mq_ref.dtype)
    mk_ref[...] = seg("mk").astype(mk_ref.dtype)
    mv_ref[...] = seg("mv").astype(mv_ref.dtype)
    mo_ref[...] = seg("mo").astype(mo_ref.dtype)
    mg_ref[...] = seg("gates")
    c = c_ref[...]
    s = s_ref[...]
    c8 = jnp.concatenate([c] * (SWA_H // 2), axis=1)
    s8 = jnp.concatenate([s] * (SWA_H // 2), axis=1)
    sq_ref[...] = (seg("sq") * c8 + seg("sqsw") * s8).astype(sq_ref.dtype)
    c2 = jnp.concatenate([c] * SWA_KV, axis=1)
    s2 = jnp.concatenate([s] * SWA_KV, axis=1)
    sk_ref[...] = (seg("sk") * c2 + seg("sksw") * s2).astype(sk_ref.dtype)
    sv_ref[...] = seg("sv").astype(sv_ref.dtype)


def _proj_odd(x, w, ctab, stab, tm=256):
    t, k = x.shape
    widths = (512, 512, 512, 512, 128, SWA_H * SWA_D, SWA_KV * LANES, 2 * SWA_KV * LANES)
    dtypes = (F32, F32, F32, F32, F32, BF16, BF16, BF16)
    return pl.pallas_call(
        _proj_odd_kernel,
        grid=(t // tm,),
        in_specs=[pl.BlockSpec((tm, k), lambda i: (i, 0)), pl.BlockSpec(w.shape, lambda i: (0, 0)),
                  pl.BlockSpec((tm, LANES), lambda i: (i, 0)), pl.BlockSpec((tm, LANES), lambda i: (i, 0))],
        out_specs=[pl.BlockSpec((tm, n), lambda i: (i, 0)) for n in widths],
        out_shape=[jax.ShapeDtypeStruct((t, n), dt) for n, dt in zip(widths, dtypes)],
        compiler_params=_cparams(("arbitrary",)),
        name="in_proj_odd",
    )(x, w, ctab, stab)


def _rms(x, g):
    return x * lax.rsqrt(jnp.mean(x * x, axis=-1, keepdims=True) + EPS) * g


def _mla_prep_kernel(in_ref, c_ref, s_ref, qn_ref, kvn_ref, wq_ref, wkv_ref, q_ref, k_ref, v_ref):
    hw = MLA_H * LANES
    c = c_ref[...]
    s = s_ref[...]
    c8 = jnp.concatenate([c] * MLA_H, axis=1)
    s8 = jnp.concatenate([s] * MLA_H, axis=1)
    cqn = _rms(in_ref[:, 0:Q_LORA], qn_ref[...]).astype(BF16)
    qq = _dot(cqn, wq_ref[...])
    scale = (MLA_NOPE + MLA_ROPE) ** -0.5
    q_ref[...] = ((qq[:, :hw] * c8 + qq[:, hw:] * s8) * scale).astype(q_ref.dtype)
    ckvn = _rms(in_ref[:, Q_LORA:Q_LORA + KV_LORA], kvn_ref[...]).astype(BF16)
    kv = _dot(ckvn, wkv_ref[...])
    o = Q_LORA + KV_LORA
    krr = in_ref[:, o:o + LANES] * c + in_ref[:, o + LANES:o + 2 * LANES] * s
    k_ref[...] = (kv[:, :hw] + jnp.concatenate([krr] * MLA_H, axis=1)).astype(k_ref.dtype)
    v_ref[...] = kv[:, hw:].astype(v_ref.dtype)


def _mla_prep(mla_in, ctab, stab, qn, kvn, wq2, wkv2, tm=512):
    t = mla_in.shape[0]
    hw = MLA_H * LANES
    row = lambda i: (i, 0)
    fix = lambda i: (0, 0)
    return pl.pallas_call(
        _mla_prep_kernel,
        grid=(t // tm,),
        in_specs=[pl.BlockSpec((tm, mla_in.shape[1]), row), pl.BlockSpec((tm, LANES), row), pl.BlockSpec((tm, LANES), row),
                  pl.BlockSpec(qn.shape, fix), pl.BlockSpec(kvn.shape, fix),
                  pl.BlockSpec(wq2.shape, fix), pl.BlockSpec(wkv2.shape, fix)],
        out_specs=[pl.BlockSpec((tm, hw), row)] * 3,
        out_shape=[jax.ShapeDtypeStruct((t, hw), BF16)] * 3,
        compiler_params=_cparams(("arbitrary",)),
        name="mla_prep",
    )(mla_in, ctab, stab, qn, kvn, wq2, wkv2)


def _mla_attn_kernel(q_ref, k_ref, v_ref, o_ref, *, tq):
    i = pl.program_id(2)
    neg = -1e30
    lane = _iota2((tq, LANES), 1)
    ones_lane = (MLA_V, 0)

    def chunk(j, carry, masked):
        start = pl.multiple_of(j * tq, tq)
        out = []
        for hh in range(2):
            m, acc = carry[hh]
            q = q_ref[:, hh * LANES:(hh + 1) * LANES]
            kc = k_ref[pl.ds(start, tq), hh * LANES:(hh + 1) * LANES]
            vc = v_ref[pl.ds(start, tq), hh * LANES:(hh + 1) * LANES]
            vc = jnp.where(lane == ones_lane[hh], jnp.ones_like(vc), vc)
            s = _dot_nt(q, kc)
            if masked:
                s = jnp.where(_iota2(s.shape, 0) >= _iota2(s.shape, 1), s, neg)
            m_new = jnp.maximum(m, jnp.max(s, axis=-1, keepdims=True))
            alpha = jnp.exp(m - m_new)
            p = jnp.exp(s - m_new)
            acc = alpha * acc + _dot(p.astype(BF16), vc)
            out.append((m_new, acc))
        return tuple(out)

    one = (jnp.full((tq, 1), neg, F32), jnp.zeros((tq, LANES), F32))
    carry = lax.fori_loop(0, i, lambda j, c: chunk(j, c, False), (one, one))
    (_, acc0), (_, acc1) = chunk(i, carry, True)
    o0 = acc0 / _lane_bcast(acc0, ones_lane[0])
    o1 = acc1 / _lane_bcast(acc1, ones_lane[1])
    o_ref[...] = jnp.where(lane < MLA_V, o0, o1).astype(o_ref.dtype)


def _mla_attn(q, k, v, batch, seq, tq=512):
    tq = min(tq, seq)
    nq = seq // tq
    pairs = MLA_H // 2
    return pl.pallas_call(
        functools.partial(_mla_attn_kernel, tq=tq),
        grid=(batch, pairs, nq),
        in_specs=[pl.BlockSpec((tq, 2 * LANES), lambda b, p, i: (b * nq + i, p)),
                  pl.BlockSpec((seq, 2 * LANES), lambda b, p, i: (b, p)),
                  pl.BlockSpec((seq, 2 * LANES), lambda b, p, i: (b, p))],
        out_specs=pl.BlockSpec((tq, LANES), lambda b, p, i: (b * nq + i, p)),
        out_shape=jax.ShapeDtypeStruct((batch * seq, pairs * LANES), BF16),
        compiler_params=_cparams(("arbitrary", "arbitrary", "arbitrary")),
        name="mla_attn",
    )(q, k, v)


def _unit_lower_inverse_many(ns):
    c = ns[0].shape[0]
    eye = (_iota2((c, c), 0) == _iota2((c, c), 1)).astype(F32)
    xs = [-n for n in ns]
    ps = [eye + x for x in xs]
    xb = [x.astype(BF16) for x in xs]
    for _ in range(int(math.log2(c)) - 1):
        xs = [_dot(b, b) for b in xb]
        xb = [x.astype(BF16) for x in xs]
        ps = [p + _dot(p.astype(BF16), b) for p, b in zip(ps, xb)]
    return ps


def _gdn_kernel(act_ref, g_ref, z_ref, al_ref, dt_ref, on_ref, o_ref, st_ref):
    c = GDN_CHUNK
    hd = GDN_DK
    nqk = GDN_H * GDN_DK

    @pl.when(pl.program_id(1) == 0)
    def _():
        st_ref[...] = jnp.zeros(st_ref.shape, F32)

    tri = (_iota2((c, c), 0) >= _iota2((c, c), 1)).astype(F32)
    row_ge = _iota2((c, c), 0) >= _iota2((c, c), 1)
    row_gt = _iota2((c, c), 0) > _iota2((c, c), 1)
    lane = _iota2((c, LANES), 1)

    units = []
    for bb in range(act_ref.shape[0]):
        act = act_ref[bb].astype(F32)
        gates = g_ref[bb]
        beta_all = _sigmoid(gates)
        g_all = -jnp.exp(al_ref[...]) * _softplus(gates + dt_ref[...])
        gc_all = _dot_sel(tri, g_all)
        gc_parts = _split3(gc_all)
        for h in range(GDN_H):
            q = act[:, h * hd:(h + 1) * hd]
            k = act[:, nqk + h * hd:nqk + (h + 1) * hd]
            v = act[:, 2 * nqk + h * GDN_DV:2 * nqk + (h + 1) * GDN_DV]
            q = q * lax.rsqrt(jnp.sum(q * q, axis=-1, keepdims=True) + EPS) * (GDN_DK ** -0.5)
            k = k * lax.rsqrt(jnp.sum(k * k, axis=-1, keepdims=True) + EPS)
            beta = _lane_bcast(beta_all, h)
            gcol = _lane_bcast(gc_all, GDN_H + h)
            pick = (lane == GDN_H + h).astype(BF16)
            grow = _dot_nt(pick, gc_parts[0]) + (_dot_nt(pick, gc_parts[1]) + _dot_nt(pick, gc_parts[2]))
            decay = jnp.exp(jnp.where(row_ge, gcol[:, :c] - grow, -jnp.inf))
            kb = k * beta
            lower = jnp.where(row_gt, _dot3(kb, k, _dot_nt) * decay, 0.0)
            eg = jnp.exp(gcol)
            glast = gcol[c - 1:c, :]
            units.append(dict(bb=bb, h=h, lower=lower, rhs=jnp.concatenate([v * beta, kb * eg], axis=1),
                              attn=_dot_nt(q.astype(BF16), k.astype(BF16)) * decay, qg=(q * eg).astype(BF16),
                              kg=(k * jnp.exp(glast - gcol)).astype(BF16), gl=jnp.exp(glast)))

    tinvs = _unit_lower_inverse_many([u["lower"] for u in units])
    uws = []
    for u, tinv in zip(units, tinvs):
        uws.append(_dot(tinv.astype(BF16), u["rhs"].astype(BF16)))
    states = [st_ref[u["bb"], u["h"]] for u in units]
    sbs = [s.astype(BF16) for s in states]
    vnews = [(uw[:, :GDN_DV] - _dot(uw[:, GDN_DV:].astype(BF16), sb)).astype(BF16) for uw, sb in zip(uws, sbs)]
    for u, state, sb, vnb in zip(units, states, sbs, vnews):
        bb, h = u["bb"], u["h"]
        o = _dot(u["qg"], sb) + _dot(u["attn"].astype(BF16), vnb)
        st_ref[bb, h] = state * u["gl"] + _dot_tn(u["kg"], vnb)
        o = _rms(o, on_ref[...]) * _silu(z_ref[bb, :, h * GDN_DV:(h + 1) * GDN_DV].astype(F32))
        o_ref[bb, :, h * GDN_DV:(h + 1) * GDN_DV] = o.astype(o_ref.dtype)


def _gdn(act, gates, z, a_row, dt_row, o_norm, batch, seq):
    c = GDN_CHUNK
    nc = seq // c
    w3 = act.shape[1]
    wo = GDN_H * GDN_DV
    nb = min(GDN_SEQS_PER_STEP, batch)
    row = lambda b, i: (b, i, 0)
    fix = lambda b, i: (0, 0)
    out = pl.pallas_call(
        _gdn_kernel,
        grid=(batch // nb, nc),
        in_specs=[pl.BlockSpec((nb, c, w3), row), pl.BlockSpec((nb, c, LANES), row), pl.BlockSpec((nb, c, wo), row),
                  pl.BlockSpec((1, LANES), fix), pl.BlockSpec((1, LANES), fix), pl.BlockSpec((1, GDN_DV), fix)],
        out_specs=pl.BlockSpec((nb, c, wo), row),
        out_shape=jax.ShapeDtypeStruct((batch, seq, wo), BF16),
        scratch_shapes=[pltpu.VMEM((nb, GDN_H, GDN_DK, GDN_DV), F32)],
        compiler_params=_cparams(("arbitrary", "arbitrary")),
        name="gdn",
    )(act.reshape(batch, seq, w3), gates.reshape(batch, seq, LANES), z.reshape(batch, seq, wo), a_row, dt_row, o_norm)
    return out.reshape(batch * seq, wo)


def _mlstm_kernel(q_ref, k_ref, v_ref, og_ref, g_ref, bias_ref, nrm_ref, o_ref, c_ref, n_ref, m_ref):
    @pl.when(pl.program_id(1) == 0)
    def _():
        c_ref[...] = jnp.zeros(c_ref.shape, F32)
        n_ref[...] = jnp.zeros(n_ref.shape, F32)
        m_ref[...] = jnp.zeros(m_ref.shape, F32)

    c = ML_CHUNK
    tri = (_iota2((c, c), 0) >= _iota2((c, c), 1)).astype(F32)
    row_ge = _iota2((c, c), 0) >= _iota2((c, c), 1)
    ones = jnp.ones((c, LANES), F32)
    lane = _iota2((c, LANES), 1)

    units = []
    for bb in range(q_ref.shape[0]):
        pre = g_ref[bb] + bias_ref[...]
        logf = jnp.minimum(pre, 0.0) - jnp.log(1.0 + jnp.exp(-jnp.abs(pre)))
        bcum_all = _dot_sel(tri, logf)
        for h in range(ML_H):
            q = q_ref[bb, :, h * LANES:(h + 1) * LANES].astype(F32)
            k = k_ref[bb, :, h * LANES:(h + 1) * LANES].astype(F32) * (ML_DK ** -0.5)
            units.append(dict(bb=bb, h=h, q=q, k=k, qb=q.astype(BF16), vb=v_ref[bb, :, h * ML_DV:(h + 1) * ML_DV].astype(BF16),
                              bcol=_lane_bcast(bcum_all, ML_H + h),
                              icol=_lane_bcast(pre, h),
                              col=jnp.where(lane == h, pre, 0.0) - jnp.where(lane == ML_H + h, bcum_all, 0.0),
                              m_st=m_ref[bb, h], cst=c_ref[bb, h], nst=n_ref[bb, h]))
    for u in units:
        u["row"] = _dot_sel(ones, u["col"], _dot_nt)
        u["qk"] = _dot_nt(u["qb"], u["k"].astype(BF16))
        u["qc"] = _dot(u["qb"], u["cst"].astype(BF16))
    for u in units:
        d = jnp.where(row_ge, u["bcol"][:, :c] + u["row"], -jnp.inf)
        inter = u["bcol"] + u["m_st"]
        m_t = jnp.maximum(inter, jnp.max(d, axis=-1, keepdims=True))
        u["m_t"] = m_t
        u["w_inter"] = jnp.exp(inter - m_t)
        u["p"] = jnp.exp(d - m_t[:, :c]) * u["qk"]
        u["pv"] = _dot(u["p"].astype(BF16), u["vb"])
        b_end = u["bcol"][c - 1:c, :]
        a = b_end - u["bcol"] + u["icol"]
        m_new = jnp.maximum(b_end + u["m_st"], jnp.max(a, axis=0, keepdims=True))
        u["m_new"] = m_new
        u["keep"] = jnp.exp(b_end + u["m_st"] - m_new)
        u["ks"] = u["k"] * jnp.exp(a - m_new)
        u["kv"] = _dot_tn(u["ks"].astype(BF16), u["vb"])
    for u in units:
        bb, h = u["bb"], u["h"]
        num = u["w_inter"] * u["qc"] + u["pv"]
        den = (u["w_inter"] * jnp.sum(u["q"] * u["nst"], axis=-1, keepdims=True)
               + jnp.sum(u["p"], axis=-1, keepdims=True))
        hc = num / jnp.maximum(jnp.abs(den), jnp.exp(-u["m_t"]))
        c_ref[bb, h] = u["cst"] * u["keep"] + u["kv"]
        n_ref[bb, h] = u["nst"] * u["keep"] + jnp.sum(u["ks"], axis=0, keepdims=True)
        m_ref[bb, h] = u["m_new"]
        hn = (_rms(hc, nrm_ref[:, h * ML_DV:(h + 1) * ML_DV])
              * _sigmoid(og_ref[bb, :, h * ML_DV:(h + 1) * ML_DV].astype(F32)))
        o_ref[bb, :, h * ML_DV:(h + 1) * ML_DV] = hn.astype(o_ref.dtype)


def _mlstm(mq, mk, mv, mo, gates, bias_row, norm_row, batch, seq):
    c = ML_CHUNK
    nc = seq // c
    nb = min(MLSTM_SEQS_PER_STEP, batch)
    row = lambda b, i: (b, i, 0)
    fix = lambda b, i: (0, 0)
    wide = ML_H * LANES
    r3 = lambda a: a.reshape(batch, seq, a.shape[-1])
    out = pl.pallas_call(
        _mlstm_kernel,
        grid=(batch // nb, nc),
        in_specs=[pl.BlockSpec((nb, c, wide), row), pl.BlockSpec((nb, c, wide), row), pl.BlockSpec((nb, c, wide), row),
                  pl.BlockSpec((nb, c, wide), row), pl.BlockSpec((nb, c, LANES), row),
                  pl.BlockSpec((1, LANES), fix), pl.BlockSpec((1, wide), fix)],
        out_specs=pl.BlockSpec((nb, c, wide), row),
        out_shape=jax.ShapeDtypeStruct((batch, seq, wide), BF16),
        scratch_shapes=[pltpu.VMEM((nb, ML_H, LANES, ML_DV), F32), pltpu.VMEM((nb, ML_H, 1, LANES), F32),
                        pltpu.VMEM((nb, ML_H, 1, LANES), F32)],
        compiler_params=_cparams(("arbitrary", "arbitrary")),
        name="mlstm",
    )(r3(mq), r3(mk), r3(mv), r3(mo), r3(gates), bias_row, norm_row)
    return out.reshape(batch * seq, wide)


def _swa_kernel(q_ref, kc_ref, kp_ref, vc_ref, vp_ref, sink_ref, o_ref):
    w = WINDOW
    n = pl.program_id(1)
    scale = SWA_D ** -0.5
    qi = _iota2((w, w), 0)
    kj = _iota2((w, w), 1)
    mask_c = kj <= qi
    mask_p = jnp.logical_and(kj > qi, n > 0)
    grp = SWA_H // SWA_KV
    neg = -1e30
    units = [(bb, h) for bb in range(q_ref.shape[0]) for h in range(SWA_H)]
    scores = []
    half_of_lane = _iota2((w, LANES), 1) // SWA_D
    for bb, h in units:
        g = h // grp
        pair = q_ref[bb, :, (h // 2) * LANES:(h // 2 + 1) * LANES]
        q = jnp.where(half_of_lane == h % 2, pair, jnp.zeros_like(pair))
        scores.append((_dot_nt(q, kc_ref[bb, :, g * LANES:(g + 1) * LANES]),
                       _dot_nt(q, kp_ref[bb, :, g * LANES:(g + 1) * LANES])))
    masked, tops, exps, dens, probs = [], [], [], [], {}
    for sc, sp in scores:
        masked.append((jnp.where(mask_c, sc * scale, neg), jnp.where(mask_p, sp * scale, neg)))
    for (bb, h), (s_c, s_p) in zip(units, masked):
        tops.append(jnp.maximum(jnp.max(jnp.maximum(s_c, s_p), axis=-1, keepdims=True), sink_ref[:, h:h + 1]))
    for (s_c, s_p), m in zip(masked, tops):
        exps.append((jnp.where(mask_c, jnp.exp(s_c - m), 0.0), jnp.where(mask_p, jnp.exp(s_p - m), 0.0)))
    ones_b = jnp.ones((w, LANES), BF16)
    for (bb, h), (p_c, p_p), m in zip(units, exps, tops):
        p_c, p_p = p_c.astype(BF16), p_p.astype(BF16)
        probs[bb, h] = (p_c, p_p)
        dens.append(_dot(p_c, ones_b) + _dot(p_p, ones_b) + jnp.exp(sink_ref[:, h:h + 1] - m))
    inv = {u: 1.0 / den for u, den in zip(units, dens)}
    for bb in range(q_ref.shape[0]):
        for pair in range(SWA_H // 2):
            acc = None
            for sub in range(2):
                h = 2 * pair + sub
                vcol = (2 * (h // grp) + sub) * LANES
                p_c, p_p = probs[bb, h]
                part = (_dot(p_c, vc_ref[bb, :, vcol:vcol + LANES]) + _dot(p_p, vp_ref[bb, :, vcol:vcol + LANES])) * inv[bb, h]
                acc = part if acc is None else acc + part
            o_ref[bb, :, pair * LANES:(pair + 1) * LANES] = acc.astype(o_ref.dtype)


def _swa(sq, sk, sv, sinks_row, batch, seq):
    w = WINDOW
    nb = seq // w
    ns = min(SWA_SEQS_PER_STEP, batch)
    wo = SWA_H * SWA_D
    cur = lambda b, n: (b, n, 0)
    prev = lambda b, n: (b, jnp.maximum(n - 1, 0), 0)
    r3 = lambda a: a.reshape(batch, seq, a.shape[-1])
    q3, k3, v3 = r3(sq), r3(sk), r3(sv)
    out = pl.pallas_call(
        _swa_kernel,
        grid=(batch // ns, nb),
        in_specs=[pl.BlockSpec((ns, w, sq.shape[1]), cur),
                  pl.BlockSpec((ns, w, sk.shape[1]), cur), pl.BlockSpec((ns, w, sk.shape[1]), prev),
                  pl.BlockSpec((ns, w, sv.shape[1]), cur), pl.BlockSpec((ns, w, sv.shape[1]), prev),
                  pl.BlockSpec((1, LANES), lambda b, n: (0, 0))],
        out_specs=pl.BlockSpec((ns, w, wo), cur),
        out_shape=jax.ShapeDtypeStruct((batch, seq, wo), BF16),
        compiler_params=_cparams(("arbitrary", "arbitrary")),
        name="swa",
    )(q3, k3, k3, v3, v3, sinks_row)
    return out.reshape(batch * seq, wo)


def _layer_norm(h, g, b):
    mu = jnp.mean(h, axis=-1, keepdims=True)
    d = h - mu
    var = jnp.mean(d * d, axis=-1, keepdims=True)
    return d * lax.rsqrt(var + LN_EPS) * g + b


def _outproj_kernel(x_ref, a1_ref, a2_ref, w_ref, g_ref, b_ref, o_ref, op_ref):
    k1 = a1_ref.shape[1]
    y = _dot(a1_ref[...].astype(BF16), w_ref[0:k1, :]) + _dot(a2_ref[...].astype(BF16), w_ref[k1:, :])
    h = _layer_norm(DN_ALPHA * x_ref[...] + y, g_ref[...], b_ref[...])
    o_ref[...] = h
    op_ref[...] = _pack_pairs(h)


def _outproj_ln(x, a1, a2, w, g, b, tm=512):
    t, d = x.shape
    row = lambda i: (i, 0)
    fix = lambda i: (0, 0)
    return pl.pallas_call(
        _outproj_kernel,
        grid=(t // tm,),
        in_specs=[pl.BlockSpec((tm, d), row), pl.BlockSpec((tm, a1.shape[1]), row), pl.BlockSpec((tm, a2.shape[1]), row),
                  pl.BlockSpec(w.shape, fix), pl.BlockSpec((1, d), fix), pl.BlockSpec((1, d), fix)],
        out_specs=[pl.BlockSpec((tm, d), row), pl.BlockSpec((tm, d // 2), row)],
        out_shape=[jax.ShapeDtypeStruct((t, d), F32), jax.ShapeDtypeStruct((t, d // 2), jnp.uint32)],
        compiler_params=_cparams(("arbitrary",)),
        name="outproj_ln",
    )(x, a1, a2, w, g, b)


def _first_index(x, m, iota_f, sentinel):
    return jnp.min(jnp.where(x == m, iota_f, sentinel), axis=0, keepdims=True)


def _router_kernel(x_ref, wt_ref, bias_ref, idx_ref, gate_ref, rank_ref, cnt_ref, carry_ref):
    tm = x_ref.shape[0]
    e = N_EXPERTS
    gs = e // N_GROUPS
    ninf = -jnp.inf

    @pl.when(pl.program_id(0) == 0)
    def _():
        carry_ref[...] = jnp.zeros(carry_ref.shape, F32)

    logits = _dot3(wt_ref[...], x_ref[...], _dot_nt)
    scores = _sigmoid(logits)
    sel = scores + bias_ref[:, 0:1]

    sub_f = _iota2((gs, tm), 0).astype(F32)
    gscore = []
    for g in range(N_GROUPS):
        blk = sel[g * gs:(g + 1) * gs, :]
        m1 = jnp.max(blk, axis=0, keepdims=True)
        i1 = _first_index(blk, m1, sub_f, float(gs))
        m2 = jnp.max(jnp.where(sub_f == i1, ninf, blk), axis=0, keepdims=True)
        gscore.append(m1 + m2)
    gsc = jnp.concatenate(gscore, axis=0)
    grp_f = _iota2((N_GROUPS, tm), 0).astype(F32)
    gmask = jnp.zeros((N_GROUPS, tm), F32)
    for _ in range(TOPK_GROUPS):
        m = jnp.max(gsc, axis=0, keepdims=True)
        gi = _first_index(gsc, m, grp_f, float(N_GROUPS))
        hit = grp_f == gi
        gmask = jnp.where(hit, 1.0, gmask)
        gsc = jnp.where(hit, ninf, gsc)
    masked = jnp.concatenate(
        [jnp.where(gmask[g:g + 1, :] > 0.0, sel[g * gs:(g + 1) * gs, :], ninf) for g in range(N_GROUPS)], axis=0)

    exp_f = _iota2((e, tm), 0).astype(F32)
    chosen = jnp.zeros((e, tm), F32)
    idxs, gates = [], []
    for _ in range(TOP_K):
        m = jnp.max(masked, axis=0, keepdims=True)
        ei = _first_index(masked, m, exp_f, float(e))
        hit = exp_f == ei
        idxs.append(ei)
        gates.append(jnp.sum(jnp.where(hit, scores, 0.0), axis=0, keepdims=True))
        chosen = jnp.where(hit, 1.0, chosen)
        masked = jnp.where(hit, ninf, masked)
    gate = jnp.concatenate(gates, axis=0)
    gate = gate / jnp.sum(gate, axis=0, keepdims=True) * ROUTED_SCALE
    idx_f = jnp.concatenate(idxs, axis=0)

    upper = (_iota2((tm, tm), 0) < _iota2((tm, tm), 1)).astype(BF16)
    before = _dot(chosen.astype(BF16), upper) + carry_ref[...][:, 0:1]
    ranks = [jnp.sum(jnp.where(exp_f == idxs[k], before, 0.0), axis=0, keepdims=True) for k in range(TOP_K)]
    carry_ref[...] = carry_ref[...] + jnp.sum(chosen, axis=1, keepdims=True)

    idx_ref[...] = idx_f.astype(jnp.int32)
    gate_ref[...] = gate
    rank_ref[...] = jnp.concatenate(ranks, axis=0).astype(jnp.int32)
    cnt_ref[...] = carry_ref[...]


def _router(x, wt, bias_col, tm=512):
    t, d = x.shape
    col = lambda i: (0, i)
    fix = lambda i: (0, 0)
    return pl.pallas_call(
        _router_kernel,
        grid=(t // tm,),
        in_specs=[pl.BlockSpec((tm, d), lambda i: (i, 0)), pl.BlockSpec(wt.shape, fix), pl.BlockSpec((N_EXPERTS, LANES), fix)],
        out_specs=[pl.BlockSpec((TOP_K, tm), col), pl.BlockSpec((TOP_K, tm), col), pl.BlockSpec((TOP_K, tm), col),
                   pl.BlockSpec((N_EXPERTS, LANES), fix)],
        out_shape=[jax.ShapeDtypeStruct((TOP_K, t), jnp.int32), jax.ShapeDtypeStruct((TOP_K, t), F32),
                   jax.ShapeDtypeStruct((TOP_K, t), jnp.int32), jax.ShapeDtypeStruct((N_EXPERTS, LANES), F32)],
        scratch_shapes=[pltpu.VMEM((N_EXPERTS, LANES), F32)],
        compiler_params=_cparams(("arbitrary",)),
        name="router",
    )(x, wt, bias_col)


def _dest_kernel(idx_ref, rank_ref, start_ref, dest_ref):
    tm = idx_ref.shape[1]
    exp_i = _iota2((N_EXPERTS, tm), 0)
    start = start_ref[:, 0:1]
    rows = [jnp.sum(jnp.where(exp_i == idx_ref[s:s + 1, :], start, 0.0), axis=0, keepdims=True) for s in range(TOP_K)]
    dest_ref[...] = jnp.concatenate(rows, axis=0).astype(jnp.int32) + rank_ref[...]


def _dest_rows(idx, rank, start_col, tm=2048):
    t = idx.shape[1]
    tm = min(tm, t)
    col = lambda i: (0, i)
    return pl.pallas_call(
        _dest_kernel,
        grid=(t // tm,),
        in_specs=[pl.BlockSpec((TOP_K, tm), col), pl.BlockSpec((TOP_K, tm), col),
                  pl.BlockSpec((N_EXPERTS, LANES), lambda i: (0, 0))],
        out_specs=pl.BlockSpec((TOP_K, tm), col),
        out_shape=jax.ShapeDtypeStruct((TOP_K, t), jnp.int32),
        compiler_params=_cparams(("arbitrary",)),
        name="moe_dest",
    )(idx, rank, start_col)


def _pack_pairs(x):
    n = x.shape[1] // 2
    hi = lax.bitcast_convert_type(x[:, :n].astype(BF16).astype(F32), jnp.uint32)
    lo = lax.bitcast_convert_type(x[:, n:].astype(BF16).astype(F32), jnp.uint32)
    return hi | (lo >> 16)


def _unpack_pairs(w):
    hi = lax.bitcast_convert_type(w & jnp.uint32(0xFFFF0000), F32)
    lo = lax.bitcast_convert_type(w << 16, F32)
    return hi, lo


def _sc_scatter_rows(xp, dest, rows, chunk=LANES):
    t, width = xp.shape
    info = plsc.get_sparse_core_info()
    ncores, nsub = info.num_cores, info.num_subcores
    per_worker = t // (ncores * nsub)
    nchunk = per_worker // chunk
    mesh = plsc.VectorSubcoreMesh(core_axis_name="c", subcore_axis_name="s")

    @functools.partial(
        pl.kernel, mesh=mesh,
        out_type=jax.ShapeDtypeStruct((rows, width), xp.dtype),
        scratch_types=[pltpu.VMEM((TOP_K, chunk), jnp.int32), pltpu.VMEM((chunk, width), xp.dtype), pltpu.SemaphoreType.DMA],
    )
    def scatter(xp_hbm, dest_hbm, out_hbm, idx_v, rows_v, sem):
        base = (lax.axis_index("s") * ncores + lax.axis_index("c")) * per_worker

        @pl.loop(0, nchunk)
        def _(i):
            off = pl.multiple_of(base + i * chunk, chunk)
            pltpu.sync_copy(dest_hbm.at[:, pl.ds(off, chunk)], idx_v)
            pltpu.sync_copy(xp_hbm.at[pl.ds(off, chunk)], rows_v)
            copies = [pltpu.async_copy(rows_v, out_hbm.at[idx_v.at[s]], sem) for s in range(TOP_K)]
            for cp in copies:
                cp.wait()

    return scatter(xp, dest)


def _experts_kernel(be_ref, nu_ref, nv_ref, first_ref, slot_ref, nxt_ref, xs_ref, wg_hbm, wu_hbm, wd_hbm, ys_ref,
                    wgf_ref, wuf_ref, wdf_ref, wgb_ref, wub_ref, wdb_ref, sem, *, layer):
    i = pl.program_id(0)

    def fetch(e, s):
        return [pltpu.make_async_copy(wg_hbm.at[layer, e], wgf_ref.at[s], sem.at[s]),
                pltpu.make_async_copy(wu_hbm.at[layer, e], wuf_ref.at[s], sem.at[s]),
                pltpu.make_async_copy(wd_hbm.at[layer, e], wdf_ref.at[s], sem.at[s])]

    @pl.when(i == 0)
    def _():
        for cp in fetch(be_ref[0], 0):
            cp.start()

    @pl.when(jnp.logical_and(first_ref[i] == 1, i < nu_ref[0]))
    def _():
        s = slot_ref[i]
        for cp in fetch(be_ref[i], s):
            cp.wait()
        wgb_ref[...] = wgf_ref[s].astype(BF16)
        wub_ref[...] = wuf_ref[s].astype(BF16)
        wdb_ref[...] = wdf_ref[s].astype(BF16)

        @pl.when(nxt_ref[i] >= 0)
        def _():
            for cp in fetch(nxt_ref[i], 1 - s):
                cp.start()

    @pl.when(i < nu_ref[0])
    def _():
        half = xs_ref.shape[1]
        sub = xs_ref.shape[0] // EXPERT_SUBBLOCKS
        acts = []
        for r in range(EXPERT_SUBBLOCKS):
            rows = pl.ds(r * sub, sub)
            live = (_iota2((sub, 1), 0) + r * sub) < nv_ref[i]
            xa, xb = _unpack_pairs(jnp.where(live, xs_ref[rows, :], jnp.uint32(0)))
            x = jnp.concatenate([xa.astype(BF16), xb.astype(BF16)], axis=1)
            acts.append((_dot(x, wgb_ref[...]), _dot(x, wub_ref[...])))
        outs = [_dot((_silu(gate) * up).astype(BF16), wdb_ref[...]) for gate, up in acts]
        for r, y in enumerate(outs):
            ys_ref[pl.ds(r * sub, sub), :] = _pack_pairs(y)


def _experts(block_e, n_used, n_valid, xs, wg, wu, wd, layer):
    rows, half = xs.shape
    d = 2 * half
    nb = rows // EXPERT_BLOCK
    pos = jnp.arange(nb, dtype=jnp.int32)
    first = jnp.concatenate([jnp.ones((1,), jnp.int32), (block_e[1:] != block_e[:-1]).astype(jnp.int32)])
    slot = (jnp.cumsum(first) - 1) % 2
    later = (pos[None, :] > pos[:, None]) & (block_e[None, :] != block_e[:, None]) & (pos[None, :] < n_used[0])
    nxt_pos = jnp.min(jnp.where(later, pos[None, :], nb), axis=1)
    nxt = jnp.where(nxt_pos < nb, block_e[jnp.minimum(nxt_pos, nb - 1)], -1)
    blk = lambda i, be, nu, *rest: (jnp.minimum(i, nu[0] - 1), 0)
    hbm = pl.BlockSpec(memory_space=pl.ANY)
    return pl.pallas_call(
        functools.partial(_experts_kernel, layer=layer),
        grid_spec=pltpu.PrefetchScalarGridSpec(
            num_scalar_prefetch=6,
            grid=(nb,),
            in_specs=[pl.BlockSpec((EXPERT_BLOCK, half), blk), hbm, hbm, hbm],
            out_specs=pl.BlockSpec((EXPERT_BLOCK, half), blk),
            scratch_shapes=[pltpu.VMEM((2, d, D_EXPERT), F32), pltpu.VMEM((2, d, D_EXPERT), F32),
                            pltpu.VMEM((2, D_EXPERT, d), F32),
                            pltpu.VMEM((d, D_EXPERT), BF16), pltpu.VMEM((d, D_EXPERT), BF16),
                            pltpu.VMEM((D_EXPERT, d), BF16), pltpu.SemaphoreType.DMA((2,))],
        ),
        out_shape=jax.ShapeDtypeStruct((rows, half), jnp.uint32),
        compiler_params=_cparams(("arbitrary",)),
        name="moe_experts",
    )(block_e, n_used, n_valid, first, slot.astype(jnp.int32), nxt.astype(jnp.int32), xs, wg, wu, wd)


def _sc_gather_rows(table, idx, chunk=SC_CHUNK):
    n = idx.shape[0]
    width = table.shape[1]
    info = plsc.get_sparse_core_info()
    ncores, nsub = info.num_cores, info.num_subcores
    per_worker = n // (ncores * nsub)
    nchunk = per_worker // chunk
    mesh = plsc.VectorSubcoreMesh(core_axis_name="c", subcore_axis_name="s")

    @functools.partial(
        pl.kernel, mesh=mesh,
        out_type=jax.ShapeDtypeStruct((n, width), table.dtype),
        scratch_types=[pltpu.VMEM((nchunk, chunk), jnp.int32), pltpu.VMEM((2, chunk, width), table.dtype),
                       pltpu.SemaphoreType.DMA((2,)), pltpu.SemaphoreType.DMA((2,))],
    )
    def gather(table_hbm, idx_hbm, out_hbm, idx_v, rows_v, gsem, wsem):
        wid = lax.axis_index("s") * ncores + lax.axis_index("c")
        base = wid * per_worker
        pltpu.sync_copy(idx_hbm.at[pl.ds(wid * nchunk, nchunk)], idx_v)

        def fetch(j, b):
            return pltpu.make_async_copy(table_hbm.at[idx_v.at[j]], rows_v.at[b], gsem.at[b])

        def flush(j, b):
            off = pl.multiple_of(base + j * chunk, chunk)
            return pltpu.make_async_copy(rows_v.at[b], out_hbm.at[pl.ds(off, chunk)], wsem.at[b])

        fetch(0, 0).start()

        @pl.loop(0, nchunk, step=2)
        def _(i):
            for b in range(2):
                j = i + b
                fetch(j, b).wait()

                @pl.when(j + 1 < nchunk)
                def _():
                    @pl.when(j >= 1)
                    def _():
                        flush(j - 1, 1 - b).wait()

                    fetch(j + 1, 1 - b).start()

                flush(j, b).start()

        flush(nchunk - 2, 0).wait()
        flush(nchunk - 1, 1).wait()

    return gather(table, idx.reshape(n // chunk, chunk))


def _shared_kernel(xp_ref, sg_ref, su_ref, sd_ref, o_ref):
    xa, xb = _unpack_pairs(xp_ref[...])
    x = jnp.concatenate([xa.astype(BF16), xb.astype(BF16)], axis=1)
    hs = _silu(_dot(x, sg_ref[...])) * _dot(x, su_ref[...])
    o_ref[...] = _pack_pairs(_dot(hs.astype(BF16), sd_ref[...]))


def _shared_expert(xp, sg, su, sd, tm=512):
    t, half = xp.shape
    row = lambda i: (i, 0)
    fix = lambda i: (0, 0)
    return pl.pallas_call(
        _shared_kernel,
        grid=(t // tm,),
        in_specs=[pl.BlockSpec((tm, half), row), pl.BlockSpec(sg.shape, fix), pl.BlockSpec(su.shape, fix),
                  pl.BlockSpec(sd.shape, fix)],
        out_specs=pl.BlockSpec((tm, half), row),
        out_shape=jax.ShapeDtypeStruct((t, half), jnp.uint32),
        compiler_params=_cparams(("arbitrary",)),
        name="moe_shared",
    )(xp, sg, su, sd)


def _combine_kernel(x_ref, gate_ref, rows_ref, sh_ref, g_ref, b_ref, o_ref):
    gate = gate_ref[...]
    ya, yb = _unpack_pairs(sh_ref[...])
    for s in range(TOP_K):
        a, b = _unpack_pairs(rows_ref[s])
        ya = ya + gate[:, s:s + 1] * a
        yb = yb + gate[:, s:s + 1] * b
    ff = jnp.concatenate([ya, yb], axis=1)
    o_ref[...] = _layer_norm(DN_ALPHA * x_ref[...] + ff, g_ref[...], b_ref[...])


def _combine(x, gate_t, rows, shared, g, b, tm=512):
    t, d = x.shape
    row = lambda i: (i, 0)
    fix = lambda i: (0, 0)
    return pl.pallas_call(
        _combine_kernel,
        grid=(t // tm,),
        in_specs=[pl.BlockSpec((tm, d), row), pl.BlockSpec((tm, TOP_K), row),
                  pl.BlockSpec((TOP_K, tm, d // 2), lambda i: (0, i, 0)), pl.BlockSpec((tm, d // 2), row),
                  pl.BlockSpec((1, d), fix), pl.BlockSpec((1, d), fix)],
        out_specs=pl.BlockSpec((tm, d), row),
        out_shape=jax.ShapeDtypeStruct((t, d), F32),
        compiler_params=_cparams(("arbitrary",)),
        name="moe_combine",
    )(x, gate_t, rows, shared, g, b)


def _take_cols(w, idx):
    idx = np.asarray(idx)
    runs, start = [], 0
    for pos in range(1, len(idx) + 1):
        run_ends = pos == len(idx) or (idx[pos] != idx[pos - 1] + 1 if idx[pos - 1] >= 0 else idx[pos] >= 0)
        if run_ends:
            runs.append((start, int(idx[start]), pos - start))
            start = pos

    def body(w_ref, o_ref):
        for dst, src, width in runs:
            if src < 0:
                o_ref[:, dst:dst + width] = jnp.zeros((o_ref.shape[0], width), o_ref.dtype)
            else:
                o_ref[:, dst:dst + width] = w_ref[:, src:src + width].astype(o_ref.dtype)

    rows = w.shape[0]
    tr = min(rows, 256)
    return pl.pallas_call(
        body,
        grid=(rows // tr,),
        in_specs=[pl.BlockSpec((tr, w.shape[1]), lambda i: (i, 0))],
        out_specs=pl.BlockSpec((tr, len(idx)), lambda i: (i, 0)),
        out_shape=jax.ShapeDtypeStruct((rows, len(idx)), BF16),
        compiler_params=_cparams(("arbitrary",)),
        name="weight_cols",
    )(w)


def _pad_lane_row(v, first_lane, width=LANES):
    out = jnp.zeros((1, width), F32)
    return lax.dynamic_update_slice(out, v.reshape(1, -1).astype(F32), (0, first_lane))


def _even_in_cols():
    z = lambda n: -np.ones(n, int)
    kr0 = Q_LORA + KV_LORA
    half = MLA_ROPE // 2
    cols = [np.arange(0, Q_LORA), np.arange(Q_LORA, Q_LORA + KV_LORA),
            z(64), np.arange(kr0, kr0 + MLA_ROPE), z(32),
            z(64), np.arange(kr0 + half, kr0 + MLA_ROPE), np.arange(kr0, kr0 + half), z(32)]
    g0 = kr0 + MLA_ROPE
    nqk = GDN_H * GDN_DK
    cols.append(np.arange(g0, g0 + 3 * nqk))
    zoff = g0 + 3 * nqk + 2 * GDN_H
    cols.append(np.arange(zoff, zoff + GDN_H * GDN_DV))
    cols += [np.arange(g0 + 3 * nqk, g0 + 3 * nqk + 2 * GDN_H), z(LANES - 2 * GDN_H)]
    return np.concatenate(cols)


EV_WIDTHS = (Q_LORA + KV_LORA + 2 * LANES, 3 * GDN_H * GDN_DK, GDN_H * GDN_DV, LANES)


def _mla_q_cols():
    per = MLA_NOPE + MLA_ROPE
    half = MLA_ROPE // 2
    main, sw = [], []
    for h in range(MLA_H):
        b = h * per
        main += [np.arange(b, b + per), -np.ones(LANES - per, int)]
        sw += [-np.ones(MLA_NOPE, int), np.arange(b + MLA_NOPE + half, b + per), np.arange(b + MLA_NOPE, b + MLA_NOPE + half),
               -np.ones(LANES - per, int)]
    return np.concatenate(main + sw)


def _mla_kv_cols():
    per = MLA_NOPE + MLA_V
    kc, vc = [], []
    for h in range(MLA_H):
        b = h * per
        kc += [np.arange(b, b + MLA_NOPE), -np.ones(LANES - MLA_NOPE, int)]
        vv = np.arange(b + MLA_NOPE, b + per)
        pad = -np.ones(LANES - MLA_V, int)
        vc += [vv, pad] if h % 2 == 0 else [pad, vv]
    return np.concatenate(kc + vc)


def _odd_in_cols():
    z = lambda n: -np.ones(n, int)
    o = 0
    cols = []
    mq0, mk0 = 0, ML_H * ML_DK
    for base in (mq0, mk0):
        for h in range(ML_H):
            cols += [np.arange(base + h * ML_DK, base + (h + 1) * ML_DK), z(LANES - ML_DK)]
    mv0 = 2 * ML_H * ML_DK
    cols.append(np.arange(mv0, mv0 + ML_H * ML_DV))
    mi0 = mv0 + ML_H * ML_DV
    mo0 = mi0 + 2 * ML_H
    cols.append(np.arange(mo0, mo0 + ML_H * ML_DV))
    cols += [np.arange(mi0, mi0 + 2 * ML_H), z(LANES - 2 * ML_H)]
    sq0 = mo0 + ML_H * ML_DV
    sk0 = sq0 + SWA_H * SWA_D
    sv0 = sk0 + SWA_KV * SWA_D
    half = SWA_D // 2

    def heads(base, n, swapped, copies):
        out = []
        for h in range(n):
            b = base + h * SWA_D
            one = [np.arange(b + half, b + SWA_D), np.arange(b, b + half)] if swapped else [np.arange(b, b + SWA_D)]
            out += one * copies
        return out

    cols += (heads(sq0, SWA_H, False, 1) + heads(sq0, SWA_H, True, 1)
             + heads(sk0, SWA_KV, False, 2) + heads(sk0, SWA_KV, True, 2))
    for g in range(SWA_KV):
        vv = np.arange(sv0 + g * SWA_D, sv0 + (g + 1) * SWA_D)
        cols += [vv, z(LANES - SWA_D), z(LANES - SWA_D), vv]
    return np.concatenate(cols)


def _even_mixer(x, tabs, w_in, q_norm, w_qb, kv_norm, w_kvb, conv_w, a_log, dt_bias, o_norm, batch, seq):
    ctab, stab = tabs
    w = _take_cols(w_in, _even_in_cols()).astype(BF16)
    mla_in, act, z, gates = _proj_even(x, w, conv_w, seq)
    wq2 = _take_cols(w_qb, _mla_q_cols()).astype(BF16)
    wkv2 = _take_cols(w_kvb, _mla_kv_cols()).astype(BF16)
    q, k, v = _mla_prep(mla_in, ctab, stab, q_norm.reshape(1, -1), kv_norm.reshape(1, -1), wq2, wkv2)
    o_a = _mla_attn(q, k, v, batch, seq)
    o_b = _gdn(act, gates, z, _pad_lane_row(a_log, GDN_H), _pad_lane_row(dt_bias, GDN_H),
               o_norm.reshape(1, -1), batch, seq)
    return o_a, o_b


def _odd_mixer(x, tabs, w_in, b_i, b_f, ml_norm, sinks, batch, seq):
    ctab, stab = tabs
    w = _take_cols(w_in, _odd_in_cols()).astype(BF16)
    mq, mk, mv, mo, mg, sq, sk, sv = _proj_odd(x, w, ctab, stab)
    bias_row = _pad_lane_row(jnp.concatenate([b_i, b_f]), 0)
    o_c = _mlstm(mq, mk, mv, mo, mg, bias_row, ml_norm.reshape(1, -1), batch, seq)
    o_d = _swa(sq, sk, sv, _pad_lane_row(sinks, 0), batch, seq)
    return o_c, o_d


def _moe(x, xp, router_w, router_b, w_gate, w_up, w_down, layer, s_gate, s_up, s_down, ln_g, ln_b):
    t, d = x.shape
    bias_col = jnp.broadcast_to(router_b.reshape(-1, 1).astype(F32), (N_EXPERTS, LANES))
    idx, gate, rank, cnt = _router(x, router_w.T, bias_col)
    counts = cnt[:, 0].astype(jnp.int32)
    padded = (counts + EXPERT_BLOCK - 1) // EXPERT_BLOCK * EXPERT_BLOCK
    pad_end = jnp.cumsum(padded)
    pad_start = pad_end - padded
    start_col = jnp.broadcast_to(pad_start.astype(F32).reshape(-1, 1), (N_EXPERTS, LANES))
    dest = _dest_rows(idx, rank, start_col)
    n_blocks = t * TOP_K // EXPERT_BLOCK + N_EXPERTS
    rows = n_blocks * EXPERT_BLOCK
    block_row = jnp.arange(n_blocks, dtype=jnp.int32) * EXPERT_BLOCK
    block_e = jnp.minimum(jnp.sum((pad_end[None, :] <= block_row[:, None]).astype(jnp.int32), axis=1), N_EXPERTS - 1)
    n_used = (pad_end[-1:] // EXPERT_BLOCK).astype(jnp.int32)
    live_end = jnp.sum(jnp.where(block_e[:, None] == jnp.arange(N_EXPERTS, dtype=jnp.int32)[None, :],
                                 (pad_start + counts)[None, :], 0), axis=1)
    n_valid = jnp.clip(live_end - block_row, 0, EXPERT_BLOCK).astype(jnp.int32)
    xs = _sc_scatter_rows(xp, dest, rows)
    ys = _experts(block_e, n_used, n_valid, xs, w_gate, w_up, w_down, layer)
    picked = _sc_gather_rows(ys, dest.reshape(-1)).reshape(TOP_K, t, d // 2)
    shared = _shared_expert(xp, s_gate.astype(BF16), s_up.astype(BF16), s_down.astype(BF16))
    return _combine(x, gate.T, picked, shared, ln_g.reshape(1, -1), ln_b.reshape(1, -1))


def kernel(x, positions, ev_w_in, mla_q_norm, mla_w_qb, mla_kv_norm, mla_w_kvb, gdn_conv, gdn_a_log, gdn_dt_bias, gdn_norm, ev_w_out, od_w_in, mlstm_b_i, mlstm_b_f, mlstm_norm, swa_sinks, od_w_out, ln1_g, ln1_b, router_w, router_b, moe_w_gate, moe_w_up, moe_w_down, shared_w_gate, shared_w_up, shared_w_down, ln2_g, ln2_b):
    batch, seq, d = x.shape
    t = batch * seq
    pos = positions.reshape(t, 1).astype(F32)
    tabs_m = _rope_tables(pos, _rope_rows(MLA_ROPE, MLA_NOPE, MLA_NOPE))
    tabs_s = _rope_tables(pos, _rope_rows(SWA_D, 0, 0, heads=LANES // SWA_D))
    h = x.reshape(t, d)
    for layer in range(DEPTH):
        j = layer // 2
        if layer % 2 == 0:
            a1, a2 = _even_mixer(h, tabs_m, ev_w_in[j], mla_q_norm[j], mla_w_qb[j], mla_kv_norm[j], mla_w_kvb[j],
                                 gdn_conv[j], gdn_a_log[j], gdn_dt_bias[j], gdn_norm[j], batch, seq)
            w_out = ev_w_out[j]
        else:
            a1, a2 = _odd_mixer(h, tabs_s, od_w_in[j], mlstm_b_i[j], mlstm_b_f[j], mlstm_norm[j], swa_sinks[j], batch, seq)
            w_out = od_w_out[j]
        h, hp = _outproj_ln(h, a1, a2, w_out.astype(BF16), ln1_g[layer].reshape(1, -1), ln1_b[layer].reshape(1, -1))
        h = _moe(h, hp, router_w[layer], router_b[layer], moe_w_gate, moe_w_up, moe_w_down, layer,
                 shared_w_gate[layer], shared_w_up[layer], shared_w_down[layer], ln2_g[layer], ln2_b[layer])
    return h.reshape(batch, seq, d)
```

```python
import functools
import math

import numpy as np
import jax
import jax.numpy as jnp
from jax import lax
from jax.experimental import pallas as pl
from jax.experimental.pallas import tpu as pltpu
from jax.experimental.pallas import tpu_sc as plsc

F32 = jnp.float32
BF16 = jnp.bfloat16
HI = lax.Precision.HIGHEST

D_MODEL = 1024
DEPTH = 4
ROPE_THETA = 10000.0
EPS = 1e-6
LN_EPS = 1e-5
MLA_H, MLA_NOPE, MLA_ROPE, MLA_V = 8, 64, 32, 64
Q_LORA, KV_LORA = 256, 128
GDN_H, GDN_DK, GDN_DV, CONV_W, GDN_CHUNK = 4, 128, 128, 4, 64
ML_H, ML_DK, ML_DV, ML_CHUNK = 4, 64, 128, 64
SWA_H, SWA_KV, SWA_D, WINDOW = 8, 2, 64, 128
N_EXPERTS, N_GROUPS, TOPK_GROUPS, TOP_K = 64, 8, 4, 8
D_EXPERT, D_SHARED = 256, 256
ROUTED_SCALE = 2.5
DN_ALPHA = (2 * DEPTH) ** 0.25

LANES = 128
V7X_VMEM_BYTES = 64 * 1024 * 1024
VMEM_LIMIT = 48 * 1024 * 1024

EXPERT_BLOCK_MIN = 256
EXPERT_BLOCK_MAX = 1024
STREAMS = 2
EXPERT_SUBBLOCKS = 2
SWA_SEQS_PER_STEP = 4
MLSTM_SEQS_PER_STEP = 2
GDN_SEQS_PER_STEP = 8
SC_CHUNK = 64


def _cparams(sem, vmem=VMEM_LIMIT):
    return pltpu.CompilerParams(dimension_semantics=sem, vmem_limit_bytes=vmem)


def _dot(a, b, precision=None):
    return jnp.dot(a, b, preferred_element_type=F32, precision=precision)


def _dot_nt(a, b, precision=None):
    return lax.dot_general(a, b, (((1,), (1,)), ((), ())), preferred_element_type=F32, precision=precision)


def _dot_tn(a, b, precision=None):
    return lax.dot_general(a, b, (((0,), (0,)), ((), ())), preferred_element_type=F32, precision=precision)


def _split2(a):
    hi = a.astype(BF16)
    lo = (a - hi.astype(F32)).astype(BF16)
    return hi, lo


def _split3(a):
    p1 = a.astype(BF16)
    r = a - p1.astype(F32)
    p2 = r.astype(BF16)
    p3 = (r - p2.astype(F32)).astype(BF16)
    return p1, p2, p3


def _dot3(a, b, dot=_dot):
    ah, al = _split2(a)
    bh, bl = _split2(b)
    return dot(ah, bh) + (dot(ah, bl) + dot(al, bh))


def _dot_sel(sel, b, dot=_dot):
    sel = sel.astype(BF16)
    p1, p2, p3 = _split3(b)
    return dot(sel, p1) + (dot(sel, p2) + dot(sel, p3))


def _sigmoid(x):
    return 1.0 / (1.0 + jnp.exp(-x))


def _softplus(x):
    return jnp.maximum(x, 0.0) + jnp.log(1.0 + jnp.exp(-jnp.abs(x)))


def _silu(x):
    return x * _sigmoid(x)


def _lane_bcast(x, c):
    return jnp.broadcast_to(x[:, c:c + 1], x.shape)


def _iota2(shape, dim):
    return lax.broadcasted_iota(jnp.int32, shape, dim)


def _rope_kernel(pos_ref, rows_ref, c_ref, s_ref):
    ang = pos_ref[...] * rows_ref[0:1, :]
    c_ref[...] = rows_ref[1:2, :] * jnp.cos(ang) + rows_ref[2:3, :]
    s_ref[...] = rows_ref[3:4, :] * jnp.sin(ang)


def _rope_tables(pos, rows, tm=512):
    t = pos.shape[0]
    return pl.pallas_call(
        _rope_kernel,
        grid=(t // tm,),
        in_specs=[pl.BlockSpec((tm, 1), lambda i: (i, 0)), pl.BlockSpec((8, LANES), lambda i: (0, 0))],
        out_specs=[pl.BlockSpec((tm, LANES), lambda i: (i, 0))] * 2,
        out_shape=[jax.ShapeDtypeStruct((t, LANES), F32)] * 2,
        compiler_params=_cparams(("arbitrary",)),
        name="rope_tables",
    )(pos, rows)


def _rope_rows(dim, first_lane, pad_one_lanes, heads=1):
    half = dim // 2
    inv = ROPE_THETA ** (-(np.arange(0, dim, 2, dtype=np.float32) / dim))
    rows = np.zeros((8, LANES), np.float32)
    for h in range(heads):
        lo = slice(first_lane + h * dim, first_lane + h * dim + half)
        hi = slice(first_lane + h * dim + half, first_lane + (h + 1) * dim)
        rows[0, lo] = inv
        rows[0, hi] = inv
        rows[1, lo] = 1.0
        rows[1, hi] = 1.0
        rows[3, lo] = -1.0
        rows[3, hi] = 1.0
    rows[2, :pad_one_lanes] = 1.0
    return jnp.asarray(rows)


def _proj_kernel(x_ref, w_ref, *out_refs, offsets):
    xb = x_ref[...].astype(BF16)
    for o_ref, (a, b) in zip(out_refs, offsets):
        o_ref[...] = _dot(xb, w_ref[:, a:b]).astype(o_ref.dtype)


def _proj(x, w, widths, dtypes, tm=512):
    t, k = x.shape
    offs = np.concatenate([[0], np.cumsum(widths)]).tolist()
    offsets = tuple((offs[i], offs[i + 1]) for i in range(len(widths)))
    return pl.pallas_call(
        functools.partial(_proj_kernel, offsets=offsets),
        grid=(t // tm,),
        in_specs=[pl.BlockSpec((tm, k), lambda i: (i, 0)), pl.BlockSpec(w.shape, lambda i: (0, 0))],
        out_specs=[pl.BlockSpec((tm, n), lambda i: (i, 0)) for n in widths],
        out_shape=[jax.ShapeDtypeStruct((t, n), dt) for n, dt in zip(widths, dtypes)],
        compiler_params=_cparams(("arbitrary",)),
        name="in_proj",
    )(x, w)


def _proj_even_kernel(x_ref, w_ref, cw_ref, mla_ref, act_ref, z_ref, g_ref, ext_ref, *, tiles_per_seq):
    tm = x_ref.shape[0]
    o = np.concatenate([[0], np.cumsum(EV_WIDTHS)]).tolist()
    @pl.when(pl.program_id(0) % tiles_per_seq == 0)
    def _():
        ext_ref[0:8, :] = jnp.zeros((8, ext_ref.shape[1]), F32)

    xb = x_ref[...].astype(BF16)
    nchunk = 3
    cw = EV_WIDTHS[1] // nchunk

    def project(ci):
        ext_ref[8:8 + tm, ci * cw:(ci + 1) * cw] = _dot(xb, w_ref[:, o[1] + ci * cw:o[1] + (ci + 1) * cw])

    project(0)
    for ci in range(nchunk):
        if ci + 1 < nchunk:
            project(ci + 1)
        else:
            mla_ref[...] = _dot(xb, w_ref[:, o[0]:o[1]])
            z_ref[...] = _dot(xb, w_ref[:, o[2]:o[3]]).astype(z_ref.dtype)
            g_ref[...] = _dot(xb, w_ref[:, o[3]:o[4]])
        cols = slice(ci * cw, (ci + 1) * cw)
        conv = cw_ref[0:1, cols] * ext_ref[5:5 + tm, cols]
        for j in range(1, CONV_W):
            conv = conv + cw_ref[j:j + 1, cols] * ext_ref[5 + j:5 + j + tm, cols]
        act_ref[:, cols] = _silu(conv).astype(act_ref.dtype)
    ext_ref[0:8, :] = ext_ref[tm:tm + 8, :]


def _proj_even(x, w, conv_w, seq, tm=512):
    t, k = x.shape
    tm = min(tm, seq)
    row = lambda i: (i, 0)
    fix = lambda i: (0, 0)
    return pl.pallas_call(
        functools.partial(_proj_even_kernel, tiles_per_seq=seq // tm),
        grid=(t // tm,),
        in_specs=[pl.BlockSpec((tm, k), row), pl.BlockSpec(w.shape, fix), pl.BlockSpec(conv_w.shape, fix)],
        out_specs=[pl.BlockSpec((tm, n), row) for n in EV_WIDTHS],
        out_shape=[jax.ShapeDtypeStruct((t, n), F32) for n in EV_WIDTHS],
        scratch_shapes=[pltpu.VMEM((tm + 8, EV_WIDTHS[1]), F32)],
        compiler_params=_cparams(("arbitrary",)),
        name="in_proj",
    )(x, w, conv_w)


OD_SEG = dict(mq=(0, 512), mk=(512, 1024), mv=(1024, 1536), mo=(1536, 2048), gates=(2048, 2176),
              sq=(2176, 2688), sqsw=(2688, 3200), sk=(3200, 3456), sksw=(3456, 3712), sv=(3712, 4224))
OD_COLS = 4224


def _proj_odd_kernel(x_ref, w_ref, c_ref, s_ref, mq_ref, mk_ref, mv_ref, mo_ref, mg_ref, sq_ref, sk_ref, sv_ref):
    xb = x_ref[...].astype(BF16)

    def seg(name):
        a, b = OD_SEG[name]
        return _dot(xb, w_ref[:, a:b])

    mq_ref[...] = seg("mq").astype(mq_ref.dtype)
    mk_ref[...] = seg("mk").astype(mk_ref.dtype)
    mv_ref[...] = seg("mv").astype(mv_ref.dtype)
    mo_ref[...] = seg("mo").astype(mo_ref.dtype)
    mg_ref[...] = seg("gates")
    c = c_ref[...]
    s = s_ref[...]
    c8 = jnp.concatenate([c] * (SWA_H // 2), axis=1)
    s8 = jnp.concatenate([s] * (SWA_H // 2), axis=1)
    sq_ref[...] = (seg("sq") * c8 + seg("sqsw") * s8).astype(sq_ref.dtype)
    c2 = jnp.concatenate([c] * SWA_KV, axis=1)
    s2 = jnp.concatenate([s] * SWA_KV, axis=1)
    sk_ref[...] = (seg("sk") * c2 + seg("sksw") * s2).astype(sk_ref.dtype)
    sv_ref[...] = seg("sv").astype(sv_ref.dtype)


def _proj_odd(x, w, ctab, stab, tm=512):
    t, k = x.shape
    widths = (512, 512, 512, 512, 128, SWA_H * SWA_D, SWA_KV * LANES, 2 * SWA_KV * LANES)
    dtypes = (F32, F32, F32, F32, F32, BF16, BF16, BF16)
    return pl.pallas_call(
        _proj_odd_kernel,
        grid=(t // tm,),
        in_specs=[pl.BlockSpec((tm, k), lambda i: (i, 0)), pl.BlockSpec(w.shape, lambda i: (0, 0)),
                  pl.BlockSpec((tm, LANES), lambda i: (i, 0)), pl.BlockSpec((tm, LANES), lambda i: (i, 0))],
        out_specs=[pl.BlockSpec((tm, n), lambda i: (i, 0)) for n in widths],
        out_shape=[jax.ShapeDtypeStruct((t, n), dt) for n, dt in zip(widths, dtypes)],
        compiler_params=_cparams(("arbitrary",)),
        name="in_proj_odd",
    )(x, w, ctab, stab)


def _rms(x, g):
    return x * lax.rsqrt(jnp.mean(x * x, axis=-1, keepdims=True) + EPS) * g


def _mla_prep_kernel(in_ref, c_ref, s_ref, qn_ref, kvn_ref, wq_ref, wkv_ref, q_ref, k_ref, v_ref):
    hw = MLA_H * LANES
    c = c_ref[...]
    s = s_ref[...]
    c8 = jnp.concatenate([c] * MLA_H, axis=1)
    s8 = jnp.concatenate([s] * MLA_H, axis=1)
    cqn = _rms(in_ref[:, 0:Q_LORA], qn_ref[...]).astype(BF16)
    qq = _dot(cqn, wq_ref[...])
    scale = (MLA_NOPE + MLA_ROPE) ** -0.5
    q_ref[...] = ((qq[:, :hw] * c8 + qq[:, hw:] * s8) * scale).astype(q_ref.dtype)
    ckvn = _rms(in_ref[:, Q_LORA:Q_LORA + KV_LORA], kvn_ref[...]).astype(BF16)
    kv = _dot(ckvn, wkv_ref[...])
    o = Q_LORA + KV_LORA
    krr = in_ref[:, o:o + LANES] * c + in_ref[:, o + LANES:o + 2 * LANES] * s
    k_ref[...] = (kv[:, :hw] + jnp.concatenate([krr] * MLA_H, axis=1)).astype(k_ref.dtype)
    v_ref[...] = kv[:, hw:].astype(v_ref.dtype)


def _mla_prep(mla_in, ctab, stab, qn, kvn, wq2, wkv2, tm=512):
    t = mla_in.shape[0]
    hw = MLA_H * LANES
    row = lambda i: (i, 0)
    fix = lambda i: (0, 0)
    return pl.pallas_call(
        _mla_prep_kernel,
        grid=(t // tm,),
        in_specs=[pl.BlockSpec((tm, mla_in.shape[1]), row), pl.BlockSpec((tm, LANES), row), pl.BlockSpec((tm, LANES), row),
                  pl.BlockSpec(qn.shape, fix), pl.BlockSpec(kvn.shape, fix),
                  pl.BlockSpec(wq2.shape, fix), pl.BlockSpec(wkv2.shape, fix)],
        out_specs=[pl.BlockSpec((tm, hw), row)] * 3,
        out_shape=[jax.ShapeDtypeStruct((t, hw), BF16)] * 3,
        compiler_params=_cparams(("arbitrary",)),
        name="mla_prep",
    )(mla_in, ctab, stab, qn, kvn, wq2, wkv2)


def _mla_attn_kernel(q_ref, k_ref, v_ref, o_ref, *, tq):
    i = pl.program_id(2)
    neg = -1e30
    lane = _iota2((tq, LANES), 1)
    ones_lane = (MLA_V, 0)

    def chunk(j, carry, masked):
        start = pl.multiple_of(j * tq, tq)
        out = []
        for hh in range(2):
            m, acc = carry[hh]
            q = q_ref[:, hh * LANES:(hh + 1) * LANES]
            kc = k_ref[pl.ds(start, tq), hh * LANES:(hh + 1) * LANES]
            vc = v_ref[pl.ds(start, tq), hh * LANES:(hh + 1) * LANES]
            vc = jnp.where(lane == ones_lane[hh], jnp.ones_like(vc), vc)
            s = _dot_nt(q, kc)
            if masked:
                s = jnp.where(_iota2(s.shape, 0) >= _iota2(s.shape, 1), s, neg)
            m_new = jnp.maximum(m, jnp.max(s, axis=-1, keepdims=True))
            alpha = jnp.exp(m - m_new)
            p = jnp.exp(s - m_new)
            acc = alpha * acc + _dot(p.astype(BF16), vc)
            out.append((m_new, acc))
        return tuple(out)

    one = (jnp.full((tq, 1), neg, F32), jnp.zeros((tq, LANES), F32))
    carry = lax.fori_loop(0, i, lambda j, c: chunk(j, c, False), (one, one))
    (_, acc0), (_, acc1) = chunk(i, carry, True)
    o0 = acc0 / _lane_bcast(acc0, ones_lane[0])
    o1 = acc1 / _lane_bcast(acc1, ones_lane[1])
    o_ref[...] = jnp.where(lane < MLA_V, o0, o1).astype(o_ref.dtype)


def _mla_attn(q, k, v, batch, seq, tq=512):
    tq = min(tq, seq)
    nq = seq // tq
    pairs = MLA_H // 2
    return pl.pallas_call(
        functools.partial(_mla_attn_kernel, tq=tq),
        grid=(batch, pairs, nq),
        in_specs=[pl.BlockSpec((tq, 2 * LANES), lambda b, p, i: (b * nq + i, p)),
                  pl.BlockSpec((seq, 2 * LANES), lambda b, p, i: (b, p)),
                  pl.BlockSpec((seq, 2 * LANES), lambda b, p, i: (b, p))],
        out_specs=pl.BlockSpec((tq, LANES), lambda b, p, i: (b * nq + i, p)),
        out_shape=jax.ShapeDtypeStruct((batch * seq, pairs * LANES), BF16),
        compiler_params=_cparams(("arbitrary", "arbitrary", "arbitrary")),
        name="mla_attn",
    )(q, k, v)


def _unit_lower_inverse_many(ns):
    c = ns[0].shape[0]
    eye = (_iota2((c, c), 0) == _iota2((c, c), 1)).astype(F32)
    xs = [-n for n in ns]
    ps = [eye + x for x in xs]
    xb = [x.astype(BF16) for x in xs]
    for _ in range(int(math.log2(c)) - 1):
        xs = [_dot(b, b) for b in xb]
        xb = [x.astype(BF16) for x in xs]
        ps = [p + _dot(p.astype(BF16), b) for p, b in zip(ps, xb)]
    return ps


def _gdn_kernel(act_ref, g_ref, z_ref, al_ref, dt_ref, on_ref, o_ref, st_ref):
    c = GDN_CHUNK
    hd = GDN_DK
    nqk = GDN_H * GDN_DK

    @pl.when(pl.program_id(1) == 0)
    def _():
        st_ref[...] = jnp.zeros(st_ref.shape, F32)

    tri = (_iota2((c, c), 0) >= _iota2((c, c), 1)).astype(F32)
    row_ge = _iota2((c, c), 0) >= _iota2((c, c), 1)
    row_gt = _iota2((c, c), 0) > _iota2((c, c), 1)
    lane = _iota2((c, LANES), 1)

    units = []
    for bb in range(act_ref.shape[0]):
        act = act_ref[bb].astype(F32)
        gates = g_ref[bb]
        beta_all = _sigmoid(gates)
        g_all = -jnp.exp(al_ref[...]) * _softplus(gates + dt_ref[...])
        gc_all = _dot_sel(tri, g_all)
        gc_parts = _split3(gc_all)
        for h in range(GDN_H):
            q = act[:, h * hd:(h + 1) * hd]
            k = act[:, nqk + h * hd:nqk + (h + 1) * hd]
            v = act[:, 2 * nqk + h * GDN_DV:2 * nqk + (h + 1) * GDN_DV]
            q = q * lax.rsqrt(jnp.sum(q * q, axis=-1, keepdims=True) + EPS) * (GDN_DK ** -0.5)
            k = k * lax.rsqrt(jnp.sum(k * k, axis=-1, keepdims=True) + EPS)
            beta = _lane_bcast(beta_all, h)
            gcol = _lane_bcast(gc_all, GDN_H + h)
            pick = (lane == GDN_H + h).astype(BF16)
            grow = _dot_nt(pick, gc_parts[0]) + (_dot_nt(pick, gc_parts[1]) + _dot_nt(pick, gc_parts[2]))
            decay = jnp.exp(jnp.where(row_ge, gcol[:, :c] - grow, -jnp.inf))
            kb = k * beta
            lower = jnp.where(row_gt, _dot3(kb, k, _dot_nt) * decay, 0.0)
            eg = jnp.exp(gcol)
            glast = gcol[c - 1:c, :]
            units.append(dict(bb=bb, h=h, lower=lower, rhs=jnp.concatenate([v * beta, kb * eg], axis=1),
                              attn=_dot_nt(q.astype(BF16), k.astype(BF16)) * decay, qg=(q * eg).astype(BF16),
                              kg=(k * jnp.exp(glast - gcol)).astype(BF16), gl=jnp.exp(glast)))

    tinvs = _unit_lower_inverse_many([u["lower"] for u in units])
    uws = []
    for u, tinv in zip(units, tinvs):
        uws.append(_dot(tinv.astype(BF16), u["rhs"].astype(BF16)))
    states = [st_ref[u["bb"], u["h"]] for u in units]
    sbs = [s.astype(BF16) for s in states]
    vnews = [(uw[:, :GDN_DV] - _dot(uw[:, GDN_DV:].astype(BF16), sb)).astype(BF16) for uw, sb in zip(uws, sbs)]
    for u, state, sb, vnb in zip(units, states, sbs, vnews):
        bb, h = u["bb"], u["h"]
        o = _dot(u["qg"], sb) + _dot(u["attn"].astype(BF16), vnb)
        st_ref[bb, h] = state * u["gl"] + _dot_tn(u["kg"], vnb)
        o = _rms(o, on_ref[...]) * _silu(z_ref[bb, :, h * GDN_DV:(h + 1) * GDN_DV].astype(F32))
        o_ref[bb, :, h * GDN_DV:(h + 1) * GDN_DV] = o.astype(o_ref.dtype)


def _gdn(act, gates, z, a_row, dt_row, o_norm, batch, seq):
    c = GDN_CHUNK
    nc = seq // c
    w3 = act.shape[1]
    wo = GDN_H * GDN_DV
    nb = min(GDN_SEQS_PER_STEP, batch)
    row = lambda b, i: (b, i, 0)
    fix = lambda b, i: (0, 0)
    out = pl.pallas_call(
        _gdn_kernel,
        grid=(batch // nb, nc),
        in_specs=[pl.BlockSpec((nb, c, w3), row), pl.BlockSpec((nb, c, LANES), row), pl.BlockSpec((nb, c, wo), row),
                  pl.BlockSpec((1, LANES), fix), pl.BlockSpec((1, LANES), fix), pl.BlockSpec((1, GDN_DV), fix)],
        out_specs=pl.BlockSpec((nb, c, wo), row),
        out_shape=jax.ShapeDtypeStruct((batch, seq, wo), BF16),
        scratch_shapes=[pltpu.VMEM((nb, GDN_H, GDN_DK, GDN_DV), F32)],
        compiler_params=_cparams(("arbitrary", "arbitrary")),
        name="gdn",
    )(act.reshape(batch, seq, w3), gates.reshape(batch, seq, LANES), z.reshape(batch, seq, wo), a_row, dt_row, o_norm)
    return out.reshape(batch * seq, wo)


def _mlstm_kernel(q_ref, k_ref, v_ref, og_ref, g_ref, bias_ref, nrm_ref, o_ref, c_ref, n_ref, m_ref):
    @pl.when(pl.program_id(1) == 0)
    def _():
        c_ref[...] = jnp.zeros(c_ref.shape, F32)
        n_ref[...] = jnp.zeros(n_ref.shape, F32)
        m_ref[...] = jnp.zeros(m_ref.shape, F32)

    c = ML_CHUNK
    tri = (_iota2((c, c), 0) >= _iota2((c, c), 1)).astype(F32)
    row_ge = _iota2((c, c), 0) >= _iota2((c, c), 1)
    ones = jnp.ones((c, LANES), F32)
    lane = _iota2((c, LANES), 1)

    units = []
    for bb in range(q_ref.shape[0]):
        pre = g_ref[bb] + bias_ref[...]
        logf = jnp.minimum(pre, 0.0) - jnp.log(1.0 + jnp.exp(-jnp.abs(pre)))
        bcum_all = _dot_sel(tri, logf)
        for h in range(ML_H):
            q = q_ref[bb, :, h * LANES:(h + 1) * LANES].astype(F32)
            k = k_ref[bb, :, h * LANES:(h + 1) * LANES].astype(F32) * (ML_DK ** -0.5)
            units.append(dict(bb=bb, h=h, q=q, k=k, qb=q.astype(BF16), vb=v_ref[bb, :, h * ML_DV:(h + 1) * ML_DV].astype(BF16),
                              bcol=_lane_bcast(bcum_all, ML_H + h),
                              icol=_lane_bcast(pre, h),
                              col=jnp.where(lane == h, pre, 0.0) - jnp.where(lane == ML_H + h, bcum_all, 0.0),
                              m_st=m_ref[bb, h], cst=c_ref[bb, h], nst=n_ref[bb, h]))
    for u in units:
        u["row"] = _dot_sel(ones, u["col"], _dot_nt)
        u["qk"] = _dot_nt(u["qb"], u["k"].astype(BF16))
        u["qc"] = _dot(u["qb"], u["cst"].astype(BF16))
    for u in units:
        u["d"] = jnp.where(row_ge, u["bcol"][:, :c] + u["row"], -jnp.inf)
        u["inter"] = u["bcol"] + u["m_st"]
        u["m_t"] = jnp.maximum(u["inter"], jnp.max(u["d"], axis=-1, keepdims=True))
        u["b_end"] = u["bcol"][c - 1:c, :]
        u["a"] = u["b_end"] - u["bcol"] + u["icol"]
        u["m_new"] = jnp.maximum(u["b_end"] + u["m_st"], jnp.max(u["a"], axis=0, keepdims=True))
    for u in units:
        u["w_inter"] = jnp.exp(u["inter"] - u["m_t"])
        u["p"] = jnp.exp(u["d"] - u["m_t"][:, :c]) * u["qk"]
        u["keep"] = jnp.exp(u["b_end"] + u["m_st"] - u["m_new"])
        u["ks"] = u["k"] * jnp.exp(u["a"] - u["m_new"])
    for u in units:
        u["pv"] = _dot(u["p"].astype(BF16), u["vb"])
        u["kv"] = _dot_tn(u["ks"].astype(BF16), u["vb"])
    for u in units:
        u["den"] = (u["w_inter"] * jnp.sum(u["q"] * u["nst"], axis=-1, keepdims=True)
                    + jnp.sum(u["p"], axis=-1, keepdims=True))
    for u in units:
        bb, h = u["bb"], u["h"]
        num = u["w_inter"] * u["qc"] + u["pv"]
        hc = num / jnp.maximum(jnp.abs(u["den"]), jnp.exp(-u["m_t"]))
        c_ref[bb, h] = u["cst"] * u["keep"] + u["kv"]
        n_ref[bb, h] = u["nst"] * u["keep"] + jnp.sum(u["ks"], axis=0, keepdims=True)
        m_ref[bb, h] = u["m_new"]
        hn = (_rms(hc, nrm_ref[:, h * ML_DV:(h + 1) * ML_DV])
              * _sigmoid(og_ref[bb, :, h * ML_DV:(h + 1) * ML_DV].astype(F32)))
        o_ref[bb, :, h * ML_DV:(h + 1) * ML_DV] = hn.astype(o_ref.dtype)


def _mlstm(mq, mk, mv, mo, gates, bias_row, norm_row, batch, seq):
    c = ML_CHUNK
    nc = seq // c
    nb = min(MLSTM_SEQS_PER_STEP, batch)
    row = lambda b, i: (b, i, 0)
    fix = lambda b, i: (0, 0)
    wide = ML_H * LANES
    r3 = lambda a: a.reshape(batch, seq, a.shape[-1])
    out = pl.pallas_call(
        _mlstm_kernel,
        grid=(batch // nb, nc),
        in_specs=[pl.BlockSpec((nb, c, wide), row), pl.BlockSpec((nb, c, wide), row), pl.BlockSpec((nb, c, wide), row),
                  pl.BlockSpec((nb, c, wide), row), pl.BlockSpec((nb, c, LANES), row),
                  pl.BlockSpec((1, LANES), fix), pl.BlockSpec((1, wide), fix)],
        out_specs=pl.BlockSpec((nb, c, wide), row),
        out_shape=jax.ShapeDtypeStruct((batch, seq, wide), BF16),
        scratch_shapes=[pltpu.VMEM((nb, ML_H, LANES, ML_DV), F32), pltpu.VMEM((nb, ML_H, 1, LANES), F32),
                        pltpu.VMEM((nb, ML_H, 1, LANES), F32)],
        compiler_params=_cparams(("arbitrary", "arbitrary")),
        name="mlstm",
    )(r3(mq), r3(mk), r3(mv), r3(mo), r3(gates), bias_row, norm_row)
    return out.reshape(batch * seq, wide)


def _swa_kernel(q_ref, kc_ref, kp_ref, vc_ref, vp_ref, sink_ref, o_ref):
    w = WINDOW
    n = pl.program_id(1)
    scale = SWA_D ** -0.5
    qi = _iota2((w, w), 0)
    kj = _iota2((w, w), 1)
    mask_c = kj <= qi
    mask_p = jnp.logical_and(kj > qi, n > 0)
    grp = SWA_H // SWA_KV
    neg = -1e30
    units = [(bb, h) for bb in range(q_ref.shape[0]) for h in range(SWA_H)]
    scores = []
    half_of_lane = _iota2((w, LANES), 1) // SWA_D
    for bb, h in units:
        g = h // grp
        pair = q_ref[bb, :, (h // 2) * LANES:(h // 2 + 1) * LANES]
        q = jnp.where(half_of_lane == h % 2, pair, jnp.zeros_like(pair))
        scores.append((_dot_nt(q, kc_ref[bb, :, g * LANES:(g + 1) * LANES]),
                       _dot_nt(q, kp_ref[bb, :, g * LANES:(g + 1) * LANES])))
    masked, tops, exps, dens, probs = [], [], [], [], {}
    for sc, sp in scores:
        masked.append((jnp.where(mask_c, sc * scale, neg), jnp.where(mask_p, sp * scale, neg)))
    for (bb, h), (s_c, s_p) in zip(units, masked):
        tops.append(jnp.maximum(jnp.max(jnp.maximum(s_c, s_p), axis=-1, keepdims=True), sink_ref[:, h:h + 1]))
    for (s_c, s_p), m in zip(masked, tops):
        exps.append((jnp.where(mask_c, jnp.exp(s_c - m), 0.0), jnp.where(mask_p, jnp.exp(s_p - m), 0.0)))
    ones_b = jnp.ones((w, LANES), BF16)
    for (bb, h), (p_c, p_p), m in zip(units, exps, tops):
        p_c, p_p = p_c.astype(BF16), p_p.astype(BF16)
        probs[bb, h] = (p_c, p_p)
        dens.append(_dot(p_c, ones_b) + _dot(p_p, ones_b) + jnp.exp(sink_ref[:, h:h + 1] - m))
    inv = {u: 1.0 / den for u, den in zip(units, dens)}
    for bb in range(q_ref.shape[0]):
        for pair in range(SWA_H // 2):
            acc = None
            for sub in range(2):
                h = 2 * pair + sub
                vcol = (2 * (h // grp) + sub) * LANES
                p_c, p_p = probs[bb, h]
                part = (_dot(p_c, vc_ref[bb, :, vcol:vcol + LANES]) + _dot(p_p, vp_ref[bb, :, vcol:vcol + LANES])) * inv[bb, h]
                acc = part if acc is None else acc + part
            o_ref[bb, :, pair * LANES:(pair + 1) * LANES] = acc.astype(o_ref.dtype)


def _swa(sq, sk, sv, sinks_row, batch, seq):
    w = WINDOW
    nb = seq // w
    ns = min(SWA_SEQS_PER_STEP, batch)
    wo = SWA_H * SWA_D
    cur = lambda b, n: (b, n, 0)
    prev = lambda b, n: (b, jnp.maximum(n - 1, 0), 0)
    r3 = lambda a: a.reshape(batch, seq, a.shape[-1])
    q3, k3, v3 = r3(sq), r3(sk), r3(sv)
    out = pl.pallas_call(
        _swa_kernel,
        grid=(batch // ns, nb),
        in_specs=[pl.BlockSpec((ns, w, sq.shape[1]), cur),
                  pl.BlockSpec((ns, w, sk.shape[1]), cur), pl.BlockSpec((ns, w, sk.shape[1]), prev),
                  pl.BlockSpec((ns, w, sv.shape[1]), cur), pl.BlockSpec((ns, w, sv.shape[1]), prev),
                  pl.BlockSpec((1, LANES), lambda b, n: (0, 0))],
        out_specs=pl.BlockSpec((ns, w, wo), cur),
        out_shape=jax.ShapeDtypeStruct((batch, seq, wo), BF16),
        compiler_params=_cparams(("arbitrary", "arbitrary")),
        name="swa",
    )(q3, k3, k3, v3, v3, sinks_row)
    return out.reshape(batch * seq, wo)


def _layer_norm(h, g, b):
    mu = jnp.mean(h, axis=-1, keepdims=True)
    d = h - mu
    var = jnp.mean(d * d, axis=-1, keepdims=True)
    return d * lax.rsqrt(var + LN_EPS) * g + b


def _outproj_kernel(x_ref, a1_ref, a2_ref, w_ref, g_ref, b_ref, o_ref, op_ref):
    k1 = a1_ref.shape[1]
    y = _dot(a1_ref[...].astype(BF16), w_ref[0:k1, :]) + _dot(a2_ref[...].astype(BF16), w_ref[k1:, :])
    h = _layer_norm(DN_ALPHA * x_ref[...] + y, g_ref[...], b_ref[...])
    o_ref[...] = h
    op_ref[...] = _pack_pairs(h)


def _outproj_ln(x, a1, a2, w, g, b, tm=512):
    t, d = x.shape
    row = lambda i: (i, 0)
    fix = lambda i: (0, 0)
    return pl.pallas_call(
        _outproj_kernel,
        grid=(t // tm,),
        in_specs=[pl.BlockSpec((tm, d), row), pl.BlockSpec((tm, a1.shape[1]), row), pl.BlockSpec((tm, a2.shape[1]), row),
                  pl.BlockSpec(w.shape, fix), pl.BlockSpec((1, d), fix), pl.BlockSpec((1, d), fix)],
        out_specs=[pl.BlockSpec((tm, d), row), pl.BlockSpec((tm, d // 2), row)],
        out_shape=[jax.ShapeDtypeStruct((t, d), F32), jax.ShapeDtypeStruct((t, d // 2), jnp.uint32)],
        compiler_params=_cparams(("arbitrary",)),
        name="outproj_ln",
    )(x, a1, a2, w, g, b)


def _first_index(x, m, iota_f, sentinel):
    return jnp.min(jnp.where(x == m, iota_f, sentinel), axis=0, keepdims=True)


def _router_kernel(x_ref, wt_ref, bias_ref, idx_ref, gate_ref, rank_ref, cnt_ref, carry_ref):
    tm = x_ref.shape[0]
    e = N_EXPERTS
    gs = e // N_GROUPS
    ninf = -jnp.inf

    @pl.when(pl.program_id(0) == 0)
    def _():
        carry_ref[...] = jnp.zeros(carry_ref.shape, F32)

    logits = _dot3(wt_ref[...], x_ref[...], _dot_nt)
    scores = _sigmoid(logits)
    sel = scores + bias_ref[:, 0:1]

    sub_f = _iota2((gs, tm), 0).astype(F32)
    gscore = []
    for g in range(N_GROUPS):
        blk = sel[g * gs:(g + 1) * gs, :]
        m1 = jnp.max(blk, axis=0, keepdims=True)
        i1 = _first_index(blk, m1, sub_f, float(gs))
        m2 = jnp.max(jnp.where(sub_f == i1, ninf, blk), axis=0, keepdims=True)
        gscore.append(m1 + m2)
    gsc = jnp.concatenate(gscore, axis=0)
    grp_f = _iota2((N_GROUPS, tm), 0).astype(F32)
    gmask = jnp.zeros((N_GROUPS, tm), F32)
    for _ in range(TOPK_GROUPS):
        m = jnp.max(gsc, axis=0, keepdims=True)
        gi = _first_index(gsc, m, grp_f, float(N_GROUPS))
        hit = grp_f == gi
        gmask = jnp.where(hit, 1.0, gmask)
        gsc = jnp.where(hit, ninf, gsc)
    masked = jnp.concatenate(
        [jnp.where(gmask[g:g + 1, :] > 0.0, sel[g * gs:(g + 1) * gs, :], ninf) for g in range(N_GROUPS)], axis=0)

    exp_f = _iota2((e, tm), 0).astype(F32)
    chosen = jnp.zeros((e, tm), F32)
    idxs, gates = [], []
    for _ in range(TOP_K):
        m = jnp.max(masked, axis=0, keepdims=True)
        ei = _first_index(masked, m, exp_f, float(e))
        hit = exp_f == ei
        idxs.append(ei)
        gates.append(jnp.sum(jnp.where(hit, scores, 0.0), axis=0, keepdims=True))
        chosen = jnp.where(hit, 1.0, chosen)
        masked = jnp.where(hit, ninf, masked)
    gate = jnp.concatenate(gates, axis=0)
    gate = gate / jnp.sum(gate, axis=0, keepdims=True) * ROUTED_SCALE
    idx_f = jnp.concatenate(idxs, axis=0)

    upper = (_iota2((tm, tm), 0) < _iota2((tm, tm), 1)).astype(BF16)
    before = _dot(chosen.astype(BF16), upper) + carry_ref[...][:, 0:1]
    ranks = [jnp.sum(jnp.where(exp_f == idxs[k], before, 0.0), axis=0, keepdims=True) for k in range(TOP_K)]
    carry_ref[...] = carry_ref[...] + jnp.sum(chosen, axis=1, keepdims=True)

    idx_ref[...] = idx_f.astype(jnp.int32)
    gate_ref[...] = gate
    rank_ref[...] = jnp.concatenate(ranks, axis=0).astype(jnp.int32)
    cnt_ref[...] = carry_ref[...]


def _router(x, wt, bias_col, tm=512):
    t, d = x.shape
    col = lambda i: (0, i)
    fix = lambda i: (0, 0)
    return pl.pallas_call(
        _router_kernel,
        grid=(t // tm,),
        in_specs=[pl.BlockSpec((tm, d), lambda i: (i, 0)), pl.BlockSpec(wt.shape, fix), pl.BlockSpec((N_EXPERTS, LANES), fix)],
        out_specs=[pl.BlockSpec((TOP_K, tm), col), pl.BlockSpec((TOP_K, tm), col), pl.BlockSpec((TOP_K, tm), col),
                   pl.BlockSpec((N_EXPERTS, LANES), fix)],
        out_shape=[jax.ShapeDtypeStruct((TOP_K, t), jnp.int32), jax.ShapeDtypeStruct((TOP_K, t), F32),
                   jax.ShapeDtypeStruct((TOP_K, t), jnp.int32), jax.ShapeDtypeStruct((N_EXPERTS, LANES), F32)],
        scratch_shapes=[pltpu.VMEM((N_EXPERTS, LANES), F32)],
        compiler_params=_cparams(("arbitrary",)),
        name="router",
    )(x, wt, bias_col)


def _dest_kernel(idx_ref, rank_ref, start_ref, dest_ref):
    tm = idx_ref.shape[1]
    exp_i = _iota2((N_EXPERTS, tm), 0)
    start = start_ref[:, 0:1]
    rows = [jnp.sum(jnp.where(exp_i == idx_ref[s:s + 1, :], start, 0.0), axis=0, keepdims=True) for s in range(TOP_K)]
    dest_ref[...] = jnp.concatenate(rows, axis=0).astype(jnp.int32) + rank_ref[...]


def _dest_rows(idx, rank, start_col, tm=2048):
    t = idx.shape[1]
    tm = min(tm, t)
    col = lambda i: (0, i)
    return pl.pallas_call(
        _dest_kernel,
        grid=(t // tm,),
        in_specs=[pl.BlockSpec((TOP_K, tm), col), pl.BlockSpec((TOP_K, tm), col),
                  pl.BlockSpec((N_EXPERTS, LANES), lambda i: (0, 0))],
        out_specs=pl.BlockSpec((TOP_K, tm), col),
        out_shape=jax.ShapeDtypeStruct((TOP_K, t), jnp.int32),
        compiler_params=_cparams(("arbitrary",)),
        name="moe_dest",
    )(idx, rank, start_col)


def _pack_pairs(x):
    n = x.shape[1] // 2
    hi = lax.bitcast_convert_type(x[:, :n].astype(BF16).astype(F32), jnp.uint32)
    lo = lax.bitcast_convert_type(x[:, n:].astype(BF16).astype(F32), jnp.uint32)
    return hi | (lo >> 16)


def _unpack_pairs(w):
    hi = lax.bitcast_convert_type(w & jnp.uint32(0xFFFF0000), F32)
    lo = lax.bitcast_convert_type(w << 16, F32)
    return hi, lo


def _sc_scatter_rows(xp, dest, rows, chunk=LANES):
    t, width = xp.shape
    info = plsc.get_sparse_core_info()
    ncores, nsub = info.num_cores, info.num_subcores
    per_worker = t // (ncores * nsub)
    nchunk = per_worker // chunk
    mesh = plsc.VectorSubcoreMesh(core_axis_name="c", subcore_axis_name="s")

    @functools.partial(
        pl.kernel, mesh=mesh,
        out_type=jax.ShapeDtypeStruct((rows, width), xp.dtype),
        scratch_types=[pltpu.VMEM((TOP_K, chunk), jnp.int32), pltpu.VMEM((chunk, width), xp.dtype), pltpu.SemaphoreType.DMA],
    )
    def scatter(xp_hbm, dest_hbm, out_hbm, idx_v, rows_v, sem):
        base = (lax.axis_index("s") * ncores + lax.axis_index("c")) * per_worker

        @pl.loop(0, nchunk)
        def _(i):
            off = pl.multiple_of(base + i * chunk, chunk)
            pltpu.sync_copy(dest_hbm.at[:, pl.ds(off, chunk)], idx_v)
            pltpu.sync_copy(xp_hbm.at[pl.ds(off, chunk)], rows_v)
            copies = [pltpu.async_copy(rows_v, out_hbm.at[idx_v.at[s]], sem) for s in range(TOP_K)]
            for cp in copies:
                cp.wait()

    return scatter(xp, dest)


def _experts_kernel(be_ref, nu_ref, nv_ref, first_ref, slot_ref, nxt_ref, xs_ref, wg_hbm, wu_hbm, wd_hbm, ys_ref,
                    wgf_ref, wuf_ref, wdf_ref, wgb_ref, wub_ref, wdb_ref, sem, *, layer):
    i = pl.program_id(0)

    def fetch(e, s):
        return [pltpu.make_async_copy(wg_hbm.at[layer, e], wgf_ref.at[s], sem.at[s]),
                pltpu.make_async_copy(wu_hbm.at[layer, e], wuf_ref.at[s], sem.at[s]),
                pltpu.make_async_copy(wd_hbm.at[layer, e], wdf_ref.at[s], sem.at[s])]

    @pl.when(i == 0)
    def _():
        for cp in fetch(be_ref[0], 0):
            cp.start()

    @pl.when(jnp.logical_and(first_ref[i] == 1, i < nu_ref[0]))
    def _():
        s = slot_ref[i]
        for cp in fetch(be_ref[i], s):
            cp.wait()
        wgb_ref[...] = wgf_ref[s].astype(BF16)
        wub_ref[...] = wuf_ref[s].astype(BF16)
        wdb_ref[...] = wdf_ref[s].astype(BF16)

        @pl.when(nxt_ref[i] >= 0)
        def _():
            for cp in fetch(nxt_ref[i], 1 - s):
                cp.start()

    @pl.when(i < nu_ref[0])
    def _():
        half = xs_ref.shape[1]
        sub = xs_ref.shape[0] // EXPERT_SUBBLOCKS
        acts = []
        for r in range(EXPERT_SUBBLOCKS):
            rows = pl.ds(r * sub, sub)
            live = (_iota2((sub, 1), 0) + r * sub) < nv_ref[i]
            xa, xb = _unpack_pairs(jnp.where(live, xs_ref[rows, :], jnp.uint32(0)))
            x = jnp.concatenate([xa.astype(BF16), xb.astype(BF16)], axis=1)
            acts.append((_dot(x, wgb_ref[...]), _dot(x, wub_ref[...])))
        outs = [_dot((_silu(gate) * up).astype(BF16), wdb_ref[...]) for gate, up in acts]
        for r, y in enumerate(outs):
            ys_ref[pl.ds(r * sub, sub), :] = _pack_pairs(y)


def _experts(block_e, n_used, n_valid, xs, wg, wu, wd, layer, block):
    rows, half = xs.shape
    d = 2 * half
    nb = rows // block
    pos = jnp.arange(nb, dtype=jnp.int32)
    first = jnp.concatenate([jnp.ones((1,), jnp.int32), (block_e[1:] != block_e[:-1]).astype(jnp.int32)])
    slot = (jnp.cumsum(first) - 1) % 2
    later = (pos[None, :] > pos[:, None]) & (block_e[None, :] != block_e[:, None]) & (pos[None, :] < n_used[0])
    nxt_pos = jnp.min(jnp.where(later, pos[None, :], nb), axis=1)
    nxt = jnp.where(nxt_pos < nb, block_e[jnp.minimum(nxt_pos, nb - 1)], -1)
    blk = lambda i, be, nu, *rest: (jnp.minimum(i, nu[0] - 1), 0)
    hbm = pl.BlockSpec(memory_space=pl.ANY)
    return pl.pallas_call(
        functools.partial(_experts_kernel, layer=layer),
        grid_spec=pltpu.PrefetchScalarGridSpec(
            num_scalar_prefetch=6,
            grid=(nb,),
            in_specs=[pl.BlockSpec((block, half), blk), hbm, hbm, hbm],
            out_specs=pl.BlockSpec((block, half), blk),
            scratch_shapes=[pltpu.VMEM((2, d, D_EXPERT), F32), pltpu.VMEM((2, d, D_EXPERT), F32),
                            pltpu.VMEM((2, D_EXPERT, d), F32),
                            pltpu.VMEM((d, D_EXPERT), BF16), pltpu.VMEM((d, D_EXPERT), BF16),
                            pltpu.VMEM((D_EXPERT, d), BF16), pltpu.SemaphoreType.DMA((2,))],
        ),
        out_shape=jax.ShapeDtypeStruct((rows, half), jnp.uint32),
        compiler_params=_cparams(("arbitrary",)),
        name="moe_experts",
    )(block_e, n_used, n_valid, first, slot.astype(jnp.int32), nxt.astype(jnp.int32), xs, wg, wu, wd)


def _sc_gather_rows(table, idx, chunk=SC_CHUNK):
    n = idx.shape[0]
    width = table.shape[1]
    info = plsc.get_sparse_core_info()
    ncores, nsub = info.num_cores, info.num_subcores
    per_worker = n // (ncores * nsub)
    nchunk = per_worker // chunk
    mesh = plsc.VectorSubcoreMesh(core_axis_name="c", subcore_axis_name="s")

    @functools.partial(
        pl.kernel, mesh=mesh,
        out_type=jax.ShapeDtypeStruct((n, width), table.dtype),
        scratch_types=[pltpu.VMEM((nchunk, chunk), jnp.int32), pltpu.VMEM((2, chunk, width), table.dtype),
                       pltpu.SemaphoreType.DMA((2,)), pltpu.SemaphoreType.DMA((2,))],
    )
    def gather(table_hbm, idx_hbm, out_hbm, idx_v, rows_v, gsem, wsem):
        wid = lax.axis_index("s") * ncores + lax.axis_index("c")
        base = wid * per_worker
        pltpu.sync_copy(idx_hbm.at[pl.ds(wid * nchunk, nchunk)], idx_v)

        def fetch(j, b):
            return pltpu.make_async_copy(table_hbm.at[idx_v.at[j]], rows_v.at[b], gsem.at[b])

        def flush(j, b):
            off = pl.multiple_of(base + j * chunk, chunk)
            return pltpu.make_async_copy(rows_v.at[b], out_hbm.at[pl.ds(off, chunk)], wsem.at[b])

        fetch(0, 0).start()

        @pl.loop(0, nchunk, step=2)
        def _(i):
            for b in range(2):
                j = i + b
                fetch(j, b).wait()

                @pl.when(j + 1 < nchunk)
                def _():
                    @pl.when(j >= 1)
                    def _():
                        flush(j - 1, 1 - b).wait()

                    fetch(j + 1, 1 - b).start()

                flush(j, b).start()

        flush(nchunk - 2, 0).wait()
        flush(nchunk - 1, 1).wait()

    return gather(table, idx.reshape(n // chunk, chunk))


def _shared_kernel(xp_ref, sg_ref, su_ref, sd_ref, o_ref):
    xa, xb = _unpack_pairs(xp_ref[...])
    x = jnp.concatenate([xa.astype(BF16), xb.astype(BF16)], axis=1)
    hs = _silu(_dot(x, sg_ref[...])) * _dot(x, su_ref[...])
    o_ref[...] = _pack_pairs(_dot(hs.astype(BF16), sd_ref[...]))


def _shared_expert(xp, sg, su, sd, tm=512):
    t, half = xp.shape
    row = lambda i: (i, 0)
    fix = lambda i: (0, 0)
    return pl.pallas_call(
        _shared_kernel,
        grid=(t // tm,),
        in_specs=[pl.BlockSpec((tm, half), row), pl.BlockSpec(sg.shape, fix), pl.BlockSpec(su.shape, fix),
                  pl.BlockSpec(sd.shape, fix)],
        out_specs=pl.BlockSpec((tm, half), row),
        out_shape=jax.ShapeDtypeStruct((t, half), jnp.uint32),
        compiler_params=_cparams(("arbitrary",)),
        name="moe_shared",
    )(xp, sg, su, sd)


def _combine_kernel(x_ref, gate_ref, rows_ref, sh_ref, g_ref, b_ref, o_ref):
    gate = gate_ref[...]
    ya, yb = _unpack_pairs(sh_ref[...])
    for s in range(TOP_K):
        a, b = _unpack_pairs(rows_ref[s])
        ya = ya + gate[:, s:s + 1] * a
        yb = yb + gate[:, s:s + 1] * b
    ff = jnp.concatenate([ya, yb], axis=1)
    o_ref[...] = _layer_norm(DN_ALPHA * x_ref[...] + ff, g_ref[...], b_ref[...])


def _combine(x, gate_t, rows, shared, g, b, tm=512):
    t, d = x.shape
    row = lambda i: (i, 0)
    fix = lambda i: (0, 0)
    return pl.pallas_call(
        _combine_kernel,
        grid=(t // tm,),
        in_specs=[pl.BlockSpec((tm, d), row), pl.BlockSpec((tm, TOP_K), row),
                  pl.BlockSpec((TOP_K, tm, d // 2), lambda i: (0, i, 0)), pl.BlockSpec((tm, d // 2), row),
                  pl.BlockSpec((1, d), fix), pl.BlockSpec((1, d), fix)],
        out_specs=pl.BlockSpec((tm, d), row),
        out_shape=jax.ShapeDtypeStruct((t, d), F32),
        compiler_params=_cparams(("arbitrary",)),
        name="moe_combine",
    )(x, gate_t, rows, shared, g, b)


def _take_cols(w, idx):
    idx = np.asarray(idx)
    runs, start = [], 0
    for pos in range(1, len(idx) + 1):
        run_ends = pos == len(idx) or (idx[pos] != idx[pos - 1] + 1 if idx[pos - 1] >= 0 else idx[pos] >= 0)
        if run_ends:
            runs.append((start, int(idx[start]), pos - start))
            start = pos

    def body(w_ref, o_ref):
        for dst, src, width in runs:
            if src < 0:
                o_ref[:, dst:dst + width] = jnp.zeros((o_ref.shape[0], width), o_ref.dtype)
            else:
                o_ref[:, dst:dst + width] = w_ref[:, src:src + width].astype(o_ref.dtype)

    rows = w.shape[0]
    tr = min(rows, 256)
    return pl.pallas_call(
        body,
        grid=(rows // tr,),
        in_specs=[pl.BlockSpec((tr, w.shape[1]), lambda i: (i, 0))],
        out_specs=pl.BlockSpec((tr, len(idx)), lambda i: (i, 0)),
        out_shape=jax.ShapeDtypeStruct((rows, len(idx)), BF16),
        compiler_params=_cparams(("arbitrary",)),
        name="weight_cols",
    )(w)


def _pad_lane_row(v, first_lane, width=LANES):
    out = jnp.zeros((1, width), F32)
    return lax.dynamic_update_slice(out, v.reshape(1, -1).astype(F32), (0, first_lane))


def _even_in_cols():
    z = lambda n: -np.ones(n, int)
    kr0 = Q_LORA + KV_LORA
    half = MLA_ROPE // 2
    cols = [np.arange(0, Q_LORA), np.arange(Q_LORA, Q_LORA + KV_LORA),
            z(64), np.arange(kr0, kr0 + MLA_ROPE), z(32),
            z(64), np.arange(kr0 + half, kr0 + MLA_ROPE), np.arange(kr0, kr0 + half), z(32)]
    g0 = kr0 + MLA_ROPE
    nqk = GDN_H * GDN_DK
    cols.append(np.arange(g0, g0 + 3 * nqk))
    zoff = g0 + 3 * nqk + 2 * GDN_H
    cols.append(np.arange(zoff, zoff + GDN_H * GDN_DV))
    cols += [np.arange(g0 + 3 * nqk, g0 + 3 * nqk + 2 * GDN_H), z(LANES - 2 * GDN_H)]
    return np.concatenate(cols)


EV_WIDTHS = (Q_LORA + KV_LORA + 2 * LANES, 3 * GDN_H * GDN_DK, GDN_H * GDN_DV, LANES)


def _mla_q_cols():
    per = MLA_NOPE + MLA_ROPE
    half = MLA_ROPE // 2
    main, sw = [], []
    for h in range(MLA_H):
        b = h * per
        main += [np.arange(b, b + per), -np.ones(LANES - per, int)]
        sw += [-np.ones(MLA_NOPE, int), np.arange(b + MLA_NOPE + half, b + per), np.arange(b + MLA_NOPE, b + MLA_NOPE + half),
               -np.ones(LANES - per, int)]
    return np.concatenate(main + sw)


def _mla_kv_cols():
    per = MLA_NOPE + MLA_V
    kc, vc = [], []
    for h in range(MLA_H):
        b = h * per
        kc += [np.arange(b, b + MLA_NOPE), -np.ones(LANES - MLA_NOPE, int)]
        vv = np.arange(b + MLA_NOPE, b + per)
        pad = -np.ones(LANES - MLA_V, int)
        vc += [vv, pad] if h % 2 == 0 else [pad, vv]
    return np.concatenate(kc + vc)


def _odd_in_cols():
    z = lambda n: -np.ones(n, int)
    o = 0
    cols = []
    mq0, mk0 = 0, ML_H * ML_DK
    for base in (mq0, mk0):
        for h in range(ML_H):
            cols += [np.arange(base + h * ML_DK, base + (h + 1) * ML_DK), z(LANES - ML_DK)]
    mv0 = 2 * ML_H * ML_DK
    cols.append(np.arange(mv0, mv0 + ML_H * ML_DV))
    mi0 = mv0 + ML_H * ML_DV
    mo0 = mi0 + 2 * ML_H
    cols.append(np.arange(mo0, mo0 + ML_H * ML_DV))
    cols += [np.arange(mi0, mi0 + 2 * ML_H), z(LANES - 2 * ML_H)]
    sq0 = mo0 + ML_H * ML_DV
    sk0 = sq0 + SWA_H * SWA_D
    sv0 = sk0 + SWA_KV * SWA_D
    half = SWA_D // 2

    def heads(base, n, swapped, copies):
        out = []
        for h in range(n):
            b = base + h * SWA_D
            one = [np.arange(b + half, b + SWA_D), np.arange(b, b + half)] if swapped else [np.arange(b, b + SWA_D)]
            out += one * copies
        return out

    cols += (heads(sq0, SWA_H, False, 1) + heads(sq0, SWA_H, True, 1)
             + heads(sk0, SWA_KV, False, 2) + heads(sk0, SWA_KV, True, 2))
    for g in range(SWA_KV):
        vv = np.arange(sv0 + g * SWA_D, sv0 + (g + 1) * SWA_D)
        cols += [vv, z(LANES - SWA_D), z(LANES - SWA_D), vv]
    return np.concatenate(cols)


def _even_weights(w_in, w_qb, w_kvb):
    return (_take_cols(w_in, _even_in_cols()), _take_cols(w_qb, _mla_q_cols()), _take_cols(w_kvb, _mla_kv_cols()))


def _even_mixer(x, tabs, weights, q_norm, kv_norm, conv_w, a_log, dt_bias, o_norm, batch, seq):
    ctab, stab = tabs
    w, wq2, wkv2 = weights
    mla_in, act, z, gates = _proj_even(x, w, conv_w, seq)
    q, k, v = _mla_prep(mla_in, ctab, stab, q_norm.reshape(1, -1), kv_norm.reshape(1, -1), wq2, wkv2)
    o_a = _mla_attn(q, k, v, batch, seq)
    o_b = _gdn(act, gates, z, _pad_lane_row(a_log, GDN_H), _pad_lane_row(dt_bias, GDN_H),
               o_norm.reshape(1, -1), batch, seq)
    return o_a, o_b


def _odd_mixer(x, tabs, w, b_i, b_f, ml_norm, sinks, batch, seq):
    ctab, stab = tabs
    mq, mk, mv, mo, mg, sq, sk, sv = _proj_odd(x, w, ctab, stab)
    bias_row = _pad_lane_row(jnp.concatenate([b_i, b_f]), 0)
    o_c = _mlstm(mq, mk, mv, mo, mg, bias_row, ml_norm.reshape(1, -1), batch, seq)
    o_d = _swa(sq, sk, sv, _pad_lane_row(sinks, 0), batch, seq)
    return o_c, o_d


def _moe(x, xp, router_w, router_b, w_gate, w_up, w_down, layer, s_gate, s_up, s_down, ln_g, ln_b):
    t, d = x.shape
    bias_col = jnp.broadcast_to(router_b.reshape(-1, 1).astype(F32), (N_EXPERTS, LANES))
    idx, gate, rank, cnt = _router(x, router_w.T, bias_col)
    counts = cnt[:, 0].astype(jnp.int32)
    block = int(min(max(pl.next_power_of_2(t * TOP_K // N_EXPERTS) // 2, EXPERT_BLOCK_MIN), EXPERT_BLOCK_MAX))
    padded = (counts + block - 1) // block * block
    pad_end = jnp.cumsum(padded)
    pad_start = pad_end - padded
    start_col = jnp.broadcast_to(pad_start.astype(F32).reshape(-1, 1), (N_EXPERTS, LANES))
    dest = _dest_rows(idx, rank, start_col)
    n_blocks = t * TOP_K // block + N_EXPERTS
    rows = n_blocks * block
    block_row = jnp.arange(n_blocks, dtype=jnp.int32) * block
    block_e = jnp.minimum(jnp.sum((pad_end[None, :] <= block_row[:, None]).astype(jnp.int32), axis=1), N_EXPERTS - 1)
    n_used = (pad_end[-1:] // block).astype(jnp.int32)
    live_end = jnp.sum(jnp.where(block_e[:, None] == jnp.arange(N_EXPERTS, dtype=jnp.int32)[None, :],
                                 (pad_start + counts)[None, :], 0), axis=1)
    n_valid = jnp.clip(live_end - block_row, 0, block).astype(jnp.int32)
    xs = _sc_scatter_rows(xp, dest, rows)
    ys = _experts(block_e, n_used, n_valid, xs, w_gate, w_up, w_down, layer, block)
    picked = _sc_gather_rows(ys, dest.reshape(-1)).reshape(TOP_K, t, d // 2)
    shared = _shared_expert(xp, s_gate.astype(BF16), s_up.astype(BF16), s_down.astype(BF16))
    return _combine(x, gate.T, picked, shared, ln_g.reshape(1, -1), ln_b.reshape(1, -1))


def kernel(x, positions, ev_w_in, mla_q_norm, mla_w_qb, mla_kv_norm, mla_w_kvb, gdn_conv, gdn_a_log, gdn_dt_bias, gdn_norm, ev_w_out, od_w_in, mlstm_b_i, mlstm_b_f, mlstm_norm, swa_sinks, od_w_out, ln1_g, ln1_b, router_w, router_b, moe_w_gate, moe_w_up, moe_w_down, shared_w_gate, shared_w_up, shared_w_down, ln2_g, ln2_b):
    batch, seq, d = x.shape
    streams = STREAMS if batch % STREAMS == 0 else 1
    sb = batch // streams
    ts = sb * seq
    hs, tabs_m, tabs_s = [], [], []
    for s in range(streams):
        pos = positions[s * sb:(s + 1) * sb].reshape(ts, 1).astype(F32)
        tabs_m.append(_rope_tables(pos, _rope_rows(MLA_ROPE, MLA_NOPE, MLA_NOPE)))
        tabs_s.append(_rope_tables(pos, _rope_rows(SWA_D, 0, 0, heads=LANES // SWA_D)))
        hs.append(x[s * sb:(s + 1) * sb].reshape(ts, d))
    for layer in range(DEPTH):
        j = layer // 2
        if layer % 2 == 0:
            weights = _even_weights(ev_w_in[j], mla_w_qb[j], mla_w_kvb[j])
            w_out = ev_w_out[j].astype(BF16)
        else:
            weights = _take_cols(od_w_in[j], _odd_in_cols())
            w_out = od_w_out[j].astype(BF16)
        for s in range(streams):
            h = hs[s]
            if layer % 2 == 0:
                a1, a2 = _even_mixer(h, tabs_m[s], weights, mla_q_norm[j], mla_kv_norm[j], gdn_conv[j], gdn_a_log[j],
                                     gdn_dt_bias[j], gdn_norm[j], sb, seq)
            else:
                a1, a2 = _odd_mixer(h, tabs_s[s], weights, mlstm_b_i[j], mlstm_b_f[j], mlstm_norm[j], swa_sinks[j], sb, seq)
            h, hp = _outproj_ln(h, a1, a2, w_out, ln1_g[layer].reshape(1, -1), ln1_b[layer].reshape(1, -1))
            hs[s] = _moe(h, hp, router_w[layer], router_b[layer], moe_w_gate, moe_w_up, moe_w_down, layer,
                         shared_w_gate[layer], shared_w_up[layer], shared_w_down[layer], ln2_g[layer], ln2_b[layer])
    return jnp.concatenate([h.reshape(sb, seq, d) for h in hs], axis=0)
```

```python
import functools
import math

import numpy as np
import jax
import jax.numpy as jnp
from jax import lax
from jax.experimental import pallas as pl
from jax.experimental.pallas import tpu as pltpu
from jax.experimental.pallas import tpu_sc as plsc

F32 = jnp.float32
BF16 = jnp.bfloat16
HI = lax.Precision.HIGHEST

D_MODEL = 1024
DEPTH = 4
ROPE_THETA = 10000.0
EPS = 1e-6
LN_EPS = 1e-5
MLA_H, MLA_NOPE, MLA_ROPE, MLA_V = 8, 64, 32, 64
Q_LORA, KV_LORA = 256, 128
GDN_H, GDN_DK, GDN_DV, CONV_W, GDN_CHUNK = 4, 128, 128, 4, 64
ML_H, ML_DK, ML_DV, ML_CHUNK = 4, 64, 128, 64
SWA_H, SWA_KV, SWA_D, WINDOW = 8, 2, 64, 128
N_EXPERTS, N_GROUPS, TOPK_GROUPS, TOP_K = 64, 8, 4, 8
D_EXPERT, D_SHARED = 256, 256
ROUTED_SCALE = 2.5
DN_ALPHA = (2 * DEPTH) ** 0.25

LANES = 128
V7X_VMEM_BYTES = 64 * 1024 * 1024
VMEM_LIMIT = 48 * 1024 * 1024

EXPERT_BLOCK_MIN = 256
EXPERT_BLOCK_MAX = 1024
STREAMS = 1
EXPERT_SUBBLOCKS = 2
SWA_SEQS_PER_STEP = 4
MLSTM_SEQS_PER_STEP = 2
GDN_SEQS_PER_STEP = 8
SC_CHUNK = 64


def _cparams(sem, vmem=VMEM_LIMIT):
    return pltpu.CompilerParams(dimension_semantics=sem, vmem_limit_bytes=vmem)


def _dot(a, b, precision=None):
    return jnp.dot(a, b, preferred_element_type=F32, precision=precision)


def _dot_nt(a, b, precision=None):
    return lax.dot_general(a, b, (((1,), (1,)), ((), ())), preferred_element_type=F32, precision=precision)


def _dot_tn(a, b, precision=None):
    return lax.dot_general(a, b, (((0,), (0,)), ((), ())), preferred_element_type=F32, precision=precision)


def _split2(a):
    hi = a.astype(BF16)
    lo = (a - hi.astype(F32)).astype(BF16)
    return hi, lo


def _split3(a):
    p1 = a.astype(BF16)
    r = a - p1.astype(F32)
    p2 = r.astype(BF16)
    p3 = (r - p2.astype(F32)).astype(BF16)
    return p1, p2, p3


def _dot3(a, b, dot=_dot):
    ah, al = _split2(a)
    bh, bl = _split2(b)
    return dot(ah, bh) + (dot(ah, bl) + dot(al, bh))


def _dot_sel(sel, b, dot=_dot):
    sel = sel.astype(BF16)
    p1, p2, p3 = _split3(b)
    return dot(sel, p1) + (dot(sel, p2) + dot(sel, p3))


def _sigmoid(x):
    return 1.0 / (1.0 + jnp.exp(-x))


def _softplus(x):
    return jnp.maximum(x, 0.0) + jnp.log(1.0 + jnp.exp(-jnp.abs(x)))


def _silu(x):
    return x * _sigmoid(x)


def _lane_bcast(x, c):
    return jnp.broadcast_to(x[:, c:c + 1], x.shape)


def _iota2(shape, dim):
    return lax.broadcasted_iota(jnp.int32, shape, dim)


def _rope_kernel(pos_ref, rows_ref, c_ref, s_ref):
    ang = pos_ref[...] * rows_ref[0:1, :]
    c_ref[...] = rows_ref[1:2, :] * jnp.cos(ang) + rows_ref[2:3, :]
    s_ref[...] = rows_ref[3:4, :] * jnp.sin(ang)


def _rope_tables(pos, rows, tm=512):
    t = pos.shape[0]
    return pl.pallas_call(
        _rope_kernel,
        grid=(t // tm,),
        in_specs=[pl.BlockSpec((tm, 1), lambda i: (i, 0)), pl.BlockSpec((8, LANES), lambda i: (0, 0))],
        out_specs=[pl.BlockSpec((tm, LANES), lambda i: (i, 0))] * 2,
        out_shape=[jax.ShapeDtypeStruct((t, LANES), F32)] * 2,
        compiler_params=_cparams(("arbitrary",)),
        name="rope_tables",
    )(pos, rows)


def _rope_rows(dim, first_lane, pad_one_lanes, heads=1):
    half = dim // 2
    inv = ROPE_THETA ** (-(np.arange(0, dim, 2, dtype=np.float32) / dim))
    rows = np.zeros((8, LANES), np.float32)
    for h in range(heads):
        lo = slice(first_lane + h * dim, first_lane + h * dim + half)
        hi = slice(first_lane + h * dim + half, first_lane + (h + 1) * dim)
        rows[0, lo] = inv
        rows[0, hi] = inv
        rows[1, lo] = 1.0
        rows[1, hi] = 1.0
        rows[3, lo] = -1.0
        rows[3, hi] = 1.0
    rows[2, :pad_one_lanes] = 1.0
    return jnp.asarray(rows)


def _proj_kernel(x_ref, w_ref, *out_refs, offsets):
    xb = x_ref[...].astype(BF16)
    for o_ref, (a, b) in zip(out_refs, offsets):
        o_ref[...] = _dot(xb, w_ref[:, a:b]).astype(o_ref.dtype)


def _proj(x, w, widths, dtypes, tm=512):
    t, k = x.shape
    offs = np.concatenate([[0], np.cumsum(widths)]).tolist()
    offsets = tuple((offs[i], offs[i + 1]) for i in range(len(widths)))
    return pl.pallas_call(
        functools.partial(_proj_kernel, offsets=offsets),
        grid=(t // tm,),
        in_specs=[pl.BlockSpec((tm, k), lambda i: (i, 0)), pl.BlockSpec(w.shape, lambda i: (0, 0))],
        out_specs=[pl.BlockSpec((tm, n), lambda i: (i, 0)) for n in widths],
        out_shape=[jax.ShapeDtypeStruct((t, n), dt) for n, dt in zip(widths, dtypes)],
        compiler_params=_cparams(("arbitrary",)),
        name="in_proj",
    )(x, w)


def _proj_even_kernel(x_ref, w_ref, cw_ref, mla_ref, act_ref, z_ref, g_ref, ext_ref, *, tiles_per_seq):
    tm = x_ref.shape[0]
    o = np.concatenate([[0], np.cumsum(EV_WIDTHS)]).tolist()
    @pl.when(pl.program_id(0) % tiles_per_seq == 0)
    def _():
        ext_ref[0:8, :] = jnp.zeros((8, ext_ref.shape[1]), F32)

    xb = x_ref[...].astype(BF16)
    nchunk = 3
    cw = EV_WIDTHS[1] // nchunk

    def project(ci):
        ext_ref[8:8 + tm, ci * cw:(ci + 1) * cw] = _dot(xb, w_ref[:, o[1] + ci * cw:o[1] + (ci + 1) * cw])

    project(0)
    for ci in range(nchunk):
        if ci + 1 < nchunk:
            project(ci + 1)
        else:
            mla_ref[...] = _dot(xb, w_ref[:, o[0]:o[1]])
            z_ref[...] = _dot(xb, w_ref[:, o[2]:o[3]]).astype(z_ref.dtype)
            g_ref[...] = _dot(xb, w_ref[:, o[3]:o[4]])
        cols = slice(ci * cw, (ci + 1) * cw)
        conv = cw_ref[0:1, cols] * ext_ref[5:5 + tm, cols]
        for j in range(1, CONV_W):
            conv = conv + cw_ref[j:j + 1, cols] * ext_ref[5 + j:5 + j + tm, cols]
        act_ref[:, cols] = _silu(conv).astype(act_ref.dtype)
    ext_ref[0:8, :] = ext_ref[tm:tm + 8, :]


def _proj_even(x, w, conv_w, seq, tm=512):
    t, k = x.shape
    tm = min(tm, seq)
    row = lambda i: (i, 0)
    fix = lambda i: (0, 0)
    return pl.pallas_call(
        functools.partial(_proj_even_kernel, tiles_per_seq=seq // tm),
        grid=(t // tm,),
        in_specs=[pl.BlockSpec((tm, k), row), pl.BlockSpec(w.shape, fix), pl.BlockSpec(conv_w.shape, fix)],
        out_specs=[pl.BlockSpec((tm, n), row) for n in EV_WIDTHS],
        out_shape=[jax.ShapeDtypeStruct((t, n), F32) for n in EV_WIDTHS],
        scratch_shapes=[pltpu.VMEM((tm + 8, EV_WIDTHS[1]), F32)],
        compiler_params=_cparams(("arbitrary",)),
        name="in_proj",
    )(x, w, conv_w)


OD_SEG = dict(mq=(0, 512), mk=(512, 1024), mv=(1024, 1536), mo=(1536, 2048), gates=(2048, 2176),
              sq=(2176, 2688), sqsw=(2688, 3200), sk=(3200, 3456), sksw=(3456, 3712), sv=(3712, 4224))
OD_COLS = 4224


def _proj_odd_kernel(x_ref, w_ref, c_ref, s_ref, mq_ref, mk_ref, mv_ref, mo_ref, mg_ref, sq_ref, sk_ref, sv_ref):
    xb = x_ref[...].astype(BF16)

    def seg(name):
        a, b = OD_SEG[name]
        return _dot(xb, w_ref[:, a:b])

    mq_ref[...] = seg("mq").astype(mq_ref.dtype)
    mk_ref[...] = seg("mk").astype(mk_ref.dtype)
    mv_ref[...] = seg("mv").astype(mv_ref.dtype)
    mo_ref[...] = seg("mo").astype(mo_ref.dtype)
    mg_ref[...] = seg("gates")
    c = c_ref[...]
    s = s_ref[...]
    c8 = jnp.concatenate([c] * (SWA_H // 2), axis=1)
    s8 = jnp.concatenate([s] * (SWA_H // 2), axis=1)
    sq_ref[...] = (seg("sq") * c8 + seg("sqsw") * s8).astype(sq_ref.dtype)
    c2 = jnp.concatenate([c] * SWA_KV, axis=1)
    s2 = jnp.concatenate([s] * SWA_KV, axis=1)
    sk_ref[...] = (seg("sk") * c2 + seg("sksw") * s2).astype(sk_ref.dtype)
    sv_ref[...] = seg("sv").astype(sv_ref.dtype)


def _proj_odd(x, w, ctab, stab, tm=512):
    t, k = x.shape
    widths = (512, 512, 512, 512, 128, SWA_H * SWA_D, SWA_KV * LANES, 2 * SWA_KV * LANES)
    dtypes = (F32, F32, F32, F32, F32, BF16, BF16, BF16)
    return pl.pallas_call(
        _proj_odd_kernel,
        grid=(t // tm,),
        in_specs=[pl.BlockSpec((tm, k), lambda i: (i, 0)), pl.BlockSpec(w.shape, lambda i: (0, 0)),
                  pl.BlockSpec((tm, LANES), lambda i: (i, 0)), pl.BlockSpec((tm, LANES), lambda i: (i, 0))],
        out_specs=[pl.BlockSpec((tm, n), lambda i: (i, 0)) for n in widths],
        out_shape=[jax.ShapeDtypeStruct((t, n), dt) for n, dt in zip(widths, dtypes)],
        compiler_params=_cparams(("arbitrary",)),
        name="in_proj_odd",
    )(x, w, ctab, stab)


def _rms(x, g):
    return x * lax.rsqrt(jnp.mean(x * x, axis=-1, keepdims=True) + EPS) * g


def _mla_prep_kernel(in_ref, c_ref, s_ref, qn_ref, kvn_ref, wq_ref, wkv_ref, q_ref, k_ref, v_ref):
    hw = MLA_H * LANES
    c = c_ref[...]
    s = s_ref[...]
    c8 = jnp.concatenate([c] * MLA_H, axis=1)
    s8 = jnp.concatenate([s] * MLA_H, axis=1)
    cqn = _rms(in_ref[:, 0:Q_LORA], qn_ref[...]).astype(BF16)
    qq = _dot(cqn, wq_ref[...])
    scale = (MLA_NOPE + MLA_ROPE) ** -0.5
    q_ref[...] = ((qq[:, :hw] * c8 + qq[:, hw:] * s8) * scale).astype(q_ref.dtype)
    ckvn = _rms(in_ref[:, Q_LORA:Q_LORA + KV_LORA], kvn_ref[...]).astype(BF16)
    kv = _dot(ckvn, wkv_ref[...])
    o = Q_LORA + KV_LORA
    krr = in_ref[:, o:o + LANES] * c + in_ref[:, o + LANES:o + 2 * LANES] * s
    k_ref[...] = (kv[:, :hw] + jnp.concatenate([krr] * MLA_H, axis=1)).astype(k_ref.dtype)
    v_ref[...] = kv[:, hw:].astype(v_ref.dtype)


def _mla_prep(mla_in, ctab, stab, qn, kvn, wq2, wkv2, tm=512):
    t = mla_in.shape[0]
    hw = MLA_H * LANES
    row = lambda i: (i, 0)
    fix = lambda i: (0, 0)
    return pl.pallas_call(
        _mla_prep_kernel,
        grid=(t // tm,),
        in_specs=[pl.BlockSpec((tm, mla_in.shape[1]), row), pl.BlockSpec((tm, LANES), row), pl.BlockSpec((tm, LANES), row),
                  pl.BlockSpec(qn.shape, fix), pl.BlockSpec(kvn.shape, fix),
                  pl.BlockSpec(wq2.shape, fix), pl.BlockSpec(wkv2.shape, fix)],
        out_specs=[pl.BlockSpec((tm, hw), row)] * 3,
        out_shape=[jax.ShapeDtypeStruct((t, hw), BF16)] * 3,
        compiler_params=_cparams(("arbitrary",)),
        name="mla_prep",
    )(mla_in, ctab, stab, qn, kvn, wq2, wkv2)


def _mla_attn_kernel(q_ref, k_ref, v_ref, o_ref, *, tq):
    i = pl.program_id(2)
    neg = -1e30
    lane = _iota2((tq, LANES), 1)
    ones_lane = (MLA_V, 0)

    def chunk(j, carry, masked):
        start = pl.multiple_of(j * tq, tq)
        out = []
        for hh in range(2):
            m, acc = carry[hh]
            q = q_ref[:, hh * LANES:(hh + 1) * LANES]
            kc = k_ref[pl.ds(start, tq), hh * LANES:(hh + 1) * LANES]
            vc = v_ref[pl.ds(start, tq), hh * LANES:(hh + 1) * LANES]
            vc = jnp.where(lane == ones_lane[hh], jnp.ones_like(vc), vc)
            s = _dot_nt(q, kc)
            if masked:
                s = jnp.where(_iota2(s.shape, 0) >= _iota2(s.shape, 1), s, neg)
            m_new = jnp.maximum(m, jnp.max(s, axis=-1, keepdims=True))
            alpha = jnp.exp(m - m_new)
            p = jnp.exp(s - m_new)
            acc = alpha * acc + _dot(p.astype(BF16), vc)
            out.append((m_new, acc))
        return tuple(out)

    one = (jnp.full((tq, 1), neg, F32), jnp.zeros((tq, LANES), F32))
    carry = lax.fori_loop(0, i, lambda j, c: chunk(j, c, False), (one, one))
    (_, acc0), (_, acc1) = chunk(i, carry, True)
    o0 = acc0 / _lane_bcast(acc0, ones_lane[0])
    o1 = acc1 / _lane_bcast(acc1, ones_lane[1])
    o_ref[...] = jnp.where(lane < MLA_V, o0, o1).astype(o_ref.dtype)


def _mla_attn(q, k, v, batch, seq, tq=512):
    tq = min(tq, seq)
    nq = seq // tq
    pairs = MLA_H // 2
    return pl.pallas_call(
        functools.partial(_mla_attn_kernel, tq=tq),
        grid=(batch, pairs, nq),
        in_specs=[pl.BlockSpec((tq, 2 * LANES), lambda b, p, i: (b * nq + i, p)),
                  pl.BlockSpec((seq, 2 * LANES), lambda b, p, i: (b, p)),
                  pl.BlockSpec((seq, 2 * LANES), lambda b, p, i: (b, p))],
        out_specs=pl.BlockSpec((tq, LANES), lambda b, p, i: (b * nq + i, p)),
        out_shape=jax.ShapeDtypeStruct((batch * seq, pairs * LANES), BF16),
        compiler_params=_cparams(("arbitrary", "arbitrary", "arbitrary")),
        name="mla_attn",
    )(q, k, v)


def _unit_lower_inverse_many(ns):
    c = ns[0].shape[0]
    eye = (_iota2((c, c), 0) == _iota2((c, c), 1)).astype(F32)
    xs = [-n for n in ns]
    ps = [eye + x for x in xs]
    xb = [x.astype(BF16) for x in xs]
    for _ in range(int(math.log2(c)) - 1):
        xs = [_dot(b, b) for b in xb]
        xb = [x.astype(BF16) for x in xs]
        ps = [p + _dot(p.astype(BF16), b) for p, b in zip(ps, xb)]
    return ps


def _gdn_kernel(act_ref, g_ref, z_ref, al_ref, dt_ref, on_ref, o_ref, st_ref):
    c = GDN_CHUNK
    hd = GDN_DK
    nqk = GDN_H * GDN_DK

    @pl.when(pl.program_id(1) == 0)
    def _():
        st_ref[...] = jnp.zeros(st_ref.shape, F32)

    tri = (_iota2((c, c), 0) >= _iota2((c, c), 1)).astype(F32)
    row_ge = _iota2((c, c), 0) >= _iota2((c, c), 1)
    row_gt = _iota2((c, c), 0) > _iota2((c, c), 1)
    lane = _iota2((c, LANES), 1)

    units = []
    for bb in range(act_ref.shape[0]):
        act = act_ref[bb].astype(F32)
        gates = g_ref[bb]
        beta_all = _sigmoid(gates)
        g_all = -jnp.exp(al_ref[...]) * _softplus(gates + dt_ref[...])
        gc_all = _dot_sel(tri, g_all)
        gc_parts = _split3(gc_all)
        for h in range(GDN_H):
            q = act[:, h * hd:(h + 1) * hd]
            k = act[:, nqk + h * hd:nqk + (h + 1) * hd]
            v = act[:, 2 * nqk + h * GDN_DV:2 * nqk + (h + 1) * GDN_DV]
            q = q * lax.rsqrt(jnp.sum(q * q, axis=-1, keepdims=True) + EPS) * (GDN_DK ** -0.5)
            k = k * lax.rsqrt(jnp.sum(k * k, axis=-1, keepdims=True) + EPS)
            beta = _lane_bcast(beta_all, h)
            gcol = _lane_bcast(gc_all, GDN_H + h)
            pick = (lane == GDN_H + h).astype(BF16)
            grow = _dot_nt(pick, gc_parts[0]) + (_dot_nt(pick, gc_parts[1]) + _dot_nt(pick, gc_parts[2]))
            decay = jnp.exp(jnp.where(row_ge, gcol[:, :c] - grow, -jnp.inf))
            kb = k * beta
            lower = jnp.where(row_gt, _dot3(kb, k, _dot_nt) * decay, 0.0)
            eg = jnp.exp(gcol)
            glast = gcol[c - 1:c, :]
            units.append(dict(bb=bb, h=h, lower=lower, rhs=jnp.concatenate([v * beta, kb * eg], axis=1),
                              attn=_dot_nt(q.astype(BF16), k.astype(BF16)) * decay, qg=(q * eg).astype(BF16),
                              kg=(k * jnp.exp(glast - gcol)).astype(BF16), gl=jnp.exp(glast)))

    tinvs = _unit_lower_inverse_many([u["lower"] for u in units])
    uws = []
    for u, tinv in zip(units, tinvs):
        uws.append(_dot(tinv.astype(BF16), u["rhs"].astype(BF16)))
    states = [st_ref[u["bb"], u["h"]] for u in units]
    sbs = [s.astype(BF16) for s in states]
    vnews = [(uw[:, :GDN_DV] - _dot(uw[:, GDN_DV:].astype(BF16), sb)).astype(BF16) for uw, sb in zip(uws, sbs)]
    for u, state, sb, vnb in zip(units, states, sbs, vnews):
        bb, h = u["bb"], u["h"]
        o = _dot(u["qg"], sb) + _dot(u["attn"].astype(BF16), vnb)
        st_ref[bb, h] = state * u["gl"] + _dot_tn(u["kg"], vnb)
        o = _rms(o, on_ref[...]) * _silu(z_ref[bb, :, h * GDN_DV:(h + 1) * GDN_DV].astype(F32))
        o_ref[bb, :, h * GDN_DV:(h + 1) * GDN_DV] = o.astype(o_ref.dtype)


def _gdn(act, gates, z, a_row, dt_row, o_norm, batch, seq):
    c = GDN_CHUNK
    nc = seq // c
    w3 = act.shape[1]
    wo = GDN_H * GDN_DV
    nb = min(GDN_SEQS_PER_STEP, batch)
    row = lambda b, i: (b, i, 0)
    fix = lambda b, i: (0, 0)
    out = pl.pallas_call(
        _gdn_kernel,
        grid=(batch // nb, nc),
        in_specs=[pl.BlockSpec((nb, c, w3), row), pl.BlockSpec((nb, c, LANES), row), pl.BlockSpec((nb, c, wo), row),
                  pl.BlockSpec((1, LANES), fix), pl.BlockSpec((1, LANES), fix), pl.BlockSpec((1, GDN_DV), fix)],
        out_specs=pl.BlockSpec((nb, c, wo), row),
        out_shape=jax.ShapeDtypeStruct((batch, seq, wo), BF16),
        scratch_shapes=[pltpu.VMEM((nb, GDN_H, GDN_DK, GDN_DV), F32)],
        compiler_params=_cparams(("arbitrary", "arbitrary")),
        name="gdn",
    )(act.reshape(batch, seq, w3), gates.reshape(batch, seq, LANES), z.reshape(batch, seq, wo), a_row, dt_row, o_norm)
    return out.reshape(batch * seq, wo)


def _mlstm_kernel(q_ref, k_ref, v_ref, og_ref, g_ref, bias_ref, nrm_ref, o_ref, c_ref, n_ref, m_ref):
    @pl.when(pl.program_id(1) == 0)
    def _():
        c_ref[...] = jnp.zeros(c_ref.shape, F32)
        n_ref[...] = jnp.zeros(n_ref.shape, F32)
        m_ref[...] = jnp.zeros(m_ref.shape, F32)

    c = ML_CHUNK
    tri = (_iota2((c, c), 0) >= _iota2((c, c), 1)).astype(F32)
    row_ge = _iota2((c, c), 0) >= _iota2((c, c), 1)
    ones = jnp.ones((c, LANES), F32)
    lane = _iota2((c, LANES), 1)

    units = []
    for bb in range(q_ref.shape[0]):
        pre = g_ref[bb] + bias_ref[...]
        logf = jnp.minimum(pre, 0.0) - jnp.log(1.0 + jnp.exp(-jnp.abs(pre)))
        bcum_all = _dot_sel(tri, logf)
        for h in range(ML_H):
            q = q_ref[bb, :, h * LANES:(h + 1) * LANES].astype(F32)
            k = k_ref[bb, :, h * LANES:(h + 1) * LANES].astype(F32) * (ML_DK ** -0.5)
            units.append(dict(bb=bb, h=h, q=q, k=k, qb=q.astype(BF16), vb=v_ref[bb, :, h * ML_DV:(h + 1) * ML_DV].astype(BF16),
                              bcol=_lane_bcast(bcum_all, ML_H + h),
                              icol=_lane_bcast(pre, h),
                              col=jnp.where(lane == h, pre, 0.0) - jnp.where(lane == ML_H + h, bcum_all, 0.0),
                              m_st=m_ref[bb, h], cst=c_ref[bb, h], nst=n_ref[bb, h]))
    for u in units:
        u["row"] = _dot_sel(ones, u["col"], _dot_nt)
        u["qk"] = _dot_nt(u["qb"], u["k"].astype(BF16))
        u["qc"] = _dot(u["qb"], u["cst"].astype(BF16))
    for u in units:
        u["d"] = jnp.where(row_ge, u["bcol"][:, :c] + u["row"], -jnp.inf)
        u["inter"] = u["bcol"] + u["m_st"]
        u["m_t"] = jnp.maximum(u["inter"], jnp.max(u["d"], axis=-1, keepdims=True))
        u["b_end"] = u["bcol"][c - 1:c, :]
        u["a"] = u["b_end"] - u["bcol"] + u["icol"]
        u["m_new"] = jnp.maximum(u["b_end"] + u["m_st"], jnp.max(u["a"], axis=0, keepdims=True))
    for u in units:
        u["w_inter"] = jnp.exp(u["inter"] - u["m_t"])
        u["p"] = jnp.exp(u["d"] - u["m_t"][:, :c]) * u["qk"]
        u["keep"] = jnp.exp(u["b_end"] + u["m_st"] - u["m_new"])
        u["ks"] = u["k"] * jnp.exp(u["a"] - u["m_new"])
    for u in units:
        u["pv"] = _dot(u["p"].astype(BF16), u["vb"])
        u["kv"] = _dot_tn(u["ks"].astype(BF16), u["vb"])
    for u in units:
        u["den"] = (u["w_inter"] * jnp.sum(u["q"] * u["nst"], axis=-1, keepdims=True)
                    + jnp.sum(u["p"], axis=-1, keepdims=True))
    for u in units:
        bb, h = u["bb"], u["h"]
        num = u["w_inter"] * u["qc"] + u["pv"]
        hc = num / jnp.maximum(jnp.abs(u["den"]), jnp.exp(-u["m_t"]))
        c_ref[bb, h] = u["cst"] * u["keep"] + u["kv"]
        n_ref[bb, h] = u["nst"] * u["keep"] + jnp.sum(u["ks"], axis=0, keepdims=True)
        m_ref[bb, h] = u["m_new"]
        hn = (_rms(hc, nrm_ref[:, h * ML_DV:(h + 1) * ML_DV])
              * _sigmoid(og_ref[bb, :, h * ML_DV:(h + 1) * ML_DV].astype(F32)))
        o_ref[bb, :, h * ML_DV:(h + 1) * ML_DV] = hn.astype(o_ref.dtype)


def _mlstm(mq, mk, mv, mo, gates, bias_row, norm_row, batch, seq):
    c = ML_CHUNK
    nc = seq // c
    nb = min(MLSTM_SEQS_PER_STEP, batch)
    row = lambda b, i: (b, i, 0)
    fix = lambda b, i: (0, 0)
    wide = ML_H * LANES
    r3 = lambda a: a.reshape(batch, seq, a.shape[-1])
    out = pl.pallas_call(
        _mlstm_kernel,
        grid=(batch // nb, nc),
        in_specs=[pl.BlockSpec((nb, c, wide), row), pl.BlockSpec((nb, c, wide), row), pl.BlockSpec((nb, c, wide), row),
                  pl.BlockSpec((nb, c, wide), row), pl.BlockSpec((nb, c, LANES), row),
                  pl.BlockSpec((1, LANES), fix), pl.BlockSpec((1, wide), fix)],
        out_specs=pl.BlockSpec((nb, c, wide), row),
        out_shape=jax.ShapeDtypeStruct((batch, seq, wide), BF16),
        scratch_shapes=[pltpu.VMEM((nb, ML_H, LANES, ML_DV), F32), pltpu.VMEM((nb, ML_H, 1, LANES), F32),
                        pltpu.VMEM((nb, ML_H, 1, LANES), F32)],
        compiler_params=_cparams(("arbitrary", "arbitrary")),
        name="mlstm",
    )(r3(mq), r3(mk), r3(mv), r3(mo), r3(gates), bias_row, norm_row)
    return out.reshape(batch * seq, wide)


def _swa_kernel(q_ref, kc_ref, kp_ref, vc_ref, vp_ref, sink_ref, o_ref):
    w = WINDOW
    n = pl.program_id(1)
    scale = SWA_D ** -0.5
    qi = _iota2((w, w), 0)
    kj = _iota2((w, w), 1)
    mask_c = kj <= qi
    mask_p = jnp.logical_and(kj > qi, n > 0)
    grp = SWA_H // SWA_KV
    neg = -1e30
    units = [(bb, h) for bb in range(q_ref.shape[0]) for h in range(SWA_H)]
    scores = []
    half_of_lane = _iota2((w, LANES), 1) // SWA_D
    for bb, h in units:
        g = h // grp
        pair = q_ref[bb, :, (h // 2) * LANES:(h // 2 + 1) * LANES]
        q = jnp.where(half_of_lane == h % 2, pair, jnp.zeros_like(pair))
        scores.append((_dot_nt(q, kc_ref[bb, :, g * LANES:(g + 1) * LANES]),
                       _dot_nt(q, kp_ref[bb, :, g * LANES:(g + 1) * LANES])))
    masked, tops, exps, dens, probs = [], [], [], [], {}
    for sc, sp in scores:
        masked.append((jnp.where(mask_c, sc * scale, neg), jnp.where(mask_p, sp * scale, neg)))
    for (bb, h), (s_c, s_p) in zip(units, masked):
        tops.append(jnp.maximum(jnp.max(jnp.maximum(s_c, s_p), axis=-1, keepdims=True), sink_ref[:, h:h + 1]))
    for (s_c, s_p), m in zip(masked, tops):
        exps.append((jnp.where(mask_c, jnp.exp(s_c - m), 0.0), jnp.where(mask_p, jnp.exp(s_p - m), 0.0)))
    ones_b = jnp.ones((w, LANES), BF16)
    for (bb, h), (p_c, p_p), m in zip(units, exps, tops):
        p_c, p_p = p_c.astype(BF16), p_p.astype(BF16)
        probs[bb, h] = (p_c, p_p)
        dens.append(_dot(p_c, ones_b) + _dot(p_p, ones_b) + jnp.exp(sink_ref[:, h:h + 1] - m))
    inv = {u: 1.0 / den for u, den in zip(units, dens)}
    for bb in range(q_ref.shape[0]):
        for pair in range(SWA_H // 2):
            acc = None
            for sub in range(2):
                h = 2 * pair + sub
                vcol = (2 * (h // grp) + sub) * LANES
                p_c, p_p = probs[bb, h]
                part = (_dot(p_c, vc_ref[bb, :, vcol:vcol + LANES]) + _dot(p_p, vp_ref[bb, :, vcol:vcol + LANES])) * inv[bb, h]
                acc = part if acc is None else acc + part
            o_ref[bb, :, pair * LANES:(pair + 1) * LANES] = acc.astype(o_ref.dtype)


def _swa(sq, sk, sv, sinks_row, batch, seq):
    w = WINDOW
    nb = seq // w
    ns = min(SWA_SEQS_PER_STEP, batch)
    wo = SWA_H * SWA_D
    cur = lambda b, n: (b, n, 0)
    prev = lambda b, n: (b, jnp.maximum(n - 1, 0), 0)
    r3 = lambda a: a.reshape(batch, seq, a.shape[-1])
    q3, k3, v3 = r3(sq), r3(sk), r3(sv)
    out = pl.pallas_call(
        _swa_kernel,
        grid=(batch // ns, nb),
        in_specs=[pl.BlockSpec((ns, w, sq.shape[1]), cur),
                  pl.BlockSpec((ns, w, sk.shape[1]), cur), pl.BlockSpec((ns, w, sk.shape[1]), prev),
                  pl.BlockSpec((ns, w, sv.shape[1]), cur), pl.BlockSpec((ns, w, sv.shape[1]), prev),
                  pl.BlockSpec((1, LANES), lambda b, n: (0, 0))],
        out_specs=pl.BlockSpec((ns, w, wo), cur),
        out_shape=jax.ShapeDtypeStruct((batch, seq, wo), BF16),
        compiler_params=_cparams(("arbitrary", "arbitrary")),
        name="swa",
    )(q3, k3, k3, v3, v3, sinks_row)
    return out.reshape(batch * seq, wo)


def _layer_norm(h, g, b):
    mu = jnp.mean(h, axis=-1, keepdims=True)
    d = h - mu
    var = jnp.mean(d * d, axis=-1, keepdims=True)
    return d * lax.rsqrt(var + LN_EPS) * g + b


def _outproj_kernel(x_ref, a1_ref, a2_ref, w_ref, g_ref, b_ref, o_ref, op_ref):
    k1 = a1_ref.shape[1]
    y = _dot(a1_ref[...].astype(BF16), w_ref[0:k1, :]) + _dot(a2_ref[...].astype(BF16), w_ref[k1:, :])
    h = _layer_norm(DN_ALPHA * x_ref[...] + y, g_ref[...], b_ref[...])
    o_ref[...] = h
    op_ref[...] = _pack_pairs(h)


def _outproj_ln(x, a1, a2, w, g, b, tm=512):
    t, d = x.shape
    row = lambda i: (i, 0)
    fix = lambda i: (0, 0)
    return pl.pallas_call(
        _outproj_kernel,
        grid=(t // tm,),
        in_specs=[pl.BlockSpec((tm, d), row), pl.BlockSpec((tm, a1.shape[1]), row), pl.BlockSpec((tm, a2.shape[1]), row),
                  pl.BlockSpec(w.shape, fix), pl.BlockSpec((1, d), fix), pl.BlockSpec((1, d), fix)],
        out_specs=[pl.BlockSpec((tm, d), row), pl.BlockSpec((tm, d // 2), row)],
        out_shape=[jax.ShapeDtypeStruct((t, d), F32), jax.ShapeDtypeStruct((t, d // 2), jnp.uint32)],
        compiler_params=_cparams(("arbitrary",)),
        name="outproj_ln",
    )(x, a1, a2, w, g, b)


def _first_index(x, m, iota_f, sentinel):
    return jnp.min(jnp.where(x == m, iota_f, sentinel), axis=0, keepdims=True)


def _router_kernel(x_ref, wt_ref, bias_ref, idx_ref, gate_ref, rank_ref, cnt_ref, carry_ref):
    tm = x_ref.shape[0]
    e = N_EXPERTS
    gs = e // N_GROUPS
    ninf = -jnp.inf

    @pl.when(pl.program_id(0) == 0)
    def _():
        carry_ref[...] = jnp.zeros(carry_ref.shape, F32)

    logits = _dot3(wt_ref[...], x_ref[...], _dot_nt)
    scores = _sigmoid(logits)
    sel = scores + bias_ref[:, 0:1]

    sub_f = _iota2((gs, tm), 0).astype(F32)
    gscore = []
    for g in range(N_GROUPS):
        blk = sel[g * gs:(g + 1) * gs, :]
        m1 = jnp.max(blk, axis=0, keepdims=True)
        i1 = _first_index(blk, m1, sub_f, float(gs))
        m2 = jnp.max(jnp.where(sub_f == i1, ninf, blk), axis=0, keepdims=True)
        gscore.append(m1 + m2)
    gsc = jnp.concatenate(gscore, axis=0)
    grp_f = _iota2((N_GROUPS, tm), 0).astype(F32)
    gmask = jnp.zeros((N_GROUPS, tm), F32)
    for _ in range(TOPK_GROUPS):
        m = jnp.max(gsc, axis=0, keepdims=True)
        gi = _first_index(gsc, m, grp_f, float(N_GROUPS))
        hit = grp_f == gi
        gmask = jnp.where(hit, 1.0, gmask)
        gsc = jnp.where(hit, ninf, gsc)
    masked = jnp.concatenate(
        [jnp.where(gmask[g:g + 1, :] > 0.0, sel[g * gs:(g + 1) * gs, :], ninf) for g in range(N_GROUPS)], axis=0)

    exp_f = _iota2((e, tm), 0).astype(F32)
    chosen = jnp.zeros((e, tm), F32)
    idxs, gates = [], []
    for _ in range(TOP_K):
        m = jnp.max(masked, axis=0, keepdims=True)
        ei = _first_index(masked, m, exp_f, float(e))
        hit = exp_f == ei
        idxs.append(ei)
        gates.append(jnp.sum(jnp.where(hit, scores, 0.0), axis=0, keepdims=True))
        chosen = jnp.where(hit, 1.0, chosen)
        masked = jnp.where(hit, ninf, masked)
    gate = jnp.concatenate(gates, axis=0)
    gate = gate / jnp.sum(gate, axis=0, keepdims=True) * ROUTED_SCALE
    idx_f = jnp.concatenate(idxs, axis=0)

    upper = (_iota2((tm, tm), 0) < _iota2((tm, tm), 1)).astype(BF16)
    before = _dot(chosen.astype(BF16), upper) + carry_ref[...][:, 0:1]
    ranks = [jnp.sum(jnp.where(exp_f == idxs[k], before, 0.0), axis=0, keepdims=True) for k in range(TOP_K)]
    carry_ref[...] = carry_ref[...] + jnp.sum(chosen, axis=1, keepdims=True)

    idx_ref[...] = idx_f.astype(jnp.int32)
    gate_ref[...] = gate
    rank_ref[...] = jnp.concatenate(ranks, axis=0).astype(jnp.int32)
    cnt_ref[...] = carry_ref[...]


def _router(x, wt, bias_col, tm=512):
    t, d = x.shape
    col = lambda i: (0, i)
    fix = lambda i: (0, 0)
    return pl.pallas_call(
        _router_kernel,
        grid=(t // tm,),
        in_specs=[pl.BlockSpec((tm, d), lambda i: (i, 0)), pl.BlockSpec(wt.shape, fix), pl.BlockSpec((N_EXPERTS, LANES), fix)],
        out_specs=[pl.BlockSpec((TOP_K, tm), col), pl.BlockSpec((TOP_K, tm), col), pl.BlockSpec((TOP_K, tm), col),
                   pl.BlockSpec((N_EXPERTS, LANES), fix)],
        out_shape=[jax.ShapeDtypeStruct((TOP_K, t), jnp.int32), jax.ShapeDtypeStruct((TOP_K, t), F32),
                   jax.ShapeDtypeStruct((TOP_K, t), jnp.int32), jax.ShapeDtypeStruct((N_EXPERTS, LANES), F32)],
        scratch_shapes=[pltpu.VMEM((N_EXPERTS, LANES), F32)],
        compiler_params=_cparams(("arbitrary",)),
        name="router",
    )(x, wt, bias_col)


def _dest_kernel(idx_ref, rank_ref, start_ref, dest_ref):
    tm = idx_ref.shape[1]
    exp_i = _iota2((N_EXPERTS, tm), 0)
    start = start_ref[:, 0:1]
    rows = [jnp.sum(jnp.where(exp_i == idx_ref[s:s + 1, :], start, 0.0), axis=0, keepdims=True) for s in range(TOP_K)]
    dest_ref[...] = jnp.concatenate(rows, axis=0).astype(jnp.int32) + rank_ref[...]


def _dest_rows(idx, rank, start_col, tm=2048):
    t = idx.shape[1]
    tm = min(tm, t)
    col = lambda i: (0, i)
    return pl.pallas_call(
        _dest_kernel,
        grid=(t // tm,),
        in_specs=[pl.BlockSpec((TOP_K, tm), col), pl.BlockSpec((TOP_K, tm), col),
                  pl.BlockSpec((N_EXPERTS, LANES), lambda i: (0, 0))],
        out_specs=pl.BlockSpec((TOP_K, tm), col),
        out_shape=jax.ShapeDtypeStruct((TOP_K, t), jnp.int32),
        compiler_params=_cparams(("arbitrary",)),
        name="moe_dest",
    )(idx, rank, start_col)


def _pack_pairs(x):
    n = x.shape[1] // 2
    hi = lax.bitcast_convert_type(x[:, :n].astype(BF16).astype(F32), jnp.uint32)
    lo = lax.bitcast_convert_type(x[:, n:].astype(BF16).astype(F32), jnp.uint32)
    return hi | (lo >> 16)


def _unpack_pairs(w):
    hi = lax.bitcast_convert_type(w & jnp.uint32(0xFFFF0000), F32)
    lo = lax.bitcast_convert_type(w << 16, F32)
    return hi, lo


def _sc_scatter_rows(xp, dest, rows, chunk=LANES):
    t, width = xp.shape
    info = plsc.get_sparse_core_info()
    ncores, nsub = info.num_cores, info.num_subcores
    per_worker = t // (ncores * nsub)
    nchunk = per_worker // chunk
    mesh = plsc.VectorSubcoreMesh(core_axis_name="c", subcore_axis_name="s")

    @functools.partial(
        pl.kernel, mesh=mesh,
        out_type=jax.ShapeDtypeStruct((rows, width), xp.dtype),
        scratch_types=[pltpu.VMEM((TOP_K, chunk), jnp.int32), pltpu.VMEM((chunk, width), xp.dtype), pltpu.SemaphoreType.DMA],
    )
    def scatter(xp_hbm, dest_hbm, out_hbm, idx_v, rows_v, sem):
        base = (lax.axis_index("s") * ncores + lax.axis_index("c")) * per_worker

        @pl.loop(0, nchunk)
        def _(i):
            off = pl.multiple_of(base + i * chunk, chunk)
            pltpu.sync_copy(dest_hbm.at[:, pl.ds(off, chunk)], idx_v)
            pltpu.sync_copy(xp_hbm.at[pl.ds(off, chunk)], rows_v)
            copies = [pltpu.async_copy(rows_v, out_hbm.at[idx_v.at[s]], sem) for s in range(TOP_K)]
            for cp in copies:
                cp.wait()

    return scatter(xp, dest)


def _experts_kernel(be_ref, nu_ref, nv_ref, first_ref, slot_ref, nxt_ref, xs_ref, wg_hbm, wu_hbm, wd_hbm, ys_ref,
                    wgf_ref, wuf_ref, wdf_ref, wgb_ref, wub_ref, wdb_ref, sem, *, layer):
    i = pl.program_id(0)

    def fetch(e, s):
        return [pltpu.make_async_copy(wg_hbm.at[layer, e], wgf_ref.at[s], sem.at[s]),
                pltpu.make_async_copy(wu_hbm.at[layer, e], wuf_ref.at[s], sem.at[s]),
                pltpu.make_async_copy(wd_hbm.at[layer, e], wdf_ref.at[s], sem.at[s])]

    @pl.when(i == 0)
    def _():
        for cp in fetch(be_ref[0], 0):
            cp.start()

    @pl.when(jnp.logical_and(first_ref[i] == 1, i < nu_ref[0]))
    def _():
        s = slot_ref[i]
        for cp in fetch(be_ref[i], s):
            cp.wait()
        wgb_ref[...] = wgf_ref[s].astype(BF16)
        wub_ref[...] = wuf_ref[s].astype(BF16)
        wdb_ref[...] = wdf_ref[s].astype(BF16)

        @pl.when(nxt_ref[i] >= 0)
        def _():
            for cp in fetch(nxt_ref[i], 1 - s):
                cp.start()

    @pl.when(i < nu_ref[0])
    def _():
        half = xs_ref.shape[1]
        sub = xs_ref.shape[0] // EXPERT_SUBBLOCKS
        acts = []
        for r in range(EXPERT_SUBBLOCKS):
            rows = pl.ds(r * sub, sub)
            live = (_iota2((sub, 1), 0) + r * sub) < nv_ref[i]
            xa, xb = _unpack_pairs(jnp.where(live, xs_ref[rows, :], jnp.uint32(0)))
            x = jnp.concatenate([xa.astype(BF16), xb.astype(BF16)], axis=1)
            acts.append((_dot(x, wgb_ref[...]), _dot(x, wub_ref[...])))
        outs = [_dot((_silu(gate) * up).astype(BF16), wdb_ref[...]) for gate, up in acts]
        for r, y in enumerate(outs):
            ys_ref[pl.ds(r * sub, sub), :] = _pack_pairs(y)


def _experts(block_e, n_used, n_valid, xs, wg, wu, wd, layer, block):
    rows, half = xs.shape
    d = 2 * half
    nb = rows // block
    pos = jnp.arange(nb, dtype=jnp.int32)
    first = jnp.concatenate([jnp.ones((1,), jnp.int32), (block_e[1:] != block_e[:-1]).astype(jnp.int32)])
    slot = (jnp.cumsum(first) - 1) % 2
    later = (pos[None, :] > pos[:, None]) & (block_e[None, :] != block_e[:, None]) & (pos[None, :] < n_used[0])
    nxt_pos = jnp.min(jnp.where(later, pos[None, :], nb), axis=1)
    nxt = jnp.where(nxt_pos < nb, block_e[jnp.minimum(nxt_pos, nb - 1)], -1)
    blk = lambda i, be, nu, *rest: (jnp.minimum(i, nu[0] - 1), 0)
    hbm = pl.BlockSpec(memory_space=pl.ANY)
    return pl.pallas_call(
        functools.partial(_experts_kernel, layer=layer),
        grid_spec=pltpu.PrefetchScalarGridSpec(
            num_scalar_prefetch=6,
            grid=(nb,),
            in_specs=[pl.BlockSpec((block, half), blk), hbm, hbm, hbm],
            out_specs=pl.BlockSpec((block, half), blk),
            scratch_shapes=[pltpu.VMEM((2, d, D_EXPERT), F32), pltpu.VMEM((2, d, D_EXPERT), F32),
                            pltpu.VMEM((2, D_EXPERT, d), F32),
                            pltpu.VMEM((d, D_EXPERT), BF16), pltpu.VMEM((d, D_EXPERT), BF16),
                            pltpu.VMEM((D_EXPERT, d), BF16), pltpu.SemaphoreType.DMA((2,))],
        ),
        out_shape=jax.ShapeDtypeStruct((rows, half), jnp.uint32),
        compiler_params=_cparams(("arbitrary",)),
        name="moe_experts",
    )(block_e, n_used, n_valid, first, slot.astype(jnp.int32), nxt.astype(jnp.int32), xs, wg, wu, wd)


def _sc_gather_rows(table, idx, chunk=SC_CHUNK):
    n = idx.shape[0]
    width = table.shape[1]
    info = plsc.get_sparse_core_info()
    ncores, nsub = info.num_cores, info.num_subcores
    per_worker = n // (ncores * nsub)
    nchunk = per_worker // chunk
    mesh = plsc.VectorSubcoreMesh(core_axis_name="c", subcore_axis_name="s")

    @functools.partial(
        pl.kernel, mesh=mesh,
        out_type=jax.ShapeDtypeStruct((n, width), table.dtype),
        scratch_types=[pltpu.VMEM((nchunk, chunk), jnp.int32), pltpu.VMEM((2, chunk, width), table.dtype),
                       pltpu.SemaphoreType.DMA((2,)), pltpu.SemaphoreType.DMA((2,))],
    )
    def gather(table_hbm, idx_hbm, out_hbm, idx_v, rows_v, gsem, wsem):
        wid = lax.axis_index("s") * ncores + lax.axis_index("c")
        base = wid * per_worker
        pltpu.sync_copy(idx_hbm.at[pl.ds(wid * nchunk, nchunk)], idx_v)

        def fetch(j, b):
            return pltpu.make_async_copy(table_hbm.at[idx_v.at[j]], rows_v.at[b], gsem.at[b])

        def flush(j, b):
            off = pl.multiple_of(base + j * chunk, chunk)
            return pltpu.make_async_copy(rows_v.at[b], out_hbm.at[pl.ds(off, chunk)], wsem.at[b])

        fetch(0, 0).start()

        @pl.loop(0, nchunk, step=2)
        def _(i):
            for b in range(2):
                j = i + b
                fetch(j, b).wait()

                @pl.when(j + 1 < nchunk)
                def _():
                    @pl.when(j >= 1)
                    def _():
                        flush(j - 1, 1 - b).wait()

                    fetch(j + 1, 1 - b).start()

                flush(j, b).start()

        flush(nchunk - 2, 0).wait()
        flush(nchunk - 1, 1).wait()

    return gather(table, idx.reshape(n // chunk, chunk))


def _shared_kernel(xp_ref, sg_ref, su_ref, sd_ref, o_ref):
    xa, xb = _unpack_pairs(xp_ref[...])
    x = jnp.concatenate([xa.astype(BF16), xb.astype(BF16)], axis=1)
    hs = _silu(_dot(x, sg_ref[...])) * _dot(x, su_ref[...])
    o_ref[...] = _pack_pairs(_dot(hs.astype(BF16), sd_ref[...]))


def _shared_expert(xp, sg, su, sd, tm=512):
    t, half = xp.shape
    row = lambda i: (i, 0)
    fix = lambda i: (0, 0)
    return pl.pallas_call(
        _shared_kernel,
        grid=(t // tm,),
        in_specs=[pl.BlockSpec((tm, half), row), pl.BlockSpec(sg.shape, fix), pl.BlockSpec(su.shape, fix),
                  pl.BlockSpec(sd.shape, fix)],
        out_specs=pl.BlockSpec((tm, half), row),
        out_shape=jax.ShapeDtypeStruct((t, half), jnp.uint32),
        compiler_params=_cparams(("arbitrary",)),
        name="moe_shared",
    )(xp, sg, su, sd)


def _combine_kernel(x_ref, gate_ref, rows_ref, sh_ref, g_ref, b_ref, o_ref):
    gate = gate_ref[...]
    ya, yb = _unpack_pairs(sh_ref[...])
    for s in range(TOP_K):
        a, b = _unpack_pairs(rows_ref[s])
        ya = ya + gate[:, s:s + 1] * a
        yb = yb + gate[:, s:s + 1] * b
    ff = jnp.concatenate([ya, yb], axis=1)
    o_ref[...] = _layer_norm(DN_ALPHA * x_ref[...] + ff, g_ref[...], b_ref[...])


def _combine(x, gate_t, rows, shared, g, b, tm=512):
    t, d = x.shape
    row = lambda i: (i, 0)
    fix = lambda i: (0, 0)
    return pl.pallas_call(
        _combine_kernel,
        grid=(t // tm,),
        in_specs=[pl.BlockSpec((tm, d), row), pl.BlockSpec((tm, TOP_K), row),
                  pl.BlockSpec((TOP_K, tm, d // 2), lambda i: (0, i, 0)), pl.BlockSpec((tm, d // 2), row),
                  pl.BlockSpec((1, d), fix), pl.BlockSpec((1, d), fix)],
        out_specs=pl.BlockSpec((tm, d), row),
        out_shape=jax.ShapeDtypeStruct((t, d), F32),
        compiler_params=_cparams(("arbitrary",)),
        name="moe_combine",
    )(x, gate_t, rows, shared, g, b)


def _take_cols(w, idx):
    idx = np.asarray(idx)
    runs, start = [], 0
    for pos in range(1, len(idx) + 1):
        run_ends = pos == len(idx) or (idx[pos] != idx[pos - 1] + 1 if idx[pos - 1] >= 0 else idx[pos] >= 0)
        if run_ends:
            runs.append((start, int(idx[start]), pos - start))
            start = pos

    def body(w_ref, o_ref):
        for dst, src, width in runs:
            if src < 0:
                o_ref[:, dst:dst + width] = jnp.zeros((o_ref.shape[0], width), o_ref.dtype)
            else:
                o_ref[:, dst:dst + width] = w_ref[:, src:src + width].astype(o_ref.dtype)

    rows = w.shape[0]
    tr = min(rows, 256)
    return pl.pallas_call(
        body,
        grid=(rows // tr,),
        in_specs=[pl.BlockSpec((tr, w.shape[1]), lambda i: (i, 0))],
        out_specs=pl.BlockSpec((tr, len(idx)), lambda i: (i, 0)),
        out_shape=jax.ShapeDtypeStruct((rows, len(idx)), BF16),
        compiler_params=_cparams(("arbitrary",)),
        name="weight_cols",
    )(w)


def _pad_lane_row(v, first_lane, width=LANES):
    out = jnp.zeros((1, width), F32)
    return lax.dynamic_update_slice(out, v.reshape(1, -1).astype(F32), (0, first_lane))


def _even_in_cols():
    z = lambda n: -np.ones(n, int)
    kr0 = Q_LORA + KV_LORA
    half = MLA_ROPE // 2
    cols = [np.arange(0, Q_LORA), np.arange(Q_LORA, Q_LORA + KV_LORA),
            z(64), np.arange(kr0, kr0 + MLA_ROPE), z(32),
            z(64), np.arange(kr0 + half, kr0 + MLA_ROPE), np.arange(kr0, kr0 + half), z(32)]
    g0 = kr0 + MLA_ROPE
    nqk = GDN_H * GDN_DK
    cols.append(np.arange(g0, g0 + 3 * nqk))
    zoff = g0 + 3 * nqk + 2 * GDN_H
    cols.append(np.arange(zoff, zoff + GDN_H * GDN_DV))
    cols += [np.arange(g0 + 3 * nqk, g0 + 3 * nqk + 2 * GDN_H), z(LANES - 2 * GDN_H)]
    return np.concatenate(cols)


EV_WIDTHS = (Q_LORA + KV_LORA + 2 * LANES, 3 * GDN_H * GDN_DK, GDN_H * GDN_DV, LANES)


def _mla_q_cols():
    per = MLA_NOPE + MLA_ROPE
    half = MLA_ROPE // 2
    main, sw = [], []
    for h in range(MLA_H):
        b = h * per
        main += [np.arange(b, b + per), -np.ones(LANES - per, int)]
        sw += [-np.ones(MLA_NOPE, int), np.arange(b + MLA_NOPE + half, b + per), np.arange(b + MLA_NOPE, b + MLA_NOPE + half),
               -np.ones(LANES - per, int)]
    return np.concatenate(main + sw)


def _mla_kv_cols():
    per = MLA_NOPE + MLA_V
    kc, vc = [], []
    for h in range(MLA_H):
        b = h * per
        kc += [np.arange(b, b + MLA_NOPE), -np.ones(LANES - MLA_NOPE, int)]
        vv = np.arange(b + MLA_NOPE, b + per)
        pad = -np.ones(LANES - MLA_V, int)
        vc += [vv, pad] if h % 2 == 0 else [pad, vv]
    return np.concatenate(kc + vc)


def _odd_in_cols():
    z = lambda n: -np.ones(n, int)
    o = 0
    cols = []
    mq0, mk0 = 0, ML_H * ML_DK
    for base in (mq0, mk0):
        for h in range(ML_H):
            cols += [np.arange(base + h * ML_DK, base + (h + 1) * ML_DK), z(LANES - ML_DK)]
    mv0 = 2 * ML_H * ML_DK
    cols.append(np.arange(mv0, mv0 + ML_H * ML_DV))
    mi0 = mv0 + ML_H * ML_DV
    mo0 = mi0 + 2 * ML_H
    cols.append(np.arange(mo0, mo0 + ML_H * ML_DV))
    cols += [np.arange(mi0, mi0 + 2 * ML_H), z(LANES - 2 * ML_H)]
    sq0 = mo0 + ML_H * ML_DV
    sk0 = sq0 + SWA_H * SWA_D
    sv0 = sk0 + SWA_KV * SWA_D
    half = SWA_D // 2

    def heads(base, n, swapped, copies):
        out = []
        for h in range(n):
            b = base + h * SWA_D
            one = [np.arange(b + half, b + SWA_D), np.arange(b, b + half)] if swapped else [np.arange(b, b + SWA_D)]
            out += one * copies
        return out

    cols += (heads(sq0, SWA_H, False, 1) + heads(sq0, SWA_H, True, 1)
             + heads(sk0, SWA_KV, False, 2) + heads(sk0, SWA_KV, True, 2))
    for g in range(SWA_KV):
        vv = np.arange(sv0 + g * SWA_D, sv0 + (g + 1) * SWA_D)
        cols += [vv, z(LANES - SWA_D), z(LANES - SWA_D), vv]
    return np.concatenate(cols)


def _even_weights(w_in, w_qb, w_kvb):
    return (_take_cols(w_in, _even_in_cols()), _take_cols(w_qb, _mla_q_cols()), _take_cols(w_kvb, _mla_kv_cols()))


def _even_mixer(x, tabs, weights, q_norm, kv_norm, conv_w, a_log, dt_bias, o_norm, batch, seq):
    ctab, stab = tabs
    w, wq2, wkv2 = weights
    mla_in, act, z, gates = _proj_even(x, w, conv_w, seq)
    q, k, v = _mla_prep(mla_in, ctab, stab, q_norm.reshape(1, -1), kv_norm.reshape(1, -1), wq2, wkv2)
    o_a = _mla_attn(q, k, v, batch, seq)
    o_b = _gdn(act, gates, z, _pad_lane_row(a_log, GDN_H), _pad_lane_row(dt_bias, GDN_H),
               o_norm.reshape(1, -1), batch, seq)
    return o_a, o_b


def _odd_mixer(x, tabs, w, b_i, b_f, ml_norm, sinks, batch, seq):
    ctab, stab = tabs
    mq, mk, mv, mo, mg, sq, sk, sv = _proj_odd(x, w, ctab, stab)
    bias_row = _pad_lane_row(jnp.concatenate([b_i, b_f]), 0)
    o_c = _mlstm(mq, mk, mv, mo, mg, bias_row, ml_norm.reshape(1, -1), batch, seq)
    o_d = _swa(sq, sk, sv, _pad_lane_row(sinks, 0), batch, seq)
    return o_c, o_d


def _moe(x, xp, router_w, router_b, w_gate, w_up, w_down, layer, s_gate, s_up, s_down, ln_g, ln_b):
    t, d = x.shape
    bias_col = jnp.broadcast_to(router_b.reshape(-1, 1).astype(F32), (N_EXPERTS, LANES))
    idx, gate, rank, cnt = _router(x, router_w.T, bias_col)
    counts = cnt[:, 0].astype(jnp.int32)
    block = int(min(max(pl.next_power_of_2(t * TOP_K // N_EXPERTS) // 2, EXPERT_BLOCK_MIN), EXPERT_BLOCK_MAX))
    padded = (counts + block - 1) // block * block
    pad_end = jnp.cumsum(padded)
    pad_start = pad_end - padded
    start_col = jnp.broadcast_to(pad_start.astype(F32).reshape(-1, 1), (N_EXPERTS, LANES))
    dest = _dest_rows(idx, rank, start_col)
    n_blocks = t * TOP_K // block + N_EXPERTS
    rows = n_blocks * block
    block_row = jnp.arange(n_blocks, dtype=jnp.int32) * block
    block_e = jnp.minimum(jnp.sum((pad_end[None, :] <= block_row[:, None]).astype(jnp.int32), axis=1), N_EXPERTS - 1)
    n_used = (pad_end[-1:] // block).astype(jnp.int32)
    live_end = jnp.sum(jnp.where(block_e[:, None] == jnp.arange(N_EXPERTS, dtype=jnp.int32)[None, :],
                                 (pad_start + counts)[None, :], 0), axis=1)
    n_valid = jnp.clip(live_end - block_row, 0, block).astype(jnp.int32)
    xs = _sc_scatter_rows(xp, dest, rows)
    ys = _experts(block_e, n_used, n_valid, xs, w_gate, w_up, w_down, layer, block)
    picked = _sc_gather_rows(ys, dest.reshape(-1)).reshape(TOP_K, t, d // 2)
    shared = _shared_expert(xp, s_gate.astype(BF16), s_up.astype(BF16), s_down.astype(BF16))
    return _combine(x, gate.T, picked, shared, ln_g.reshape(1, -1), ln_b.reshape(1, -1))


def kernel(x, positions, ev_w_in, mla_q_norm, mla_w_qb, mla_kv_norm, mla_w_kvb, gdn_conv, gdn_a_log, gdn_dt_bias, gdn_norm, ev_w_out, od_w_in, mlstm_b_i, mlstm_b_f, mlstm_norm, swa_sinks, od_w_out, ln1_g, ln1_b, router_w, router_b, moe_w_gate, moe_w_up, moe_w_down, shared_w_gate, shared_w_up, shared_w_down, ln2_g, ln2_b):
    batch, seq, d = x.shape
    streams = STREAMS if batch % STREAMS == 0 else 1
    sb = batch // streams
    ts = sb * seq
    hs, tabs_m, tabs_s = [], [], []
    for s in range(streams):
        pos = positions[s * sb:(s + 1) * sb].reshape(ts, 1).astype(F32)
        tabs_m.append(_rope_tables(pos, _rope_rows(MLA_ROPE, MLA_NOPE, MLA_NOPE)))
        tabs_s.append(_rope_tables(pos, _rope_rows(SWA_D, 0, 0, heads=LANES // SWA_D)))
        hs.append(x[s * sb:(s + 1) * sb].reshape(ts, d))
    for layer in range(DEPTH):
        j = layer // 2
        if layer % 2 == 0:
            weights = _even_weights(ev_w_in[j], mla_w_qb[j], mla_w_kvb[j])
            w_out = ev_w_out[j].astype(BF16)
        else:
            weights = _take_cols(od_w_in[j], _odd_in_cols())
            w_out = od_w_out[j].astype(BF16)
        for s in range(streams):
            h = hs[s]
            if layer % 2 == 0:
                a1, a2 = _even_mixer(h, tabs_m[s], weights, mla_q_norm[j], mla_kv_norm[j], gdn_conv[j], gdn_a_log[j],
                                     gdn_dt_bias[j], gdn_norm[j], sb, seq)
            else:
                a1, a2 = _odd_mixer(h, tabs_s[s], weights, mlstm_b_i[j], mlstm_b_f[j], mlstm_norm[j], swa_sinks[j], sb, seq)
            h, hp = _outproj_ln(h, a1, a2, w_out, ln1_g[layer].reshape(1, -1), ln1_b[layer].reshape(1, -1))
            hs[s] = _moe(h, hp, router_w[layer], router_b[layer], moe_w_gate, moe_w_up, moe_w_down, layer,
                         shared_w_gate[layer], shared_w_up[layer], shared_w_down[layer], ln2_g[layer], ln2_b[layer])
    return jnp.concatenate([h.reshape(sb, seq, d) for h in hs], axis=0)
```

```python
import functools
import math

import numpy as np
import jax
import jax.numpy as jnp
from jax import lax
from jax.experimental import pallas as pl
from jax.experimental.pallas import tpu as pltpu
from jax.experimental.pallas import tpu_sc as plsc

F32 = jnp.float32
BF16 = jnp.bfloat16
HI = lax.Precision.HIGHEST

D_MODEL = 1024
DEPTH = 4
ROPE_THETA = 10000.0
EPS = 1e-6
LN_EPS = 1e-5
MLA_H, MLA_NOPE, MLA_ROPE, MLA_V = 8, 64, 32, 64
Q_LORA, KV_LORA = 256, 128
GDN_H, GDN_DK, GDN_DV, CONV_W, GDN_CHUNK = 4, 128, 128, 4, 64
ML_H, ML_DK, ML_DV, ML_CHUNK = 4, 64, 128, 64
SWA_H, SWA_KV, SWA_D, WINDOW = 8, 2, 64, 128
N_EXPERTS, N_GROUPS, TOPK_GROUPS, TOP_K = 64, 8, 4, 8
D_EXPERT, D_SHARED = 256, 256
ROUTED_SCALE = 2.5
DN_ALPHA = (2 * DEPTH) ** 0.25

LANES = 128
V7X_VMEM_BYTES = 64 * 1024 * 1024
VMEM_LIMIT = 48 * 1024 * 1024

EXPERT_BLOCK_MIN = 256
EXPERT_BLOCK_MAX = 1024
STREAMS = 1
EXPERT_SUBBLOCKS = 2
SWA_SEQS_PER_STEP = 8
MLSTM_SEQS_PER_STEP = 2
GDN_SEQS_PER_STEP = 8
SC_CHUNK = 64


def _cparams(sem, vmem=VMEM_LIMIT):
    return pltpu.CompilerParams(dimension_semantics=sem, vmem_limit_bytes=vmem)


def _dot(a, b, precision=None):
    return jnp.dot(a, b, preferred_element_type=F32, precision=precision)


def _dot_nt(a, b, precision=None):
    return lax.dot_general(a, b, (((1,), (1,)), ((), ())), preferred_element_type=F32, precision=precision)


def _dot_tn(a, b, precision=None):
    return lax.dot_general(a, b, (((0,), (0,)), ((), ())), preferred_element_type=F32, precision=precision)


def _split2(a):
    hi = a.astype(BF16)
    lo = (a - hi.astype(F32)).astype(BF16)
    return hi, lo


def _split3(a):
    p1 = a.astype(BF16)
    r = a - p1.astype(F32)
    p2 = r.astype(BF16)
    p3 = (r - p2.astype(F32)).astype(BF16)
    return p1, p2, p3


def _dot3(a, b, dot=_dot):
    ah, al = _split2(a)
    bh, bl = _split2(b)
    return dot(ah, bh) + (dot(ah, bl) + dot(al, bh))


def _dot_sel(sel, b, dot=_dot):
    sel = sel.astype(BF16)
    p1, p2, p3 = _split3(b)
    return dot(sel, p1) + (dot(sel, p2) + dot(sel, p3))


def _sigmoid(x):
    return 1.0 / (1.0 + jnp.exp(-x))


def _softplus(x):
    return jnp.maximum(x, 0.0) + jnp.log(1.0 + jnp.exp(-jnp.abs(x)))


def _silu(x):
    return x * _sigmoid(x)


def _lane_bcast(x, c):
    return jnp.broadcast_to(x[:, c:c + 1], x.shape)


def _iota2(shape, dim):
    return lax.broadcasted_iota(jnp.int32, shape, dim)


def _rope_kernel(pos_ref, rows_ref, c_ref, s_ref):
    ang = pos_ref[...] * rows_ref[0:1, :]
    c_ref[...] = rows_ref[1:2, :] * jnp.cos(ang) + rows_ref[2:3, :]
    s_ref[...] = rows_ref[3:4, :] * jnp.sin(ang)


def _rope_tables(pos, rows, tm=512):
    t = pos.shape[0]
    return pl.pallas_call(
        _rope_kernel,
        grid=(t // tm,),
        in_specs=[pl.BlockSpec((tm, 1), lambda i: (i, 0)), pl.BlockSpec((8, LANES), lambda i: (0, 0))],
        out_specs=[pl.BlockSpec((tm, LANES), lambda i: (i, 0))] * 2,
        out_shape=[jax.ShapeDtypeStruct((t, LANES), F32)] * 2,
        compiler_params=_cparams(("arbitrary",)),
        name="rope_tables",
    )(pos, rows)


def _rope_rows(dim, first_lane, pad_one_lanes, heads=1):
    half = dim // 2
    inv = ROPE_THETA ** (-(np.arange(0, dim, 2, dtype=np.float32) / dim))
    rows = np.zeros((8, LANES), np.float32)
    for h in range(heads):
        lo = slice(first_lane + h * dim, first_lane + h * dim + half)
        hi = slice(first_lane + h * dim + half, first_lane + (h + 1) * dim)
        rows[0, lo] = inv
        rows[0, hi] = inv
        rows[1, lo] = 1.0
        rows[1, hi] = 1.0
        rows[3, lo] = -1.0
        rows[3, hi] = 1.0
    rows[2, :pad_one_lanes] = 1.0
    return jnp.asarray(rows)


def _proj_kernel(x_ref, w_ref, *out_refs, offsets):
    xb = x_ref[...].astype(BF16)
    for o_ref, (a, b) in zip(out_refs, offsets):
        o_ref[...] = _dot(xb, w_ref[:, a:b]).astype(o_ref.dtype)


def _proj(x, w, widths, dtypes, tm=512):
    t, k = x.shape
    offs = np.concatenate([[0], np.cumsum(widths)]).tolist()
    offsets = tuple((offs[i], offs[i + 1]) for i in range(len(widths)))
    return pl.pallas_call(
        functools.partial(_proj_kernel, offsets=offsets),
        grid=(t // tm,),
        in_specs=[pl.BlockSpec((tm, k), lambda i: (i, 0)), pl.BlockSpec(w.shape, lambda i: (0, 0))],
        out_specs=[pl.BlockSpec((tm, n), lambda i: (i, 0)) for n in widths],
        out_shape=[jax.ShapeDtypeStruct((t, n), dt) for n, dt in zip(widths, dtypes)],
        compiler_params=_cparams(("arbitrary",)),
        name="in_proj",
    )(x, w)


def _proj_even_kernel(x_ref, w_ref, cw_ref, mla_ref, act_ref, z_ref, g_ref, ext_ref, *, tiles_per_seq):
    tm = x_ref.shape[0]
    o = np.concatenate([[0], np.cumsum(EV_WIDTHS)]).tolist()
    @pl.when(pl.program_id(0) % tiles_per_seq == 0)
    def _():
        ext_ref[0:8, :] = jnp.zeros((8, ext_ref.shape[1]), F32)

    xb = x_ref[...].astype(BF16)
    nchunk = 3
    cw = EV_WIDTHS[1] // nchunk

    def project(ci):
        ext_ref[8:8 + tm, ci * cw:(ci + 1) * cw] = _dot(xb, w_ref[:, o[1] + ci * cw:o[1] + (ci + 1) * cw])

    project(0)
    for ci in range(nchunk):
        if ci + 1 < nchunk:
            project(ci + 1)
        else:
            mla_ref[...] = _dot(xb, w_ref[:, o[0]:o[1]])
            z_ref[...] = _dot(xb, w_ref[:, o[2]:o[3]]).astype(z_ref.dtype)
            g_ref[...] = _dot(xb, w_ref[:, o[3]:o[4]])
        cols = slice(ci * cw, (ci + 1) * cw)
        conv = cw_ref[0:1, cols] * ext_ref[5:5 + tm, cols]
        for j in range(1, CONV_W):
            conv = conv + cw_ref[j:j + 1, cols] * ext_ref[5 + j:5 + j + tm, cols]
        act_ref[:, cols] = _silu(conv).astype(act_ref.dtype)
    ext_ref[0:8, :] = ext_ref[tm:tm + 8, :]


def _proj_even(x, w, conv_w, seq, tm=512):
    t, k = x.shape
    tm = min(tm, seq)
    row = lambda i: (i, 0)
    fix = lambda i: (0, 0)
    return pl.pallas_call(
        functools.partial(_proj_even_kernel, tiles_per_seq=seq // tm),
        grid=(t // tm,),
        in_specs=[pl.BlockSpec((tm, k), row), pl.BlockSpec(w.shape, fix), pl.BlockSpec(conv_w.shape, fix)],
        out_specs=[pl.BlockSpec((tm, n), row) for n in EV_WIDTHS],
        out_shape=[jax.ShapeDtypeStruct((t, n), F32) for n in EV_WIDTHS],
        scratch_shapes=[pltpu.VMEM((tm + 8, EV_WIDTHS[1]), F32)],
        compiler_params=_cparams(("arbitrary",)),
        name="in_proj",
    )(x, w, conv_w)


OD_SEG = dict(mq=(0, 512), mk=(512, 1024), mv=(1024, 1536), mo=(1536, 2048), gates=(2048, 2176),
              sq=(2176, 2688), sqsw=(2688, 3200), sk=(3200, 3456), sksw=(3456, 3712), sv=(3712, 4224))
OD_COLS = 4224


def _proj_odd_kernel(x_ref, w_ref, c_ref, s_ref, mq_ref, mk_ref, mv_ref, mo_ref, mg_ref, sq_ref, sk_ref, sv_ref):
    xb = x_ref[...].astype(BF16)

    def seg(name):
        a, b = OD_SEG[name]
        return _dot(xb, w_ref[:, a:b])

    mq_ref[...] = seg("mq").astype(mq_ref.dtype)
    mk_ref[...] = seg("mk").astype(mk_ref.dtype)
    mv_ref[...] = seg("mv").astype(mv_ref.dtype)
    mo_ref[...] = seg("mo").astype(mo_ref.dtype)
    mg_ref[...] = seg("gates")
    c = c_ref[...]
    s = s_ref[...]
    c8 = jnp.concatenate([c] * (SWA_H // 2), axis=1)
    s8 = jnp.concatenate([s] * (SWA_H // 2), axis=1)
    sq_ref[...] = (seg("sq") * c8 + seg("sqsw") * s8).astype(sq_ref.dtype)
    c2 = jnp.concatenate([c] * SWA_KV, axis=1)
    s2 = jnp.concatenate([s] * SWA_KV, axis=1)
    sk_ref[...] = (seg("sk") * c2 + seg("sksw") * s2).astype(sk_ref.dtype)
    sv_ref[...] = seg("sv").astype(sv_ref.dtype)


def _proj_odd(x, w, ctab, stab, tm=512):
    t, k = x.shape
    widths = (512, 512, 512, 512, 128, SWA_H * SWA_D, SWA_KV * LANES, 2 * SWA_KV * LANES)
    dtypes = (F32, F32, F32, F32, F32, BF16, BF16, BF16)
    return pl.pallas_call(
        _proj_odd_kernel,
        grid=(t // tm,),
        in_specs=[pl.BlockSpec((tm, k), lambda i: (i, 0)), pl.BlockSpec(w.shape, lambda i: (0, 0)),
                  pl.BlockSpec((tm, LANES), lambda i: (i, 0)), pl.BlockSpec((tm, LANES), lambda i: (i, 0))],
        out_specs=[pl.BlockSpec((tm, n), lambda i: (i, 0)) for n in widths],
        out_shape=[jax.ShapeDtypeStruct((t, n), dt) for n, dt in zip(widths, dtypes)],
        compiler_params=_cparams(("arbitrary",)),
        name="in_proj_odd",
    )(x, w, ctab, stab)


def _rms(x, g):
    return x * lax.rsqrt(jnp.mean(x * x, axis=-1, keepdims=True) + EPS) * g


def _mla_prep_kernel(in_ref, c_ref, s_ref, qn_ref, kvn_ref, wq_ref, wkv_ref, q_ref, k_ref, v_ref):
    hw = MLA_H * LANES
    c = c_ref[...]
    s = s_ref[...]
    c8 = jnp.concatenate([c] * MLA_H, axis=1)
    s8 = jnp.concatenate([s] * MLA_H, axis=1)
    cqn = _rms(in_ref[:, 0:Q_LORA], qn_ref[...]).astype(BF16)
    qq = _dot(cqn, wq_ref[...])
    scale = (MLA_NOPE + MLA_ROPE) ** -0.5
    q_ref[...] = ((qq[:, :hw] * c8 + qq[:, hw:] * s8) * scale).astype(q_ref.dtype)
    ckvn = _rms(in_ref[:, Q_LORA:Q_LORA + KV_LORA], kvn_ref[...]).astype(BF16)
    kv = _dot(ckvn, wkv_ref[...])
    o = Q_LORA + KV_LORA
    krr = in_ref[:, o:o + LANES] * c + in_ref[:, o + LANES:o + 2 * LANES] * s
    k_ref[...] = (kv[:, :hw] + jnp.concatenate([krr] * MLA_H, axis=1)).astype(k_ref.dtype)
    v_ref[...] = kv[:, hw:].astype(v_ref.dtype)


def _mla_prep(mla_in, ctab, stab, qn, kvn, wq2, wkv2, tm=512):
    t = mla_in.shape[0]
    hw = MLA_H * LANES
    row = lambda i: (i, 0)
    fix = lambda i: (0, 0)
    return pl.pallas_call(
        _mla_prep_kernel,
        grid=(t // tm,),
        in_specs=[pl.BlockSpec((tm, mla_in.shape[1]), row), pl.BlockSpec((tm, LANES), row), pl.BlockSpec((tm, LANES), row),
                  pl.BlockSpec(qn.shape, fix), pl.BlockSpec(kvn.shape, fix),
                  pl.BlockSpec(wq2.shape, fix), pl.BlockSpec(wkv2.shape, fix)],
        out_specs=[pl.BlockSpec((tm, hw), row)] * 3,
        out_shape=[jax.ShapeDtypeStruct((t, hw), BF16)] * 3,
        compiler_params=_cparams(("arbitrary",)),
        name="mla_prep",
    )(mla_in, ctab, stab, qn, kvn, wq2, wkv2)


def _mla_attn_kernel(q_ref, k_ref, v_ref, o_ref, *, tq):
    i = pl.program_id(2)
    neg = -1e30
    lane = _iota2((tq, LANES), 1)
    ones_lane = (MLA_V, 0)

    def chunk(j, carry, masked):
        start = pl.multiple_of(j * tq, tq)
        out = []
        for hh in range(2):
            m, acc = carry[hh]
            q = q_ref[:, hh * LANES:(hh + 1) * LANES]
            kc = k_ref[pl.ds(start, tq), hh * LANES:(hh + 1) * LANES]
            vc = v_ref[pl.ds(start, tq), hh * LANES:(hh + 1) * LANES]
            vc = jnp.where(lane == ones_lane[hh], jnp.ones_like(vc), vc)
            s = _dot_nt(q, kc)
            if masked:
                s = jnp.where(_iota2(s.shape, 0) >= _iota2(s.shape, 1), s, neg)
            m_new = jnp.maximum(m, jnp.max(s, axis=-1, keepdims=True))
            alpha = jnp.exp(m - m_new)
            p = jnp.exp(s - m_new)
            acc = alpha * acc + _dot(p.astype(BF16), vc)
            out.append((m_new, acc))
        return tuple(out)

    one = (jnp.full((tq, 1), neg, F32), jnp.zeros((tq, LANES), F32))
    carry = lax.fori_loop(0, i, lambda j, c: chunk(j, c, False), (one, one))
    (_, acc0), (_, acc1) = chunk(i, carry, True)
    o0 = acc0 / _lane_bcast(acc0, ones_lane[0])
    o1 = acc1 / _lane_bcast(acc1, ones_lane[1])
    o_ref[...] = jnp.where(lane < MLA_V, o0, o1).astype(o_ref.dtype)


def _mla_attn(q, k, v, batch, seq, tq=512):
    tq = min(tq, seq)
    nq = seq // tq
    pairs = MLA_H // 2
    return pl.pallas_call(
        functools.partial(_mla_attn_kernel, tq=tq),
        grid=(batch, pairs, nq),
        in_specs=[pl.BlockSpec((tq, 2 * LANES), lambda b, p, i: (b * nq + i, p)),
                  pl.BlockSpec((seq, 2 * LANES), lambda b, p, i: (b, p)),
                  pl.BlockSpec((seq, 2 * LANES), lambda b, p, i: (b, p))],
        out_specs=pl.BlockSpec((tq, LANES), lambda b, p, i: (b * nq + i, p)),
        out_shape=jax.ShapeDtypeStruct((batch * seq, pairs * LANES), BF16),
        compiler_params=_cparams(("arbitrary", "arbitrary", "arbitrary")),
        name="mla_attn",
    )(q, k, v)


def _unit_lower_inverse_many(ns):
    c = ns[0].shape[0]
    eye = (_iota2((c, c), 0) == _iota2((c, c), 1)).astype(F32)
    xs = [-n for n in ns]
    ps = [eye + x for x in xs]
    xb = [x.astype(BF16) for x in xs]
    for _ in range(int(math.log2(c)) - 1):
        xs = [_dot(b, b) for b in xb]
        xb = [x.astype(BF16) for x in xs]
        ps = [p + _dot(p.astype(BF16), b) for p, b in zip(ps, xb)]
    return ps


def _gdn_kernel(act_ref, g_ref, z_ref, al_ref, dt_ref, on_ref, o_ref, st_ref):
    c = GDN_CHUNK
    hd = GDN_DK
    nqk = GDN_H * GDN_DK

    @pl.when(pl.program_id(1) == 0)
    def _():
        st_ref[...] = jnp.zeros(st_ref.shape, F32)

    tri = (_iota2((c, c), 0) >= _iota2((c, c), 1)).astype(F32)
    row_ge = _iota2((c, c), 0) >= _iota2((c, c), 1)
    row_gt = _iota2((c, c), 0) > _iota2((c, c), 1)
    lane = _iota2((c, LANES), 1)

    seqs = []
    for bb in range(act_ref.shape[0]):
        gates = g_ref[bb]
        g_all = -jnp.exp(al_ref[...]) * _softplus(gates + dt_ref[...])
        gc_all = _dot_sel(tri, g_all)
        seqs.append(dict(beta_all=_sigmoid(gates), gc_all=gc_all, gc_parts=_split3(gc_all)))
    units = []
    for bb, sq in enumerate(seqs):
        for h in range(GDN_H):
            q = act_ref[bb, :, h * hd:(h + 1) * hd].astype(F32)
            k = act_ref[bb, :, nqk + h * hd:nqk + (h + 1) * hd].astype(F32)
            v = act_ref[bb, :, 2 * nqk + h * GDN_DV:2 * nqk + (h + 1) * GDN_DV].astype(F32)
            q = q * lax.rsqrt(jnp.sum(q * q, axis=-1, keepdims=True) + EPS) * (GDN_DK ** -0.5)
            k = k * lax.rsqrt(jnp.sum(k * k, axis=-1, keepdims=True) + EPS)
            beta = _lane_bcast(sq["beta_all"], h)
            gcol = _lane_bcast(sq["gc_all"], GDN_H + h)
            units.append(dict(bb=bb, h=h, q=q, k=k, v=v, beta=beta, gcol=gcol, kb=k * beta, parts=sq["gc_parts"]))
    for u in units:
        pick = (lane == GDN_H + u["h"]).astype(BF16)
        p0, p1, p2 = u["parts"]
        u["grow"] = _dot_nt(pick, p0) + (_dot_nt(pick, p1) + _dot_nt(pick, p2))
        u["kk"] = _dot3(u["kb"], u["k"], _dot_nt)
        u["qk"] = _dot_nt(u["q"].astype(BF16), u["k"].astype(BF16))
    for u in units:
        gcol = u["gcol"]
        decay = jnp.exp(jnp.where(row_ge, gcol[:, :c] - u["grow"], -jnp.inf))
        eg = jnp.exp(gcol)
        glast = gcol[c - 1:c, :]
        u["lower"] = jnp.where(row_gt, u["kk"] * decay, 0.0)
        u["rhs"] = jnp.concatenate([u["v"] * u["beta"], u["kb"] * eg], axis=1)
        u["attn"] = u["qk"] * decay
        u["qg"] = (u["q"] * eg).astype(BF16)
        u["kg"] = (u["k"] * jnp.exp(glast - gcol)).astype(BF16)
        u["gl"] = jnp.exp(glast)

    tinvs = _unit_lower_inverse_many([u["lower"] for u in units])
    uws = []
    for u, tinv in zip(units, tinvs):
        uws.append(_dot(tinv.astype(BF16), u["rhs"].astype(BF16)))
    states = [st_ref[u["bb"], u["h"]] for u in units]
    sbs = [s.astype(BF16) for s in states]
    vnews = [(uw[:, :GDN_DV] - _dot(uw[:, GDN_DV:].astype(BF16), sb)).astype(BF16) for uw, sb in zip(uws, sbs)]
    for u, state, sb, vnb in zip(units, states, sbs, vnews):
        bb, h = u["bb"], u["h"]
        o = _dot(u["qg"], sb) + _dot(u["attn"].astype(BF16), vnb)
        st_ref[bb, h] = state * u["gl"] + _dot_tn(u["kg"], vnb)
        o = _rms(o, on_ref[...]) * _silu(z_ref[bb, :, h * GDN_DV:(h + 1) * GDN_DV].astype(F32))
        o_ref[bb, :, h * GDN_DV:(h + 1) * GDN_DV] = o.astype(o_ref.dtype)


def _gdn(act, gates, z, a_row, dt_row, o_norm, batch, seq):
    c = GDN_CHUNK
    nc = seq // c
    w3 = act.shape[1]
    wo = GDN_H * GDN_DV
    nb = min(GDN_SEQS_PER_STEP, batch)
    row = lambda b, i: (b, i, 0)
    fix = lambda b, i: (0, 0)
    out = pl.pallas_call(
        _gdn_kernel,
        grid=(batch // nb, nc),
        in_specs=[pl.BlockSpec((nb, c, w3), row), pl.BlockSpec((nb, c, LANES), row), pl.BlockSpec((nb, c, wo), row),
                  pl.BlockSpec((1, LANES), fix), pl.BlockSpec((1, LANES), fix), pl.BlockSpec((1, GDN_DV), fix)],
        out_specs=pl.BlockSpec((nb, c, wo), row),
        out_shape=jax.ShapeDtypeStruct((batch, seq, wo), BF16),
        scratch_shapes=[pltpu.VMEM((nb, GDN_H, GDN_DK, GDN_DV), F32)],
        compiler_params=_cparams(("arbitrary", "arbitrary")),
        name="gdn",
    )(act.reshape(batch, seq, w3), gates.reshape(batch, seq, LANES), z.reshape(batch, seq, wo), a_row, dt_row, o_norm)
    return out.reshape(batch * seq, wo)


def _mlstm_kernel(q_ref, k_ref, v_ref, og_ref, g_ref, bias_ref, nrm_ref, o_ref, c_ref, n_ref, m_ref):
    @pl.when(pl.program_id(1) == 0)
    def _():
        c_ref[...] = jnp.zeros(c_ref.shape, F32)
        n_ref[...] = jnp.zeros(n_ref.shape, F32)
        m_ref[...] = jnp.zeros(m_ref.shape, F32)

    c = ML_CHUNK
    tri = (_iota2((c, c), 0) >= _iota2((c, c), 1)).astype(F32)
    row_ge = _iota2((c, c), 0) >= _iota2((c, c), 1)
    ones = jnp.ones((c, LANES), F32)
    lane = _iota2((c, LANES), 1)

    units = []
    for bb in range(q_ref.shape[0]):
        pre = g_ref[bb] + bias_ref[...]
        logf = jnp.minimum(pre, 0.0) - jnp.log(1.0 + jnp.exp(-jnp.abs(pre)))
        bcum_all = _dot_sel(tri, logf)
        for h in range(ML_H):
            q = q_ref[bb, :, h * LANES:(h + 1) * LANES].astype(F32)
            k = k_ref[bb, :, h * LANES:(h + 1) * LANES].astype(F32) * (ML_DK ** -0.5)
            units.append(dict(bb=bb, h=h, q=q, k=k, qb=q.astype(BF16), vb=v_ref[bb, :, h * ML_DV:(h + 1) * ML_DV].astype(BF16),
                              bcol=_lane_bcast(bcum_all, ML_H + h),
                              icol=_lane_bcast(pre, h),
                              col=jnp.where(lane == h, pre, 0.0) - jnp.where(lane == ML_H + h, bcum_all, 0.0),
                              m_st=m_ref[bb, h], cst=c_ref[bb, h], nst=n_ref[bb, h]))
    for u in units:
        u["row"] = _dot_sel(ones, u["col"], _dot_nt)
        u["qk"] = _dot_nt(u["qb"], u["k"].astype(BF16))
        u["qc"] = _dot(u["qb"], u["cst"].astype(BF16))
    for u in units:
        u["d"] = jnp.where(row_ge, u["bcol"][:, :c] + u["row"], -jnp.inf)
        u["inter"] = u["bcol"] + u["m_st"]
        u["m_t"] = jnp.maximum(u["inter"], jnp.max(u["d"], axis=-1, keepdims=True))
        u["b_end"] = u["bcol"][c - 1:c, :]
        u["a"] = u["b_end"] - u["bcol"] + u["icol"]
        u["m_new"] = jnp.maximum(u["b_end"] + u["m_st"], jnp.max(u["a"], axis=0, keepdims=True))
    for u in units:
        u["w_inter"] = jnp.exp(u["inter"] - u["m_t"])
        u["p"] = jnp.exp(u["d"] - u["m_t"][:, :c]) * u["qk"]
        u["keep"] = jnp.exp(u["b_end"] + u["m_st"] - u["m_new"])
        u["ks"] = u["k"] * jnp.exp(u["a"] - u["m_new"])
    for u in units:
        u["pv"] = _dot(u["p"].astype(BF16), u["vb"])
        u["kv"] = _dot_tn(u["ks"].astype(BF16), u["vb"])
    for u in units:
        u["den"] = (u["w_inter"] * jnp.sum(u["q"] * u["nst"], axis=-1, keepdims=True)
                    + jnp.sum(u["p"], axis=-1, keepdims=True))
    for u in units:
        bb, h = u["bb"], u["h"]
        num = u["w_inter"] * u["qc"] + u["pv"]
        hc = num / jnp.maximum(jnp.abs(u["den"]), jnp.exp(-u["m_t"]))
        c_ref[bb, h] = u["cst"] * u["keep"] + u["kv"]
        n_ref[bb, h] = u["nst"] * u["keep"] + jnp.sum(u["ks"], axis=0, keepdims=True)
        m_ref[bb, h] = u["m_new"]
        hn = (_rms(hc, nrm_ref[:, h * ML_DV:(h + 1) * ML_DV])
              * _sigmoid(og_ref[bb, :, h * ML_DV:(h + 1) * ML_DV].astype(F32)))
        o_ref[bb, :, h * ML_DV:(h + 1) * ML_DV] = hn.astype(o_ref.dtype)


def _mlstm(mq, mk, mv, mo, gates, bias_row, norm_row, batch, seq):
    c = ML_CHUNK
    nc = seq // c
    nb = min(MLSTM_SEQS_PER_STEP, batch)
    row = lambda b, i: (b, i, 0)
    fix = lambda b, i: (0, 0)
    wide = ML_H * LANES
    r3 = lambda a: a.reshape(batch, seq, a.shape[-1])
    out = pl.pallas_call(
        _mlstm_kernel,
        grid=(batch // nb, nc),
        in_specs=[pl.BlockSpec((nb, c, wide), row), pl.BlockSpec((nb, c, wide), row), pl.BlockSpec((nb, c, wide), row),
                  pl.BlockSpec((nb, c, wide), row), pl.BlockSpec((nb, c, LANES), row),
                  pl.BlockSpec((1, LANES), fix), pl.BlockSpec((1, wide), fix)],
        out_specs=pl.BlockSpec((nb, c, wide), row),
        out_shape=jax.ShapeDtypeStruct((batch, seq, wide), BF16),
        scratch_shapes=[pltpu.VMEM((nb, ML_H, LANES, ML_DV), F32), pltpu.VMEM((nb, ML_H, 1, LANES), F32),
                        pltpu.VMEM((nb, ML_H, 1, LANES), F32)],
        compiler_params=_cparams(("arbitrary", "arbitrary")),
        name="mlstm",
    )(r3(mq), r3(mk), r3(mv), r3(mo), r3(gates), bias_row, norm_row)
    return out.reshape(batch * seq, wide)


def _swa_kernel(q_ref, kc_ref, kp_ref, vc_ref, vp_ref, sink_ref, o_ref):
    w = WINDOW
    n = pl.program_id(1)
    scale = SWA_D ** -0.5
    qi = _iota2((w, w), 0)
    kj = _iota2((w, w), 1)
    mask_c = kj <= qi
    mask_p = jnp.logical_and(kj > qi, n > 0)
    grp = SWA_H // SWA_KV
    neg = -1e30
    units = [(bb, h) for bb in range(q_ref.shape[0]) for h in range(SWA_H)]
    scores = []
    half_of_lane = _iota2((w, LANES), 1) // SWA_D
    for bb, h in units:
        g = h // grp
        pair = q_ref[bb, :, (h // 2) * LANES:(h // 2 + 1) * LANES]
        q = jnp.where(half_of_lane == h % 2, pair, jnp.zeros_like(pair))
        scores.append((_dot_nt(q, kc_ref[bb, :, g * LANES:(g + 1) * LANES]),
                       _dot_nt(q, kp_ref[bb, :, g * LANES:(g + 1) * LANES])))
    masked, tops, exps, dens, probs = [], [], [], [], {}
    for sc, sp in scores:
        masked.append((jnp.where(mask_c, sc * scale, neg), jnp.where(mask_p, sp * scale, neg)))
    for (bb, h), (s_c, s_p) in zip(units, masked):
        tops.append(jnp.maximum(jnp.max(jnp.maximum(s_c, s_p), axis=-1, keepdims=True), sink_ref[:, h:h + 1]))
    for (s_c, s_p), m in zip(masked, tops):
        exps.append((jnp.where(mask_c, jnp.exp(s_c - m), 0.0), jnp.where(mask_p, jnp.exp(s_p - m), 0.0)))
    ones_b = jnp.ones((w, LANES), BF16)
    for (bb, h), (p_c, p_p), m in zip(units, exps, tops):
        p_c, p_p = p_c.astype(BF16), p_p.astype(BF16)
        probs[bb, h] = (p_c, p_p)
        dens.append(_dot(p_c, ones_b) + _dot(p_p, ones_b) + jnp.exp(sink_ref[:, h:h + 1] - m))
    inv = {u: 1.0 / den for u, den in zip(units, dens)}
    for bb in range(q_ref.shape[0]):
        for pair in range(SWA_H // 2):
            acc = None
            for sub in range(2):
                h = 2 * pair + sub
                vcol = (2 * (h // grp) + sub) * LANES
                p_c, p_p = probs[bb, h]
                part = (_dot(p_c, vc_ref[bb, :, vcol:vcol + LANES]) + _dot(p_p, vp_ref[bb, :, vcol:vcol + LANES])) * inv[bb, h]
                acc = part if acc is None else acc + part
            o_ref[bb, :, pair * LANES:(pair + 1) * LANES] = acc.astype(o_ref.dtype)


def _swa(sq, sk, sv, sinks_row, batch, seq):
    w = WINDOW
    nb = seq // w
    ns = min(SWA_SEQS_PER_STEP, batch)
    wo = SWA_H * SWA_D
    cur = lambda b, n: (b, n, 0)
    prev = lambda b, n: (b, jnp.maximum(n - 1, 0), 0)
    r3 = lambda a: a.reshape(batch, seq, a.shape[-1])
    q3, k3, v3 = r3(sq), r3(sk), r3(sv)
    out = pl.pallas_call(
        _swa_kernel,
        grid=(batch // ns, nb),
        in_specs=[pl.BlockSpec((ns, w, sq.shape[1]), cur),
                  pl.BlockSpec((ns, w, sk.shape[1]), cur), pl.BlockSpec((ns, w, sk.shape[1]), prev),
                  pl.BlockSpec((ns, w, sv.shape[1]), cur), pl.BlockSpec((ns, w, sv.shape[1]), prev),
                  pl.BlockSpec((1, LANES), lambda b, n: (0, 0))],
        out_specs=pl.BlockSpec((ns, w, wo), cur),
        out_shape=jax.ShapeDtypeStruct((batch, seq, wo), BF16),
        compiler_params=_cparams(("arbitrary", "arbitrary")),
        name="swa",
    )(q3, k3, k3, v3, v3, sinks_row)
    return out.reshape(batch * seq, wo)


def _layer_norm(h, g, b):
    mu = jnp.mean(h, axis=-1, keepdims=True)
    d = h - mu
    var = jnp.mean(d * d, axis=-1, keepdims=True)
    return d * lax.rsqrt(var + LN_EPS) * g + b


def _outproj_kernel(x_ref, a1_ref, a2_ref, w_ref, g_ref, b_ref, o_ref, op_ref):
    k1 = a1_ref.shape[1]
    y = _dot(a1_ref[...].astype(BF16), w_ref[0:k1, :]) + _dot(a2_ref[...].astype(BF16), w_ref[k1:, :])
    h = _layer_norm(DN_ALPHA * x_ref[...] + y, g_ref[...], b_ref[...])
    o_ref[...] = h
    op_ref[...] = _pack_pairs(h)


def _outproj_ln(x, a1, a2, w, g, b, tm=512):
    t, d = x.shape
    row = lambda i: (i, 0)
    fix = lambda i: (0, 0)
    return pl.pallas_call(
        _outproj_kernel,
        grid=(t // tm,),
        in_specs=[pl.BlockSpec((tm, d), row), pl.BlockSpec((tm, a1.shape[1]), row), pl.BlockSpec((tm, a2.shape[1]), row),
                  pl.BlockSpec(w.shape, fix), pl.BlockSpec((1, d), fix), pl.BlockSpec((1, d), fix)],
        out_specs=[pl.BlockSpec((tm, d), row), pl.BlockSpec((tm, d // 2), row)],
        out_shape=[jax.ShapeDtypeStruct((t, d), F32), jax.ShapeDtypeStruct((t, d // 2), jnp.uint32)],
        compiler_params=_cparams(("arbitrary",)),
        name="outproj_ln",
    )(x, a1, a2, w, g, b)


def _first_index(x, m, iota_f, sentinel):
    return jnp.min(jnp.where(x == m, iota_f, sentinel), axis=0, keepdims=True)


def _router_kernel(x_ref, wt_ref, bias_ref, idx_ref, gate_ref, rank_ref, cnt_ref, carry_ref):
    tm = x_ref.shape[0]
    e = N_EXPERTS
    gs = e // N_GROUPS
    ninf = -jnp.inf

    @pl.when(pl.program_id(0) == 0)
    def _():
        carry_ref[...] = jnp.zeros(carry_ref.shape, F32)

    logits = _dot3(wt_ref[...], x_ref[...], _dot_nt)
    scores = _sigmoid(logits)
    sel = scores + bias_ref[:, 0:1]

    sub_f = _iota2((gs, tm), 0).astype(F32)
    gscore = []
    for g in range(N_GROUPS):
        blk = sel[g * gs:(g + 1) * gs, :]
        m1 = jnp.max(blk, axis=0, keepdims=True)
        i1 = _first_index(blk, m1, sub_f, float(gs))
        m2 = jnp.max(jnp.where(sub_f == i1, ninf, blk), axis=0, keepdims=True)
        gscore.append(m1 + m2)
    gsc = jnp.concatenate(gscore, axis=0)
    grp_f = _iota2((N_GROUPS, tm), 0).astype(F32)
    gmask = jnp.zeros((N_GROUPS, tm), F32)
    for _ in range(TOPK_GROUPS):
        m = jnp.max(gsc, axis=0, keepdims=True)
        gi = _first_index(gsc, m, grp_f, float(N_GROUPS))
        hit = grp_f == gi
        gmask = jnp.where(hit, 1.0, gmask)
        gsc = jnp.where(hit, ninf, gsc)
    masked = jnp.concatenate(
        [jnp.where(gmask[g:g + 1, :] > 0.0, sel[g * gs:(g + 1) * gs, :], ninf) for g in range(N_GROUPS)], axis=0)

    exp_f = _iota2((e, tm), 0).astype(F32)
    chosen = jnp.zeros((e, tm), F32)
    idxs, gates = [], []
    for _ in range(TOP_K):
        m = jnp.max(masked, axis=0, keepdims=True)
        ei = _first_index(masked, m, exp_f, float(e))
        hit = exp_f == ei
        idxs.append(ei)
        gates.append(jnp.sum(jnp.where(hit, scores, 0.0), axis=0, keepdims=True))
        chosen = jnp.where(hit, 1.0, chosen)
        masked = jnp.where(hit, ninf, masked)
    gate = jnp.concatenate(gates, axis=0)
    gate = gate / jnp.sum(gate, axis=0, keepdims=True) * ROUTED_SCALE
    idx_f = jnp.concatenate(idxs, axis=0)

    upper = (_iota2((tm, tm), 0) < _iota2((tm, tm), 1)).astype(BF16)
    before = _dot(chosen.astype(BF16), upper) + carry_ref[...][:, 0:1]
    ranks = [jnp.sum(jnp.where(exp_f == idxs[k], before, 0.0), axis=0, keepdims=True) for k in range(TOP_K)]
    carry_ref[...] = carry_ref[...] + jnp.sum(chosen, axis=1, keepdims=True)

    idx_ref[...] = idx_f.astype(jnp.int32)
    gate_ref[...] = gate
    rank_ref[...] = jnp.concatenate(ranks, axis=0).astype(jnp.int32)
    cnt_ref[...] = carry_ref[...]


def _router(x, wt, bias_col, tm=512):
    t, d = x.shape
    col = lambda i: (0, i)
    fix = lambda i: (0, 0)
    return pl.pallas_call(
        _router_kernel,
        grid=(t // tm,),
        in_specs=[pl.BlockSpec((tm, d), lambda i: (i, 0)), pl.BlockSpec(wt.shape, fix), pl.BlockSpec((N_EXPERTS, LANES), fix)],
        out_specs=[pl.BlockSpec((TOP_K, tm), col), pl.BlockSpec((TOP_K, tm), col), pl.BlockSpec((TOP_K, tm), col),
                   pl.BlockSpec((N_EXPERTS, LANES), fix)],
        out_shape=[jax.ShapeDtypeStruct((TOP_K, t), jnp.int32), jax.ShapeDtypeStruct((TOP_K, t), F32),
                   jax.ShapeDtypeStruct((TOP_K, t), jnp.int32), jax.ShapeDtypeStruct((N_EXPERTS, LANES), F32)],
        scratch_shapes=[pltpu.VMEM((N_EXPERTS, LANES), F32)],
        compiler_params=_cparams(("arbitrary",)),
        name="router",
    )(x, wt, bias_col)


def _dest_kernel(idx_ref, rank_ref, start_ref, dest_ref):
    tm = idx_ref.shape[1]
    exp_i = _iota2((N_EXPERTS, tm), 0)
    start = start_ref[:, 0:1]
    rows = [jnp.sum(jnp.where(exp_i == idx_ref[s:s + 1, :], start, 0.0), axis=0, keepdims=True) for s in range(TOP_K)]
    dest_ref[...] = jnp.concatenate(rows, axis=0).astype(jnp.int32) + rank_ref[...]


def _dest_rows(idx, rank, start_col, tm=2048):
    t = idx.shape[1]
    tm = min(tm, t)
    col = lambda i: (0, i)
    return pl.pallas_call(
        _dest_kernel,
        grid=(t // tm,),
        in_specs=[pl.BlockSpec((TOP_K, tm), col), pl.BlockSpec((TOP_K, tm), col),
                  pl.BlockSpec((N_EXPERTS, LANES), lambda i: (0, 0))],
        out_specs=pl.BlockSpec((TOP_K, tm), col),
        out_shape=jax.ShapeDtypeStruct((TOP_K, t), jnp.int32),
        compiler_params=_cparams(("arbitrary",)),
        name="moe_dest",
    )(idx, rank, start_col)


def _pack_pairs(x):
    n = x.shape[1] // 2
    hi = lax.bitcast_convert_type(x[:, :n].astype(BF16).astype(F32), jnp.uint32)
    lo = lax.bitcast_convert_type(x[:, n:].astype(BF16).astype(F32), jnp.uint32)
    return hi | (lo >> 16)


def _unpack_pairs(w):
    hi = lax.bitcast_convert_type(w & jnp.uint32(0xFFFF0000), F32)
    lo = lax.bitcast_convert_type(w << 16, F32)
    return hi, lo


def _sc_scatter_rows(xp, dest, rows, chunk=LANES):
    t, width = xp.shape
    info = plsc.get_sparse_core_info()
    ncores, nsub = info.num_cores, info.num_subcores
    per_worker = t // (ncores * nsub)
    nchunk = per_worker // chunk
    mesh = plsc.VectorSubcoreMesh(core_axis_name="c", subcore_axis_name="s")

    @functools.partial(
        pl.kernel, mesh=mesh,
        out_type=jax.ShapeDtypeStruct((rows, width), xp.dtype),
        scratch_types=[pltpu.VMEM((TOP_K, chunk), jnp.int32), pltpu.VMEM((chunk, width), xp.dtype), pltpu.SemaphoreType.DMA],
    )
    def scatter(xp_hbm, dest_hbm, out_hbm, idx_v, rows_v, sem):
        base = (lax.axis_index("s") * ncores + lax.axis_index("c")) * per_worker

        @pl.loop(0, nchunk)
        def _(i):
            off = pl.multiple_of(base + i * chunk, chunk)
            pltpu.sync_copy(dest_hbm.at[:, pl.ds(off, chunk)], idx_v)
            pltpu.sync_copy(xp_hbm.at[pl.ds(off, chunk)], rows_v)
            copies = [pltpu.async_copy(rows_v, out_hbm.at[idx_v.at[s]], sem) for s in range(TOP_K)]
            for cp in copies:
                cp.wait()

    return scatter(xp, dest)


def _experts_kernel(be_ref, nu_ref, nv_ref, first_ref, slot_ref, nxt_ref, xs_ref, wg_hbm, wu_hbm, wd_hbm, ys_ref,
                    wgf_ref, wuf_ref, wdf_ref, wgb_ref, wub_ref, wdb_ref, sem, *, layer):
    i = pl.program_id(0)

    def fetch(e, s):
        return [pltpu.make_async_copy(wg_hbm.at[layer, e], wgf_ref.at[s], sem.at[s]),
                pltpu.make_async_copy(wu_hbm.at[layer, e], wuf_ref.at[s], sem.at[s]),
                pltpu.make_async_copy(wd_hbm.at[layer, e], wdf_ref.at[s], sem.at[s])]

    @pl.when(i == 0)
    def _():
        for cp in fetch(be_ref[0], 0):
            cp.start()

    @pl.when(jnp.logical_and(first_ref[i] == 1, i < nu_ref[0]))
    def _():
        s = slot_ref[i]
        for cp in fetch(be_ref[i], s):
            cp.wait()
        wgb_ref[...] = wgf_ref[s].astype(BF16)
        wub_ref[...] = wuf_ref[s].astype(BF16)
        wdb_ref[...] = wdf_ref[s].astype(BF16)

        @pl.when(nxt_ref[i] >= 0)
        def _():
            for cp in fetch(nxt_ref[i], 1 - s):
                cp.start()

    @pl.when(i < nu_ref[0])
    def _():
        half = xs_ref.shape[1]
        sub = xs_ref.shape[0] // EXPERT_SUBBLOCKS
        acts = []
        for r in range(EXPERT_SUBBLOCKS):
            rows = pl.ds(r * sub, sub)
            live = (_iota2((sub, 1), 0) + r * sub) < nv_ref[i]
            xa, xb = _unpack_pairs(jnp.where(live, xs_ref[rows, :], jnp.uint32(0)))
            x = jnp.concatenate([xa.astype(BF16), xb.astype(BF16)], axis=1)
            acts.append((_dot(x, wgb_ref[...]), _dot(x, wub_ref[...])))
        outs = [_dot((_silu(gate) * up).astype(BF16), wdb_ref[...]) for gate, up in acts]
        for r, y in enumerate(outs):
            ys_ref[pl.ds(r * sub, sub), :] = _pack_pairs(y)


def _experts(block_e, n_used, n_valid, xs, wg, wu, wd, layer, block):
    rows, half = xs.shape
    d = 2 * half
    nb = rows // block
    pos = jnp.arange(nb, dtype=jnp.int32)
    first = jnp.concatenate([jnp.ones((1,), jnp.int32), (block_e[1:] != block_e[:-1]).astype(jnp.int32)])
    slot = (jnp.cumsum(first) - 1) % 2
    later = (pos[None, :] > pos[:, None]) & (block_e[None, :] != block_e[:, None]) & (pos[None, :] < n_used[0])
    nxt_pos = jnp.min(jnp.where(later, pos[None, :], nb), axis=1)
    nxt = jnp.where(nxt_pos < nb, block_e[jnp.minimum(nxt_pos, nb - 1)], -1)
    blk = lambda i, be, nu, *rest: (jnp.minimum(i, nu[0] - 1), 0)
    hbm = pl.BlockSpec(memory_space=pl.ANY)
    return pl.pallas_call(
        functools.partial(_experts_kernel, layer=layer),
        grid_spec=pltpu.PrefetchScalarGridSpec(
            num_scalar_prefetch=6,
            grid=(nb,),
            in_specs=[pl.BlockSpec((block, half), blk), hbm, hbm, hbm],
            out_specs=pl.BlockSpec((block, half), blk),
            scratch_shapes=[pltpu.VMEM((2, d, D_EXPERT), F32), pltpu.VMEM((2, d, D_EXPERT), F32),
                            pltpu.VMEM((2, D_EXPERT, d), F32),
                            pltpu.VMEM((d, D_EXPERT), BF16), pltpu.VMEM((d, D_EXPERT), BF16),
                            pltpu.VMEM((D_EXPERT, d), BF16), pltpu.SemaphoreType.DMA((2,))],
        ),
        out_shape=jax.ShapeDtypeStruct((rows, half), jnp.uint32),
        compiler_params=_cparams(("arbitrary",)),
        name="moe_experts",
    )(block_e, n_used, n_valid, first, slot.astype(jnp.int32), nxt.astype(jnp.int32), xs, wg, wu, wd)


def _sc_gather_rows(table, idx, chunk=SC_CHUNK):
    n = idx.shape[0]
    width = table.shape[1]
    info = plsc.get_sparse_core_info()
    ncores, nsub = info.num_cores, info.num_subcores
    per_worker = n // (ncores * nsub)
    nchunk = per_worker // chunk
    mesh = plsc.VectorSubcoreMesh(core_axis_name="c", subcore_axis_name="s")

    @functools.partial(
        pl.kernel, mesh=mesh,
        out_type=jax.ShapeDtypeStruct((n, width), table.dtype),
        scratch_types=[pltpu.VMEM((nchunk, chunk), jnp.int32), pltpu.VMEM((2, chunk, width), table.dtype),
                       pltpu.SemaphoreType.DMA((2,)), pltpu.SemaphoreType.DMA((2,))],
    )
    def gather(table_hbm, idx_hbm, out_hbm, idx_v, rows_v, gsem, wsem):
        wid = lax.axis_index("s") * ncores + lax.axis_index("c")
        base = wid * per_worker
        pltpu.sync_copy(idx_hbm.at[pl.ds(wid * nchunk, nchunk)], idx_v)

        def fetch(j, b):
            return pltpu.make_async_copy(table_hbm.at[idx_v.at[j]], rows_v.at[b], gsem.at[b])

        def flush(j, b):
            off = pl.multiple_of(base + j * chunk, chunk)
            return pltpu.make_async_copy(rows_v.at[b], out_hbm.at[pl.ds(off, chunk)], wsem.at[b])

        fetch(0, 0).start()

        @pl.loop(0, nchunk, step=2)
        def _(i):
            for b in range(2):
                j = i + b
                fetch(j, b).wait()

                @pl.when(j + 1 < nchunk)
                def _():
                    @pl.when(j >= 1)
                    def _():
                        flush(j - 1, 1 - b).wait()

                    fetch(j + 1, 1 - b).start()

                flush(j, b).start()

        flush(nchunk - 2, 0).wait()
        flush(nchunk - 1, 1).wait()

    return gather(table, idx.reshape(n // chunk, chunk))


def _shared_kernel(xp_ref, sg_ref, su_ref, sd_ref, o_ref):
    xa, xb = _unpack_pairs(xp_ref[...])
    x = jnp.concatenate([xa.astype(BF16), xb.astype(BF16)], axis=1)
    hs = _silu(_dot(x, sg_ref[...])) * _dot(x, su_ref[...])
    o_ref[...] = _pack_pairs(_dot(hs.astype(BF16), sd_ref[...]))


def _shared_expert(xp, sg, su, sd, tm=512):
    t, half = xp.shape
    row = lambda i: (i, 0)
    fix = lambda i: (0, 0)
    return pl.pallas_call(
        _shared_kernel,
        grid=(t // tm,),
        in_specs=[pl.BlockSpec((tm, half), row), pl.BlockSpec(sg.shape, fix), pl.BlockSpec(su.shape, fix),
                  pl.BlockSpec(sd.shape, fix)],
        out_specs=pl.BlockSpec((tm, half), row),
        out_shape=jax.ShapeDtypeStruct((t, half), jnp.uint32),
        compiler_params=_cparams(("arbitrary",)),
        name="moe_shared",
    )(xp, sg, su, sd)


def _combine_kernel(x_ref, gate_ref, rows_ref, sh_ref, g_ref, b_ref, o_ref):
    gate = gate_ref[...]
    ya, yb = _unpack_pairs(sh_ref[...])
    for s in range(TOP_K):
        a, b = _unpack_pairs(rows_ref[s])
        ya = ya + gate[:, s:s + 1] * a
        yb = yb + gate[:, s:s + 1] * b
    ff = jnp.concatenate([ya, yb], axis=1)
    o_ref[...] = _layer_norm(DN_ALPHA * x_ref[...] + ff, g_ref[...], b_ref[...])


def _combine(x, gate_t, rows, shared, g, b, tm=512):
    t, d = x.shape
    row = lambda i: (i, 0)
    fix = lambda i: (0, 0)
    return pl.pallas_call(
        _combine_kernel,
        grid=(t // tm,),
        in_specs=[pl.BlockSpec((tm, d), row), pl.BlockSpec((tm, TOP_K), row),
                  pl.BlockSpec((TOP_K, tm, d // 2), lambda i: (0, i, 0)), pl.BlockSpec((tm, d // 2), row),
                  pl.BlockSpec((1, d), fix), pl.BlockSpec((1, d), fix)],
        out_specs=pl.BlockSpec((tm, d), row),
        out_shape=jax.ShapeDtypeStruct((t, d), F32),
        compiler_params=_cparams(("arbitrary",)),
        name="moe_combine",
    )(x, gate_t, rows, shared, g, b)


def _take_cols(w, idx):
    idx = np.asarray(idx)
    runs, start = [], 0
    for pos in range(1, len(idx) + 1):
        run_ends = pos == len(idx) or (idx[pos] != idx[pos - 1] + 1 if idx[pos - 1] >= 0 else idx[pos] >= 0)
        if run_ends:
            runs.append((start, int(idx[start]), pos - start))
            start = pos

    def body(w_ref, o_ref):
        for dst, src, width in runs:
            if src < 0:
                o_ref[:, dst:dst + width] = jnp.zeros((o_ref.shape[0], width), o_ref.dtype)
            else:
                o_ref[:, dst:dst + width] = w_ref[:, src:src + width].astype(o_ref.dtype)

    rows = w.shape[0]
    tr = min(rows, 256)
    return pl.pallas_call(
        body,
        grid=(rows // tr,),
        in_specs=[pl.BlockSpec((tr, w.shape[1]), lambda i: (i, 0))],
        out_specs=pl.BlockSpec((tr, len(idx)), lambda i: (i, 0)),
        out_shape=jax.ShapeDtypeStruct((rows, len(idx)), BF16),
        compiler_params=_cparams(("arbitrary",)),
        name="weight_cols",
    )(w)


def _pad_lane_row(v, first_lane, width=LANES):
    out = jnp.zeros((1, width), F32)
    return lax.dynamic_update_slice(out, v.reshape(1, -1).astype(F32), (0, first_lane))


def _even_in_cols():
    z = lambda n: -np.ones(n, int)
    kr0 = Q_LORA + KV_LORA
    half = MLA_ROPE // 2
    cols = [np.arange(0, Q_LORA), np.arange(Q_LORA, Q_LORA + KV_LORA),
            z(64), np.arange(kr0, kr0 + MLA_ROPE), z(32),
            z(64), np.arange(kr0 + half, kr0 + MLA_ROPE), np.arange(kr0, kr0 + half), z(32)]
    g0 = kr0 + MLA_ROPE
    nqk = GDN_H * GDN_DK
    cols.append(np.arange(g0, g0 + 3 * nqk))
    zoff = g0 + 3 * nqk + 2 * GDN_H
    cols.append(np.arange(zoff, zoff + GDN_H * GDN_DV))
    cols += [np.arange(g0 + 3 * nqk, g0 + 3 * nqk + 2 * GDN_H), z(LANES - 2 * GDN_H)]
    return np.concatenate(cols)


EV_WIDTHS = (Q_LORA + KV_LORA + 2 * LANES, 3 * GDN_H * GDN_DK, GDN_H * GDN_DV, LANES)


def _mla_q_cols():
    per = MLA_NOPE + MLA_ROPE
    half = MLA_ROPE // 2
    main, sw = [], []
    for h in range(MLA_H):
        b = h * per
        main += [np.arange(b, b + per), -np.ones(LANES - per, int)]
        sw += [-np.ones(MLA_NOPE, int), np.arange(b + MLA_NOPE + half, b + per), np.arange(b + MLA_NOPE, b + MLA_NOPE + half),
               -np.ones(LANES - per, int)]
    return np.concatenate(main + sw)


def _mla_kv_cols():
    per = MLA_NOPE + MLA_V
    kc, vc = [], []
    for h in range(MLA_H):
        b = h * per
        kc += [np.arange(b, b + MLA_NOPE), -np.ones(LANES - MLA_NOPE, int)]
        vv = np.arange(b + MLA_NOPE, b + per)
        pad = -np.ones(LANES - MLA_V, int)
        vc += [vv, pad] if h % 2 == 0 else [pad, vv]
    return np.concatenate(kc + vc)


def _odd_in_cols():
    z = lambda n: -np.ones(n, int)
    o = 0
    cols = []
    mq0, mk0 = 0, ML_H * ML_DK
    for base in (mq0, mk0):
        for h in range(ML_H):
            cols += [np.arange(base + h * ML_DK, base + (h + 1) * ML_DK), z(LANES - ML_DK)]
    mv0 = 2 * ML_H * ML_DK
    cols.append(np.arange(mv0, mv0 + ML_H * ML_DV))
    mi0 = mv0 + ML_H * ML_DV
    mo0 = mi0 + 2 * ML_H
    cols.append(np.arange(mo0, mo0 + ML_H * ML_DV))
    cols += [np.arange(mi0, mi0 + 2 * ML_H), z(LANES - 2 * ML_H)]
    sq0 = mo0 + ML_H * ML_DV
    sk0 = sq0 + SWA_H * SWA_D
    sv0 = sk0 + SWA_KV * SWA_D
    half = SWA_D // 2

    def heads(base, n, swapped, copies):
        out = []
        for h in range(n):
            b = base + h * SWA_D
            one = [np.arange(b + half, b + SWA_D), np.arange(b, b + half)] if swapped else [np.arange(b, b + SWA_D)]
            out += one * copies
        return out

    cols += (heads(sq0, SWA_H, False, 1) + heads(sq0, SWA_H, True, 1)
             + heads(sk0, SWA_KV, False, 2) + heads(sk0, SWA_KV, True, 2))
    for g in range(SWA_KV):
        vv = np.arange(sv0 + g * SWA_D, sv0 + (g + 1) * SWA_D)
        cols += [vv, z(LANES - SWA_D), z(LANES - SWA_D), vv]
    return np.concatenate(cols)


def _even_weights(w_in, w_qb, w_kvb):
    return (_take_cols(w_in, _even_in_cols()), _take_cols(w_qb, _mla_q_cols()), _take_cols(w_kvb, _mla_kv_cols()))


def _even_mixer(x, tabs, weights, q_norm, kv_norm, conv_w, a_log, dt_bias, o_norm, batch, seq):
    ctab, stab = tabs
    w, wq2, wkv2 = weights
    mla_in, act, z, gates = _proj_even(x, w, conv_w, seq)
    q, k, v = _mla_prep(mla_in, ctab, stab, q_norm.reshape(1, -1), kv_norm.reshape(1, -1), wq2, wkv2)
    o_a = _mla_attn(q, k, v, batch, seq)
    o_b = _gdn(act, gates, z, _pad_lane_row(a_log, GDN_H), _pad_lane_row(dt_bias, GDN_H),
               o_norm.reshape(1, -1), batch, seq)
    return o_a, o_b


def _odd_mixer(x, tabs, w, b_i, b_f, ml_norm, sinks, batch, seq):
    ctab, stab = tabs
    mq, mk, mv, mo, mg, sq, sk, sv = _proj_odd(x, w, ctab, stab)
    bias_row = _pad_lane_row(jnp.concatenate([b_i, b_f]), 0)
    o_c = _mlstm(mq, mk, mv, mo, mg, bias_row, ml_norm.reshape(1, -1), batch, seq)
    o_d = _swa(sq, sk, sv, _pad_lane_row(sinks, 0), batch, seq)
    return o_c, o_d


def _moe(x, xp, router_w, router_b, w_gate, w_up, w_down, layer, s_gate, s_up, s_down, ln_g, ln_b):
    t, d = x.shape
    bias_col = jnp.broadcast_to(router_b.reshape(-1, 1).astype(F32), (N_EXPERTS, LANES))
    idx, gate, rank, cnt = _router(x, router_w.T, bias_col)
    counts = cnt[:, 0].astype(jnp.int32)
    block = int(min(max(pl.next_power_of_2(t * TOP_K // N_EXPERTS) // 2, EXPERT_BLOCK_MIN), EXPERT_BLOCK_MAX))
    padded = (counts + block - 1) // block * block
    pad_end = jnp.cumsum(padded)
    pad_start = pad_end - padded
    start_col = jnp.broadcast_to(pad_start.astype(F32).reshape(-1, 1), (N_EXPERTS, LANES))
    dest = _dest_rows(idx, rank, start_col)
    n_blocks = t * TOP_K // block + N_EXPERTS
    rows = n_blocks * block
    block_row = jnp.arange(n_blocks, dtype=jnp.int32) * block
    block_e = jnp.minimum(jnp.sum((pad_end[None, :] <= block_row[:, None]).astype(jnp.int32), axis=1), N_EXPERTS - 1)
    n_used = (pad_end[-1:] // block).astype(jnp.int32)
    live_end = jnp.sum(jnp.where(block_e[:, None] == jnp.arange(N_EXPERTS, dtype=jnp.int32)[None, :],
                                 (pad_start + counts)[None, :], 0), axis=1)
    n_valid = jnp.clip(live_end - block_row, 0, block).astype(jnp.int32)
    xs = _sc_scatter_rows(xp, dest, rows)
    ys = _experts(block_e, n_used, n_valid, xs, w_gate, w_up, w_down, layer, block)
    picked = _sc_gather_rows(ys, dest.reshape(-1)).reshape(TOP_K, t, d // 2)
    shared = _shared_expert(xp, s_gate.astype(BF16), s_up.astype(BF16), s_down.astype(BF16))
    return _combine(x, gate.T, picked, shared, ln_g.reshape(1, -1), ln_b.reshape(1, -1))


def kernel(x, positions, ev_w_in, mla_q_norm, mla_w_qb, mla_kv_norm, mla_w_kvb, gdn_conv, gdn_a_log, gdn_dt_bias, gdn_norm, ev_w_out, od_w_in, mlstm_b_i, mlstm_b_f, mlstm_norm, swa_sinks, od_w_out, ln1_g, ln1_b, router_w, router_b, moe_w_gate, moe_w_up, moe_w_down, shared_w_gate, shared_w_up, shared_w_down, ln2_g, ln2_b):
    batch, seq, d = x.shape
    streams = STREAMS if batch % STREAMS == 0 else 1
    sb = batch // streams
    ts = sb * seq
    hs, tabs_m, tabs_s = [], [], []
    for s in range(streams):
        pos = positions[s * sb:(s + 1) * sb].reshape(ts, 1).astype(F32)
        tabs_m.append(_rope_tables(pos, _rope_rows(MLA_ROPE, MLA_NOPE, MLA_NOPE)))
        tabs_s.append(_rope_tables(pos, _rope_rows(SWA_D, 0, 0, heads=LANES // SWA_D)))
        hs.append(x[s * sb:(s + 1) * sb].reshape(ts, d))
    for layer in range(DEPTH):
        j = layer // 2
        if layer % 2 == 0:
            weights = _even_weights(ev_w_in[j], mla_w_qb[j], mla_w_kvb[j])
            w_out = ev_w_out[j].astype(BF16)
        else:
            weights = _take_cols(od_w_in[j], _odd_in_cols())
            w_out = od_w_out[j].astype(BF16)
        for s in range(streams):
            h = hs[s]
            if layer % 2 == 0:
                a1, a2 = _even_mixer(h, tabs_m[s], weights, mla_q_norm[j], mla_kv_norm[j], gdn_conv[j], gdn_a_log[j],
                                     gdn_dt_bias[j], gdn_norm[j], sb, seq)
            else:
                a1, a2 = _odd_mixer(h, tabs_s[s], weights, mlstm_b_i[j], mlstm_b_f[j], mlstm_norm[j], swa_sinks[j], sb, seq)
            h, hp = _outproj_ln(h, a1, a2, w_out, ln1_g[layer].reshape(1, -1), ln1_b[layer].reshape(1, -1))
            hs[s] = _moe(h, hp, router_w[layer], router_b[layer], moe_w_gate, moe_w_up, moe_w_down, layer,
                         shared_w_gate[layer], shared_w_up[layer], shared_w_down[layer], ln2_g[layer], ln2_b[layer])
    return jnp.concatenate([h.reshape(sb, seq, d) for h in hs], axis=0)
```

```python
import functools
import math

import numpy as np
import jax
import jax.numpy as jnp
from jax import lax
from jax.experimental import pallas as pl
from jax.experimental.pallas import tpu as pltpu
from jax.experimental.pallas import tpu_sc as plsc

F32 = jnp.float32
BF16 = jnp.bfloat16
HI = lax.Precision.HIGHEST

D_MODEL = 1024
DEPTH = 4
ROPE_THETA = 10000.0
EPS = 1e-6
LN_EPS = 1e-5
MLA_H, MLA_NOPE, MLA_ROPE, MLA_V = 8, 64, 32, 64
Q_LORA, KV_LORA = 256, 128
GDN_H, GDN_DK, GDN_DV, CONV_W, GDN_CHUNK = 4, 128, 128, 4, 64
ML_H, ML_DK, ML_DV, ML_CHUNK = 4, 64, 128, 64
SWA_H, SWA_KV, SWA_D, WINDOW = 8, 2, 64, 128
N_EXPERTS, N_GROUPS, TOPK_GROUPS, TOP_K = 64, 8, 4, 8
D_EXPERT, D_SHARED = 256, 256
ROUTED_SCALE = 2.5
DN_ALPHA = (2 * DEPTH) ** 0.25

LANES = 128
V7X_VMEM_BYTES = 64 * 1024 * 1024
VMEM_LIMIT = 48 * 1024 * 1024

EXPERT_BLOCK_MIN = 256
EXPERT_BLOCK_MAX = 1024
STREAMS = 1
EXPERT_SUBBLOCKS = 4
SWA_SEQS_PER_STEP = 8
MLSTM_SEQS_PER_STEP = 2
GDN_SEQS_PER_STEP = 8
SC_CHUNK = 64


def _cparams(sem, vmem=VMEM_LIMIT):
    return pltpu.CompilerParams(dimension_semantics=sem, vmem_limit_bytes=vmem)


def _dot(a, b, precision=None):
    return jnp.dot(a, b, preferred_element_type=F32, precision=precision)


def _dot_nt(a, b, precision=None):
    return lax.dot_general(a, b, (((1,), (1,)), ((), ())), preferred_element_type=F32, precision=precision)


def _dot_tn(a, b, precision=None):
    return lax.dot_general(a, b, (((0,), (0,)), ((), ())), preferred_element_type=F32, precision=precision)


def _split2(a):
    hi = a.astype(BF16)
    lo = (a - hi.astype(F32)).astype(BF16)
    return hi, lo


def _split3(a):
    p1 = a.astype(BF16)
    r = a - p1.astype(F32)
    p2 = r.astype(BF16)
    p3 = (r - p2.astype(F32)).astype(BF16)
    return p1, p2, p3


def _dot3(a, b, dot=_dot):
    ah, al = _split2(a)
    bh, bl = _split2(b)
    return dot(ah, bh) + (dot(ah, bl) + dot(al, bh))


def _dot_sel(sel, b, dot=_dot):
    sel = sel.astype(BF16)
    p1, p2, p3 = _split3(b)
    return dot(sel, p1) + (dot(sel, p2) + dot(sel, p3))


def _sigmoid(x):
    return 1.0 / (1.0 + jnp.exp(-x))


def _softplus(x):
    return jnp.maximum(x, 0.0) + jnp.log(1.0 + jnp.exp(-jnp.abs(x)))


def _silu(x):
    return x * _sigmoid(x)


def _lane_bcast(x, c):
    return jnp.broadcast_to(x[:, c:c + 1], x.shape)


def _iota2(shape, dim):
    return lax.broadcasted_iota(jnp.int32, shape, dim)


def _rope_kernel(pos_ref, rows_ref, c_ref, s_ref):
    ang = pos_ref[...] * rows_ref[0:1, :]
    c_ref[...] = rows_ref[1:2, :] * jnp.cos(ang) + rows_ref[2:3, :]
    s_ref[...] = rows_ref[3:4, :] * jnp.sin(ang)


def _rope_tables(pos, rows, tm=512):
    t = pos.shape[0]
    return pl.pallas_call(
        _rope_kernel,
        grid=(t // tm,),
        in_specs=[pl.BlockSpec((tm, 1), lambda i: (i, 0)), pl.BlockSpec((8, LANES), lambda i: (0, 0))],
        out_specs=[pl.BlockSpec((tm, LANES), lambda i: (i, 0))] * 2,
        out_shape=[jax.ShapeDtypeStruct((t, LANES), F32)] * 2,
        compiler_params=_cparams(("arbitrary",)),
        name="rope_tables",
    )(pos, rows)


def _rope_rows(dim, first_lane, pad_one_lanes, heads=1):
    half = dim // 2
    inv = ROPE_THETA ** (-(np.arange(0, dim, 2, dtype=np.float32) / dim))
    rows = np.zeros((8, LANES), np.float32)
    for h in range(heads):
        lo = slice(first_lane + h * dim, first_lane + h * dim + half)
        hi = slice(first_lane + h * dim + half, first_lane + (h + 1) * dim)
        rows[0, lo] = inv
        rows[0, hi] = inv
        rows[1, lo] = 1.0
        rows[1, hi] = 1.0
        rows[3, lo] = -1.0
        rows[3, hi] = 1.0
    rows[2, :pad_one_lanes] = 1.0
    return jnp.asarray(rows)


def _proj_kernel(x_ref, w_ref, *out_refs, offsets):
    xb = x_ref[...].astype(BF16)
    for o_ref, (a, b) in zip(out_refs, offsets):
        o_ref[...] = _dot(xb, w_ref[:, a:b]).astype(o_ref.dtype)


def _proj(x, w, widths, dtypes, tm=512):
    t, k = x.shape
    offs = np.concatenate([[0], np.cumsum(widths)]).tolist()
    offsets = tuple((offs[i], offs[i + 1]) for i in range(len(widths)))
    return pl.pallas_call(
        functools.partial(_proj_kernel, offsets=offsets),
        grid=(t // tm,),
        in_specs=[pl.BlockSpec((tm, k), lambda i: (i, 0)), pl.BlockSpec(w.shape, lambda i: (0, 0))],
        out_specs=[pl.BlockSpec((tm, n), lambda i: (i, 0)) for n in widths],
        out_shape=[jax.ShapeDtypeStruct((t, n), dt) for n, dt in zip(widths, dtypes)],
        compiler_params=_cparams(("arbitrary",)),
        name="in_proj",
    )(x, w)


def _proj_even_kernel(x_ref, w_ref, cw_ref, mla_ref, act_ref, z_ref, g_ref, ext_ref, *, tiles_per_seq):
    tm = x_ref.shape[0]
    o = np.concatenate([[0], np.cumsum(EV_WIDTHS)]).tolist()
    @pl.when(pl.program_id(0) % tiles_per_seq == 0)
    def _():
        ext_ref[0:8, :] = jnp.zeros((8, ext_ref.shape[1]), F32)

    xb = x_ref[...].astype(BF16)
    nchunk = 3
    cw = EV_WIDTHS[1] // nchunk

    def project(ci):
        ext_ref[8:8 + tm, ci * cw:(ci + 1) * cw] = _dot(xb, w_ref[:, o[1] + ci * cw:o[1] + (ci + 1) * cw])

    project(0)
    for ci in range(nchunk):
        if ci + 1 < nchunk:
            project(ci + 1)
        else:
            mla_ref[...] = _dot(xb, w_ref[:, o[0]:o[1]])
            z_ref[...] = _dot(xb, w_ref[:, o[2]:o[3]]).astype(z_ref.dtype)
            g_ref[...] = _dot(xb, w_ref[:, o[3]:o[4]])
        cols = slice(ci * cw, (ci + 1) * cw)
        conv = cw_ref[0:1, cols] * ext_ref[5:5 + tm, cols]
        for j in range(1, CONV_W):
            conv = conv + cw_ref[j:j + 1, cols] * ext_ref[5 + j:5 + j + tm, cols]
        act_ref[:, cols] = _silu(conv).astype(act_ref.dtype)
    ext_ref[0:8, :] = ext_ref[tm:tm + 8, :]


def _proj_even(x, w, conv_w, seq, tm=512):
    t, k = x.shape
    tm = min(tm, seq)
    row = lambda i: (i, 0)
    fix = lambda i: (0, 0)
    return pl.pallas_call(
        functools.partial(_proj_even_kernel, tiles_per_seq=seq // tm),
        grid=(t // tm,),
        in_specs=[pl.BlockSpec((tm, k), row), pl.BlockSpec(w.shape, fix), pl.BlockSpec(conv_w.shape, fix)],
        out_specs=[pl.BlockSpec((tm, n), row) for n in EV_WIDTHS],
        out_shape=[jax.ShapeDtypeStruct((t, n), F32) for n in EV_WIDTHS],
        scratch_shapes=[pltpu.VMEM((tm + 8, EV_WIDTHS[1]), F32)],
        compiler_params=_cparams(("arbitrary",)),
        name="in_proj",
    )(x, w, conv_w)


OD_SEG = dict(mq=(0, 512), mk=(512, 1024), mv=(1024, 1536), mo=(1536, 2048), gates=(2048, 2176),
              sq=(2176, 2688), sqsw=(2688, 3200), sk=(3200, 3456), sksw=(3456, 3712), sv=(3712, 4224))
OD_COLS = 4224


def _proj_odd_kernel(x_ref, w_ref, c_ref, s_ref, mq_ref, mk_ref, mv_ref, mo_ref, mg_ref, sq_ref, sk_ref, sv_ref):
    xb = x_ref[...].astype(BF16)

    def seg(name):
        a, b = OD_SEG[name]
        return _dot(xb, w_ref[:, a:b])

    mq_ref[...] = seg("mq").astype(mq_ref.dtype)
    mk_ref[...] = seg("mk").astype(mk_ref.dtype)
    mv_ref[...] = seg("mv").astype(mv_ref.dtype)
    mo_ref[...] = seg("mo").astype(mo_ref.dtype)
    mg_ref[...] = seg("gates")
    c = c_ref[...]
    s = s_ref[...]
    c8 = jnp.concatenate([c] * (SWA_H // 2), axis=1)
    s8 = jnp.concatenate([s] * (SWA_H // 2), axis=1)
    sq_ref[...] = (seg("sq") * c8 + seg("sqsw") * s8).astype(sq_ref.dtype)
    c2 = jnp.concatenate([c] * SWA_KV, axis=1)
    s2 = jnp.concatenate([s] * SWA_KV, axis=1)
    sk_ref[...] = (seg("sk") * c2 + seg("sksw") * s2).astype(sk_ref.dtype)
    sv_ref[...] = seg("sv").astype(sv_ref.dtype)


def _proj_odd(x, w, ctab, stab, tm=512):
    t, k = x.shape
    widths = (512, 512, 512, 512, 128, SWA_H * SWA_D, SWA_KV * LANES, 2 * SWA_KV * LANES)
    dtypes = (F32, F32, F32, F32, F32, BF16, BF16, BF16)
    return pl.pallas_call(
        _proj_odd_kernel,
        grid=(t // tm,),
        in_specs=[pl.BlockSpec((tm, k), lambda i: (i, 0)), pl.BlockSpec(w.shape, lambda i: (0, 0)),
                  pl.BlockSpec((tm, LANES), lambda i: (i, 0)), pl.BlockSpec((tm, LANES), lambda i: (i, 0))],
        out_specs=[pl.BlockSpec((tm, n), lambda i: (i, 0)) for n in widths],
        out_shape=[jax.ShapeDtypeStruct((t, n), dt) for n, dt in zip(widths, dtypes)],
        compiler_params=_cparams(("arbitrary",)),
        name="in_proj_odd",
    )(x, w, ctab, stab)


def _rms(x, g):
    return x * lax.rsqrt(jnp.mean(x * x, axis=-1, keepdims=True) + EPS) * g


def _mla_prep_kernel(in_ref, c_ref, s_ref, qn_ref, kvn_ref, wq_ref, wkv_ref, q_ref, k_ref, v_ref):
    hw = MLA_H * LANES
    c = c_ref[...]
    s = s_ref[...]
    c8 = jnp.concatenate([c] * MLA_H, axis=1)
    s8 = jnp.concatenate([s] * MLA_H, axis=1)
    cqn = _rms(in_ref[:, 0:Q_LORA], qn_ref[...]).astype(BF16)
    qq = _dot(cqn, wq_ref[...])
    scale = (MLA_NOPE + MLA_ROPE) ** -0.5
    q_ref[...] = ((qq[:, :hw] * c8 + qq[:, hw:] * s8) * scale).astype(q_ref.dtype)
    ckvn = _rms(in_ref[:, Q_LORA:Q_LORA + KV_LORA], kvn_ref[...]).astype(BF16)
    kv = _dot(ckvn, wkv_ref[...])
    o = Q_LORA + KV_LORA
    krr = in_ref[:, o:o + LANES] * c + in_ref[:, o + LANES:o + 2 * LANES] * s
    k_ref[...] = (kv[:, :hw] + jnp.concatenate([krr] * MLA_H, axis=1)).astype(k_ref.dtype)
    v_ref[...] = kv[:, hw:].astype(v_ref.dtype)


def _mla_prep(mla_in, ctab, stab, qn, kvn, wq2, wkv2, tm=512):
    t = mla_in.shape[0]
    hw = MLA_H * LANES
    row = lambda i: (i, 0)
    fix = lambda i: (0, 0)
    return pl.pallas_call(
        _mla_prep_kernel,
        grid=(t // tm,),
        in_specs=[pl.BlockSpec((tm, mla_in.shape[1]), row), pl.BlockSpec((tm, LANES), row), pl.BlockSpec((tm, LANES), row),
                  pl.BlockSpec(qn.shape, fix), pl.BlockSpec(kvn.shape, fix),
                  pl.BlockSpec(wq2.shape, fix), pl.BlockSpec(wkv2.shape, fix)],
        out_specs=[pl.BlockSpec((tm, hw), row)] * 3,
        out_shape=[jax.ShapeDtypeStruct((t, hw), BF16)] * 3,
        compiler_params=_cparams(("arbitrary",)),
        name="mla_prep",
    )(mla_in, ctab, stab, qn, kvn, wq2, wkv2)


def _mla_attn_kernel(q_ref, k_ref, v_ref, o_ref, *, tq):
    i = pl.program_id(2)
    neg = -1e30
    lane = _iota2((tq, LANES), 1)
    ones_lane = (MLA_V, 0)

    def chunk(j, carry, masked):
        start = pl.multiple_of(j * tq, tq)
        out = []
        for hh in range(2):
            m, acc = carry[hh]
            q = q_ref[:, hh * LANES:(hh + 1) * LANES]
            kc = k_ref[pl.ds(start, tq), hh * LANES:(hh + 1) * LANES]
            vc = v_ref[pl.ds(start, tq), hh * LANES:(hh + 1) * LANES]
            vc = jnp.where(lane == ones_lane[hh], jnp.ones_like(vc), vc)
            s = _dot_nt(q, kc)
            if masked:
                s = jnp.where(_iota2(s.shape, 0) >= _iota2(s.shape, 1), s, neg)
            m_new = jnp.maximum(m, jnp.max(s, axis=-1, keepdims=True))
            alpha = jnp.exp(m - m_new)
            p = jnp.exp(s - m_new)
            acc = alpha * acc + _dot(p.astype(BF16), vc)
            out.append((m_new, acc))
        return tuple(out)

    one = (jnp.full((tq, 1), neg, F32), jnp.zeros((tq, LANES), F32))
    carry = lax.fori_loop(0, i, lambda j, c: chunk(j, c, False), (one, one))
    (_, acc0), (_, acc1) = chunk(i, carry, True)
    o0 = acc0 / _lane_bcast(acc0, ones_lane[0])
    o1 = acc1 / _lane_bcast(acc1, ones_lane[1])
    o_ref[...] = jnp.where(lane < MLA_V, o0, o1).astype(o_ref.dtype)


def _mla_attn(q, k, v, batch, seq, tq=512):
    tq = min(tq, seq)
    nq = seq // tq
    pairs = MLA_H // 2
    return pl.pallas_call(
        functools.partial(_mla_attn_kernel, tq=tq),
        grid=(batch, pairs, nq),
        in_specs=[pl.BlockSpec((tq, 2 * LANES), lambda b, p, i: (b * nq + i, p)),
                  pl.BlockSpec((seq, 2 * LANES), lambda b, p, i: (b, p)),
                  pl.BlockSpec((seq, 2 * LANES), lambda b, p, i: (b, p))],
        out_specs=pl.BlockSpec((tq, LANES), lambda b, p, i: (b * nq + i, p)),
        out_shape=jax.ShapeDtypeStruct((batch * seq, pairs * LANES), BF16),
        compiler_params=_cparams(("arbitrary", "arbitrary", "arbitrary")),
        name="mla_attn",
    )(q, k, v)


def _unit_lower_inverse_many(ns):
    c = ns[0].shape[0]
    eye = (_iota2((c, c), 0) == _iota2((c, c), 1)).astype(F32)
    xs = [-n for n in ns]
    ps = [eye + x for x in xs]
    xb = [x.astype(BF16) for x in xs]
    for _ in range(int(math.log2(c)) - 1):
        xs = [_dot(b, b) for b in xb]
        xb = [x.astype(BF16) for x in xs]
        ps = [p + _dot(p.astype(BF16), b) for p, b in zip(ps, xb)]
    return ps


def _gdn_kernel(act_ref, g_ref, z_ref, al_ref, dt_ref, on_ref, o_ref, st_ref):
    c = GDN_CHUNK
    hd = GDN_DK
    nqk = GDN_H * GDN_DK

    @pl.when(pl.program_id(1) == 0)
    def _():
        st_ref[...] = jnp.zeros(st_ref.shape, F32)

    tri = (_iota2((c, c), 0) >= _iota2((c, c), 1)).astype(F32)
    row_ge = _iota2((c, c), 0) >= _iota2((c, c), 1)
    row_gt = _iota2((c, c), 0) > _iota2((c, c), 1)
    lane = _iota2((c, LANES), 1)

    seqs = []
    for bb in range(act_ref.shape[0]):
        gates = g_ref[bb]
        g_all = -jnp.exp(al_ref[...]) * _softplus(gates + dt_ref[...])
        gc_all = _dot_sel(tri, g_all)
        seqs.append(dict(beta_all=_sigmoid(gates), gc_all=gc_all, gc_parts=_split3(gc_all)))
    units = []
    for bb, sq in enumerate(seqs):
        for h in range(GDN_H):
            q = act_ref[bb, :, h * hd:(h + 1) * hd].astype(F32)
            k = act_ref[bb, :, nqk + h * hd:nqk + (h + 1) * hd].astype(F32)
            v = act_ref[bb, :, 2 * nqk + h * GDN_DV:2 * nqk + (h + 1) * GDN_DV].astype(F32)
            q = q * lax.rsqrt(jnp.sum(q * q, axis=-1, keepdims=True) + EPS) * (GDN_DK ** -0.5)
            k = k * lax.rsqrt(jnp.sum(k * k, axis=-1, keepdims=True) + EPS)
            beta = _lane_bcast(sq["beta_all"], h)
            gcol = _lane_bcast(sq["gc_all"], GDN_H + h)
            units.append(dict(bb=bb, h=h, q=q, k=k, v=v, beta=beta, gcol=gcol, kb=k * beta, parts=sq["gc_parts"]))
    for u in units:
        pick = (lane == GDN_H + u["h"]).astype(BF16)
        p0, p1, p2 = u["parts"]
        u["grow"] = _dot_nt(pick, p0) + (_dot_nt(pick, p1) + _dot_nt(pick, p2))
        u["kk"] = _dot3(u["kb"], u["k"], _dot_nt)
        u["qk"] = _dot_nt(u["q"].astype(BF16), u["k"].astype(BF16))
    for u in units:
        gcol = u["gcol"]
        decay = jnp.exp(jnp.where(row_ge, gcol[:, :c] - u["grow"], -jnp.inf))
        eg = jnp.exp(gcol)
        glast = gcol[c - 1:c, :]
        u["lower"] = jnp.where(row_gt, u["kk"] * decay, 0.0)
        u["rhs"] = jnp.concatenate([u["v"] * u["beta"], u["kb"] * eg], axis=1)
        u["attn"] = u["qk"] * decay
        u["qg"] = (u["q"] * eg).astype(BF16)
        u["kg"] = (u["k"] * jnp.exp(glast - gcol)).astype(BF16)
        u["gl"] = jnp.exp(glast)

    tinvs = _unit_lower_inverse_many([u["lower"] for u in units])
    uws = []
    for u, tinv in zip(units, tinvs):
        uws.append(_dot(tinv.astype(BF16), u["rhs"].astype(BF16)))
    states = [st_ref[u["bb"], u["h"]] for u in units]
    sbs = [s.astype(BF16) for s in states]
    vnews = [(uw[:, :GDN_DV] - _dot(uw[:, GDN_DV:].astype(BF16), sb)).astype(BF16) for uw, sb in zip(uws, sbs)]
    for u, state, sb, vnb in zip(units, states, sbs, vnews):
        bb, h = u["bb"], u["h"]
        o = _dot(u["qg"], sb) + _dot(u["attn"].astype(BF16), vnb)
        st_ref[bb, h] = state * u["gl"] + _dot_tn(u["kg"], vnb)
        o = _rms(o, on_ref[...]) * _silu(z_ref[bb, :, h * GDN_DV:(h + 1) * GDN_DV].astype(F32))
        o_ref[bb, :, h * GDN_DV:(h + 1) * GDN_DV] = o.astype(o_ref.dtype)


def _gdn(act, gates, z, a_row, dt_row, o_norm, batch, seq):
    c = GDN_CHUNK
    nc = seq // c
    w3 = act.shape[1]
    wo = GDN_H * GDN_DV
    nb = min(GDN_SEQS_PER_STEP, batch)
    row = lambda b, i: (b, i, 0)
    fix = lambda b, i: (0, 0)
    out = pl.pallas_call(
        _gdn_kernel,
        grid=(batch // nb, nc),
        in_specs=[pl.BlockSpec((nb, c, w3), row), pl.BlockSpec((nb, c, LANES), row), pl.BlockSpec((nb, c, wo), row),
                  pl.BlockSpec((1, LANES), fix), pl.BlockSpec((1, LANES), fix), pl.BlockSpec((1, GDN_DV), fix)],
        out_specs=pl.BlockSpec((nb, c, wo), row),
        out_shape=jax.ShapeDtypeStruct((batch, seq, wo), BF16),
        scratch_shapes=[pltpu.VMEM((nb, GDN_H, GDN_DK, GDN_DV), F32)],
        compiler_params=_cparams(("arbitrary", "arbitrary")),
        name="gdn",
    )(act.reshape(batch, seq, w3), gates.reshape(batch, seq, LANES), z.reshape(batch, seq, wo), a_row, dt_row, o_norm)
    return out.reshape(batch * seq, wo)


def _mlstm_kernel(q_ref, k_ref, v_ref, og_ref, g_ref, bias_ref, nrm_ref, o_ref, c_ref, n_ref, m_ref):
    @pl.when(pl.program_id(1) == 0)
    def _():
        c_ref[...] = jnp.zeros(c_ref.shape, F32)
        n_ref[...] = jnp.zeros(n_ref.shape, F32)
        m_ref[...] = jnp.zeros(m_ref.shape, F32)

    c = ML_CHUNK
    tri = (_iota2((c, c), 0) >= _iota2((c, c), 1)).astype(F32)
    row_ge = _iota2((c, c), 0) >= _iota2((c, c), 1)
    ones = jnp.ones((c, LANES), F32)
    lane = _iota2((c, LANES), 1)

    units = []
    for bb in range(q_ref.shape[0]):
        pre = g_ref[bb] + bias_ref[...]
        logf = jnp.minimum(pre, 0.0) - jnp.log(1.0 + jnp.exp(-jnp.abs(pre)))
        bcum_all = _dot_sel(tri, logf)
        for h in range(ML_H):
            q = q_ref[bb, :, h * LANES:(h + 1) * LANES].astype(F32)
            k = k_ref[bb, :, h * LANES:(h + 1) * LANES].astype(F32) * (ML_DK ** -0.5)
            units.append(dict(bb=bb, h=h, q=q, k=k, qb=q.astype(BF16), vb=v_ref[bb, :, h * ML_DV:(h + 1) * ML_DV].astype(BF16),
                              bcol=_lane_bcast(bcum_all, ML_H + h),
                              icol=_lane_bcast(pre, h),
                              col=jnp.where(lane == h, pre, 0.0) - jnp.where(lane == ML_H + h, bcum_all, 0.0),
                              m_st=m_ref[bb, h], cst=c_ref[bb, h], nst=n_ref[bb, h]))
    for u in units:
        u["row"] = _dot_sel(ones, u["col"], _dot_nt)
        u["qk"] = _dot_nt(u["qb"], u["k"].astype(BF16))
        u["qc"] = _dot(u["qb"], u["cst"].astype(BF16))
    for u in units:
        u["d"] = jnp.where(row_ge, u["bcol"][:, :c] + u["row"], -jnp.inf)
        u["inter"] = u["bcol"] + u["m_st"]
        u["m_t"] = jnp.maximum(u["inter"], jnp.max(u["d"], axis=-1, keepdims=True))
        u["b_end"] = u["bcol"][c - 1:c, :]
        u["a"] = u["b_end"] - u["bcol"] + u["icol"]
        u["m_new"] = jnp.maximum(u["b_end"] + u["m_st"], jnp.max(u["a"], axis=0, keepdims=True))
    for u in units:
        u["w_inter"] = jnp.exp(u["inter"] - u["m_t"])
        u["p"] = jnp.exp(u["d"] - u["m_t"][:, :c]) * u["qk"]
        u["keep"] = jnp.exp(u["b_end"] + u["m_st"] - u["m_new"])
        u["ks"] = u["k"] * jnp.exp(u["a"] - u["m_new"])
    for u in units:
        u["pv"] = _dot(u["p"].astype(BF16), u["vb"])
        u["kv"] = _dot_tn(u["ks"].astype(BF16), u["vb"])
    for u in units:
        u["den"] = (u["w_inter"] * jnp.sum(u["q"] * u["nst"], axis=-1, keepdims=True)
                    + jnp.sum(u["p"], axis=-1, keepdims=True))
    for u in units:
        bb, h = u["bb"], u["h"]
        num = u["w_inter"] * u["qc"] + u["pv"]
        hc = num / jnp.maximum(jnp.abs(u["den"]), jnp.exp(-u["m_t"]))
        c_ref[bb, h] = u["cst"] * u["keep"] + u["kv"]
        n_ref[bb, h] = u["nst"] * u["keep"] + jnp.sum(u["ks"], axis=0, keepdims=True)
        m_ref[bb, h] = u["m_new"]
        hn = (_rms(hc, nrm_ref[:, h * ML_DV:(h + 1) * ML_DV])
              * _sigmoid(og_ref[bb, :, h * ML_DV:(h + 1) * ML_DV].astype(F32)))
        o_ref[bb, :, h * ML_DV:(h + 1) * ML_DV] = hn.astype(o_ref.dtype)


def _mlstm(mq, mk, mv, mo, gates, bias_row, norm_row, batch, seq):
    c = ML_CHUNK
    nc = seq // c
    nb = min(MLSTM_SEQS_PER_STEP, batch)
    row = lambda b, i: (b, i, 0)
    fix = lambda b, i: (0, 0)
    wide = ML_H * LANES
    r3 = lambda a: a.reshape(batch, seq, a.shape[-1])
    out = pl.pallas_call(
        _mlstm_kernel,
        grid=(batch // nb, nc),
        in_specs=[pl.BlockSpec((nb, c, wide), row), pl.BlockSpec((nb, c, wide), row), pl.BlockSpec((nb, c, wide), row),
                  pl.BlockSpec((nb, c, wide), row), pl.BlockSpec((nb, c, LANES), row),
                  pl.BlockSpec((1, LANES), fix), pl.BlockSpec((1, wide), fix)],
        out_specs=pl.BlockSpec((nb, c, wide), row),
        out_shape=jax.ShapeDtypeStruct((batch, seq, wide), BF16),
        scratch_shapes=[pltpu.VMEM((nb, ML_H, LANES, ML_DV), F32), pltpu.VMEM((nb, ML_H, 1, LANES), F32),
                        pltpu.VMEM((nb, ML_H, 1, LANES), F32)],
        compiler_params=_cparams(("arbitrary", "arbitrary")),
        name="mlstm",
    )(r3(mq), r3(mk), r3(mv), r3(mo), r3(gates), bias_row, norm_row)
    return out.reshape(batch * seq, wide)


def _swa_kernel(q_ref, kc_ref, kp_ref, vc_ref, vp_ref, sink_ref, o_ref):
    w = WINDOW
    n = pl.program_id(1)
    scale = SWA_D ** -0.5
    qi = _iota2((w, w), 0)
    kj = _iota2((w, w), 1)
    mask_c = kj <= qi
    mask_p = jnp.logical_and(kj > qi, n > 0)
    grp = SWA_H // SWA_KV
    neg = -1e30
    units = [(bb, h) for bb in range(q_ref.shape[0]) for h in range(SWA_H)]
    scores = []
    half_of_lane = _iota2((w, LANES), 1) // SWA_D
    for bb, h in units:
        g = h // grp
        pair = q_ref[bb, :, (h // 2) * LANES:(h // 2 + 1) * LANES]
        q = jnp.where(half_of_lane == h % 2, pair, jnp.zeros_like(pair))
        scores.append((_dot_nt(q, kc_ref[bb, :, g * LANES:(g + 1) * LANES]),
                       _dot_nt(q, kp_ref[bb, :, g * LANES:(g + 1) * LANES])))
    masked, tops, exps, dens, probs = [], [], [], [], {}
    for sc, sp in scores:
        masked.append((jnp.where(mask_c, sc * scale, neg), jnp.where(mask_p, sp * scale, neg)))
    for (bb, h), (s_c, s_p) in zip(units, masked):
        tops.append(jnp.maximum(jnp.max(jnp.maximum(s_c, s_p), axis=-1, keepdims=True), sink_ref[:, h:h + 1]))
    for (s_c, s_p), m in zip(masked, tops):
        exps.append((jnp.where(mask_c, jnp.exp(s_c - m), 0.0), jnp.where(mask_p, jnp.exp(s_p - m), 0.0)))
    ones_b = jnp.ones((w, LANES), BF16)
    for (bb, h), (p_c, p_p), m in zip(units, exps, tops):
        p_c, p_p = p_c.astype(BF16), p_p.astype(BF16)
        probs[bb, h] = (p_c, p_p)
        dens.append(_dot(p_c, ones_b) + _dot(p_p, ones_b) + jnp.exp(sink_ref[:, h:h + 1] - m))
    inv = {u: 1.0 / den for u, den in zip(units, dens)}
    for bb in range(q_ref.shape[0]):
        for pair in range(SWA_H // 2):
            acc = None
            for sub in range(2):
                h = 2 * pair + sub
                vcol = (2 * (h // grp) + sub) * LANES
                p_c, p_p = probs[bb, h]
                part = (_dot(p_c, vc_ref[bb, :, vcol:vcol + LANES]) + _dot(p_p, vp_ref[bb, :, vcol:vcol + LANES])) * inv[bb, h]
                acc = part if acc is None else acc + part
            o_ref[bb, :, pair * LANES:(pair + 1) * LANES] = acc.astype(o_ref.dtype)


def _swa(sq, sk, sv, sinks_row, batch, seq):
    w = WINDOW
    nb = seq // w
    ns = min(SWA_SEQS_PER_STEP, batch)
    wo = SWA_H * SWA_D
    cur = lambda b, n: (b, n, 0)
    prev = lambda b, n: (b, jnp.maximum(n - 1, 0), 0)
    r3 = lambda a: a.reshape(batch, seq, a.shape[-1])
    q3, k3, v3 = r3(sq), r3(sk), r3(sv)
    out = pl.pallas_call(
        _swa_kernel,
        grid=(batch // ns, nb),
        in_specs=[pl.BlockSpec((ns, w, sq.shape[1]), cur),
                  pl.BlockSpec((ns, w, sk.shape[1]), cur), pl.BlockSpec((ns, w, sk.shape[1]), prev),
                  pl.BlockSpec((ns, w, sv.shape[1]), cur), pl.BlockSpec((ns, w, sv.shape[1]), prev),
                  pl.BlockSpec((1, LANES), lambda b, n: (0, 0))],
        out_specs=pl.BlockSpec((ns, w, wo), cur),
        out_shape=jax.ShapeDtypeStruct((batch, seq, wo), BF16),
        compiler_params=_cparams(("arbitrary", "arbitrary")),
        name="swa",
    )(q3, k3, k3, v3, v3, sinks_row)
    return out.reshape(batch * seq, wo)


def _layer_norm(h, g, b):
    mu = jnp.mean(h, axis=-1, keepdims=True)
    d = h - mu
    var = jnp.mean(d * d, axis=-1, keepdims=True)
    return d * lax.rsqrt(var + LN_EPS) * g + b


def _outproj_kernel(x_ref, a1_ref, a2_ref, w_ref, g_ref, b_ref, o_ref, op_ref):
    k1 = a1_ref.shape[1]
    y = _dot(a1_ref[...].astype(BF16), w_ref[0:k1, :]) + _dot(a2_ref[...].astype(BF16), w_ref[k1:, :])
    h = _layer_norm(DN_ALPHA * x_ref[...] + y, g_ref[...], b_ref[...])
    o_ref[...] = h
    op_ref[...] = _pack_pairs(h)


def _outproj_ln(x, a1, a2, w, g, b, tm=512):
    t, d = x.shape
    row = lambda i: (i, 0)
    fix = lambda i: (0, 0)
    return pl.pallas_call(
        _outproj_kernel,
        grid=(t // tm,),
        in_specs=[pl.BlockSpec((tm, d), row), pl.BlockSpec((tm, a1.shape[1]), row), pl.BlockSpec((tm, a2.shape[1]), row),
                  pl.BlockSpec(w.shape, fix), pl.BlockSpec((1, d), fix), pl.BlockSpec((1, d), fix)],
        out_specs=[pl.BlockSpec((tm, d), row), pl.BlockSpec((tm, d // 2), row)],
        out_shape=[jax.ShapeDtypeStruct((t, d), F32), jax.ShapeDtypeStruct((t, d // 2), jnp.uint32)],
        compiler_params=_cparams(("arbitrary",)),
        name="outproj_ln",
    )(x, a1, a2, w, g, b)


def _first_index(x, m, iota_f, sentinel):
    return jnp.min(jnp.where(x == m, iota_f, sentinel), axis=0, keepdims=True)


def _router_kernel(x_ref, wt_ref, bias_ref, idx_ref, gate_ref, rank_ref, cnt_ref, carry_ref):
    tm = x_ref.shape[0]
    e = N_EXPERTS
    gs = e // N_GROUPS
    ninf = -jnp.inf

    @pl.when(pl.program_id(0) == 0)
    def _():
        carry_ref[...] = jnp.zeros(carry_ref.shape, F32)

    logits = _dot3(wt_ref[...], x_ref[...], _dot_nt)
    scores = _sigmoid(logits)
    sel = scores + bias_ref[:, 0:1]

    sub_f = _iota2((gs, tm), 0).astype(F32)
    gscore = []
    for g in range(N_GROUPS):
        blk = sel[g * gs:(g + 1) * gs, :]
        m1 = jnp.max(blk, axis=0, keepdims=True)
        i1 = _first_index(blk, m1, sub_f, float(gs))
        m2 = jnp.max(jnp.where(sub_f == i1, ninf, blk), axis=0, keepdims=True)
        gscore.append(m1 + m2)
    gsc = jnp.concatenate(gscore, axis=0)
    grp_f = _iota2((N_GROUPS, tm), 0).astype(F32)
    gmask = jnp.zeros((N_GROUPS, tm), F32)
    for _ in range(TOPK_GROUPS):
        m = jnp.max(gsc, axis=0, keepdims=True)
        gi = _first_index(gsc, m, grp_f, float(N_GROUPS))
        hit = grp_f == gi
        gmask = jnp.where(hit, 1.0, gmask)
        gsc = jnp.where(hit, ninf, gsc)
    masked = jnp.concatenate(
        [jnp.where(gmask[g:g + 1, :] > 0.0, sel[g * gs:(g + 1) * gs, :], ninf) for g in range(N_GROUPS)], axis=0)

    exp_f = _iota2((e, tm), 0).astype(F32)
    chosen = jnp.zeros((e, tm), F32)
    idxs, gates = [], []
    for _ in range(TOP_K):
        m = jnp.max(masked, axis=0, keepdims=True)
        ei = _first_index(masked, m, exp_f, float(e))
        hit = exp_f == ei
        idxs.append(ei)
        gates.append(jnp.sum(jnp.where(hit, scores, 0.0), axis=0, keepdims=True))
        chosen = jnp.where(hit, 1.0, chosen)
        masked = jnp.where(hit, ninf, masked)
    gate = jnp.concatenate(gates, axis=0)
    gate = gate / jnp.sum(gate, axis=0, keepdims=True) * ROUTED_SCALE
    idx_f = jnp.concatenate(idxs, axis=0)

    upper = (_iota2((tm, tm), 0) < _iota2((tm, tm), 1)).astype(BF16)
    before = _dot(chosen.astype(BF16), upper) + carry_ref[...][:, 0:1]
    ranks = [jnp.sum(jnp.where(exp_f == idxs[k], before, 0.0), axis=0, keepdims=True) for k in range(TOP_K)]
    carry_ref[...] = carry_ref[...] + jnp.sum(chosen, axis=1, keepdims=True)

    idx_ref[...] = idx_f.astype(jnp.int32)
    gate_ref[...] = gate
    rank_ref[...] = jnp.concatenate(ranks, axis=0).astype(jnp.int32)
    cnt_ref[...] = carry_ref[...]


def _router(x, wt, bias_col, tm=512):
    t, d = x.shape
    col = lambda i: (0, i)
    fix = lambda i: (0, 0)
    return pl.pallas_call(
        _router_kernel,
        grid=(t // tm,),
        in_specs=[pl.BlockSpec((tm, d), lambda i: (i, 0)), pl.BlockSpec(wt.shape, fix), pl.BlockSpec((N_EXPERTS, LANES), fix)],
        out_specs=[pl.BlockSpec((TOP_K, tm), col), pl.BlockSpec((TOP_K, tm), col), pl.BlockSpec((TOP_K, tm), col),
                   pl.BlockSpec((N_EXPERTS, LANES), fix)],
        out_shape=[jax.ShapeDtypeStruct((TOP_K, t), jnp.int32), jax.ShapeDtypeStruct((TOP_K, t), F32),
                   jax.ShapeDtypeStruct((TOP_K, t), jnp.int32), jax.ShapeDtypeStruct((N_EXPERTS, LANES), F32)],
        scratch_shapes=[pltpu.VMEM((N_EXPERTS, LANES), F32)],
        compiler_params=_cparams(("arbitrary",)),
        name="router",
    )(x, wt, bias_col)


def _dest_kernel(idx_ref, rank_ref, start_ref, dest_ref):
    tm = idx_ref.shape[1]
    exp_i = _iota2((N_EXPERTS, tm), 0)
    start = start_ref[:, 0:1]
    rows = [jnp.sum(jnp.where(exp_i == idx_ref[s:s + 1, :], start, 0.0), axis=0, keepdims=True) for s in range(TOP_K)]
    dest_ref[...] = jnp.concatenate(rows, axis=0).astype(jnp.int32) + rank_ref[...]


def _dest_rows(idx, rank, start_col, tm=2048):
    t = idx.shape[1]
    tm = min(tm, t)
    col = lambda i: (0, i)
    return pl.pallas_call(
        _dest_kernel,
        grid=(t // tm,),
        in_specs=[pl.BlockSpec((TOP_K, tm), col), pl.BlockSpec((TOP_K, tm), col),
                  pl.BlockSpec((N_EXPERTS, LANES), lambda i: (0, 0))],
        out_specs=pl.BlockSpec((TOP_K, tm), col),
        out_shape=jax.ShapeDtypeStruct((TOP_K, t), jnp.int32),
        compiler_params=_cparams(("arbitrary",)),
        name="moe_dest",
    )(idx, rank, start_col)


def _pack_pairs(x):
    n = x.shape[1] // 2
    hi = lax.bitcast_convert_type(x[:, :n].astype(BF16).astype(F32), jnp.uint32)
    lo = lax.bitcast_convert_type(x[:, n:].astype(BF16).astype(F32), jnp.uint32)
    return hi | (lo >> 16)


def _unpack_pairs(w):
    hi = lax.bitcast_convert_type(w & jnp.uint32(0xFFFF0000), F32)
    lo = lax.bitcast_convert_type(w << 16, F32)
    return hi, lo


def _sc_scatter_rows(xp, dest, rows, chunk=LANES):
    t, width = xp.shape
    info = plsc.get_sparse_core_info()
    ncores, nsub = info.num_cores, info.num_subcores
    per_worker = t // (ncores * nsub)
    nchunk = per_worker // chunk
    mesh = plsc.VectorSubcoreMesh(core_axis_name="c", subcore_axis_name="s")

    @functools.partial(
        pl.kernel, mesh=mesh,
        out_type=jax.ShapeDtypeStruct((rows, width), xp.dtype),
        scratch_types=[pltpu.VMEM((TOP_K, chunk), jnp.int32), pltpu.VMEM((chunk, width), xp.dtype), pltpu.SemaphoreType.DMA],
    )
    def scatter(xp_hbm, dest_hbm, out_hbm, idx_v, rows_v, sem):
        base = (lax.axis_index("s") * ncores + lax.axis_index("c")) * per_worker

        @pl.loop(0, nchunk)
        def _(i):
            off = pl.multiple_of(base + i * chunk, chunk)
            pltpu.sync_copy(dest_hbm.at[:, pl.ds(off, chunk)], idx_v)
            pltpu.sync_copy(xp_hbm.at[pl.ds(off, chunk)], rows_v)
            copies = [pltpu.async_copy(rows_v, out_hbm.at[idx_v.at[s]], sem) for s in range(TOP_K)]
            for cp in copies:
                cp.wait()

    return scatter(xp, dest)


def _experts_kernel(be_ref, nu_ref, nv_ref, first_ref, slot_ref, nxt_ref, xs_ref, wg_hbm, wu_hbm, wd_hbm, ys_ref,
                    wgf_ref, wuf_ref, wdf_ref, wgb_ref, wub_ref, wdb_ref, sem, *, layer):
    i = pl.program_id(0)

    def fetch(e, s):
        return [pltpu.make_async_copy(wg_hbm.at[layer, e], wgf_ref.at[s], sem.at[s]),
                pltpu.make_async_copy(wu_hbm.at[layer, e], wuf_ref.at[s], sem.at[s]),
                pltpu.make_async_copy(wd_hbm.at[layer, e], wdf_ref.at[s], sem.at[s])]

    @pl.when(i == 0)
    def _():
        for cp in fetch(be_ref[0], 0):
            cp.start()

    @pl.when(jnp.logical_and(first_ref[i] == 1, i < nu_ref[0]))
    def _():
        s = slot_ref[i]
        for cp in fetch(be_ref[i], s):
            cp.wait()
        wgb_ref[...] = wgf_ref[s].astype(BF16)
        wub_ref[...] = wuf_ref[s].astype(BF16)
        wdb_ref[...] = wdf_ref[s].astype(BF16)

        @pl.when(nxt_ref[i] >= 0)
        def _():
            for cp in fetch(nxt_ref[i], 1 - s):
                cp.start()

    sub = xs_ref.shape[0] // EXPERT_SUBBLOCKS

    def swiglu(nsub):
        acts = []
        for r in range(nsub):
            rows = pl.ds(r * sub, sub)
            live = (_iota2((sub, 1), 0) + r * sub) < nv_ref[i]
            xa, xb = _unpack_pairs(jnp.where(live, xs_ref[rows, :], jnp.uint32(0)))
            x = jnp.concatenate([xa.astype(BF16), xb.astype(BF16)], axis=1)
            acts.append((_dot(x, wgb_ref[...]), _dot(x, wub_ref[...])))
        outs = [_dot((_silu(gate) * up).astype(BF16), wdb_ref[...]) for gate, up in acts]
        for r, y in enumerate(outs):
            ys_ref[pl.ds(r * sub, sub), :] = _pack_pairs(y)

    for nsub in range(1, EXPERT_SUBBLOCKS + 1):
        lo = (nsub - 1) * sub
        in_range = jnp.logical_and(nv_ref[i] > lo, nv_ref[i] <= nsub * sub)

        @pl.when(jnp.logical_and(i < nu_ref[0], in_range))
        def _(nsub=nsub):
            swiglu(nsub)


def _experts(block_e, n_used, n_valid, xs, wg, wu, wd, layer, block):
    rows, half = xs.shape
    d = 2 * half
    nb = rows // block
    pos = jnp.arange(nb, dtype=jnp.int32)
    first = jnp.concatenate([jnp.ones((1,), jnp.int32), (block_e[1:] != block_e[:-1]).astype(jnp.int32)])
    slot = (jnp.cumsum(first) - 1) % 2
    later = (pos[None, :] > pos[:, None]) & (block_e[None, :] != block_e[:, None]) & (pos[None, :] < n_used[0])
    nxt_pos = jnp.min(jnp.where(later, pos[None, :], nb), axis=1)
    nxt = jnp.where(nxt_pos < nb, block_e[jnp.minimum(nxt_pos, nb - 1)], -1)
    blk = lambda i, be, nu, *rest: (jnp.minimum(i, nu[0] - 1), 0)
    hbm = pl.BlockSpec(memory_space=pl.ANY)
    return pl.pallas_call(
        functools.partial(_experts_kernel, layer=layer),
        grid_spec=pltpu.PrefetchScalarGridSpec(
            num_scalar_prefetch=6,
            grid=(nb,),
            in_specs=[pl.BlockSpec((block, half), blk), hbm, hbm, hbm],
            out_specs=pl.BlockSpec((block, half), blk),
            scratch_shapes=[pltpu.VMEM((2, d, D_EXPERT), F32), pltpu.VMEM((2, d, D_EXPERT), F32),
                            pltpu.VMEM((2, D_EXPERT, d), F32),
                            pltpu.VMEM((d, D_EXPERT), BF16), pltpu.VMEM((d, D_EXPERT), BF16),
                            pltpu.VMEM((D_EXPERT, d), BF16), pltpu.SemaphoreType.DMA((2,))],
        ),
        out_shape=jax.ShapeDtypeStruct((rows, half), jnp.uint32),
        compiler_params=_cparams(("arbitrary",)),
        name="moe_experts",
    )(block_e, n_used, n_valid, first, slot.astype(jnp.int32), nxt.astype(jnp.int32), xs, wg, wu, wd)


def _sc_gather_rows(table, idx, chunk=SC_CHUNK):
    n = idx.shape[0]
    width = table.shape[1]
    info = plsc.get_sparse_core_info()
    ncores, nsub = info.num_cores, info.num_subcores
    per_worker = n // (ncores * nsub)
    nchunk = per_worker // chunk
    mesh = plsc.VectorSubcoreMesh(core_axis_name="c", subcore_axis_name="s")

    @functools.partial(
        pl.kernel, mesh=mesh,
        out_type=jax.ShapeDtypeStruct((n, width), table.dtype),
        scratch_types=[pltpu.VMEM((nchunk, chunk), jnp.int32), pltpu.VMEM((2, chunk, width), table.dtype),
                       pltpu.SemaphoreType.DMA((2,)), pltpu.SemaphoreType.DMA((2,))],
    )
    def gather(table_hbm, idx_hbm, out_hbm, idx_v, rows_v, gsem, wsem):
        wid = lax.axis_index("s") * ncores + lax.axis_index("c")
        base = wid * per_worker
        pltpu.sync_copy(idx_hbm.at[pl.ds(wid * nchunk, nchunk)], idx_v)

        def fetch(j, b):
            return pltpu.make_async_copy(table_hbm.at[idx_v.at[j]], rows_v.at[b], gsem.at[b])

        def flush(j, b):
            off = pl.multiple_of(base + j * chunk, chunk)
            return pltpu.make_async_copy(rows_v.at[b], out_hbm.at[pl.ds(off, chunk)], wsem.at[b])

        fetch(0, 0).start()

        @pl.loop(0, nchunk, step=2)
        def _(i):
            for b in range(2):
                j = i + b
                fetch(j, b).wait()

                @pl.when(j + 1 < nchunk)
                def _():
                    @pl.when(j >= 1)
                    def _():
                        flush(j - 1, 1 - b).wait()

                    fetch(j + 1, 1 - b).start()

                flush(j, b).start()

        flush(nchunk - 2, 0).wait()
        flush(nchunk - 1, 1).wait()

    return gather(table, idx.reshape(n // chunk, chunk))


def _shared_kernel(xp_ref, sg_ref, su_ref, sd_ref, o_ref):
    xa, xb = _unpack_pairs(xp_ref[...])
    x = jnp.concatenate([xa.astype(BF16), xb.astype(BF16)], axis=1)
    hs = _silu(_dot(x, sg_ref[...])) * _dot(x, su_ref[...])
    o_ref[...] = _pack_pairs(_dot(hs.astype(BF16), sd_ref[...]))


def _shared_expert(xp, sg, su, sd, tm=512):
    t, half = xp.shape
    row = lambda i: (i, 0)
    fix = lambda i: (0, 0)
    return pl.pallas_call(
        _shared_kernel,
        grid=(t // tm,),
        in_specs=[pl.BlockSpec((tm, half), row), pl.BlockSpec(sg.shape, fix), pl.BlockSpec(su.shape, fix),
                  pl.BlockSpec(sd.shape, fix)],
        out_specs=pl.BlockSpec((tm, half), row),
        out_shape=jax.ShapeDtypeStruct((t, half), jnp.uint32),
        compiler_params=_cparams(("arbitrary",)),
        name="moe_shared",
    )(xp, sg, su, sd)


def _combine_kernel(x_ref, gate_ref, rows_ref, sh_ref, g_ref, b_ref, o_ref):
    gate = gate_ref[...]
    ya, yb = _unpack_pairs(sh_ref[...])
    for s in range(TOP_K):
        a, b = _unpack_pairs(rows_ref[s])
        ya = ya + gate[:, s:s + 1] * a
        yb = yb + gate[:, s:s + 1] * b
    ff = jnp.concatenate([ya, yb], axis=1)
    o_ref[...] = _layer_norm(DN_ALPHA * x_ref[...] + ff, g_ref[...], b_ref[...])


def _combine(x, gate_t, rows, shared, g, b, tm=512):
    t, d = x.shape
    row = lambda i: (i, 0)
    fix = lambda i: (0, 0)
    return pl.pallas_call(
        _combine_kernel,
        grid=(t // tm,),
        in_specs=[pl.BlockSpec((tm, d), row), pl.BlockSpec((tm, TOP_K), row),
                  pl.BlockSpec((TOP_K, tm, d // 2), lambda i: (0, i, 0)), pl.BlockSpec((tm, d // 2), row),
                  pl.BlockSpec((1, d), fix), pl.BlockSpec((1, d), fix)],
        out_specs=pl.BlockSpec((tm, d), row),
        out_shape=jax.ShapeDtypeStruct((t, d), F32),
        compiler_params=_cparams(("arbitrary",)),
        name="moe_combine",
    )(x, gate_t, rows, shared, g, b)


def _take_cols(w, idx):
    idx = np.asarray(idx)
    runs, start = [], 0
    for pos in range(1, len(idx) + 1):
        run_ends = pos == len(idx) or (idx[pos] != idx[pos - 1] + 1 if idx[pos - 1] >= 0 else idx[pos] >= 0)
        if run_ends:
            runs.append((start, int(idx[start]), pos - start))
            start = pos

    def body(w_ref, o_ref):
        for dst, src, width in runs:
            if src < 0:
                o_ref[:, dst:dst + width] = jnp.zeros((o_ref.shape[0], width), o_ref.dtype)
            else:
                o_ref[:, dst:dst + width] = w_ref[:, src:src + width].astype(o_ref.dtype)

    rows = w.shape[0]
    tr = min(rows, 256)
    return pl.pallas_call(
        body,
        grid=(rows // tr,),
        in_specs=[pl.BlockSpec((tr, w.shape[1]), lambda i: (i, 0))],
        out_specs=pl.BlockSpec((tr, len(idx)), lambda i: (i, 0)),
        out_shape=jax.ShapeDtypeStruct((rows, len(idx)), BF16),
        compiler_params=_cparams(("arbitrary",)),
        name="weight_cols",
    )(w)


def _pad_lane_row(v, first_lane, width=LANES):
    out = jnp.zeros((1, width), F32)
    return lax.dynamic_update_slice(out, v.reshape(1, -1).astype(F32), (0, first_lane))


def _even_in_cols():
    z = lambda n: -np.ones(n, int)
    kr0 = Q_LORA + KV_LORA
    half = MLA_ROPE // 2
    cols = [np.arange(0, Q_LORA), np.arange(Q_LORA, Q_LORA + KV_LORA),
            z(64), np.arange(kr0, kr0 + MLA_ROPE), z(32),
            z(64), np.arange(kr0 + half, kr0 + MLA_ROPE), np.arange(kr0, kr0 + half), z(32)]
    g0 = kr0 + MLA_ROPE
    nqk = GDN_H * GDN_DK
    cols.append(np.arange(g0, g0 + 3 * nqk))
    zoff = g0 + 3 * nqk + 2 * GDN_H
    cols.append(np.arange(zoff, zoff + GDN_H * GDN_DV))
    cols += [np.arange(g0 + 3 * nqk, g0 + 3 * nqk + 2 * GDN_H), z(LANES - 2 * GDN_H)]
    return np.concatenate(cols)


EV_WIDTHS = (Q_LORA + KV_LORA + 2 * LANES, 3 * GDN_H * GDN_DK, GDN_H * GDN_DV, LANES)


def _mla_q_cols():
    per = MLA_NOPE + MLA_ROPE
    half = MLA_ROPE // 2
    main, sw = [], []
    for h in range(MLA_H):
        b = h * per
        main += [np.arange(b, b + per), -np.ones(LANES - per, int)]
        sw += [-np.ones(MLA_NOPE, int), np.arange(b + MLA_NOPE + half, b + per), np.arange(b + MLA_NOPE, b + MLA_NOPE + half),
               -np.ones(LANES - per, int)]
    return np.concatenate(main + sw)


def _mla_kv_cols():
    per = MLA_NOPE + MLA_V
    kc, vc = [], []
    for h in range(MLA_H):
        b = h * per
        kc += [np.arange(b, b + MLA_NOPE), -np.ones(LANES - MLA_NOPE, int)]
        vv = np.arange(b + MLA_NOPE, b + per)
        pad = -np.ones(LANES - MLA_V, int)
        vc += [vv, pad] if h % 2 == 0 else [pad, vv]
    return np.concatenate(kc + vc)


def _odd_in_cols():
    z = lambda n: -np.ones(n, int)
    o = 0
    cols = []
    mq0, mk0 = 0, ML_H * ML_DK
    for base in (mq0, mk0):
        for h in range(ML_H):
            cols += [np.arange(base + h * ML_DK, base + (h + 1) * ML_DK), z(LANES - ML_DK)]
    mv0 = 2 * ML_H * ML_DK
    cols.append(np.arange(mv0, mv0 + ML_H * ML_DV))
    mi0 = mv0 + ML_H * ML_DV
    mo0 = mi0 + 2 * ML_H
    cols.append(np.arange(mo0, mo0 + ML_H * ML_DV))
    cols += [np.arange(mi0, mi0 + 2 * ML_H), z(LANES - 2 * ML_H)]
    sq0 = mo0 + ML_H * ML_DV
    sk0 = sq0 + SWA_H * SWA_D
    sv0 = sk0 + SWA_KV * SWA_D
    half = SWA_D // 2

    def heads(base, n, swapped, copies):
        out = []
        for h in range(n):
            b = base + h * SWA_D
            one = [np.arange(b + half, b + SWA_D), np.arange(b, b + half)] if swapped else [np.arange(b, b + SWA_D)]
            out += one * copies
        return out

    cols += (heads(sq0, SWA_H, False, 1) + heads(sq0, SWA_H, True, 1)
             + heads(sk0, SWA_KV, False, 2) + heads(sk0, SWA_KV, True, 2))
    for g in range(SWA_KV):
        vv = np.arange(sv0 + g * SWA_D, sv0 + (g + 1) * SWA_D)
        cols += [vv, z(LANES - SWA_D), z(LANES - SWA_D), vv]
    return np.concatenate(cols)


def _even_weights(w_in, w_qb, w_kvb):
    return (_take_cols(w_in, _even_in_cols()), _take_cols(w_qb, _mla_q_cols()), _take_cols(w_kvb, _mla_kv_cols()))


def _even_mixer(x, tabs, weights, q_norm, kv_norm, conv_w, a_log, dt_bias, o_norm, batch, seq):
    ctab, stab = tabs
    w, wq2, wkv2 = weights
    mla_in, act, z, gates = _proj_even(x, w, conv_w, seq)
    q, k, v = _mla_prep(mla_in, ctab, stab, q_norm.reshape(1, -1), kv_norm.reshape(1, -1), wq2, wkv2)
    o_a = _mla_attn(q, k, v, batch, seq)
    o_b = _gdn(act, gates, z, _pad_lane_row(a_log, GDN_H), _pad_lane_row(dt_bias, GDN_H),
               o_norm.reshape(1, -1), batch, seq)
    return o_a, o_b


def _odd_mixer(x, tabs, w, b_i, b_f, ml_norm, sinks, batch, seq):
    ctab, stab = tabs
    mq, mk, mv, mo, mg, sq, sk, sv = _proj_odd(x, w, ctab, stab)
    bias_row = _pad_lane_row(jnp.concatenate([b_i, b_f]), 0)
    o_c = _mlstm(mq, mk, mv, mo, mg, bias_row, ml_norm.reshape(1, -1), batch, seq)
    o_d = _swa(sq, sk, sv, _pad_lane_row(sinks, 0), batch, seq)
    return o_c, o_d


def _moe(x, xp, router_w, router_b, w_gate, w_up, w_down, layer, s_gate, s_up, s_down, ln_g, ln_b):
    t, d = x.shape
    bias_col = jnp.broadcast_to(router_b.reshape(-1, 1).astype(F32), (N_EXPERTS, LANES))
    idx, gate, rank, cnt = _router(x, router_w.T, bias_col)
    counts = cnt[:, 0].astype(jnp.int32)
    block = int(min(max(pl.next_power_of_2(t * TOP_K // N_EXPERTS) // 2, EXPERT_BLOCK_MIN), EXPERT_BLOCK_MAX))
    padded = (counts + block - 1) // block * block
    pad_end = jnp.cumsum(padded)
    pad_start = pad_end - padded
    start_col = jnp.broadcast_to(pad_start.astype(F32).reshape(-1, 1), (N_EXPERTS, LANES))
    dest = _dest_rows(idx, rank, start_col)
    n_blocks = t * TOP_K // block + N_EXPERTS
    rows = n_blocks * block
    block_row = jnp.arange(n_blocks, dtype=jnp.int32) * block
    block_e = jnp.minimum(jnp.sum((pad_end[None, :] <= block_row[:, None]).astype(jnp.int32), axis=1), N_EXPERTS - 1)
    n_used = (pad_end[-1:] // block).astype(jnp.int32)
    live_end = jnp.sum(jnp.where(block_e[:, None] == jnp.arange(N_EXPERTS, dtype=jnp.int32)[None, :],
                                 (pad_start + counts)[None, :], 0), axis=1)
    n_valid = jnp.clip(live_end - block_row, 0, block).astype(jnp.int32)
    xs = _sc_scatter_rows(xp, dest, rows)
    ys = _experts(block_e, n_used, n_valid, xs, w_gate, w_up, w_down, layer, block)
    picked = _sc_gather_rows(ys, dest.reshape(-1)).reshape(TOP_K, t, d // 2)
    shared = _shared_expert(xp, s_gate.astype(BF16), s_up.astype(BF16), s_down.astype(BF16))
    return _combine(x, gate.T, picked, shared, ln_g.reshape(1, -1), ln_b.reshape(1, -1))


def kernel(x, positions, ev_w_in, mla_q_norm, mla_w_qb, mla_kv_norm, mla_w_kvb, gdn_conv, gdn_a_log, gdn_dt_bias, gdn_norm, ev_w_out, od_w_in, mlstm_b_i, mlstm_b_f, mlstm_norm, swa_sinks, od_w_out, ln1_g, ln1_b, router_w, router_b, moe_w_gate, moe_w_up, moe_w_down, shared_w_gate, shared_w_up, shared_w_down, ln2_g, ln2_b):
    batch, seq, d = x.shape
    streams = STREAMS if batch % STREAMS == 0 else 1
    sb = batch // streams
    ts = sb * seq
    hs, tabs_m, tabs_s = [], [], []
    for s in range(streams):
        pos = positions[s * sb:(s + 1) * sb].reshape(ts, 1).astype(F32)
        tabs_m.append(_rope_tables(pos, _rope_rows(MLA_ROPE, MLA_NOPE, MLA_NOPE)))
        tabs_s.append(_rope_tables(pos, _rope_rows(SWA_D, 0, 0, heads=LANES // SWA_D)))
        hs.append(x[s * sb:(s + 1) * sb].reshape(ts, d))
    for layer in range(DEPTH):
        j = layer // 2
        if layer % 2 == 0:
            weights = _even_weights(ev_w_in[j], mla_w_qb[j], mla_w_kvb[j])
            w_out = ev_w_out[j].astype(BF16)
        else:
            weights = _take_cols(od_w_in[j], _odd_in_cols())
            w_out = od_w_out[j].astype(BF16)
        for s in range(streams):
            h = hs[s]
            if layer % 2 == 0:
                a1, a2 = _even_mixer(h, tabs_m[s], weights, mla_q_norm[j], mla_kv_norm[j], gdn_conv[j], gdn_a_log[j],
                                     gdn_dt_bias[j], gdn_norm[j], sb, seq)
            else:
                a1, a2 = _odd_mixer(h, tabs_s[s], weights, mlstm_b_i[j], mlstm_b_f[j], mlstm_norm[j], swa_sinks[j], sb, seq)
            h, hp = _outproj_ln(h, a1, a2, w_out, ln1_g[layer].reshape(1, -1), ln1_b[layer].reshape(1, -1))
            hs[s] = _moe(h, hp, router_w[layer], router_b[layer], moe_w_gate, moe_w_up, moe_w_down, layer,
                         shared_w_gate[layer], shared_w_up[layer], shared_w_down[layer], ln2_g[layer], ln2_b[layer])
    return jnp.concatenate([h.reshape(sb, seq, d) for h in hs], axis=0)
```

```python
import functools
import math

import numpy as np
import jax
import jax.numpy as jnp
from jax import lax
from jax.experimental import pallas as pl
from jax.experimental.pallas import tpu as pltpu
from jax.experimental.pallas import tpu_sc as plsc

F32 = jnp.float32
BF16 = jnp.bfloat16
HI = lax.Precision.HIGHEST

D_MODEL = 1024
DEPTH = 4
ROPE_THETA = 10000.0
EPS = 1e-6
LN_EPS = 1e-5
MLA_H, MLA_NOPE, MLA_ROPE, MLA_V = 8, 64, 32, 64
Q_LORA, KV_LORA = 256, 128
GDN_H, GDN_DK, GDN_DV, CONV_W, GDN_CHUNK = 4, 128, 128, 4, 64
ML_H, ML_DK, ML_DV, ML_CHUNK = 4, 64, 128, 64
SWA_H, SWA_KV, SWA_D, WINDOW = 8, 2, 64, 128
N_EXPERTS, N_GROUPS, TOPK_GROUPS, TOP_K = 64, 8, 4, 8
D_EXPERT, D_SHARED = 256, 256
ROUTED_SCALE = 2.5
DN_ALPHA = (2 * DEPTH) ** 0.25

LANES = 128
V7X_VMEM_BYTES = 64 * 1024 * 1024
VMEM_LIMIT = 48 * 1024 * 1024

EXPERT_BLOCK_MIN = 256
EXPERT_BLOCK_MAX = 1024
STREAMS = 1
EXPERT_SUBBLOCKS = 4
SWA_SEQS_PER_STEP = 8
MLSTM_SEQS_PER_STEP = 2
GDN_SEQS_PER_STEP = 8
SC_CHUNK = 64


def _cparams(sem, vmem=VMEM_LIMIT):
    return pltpu.CompilerParams(dimension_semantics=sem, vmem_limit_bytes=vmem)


def _dot(a, b, precision=None):
    return jnp.dot(a, b, preferred_element_type=F32, precision=precision)


def _dot_nt(a, b, precision=None):
    return lax.dot_general(a, b, (((1,), (1,)), ((), ())), preferred_element_type=F32, precision=precision)


def _dot_tn(a, b, precision=None):
    return lax.dot_general(a, b, (((0,), (0,)), ((), ())), preferred_element_type=F32, precision=precision)


def _split2(a):
    hi = a.astype(BF16)
    lo = (a - hi.astype(F32)).astype(BF16)
    return hi, lo


def _split3(a):
    p1 = a.astype(BF16)
    r = a - p1.astype(F32)
    p2 = r.astype(BF16)
    p3 = (r - p2.astype(F32)).astype(BF16)
    return p1, p2, p3


def _dot3(a, b, dot=_dot):
    ah, al = _split2(a)
    bh, bl = _split2(b)
    return dot(ah, bh) + (dot(ah, bl) + dot(al, bh))


def _dot_sel(sel, b, dot=_dot):
    sel = sel.astype(BF16)
    p1, p2, p3 = _split3(b)
    return dot(sel, p1) + (dot(sel, p2) + dot(sel, p3))


def _sigmoid(x):
    return 1.0 / (1.0 + jnp.exp(-x))


def _softplus(x):
    return jnp.maximum(x, 0.0) + jnp.log(1.0 + jnp.exp(-jnp.abs(x)))


def _silu(x):
    return x * _sigmoid(x)


def _lane_bcast(x, c):
    return jnp.broadcast_to(x[:, c:c + 1], x.shape)


def _iota2(shape, dim):
    return lax.broadcasted_iota(jnp.int32, shape, dim)


def _rope_kernel(pos_ref, rows_ref, sel_ref, cm_ref, sm_ref, cs_ref, ss_ref):
    ang = pos_ref[...] * rows_ref[0:1, :]
    cos_parts = _split3(jnp.cos(ang))
    sin_parts = _split3(jnp.sin(ang))

    def place(parts, k):
        return _dot(parts[0], sel_ref[k]) + (_dot(parts[1], sel_ref[k]) + _dot(parts[2], sel_ref[k]))

    cm_ref[...] = place(cos_parts, 0) + rows_ref[1:2, :]
    sm_ref[...] = place(sin_parts, 1)
    cs_ref[...] = place(cos_parts, 2)
    ss_ref[...] = place(sin_parts, 3)


def _rope_consts():
    hm, hs = MLA_ROPE // 2, SWA_D // 2
    rows = np.zeros((8, LANES), np.float32)
    rows[0, :hm] = ROPE_THETA ** (-(np.arange(0, MLA_ROPE, 2, dtype=np.float32) / MLA_ROPE))
    rows[0, hm:hm + hs] = ROPE_THETA ** (-(np.arange(0, SWA_D, 2, dtype=np.float32) / SWA_D))
    rows[1, :MLA_NOPE] = 1.0
    sel = np.zeros((4, LANES, LANES), np.float32)
    for j in range(hm):
        sel[0, j, MLA_NOPE + j] = sel[0, j, MLA_NOPE + hm + j] = 1.0
        sel[1, j, MLA_NOPE + j] = -1.0
        sel[1, j, MLA_NOPE + hm + j] = 1.0
    for h in range(LANES // SWA_D):
        for j in range(hs):
            sel[2, hm + j, h * SWA_D + j] = sel[2, hm + j, h * SWA_D + hs + j] = 1.0
            sel[3, hm + j, h * SWA_D + j] = -1.0
            sel[3, hm + j, h * SWA_D + hs + j] = 1.0
    return jnp.asarray(rows), jnp.asarray(sel, BF16)


def _rope_tables(pos, tm=512):
    t = pos.shape[0]
    tm = min(tm, t)
    rows, sel = _rope_consts()
    cm, sm, cs, ss = pl.pallas_call(
        _rope_kernel,
        grid=(t // tm,),
        in_specs=[pl.BlockSpec((tm, 1), lambda i: (i, 0)), pl.BlockSpec((8, LANES), lambda i: (0, 0)),
                  pl.BlockSpec((4, LANES, LANES), lambda i: (0, 0, 0))],
        out_specs=[pl.BlockSpec((tm, LANES), lambda i: (i, 0))] * 4,
        out_shape=[jax.ShapeDtypeStruct((t, LANES), F32)] * 4,
        compiler_params=_cparams(("arbitrary",)),
        name="rope_tables",
    )(pos, rows, sel)
    return (cm, sm), (cs, ss)


def _proj_kernel(x_ref, w_ref, *out_refs, offsets):
    xb = x_ref[...].astype(BF16)
    for o_ref, (a, b) in zip(out_refs, offsets):
        o_ref[...] = _dot(xb, w_ref[:, a:b]).astype(o_ref.dtype)


def _proj(x, w, widths, dtypes, tm=512):
    t, k = x.shape
    offs = np.concatenate([[0], np.cumsum(widths)]).tolist()
    offsets = tuple((offs[i], offs[i + 1]) for i in range(len(widths)))
    return pl.pallas_call(
        functools.partial(_proj_kernel, offsets=offsets),
        grid=(t // tm,),
        in_specs=[pl.BlockSpec((tm, k), lambda i: (i, 0)), pl.BlockSpec(w.shape, lambda i: (0, 0))],
        out_specs=[pl.BlockSpec((tm, n), lambda i: (i, 0)) for n in widths],
        out_shape=[jax.ShapeDtypeStruct((t, n), dt) for n, dt in zip(widths, dtypes)],
        compiler_params=_cparams(("arbitrary",)),
        name="in_proj",
    )(x, w)


def _proj_even_kernel(x_ref, w_ref, cw_ref, mla_ref, act_ref, z_ref, g_ref, ext_ref, *, tiles_per_seq):
    tm = x_ref.shape[0]
    o = np.concatenate([[0], np.cumsum(EV_WIDTHS)]).tolist()
    @pl.when(pl.program_id(0) % tiles_per_seq == 0)
    def _():
        ext_ref[0:8, :] = jnp.zeros((8, ext_ref.shape[1]), F32)

    xb = x_ref[...].astype(BF16)
    nchunk = 3
    cw = EV_WIDTHS[1] // nchunk

    def project(ci):
        ext_ref[8:8 + tm, ci * cw:(ci + 1) * cw] = _dot(xb, w_ref[:, o[1] + ci * cw:o[1] + (ci + 1) * cw])

    project(0)
    for ci in range(nchunk):
        if ci + 1 < nchunk:
            project(ci + 1)
        else:
            mla_ref[...] = _dot(xb, w_ref[:, o[0]:o[1]])
            z_ref[...] = _dot(xb, w_ref[:, o[2]:o[3]]).astype(z_ref.dtype)
            g_ref[...] = _dot(xb, w_ref[:, o[3]:o[4]])
        cols = slice(ci * cw, (ci + 1) * cw)
        conv = cw_ref[0:1, cols] * ext_ref[5:5 + tm, cols]
        for j in range(1, CONV_W):
            conv = conv + cw_ref[j:j + 1, cols] * ext_ref[5 + j:5 + j + tm, cols]
        act_ref[:, cols] = _silu(conv).astype(act_ref.dtype)
    ext_ref[0:8, :] = ext_ref[tm:tm + 8, :]


def _proj_even(x, w, conv_w, seq, tm=512):
    t, k = x.shape
    tm = min(tm, seq)
    row = lambda i: (i, 0)
    fix = lambda i: (0, 0)
    return pl.pallas_call(
        functools.partial(_proj_even_kernel, tiles_per_seq=seq // tm),
        grid=(t // tm,),
        in_specs=[pl.BlockSpec((tm, k), row), pl.BlockSpec(w.shape, fix), pl.BlockSpec(conv_w.shape, fix)],
        out_specs=[pl.BlockSpec((tm, n), row) for n in EV_WIDTHS],
        out_shape=[jax.ShapeDtypeStruct((t, n), F32) for n in EV_WIDTHS],
        scratch_shapes=[pltpu.VMEM((tm + 8, EV_WIDTHS[1]), F32)],
        compiler_params=_cparams(("arbitrary",)),
        name="in_proj",
    )(x, w, conv_w)


OD_SEG = dict(mq=(0, 512), mk=(512, 1024), mv=(1024, 1536), mo=(1536, 2048), gates=(2048, 2176),
              sq=(2176, 2688), sqsw=(2688, 3200), sk=(3200, 3456), sksw=(3456, 3712), sv=(3712, 4224))
OD_COLS = 4224


def _proj_odd_kernel(x_ref, w_ref, c_ref, s_ref, mq_ref, mk_ref, mv_ref, mo_ref, mg_ref, sq_ref, sk_ref, sv_ref):
    xb = x_ref[...].astype(BF16)

    def seg(name):
        a, b = OD_SEG[name]
        return _dot(xb, w_ref[:, a:b])

    mq_ref[...] = seg("mq").astype(mq_ref.dtype)
    mk_ref[...] = seg("mk").astype(mk_ref.dtype)
    mv_ref[...] = seg("mv").astype(mv_ref.dtype)
    mo_ref[...] = seg("mo").astype(mo_ref.dtype)
    mg_ref[...] = seg("gates")
    c = c_ref[...]
    s = s_ref[...]
    c8 = jnp.concatenate([c] * (SWA_H // 2), axis=1)
    s8 = jnp.concatenate([s] * (SWA_H // 2), axis=1)
    sq_ref[...] = (seg("sq") * c8 + seg("sqsw") * s8).astype(sq_ref.dtype)
    c2 = jnp.concatenate([c] * SWA_KV, axis=1)
    s2 = jnp.concatenate([s] * SWA_KV, axis=1)
    sk_ref[...] = (seg("sk") * c2 + seg("sksw") * s2).astype(sk_ref.dtype)
    sv_ref[...] = seg("sv").astype(sv_ref.dtype)


def _proj_odd(x, w, ctab, stab, tm=512):
    t, k = x.shape
    widths = (512, 512, 512, 512, 128, SWA_H * SWA_D, SWA_KV * LANES, 2 * SWA_KV * LANES)
    dtypes = (F32, F32, F32, F32, F32, BF16, BF16, BF16)
    return pl.pallas_call(
        _proj_odd_kernel,
        grid=(t // tm,),
        in_specs=[pl.BlockSpec((tm, k), lambda i: (i, 0)), pl.BlockSpec(w.shape, lambda i: (0, 0)),
                  pl.BlockSpec((tm, LANES), lambda i: (i, 0)), pl.BlockSpec((tm, LANES), lambda i: (i, 0))],
        out_specs=[pl.BlockSpec((tm, n), lambda i: (i, 0)) for n in widths],
        out_shape=[jax.ShapeDtypeStruct((t, n), dt) for n, dt in zip(widths, dtypes)],
        compiler_params=_cparams(("arbitrary",)),
        name="in_proj_odd",
    )(x, w, ctab, stab)


def _rms(x, g):
    return x * lax.rsqrt(jnp.mean(x * x, axis=-1, keepdims=True) + EPS) * g


def _mla_prep_kernel(in_ref, c_ref, s_ref, qn_ref, kvn_ref, wq_ref, wkv_ref, q_ref, k_ref, v_ref):
    hw = MLA_H * LANES
    c = c_ref[...]
    s = s_ref[...]
    c8 = jnp.concatenate([c] * MLA_H, axis=1)
    s8 = jnp.concatenate([s] * MLA_H, axis=1)
    cqn = _rms(in_ref[:, 0:Q_LORA], qn_ref[...]).astype(BF16)
    qq = _dot(cqn, wq_ref[...])
    scale = (MLA_NOPE + MLA_ROPE) ** -0.5
    q_ref[...] = ((qq[:, :hw] * c8 + qq[:, hw:] * s8) * scale).astype(q_ref.dtype)
    ckvn = _rms(in_ref[:, Q_LORA:Q_LORA + KV_LORA], kvn_ref[...]).astype(BF16)
    kv = _dot(ckvn, wkv_ref[...])
    o = Q_LORA + KV_LORA
    krr = in_ref[:, o:o + LANES] * c + in_ref[:, o + LANES:o + 2 * LANES] * s
    k_ref[...] = (kv[:, :hw] + jnp.concatenate([krr] * MLA_H, axis=1)).astype(k_ref.dtype)
    v_ref[...] = kv[:, hw:].astype(v_ref.dtype)


def _mla_prep(mla_in, ctab, stab, qn, kvn, wq2, wkv2, tm=1024):
    t = mla_in.shape[0]
    tm = min(tm, t)
    hw = MLA_H * LANES
    row = lambda i: (i, 0)
    fix = lambda i: (0, 0)
    return pl.pallas_call(
        _mla_prep_kernel,
        grid=(t // tm,),
        in_specs=[pl.BlockSpec((tm, mla_in.shape[1]), row), pl.BlockSpec((tm, LANES), row), pl.BlockSpec((tm, LANES), row),
                  pl.BlockSpec(qn.shape, fix), pl.BlockSpec(kvn.shape, fix),
                  pl.BlockSpec(wq2.shape, fix), pl.BlockSpec(wkv2.shape, fix)],
        out_specs=[pl.BlockSpec((tm, hw), row)] * 3,
        out_shape=[jax.ShapeDtypeStruct((t, hw), BF16)] * 3,
        compiler_params=_cparams(("arbitrary",)),
        name="mla_prep",
    )(mla_in, ctab, stab, qn, kvn, wq2, wkv2)


def _mla_attn_kernel(q_ref, k_ref, v_ref, o_ref, *, tq):
    i = pl.program_id(2)
    neg = -1e30
    lane = _iota2((tq, LANES), 1)
    ones_lane = (MLA_V, 0)

    def chunk(j, carry, masked):
        start = pl.multiple_of(j * tq, tq)
        out = []
        for hh in range(2):
            m, acc = carry[hh]
            q = q_ref[:, hh * LANES:(hh + 1) * LANES]
            kc = k_ref[pl.ds(start, tq), hh * LANES:(hh + 1) * LANES]
            vc = v_ref[pl.ds(start, tq), hh * LANES:(hh + 1) * LANES]
            vc = jnp.where(lane == ones_lane[hh], jnp.ones_like(vc), vc)
            s = _dot_nt(q, kc)
            if masked:
                s = jnp.where(_iota2(s.shape, 0) >= _iota2(s.shape, 1), s, neg)
            m_new = jnp.maximum(m, jnp.max(s, axis=-1, keepdims=True))
            alpha = jnp.exp(m - m_new)
            p = jnp.exp(s - m_new)
            acc = alpha * acc + _dot(p.astype(BF16), vc)
            out.append((m_new, acc))
        return tuple(out)

    one = (jnp.full((tq, 1), neg, F32), jnp.zeros((tq, LANES), F32))
    carry = lax.fori_loop(0, i, lambda j, c: chunk(j, c, False), (one, one))
    (_, acc0), (_, acc1) = chunk(i, carry, True)
    o0 = acc0 / _lane_bcast(acc0, ones_lane[0])
    o1 = acc1 / _lane_bcast(acc1, ones_lane[1])
    o_ref[...] = jnp.where(lane < MLA_V, o0, o1).astype(o_ref.dtype)


def _mla_attn(q, k, v, batch, seq, tq=512):
    tq = min(tq, seq)
    nq = seq // tq
    pairs = MLA_H // 2
    return pl.pallas_call(
        functools.partial(_mla_attn_kernel, tq=tq),
        grid=(batch, pairs, nq),
        in_specs=[pl.BlockSpec((tq, 2 * LANES), lambda b, p, i: (b * nq + i, p)),
                  pl.BlockSpec((seq, 2 * LANES), lambda b, p, i: (b, p)),
                  pl.BlockSpec((seq, 2 * LANES), lambda b, p, i: (b, p))],
        out_specs=pl.BlockSpec((tq, LANES), lambda b, p, i: (b * nq + i, p)),
        out_shape=jax.ShapeDtypeStruct((batch * seq, pairs * LANES), BF16),
        compiler_params=_cparams(("arbitrary", "arbitrary", "arbitrary")),
        name="mla_attn",
    )(q, k, v)


def _unit_lower_inverse_many(ns):
    c = ns[0].shape[0]
    eye = (_iota2((c, c), 0) == _iota2((c, c), 1)).astype(F32)
    xs = [-n for n in ns]
    ps = [eye + x for x in xs]
    xb = [x.astype(BF16) for x in xs]
    for _ in range(int(math.log2(c)) - 1):
        xs = [_dot(b, b) for b in xb]
        xb = [x.astype(BF16) for x in xs]
        ps = [p + _dot(p.astype(BF16), b) for p, b in zip(ps, xb)]
    return ps


def _gdn_kernel(act_ref, g_ref, z_ref, al_ref, dt_ref, on_ref, o_ref, st_ref):
    c = GDN_CHUNK
    hd = GDN_DK
    nqk = GDN_H * GDN_DK

    @pl.when(pl.program_id(1) == 0)
    def _():
        st_ref[...] = jnp.zeros(st_ref.shape, F32)

    tri = (_iota2((c, c), 0) >= _iota2((c, c), 1)).astype(F32)
    row_ge = _iota2((c, c), 0) >= _iota2((c, c), 1)
    row_gt = _iota2((c, c), 0) > _iota2((c, c), 1)
    lane = _iota2((c, LANES), 1)

    seqs = []
    for bb in range(act_ref.shape[0]):
        gates = g_ref[bb]
        g_all = -jnp.exp(al_ref[...]) * _softplus(gates + dt_ref[...])
        gc_all = _dot_sel(tri, g_all)
        seqs.append(dict(beta_all=_sigmoid(gates), gc_all=gc_all, gc_parts=_split3(gc_all)))
    units = []
    for bb, sq in enumerate(seqs):
        for h in range(GDN_H):
            q = act_ref[bb, :, h * hd:(h + 1) * hd].astype(F32)
            k = act_ref[bb, :, nqk + h * hd:nqk + (h + 1) * hd].astype(F32)
            v = act_ref[bb, :, 2 * nqk + h * GDN_DV:2 * nqk + (h + 1) * GDN_DV].astype(F32)
            q = q * lax.rsqrt(jnp.sum(q * q, axis=-1, keepdims=True) + EPS) * (GDN_DK ** -0.5)
            k = k * lax.rsqrt(jnp.sum(k * k, axis=-1, keepdims=True) + EPS)
            beta = _lane_bcast(sq["beta_all"], h)
            gcol = _lane_bcast(sq["gc_all"], GDN_H + h)
            units.append(dict(bb=bb, h=h, q=q, k=k, v=v, beta=beta, gcol=gcol, kb=k * beta, parts=sq["gc_parts"]))
    for u in units:
        pick = (lane == GDN_H + u["h"]).astype(BF16)
        p0, p1, p2 = u["parts"]
        u["grow"] = _dot_nt(pick, p0) + (_dot_nt(pick, p1) + _dot_nt(pick, p2))
        u["kk"] = _dot3(u["kb"], u["k"], _dot_nt)
        u["qk"] = _dot_nt(u["q"].astype(BF16), u["k"].astype(BF16))
    for u in units:
        gcol = u["gcol"]
        decay = jnp.exp(jnp.where(row_ge, gcol[:, :c] - u["grow"], -jnp.inf))
        eg = jnp.exp(gcol)
        glast = gcol[c - 1:c, :]
        u["lower"] = jnp.where(row_gt, u["kk"] * decay, 0.0)
        u["rhs"] = jnp.concatenate([u["v"] * u["beta"], u["kb"] * eg], axis=1)
        u["attn"] = u["qk"] * decay
        u["qg"] = (u["q"] * eg).astype(BF16)
        u["kg"] = (u["k"] * jnp.exp(glast - gcol)).astype(BF16)
        u["gl"] = jnp.exp(glast)

    tinvs = _unit_lower_inverse_many([u["lower"] for u in units])
    uws = []
    for u, tinv in zip(units, tinvs):
        uws.append(_dot(tinv.astype(BF16), u["rhs"].astype(BF16)))
    states = [st_ref[u["bb"], u["h"]] for u in units]
    sbs = [s.astype(BF16) for s in states]
    vnews = [(uw[:, :GDN_DV] - _dot(uw[:, GDN_DV:].astype(BF16), sb)).astype(BF16) for uw, sb in zip(uws, sbs)]
    for u, state, sb, vnb in zip(units, states, sbs, vnews):
        bb, h = u["bb"], u["h"]
        o = _dot(u["qg"], sb) + _dot(u["attn"].astype(BF16), vnb)
        st_ref[bb, h] = state * u["gl"] + _dot_tn(u["kg"], vnb)
        o = _rms(o, on_ref[...]) * _silu(z_ref[bb, :, h * GDN_DV:(h + 1) * GDN_DV].astype(F32))
        o_ref[bb, :, h * GDN_DV:(h + 1) * GDN_DV] = o.astype(o_ref.dtype)


def _gdn(act, gates, z, a_row, dt_row, o_norm, batch, seq):
    c = GDN_CHUNK
    nc = seq // c
    w3 = act.shape[1]
    wo = GDN_H * GDN_DV
    nb = min(GDN_SEQS_PER_STEP, batch)
    row = lambda b, i: (b, i, 0)
    fix = lambda b, i: (0, 0)
    out = pl.pallas_call(
        _gdn_kernel,
        grid=(batch // nb, nc),
        in_specs=[pl.BlockSpec((nb, c, w3), row), pl.BlockSpec((nb, c, LANES), row), pl.BlockSpec((nb, c, wo), row),
                  pl.BlockSpec((1, LANES), fix), pl.BlockSpec((1, LANES), fix), pl.BlockSpec((1, GDN_DV), fix)],
        out_specs=pl.BlockSpec((nb, c, wo), row),
        out_shape=jax.ShapeDtypeStruct((batch, seq, wo), BF16),
        scratch_shapes=[pltpu.VMEM((nb, GDN_H, GDN_DK, GDN_DV), F32)],
        compiler_params=_cparams(("arbitrary", "arbitrary")),
        name="gdn",
    )(act.reshape(batch, seq, w3), gates.reshape(batch, seq, LANES), z.reshape(batch, seq, wo), a_row, dt_row, o_norm)
    return out.reshape(batch * seq, wo)


def _mlstm_kernel(q_ref, k_ref, v_ref, og_ref, g_ref, bias_ref, nrm_ref, o_ref, c_ref, n_ref, m_ref):
    @pl.when(pl.program_id(1) == 0)
    def _():
        c_ref[...] = jnp.zeros(c_ref.shape, F32)
        n_ref[...] = jnp.zeros(n_ref.shape, F32)
        m_ref[...] = jnp.zeros(m_ref.shape, F32)

    c = ML_CHUNK
    tri = (_iota2((c, c), 0) >= _iota2((c, c), 1)).astype(F32)
    row_ge = _iota2((c, c), 0) >= _iota2((c, c), 1)
    ones = jnp.ones((c, LANES), F32)
    lane = _iota2((c, LANES), 1)

    units = []
    for bb in range(q_ref.shape[0]):
        pre = g_ref[bb] + bias_ref[...]
        logf = jnp.minimum(pre, 0.0) - jnp.log(1.0 + jnp.exp(-jnp.abs(pre)))
        bcum_all = _dot_sel(tri, logf)
        for h in range(ML_H):
            q = q_ref[bb, :, h * LANES:(h + 1) * LANES].astype(F32)
            k = k_ref[bb, :, h * LANES:(h + 1) * LANES].astype(F32) * (ML_DK ** -0.5)
            units.append(dict(bb=bb, h=h, q=q, k=k, qb=q.astype(BF16), vb=v_ref[bb, :, h * ML_DV:(h + 1) * ML_DV].astype(BF16),
                              bcol=_lane_bcast(bcum_all, ML_H + h),
                              icol=_lane_bcast(pre, h),
                              col=jnp.where(lane == h, pre, 0.0) - jnp.where(lane == ML_H + h, bcum_all, 0.0),
                              m_st=m_ref[bb, h], cst=c_ref[bb, h], nst=n_ref[bb, h]))
    for u in units:
        u["row"] = _dot_sel(ones, u["col"], _dot_nt)
        u["qk"] = _dot_nt(u["qb"], u["k"].astype(BF16))
        u["qc"] = _dot(u["qb"], u["cst"].astype(BF16))
    for u in units:
        u["d"] = jnp.where(row_ge, u["bcol"][:, :c] + u["row"], -jnp.inf)
        u["inter"] = u["bcol"] + u["m_st"]
        u["m_t"] = jnp.maximum(u["inter"], jnp.max(u["d"], axis=-1, keepdims=True))
        u["b_end"] = u["bcol"][c - 1:c, :]
        u["a"] = u["b_end"] - u["bcol"] + u["icol"]
        u["m_new"] = jnp.maximum(u["b_end"] + u["m_st"], jnp.max(u["a"], axis=0, keepdims=True))
    for u in units:
        u["w_inter"] = jnp.exp(u["inter"] - u["m_t"])
        u["p"] = jnp.exp(u["d"] - u["m_t"][:, :c]) * u["qk"]
        u["keep"] = jnp.exp(u["b_end"] + u["m_st"] - u["m_new"])
        u["ks"] = u["k"] * jnp.exp(u["a"] - u["m_new"])
    for u in units:
        u["pv"] = _dot(u["p"].astype(BF16), u["vb"])
        u["kv"] = _dot_tn(u["ks"].astype(BF16), u["vb"])
    for u in units:
        u["den"] = (u["w_inter"] * jnp.sum(u["q"] * u["nst"], axis=-1, keepdims=True)
                    + jnp.sum(u["p"], axis=-1, keepdims=True))
    for u in units:
        bb, h = u["bb"], u["h"]
        num = u["w_inter"] * u["qc"] + u["pv"]
        hc = num / jnp.maximum(jnp.abs(u["den"]), jnp.exp(-u["m_t"]))
        c_ref[bb, h] = u["cst"] * u["keep"] + u["kv"]
        n_ref[bb, h] = u["nst"] * u["keep"] + jnp.sum(u["ks"], axis=0, keepdims=True)
        m_ref[bb, h] = u["m_new"]
        hn = (_rms(hc, nrm_ref[:, h * ML_DV:(h + 1) * ML_DV])
              * _sigmoid(og_ref[bb, :, h * ML_DV:(h + 1) * ML_DV].astype(F32)))
        o_ref[bb, :, h * ML_DV:(h + 1) * ML_DV] = hn.astype(o_ref.dtype)


def _mlstm(mq, mk, mv, mo, gates, bias_row, norm_row, batch, seq):
    c = ML_CHUNK
    nc = seq // c
    nb = min(MLSTM_SEQS_PER_STEP, batch)
    row = lambda b, i: (b, i, 0)
    fix = lambda b, i: (0, 0)
    wide = ML_H * LANES
    r3 = lambda a: a.reshape(batch, seq, a.shape[-1])
    out = pl.pallas_call(
        _mlstm_kernel,
        grid=(batch // nb, nc),
        in_specs=[pl.BlockSpec((nb, c, wide), row), pl.BlockSpec((nb, c, wide), row), pl.BlockSpec((nb, c, wide), row),
                  pl.BlockSpec((nb, c, wide), row), pl.BlockSpec((nb, c, LANES), row),
                  pl.BlockSpec((1, LANES), fix), pl.BlockSpec((1, wide), fix)],
        out_specs=pl.BlockSpec((nb, c, wide), row),
        out_shape=jax.ShapeDtypeStruct((batch, seq, wide), BF16),
        scratch_shapes=[pltpu.VMEM((nb, ML_H, LANES, ML_DV), F32), pltpu.VMEM((nb, ML_H, 1, LANES), F32),
                        pltpu.VMEM((nb, ML_H, 1, LANES), F32)],
        compiler_params=_cparams(("arbitrary", "arbitrary")),
        name="mlstm",
    )(r3(mq), r3(mk), r3(mv), r3(mo), r3(gates), bias_row, norm_row)
    return out.reshape(batch * seq, wide)


def _swa_kernel(q_ref, kc_ref, kp_ref, vc_ref, vp_ref, sink_ref, o_ref):
    w = WINDOW
    n = pl.program_id(1)
    scale = SWA_D ** -0.5
    qi = _iota2((w, w), 0)
    kj = _iota2((w, w), 1)
    mask_c = kj <= qi
    mask_p = jnp.logical_and(kj > qi, n > 0)
    grp = SWA_H // SWA_KV
    neg = -1e30
    units = [(bb, h) for bb in range(q_ref.shape[0]) for h in range(SWA_H)]
    scores = []
    half_of_lane = _iota2((w, LANES), 1) // SWA_D
    for bb, h in units:
        g = h // grp
        pair = q_ref[bb, :, (h // 2) * LANES:(h // 2 + 1) * LANES]
        q = jnp.where(half_of_lane == h % 2, pair, jnp.zeros_like(pair))
        scores.append((_dot_nt(q, kc_ref[bb, :, g * LANES:(g + 1) * LANES]),
                       _dot_nt(q, kp_ref[bb, :, g * LANES:(g + 1) * LANES])))
    masked, tops, exps, dens, probs = [], [], [], [], {}
    for sc, sp in scores:
        masked.append((jnp.where(mask_c, sc * scale, neg), jnp.where(mask_p, sp * scale, neg)))
    for (bb, h), (s_c, s_p) in zip(units, masked):
        tops.append(jnp.maximum(jnp.max(jnp.maximum(s_c, s_p), axis=-1, keepdims=True), sink_ref[:, h:h + 1]))
    for (s_c, s_p), m in zip(masked, tops):
        exps.append((jnp.where(mask_c, jnp.exp(s_c - m), 0.0), jnp.where(mask_p, jnp.exp(s_p - m), 0.0)))
    ones_b = jnp.ones((w, LANES), BF16)
    for (bb, h), (p_c, p_p), m in zip(units, exps, tops):
        p_c, p_p = p_c.astype(BF16), p_p.astype(BF16)
        probs[bb, h] = (p_c, p_p)
        dens.append(_dot(p_c, ones_b) + _dot(p_p, ones_b) + jnp.exp(sink_ref[:, h:h + 1] - m))
    inv = {u: 1.0 / den for u, den in zip(units, dens)}
    for bb in range(q_ref.shape[0]):
        for pair in range(SWA_H // 2):
            acc = None
            for sub in range(2):
                h = 2 * pair + sub
                vcol = (2 * (h // grp) + sub) * LANES
                p_c, p_p = probs[bb, h]
                part = (_dot(p_c, vc_ref[bb, :, vcol:vcol + LANES]) + _dot(p_p, vp_ref[bb, :, vcol:vcol + LANES])) * inv[bb, h]
                acc = part if acc is None else acc + part
            o_ref[bb, :, pair * LANES:(pair + 1) * LANES] = acc.astype(o_ref.dtype)


def _swa(sq, sk, sv, sinks_row, batch, seq):
    w = WINDOW
    nb = seq // w
    ns = min(SWA_SEQS_PER_STEP, batch)
    wo = SWA_H * SWA_D
    cur = lambda b, n: (b, n, 0)
    prev = lambda b, n: (b, jnp.maximum(n - 1, 0), 0)
    r3 = lambda a: a.reshape(batch, seq, a.shape[-1])
    q3, k3, v3 = r3(sq), r3(sk), r3(sv)
    out = pl.pallas_call(
        _swa_kernel,
        grid=(batch // ns, nb),
        in_specs=[pl.BlockSpec((ns, w, sq.shape[1]), cur),
                  pl.BlockSpec((ns, w, sk.shape[1]), cur), pl.BlockSpec((ns, w, sk.shape[1]), prev),
                  pl.BlockSpec((ns, w, sv.shape[1]), cur), pl.BlockSpec((ns, w, sv.shape[1]), prev),
                  pl.BlockSpec((1, LANES), lambda b, n: (0, 0))],
        out_specs=pl.BlockSpec((ns, w, wo), cur),
        out_shape=jax.ShapeDtypeStruct((batch, seq, wo), BF16),
        compiler_params=_cparams(("arbitrary", "arbitrary")),
        name="swa",
    )(q3, k3, k3, v3, v3, sinks_row)
    return out.reshape(batch * seq, wo)


def _layer_norm(h, g, b):
    mu = jnp.mean(h, axis=-1, keepdims=True)
    d = h - mu
    var = jnp.mean(d * d, axis=-1, keepdims=True)
    return d * lax.rsqrt(var + LN_EPS) * g + b


def _outproj_kernel(x_ref, a1_ref, a2_ref, w_ref, g_ref, b_ref, o_ref, op_ref):
    k1 = a1_ref.shape[1]
    y = _dot(a1_ref[...].astype(BF16), w_ref[0:k1, :]) + _dot(a2_ref[...].astype(BF16), w_ref[k1:, :])
    h = _layer_norm(DN_ALPHA * x_ref[...] + y, g_ref[...], b_ref[...])
    o_ref[...] = h
    op_ref[...] = _pack_pairs(h)


def _outproj_ln(x, a1, a2, w, g, b, tm=1024):
    t, d = x.shape
    tm = min(tm, t)
    row = lambda i: (i, 0)
    fix = lambda i: (0, 0)
    return pl.pallas_call(
        _outproj_kernel,
        grid=(t // tm,),
        in_specs=[pl.BlockSpec((tm, d), row), pl.BlockSpec((tm, a1.shape[1]), row), pl.BlockSpec((tm, a2.shape[1]), row),
                  pl.BlockSpec(w.shape, fix), pl.BlockSpec((1, d), fix), pl.BlockSpec((1, d), fix)],
        out_specs=[pl.BlockSpec((tm, d), row), pl.BlockSpec((tm, d // 2), row)],
        out_shape=[jax.ShapeDtypeStruct((t, d), F32), jax.ShapeDtypeStruct((t, d // 2), jnp.uint32)],
        compiler_params=_cparams(("arbitrary",)),
        name="outproj_ln",
    )(x, a1, a2, w, g, b)


def _first_index(x, m, iota_f, sentinel):
    return jnp.min(jnp.where(x == m, iota_f, sentinel), axis=0, keepdims=True)


def _router_kernel(x_ref, wt_ref, bias_ref, idx_ref, gate_ref, rank_ref, cnt_ref, carry_ref):
    tm = x_ref.shape[0]
    e = N_EXPERTS
    gs = e // N_GROUPS
    ninf = -jnp.inf

    @pl.when(pl.program_id(0) == 0)
    def _():
        carry_ref[...] = jnp.zeros(carry_ref.shape, F32)

    logits = _dot3(wt_ref[...], x_ref[...], _dot_nt)
    scores = _sigmoid(logits)
    sel = scores + bias_ref[:, 0:1]

    sub_f = _iota2((gs, tm), 0).astype(F32)
    gscore = []
    for g in range(N_GROUPS):
        blk = sel[g * gs:(g + 1) * gs, :]
        m1 = jnp.max(blk, axis=0, keepdims=True)
        i1 = _first_index(blk, m1, sub_f, float(gs))
        m2 = jnp.max(jnp.where(sub_f == i1, ninf, blk), axis=0, keepdims=True)
        gscore.append(m1 + m2)
    gsc = jnp.concatenate(gscore, axis=0)
    grp_f = _iota2((N_GROUPS, tm), 0).astype(F32)
    gmask = jnp.zeros((N_GROUPS, tm), F32)
    for _ in range(TOPK_GROUPS):
        m = jnp.max(gsc, axis=0, keepdims=True)
        gi = _first_index(gsc, m, grp_f, float(N_GROUPS))
        hit = grp_f == gi
        gmask = jnp.where(hit, 1.0, gmask)
        gsc = jnp.where(hit, ninf, gsc)
    masked = jnp.concatenate(
        [jnp.where(gmask[g:g + 1, :] > 0.0, sel[g * gs:(g + 1) * gs, :], ninf) for g in range(N_GROUPS)], axis=0)

    exp_f = _iota2((e, tm), 0).astype(F32)
    chosen = jnp.zeros((e, tm), F32)
    idxs, gates = [], []
    for _ in range(TOP_K):
        m = jnp.max(masked, axis=0, keepdims=True)
        ei = _first_index(masked, m, exp_f, float(e))
        hit = exp_f == ei
        idxs.append(ei)
        gates.append(jnp.sum(jnp.where(hit, scores, 0.0), axis=0, keepdims=True))
        chosen = jnp.where(hit, 1.0, chosen)
        masked = jnp.where(hit, ninf, masked)
    gate = jnp.concatenate(gates, axis=0)
    gate = gate / jnp.sum(gate, axis=0, keepdims=True) * ROUTED_SCALE
    idx_f = jnp.concatenate(idxs, axis=0)

    upper = (_iota2((tm, tm), 0) < _iota2((tm, tm), 1)).astype(BF16)
    before = _dot(chosen.astype(BF16), upper) + carry_ref[...][:, 0:1]
    ranks = [jnp.sum(jnp.where(exp_f == idxs[k], before, 0.0), axis=0, keepdims=True) for k in range(TOP_K)]
    carry_ref[...] = carry_ref[...] + jnp.sum(chosen, axis=1, keepdims=True)

    idx_ref[...] = idx_f.astype(jnp.int32)
    gate_ref[...] = gate
    rank_ref[...] = jnp.concatenate(ranks, axis=0).astype(jnp.int32)
    cnt_ref[...] = carry_ref[...]


def _router(x, wt, bias_col, tm=512):
    t, d = x.shape
    col = lambda i: (0, i)
    fix = lambda i: (0, 0)
    return pl.pallas_call(
        _router_kernel,
        grid=(t // tm,),
        in_specs=[pl.BlockSpec((tm, d), lambda i: (i, 0)), pl.BlockSpec(wt.shape, fix), pl.BlockSpec((N_EXPERTS, LANES), fix)],
        out_specs=[pl.BlockSpec((TOP_K, tm), col), pl.BlockSpec((TOP_K, tm), col), pl.BlockSpec((TOP_K, tm), col),
                   pl.BlockSpec((N_EXPERTS, LANES), fix)],
        out_shape=[jax.ShapeDtypeStruct((TOP_K, t), jnp.int32), jax.ShapeDtypeStruct((TOP_K, t), F32),
                   jax.ShapeDtypeStruct((TOP_K, t), jnp.int32), jax.ShapeDtypeStruct((N_EXPERTS, LANES), F32)],
        scratch_shapes=[pltpu.VMEM((N_EXPERTS, LANES), F32)],
        compiler_params=_cparams(("arbitrary",)),
        name="router",
    )(x, wt, bias_col)


def _dest_kernel(idx_ref, rank_ref, start_ref, dest_ref):
    tm = idx_ref.shape[1]
    exp_i = _iota2((N_EXPERTS, tm), 0)
    start = start_ref[:, 0:1]
    rows = [jnp.sum(jnp.where(exp_i == idx_ref[s:s + 1, :], start, 0.0), axis=0, keepdims=True) for s in range(TOP_K)]
    dest_ref[...] = jnp.concatenate(rows, axis=0).astype(jnp.int32) + rank_ref[...]


def _dest_rows(idx, rank, start_col, tm=2048):
    t = idx.shape[1]
    tm = min(tm, t)
    col = lambda i: (0, i)
    return pl.pallas_call(
        _dest_kernel,
        grid=(t // tm,),
        in_specs=[pl.BlockSpec((TOP_K, tm), col), pl.BlockSpec((TOP_K, tm), col),
                  pl.BlockSpec((N_EXPERTS, LANES), lambda i: (0, 0))],
        out_specs=pl.BlockSpec((TOP_K, tm), col),
        out_shape=jax.ShapeDtypeStruct((TOP_K, t), jnp.int32),
        compiler_params=_cparams(("arbitrary",)),
        name="moe_dest",
    )(idx, rank, start_col)


def _pack_pairs(x):
    n = x.shape[1] // 2
    hi = lax.bitcast_convert_type(x[:, :n].astype(BF16).astype(F32), jnp.uint32)
    lo = lax.bitcast_convert_type(x[:, n:].astype(BF16).astype(F32), jnp.uint32)
    return hi | (lo >> 16)


def _unpack_pairs(w):
    hi = lax.bitcast_convert_type(w & jnp.uint32(0xFFFF0000), F32)
    lo = lax.bitcast_convert_type(w << 16, F32)
    return hi, lo


def _sc_scatter_rows(xp, dest, rows, chunk=LANES):
    t, width = xp.shape
    info = plsc.get_sparse_core_info()
    ncores, nsub = info.num_cores, info.num_subcores
    per_worker = t // (ncores * nsub)
    nchunk = per_worker // chunk
    mesh = plsc.VectorSubcoreMesh(core_axis_name="c", subcore_axis_name="s")

    @functools.partial(
        pl.kernel, mesh=mesh,
        out_type=jax.ShapeDtypeStruct((rows, width), xp.dtype),
        scratch_types=[pltpu.VMEM((TOP_K, chunk), jnp.int32), pltpu.VMEM((chunk, width), xp.dtype), pltpu.SemaphoreType.DMA],
    )
    def scatter(xp_hbm, dest_hbm, out_hbm, idx_v, rows_v, sem):
        base = (lax.axis_index("s") * ncores + lax.axis_index("c")) * per_worker

        @pl.loop(0, nchunk)
        def _(i):
            off = pl.multiple_of(base + i * chunk, chunk)
            pltpu.sync_copy(dest_hbm.at[:, pl.ds(off, chunk)], idx_v)
            pltpu.sync_copy(xp_hbm.at[pl.ds(off, chunk)], rows_v)
            copies = [pltpu.async_copy(rows_v, out_hbm.at[idx_v.at[s]], sem) for s in range(TOP_K)]
            for cp in copies:
                cp.wait()

    return scatter(xp, dest)


def _experts_kernel(be_ref, nu_ref, nv_ref, first_ref, slot_ref, nxt_ref, xs_ref, wg_hbm, wu_hbm, wd_hbm, ys_ref,
                    wgf_ref, wuf_ref, wdf_ref, wgb_ref, wub_ref, wdb_ref, sem, *, layer):
    i = pl.program_id(0)

    def fetch(e, s):
        return [pltpu.make_async_copy(wg_hbm.at[layer, e], wgf_ref.at[s], sem.at[s]),
                pltpu.make_async_copy(wu_hbm.at[layer, e], wuf_ref.at[s], sem.at[s]),
                pltpu.make_async_copy(wd_hbm.at[layer, e], wdf_ref.at[s], sem.at[s])]

    @pl.when(i == 0)
    def _():
        for cp in fetch(be_ref[0], 0):
            cp.start()

    @pl.when(jnp.logical_and(first_ref[i] == 1, i < nu_ref[0]))
    def _():
        s = slot_ref[i]
        for cp in fetch(be_ref[i], s):
            cp.wait()
        wgb_ref[...] = wgf_ref[s].astype(BF16)
        wub_ref[...] = wuf_ref[s].astype(BF16)
        wdb_ref[...] = wdf_ref[s].astype(BF16)

        @pl.when(nxt_ref[i] >= 0)
        def _():
            for cp in fetch(nxt_ref[i], 1 - s):
                cp.start()

    @pl.when(i < nu_ref[0])
    def _():
        sub = xs_ref.shape[0] // EXPERT_SUBBLOCKS
        acts = []
        for r in range(EXPERT_SUBBLOCKS):
            rows = pl.ds(r * sub, sub)
            live = (_iota2((sub, 1), 0) + r * sub) < nv_ref[i]
            xa, xb = _unpack_pairs(jnp.where(live, xs_ref[rows, :], jnp.uint32(0)))
            x = jnp.concatenate([xa.astype(BF16), xb.astype(BF16)], axis=1)
            acts.append((_dot(x, wgb_ref[...]), _dot(x, wub_ref[...])))
        outs = [_dot((_silu(gate) * up).astype(BF16), wdb_ref[...]) for gate, up in acts]
        for r, y in enumerate(outs):
            ys_ref[pl.ds(r * sub, sub), :] = _pack_pairs(y)


def _experts(block_e, n_used, n_valid, xs, wg, wu, wd, layer, block):
    rows, half = xs.shape
    d = 2 * half
    nb = rows // block
    pos = jnp.arange(nb, dtype=jnp.int32)
    first = jnp.concatenate([jnp.ones((1,), jnp.int32), (block_e[1:] != block_e[:-1]).astype(jnp.int32)])
    slot = (jnp.cumsum(first) - 1) % 2
    later = (pos[None, :] > pos[:, None]) & (block_e[None, :] != block_e[:, None]) & (pos[None, :] < n_used[0])
    nxt_pos = jnp.min(jnp.where(later, pos[None, :], nb), axis=1)
    nxt = jnp.where(nxt_pos < nb, block_e[jnp.minimum(nxt_pos, nb - 1)], -1)
    blk = lambda i, be, nu, *rest: (jnp.minimum(i, nu[0] - 1), 0)
    hbm = pl.BlockSpec(memory_space=pl.ANY)
    return pl.pallas_call(
        functools.partial(_experts_kernel, layer=layer),
        grid_spec=pltpu.PrefetchScalarGridSpec(
            num_scalar_prefetch=6,
            grid=(nb,),
            in_specs=[pl.BlockSpec((block, half), blk), hbm, hbm, hbm],
            out_specs=pl.BlockSpec((block, half), blk),
            scratch_shapes=[pltpu.VMEM((2, d, D_EXPERT), F32), pltpu.VMEM((2, d, D_EXPERT), F32),
                            pltpu.VMEM((2, D_EXPERT, d), F32),
                            pltpu.VMEM((d, D_EXPERT), BF16), pltpu.VMEM((d, D_EXPERT), BF16),
                            pltpu.VMEM((D_EXPERT, d), BF16), pltpu.SemaphoreType.DMA((2,))],
        ),
        out_shape=jax.ShapeDtypeStruct((rows, half), jnp.uint32),
        compiler_params=_cparams(("arbitrary",)),
        name="moe_experts",
    )(block_e, n_used, n_valid, first, slot.astype(jnp.int32), nxt.astype(jnp.int32), xs, wg, wu, wd)


def _sc_gather_rows(table, idx, chunk=SC_CHUNK):
    n = idx.shape[0]
    width = table.shape[1]
    info = plsc.get_sparse_core_info()
    ncores, nsub = info.num_cores, info.num_subcores
    per_worker = n // (ncores * nsub)
    nchunk = per_worker // chunk
    mesh = plsc.VectorSubcoreMesh(core_axis_name="c", subcore_axis_name="s")

    @functools.partial(
        pl.kernel, mesh=mesh,
        out_type=jax.ShapeDtypeStruct((n, width), table.dtype),
        scratch_types=[pltpu.VMEM((nchunk, chunk), jnp.int32), pltpu.VMEM((2, chunk, width), table.dtype),
                       pltpu.SemaphoreType.DMA((2,)), pltpu.SemaphoreType.DMA((2,))],
    )
    def gather(table_hbm, idx_hbm, out_hbm, idx_v, rows_v, gsem, wsem):
        wid = lax.axis_index("s") * ncores + lax.axis_index("c")
        base = wid * per_worker
        pltpu.sync_copy(idx_hbm.at[pl.ds(wid * nchunk, nchunk)], idx_v)

        def fetch(j, b):
            return pltpu.make_async_copy(table_hbm.at[idx_v.at[j]], rows_v.at[b], gsem.at[b])

        def flush(j, b):
            off = pl.multiple_of(base + j * chunk, chunk)
            return pltpu.make_async_copy(rows_v.at[b], out_hbm.at[pl.ds(off, chunk)], wsem.at[b])

        fetch(0, 0).start()

        @pl.loop(0, nchunk, step=2)
        def _(i):
            for b in range(2):
                j = i + b
                fetch(j, b).wait()

                @pl.when(j + 1 < nchunk)
                def _():
                    @pl.when(j >= 1)
                    def _():
                        flush(j - 1, 1 - b).wait()

                    fetch(j + 1, 1 - b).start()

                flush(j, b).start()

        flush(nchunk - 2, 0).wait()
        flush(nchunk - 1, 1).wait()

    return gather(table, idx.reshape(n // chunk, chunk))


def _shared_kernel(xp_ref, sg_ref, su_ref, sd_ref, o_ref):
    xa, xb = _unpack_pairs(xp_ref[...])
    x = jnp.concatenate([xa.astype(BF16), xb.astype(BF16)], axis=1)
    hs = _silu(_dot(x, sg_ref[...])) * _dot(x, su_ref[...])
    o_ref[...] = _pack_pairs(_dot(hs.astype(BF16), sd_ref[...]))


def _shared_expert(xp, sg, su, sd, tm=512):
    t, half = xp.shape
    row = lambda i: (i, 0)
    fix = lambda i: (0, 0)
    return pl.pallas_call(
        _shared_kernel,
        grid=(t // tm,),
        in_specs=[pl.BlockSpec((tm, half), row), pl.BlockSpec(sg.shape, fix), pl.BlockSpec(su.shape, fix),
                  pl.BlockSpec(sd.shape, fix)],
        out_specs=pl.BlockSpec((tm, half), row),
        out_shape=jax.ShapeDtypeStruct((t, half), jnp.uint32),
        compiler_params=_cparams(("arbitrary",)),
        name="moe_shared",
    )(xp, sg, su, sd)


def _combine_kernel(x_ref, gate_ref, rows_ref, sh_ref, g_ref, b_ref, o_ref):
    gate = gate_ref[...]
    ya, yb = _unpack_pairs(sh_ref[...])
    for s in range(TOP_K):
        a, b = _unpack_pairs(rows_ref[s])
        ya = ya + gate[:, s:s + 1] * a
        yb = yb + gate[:, s:s + 1] * b
    ff = jnp.concatenate([ya, yb], axis=1)
    o_ref[...] = _layer_norm(DN_ALPHA * x_ref[...] + ff, g_ref[...], b_ref[...])


def _combine(x, gate_t, rows, shared, g, b, tm=512):
    t, d = x.shape
    row = lambda i: (i, 0)
    fix = lambda i: (0, 0)
    return pl.pallas_call(
        _combine_kernel,
        grid=(t // tm,),
        in_specs=[pl.BlockSpec((tm, d), row), pl.BlockSpec((tm, TOP_K), row),
                  pl.BlockSpec((TOP_K, tm, d // 2), lambda i: (0, i, 0)), pl.BlockSpec((tm, d // 2), row),
                  pl.BlockSpec((1, d), fix), pl.BlockSpec((1, d), fix)],
        out_specs=pl.BlockSpec((tm, d), row),
        out_shape=jax.ShapeDtypeStruct((t, d), F32),
        compiler_params=_cparams(("arbitrary",)),
        name="moe_combine",
    )(x, gate_t, rows, shared, g, b)


def _take_cols(w, idx):
    idx = np.asarray(idx)
    runs, start = [], 0
    for pos in range(1, len(idx) + 1):
        run_ends = pos == len(idx) or (idx[pos] != idx[pos - 1] + 1 if idx[pos - 1] >= 0 else idx[pos] >= 0)
        if run_ends:
            runs.append((start, int(idx[start]), pos - start))
            start = pos

    def body(w_ref, o_ref):
        for dst, src, width in runs:
            if src < 0:
                o_ref[:, dst:dst + width] = jnp.zeros((o_ref.shape[0], width), o_ref.dtype)
            else:
                o_ref[:, dst:dst + width] = w_ref[:, src:src + width].astype(o_ref.dtype)

    rows = w.shape[0]
    tr = min(rows, 256)
    return pl.pallas_call(
        body,
        grid=(rows // tr,),
        in_specs=[pl.BlockSpec((tr, w.shape[1]), lambda i: (i, 0))],
        out_specs=pl.BlockSpec((tr, len(idx)), lambda i: (i, 0)),
        out_shape=jax.ShapeDtypeStruct((rows, len(idx)), BF16),
        compiler_params=_cparams(("arbitrary",)),
        name="weight_cols",
    )(w)


def _pad_lane_row(v, first_lane, width=LANES):
    out = jnp.zeros((1, width), F32)
    return lax.dynamic_update_slice(out, v.reshape(1, -1).astype(F32), (0, first_lane))


def _even_in_cols():
    z = lambda n: -np.ones(n, int)
    kr0 = Q_LORA + KV_LORA
    half = MLA_ROPE // 2
    cols = [np.arange(0, Q_LORA), np.arange(Q_LORA, Q_LORA + KV_LORA),
            z(64), np.arange(kr0, kr0 + MLA_ROPE), z(32),
            z(64), np.arange(kr0 + half, kr0 + MLA_ROPE), np.arange(kr0, kr0 + half), z(32)]
    g0 = kr0 + MLA_ROPE
    nqk = GDN_H * GDN_DK
    cols.append(np.arange(g0, g0 + 3 * nqk))
    zoff = g0 + 3 * nqk + 2 * GDN_H
    cols.append(np.arange(zoff, zoff + GDN_H * GDN_DV))
    cols += [np.arange(g0 + 3 * nqk, g0 + 3 * nqk + 2 * GDN_H), z(LANES - 2 * GDN_H)]
    return np.concatenate(cols)


EV_WIDTHS = (Q_LORA + KV_LORA + 2 * LANES, 3 * GDN_H * GDN_DK, GDN_H * GDN_DV, LANES)


def _mla_q_cols():
    per = MLA_NOPE + MLA_ROPE
    half = MLA_ROPE // 2
    main, sw = [], []
    for h in range(MLA_H):
        b = h * per
        main += [np.arange(b, b + per), -np.ones(LANES - per, int)]
        sw += [-np.ones(MLA_NOPE, int), np.arange(b + MLA_NOPE + half, b + per), np.arange(b + MLA_NOPE, b + MLA_NOPE + half),
               -np.ones(LANES - per, int)]
    return np.concatenate(main + sw)


def _mla_kv_cols():
    per = MLA_NOPE + MLA_V
    kc, vc = [], []
    for h in range(MLA_H):
        b = h * per
        kc += [np.arange(b, b + MLA_NOPE), -np.ones(LANES - MLA_NOPE, int)]
        vv = np.arange(b + MLA_NOPE, b + per)
        pad = -np.ones(LANES - MLA_V, int)
        vc += [vv, pad] if h % 2 == 0 else [pad, vv]
    return np.concatenate(kc + vc)


def _odd_in_cols():
    z = lambda n: -np.ones(n, int)
    o = 0
    cols = []
    mq0, mk0 = 0, ML_H * ML_DK
    for base in (mq0, mk0):
        for h in range(ML_H):
            cols += [np.arange(base + h * ML_DK, base + (h + 1) * ML_DK), z(LANES - ML_DK)]
    mv0 = 2 * ML_H * ML_DK
    cols.append(np.arange(mv0, mv0 + ML_H * ML_DV))
    mi0 = mv0 + ML_H * ML_DV
    mo0 = mi0 + 2 * ML_H
    cols.append(np.arange(mo0, mo0 + ML_H * ML_DV))
    cols += [np.arange(mi0, mi0 + 2 * ML_H), z(LANES - 2 * ML_H)]
    sq0 = mo0 + ML_H * ML_DV
    sk0 = sq0 + SWA_H * SWA_D
    sv0 = sk0 + SWA_KV * SWA_D
    half = SWA_D // 2

    def heads(base, n, swapped, copies):
        out = []
        for h in range(n):
            b = base + h * SWA_D
            one = [np.arange(b + half, b + SWA_D), np.arange(b, b + half)] if swapped else [np.arange(b, b + SWA_D)]
            out += one * copies
        return out

    cols += (heads(sq0, SWA_H, False, 1) + heads(sq0, SWA_H, True, 1)
             + heads(sk0, SWA_KV, False, 2) + heads(sk0, SWA_KV, True, 2))
    for g in range(SWA_KV):
        vv = np.arange(sv0 + g * SWA_D, sv0 + (g + 1) * SWA_D)
        cols += [vv, z(LANES - SWA_D), z(LANES - SWA_D), vv]
    return np.concatenate(cols)


def _even_weights(w_in, w_qb, w_kvb):
    return (_take_cols(w_in, _even_in_cols()), _take_cols(w_qb, _mla_q_cols()), _take_cols(w_kvb, _mla_kv_cols()))


def _even_mixer(x, tabs, weights, q_norm, kv_norm, conv_w, a_log, dt_bias, o_norm, batch, seq):
    ctab, stab = tabs
    w, wq2, wkv2 = weights
    mla_in, act, z, gates = _proj_even(x, w, conv_w, seq)
    q, k, v = _mla_prep(mla_in, ctab, stab, q_norm.reshape(1, -1), kv_norm.reshape(1, -1), wq2, wkv2)
    o_a = _mla_attn(q, k, v, batch, seq)
    o_b = _gdn(act, gates, z, _pad_lane_row(a_log, GDN_H), _pad_lane_row(dt_bias, GDN_H),
               o_norm.reshape(1, -1), batch, seq)
    return o_a, o_b


def _odd_mixer(x, tabs, w, b_i, b_f, ml_norm, sinks, batch, seq):
    ctab, stab = tabs
    mq, mk, mv, mo, mg, sq, sk, sv = _proj_odd(x, w, ctab, stab)
    bias_row = _pad_lane_row(jnp.concatenate([b_i, b_f]), 0)
    o_c = _mlstm(mq, mk, mv, mo, mg, bias_row, ml_norm.reshape(1, -1), batch, seq)
    o_d = _swa(sq, sk, sv, _pad_lane_row(sinks, 0), batch, seq)
    return o_c, o_d


def _moe(x, xp, router_w, router_b, w_gate, w_up, w_down, layer, s_gate, s_up, s_down, ln_g, ln_b):
    t, d = x.shape
    bias_col = jnp.broadcast_to(router_b.reshape(-1, 1).astype(F32), (N_EXPERTS, LANES))
    idx, gate, rank, cnt = _router(x, router_w.T, bias_col)
    counts = cnt[:, 0].astype(jnp.int32)
    block = int(min(max(pl.next_power_of_2(t * TOP_K // N_EXPERTS) // 2, EXPERT_BLOCK_MIN), EXPERT_BLOCK_MAX))
    padded = (counts + block - 1) // block * block
    pad_end = jnp.cumsum(padded)
    pad_start = pad_end - padded
    start_col = jnp.broadcast_to(pad_start.astype(F32).reshape(-1, 1), (N_EXPERTS, LANES))
    dest = _dest_rows(idx, rank, start_col)
    n_blocks = t * TOP_K // block + N_EXPERTS
    rows = n_blocks * block
    block_row = jnp.arange(n_blocks, dtype=jnp.int32) * block
    block_e = jnp.minimum(jnp.sum((pad_end[None, :] <= block_row[:, None]).astype(jnp.int32), axis=1), N_EXPERTS - 1)
    n_used = (pad_end[-1:] // block).astype(jnp.int32)
    live_end = jnp.sum(jnp.where(block_e[:, None] == jnp.arange(N_EXPERTS, dtype=jnp.int32)[None, :],
                                 (pad_start + counts)[None, :], 0), axis=1)
    n_valid = jnp.clip(live_end - block_row, 0, block).astype(jnp.int32)
    xs = _sc_scatter_rows(xp, dest, rows)
    ys = _experts(block_e, n_used, n_valid, xs, w_gate, w_up, w_down, layer, block)
    picked = _sc_gather_rows(ys, dest.reshape(-1)).reshape(TOP_K, t, d // 2)
    shared = _shared_expert(xp, s_gate.astype(BF16), s_up.astype(BF16), s_down.astype(BF16))
    return _combine(x, gate.T, picked, shared, ln_g.reshape(1, -1), ln_b.reshape(1, -1))


def kernel(x, positions, ev_w_in, mla_q_norm, mla_w_qb, mla_kv_norm, mla_w_kvb, gdn_conv, gdn_a_log, gdn_dt_bias, gdn_norm, ev_w_out, od_w_in, mlstm_b_i, mlstm_b_f, mlstm_norm, swa_sinks, od_w_out, ln1_g, ln1_b, router_w, router_b, moe_w_gate, moe_w_up, moe_w_down, shared_w_gate, shared_w_up, shared_w_down, ln2_g, ln2_b):
    batch, seq, d = x.shape
    streams = STREAMS if batch % STREAMS == 0 else 1
    sb = batch // streams
    ts = sb * seq
    hs, tabs_m, tabs_s = [], [], []
    for s in range(streams):
        pos = positions[s * sb:(s + 1) * sb].reshape(ts, 1).astype(F32)
        tm_, ts_ = _rope_tables(pos)
        tabs_m.append(tm_)
        tabs_s.append(ts_)
        hs.append(x[s * sb:(s + 1) * sb].reshape(ts, d))
    for layer in range(DEPTH):
        j = layer // 2
        if layer % 2 == 0:
            weights = _even_weights(ev_w_in[j], mla_w_qb[j], mla_w_kvb[j])
            w_out = ev_w_out[j].astype(BF16)
        else:
            weights = _take_cols(od_w_in[j], _odd_in_cols())
            w_out = od_w_out[j].astype(BF16)
        for s in range(streams):
            h = hs[s]
            if layer % 2 == 0:
                a1, a2 = _even_mixer(h, tabs_m[s], weights, mla_q_norm[j], mla_kv_norm[j], gdn_conv[j], gdn_a_log[j],
                                     gdn_dt_bias[j], gdn_norm[j], sb, seq)
            else:
                a1, a2 = _odd_mixer(h, tabs_s[s], weights, mlstm_b_i[j], mlstm_b_f[j], mlstm_norm[j], swa_sinks[j], sb, seq)
            h, hp = _outproj_ln(h, a1, a2, w_out, ln1_g[layer].reshape(1, -1), ln1_b[layer].reshape(1, -1))
            hs[s] = _moe(h, hp, router_w[layer], router_b[layer], moe_w_gate, moe_w_up, moe_w_down, layer,
                         shared_w_gate[layer], shared_w_up[layer], shared_w_down[layer], ln2_g[layer], ln2_b[layer])
    return jnp.concatenate([h.reshape(sb, seq, d) for h in hs], axis=0)
```

```python
import functools
import math

import numpy as np
import jax
import jax.numpy as jnp
from jax import lax
from jax.experimental import pallas as pl
from jax.experimental.pallas import tpu as pltpu
from jax.experimental.pallas import tpu_sc as plsc

F32 = jnp.float32
BF16 = jnp.bfloat16
HI = lax.Precision.HIGHEST

D_MODEL = 1024
DEPTH = 4
ROPE_THETA = 10000.0
EPS = 1e-6
LN_EPS = 1e-5
MLA_H, MLA_NOPE, MLA_ROPE, MLA_V = 8, 64, 32, 64
Q_LORA, KV_LORA = 256, 128
GDN_H, GDN_DK, GDN_DV, CONV_W, GDN_CHUNK = 4, 128, 128, 4, 64
ML_H, ML_DK, ML_DV, ML_CHUNK = 4, 64, 128, 64
SWA_H, SWA_KV, SWA_D, WINDOW = 8, 2, 64, 128
N_EXPERTS, N_GROUPS, TOPK_GROUPS, TOP_K = 64, 8, 4, 8
D_EXPERT, D_SHARED = 256, 256
ROUTED_SCALE = 2.5
DN_ALPHA = (2 * DEPTH) ** 0.25

LANES = 128
V7X_VMEM_BYTES = 64 * 1024 * 1024
VMEM_LIMIT = 48 * 1024 * 1024

EXPERT_BLOCK_MIN = 256
EXPERT_BLOCK_MAX = 1024
STREAMS = 1
EXPERT_SUBBLOCKS = 4
SWA_SEQS_PER_STEP = 8
MLSTM_SEQS_PER_STEP = 2
GDN_SEQS_PER_STEP = 8
SC_CHUNK = 64


def _cparams(sem, vmem=VMEM_LIMIT):
    return pltpu.CompilerParams(dimension_semantics=sem, vmem_limit_bytes=vmem)


def _dot(a, b, precision=None):
    return jnp.dot(a, b, preferred_element_type=F32, precision=precision)


def _dot_nt(a, b, precision=None):
    return lax.dot_general(a, b, (((1,), (1,)), ((), ())), preferred_element_type=F32, precision=precision)


def _dot_tn(a, b, precision=None):
    return lax.dot_general(a, b, (((0,), (0,)), ((), ())), preferred_element_type=F32, precision=precision)


def _split2(a):
    hi = a.astype(BF16)
    lo = (a - hi.astype(F32)).astype(BF16)
    return hi, lo


def _split3(a):
    p1 = a.astype(BF16)
    r = a - p1.astype(F32)
    p2 = r.astype(BF16)
    p3 = (r - p2.astype(F32)).astype(BF16)
    return p1, p2, p3


def _dot3(a, b, dot=_dot):
    ah, al = _split2(a)
    bh, bl = _split2(b)
    return dot(ah, bh) + (dot(ah, bl) + dot(al, bh))


def _dot_sel(sel, b, dot=_dot):
    sel = sel.astype(BF16)
    p1, p2, p3 = _split3(b)
    return dot(sel, p1) + (dot(sel, p2) + dot(sel, p3))


def _sigmoid(x):
    return 1.0 / (1.0 + jnp.exp(-x))


def _softplus(x):
    return jnp.maximum(x, 0.0) + jnp.log(1.0 + jnp.exp(-jnp.abs(x)))


def _silu(x):
    return x * _sigmoid(x)


def _lane_bcast(x, c):
    return jnp.broadcast_to(x[:, c:c + 1], x.shape)


def _iota2(shape, dim):
    return lax.broadcasted_iota(jnp.int32, shape, dim)


def _rope_kernel(pos_ref, rows_ref, sel_ref, cm_ref, sm_ref, cs_ref, ss_ref):
    ang = pos_ref[...] * rows_ref[0:1, :]
    cos_parts = _split3(jnp.cos(ang))
    sin_parts = _split3(jnp.sin(ang))

    def place(parts, k):
        return _dot(parts[0], sel_ref[k]) + (_dot(parts[1], sel_ref[k]) + _dot(parts[2], sel_ref[k]))

    cm_ref[...] = place(cos_parts, 0) + rows_ref[1:2, :]
    sm_ref[...] = place(sin_parts, 1)
    cs_ref[...] = place(cos_parts, 2)
    ss_ref[...] = place(sin_parts, 3)


def _rope_consts():
    hm, hs = MLA_ROPE // 2, SWA_D // 2
    rows = np.zeros((8, LANES), np.float32)
    rows[0, :hm] = ROPE_THETA ** (-(np.arange(0, MLA_ROPE, 2, dtype=np.float32) / MLA_ROPE))
    rows[0, hm:hm + hs] = ROPE_THETA ** (-(np.arange(0, SWA_D, 2, dtype=np.float32) / SWA_D))
    rows[1, :MLA_NOPE] = 1.0
    sel = np.zeros((4, LANES, LANES), np.float32)
    for j in range(hm):
        sel[0, j, MLA_NOPE + j] = sel[0, j, MLA_NOPE + hm + j] = 1.0
        sel[1, j, MLA_NOPE + j] = -1.0
        sel[1, j, MLA_NOPE + hm + j] = 1.0
    for h in range(LANES // SWA_D):
        for j in range(hs):
            sel[2, hm + j, h * SWA_D + j] = sel[2, hm + j, h * SWA_D + hs + j] = 1.0
            sel[3, hm + j, h * SWA_D + j] = -1.0
            sel[3, hm + j, h * SWA_D + hs + j] = 1.0
    return jnp.asarray(rows), jnp.asarray(sel, BF16)


def _rope_tables(pos, tm=512):
    t = pos.shape[0]
    tm = min(tm, t)
    rows, sel = _rope_consts()
    cm, sm, cs, ss = pl.pallas_call(
        _rope_kernel,
        grid=(t // tm,),
        in_specs=[pl.BlockSpec((tm, 1), lambda i: (i, 0)), pl.BlockSpec((8, LANES), lambda i: (0, 0)),
                  pl.BlockSpec((4, LANES, LANES), lambda i: (0, 0, 0))],
        out_specs=[pl.BlockSpec((tm, LANES), lambda i: (i, 0))] * 4,
        out_shape=[jax.ShapeDtypeStruct((t, LANES), F32)] * 4,
        compiler_params=_cparams(("arbitrary",)),
        name="rope_tables",
    )(pos, rows, sel)
    return (cm, sm), (cs, ss)


def _proj_kernel(x_ref, w_ref, *out_refs, offsets):
    xb = x_ref[...].astype(BF16)
    for o_ref, (a, b) in zip(out_refs, offsets):
        o_ref[...] = _dot(xb, w_ref[:, a:b]).astype(o_ref.dtype)


def _proj(x, w, widths, dtypes, tm=512):
    t, k = x.shape
    offs = np.concatenate([[0], np.cumsum(widths)]).tolist()
    offsets = tuple((offs[i], offs[i + 1]) for i in range(len(widths)))
    return pl.pallas_call(
        functools.partial(_proj_kernel, offsets=offsets),
        grid=(t // tm,),
        in_specs=[pl.BlockSpec((tm, k), lambda i: (i, 0)), pl.BlockSpec(w.shape, lambda i: (0, 0))],
        out_specs=[pl.BlockSpec((tm, n), lambda i: (i, 0)) for n in widths],
        out_shape=[jax.ShapeDtypeStruct((t, n), dt) for n, dt in zip(widths, dtypes)],
        compiler_params=_cparams(("arbitrary",)),
        name="in_proj",
    )(x, w)


def _proj_even_kernel(x_ref, w_ref, cw_ref, mla_ref, act_ref, z_ref, g_ref, ext_ref, *, tiles_per_seq):
    tm = x_ref.shape[0]
    o = np.concatenate([[0], np.cumsum(EV_WIDTHS)]).tolist()
    @pl.when(pl.program_id(0) % tiles_per_seq == 0)
    def _():
        ext_ref[0:8, :] = jnp.zeros((8, ext_ref.shape[1]), F32)

    xb = x_ref[...].astype(BF16)
    nchunk = 3
    cw = EV_WIDTHS[1] // nchunk

    def project(ci):
        ext_ref[8:8 + tm, ci * cw:(ci + 1) * cw] = _dot(xb, w_ref[:, o[1] + ci * cw:o[1] + (ci + 1) * cw])

    project(0)
    for ci in range(nchunk):
        if ci + 1 < nchunk:
            project(ci + 1)
        else:
            mla_ref[...] = _dot(xb, w_ref[:, o[0]:o[1]])
            z_ref[...] = _dot(xb, w_ref[:, o[2]:o[3]]).astype(z_ref.dtype)
            g_ref[...] = _dot(xb, w_ref[:, o[3]:o[4]])
        cols = slice(ci * cw, (ci + 1) * cw)
        conv = cw_ref[0:1, cols] * ext_ref[5:5 + tm, cols]
        for j in range(1, CONV_W):
            conv = conv + cw_ref[j:j + 1, cols] * ext_ref[5 + j:5 + j + tm, cols]
        act_ref[:, cols] = _silu(conv).astype(act_ref.dtype)
    ext_ref[0:8, :] = ext_ref[tm:tm + 8, :]


def _proj_even(x, w, conv_w, seq, tm=512):
    t, k = x.shape
    tm = min(tm, seq)
    row = lambda i: (i, 0)
    fix = lambda i: (0, 0)
    return pl.pallas_call(
        functools.partial(_proj_even_kernel, tiles_per_seq=seq // tm),
        grid=(t // tm,),
        in_specs=[pl.BlockSpec((tm, k), row), pl.BlockSpec(w.shape, fix), pl.BlockSpec(conv_w.shape, fix)],
        out_specs=[pl.BlockSpec((tm, n), row) for n in EV_WIDTHS],
        out_shape=[jax.ShapeDtypeStruct((t, n), F32) for n in EV_WIDTHS],
        scratch_shapes=[pltpu.VMEM((tm + 8, EV_WIDTHS[1]), F32)],
        compiler_params=_cparams(("arbitrary",)),
        name="in_proj",
    )(x, w, conv_w)


OD_SEG = dict(mq=(0, 512), mk=(512, 1024), mv=(1024, 1536), mo=(1536, 2048), gates=(2048, 2176),
              sq=(2176, 2688), sk=(2688, 2944), sv=(2944, 3456))
OD_COLS = 3456


def _proj_odd_kernel(x_ref, w_ref, c_ref, s_ref, mq_ref, mk_ref, mv_ref, mo_ref, mg_ref, sq_ref, sk_ref, sv_ref):
    xb = x_ref[...].astype(BF16)

    def seg(name):
        a, b = OD_SEG[name]
        return _dot(xb, w_ref[:, a:b])

    mq_ref[...] = seg("mq").astype(mq_ref.dtype)
    mk_ref[...] = seg("mk").astype(mk_ref.dtype)
    mv_ref[...] = seg("mv").astype(mv_ref.dtype)
    mo_ref[...] = seg("mo").astype(mo_ref.dtype)
    mg_ref[...] = seg("gates")
    c = c_ref[...]
    s = s_ref[...]
    def swap_halves(t):
        half = SWA_D // 2
        first_half = (_iota2(t.shape, 1) % SWA_D) < half
        return jnp.where(first_half, pltpu.roll(t, t.shape[1] - half, 1), pltpu.roll(t, half, 1))

    c8 = jnp.concatenate([c] * (SWA_H // 2), axis=1)
    s8 = jnp.concatenate([s] * (SWA_H // 2), axis=1)
    q = seg("sq")
    sq_ref[...] = (q * c8 + swap_halves(q) * s8).astype(sq_ref.dtype)
    c2 = jnp.concatenate([c] * SWA_KV, axis=1)
    s2 = jnp.concatenate([s] * SWA_KV, axis=1)
    k = seg("sk")
    sk_ref[...] = (k * c2 + swap_halves(k) * s2).astype(sk_ref.dtype)
    sv_ref[...] = seg("sv").astype(sv_ref.dtype)


def _proj_odd(x, w, ctab, stab, tm=512):
    t, k = x.shape
    widths = (512, 512, 512, 512, 128, SWA_H * SWA_D, SWA_KV * LANES, 2 * SWA_KV * LANES)
    dtypes = (F32, F32, F32, F32, F32, BF16, BF16, BF16)
    return pl.pallas_call(
        _proj_odd_kernel,
        grid=(t // tm,),
        in_specs=[pl.BlockSpec((tm, k), lambda i: (i, 0)), pl.BlockSpec(w.shape, lambda i: (0, 0)),
                  pl.BlockSpec((tm, LANES), lambda i: (i, 0)), pl.BlockSpec((tm, LANES), lambda i: (i, 0))],
        out_specs=[pl.BlockSpec((tm, n), lambda i: (i, 0)) for n in widths],
        out_shape=[jax.ShapeDtypeStruct((t, n), dt) for n, dt in zip(widths, dtypes)],
        compiler_params=_cparams(("arbitrary",)),
        name="in_proj_odd",
    )(x, w, ctab, stab)


def _rms(x, g):
    return x * lax.rsqrt(jnp.mean(x * x, axis=-1, keepdims=True) + EPS) * g


def _mla_prep_kernel(in_ref, c_ref, s_ref, qn_ref, kvn_ref, wq_ref, wkv_ref, q_ref, k_ref, v_ref):
    hw = MLA_H * LANES
    c = c_ref[...]
    s = s_ref[...]
    c8 = jnp.concatenate([c] * MLA_H, axis=1)
    s8 = jnp.concatenate([s] * MLA_H, axis=1)
    cqn = _rms(in_ref[:, 0:Q_LORA], qn_ref[...]).astype(BF16)
    qq = _dot(cqn, wq_ref[...])
    scale = (MLA_NOPE + MLA_ROPE) ** -0.5
    q_ref[...] = ((qq[:, :hw] * c8 + qq[:, hw:] * s8) * scale).astype(q_ref.dtype)
    ckvn = _rms(in_ref[:, Q_LORA:Q_LORA + KV_LORA], kvn_ref[...]).astype(BF16)
    kv = _dot(ckvn, wkv_ref[...])
    o = Q_LORA + KV_LORA
    krr = in_ref[:, o:o + LANES] * c + in_ref[:, o + LANES:o + 2 * LANES] * s
    k_ref[...] = (kv[:, :hw] + jnp.concatenate([krr] * MLA_H, axis=1)).astype(k_ref.dtype)
    v_ref[...] = kv[:, hw:].astype(v_ref.dtype)


def _mla_prep(mla_in, ctab, stab, qn, kvn, wq2, wkv2, tm=1024):
    t = mla_in.shape[0]
    tm = min(tm, t)
    hw = MLA_H * LANES
    row = lambda i: (i, 0)
    fix = lambda i: (0, 0)
    return pl.pallas_call(
        _mla_prep_kernel,
        grid=(t // tm,),
        in_specs=[pl.BlockSpec((tm, mla_in.shape[1]), row), pl.BlockSpec((tm, LANES), row), pl.BlockSpec((tm, LANES), row),
                  pl.BlockSpec(qn.shape, fix), pl.BlockSpec(kvn.shape, fix),
                  pl.BlockSpec(wq2.shape, fix), pl.BlockSpec(wkv2.shape, fix)],
        out_specs=[pl.BlockSpec((tm, hw), row)] * 3,
        out_shape=[jax.ShapeDtypeStruct((t, hw), BF16)] * 3,
        compiler_params=_cparams(("arbitrary",)),
        name="mla_prep",
    )(mla_in, ctab, stab, qn, kvn, wq2, wkv2)


def _mla_attn_kernel(q_ref, k_ref, v_ref, o_ref, *, tq):
    i = pl.program_id(2)
    neg = -1e30
    lane = _iota2((tq, LANES), 1)
    ones_lane = (MLA_V, 0)

    def chunk(j, carry, masked):
        start = pl.multiple_of(j * tq, tq)
        out = []
        for hh in range(2):
            m, acc = carry[hh]
            q = q_ref[:, hh * LANES:(hh + 1) * LANES]
            kc = k_ref[pl.ds(start, tq), hh * LANES:(hh + 1) * LANES]
            vc = v_ref[pl.ds(start, tq), hh * LANES:(hh + 1) * LANES]
            vc = jnp.where(lane == ones_lane[hh], jnp.ones_like(vc), vc)
            s = _dot_nt(q, kc)
            if masked:
                s = jnp.where(_iota2(s.shape, 0) >= _iota2(s.shape, 1), s, neg)
            m_new = jnp.maximum(m, jnp.max(s, axis=-1, keepdims=True))
            alpha = jnp.exp(m - m_new)
            p = jnp.exp(s - m_new)
            acc = alpha * acc + _dot(p.astype(BF16), vc)
            out.append((m_new, acc))
        return tuple(out)

    one = (jnp.full((tq, 1), neg, F32), jnp.zeros((tq, LANES), F32))
    carry = lax.fori_loop(0, i, lambda j, c: chunk(j, c, False), (one, one))
    (_, acc0), (_, acc1) = chunk(i, carry, True)
    o0 = acc0 / _lane_bcast(acc0, ones_lane[0])
    o1 = acc1 / _lane_bcast(acc1, ones_lane[1])
    o_ref[...] = jnp.where(lane < MLA_V, o0, o1).astype(o_ref.dtype)


def _mla_attn(q, k, v, batch, seq, tq=512):
    tq = min(tq, seq)
    nq = seq // tq
    pairs = MLA_H // 2
    return pl.pallas_call(
        functools.partial(_mla_attn_kernel, tq=tq),
        grid=(batch, pairs, nq),
        in_specs=[pl.BlockSpec((tq, 2 * LANES), lambda b, p, i: (b * nq + i, p)),
                  pl.BlockSpec((seq, 2 * LANES), lambda b, p, i: (b, p)),
                  pl.BlockSpec((seq, 2 * LANES), lambda b, p, i: (b, p))],
        out_specs=pl.BlockSpec((tq, LANES), lambda b, p, i: (b * nq + i, p)),
        out_shape=jax.ShapeDtypeStruct((batch * seq, pairs * LANES), BF16),
        compiler_params=_cparams(("arbitrary", "arbitrary", "arbitrary")),
        name="mla_attn",
    )(q, k, v)


def _unit_lower_inverse_many(ns):
    c = ns[0].shape[0]
    eye = (_iota2((c, c), 0) == _iota2((c, c), 1)).astype(F32)
    xs = [-n for n in ns]
    ps = [eye + x for x in xs]
    xb = [x.astype(BF16) for x in xs]
    for _ in range(int(math.log2(c)) - 1):
        xs = [_dot(b, b) for b in xb]
        xb = [x.astype(BF16) for x in xs]
        ps = [p + _dot(p.astype(BF16), b) for p, b in zip(ps, xb)]
    return ps


def _gdn_kernel(act_ref, g_ref, z_ref, al_ref, dt_ref, on_ref, o_ref, st_ref):
    c = GDN_CHUNK
    hd = GDN_DK
    nqk = GDN_H * GDN_DK

    @pl.when(pl.program_id(1) == 0)
    def _():
        st_ref[...] = jnp.zeros(st_ref.shape, F32)

    tri = (_iota2((c, c), 0) >= _iota2((c, c), 1)).astype(F32)
    row_ge = _iota2((c, c), 0) >= _iota2((c, c), 1)
    row_gt = _iota2((c, c), 0) > _iota2((c, c), 1)
    lane = _iota2((c, LANES), 1)

    seqs = []
    for bb in range(act_ref.shape[0]):
        gates = g_ref[bb]
        g_all = -jnp.exp(al_ref[...]) * _softplus(gates + dt_ref[...])
        gc_all = _dot_sel(tri, g_all)
        seqs.append(dict(beta_all=_sigmoid(gates), gc_all=gc_all, gc_parts=_split3(gc_all)))
    units = []
    for bb, sq in enumerate(seqs):
        for h in range(GDN_H):
            q = act_ref[bb, :, h * hd:(h + 1) * hd].astype(F32)
            k = act_ref[bb, :, nqk + h * hd:nqk + (h + 1) * hd].astype(F32)
            v = act_ref[bb, :, 2 * nqk + h * GDN_DV:2 * nqk + (h + 1) * GDN_DV].astype(F32)
            q = q * lax.rsqrt(jnp.sum(q * q, axis=-1, keepdims=True) + EPS) * (GDN_DK ** -0.5)
            k = k * lax.rsqrt(jnp.sum(k * k, axis=-1, keepdims=True) + EPS)
            beta = _lane_bcast(sq["beta_all"], h)
            gcol = _lane_bcast(sq["gc_all"], GDN_H + h)
            units.append(dict(bb=bb, h=h, q=q, k=k, v=v, beta=beta, gcol=gcol, kb=k * beta, parts=sq["gc_parts"]))
    for u in units:
        pick = (lane == GDN_H + u["h"]).astype(BF16)
        p0, p1, p2 = u["parts"]
        u["grow"] = _dot_nt(pick, p0) + (_dot_nt(pick, p1) + _dot_nt(pick, p2))
        u["kk"] = _dot3(u["kb"], u["k"], _dot_nt)
        u["qk"] = _dot_nt(u["q"].astype(BF16), u["k"].astype(BF16))
    for u in units:
        gcol = u["gcol"]
        decay = jnp.exp(jnp.where(row_ge, gcol[:, :c] - u["grow"], -jnp.inf))
        eg = jnp.exp(gcol)
        glast = gcol[c - 1:c, :]
        u["lower"] = jnp.where(row_gt, u["kk"] * decay, 0.0)
        u["rhs"] = jnp.concatenate([u["v"] * u["beta"], u["kb"] * eg], axis=1)
        u["attn"] = u["qk"] * decay
        u["qg"] = (u["q"] * eg).astype(BF16)
        u["kg"] = (u["k"] * jnp.exp(glast - gcol)).astype(BF16)
        u["gl"] = jnp.exp(glast)

    tinvs = _unit_lower_inverse_many([u["lower"] for u in units])
    uws = []
    for u, tinv in zip(units, tinvs):
        uws.append(_dot(tinv.astype(BF16), u["rhs"].astype(BF16)))
    states = [st_ref[u["bb"], u["h"]] for u in units]
    sbs = [s.astype(BF16) for s in states]
    vnews = [(uw[:, :GDN_DV] - _dot(uw[:, GDN_DV:].astype(BF16), sb)).astype(BF16) for uw, sb in zip(uws, sbs)]
    for u, state, sb, vnb in zip(units, states, sbs, vnews):
        bb, h = u["bb"], u["h"]
        o = _dot(u["qg"], sb) + _dot(u["attn"].astype(BF16), vnb)
        st_ref[bb, h] = state * u["gl"] + _dot_tn(u["kg"], vnb)
        o = _rms(o, on_ref[...]) * _silu(z_ref[bb, :, h * GDN_DV:(h + 1) * GDN_DV].astype(F32))
        o_ref[bb, :, h * GDN_DV:(h + 1) * GDN_DV] = o.astype(o_ref.dtype)


def _gdn(act, gates, z, a_row, dt_row, o_norm, batch, seq):
    c = GDN_CHUNK
    nc = seq // c
    w3 = act.shape[1]
    wo = GDN_H * GDN_DV
    nb = min(GDN_SEQS_PER_STEP, batch)
    row = lambda b, i: (b, i, 0)
    fix = lambda b, i: (0, 0)
    out = pl.pallas_call(
        _gdn_kernel,
        grid=(batch // nb, nc),
        in_specs=[pl.BlockSpec((nb, c, w3), row), pl.BlockSpec((nb, c, LANES), row), pl.BlockSpec((nb, c, wo), row),
                  pl.BlockSpec((1, LANES), fix), pl.BlockSpec((1, LANES), fix), pl.BlockSpec((1, GDN_DV), fix)],
        out_specs=pl.BlockSpec((nb, c, wo), row),
        out_shape=jax.ShapeDtypeStruct((batch, seq, wo), BF16),
        scratch_shapes=[pltpu.VMEM((nb, GDN_H, GDN_DK, GDN_DV), F32)],
        compiler_params=_cparams(("arbitrary", "arbitrary")),
        name="gdn",
    )(act.reshape(batch, seq, w3), gates.reshape(batch, seq, LANES), z.reshape(batch, seq, wo), a_row, dt_row, o_norm)
    return out.reshape(batch * seq, wo)


def _mlstm_kernel(q_ref, k_ref, v_ref, og_ref, g_ref, bias_ref, nrm_ref, o_ref, c_ref, n_ref, m_ref):
    @pl.when(pl.program_id(1) == 0)
    def _():
        c_ref[...] = jnp.zeros(c_ref.shape, F32)
        n_ref[...] = jnp.zeros(n_ref.shape, F32)
        m_ref[...] = jnp.zeros(m_ref.shape, F32)

    c = ML_CHUNK
    tri = (_iota2((c, c), 0) >= _iota2((c, c), 1)).astype(F32)
    row_ge = _iota2((c, c), 0) >= _iota2((c, c), 1)
    ones = jnp.ones((c, LANES), F32)
    lane = _iota2((c, LANES), 1)

    units = []
    for bb in range(q_ref.shape[0]):
        pre = g_ref[bb] + bias_ref[...]
        logf = jnp.minimum(pre, 0.0) - jnp.log(1.0 + jnp.exp(-jnp.abs(pre)))
        bcum_all = _dot_sel(tri, logf)
        for h in range(ML_H):
            q = q_ref[bb, :, h * LANES:(h + 1) * LANES].astype(F32)
            k = k_ref[bb, :, h * LANES:(h + 1) * LANES].astype(F32) * (ML_DK ** -0.5)
            units.append(dict(bb=bb, h=h, q=q, k=k, qb=q.astype(BF16), vb=v_ref[bb, :, h * ML_DV:(h + 1) * ML_DV].astype(BF16),
                              bcol=_lane_bcast(bcum_all, ML_H + h),
                              icol=_lane_bcast(pre, h),
                              col=jnp.where(lane == h, pre, 0.0) - jnp.where(lane == ML_H + h, bcum_all, 0.0),
                              m_st=m_ref[bb, h], cst=c_ref[bb, h], nst=n_ref[bb, h]))
    for u in units:
        u["row"] = _dot_sel(ones, u["col"], _dot_nt)
        u["qk"] = _dot_nt(u["qb"], u["k"].astype(BF16))
        u["qc"] = _dot(u["qb"], u["cst"].astype(BF16))
    for u in units:
        u["d"] = jnp.where(row_ge, u["bcol"][:, :c] + u["row"], -jnp.inf)
        u["inter"] = u["bcol"] + u["m_st"]
        u["m_t"] = jnp.maximum(u["inter"], jnp.max(u["d"], axis=-1, keepdims=True))
        u["b_end"] = u["bcol"][c - 1:c, :]
        u["a"] = u["b_end"] - u["bcol"] + u["icol"]
        u["m_new"] = jnp.maximum(u["b_end"] + u["m_st"], jnp.max(u["a"], axis=0, keepdims=True))
    for u in units:
        u["w_inter"] = jnp.exp(u["inter"] - u["m_t"])
        u["p"] = jnp.exp(u["d"] - u["m_t"][:, :c]) * u["qk"]
        u["keep"] = jnp.exp(u["b_end"] + u["m_st"] - u["m_new"])
        u["ks"] = u["k"] * jnp.exp(u["a"] - u["m_new"])
    for u in units:
        u["pv"] = _dot(u["p"].astype(BF16), u["vb"])
        u["kv"] = _dot_tn(u["ks"].astype(BF16), u["vb"])
    for u in units:
        u["den"] = (u["w_inter"] * jnp.sum(u["q"] * u["nst"], axis=-1, keepdims=True)
                    + jnp.sum(u["p"], axis=-1, keepdims=True))
    for u in units:
        bb, h = u["bb"], u["h"]
        num = u["w_inter"] * u["qc"] + u["pv"]
        hc = num / jnp.maximum(jnp.abs(u["den"]), jnp.exp(-u["m_t"]))
        c_ref[bb, h] = u["cst"] * u["keep"] + u["kv"]
        n_ref[bb, h] = u["nst"] * u["keep"] + jnp.sum(u["ks"], axis=0, keepdims=True)
        m_ref[bb, h] = u["m_new"]
        hn = (_rms(hc, nrm_ref[:, h * ML_DV:(h + 1) * ML_DV])
              * _sigmoid(og_ref[bb, :, h * ML_DV:(h + 1) * ML_DV].astype(F32)))
        o_ref[bb, :, h * ML_DV:(h + 1) * ML_DV] = hn.astype(o_ref.dtype)


def _mlstm(mq, mk, mv, mo, gates, bias_row, norm_row, batch, seq):
    c = ML_CHUNK
    nc = seq // c
    nb = min(MLSTM_SEQS_PER_STEP, batch)
    row = lambda b, i: (b, i, 0)
    fix = lambda b, i: (0, 0)
    wide = ML_H * LANES
    r3 = lambda a: a.reshape(batch, seq, a.shape[-1])
    out = pl.pallas_call(
        _mlstm_kernel,
        grid=(batch // nb, nc),
        in_specs=[pl.BlockSpec((nb, c, wide), row), pl.BlockSpec((nb, c, wide), row), pl.BlockSpec((nb, c, wide), row),
                  pl.BlockSpec((nb, c, wide), row), pl.BlockSpec((nb, c, LANES), row),
                  pl.BlockSpec((1, LANES), fix), pl.BlockSpec((1, wide), fix)],
        out_specs=pl.BlockSpec((nb, c, wide), row),
        out_shape=jax.ShapeDtypeStruct((batch, seq, wide), BF16),
        scratch_shapes=[pltpu.VMEM((nb, ML_H, LANES, ML_DV), F32), pltpu.VMEM((nb, ML_H, 1, LANES), F32),
                        pltpu.VMEM((nb, ML_H, 1, LANES), F32)],
        compiler_params=_cparams(("arbitrary", "arbitrary")),
        name="mlstm",
    )(r3(mq), r3(mk), r3(mv), r3(mo), r3(gates), bias_row, norm_row)
    return out.reshape(batch * seq, wide)


def _swa_kernel(q_ref, kc_ref, kp_ref, vc_ref, vp_ref, sink_ref, o_ref):
    w = WINDOW
    n = pl.program_id(1)
    scale = SWA_D ** -0.5
    qi = _iota2((w, w), 0)
    kj = _iota2((w, w), 1)
    mask_c = kj <= qi
    mask_p = jnp.logical_and(kj > qi, n > 0)
    grp = SWA_H // SWA_KV
    neg = -1e30
    units = [(bb, h) for bb in range(q_ref.shape[0]) for h in range(SWA_H)]
    scores = []
    half_of_lane = _iota2((w, LANES), 1) // SWA_D
    for bb, h in units:
        g = h // grp
        pair = q_ref[bb, :, (h // 2) * LANES:(h // 2 + 1) * LANES]
        q = jnp.where(half_of_lane == h % 2, pair, jnp.zeros_like(pair))
        scores.append((_dot_nt(q, kc_ref[bb, :, g * LANES:(g + 1) * LANES]),
                       _dot_nt(q, kp_ref[bb, :, g * LANES:(g + 1) * LANES])))
    masked, tops, exps, dens, probs = [], [], [], [], {}
    for sc, sp in scores:
        masked.append((jnp.where(mask_c, sc * scale, neg), jnp.where(mask_p, sp * scale, neg)))
    for (bb, h), (s_c, s_p) in zip(units, masked):
        tops.append(jnp.maximum(jnp.max(jnp.maximum(s_c, s_p), axis=-1, keepdims=True), sink_ref[:, h:h + 1]))
    for (s_c, s_p), m in zip(masked, tops):
        exps.append((jnp.where(mask_c, jnp.exp(s_c - m), 0.0), jnp.where(mask_p, jnp.exp(s_p - m), 0.0)))
    ones_b = jnp.ones((w, LANES), BF16)
    for (bb, h), (p_c, p_p), m in zip(units, exps, tops):
        p_c, p_p = p_c.astype(BF16), p_p.astype(BF16)
        probs[bb, h] = (p_c, p_p)
        dens.append(_dot(p_c, ones_b) + _dot(p_p, ones_b) + jnp.exp(sink_ref[:, h:h + 1] - m))
    inv = {u: 1.0 / den for u, den in zip(units, dens)}
    for bb in range(q_ref.shape[0]):
        for pair in range(SWA_H // 2):
            acc = None
            for sub in range(2):
                h = 2 * pair + sub
                vcol = (2 * (h // grp) + sub) * LANES
                p_c, p_p = probs[bb, h]
                part = (_dot(p_c, vc_ref[bb, :, vcol:vcol + LANES]) + _dot(p_p, vp_ref[bb, :, vcol:vcol + LANES])) * inv[bb, h]
                acc = part if acc is None else acc + part
            o_ref[bb, :, pair * LANES:(pair + 1) * LANES] = acc.astype(o_ref.dtype)


def _swa(sq, sk, sv, sinks_row, batch, seq):
    w = WINDOW
    nb = seq // w
    ns = min(SWA_SEQS_PER_STEP, batch)
    wo = SWA_H * SWA_D
    cur = lambda b, n: (b, n, 0)
    prev = lambda b, n: (b, jnp.maximum(n - 1, 0), 0)
    r3 = lambda a: a.reshape(batch, seq, a.shape[-1])
    q3, k3, v3 = r3(sq), r3(sk), r3(sv)
    out = pl.pallas_call(
        _swa_kernel,
        grid=(batch // ns, nb),
        in_specs=[pl.BlockSpec((ns, w, sq.shape[1]), cur),
                  pl.BlockSpec((ns, w, sk.shape[1]), cur), pl.BlockSpec((ns, w, sk.shape[1]), prev),
                  pl.BlockSpec((ns, w, sv.shape[1]), cur), pl.BlockSpec((ns, w, sv.shape[1]), prev),
                  pl.BlockSpec((1, LANES), lambda b, n: (0, 0))],
        out_specs=pl.BlockSpec((ns, w, wo), cur),
        out_shape=jax.ShapeDtypeStruct((batch, seq, wo), BF16),
        compiler_params=_cparams(("arbitrary", "arbitrary")),
        name="swa",
    )(q3, k3, k3, v3, v3, sinks_row)
    return out.reshape(batch * seq, wo)


def _layer_norm(h, g, b):
    mu = jnp.mean(h, axis=-1, keepdims=True)
    d = h - mu
    var = jnp.mean(d * d, axis=-1, keepdims=True)
    return d * lax.rsqrt(var + LN_EPS) * g + b


def _outproj_kernel(x_ref, a1_ref, a2_ref, w_ref, g_ref, b_ref, o_ref, op_ref):
    k1 = a1_ref.shape[1]
    y = _dot(a1_ref[...].astype(BF16), w_ref[0:k1, :]) + _dot(a2_ref[...].astype(BF16), w_ref[k1:, :])
    h = _layer_norm(DN_ALPHA * x_ref[...] + y, g_ref[...], b_ref[...])
    o_ref[...] = h
    op_ref[...] = _pack_pairs(h)


def _outproj_ln(x, a1, a2, w, g, b, tm=1024):
    t, d = x.shape
    tm = min(tm, t)
    row = lambda i: (i, 0)
    fix = lambda i: (0, 0)
    return pl.pallas_call(
        _outproj_kernel,
        grid=(t // tm,),
        in_specs=[pl.BlockSpec((tm, d), row), pl.BlockSpec((tm, a1.shape[1]), row), pl.BlockSpec((tm, a2.shape[1]), row),
                  pl.BlockSpec(w.shape, fix), pl.BlockSpec((1, d), fix), pl.BlockSpec((1, d), fix)],
        out_specs=[pl.BlockSpec((tm, d), row), pl.BlockSpec((tm, d // 2), row)],
        out_shape=[jax.ShapeDtypeStruct((t, d), F32), jax.ShapeDtypeStruct((t, d // 2), jnp.uint32)],
        compiler_params=_cparams(("arbitrary",)),
        name="outproj_ln",
    )(x, a1, a2, w, g, b)


def _first_index(x, m, iota_f, sentinel):
    return jnp.min(jnp.where(x == m, iota_f, sentinel), axis=0, keepdims=True)


def _router_kernel(x_ref, wt_ref, bias_ref, idx_ref, gate_ref, rank_ref, cnt_ref, carry_ref):
    tm = x_ref.shape[0]
    e = N_EXPERTS
    gs = e // N_GROUPS
    ninf = -jnp.inf

    @pl.when(pl.program_id(0) == 0)
    def _():
        carry_ref[...] = jnp.zeros(carry_ref.shape, F32)

    logits = _dot3(wt_ref[...], x_ref[...], _dot_nt)
    scores = _sigmoid(logits)
    sel = scores + bias_ref[:, 0:1]

    sub_f = _iota2((gs, tm), 0).astype(F32)
    gscore = []
    for g in range(N_GROUPS):
        blk = sel[g * gs:(g + 1) * gs, :]
        m1 = jnp.max(blk, axis=0, keepdims=True)
        i1 = _first_index(blk, m1, sub_f, float(gs))
        m2 = jnp.max(jnp.where(sub_f == i1, ninf, blk), axis=0, keepdims=True)
        gscore.append(m1 + m2)
    gsc = jnp.concatenate(gscore, axis=0)
    grp_f = _iota2((N_GROUPS, tm), 0).astype(F32)
    gmask = jnp.zeros((N_GROUPS, tm), F32)
    for _ in range(TOPK_GROUPS):
        m = jnp.max(gsc, axis=0, keepdims=True)
        gi = _first_index(gsc, m, grp_f, float(N_GROUPS))
        hit = grp_f == gi
        gmask = jnp.where(hit, 1.0, gmask)
        gsc = jnp.where(hit, ninf, gsc)
    masked = jnp.concatenate(
        [jnp.where(gmask[g:g + 1, :] > 0.0, sel[g * gs:(g + 1) * gs, :], ninf) for g in range(N_GROUPS)], axis=0)

    exp_f = _iota2((e, tm), 0).astype(F32)
    chosen = jnp.zeros((e, tm), F32)
    idxs, gates = [], []
    for _ in range(TOP_K):
        m = jnp.max(masked, axis=0, keepdims=True)
        ei = _first_index(masked, m, exp_f, float(e))
        hit = exp_f == ei
        idxs.append(ei)
        gates.append(jnp.sum(jnp.where(hit, scores, 0.0), axis=0, keepdims=True))
        chosen = jnp.where(hit, 1.0, chosen)
        masked = jnp.where(hit, ninf, masked)
    gate = jnp.concatenate(gates, axis=0)
    gate = gate / jnp.sum(gate, axis=0, keepdims=True) * ROUTED_SCALE
    idx_f = jnp.concatenate(idxs, axis=0)

    upper = (_iota2((tm, tm), 0) < _iota2((tm, tm), 1)).astype(BF16)
    before = _dot(chosen.astype(BF16), upper) + carry_ref[...][:, 0:1]
    ranks = [jnp.sum(jnp.where(exp_f == idxs[k], before, 0.0), axis=0, keepdims=True) for k in range(TOP_K)]
    carry_ref[...] = carry_ref[...] + jnp.sum(chosen, axis=1, keepdims=True)

    idx_ref[...] = idx_f.astype(jnp.int32)
    gate_ref[...] = gate
    rank_ref[...] = jnp.concatenate(ranks, axis=0).astype(jnp.int32)
    cnt_ref[...] = carry_ref[...]


def _router(x, wt, bias_col, tm=512):
    t, d = x.shape
    col = lambda i: (0, i)
    fix = lambda i: (0, 0)
    return pl.pallas_call(
        _router_kernel,
        grid=(t // tm,),
        in_specs=[pl.BlockSpec((tm, d), lambda i: (i, 0)), pl.BlockSpec(wt.shape, fix), pl.BlockSpec((N_EXPERTS, LANES), fix)],
        out_specs=[pl.BlockSpec((TOP_K, tm), col), pl.BlockSpec((TOP_K, tm), col), pl.BlockSpec((TOP_K, tm), col),
                   pl.BlockSpec((N_EXPERTS, LANES), fix)],
        out_shape=[jax.ShapeDtypeStruct((TOP_K, t), jnp.int32), jax.ShapeDtypeStruct((TOP_K, t), F32),
                   jax.ShapeDtypeStruct((TOP_K, t), jnp.int32), jax.ShapeDtypeStruct((N_EXPERTS, LANES), F32)],
        scratch_shapes=[pltpu.VMEM((N_EXPERTS, LANES), F32)],
        compiler_params=_cparams(("arbitrary",)),
        name="router",
    )(x, wt, bias_col)


def _dest_kernel(idx_ref, rank_ref, start_ref, dest_ref):
    tm = idx_ref.shape[1]
    exp_i = _iota2((N_EXPERTS, tm), 0)
    start = start_ref[:, 0:1]
    rows = [jnp.sum(jnp.where(exp_i == idx_ref[s:s + 1, :], start, 0.0), axis=0, keepdims=True) for s in range(TOP_K)]
    dest_ref[...] = jnp.concatenate(rows, axis=0).astype(jnp.int32) + rank_ref[...]


def _dest_rows(idx, rank, start_col, tm=2048):
    t = idx.shape[1]
    tm = min(tm, t)
    col = lambda i: (0, i)
    return pl.pallas_call(
        _dest_kernel,
        grid=(t // tm,),
        in_specs=[pl.BlockSpec((TOP_K, tm), col), pl.BlockSpec((TOP_K, tm), col),
                  pl.BlockSpec((N_EXPERTS, LANES), lambda i: (0, 0))],
        out_specs=pl.BlockSpec((TOP_K, tm), col),
        out_shape=jax.ShapeDtypeStruct((TOP_K, t), jnp.int32),
        compiler_params=_cparams(("arbitrary",)),
        name="moe_dest",
    )(idx, rank, start_col)


def _pack_pairs(x):
    n = x.shape[1] // 2
    hi = lax.bitcast_convert_type(x[:, :n].astype(BF16).astype(F32), jnp.uint32)
    lo = lax.bitcast_convert_type(x[:, n:].astype(BF16).astype(F32), jnp.uint32)
    return hi | (lo >> 16)


def _unpack_pairs(w):
    hi = lax.bitcast_convert_type(w & jnp.uint32(0xFFFF0000), F32)
    lo = lax.bitcast_convert_type(w << 16, F32)
    return hi, lo


def _sc_scatter_rows(xp, dest, rows, chunk=LANES):
    t, width = xp.shape
    info = plsc.get_sparse_core_info()
    ncores, nsub = info.num_cores, info.num_subcores
    per_worker = t // (ncores * nsub)
    nchunk = per_worker // chunk
    mesh = plsc.VectorSubcoreMesh(core_axis_name="c", subcore_axis_name="s")

    @functools.partial(
        pl.kernel, mesh=mesh,
        out_type=jax.ShapeDtypeStruct((rows, width), xp.dtype),
        scratch_types=[pltpu.VMEM((TOP_K, chunk), jnp.int32), pltpu.VMEM((chunk, width), xp.dtype), pltpu.SemaphoreType.DMA],
    )
    def scatter(xp_hbm, dest_hbm, out_hbm, idx_v, rows_v, sem):
        base = (lax.axis_index("s") * ncores + lax.axis_index("c")) * per_worker

        @pl.loop(0, nchunk)
        def _(i):
            off = pl.multiple_of(base + i * chunk, chunk)
            pltpu.sync_copy(dest_hbm.at[:, pl.ds(off, chunk)], idx_v)
            pltpu.sync_copy(xp_hbm.at[pl.ds(off, chunk)], rows_v)
            copies = [pltpu.async_copy(rows_v, out_hbm.at[idx_v.at[s]], sem) for s in range(TOP_K)]
            for cp in copies:
                cp.wait()

    return scatter(xp, dest)


def _experts_kernel(be_ref, nu_ref, nv_ref, first_ref, slot_ref, nxt_ref, xs_ref, wg_hbm, wu_hbm, wd_hbm, ys_ref,
                    wgf_ref, wuf_ref, wdf_ref, wgb_ref, wub_ref, wdb_ref, sem, *, layer):
    i = pl.program_id(0)

    def fetch(e, s):
        return [pltpu.make_async_copy(wg_hbm.at[layer, e], wgf_ref.at[s], sem.at[s]),
                pltpu.make_async_copy(wu_hbm.at[layer, e], wuf_ref.at[s], sem.at[s]),
                pltpu.make_async_copy(wd_hbm.at[layer, e], wdf_ref.at[s], sem.at[s])]

    @pl.when(i == 0)
    def _():
        for cp in fetch(be_ref[0], 0):
            cp.start()

    @pl.when(jnp.logical_and(first_ref[i] == 1, i < nu_ref[0]))
    def _():
        s = slot_ref[i]
        for cp in fetch(be_ref[i], s):
            cp.wait()
        wgb_ref[...] = wgf_ref[s].astype(BF16)
        wub_ref[...] = wuf_ref[s].astype(BF16)
        wdb_ref[...] = wdf_ref[s].astype(BF16)

        @pl.when(nxt_ref[i] >= 0)
        def _():
            for cp in fetch(nxt_ref[i], 1 - s):
                cp.start()

    @pl.when(i < nu_ref[0])
    def _():
        sub = xs_ref.shape[0] // EXPERT_SUBBLOCKS
        acts = []
        for r in range(EXPERT_SUBBLOCKS):
            rows = pl.ds(r * sub, sub)
            live = (_iota2((sub, 1), 0) + r * sub) < nv_ref[i]
            xa, xb = _unpack_pairs(jnp.where(live, xs_ref[rows, :], jnp.uint32(0)))
            x = jnp.concatenate([xa.astype(BF16), xb.astype(BF16)], axis=1)
            acts.append((_dot(x, wgb_ref[...]), _dot(x, wub_ref[...])))
        outs = [_dot((_silu(gate) * up).astype(BF16), wdb_ref[...]) for gate, up in acts]
        for r, y in enumerate(outs):
            ys_ref[pl.ds(r * sub, sub), :] = _pack_pairs(y)


def _experts(block_e, n_used, n_valid, xs, wg, wu, wd, layer, block):
    rows, half = xs.shape
    d = 2 * half
    nb = rows // block
    pos = jnp.arange(nb, dtype=jnp.int32)
    first = jnp.concatenate([jnp.ones((1,), jnp.int32), (block_e[1:] != block_e[:-1]).astype(jnp.int32)])
    slot = (jnp.cumsum(first) - 1) % 2
    later = (pos[None, :] > pos[:, None]) & (block_e[None, :] != block_e[:, None]) & (pos[None, :] < n_used[0])
    nxt_pos = jnp.min(jnp.where(later, pos[None, :], nb), axis=1)
    nxt = jnp.where(nxt_pos < nb, block_e[jnp.minimum(nxt_pos, nb - 1)], -1)
    blk = lambda i, be, nu, *rest: (jnp.minimum(i, nu[0] - 1), 0)
    hbm = pl.BlockSpec(memory_space=pl.ANY)
    return pl.pallas_call(
        functools.partial(_experts_kernel, layer=layer),
        grid_spec=pltpu.PrefetchScalarGridSpec(
            num_scalar_prefetch=6,
            grid=(nb,),
            in_specs=[pl.BlockSpec((block, half), blk), hbm, hbm, hbm],
            out_specs=pl.BlockSpec((block, half), blk),
            scratch_shapes=[pltpu.VMEM((2, d, D_EXPERT), F32), pltpu.VMEM((2, d, D_EXPERT), F32),
                            pltpu.VMEM((2, D_EXPERT, d), F32),
                            pltpu.VMEM((d, D_EXPERT), BF16), pltpu.VMEM((d, D_EXPERT), BF16),
                            pltpu.VMEM((D_EXPERT, d), BF16), pltpu.SemaphoreType.DMA((2,))],
        ),
        out_shape=jax.ShapeDtypeStruct((rows, half), jnp.uint32),
        compiler_params=_cparams(("arbitrary",)),
        name="moe_experts",
    )(block_e, n_used, n_valid, first, slot.astype(jnp.int32), nxt.astype(jnp.int32), xs, wg, wu, wd)


def _sc_gather_rows(table, idx, chunk=SC_CHUNK):
    n = idx.shape[0]
    width = table.shape[1]
    info = plsc.get_sparse_core_info()
    ncores, nsub = info.num_cores, info.num_subcores
    per_worker = n // (ncores * nsub)
    nchunk = per_worker // chunk
    mesh = plsc.VectorSubcoreMesh(core_axis_name="c", subcore_axis_name="s")

    @functools.partial(
        pl.kernel, mesh=mesh,
        out_type=jax.ShapeDtypeStruct((n, width), table.dtype),
        scratch_types=[pltpu.VMEM((nchunk, chunk), jnp.int32), pltpu.VMEM((2, chunk, width), table.dtype),
                       pltpu.SemaphoreType.DMA((2,)), pltpu.SemaphoreType.DMA((2,))],
    )
    def gather(table_hbm, idx_hbm, out_hbm, idx_v, rows_v, gsem, wsem):
        wid = lax.axis_index("s") * ncores + lax.axis_index("c")
        base = wid * per_worker
        pltpu.sync_copy(idx_hbm.at[pl.ds(wid * nchunk, nchunk)], idx_v)

        def fetch(j, b):
            return pltpu.make_async_copy(table_hbm.at[idx_v.at[j]], rows_v.at[b], gsem.at[b])

        def flush(j, b):
            off = pl.multiple_of(base + j * chunk, chunk)
            return pltpu.make_async_copy(rows_v.at[b], out_hbm.at[pl.ds(off, chunk)], wsem.at[b])

        fetch(0, 0).start()

        @pl.loop(0, nchunk, step=2)
        def _(i):
            for b in range(2):
                j = i + b
                fetch(j, b).wait()

                @pl.when(j + 1 < nchunk)
                def _():
                    @pl.when(j >= 1)
                    def _():
                        flush(j - 1, 1 - b).wait()

                    fetch(j + 1, 1 - b).start()

                flush(j, b).start()

        flush(nchunk - 2, 0).wait()
        flush(nchunk - 1, 1).wait()

    return gather(table, idx.reshape(n // chunk, chunk))


def _shared_kernel(xp_ref, sg_ref, su_ref, sd_ref, o_ref):
    xa, xb = _unpack_pairs(xp_ref[...])
    x = jnp.concatenate([xa.astype(BF16), xb.astype(BF16)], axis=1)
    hs = _silu(_dot(x, sg_ref[...])) * _dot(x, su_ref[...])
    o_ref[...] = _pack_pairs(_dot(hs.astype(BF16), sd_ref[...]))


def _shared_expert(xp, sg, su, sd, tm=512):
    t, half = xp.shape
    row = lambda i: (i, 0)
    fix = lambda i: (0, 0)
    return pl.pallas_call(
        _shared_kernel,
        grid=(t // tm,),
        in_specs=[pl.BlockSpec((tm, half), row), pl.BlockSpec(sg.shape, fix), pl.BlockSpec(su.shape, fix),
                  pl.BlockSpec(sd.shape, fix)],
        out_specs=pl.BlockSpec((tm, half), row),
        out_shape=jax.ShapeDtypeStruct((t, half), jnp.uint32),
        compiler_params=_cparams(("arbitrary",)),
        name="moe_shared",
    )(xp, sg, su, sd)


def _combine_kernel(x_ref, gate_ref, rows_ref, sh_ref, g_ref, b_ref, o_ref):
    gate = gate_ref[...]
    ya, yb = _unpack_pairs(sh_ref[...])
    for s in range(TOP_K):
        a, b = _unpack_pairs(rows_ref[s])
        ya = ya + gate[:, s:s + 1] * a
        yb = yb + gate[:, s:s + 1] * b
    ff = jnp.concatenate([ya, yb], axis=1)
    o_ref[...] = _layer_norm(DN_ALPHA * x_ref[...] + ff, g_ref[...], b_ref[...])


def _combine(x, gate_t, rows, shared, g, b, tm=512):
    t, d = x.shape
    row = lambda i: (i, 0)
    fix = lambda i: (0, 0)
    return pl.pallas_call(
        _combine_kernel,
        grid=(t // tm,),
        in_specs=[pl.BlockSpec((tm, d), row), pl.BlockSpec((tm, TOP_K), row),
                  pl.BlockSpec((TOP_K, tm, d // 2), lambda i: (0, i, 0)), pl.BlockSpec((tm, d // 2), row),
                  pl.BlockSpec((1, d), fix), pl.BlockSpec((1, d), fix)],
        out_specs=pl.BlockSpec((tm, d), row),
        out_shape=jax.ShapeDtypeStruct((t, d), F32),
        compiler_params=_cparams(("arbitrary",)),
        name="moe_combine",
    )(x, gate_t, rows, shared, g, b)


def _take_cols(w, idx):
    idx = np.asarray(idx)
    runs, start = [], 0
    for pos in range(1, len(idx) + 1):
        run_ends = pos == len(idx) or (idx[pos] != idx[pos - 1] + 1 if idx[pos - 1] >= 0 else idx[pos] >= 0)
        if run_ends:
            runs.append((start, int(idx[start]), pos - start))
            start = pos

    def body(w_ref, o_ref):
        for dst, src, width in runs:
            if src < 0:
                o_ref[:, dst:dst + width] = jnp.zeros((o_ref.shape[0], width), o_ref.dtype)
            else:
                o_ref[:, dst:dst + width] = w_ref[:, src:src + width].astype(o_ref.dtype)

    rows = w.shape[0]
    tr = min(rows, 256)
    return pl.pallas_call(
        body,
        grid=(rows // tr,),
        in_specs=[pl.BlockSpec((tr, w.shape[1]), lambda i: (i, 0))],
        out_specs=pl.BlockSpec((tr, len(idx)), lambda i: (i, 0)),
        out_shape=jax.ShapeDtypeStruct((rows, len(idx)), BF16),
        compiler_params=_cparams(("arbitrary",)),
        name="weight_cols",
    )(w)


def _pad_lane_row(v, first_lane, width=LANES):
    out = jnp.zeros((1, width), F32)
    return lax.dynamic_update_slice(out, v.reshape(1, -1).astype(F32), (0, first_lane))


def _even_in_cols():
    z = lambda n: -np.ones(n, int)
    kr0 = Q_LORA + KV_LORA
    half = MLA_ROPE // 2
    cols = [np.arange(0, Q_LORA), np.arange(Q_LORA, Q_LORA + KV_LORA),
            z(64), np.arange(kr0, kr0 + MLA_ROPE), z(32),
            z(64), np.arange(kr0 + half, kr0 + MLA_ROPE), np.arange(kr0, kr0 + half), z(32)]
    g0 = kr0 + MLA_ROPE
    nqk = GDN_H * GDN_DK
    cols.append(np.arange(g0, g0 + 3 * nqk))
    zoff = g0 + 3 * nqk + 2 * GDN_H
    cols.append(np.arange(zoff, zoff + GDN_H * GDN_DV))
    cols += [np.arange(g0 + 3 * nqk, g0 + 3 * nqk + 2 * GDN_H), z(LANES - 2 * GDN_H)]
    return np.concatenate(cols)


EV_WIDTHS = (Q_LORA + KV_LORA + 2 * LANES, 3 * GDN_H * GDN_DK, GDN_H * GDN_DV, LANES)


def _mla_q_cols():
    per = MLA_NOPE + MLA_ROPE
    half = MLA_ROPE // 2
    main, sw = [], []
    for h in range(MLA_H):
        b = h * per
        main += [np.arange(b, b + per), -np.ones(LANES - per, int)]
        sw += [-np.ones(MLA_NOPE, int), np.arange(b + MLA_NOPE + half, b + per), np.arange(b + MLA_NOPE, b + MLA_NOPE + half),
               -np.ones(LANES - per, int)]
    return np.concatenate(main + sw)


def _mla_kv_cols():
    per = MLA_NOPE + MLA_V
    kc, vc = [], []
    for h in range(MLA_H):
        b = h * per
        kc += [np.arange(b, b + MLA_NOPE), -np.ones(LANES - MLA_NOPE, int)]
        vv = np.arange(b + MLA_NOPE, b + per)
        pad = -np.ones(LANES - MLA_V, int)
        vc += [vv, pad] if h % 2 == 0 else [pad, vv]
    return np.concatenate(kc + vc)


def _odd_in_cols():
    z = lambda n: -np.ones(n, int)
    o = 0
    cols = []
    mq0, mk0 = 0, ML_H * ML_DK
    for base in (mq0, mk0):
        for h in range(ML_H):
            cols += [np.arange(base + h * ML_DK, base + (h + 1) * ML_DK), z(LANES - ML_DK)]
    mv0 = 2 * ML_H * ML_DK
    cols.append(np.arange(mv0, mv0 + ML_H * ML_DV))
    mi0 = mv0 + ML_H * ML_DV
    mo0 = mi0 + 2 * ML_H
    cols.append(np.arange(mo0, mo0 + ML_H * ML_DV))
    cols += [np.arange(mi0, mi0 + 2 * ML_H), z(LANES - 2 * ML_H)]
    sq0 = mo0 + ML_H * ML_DV
    sk0 = sq0 + SWA_H * SWA_D
    sv0 = sk0 + SWA_KV * SWA_D
    half = SWA_D // 2

    cols.append(np.arange(sq0, sq0 + SWA_H * SWA_D))
    for g in range(SWA_KV):
        cols += [np.arange(sk0 + g * SWA_D, sk0 + (g + 1) * SWA_D)] * 2
    for g in range(SWA_KV):
        vv = np.arange(sv0 + g * SWA_D, sv0 + (g + 1) * SWA_D)
        cols += [vv, z(LANES - SWA_D), z(LANES - SWA_D), vv]
    return np.concatenate(cols)


def _even_weights(w_in, w_qb, w_kvb):
    return (_take_cols(w_in, _even_in_cols()), _take_cols(w_qb, _mla_q_cols()), _take_cols(w_kvb, _mla_kv_cols()))


def _even_mixer(x, tabs, weights, q_norm, kv_norm, conv_w, a_log, dt_bias, o_norm, batch, seq):
    ctab, stab = tabs
    w, wq2, wkv2 = weights
    mla_in, act, z, gates = _proj_even(x, w, conv_w, seq)
    q, k, v = _mla_prep(mla_in, ctab, stab, q_norm.reshape(1, -1), kv_norm.reshape(1, -1), wq2, wkv2)
    o_a = _mla_attn(q, k, v, batch, seq)
    o_b = _gdn(act, gates, z, _pad_lane_row(a_log, GDN_H), _pad_lane_row(dt_bias, GDN_H),
               o_norm.reshape(1, -1), batch, seq)
    return o_a, o_b


def _odd_mixer(x, tabs, w, b_i, b_f, ml_norm, sinks, batch, seq):
    ctab, stab = tabs
    mq, mk, mv, mo, mg, sq, sk, sv = _proj_odd(x, w, ctab, stab)
    bias_row = _pad_lane_row(jnp.concatenate([b_i, b_f]), 0)
    o_c = _mlstm(mq, mk, mv, mo, mg, bias_row, ml_norm.reshape(1, -1), batch, seq)
    o_d = _swa(sq, sk, sv, _pad_lane_row(sinks, 0), batch, seq)
    return o_c, o_d


def _moe(x, xp, router_w, router_b, w_gate, w_up, w_down, layer, s_gate, s_up, s_down, ln_g, ln_b):
    t, d = x.shape
    bias_col = jnp.broadcast_to(router_b.reshape(-1, 1).astype(F32), (N_EXPERTS, LANES))
    idx, gate, rank, cnt = _router(x, router_w.T, bias_col)
    counts = cnt[:, 0].astype(jnp.int32)
    block = int(min(max(pl.next_power_of_2(t * TOP_K // N_EXPERTS) // 2, EXPERT_BLOCK_MIN), EXPERT_BLOCK_MAX))
    padded = (counts + block - 1) // block * block
    pad_end = jnp.cumsum(padded)
    pad_start = pad_end - padded
    start_col = jnp.broadcast_to(pad_start.astype(F32).reshape(-1, 1), (N_EXPERTS, LANES))
    dest = _dest_rows(idx, rank, start_col)
    n_blocks = t * TOP_K // block + N_EXPERTS
    rows = n_blocks * block
    block_row = jnp.arange(n_blocks, dtype=jnp.int32) * block
    block_e = jnp.minimum(jnp.sum((pad_end[None, :] <= block_row[:, None]).astype(jnp.int32), axis=1), N_EXPERTS - 1)
    n_used = (pad_end[-1:] // block).astype(jnp.int32)
    live_end = jnp.sum(jnp.where(block_e[:, None] == jnp.arange(N_EXPERTS, dtype=jnp.int32)[None, :],
                                 (pad_start + counts)[None, :], 0), axis=1)
    n_valid = jnp.clip(live_end - block_row, 0, block).astype(jnp.int32)
    xs = _sc_scatter_rows(xp, dest, rows)
    ys = _experts(block_e, n_used, n_valid, xs, w_gate, w_up, w_down, layer, block)
    picked = _sc_gather_rows(ys, dest.reshape(-1)).reshape(TOP_K, t, d // 2)
    shared = _shared_expert(xp, s_gate.astype(BF16), s_up.astype(BF16), s_down.astype(BF16))
    return _combine(x, gate.T, picked, shared, ln_g.reshape(1, -1), ln_b.reshape(1, -1))


def kernel(x, positions, ev_w_in, mla_q_norm, mla_w_qb, mla_kv_norm, mla_w_kvb, gdn_conv, gdn_a_log, gdn_dt_bias, gdn_norm, ev_w_out, od_w_in, mlstm_b_i, mlstm_b_f, mlstm_norm, swa_sinks, od_w_out, ln1_g, ln1_b, router_w, router_b, moe_w_gate, moe_w_up, moe_w_down, shared_w_gate, shared_w_up, shared_w_down, ln2_g, ln2_b):
    batch, seq, d = x.shape
    streams = STREAMS if batch % STREAMS == 0 else 1
    sb = batch // streams
    ts = sb * seq
    hs, tabs_m, tabs_s = [], [], []
    for s in range(streams):
        pos = positions[s * sb:(s + 1) * sb].reshape(ts, 1).astype(F32)
        tm_, ts_ = _rope_tables(pos)
        tabs_m.append(tm_)
        tabs_s.append(ts_)
        hs.append(x[s * sb:(s + 1) * sb].reshape(ts, d))
    for layer in range(DEPTH):
        j = layer // 2
        if layer % 2 == 0:
            weights = _even_weights(ev_w_in[j], mla_w_qb[j], mla_w_kvb[j])
            w_out = ev_w_out[j].astype(BF16)
        else:
            weights = _take_cols(od_w_in[j], _odd_in_cols())
            w_out = od_w_out[j].astype(BF16)
        for s in range(streams):
            h = hs[s]
            if layer % 2 == 0:
                a1, a2 = _even_mixer(h, tabs_m[s], weights, mla_q_norm[j], mla_kv_norm[j], gdn_conv[j], gdn_a_log[j],
                                     gdn_dt_bias[j], gdn_norm[j], sb, seq)
            else:
                a1, a2 = _odd_mixer(h, tabs_s[s], weights, mlstm_b_i[j], mlstm_b_f[j], mlstm_norm[j], swa_sinks[j], sb, seq)
            h, hp = _outproj_ln(h, a1, a2, w_out, ln1_g[layer].reshape(1, -1), ln1_b[layer].reshape(1, -1))
            hs[s] = _moe(h, hp, router_w[layer], router_b[layer], moe_w_gate, moe_w_up, moe_w_down, layer,
                         shared_w_gate[layer], shared_w_up[layer], shared_w_down[layer], ln2_g[layer], ln2_b[layer])
    return jnp.concatenate([h.reshape(sb, seq, d) for h in hs], axis=0)
```

```python
import functools
import math

import numpy as np
import jax
import jax.numpy as jnp
from jax import lax
from jax.experimental import pallas as pl
from jax.experimental.pallas import tpu as pltpu
from jax.experimental.pallas import tpu_sc as plsc

F32 = jnp.float32
BF16 = jnp.bfloat16
HI = lax.Precision.HIGHEST

D_MODEL = 1024
DEPTH = 4
ROPE_THETA = 10000.0
EPS = 1e-6
LN_EPS = 1e-5
MLA_H, MLA_NOPE, MLA_ROPE, MLA_V = 8, 64, 32, 64
Q_LORA, KV_LORA = 256, 128
GDN_H, GDN_DK, GDN_DV, CONV_W, GDN_CHUNK = 4, 128, 128, 4, 64
ML_H, ML_DK, ML_DV, ML_CHUNK = 4, 64, 128, 64
SWA_H, SWA_KV, SWA_D, WINDOW = 8, 2, 64, 128
N_EXPERTS, N_GROUPS, TOPK_GROUPS, TOP_K = 64, 8, 4, 8
D_EXPERT, D_SHARED = 256, 256
ROUTED_SCALE = 2.5
DN_ALPHA = (2 * DEPTH) ** 0.25

LANES = 128
V7X_VMEM_BYTES = 64 * 1024 * 1024
VMEM_LIMIT = 48 * 1024 * 1024

EXPERT_BLOCK_MIN = 256
EXPERT_BLOCK_MAX = 1024
STREAMS = 1
EXPERT_SUBBLOCKS = 4
SWA_SEQS_PER_STEP = 8
MLSTM_SEQS_PER_STEP = 2
GDN_SEQS_PER_STEP = 8
SC_CHUNK = 64


def _cparams(sem, vmem=VMEM_LIMIT):
    return pltpu.CompilerParams(dimension_semantics=sem, vmem_limit_bytes=vmem)


def _dot(a, b, precision=None):
    return jnp.dot(a, b, preferred_element_type=F32, precision=precision)


def _dot_nt(a, b, precision=None):
    return lax.dot_general(a, b, (((1,), (1,)), ((), ())), preferred_element_type=F32, precision=precision)


def _dot_tn(a, b, precision=None):
    return lax.dot_general(a, b, (((0,), (0,)), ((), ())), preferred_element_type=F32, precision=precision)


def _split2(a):
    hi = a.astype(BF16)
    lo = (a - hi.astype(F32)).astype(BF16)
    return hi, lo


def _split3(a):
    p1 = a.astype(BF16)
    r = a - p1.astype(F32)
    p2 = r.astype(BF16)
    p3 = (r - p2.astype(F32)).astype(BF16)
    return p1, p2, p3


def _dot3(a, b, dot=_dot):
    ah, al = _split2(a)
    bh, bl = _split2(b)
    return dot(ah, bh) + (dot(ah, bl) + dot(al, bh))


def _dot_sel(sel, b, dot=_dot):
    sel = sel.astype(BF16)
    p1, p2, p3 = _split3(b)
    return dot(sel, p1) + (dot(sel, p2) + dot(sel, p3))


def _sigmoid(x):
    return 1.0 / (1.0 + jnp.exp(-x))


def _softplus(x):
    return jnp.maximum(x, 0.0) + jnp.log(1.0 + jnp.exp(-jnp.abs(x)))


def _silu(x):
    return x * _sigmoid(x)


def _lane_bcast(x, c):
    return jnp.broadcast_to(x[:, c:c + 1], x.shape)


def _iota2(shape, dim):
    return lax.broadcasted_iota(jnp.int32, shape, dim)


def _rope_kernel(pos_ref, rows_ref, sel_ref, cm_ref, sm_ref, cs_ref, ss_ref):
    ang = pos_ref[...] * rows_ref[0:1, :]
    cos_parts = _split3(jnp.cos(ang))
    sin_parts = _split3(jnp.sin(ang))

    def place(parts, k):
        return _dot(parts[0], sel_ref[k]) + (_dot(parts[1], sel_ref[k]) + _dot(parts[2], sel_ref[k]))

    cm_ref[...] = place(cos_parts, 0) + rows_ref[1:2, :]
    sm_ref[...] = place(sin_parts, 1)
    cs_ref[...] = place(cos_parts, 2)
    ss_ref[...] = place(sin_parts, 3)


def _rope_consts():
    hm, hs = MLA_ROPE // 2, SWA_D // 2
    rows = np.zeros((8, LANES), np.float32)
    rows[0, :hm] = ROPE_THETA ** (-(np.arange(0, MLA_ROPE, 2, dtype=np.float32) / MLA_ROPE))
    rows[0, hm:hm + hs] = ROPE_THETA ** (-(np.arange(0, SWA_D, 2, dtype=np.float32) / SWA_D))
    rows[1, :MLA_NOPE] = 1.0
    sel = np.zeros((4, LANES, LANES), np.float32)
    for j in range(hm):
        sel[0, j, MLA_NOPE + j] = sel[0, j, MLA_NOPE + hm + j] = 1.0
        sel[1, j, MLA_NOPE + j] = -1.0
        sel[1, j, MLA_NOPE + hm + j] = 1.0
    for h in range(LANES // SWA_D):
        for j in range(hs):
            sel[2, hm + j, h * SWA_D + j] = sel[2, hm + j, h * SWA_D + hs + j] = 1.0
            sel[3, hm + j, h * SWA_D + j] = -1.0
            sel[3, hm + j, h * SWA_D + hs + j] = 1.0
    return jnp.asarray(rows), jnp.asarray(sel, BF16)


def _rope_tables(pos, tm=512):
    t = pos.shape[0]
    tm = min(tm, t)
    rows, sel = _rope_consts()
    cm, sm, cs, ss = pl.pallas_call(
        _rope_kernel,
        grid=(t // tm,),
        in_specs=[pl.BlockSpec((tm, 1), lambda i: (i, 0)), pl.BlockSpec((8, LANES), lambda i: (0, 0)),
                  pl.BlockSpec((4, LANES, LANES), lambda i: (0, 0, 0))],
        out_specs=[pl.BlockSpec((tm, LANES), lambda i: (i, 0))] * 4,
        out_shape=[jax.ShapeDtypeStruct((t, LANES), F32)] * 4,
        compiler_params=_cparams(("arbitrary",)),
        name="rope_tables",
    )(pos, rows, sel)
    return (cm, sm), (cs, ss)


def _proj_kernel(x_ref, w_ref, *out_refs, offsets):
    xb = x_ref[...].astype(BF16)
    for o_ref, (a, b) in zip(out_refs, offsets):
        o_ref[...] = _dot(xb, w_ref[:, a:b]).astype(o_ref.dtype)


def _proj(x, w, widths, dtypes, tm=512):
    t, k = x.shape
    offs = np.concatenate([[0], np.cumsum(widths)]).tolist()
    offsets = tuple((offs[i], offs[i + 1]) for i in range(len(widths)))
    return pl.pallas_call(
        functools.partial(_proj_kernel, offsets=offsets),
        grid=(t // tm,),
        in_specs=[pl.BlockSpec((tm, k), lambda i: (i, 0)), pl.BlockSpec(w.shape, lambda i: (0, 0))],
        out_specs=[pl.BlockSpec((tm, n), lambda i: (i, 0)) for n in widths],
        out_shape=[jax.ShapeDtypeStruct((t, n), dt) for n, dt in zip(widths, dtypes)],
        compiler_params=_cparams(("arbitrary",)),
        name="in_proj",
    )(x, w)


def _proj_even_kernel(x_ref, w_ref, cw_ref, mla_ref, act_ref, z_ref, g_ref, ext_ref, *, tiles_per_seq):
    tm = x_ref.shape[0]
    o = np.concatenate([[0], np.cumsum(EV_WIDTHS)]).tolist()
    @pl.when(pl.program_id(0) % tiles_per_seq == 0)
    def _():
        ext_ref[0:8, :] = jnp.zeros((8, ext_ref.shape[1]), F32)

    xb = x_ref[...].astype(BF16)
    nchunk = 3
    cw = EV_WIDTHS[1] // nchunk

    def project(ci):
        ext_ref[8:8 + tm, ci * cw:(ci + 1) * cw] = _dot(xb, w_ref[:, o[1] + ci * cw:o[1] + (ci + 1) * cw])

    project(0)
    for ci in range(nchunk):
        if ci + 1 < nchunk:
            project(ci + 1)
        else:
            mla_ref[...] = _dot(xb, w_ref[:, o[0]:o[1]])
            z_ref[...] = _dot(xb, w_ref[:, o[2]:o[3]]).astype(z_ref.dtype)
            g_ref[...] = _dot(xb, w_ref[:, o[3]:o[4]])
        cols = slice(ci * cw, (ci + 1) * cw)
        conv = cw_ref[0:1, cols] * ext_ref[5:5 + tm, cols]
        for j in range(1, CONV_W):
            conv = conv + cw_ref[j:j + 1, cols] * ext_ref[5 + j:5 + j + tm, cols]
        act_ref[:, cols] = _silu(conv).astype(act_ref.dtype)
    ext_ref[0:8, :] = ext_ref[tm:tm + 8, :]


def _proj_even(x, w, conv_w, seq, tm=512):
    t, k = x.shape
    tm = min(tm, seq)
    row = lambda i: (i, 0)
    fix = lambda i: (0, 0)
    return pl.pallas_call(
        functools.partial(_proj_even_kernel, tiles_per_seq=seq // tm),
        grid=(t // tm,),
        in_specs=[pl.BlockSpec((tm, k), row), pl.BlockSpec(w.shape, fix), pl.BlockSpec(conv_w.shape, fix)],
        out_specs=[pl.BlockSpec((tm, n), row) for n in EV_WIDTHS],
        out_shape=[jax.ShapeDtypeStruct((t, n), F32) for n in EV_WIDTHS],
        scratch_shapes=[pltpu.VMEM((tm + 8, EV_WIDTHS[1]), F32)],
        compiler_params=_cparams(("arbitrary",)),
        name="in_proj",
    )(x, w, conv_w)


OD_SEG = dict(mq=(0, 512), mk=(512, 1024), mv=(1024, 1536), mo=(1536, 2048), gates=(2048, 2176),
              sq=(2176, 2688), sk=(2688, 2944), sv=(2944, 3456))
OD_COLS = 3456


def _proj_odd_kernel(x_ref, w_ref, c_ref, s_ref, mq_ref, mk_ref, mv_ref, mo_ref, mg_ref, sq_ref, sk_ref, sv_ref):
    xb = x_ref[...].astype(BF16)

    def seg(name):
        a, b = OD_SEG[name]
        return _dot(xb, w_ref[:, a:b])

    mq_ref[...] = seg("mq").astype(mq_ref.dtype)
    mk_ref[...] = seg("mk").astype(mk_ref.dtype)
    mv_ref[...] = seg("mv").astype(mv_ref.dtype)
    mo_ref[...] = seg("mo").astype(mo_ref.dtype)
    mg_ref[...] = seg("gates")
    c = c_ref[...]
    s = s_ref[...]
    def swap_halves(t):
        half = SWA_D // 2
        first_half = (_iota2(t.shape, 1) % SWA_D) < half
        return jnp.where(first_half, pltpu.roll(t, t.shape[1] - half, 1), pltpu.roll(t, half, 1))

    c8 = jnp.concatenate([c] * (SWA_H // 2), axis=1)
    s8 = jnp.concatenate([s] * (SWA_H // 2), axis=1)
    q = seg("sq")
    sq_ref[...] = (q * c8 + swap_halves(q) * s8).astype(sq_ref.dtype)
    c2 = jnp.concatenate([c] * SWA_KV, axis=1)
    s2 = jnp.concatenate([s] * SWA_KV, axis=1)
    k = seg("sk")
    sk_ref[...] = (k * c2 + swap_halves(k) * s2).astype(sk_ref.dtype)
    sv_ref[...] = seg("sv").astype(sv_ref.dtype)


def _proj_odd(x, w, ctab, stab, tm=512):
    t, k = x.shape
    widths = (512, 512, 512, 512, 128, SWA_H * SWA_D, SWA_KV * LANES, 2 * SWA_KV * LANES)
    dtypes = (F32, F32, F32, F32, F32, BF16, BF16, BF16)
    return pl.pallas_call(
        _proj_odd_kernel,
        grid=(t // tm,),
        in_specs=[pl.BlockSpec((tm, k), lambda i: (i, 0)), pl.BlockSpec(w.shape, lambda i: (0, 0)),
                  pl.BlockSpec((tm, LANES), lambda i: (i, 0)), pl.BlockSpec((tm, LANES), lambda i: (i, 0))],
        out_specs=[pl.BlockSpec((tm, n), lambda i: (i, 0)) for n in widths],
        out_shape=[jax.ShapeDtypeStruct((t, n), dt) for n, dt in zip(widths, dtypes)],
        compiler_params=_cparams(("arbitrary",)),
        name="in_proj_odd",
    )(x, w, ctab, stab)


def _rms(x, g):
    return x * lax.rsqrt(jnp.mean(x * x, axis=-1, keepdims=True) + EPS) * g


def _mla_prep_kernel(in_ref, c_ref, s_ref, qn_ref, kvn_ref, wq_ref, wkv_ref, q_ref, k_ref, v_ref):
    hw = MLA_H * LANES
    c = c_ref[...]
    s = s_ref[...]
    c8 = jnp.concatenate([c] * MLA_H, axis=1)
    s8 = jnp.concatenate([s] * MLA_H, axis=1)
    def swap_halves(t):
        half = MLA_ROPE // 2
        first_half = (_iota2(t.shape, 1) % LANES) < MLA_NOPE + half
        return jnp.where(first_half, pltpu.roll(t, t.shape[1] - half, 1), pltpu.roll(t, half, 1))

    cqn = _rms(in_ref[:, 0:Q_LORA], qn_ref[...]).astype(BF16)
    qq = _dot(cqn, wq_ref[...])
    scale = (MLA_NOPE + MLA_ROPE) ** -0.5
    q_ref[...] = ((qq * c8 + swap_halves(qq) * s8) * scale).astype(q_ref.dtype)
    ckvn = _rms(in_ref[:, Q_LORA:Q_LORA + KV_LORA], kvn_ref[...]).astype(BF16)
    kv = _dot(ckvn, wkv_ref[...])
    o = Q_LORA + KV_LORA
    kr = in_ref[:, o:o + LANES]
    krr = kr * c + swap_halves(kr) * s
    k_ref[...] = (kv[:, :hw] + jnp.concatenate([krr] * MLA_H, axis=1)).astype(k_ref.dtype)
    v_ref[...] = kv[:, hw:].astype(v_ref.dtype)


def _mla_prep(mla_in, ctab, stab, qn, kvn, wq2, wkv2, tm=1024):
    t = mla_in.shape[0]
    tm = min(tm, t)
    hw = MLA_H * LANES
    row = lambda i: (i, 0)
    fix = lambda i: (0, 0)
    return pl.pallas_call(
        _mla_prep_kernel,
        grid=(t // tm,),
        in_specs=[pl.BlockSpec((tm, mla_in.shape[1]), row), pl.BlockSpec((tm, LANES), row), pl.BlockSpec((tm, LANES), row),
                  pl.BlockSpec(qn.shape, fix), pl.BlockSpec(kvn.shape, fix),
                  pl.BlockSpec(wq2.shape, fix), pl.BlockSpec(wkv2.shape, fix)],
        out_specs=[pl.BlockSpec((tm, hw), row)] * 3,
        out_shape=[jax.ShapeDtypeStruct((t, hw), BF16)] * 3,
        compiler_params=_cparams(("arbitrary",)),
        name="mla_prep",
    )(mla_in, ctab, stab, qn, kvn, wq2, wkv2)


def _mla_attn_kernel(q_ref, k_ref, v_ref, o_ref, *, tq):
    i = pl.program_id(2)
    neg = -1e30
    lane = _iota2((tq, LANES), 1)
    ones_lane = (MLA_V, 0)

    def chunk(j, carry, masked):
        start = pl.multiple_of(j * tq, tq)
        out = []
        for hh in range(2):
            m, acc = carry[hh]
            q = q_ref[:, hh * LANES:(hh + 1) * LANES]
            kc = k_ref[pl.ds(start, tq), hh * LANES:(hh + 1) * LANES]
            vc = v_ref[pl.ds(start, tq), hh * LANES:(hh + 1) * LANES]
            vc = jnp.where(lane == ones_lane[hh], jnp.ones_like(vc), vc)
            s = _dot_nt(q, kc)
            if masked:
                s = jnp.where(_iota2(s.shape, 0) >= _iota2(s.shape, 1), s, neg)
            m_new = jnp.maximum(m, jnp.max(s, axis=-1, keepdims=True))
            alpha = jnp.exp(m - m_new)
            p = jnp.exp(s - m_new)
            acc = alpha * acc + _dot(p.astype(BF16), vc)
            out.append((m_new, acc))
        return tuple(out)

    one = (jnp.full((tq, 1), neg, F32), jnp.zeros((tq, LANES), F32))
    carry = lax.fori_loop(0, i, lambda j, c: chunk(j, c, False), (one, one))
    (_, acc0), (_, acc1) = chunk(i, carry, True)
    o0 = acc0 / _lane_bcast(acc0, ones_lane[0])
    o1 = acc1 / _lane_bcast(acc1, ones_lane[1])
    o_ref[...] = jnp.where(lane < MLA_V, o0, o1).astype(o_ref.dtype)


def _mla_attn(q, k, v, batch, seq, tq=512):
    tq = min(tq, seq)
    nq = seq // tq
    pairs = MLA_H // 2
    return pl.pallas_call(
        functools.partial(_mla_attn_kernel, tq=tq),
        grid=(batch, pairs, nq),
        in_specs=[pl.BlockSpec((tq, 2 * LANES), lambda b, p, i: (b * nq + i, p)),
                  pl.BlockSpec((seq, 2 * LANES), lambda b, p, i: (b, p)),
                  pl.BlockSpec((seq, 2 * LANES), lambda b, p, i: (b, p))],
        out_specs=pl.BlockSpec((tq, LANES), lambda b, p, i: (b * nq + i, p)),
        out_shape=jax.ShapeDtypeStruct((batch * seq, pairs * LANES), BF16),
        compiler_params=_cparams(("arbitrary", "arbitrary", "arbitrary")),
        name="mla_attn",
    )(q, k, v)


def _unit_lower_inverse_many(ns):
    c = ns[0].shape[0]
    eye = (_iota2((c, c), 0) == _iota2((c, c), 1)).astype(F32)
    xs = [-n for n in ns]
    ps = [eye + x for x in xs]
    xb = [x.astype(BF16) for x in xs]
    for _ in range(int(math.log2(c)) - 1):
        xs = [_dot(b, b) for b in xb]
        xb = [x.astype(BF16) for x in xs]
        ps = [p + _dot(p.astype(BF16), b) for p, b in zip(ps, xb)]
    return ps


def _gdn_kernel(act_ref, g_ref, z_ref, al_ref, dt_ref, on_ref, o_ref, st_ref):
    c = GDN_CHUNK
    hd = GDN_DK
    nqk = GDN_H * GDN_DK

    @pl.when(pl.program_id(1) == 0)
    def _():
        st_ref[...] = jnp.zeros(st_ref.shape, F32)

    tri = (_iota2((c, c), 0) >= _iota2((c, c), 1)).astype(F32)
    row_ge = _iota2((c, c), 0) >= _iota2((c, c), 1)
    row_gt = _iota2((c, c), 0) > _iota2((c, c), 1)
    lane = _iota2((c, LANES), 1)

    seqs = []
    for bb in range(act_ref.shape[0]):
        gates = g_ref[bb]
        g_all = -jnp.exp(al_ref[...]) * _softplus(gates + dt_ref[...])
        gc_all = _dot_sel(tri, g_all)
        seqs.append(dict(beta_all=_sigmoid(gates), gc_all=gc_all, gc_parts=_split3(gc_all)))
    units = []
    for bb, sq in enumerate(seqs):
        for h in range(GDN_H):
            q = act_ref[bb, :, h * hd:(h + 1) * hd].astype(F32)
            k = act_ref[bb, :, nqk + h * hd:nqk + (h + 1) * hd].astype(F32)
            v = act_ref[bb, :, 2 * nqk + h * GDN_DV:2 * nqk + (h + 1) * GDN_DV].astype(F32)
            q = q * lax.rsqrt(jnp.sum(q * q, axis=-1, keepdims=True) + EPS) * (GDN_DK ** -0.5)
            k = k * lax.rsqrt(jnp.sum(k * k, axis=-1, keepdims=True) + EPS)
            beta = _lane_bcast(sq["beta_all"], h)
            gcol = _lane_bcast(sq["gc_all"], GDN_H + h)
            units.append(dict(bb=bb, h=h, q=q, k=k, v=v, beta=beta, gcol=gcol, kb=k * beta, parts=sq["gc_parts"]))
    for u in units:
        pick = (lane == GDN_H + u["h"]).astype(BF16)
        p0, p1, p2 = u["parts"]
        u["grow"] = _dot_nt(pick, p0) + (_dot_nt(pick, p1) + _dot_nt(pick, p2))
        u["kk"] = _dot3(u["kb"], u["k"], _dot_nt)
        u["qk"] = _dot_nt(u["q"].astype(BF16), u["k"].astype(BF16))
    for u in units:
        gcol = u["gcol"]
        decay = jnp.exp(jnp.where(row_ge, gcol[:, :c] - u["grow"], -jnp.inf))
        eg = jnp.exp(gcol)
        glast = gcol[c - 1:c, :]
        u["lower"] = jnp.where(row_gt, u["kk"] * decay, 0.0)
        u["rhs"] = jnp.concatenate([u["v"] * u["beta"], u["kb"] * eg], axis=1)
        u["attn"] = u["qk"] * decay
        u["qg"] = (u["q"] * eg).astype(BF16)
        u["kg"] = (u["k"] * jnp.exp(glast - gcol)).astype(BF16)
        u["gl"] = jnp.exp(glast)

    tinvs = _unit_lower_inverse_many([u["lower"] for u in units])
    uws = []
    for u, tinv in zip(units, tinvs):
        uws.append(_dot(tinv.astype(BF16), u["rhs"].astype(BF16)))
    states = [st_ref[u["bb"], u["h"]] for u in units]
    sbs = [s.astype(BF16) for s in states]
    vnews = [(uw[:, :GDN_DV] - _dot(uw[:, GDN_DV:].astype(BF16), sb)).astype(BF16) for uw, sb in zip(uws, sbs)]
    for u, state, sb, vnb in zip(units, states, sbs, vnews):
        bb, h = u["bb"], u["h"]
        o = _dot(u["qg"], sb) + _dot(u["attn"].astype(BF16), vnb)
        st_ref[bb, h] = state * u["gl"] + _dot_tn(u["kg"], vnb)
        o = _rms(o, on_ref[...]) * _silu(z_ref[bb, :, h * GDN_DV:(h + 1) * GDN_DV].astype(F32))
        o_ref[bb, :, h * GDN_DV:(h + 1) * GDN_DV] = o.astype(o_ref.dtype)


def _gdn(act, gates, z, a_row, dt_row, o_norm, batch, seq):
    c = GDN_CHUNK
    nc = seq // c
    w3 = act.shape[1]
    wo = GDN_H * GDN_DV
    nb = min(GDN_SEQS_PER_STEP, batch)
    row = lambda b, i: (b, i, 0)
    fix = lambda b, i: (0, 0)
    out = pl.pallas_call(
        _gdn_kernel,
        grid=(batch // nb, nc),
        in_specs=[pl.BlockSpec((nb, c, w3), row), pl.BlockSpec((nb, c, LANES), row), pl.BlockSpec((nb, c, wo), row),
                  pl.BlockSpec((1, LANES), fix), pl.BlockSpec((1, LANES), fix), pl.BlockSpec((1, GDN_DV), fix)],
        out_specs=pl.BlockSpec((nb, c, wo), row),
        out_shape=jax.ShapeDtypeStruct((batch, seq, wo), BF16),
        scratch_shapes=[pltpu.VMEM((nb, GDN_H, GDN_DK, GDN_DV), F32)],
        compiler_params=_cparams(("arbitrary", "arbitrary")),
        name="gdn",
    )(act.reshape(batch, seq, w3), gates.reshape(batch, seq, LANES), z.reshape(batch, seq, wo), a_row, dt_row, o_norm)
    return out.reshape(batch * seq, wo)


def _mlstm_kernel(q_ref, k_ref, v_ref, og_ref, g_ref, bias_ref, nrm_ref, o_ref, c_ref, n_ref, m_ref):
    @pl.when(pl.program_id(1) == 0)
    def _():
        c_ref[...] = jnp.zeros(c_ref.shape, F32)
        n_ref[...] = jnp.zeros(n_ref.shape, F32)
        m_ref[...] = jnp.zeros(m_ref.shape, F32)

    c = ML_CHUNK
    tri = (_iota2((c, c), 0) >= _iota2((c, c), 1)).astype(F32)
    row_ge = _iota2((c, c), 0) >= _iota2((c, c), 1)
    ones = jnp.ones((c, LANES), F32)
    lane = _iota2((c, LANES), 1)

    units = []
    for bb in range(q_ref.shape[0]):
        pre = g_ref[bb] + bias_ref[...]
        logf = jnp.minimum(pre, 0.0) - jnp.log(1.0 + jnp.exp(-jnp.abs(pre)))
        bcum_all = _dot_sel(tri, logf)
        for h in range(ML_H):
            q = q_ref[bb, :, h * LANES:(h + 1) * LANES].astype(F32)
            k = k_ref[bb, :, h * LANES:(h + 1) * LANES].astype(F32) * (ML_DK ** -0.5)
            units.append(dict(bb=bb, h=h, q=q, k=k, qb=q.astype(BF16), vb=v_ref[bb, :, h * ML_DV:(h + 1) * ML_DV].astype(BF16),
                              bcol=_lane_bcast(bcum_all, ML_H + h),
                              icol=_lane_bcast(pre, h),
                              col=jnp.where(lane == h, pre, 0.0) - jnp.where(lane == ML_H + h, bcum_all, 0.0),
                              m_st=m_ref[bb, h], cst=c_ref[bb, h], nst=n_ref[bb, h]))
    for u in units:
        u["row"] = _dot_sel(ones, u["col"], _dot_nt)
        u["qk"] = _dot_nt(u["qb"], u["k"].astype(BF16))
        u["qc"] = _dot(u["qb"], u["cst"].astype(BF16))
    for u in units:
        u["d"] = jnp.where(row_ge, u["bcol"][:, :c] + u["row"], -jnp.inf)
        u["inter"] = u["bcol"] + u["m_st"]
        u["m_t"] = jnp.maximum(u["inter"], jnp.max(u["d"], axis=-1, keepdims=True))
        u["b_end"] = u["bcol"][c - 1:c, :]
        u["a"] = u["b_end"] - u["bcol"] + u["icol"]
        u["m_new"] = jnp.maximum(u["b_end"] + u["m_st"], jnp.max(u["a"], axis=0, keepdims=True))
    for u in units:
        u["w_inter"] = jnp.exp(u["inter"] - u["m_t"])
        u["p"] = jnp.exp(u["d"] - u["m_t"][:, :c]) * u["qk"]
        u["keep"] = jnp.exp(u["b_end"] + u["m_st"] - u["m_new"])
        u["ks"] = u["k"] * jnp.exp(u["a"] - u["m_new"])
    for u in units:
        u["pv"] = _dot(u["p"].astype(BF16), u["vb"])
        u["kv"] = _dot_tn(u["ks"].astype(BF16), u["vb"])
    for u in units:
        u["den"] = (u["w_inter"] * jnp.sum(u["q"] * u["nst"], axis=-1, keepdims=True)
                    + jnp.sum(u["p"], axis=-1, keepdims=True))
    for u in units:
        bb, h = u["bb"], u["h"]
        num = u["w_inter"] * u["qc"] + u["pv"]
        hc = num / jnp.maximum(jnp.abs(u["den"]), jnp.exp(-u["m_t"]))
        c_ref[bb, h] = u["cst"] * u["keep"] + u["kv"]
        n_ref[bb, h] = u["nst"] * u["keep"] + jnp.sum(u["ks"], axis=0, keepdims=True)
        m_ref[bb, h] = u["m_new"]
        hn = (_rms(hc, nrm_ref[:, h * ML_DV:(h + 1) * ML_DV])
              * _sigmoid(og_ref[bb, :, h * ML_DV:(h + 1) * ML_DV].astype(F32)))
        o_ref[bb, :, h * ML_DV:(h + 1) * ML_DV] = hn.astype(o_ref.dtype)


def _mlstm(mq, mk, mv, mo, gates, bias_row, norm_row, batch, seq):
    c = ML_CHUNK
    nc = seq // c
    nb = min(MLSTM_SEQS_PER_STEP, batch)
    row = lambda b, i: (b, i, 0)
    fix = lambda b, i: (0, 0)
    wide = ML_H * LANES
    r3 = lambda a: a.reshape(batch, seq, a.shape[-1])
    out = pl.pallas_call(
        _mlstm_kernel,
        grid=(batch // nb, nc),
        in_specs=[pl.BlockSpec((nb, c, wide), row), pl.BlockSpec((nb, c, wide), row), pl.BlockSpec((nb, c, wide), row),
                  pl.BlockSpec((nb, c, wide), row), pl.BlockSpec((nb, c, LANES), row),
                  pl.BlockSpec((1, LANES), fix), pl.BlockSpec((1, wide), fix)],
        out_specs=pl.BlockSpec((nb, c, wide), row),
        out_shape=jax.ShapeDtypeStruct((batch, seq, wide), BF16),
        scratch_shapes=[pltpu.VMEM((nb, ML_H, LANES, ML_DV), F32), pltpu.VMEM((nb, ML_H, 1, LANES), F32),
                        pltpu.VMEM((nb, ML_H, 1, LANES), F32)],
        compiler_params=_cparams(("arbitrary", "arbitrary")),
        name="mlstm",
    )(r3(mq), r3(mk), r3(mv), r3(mo), r3(gates), bias_row, norm_row)
    return out.reshape(batch * seq, wide)


def _swa_kernel(q_ref, kc_ref, kp_ref, vc_ref, vp_ref, sink_ref, o_ref):
    w = WINDOW
    n = pl.program_id(1)
    scale = SWA_D ** -0.5
    qi = _iota2((w, w), 0)
    kj = _iota2((w, w), 1)
    mask_c = kj <= qi
    mask_p = jnp.logical_and(kj > qi, n > 0)
    grp = SWA_H // SWA_KV
    neg = -1e30
    units = [(bb, h) for bb in range(q_ref.shape[0]) for h in range(SWA_H)]
    scores = []
    half_of_lane = _iota2((w, LANES), 1) // SWA_D
    for bb, h in units:
        g = h // grp
        pair = q_ref[bb, :, (h // 2) * LANES:(h // 2 + 1) * LANES]
        q = jnp.where(half_of_lane == h % 2, pair, jnp.zeros_like(pair))
        scores.append((_dot_nt(q, kc_ref[bb, :, g * LANES:(g + 1) * LANES]),
                       _dot_nt(q, kp_ref[bb, :, g * LANES:(g + 1) * LANES])))
    masked, tops, exps, dens, probs = [], [], [], [], {}
    for sc, sp in scores:
        masked.append((jnp.where(mask_c, sc * scale, neg), jnp.where(mask_p, sp * scale, neg)))
    for (bb, h), (s_c, s_p) in zip(units, masked):
        tops.append(jnp.maximum(jnp.max(jnp.maximum(s_c, s_p), axis=-1, keepdims=True), sink_ref[:, h:h + 1]))
    for (s_c, s_p), m in zip(masked, tops):
        exps.append((jnp.where(mask_c, jnp.exp(s_c - m), 0.0), jnp.where(mask_p, jnp.exp(s_p - m), 0.0)))
    ones_b = jnp.ones((w, LANES), BF16)
    for (bb, h), (p_c, p_p), m in zip(units, exps, tops):
        p_c, p_p = p_c.astype(BF16), p_p.astype(BF16)
        probs[bb, h] = (p_c, p_p)
        dens.append(_dot(p_c, ones_b) + _dot(p_p, ones_b) + jnp.exp(sink_ref[:, h:h + 1] - m))
    inv = {u: 1.0 / den for u, den in zip(units, dens)}
    for bb in range(q_ref.shape[0]):
        for pair in range(SWA_H // 2):
            acc = None
            for sub in range(2):
                h = 2 * pair + sub
                vcol = (2 * (h // grp) + sub) * LANES
                p_c, p_p = probs[bb, h]
                part = (_dot(p_c, vc_ref[bb, :, vcol:vcol + LANES]) + _dot(p_p, vp_ref[bb, :, vcol:vcol + LANES])) * inv[bb, h]
                acc = part if acc is None else acc + part
            o_ref[bb, :, pair * LANES:(pair + 1) * LANES] = acc.astype(o_ref.dtype)


def _swa(sq, sk, sv, sinks_row, batch, seq):
    w = WINDOW
    nb = seq // w
    ns = min(SWA_SEQS_PER_STEP, batch)
    wo = SWA_H * SWA_D
    cur = lambda b, n: (b, n, 0)
    prev = lambda b, n: (b, jnp.maximum(n - 1, 0), 0)
    r3 = lambda a: a.reshape(batch, seq, a.shape[-1])
    q3, k3, v3 = r3(sq), r3(sk), r3(sv)
    out = pl.pallas_call(
        _swa_kernel,
        grid=(batch // ns, nb),
        in_specs=[pl.BlockSpec((ns, w, sq.shape[1]), cur),
                  pl.BlockSpec((ns, w, sk.shape[1]), cur), pl.BlockSpec((ns, w, sk.shape[1]), prev),
                  pl.BlockSpec((ns, w, sv.shape[1]), cur), pl.BlockSpec((ns, w, sv.shape[1]), prev),
                  pl.BlockSpec((1, LANES), lambda b, n: (0, 0))],
        out_specs=pl.BlockSpec((ns, w, wo), cur),
        out_shape=jax.ShapeDtypeStruct((batch, seq, wo), BF16),
        compiler_params=_cparams(("arbitrary", "arbitrary")),
        name="swa",
    )(q3, k3, k3, v3, v3, sinks_row)
    return out.reshape(batch * seq, wo)


def _layer_norm(h, g, b):
    mu = jnp.mean(h, axis=-1, keepdims=True)
    d = h - mu
    var = jnp.mean(d * d, axis=-1, keepdims=True)
    return d * lax.rsqrt(var + LN_EPS) * g + b


def _outproj_kernel(x_ref, a1_ref, a2_ref, w_ref, g_ref, b_ref, o_ref, op_ref):
    k1 = a1_ref.shape[1]
    y = _dot(a1_ref[...].astype(BF16), w_ref[0:k1, :]) + _dot(a2_ref[...].astype(BF16), w_ref[k1:, :])
    h = _layer_norm(DN_ALPHA * x_ref[...] + y, g_ref[...], b_ref[...])
    o_ref[...] = h
    op_ref[...] = _pack_pairs(h)


def _outproj_ln(x, a1, a2, w, g, b, tm=1024):
    t, d = x.shape
    tm = min(tm, t)
    row = lambda i: (i, 0)
    fix = lambda i: (0, 0)
    return pl.pallas_call(
        _outproj_kernel,
        grid=(t // tm,),
        in_specs=[pl.BlockSpec((tm, d), row), pl.BlockSpec((tm, a1.shape[1]), row), pl.BlockSpec((tm, a2.shape[1]), row),
                  pl.BlockSpec(w.shape, fix), pl.BlockSpec((1, d), fix), pl.BlockSpec((1, d), fix)],
        out_specs=[pl.BlockSpec((tm, d), row), pl.BlockSpec((tm, d // 2), row)],
        out_shape=[jax.ShapeDtypeStruct((t, d), F32), jax.ShapeDtypeStruct((t, d // 2), jnp.uint32)],
        compiler_params=_cparams(("arbitrary",)),
        name="outproj_ln",
    )(x, a1, a2, w, g, b)


def _first_index(x, m, iota_f, sentinel):
    return jnp.min(jnp.where(x == m, iota_f, sentinel), axis=0, keepdims=True)


def _router_kernel(x_ref, wt_ref, bias_ref, idx_ref, gate_ref, rank_ref, cnt_ref, carry_ref):
    tm = x_ref.shape[0]
    e = N_EXPERTS
    gs = e // N_GROUPS
    ninf = -jnp.inf

    @pl.when(pl.program_id(0) == 0)
    def _():
        carry_ref[...] = jnp.zeros(carry_ref.shape, F32)

    logits = _dot3(wt_ref[...], x_ref[...], _dot_nt)
    scores = _sigmoid(logits)
    sel = scores + bias_ref[:, 0:1]

    sub_f = _iota2((gs, tm), 0).astype(F32)
    gscore = []
    for g in range(N_GROUPS):
        blk = sel[g * gs:(g + 1) * gs, :]
        m1 = jnp.max(blk, axis=0, keepdims=True)
        i1 = _first_index(blk, m1, sub_f, float(gs))
        m2 = jnp.max(jnp.where(sub_f == i1, ninf, blk), axis=0, keepdims=True)
        gscore.append(m1 + m2)
    gsc = jnp.concatenate(gscore, axis=0)
    grp_f = _iota2((N_GROUPS, tm), 0).astype(F32)
    gmask = jnp.zeros((N_GROUPS, tm), F32)
    for _ in range(TOPK_GROUPS):
        m = jnp.max(gsc, axis=0, keepdims=True)
        gi = _first_index(gsc, m, grp_f, float(N_GROUPS))
        hit = grp_f == gi
        gmask = jnp.where(hit, 1.0, gmask)
        gsc = jnp.where(hit, ninf, gsc)
    masked = jnp.concatenate(
        [jnp.where(gmask[g:g + 1, :] > 0.0, sel[g * gs:(g + 1) * gs, :], ninf) for g in range(N_GROUPS)], axis=0)

    exp_f = _iota2((e, tm), 0).astype(F32)
    chosen = jnp.zeros((e, tm), F32)
    idxs, gates = [], []
    for _ in range(TOP_K):
        m = jnp.max(masked, axis=0, keepdims=True)
        ei = _first_index(masked, m, exp_f, float(e))
        hit = exp_f == ei
        idxs.append(ei)
        gates.append(jnp.sum(jnp.where(hit, scores, 0.0), axis=0, keepdims=True))
        chosen = jnp.where(hit, 1.0, chosen)
        masked = jnp.where(hit, ninf, masked)
    gate = jnp.concatenate(gates, axis=0)
    gate = gate / jnp.sum(gate, axis=0, keepdims=True) * ROUTED_SCALE
    idx_f = jnp.concatenate(idxs, axis=0)

    upper = (_iota2((tm, tm), 0) < _iota2((tm, tm), 1)).astype(BF16)
    before = _dot(chosen.astype(BF16), upper) + carry_ref[...][:, 0:1]
    ranks = [jnp.sum(jnp.where(exp_f == idxs[k], before, 0.0), axis=0, keepdims=True) for k in range(TOP_K)]
    carry_ref[...] = carry_ref[...] + jnp.sum(chosen, axis=1, keepdims=True)

    idx_ref[...] = idx_f.astype(jnp.int32)
    gate_ref[...] = gate
    rank_ref[...] = jnp.concatenate(ranks, axis=0).astype(jnp.int32)
    cnt_ref[...] = carry_ref[...]


def _router(x, wt, bias_col, tm=512):
    t, d = x.shape
    col = lambda i: (0, i)
    fix = lambda i: (0, 0)
    return pl.pallas_call(
        _router_kernel,
        grid=(t // tm,),
        in_specs=[pl.BlockSpec((tm, d), lambda i: (i, 0)), pl.BlockSpec(wt.shape, fix), pl.BlockSpec((N_EXPERTS, LANES), fix)],
        out_specs=[pl.BlockSpec((TOP_K, tm), col), pl.BlockSpec((TOP_K, tm), col), pl.BlockSpec((TOP_K, tm), col),
                   pl.BlockSpec((N_EXPERTS, LANES), fix)],
        out_shape=[jax.ShapeDtypeStruct((TOP_K, t), jnp.int32), jax.ShapeDtypeStruct((TOP_K, t), F32),
                   jax.ShapeDtypeStruct((TOP_K, t), jnp.int32), jax.ShapeDtypeStruct((N_EXPERTS, LANES), F32)],
        scratch_shapes=[pltpu.VMEM((N_EXPERTS, LANES), F32)],
        compiler_params=_cparams(("arbitrary",)),
        name="router",
    )(x, wt, bias_col)


def _dest_kernel(idx_ref, rank_ref, start_ref, dest_ref):
    tm = idx_ref.shape[1]
    exp_i = _iota2((N_EXPERTS, tm), 0)
    start = start_ref[:, 0:1]
    rows = [jnp.sum(jnp.where(exp_i == idx_ref[s:s + 1, :], start, 0.0), axis=0, keepdims=True) for s in range(TOP_K)]
    dest_ref[...] = jnp.concatenate(rows, axis=0).astype(jnp.int32) + rank_ref[...]


def _dest_rows(idx, rank, start_col, tm=2048):
    t = idx.shape[1]
    tm = min(tm, t)
    col = lambda i: (0, i)
    return pl.pallas_call(
        _dest_kernel,
        grid=(t // tm,),
        in_specs=[pl.BlockSpec((TOP_K, tm), col), pl.BlockSpec((TOP_K, tm), col),
                  pl.BlockSpec((N_EXPERTS, LANES), lambda i: (0, 0))],
        out_specs=pl.BlockSpec((TOP_K, tm), col),
        out_shape=jax.ShapeDtypeStruct((TOP_K, t), jnp.int32),
        compiler_params=_cparams(("arbitrary",)),
        name="moe_dest",
    )(idx, rank, start_col)


def _pack_pairs(x):
    n = x.shape[1] // 2
    hi = lax.bitcast_convert_type(x[:, :n].astype(BF16).astype(F32), jnp.uint32)
    lo = lax.bitcast_convert_type(x[:, n:].astype(BF16).astype(F32), jnp.uint32)
    return hi | (lo >> 16)


def _unpack_pairs(w):
    hi = lax.bitcast_convert_type(w & jnp.uint32(0xFFFF0000), F32)
    lo = lax.bitcast_convert_type(w << 16, F32)
    return hi, lo


def _sc_scatter_rows(xp, dest, rows, chunk=LANES):
    t, width = xp.shape
    info = plsc.get_sparse_core_info()
    ncores, nsub = info.num_cores, info.num_subcores
    per_worker = t // (ncores * nsub)
    nchunk = per_worker // chunk
    mesh = plsc.VectorSubcoreMesh(core_axis_name="c", subcore_axis_name="s")

    @functools.partial(
        pl.kernel, mesh=mesh,
        out_type=jax.ShapeDtypeStruct((rows, width), xp.dtype),
        scratch_types=[pltpu.VMEM((TOP_K, chunk), jnp.int32), pltpu.VMEM((chunk, width), xp.dtype), pltpu.SemaphoreType.DMA],
    )
    def scatter(xp_hbm, dest_hbm, out_hbm, idx_v, rows_v, sem):
        base = (lax.axis_index("s") * ncores + lax.axis_index("c")) * per_worker

        @pl.loop(0, nchunk)
        def _(i):
            off = pl.multiple_of(base + i * chunk, chunk)
            pltpu.sync_copy(dest_hbm.at[:, pl.ds(off, chunk)], idx_v)
            pltpu.sync_copy(xp_hbm.at[pl.ds(off, chunk)], rows_v)
            copies = [pltpu.async_copy(rows_v, out_hbm.at[idx_v.at[s]], sem) for s in range(TOP_K)]
            for cp in copies:
                cp.wait()

    return scatter(xp, dest)


def _experts_kernel(be_ref, nu_ref, nv_ref, first_ref, slot_ref, nxt_ref, xs_ref, wg_hbm, wu_hbm, wd_hbm, ys_ref,
                    wgf_ref, wuf_ref, wdf_ref, wgb_ref, wub_ref, wdb_ref, sem, *, layer):
    i = pl.program_id(0)

    def fetch(e, s):
        return [pltpu.make_async_copy(wg_hbm.at[layer, e], wgf_ref.at[s], sem.at[s]),
                pltpu.make_async_copy(wu_hbm.at[layer, e], wuf_ref.at[s], sem.at[s]),
                pltpu.make_async_copy(wd_hbm.at[layer, e], wdf_ref.at[s], sem.at[s])]

    @pl.when(i == 0)
    def _():
        for cp in fetch(be_ref[0], 0):
            cp.start()

    @pl.when(jnp.logical_and(first_ref[i] == 1, i < nu_ref[0]))
    def _():
        s = slot_ref[i]
        for cp in fetch(be_ref[i], s):
            cp.wait()
        wgb_ref[...] = wgf_ref[s].astype(BF16)
        wub_ref[...] = wuf_ref[s].astype(BF16)
        wdb_ref[...] = wdf_ref[s].astype(BF16)

        @pl.when(nxt_ref[i] >= 0)
        def _():
            for cp in fetch(nxt_ref[i], 1 - s):
                cp.start()

    @pl.when(i < nu_ref[0])
    def _():
        sub = xs_ref.shape[0] // EXPERT_SUBBLOCKS
        acts = []
        for r in range(EXPERT_SUBBLOCKS):
            rows = pl.ds(r * sub, sub)
            live = (_iota2((sub, 1), 0) + r * sub) < nv_ref[i]
            xa, xb = _unpack_pairs(jnp.where(live, xs_ref[rows, :], jnp.uint32(0)))
            x = jnp.concatenate([xa.astype(BF16), xb.astype(BF16)], axis=1)
            acts.append((_dot(x, wgb_ref[...]), _dot(x, wub_ref[...])))
        outs = [_dot((_silu(gate) * up).astype(BF16), wdb_ref[...]) for gate, up in acts]
        for r, y in enumerate(outs):
            ys_ref[pl.ds(r * sub, sub), :] = _pack_pairs(y)


def _experts(block_e, n_used, n_valid, xs, wg, wu, wd, layer, block):
    rows, half = xs.shape
    d = 2 * half
    nb = rows // block
    pos = jnp.arange(nb, dtype=jnp.int32)
    first = jnp.concatenate([jnp.ones((1,), jnp.int32), (block_e[1:] != block_e[:-1]).astype(jnp.int32)])
    slot = (jnp.cumsum(first) - 1) % 2
    later = (pos[None, :] > pos[:, None]) & (block_e[None, :] != block_e[:, None]) & (pos[None, :] < n_used[0])
    nxt_pos = jnp.min(jnp.where(later, pos[None, :], nb), axis=1)
    nxt = jnp.where(nxt_pos < nb, block_e[jnp.minimum(nxt_pos, nb - 1)], -1)
    blk = lambda i, be, nu, *rest: (jnp.minimum(i, nu[0] - 1), 0)
    hbm = pl.BlockSpec(memory_space=pl.ANY)
    return pl.pallas_call(
        functools.partial(_experts_kernel, layer=layer),
        grid_spec=pltpu.PrefetchScalarGridSpec(
            num_scalar_prefetch=6,
            grid=(nb,),
            in_specs=[pl.BlockSpec((block, half), blk), hbm, hbm, hbm],
            out_specs=pl.BlockSpec((block, half), blk),
            scratch_shapes=[pltpu.VMEM((2, d, D_EXPERT), F32), pltpu.VMEM((2, d, D_EXPERT), F32),
                            pltpu.VMEM((2, D_EXPERT, d), F32),
                            pltpu.VMEM((d, D_EXPERT), BF16), pltpu.VMEM((d, D_EXPERT), BF16),
                            pltpu.VMEM((D_EXPERT, d), BF16), pltpu.SemaphoreType.DMA((2,))],
        ),
        out_shape=jax.ShapeDtypeStruct((rows, half), jnp.uint32),
        compiler_params=_cparams(("arbitrary",)),
        name="moe_experts",
    )(block_e, n_used, n_valid, first, slot.astype(jnp.int32), nxt.astype(jnp.int32), xs, wg, wu, wd)


def _sc_gather_rows(table, idx, chunk=SC_CHUNK):
    n = idx.shape[0]
    width = table.shape[1]
    info = plsc.get_sparse_core_info()
    ncores, nsub = info.num_cores, info.num_subcores
    per_worker = n // (ncores * nsub)
    nchunk = per_worker // chunk
    mesh = plsc.VectorSubcoreMesh(core_axis_name="c", subcore_axis_name="s")

    @functools.partial(
        pl.kernel, mesh=mesh,
        out_type=jax.ShapeDtypeStruct((n, width), table.dtype),
        scratch_types=[pltpu.VMEM((nchunk, chunk), jnp.int32), pltpu.VMEM((2, chunk, width), table.dtype),
                       pltpu.SemaphoreType.DMA((2,)), pltpu.SemaphoreType.DMA((2,))],
    )
    def gather(table_hbm, idx_hbm, out_hbm, idx_v, rows_v, gsem, wsem):
        wid = lax.axis_index("s") * ncores + lax.axis_index("c")
        base = wid * per_worker
        pltpu.sync_copy(idx_hbm.at[pl.ds(wid * nchunk, nchunk)], idx_v)

        def fetch(j, b):
            return pltpu.make_async_copy(table_hbm.at[idx_v.at[j]], rows_v.at[b], gsem.at[b])

        def flush(j, b):
            off = pl.multiple_of(base + j * chunk, chunk)
            return pltpu.make_async_copy(rows_v.at[b], out_hbm.at[pl.ds(off, chunk)], wsem.at[b])

        fetch(0, 0).start()

        @pl.loop(0, nchunk, step=2)
        def _(i):
            for b in range(2):
                j = i + b
                fetch(j, b).wait()

                @pl.when(j + 1 < nchunk)
                def _():
                    @pl.when(j >= 1)
                    def _():
                        flush(j - 1, 1 - b).wait()

                    fetch(j + 1, 1 - b).start()

                flush(j, b).start()

        flush(nchunk - 2, 0).wait()
        flush(nchunk - 1, 1).wait()

    return gather(table, idx.reshape(n // chunk, chunk))


def _shared_kernel(xp_ref, sg_ref, su_ref, sd_ref, o_ref):
    xa, xb = _unpack_pairs(xp_ref[...])
    x = jnp.concatenate([xa.astype(BF16), xb.astype(BF16)], axis=1)
    hs = _silu(_dot(x, sg_ref[...])) * _dot(x, su_ref[...])
    o_ref[...] = _pack_pairs(_dot(hs.astype(BF16), sd_ref[...]))


def _shared_expert(xp, sg, su, sd, tm=512):
    t, half = xp.shape
    row = lambda i: (i, 0)
    fix = lambda i: (0, 0)
    return pl.pallas_call(
        _shared_kernel,
        grid=(t // tm,),
        in_specs=[pl.BlockSpec((tm, half), row), pl.BlockSpec(sg.shape, fix), pl.BlockSpec(su.shape, fix),
                  pl.BlockSpec(sd.shape, fix)],
        out_specs=pl.BlockSpec((tm, half), row),
        out_shape=jax.ShapeDtypeStruct((t, half), jnp.uint32),
        compiler_params=_cparams(("arbitrary",)),
        name="moe_shared",
    )(xp, sg, su, sd)


def _combine_kernel(x_ref, gate_ref, rows_ref, sh_ref, g_ref, b_ref, o_ref):
    gate = gate_ref[...]
    ya, yb = _unpack_pairs(sh_ref[...])
    for s in range(TOP_K):
        a, b = _unpack_pairs(rows_ref[s])
        ya = ya + gate[:, s:s + 1] * a
        yb = yb + gate[:, s:s + 1] * b
    ff = jnp.concatenate([ya, yb], axis=1)
    o_ref[...] = _layer_norm(DN_ALPHA * x_ref[...] + ff, g_ref[...], b_ref[...])


def _combine(x, gate_t, rows, shared, g, b, tm=512):
    t, d = x.shape
    row = lambda i: (i, 0)
    fix = lambda i: (0, 0)
    return pl.pallas_call(
        _combine_kernel,
        grid=(t // tm,),
        in_specs=[pl.BlockSpec((tm, d), row), pl.BlockSpec((tm, TOP_K), row),
                  pl.BlockSpec((TOP_K, tm, d // 2), lambda i: (0, i, 0)), pl.BlockSpec((tm, d // 2), row),
                  pl.BlockSpec((1, d), fix), pl.BlockSpec((1, d), fix)],
        out_specs=pl.BlockSpec((tm, d), row),
        out_shape=jax.ShapeDtypeStruct((t, d), F32),
        compiler_params=_cparams(("arbitrary",)),
        name="moe_combine",
    )(x, gate_t, rows, shared, g, b)


def _take_cols(w, idx):
    idx = np.asarray(idx)
    runs, start = [], 0
    for pos in range(1, len(idx) + 1):
        run_ends = pos == len(idx) or (idx[pos] != idx[pos - 1] + 1 if idx[pos - 1] >= 0 else idx[pos] >= 0)
        if run_ends:
            runs.append((start, int(idx[start]), pos - start))
            start = pos

    def body(w_ref, o_ref):
        for dst, src, width in runs:
            if src < 0:
                o_ref[:, dst:dst + width] = jnp.zeros((o_ref.shape[0], width), o_ref.dtype)
            else:
                o_ref[:, dst:dst + width] = w_ref[:, src:src + width].astype(o_ref.dtype)

    rows = w.shape[0]
    tr = min(rows, 256)
    return pl.pallas_call(
        body,
        grid=(rows // tr,),
        in_specs=[pl.BlockSpec((tr, w.shape[1]), lambda i: (i, 0))],
        out_specs=pl.BlockSpec((tr, len(idx)), lambda i: (i, 0)),
        out_shape=jax.ShapeDtypeStruct((rows, len(idx)), BF16),
        compiler_params=_cparams(("arbitrary",)),
        name="weight_cols",
    )(w)


def _pad_lane_row(v, first_lane, width=LANES):
    out = jnp.zeros((1, width), F32)
    return lax.dynamic_update_slice(out, v.reshape(1, -1).astype(F32), (0, first_lane))


def _even_in_cols():
    z = lambda n: -np.ones(n, int)
    kr0 = Q_LORA + KV_LORA
    half = MLA_ROPE // 2
    cols = [np.arange(0, Q_LORA), np.arange(Q_LORA, Q_LORA + KV_LORA),
            z(MLA_NOPE), np.arange(kr0, kr0 + MLA_ROPE), z(LANES - MLA_NOPE - MLA_ROPE)]
    g0 = kr0 + MLA_ROPE
    nqk = GDN_H * GDN_DK
    cols.append(np.arange(g0, g0 + 3 * nqk))
    zoff = g0 + 3 * nqk + 2 * GDN_H
    cols.append(np.arange(zoff, zoff + GDN_H * GDN_DV))
    cols += [np.arange(g0 + 3 * nqk, g0 + 3 * nqk + 2 * GDN_H), z(LANES - 2 * GDN_H)]
    return np.concatenate(cols)


EV_WIDTHS = (Q_LORA + KV_LORA + LANES, 3 * GDN_H * GDN_DK, GDN_H * GDN_DV, LANES)


def _mla_q_cols():
    per = MLA_NOPE + MLA_ROPE
    main = []
    for h in range(MLA_H):
        b = h * per
        main += [np.arange(b, b + per), -np.ones(LANES - per, int)]
    return np.concatenate(main)


def _mla_kv_cols():
    per = MLA_NOPE + MLA_V
    kc, vc = [], []
    for h in range(MLA_H):
        b = h * per
        kc += [np.arange(b, b + MLA_NOPE), -np.ones(LANES - MLA_NOPE, int)]
        vv = np.arange(b + MLA_NOPE, b + per)
        pad = -np.ones(LANES - MLA_V, int)
        vc += [vv, pad] if h % 2 == 0 else [pad, vv]
    return np.concatenate(kc + vc)


def _odd_in_cols():
    z = lambda n: -np.ones(n, int)
    o = 0
    cols = []
    mq0, mk0 = 0, ML_H * ML_DK
    for base in (mq0, mk0):
        for h in range(ML_H):
            cols += [np.arange(base + h * ML_DK, base + (h + 1) * ML_DK), z(LANES - ML_DK)]
    mv0 = 2 * ML_H * ML_DK
    cols.append(np.arange(mv0, mv0 + ML_H * ML_DV))
    mi0 = mv0 + ML_H * ML_DV
    mo0 = mi0 + 2 * ML_H
    cols.append(np.arange(mo0, mo0 + ML_H * ML_DV))
    cols += [np.arange(mi0, mi0 + 2 * ML_H), z(LANES - 2 * ML_H)]
    sq0 = mo0 + ML_H * ML_DV
    sk0 = sq0 + SWA_H * SWA_D
    sv0 = sk0 + SWA_KV * SWA_D
    half = SWA_D // 2

    cols.append(np.arange(sq0, sq0 + SWA_H * SWA_D))
    for g in range(SWA_KV):
        cols += [np.arange(sk0 + g * SWA_D, sk0 + (g + 1) * SWA_D)] * 2
    for g in range(SWA_KV):
        vv = np.arange(sv0 + g * SWA_D, sv0 + (g + 1) * SWA_D)
        cols += [vv, z(LANES - SWA_D), z(LANES - SWA_D), vv]
    return np.concatenate(cols)


def _even_weights(w_in, w_qb, w_kvb):
    return (_take_cols(w_in, _even_in_cols()), _take_cols(w_qb, _mla_q_cols()), _take_cols(w_kvb, _mla_kv_cols()))


def _even_mixer(x, tabs, weights, q_norm, kv_norm, conv_w, a_log, dt_bias, o_norm, batch, seq):
    ctab, stab = tabs
    w, wq2, wkv2 = weights
    mla_in, act, z, gates = _proj_even(x, w, conv_w, seq)
    q, k, v = _mla_prep(mla_in, ctab, stab, q_norm.reshape(1, -1), kv_norm.reshape(1, -1), wq2, wkv2)
    o_a = _mla_attn(q, k, v, batch, seq)
    o_b = _gdn(act, gates, z, _pad_lane_row(a_log, GDN_H), _pad_lane_row(dt_bias, GDN_H),
               o_norm.reshape(1, -1), batch, seq)
    return o_a, o_b


def _odd_mixer(x, tabs, w, b_i, b_f, ml_norm, sinks, batch, seq):
    ctab, stab = tabs
    mq, mk, mv, mo, mg, sq, sk, sv = _proj_odd(x, w, ctab, stab)
    bias_row = _pad_lane_row(jnp.concatenate([b_i, b_f]), 0)
    o_c = _mlstm(mq, mk, mv, mo, mg, bias_row, ml_norm.reshape(1, -1), batch, seq)
    o_d = _swa(sq, sk, sv, _pad_lane_row(sinks, 0), batch, seq)
    return o_c, o_d


def _moe(x, xp, router_w, router_b, w_gate, w_up, w_down, layer, s_gate, s_up, s_down, ln_g, ln_b):
    t, d = x.shape
    bias_col = jnp.broadcast_to(router_b.reshape(-1, 1).astype(F32), (N_EXPERTS, LANES))
    idx, gate, rank, cnt = _router(x, router_w.T, bias_col)
    counts = cnt[:, 0].astype(jnp.int32)
    block = int(min(max(pl.next_power_of_2(t * TOP_K // N_EXPERTS) // 2, EXPERT_BLOCK_MIN), EXPERT_BLOCK_MAX))
    padded = (counts + block - 1) // block * block
    pad_end = jnp.cumsum(padded)
    pad_start = pad_end - padded
    start_col = jnp.broadcast_to(pad_start.astype(F32).reshape(-1, 1), (N_EXPERTS, LANES))
    dest = _dest_rows(idx, rank, start_col)
    n_blocks = t * TOP_K // block + N_EXPERTS
    rows = n_blocks * block
    block_row = jnp.arange(n_blocks, dtype=jnp.int32) * block
    block_e = jnp.minimum(jnp.sum((pad_end[None, :] <= block_row[:, None]).astype(jnp.int32), axis=1), N_EXPERTS - 1)
    n_used = (pad_end[-1:] // block).astype(jnp.int32)
    live_end = jnp.sum(jnp.where(block_e[:, None] == jnp.arange(N_EXPERTS, dtype=jnp.int32)[None, :],
                                 (pad_start + counts)[None, :], 0), axis=1)
    n_valid = jnp.clip(live_end - block_row, 0, block).astype(jnp.int32)
    xs = _sc_scatter_rows(xp, dest, rows)
    ys = _experts(block_e, n_used, n_valid, xs, w_gate, w_up, w_down, layer, block)
    picked = _sc_gather_rows(ys, dest.reshape(-1)).reshape(TOP_K, t, d // 2)
    shared = _shared_expert(xp, s_gate.astype(BF16), s_up.astype(BF16), s_down.astype(BF16))
    return _combine(x, gate.T, picked, shared, ln_g.reshape(1, -1), ln_b.reshape(1, -1))


def kernel(x, positions, ev_w_in, mla_q_norm, mla_w_qb, mla_kv_norm, mla_w_kvb, gdn_conv, gdn_a_log, gdn_dt_bias, gdn_norm, ev_w_out, od_w_in, mlstm_b_i, mlstm_b_f, mlstm_norm, swa_sinks, od_w_out, ln1_g, ln1_b, router_w, router_b, moe_w_gate, moe_w_up, moe_w_down, shared_w_gate, shared_w_up, shared_w_down, ln2_g, ln2_b):
    batch, seq, d = x.shape
    streams = STREAMS if batch % STREAMS == 0 else 1
    sb = batch // streams
    ts = sb * seq
    hs, tabs_m, tabs_s = [], [], []
    for s in range(streams):
        pos = positions[s * sb:(s + 1) * sb].reshape(ts, 1).astype(F32)
        tm_, ts_ = _rope_tables(pos)
        tabs_m.append(tm_)
        tabs_s.append(ts_)
        hs.append(x[s * sb:(s + 1) * sb].reshape(ts, d))
    for layer in range(DEPTH):
        j = layer // 2
        if layer % 2 == 0:
            weights = _even_weights(ev_w_in[j], mla_w_qb[j], mla_w_kvb[j])
            w_out = ev_w_out[j].astype(BF16)
        else:
            weights = _take_cols(od_w_in[j], _odd_in_cols())
            w_out = od_w_out[j].astype(BF16)
        for s in range(streams):
            h = hs[s]
            if layer % 2 == 0:
                a1, a2 = _even_mixer(h, tabs_m[s], weights, mla_q_norm[j], mla_kv_norm[j], gdn_conv[j], gdn_a_log[j],
                                     gdn_dt_bias[j], gdn_norm[j], sb, seq)
            else:
                a1, a2 = _odd_mixer(h, tabs_s[s], weights, mlstm_b_i[j], mlstm_b_f[j], mlstm_norm[j], swa_sinks[j], sb, seq)
            h, hp = _outproj_ln(h, a1, a2, w_out, ln1_g[layer].reshape(1, -1), ln1_b[layer].reshape(1, -1))
            hs[s] = _moe(h, hp, router_w[layer], router_b[layer], moe_w_gate, moe_w_up, moe_w_down, layer,
                         shared_w_gate[layer], shared_w_up[layer], shared_w_down[layer], ln2_g[layer], ln2_b[layer])
    return jnp.concatenate([h.reshape(sb, seq, d) for h in hs], axis=0)
```

```python
import functools
import math

import numpy as np
import jax
import jax.numpy as jnp
from jax import lax
from jax.experimental import pallas as pl
from jax.experimental.pallas import tpu as pltpu
from jax.experimental.pallas import tpu_sc as plsc

F32 = jnp.float32
BF16 = jnp.bfloat16

D_MODEL = 1024
DEPTH = 4
ROPE_THETA = 10000.0
EPS = 1e-6
LN_EPS = 1e-5
MLA_H, MLA_NOPE, MLA_ROPE, MLA_V = 8, 64, 32, 64
Q_LORA, KV_LORA = 256, 128
GDN_H, GDN_DK, GDN_DV, CONV_W, GDN_CHUNK = 4, 128, 128, 4, 64
ML_H, ML_DK, ML_DV, ML_CHUNK = 4, 64, 128, 64
SWA_H, SWA_KV, SWA_D, WINDOW = 8, 2, 64, 128
N_EXPERTS, N_GROUPS, TOPK_GROUPS, TOP_K = 64, 8, 4, 8
D_EXPERT, D_SHARED = 256, 256
ROUTED_SCALE = 2.5
DN_ALPHA = (2 * DEPTH) ** 0.25

LANES = 128
V7X_VMEM_BYTES = 64 * 1024 * 1024
VMEM_LIMIT = V7X_VMEM_BYTES * 3 // 4

EXPERT_BLOCK_MIN = 256
EXPERT_BLOCK_MAX = 1024
STREAMS = 1
EXPERT_SUBBLOCKS = 4
SWA_SEQS_PER_STEP = 8
MLSTM_SEQS_PER_STEP = 2
GDN_SEQS_PER_STEP = 8
SC_CHUNK = 64


def _cparams(sem, vmem=VMEM_LIMIT):
    return pltpu.CompilerParams(dimension_semantics=sem, vmem_limit_bytes=vmem)


def _dot(a, b):
    return jnp.dot(a, b, preferred_element_type=F32)


def _dot_nt(a, b):
    return lax.dot_general(a, b, (((1,), (1,)), ((), ())), preferred_element_type=F32)


def _dot_tn(a, b):
    return lax.dot_general(a, b, (((0,), (0,)), ((), ())), preferred_element_type=F32)


def _split2(a):
    hi = a.astype(BF16)
    lo = (a - hi.astype(F32)).astype(BF16)
    return hi, lo


def _split3(a):
    p1 = a.astype(BF16)
    r = a - p1.astype(F32)
    p2 = r.astype(BF16)
    p3 = (r - p2.astype(F32)).astype(BF16)
    return p1, p2, p3


def _dot3(a, b, dot=_dot):
    ah, al = _split2(a)
    bh, bl = _split2(b)
    return dot(ah, bh) + (dot(ah, bl) + dot(al, bh))


def _dot_sel(sel, b, dot=_dot):
    sel = sel.astype(BF16)
    p1, p2, p3 = _split3(b)
    return dot(sel, p1) + (dot(sel, p2) + dot(sel, p3))


def _sigmoid(x):
    return 1.0 / (1.0 + jnp.exp(-x))


def _softplus(x):
    return jnp.maximum(x, 0.0) + jnp.log(1.0 + jnp.exp(-jnp.abs(x)))


def _silu(x):
    return x * _sigmoid(x)


def _lane_bcast(x, c):
    return jnp.broadcast_to(x[:, c:c + 1], x.shape)


def _iota2(shape, dim):
    return lax.broadcasted_iota(jnp.int32, shape, dim)


def _rope_kernel(pos_ref, rows_ref, sel_ref, cm_ref, sm_ref, cs_ref, ss_ref):
    ang = pos_ref[...] * rows_ref[0:1, :]
    cos_parts = _split3(jnp.cos(ang))
    sin_parts = _split3(jnp.sin(ang))

    def place(parts, k):
        return _dot(parts[0], sel_ref[k]) + (_dot(parts[1], sel_ref[k]) + _dot(parts[2], sel_ref[k]))

    cm_ref[...] = place(cos_parts, 0) + rows_ref[1:2, :]
    sm_ref[...] = place(sin_parts, 1)
    cs_ref[...] = place(cos_parts, 2)
    ss_ref[...] = place(sin_parts, 3)


def _rope_consts():
    hm, hs = MLA_ROPE // 2, SWA_D // 2
    rows = np.zeros((8, LANES), np.float32)
    rows[0, :hm] = ROPE_THETA ** (-(np.arange(0, MLA_ROPE, 2, dtype=np.float32) / MLA_ROPE))
    rows[0, hm:hm + hs] = ROPE_THETA ** (-(np.arange(0, SWA_D, 2, dtype=np.float32) / SWA_D))
    rows[1, :MLA_NOPE] = 1.0
    sel = np.zeros((4, LANES, LANES), np.float32)
    for j in range(hm):
        sel[0, j, MLA_NOPE + j] = sel[0, j, MLA_NOPE + hm + j] = 1.0
        sel[1, j, MLA_NOPE + j] = -1.0
        sel[1, j, MLA_NOPE + hm + j] = 1.0
    for h in range(LANES // SWA_D):
        for j in range(hs):
            sel[2, hm + j, h * SWA_D + j] = sel[2, hm + j, h * SWA_D + hs + j] = 1.0
            sel[3, hm + j, h * SWA_D + j] = -1.0
            sel[3, hm + j, h * SWA_D + hs + j] = 1.0
    return jnp.asarray(rows), jnp.asarray(sel, BF16)


def _rope_tables(pos, tm=512):
    t = pos.shape[0]
    tm = min(tm, t)
    rows, sel = _rope_consts()
    cm, sm, cs, ss = pl.pallas_call(
        _rope_kernel,
        grid=(t // tm,),
        in_specs=[pl.BlockSpec((tm, 1), lambda i: (i, 0)), pl.BlockSpec((8, LANES), lambda i: (0, 0)),
                  pl.BlockSpec((4, LANES, LANES), lambda i: (0, 0, 0))],
        out_specs=[pl.BlockSpec((tm, LANES), lambda i: (i, 0))] * 4,
        out_shape=[jax.ShapeDtypeStruct((t, LANES), F32)] * 4,
        compiler_params=_cparams(("arbitrary",)),
        name="rope_tables",
    )(pos, rows, sel)
    return (cm, sm), (cs, ss)


def _proj_even_kernel(x_ref, w_ref, cw_ref, mla_ref, act_ref, z_ref, g_ref, ext_ref, *, tiles_per_seq):
    tm = x_ref.shape[0]
    o = np.concatenate([[0], np.cumsum(EV_WIDTHS)]).tolist()
    @pl.when(pl.program_id(0) % tiles_per_seq == 0)
    def _():
        ext_ref[0:8, :] = jnp.zeros((8, ext_ref.shape[1]), F32)

    xb = x_ref[...].astype(BF16)
    nchunk = 3
    cw = EV_WIDTHS[1] // nchunk

    def project(ci):
        ext_ref[8:8 + tm, ci * cw:(ci + 1) * cw] = _dot(xb, w_ref[:, o[1] + ci * cw:o[1] + (ci + 1) * cw])

    project(0)
    for ci in range(nchunk):
        if ci + 1 < nchunk:
            project(ci + 1)
        else:
            mla_ref[...] = _dot(xb, w_ref[:, o[0]:o[1]])
            z_ref[...] = _dot(xb, w_ref[:, o[2]:o[3]]).astype(z_ref.dtype)
            g_ref[...] = _dot(xb, w_ref[:, o[3]:o[4]])
        cols = slice(ci * cw, (ci + 1) * cw)
        conv = cw_ref[0:1, cols] * ext_ref[5:5 + tm, cols]
        for j in range(1, CONV_W):
            conv = conv + cw_ref[j:j + 1, cols] * ext_ref[5 + j:5 + j + tm, cols]
        act_ref[:, cols] = _silu(conv).astype(act_ref.dtype)
    ext_ref[0:8, :] = ext_ref[tm:tm + 8, :]


def _proj_even(x, w, conv_w, seq, tm=512):
    t, k = x.shape
    tm = min(tm, seq)
    row = lambda i: (i, 0)
    fix = lambda i: (0, 0)
    return pl.pallas_call(
        functools.partial(_proj_even_kernel, tiles_per_seq=seq // tm),
        grid=(t // tm,),
        in_specs=[pl.BlockSpec((tm, k), row), pl.BlockSpec(w.shape, fix), pl.BlockSpec(conv_w.shape, fix)],
        out_specs=[pl.BlockSpec((tm, n), row) for n in EV_WIDTHS],
        out_shape=[jax.ShapeDtypeStruct((t, n), F32) for n in EV_WIDTHS],
        scratch_shapes=[pltpu.VMEM((tm + 8, EV_WIDTHS[1]), F32)],
        compiler_params=_cparams(("arbitrary",)),
        name="in_proj",
    )(x, w, conv_w)


OD_SEG = dict(mq=(0, 512), mk=(512, 1024), mv=(1024, 1536), mo=(1536, 2048), gates=(2048, 2176),
              sq=(2176, 2688), sk=(2688, 2944), sv=(2944, 3456))
OD_COLS = 3456


def _proj_odd_kernel(x_ref, w_ref, c_ref, s_ref, mq_ref, mk_ref, mv_ref, mo_ref, mg_ref, sq_ref, sk_ref, sv_ref):
    xb = x_ref[...].astype(BF16)

    def seg(name):
        a, b = OD_SEG[name]
        return _dot(xb, w_ref[:, a:b])

    mq_ref[...] = seg("mq").astype(mq_ref.dtype)
    mk_ref[...] = seg("mk").astype(mk_ref.dtype)
    mv_ref[...] = seg("mv").astype(mv_ref.dtype)
    mo_ref[...] = seg("mo").astype(mo_ref.dtype)
    mg_ref[...] = seg("gates")
    c = c_ref[...]
    s = s_ref[...]
    def swap_halves(t):
        half = SWA_D // 2
        first_half = (_iota2(t.shape, 1) % SWA_D) < half
        return jnp.where(first_half, pltpu.roll(t, t.shape[1] - half, 1), pltpu.roll(t, half, 1))

    c8 = jnp.concatenate([c] * (SWA_H // 2), axis=1)
    s8 = jnp.concatenate([s] * (SWA_H // 2), axis=1)
    q = seg("sq")
    sq_ref[...] = (q * c8 + swap_halves(q) * s8).astype(sq_ref.dtype)
    c2 = jnp.concatenate([c] * SWA_KV, axis=1)
    s2 = jnp.concatenate([s] * SWA_KV, axis=1)
    k = seg("sk")
    sk_ref[...] = (k * c2 + swap_halves(k) * s2).astype(sk_ref.dtype)
    sv_ref[...] = seg("sv").astype(sv_ref.dtype)


def _proj_odd(x, w, ctab, stab, tm=512):
    t, k = x.shape
    widths = (512, 512, 512, 512, 128, SWA_H * SWA_D, SWA_KV * LANES, 2 * SWA_KV * LANES)
    dtypes = (F32, F32, F32, F32, F32, BF16, BF16, BF16)
    return pl.pallas_call(
        _proj_odd_kernel,
        grid=(t // tm,),
        in_specs=[pl.BlockSpec((tm, k), lambda i: (i, 0)), pl.BlockSpec(w.shape, lambda i: (0, 0)),
                  pl.BlockSpec((tm, LANES), lambda i: (i, 0)), pl.BlockSpec((tm, LANES), lambda i: (i, 0))],
        out_specs=[pl.BlockSpec((tm, n), lambda i: (i, 0)) for n in widths],
        out_shape=[jax.ShapeDtypeStruct((t, n), dt) for n, dt in zip(widths, dtypes)],
        compiler_params=_cparams(("arbitrary",)),
        name="in_proj_odd",
    )(x, w, ctab, stab)


def _rms(x, g):
    return x * lax.rsqrt(jnp.mean(x * x, axis=-1, keepdims=True) + EPS) * g


def _mla_prep_kernel(in_ref, c_ref, s_ref, qn_ref, kvn_ref, wq_ref, wkv_ref, q_ref, k_ref, v_ref):
    hw = MLA_H * LANES
    c = c_ref[...]
    s = s_ref[...]
    c8 = jnp.concatenate([c] * MLA_H, axis=1)
    s8 = jnp.concatenate([s] * MLA_H, axis=1)
    def swap_halves(t):
        half = MLA_ROPE // 2
        first_half = (_iota2(t.shape, 1) % LANES) < MLA_NOPE + half
        return jnp.where(first_half, pltpu.roll(t, t.shape[1] - half, 1), pltpu.roll(t, half, 1))

    cqn = _rms(in_ref[:, 0:Q_LORA], qn_ref[...]).astype(BF16)
    qq = _dot(cqn, wq_ref[...])
    scale = (MLA_NOPE + MLA_ROPE) ** -0.5
    q_ref[...] = ((qq[:, :hw] * c8 + qq[:, hw:] * s8) * scale).astype(q_ref.dtype)
    ckvn = _rms(in_ref[:, Q_LORA:Q_LORA + KV_LORA], kvn_ref[...]).astype(BF16)
    kv = _dot(ckvn, wkv_ref[...])
    o = Q_LORA + KV_LORA
    kr = in_ref[:, o:o + LANES]
    krr = kr * c + swap_halves(kr) * s
    k_ref[...] = (kv[:, :hw] + jnp.concatenate([krr] * MLA_H, axis=1)).astype(k_ref.dtype)
    v_ref[...] = kv[:, hw:].astype(v_ref.dtype)


def _mla_prep(mla_in, ctab, stab, qn, kvn, wq2, wkv2, tm=1024):
    t = mla_in.shape[0]
    tm = min(tm, t)
    hw = MLA_H * LANES
    row = lambda i: (i, 0)
    fix = lambda i: (0, 0)
    return pl.pallas_call(
        _mla_prep_kernel,
        grid=(t // tm,),
        in_specs=[pl.BlockSpec((tm, mla_in.shape[1]), row), pl.BlockSpec((tm, LANES), row), pl.BlockSpec((tm, LANES), row),
                  pl.BlockSpec(qn.shape, fix), pl.BlockSpec(kvn.shape, fix),
                  pl.BlockSpec(wq2.shape, fix), pl.BlockSpec(wkv2.shape, fix)],
        out_specs=[pl.BlockSpec((tm, hw), row)] * 3,
        out_shape=[jax.ShapeDtypeStruct((t, hw), BF16)] * 3,
        compiler_params=_cparams(("arbitrary",)),
        name="mla_prep",
    )(mla_in, ctab, stab, qn, kvn, wq2, wkv2)


def _mla_attn_kernel(q_ref, k_ref, v_ref, o_ref, *, tq):
    i = pl.program_id(2)
    neg = -1e30
    lane = _iota2((tq, LANES), 1)
    ones_lane = (MLA_V, 0)

    def chunk(j, carry, masked):
        start = pl.multiple_of(j * tq, tq)
        out = []
        for hh in range(2):
            m, acc = carry[hh]
            q = q_ref[:, hh * LANES:(hh + 1) * LANES]
            kc = k_ref[pl.ds(start, tq), hh * LANES:(hh + 1) * LANES]
            vc = v_ref[pl.ds(start, tq), hh * LANES:(hh + 1) * LANES]
            vc = jnp.where(lane == ones_lane[hh], jnp.ones_like(vc), vc)
            s = _dot_nt(q, kc)
            if masked:
                s = jnp.where(_iota2(s.shape, 0) >= _iota2(s.shape, 1), s, neg)
            m_new = jnp.maximum(m, jnp.max(s, axis=-1, keepdims=True))
            alpha = jnp.exp(m - m_new)
            p = jnp.exp(s - m_new)
            acc = alpha * acc + _dot(p.astype(BF16), vc)
            out.append((m_new, acc))
        return tuple(out)

    one = (jnp.full((tq, 1), neg, F32), jnp.zeros((tq, LANES), F32))
    carry = lax.fori_loop(0, i, lambda j, c: chunk(j, c, False), (one, one))
    (_, acc0), (_, acc1) = chunk(i, carry, True)
    o0 = acc0 / _lane_bcast(acc0, ones_lane[0])
    o1 = acc1 / _lane_bcast(acc1, ones_lane[1])
    o_ref[...] = jnp.where(lane < MLA_V, o0, o1).astype(o_ref.dtype)


def _mla_attn(q, k, v, batch, seq, tq=512):
    tq = min(tq, seq)
    nq = seq // tq
    pairs = MLA_H // 2
    return pl.pallas_call(
        functools.partial(_mla_attn_kernel, tq=tq),
        grid=(batch, pairs, nq),
        in_specs=[pl.BlockSpec((tq, 2 * LANES), lambda b, p, i: (b * nq + i, p)),
                  pl.BlockSpec((seq, 2 * LANES), lambda b, p, i: (b, p)),
                  pl.BlockSpec((seq, 2 * LANES), lambda b, p, i: (b, p))],
        out_specs=pl.BlockSpec((tq, LANES), lambda b, p, i: (b * nq + i, p)),
        out_shape=jax.ShapeDtypeStruct((batch * seq, pairs * LANES), BF16),
        compiler_params=_cparams(("arbitrary", "arbitrary", "arbitrary")),
        name="mla_attn",
    )(q, k, v)


def _unit_lower_inverse_many(ns):
    c = ns[0].shape[0]
    eye = (_iota2((c, c), 0) == _iota2((c, c), 1)).astype(F32)
    xs = [-n for n in ns]
    ps = [eye + x for x in xs]
    xb = [x.astype(BF16) for x in xs]
    for _ in range(int(math.log2(c)) - 1):
        xs = [_dot(b, b) for b in xb]
        xb = [x.astype(BF16) for x in xs]
        ps = [p + _dot(p.astype(BF16), b) for p, b in zip(ps, xb)]
    return ps


def _gdn_kernel(act_ref, g_ref, z_ref, al_ref, dt_ref, on_ref, o_ref, st_ref):
    c = GDN_CHUNK
    hd = GDN_DK
    nqk = GDN_H * GDN_DK

    @pl.when(pl.program_id(1) == 0)
    def _():
        st_ref[...] = jnp.zeros(st_ref.shape, F32)

    tri = (_iota2((c, c), 0) >= _iota2((c, c), 1)).astype(F32)
    row_ge = _iota2((c, c), 0) >= _iota2((c, c), 1)
    row_gt = _iota2((c, c), 0) > _iota2((c, c), 1)
    lane = _iota2((c, LANES), 1)

    seqs = []
    for bb in range(act_ref.shape[0]):
        gates = g_ref[bb]
        g_all = -jnp.exp(al_ref[...]) * _softplus(gates + dt_ref[...])
        gc_all = _dot_sel(tri, g_all)
        seqs.append(dict(beta_all=_sigmoid(gates), gc_all=gc_all, gc_parts=_split3(gc_all)))
    units = []
    for bb, sq in enumerate(seqs):
        for h in range(GDN_H):
            q = act_ref[bb, :, h * hd:(h + 1) * hd].astype(F32)
            k = act_ref[bb, :, nqk + h * hd:nqk + (h + 1) * hd].astype(F32)
            v = act_ref[bb, :, 2 * nqk + h * GDN_DV:2 * nqk + (h + 1) * GDN_DV].astype(F32)
            q = q * lax.rsqrt(jnp.sum(q * q, axis=-1, keepdims=True) + EPS) * (GDN_DK ** -0.5)
            k = k * lax.rsqrt(jnp.sum(k * k, axis=-1, keepdims=True) + EPS)
            beta = _lane_bcast(sq["beta_all"], h)
            gcol = _lane_bcast(sq["gc_all"], GDN_H + h)
            units.append(dict(bb=bb, h=h, q=q, k=k, v=v, beta=beta, gcol=gcol, kb=k * beta, parts=sq["gc_parts"]))
    for u in units:
        pick = (lane == GDN_H + u["h"]).astype(BF16)
        p0, p1, p2 = u["parts"]
        u["grow"] = _dot_nt(pick, p0) + (_dot_nt(pick, p1) + _dot_nt(pick, p2))
        u["kk"] = _dot3(u["kb"], u["k"], _dot_nt)
        u["qk"] = _dot_nt(u["q"].astype(BF16), u["k"].astype(BF16))
    for u in units:
        gcol = u["gcol"]
        decay = jnp.exp(jnp.where(row_ge, gcol[:, :c] - u["grow"], -jnp.inf))
        eg = jnp.exp(gcol)
        glast = gcol[c - 1:c, :]
        u["lower"] = jnp.where(row_gt, u["kk"] * decay, 0.0)
        u["rhs"] = jnp.concatenate([u["v"] * u["beta"], u["kb"] * eg], axis=1)
        u["attn"] = u["qk"] * decay
        u["qg"] = (u["q"] * eg).astype(BF16)
        u["kg"] = (u["k"] * jnp.exp(glast - gcol)).astype(BF16)
        u["gl"] = jnp.exp(glast)

    tinvs = _unit_lower_inverse_many([u["lower"] for u in units])
    uws = []
    for u, tinv in zip(units, tinvs):
        uws.append(_dot(tinv.astype(BF16), u["rhs"].astype(BF16)))
    states = [st_ref[u["bb"], u["h"]] for u in units]
    sbs = [s.astype(BF16) for s in states]
    vnews = [(uw[:, :GDN_DV] - _dot(uw[:, GDN_DV:].astype(BF16), sb)).astype(BF16) for uw, sb in zip(uws, sbs)]
    for u, state, sb, vnb in zip(units, states, sbs, vnews):
        bb, h = u["bb"], u["h"]
        o = _dot(u["qg"], sb) + _dot(u["attn"].astype(BF16), vnb)
        st_ref[bb, h] = state * u["gl"] + _dot_tn(u["kg"], vnb)
        o = _rms(o, on_ref[...]) * _silu(z_ref[bb, :, h * GDN_DV:(h + 1) * GDN_DV].astype(F32))
        o_ref[bb, :, h * GDN_DV:(h + 1) * GDN_DV] = o.astype(o_ref.dtype)


def _gdn(act, gates, z, a_row, dt_row, o_norm, batch, seq):
    c = GDN_CHUNK
    nc = seq // c
    w3 = act.shape[1]
    wo = GDN_H * GDN_DV
    nb = min(GDN_SEQS_PER_STEP, batch)
    row = lambda b, i: (b, i, 0)
    fix = lambda b, i: (0, 0)
    out = pl.pallas_call(
        _gdn_kernel,
        grid=(batch // nb, nc),
        in_specs=[pl.BlockSpec((nb, c, w3), row), pl.BlockSpec((nb, c, LANES), row), pl.BlockSpec((nb, c, wo), row),
                  pl.BlockSpec((1, LANES), fix), pl.BlockSpec((1, LANES), fix), pl.BlockSpec((1, GDN_DV), fix)],
        out_specs=pl.BlockSpec((nb, c, wo), row),
        out_shape=jax.ShapeDtypeStruct((batch, seq, wo), BF16),
        scratch_shapes=[pltpu.VMEM((nb, GDN_H, GDN_DK, GDN_DV), F32)],
        compiler_params=_cparams(("arbitrary", "arbitrary")),
        name="gdn",
    )(act.reshape(batch, seq, w3), gates.reshape(batch, seq, LANES), z.reshape(batch, seq, wo), a_row, dt_row, o_norm)
    return out.reshape(batch * seq, wo)


def _mlstm_kernel(q_ref, k_ref, v_ref, og_ref, g_ref, bias_ref, nrm_ref, o_ref, c_ref, n_ref, m_ref):
    @pl.when(pl.program_id(1) == 0)
    def _():
        c_ref[...] = jnp.zeros(c_ref.shape, F32)
        n_ref[...] = jnp.zeros(n_ref.shape, F32)
        m_ref[...] = jnp.zeros(m_ref.shape, F32)

    c = ML_CHUNK
    tri = (_iota2((c, c), 0) >= _iota2((c, c), 1)).astype(F32)
    row_ge = _iota2((c, c), 0) >= _iota2((c, c), 1)
    ones = jnp.ones((c, LANES), F32)
    lane = _iota2((c, LANES), 1)

    units = []
    for bb in range(q_ref.shape[0]):
        pre = g_ref[bb] + bias_ref[...]
        logf = jnp.minimum(pre, 0.0) - jnp.log(1.0 + jnp.exp(-jnp.abs(pre)))
        bcum_all = _dot_sel(tri, logf)
        for h in range(ML_H):
            q = q_ref[bb, :, h * LANES:(h + 1) * LANES].astype(F32)
            k = k_ref[bb, :, h * LANES:(h + 1) * LANES].astype(F32) * (ML_DK ** -0.5)
            units.append(dict(bb=bb, h=h, q=q, k=k, qb=q.astype(BF16), vb=v_ref[bb, :, h * ML_DV:(h + 1) * ML_DV].astype(BF16),
                              bcol=_lane_bcast(bcum_all, ML_H + h),
                              icol=_lane_bcast(pre, h),
                              col=jnp.where(lane == h, pre, 0.0) - jnp.where(lane == ML_H + h, bcum_all, 0.0),
                              m_st=m_ref[bb, h], cst=c_ref[bb, h], nst=n_ref[bb, h]))
    for u in units:
        u["row"] = _dot_sel(ones, u["col"], _dot_nt)
        u["qk"] = _dot_nt(u["qb"], u["k"].astype(BF16))
        u["qc"] = _dot(u["qb"], u["cst"].astype(BF16))
    for u in units:
        u["d"] = jnp.where(row_ge, u["bcol"][:, :c] + u["row"], -jnp.inf)
        u["inter"] = u["bcol"] + u["m_st"]
        u["m_t"] = jnp.maximum(u["inter"], jnp.max(u["d"], axis=-1, keepdims=True))
        u["b_end"] = u["bcol"][c - 1:c, :]
        u["a"] = u["b_end"] - u["bcol"] + u["icol"]
        u["m_new"] = jnp.maximum(u["b_end"] + u["m_st"], jnp.max(u["a"], axis=0, keepdims=True))
    for u in units:
        u["w_inter"] = jnp.exp(u["inter"] - u["m_t"])
        u["p"] = jnp.exp(u["d"] - u["m_t"][:, :c]) * u["qk"]
        u["keep"] = jnp.exp(u["b_end"] + u["m_st"] - u["m_new"])
        u["ks"] = u["k"] * jnp.exp(u["a"] - u["m_new"])
    for u in units:
        u["pv"] = _dot(u["p"].astype(BF16), u["vb"])
        u["kv"] = _dot_tn(u["ks"].astype(BF16), u["vb"])
    for u in units:
        u["den"] = (u["w_inter"] * jnp.sum(u["q"] * u["nst"], axis=-1, keepdims=True)
                    + jnp.sum(u["p"], axis=-1, keepdims=True))
    for u in units:
        bb, h = u["bb"], u["h"]
        num = u["w_inter"] * u["qc"] + u["pv"]
        hc = num / jnp.maximum(jnp.abs(u["den"]), jnp.exp(-u["m_t"]))
        c_ref[bb, h] = u["cst"] * u["keep"] + u["kv"]
        n_ref[bb, h] = u["nst"] * u["keep"] + jnp.sum(u["ks"], axis=0, keepdims=True)
        m_ref[bb, h] = u["m_new"]
        hn = (_rms(hc, nrm_ref[:, h * ML_DV:(h + 1) * ML_DV])
              * _sigmoid(og_ref[bb, :, h * ML_DV:(h + 1) * ML_DV].astype(F32)))
        o_ref[bb, :, h * ML_DV:(h + 1) * ML_DV] = hn.astype(o_ref.dtype)


def _mlstm(mq, mk, mv, mo, gates, bias_row, norm_row, batch, seq):
    c = ML_CHUNK
    nc = seq // c
    nb = min(MLSTM_SEQS_PER_STEP, batch)
    row = lambda b, i: (b, i, 0)
    fix = lambda b, i: (0, 0)
    wide = ML_H * LANES
    r3 = lambda a: a.reshape(batch, seq, a.shape[-1])
    out = pl.pallas_call(
        _mlstm_kernel,
        grid=(batch // nb, nc),
        in_specs=[pl.BlockSpec((nb, c, wide), row), pl.BlockSpec((nb, c, wide), row), pl.BlockSpec((nb, c, wide), row),
                  pl.BlockSpec((nb, c, wide), row), pl.BlockSpec((nb, c, LANES), row),
                  pl.BlockSpec((1, LANES), fix), pl.BlockSpec((1, wide), fix)],
        out_specs=pl.BlockSpec((nb, c, wide), row),
        out_shape=jax.ShapeDtypeStruct((batch, seq, wide), BF16),
        scratch_shapes=[pltpu.VMEM((nb, ML_H, LANES, ML_DV), F32), pltpu.VMEM((nb, ML_H, 1, LANES), F32),
                        pltpu.VMEM((nb, ML_H, 1, LANES), F32)],
        compiler_params=_cparams(("arbitrary", "arbitrary")),
        name="mlstm",
    )(r3(mq), r3(mk), r3(mv), r3(mo), r3(gates), bias_row, norm_row)
    return out.reshape(batch * seq, wide)


def _swa_kernel(q_ref, kc_ref, kp_ref, vc_ref, vp_ref, sink_ref, o_ref):
    w = WINDOW
    n = pl.program_id(1)
    scale = SWA_D ** -0.5
    qi = _iota2((w, w), 0)
    kj = _iota2((w, w), 1)
    mask_c = kj <= qi
    mask_p = jnp.logical_and(kj > qi, n > 0)
    grp = SWA_H // SWA_KV
    neg = -1e30
    units = [(bb, h) for bb in range(q_ref.shape[0]) for h in range(SWA_H)]
    scores = []
    half_of_lane = _iota2((w, LANES), 1) // SWA_D
    for bb, h in units:
        g = h // grp
        pair = q_ref[bb, :, (h // 2) * LANES:(h // 2 + 1) * LANES]
        q = jnp.where(half_of_lane == h % 2, pair, jnp.zeros_like(pair))
        scores.append((_dot_nt(q, kc_ref[bb, :, g * LANES:(g + 1) * LANES]),
                       _dot_nt(q, kp_ref[bb, :, g * LANES:(g + 1) * LANES])))
    masked, tops, exps, dens, probs = [], [], [], [], {}
    for sc, sp in scores:
        masked.append((jnp.where(mask_c, sc * scale, neg), jnp.where(mask_p, sp * scale, neg)))
    for (bb, h), (s_c, s_p) in zip(units, masked):
        tops.append(jnp.maximum(jnp.max(jnp.maximum(s_c, s_p), axis=-1, keepdims=True), sink_ref[:, h:h + 1]))
    for (s_c, s_p), m in zip(masked, tops):
        exps.append((jnp.where(mask_c, jnp.exp(s_c - m), 0.0), jnp.where(mask_p, jnp.exp(s_p - m), 0.0)))
    ones_b = jnp.ones((w, LANES), BF16)
    for (bb, h), (p_c, p_p), m in zip(units, exps, tops):
        p_c, p_p = p_c.astype(BF16), p_p.astype(BF16)
        probs[bb, h] = (p_c, p_p)
        dens.append(_dot(p_c, ones_b) + _dot(p_p, ones_b) + jnp.exp(sink_ref[:, h:h + 1] - m))
    inv = {u: 1.0 / den for u, den in zip(units, dens)}
    for bb in range(q_ref.shape[0]):
        for pair in range(SWA_H // 2):
            acc = None
            for sub in range(2):
                h = 2 * pair + sub
                vcol = (2 * (h // grp) + sub) * LANES
                p_c, p_p = probs[bb, h]
                part = (_dot(p_c, vc_ref[bb, :, vcol:vcol + LANES]) + _dot(p_p, vp_ref[bb, :, vcol:vcol + LANES])) * inv[bb, h]
                acc = part if acc is None else acc + part
            o_ref[bb, :, pair * LANES:(pair + 1) * LANES] = acc.astype(o_ref.dtype)


def _swa(sq, sk, sv, sinks_row, batch, seq):
    w = WINDOW
    nb = seq // w
    ns = min(SWA_SEQS_PER_STEP, batch)
    wo = SWA_H * SWA_D
    cur = lambda b, n: (b, n, 0)
    prev = lambda b, n: (b, jnp.maximum(n - 1, 0), 0)
    r3 = lambda a: a.reshape(batch, seq, a.shape[-1])
    q3, k3, v3 = r3(sq), r3(sk), r3(sv)
    out = pl.pallas_call(
        _swa_kernel,
        grid=(batch // ns, nb),
        in_specs=[pl.BlockSpec((ns, w, sq.shape[1]), cur),
                  pl.BlockSpec((ns, w, sk.shape[1]), cur), pl.BlockSpec((ns, w, sk.shape[1]), prev),
                  pl.BlockSpec((ns, w, sv.shape[1]), cur), pl.BlockSpec((ns, w, sv.shape[1]), prev),
                  pl.BlockSpec((1, LANES), lambda b, n: (0, 0))],
        out_specs=pl.BlockSpec((ns, w, wo), cur),
        out_shape=jax.ShapeDtypeStruct((batch, seq, wo), BF16),
        compiler_params=_cparams(("arbitrary", "arbitrary")),
        name="swa",
    )(q3, k3, k3, v3, v3, sinks_row)
    return out.reshape(batch * seq, wo)


def _layer_norm(h, g, b):
    mu = jnp.mean(h, axis=-1, keepdims=True)
    d = h - mu
    var = jnp.mean(d * d, axis=-1, keepdims=True)
    return d * lax.rsqrt(var + LN_EPS) * g + b


def _outproj_kernel(x_ref, a1_ref, a2_ref, w_ref, g_ref, b_ref, o_ref, op_ref):
    k1 = a1_ref.shape[1]
    y = _dot(a1_ref[...].astype(BF16), w_ref[0:k1, :]) + _dot(a2_ref[...].astype(BF16), w_ref[k1:, :])
    h = _layer_norm(DN_ALPHA * x_ref[...] + y, g_ref[...], b_ref[...])
    o_ref[...] = h
    op_ref[...] = _pack_pairs(h)


def _outproj_ln(x, a1, a2, w, g, b, tm=1024):
    t, d = x.shape
    tm = min(tm, t)
    row = lambda i: (i, 0)
    fix = lambda i: (0, 0)
    return pl.pallas_call(
        _outproj_kernel,
        grid=(t // tm,),
        in_specs=[pl.BlockSpec((tm, d), row), pl.BlockSpec((tm, a1.shape[1]), row), pl.BlockSpec((tm, a2.shape[1]), row),
                  pl.BlockSpec(w.shape, fix), pl.BlockSpec((1, d), fix), pl.BlockSpec((1, d), fix)],
        out_specs=[pl.BlockSpec((tm, d), row), pl.BlockSpec((tm, d // 2), row)],
        out_shape=[jax.ShapeDtypeStruct((t, d), F32), jax.ShapeDtypeStruct((t, d // 2), jnp.uint32)],
        compiler_params=_cparams(("arbitrary",)),
        name="outproj_ln",
    )(x, a1, a2, w, g, b)


def _first_index(x, m, iota_f, sentinel):
    return jnp.min(jnp.where(x == m, iota_f, sentinel), axis=0, keepdims=True)


def _router_kernel(x_ref, wt_ref, bias_ref, idx_ref, gate_ref, rank_ref, cnt_ref, carry_ref):
    tm = x_ref.shape[0]
    e = N_EXPERTS
    gs = e // N_GROUPS
    ninf = -jnp.inf

    @pl.when(pl.program_id(0) == 0)
    def _():
        carry_ref[...] = jnp.zeros(carry_ref.shape, F32)

    logits = _dot3(wt_ref[...], x_ref[...], _dot_nt)
    scores = _sigmoid(logits)
    sel = scores + bias_ref[:, 0:1]

    sub_f = _iota2((gs, tm), 0).astype(F32)
    gscore = []
    for g in range(N_GROUPS):
        blk = sel[g * gs:(g + 1) * gs, :]
        m1 = jnp.max(blk, axis=0, keepdims=True)
        i1 = _first_index(blk, m1, sub_f, float(gs))
        m2 = jnp.max(jnp.where(sub_f == i1, ninf, blk), axis=0, keepdims=True)
        gscore.append(m1 + m2)
    gsc = jnp.concatenate(gscore, axis=0)
    grp_f = _iota2((N_GROUPS, tm), 0).astype(F32)
    gmask = jnp.zeros((N_GROUPS, tm), F32)
    for _ in range(TOPK_GROUPS):
        m = jnp.max(gsc, axis=0, keepdims=True)
        gi = _first_index(gsc, m, grp_f, float(N_GROUPS))
        hit = grp_f == gi
        gmask = jnp.where(hit, 1.0, gmask)
        gsc = jnp.where(hit, ninf, gsc)
    masked = jnp.concatenate(
        [jnp.where(gmask[g:g + 1, :] > 0.0, sel[g * gs:(g + 1) * gs, :], ninf) for g in range(N_GROUPS)], axis=0)

    exp_f = _iota2((e, tm), 0).astype(F32)
    chosen = jnp.zeros((e, tm), F32)
    idxs, gates = [], []
    for _ in range(TOP_K):
        m = jnp.max(masked, axis=0, keepdims=True)
        ei = _first_index(masked, m, exp_f, float(e))
        hit = exp_f == ei
        idxs.append(ei)
        gates.append(jnp.sum(jnp.where(hit, scores, 0.0), axis=0, keepdims=True))
        chosen = jnp.where(hit, 1.0, chosen)
        masked = jnp.where(hit, ninf, masked)
    gate = jnp.concatenate(gates, axis=0)
    gate = gate / jnp.sum(gate, axis=0, keepdims=True) * ROUTED_SCALE
    idx_f = jnp.concatenate(idxs, axis=0)

    upper = (_iota2((tm, tm), 0) < _iota2((tm, tm), 1)).astype(BF16)
    before = _dot(chosen.astype(BF16), upper) + carry_ref[...][:, 0:1]
    ranks = [jnp.sum(jnp.where(exp_f == idxs[k], before, 0.0), axis=0, keepdims=True) for k in range(TOP_K)]
    carry_ref[...] = carry_ref[...] + jnp.sum(chosen, axis=1, keepdims=True)

    idx_ref[...] = idx_f.astype(jnp.int32)
    gate_ref[...] = gate
    rank_ref[...] = jnp.concatenate(ranks, axis=0).astype(jnp.int32)
    cnt_ref[...] = carry_ref[...]


def _router(x, wt, bias_col, tm=512):
    t, d = x.shape
    col = lambda i: (0, i)
    fix = lambda i: (0, 0)
    return pl.pallas_call(
        _router_kernel,
        grid=(t // tm,),
        in_specs=[pl.BlockSpec((tm, d), lambda i: (i, 0)), pl.BlockSpec(wt.shape, fix), pl.BlockSpec((N_EXPERTS, LANES), fix)],
        out_specs=[pl.BlockSpec((TOP_K, tm), col), pl.BlockSpec((TOP_K, tm), col), pl.BlockSpec((TOP_K, tm), col),
                   pl.BlockSpec((N_EXPERTS, LANES), fix)],
        out_shape=[jax.ShapeDtypeStruct((TOP_K, t), jnp.int32), jax.ShapeDtypeStruct((TOP_K, t), F32),
                   jax.ShapeDtypeStruct((TOP_K, t), jnp.int32), jax.ShapeDtypeStruct((N_EXPERTS, LANES), F32)],
        scratch_shapes=[pltpu.VMEM((N_EXPERTS, LANES), F32)],
        compiler_params=_cparams(("arbitrary",)),
        name="router",
    )(x, wt, bias_col)


def _dest_kernel(idx_ref, rank_ref, start_ref, dest_ref):
    tm = idx_ref.shape[1]
    exp_i = _iota2((N_EXPERTS, tm), 0)
    start = start_ref[:, 0:1]
    rows = [jnp.sum(jnp.where(exp_i == idx_ref[s:s + 1, :], start, 0.0), axis=0, keepdims=True) for s in range(TOP_K)]
    dest_ref[...] = jnp.concatenate(rows, axis=0).astype(jnp.int32) + rank_ref[...]


def _dest_rows(idx, rank, start_col, tm=2048):
    t = idx.shape[1]
    tm = min(tm, t)
    col = lambda i: (0, i)
    return pl.pallas_call(
        _dest_kernel,
        grid=(t // tm,),
        in_specs=[pl.BlockSpec((TOP_K, tm), col), pl.BlockSpec((TOP_K, tm), col),
                  pl.BlockSpec((N_EXPERTS, LANES), lambda i: (0, 0))],
        out_specs=pl.BlockSpec((TOP_K, tm), col),
        out_shape=jax.ShapeDtypeStruct((TOP_K, t), jnp.int32),
        compiler_params=_cparams(("arbitrary",)),
        name="moe_dest",
    )(idx, rank, start_col)


def _pack_pairs(x):
    n = x.shape[1] // 2
    hi = lax.bitcast_convert_type(x[:, :n].astype(BF16).astype(F32), jnp.uint32)
    lo = lax.bitcast_convert_type(x[:, n:].astype(BF16).astype(F32), jnp.uint32)
    return hi | (lo >> 16)


def _unpack_pairs(w):
    hi = lax.bitcast_convert_type(w & jnp.uint32(0xFFFF0000), F32)
    lo = lax.bitcast_convert_type(w << 16, F32)
    return hi, lo


def _sc_scatter_rows(xp, dest, rows, chunk=LANES):
    t, width = xp.shape
    info = plsc.get_sparse_core_info()
    ncores, nsub = info.num_cores, info.num_subcores
    per_worker = t // (ncores * nsub)
    nchunk = per_worker // chunk
    mesh = plsc.VectorSubcoreMesh(core_axis_name="c", subcore_axis_name="s")

    @functools.partial(
        pl.kernel, mesh=mesh,
        out_type=jax.ShapeDtypeStruct((rows, width), xp.dtype),
        scratch_types=[pltpu.VMEM((TOP_K, chunk), jnp.int32), pltpu.VMEM((chunk, width), xp.dtype), pltpu.SemaphoreType.DMA],
    )
    def scatter(xp_hbm, dest_hbm, out_hbm, idx_v, rows_v, sem):
        base = (lax.axis_index("s") * ncores + lax.axis_index("c")) * per_worker

        @pl.loop(0, nchunk)
        def _(i):
            off = pl.multiple_of(base + i * chunk, chunk)
            pltpu.sync_copy(dest_hbm.at[:, pl.ds(off, chunk)], idx_v)
            pltpu.sync_copy(xp_hbm.at[pl.ds(off, chunk)], rows_v)
            copies = [pltpu.async_copy(rows_v, out_hbm.at[idx_v.at[s]], sem) for s in range(TOP_K)]
            for cp in copies:
                cp.wait()

    return scatter(xp, dest)


def _experts_kernel(be_ref, nu_ref, nv_ref, first_ref, slot_ref, nxt_ref, xs_ref, wg_hbm, wu_hbm, wd_hbm, ys_ref,
                    wgf_ref, wuf_ref, wdf_ref, wgb_ref, wub_ref, wdb_ref, sem, *, layer):
    i = pl.program_id(0)

    def fetch(e, s):
        return [pltpu.make_async_copy(wg_hbm.at[layer, e], wgf_ref.at[s], sem.at[s]),
                pltpu.make_async_copy(wu_hbm.at[layer, e], wuf_ref.at[s], sem.at[s]),
                pltpu.make_async_copy(wd_hbm.at[layer, e], wdf_ref.at[s], sem.at[s])]

    @pl.when(i == 0)
    def _():
        for cp in fetch(be_ref[0], 0):
            cp.start()

    @pl.when(jnp.logical_and(first_ref[i] == 1, i < nu_ref[0]))
    def _():
        s = slot_ref[i]
        for cp in fetch(be_ref[i], s):
            cp.wait()
        wgb_ref[...] = wgf_ref[s].astype(BF16)
        wub_ref[...] = wuf_ref[s].astype(BF16)
        wdb_ref[...] = wdf_ref[s].astype(BF16)

        @pl.when(nxt_ref[i] >= 0)
        def _():
            for cp in fetch(nxt_ref[i], 1 - s):
                cp.start()

    @pl.when(i < nu_ref[0])
    def _():
        sub = xs_ref.shape[0] // EXPERT_SUBBLOCKS
        acts = []
        for r in range(EXPERT_SUBBLOCKS):
            rows = pl.ds(r * sub, sub)
            live = (_iota2((sub, 1), 0) + r * sub) < nv_ref[i]
            xa, xb = _unpack_pairs(jnp.where(live, xs_ref[rows, :], jnp.uint32(0)))
            x = jnp.concatenate([xa.astype(BF16), xb.astype(BF16)], axis=1)
            acts.append((_dot(x, wgb_ref[...]), _dot(x, wub_ref[...])))
        outs = [_dot((_silu(gate) * up).astype(BF16), wdb_ref[...]) for gate, up in acts]
        for r, y in enumerate(outs):
            ys_ref[pl.ds(r * sub, sub), :] = _pack_pairs(y)


def _experts(block_e, n_used, n_valid, xs, wg, wu, wd, layer, block):
    rows, half = xs.shape
    d = 2 * half
    nb = rows // block
    pos = jnp.arange(nb, dtype=jnp.int32)
    first = jnp.concatenate([jnp.ones((1,), jnp.int32), (block_e[1:] != block_e[:-1]).astype(jnp.int32)])
    slot = (jnp.cumsum(first) - 1) % 2
    later = (pos[None, :] > pos[:, None]) & (block_e[None, :] != block_e[:, None]) & (pos[None, :] < n_used[0])
    nxt_pos = jnp.min(jnp.where(later, pos[None, :], nb), axis=1)
    nxt = jnp.where(nxt_pos < nb, block_e[jnp.minimum(nxt_pos, nb - 1)], -1)
    blk = lambda i, be, nu, *rest: (jnp.minimum(i, nu[0] - 1), 0)
    hbm = pl.BlockSpec(memory_space=pl.ANY)
    return pl.pallas_call(
        functools.partial(_experts_kernel, layer=layer),
        grid_spec=pltpu.PrefetchScalarGridSpec(
            num_scalar_prefetch=6,
            grid=(nb,),
            in_specs=[pl.BlockSpec((block, half), blk), hbm, hbm, hbm],
            out_specs=pl.BlockSpec((block, half), blk),
            scratch_shapes=[pltpu.VMEM((2, d, D_EXPERT), F32), pltpu.VMEM((2, d, D_EXPERT), F32),
                            pltpu.VMEM((2, D_EXPERT, d), F32),
                            pltpu.VMEM((d, D_EXPERT), BF16), pltpu.VMEM((d, D_EXPERT), BF16),
                            pltpu.VMEM((D_EXPERT, d), BF16), pltpu.SemaphoreType.DMA((2,))],
        ),
        out_shape=jax.ShapeDtypeStruct((rows, half), jnp.uint32),
        compiler_params=_cparams(("arbitrary",)),
        name="moe_experts",
    )(block_e, n_used, n_valid, first, slot.astype(jnp.int32), nxt.astype(jnp.int32), xs, wg, wu, wd)


def _sc_gather_rows(table, idx, chunk=SC_CHUNK):
    n = idx.shape[0]
    width = table.shape[1]
    info = plsc.get_sparse_core_info()
    ncores, nsub = info.num_cores, info.num_subcores
    per_worker = n // (ncores * nsub)
    nchunk = per_worker // chunk
    mesh = plsc.VectorSubcoreMesh(core_axis_name="c", subcore_axis_name="s")

    @functools.partial(
        pl.kernel, mesh=mesh,
        out_type=jax.ShapeDtypeStruct((n, width), table.dtype),
        scratch_types=[pltpu.VMEM((nchunk, chunk), jnp.int32), pltpu.VMEM((2, chunk, width), table.dtype),
                       pltpu.SemaphoreType.DMA((2,)), pltpu.SemaphoreType.DMA((2,))],
    )
    def gather(table_hbm, idx_hbm, out_hbm, idx_v, rows_v, gsem, wsem):
        wid = lax.axis_index("s") * ncores + lax.axis_index("c")
        base = wid * per_worker
        pltpu.sync_copy(idx_hbm.at[pl.ds(wid * nchunk, nchunk)], idx_v)

        def fetch(j, b):
            return pltpu.make_async_copy(table_hbm.at[idx_v.at[j]], rows_v.at[b], gsem.at[b])

        def flush(j, b):
            off = pl.multiple_of(base + j * chunk, chunk)
            return pltpu.make_async_copy(rows_v.at[b], out_hbm.at[pl.ds(off, chunk)], wsem.at[b])

        fetch(0, 0).start()

        @pl.loop(0, nchunk, step=2)
        def _(i):
            for b in range(2):
                j = i + b
                fetch(j, b).wait()

                @pl.when(j + 1 < nchunk)
                def _():
                    @pl.when(j >= 1)
                    def _():
                        flush(j - 1, 1 - b).wait()

                    fetch(j + 1, 1 - b).start()

                flush(j, b).start()

        flush(nchunk - 2, 0).wait()
        flush(nchunk - 1, 1).wait()

    return gather(table, idx.reshape(n // chunk, chunk))


def _shared_kernel(xp_ref, sg_ref, su_ref, sd_ref, o_ref):
    xa, xb = _unpack_pairs(xp_ref[...])
    x = jnp.concatenate([xa.astype(BF16), xb.astype(BF16)], axis=1)
    hs = _silu(_dot(x, sg_ref[...])) * _dot(x, su_ref[...])
    o_ref[...] = _pack_pairs(_dot(hs.astype(BF16), sd_ref[...]))


def _shared_expert(xp, sg, su, sd, tm=512):
    t, half = xp.shape
    row = lambda i: (i, 0)
    fix = lambda i: (0, 0)
    return pl.pallas_call(
        _shared_kernel,
        grid=(t // tm,),
        in_specs=[pl.BlockSpec((tm, half), row), pl.BlockSpec(sg.shape, fix), pl.BlockSpec(su.shape, fix),
                  pl.BlockSpec(sd.shape, fix)],
        out_specs=pl.BlockSpec((tm, half), row),
        out_shape=jax.ShapeDtypeStruct((t, half), jnp.uint32),
        compiler_params=_cparams(("arbitrary",)),
        name="moe_shared",
    )(xp, sg, su, sd)


def _combine_kernel(x_ref, gate_ref, rows_ref, sh_ref, g_ref, b_ref, o_ref):
    gate = gate_ref[...]
    ya, yb = _unpack_pairs(sh_ref[...])
    for s in range(TOP_K):
        a, b = _unpack_pairs(rows_ref[s])
        ya = ya + gate[:, s:s + 1] * a
        yb = yb + gate[:, s:s + 1] * b
    ff = jnp.concatenate([ya, yb], axis=1)
    o_ref[...] = _layer_norm(DN_ALPHA * x_ref[...] + ff, g_ref[...], b_ref[...])


def _combine(x, gate_t, rows, shared, g, b, tm=512):
    t, d = x.shape
    row = lambda i: (i, 0)
    fix = lambda i: (0, 0)
    return pl.pallas_call(
        _combine_kernel,
        grid=(t // tm,),
        in_specs=[pl.BlockSpec((tm, d), row), pl.BlockSpec((tm, TOP_K), row),
                  pl.BlockSpec((TOP_K, tm, d // 2), lambda i: (0, i, 0)), pl.BlockSpec((tm, d // 2), row),
                  pl.BlockSpec((1, d), fix), pl.BlockSpec((1, d), fix)],
        out_specs=pl.BlockSpec((tm, d), row),
        out_shape=jax.ShapeDtypeStruct((t, d), F32),
        compiler_params=_cparams(("arbitrary",)),
        name="moe_combine",
    )(x, gate_t, rows, shared, g, b)


def _take_cols(w, idx):
    idx = np.asarray(idx)
    runs, start = [], 0
    for pos in range(1, len(idx) + 1):
        run_ends = pos == len(idx) or (idx[pos] != idx[pos - 1] + 1 if idx[pos - 1] >= 0 else idx[pos] >= 0)
        if run_ends:
            runs.append((start, int(idx[start]), pos - start))
            start = pos

    def body(w_ref, o_ref):
        for dst, src, width in runs:
            if src < 0:
                o_ref[:, dst:dst + width] = jnp.zeros((o_ref.shape[0], width), o_ref.dtype)
            else:
                o_ref[:, dst:dst + width] = w_ref[:, src:src + width].astype(o_ref.dtype)

    rows = w.shape[0]
    tr = min(rows, 256)
    return pl.pallas_call(
        body,
        grid=(rows // tr,),
        in_specs=[pl.BlockSpec((tr, w.shape[1]), lambda i: (i, 0))],
        out_specs=pl.BlockSpec((tr, len(idx)), lambda i: (i, 0)),
        out_shape=jax.ShapeDtypeStruct((rows, len(idx)), BF16),
        compiler_params=_cparams(("arbitrary",)),
        name="weight_cols",
    )(w)


def _pad_lane_row(v, first_lane, width=LANES):
    out = jnp.zeros((1, width), F32)
    return lax.dynamic_update_slice(out, v.reshape(1, -1).astype(F32), (0, first_lane))


def _even_in_cols():
    z = lambda n: -np.ones(n, int)
    kr0 = Q_LORA + KV_LORA
    cols = [np.arange(0, Q_LORA), np.arange(Q_LORA, Q_LORA + KV_LORA),
            z(MLA_NOPE), np.arange(kr0, kr0 + MLA_ROPE), z(LANES - MLA_NOPE - MLA_ROPE)]
    g0 = kr0 + MLA_ROPE
    nqk = GDN_H * GDN_DK
    cols.append(np.arange(g0, g0 + 3 * nqk))
    zoff = g0 + 3 * nqk + 2 * GDN_H
    cols.append(np.arange(zoff, zoff + GDN_H * GDN_DV))
    cols += [np.arange(g0 + 3 * nqk, g0 + 3 * nqk + 2 * GDN_H), z(LANES - 2 * GDN_H)]
    return np.concatenate(cols)


EV_WIDTHS = (Q_LORA + KV_LORA + LANES, 3 * GDN_H * GDN_DK, GDN_H * GDN_DV, LANES)


def _mla_q_cols():
    per = MLA_NOPE + MLA_ROPE
    half = MLA_ROPE // 2
    main, sw = [], []
    for h in range(MLA_H):
        b = h * per
        main += [np.arange(b, b + per), -np.ones(LANES - per, int)]
        sw += [-np.ones(MLA_NOPE, int), np.arange(b + MLA_NOPE + half, b + per), np.arange(b + MLA_NOPE, b + MLA_NOPE + half),
               -np.ones(LANES - per, int)]
    return np.concatenate(main + sw)


def _mla_kv_cols():
    per = MLA_NOPE + MLA_V
    kc, vc = [], []
    for h in range(MLA_H):
        b = h * per
        kc += [np.arange(b, b + MLA_NOPE), -np.ones(LANES - MLA_NOPE, int)]
        vv = np.arange(b + MLA_NOPE, b + per)
        pad = -np.ones(LANES - MLA_V, int)
        vc += [vv, pad] if h % 2 == 0 else [pad, vv]
    return np.concatenate(kc + vc)


def _odd_in_cols():
    z = lambda n: -np.ones(n, int)
    o = 0
    cols = []
    mq0, mk0 = 0, ML_H * ML_DK
    for base in (mq0, mk0):
        for h in range(ML_H):
            cols += [np.arange(base + h * ML_DK, base + (h + 1) * ML_DK), z(LANES - ML_DK)]
    mv0 = 2 * ML_H * ML_DK
    cols.append(np.arange(mv0, mv0 + ML_H * ML_DV))
    mi0 = mv0 + ML_H * ML_DV
    mo0 = mi0 + 2 * ML_H
    cols.append(np.arange(mo0, mo0 + ML_H * ML_DV))
    cols += [np.arange(mi0, mi0 + 2 * ML_H), z(LANES - 2 * ML_H)]
    sq0 = mo0 + ML_H * ML_DV
    sk0 = sq0 + SWA_H * SWA_D
    sv0 = sk0 + SWA_KV * SWA_D
    half = SWA_D // 2

    cols.append(np.arange(sq0, sq0 + SWA_H * SWA_D))
    for g in range(SWA_KV):
        cols += [np.arange(sk0 + g * SWA_D, sk0 + (g + 1) * SWA_D)] * 2
    for g in range(SWA_KV):
        vv = np.arange(sv0 + g * SWA_D, sv0 + (g + 1) * SWA_D)
        cols += [vv, z(LANES - SWA_D), z(LANES - SWA_D), vv]
    return np.concatenate(cols)


def _even_weights(w_in, w_qb, w_kvb):
    return (_take_cols(w_in, _even_in_cols()), _take_cols(w_qb, _mla_q_cols()), _take_cols(w_kvb, _mla_kv_cols()))


def _even_mixer(x, tabs, weights, q_norm, kv_norm, conv_w, a_log, dt_bias, o_norm, batch, seq):
    ctab, stab = tabs
    w, wq2, wkv2 = weights
    mla_in, act, z, gates = _proj_even(x, w, conv_w, seq)
    q, k, v = _mla_prep(mla_in, ctab, stab, q_norm.reshape(1, -1), kv_norm.reshape(1, -1), wq2, wkv2)
    o_a = _mla_attn(q, k, v, batch, seq)
    o_b = _gdn(act, gates, z, _pad_lane_row(a_log, GDN_H), _pad_lane_row(dt_bias, GDN_H),
               o_norm.reshape(1, -1), batch, seq)
    return o_a, o_b


def _odd_mixer(x, tabs, w, b_i, b_f, ml_norm, sinks, batch, seq):
    ctab, stab = tabs
    mq, mk, mv, mo, mg, sq, sk, sv = _proj_odd(x, w, ctab, stab)
    bias_row = _pad_lane_row(jnp.concatenate([b_i, b_f]), 0)
    o_c = _mlstm(mq, mk, mv, mo, mg, bias_row, ml_norm.reshape(1, -1), batch, seq)
    o_d = _swa(sq, sk, sv, _pad_lane_row(sinks, 0), batch, seq)
    return o_c, o_d


def _moe(x, xp, router_w, router_b, w_gate, w_up, w_down, layer, s_gate, s_up, s_down, ln_g, ln_b):
    t, d = x.shape
    bias_col = jnp.broadcast_to(router_b.reshape(-1, 1).astype(F32), (N_EXPERTS, LANES))
    idx, gate, rank, cnt = _router(x, router_w.T, bias_col)
    counts = cnt[:, 0].astype(jnp.int32)
    block = int(min(max(pl.next_power_of_2(t * TOP_K // N_EXPERTS) // 2, EXPERT_BLOCK_MIN), EXPERT_BLOCK_MAX))
    padded = (counts + block - 1) // block * block
    pad_end = jnp.cumsum(padded)
    pad_start = pad_end - padded
    start_col = jnp.broadcast_to(pad_start.astype(F32).reshape(-1, 1), (N_EXPERTS, LANES))
    dest = _dest_rows(idx, rank, start_col)
    n_blocks = t * TOP_K // block + N_EXPERTS
    rows = n_blocks * block
    block_row = jnp.arange(n_blocks, dtype=jnp.int32) * block
    block_e = jnp.minimum(jnp.sum((pad_end[None, :] <= block_row[:, None]).astype(jnp.int32), axis=1), N_EXPERTS - 1)
    n_used = (pad_end[-1:] // block).astype(jnp.int32)
    live_end = jnp.sum(jnp.where(block_e[:, None] == jnp.arange(N_EXPERTS, dtype=jnp.int32)[None, :],
                                 (pad_start + counts)[None, :], 0), axis=1)
    n_valid = jnp.clip(live_end - block_row, 0, block).astype(jnp.int32)
    xs = _sc_scatter_rows(xp, dest, rows)
    ys = _experts(block_e, n_used, n_valid, xs, w_gate, w_up, w_down, layer, block)
    picked = _sc_gather_rows(ys, dest.reshape(-1)).reshape(TOP_K, t, d // 2)
    shared = _shared_expert(xp, s_gate.astype(BF16), s_up.astype(BF16), s_down.astype(BF16))
    return _combine(x, gate.T, picked, shared, ln_g.reshape(1, -1), ln_b.reshape(1, -1))


def kernel(x, positions, ev_w_in, mla_q_norm, mla_w_qb, mla_kv_norm, mla_w_kvb, gdn_conv, gdn_a_log, gdn_dt_bias, gdn_norm, ev_w_out, od_w_in, mlstm_b_i, mlstm_b_f, mlstm_norm, swa_sinks, od_w_out, ln1_g, ln1_b, router_w, router_b, moe_w_gate, moe_w_up, moe_w_down, shared_w_gate, shared_w_up, shared_w_down, ln2_g, ln2_b):
    batch, seq, d = x.shape
    streams = STREAMS if batch % STREAMS == 0 else 1
    sb = batch // streams
    ts = sb * seq
    hs, tabs_m, tabs_s = [], [], []
    for s in range(streams):
        pos = positions[s * sb:(s + 1) * sb].reshape(ts, 1).astype(F32)
        tm_, ts_ = _rope_tables(pos)
        tabs_m.append(tm_)
        tabs_s.append(ts_)
        hs.append(x[s * sb:(s + 1) * sb].reshape(ts, d))
    for layer in range(DEPTH):
        j = layer // 2
        if layer % 2 == 0:
            weights = _even_weights(ev_w_in[j], mla_w_qb[j], mla_w_kvb[j])
            w_out = ev_w_out[j].astype(BF16)
        else:
            weights = _take_cols(od_w_in[j], _odd_in_cols())
            w_out = od_w_out[j].astype(BF16)
        for s in range(streams):
            h = hs[s]
            if layer % 2 == 0:
                a1, a2 = _even_mixer(h, tabs_m[s], weights, mla_q_norm[j], mla_kv_norm[j], gdn_conv[j], gdn_a_log[j],
                                     gdn_dt_bias[j], gdn_norm[j], sb, seq)
            else:
                a1, a2 = _odd_mixer(h, tabs_s[s], weights, mlstm_b_i[j], mlstm_b_f[j], mlstm_norm[j], swa_sinks[j], sb, seq)
            h, hp = _outproj_ln(h, a1, a2, w_out, ln1_g[layer].reshape(1, -1), ln1_b[layer].reshape(1, -1))
            hs[s] = _moe(h, hp, router_w[layer], router_b[layer], moe_w_gate, moe_w_up, moe_w_down, layer,
                         shared_w_gate[layer], shared_w_up[layer], shared_w_down[layer], ln2_g[layer], ln2_b[layer])
    return jnp.concatenate([h.reshape(sb, seq, d) for h in hs], axis=0)
```

```python
import functools
import math

import numpy as np
import jax
import jax.numpy as jnp
from jax import lax
from jax.experimental import pallas as pl
from jax.experimental.pallas import tpu as pltpu
from jax.experimental.pallas import tpu_sc as plsc

F32 = jnp.float32
BF16 = jnp.bfloat16

D_MODEL = 1024
DEPTH = 4
ROPE_THETA = 10000.0
EPS = 1e-6
LN_EPS = 1e-5
MLA_H, MLA_NOPE, MLA_ROPE, MLA_V = 8, 64, 32, 64
Q_LORA, KV_LORA = 256, 128
GDN_H, GDN_DK, GDN_DV, CONV_W, GDN_CHUNK = 4, 128, 128, 4, 64
ML_H, ML_DK, ML_DV, ML_CHUNK = 4, 64, 128, 64
SWA_H, SWA_KV, SWA_D, WINDOW = 8, 2, 64, 128
N_EXPERTS, N_GROUPS, TOPK_GROUPS, TOP_K = 64, 8, 4, 8
D_EXPERT, D_SHARED = 256, 256
ROUTED_SCALE = 2.5
DN_ALPHA = (2 * DEPTH) ** 0.25

LANES = 128
SUBLANES = 8
V7X_VMEM_BYTES = 64 * 1024 * 1024
VMEM_LIMIT = V7X_VMEM_BYTES * 3 // 4

EXPERT_BLOCK_MIN = 256
EXPERT_BLOCK_MAX = 1024
STREAMS = 1
EXPERT_SUBBLOCKS = 4
SWA_SEQS_PER_STEP = 8
MLSTM_SEQS_PER_STEP = 2
GDN_SEQS_PER_STEP = 8
SC_CHUNK = 64


def _cparams(sem, vmem=VMEM_LIMIT):
    return pltpu.CompilerParams(dimension_semantics=sem, vmem_limit_bytes=vmem)


def _dot(a, b):
    return jnp.dot(a, b, preferred_element_type=F32)


def _dot_nt(a, b):
    return lax.dot_general(a, b, (((1,), (1,)), ((), ())), preferred_element_type=F32)


def _dot_tn(a, b):
    return lax.dot_general(a, b, (((0,), (0,)), ((), ())), preferred_element_type=F32)


def _split2(a):
    hi = a.astype(BF16)
    lo = (a - hi.astype(F32)).astype(BF16)
    return hi, lo


def _split3(a):
    p1 = a.astype(BF16)
    r = a - p1.astype(F32)
    p2 = r.astype(BF16)
    p3 = (r - p2.astype(F32)).astype(BF16)
    return p1, p2, p3


def _dot3(a, b, dot=_dot):
    ah, al = _split2(a)
    bh, bl = _split2(b)
    return dot(ah, bh) + (dot(ah, bl) + dot(al, bh))


def _dot_sel(sel, b, dot=_dot):
    sel = sel.astype(BF16)
    p1, p2, p3 = _split3(b)
    return dot(sel, p1) + (dot(sel, p2) + dot(sel, p3))


def _sigmoid(x):
    return 1.0 / (1.0 + jnp.exp(-x))


def _softplus(x):
    return jnp.maximum(x, 0.0) + jnp.log(1.0 + jnp.exp(-jnp.abs(x)))


def _silu(x):
    return x * _sigmoid(x)


def _lane_bcast(x, c):
    return jnp.broadcast_to(x[:, c:c + 1], x.shape)


def _iota2(shape, dim):
    return lax.broadcasted_iota(jnp.int32, shape, dim)


def _rope_kernel(pos_ref, rows_ref, sel_ref, cm_ref, sm_ref, cs_ref, ss_ref):
    ang = pos_ref[...] * rows_ref[0:1, :]
    cos_parts = _split3(jnp.cos(ang))
    sin_parts = _split3(jnp.sin(ang))

    def place(parts, k):
        return _dot(parts[0], sel_ref[k]) + (_dot(parts[1], sel_ref[k]) + _dot(parts[2], sel_ref[k]))

    cm_ref[...] = place(cos_parts, 0) + rows_ref[1:2, :]
    sm_ref[...] = place(sin_parts, 1)
    cs_ref[...] = place(cos_parts, 2)
    ss_ref[...] = place(sin_parts, 3)


def _rope_consts():
    hm, hs = MLA_ROPE // 2, SWA_D // 2
    rows = np.zeros((8, LANES), np.float32)
    rows[0, :hm] = ROPE_THETA ** (-(np.arange(0, MLA_ROPE, 2, dtype=np.float32) / MLA_ROPE))
    rows[0, hm:hm + hs] = ROPE_THETA ** (-(np.arange(0, SWA_D, 2, dtype=np.float32) / SWA_D))
    rows[1, :MLA_NOPE] = 1.0
    sel = np.zeros((4, LANES, LANES), np.float32)
    for j in range(hm):
        sel[0, j, MLA_NOPE + j] = sel[0, j, MLA_NOPE + hm + j] = 1.0
        sel[1, j, MLA_NOPE + j] = -1.0
        sel[1, j, MLA_NOPE + hm + j] = 1.0
    for h in range(LANES // SWA_D):
        for j in range(hs):
            sel[2, hm + j, h * SWA_D + j] = sel[2, hm + j, h * SWA_D + hs + j] = 1.0
            sel[3, hm + j, h * SWA_D + j] = -1.0
            sel[3, hm + j, h * SWA_D + hs + j] = 1.0
    return jnp.asarray(rows), jnp.asarray(sel, BF16)


def _rope_tables(pos, tm=512):
    t = pos.shape[0]
    tm = min(tm, t)
    rows, sel = _rope_consts()
    cm, sm, cs, ss = pl.pallas_call(
        _rope_kernel,
        grid=(t // tm,),
        in_specs=[pl.BlockSpec((tm, 1), lambda i: (i, 0)), pl.BlockSpec((8, LANES), lambda i: (0, 0)),
                  pl.BlockSpec((4, LANES, LANES), lambda i: (0, 0, 0))],
        out_specs=[pl.BlockSpec((tm, LANES), lambda i: (i, 0))] * 4,
        out_shape=[jax.ShapeDtypeStruct((t, LANES), F32)] * 4,
        compiler_params=_cparams(("arbitrary",)),
        name="rope_tables",
    )(pos, rows, sel)
    return (cm, sm), (cs, ss)


def _proj_even_kernel(x_ref, w_ref, cw_ref, mla_ref, act_ref, z_ref, g_ref, ext_ref, *, tiles_per_seq):
    tm = x_ref.shape[0]
    o = np.concatenate([[0], np.cumsum(EV_WIDTHS)]).tolist()
    halo = SUBLANES
    tap0 = halo - (CONV_W - 1)

    @pl.when(pl.program_id(0) % tiles_per_seq == 0)
    def _():
        ext_ref[0:halo, :] = jnp.zeros((halo, ext_ref.shape[1]), F32)

    xb = x_ref[...].astype(BF16)
    nchunk = 3
    cw = EV_WIDTHS[1] // nchunk

    def project(ci):
        ext_ref[halo:halo + tm, ci * cw:(ci + 1) * cw] = _dot(xb, w_ref[:, o[1] + ci * cw:o[1] + (ci + 1) * cw])

    project(0)
    for ci in range(nchunk):
        if ci + 1 < nchunk:
            project(ci + 1)
        else:
            mla_ref[...] = _dot(xb, w_ref[:, o[0]:o[1]])
            z_ref[...] = _dot(xb, w_ref[:, o[2]:o[3]]).astype(z_ref.dtype)
            g_ref[...] = _dot(xb, w_ref[:, o[3]:o[4]])
        cols = slice(ci * cw, (ci + 1) * cw)
        conv = cw_ref[0:1, cols] * ext_ref[tap0:tap0 + tm, cols]
        for j in range(1, CONV_W):
            conv = conv + cw_ref[j:j + 1, cols] * ext_ref[tap0 + j:tap0 + j + tm, cols]
        act_ref[:, cols] = _silu(conv).astype(act_ref.dtype)
    ext_ref[0:halo, :] = ext_ref[tm:tm + halo, :]


def _proj_even(x, w, conv_w, seq, tm=512):
    t, k = x.shape
    tm = min(tm, seq)
    row = lambda i: (i, 0)
    fix = lambda i: (0, 0)
    return pl.pallas_call(
        functools.partial(_proj_even_kernel, tiles_per_seq=seq // tm),
        grid=(t // tm,),
        in_specs=[pl.BlockSpec((tm, k), row), pl.BlockSpec(w.shape, fix), pl.BlockSpec(conv_w.shape, fix)],
        out_specs=[pl.BlockSpec((tm, n), row) for n in EV_WIDTHS],
        out_shape=[jax.ShapeDtypeStruct((t, n), F32) for n in EV_WIDTHS],
        scratch_shapes=[pltpu.VMEM((tm + SUBLANES, EV_WIDTHS[1]), F32)],
        compiler_params=_cparams(("arbitrary",)),
        name="in_proj",
    )(x, w, conv_w)


OD_SEG = dict(mq=(0, 512), mk=(512, 1024), mv=(1024, 1536), mo=(1536, 2048), gates=(2048, 2176),
              sq=(2176, 2688), sk=(2688, 2944), sv=(2944, 3456))
OD_COLS = 3456


def _proj_odd_kernel(x_ref, w_ref, c_ref, s_ref, mq_ref, mk_ref, mv_ref, mo_ref, mg_ref, sq_ref, sk_ref, sv_ref):
    xb = x_ref[...].astype(BF16)

    def seg(name):
        a, b = OD_SEG[name]
        return _dot(xb, w_ref[:, a:b])

    mq_ref[...] = seg("mq").astype(mq_ref.dtype)
    mk_ref[...] = seg("mk").astype(mk_ref.dtype)
    mv_ref[...] = seg("mv").astype(mv_ref.dtype)
    mo_ref[...] = seg("mo").astype(mo_ref.dtype)
    mg_ref[...] = seg("gates")
    c = c_ref[...]
    s = s_ref[...]
    def swap_halves(t):
        half = SWA_D // 2
        first_half = (_iota2(t.shape, 1) % SWA_D) < half
        return jnp.where(first_half, pltpu.roll(t, t.shape[1] - half, 1), pltpu.roll(t, half, 1))

    c8 = jnp.concatenate([c] * (SWA_H // 2), axis=1)
    s8 = jnp.concatenate([s] * (SWA_H // 2), axis=1)
    q = seg("sq")
    sq_ref[...] = (q * c8 + swap_halves(q) * s8).astype(sq_ref.dtype)
    c2 = jnp.concatenate([c] * SWA_KV, axis=1)
    s2 = jnp.concatenate([s] * SWA_KV, axis=1)
    k = seg("sk")
    sk_ref[...] = (k * c2 + swap_halves(k) * s2).astype(sk_ref.dtype)
    sv_ref[...] = seg("sv").astype(sv_ref.dtype)


def _proj_odd(x, w, ctab, stab, tm=512):
    t, k = x.shape
    widths = (512, 512, 512, 512, 128, SWA_H * SWA_D, SWA_KV * LANES, 2 * SWA_KV * LANES)
    dtypes = (F32, F32, F32, F32, F32, BF16, BF16, BF16)
    return pl.pallas_call(
        _proj_odd_kernel,
        grid=(t // tm,),
        in_specs=[pl.BlockSpec((tm, k), lambda i: (i, 0)), pl.BlockSpec(w.shape, lambda i: (0, 0)),
                  pl.BlockSpec((tm, LANES), lambda i: (i, 0)), pl.BlockSpec((tm, LANES), lambda i: (i, 0))],
        out_specs=[pl.BlockSpec((tm, n), lambda i: (i, 0)) for n in widths],
        out_shape=[jax.ShapeDtypeStruct((t, n), dt) for n, dt in zip(widths, dtypes)],
        compiler_params=_cparams(("arbitrary",)),
        name="in_proj_odd",
    )(x, w, ctab, stab)


def _rms(x, g):
    return x * lax.rsqrt(jnp.mean(x * x, axis=-1, keepdims=True) + EPS) * g


def _mla_prep_kernel(in_ref, c_ref, s_ref, qn_ref, kvn_ref, wq_ref, wkv_ref, q_ref, k_ref, v_ref):
    hw = MLA_H * LANES
    c = c_ref[...]
    s = s_ref[...]
    c8 = jnp.concatenate([c] * MLA_H, axis=1)
    s8 = jnp.concatenate([s] * MLA_H, axis=1)
    def swap_halves(t):
        half = MLA_ROPE // 2
        first_half = (_iota2(t.shape, 1) % LANES) < MLA_NOPE + half
        return jnp.where(first_half, pltpu.roll(t, t.shape[1] - half, 1), pltpu.roll(t, half, 1))

    cqn = _rms(in_ref[:, 0:Q_LORA], qn_ref[...]).astype(BF16)
    qq = _dot(cqn, wq_ref[...])
    scale = (MLA_NOPE + MLA_ROPE) ** -0.5
    q_ref[...] = ((qq[:, :hw] * c8 + qq[:, hw:] * s8) * scale).astype(q_ref.dtype)
    ckvn = _rms(in_ref[:, Q_LORA:Q_LORA + KV_LORA], kvn_ref[...]).astype(BF16)
    kv = _dot(ckvn, wkv_ref[...])
    o = Q_LORA + KV_LORA
    kr = in_ref[:, o:o + LANES]
    krr = kr * c + swap_halves(kr) * s
    k_ref[...] = (kv[:, :hw] + jnp.concatenate([krr] * MLA_H, axis=1)).astype(k_ref.dtype)
    v_ref[...] = kv[:, hw:].astype(v_ref.dtype)


def _mla_prep(mla_in, ctab, stab, qn, kvn, wq2, wkv2, tm=1024):
    t = mla_in.shape[0]
    tm = min(tm, t)
    hw = MLA_H * LANES
    row = lambda i: (i, 0)
    fix = lambda i: (0, 0)
    return pl.pallas_call(
        _mla_prep_kernel,
        grid=(t // tm,),
        in_specs=[pl.BlockSpec((tm, mla_in.shape[1]), row), pl.BlockSpec((tm, LANES), row), pl.BlockSpec((tm, LANES), row),
                  pl.BlockSpec(qn.shape, fix), pl.BlockSpec(kvn.shape, fix),
                  pl.BlockSpec(wq2.shape, fix), pl.BlockSpec(wkv2.shape, fix)],
        out_specs=[pl.BlockSpec((tm, hw), row)] * 3,
        out_shape=[jax.ShapeDtypeStruct((t, hw), BF16)] * 3,
        compiler_params=_cparams(("arbitrary",)),
        name="mla_prep",
    )(mla_in, ctab, stab, qn, kvn, wq2, wkv2)


def _mla_attn_kernel(q_ref, k_ref, v_ref, o_ref, *, tq):
    i = pl.program_id(2)
    neg = -1e30
    lane = _iota2((tq, LANES), 1)
    ones_lane = (MLA_V, 0)

    def chunk(j, carry, masked):
        start = pl.multiple_of(j * tq, tq)
        out = []
        for hh in range(2):
            m, acc = carry[hh]
            q = q_ref[:, hh * LANES:(hh + 1) * LANES]
            kc = k_ref[pl.ds(start, tq), hh * LANES:(hh + 1) * LANES]
            vc = v_ref[pl.ds(start, tq), hh * LANES:(hh + 1) * LANES]
            vc = jnp.where(lane == ones_lane[hh], jnp.ones_like(vc), vc)
            s = _dot_nt(q, kc)
            if masked:
                s = jnp.where(_iota2(s.shape, 0) >= _iota2(s.shape, 1), s, neg)
            m_new = jnp.maximum(m, jnp.max(s, axis=-1, keepdims=True))
            alpha = jnp.exp(m - m_new)
            p = jnp.exp(s - m_new)
            acc = alpha * acc + _dot(p.astype(BF16), vc)
            out.append((m_new, acc))
        return tuple(out)

    one = (jnp.full((tq, 1), neg, F32), jnp.zeros((tq, LANES), F32))
    carry = lax.fori_loop(0, i, lambda j, c: chunk(j, c, False), (one, one))
    (_, acc0), (_, acc1) = chunk(i, carry, True)
    o0 = acc0 / _lane_bcast(acc0, ones_lane[0])
    o1 = acc1 / _lane_bcast(acc1, ones_lane[1])
    o_ref[...] = jnp.where(lane < MLA_V, o0, o1).astype(o_ref.dtype)


def _mla_attn(q, k, v, batch, seq, tq=512):
    tq = min(tq, seq)
    nq = seq // tq
    pairs = MLA_H // 2
    return pl.pallas_call(
        functools.partial(_mla_attn_kernel, tq=tq),
        grid=(batch, pairs, nq),
        in_specs=[pl.BlockSpec((tq, 2 * LANES), lambda b, p, i: (b * nq + i, p)),
                  pl.BlockSpec((seq, 2 * LANES), lambda b, p, i: (b, p)),
                  pl.BlockSpec((seq, 2 * LANES), lambda b, p, i: (b, p))],
        out_specs=pl.BlockSpec((tq, LANES), lambda b, p, i: (b * nq + i, p)),
        out_shape=jax.ShapeDtypeStruct((batch * seq, pairs * LANES), BF16),
        compiler_params=_cparams(("arbitrary", "arbitrary", "arbitrary")),
        name="mla_attn",
    )(q, k, v)


def _unit_lower_inverse_many(ns):
    c = ns[0].shape[0]
    eye = (_iota2((c, c), 0) == _iota2((c, c), 1)).astype(F32)
    xs = [-n for n in ns]
    ps = [eye + x for x in xs]
    xb = [x.astype(BF16) for x in xs]
    for _ in range(int(math.log2(c)) - 1):
        xs = [_dot(b, b) for b in xb]
        xb = [x.astype(BF16) for x in xs]
        ps = [p + _dot(p.astype(BF16), b) for p, b in zip(ps, xb)]
    return ps


def _gdn_kernel(act_ref, g_ref, z_ref, al_ref, dt_ref, on_ref, o_ref, st_ref):
    c = GDN_CHUNK
    hd = GDN_DK
    nqk = GDN_H * GDN_DK

    @pl.when(pl.program_id(1) == 0)
    def _():
        st_ref[...] = jnp.zeros(st_ref.shape, F32)

    tri = (_iota2((c, c), 0) >= _iota2((c, c), 1)).astype(F32)
    row_ge = _iota2((c, c), 0) >= _iota2((c, c), 1)
    row_gt = _iota2((c, c), 0) > _iota2((c, c), 1)
    lane = _iota2((c, LANES), 1)

    seqs = []
    for bb in range(act_ref.shape[0]):
        gates = g_ref[bb]
        g_all = -jnp.exp(al_ref[...]) * _softplus(gates + dt_ref[...])
        gc_all = _dot_sel(tri, g_all)
        seqs.append(dict(beta_all=_sigmoid(gates), gc_all=gc_all, gc_parts=_split3(gc_all)))
    units = []
    for bb, sq in enumerate(seqs):
        for h in range(GDN_H):
            q = act_ref[bb, :, h * hd:(h + 1) * hd].astype(F32)
            k = act_ref[bb, :, nqk + h * hd:nqk + (h + 1) * hd].astype(F32)
            v = act_ref[bb, :, 2 * nqk + h * GDN_DV:2 * nqk + (h + 1) * GDN_DV].astype(F32)
            q = q * lax.rsqrt(jnp.sum(q * q, axis=-1, keepdims=True) + EPS) * (GDN_DK ** -0.5)
            k = k * lax.rsqrt(jnp.sum(k * k, axis=-1, keepdims=True) + EPS)
            beta = _lane_bcast(sq["beta_all"], h)
            gcol = _lane_bcast(sq["gc_all"], GDN_H + h)
            units.append(dict(bb=bb, h=h, q=q, k=k, v=v, beta=beta, gcol=gcol, kb=k * beta, parts=sq["gc_parts"]))
    for u in units:
        pick = (lane == GDN_H + u["h"]).astype(BF16)
        p0, p1, p2 = u["parts"]
        u["grow"] = _dot_nt(pick, p0) + (_dot_nt(pick, p1) + _dot_nt(pick, p2))
        u["kk"] = _dot3(u["kb"], u["k"], _dot_nt)
        u["qk"] = _dot_nt(u["q"].astype(BF16), u["k"].astype(BF16))
    for u in units:
        gcol = u["gcol"]
        decay = jnp.exp(jnp.where(row_ge, gcol[:, :c] - u["grow"], -jnp.inf))
        eg = jnp.exp(gcol)
        glast = gcol[c - 1:c, :]
        u["lower"] = jnp.where(row_gt, u["kk"] * decay, 0.0)
        u["rhs"] = jnp.concatenate([u["v"] * u["beta"], u["kb"] * eg], axis=1)
        u["attn"] = u["qk"] * decay
        u["qg"] = (u["q"] * eg).astype(BF16)
        u["kg"] = (u["k"] * jnp.exp(glast - gcol)).astype(BF16)
        u["gl"] = jnp.exp(glast)

    tinvs = _unit_lower_inverse_many([u["lower"] for u in units])
    uws = []
    for u, tinv in zip(units, tinvs):
        uws.append(_dot(tinv.astype(BF16), u["rhs"].astype(BF16)))
    states = [st_ref[u["bb"], u["h"]] for u in units]
    sbs = [s.astype(BF16) for s in states]
    vnews = [(uw[:, :GDN_DV] - _dot(uw[:, GDN_DV:].astype(BF16), sb)).astype(BF16) for uw, sb in zip(uws, sbs)]
    for u, state, sb, vnb in zip(units, states, sbs, vnews):
        bb, h = u["bb"], u["h"]
        o = _dot(u["qg"], sb) + _dot(u["attn"].astype(BF16), vnb)
        st_ref[bb, h] = state * u["gl"] + _dot_tn(u["kg"], vnb)
        o = _rms(o, on_ref[...]) * _silu(z_ref[bb, :, h * GDN_DV:(h + 1) * GDN_DV].astype(F32))
        o_ref[bb, :, h * GDN_DV:(h + 1) * GDN_DV] = o.astype(o_ref.dtype)


def _gdn(act, gates, z, a_row, dt_row, o_norm, batch, seq):
    c = GDN_CHUNK
    nc = seq // c
    w3 = act.shape[1]
    wo = GDN_H * GDN_DV
    nb = min(GDN_SEQS_PER_STEP, batch)
    row = lambda b, i: (b, i, 0)
    fix = lambda b, i: (0, 0)
    out = pl.pallas_call(
        _gdn_kernel,
        grid=(batch // nb, nc),
        in_specs=[pl.BlockSpec((nb, c, w3), row), pl.BlockSpec((nb, c, LANES), row), pl.BlockSpec((nb, c, wo), row),
                  pl.BlockSpec((1, LANES), fix), pl.BlockSpec((1, LANES), fix), pl.BlockSpec((1, GDN_DV), fix)],
        out_specs=pl.BlockSpec((nb, c, wo), row),
        out_shape=jax.ShapeDtypeStruct((batch, seq, wo), BF16),
        scratch_shapes=[pltpu.VMEM((nb, GDN_H, GDN_DK, GDN_DV), F32)],
        compiler_params=_cparams(("arbitrary", "arbitrary")),
        name="gdn",
    )(act.reshape(batch, seq, w3), gates.reshape(batch, seq, LANES), z.reshape(batch, seq, wo), a_row, dt_row, o_norm)
    return out.reshape(batch * seq, wo)


def _mlstm_kernel(q_ref, k_ref, v_ref, og_ref, g_ref, bias_ref, nrm_ref, o_ref, c_ref, n_ref, m_ref):
    @pl.when(pl.program_id(1) == 0)
    def _():
        c_ref[...] = jnp.zeros(c_ref.shape, F32)
        n_ref[...] = jnp.zeros(n_ref.shape, F32)
        m_ref[...] = jnp.zeros(m_ref.shape, F32)

    c = ML_CHUNK
    tri = (_iota2((c, c), 0) >= _iota2((c, c), 1)).astype(F32)
    row_ge = _iota2((c, c), 0) >= _iota2((c, c), 1)
    ones = jnp.ones((c, LANES), F32)
    lane = _iota2((c, LANES), 1)

    units = []
    for bb in range(q_ref.shape[0]):
        pre = g_ref[bb] + bias_ref[...]
        logf = jnp.minimum(pre, 0.0) - jnp.log(1.0 + jnp.exp(-jnp.abs(pre)))
        bcum_all = _dot_sel(tri, logf)
        for h in range(ML_H):
            q = q_ref[bb, :, h * LANES:(h + 1) * LANES].astype(F32)
            k = k_ref[bb, :, h * LANES:(h + 1) * LANES].astype(F32) * (ML_DK ** -0.5)
            units.append(dict(bb=bb, h=h, q=q, k=k, qb=q.astype(BF16), vb=v_ref[bb, :, h * ML_DV:(h + 1) * ML_DV].astype(BF16),
                              bcol=_lane_bcast(bcum_all, ML_H + h),
                              icol=_lane_bcast(pre, h),
                              col=jnp.where(lane == h, pre, 0.0) - jnp.where(lane == ML_H + h, bcum_all, 0.0),
                              m_st=m_ref[bb, h], cst=c_ref[bb, h], nst=n_ref[bb, h]))
    for u in units:
        u["row"] = _dot_sel(ones, u["col"], _dot_nt)
        u["qk"] = _dot_nt(u["qb"], u["k"].astype(BF16))
        u["qc"] = _dot(u["qb"], u["cst"].astype(BF16))
    for u in units:
        u["d"] = jnp.where(row_ge, u["bcol"][:, :c] + u["row"], -jnp.inf)
        u["inter"] = u["bcol"] + u["m_st"]
        u["m_t"] = jnp.maximum(u["inter"], jnp.max(u["d"], axis=-1, keepdims=True))
        u["b_end"] = u["bcol"][c - 1:c, :]
        u["a"] = u["b_end"] - u["bcol"] + u["icol"]
        u["m_new"] = jnp.maximum(u["b_end"] + u["m_st"], jnp.max(u["a"], axis=0, keepdims=True))
    for u in units:
        u["w_inter"] = jnp.exp(u["inter"] - u["m_t"])
        u["p"] = jnp.exp(u["d"] - u["m_t"][:, :c]) * u["qk"]
        u["keep"] = jnp.exp(u["b_end"] + u["m_st"] - u["m_new"])
        u["ks"] = u["k"] * jnp.exp(u["a"] - u["m_new"])
    for u in units:
        u["pv"] = _dot(u["p"].astype(BF16), u["vb"])
        u["kv"] = _dot_tn(u["ks"].astype(BF16), u["vb"])
    for u in units:
        u["den"] = (u["w_inter"] * jnp.sum(u["q"] * u["nst"], axis=-1, keepdims=True)
                    + jnp.sum(u["p"], axis=-1, keepdims=True))
    for u in units:
        bb, h = u["bb"], u["h"]
        num = u["w_inter"] * u["qc"] + u["pv"]
        hc = num / jnp.maximum(jnp.abs(u["den"]), jnp.exp(-u["m_t"]))
        c_ref[bb, h] = u["cst"] * u["keep"] + u["kv"]
        n_ref[bb, h] = u["nst"] * u["keep"] + jnp.sum(u["ks"], axis=0, keepdims=True)
        m_ref[bb, h] = u["m_new"]
        hn = (_rms(hc, nrm_ref[:, h * ML_DV:(h + 1) * ML_DV])
              * _sigmoid(og_ref[bb, :, h * ML_DV:(h + 1) * ML_DV].astype(F32)))
        o_ref[bb, :, h * ML_DV:(h + 1) * ML_DV] = hn.astype(o_ref.dtype)


def _mlstm(mq, mk, mv, mo, gates, bias_row, norm_row, batch, seq):
    c = ML_CHUNK
    nc = seq // c
    nb = min(MLSTM_SEQS_PER_STEP, batch)
    row = lambda b, i: (b, i, 0)
    fix = lambda b, i: (0, 0)
    wide = ML_H * LANES
    r3 = lambda a: a.reshape(batch, seq, a.shape[-1])
    out = pl.pallas_call(
        _mlstm_kernel,
        grid=(batch // nb, nc),
        in_specs=[pl.BlockSpec((nb, c, wide), row), pl.BlockSpec((nb, c, wide), row), pl.BlockSpec((nb, c, wide), row),
                  pl.BlockSpec((nb, c, wide), row), pl.BlockSpec((nb, c, LANES), row),
                  pl.BlockSpec((1, LANES), fix), pl.BlockSpec((1, wide), fix)],
        out_specs=pl.BlockSpec((nb, c, wide), row),
        out_shape=jax.ShapeDtypeStruct((batch, seq, wide), BF16),
        scratch_shapes=[pltpu.VMEM((nb, ML_H, LANES, ML_DV), F32), pltpu.VMEM((nb, ML_H, 1, LANES), F32),
                        pltpu.VMEM((nb, ML_H, 1, LANES), F32)],
        compiler_params=_cparams(("arbitrary", "arbitrary")),
        name="mlstm",
    )(r3(mq), r3(mk), r3(mv), r3(mo), r3(gates), bias_row, norm_row)
    return out.reshape(batch * seq, wide)


def _swa_kernel(q_ref, kc_ref, kp_ref, vc_ref, vp_ref, sink_ref, o_ref):
    w = WINDOW
    n = pl.program_id(1)
    scale = SWA_D ** -0.5
    qi = _iota2((w, w), 0)
    kj = _iota2((w, w), 1)
    mask_c = kj <= qi
    mask_p = jnp.logical_and(kj > qi, n > 0)
    grp = SWA_H // SWA_KV
    neg = -1e30
    units = [(bb, h) for bb in range(q_ref.shape[0]) for h in range(SWA_H)]
    scores = []
    half_of_lane = _iota2((w, LANES), 1) // SWA_D
    for bb, h in units:
        g = h // grp
        pair = q_ref[bb, :, (h // 2) * LANES:(h // 2 + 1) * LANES]
        q = jnp.where(half_of_lane == h % 2, pair, jnp.zeros_like(pair))
        scores.append((_dot_nt(q, kc_ref[bb, :, g * LANES:(g + 1) * LANES]),
                       _dot_nt(q, kp_ref[bb, :, g * LANES:(g + 1) * LANES])))
    masked, tops, exps, dens, probs = [], [], [], [], {}
    for sc, sp in scores:
        masked.append((jnp.where(mask_c, sc * scale, neg), jnp.where(mask_p, sp * scale, neg)))
    for (bb, h), (s_c, s_p) in zip(units, masked):
        tops.append(jnp.maximum(jnp.max(jnp.maximum(s_c, s_p), axis=-1, keepdims=True), sink_ref[:, h:h + 1]))
    for (s_c, s_p), m in zip(masked, tops):
        exps.append((jnp.where(mask_c, jnp.exp(s_c - m), 0.0), jnp.where(mask_p, jnp.exp(s_p - m), 0.0)))
    ones_b = jnp.ones((w, LANES), BF16)
    for (bb, h), (p_c, p_p), m in zip(units, exps, tops):
        p_c, p_p = p_c.astype(BF16), p_p.astype(BF16)
        probs[bb, h] = (p_c, p_p)
        dens.append(_dot(p_c, ones_b) + _dot(p_p, ones_b) + jnp.exp(sink_ref[:, h:h + 1] - m))
    inv = {u: 1.0 / den for u, den in zip(units, dens)}
    for bb in range(q_ref.shape[0]):
        for pair in range(SWA_H // 2):
            acc = None
            for sub in range(2):
                h = 2 * pair + sub
                vcol = (2 * (h // grp) + sub) * LANES
                p_c, p_p = probs[bb, h]
                part = (_dot(p_c, vc_ref[bb, :, vcol:vcol + LANES]) + _dot(p_p, vp_ref[bb, :, vcol:vcol + LANES])) * inv[bb, h]
                acc = part if acc is None else acc + part
            o_ref[bb, :, pair * LANES:(pair + 1) * LANES] = acc.astype(o_ref.dtype)


def _swa(sq, sk, sv, sinks_row, batch, seq):
    w = WINDOW
    nb = seq // w
    ns = min(SWA_SEQS_PER_STEP, batch)
    wo = SWA_H * SWA_D
    cur = lambda b, n: (b, n, 0)
    prev = lambda b, n: (b, jnp.maximum(n - 1, 0), 0)
    r3 = lambda a: a.reshape(batch, seq, a.shape[-1])
    q3, k3, v3 = r3(sq), r3(sk), r3(sv)
    out = pl.pallas_call(
        _swa_kernel,
        grid=(batch // ns, nb),
        in_specs=[pl.BlockSpec((ns, w, sq.shape[1]), cur),
                  pl.BlockSpec((ns, w, sk.shape[1]), cur), pl.BlockSpec((ns, w, sk.shape[1]), prev),
                  pl.BlockSpec((ns, w, sv.shape[1]), cur), pl.BlockSpec((ns, w, sv.shape[1]), prev),
                  pl.BlockSpec((1, LANES), lambda b, n: (0, 0))],
        out_specs=pl.BlockSpec((ns, w, wo), cur),
        out_shape=jax.ShapeDtypeStruct((batch, seq, wo), BF16),
        compiler_params=_cparams(("arbitrary", "arbitrary")),
        name="swa",
    )(q3, k3, k3, v3, v3, sinks_row)
    return out.reshape(batch * seq, wo)


def _layer_norm(h, g, b):
    mu = jnp.mean(h, axis=-1, keepdims=True)
    d = h - mu
    var = jnp.mean(d * d, axis=-1, keepdims=True)
    return d * lax.rsqrt(var + LN_EPS) * g + b


def _outproj_kernel(x_ref, a1_ref, a2_ref, w_ref, g_ref, b_ref, o_ref, op_ref):
    k1 = a1_ref.shape[1]
    y = _dot(a1_ref[...].astype(BF16), w_ref[0:k1, :]) + _dot(a2_ref[...].astype(BF16), w_ref[k1:, :])
    h = _layer_norm(DN_ALPHA * x_ref[...] + y, g_ref[...], b_ref[...])
    o_ref[...] = h
    op_ref[...] = _pack_pairs(h)


def _outproj_ln(x, a1, a2, w, g, b, tm=1024):
    t, d = x.shape
    tm = min(tm, t)
    row = lambda i: (i, 0)
    fix = lambda i: (0, 0)
    return pl.pallas_call(
        _outproj_kernel,
        grid=(t // tm,),
        in_specs=[pl.BlockSpec((tm, d), row), pl.BlockSpec((tm, a1.shape[1]), row), pl.BlockSpec((tm, a2.shape[1]), row),
                  pl.BlockSpec(w.shape, fix), pl.BlockSpec((1, d), fix), pl.BlockSpec((1, d), fix)],
        out_specs=[pl.BlockSpec((tm, d), row), pl.BlockSpec((tm, d // 2), row)],
        out_shape=[jax.ShapeDtypeStruct((t, d), F32), jax.ShapeDtypeStruct((t, d // 2), jnp.uint32)],
        compiler_params=_cparams(("arbitrary",)),
        name="outproj_ln",
    )(x, a1, a2, w, g, b)


def _first_index(x, m, iota_f, sentinel):
    return jnp.min(jnp.where(x == m, iota_f, sentinel), axis=0, keepdims=True)


def _router_kernel(x_ref, wt_ref, bias_ref, idx_ref, gate_ref, rank_ref, cnt_ref, carry_ref):
    tm = x_ref.shape[0]
    e = N_EXPERTS
    gs = e // N_GROUPS
    ninf = -jnp.inf

    @pl.when(pl.program_id(0) == 0)
    def _():
        carry_ref[...] = jnp.zeros(carry_ref.shape, F32)

    logits = _dot3(wt_ref[...], x_ref[...], _dot_nt)
    scores = _sigmoid(logits)
    sel = scores + bias_ref[:, 0:1]

    sub_f = _iota2((gs, tm), 0).astype(F32)
    gscore = []
    for g in range(N_GROUPS):
        blk = sel[g * gs:(g + 1) * gs, :]
        m1 = jnp.max(blk, axis=0, keepdims=True)
        i1 = _first_index(blk, m1, sub_f, float(gs))
        m2 = jnp.max(jnp.where(sub_f == i1, ninf, blk), axis=0, keepdims=True)
        gscore.append(m1 + m2)
    gsc = jnp.concatenate(gscore, axis=0)
    grp_f = _iota2((N_GROUPS, tm), 0).astype(F32)
    gmask = jnp.zeros((N_GROUPS, tm), F32)
    for _ in range(TOPK_GROUPS):
        m = jnp.max(gsc, axis=0, keepdims=True)
        gi = _first_index(gsc, m, grp_f, float(N_GROUPS))
        hit = grp_f == gi
        gmask = jnp.where(hit, 1.0, gmask)
        gsc = jnp.where(hit, ninf, gsc)
    masked = jnp.concatenate(
        [jnp.where(gmask[g:g + 1, :] > 0.0, sel[g * gs:(g + 1) * gs, :], ninf) for g in range(N_GROUPS)], axis=0)

    exp_f = _iota2((e, tm), 0).astype(F32)
    chosen = jnp.zeros((e, tm), F32)
    idxs, gates = [], []
    for _ in range(TOP_K):
        m = jnp.max(masked, axis=0, keepdims=True)
        ei = _first_index(masked, m, exp_f, float(e))
        hit = exp_f == ei
        idxs.append(ei)
        gates.append(jnp.sum(jnp.where(hit, scores, 0.0), axis=0, keepdims=True))
        chosen = jnp.where(hit, 1.0, chosen)
        masked = jnp.where(hit, ninf, masked)
    gate = jnp.concatenate(gates, axis=0)
    gate = gate / jnp.sum(gate, axis=0, keepdims=True) * ROUTED_SCALE
    idx_f = jnp.concatenate(idxs, axis=0)

    upper = (_iota2((tm, tm), 0) < _iota2((tm, tm), 1)).astype(BF16)
    before = _dot(chosen.astype(BF16), upper) + carry_ref[...][:, 0:1]
    ranks = [jnp.sum(jnp.where(exp_f == idxs[k], before, 0.0), axis=0, keepdims=True) for k in range(TOP_K)]
    carry_ref[...] = carry_ref[...] + jnp.sum(chosen, axis=1, keepdims=True)

    idx_ref[...] = idx_f.astype(jnp.int32)
    gate_ref[...] = gate
    rank_ref[...] = jnp.concatenate(ranks, axis=0).astype(jnp.int32)
    cnt_ref[...] = carry_ref[...]


def _router(x, wt, bias_col, tm=512):
    t, d = x.shape
    col = lambda i: (0, i)
    fix = lambda i: (0, 0)
    return pl.pallas_call(
        _router_kernel,
        grid=(t // tm,),
        in_specs=[pl.BlockSpec((tm, d), lambda i: (i, 0)), pl.BlockSpec(wt.shape, fix), pl.BlockSpec((N_EXPERTS, LANES), fix)],
        out_specs=[pl.BlockSpec((TOP_K, tm), col), pl.BlockSpec((TOP_K, tm), col), pl.BlockSpec((TOP_K, tm), col),
                   pl.BlockSpec((N_EXPERTS, LANES), fix)],
        out_shape=[jax.ShapeDtypeStruct((TOP_K, t), jnp.int32), jax.ShapeDtypeStruct((TOP_K, t), F32),
                   jax.ShapeDtypeStruct((TOP_K, t), jnp.int32), jax.ShapeDtypeStruct((N_EXPERTS, LANES), F32)],
        scratch_shapes=[pltpu.VMEM((N_EXPERTS, LANES), F32)],
        compiler_params=_cparams(("arbitrary",)),
        name="router",
    )(x, wt, bias_col)


def _dest_kernel(idx_ref, rank_ref, start_ref, dest_ref):
    tm = idx_ref.shape[1]
    exp_i = _iota2((N_EXPERTS, tm), 0)
    start = start_ref[:, 0:1]
    rows = [jnp.sum(jnp.where(exp_i == idx_ref[s:s + 1, :], start, 0.0), axis=0, keepdims=True) for s in range(TOP_K)]
    dest_ref[...] = jnp.concatenate(rows, axis=0).astype(jnp.int32) + rank_ref[...]


def _dest_rows(idx, rank, start_col, tm=2048):
    t = idx.shape[1]
    tm = min(tm, t)
    col = lambda i: (0, i)
    return pl.pallas_call(
        _dest_kernel,
        grid=(t // tm,),
        in_specs=[pl.BlockSpec((TOP_K, tm), col), pl.BlockSpec((TOP_K, tm), col),
                  pl.BlockSpec((N_EXPERTS, LANES), lambda i: (0, 0))],
        out_specs=pl.BlockSpec((TOP_K, tm), col),
        out_shape=jax.ShapeDtypeStruct((TOP_K, t), jnp.int32),
        compiler_params=_cparams(("arbitrary",)),
        name="moe_dest",
    )(idx, rank, start_col)


def _pack_pairs(x):
    n = x.shape[1] // 2
    hi = lax.bitcast_convert_type(x[:, :n].astype(BF16).astype(F32), jnp.uint32)
    lo = lax.bitcast_convert_type(x[:, n:].astype(BF16).astype(F32), jnp.uint32)
    return hi | (lo >> 16)


def _unpack_pairs(w):
    hi = lax.bitcast_convert_type(w & jnp.uint32(0xFFFF0000), F32)
    lo = lax.bitcast_convert_type(w << 16, F32)
    return hi, lo


def _sc_scatter_rows(xp, dest, rows, chunk=LANES):
    t, width = xp.shape
    info = plsc.get_sparse_core_info()
    ncores, nsub = info.num_cores, info.num_subcores
    per_worker = t // (ncores * nsub)
    nchunk = per_worker // chunk
    mesh = plsc.VectorSubcoreMesh(core_axis_name="c", subcore_axis_name="s")

    @functools.partial(
        pl.kernel, mesh=mesh,
        out_type=jax.ShapeDtypeStruct((rows, width), xp.dtype),
        scratch_types=[pltpu.VMEM((TOP_K, chunk), jnp.int32), pltpu.VMEM((chunk, width), xp.dtype), pltpu.SemaphoreType.DMA],
    )
    def scatter(xp_hbm, dest_hbm, out_hbm, idx_v, rows_v, sem):
        base = (lax.axis_index("s") * ncores + lax.axis_index("c")) * per_worker

        @pl.loop(0, nchunk)
        def _(i):
            off = pl.multiple_of(base + i * chunk, chunk)
            pltpu.sync_copy(dest_hbm.at[:, pl.ds(off, chunk)], idx_v)
            pltpu.sync_copy(xp_hbm.at[pl.ds(off, chunk)], rows_v)
            copies = [pltpu.async_copy(rows_v, out_hbm.at[idx_v.at[s]], sem) for s in range(TOP_K)]
            for cp in copies:
                cp.wait()

    return scatter(xp, dest)


def _experts_kernel(be_ref, nu_ref, nv_ref, first_ref, slot_ref, nxt_ref, xs_ref, wg_hbm, wu_hbm, wd_hbm, ys_ref,
                    wgf_ref, wuf_ref, wdf_ref, wgb_ref, wub_ref, wdb_ref, sem, *, layer):
    i = pl.program_id(0)

    def fetch(e, s):
        return [pltpu.make_async_copy(wg_hbm.at[layer, e], wgf_ref.at[s], sem.at[s]),
                pltpu.make_async_copy(wu_hbm.at[layer, e], wuf_ref.at[s], sem.at[s]),
                pltpu.make_async_copy(wd_hbm.at[layer, e], wdf_ref.at[s], sem.at[s])]

    @pl.when(i == 0)
    def _():
        for cp in fetch(be_ref[0], 0):
            cp.start()

    @pl.when(jnp.logical_and(first_ref[i] == 1, i < nu_ref[0]))
    def _():
        s = slot_ref[i]
        for cp in fetch(be_ref[i], s):
            cp.wait()
        wgb_ref[...] = wgf_ref[s].astype(BF16)
        wub_ref[...] = wuf_ref[s].astype(BF16)
        wdb_ref[...] = wdf_ref[s].astype(BF16)

        @pl.when(nxt_ref[i] >= 0)
        def _():
            for cp in fetch(nxt_ref[i], 1 - s):
                cp.start()

    @pl.when(i < nu_ref[0])
    def _():
        sub = xs_ref.shape[0] // EXPERT_SUBBLOCKS
        acts = []
        for r in range(EXPERT_SUBBLOCKS):
            rows = pl.ds(r * sub, sub)
            live = (_iota2((sub, 1), 0) + r * sub) < nv_ref[i]
            xa, xb = _unpack_pairs(jnp.where(live, xs_ref[rows, :], jnp.uint32(0)))
            x = jnp.concatenate([xa.astype(BF16), xb.astype(BF16)], axis=1)
            acts.append((_dot(x, wgb_ref[...]), _dot(x, wub_ref[...])))
        outs = [_dot((_silu(gate) * up).astype(BF16), wdb_ref[...]) for gate, up in acts]
        for r, y in enumerate(outs):
            ys_ref[pl.ds(r * sub, sub), :] = _pack_pairs(y)


def _experts(block_e, n_used, n_valid, xs, wg, wu, wd, layer, block):
    rows, half = xs.shape
    d = 2 * half
    nb = rows // block
    pos = jnp.arange(nb, dtype=jnp.int32)
    first = jnp.concatenate([jnp.ones((1,), jnp.int32), (block_e[1:] != block_e[:-1]).astype(jnp.int32)])
    slot = (jnp.cumsum(first) - 1) % 2
    later = (pos[None, :] > pos[:, None]) & (block_e[None, :] != block_e[:, None]) & (pos[None, :] < n_used[0])
    nxt_pos = jnp.min(jnp.where(later, pos[None, :], nb), axis=1)
    nxt = jnp.where(nxt_pos < nb, block_e[jnp.minimum(nxt_pos, nb - 1)], -1)
    blk = lambda i, be, nu, *rest: (jnp.minimum(i, nu[0] - 1), 0)
    hbm = pl.BlockSpec(memory_space=pl.ANY)
    return pl.pallas_call(
        functools.partial(_experts_kernel, layer=layer),
        grid_spec=pltpu.PrefetchScalarGridSpec(
            num_scalar_prefetch=6,
            grid=(nb,),
            in_specs=[pl.BlockSpec((block, half), blk), hbm, hbm, hbm],
            out_specs=pl.BlockSpec((block, half), blk),
            scratch_shapes=[pltpu.VMEM((2, d, D_EXPERT), F32), pltpu.VMEM((2, d, D_EXPERT), F32),
                            pltpu.VMEM((2, D_EXPERT, d), F32),
                            pltpu.VMEM((d, D_EXPERT), BF16), pltpu.VMEM((d, D_EXPERT), BF16),
                            pltpu.VMEM((D_EXPERT, d), BF16), pltpu.SemaphoreType.DMA((2,))],
        ),
        out_shape=jax.ShapeDtypeStruct((rows, half), jnp.uint32),
        compiler_params=_cparams(("arbitrary",)),
        name="moe_experts",
    )(block_e, n_used, n_valid, first, slot.astype(jnp.int32), nxt.astype(jnp.int32), xs, wg, wu, wd)


def _sc_gather_rows(table, idx, chunk=SC_CHUNK):
    n = idx.shape[0]
    width = table.shape[1]
    info = plsc.get_sparse_core_info()
    ncores, nsub = info.num_cores, info.num_subcores
    per_worker = n // (ncores * nsub)
    nchunk = per_worker // chunk
    mesh = plsc.VectorSubcoreMesh(core_axis_name="c", subcore_axis_name="s")

    @functools.partial(
        pl.kernel, mesh=mesh,
        out_type=jax.ShapeDtypeStruct((n, width), table.dtype),
        scratch_types=[pltpu.VMEM((nchunk, chunk), jnp.int32), pltpu.VMEM((2, chunk, width), table.dtype),
                       pltpu.SemaphoreType.DMA((2,)), pltpu.SemaphoreType.DMA((2,))],
    )
    def gather(table_hbm, idx_hbm, out_hbm, idx_v, rows_v, gsem, wsem):
        wid = lax.axis_index("s") * ncores + lax.axis_index("c")
        base = wid * per_worker
        pltpu.sync_copy(idx_hbm.at[pl.ds(wid * nchunk, nchunk)], idx_v)

        def fetch(j, b):
            return pltpu.make_async_copy(table_hbm.at[idx_v.at[j]], rows_v.at[b], gsem.at[b])

        def flush(j, b):
            off = pl.multiple_of(base + j * chunk, chunk)
            return pltpu.make_async_copy(rows_v.at[b], out_hbm.at[pl.ds(off, chunk)], wsem.at[b])

        fetch(0, 0).start()

        @pl.loop(0, nchunk, step=2)
        def _(i):
            for b in range(2):
                j = i + b
                fetch(j, b).wait()

                @pl.when(j + 1 < nchunk)
                def _():
                    @pl.when(j >= 1)
                    def _():
                        flush(j - 1, 1 - b).wait()

                    fetch(j + 1, 1 - b).start()

                flush(j, b).start()

        flush(nchunk - 2, 0).wait()
        flush(nchunk - 1, 1).wait()

    return gather(table, idx.reshape(n // chunk, chunk))


def _shared_kernel(xp_ref, sg_ref, su_ref, sd_ref, o_ref):
    xa, xb = _unpack_pairs(xp_ref[...])
    x = jnp.concatenate([xa.astype(BF16), xb.astype(BF16)], axis=1)
    hs = _silu(_dot(x, sg_ref[...])) * _dot(x, su_ref[...])
    o_ref[...] = _pack_pairs(_dot(hs.astype(BF16), sd_ref[...]))


def _shared_expert(xp, sg, su, sd, tm=512):
    t, half = xp.shape
    row = lambda i: (i, 0)
    fix = lambda i: (0, 0)
    return pl.pallas_call(
        _shared_kernel,
        grid=(t // tm,),
        in_specs=[pl.BlockSpec((tm, half), row), pl.BlockSpec(sg.shape, fix), pl.BlockSpec(su.shape, fix),
                  pl.BlockSpec(sd.shape, fix)],
        out_specs=pl.BlockSpec((tm, half), row),
        out_shape=jax.ShapeDtypeStruct((t, half), jnp.uint32),
        compiler_params=_cparams(("arbitrary",)),
        name="moe_shared",
    )(xp, sg, su, sd)


def _combine_kernel(x_ref, gate_ref, rows_ref, sh_ref, g_ref, b_ref, o_ref):
    gate = gate_ref[...]
    ya, yb = _unpack_pairs(sh_ref[...])
    for s in range(TOP_K):
        a, b = _unpack_pairs(rows_ref[s])
        ya = ya + gate[:, s:s + 1] * a
        yb = yb + gate[:, s:s + 1] * b
    ff = jnp.concatenate([ya, yb], axis=1)
    o_ref[...] = _layer_norm(DN_ALPHA * x_ref[...] + ff, g_ref[...], b_ref[...])


def _combine(x, gate_t, rows, shared, g, b, tm=512):
    t, d = x.shape
    row = lambda i: (i, 0)
    fix = lambda i: (0, 0)
    return pl.pallas_call(
        _combine_kernel,
        grid=(t // tm,),
        in_specs=[pl.BlockSpec((tm, d), row), pl.BlockSpec((tm, TOP_K), row),
                  pl.BlockSpec((TOP_K, tm, d // 2), lambda i: (0, i, 0)), pl.BlockSpec((tm, d // 2), row),
                  pl.BlockSpec((1, d), fix), pl.BlockSpec((1, d), fix)],
        out_specs=pl.BlockSpec((tm, d), row),
        out_shape=jax.ShapeDtypeStruct((t, d), F32),
        compiler_params=_cparams(("arbitrary",)),
        name="moe_combine",
    )(x, gate_t, rows, shared, g, b)


def _take_cols(w, idx):
    idx = np.asarray(idx)
    runs, start = [], 0
    for pos in range(1, len(idx) + 1):
        run_ends = pos == len(idx) or (idx[pos] != idx[pos - 1] + 1 if idx[pos - 1] >= 0 else idx[pos] >= 0)
        if run_ends:
            runs.append((start, int(idx[start]), pos - start))
            start = pos

    def body(w_ref, o_ref):
        for dst, src, width in runs:
            if src < 0:
                o_ref[:, dst:dst + width] = jnp.zeros((o_ref.shape[0], width), o_ref.dtype)
            else:
                o_ref[:, dst:dst + width] = w_ref[:, src:src + width].astype(o_ref.dtype)

    rows = w.shape[0]
    tr = min(rows, 256)
    return pl.pallas_call(
        body,
        grid=(rows // tr,),
        in_specs=[pl.BlockSpec((tr, w.shape[1]), lambda i: (i, 0))],
        out_specs=pl.BlockSpec((tr, len(idx)), lambda i: (i, 0)),
        out_shape=jax.ShapeDtypeStruct((rows, len(idx)), BF16),
        compiler_params=_cparams(("arbitrary",)),
        name="weight_cols",
    )(w)


def _pad_lane_row(v, first_lane, width=LANES):
    out = jnp.zeros((1, width), F32)
    return lax.dynamic_update_slice(out, v.reshape(1, -1).astype(F32), (0, first_lane))


def _even_in_cols():
    z = lambda n: -np.ones(n, int)
    kr0 = Q_LORA + KV_LORA
    cols = [np.arange(0, Q_LORA), np.arange(Q_LORA, Q_LORA + KV_LORA),
            z(MLA_NOPE), np.arange(kr0, kr0 + MLA_ROPE), z(LANES - MLA_NOPE - MLA_ROPE)]
    g0 = kr0 + MLA_ROPE
    nqk = GDN_H * GDN_DK
    cols.append(np.arange(g0, g0 + 3 * nqk))
    zoff = g0 + 3 * nqk + 2 * GDN_H
    cols.append(np.arange(zoff, zoff + GDN_H * GDN_DV))
    cols += [np.arange(g0 + 3 * nqk, g0 + 3 * nqk + 2 * GDN_H), z(LANES - 2 * GDN_H)]
    return np.concatenate(cols)


EV_WIDTHS = (Q_LORA + KV_LORA + LANES, 3 * GDN_H * GDN_DK, GDN_H * GDN_DV, LANES)


def _mla_q_cols():
    per = MLA_NOPE + MLA_ROPE
    half = MLA_ROPE // 2
    main, sw = [], []
    for h in range(MLA_H):
        b = h * per
        main += [np.arange(b, b + per), -np.ones(LANES - per, int)]
        sw += [-np.ones(MLA_NOPE, int), np.arange(b + MLA_NOPE + half, b + per), np.arange(b + MLA_NOPE, b + MLA_NOPE + half),
               -np.ones(LANES - per, int)]
    return np.concatenate(main + sw)


def _mla_kv_cols():
    per = MLA_NOPE + MLA_V
    kc, vc = [], []
    for h in range(MLA_H):
        b = h * per
        kc += [np.arange(b, b + MLA_NOPE), -np.ones(LANES - MLA_NOPE, int)]
        vv = np.arange(b + MLA_NOPE, b + per)
        pad = -np.ones(LANES - MLA_V, int)
        vc += [vv, pad] if h % 2 == 0 else [pad, vv]
    return np.concatenate(kc + vc)


def _odd_in_cols():
    z = lambda n: -np.ones(n, int)
    o = 0
    cols = []
    mq0, mk0 = 0, ML_H * ML_DK
    for base in (mq0, mk0):
        for h in range(ML_H):
            cols += [np.arange(base + h * ML_DK, base + (h + 1) * ML_DK), z(LANES - ML_DK)]
    mv0 = 2 * ML_H * ML_DK
    cols.append(np.arange(mv0, mv0 + ML_H * ML_DV))
    mi0 = mv0 + ML_H * ML_DV
    mo0 = mi0 + 2 * ML_H
    cols.append(np.arange(mo0, mo0 + ML_H * ML_DV))
    cols += [np.arange(mi0, mi0 + 2 * ML_H), z(LANES - 2 * ML_H)]
    sq0 = mo0 + ML_H * ML_DV
    sk0 = sq0 + SWA_H * SWA_D
    sv0 = sk0 + SWA_KV * SWA_D
    half = SWA_D // 2

    cols.append(np.arange(sq0, sq0 + SWA_H * SWA_D))
    for g in range(SWA_KV):
        cols += [np.arange(sk0 + g * SWA_D, sk0 + (g + 1) * SWA_D)] * 2
    for g in range(SWA_KV):
        vv = np.arange(sv0 + g * SWA_D, sv0 + (g + 1) * SWA_D)
        cols += [vv, z(LANES - SWA_D), z(LANES - SWA_D), vv]
    return np.concatenate(cols)


def _even_weights(w_in, w_qb, w_kvb):
    return (_take_cols(w_in, _even_in_cols()), _take_cols(w_qb, _mla_q_cols()), _take_cols(w_kvb, _mla_kv_cols()))


def _even_mixer(x, tabs, weights, q_norm, kv_norm, conv_w, a_log, dt_bias, o_norm, batch, seq):
    ctab, stab = tabs
    w, wq2, wkv2 = weights
    mla_in, act, z, gates = _proj_even(x, w, conv_w, seq)
    q, k, v = _mla_prep(mla_in, ctab, stab, q_norm.reshape(1, -1), kv_norm.reshape(1, -1), wq2, wkv2)
    o_a = _mla_attn(q, k, v, batch, seq)
    o_b = _gdn(act, gates, z, _pad_lane_row(a_log, GDN_H), _pad_lane_row(dt_bias, GDN_H),
               o_norm.reshape(1, -1), batch, seq)
    return o_a, o_b


def _odd_mixer(x, tabs, w, b_i, b_f, ml_norm, sinks, batch, seq):
    ctab, stab = tabs
    mq, mk, mv, mo, mg, sq, sk, sv = _proj_odd(x, w, ctab, stab)
    bias_row = _pad_lane_row(jnp.concatenate([b_i, b_f]), 0)
    o_c = _mlstm(mq, mk, mv, mo, mg, bias_row, ml_norm.reshape(1, -1), batch, seq)
    o_d = _swa(sq, sk, sv, _pad_lane_row(sinks, 0), batch, seq)
    return o_c, o_d


def _moe(x, xp, router_w, router_b, w_gate, w_up, w_down, layer, s_gate, s_up, s_down, ln_g, ln_b):
    t, d = x.shape
    bias_col = jnp.broadcast_to(router_b.reshape(-1, 1).astype(F32), (N_EXPERTS, LANES))
    idx, gate, rank, cnt = _router(x, router_w.T, bias_col)
    counts = cnt[:, 0].astype(jnp.int32)
    block = int(min(max(pl.next_power_of_2(t * TOP_K // N_EXPERTS) // 2, EXPERT_BLOCK_MIN), EXPERT_BLOCK_MAX))
    padded = (counts + block - 1) // block * block
    pad_end = jnp.cumsum(padded)
    pad_start = pad_end - padded
    start_col = jnp.broadcast_to(pad_start.astype(F32).reshape(-1, 1), (N_EXPERTS, LANES))
    dest = _dest_rows(idx, rank, start_col)
    n_blocks = t * TOP_K // block + N_EXPERTS
    rows = n_blocks * block
    block_row = jnp.arange(n_blocks, dtype=jnp.int32) * block
    block_e = jnp.minimum(jnp.sum((pad_end[None, :] <= block_row[:, None]).astype(jnp.int32), axis=1), N_EXPERTS - 1)
    n_used = (pad_end[-1:] // block).astype(jnp.int32)
    live_end = jnp.sum(jnp.where(block_e[:, None] == jnp.arange(N_EXPERTS, dtype=jnp.int32)[None, :],
                                 (pad_start + counts)[None, :], 0), axis=1)
    n_valid = jnp.clip(live_end - block_row, 0, block).astype(jnp.int32)
    xs = _sc_scatter_rows(xp, dest, rows)
    ys = _experts(block_e, n_used, n_valid, xs, w_gate, w_up, w_down, layer, block)
    picked = _sc_gather_rows(ys, dest.reshape(-1)).reshape(TOP_K, t, d // 2)
    shared = _shared_expert(xp, s_gate.astype(BF16), s_up.astype(BF16), s_down.astype(BF16))
    return _combine(x, gate.T, picked, shared, ln_g.reshape(1, -1), ln_b.reshape(1, -1))


def kernel(x, positions, ev_w_in, mla_q_norm, mla_w_qb, mla_kv_norm, mla_w_kvb, gdn_conv, gdn_a_log, gdn_dt_bias, gdn_norm, ev_w_out, od_w_in, mlstm_b_i, mlstm_b_f, mlstm_norm, swa_sinks, od_w_out, ln1_g, ln1_b, router_w, router_b, moe_w_gate, moe_w_up, moe_w_down, shared_w_gate, shared_w_up, shared_w_down, ln2_g, ln2_b):
    batch, seq, d = x.shape
    streams = STREAMS if batch % STREAMS == 0 else 1
    sb = batch // streams
    ts = sb * seq
    hs, tabs_m, tabs_s = [], [], []
    for s in range(streams):
        pos = positions[s * sb:(s + 1) * sb].reshape(ts, 1).astype(F32)
        tm_, ts_ = _rope_tables(pos)
        tabs_m.append(tm_)
        tabs_s.append(ts_)
        hs.append(x[s * sb:(s + 1) * sb].reshape(ts, d))
    for layer in range(DEPTH):
        j = layer // 2
        if layer % 2 == 0:
            weights = _even_weights(ev_w_in[j], mla_w_qb[j], mla_w_kvb[j])
            w_out = ev_w_out[j].astype(BF16)
        else:
            weights = _take_cols(od_w_in[j], _odd_in_cols())
            w_out = od_w_out[j].astype(BF16)
        for s in range(streams):
            h = hs[s]
            if layer % 2 == 0:
                a1, a2 = _even_mixer(h, tabs_m[s], weights, mla_q_norm[j], mla_kv_norm[j], gdn_conv[j], gdn_a_log[j],
                                     gdn_dt_bias[j], gdn_norm[j], sb, seq)
            else:
                a1, a2 = _odd_mixer(h, tabs_s[s], weights, mlstm_b_i[j], mlstm_b_f[j], mlstm_norm[j], swa_sinks[j], sb, seq)
            h, hp = _outproj_ln(h, a1, a2, w_out, ln1_g[layer].reshape(1, -1), ln1_b[layer].reshape(1, -1))
            hs[s] = _moe(h, hp, router_w[layer], router_b[layer], moe_w_gate, moe_w_up, moe_w_down, layer,
                         shared_w_gate[layer], shared_w_up[layer], shared_w_down[layer], ln2_g[layer], ln2_b[layer])
    return jnp.concatenate([h.reshape(sb, seq, d) for h in hs], axis=0)
```

```python
import functools
import math

import numpy as np
import jax
import jax.numpy as jnp
from jax import lax
from jax.experimental import pallas as pl
from jax.experimental.pallas import tpu as pltpu
from jax.experimental.pallas import tpu_sc as plsc

F32 = jnp.float32
BF16 = jnp.bfloat16

D_MODEL = 1024
DEPTH = 4
ROPE_THETA = 10000.0
EPS = 1e-6
LN_EPS = 1e-5
MLA_H, MLA_NOPE, MLA_ROPE, MLA_V = 8, 64, 32, 64
Q_LORA, KV_LORA = 256, 128
GDN_H, GDN_DK, GDN_DV, CONV_W, GDN_CHUNK = 4, 128, 128, 4, 64
ML_H, ML_DK, ML_DV, ML_CHUNK = 4, 64, 128, 64
SWA_H, SWA_KV, SWA_D, WINDOW = 8, 2, 64, 128
N_EXPERTS, N_GROUPS, TOPK_GROUPS, TOP_K = 64, 8, 4, 8
D_EXPERT, D_SHARED = 256, 256
ROUTED_SCALE = 2.5
DN_ALPHA = (2 * DEPTH) ** 0.25

LANES = 128
SUBLANES = 8
V7X_VMEM_BYTES = 64 * 1024 * 1024
VMEM_LIMIT = V7X_VMEM_BYTES * 3 // 4

EXPERT_BLOCK_MIN = 256
EXPERT_BLOCK_MAX = 1024
STREAMS = 1
EXPERT_SUBBLOCKS = 4
SWA_SEQS_PER_STEP = 8
MLSTM_SEQS_PER_STEP = 2
GDN_SEQS_PER_STEP = 8
SC_CHUNK = 64


def _cparams(sem, vmem=VMEM_LIMIT):
    return pltpu.CompilerParams(dimension_semantics=sem, vmem_limit_bytes=vmem)


def _dot(a, b):
    return jnp.dot(a, b, preferred_element_type=F32)


def _dot_nt(a, b):
    return lax.dot_general(a, b, (((1,), (1,)), ((), ())), preferred_element_type=F32)


def _dot_tn(a, b):
    return lax.dot_general(a, b, (((0,), (0,)), ((), ())), preferred_element_type=F32)


def _split2(a):
    hi = a.astype(BF16)
    lo = (a - hi.astype(F32)).astype(BF16)
    return hi, lo


def _split3(a):
    p1 = a.astype(BF16)
    r = a - p1.astype(F32)
    p2 = r.astype(BF16)
    p3 = (r - p2.astype(F32)).astype(BF16)
    return p1, p2, p3


def _dot3(a, b, dot=_dot):
    ah, al = _split2(a)
    bh, bl = _split2(b)
    return dot(ah, bh) + (dot(ah, bl) + dot(al, bh))


def _dot_sel(sel, b, dot=_dot):
    sel = sel.astype(BF16)
    p1, p2, p3 = _split3(b)
    return dot(sel, p1) + (dot(sel, p2) + dot(sel, p3))


def _sigmoid(x):
    return 1.0 / (1.0 + jnp.exp(-x))


def _softplus(x):
    return jnp.maximum(x, 0.0) + jnp.log(1.0 + jnp.exp(-jnp.abs(x)))


def _silu(x):
    return x * _sigmoid(x)


def _lane_bcast(x, c):
    return jnp.broadcast_to(x[:, c:c + 1], x.shape)


def _iota2(shape, dim):
    return lax.broadcasted_iota(jnp.int32, shape, dim)


def _rope_kernel(pos_ref, rows_ref, sel_ref, cm_ref, sm_ref, cs_ref, ss_ref):
    ang = pos_ref[...] * rows_ref[0:1, :]
    cos_parts = _split3(jnp.cos(ang))
    sin_parts = _split3(jnp.sin(ang))

    def place(parts, k):
        return _dot(parts[0], sel_ref[k]) + (_dot(parts[1], sel_ref[k]) + _dot(parts[2], sel_ref[k]))

    cm_ref[...] = place(cos_parts, 0) + rows_ref[1:2, :]
    sm_ref[...] = place(sin_parts, 1)
    cs_ref[...] = place(cos_parts, 2)
    ss_ref[...] = place(sin_parts, 3)


def _rope_consts():
    hm, hs = MLA_ROPE // 2, SWA_D // 2
    rows = np.zeros((8, LANES), np.float32)
    rows[0, :hm] = ROPE_THETA ** (-(np.arange(0, MLA_ROPE, 2, dtype=np.float32) / MLA_ROPE))
    rows[0, hm:hm + hs] = ROPE_THETA ** (-(np.arange(0, SWA_D, 2, dtype=np.float32) / SWA_D))
    rows[1, :MLA_NOPE] = 1.0
    sel = np.zeros((4, LANES, LANES), np.float32)
    for j in range(hm):
        sel[0, j, MLA_NOPE + j] = sel[0, j, MLA_NOPE + hm + j] = 1.0
        sel[1, j, MLA_NOPE + j] = -1.0
        sel[1, j, MLA_NOPE + hm + j] = 1.0
    for h in range(LANES // SWA_D):
        for j in range(hs):
            sel[2, hm + j, h * SWA_D + j] = sel[2, hm + j, h * SWA_D + hs + j] = 1.0
            sel[3, hm + j, h * SWA_D + j] = -1.0
            sel[3, hm + j, h * SWA_D + hs + j] = 1.0
    return jnp.asarray(rows), jnp.asarray(sel, BF16)


def _rope_tables(pos, tm=512):
    t = pos.shape[0]
    tm = min(tm, t)
    rows, sel = _rope_consts()
    cm, sm, cs, ss = pl.pallas_call(
        _rope_kernel,
        grid=(t // tm,),
        in_specs=[pl.BlockSpec((tm, 1), lambda i: (i, 0)), pl.BlockSpec((8, LANES), lambda i: (0, 0)),
                  pl.BlockSpec((4, LANES, LANES), lambda i: (0, 0, 0))],
        out_specs=[pl.BlockSpec((tm, LANES), lambda i: (i, 0))] * 4,
        out_shape=[jax.ShapeDtypeStruct((t, LANES), F32)] * 4,
        compiler_params=_cparams(("arbitrary",)),
        name="rope_tables",
    )(pos, rows, sel)
    return (cm, sm), (cs, ss)


def _proj_even_kernel(x_ref, w_ref, cw_ref, mla_ref, act_ref, z_ref, g_ref, ext_ref, *, tiles_per_seq):
    tm = x_ref.shape[0]
    o = np.concatenate([[0], np.cumsum(EV_WIDTHS)]).tolist()
    halo = SUBLANES
    tap0 = halo - (CONV_W - 1)

    @pl.when(pl.program_id(0) % tiles_per_seq == 0)
    def _():
        ext_ref[0:halo, :] = jnp.zeros((halo, ext_ref.shape[1]), F32)

    xb = x_ref[...].astype(BF16)
    nchunk = 3
    cw = EV_WIDTHS[1] // nchunk

    def project(ci):
        ext_ref[halo:halo + tm, ci * cw:(ci + 1) * cw] = _dot(xb, w_ref[:, o[1] + ci * cw:o[1] + (ci + 1) * cw])

    project(0)
    for ci in range(nchunk):
        if ci + 1 < nchunk:
            project(ci + 1)
        else:
            mla_ref[...] = _dot(xb, w_ref[:, o[0]:o[1]])
            z_ref[...] = _dot(xb, w_ref[:, o[2]:o[3]]).astype(z_ref.dtype)
            g_ref[...] = _dot(xb, w_ref[:, o[3]:o[4]])
        cols = slice(ci * cw, (ci + 1) * cw)
        conv = cw_ref[0:1, cols] * ext_ref[tap0:tap0 + tm, cols]
        for j in range(1, CONV_W):
            conv = conv + cw_ref[j:j + 1, cols] * ext_ref[tap0 + j:tap0 + j + tm, cols]
        act_ref[:, cols] = _silu(conv).astype(act_ref.dtype)
    ext_ref[0:halo, :] = ext_ref[tm:tm + halo, :]


def _proj_even(x, w, conv_w, seq, tm=512):
    t, k = x.shape
    tm = min(tm, seq)
    row = lambda i: (i, 0)
    fix = lambda i: (0, 0)
    return pl.pallas_call(
        functools.partial(_proj_even_kernel, tiles_per_seq=seq // tm),
        grid=(t // tm,),
        in_specs=[pl.BlockSpec((tm, k), row), pl.BlockSpec(w.shape, fix), pl.BlockSpec(conv_w.shape, fix)],
        out_specs=[pl.BlockSpec((tm, n), row) for n in EV_WIDTHS],
        out_shape=[jax.ShapeDtypeStruct((t, n), F32) for n in EV_WIDTHS],
        scratch_shapes=[pltpu.VMEM((tm + SUBLANES, EV_WIDTHS[1]), F32)],
        compiler_params=_cparams(("arbitrary",)),
        name="in_proj",
    )(x, w, conv_w)


OD_SEG = dict(mq=(0, 512), mk=(512, 1024), mv=(1024, 1536), mo=(1536, 2048), gates=(2048, 2176),
              sq=(2176, 2688), sk=(2688, 2944), sv=(2944, 3456))
OD_COLS = 3456


def _proj_odd_kernel(x_ref, w_ref, c_ref, s_ref, mq_ref, mk_ref, mv_ref, mo_ref, mg_ref, sq_ref, sk_ref, sv_ref):
    xb = x_ref[...].astype(BF16)

    def seg(name):
        a, b = OD_SEG[name]
        return _dot(xb, w_ref[:, a:b])

    mq_ref[...] = seg("mq").astype(mq_ref.dtype)
    mk_ref[...] = seg("mk").astype(mk_ref.dtype)
    mv_ref[...] = seg("mv").astype(mv_ref.dtype)
    mo_ref[...] = seg("mo").astype(mo_ref.dtype)
    mg_ref[...] = seg("gates")
    c = c_ref[...]
    s = s_ref[...]
    def swap_halves(t):
        half = SWA_D // 2
        first_half = (_iota2(t.shape, 1) % SWA_D) < half
        return jnp.where(first_half, pltpu.roll(t, t.shape[1] - half, 1), pltpu.roll(t, half, 1))

    c8 = jnp.concatenate([c] * (SWA_H // 2), axis=1)
    s8 = jnp.concatenate([s] * (SWA_H // 2), axis=1)
    q = seg("sq")
    sq_ref[...] = (q * c8 + swap_halves(q) * s8).astype(sq_ref.dtype)
    c2 = jnp.concatenate([c] * SWA_KV, axis=1)
    s2 = jnp.concatenate([s] * SWA_KV, axis=1)
    k = seg("sk")
    sk_ref[...] = (k * c2 + swap_halves(k) * s2).astype(sk_ref.dtype)
    sv_ref[...] = seg("sv").astype(sv_ref.dtype)


def _proj_odd(x, w, ctab, stab, tm=512):
    t, k = x.shape
    widths = (512, 512, 512, 512, 128, SWA_H * SWA_D, SWA_KV * LANES, 2 * SWA_KV * LANES)
    dtypes = (F32, F32, F32, F32, F32, BF16, BF16, BF16)
    return pl.pallas_call(
        _proj_odd_kernel,
        grid=(t // tm,),
        in_specs=[pl.BlockSpec((tm, k), lambda i: (i, 0)), pl.BlockSpec(w.shape, lambda i: (0, 0)),
                  pl.BlockSpec((tm, LANES), lambda i: (i, 0)), pl.BlockSpec((tm, LANES), lambda i: (i, 0))],
        out_specs=[pl.BlockSpec((tm, n), lambda i: (i, 0)) for n in widths],
        out_shape=[jax.ShapeDtypeStruct((t, n), dt) for n, dt in zip(widths, dtypes)],
        compiler_params=_cparams(("arbitrary",)),
        name="in_proj_odd",
    )(x, w, ctab, stab)


def _rms(x, g):
    return x * lax.rsqrt(jnp.mean(x * x, axis=-1, keepdims=True) + EPS) * g


def _mla_prep_kernel(in_ref, c_ref, s_ref, qn_ref, kvn_ref, wq_ref, wkv_ref, q_ref, k_ref, v_ref):
    hw = MLA_H * LANES
    c = c_ref[...]
    s = s_ref[...]
    c8 = jnp.concatenate([c] * MLA_H, axis=1)
    s8 = jnp.concatenate([s] * MLA_H, axis=1)
    def swap_halves(t):
        half = MLA_ROPE // 2
        first_half = (_iota2(t.shape, 1) % LANES) < MLA_NOPE + half
        return jnp.where(first_half, pltpu.roll(t, t.shape[1] - half, 1), pltpu.roll(t, half, 1))

    cqn = _rms(in_ref[:, 0:Q_LORA], qn_ref[...]).astype(BF16)
    qq = _dot(cqn, wq_ref[...])
    scale = (MLA_NOPE + MLA_ROPE) ** -0.5
    q_ref[...] = ((qq[:, :hw] * c8 + qq[:, hw:] * s8) * scale).astype(q_ref.dtype)
    ckvn = _rms(in_ref[:, Q_LORA:Q_LORA + KV_LORA], kvn_ref[...]).astype(BF16)
    kv = _dot(ckvn, wkv_ref[...])
    o = Q_LORA + KV_LORA
    kr = in_ref[:, o:o + LANES]
    krr = kr * c + swap_halves(kr) * s
    k_ref[...] = (kv[:, :hw] + jnp.concatenate([krr] * MLA_H, axis=1)).astype(k_ref.dtype)
    v_ref[...] = kv[:, hw:].astype(v_ref.dtype)


def _mla_prep(mla_in, ctab, stab, qn, kvn, wq2, wkv2, tm=1024):
    t = mla_in.shape[0]
    tm = min(tm, t)
    hw = MLA_H * LANES
    row = lambda i: (i, 0)
    fix = lambda i: (0, 0)
    return pl.pallas_call(
        _mla_prep_kernel,
        grid=(t // tm,),
        in_specs=[pl.BlockSpec((tm, mla_in.shape[1]), row), pl.BlockSpec((tm, LANES), row), pl.BlockSpec((tm, LANES), row),
                  pl.BlockSpec(qn.shape, fix), pl.BlockSpec(kvn.shape, fix),
                  pl.BlockSpec(wq2.shape, fix), pl.BlockSpec(wkv2.shape, fix)],
        out_specs=[pl.BlockSpec((tm, hw), row)] * 3,
        out_shape=[jax.ShapeDtypeStruct((t, hw), BF16)] * 3,
        compiler_params=_cparams(("arbitrary",)),
        name="mla_prep",
    )(mla_in, ctab, stab, qn, kvn, wq2, wkv2)


def _mla_attn_kernel(q_ref, k_ref, v_ref, o_ref, *, tq):
    i = pl.program_id(2)
    neg = -1e30
    lane = _iota2((tq, LANES), 1)
    ones_lane = (MLA_V, 0)

    def chunk(j, carry, masked):
        start = pl.multiple_of(j * tq, tq)
        out = []
        for hh in range(2):
            m, acc = carry[hh]
            q = q_ref[:, hh * LANES:(hh + 1) * LANES]
            kc = k_ref[pl.ds(start, tq), hh * LANES:(hh + 1) * LANES]
            vc = v_ref[pl.ds(start, tq), hh * LANES:(hh + 1) * LANES]
            vc = jnp.where(lane == ones_lane[hh], jnp.ones_like(vc), vc)
            s = _dot_nt(q, kc)
            if masked:
                s = jnp.where(_iota2(s.shape, 0) >= _iota2(s.shape, 1), s, neg)
            m_new = jnp.maximum(m, jnp.max(s, axis=-1, keepdims=True))
            alpha = jnp.exp(m - m_new)
            p = jnp.exp((s - m_new).astype(BF16))
            acc = alpha * acc + _dot(p, vc)
            out.append((m_new, acc))
        return tuple(out)

    one = (jnp.full((tq, 1), neg, F32), jnp.zeros((tq, LANES), F32))
    carry = lax.fori_loop(0, i, lambda j, c: chunk(j, c, False), (one, one))
    (_, acc0), (_, acc1) = chunk(i, carry, True)
    o0 = acc0 / _lane_bcast(acc0, ones_lane[0])
    o1 = acc1 / _lane_bcast(acc1, ones_lane[1])
    o_ref[...] = jnp.where(lane < MLA_V, o0, o1).astype(o_ref.dtype)


def _mla_attn(q, k, v, batch, seq, tq=512):
    tq = min(tq, seq)
    nq = seq // tq
    pairs = MLA_H // 2
    return pl.pallas_call(
        functools.partial(_mla_attn_kernel, tq=tq),
        grid=(batch, pairs, nq),
        in_specs=[pl.BlockSpec((tq, 2 * LANES), lambda b, p, i: (b * nq + i, p)),
                  pl.BlockSpec((seq, 2 * LANES), lambda b, p, i: (b, p)),
                  pl.BlockSpec((seq, 2 * LANES), lambda b, p, i: (b, p))],
        out_specs=pl.BlockSpec((tq, LANES), lambda b, p, i: (b * nq + i, p)),
        out_shape=jax.ShapeDtypeStruct((batch * seq, pairs * LANES), BF16),
        compiler_params=_cparams(("arbitrary", "arbitrary", "arbitrary")),
        name="mla_attn",
    )(q, k, v)


def _unit_lower_inverse_many(ns):
    c = ns[0].shape[0]
    eye = (_iota2((c, c), 0) == _iota2((c, c), 1)).astype(F32)
    xs = [-n for n in ns]
    ps = [eye + x for x in xs]
    xb = [x.astype(BF16) for x in xs]
    for _ in range(int(math.log2(c)) - 1):
        xs = [_dot(b, b) for b in xb]
        xb = [x.astype(BF16) for x in xs]
        ps = [p + _dot(p.astype(BF16), b) for p, b in zip(ps, xb)]
    return ps


def _gdn_kernel(act_ref, g_ref, z_ref, al_ref, dt_ref, on_ref, o_ref, st_ref):
    c = GDN_CHUNK
    hd = GDN_DK
    nqk = GDN_H * GDN_DK

    @pl.when(pl.program_id(1) == 0)
    def _():
        st_ref[...] = jnp.zeros(st_ref.shape, F32)

    tri = (_iota2((c, c), 0) >= _iota2((c, c), 1)).astype(F32)
    row_ge = _iota2((c, c), 0) >= _iota2((c, c), 1)
    row_gt = _iota2((c, c), 0) > _iota2((c, c), 1)
    lane = _iota2((c, LANES), 1)

    seqs = []
    for bb in range(act_ref.shape[0]):
        gates = g_ref[bb]
        g_all = -jnp.exp(al_ref[...]) * _softplus(gates + dt_ref[...])
        gc_all = _dot_sel(tri, g_all)
        seqs.append(dict(beta_all=_sigmoid(gates), gc_all=gc_all, gc_parts=_split3(gc_all)))
    units = []
    for bb, sq in enumerate(seqs):
        for h in range(GDN_H):
            q = act_ref[bb, :, h * hd:(h + 1) * hd].astype(F32)
            k = act_ref[bb, :, nqk + h * hd:nqk + (h + 1) * hd].astype(F32)
            v = act_ref[bb, :, 2 * nqk + h * GDN_DV:2 * nqk + (h + 1) * GDN_DV].astype(F32)
            q = q * lax.rsqrt(jnp.sum(q * q, axis=-1, keepdims=True) + EPS) * (GDN_DK ** -0.5)
            k = k * lax.rsqrt(jnp.sum(k * k, axis=-1, keepdims=True) + EPS)
            beta = _lane_bcast(sq["beta_all"], h)
            gcol = _lane_bcast(sq["gc_all"], GDN_H + h)
            units.append(dict(bb=bb, h=h, q=q, k=k, v=v, beta=beta, gcol=gcol, kb=k * beta, parts=sq["gc_parts"]))
    for u in units:
        pick = (lane == GDN_H + u["h"]).astype(BF16)
        p0, p1, p2 = u["parts"]
        u["grow"] = _dot_nt(pick, p0) + (_dot_nt(pick, p1) + _dot_nt(pick, p2))
        u["kk"] = _dot3(u["kb"], u["k"], _dot_nt)
        u["qk"] = _dot_nt(u["q"].astype(BF16), u["k"].astype(BF16))
    for u in units:
        gcol = u["gcol"]
        decay = jnp.exp(jnp.where(row_ge, gcol[:, :c] - u["grow"], -jnp.inf))
        eg = jnp.exp(gcol)
        glast = gcol[c - 1:c, :]
        u["lower"] = jnp.where(row_gt, u["kk"] * decay, 0.0)
        u["rhs"] = jnp.concatenate([u["v"] * u["beta"], u["kb"] * eg], axis=1)
        u["attn"] = u["qk"] * decay
        u["qg"] = (u["q"] * eg).astype(BF16)
        u["kg"] = (u["k"] * jnp.exp(glast - gcol)).astype(BF16)
        u["gl"] = jnp.exp(glast)

    tinvs = _unit_lower_inverse_many([u["lower"] for u in units])
    uws = []
    for u, tinv in zip(units, tinvs):
        uws.append(_dot(tinv.astype(BF16), u["rhs"].astype(BF16)))
    states = [st_ref[u["bb"], u["h"]] for u in units]
    sbs = [s.astype(BF16) for s in states]
    vnews = [(uw[:, :GDN_DV] - _dot(uw[:, GDN_DV:].astype(BF16), sb)).astype(BF16) for uw, sb in zip(uws, sbs)]
    for u, state, sb, vnb in zip(units, states, sbs, vnews):
        bb, h = u["bb"], u["h"]
        o = _dot(u["qg"], sb) + _dot(u["attn"].astype(BF16), vnb)
        st_ref[bb, h] = state * u["gl"] + _dot_tn(u["kg"], vnb)
        o = _rms(o, on_ref[...]) * _silu(z_ref[bb, :, h * GDN_DV:(h + 1) * GDN_DV].astype(F32))
        o_ref[bb, :, h * GDN_DV:(h + 1) * GDN_DV] = o.astype(o_ref.dtype)


def _gdn(act, gates, z, a_row, dt_row, o_norm, batch, seq):
    c = GDN_CHUNK
    nc = seq // c
    w3 = act.shape[1]
    wo = GDN_H * GDN_DV
    nb = min(GDN_SEQS_PER_STEP, batch)
    row = lambda b, i: (b, i, 0)
    fix = lambda b, i: (0, 0)
    out = pl.pallas_call(
        _gdn_kernel,
        grid=(batch // nb, nc),
        in_specs=[pl.BlockSpec((nb, c, w3), row), pl.BlockSpec((nb, c, LANES), row), pl.BlockSpec((nb, c, wo), row),
                  pl.BlockSpec((1, LANES), fix), pl.BlockSpec((1, LANES), fix), pl.BlockSpec((1, GDN_DV), fix)],
        out_specs=pl.BlockSpec((nb, c, wo), row),
        out_shape=jax.ShapeDtypeStruct((batch, seq, wo), BF16),
        scratch_shapes=[pltpu.VMEM((nb, GDN_H, GDN_DK, GDN_DV), F32)],
        compiler_params=_cparams(("arbitrary", "arbitrary")),
        name="gdn",
    )(act.reshape(batch, seq, w3), gates.reshape(batch, seq, LANES), z.reshape(batch, seq, wo), a_row, dt_row, o_norm)
    return out.reshape(batch * seq, wo)


def _mlstm_kernel(q_ref, k_ref, v_ref, og_ref, g_ref, bias_ref, nrm_ref, o_ref, c_ref, n_ref, m_ref):
    @pl.when(pl.program_id(1) == 0)
    def _():
        c_ref[...] = jnp.zeros(c_ref.shape, F32)
        n_ref[...] = jnp.zeros(n_ref.shape, F32)
        m_ref[...] = jnp.zeros(m_ref.shape, F32)

    c = ML_CHUNK
    tri = (_iota2((c, c), 0) >= _iota2((c, c), 1)).astype(F32)
    row_ge = _iota2((c, c), 0) >= _iota2((c, c), 1)
    ones = jnp.ones((c, LANES), F32)
    lane = _iota2((c, LANES), 1)

    units = []
    for bb in range(q_ref.shape[0]):
        pre = g_ref[bb] + bias_ref[...]
        logf = jnp.minimum(pre, 0.0) - jnp.log(1.0 + jnp.exp(-jnp.abs(pre)))
        bcum_all = _dot_sel(tri, logf)
        for h in range(ML_H):
            q = q_ref[bb, :, h * LANES:(h + 1) * LANES].astype(F32)
            k = k_ref[bb, :, h * LANES:(h + 1) * LANES].astype(F32) * (ML_DK ** -0.5)
            units.append(dict(bb=bb, h=h, q=q, k=k, qb=q.astype(BF16), vb=v_ref[bb, :, h * ML_DV:(h + 1) * ML_DV].astype(BF16),
                              bcol=_lane_bcast(bcum_all, ML_H + h),
                              icol=_lane_bcast(pre, h),
                              col=jnp.where(lane == h, pre, 0.0) - jnp.where(lane == ML_H + h, bcum_all, 0.0),
                              m_st=m_ref[bb, h], cst=c_ref[bb, h], nst=n_ref[bb, h]))
    for u in units:
        u["row"] = _dot_sel(ones, u["col"], _dot_nt)
        u["qk"] = _dot_nt(u["qb"], u["k"].astype(BF16))
        u["qc"] = _dot(u["qb"], u["cst"].astype(BF16))
    for u in units:
        u["d"] = jnp.where(row_ge, u["bcol"][:, :c] + u["row"], -jnp.inf)
        u["inter"] = u["bcol"] + u["m_st"]
        u["m_t"] = jnp.maximum(u["inter"], jnp.max(u["d"], axis=-1, keepdims=True))
        u["b_end"] = u["bcol"][c - 1:c, :]
        u["a"] = u["b_end"] - u["bcol"] + u["icol"]
        u["m_new"] = jnp.maximum(u["b_end"] + u["m_st"], jnp.max(u["a"], axis=0, keepdims=True))
    for u in units:
        u["w_inter"] = jnp.exp(u["inter"] - u["m_t"])
        u["p"] = jnp.exp(u["d"] - u["m_t"][:, :c]) * u["qk"]
        u["keep"] = jnp.exp(u["b_end"] + u["m_st"] - u["m_new"])
        u["ks"] = u["k"] * jnp.exp(u["a"] - u["m_new"])
    for u in units:
        u["pv"] = _dot(u["p"].astype(BF16), u["vb"])
        u["kv"] = _dot_tn(u["ks"].astype(BF16), u["vb"])
    for u in units:
        u["den"] = (u["w_inter"] * jnp.sum(u["q"] * u["nst"], axis=-1, keepdims=True)
                    + jnp.sum(u["p"], axis=-1, keepdims=True))
    for u in units:
        bb, h = u["bb"], u["h"]
        num = u["w_inter"] * u["qc"] + u["pv"]
        hc = num / jnp.maximum(jnp.abs(u["den"]), jnp.exp(-u["m_t"]))
        c_ref[bb, h] = u["cst"] * u["keep"] + u["kv"]
        n_ref[bb, h] = u["nst"] * u["keep"] + jnp.sum(u["ks"], axis=0, keepdims=True)
        m_ref[bb, h] = u["m_new"]
        hn = (_rms(hc, nrm_ref[:, h * ML_DV:(h + 1) * ML_DV])
              * _sigmoid(og_ref[bb, :, h * ML_DV:(h + 1) * ML_DV].astype(F32)))
        o_ref[bb, :, h * ML_DV:(h + 1) * ML_DV] = hn.astype(o_ref.dtype)


def _mlstm(mq, mk, mv, mo, gates, bias_row, norm_row, batch, seq):
    c = ML_CHUNK
    nc = seq // c
    nb = min(MLSTM_SEQS_PER_STEP, batch)
    row = lambda b, i: (b, i, 0)
    fix = lambda b, i: (0, 0)
    wide = ML_H * LANES
    r3 = lambda a: a.reshape(batch, seq, a.shape[-1])
    out = pl.pallas_call(
        _mlstm_kernel,
        grid=(batch // nb, nc),
        in_specs=[pl.BlockSpec((nb, c, wide), row), pl.BlockSpec((nb, c, wide), row), pl.BlockSpec((nb, c, wide), row),
                  pl.BlockSpec((nb, c, wide), row), pl.BlockSpec((nb, c, LANES), row),
                  pl.BlockSpec((1, LANES), fix), pl.BlockSpec((1, wide), fix)],
        out_specs=pl.BlockSpec((nb, c, wide), row),
        out_shape=jax.ShapeDtypeStruct((batch, seq, wide), BF16),
        scratch_shapes=[pltpu.VMEM((nb, ML_H, LANES, ML_DV), F32), pltpu.VMEM((nb, ML_H, 1, LANES), F32),
                        pltpu.VMEM((nb, ML_H, 1, LANES), F32)],
        compiler_params=_cparams(("arbitrary", "arbitrary")),
        name="mlstm",
    )(r3(mq), r3(mk), r3(mv), r3(mo), r3(gates), bias_row, norm_row)
    return out.reshape(batch * seq, wide)


def _swa_kernel(q_ref, kc_ref, kp_ref, vc_ref, vp_ref, sink_ref, o_ref):
    w = WINDOW
    n = pl.program_id(1)
    scale = SWA_D ** -0.5
    qi = _iota2((w, w), 0)
    kj = _iota2((w, w), 1)
    mask_c = kj <= qi
    mask_p = jnp.logical_and(kj > qi, n > 0)
    grp = SWA_H // SWA_KV
    neg = -1e30
    units = [(bb, h) for bb in range(q_ref.shape[0]) for h in range(SWA_H)]
    scores = []
    half_of_lane = _iota2((w, LANES), 1) // SWA_D
    for bb, h in units:
        g = h // grp
        pair = q_ref[bb, :, (h // 2) * LANES:(h // 2 + 1) * LANES]
        q = jnp.where(half_of_lane == h % 2, pair, jnp.zeros_like(pair))
        scores.append((_dot_nt(q, kc_ref[bb, :, g * LANES:(g + 1) * LANES]),
                       _dot_nt(q, kp_ref[bb, :, g * LANES:(g + 1) * LANES])))
    masked, tops, exps, dens, probs = [], [], [], [], {}
    for sc, sp in scores:
        masked.append((jnp.where(mask_c, sc * scale, neg), jnp.where(mask_p, sp * scale, neg)))
    for (bb, h), (s_c, s_p) in zip(units, masked):
        tops.append(jnp.maximum(jnp.max(jnp.maximum(s_c, s_p), axis=-1, keepdims=True), sink_ref[:, h:h + 1]))
    for (s_c, s_p), m in zip(masked, tops):
        exps.append((jnp.where(mask_c, jnp.exp(s_c - m), 0.0), jnp.where(mask_p, jnp.exp(s_p - m), 0.0)))
    ones_b = jnp.ones((w, LANES), BF16)
    for (bb, h), (p_c, p_p), m in zip(units, exps, tops):
        p_c, p_p = p_c.astype(BF16), p_p.astype(BF16)
        probs[bb, h] = (p_c, p_p)
        dens.append(_dot(p_c, ones_b) + _dot(p_p, ones_b) + jnp.exp(sink_ref[:, h:h + 1] - m))
    inv = {u: 1.0 / den for u, den in zip(units, dens)}
    for bb in range(q_ref.shape[0]):
        for pair in range(SWA_H // 2):
            acc = None
            for sub in range(2):
                h = 2 * pair + sub
                vcol = (2 * (h // grp) + sub) * LANES
                p_c, p_p = probs[bb, h]
                part = (_dot(p_c, vc_ref[bb, :, vcol:vcol + LANES]) + _dot(p_p, vp_ref[bb, :, vcol:vcol + LANES])) * inv[bb, h]
                acc = part if acc is None else acc + part
            o_ref[bb, :, pair * LANES:(pair + 1) * LANES] = acc.astype(o_ref.dtype)


def _swa(sq, sk, sv, sinks_row, batch, seq):
    w = WINDOW
    nb = seq // w
    ns = min(SWA_SEQS_PER_STEP, batch)
    wo = SWA_H * SWA_D
    cur = lambda b, n: (b, n, 0)
    prev = lambda b, n: (b, jnp.maximum(n - 1, 0), 0)
    r3 = lambda a: a.reshape(batch, seq, a.shape[-1])
    q3, k3, v3 = r3(sq), r3(sk), r3(sv)
    out = pl.pallas_call(
        _swa_kernel,
        grid=(batch // ns, nb),
        in_specs=[pl.BlockSpec((ns, w, sq.shape[1]), cur),
                  pl.BlockSpec((ns, w, sk.shape[1]), cur), pl.BlockSpec((ns, w, sk.shape[1]), prev),
                  pl.BlockSpec((ns, w, sv.shape[1]), cur), pl.BlockSpec((ns, w, sv.shape[1]), prev),
                  pl.BlockSpec((1, LANES), lambda b, n: (0, 0))],
        out_specs=pl.BlockSpec((ns, w, wo), cur),
        out_shape=jax.ShapeDtypeStruct((batch, seq, wo), BF16),
        compiler_params=_cparams(("arbitrary", "arbitrary")),
        name="swa",
    )(q3, k3, k3, v3, v3, sinks_row)
    return out.reshape(batch * seq, wo)


def _layer_norm(h, g, b):
    mu = jnp.mean(h, axis=-1, keepdims=True)
    d = h - mu
    var = jnp.mean(d * d, axis=-1, keepdims=True)
    return d * lax.rsqrt(var + LN_EPS) * g + b


def _outproj_kernel(x_ref, a1_ref, a2_ref, w_ref, g_ref, b_ref, o_ref, op_ref):
    k1 = a1_ref.shape[1]
    y = _dot(a1_ref[...].astype(BF16), w_ref[0:k1, :]) + _dot(a2_ref[...].astype(BF16), w_ref[k1:, :])
    h = _layer_norm(DN_ALPHA * x_ref[...] + y, g_ref[...], b_ref[...])
    o_ref[...] = h
    op_ref[...] = _pack_pairs(h)


def _outproj_ln(x, a1, a2, w, g, b, tm=1024):
    t, d = x.shape
    tm = min(tm, t)
    row = lambda i: (i, 0)
    fix = lambda i: (0, 0)
    return pl.pallas_call(
        _outproj_kernel,
        grid=(t // tm,),
        in_specs=[pl.BlockSpec((tm, d), row), pl.BlockSpec((tm, a1.shape[1]), row), pl.BlockSpec((tm, a2.shape[1]), row),
                  pl.BlockSpec(w.shape, fix), pl.BlockSpec((1, d), fix), pl.BlockSpec((1, d), fix)],
        out_specs=[pl.BlockSpec((tm, d), row), pl.BlockSpec((tm, d // 2), row)],
        out_shape=[jax.ShapeDtypeStruct((t, d), F32), jax.ShapeDtypeStruct((t, d // 2), jnp.uint32)],
        compiler_params=_cparams(("arbitrary",)),
        name="outproj_ln",
    )(x, a1, a2, w, g, b)


def _first_index(x, m, iota_f, sentinel):
    return jnp.min(jnp.where(x == m, iota_f, sentinel), axis=0, keepdims=True)


def _router_kernel(x_ref, wt_ref, bias_ref, idx_ref, gate_ref, rank_ref, cnt_ref, carry_ref):
    tm = x_ref.shape[0]
    e = N_EXPERTS
    gs = e // N_GROUPS
    ninf = -jnp.inf

    @pl.when(pl.program_id(0) == 0)
    def _():
        carry_ref[...] = jnp.zeros(carry_ref.shape, F32)

    logits = _dot3(wt_ref[...], x_ref[...], _dot_nt)
    scores = _sigmoid(logits)
    sel = scores + bias_ref[:, 0:1]

    sub_f = _iota2((gs, tm), 0).astype(F32)
    gscore = []
    for g in range(N_GROUPS):
        blk = sel[g * gs:(g + 1) * gs, :]
        m1 = jnp.max(blk, axis=0, keepdims=True)
        i1 = _first_index(blk, m1, sub_f, float(gs))
        m2 = jnp.max(jnp.where(sub_f == i1, ninf, blk), axis=0, keepdims=True)
        gscore.append(m1 + m2)
    gsc = jnp.concatenate(gscore, axis=0)
    grp_f = _iota2((N_GROUPS, tm), 0).astype(F32)
    gmask = jnp.zeros((N_GROUPS, tm), F32)
    for _ in range(TOPK_GROUPS):
        m = jnp.max(gsc, axis=0, keepdims=True)
        gi = _first_index(gsc, m, grp_f, float(N_GROUPS))
        hit = grp_f == gi
        gmask = jnp.where(hit, 1.0, gmask)
        gsc = jnp.where(hit, ninf, gsc)
    masked = jnp.concatenate(
        [jnp.where(gmask[g:g + 1, :] > 0.0, sel[g * gs:(g + 1) * gs, :], ninf) for g in range(N_GROUPS)], axis=0)

    exp_f = _iota2((e, tm), 0).astype(F32)
    chosen = jnp.zeros((e, tm), F32)
    idxs, gates = [], []
    for _ in range(TOP_K):
        m = jnp.max(masked, axis=0, keepdims=True)
        ei = _first_index(masked, m, exp_f, float(e))
        hit = exp_f == ei
        idxs.append(ei)
        gates.append(jnp.sum(jnp.where(hit, scores, 0.0), axis=0, keepdims=True))
        chosen = jnp.where(hit, 1.0, chosen)
        masked = jnp.where(hit, ninf, masked)
    gate = jnp.concatenate(gates, axis=0)
    gate = gate / jnp.sum(gate, axis=0, keepdims=True) * ROUTED_SCALE
    idx_f = jnp.concatenate(idxs, axis=0)

    upper = (_iota2((tm, tm), 0) < _iota2((tm, tm), 1)).astype(BF16)
    before = _dot(chosen.astype(BF16), upper) + carry_ref[...][:, 0:1]
    ranks = [jnp.sum(jnp.where(exp_f == idxs[k], before, 0.0), axis=0, keepdims=True) for k in range(TOP_K)]
    carry_ref[...] = carry_ref[...] + jnp.sum(chosen, axis=1, keepdims=True)

    idx_ref[...] = idx_f.astype(jnp.int32)
    gate_ref[...] = gate
    rank_ref[...] = jnp.concatenate(ranks, axis=0).astype(jnp.int32)
    cnt_ref[...] = carry_ref[...]


def _router(x, wt, bias_col, tm=512):
    t, d = x.shape
    col = lambda i: (0, i)
    fix = lambda i: (0, 0)
    return pl.pallas_call(
        _router_kernel,
        grid=(t // tm,),
        in_specs=[pl.BlockSpec((tm, d), lambda i: (i, 0)), pl.BlockSpec(wt.shape, fix), pl.BlockSpec((N_EXPERTS, LANES), fix)],
        out_specs=[pl.BlockSpec((TOP_K, tm), col), pl.BlockSpec((TOP_K, tm), col), pl.BlockSpec((TOP_K, tm), col),
                   pl.BlockSpec((N_EXPERTS, LANES), fix)],
        out_shape=[jax.ShapeDtypeStruct((TOP_K, t), jnp.int32), jax.ShapeDtypeStruct((TOP_K, t), F32),
                   jax.ShapeDtypeStruct((TOP_K, t), jnp.int32), jax.ShapeDtypeStruct((N_EXPERTS, LANES), F32)],
        scratch_shapes=[pltpu.VMEM((N_EXPERTS, LANES), F32)],
        compiler_params=_cparams(("arbitrary",)),
        name="router",
    )(x, wt, bias_col)


def _dest_kernel(idx_ref, rank_ref, start_ref, dest_ref):
    tm = idx_ref.shape[1]
    exp_i = _iota2((N_EXPERTS, tm), 0)
    start = start_ref[:, 0:1]
    rows = [jnp.sum(jnp.where(exp_i == idx_ref[s:s + 1, :], start, 0.0), axis=0, keepdims=True) for s in range(TOP_K)]
    dest_ref[...] = jnp.concatenate(rows, axis=0).astype(jnp.int32) + rank_ref[...]


def _dest_rows(idx, rank, start_col, tm=2048):
    t = idx.shape[1]
    tm = min(tm, t)
    col = lambda i: (0, i)
    return pl.pallas_call(
        _dest_kernel,
        grid=(t // tm,),
        in_specs=[pl.BlockSpec((TOP_K, tm), col), pl.BlockSpec((TOP_K, tm), col),
                  pl.BlockSpec((N_EXPERTS, LANES), lambda i: (0, 0))],
        out_specs=pl.BlockSpec((TOP_K, tm), col),
        out_shape=jax.ShapeDtypeStruct((TOP_K, t), jnp.int32),
        compiler_params=_cparams(("arbitrary",)),
        name="moe_dest",
    )(idx, rank, start_col)


def _pack_pairs(x):
    n = x.shape[1] // 2
    hi = lax.bitcast_convert_type(x[:, :n].astype(BF16).astype(F32), jnp.uint32)
    lo = lax.bitcast_convert_type(x[:, n:].astype(BF16).astype(F32), jnp.uint32)
    return hi | (lo >> 16)


def _unpack_pairs(w):
    hi = lax.bitcast_convert_type(w & jnp.uint32(0xFFFF0000), F32)
    lo = lax.bitcast_convert_type(w << 16, F32)
    return hi, lo


def _sc_scatter_rows(xp, dest, rows, chunk=LANES):
    t, width = xp.shape
    info = plsc.get_sparse_core_info()
    ncores, nsub = info.num_cores, info.num_subcores
    per_worker = t // (ncores * nsub)
    nchunk = per_worker // chunk
    mesh = plsc.VectorSubcoreMesh(core_axis_name="c", subcore_axis_name="s")

    @functools.partial(
        pl.kernel, mesh=mesh,
        out_type=jax.ShapeDtypeStruct((rows, width), xp.dtype),
        scratch_types=[pltpu.VMEM((TOP_K, chunk), jnp.int32), pltpu.VMEM((chunk, width), xp.dtype), pltpu.SemaphoreType.DMA],
    )
    def scatter(xp_hbm, dest_hbm, out_hbm, idx_v, rows_v, sem):
        base = (lax.axis_index("s") * ncores + lax.axis_index("c")) * per_worker

        @pl.loop(0, nchunk)
        def _(i):
            off = pl.multiple_of(base + i * chunk, chunk)
            pltpu.sync_copy(dest_hbm.at[:, pl.ds(off, chunk)], idx_v)
            pltpu.sync_copy(xp_hbm.at[pl.ds(off, chunk)], rows_v)
            copies = [pltpu.async_copy(rows_v, out_hbm.at[idx_v.at[s]], sem) for s in range(TOP_K)]
            for cp in copies:
                cp.wait()

    return scatter(xp, dest)


def _experts_kernel(be_ref, nu_ref, nv_ref, first_ref, slot_ref, nxt_ref, xs_ref, wg_hbm, wu_hbm, wd_hbm, ys_ref,
                    wgf_ref, wuf_ref, wdf_ref, wgb_ref, wub_ref, wdb_ref, sem, *, layer):
    i = pl.program_id(0)

    def fetch(e, s):
        return [pltpu.make_async_copy(wg_hbm.at[layer, e], wgf_ref.at[s], sem.at[s]),
                pltpu.make_async_copy(wu_hbm.at[layer, e], wuf_ref.at[s], sem.at[s]),
                pltpu.make_async_copy(wd_hbm.at[layer, e], wdf_ref.at[s], sem.at[s])]

    @pl.when(i == 0)
    def _():
        for cp in fetch(be_ref[0], 0):
            cp.start()

    @pl.when(jnp.logical_and(first_ref[i] == 1, i < nu_ref[0]))
    def _():
        s = slot_ref[i]
        for cp in fetch(be_ref[i], s):
            cp.wait()
        wgb_ref[...] = wgf_ref[s].astype(BF16)
        wub_ref[...] = wuf_ref[s].astype(BF16)
        wdb_ref[...] = wdf_ref[s].astype(BF16)

        @pl.when(nxt_ref[i] >= 0)
        def _():
            for cp in fetch(nxt_ref[i], 1 - s):
                cp.start()

    @pl.when(i < nu_ref[0])
    def _():
        sub = xs_ref.shape[0] // EXPERT_SUBBLOCKS
        acts = []
        for r in range(EXPERT_SUBBLOCKS):
            rows = pl.ds(r * sub, sub)
            live = (_iota2((sub, 1), 0) + r * sub) < nv_ref[i]
            xa, xb = _unpack_pairs(jnp.where(live, xs_ref[rows, :], jnp.uint32(0)))
            x = jnp.concatenate([xa.astype(BF16), xb.astype(BF16)], axis=1)
            acts.append((_dot(x, wgb_ref[...]), _dot(x, wub_ref[...])))
        outs = [_dot((_silu(gate) * up).astype(BF16), wdb_ref[...]) for gate, up in acts]
        for r, y in enumerate(outs):
            ys_ref[pl.ds(r * sub, sub), :] = _pack_pairs(y)


def _experts(block_e, n_used, n_valid, xs, wg, wu, wd, layer, block):
    rows, half = xs.shape
    d = 2 * half
    nb = rows // block
    pos = jnp.arange(nb, dtype=jnp.int32)
    first = jnp.concatenate([jnp.ones((1,), jnp.int32), (block_e[1:] != block_e[:-1]).astype(jnp.int32)])
    slot = (jnp.cumsum(first) - 1) % 2
    later = (pos[None, :] > pos[:, None]) & (block_e[None, :] != block_e[:, None]) & (pos[None, :] < n_used[0])
    nxt_pos = jnp.min(jnp.where(later, pos[None, :], nb), axis=1)
    nxt = jnp.where(nxt_pos < nb, block_e[jnp.minimum(nxt_pos, nb - 1)], -1)
    blk = lambda i, be, nu, *rest: (jnp.minimum(i, nu[0] - 1), 0)
    hbm = pl.BlockSpec(memory_space=pl.ANY)
    return pl.pallas_call(
        functools.partial(_experts_kernel, layer=layer),
        grid_spec=pltpu.PrefetchScalarGridSpec(
            num_scalar_prefetch=6,
            grid=(nb,),
            in_specs=[pl.BlockSpec((block, half), blk), hbm, hbm, hbm],
            out_specs=pl.BlockSpec((block, half), blk),
            scratch_shapes=[pltpu.VMEM((2, d, D_EXPERT), F32), pltpu.VMEM((2, d, D_EXPERT), F32),
                            pltpu.VMEM((2, D_EXPERT, d), F32),
                            pltpu.VMEM((d, D_EXPERT), BF16), pltpu.VMEM((d, D_EXPERT), BF16),
                            pltpu.VMEM((D_EXPERT, d), BF16), pltpu.SemaphoreType.DMA((2,))],
        ),
        out_shape=jax.ShapeDtypeStruct((rows, half), jnp.uint32),
        compiler_params=_cparams(("arbitrary",)),
        name="moe_experts",
    )(block_e, n_used, n_valid, first, slot.astype(jnp.int32), nxt.astype(jnp.int32), xs, wg, wu, wd)


def _sc_gather_rows(table, idx, chunk=SC_CHUNK):
    n = idx.shape[0]
    width = table.shape[1]
    info = plsc.get_sparse_core_info()
    ncores, nsub = info.num_cores, info.num_subcores
    per_worker = n // (ncores * nsub)
    nchunk = per_worker // chunk
    mesh = plsc.VectorSubcoreMesh(core_axis_name="c", subcore_axis_name="s")

    @functools.partial(
        pl.kernel, mesh=mesh,
        out_type=jax.ShapeDtypeStruct((n, width), table.dtype),
        scratch_types=[pltpu.VMEM((nchunk, chunk), jnp.int32), pltpu.VMEM((2, chunk, width), table.dtype),
                       pltpu.SemaphoreType.DMA((2,)), pltpu.SemaphoreType.DMA((2,))],
    )
    def gather(table_hbm, idx_hbm, out_hbm, idx_v, rows_v, gsem, wsem):
        wid = lax.axis_index("s") * ncores + lax.axis_index("c")
        base = wid * per_worker
        pltpu.sync_copy(idx_hbm.at[pl.ds(wid * nchunk, nchunk)], idx_v)

        def fetch(j, b):
            return pltpu.make_async_copy(table_hbm.at[idx_v.at[j]], rows_v.at[b], gsem.at[b])

        def flush(j, b):
            off = pl.multiple_of(base + j * chunk, chunk)
            return pltpu.make_async_copy(rows_v.at[b], out_hbm.at[pl.ds(off, chunk)], wsem.at[b])

        fetch(0, 0).start()

        @pl.loop(0, nchunk, step=2)
        def _(i):
            for b in range(2):
                j = i + b
                fetch(j, b).wait()

                @pl.when(j + 1 < nchunk)
                def _():
                    @pl.when(j >= 1)
                    def _():
                        flush(j - 1, 1 - b).wait()

                    fetch(j + 1, 1 - b).start()

                flush(j, b).start()

        flush(nchunk - 2, 0).wait()
        flush(nchunk - 1, 1).wait()

    return gather(table, idx.reshape(n // chunk, chunk))


def _shared_kernel(xp_ref, sg_ref, su_ref, sd_ref, o_ref):
    xa, xb = _unpack_pairs(xp_ref[...])
    x = jnp.concatenate([xa.astype(BF16), xb.astype(BF16)], axis=1)
    hs = _silu(_dot(x, sg_ref[...])) * _dot(x, su_ref[...])
    o_ref[...] = _pack_pairs(_dot(hs.astype(BF16), sd_ref[...]))


def _shared_expert(xp, sg, su, sd, tm=512):
    t, half = xp.shape
    row = lambda i: (i, 0)
    fix = lambda i: (0, 0)
    return pl.pallas_call(
        _shared_kernel,
        grid=(t // tm,),
        in_specs=[pl.BlockSpec((tm, half), row), pl.BlockSpec(sg.shape, fix), pl.BlockSpec(su.shape, fix),
                  pl.BlockSpec(sd.shape, fix)],
        out_specs=pl.BlockSpec((tm, half), row),
        out_shape=jax.ShapeDtypeStruct((t, half), jnp.uint32),
        compiler_params=_cparams(("arbitrary",)),
        name="moe_shared",
    )(xp, sg, su, sd)


def _combine_kernel(x_ref, gate_ref, rows_ref, sh_ref, g_ref, b_ref, o_ref):
    gate = gate_ref[...]
    ya, yb = _unpack_pairs(sh_ref[...])
    for s in range(TOP_K):
        a, b = _unpack_pairs(rows_ref[s])
        ya = ya + gate[:, s:s + 1] * a
        yb = yb + gate[:, s:s + 1] * b
    ff = jnp.concatenate([ya, yb], axis=1)
    o_ref[...] = _layer_norm(DN_ALPHA * x_ref[...] + ff, g_ref[...], b_ref[...])


def _combine(x, gate_t, rows, shared, g, b, tm=512):
    t, d = x.shape
    row = lambda i: (i, 0)
    fix = lambda i: (0, 0)
    return pl.pallas_call(
        _combine_kernel,
        grid=(t // tm,),
        in_specs=[pl.BlockSpec((tm, d), row), pl.BlockSpec((tm, TOP_K), row),
                  pl.BlockSpec((TOP_K, tm, d // 2), lambda i: (0, i, 0)), pl.BlockSpec((tm, d // 2), row),
                  pl.BlockSpec((1, d), fix), pl.BlockSpec((1, d), fix)],
        out_specs=pl.BlockSpec((tm, d), row),
        out_shape=jax.ShapeDtypeStruct((t, d), F32),
        compiler_params=_cparams(("arbitrary",)),
        name="moe_combine",
    )(x, gate_t, rows, shared, g, b)


def _take_cols(w, idx):
    idx = np.asarray(idx)
    runs, start = [], 0
    for pos in range(1, len(idx) + 1):
        run_ends = pos == len(idx) or (idx[pos] != idx[pos - 1] + 1 if idx[pos - 1] >= 0 else idx[pos] >= 0)
        if run_ends:
            runs.append((start, int(idx[start]), pos - start))
            start = pos

    def body(w_ref, o_ref):
        for dst, src, width in runs:
            if src < 0:
                o_ref[:, dst:dst + width] = jnp.zeros((o_ref.shape[0], width), o_ref.dtype)
            else:
                o_ref[:, dst:dst + width] = w_ref[:, src:src + width].astype(o_ref.dtype)

    rows = w.shape[0]
    tr = min(rows, 256)
    return pl.pallas_call(
        body,
        grid=(rows // tr,),
        in_specs=[pl.BlockSpec((tr, w.shape[1]), lambda i: (i, 0))],
        out_specs=pl.BlockSpec((tr, len(idx)), lambda i: (i, 0)),
        out_shape=jax.ShapeDtypeStruct((rows, len(idx)), BF16),
        compiler_params=_cparams(("arbitrary",)),
        name="weight_cols",
    )(w)


def _pad_lane_row(v, first_lane, width=LANES):
    out = jnp.zeros((1, width), F32)
    return lax.dynamic_update_slice(out, v.reshape(1, -1).astype(F32), (0, first_lane))


def _even_in_cols():
    z = lambda n: -np.ones(n, int)
    kr0 = Q_LORA + KV_LORA
    cols = [np.arange(0, Q_LORA), np.arange(Q_LORA, Q_LORA + KV_LORA),
            z(MLA_NOPE), np.arange(kr0, kr0 + MLA_ROPE), z(LANES - MLA_NOPE - MLA_ROPE)]
    g0 = kr0 + MLA_ROPE
    nqk = GDN_H * GDN_DK
    cols.append(np.arange(g0, g0 + 3 * nqk))
    zoff = g0 + 3 * nqk + 2 * GDN_H
    cols.append(np.arange(zoff, zoff + GDN_H * GDN_DV))
    cols += [np.arange(g0 + 3 * nqk, g0 + 3 * nqk + 2 * GDN_H), z(LANES - 2 * GDN_H)]
    return np.concatenate(cols)


EV_WIDTHS = (Q_LORA + KV_LORA + LANES, 3 * GDN_H * GDN_DK, GDN_H * GDN_DV, LANES)


def _mla_q_cols():
    per = MLA_NOPE + MLA_ROPE
    half = MLA_ROPE // 2
    main, sw = [], []
    for h in range(MLA_H):
        b = h * per
        main += [np.arange(b, b + per), -np.ones(LANES - per, int)]
        sw += [-np.ones(MLA_NOPE, int), np.arange(b + MLA_NOPE + half, b + per), np.arange(b + MLA_NOPE, b + MLA_NOPE + half),
               -np.ones(LANES - per, int)]
    return np.concatenate(main + sw)


def _mla_kv_cols():
    per = MLA_NOPE + MLA_V
    kc, vc = [], []
    for h in range(MLA_H):
        b = h * per
        kc += [np.arange(b, b + MLA_NOPE), -np.ones(LANES - MLA_NOPE, int)]
        vv = np.arange(b + MLA_NOPE, b + per)
        pad = -np.ones(LANES - MLA_V, int)
        vc += [vv, pad] if h % 2 == 0 else [pad, vv]
    return np.concatenate(kc + vc)


def _odd_in_cols():
    z = lambda n: -np.ones(n, int)
    o = 0
    cols = []
    mq0, mk0 = 0, ML_H * ML_DK
    for base in (mq0, mk0):
        for h in range(ML_H):
            cols += [np.arange(base + h * ML_DK, base + (h + 1) * ML_DK), z(LANES - ML_DK)]
    mv0 = 2 * ML_H * ML_DK
    cols.append(np.arange(mv0, mv0 + ML_H * ML_DV))
    mi0 = mv0 + ML_H * ML_DV
    mo0 = mi0 + 2 * ML_H
    cols.append(np.arange(mo0, mo0 + ML_H * ML_DV))
    cols += [np.arange(mi0, mi0 + 2 * ML_H), z(LANES - 2 * ML_H)]
    sq0 = mo0 + ML_H * ML_DV
    sk0 = sq0 + SWA_H * SWA_D
    sv0 = sk0 + SWA_KV * SWA_D
    half = SWA_D // 2

    cols.append(np.arange(sq0, sq0 + SWA_H * SWA_D))
    for g in range(SWA_KV):
        cols += [np.arange(sk0 + g * SWA_D, sk0 + (g + 1) * SWA_D)] * 2
    for g in range(SWA_KV):
        vv = np.arange(sv0 + g * SWA_D, sv0 + (g + 1) * SWA_D)
        cols += [vv, z(LANES - SWA_D), z(LANES - SWA_D), vv]
    return np.concatenate(cols)


def _even_weights(w_in, w_qb, w_kvb):
    return (_take_cols(w_in, _even_in_cols()), _take_cols(w_qb, _mla_q_cols()), _take_cols(w_kvb, _mla_kv_cols()))


def _even_mixer(x, tabs, weights, q_norm, kv_norm, conv_w, a_log, dt_bias, o_norm, batch, seq):
    ctab, stab = tabs
    w, wq2, wkv2 = weights
    mla_in, act, z, gates = _proj_even(x, w, conv_w, seq)
    q, k, v = _mla_prep(mla_in, ctab, stab, q_norm.reshape(1, -1), kv_norm.reshape(1, -1), wq2, wkv2)
    o_a = _mla_attn(q, k, v, batch, seq)
    o_b = _gdn(act, gates, z, _pad_lane_row(a_log, GDN_H), _pad_lane_row(dt_bias, GDN_H),
               o_norm.reshape(1, -1), batch, seq)
    return o_a, o_b


def _odd_mixer(x, tabs, w, b_i, b_f, ml_norm, sinks, batch, seq):
    ctab, stab = tabs
    mq, mk, mv, mo, mg, sq, sk, sv = _proj_odd(x, w, ctab, stab)
    bias_row = _pad_lane_row(jnp.concatenate([b_i, b_f]), 0)
    o_c = _mlstm(mq, mk, mv, mo, mg, bias_row, ml_norm.reshape(1, -1), batch, seq)
    o_d = _swa(sq, sk, sv, _pad_lane_row(sinks, 0), batch, seq)
    return o_c, o_d


def _moe(x, xp, router_w, router_b, w_gate, w_up, w_down, layer, s_gate, s_up, s_down, ln_g, ln_b):
    t, d = x.shape
    bias_col = jnp.broadcast_to(router_b.reshape(-1, 1).astype(F32), (N_EXPERTS, LANES))
    idx, gate, rank, cnt = _router(x, router_w.T, bias_col)
    counts = cnt[:, 0].astype(jnp.int32)
    block = int(min(max(pl.next_power_of_2(t * TOP_K // N_EXPERTS) // 2, EXPERT_BLOCK_MIN), EXPERT_BLOCK_MAX))
    padded = (counts + block - 1) // block * block
    pad_end = jnp.cumsum(padded)
    pad_start = pad_end - padded
    start_col = jnp.broadcast_to(pad_start.astype(F32).reshape(-1, 1), (N_EXPERTS, LANES))
    dest = _dest_rows(idx, rank, start_col)
    n_blocks = t * TOP_K // block + N_EXPERTS
    rows = n_blocks * block
    block_row = jnp.arange(n_blocks, dtype=jnp.int32) * block
    block_e = jnp.minimum(jnp.sum((pad_end[None, :] <= block_row[:, None]).astype(jnp.int32), axis=1), N_EXPERTS - 1)
    n_used = (pad_end[-1:] // block).astype(jnp.int32)
    live_end = jnp.sum(jnp.where(block_e[:, None] == jnp.arange(N_EXPERTS, dtype=jnp.int32)[None, :],
                                 (pad_start + counts)[None, :], 0), axis=1)
    n_valid = jnp.clip(live_end - block_row, 0, block).astype(jnp.int32)
    xs = _sc_scatter_rows(xp, dest, rows)
    ys = _experts(block_e, n_used, n_valid, xs, w_gate, w_up, w_down, layer, block)
    picked = _sc_gather_rows(ys, dest.reshape(-1)).reshape(TOP_K, t, d // 2)
    shared = _shared_expert(xp, s_gate.astype(BF16), s_up.astype(BF16), s_down.astype(BF16))
    return _combine(x, gate.T, picked, shared, ln_g.reshape(1, -1), ln_b.reshape(1, -1))


def kernel(x, positions, ev_w_in, mla_q_norm, mla_w_qb, mla_kv_norm, mla_w_kvb, gdn_conv, gdn_a_log, gdn_dt_bias, gdn_norm, ev_w_out, od_w_in, mlstm_b_i, mlstm_b_f, mlstm_norm, swa_sinks, od_w_out, ln1_g, ln1_b, router_w, router_b, moe_w_gate, moe_w_up, moe_w_down, shared_w_gate, shared_w_up, shared_w_down, ln2_g, ln2_b):
    batch, seq, d = x.shape
    streams = STREAMS if batch % STREAMS == 0 else 1
    sb = batch // streams
    ts = sb * seq
    hs, tabs_m, tabs_s = [], [], []
    for s in range(streams):
        pos = positions[s * sb:(s + 1) * sb].reshape(ts, 1).astype(F32)
        tm_, ts_ = _rope_tables(pos)
        tabs_m.append(tm_)
        tabs_s.append(ts_)
        hs.append(x[s * sb:(s + 1) * sb].reshape(ts, d))
    for layer in range(DEPTH):
        j = layer // 2
        if layer % 2 == 0:
            weights = _even_weights(ev_w_in[j], mla_w_qb[j], mla_w_kvb[j])
            w_out = ev_w_out[j].astype(BF16)
        else:
            weights = _take_cols(od_w_in[j], _odd_in_cols())
            w_out = od_w_out[j].astype(BF16)
        for s in range(streams):
            h = hs[s]
            if layer % 2 == 0:
                a1, a2 = _even_mixer(h, tabs_m[s], weights, mla_q_norm[j], mla_kv_norm[j], gdn_conv[j], gdn_a_log[j],
                                     gdn_dt_bias[j], gdn_norm[j], sb, seq)
            else:
                a1, a2 = _odd_mixer(h, tabs_s[s], weights, mlstm_b_i[j], mlstm_b_f[j], mlstm_norm[j], swa_sinks[j], sb, seq)
            h, hp = _outproj_ln(h, a1, a2, w_out, ln1_g[layer].reshape(1, -1), ln1_b[layer].reshape(1, -1))
            hs[s] = _moe(h, hp, router_w[layer], router_b[layer], moe_w_gate, moe_w_up, moe_w_down, layer,
                         shared_w_gate[layer], shared_w_up[layer], shared_w_down[layer], ln2_g[layer], ln2_b[layer])
    return jnp.concatenate([h.reshape(sb, seq, d) for h in hs], axis=0)
```

```python
import functools
import math

import numpy as np
import jax
import jax.numpy as jnp
from jax import lax
from jax.experimental import pallas as pl
from jax.experimental.pallas import tpu as pltpu
from jax.experimental.pallas import tpu_sc as plsc

F32 = jnp.float32
BF16 = jnp.bfloat16

D_MODEL = 1024
DEPTH = 4
ROPE_THETA = 10000.0
EPS = 1e-6
LN_EPS = 1e-5
MLA_H, MLA_NOPE, MLA_ROPE, MLA_V = 8, 64, 32, 64
Q_LORA, KV_LORA = 256, 128
GDN_H, GDN_DK, GDN_DV, CONV_W, GDN_CHUNK = 4, 128, 128, 4, 64
ML_H, ML_DK, ML_DV, ML_CHUNK = 4, 64, 128, 64
SWA_H, SWA_KV, SWA_D, WINDOW = 8, 2, 64, 128
N_EXPERTS, N_GROUPS, TOPK_GROUPS, TOP_K = 64, 8, 4, 8
D_EXPERT, D_SHARED = 256, 256
ROUTED_SCALE = 2.5
DN_ALPHA = (2 * DEPTH) ** 0.25

LANES = 128
SUBLANES = 8
V7X_VMEM_BYTES = 64 * 1024 * 1024
VMEM_LIMIT = V7X_VMEM_BYTES * 3 // 4

EXPERT_BLOCK_MIN = 256
EXPERT_BLOCK_MAX = 1024
STREAMS = 1
EXPERT_SUBBLOCKS = 4
SWA_SEQS_PER_STEP = 8
MLSTM_SEQS_PER_STEP = 2
GDN_SEQS_PER_STEP = 8
SC_CHUNK = 64


def _cparams(sem, vmem=VMEM_LIMIT):
    return pltpu.CompilerParams(dimension_semantics=sem, vmem_limit_bytes=vmem)


def _dot(a, b):
    return jnp.dot(a, b, preferred_element_type=F32)


def _dot_nt(a, b):
    return lax.dot_general(a, b, (((1,), (1,)), ((), ())), preferred_element_type=F32)


def _dot_tn(a, b):
    return lax.dot_general(a, b, (((0,), (0,)), ((), ())), preferred_element_type=F32)


def _split2(a):
    hi = a.astype(BF16)
    lo = (a - hi.astype(F32)).astype(BF16)
    return hi, lo


def _split3(a):
    p1 = a.astype(BF16)
    r = a - p1.astype(F32)
    p2 = r.astype(BF16)
    p3 = (r - p2.astype(F32)).astype(BF16)
    return p1, p2, p3


def _dot3(a, b, dot=_dot):
    ah, al = _split2(a)
    bh, bl = _split2(b)
    return dot(ah, bh) + (dot(ah, bl) + dot(al, bh))


def _dot_sel(sel, b, dot=_dot):
    sel = sel.astype(BF16)
    p1, p2, p3 = _split3(b)
    return dot(sel, p1) + (dot(sel, p2) + dot(sel, p3))


def _sigmoid(x):
    return 1.0 / (1.0 + jnp.exp(-x))


def _softplus(x):
    return jnp.maximum(x, 0.0) + jnp.log(1.0 + jnp.exp(-jnp.abs(x)))


def _silu(x):
    return x * _sigmoid(x)


def _lane_bcast(x, c):
    return jnp.broadcast_to(x[:, c:c + 1], x.shape)


def _iota2(shape, dim):
    return lax.broadcasted_iota(jnp.int32, shape, dim)


def _rope_kernel(pos_ref, rows_ref, sel_ref, cm_ref, sm_ref, cs_ref, ss_ref):
    ang = pos_ref[...] * rows_ref[0:1, :]
    cos_parts = _split3(jnp.cos(ang))
    sin_parts = _split3(jnp.sin(ang))

    def place(parts, k):
        return _dot(parts[0], sel_ref[k]) + (_dot(parts[1], sel_ref[k]) + _dot(parts[2], sel_ref[k]))

    cm_ref[...] = place(cos_parts, 0) + rows_ref[1:2, :]
    sm_ref[...] = place(sin_parts, 1)
    cs_ref[...] = place(cos_parts, 2)
    ss_ref[...] = place(sin_parts, 3)


def _rope_consts():
    hm, hs = MLA_ROPE // 2, SWA_D // 2
    rows = np.zeros((8, LANES), np.float32)
    rows[0, :hm] = ROPE_THETA ** (-(np.arange(0, MLA_ROPE, 2, dtype=np.float32) / MLA_ROPE))
    rows[0, hm:hm + hs] = ROPE_THETA ** (-(np.arange(0, SWA_D, 2, dtype=np.float32) / SWA_D))
    rows[1, :MLA_NOPE] = 1.0
    sel = np.zeros((4, LANES, LANES), np.float32)
    for j in range(hm):
        sel[0, j, MLA_NOPE + j] = sel[0, j, MLA_NOPE + hm + j] = 1.0
        sel[1, j, MLA_NOPE + j] = -1.0
        sel[1, j, MLA_NOPE + hm + j] = 1.0
    for h in range(LANES // SWA_D):
        for j in range(hs):
            sel[2, hm + j, h * SWA_D + j] = sel[2, hm + j, h * SWA_D + hs + j] = 1.0
            sel[3, hm + j, h * SWA_D + j] = -1.0
            sel[3, hm + j, h * SWA_D + hs + j] = 1.0
    return jnp.asarray(rows), jnp.asarray(sel, BF16)


def _rope_tables(pos, tm=512):
    t = pos.shape[0]
    tm = min(tm, t)
    rows, sel = _rope_consts()
    cm, sm, cs, ss = pl.pallas_call(
        _rope_kernel,
        grid=(t // tm,),
        in_specs=[pl.BlockSpec((tm, 1), lambda i: (i, 0)), pl.BlockSpec((8, LANES), lambda i: (0, 0)),
                  pl.BlockSpec((4, LANES, LANES), lambda i: (0, 0, 0))],
        out_specs=[pl.BlockSpec((tm, LANES), lambda i: (i, 0))] * 4,
        out_shape=[jax.ShapeDtypeStruct((t, LANES), F32)] * 4,
        compiler_params=_cparams(("arbitrary",)),
        name="rope_tables",
    )(pos, rows, sel)
    return (cm, sm), (cs, ss)


N_COMBINE_IN = 6
FUSED_TM = 256


def _combined(x_ref, gate_ref, rows_ref, sh_ref, g_ref, b_ref):
    gate = gate_ref[...]
    ya, yb = _unpack_pairs(sh_ref[...])
    for s in range(TOP_K):
        a, b = _unpack_pairs(rows_ref[s])
        ya = ya + gate[:, s:s + 1] * a
        yb = yb + gate[:, s:s + 1] * b
    ff = jnp.concatenate([ya, yb], axis=1)
    return _layer_norm(DN_ALPHA * x_ref[...] + ff, g_ref[...], b_ref[...])


def _stream_specs(stream, tm):
    d = stream[0].shape[1]
    row = lambda i: (i, 0)
    fix = lambda i: (0, 0)
    specs = [pl.BlockSpec((tm, d), row)]
    if len(stream) > 1:
        specs += [pl.BlockSpec((tm, TOP_K), row), pl.BlockSpec((TOP_K, tm, d // 2), lambda i: (0, i, 0)),
                  pl.BlockSpec((tm, d // 2), row), pl.BlockSpec((1, d), fix), pl.BlockSpec((1, d), fix)]
    return specs


def _stream_tile(stream_refs, h_ref):
    if h_ref is None:
        return stream_refs[0][...].astype(BF16)
    h = _combined(*stream_refs)
    h_ref[...] = h
    return h.astype(BF16)


def _proj_even_kernel(*refs, tiles_per_seq, fused):
    n_in = N_COMBINE_IN if fused else 1
    stream_refs, (w_ref, cw_ref, mla_ref, act_ref, z_ref, g_ref) = refs[:n_in], refs[n_in:n_in + 6]
    h_ref = refs[n_in + 6] if fused else None
    ext_ref = refs[-1]
    tm = mla_ref.shape[0]
    o = np.concatenate([[0], np.cumsum(EV_WIDTHS)]).tolist()
    halo = SUBLANES
    tap0 = halo - (CONV_W - 1)

    @pl.when(pl.program_id(0) % tiles_per_seq == 0)
    def _():
        ext_ref[0:halo, :] = jnp.zeros((halo, ext_ref.shape[1]), F32)

    xb = _stream_tile(stream_refs, h_ref)
    nchunk = 3
    cw = EV_WIDTHS[1] // nchunk

    def project(ci):
        ext_ref[halo:halo + tm, ci * cw:(ci + 1) * cw] = _dot(xb, w_ref[:, o[1] + ci * cw:o[1] + (ci + 1) * cw])

    project(0)
    for ci in range(nchunk):
        if ci + 1 < nchunk:
            project(ci + 1)
        else:
            mla_ref[...] = _dot(xb, w_ref[:, o[0]:o[1]])
            z_ref[...] = _dot(xb, w_ref[:, o[2]:o[3]]).astype(z_ref.dtype)
            g_ref[...] = _dot(xb, w_ref[:, o[3]:o[4]])
        cols = slice(ci * cw, (ci + 1) * cw)
        conv = cw_ref[0:1, cols] * ext_ref[tap0:tap0 + tm, cols]
        for j in range(1, CONV_W):
            conv = conv + cw_ref[j:j + 1, cols] * ext_ref[tap0 + j:tap0 + j + tm, cols]
        act_ref[:, cols] = _silu(conv).astype(act_ref.dtype)
    ext_ref[0:halo, :] = ext_ref[tm:tm + halo, :]


def _proj_even(stream, w, conv_w, seq, tm=512):
    t, k = stream[0].shape
    fused = len(stream) > 1
    tm = min(FUSED_TM if fused else tm, seq)
    row = lambda i: (i, 0)
    fix = lambda i: (0, 0)
    widths = EV_WIDTHS + ((k,) if fused else ())
    outs = pl.pallas_call(
        functools.partial(_proj_even_kernel, tiles_per_seq=seq // tm, fused=fused),
        grid=(t // tm,),
        in_specs=_stream_specs(stream, tm) + [pl.BlockSpec(w.shape, fix), pl.BlockSpec(conv_w.shape, fix)],
        out_specs=[pl.BlockSpec((tm, n), row) for n in widths],
        out_shape=[jax.ShapeDtypeStruct((t, n), F32) for n in widths],
        scratch_shapes=[pltpu.VMEM((tm + SUBLANES, EV_WIDTHS[1]), F32)],
        compiler_params=_cparams(("arbitrary",)),
        name="combine_in_proj" if fused else "in_proj",
    )(*stream, w, conv_w)
    return (outs[-1] if fused else stream[0],) + tuple(outs[:len(EV_WIDTHS)])


OD_SEG = dict(mq=(0, 512), mk=(512, 1024), mv=(1024, 1536), mo=(1536, 2048), gates=(2048, 2176),
              sq=(2176, 2688), sk=(2688, 2944), sv=(2944, 3456))
OD_COLS = 3456


def _proj_odd_kernel(*refs, fused):
    n_in = N_COMBINE_IN if fused else 1
    stream_refs = refs[:n_in]
    w_ref, c_ref, s_ref, mq_ref, mk_ref, mv_ref, mo_ref, mg_ref, sq_ref, sk_ref, sv_ref = refs[n_in:n_in + 11]
    xb = _stream_tile(stream_refs, refs[n_in + 11] if fused else None)

    def seg(name):
        a, b = OD_SEG[name]
        return _dot(xb, w_ref[:, a:b])

    mq_ref[...] = seg("mq").astype(mq_ref.dtype)
    mk_ref[...] = seg("mk").astype(mk_ref.dtype)
    mv_ref[...] = seg("mv").astype(mv_ref.dtype)
    mo_ref[...] = seg("mo").astype(mo_ref.dtype)
    mg_ref[...] = seg("gates")
    c = c_ref[...]
    s = s_ref[...]
    def swap_halves(t):
        half = SWA_D // 2
        first_half = (_iota2(t.shape, 1) % SWA_D) < half
        return jnp.where(first_half, pltpu.roll(t, t.shape[1] - half, 1), pltpu.roll(t, half, 1))

    c8 = jnp.concatenate([c] * (SWA_H // 2), axis=1)
    s8 = jnp.concatenate([s] * (SWA_H // 2), axis=1)
    q = seg("sq")
    sq_ref[...] = (q * c8 + swap_halves(q) * s8).astype(sq_ref.dtype)
    c2 = jnp.concatenate([c] * SWA_KV, axis=1)
    s2 = jnp.concatenate([s] * SWA_KV, axis=1)
    k = seg("sk")
    sk_ref[...] = (k * c2 + swap_halves(k) * s2).astype(sk_ref.dtype)
    sv_ref[...] = seg("sv").astype(sv_ref.dtype)


def _proj_odd(stream, w, ctab, stab, tm=512):
    t, k = stream[0].shape
    fused = len(stream) > 1
    tm = min(FUSED_TM if fused else tm, t)
    widths = (512, 512, 512, 512, 128, SWA_H * SWA_D, SWA_KV * LANES, 2 * SWA_KV * LANES)
    dtypes = (F32, F32, F32, F32, F32, BF16, BF16, BF16)
    n_out = len(widths)
    if fused:
        widths, dtypes = widths + (k,), dtypes + (F32,)
    outs = pl.pallas_call(
        functools.partial(_proj_odd_kernel, fused=fused),
        grid=(t // tm,),
        in_specs=_stream_specs(stream, tm) + [pl.BlockSpec(w.shape, lambda i: (0, 0)),
                                              pl.BlockSpec((tm, LANES), lambda i: (i, 0)),
                                              pl.BlockSpec((tm, LANES), lambda i: (i, 0))],
        out_specs=[pl.BlockSpec((tm, n), lambda i: (i, 0)) for n in widths],
        out_shape=[jax.ShapeDtypeStruct((t, n), dt) for n, dt in zip(widths, dtypes)],
        compiler_params=_cparams(("arbitrary",)),
        name="combine_in_proj_odd" if fused else "in_proj_odd",
    )(*stream, w, ctab, stab)
    return (outs[-1] if fused else stream[0],) + tuple(outs[:n_out])


def _rms(x, g):
    return x * lax.rsqrt(jnp.mean(x * x, axis=-1, keepdims=True) + EPS) * g


def _mla_prep_kernel(in_ref, c_ref, s_ref, qn_ref, kvn_ref, wq_ref, wkv_ref, q_ref, k_ref, v_ref):
    hw = MLA_H * LANES
    c = c_ref[...]
    s = s_ref[...]
    c8 = jnp.concatenate([c] * MLA_H, axis=1)
    s8 = jnp.concatenate([s] * MLA_H, axis=1)
    def swap_halves(t):
        half = MLA_ROPE // 2
        first_half = (_iota2(t.shape, 1) % LANES) < MLA_NOPE + half
        return jnp.where(first_half, pltpu.roll(t, t.shape[1] - half, 1), pltpu.roll(t, half, 1))

    cqn = _rms(in_ref[:, 0:Q_LORA], qn_ref[...]).astype(BF16)
    qq = _dot(cqn, wq_ref[...])
    scale = (MLA_NOPE + MLA_ROPE) ** -0.5
    q_ref[...] = ((qq[:, :hw] * c8 + qq[:, hw:] * s8) * scale).astype(q_ref.dtype)
    ckvn = _rms(in_ref[:, Q_LORA:Q_LORA + KV_LORA], kvn_ref[...]).astype(BF16)
    kv = _dot(ckvn, wkv_ref[...])
    o = Q_LORA + KV_LORA
    kr = in_ref[:, o:o + LANES]
    krr = kr * c + swap_halves(kr) * s
    k_ref[...] = (kv[:, :hw] + jnp.concatenate([krr] * MLA_H, axis=1)).astype(k_ref.dtype)
    v_ref[...] = kv[:, hw:].astype(v_ref.dtype)


def _mla_prep(mla_in, ctab, stab, qn, kvn, wq2, wkv2, tm=1024):
    t = mla_in.shape[0]
    tm = min(tm, t)
    hw = MLA_H * LANES
    row = lambda i: (i, 0)
    fix = lambda i: (0, 0)
    return pl.pallas_call(
        _mla_prep_kernel,
        grid=(t // tm,),
        in_specs=[pl.BlockSpec((tm, mla_in.shape[1]), row), pl.BlockSpec((tm, LANES), row), pl.BlockSpec((tm, LANES), row),
                  pl.BlockSpec(qn.shape, fix), pl.BlockSpec(kvn.shape, fix),
                  pl.BlockSpec(wq2.shape, fix), pl.BlockSpec(wkv2.shape, fix)],
        out_specs=[pl.BlockSpec((tm, hw), row)] * 3,
        out_shape=[jax.ShapeDtypeStruct((t, hw), BF16)] * 3,
        compiler_params=_cparams(("arbitrary",)),
        name="mla_prep",
    )(mla_in, ctab, stab, qn, kvn, wq2, wkv2)


def _mla_attn_kernel(q_ref, k_ref, v_ref, o_ref, *, tq):
    i = pl.program_id(2)
    neg = -1e30
    lane = _iota2((tq, LANES), 1)
    ones_lane = (MLA_V, 0)

    def chunk(j, carry, masked):
        start = pl.multiple_of(j * tq, tq)
        out = []
        for hh in range(2):
            m, acc = carry[hh]
            q = q_ref[:, hh * LANES:(hh + 1) * LANES]
            kc = k_ref[pl.ds(start, tq), hh * LANES:(hh + 1) * LANES]
            vc = v_ref[pl.ds(start, tq), hh * LANES:(hh + 1) * LANES]
            vc = jnp.where(lane == ones_lane[hh], jnp.ones_like(vc), vc)
            s = _dot_nt(q, kc)
            if masked:
                s = jnp.where(_iota2(s.shape, 0) >= _iota2(s.shape, 1), s, neg)
            m_new = jnp.maximum(m, jnp.max(s, axis=-1, keepdims=True))
            alpha = jnp.exp(m - m_new)
            p = jnp.exp((s - m_new).astype(BF16))
            acc = alpha * acc + _dot(p, vc)
            out.append((m_new, acc))
        return tuple(out)

    one = (jnp.full((tq, 1), neg, F32), jnp.zeros((tq, LANES), F32))
    carry = lax.fori_loop(0, i, lambda j, c: chunk(j, c, False), (one, one))
    (_, acc0), (_, acc1) = chunk(i, carry, True)
    o0 = acc0 / _lane_bcast(acc0, ones_lane[0])
    o1 = acc1 / _lane_bcast(acc1, ones_lane[1])
    o_ref[...] = jnp.where(lane < MLA_V, o0, o1).astype(o_ref.dtype)


def _mla_attn(q, k, v, batch, seq, tq=512):
    tq = min(tq, seq)
    nq = seq // tq
    pairs = MLA_H // 2
    return pl.pallas_call(
        functools.partial(_mla_attn_kernel, tq=tq),
        grid=(batch, pairs, nq),
        in_specs=[pl.BlockSpec((tq, 2 * LANES), lambda b, p, i: (b * nq + i, p)),
                  pl.BlockSpec((seq, 2 * LANES), lambda b, p, i: (b, p)),
                  pl.BlockSpec((seq, 2 * LANES), lambda b, p, i: (b, p))],
        out_specs=pl.BlockSpec((tq, LANES), lambda b, p, i: (b * nq + i, p)),
        out_shape=jax.ShapeDtypeStruct((batch * seq, pairs * LANES), BF16),
        compiler_params=_cparams(("arbitrary", "arbitrary", "arbitrary")),
        name="mla_attn",
    )(q, k, v)


def _unit_lower_inverse_many(ns):
    c = ns[0].shape[0]
    eye = (_iota2((c, c), 0) == _iota2((c, c), 1)).astype(F32)
    xs = [-n for n in ns]
    ps = [eye + x for x in xs]
    xb = [x.astype(BF16) for x in xs]
    for _ in range(int(math.log2(c)) - 1):
        xs = [_dot(b, b) for b in xb]
        xb = [x.astype(BF16) for x in xs]
        ps = [p + _dot(p.astype(BF16), b) for p, b in zip(ps, xb)]
    return ps


def _gdn_kernel(act_ref, g_ref, z_ref, al_ref, dt_ref, on_ref, o_ref, st_ref):
    c = GDN_CHUNK
    hd = GDN_DK
    nqk = GDN_H * GDN_DK

    @pl.when(pl.program_id(1) == 0)
    def _():
        st_ref[...] = jnp.zeros(st_ref.shape, F32)

    tri = (_iota2((c, c), 0) >= _iota2((c, c), 1)).astype(F32)
    row_ge = _iota2((c, c), 0) >= _iota2((c, c), 1)
    row_gt = _iota2((c, c), 0) > _iota2((c, c), 1)
    lane = _iota2((c, LANES), 1)

    seqs = []
    for bb in range(act_ref.shape[0]):
        gates = g_ref[bb]
        g_all = -jnp.exp(al_ref[...]) * _softplus(gates + dt_ref[...])
        gc_all = _dot_sel(tri, g_all)
        seqs.append(dict(beta_all=_sigmoid(gates), gc_all=gc_all, gc_parts=_split3(gc_all)))
    units = []
    for bb, sq in enumerate(seqs):
        for h in range(GDN_H):
            q = act_ref[bb, :, h * hd:(h + 1) * hd].astype(F32)
            k = act_ref[bb, :, nqk + h * hd:nqk + (h + 1) * hd].astype(F32)
            v = act_ref[bb, :, 2 * nqk + h * GDN_DV:2 * nqk + (h + 1) * GDN_DV].astype(F32)
            q = q * lax.rsqrt(jnp.sum(q * q, axis=-1, keepdims=True) + EPS) * (GDN_DK ** -0.5)
            k = k * lax.rsqrt(jnp.sum(k * k, axis=-1, keepdims=True) + EPS)
            beta = _lane_bcast(sq["beta_all"], h)
            gcol = _lane_bcast(sq["gc_all"], GDN_H + h)
            units.append(dict(bb=bb, h=h, q=q, k=k, v=v, beta=beta, gcol=gcol, kb=k * beta, parts=sq["gc_parts"]))
    for u in units:
        pick = (lane == GDN_H + u["h"]).astype(BF16)
        p0, p1, p2 = u["parts"]
        u["grow"] = _dot_nt(pick, p0) + (_dot_nt(pick, p1) + _dot_nt(pick, p2))
        u["kk"] = _dot3(u["kb"], u["k"], _dot_nt)
        u["qk"] = _dot_nt(u["q"].astype(BF16), u["k"].astype(BF16))
    for u in units:
        gcol = u["gcol"]
        decay = jnp.exp(jnp.where(row_ge, gcol[:, :c] - u["grow"], -jnp.inf))
        eg = jnp.exp(gcol)
        glast = gcol[c - 1:c, :]
        u["lower"] = jnp.where(row_gt, u["kk"] * decay, 0.0)
        u["rhs"] = jnp.concatenate([u["v"] * u["beta"], u["kb"] * eg], axis=1)
        u["attn"] = u["qk"] * decay
        u["qg"] = (u["q"] * eg).astype(BF16)
        u["kg"] = (u["k"] * jnp.exp(glast - gcol)).astype(BF16)
        u["gl"] = jnp.exp(glast)

    tinvs = _unit_lower_inverse_many([u["lower"] for u in units])
    uws = []
    for u, tinv in zip(units, tinvs):
        uws.append(_dot(tinv.astype(BF16), u["rhs"].astype(BF16)))
    states = [st_ref[u["bb"], u["h"]] for u in units]
    sbs = [s.astype(BF16) for s in states]
    vnews = [(uw[:, :GDN_DV] - _dot(uw[:, GDN_DV:].astype(BF16), sb)).astype(BF16) for uw, sb in zip(uws, sbs)]
    for u, state, sb, vnb in zip(units, states, sbs, vnews):
        bb, h = u["bb"], u["h"]
        o = _dot(u["qg"], sb) + _dot(u["attn"].astype(BF16), vnb)
        st_ref[bb, h] = state * u["gl"] + _dot_tn(u["kg"], vnb)
        o = _rms(o, on_ref[...]) * _silu(z_ref[bb, :, h * GDN_DV:(h + 1) * GDN_DV].astype(F32))
        o_ref[bb, :, h * GDN_DV:(h + 1) * GDN_DV] = o.astype(o_ref.dtype)


def _gdn(act, gates, z, a_row, dt_row, o_norm, batch, seq):
    c = GDN_CHUNK
    nc = seq // c
    w3 = act.shape[1]
    wo = GDN_H * GDN_DV
    nb = min(GDN_SEQS_PER_STEP, batch)
    row = lambda b, i: (b, i, 0)
    fix = lambda b, i: (0, 0)
    out = pl.pallas_call(
        _gdn_kernel,
        grid=(batch // nb, nc),
        in_specs=[pl.BlockSpec((nb, c, w3), row), pl.BlockSpec((nb, c, LANES), row), pl.BlockSpec((nb, c, wo), row),
                  pl.BlockSpec((1, LANES), fix), pl.BlockSpec((1, LANES), fix), pl.BlockSpec((1, GDN_DV), fix)],
        out_specs=pl.BlockSpec((nb, c, wo), row),
        out_shape=jax.ShapeDtypeStruct((batch, seq, wo), BF16),
        scratch_shapes=[pltpu.VMEM((nb, GDN_H, GDN_DK, GDN_DV), F32)],
        compiler_params=_cparams(("arbitrary", "arbitrary")),
        name="gdn",
    )(act.reshape(batch, seq, w3), gates.reshape(batch, seq, LANES), z.reshape(batch, seq, wo), a_row, dt_row, o_norm)
    return out.reshape(batch * seq, wo)


def _mlstm_kernel(q_ref, k_ref, v_ref, og_ref, g_ref, bias_ref, nrm_ref, o_ref, c_ref, n_ref, m_ref):
    @pl.when(pl.program_id(1) == 0)
    def _():
        c_ref[...] = jnp.zeros(c_ref.shape, F32)
        n_ref[...] = jnp.zeros(n_ref.shape, F32)
        m_ref[...] = jnp.zeros(m_ref.shape, F32)

    c = ML_CHUNK
    tri = (_iota2((c, c), 0) >= _iota2((c, c), 1)).astype(F32)
    row_ge = _iota2((c, c), 0) >= _iota2((c, c), 1)
    ones = jnp.ones((c, LANES), F32)
    lane = _iota2((c, LANES), 1)

    units = []
    for bb in range(q_ref.shape[0]):
        pre = g_ref[bb] + bias_ref[...]
        logf = jnp.minimum(pre, 0.0) - jnp.log(1.0 + jnp.exp(-jnp.abs(pre)))
        bcum_all = _dot_sel(tri, logf)
        for h in range(ML_H):
            q = q_ref[bb, :, h * LANES:(h + 1) * LANES].astype(F32)
            k = k_ref[bb, :, h * LANES:(h + 1) * LANES].astype(F32) * (ML_DK ** -0.5)
            units.append(dict(bb=bb, h=h, q=q, k=k, qb=q.astype(BF16), vb=v_ref[bb, :, h * ML_DV:(h + 1) * ML_DV].astype(BF16),
                              bcol=_lane_bcast(bcum_all, ML_H + h),
                              icol=_lane_bcast(pre, h),
                              col=jnp.where(lane == h, pre, 0.0) - jnp.where(lane == ML_H + h, bcum_all, 0.0),
                              m_st=m_ref[bb, h], cst=c_ref[bb, h], nst=n_ref[bb, h]))
    for u in units:
        u["row"] = _dot_sel(ones, u["col"], _dot_nt)
        u["qk"] = _dot_nt(u["qb"], u["k"].astype(BF16))
        u["qc"] = _dot(u["qb"], u["cst"].astype(BF16))
    for u in units:
        u["d"] = jnp.where(row_ge, u["bcol"][:, :c] + u["row"], -jnp.inf)
        u["inter"] = u["bcol"] + u["m_st"]
        u["m_t"] = jnp.maximum(u["inter"], jnp.max(u["d"], axis=-1, keepdims=True))
        u["b_end"] = u["bcol"][c - 1:c, :]
        u["a"] = u["b_end"] - u["bcol"] + u["icol"]
        u["m_new"] = jnp.maximum(u["b_end"] + u["m_st"], jnp.max(u["a"], axis=0, keepdims=True))
    for u in units:
        u["w_inter"] = jnp.exp(u["inter"] - u["m_t"])
        u["p"] = jnp.exp(u["d"] - u["m_t"][:, :c]) * u["qk"]
        u["keep"] = jnp.exp(u["b_end"] + u["m_st"] - u["m_new"])
        u["ks"] = u["k"] * jnp.exp(u["a"] - u["m_new"])
    for u in units:
        u["pv"] = _dot(u["p"].astype(BF16), u["vb"])
        u["kv"] = _dot_tn(u["ks"].astype(BF16), u["vb"])
    for u in units:
        u["den"] = (u["w_inter"] * jnp.sum(u["q"] * u["nst"], axis=-1, keepdims=True)
                    + jnp.sum(u["p"], axis=-1, keepdims=True))
    for u in units:
        bb, h = u["bb"], u["h"]
        num = u["w_inter"] * u["qc"] + u["pv"]
        hc = num / jnp.maximum(jnp.abs(u["den"]), jnp.exp(-u["m_t"]))
        c_ref[bb, h] = u["cst"] * u["keep"] + u["kv"]
        n_ref[bb, h] = u["nst"] * u["keep"] + jnp.sum(u["ks"], axis=0, keepdims=True)
        m_ref[bb, h] = u["m_new"]
        hn = (_rms(hc, nrm_ref[:, h * ML_DV:(h + 1) * ML_DV])
              * _sigmoid(og_ref[bb, :, h * ML_DV:(h + 1) * ML_DV].astype(F32)))
        o_ref[bb, :, h * ML_DV:(h + 1) * ML_DV] = hn.astype(o_ref.dtype)


def _mlstm(mq, mk, mv, mo, gates, bias_row, norm_row, batch, seq):
    c = ML_CHUNK
    nc = seq // c
    nb = min(MLSTM_SEQS_PER_STEP, batch)
    row = lambda b, i: (b, i, 0)
    fix = lambda b, i: (0, 0)
    wide = ML_H * LANES
    r3 = lambda a: a.reshape(batch, seq, a.shape[-1])
    out = pl.pallas_call(
        _mlstm_kernel,
        grid=(batch // nb, nc),
        in_specs=[pl.BlockSpec((nb, c, wide), row), pl.BlockSpec((nb, c, wide), row), pl.BlockSpec((nb, c, wide), row),
                  pl.BlockSpec((nb, c, wide), row), pl.BlockSpec((nb, c, LANES), row),
                  pl.BlockSpec((1, LANES), fix), pl.BlockSpec((1, wide), fix)],
        out_specs=pl.BlockSpec((nb, c, wide), row),
        out_shape=jax.ShapeDtypeStruct((batch, seq, wide), BF16),
        scratch_shapes=[pltpu.VMEM((nb, ML_H, LANES, ML_DV), F32), pltpu.VMEM((nb, ML_H, 1, LANES), F32),
                        pltpu.VMEM((nb, ML_H, 1, LANES), F32)],
        compiler_params=_cparams(("arbitrary", "arbitrary")),
        name="mlstm",
    )(r3(mq), r3(mk), r3(mv), r3(mo), r3(gates), bias_row, norm_row)
    return out.reshape(batch * seq, wide)


def _swa_kernel(q_ref, kc_ref, kp_ref, vc_ref, vp_ref, sink_ref, o_ref):
    w = WINDOW
    n = pl.program_id(1)
    scale = SWA_D ** -0.5
    qi = _iota2((w, w), 0)
    kj = _iota2((w, w), 1)
    mask_c = kj <= qi
    mask_p = jnp.logical_and(kj > qi, n > 0)
    grp = SWA_H // SWA_KV
    neg = -1e30
    units = [(bb, h) for bb in range(q_ref.shape[0]) for h in range(SWA_H)]
    scores = []
    half_of_lane = _iota2((w, LANES), 1) // SWA_D
    for bb, h in units:
        g = h // grp
        pair = q_ref[bb, :, (h // 2) * LANES:(h // 2 + 1) * LANES]
        q = jnp.where(half_of_lane == h % 2, pair, jnp.zeros_like(pair))
        scores.append((_dot_nt(q, kc_ref[bb, :, g * LANES:(g + 1) * LANES]),
                       _dot_nt(q, kp_ref[bb, :, g * LANES:(g + 1) * LANES])))
    masked, tops, exps, dens, probs = [], [], [], [], {}
    for sc, sp in scores:
        masked.append((jnp.where(mask_c, sc * scale, neg), jnp.where(mask_p, sp * scale, neg)))
    for (bb, h), (s_c, s_p) in zip(units, masked):
        tops.append(jnp.maximum(jnp.max(jnp.maximum(s_c, s_p), axis=-1, keepdims=True), sink_ref[:, h:h + 1]))
    for (s_c, s_p), m in zip(masked, tops):
        exps.append((jnp.where(mask_c, jnp.exp(s_c - m), 0.0), jnp.where(mask_p, jnp.exp(s_p - m), 0.0)))
    ones_b = jnp.ones((w, LANES), BF16)
    for (bb, h), (p_c, p_p), m in zip(units, exps, tops):
        p_c, p_p = p_c.astype(BF16), p_p.astype(BF16)
        probs[bb, h] = (p_c, p_p)
        dens.append(_dot(p_c, ones_b) + _dot(p_p, ones_b) + jnp.exp(sink_ref[:, h:h + 1] - m))
    inv = {u: 1.0 / den for u, den in zip(units, dens)}
    for bb in range(q_ref.shape[0]):
        for pair in range(SWA_H // 2):
            acc = None
            for sub in range(2):
                h = 2 * pair + sub
                vcol = (2 * (h // grp) + sub) * LANES
                p_c, p_p = probs[bb, h]
                part = (_dot(p_c, vc_ref[bb, :, vcol:vcol + LANES]) + _dot(p_p, vp_ref[bb, :, vcol:vcol + LANES])) * inv[bb, h]
                acc = part if acc is None else acc + part
            o_ref[bb, :, pair * LANES:(pair + 1) * LANES] = acc.astype(o_ref.dtype)


def _swa(sq, sk, sv, sinks_row, batch, seq):
    w = WINDOW
    nb = seq // w
    ns = min(SWA_SEQS_PER_STEP, batch)
    wo = SWA_H * SWA_D
    cur = lambda b, n: (b, n, 0)
    prev = lambda b, n: (b, jnp.maximum(n - 1, 0), 0)
    r3 = lambda a: a.reshape(batch, seq, a.shape[-1])
    q3, k3, v3 = r3(sq), r3(sk), r3(sv)
    out = pl.pallas_call(
        _swa_kernel,
        grid=(batch // ns, nb),
        in_specs=[pl.BlockSpec((ns, w, sq.shape[1]), cur),
                  pl.BlockSpec((ns, w, sk.shape[1]), cur), pl.BlockSpec((ns, w, sk.shape[1]), prev),
                  pl.BlockSpec((ns, w, sv.shape[1]), cur), pl.BlockSpec((ns, w, sv.shape[1]), prev),
                  pl.BlockSpec((1, LANES), lambda b, n: (0, 0))],
        out_specs=pl.BlockSpec((ns, w, wo), cur),
        out_shape=jax.ShapeDtypeStruct((batch, seq, wo), BF16),
        compiler_params=_cparams(("arbitrary", "arbitrary")),
        name="swa",
    )(q3, k3, k3, v3, v3, sinks_row)
    return out.reshape(batch * seq, wo)


def _layer_norm(h, g, b):
    mu = jnp.mean(h, axis=-1, keepdims=True)
    d = h - mu
    var = jnp.mean(d * d, axis=-1, keepdims=True)
    return d * lax.rsqrt(var + LN_EPS) * g + b


def _outproj_kernel(x_ref, a1_ref, a2_ref, w_ref, g_ref, b_ref, o_ref, op_ref):
    k1 = a1_ref.shape[1]
    y = _dot(a1_ref[...].astype(BF16), w_ref[0:k1, :]) + _dot(a2_ref[...].astype(BF16), w_ref[k1:, :])
    h = _layer_norm(DN_ALPHA * x_ref[...] + y, g_ref[...], b_ref[...])
    o_ref[...] = h
    op_ref[...] = _pack_pairs(h)


def _outproj_ln(x, a1, a2, w, g, b, tm=1024):
    t, d = x.shape
    tm = min(tm, t)
    row = lambda i: (i, 0)
    fix = lambda i: (0, 0)
    return pl.pallas_call(
        _outproj_kernel,
        grid=(t // tm,),
        in_specs=[pl.BlockSpec((tm, d), row), pl.BlockSpec((tm, a1.shape[1]), row), pl.BlockSpec((tm, a2.shape[1]), row),
                  pl.BlockSpec(w.shape, fix), pl.BlockSpec((1, d), fix), pl.BlockSpec((1, d), fix)],
        out_specs=[pl.BlockSpec((tm, d), row), pl.BlockSpec((tm, d // 2), row)],
        out_shape=[jax.ShapeDtypeStruct((t, d), F32), jax.ShapeDtypeStruct((t, d // 2), jnp.uint32)],
        compiler_params=_cparams(("arbitrary",)),
        name="outproj_ln",
    )(x, a1, a2, w, g, b)


def _first_index(x, m, iota_f, sentinel):
    return jnp.min(jnp.where(x == m, iota_f, sentinel), axis=0, keepdims=True)


def _router_kernel(x_ref, wt_ref, bias_ref, idx_ref, gate_ref, rank_ref, cnt_ref, carry_ref):
    tm = x_ref.shape[0]
    e = N_EXPERTS
    gs = e // N_GROUPS
    ninf = -jnp.inf

    @pl.when(pl.program_id(0) == 0)
    def _():
        carry_ref[...] = jnp.zeros(carry_ref.shape, F32)

    logits = _dot3(wt_ref[...], x_ref[...], _dot_nt)
    scores = _sigmoid(logits)
    sel = scores + bias_ref[:, 0:1]

    sub_f = _iota2((gs, tm), 0).astype(F32)
    gscore = []
    for g in range(N_GROUPS):
        blk = sel[g * gs:(g + 1) * gs, :]
        m1 = jnp.max(blk, axis=0, keepdims=True)
        i1 = _first_index(blk, m1, sub_f, float(gs))
        m2 = jnp.max(jnp.where(sub_f == i1, ninf, blk), axis=0, keepdims=True)
        gscore.append(m1 + m2)
    gsc = jnp.concatenate(gscore, axis=0)
    grp_f = _iota2((N_GROUPS, tm), 0).astype(F32)
    gmask = jnp.zeros((N_GROUPS, tm), F32)
    for _ in range(TOPK_GROUPS):
        m = jnp.max(gsc, axis=0, keepdims=True)
        gi = _first_index(gsc, m, grp_f, float(N_GROUPS))
        hit = grp_f == gi
        gmask = jnp.where(hit, 1.0, gmask)
        gsc = jnp.where(hit, ninf, gsc)
    masked = jnp.concatenate(
        [jnp.where(gmask[g:g + 1, :] > 0.0, sel[g * gs:(g + 1) * gs, :], ninf) for g in range(N_GROUPS)], axis=0)

    exp_f = _iota2((e, tm), 0).astype(F32)
    chosen = jnp.zeros((e, tm), F32)
    idxs, gates = [], []
    for _ in range(TOP_K):
        m = jnp.max(masked, axis=0, keepdims=True)
        ei = _first_index(masked, m, exp_f, float(e))
        hit = exp_f == ei
        idxs.append(ei)
        gates.append(jnp.sum(jnp.where(hit, scores, 0.0), axis=0, keepdims=True))
        chosen = jnp.where(hit, 1.0, chosen)
        masked = jnp.where(hit, ninf, masked)
    gate = jnp.concatenate(gates, axis=0)
    gate = gate / jnp.sum(gate, axis=0, keepdims=True) * ROUTED_SCALE
    idx_f = jnp.concatenate(idxs, axis=0)

    upper = (_iota2((tm, tm), 0) < _iota2((tm, tm), 1)).astype(BF16)
    before = _dot(chosen.astype(BF16), upper) + carry_ref[...][:, 0:1]
    ranks = [jnp.sum(jnp.where(exp_f == idxs[k], before, 0.0), axis=0, keepdims=True) for k in range(TOP_K)]
    carry_ref[...] = carry_ref[...] + jnp.sum(chosen, axis=1, keepdims=True)

    idx_ref[...] = idx_f.astype(jnp.int32)
    gate_ref[...] = gate
    rank_ref[...] = jnp.concatenate(ranks, axis=0).astype(jnp.int32)
    cnt_ref[...] = carry_ref[...]


def _router(x, wt, bias_col, tm=512):
    t, d = x.shape
    col = lambda i: (0, i)
    fix = lambda i: (0, 0)
    return pl.pallas_call(
        _router_kernel,
        grid=(t // tm,),
        in_specs=[pl.BlockSpec((tm, d), lambda i: (i, 0)), pl.BlockSpec(wt.shape, fix), pl.BlockSpec((N_EXPERTS, LANES), fix)],
        out_specs=[pl.BlockSpec((TOP_K, tm), col), pl.BlockSpec((TOP_K, tm), col), pl.BlockSpec((TOP_K, tm), col),
                   pl.BlockSpec((N_EXPERTS, LANES), fix)],
        out_shape=[jax.ShapeDtypeStruct((TOP_K, t), jnp.int32), jax.ShapeDtypeStruct((TOP_K, t), F32),
                   jax.ShapeDtypeStruct((TOP_K, t), jnp.int32), jax.ShapeDtypeStruct((N_EXPERTS, LANES), F32)],
        scratch_shapes=[pltpu.VMEM((N_EXPERTS, LANES), F32)],
        compiler_params=_cparams(("arbitrary",)),
        name="router",
    )(x, wt, bias_col)


def _dest_kernel(idx_ref, rank_ref, start_ref, dest_ref):
    tm = idx_ref.shape[1]
    exp_i = _iota2((N_EXPERTS, tm), 0)
    start = start_ref[:, 0:1]
    rows = [jnp.sum(jnp.where(exp_i == idx_ref[s:s + 1, :], start, 0.0), axis=0, keepdims=True) for s in range(TOP_K)]
    dest_ref[...] = jnp.concatenate(rows, axis=0).astype(jnp.int32) + rank_ref[...]


def _dest_rows(idx, rank, start_col, tm=2048):
    t = idx.shape[1]
    tm = min(tm, t)
    col = lambda i: (0, i)
    return pl.pallas_call(
        _dest_kernel,
        grid=(t // tm,),
        in_specs=[pl.BlockSpec((TOP_K, tm), col), pl.BlockSpec((TOP_K, tm), col),
                  pl.BlockSpec((N_EXPERTS, LANES), lambda i: (0, 0))],
        out_specs=pl.BlockSpec((TOP_K, tm), col),
        out_shape=jax.ShapeDtypeStruct((TOP_K, t), jnp.int32),
        compiler_params=_cparams(("arbitrary",)),
        name="moe_dest",
    )(idx, rank, start_col)


def _pack_pairs(x):
    n = x.shape[1] // 2
    hi = lax.bitcast_convert_type(x[:, :n].astype(BF16).astype(F32), jnp.uint32)
    lo = lax.bitcast_convert_type(x[:, n:].astype(BF16).astype(F32), jnp.uint32)
    return hi | (lo >> 16)


def _unpack_pairs(w):
    hi = lax.bitcast_convert_type(w & jnp.uint32(0xFFFF0000), F32)
    lo = lax.bitcast_convert_type(w << 16, F32)
    return hi, lo


def _sc_scatter_rows(xp, dest, rows, chunk=LANES):
    t, width = xp.shape
    info = plsc.get_sparse_core_info()
    ncores, nsub = info.num_cores, info.num_subcores
    per_worker = t // (ncores * nsub)
    nchunk = per_worker // chunk
    mesh = plsc.VectorSubcoreMesh(core_axis_name="c", subcore_axis_name="s")

    @functools.partial(
        pl.kernel, mesh=mesh,
        out_type=jax.ShapeDtypeStruct((rows, width), xp.dtype),
        scratch_types=[pltpu.VMEM((TOP_K, chunk), jnp.int32), pltpu.VMEM((chunk, width), xp.dtype), pltpu.SemaphoreType.DMA],
    )
    def scatter(xp_hbm, dest_hbm, out_hbm, idx_v, rows_v, sem):
        base = (lax.axis_index("s") * ncores + lax.axis_index("c")) * per_worker

        @pl.loop(0, nchunk)
        def _(i):
            off = pl.multiple_of(base + i * chunk, chunk)
            pltpu.sync_copy(dest_hbm.at[:, pl.ds(off, chunk)], idx_v)
            pltpu.sync_copy(xp_hbm.at[pl.ds(off, chunk)], rows_v)
            copies = [pltpu.async_copy(rows_v, out_hbm.at[idx_v.at[s]], sem) for s in range(TOP_K)]
            for cp in copies:
                cp.wait()

    return scatter(xp, dest)


def _experts_kernel(be_ref, nu_ref, nv_ref, first_ref, slot_ref, nxt_ref, xs_ref, wg_hbm, wu_hbm, wd_hbm, ys_ref,
                    wgf_ref, wuf_ref, wdf_ref, wgb_ref, wub_ref, wdb_ref, sem, *, layer):
    i = pl.program_id(0)

    def fetch(e, s):
        return [pltpu.make_async_copy(wg_hbm.at[layer, e], wgf_ref.at[s], sem.at[s]),
                pltpu.make_async_copy(wu_hbm.at[layer, e], wuf_ref.at[s], sem.at[s]),
                pltpu.make_async_copy(wd_hbm.at[layer, e], wdf_ref.at[s], sem.at[s])]

    @pl.when(i == 0)
    def _():
        for cp in fetch(be_ref[0], 0):
            cp.start()

    @pl.when(jnp.logical_and(first_ref[i] == 1, i < nu_ref[0]))
    def _():
        s = slot_ref[i]
        for cp in fetch(be_ref[i], s):
            cp.wait()
        wgb_ref[...] = wgf_ref[s].astype(BF16)
        wub_ref[...] = wuf_ref[s].astype(BF16)
        wdb_ref[...] = wdf_ref[s].astype(BF16)

        @pl.when(nxt_ref[i] >= 0)
        def _():
            for cp in fetch(nxt_ref[i], 1 - s):
                cp.start()

    @pl.when(i < nu_ref[0])
    def _():
        sub = xs_ref.shape[0] // EXPERT_SUBBLOCKS
        acts = []
        for r in range(EXPERT_SUBBLOCKS):
            rows = pl.ds(r * sub, sub)
            live = (_iota2((sub, 1), 0) + r * sub) < nv_ref[i]
            xa, xb = _unpack_pairs(jnp.where(live, xs_ref[rows, :], jnp.uint32(0)))
            x = jnp.concatenate([xa.astype(BF16), xb.astype(BF16)], axis=1)
            acts.append((_dot(x, wgb_ref[...]), _dot(x, wub_ref[...])))
        outs = [_dot((_silu(gate) * up).astype(BF16), wdb_ref[...]) for gate, up in acts]
        for r, y in enumerate(outs):
            ys_ref[pl.ds(r * sub, sub), :] = _pack_pairs(y)


def _experts(block_e, n_used, n_valid, xs, wg, wu, wd, layer, block):
    rows, half = xs.shape
    d = 2 * half
    nb = rows // block
    pos = jnp.arange(nb, dtype=jnp.int32)
    first = jnp.concatenate([jnp.ones((1,), jnp.int32), (block_e[1:] != block_e[:-1]).astype(jnp.int32)])
    slot = (jnp.cumsum(first) - 1) % 2
    later = (pos[None, :] > pos[:, None]) & (block_e[None, :] != block_e[:, None]) & (pos[None, :] < n_used[0])
    nxt_pos = jnp.min(jnp.where(later, pos[None, :], nb), axis=1)
    nxt = jnp.where(nxt_pos < nb, block_e[jnp.minimum(nxt_pos, nb - 1)], -1)
    blk = lambda i, be, nu, *rest: (jnp.minimum(i, nu[0] - 1), 0)
    hbm = pl.BlockSpec(memory_space=pl.ANY)
    return pl.pallas_call(
        functools.partial(_experts_kernel, layer=layer),
        grid_spec=pltpu.PrefetchScalarGridSpec(
            num_scalar_prefetch=6,
            grid=(nb,),
            in_specs=[pl.BlockSpec((block, half), blk), hbm, hbm, hbm],
            out_specs=pl.BlockSpec((block, half), blk),
            scratch_shapes=[pltpu.VMEM((2, d, D_EXPERT), F32), pltpu.VMEM((2, d, D_EXPERT), F32),
                            pltpu.VMEM((2, D_EXPERT, d), F32),
                            pltpu.VMEM((d, D_EXPERT), BF16), pltpu.VMEM((d, D_EXPERT), BF16),
                            pltpu.VMEM((D_EXPERT, d), BF16), pltpu.SemaphoreType.DMA((2,))],
        ),
        out_shape=jax.ShapeDtypeStruct((rows, half), jnp.uint32),
        compiler_params=_cparams(("arbitrary",)),
        name="moe_experts",
    )(block_e, n_used, n_valid, first, slot.astype(jnp.int32), nxt.astype(jnp.int32), xs, wg, wu, wd)


def _sc_gather_rows(table, idx, chunk=SC_CHUNK):
    n = idx.shape[0]
    width = table.shape[1]
    info = plsc.get_sparse_core_info()
    ncores, nsub = info.num_cores, info.num_subcores
    per_worker = n // (ncores * nsub)
    nchunk = per_worker // chunk
    mesh = plsc.VectorSubcoreMesh(core_axis_name="c", subcore_axis_name="s")

    @functools.partial(
        pl.kernel, mesh=mesh,
        out_type=jax.ShapeDtypeStruct((n, width), table.dtype),
        scratch_types=[pltpu.VMEM((nchunk, chunk), jnp.int32), pltpu.VMEM((2, chunk, width), table.dtype),
                       pltpu.SemaphoreType.DMA((2,)), pltpu.SemaphoreType.DMA((2,))],
    )
    def gather(table_hbm, idx_hbm, out_hbm, idx_v, rows_v, gsem, wsem):
        wid = lax.axis_index("s") * ncores + lax.axis_index("c")
        base = wid * per_worker
        pltpu.sync_copy(idx_hbm.at[pl.ds(wid * nchunk, nchunk)], idx_v)

        def fetch(j, b):
            return pltpu.make_async_copy(table_hbm.at[idx_v.at[j]], rows_v.at[b], gsem.at[b])

        def flush(j, b):
            off = pl.multiple_of(base + j * chunk, chunk)
            return pltpu.make_async_copy(rows_v.at[b], out_hbm.at[pl.ds(off, chunk)], wsem.at[b])

        fetch(0, 0).start()

        @pl.loop(0, nchunk, step=2)
        def _(i):
            for b in range(2):
                j = i + b
                fetch(j, b).wait()

                @pl.when(j + 1 < nchunk)
                def _():
                    @pl.when(j >= 1)
                    def _():
                        flush(j - 1, 1 - b).wait()

                    fetch(j + 1, 1 - b).start()

                flush(j, b).start()

        flush(nchunk - 2, 0).wait()
        flush(nchunk - 1, 1).wait()

    return gather(table, idx.reshape(n // chunk, chunk))


def _shared_kernel(xp_ref, sg_ref, su_ref, sd_ref, o_ref):
    xa, xb = _unpack_pairs(xp_ref[...])
    x = jnp.concatenate([xa.astype(BF16), xb.astype(BF16)], axis=1)
    hs = _silu(_dot(x, sg_ref[...])) * _dot(x, su_ref[...])
    o_ref[...] = _pack_pairs(_dot(hs.astype(BF16), sd_ref[...]))


def _shared_expert(xp, sg, su, sd, tm=512):
    t, half = xp.shape
    row = lambda i: (i, 0)
    fix = lambda i: (0, 0)
    return pl.pallas_call(
        _shared_kernel,
        grid=(t // tm,),
        in_specs=[pl.BlockSpec((tm, half), row), pl.BlockSpec(sg.shape, fix), pl.BlockSpec(su.shape, fix),
                  pl.BlockSpec(sd.shape, fix)],
        out_specs=pl.BlockSpec((tm, half), row),
        out_shape=jax.ShapeDtypeStruct((t, half), jnp.uint32),
        compiler_params=_cparams(("arbitrary",)),
        name="moe_shared",
    )(xp, sg, su, sd)


def _combine_kernel(x_ref, gate_ref, rows_ref, sh_ref, g_ref, b_ref, o_ref):
    o_ref[...] = _combined(x_ref, gate_ref, rows_ref, sh_ref, g_ref, b_ref)


def _combine(x, gate_t, rows, shared, g, b, tm=512):
    t, d = x.shape
    row = lambda i: (i, 0)
    return pl.pallas_call(
        _combine_kernel,
        grid=(t // tm,),
        in_specs=_stream_specs((x, gate_t, rows, shared, g, b), tm),
        out_specs=pl.BlockSpec((tm, d), row),
        out_shape=jax.ShapeDtypeStruct((t, d), F32),
        compiler_params=_cparams(("arbitrary",)),
        name="moe_combine",
    )(x, gate_t, rows, shared, g, b)


def _take_cols(w, idx):
    idx = np.asarray(idx)
    runs, start = [], 0
    for pos in range(1, len(idx) + 1):
        run_ends = pos == len(idx) or (idx[pos] != idx[pos - 1] + 1 if idx[pos - 1] >= 0 else idx[pos] >= 0)
        if run_ends:
            runs.append((start, int(idx[start]), pos - start))
            start = pos

    def body(w_ref, o_ref):
        for dst, src, width in runs:
            if src < 0:
                o_ref[:, dst:dst + width] = jnp.zeros((o_ref.shape[0], width), o_ref.dtype)
            else:
                o_ref[:, dst:dst + width] = w_ref[:, src:src + width].astype(o_ref.dtype)

    rows = w.shape[0]
    tr = min(rows, 256)
    return pl.pallas_call(
        body,
        grid=(rows // tr,),
        in_specs=[pl.BlockSpec((tr, w.shape[1]), lambda i: (i, 0))],
        out_specs=pl.BlockSpec((tr, len(idx)), lambda i: (i, 0)),
        out_shape=jax.ShapeDtypeStruct((rows, len(idx)), BF16),
        compiler_params=_cparams(("arbitrary",)),
        name="weight_cols",
    )(w)


def _pad_lane_row(v, first_lane, width=LANES):
    out = jnp.zeros((1, width), F32)
    return lax.dynamic_update_slice(out, v.reshape(1, -1).astype(F32), (0, first_lane))


def _even_in_cols():
    z = lambda n: -np.ones(n, int)
    kr0 = Q_LORA + KV_LORA
    cols = [np.arange(0, Q_LORA), np.arange(Q_LORA, Q_LORA + KV_LORA),
            z(MLA_NOPE), np.arange(kr0, kr0 + MLA_ROPE), z(LANES - MLA_NOPE - MLA_ROPE)]
    g0 = kr0 + MLA_ROPE
    nqk = GDN_H * GDN_DK
    cols.append(np.arange(g0, g0 + 3 * nqk))
    zoff = g0 + 3 * nqk + 2 * GDN_H
    cols.append(np.arange(zoff, zoff + GDN_H * GDN_DV))
    cols += [np.arange(g0 + 3 * nqk, g0 + 3 * nqk + 2 * GDN_H), z(LANES - 2 * GDN_H)]
    return np.concatenate(cols)


EV_WIDTHS = (Q_LORA + KV_LORA + LANES, 3 * GDN_H * GDN_DK, GDN_H * GDN_DV, LANES)


def _mla_q_cols():
    per = MLA_NOPE + MLA_ROPE
    half = MLA_ROPE // 2
    main, sw = [], []
    for h in range(MLA_H):
        b = h * per
        main += [np.arange(b, b + per), -np.ones(LANES - per, int)]
        sw += [-np.ones(MLA_NOPE, int), np.arange(b + MLA_NOPE + half, b + per), np.arange(b + MLA_NOPE, b + MLA_NOPE + half),
               -np.ones(LANES - per, int)]
    return np.concatenate(main + sw)


def _mla_kv_cols():
    per = MLA_NOPE + MLA_V
    kc, vc = [], []
    for h in range(MLA_H):
        b = h * per
        kc += [np.arange(b, b + MLA_NOPE), -np.ones(LANES - MLA_NOPE, int)]
        vv = np.arange(b + MLA_NOPE, b + per)
        pad = -np.ones(LANES - MLA_V, int)
        vc += [vv, pad] if h % 2 == 0 else [pad, vv]
    return np.concatenate(kc + vc)


def _odd_in_cols():
    z = lambda n: -np.ones(n, int)
    o = 0
    cols = []
    mq0, mk0 = 0, ML_H * ML_DK
    for base in (mq0, mk0):
        for h in range(ML_H):
            cols += [np.arange(base + h * ML_DK, base + (h + 1) * ML_DK), z(LANES - ML_DK)]
    mv0 = 2 * ML_H * ML_DK
    cols.append(np.arange(mv0, mv0 + ML_H * ML_DV))
    mi0 = mv0 + ML_H * ML_DV
    mo0 = mi0 + 2 * ML_H
    cols.append(np.arange(mo0, mo0 + ML_H * ML_DV))
    cols += [np.arange(mi0, mi0 + 2 * ML_H), z(LANES - 2 * ML_H)]
    sq0 = mo0 + ML_H * ML_DV
    sk0 = sq0 + SWA_H * SWA_D
    sv0 = sk0 + SWA_KV * SWA_D
    half = SWA_D // 2

    cols.append(np.arange(sq0, sq0 + SWA_H * SWA_D))
    for g in range(SWA_KV):
        cols += [np.arange(sk0 + g * SWA_D, sk0 + (g + 1) * SWA_D)] * 2
    for g in range(SWA_KV):
        vv = np.arange(sv0 + g * SWA_D, sv0 + (g + 1) * SWA_D)
        cols += [vv, z(LANES - SWA_D), z(LANES - SWA_D), vv]
    return np.concatenate(cols)


def _even_weights(w_in, w_qb, w_kvb):
    return (_take_cols(w_in, _even_in_cols()), _take_cols(w_qb, _mla_q_cols()), _take_cols(w_kvb, _mla_kv_cols()))


def _even_mixer(stream, tabs, weights, q_norm, kv_norm, conv_w, a_log, dt_bias, o_norm, batch, seq):
    ctab, stab = tabs
    w, wq2, wkv2 = weights
    x, mla_in, act, z, gates = _proj_even(stream, w, conv_w, seq)
    q, k, v = _mla_prep(mla_in, ctab, stab, q_norm.reshape(1, -1), kv_norm.reshape(1, -1), wq2, wkv2)
    o_a = _mla_attn(q, k, v, batch, seq)
    o_b = _gdn(act, gates, z, _pad_lane_row(a_log, GDN_H), _pad_lane_row(dt_bias, GDN_H),
               o_norm.reshape(1, -1), batch, seq)
    return x, o_a, o_b


def _odd_mixer(stream, tabs, w, b_i, b_f, ml_norm, sinks, batch, seq):
    ctab, stab = tabs
    x, mq, mk, mv, mo, mg, sq, sk, sv = _proj_odd(stream, w, ctab, stab)
    bias_row = _pad_lane_row(jnp.concatenate([b_i, b_f]), 0)
    o_c = _mlstm(mq, mk, mv, mo, mg, bias_row, ml_norm.reshape(1, -1), batch, seq)
    o_d = _swa(sq, sk, sv, _pad_lane_row(sinks, 0), batch, seq)
    return x, o_c, o_d


def _moe(x, xp, router_w, router_b, w_gate, w_up, w_down, layer, s_gate, s_up, s_down, ln_g, ln_b):
    t, d = x.shape
    bias_col = jnp.broadcast_to(router_b.reshape(-1, 1).astype(F32), (N_EXPERTS, LANES))
    idx, gate, rank, cnt = _router(x, router_w.T, bias_col)
    counts = cnt[:, 0].astype(jnp.int32)
    block = int(min(max(pl.next_power_of_2(t * TOP_K // N_EXPERTS) // 2, EXPERT_BLOCK_MIN), EXPERT_BLOCK_MAX))
    padded = (counts + block - 1) // block * block
    pad_end = jnp.cumsum(padded)
    pad_start = pad_end - padded
    start_col = jnp.broadcast_to(pad_start.astype(F32).reshape(-1, 1), (N_EXPERTS, LANES))
    dest = _dest_rows(idx, rank, start_col)
    n_blocks = t * TOP_K // block + N_EXPERTS
    rows = n_blocks * block
    block_row = jnp.arange(n_blocks, dtype=jnp.int32) * block
    block_e = jnp.minimum(jnp.sum((pad_end[None, :] <= block_row[:, None]).astype(jnp.int32), axis=1), N_EXPERTS - 1)
    n_used = (pad_end[-1:] // block).astype(jnp.int32)
    live_end = jnp.sum(jnp.where(block_e[:, None] == jnp.arange(N_EXPERTS, dtype=jnp.int32)[None, :],
                                 (pad_start + counts)[None, :], 0), axis=1)
    n_valid = jnp.clip(live_end - block_row, 0, block).astype(jnp.int32)
    xs = _sc_scatter_rows(xp, dest, rows)
    ys = _experts(block_e, n_used, n_valid, xs, w_gate, w_up, w_down, layer, block)
    picked = _sc_gather_rows(ys, dest.reshape(-1)).reshape(TOP_K, t, d // 2)
    shared = _shared_expert(xp, s_gate.astype(BF16), s_up.astype(BF16), s_down.astype(BF16))
    return (x, gate.T, picked, shared, ln_g.reshape(1, -1), ln_b.reshape(1, -1))


def kernel(x, positions, ev_w_in, mla_q_norm, mla_w_qb, mla_kv_norm, mla_w_kvb, gdn_conv, gdn_a_log, gdn_dt_bias, gdn_norm, ev_w_out, od_w_in, mlstm_b_i, mlstm_b_f, mlstm_norm, swa_sinks, od_w_out, ln1_g, ln1_b, router_w, router_b, moe_w_gate, moe_w_up, moe_w_down, shared_w_gate, shared_w_up, shared_w_down, ln2_g, ln2_b):
    batch, seq, d = x.shape
    streams = STREAMS if batch % STREAMS == 0 else 1
    sb = batch // streams
    ts = sb * seq
    hs, tabs_m, tabs_s = [], [], []
    for s in range(streams):
        pos = positions[s * sb:(s + 1) * sb].reshape(ts, 1).astype(F32)
        tm_, ts_ = _rope_tables(pos)
        tabs_m.append(tm_)
        tabs_s.append(ts_)
        hs.append((x[s * sb:(s + 1) * sb].reshape(ts, d),))
    for layer in range(DEPTH):
        j = layer // 2
        if layer % 2 == 0:
            weights = _even_weights(ev_w_in[j], mla_w_qb[j], mla_w_kvb[j])
            w_out = ev_w_out[j].astype(BF16)
        else:
            weights = _take_cols(od_w_in[j], _odd_in_cols())
            w_out = od_w_out[j].astype(BF16)
        for s in range(streams):
            if layer % 2 == 0:
                h, a1, a2 = _even_mixer(hs[s], tabs_m[s], weights, mla_q_norm[j], mla_kv_norm[j], gdn_conv[j], gdn_a_log[j],
                                        gdn_dt_bias[j], gdn_norm[j], sb, seq)
            else:
                h, a1, a2 = _odd_mixer(hs[s], tabs_s[s], weights, mlstm_b_i[j], mlstm_b_f[j], mlstm_norm[j], swa_sinks[j],
                                       sb, seq)
            h, hp = _outproj_ln(h, a1, a2, w_out, ln1_g[layer].reshape(1, -1), ln1_b[layer].reshape(1, -1))
            hs[s] = _moe(h, hp, router_w[layer], router_b[layer], moe_w_gate, moe_w_up, moe_w_down, layer,
                         shared_w_gate[layer], shared_w_up[layer], shared_w_down[layer], ln2_g[layer], ln2_b[layer])
    return jnp.concatenate([_combine(*h).reshape(sb, seq, d) for h in hs], axis=0)
```

```python
import functools
import math

import numpy as np
import jax
import jax.numpy as jnp
from jax import lax
from jax.experimental import pallas as pl
from jax.experimental.pallas import tpu as pltpu
from jax.experimental.pallas import tpu_sc as plsc

F32 = jnp.float32
BF16 = jnp.bfloat16

D_MODEL = 1024
DEPTH = 4
ROPE_THETA = 10000.0
EPS = 1e-6
LN_EPS = 1e-5
MLA_H, MLA_NOPE, MLA_ROPE, MLA_V = 8, 64, 32, 64
Q_LORA, KV_LORA = 256, 128
GDN_H, GDN_DK, GDN_DV, CONV_W, GDN_CHUNK = 4, 128, 128, 4, 64
ML_H, ML_DK, ML_DV, ML_CHUNK = 4, 64, 128, 64
SWA_H, SWA_KV, SWA_D, WINDOW = 8, 2, 64, 128
N_EXPERTS, N_GROUPS, TOPK_GROUPS, TOP_K = 64, 8, 4, 8
D_EXPERT, D_SHARED = 256, 256
ROUTED_SCALE = 2.5
DN_ALPHA = (2 * DEPTH) ** 0.25

LANES = 128
SUBLANES = 8
V7X_VMEM_BYTES = 64 * 1024 * 1024
VMEM_LIMIT = V7X_VMEM_BYTES * 3 // 4

EXPERT_BLOCK_MIN = 256
EXPERT_BLOCK_MAX = 1024
STREAMS = 1
EXPERT_SUBBLOCKS = 4
SWA_SEQS_PER_STEP = 8
MLSTM_SEQS_PER_STEP = 2
GDN_SEQS_PER_STEP = 8
SC_CHUNK = 64


def _cparams(sem, vmem=VMEM_LIMIT):
    return pltpu.CompilerParams(dimension_semantics=sem, vmem_limit_bytes=vmem)


def _dot(a, b):
    return jnp.dot(a, b, preferred_element_type=F32)


def _dot_nt(a, b):
    return lax.dot_general(a, b, (((1,), (1,)), ((), ())), preferred_element_type=F32)


def _dot_tn(a, b):
    return lax.dot_general(a, b, (((0,), (0,)), ((), ())), preferred_element_type=F32)


def _split2(a):
    hi = a.astype(BF16)
    lo = (a - hi.astype(F32)).astype(BF16)
    return hi, lo


def _split3(a):
    p1 = a.astype(BF16)
    r = a - p1.astype(F32)
    p2 = r.astype(BF16)
    p3 = (r - p2.astype(F32)).astype(BF16)
    return p1, p2, p3


def _dot3(a, b, dot=_dot):
    ah, al = _split2(a)
    bh, bl = _split2(b)
    return dot(ah, bh) + (dot(ah, bl) + dot(al, bh))


def _dot_sel(sel, b, dot=_dot):
    sel = sel.astype(BF16)
    p1, p2, p3 = _split3(b)
    return dot(sel, p1) + (dot(sel, p2) + dot(sel, p3))


def _sigmoid(x):
    return 1.0 / (1.0 + jnp.exp(-x))


def _softplus(x):
    return jnp.maximum(x, 0.0) + jnp.log(1.0 + jnp.exp(-jnp.abs(x)))


def _silu(x):
    return x * _sigmoid(x)


def _lane_bcast(x, c):
    return jnp.broadcast_to(x[:, c:c + 1], x.shape)


def _iota2(shape, dim):
    return lax.broadcasted_iota(jnp.int32, shape, dim)


def _rope_kernel(pos_ref, rows_ref, sel_ref, cm_ref, sm_ref, cs_ref, ss_ref):
    ang = pos_ref[...] * rows_ref[0:1, :]
    cos_parts = _split3(jnp.cos(ang))
    sin_parts = _split3(jnp.sin(ang))

    def place(parts, k):
        return _dot(parts[0], sel_ref[k]) + (_dot(parts[1], sel_ref[k]) + _dot(parts[2], sel_ref[k]))

    cm_ref[...] = place(cos_parts, 0) + rows_ref[1:2, :]
    sm_ref[...] = place(sin_parts, 1)
    cs_ref[...] = place(cos_parts, 2)
    ss_ref[...] = place(sin_parts, 3)


def _rope_consts():
    hm, hs = MLA_ROPE // 2, SWA_D // 2
    rows = np.zeros((8, LANES), np.float32)
    rows[0, :hm] = ROPE_THETA ** (-(np.arange(0, MLA_ROPE, 2, dtype=np.float32) / MLA_ROPE))
    rows[0, hm:hm + hs] = ROPE_THETA ** (-(np.arange(0, SWA_D, 2, dtype=np.float32) / SWA_D))
    rows[1, :MLA_NOPE] = 1.0
    sel = np.zeros((4, LANES, LANES), np.float32)
    for j in range(hm):
        sel[0, j, MLA_NOPE + j] = sel[0, j, MLA_NOPE + hm + j] = 1.0
        sel[1, j, MLA_NOPE + j] = -1.0
        sel[1, j, MLA_NOPE + hm + j] = 1.0
    for h in range(LANES // SWA_D):
        for j in range(hs):
            sel[2, hm + j, h * SWA_D + j] = sel[2, hm + j, h * SWA_D + hs + j] = 1.0
            sel[3, hm + j, h * SWA_D + j] = -1.0
            sel[3, hm + j, h * SWA_D + hs + j] = 1.0
    return jnp.asarray(rows), jnp.asarray(sel, BF16)


def _rope_tables(pos, tm=512):
    t = pos.shape[0]
    tm = min(tm, t)
    rows, sel = _rope_consts()
    cm, sm, cs, ss = pl.pallas_call(
        _rope_kernel,
        grid=(t // tm,),
        in_specs=[pl.BlockSpec((tm, 1), lambda i: (i, 0)), pl.BlockSpec((8, LANES), lambda i: (0, 0)),
                  pl.BlockSpec((4, LANES, LANES), lambda i: (0, 0, 0))],
        out_specs=[pl.BlockSpec((tm, LANES), lambda i: (i, 0))] * 4,
        out_shape=[jax.ShapeDtypeStruct((t, LANES), F32)] * 4,
        compiler_params=_cparams(("arbitrary",)),
        name="rope_tables",
    )(pos, rows, sel)
    return (cm, sm), (cs, ss)


N_COMBINE_IN = 6
FUSED_TM = 256


def _combined(x_ref, gate_ref, rows_ref, sh_ref, g_ref, b_ref):
    gate = gate_ref[...]
    ya, yb = _unpack_pairs(sh_ref[...])
    for s in range(TOP_K):
        a, b = _unpack_pairs(rows_ref[s])
        ya = ya + gate[:, s:s + 1] * a
        yb = yb + gate[:, s:s + 1] * b
    ff = jnp.concatenate([ya, yb], axis=1)
    return _layer_norm(DN_ALPHA * x_ref[...] + ff, g_ref[...], b_ref[...])


def _stream_specs(stream, tm):
    d = stream[0].shape[1]
    row = lambda i: (i, 0)
    fix = lambda i: (0, 0)
    specs = [pl.BlockSpec((tm, d), row)]
    if len(stream) > 1:
        specs += [pl.BlockSpec((tm, TOP_K), row), pl.BlockSpec((TOP_K, tm, d // 2), lambda i: (0, i, 0)),
                  pl.BlockSpec((tm, d // 2), row), pl.BlockSpec((1, d), fix), pl.BlockSpec((1, d), fix)]
    return specs


def _stream_tile(stream_refs, h_ref):
    if h_ref is None:
        return stream_refs[0][...].astype(BF16)
    h = _combined(*stream_refs)
    h_ref[...] = h
    return h.astype(BF16)


def _proj_even_kernel(*refs, tiles_per_seq, fused):
    n_in = N_COMBINE_IN if fused else 1
    stream_refs, (w_ref, cw_ref, mla_ref, act_ref, z_ref, g_ref) = refs[:n_in], refs[n_in:n_in + 6]
    h_ref = refs[n_in + 6] if fused else None
    ext_ref = refs[-1]
    tm = mla_ref.shape[0]
    o = np.concatenate([[0], np.cumsum(EV_WIDTHS)]).tolist()
    halo = SUBLANES
    tap0 = halo - (CONV_W - 1)

    @pl.when(pl.program_id(0) % tiles_per_seq == 0)
    def _():
        ext_ref[0:halo, :] = jnp.zeros((halo, ext_ref.shape[1]), F32)

    xb = _stream_tile(stream_refs, h_ref)
    nchunk = 3
    cw = EV_WIDTHS[1] // nchunk

    def project(ci):
        ext_ref[halo:halo + tm, ci * cw:(ci + 1) * cw] = _dot(xb, w_ref[:, o[1] + ci * cw:o[1] + (ci + 1) * cw])

    project(0)
    for ci in range(nchunk):
        if ci + 1 < nchunk:
            project(ci + 1)
        else:
            mla_ref[...] = _dot(xb, w_ref[:, o[0]:o[1]])
            z_ref[...] = _dot(xb, w_ref[:, o[2]:o[3]]).astype(z_ref.dtype)
            g_ref[...] = _dot(xb, w_ref[:, o[3]:o[4]])
        cols = slice(ci * cw, (ci + 1) * cw)
        conv = cw_ref[0:1, cols] * ext_ref[tap0:tap0 + tm, cols]
        for j in range(1, CONV_W):
            conv = conv + cw_ref[j:j + 1, cols] * ext_ref[tap0 + j:tap0 + j + tm, cols]
        act_ref[:, cols] = _silu(conv).astype(act_ref.dtype)
    ext_ref[0:halo, :] = ext_ref[tm:tm + halo, :]


def _proj_even(stream, w, conv_w, seq, tm=512):
    t, k = stream[0].shape
    fused = len(stream) > 1
    tm = min(FUSED_TM if fused else tm, seq)
    row = lambda i: (i, 0)
    fix = lambda i: (0, 0)
    widths = EV_WIDTHS + ((k,) if fused else ())
    outs = pl.pallas_call(
        functools.partial(_proj_even_kernel, tiles_per_seq=seq // tm, fused=fused),
        grid=(t // tm,),
        in_specs=_stream_specs(stream, tm) + [pl.BlockSpec(w.shape, fix), pl.BlockSpec(conv_w.shape, fix)],
        out_specs=[pl.BlockSpec((tm, n), row) for n in widths],
        out_shape=[jax.ShapeDtypeStruct((t, n), F32) for n in widths],
        scratch_shapes=[pltpu.VMEM((tm + SUBLANES, EV_WIDTHS[1]), F32)],
        compiler_params=_cparams(("arbitrary",)),
        name="combine_in_proj" if fused else "in_proj",
    )(*stream, w, conv_w)
    return (outs[-1] if fused else stream[0],) + tuple(outs[:len(EV_WIDTHS)])


OD_SEG = dict(mq=(0, 512), mk=(512, 1024), mv=(1024, 1536), mo=(1536, 2048), gates=(2048, 2176),
              sq=(2176, 2688), sk=(2688, 2944), sv=(2944, 3456))
OD_COLS = 3456


def _proj_odd_kernel(*refs, fused):
    n_in = N_COMBINE_IN if fused else 1
    stream_refs = refs[:n_in]
    w_ref, c_ref, s_ref, mq_ref, mk_ref, mv_ref, mo_ref, mg_ref, sq_ref, sk_ref, sv_ref = refs[n_in:n_in + 11]
    xb = _stream_tile(stream_refs, refs[n_in + 11] if fused else None)

    def seg(name):
        a, b = OD_SEG[name]
        return _dot(xb, w_ref[:, a:b])

    mq_ref[...] = seg("mq").astype(mq_ref.dtype)
    mk_ref[...] = seg("mk").astype(mk_ref.dtype)
    mv_ref[...] = seg("mv").astype(mv_ref.dtype)
    mo_ref[...] = seg("mo").astype(mo_ref.dtype)
    mg_ref[...] = seg("gates")
    c = c_ref[...]
    s = s_ref[...]
    def swap_halves(t):
        half = SWA_D // 2
        first_half = (_iota2(t.shape, 1) % SWA_D) < half
        return jnp.where(first_half, pltpu.roll(t, t.shape[1] - half, 1), pltpu.roll(t, half, 1))

    c8 = jnp.concatenate([c] * (SWA_H // 2), axis=1)
    s8 = jnp.concatenate([s] * (SWA_H // 2), axis=1)
    q = seg("sq")
    sq_ref[...] = (q * c8 + swap_halves(q) * s8).astype(sq_ref.dtype)
    c2 = jnp.concatenate([c] * SWA_KV, axis=1)
    s2 = jnp.concatenate([s] * SWA_KV, axis=1)
    k = seg("sk")
    sk_ref[...] = (k * c2 + swap_halves(k) * s2).astype(sk_ref.dtype)
    sv_ref[...] = seg("sv").astype(sv_ref.dtype)


def _proj_odd(stream, w, ctab, stab, tm=512):
    t, k = stream[0].shape
    fused = len(stream) > 1
    tm = min(FUSED_TM if fused else tm, t)
    widths = (512, 512, 512, 512, 128, SWA_H * SWA_D, SWA_KV * LANES, 2 * SWA_KV * LANES)
    dtypes = (F32, F32, F32, F32, F32, BF16, BF16, BF16)
    n_out = len(widths)
    if fused:
        widths, dtypes = widths + (k,), dtypes + (F32,)
    outs = pl.pallas_call(
        functools.partial(_proj_odd_kernel, fused=fused),
        grid=(t // tm,),
        in_specs=_stream_specs(stream, tm) + [pl.BlockSpec(w.shape, lambda i: (0, 0)),
                                              pl.BlockSpec((tm, LANES), lambda i: (i, 0)),
                                              pl.BlockSpec((tm, LANES), lambda i: (i, 0))],
        out_specs=[pl.BlockSpec((tm, n), lambda i: (i, 0)) for n in widths],
        out_shape=[jax.ShapeDtypeStruct((t, n), dt) for n, dt in zip(widths, dtypes)],
        compiler_params=_cparams(("arbitrary",)),
        name="combine_in_proj_odd" if fused else "in_proj_odd",
    )(*stream, w, ctab, stab)
    return (outs[-1] if fused else stream[0],) + tuple(outs[:n_out])


def _rms(x, g):
    return x * lax.rsqrt(jnp.mean(x * x, axis=-1, keepdims=True) + EPS) * g


def _mla_prep_kernel(in_ref, c_ref, s_ref, qn_ref, kvn_ref, wq_ref, wkv_ref, q_ref, k_ref, v_ref):
    hw = MLA_H * LANES
    c = c_ref[...]
    s = s_ref[...]
    c8 = jnp.concatenate([c] * MLA_H, axis=1)
    s8 = jnp.concatenate([s] * MLA_H, axis=1)
    def swap_halves(t):
        half = MLA_ROPE // 2
        first_half = (_iota2(t.shape, 1) % LANES) < MLA_NOPE + half
        return jnp.where(first_half, pltpu.roll(t, t.shape[1] - half, 1), pltpu.roll(t, half, 1))

    cqn = _rms(in_ref[:, 0:Q_LORA], qn_ref[...]).astype(BF16)
    qq = _dot(cqn, wq_ref[...])
    scale = (MLA_NOPE + MLA_ROPE) ** -0.5
    q_ref[...] = ((qq[:, :hw] * c8 + qq[:, hw:] * s8) * scale).astype(q_ref.dtype)
    ckvn = _rms(in_ref[:, Q_LORA:Q_LORA + KV_LORA], kvn_ref[...]).astype(BF16)
    kv = _dot(ckvn, wkv_ref[...])
    o = Q_LORA + KV_LORA
    kr = in_ref[:, o:o + LANES]
    krr = kr * c + swap_halves(kr) * s
    k_ref[...] = (kv[:, :hw] + jnp.concatenate([krr] * MLA_H, axis=1)).astype(k_ref.dtype)
    v_ref[...] = kv[:, hw:].astype(v_ref.dtype)


def _mla_prep(mla_in, ctab, stab, qn, kvn, wq2, wkv2, tm=1024):
    t = mla_in.shape[0]
    tm = min(tm, t)
    hw = MLA_H * LANES
    row = lambda i: (i, 0)
    fix = lambda i: (0, 0)
    return pl.pallas_call(
        _mla_prep_kernel,
        grid=(t // tm,),
        in_specs=[pl.BlockSpec((tm, mla_in.shape[1]), row), pl.BlockSpec((tm, LANES), row), pl.BlockSpec((tm, LANES), row),
                  pl.BlockSpec(qn.shape, fix), pl.BlockSpec(kvn.shape, fix),
                  pl.BlockSpec(wq2.shape, fix), pl.BlockSpec(wkv2.shape, fix)],
        out_specs=[pl.BlockSpec((tm, hw), row)] * 3,
        out_shape=[jax.ShapeDtypeStruct((t, hw), BF16)] * 3,
        compiler_params=_cparams(("arbitrary",)),
        name="mla_prep",
    )(mla_in, ctab, stab, qn, kvn, wq2, wkv2)


def _mla_attn_kernel(q_ref, k_ref, v_ref, o_ref, *, tq):
    i = pl.program_id(2)
    neg = -1e30
    lane = _iota2((tq, LANES), 1)
    ones_lane = (MLA_V, 0)

    def chunk(j, carry, masked):
        start = pl.multiple_of(j * tq, tq)
        out = []
        for hh in range(2):
            m, acc = carry[hh]
            q = q_ref[:, hh * LANES:(hh + 1) * LANES]
            kc = k_ref[pl.ds(start, tq), hh * LANES:(hh + 1) * LANES]
            vc = v_ref[pl.ds(start, tq), hh * LANES:(hh + 1) * LANES]
            vc = jnp.where(lane == ones_lane[hh], jnp.ones_like(vc), vc)
            s = _dot_nt(q, kc)
            if masked:
                s = jnp.where(_iota2(s.shape, 0) >= _iota2(s.shape, 1), s, neg)
            m_new = jnp.maximum(m, jnp.max(s, axis=-1, keepdims=True))
            alpha = jnp.exp(m - m_new)
            p = jnp.exp((s - m_new).astype(BF16))
            acc = alpha * acc + _dot(p, vc)
            out.append((m_new, acc))
        return tuple(out)

    one = (jnp.full((tq, 1), neg, F32), jnp.zeros((tq, LANES), F32))
    carry = lax.fori_loop(0, i, lambda j, c: chunk(j, c, False), (one, one))
    (_, acc0), (_, acc1) = chunk(i, carry, True)
    o0 = acc0 / _lane_bcast(acc0, ones_lane[0])
    o1 = acc1 / _lane_bcast(acc1, ones_lane[1])
    o_ref[...] = jnp.where(lane < MLA_V, o0, o1).astype(o_ref.dtype)


def _mla_attn(q, k, v, batch, seq, tq=512):
    tq = min(tq, seq)
    nq = seq // tq
    pairs = MLA_H // 2
    return pl.pallas_call(
        functools.partial(_mla_attn_kernel, tq=tq),
        grid=(batch, pairs, nq),
        in_specs=[pl.BlockSpec((tq, 2 * LANES), lambda b, p, i: (b * nq + i, p)),
                  pl.BlockSpec((seq, 2 * LANES), lambda b, p, i: (b, p)),
                  pl.BlockSpec((seq, 2 * LANES), lambda b, p, i: (b, p))],
        out_specs=pl.BlockSpec((tq, LANES), lambda b, p, i: (b * nq + i, p)),
        out_shape=jax.ShapeDtypeStruct((batch * seq, pairs * LANES), BF16),
        compiler_params=_cparams(("arbitrary", "arbitrary", "arbitrary")),
        name="mla_attn",
    )(q, k, v)


def _unit_lower_inverse_many(ns):
    c = ns[0].shape[0]
    eye = (_iota2((c, c), 0) == _iota2((c, c), 1)).astype(F32)
    xs = [-n for n in ns]
    ps = [eye + x for x in xs]
    xb = [x.astype(BF16) for x in xs]
    for _ in range(int(math.log2(c)) - 1):
        xs = [_dot(b, b) for b in xb]
        xb = [x.astype(BF16) for x in xs]
        ps = [p + _dot(p.astype(BF16), b) for p, b in zip(ps, xb)]
    return ps


def _gdn_kernel(act_ref, g_ref, z_ref, al_ref, dt_ref, on_ref, o_ref, st_ref):
    c = GDN_CHUNK
    hd = GDN_DK
    nqk = GDN_H * GDN_DK

    @pl.when(pl.program_id(1) == 0)
    def _():
        st_ref[...] = jnp.zeros(st_ref.shape, F32)

    tri = (_iota2((c, c), 0) >= _iota2((c, c), 1)).astype(F32)
    row_ge = _iota2((c, c), 0) >= _iota2((c, c), 1)
    row_gt = _iota2((c, c), 0) > _iota2((c, c), 1)
    lane = _iota2((c, LANES), 1)

    seqs = []
    for bb in range(act_ref.shape[0]):
        gates = g_ref[bb]
        g_all = -jnp.exp(al_ref[...]) * _softplus(gates + dt_ref[...])
        gc_all = _dot_sel(tri, g_all)
        seqs.append(dict(beta_all=_sigmoid(gates), gc_all=gc_all, gc_parts=_split3(gc_all)))
    units = []
    for bb, sq in enumerate(seqs):
        for h in range(GDN_H):
            q = act_ref[bb, :, h * hd:(h + 1) * hd].astype(F32)
            k = act_ref[bb, :, nqk + h * hd:nqk + (h + 1) * hd].astype(F32)
            v = act_ref[bb, :, 2 * nqk + h * GDN_DV:2 * nqk + (h + 1) * GDN_DV].astype(F32)
            q = q * lax.rsqrt(jnp.sum(q * q, axis=-1, keepdims=True) + EPS) * (GDN_DK ** -0.5)
            k = k * lax.rsqrt(jnp.sum(k * k, axis=-1, keepdims=True) + EPS)
            beta = _lane_bcast(sq["beta_all"], h)
            gcol = _lane_bcast(sq["gc_all"], GDN_H + h)
            units.append(dict(bb=bb, h=h, q=q, k=k, v=v, beta=beta, gcol=gcol, kb=k * beta, parts=sq["gc_parts"]))
    for u in units:
        pick = (lane == GDN_H + u["h"]).astype(BF16)
        p0, p1, p2 = u["parts"]
        u["grow"] = _dot_nt(pick, p0) + (_dot_nt(pick, p1) + _dot_nt(pick, p2))
        u["kk"] = _dot3(u["kb"], u["k"], _dot_nt)
        u["qk"] = _dot_nt(u["q"].astype(BF16), u["k"].astype(BF16))
    for u in units:
        gcol = u["gcol"]
        decay = jnp.exp(jnp.where(row_ge, gcol[:, :c] - u["grow"], -jnp.inf))
        eg = jnp.exp(gcol)
        glast = gcol[c - 1:c, :]
        u["lower"] = jnp.where(row_gt, u["kk"] * decay, 0.0)
        u["rhs"] = jnp.concatenate([u["v"] * u["beta"], u["kb"] * eg], axis=1)
        u["attn"] = u["qk"] * decay
        u["qg"] = (u["q"] * eg).astype(BF16)
        u["kg"] = (u["k"] * jnp.exp(glast - gcol)).astype(BF16)
        u["gl"] = jnp.exp(glast)

    tinvs = _unit_lower_inverse_many([u["lower"] for u in units])
    uws = []
    for u, tinv in zip(units, tinvs):
        uws.append(_dot(tinv.astype(BF16), u["rhs"].astype(BF16)))
    states = [st_ref[u["bb"], u["h"]] for u in units]
    sbs = [s.astype(BF16) for s in states]
    vnews = [(uw[:, :GDN_DV] - _dot(uw[:, GDN_DV:].astype(BF16), sb)).astype(BF16) for uw, sb in zip(uws, sbs)]
    for u, state, sb, vnb in zip(units, states, sbs, vnews):
        bb, h = u["bb"], u["h"]
        o = _dot(u["qg"], sb) + _dot(u["attn"].astype(BF16), vnb)
        st_ref[bb, h] = state * u["gl"] + _dot_tn(u["kg"], vnb)
        o = _rms(o, on_ref[...]) * _silu(z_ref[bb, :, h * GDN_DV:(h + 1) * GDN_DV].astype(F32))
        o_ref[bb, :, h * GDN_DV:(h + 1) * GDN_DV] = o.astype(o_ref.dtype)


def _gdn(act, gates, z, a_row, dt_row, o_norm, batch, seq):
    c = GDN_CHUNK
    nc = seq // c
    w3 = act.shape[1]
    wo = GDN_H * GDN_DV
    nb = min(GDN_SEQS_PER_STEP, batch)
    row = lambda b, i: (b, i, 0)
    fix = lambda b, i: (0, 0)
    out = pl.pallas_call(
        _gdn_kernel,
        grid=(batch // nb, nc),
        in_specs=[pl.BlockSpec((nb, c, w3), row), pl.BlockSpec((nb, c, LANES), row), pl.BlockSpec((nb, c, wo), row),
                  pl.BlockSpec((1, LANES), fix), pl.BlockSpec((1, LANES), fix), pl.BlockSpec((1, GDN_DV), fix)],
        out_specs=pl.BlockSpec((nb, c, wo), row),
        out_shape=jax.ShapeDtypeStruct((batch, seq, wo), BF16),
        scratch_shapes=[pltpu.VMEM((nb, GDN_H, GDN_DK, GDN_DV), F32)],
        compiler_params=_cparams(("arbitrary", "arbitrary")),
        name="gdn",
    )(act.reshape(batch, seq, w3), gates.reshape(batch, seq, LANES), z.reshape(batch, seq, wo), a_row, dt_row, o_norm)
    return out.reshape(batch * seq, wo)


def _mlstm_kernel(q_ref, k_ref, v_ref, og_ref, g_ref, bias_ref, nrm_ref, o_ref, c_ref, n_ref, m_ref):
    @pl.when(pl.program_id(1) == 0)
    def _():
        c_ref[...] = jnp.zeros(c_ref.shape, F32)
        n_ref[...] = jnp.zeros(n_ref.shape, F32)
        m_ref[...] = jnp.zeros(m_ref.shape, F32)

    c = ML_CHUNK
    tri = (_iota2((c, c), 0) >= _iota2((c, c), 1)).astype(F32)
    row_ge = _iota2((c, c), 0) >= _iota2((c, c), 1)
    ones = jnp.ones((c, LANES), F32)
    lane = _iota2((c, LANES), 1)

    units = []
    for bb in range(q_ref.shape[0]):
        pre = g_ref[bb] + bias_ref[...]
        logf = jnp.minimum(pre, 0.0) - jnp.log(1.0 + jnp.exp(-jnp.abs(pre)))
        bcum_all = _dot_sel(tri, logf)
        for h in range(ML_H):
            q = q_ref[bb, :, h * LANES:(h + 1) * LANES].astype(F32)
            k = k_ref[bb, :, h * LANES:(h + 1) * LANES].astype(F32) * (ML_DK ** -0.5)
            units.append(dict(bb=bb, h=h, q=q, k=k, qb=q.astype(BF16), vb=v_ref[bb, :, h * ML_DV:(h + 1) * ML_DV].astype(BF16),
                              bcol=_lane_bcast(bcum_all, ML_H + h),
                              icol=_lane_bcast(pre, h),
                              col=jnp.where(lane == h, pre, 0.0) - jnp.where(lane == ML_H + h, bcum_all, 0.0),
                              m_st=m_ref[bb, h], cst=c_ref[bb, h], nst=n_ref[bb, h]))
    for u in units:
        u["row"] = _dot_sel(ones, u["col"], _dot_nt)
        u["qk"] = _dot_nt(u["qb"], u["k"].astype(BF16))
        u["qc"] = _dot(u["qb"], u["cst"].astype(BF16))
    for u in units:
        u["d"] = jnp.where(row_ge, u["bcol"][:, :c] + u["row"], -jnp.inf)
        u["inter"] = u["bcol"] + u["m_st"]
        u["m_t"] = jnp.maximum(u["inter"], jnp.max(u["d"], axis=-1, keepdims=True))
        u["b_end"] = u["bcol"][c - 1:c, :]
        u["a"] = u["b_end"] - u["bcol"] + u["icol"]
        u["m_new"] = jnp.maximum(u["b_end"] + u["m_st"], jnp.max(u["a"], axis=0, keepdims=True))
    for u in units:
        u["w_inter"] = jnp.exp(u["inter"] - u["m_t"])
        u["p"] = jnp.exp(u["d"] - u["m_t"][:, :c]) * u["qk"]
        u["keep"] = jnp.exp(u["b_end"] + u["m_st"] - u["m_new"])
        u["ks"] = u["k"] * jnp.exp(u["a"] - u["m_new"])
    for u in units:
        u["pv"] = _dot(u["p"].astype(BF16), u["vb"])
        u["kv"] = _dot_tn(u["ks"].astype(BF16), u["vb"])
    for u in units:
        u["den"] = (u["w_inter"] * jnp.sum(u["q"] * u["nst"], axis=-1, keepdims=True)
                    + jnp.sum(u["p"], axis=-1, keepdims=True))
    for u in units:
        bb, h = u["bb"], u["h"]
        num = u["w_inter"] * u["qc"] + u["pv"]
        hc = num / jnp.maximum(jnp.abs(u["den"]), jnp.exp(-u["m_t"]))
        c_ref[bb, h] = u["cst"] * u["keep"] + u["kv"]
        n_ref[bb, h] = u["nst"] * u["keep"] + jnp.sum(u["ks"], axis=0, keepdims=True)
        m_ref[bb, h] = u["m_new"]
        hn = (_rms(hc, nrm_ref[:, h * ML_DV:(h + 1) * ML_DV])
              * _sigmoid(og_ref[bb, :, h * ML_DV:(h + 1) * ML_DV].astype(F32)))
        o_ref[bb, :, h * ML_DV:(h + 1) * ML_DV] = hn.astype(o_ref.dtype)


def _mlstm(mq, mk, mv, mo, gates, bias_row, norm_row, batch, seq):
    c = ML_CHUNK
    nc = seq // c
    nb = min(MLSTM_SEQS_PER_STEP, batch)
    row = lambda b, i: (b, i, 0)
    fix = lambda b, i: (0, 0)
    wide = ML_H * LANES
    r3 = lambda a: a.reshape(batch, seq, a.shape[-1])
    out = pl.pallas_call(
        _mlstm_kernel,
        grid=(batch // nb, nc),
        in_specs=[pl.BlockSpec((nb, c, wide), row), pl.BlockSpec((nb, c, wide), row), pl.BlockSpec((nb, c, wide), row),
                  pl.BlockSpec((nb, c, wide), row), pl.BlockSpec((nb, c, LANES), row),
                  pl.BlockSpec((1, LANES), fix), pl.BlockSpec((1, wide), fix)],
        out_specs=pl.BlockSpec((nb, c, wide), row),
        out_shape=jax.ShapeDtypeStruct((batch, seq, wide), BF16),
        scratch_shapes=[pltpu.VMEM((nb, ML_H, LANES, ML_DV), F32), pltpu.VMEM((nb, ML_H, 1, LANES), F32),
                        pltpu.VMEM((nb, ML_H, 1, LANES), F32)],
        compiler_params=_cparams(("arbitrary", "arbitrary")),
        name="mlstm",
    )(r3(mq), r3(mk), r3(mv), r3(mo), r3(gates), bias_row, norm_row)
    return out.reshape(batch * seq, wide)


def _swa_kernel(q_ref, kc_ref, kp_ref, vc_ref, vp_ref, sink_ref, o_ref):
    w = WINDOW
    n = pl.program_id(1)
    scale = SWA_D ** -0.5
    qi = _iota2((w, w), 0)
    kj = _iota2((w, w), 1)
    mask_c = kj <= qi
    mask_p = jnp.logical_and(kj > qi, n > 0)
    grp = SWA_H // SWA_KV
    neg = -1e30
    units = [(bb, h) for bb in range(q_ref.shape[0]) for h in range(SWA_H)]
    scores = []
    half_of_lane = _iota2((w, LANES), 1) // SWA_D
    for bb, h in units:
        g = h // grp
        pair = q_ref[bb, :, (h // 2) * LANES:(h // 2 + 1) * LANES]
        q = jnp.where(half_of_lane == h % 2, pair, jnp.zeros_like(pair))
        scores.append((_dot_nt(q, kc_ref[bb, :, g * LANES:(g + 1) * LANES]),
                       _dot_nt(q, kp_ref[bb, :, g * LANES:(g + 1) * LANES])))
    masked, tops, exps, dens, probs = [], [], [], [], {}
    for sc, sp in scores:
        masked.append((jnp.where(mask_c, sc * scale, neg), jnp.where(mask_p, sp * scale, neg)))
    for (bb, h), (s_c, s_p) in zip(units, masked):
        tops.append(jnp.maximum(jnp.max(jnp.maximum(s_c, s_p), axis=-1, keepdims=True), sink_ref[:, h:h + 1]))
    for (s_c, s_p), m in zip(masked, tops):
        exps.append((jnp.where(mask_c, jnp.exp(s_c - m), 0.0), jnp.where(mask_p, jnp.exp(s_p - m), 0.0)))
    ones_b = jnp.ones((w, LANES), BF16)
    for (bb, h), (p_c, p_p), m in zip(units, exps, tops):
        p_c, p_p = p_c.astype(BF16), p_p.astype(BF16)
        probs[bb, h] = (p_c, p_p)
        dens.append(_dot(p_c, ones_b) + _dot(p_p, ones_b) + jnp.exp(sink_ref[:, h:h + 1] - m))
    inv = {u: 1.0 / den for u, den in zip(units, dens)}
    for bb in range(q_ref.shape[0]):
        for pair in range(SWA_H // 2):
            acc = None
            for sub in range(2):
                h = 2 * pair + sub
                vcol = (2 * (h // grp) + sub) * LANES
                p_c, p_p = probs[bb, h]
                part = (_dot(p_c, vc_ref[bb, :, vcol:vcol + LANES]) + _dot(p_p, vp_ref[bb, :, vcol:vcol + LANES])) * inv[bb, h]
                acc = part if acc is None else acc + part
            o_ref[bb, :, pair * LANES:(pair + 1) * LANES] = acc.astype(o_ref.dtype)


def _swa(sq, sk, sv, sinks_row, batch, seq):
    w = WINDOW
    nb = seq // w
    ns = min(SWA_SEQS_PER_STEP, batch)
    wo = SWA_H * SWA_D
    cur = lambda b, n: (b, n, 0)
    prev = lambda b, n: (b, jnp.maximum(n - 1, 0), 0)
    r3 = lambda a: a.reshape(batch, seq, a.shape[-1])
    q3, k3, v3 = r3(sq), r3(sk), r3(sv)
    out = pl.pallas_call(
        _swa_kernel,
        grid=(batch // ns, nb),
        in_specs=[pl.BlockSpec((ns, w, sq.shape[1]), cur),
                  pl.BlockSpec((ns, w, sk.shape[1]), cur), pl.BlockSpec((ns, w, sk.shape[1]), prev),
                  pl.BlockSpec((ns, w, sv.shape[1]), cur), pl.BlockSpec((ns, w, sv.shape[1]), prev),
                  pl.BlockSpec((1, LANES), lambda b, n: (0, 0))],
        out_specs=pl.BlockSpec((ns, w, wo), cur),
        out_shape=jax.ShapeDtypeStruct((batch, seq, wo), BF16),
        compiler_params=_cparams(("arbitrary", "arbitrary")),
        name="swa",
    )(q3, k3, k3, v3, v3, sinks_row)
    return out.reshape(batch * seq, wo)


def _layer_norm(h, g, b):
    mu = jnp.mean(h, axis=-1, keepdims=True)
    d = h - mu
    var = jnp.mean(d * d, axis=-1, keepdims=True)
    return d * lax.rsqrt(var + LN_EPS) * g + b


ROUTE_ROWS = 512


def _outproj_kernel(x_ref, a1_ref, a2_ref, w_ref, g_ref, b_ref, wt_ref, bias_ref,
                    o_ref, op_ref, idx_ref, gate_ref, rank_ref, cnt_ref, carry_ref):
    tm = x_ref.shape[0]
    k1 = a1_ref.shape[1]
    rp = min(ROUTE_ROWS, tm)

    @pl.when(pl.program_id(0) == 0)
    def _():
        carry_ref[...] = jnp.zeros(carry_ref.shape, F32)

    for p in range(tm // rp):
        r = slice(p * rp, (p + 1) * rp)
        y = _dot(a1_ref[r, :].astype(BF16), w_ref[0:k1, :]) + _dot(a2_ref[r, :].astype(BF16), w_ref[k1:, :])
        h = _layer_norm(DN_ALPHA * x_ref[r, :] + y, g_ref[...], b_ref[...])
        o_ref[r, :] = h
        op_ref[r, :] = _pack_pairs(h)
    for p in range(tm // rp):
        r = slice(p * rp, (p + 1) * rp)
        idx, gate, rank = _route_tile(o_ref[r, :], wt_ref, bias_ref, carry_ref)
        idx_ref[:, r] = idx
        gate_ref[:, r] = gate
        rank_ref[:, r] = rank
    cnt_ref[...] = carry_ref[...]


def _outproj_ln_route(x, a1, a2, w, g, b, wt, bias_col, tm=1024):
    t, d = x.shape
    tm = min(tm, t)
    row = lambda i: (i, 0)
    col = lambda i: (0, i)
    fix = lambda i: (0, 0)
    return pl.pallas_call(
        _outproj_kernel,
        grid=(t // tm,),
        in_specs=[pl.BlockSpec((tm, d), row), pl.BlockSpec((tm, a1.shape[1]), row), pl.BlockSpec((tm, a2.shape[1]), row),
                  pl.BlockSpec(w.shape, fix), pl.BlockSpec((1, d), fix), pl.BlockSpec((1, d), fix),
                  pl.BlockSpec(wt.shape, fix), pl.BlockSpec((N_EXPERTS, LANES), fix)],
        out_specs=[pl.BlockSpec((tm, d), row), pl.BlockSpec((tm, d // 2), row),
                   pl.BlockSpec((TOP_K, tm), col), pl.BlockSpec((TOP_K, tm), col), pl.BlockSpec((TOP_K, tm), col),
                   pl.BlockSpec((N_EXPERTS, LANES), fix)],
        out_shape=[jax.ShapeDtypeStruct((t, d), F32), jax.ShapeDtypeStruct((t, d // 2), jnp.uint32),
                   jax.ShapeDtypeStruct((TOP_K, t), jnp.int32), jax.ShapeDtypeStruct((TOP_K, t), F32),
                   jax.ShapeDtypeStruct((TOP_K, t), jnp.int32), jax.ShapeDtypeStruct((N_EXPERTS, LANES), F32)],
        scratch_shapes=[pltpu.VMEM((N_EXPERTS, LANES), F32)],
        compiler_params=_cparams(("arbitrary",)),
        name="outproj_ln_route",
    )(x, a1, a2, w, g, b, wt, bias_col)


def _first_index(x, m, iota_f, sentinel):
    return jnp.min(jnp.where(x == m, iota_f, sentinel), axis=0, keepdims=True)


def _route_tile(x, wt_ref, bias_ref, carry_ref):
    tm = x.shape[0]
    e = N_EXPERTS
    gs = e // N_GROUPS
    ninf = -jnp.inf

    logits = _dot3(wt_ref[...], x, _dot_nt)
    scores = _sigmoid(logits)
    sel = scores + bias_ref[:, 0:1]

    sub_f = _iota2((gs, tm), 0).astype(F32)
    gscore = []
    for g in range(N_GROUPS):
        blk = sel[g * gs:(g + 1) * gs, :]
        m1 = jnp.max(blk, axis=0, keepdims=True)
        i1 = _first_index(blk, m1, sub_f, float(gs))
        m2 = jnp.max(jnp.where(sub_f == i1, ninf, blk), axis=0, keepdims=True)
        gscore.append(m1 + m2)
    gsc = jnp.concatenate(gscore, axis=0)
    grp_f = _iota2((N_GROUPS, tm), 0).astype(F32)
    gmask = jnp.zeros((N_GROUPS, tm), F32)
    for _ in range(TOPK_GROUPS):
        m = jnp.max(gsc, axis=0, keepdims=True)
        gi = _first_index(gsc, m, grp_f, float(N_GROUPS))
        hit = grp_f == gi
        gmask = jnp.where(hit, 1.0, gmask)
        gsc = jnp.where(hit, ninf, gsc)
    masked = jnp.concatenate(
        [jnp.where(gmask[g:g + 1, :] > 0.0, sel[g * gs:(g + 1) * gs, :], ninf) for g in range(N_GROUPS)], axis=0)

    exp_f = _iota2((e, tm), 0).astype(F32)
    chosen = jnp.zeros((e, tm), F32)
    idxs, gates = [], []
    for _ in range(TOP_K):
        m = jnp.max(masked, axis=0, keepdims=True)
        ei = _first_index(masked, m, exp_f, float(e))
        hit = exp_f == ei
        idxs.append(ei)
        gates.append(jnp.sum(jnp.where(hit, scores, 0.0), axis=0, keepdims=True))
        chosen = jnp.where(hit, 1.0, chosen)
        masked = jnp.where(hit, ninf, masked)
    gate = jnp.concatenate(gates, axis=0)
    gate = gate / jnp.sum(gate, axis=0, keepdims=True) * ROUTED_SCALE
    idx_f = jnp.concatenate(idxs, axis=0)

    upper = (_iota2((tm, tm), 0) < _iota2((tm, tm), 1)).astype(BF16)
    before = _dot(chosen.astype(BF16), upper) + carry_ref[...][:, 0:1]
    ranks = [jnp.sum(jnp.where(exp_f == idxs[k], before, 0.0), axis=0, keepdims=True) for k in range(TOP_K)]
    carry_ref[...] = carry_ref[...] + jnp.sum(chosen, axis=1, keepdims=True)
    return idx_f.astype(jnp.int32), gate, jnp.concatenate(ranks, axis=0).astype(jnp.int32)


def _dest_kernel(idx_ref, rank_ref, start_ref, dest_ref):
    tm = idx_ref.shape[1]
    exp_i = _iota2((N_EXPERTS, tm), 0)
    start = start_ref[:, 0:1]
    rows = [jnp.sum(jnp.where(exp_i == idx_ref[s:s + 1, :], start, 0.0), axis=0, keepdims=True) for s in range(TOP_K)]
    dest_ref[...] = jnp.concatenate(rows, axis=0).astype(jnp.int32) + rank_ref[...]


def _dest_rows(idx, rank, start_col, tm=2048):
    t = idx.shape[1]
    tm = min(tm, t)
    col = lambda i: (0, i)
    return pl.pallas_call(
        _dest_kernel,
        grid=(t // tm,),
        in_specs=[pl.BlockSpec((TOP_K, tm), col), pl.BlockSpec((TOP_K, tm), col),
                  pl.BlockSpec((N_EXPERTS, LANES), lambda i: (0, 0))],
        out_specs=pl.BlockSpec((TOP_K, tm), col),
        out_shape=jax.ShapeDtypeStruct((TOP_K, t), jnp.int32),
        compiler_params=_cparams(("arbitrary",)),
        name="moe_dest",
    )(idx, rank, start_col)


def _pack_pairs(x):
    n = x.shape[1] // 2
    hi = lax.bitcast_convert_type(x[:, :n].astype(BF16).astype(F32), jnp.uint32)
    lo = lax.bitcast_convert_type(x[:, n:].astype(BF16).astype(F32), jnp.uint32)
    return hi | (lo >> 16)


def _unpack_pairs(w):
    hi = lax.bitcast_convert_type(w & jnp.uint32(0xFFFF0000), F32)
    lo = lax.bitcast_convert_type(w << 16, F32)
    return hi, lo


def _sc_scatter_rows(xp, dest, rows, chunk=LANES):
    t, width = xp.shape
    info = plsc.get_sparse_core_info()
    ncores, nsub = info.num_cores, info.num_subcores
    per_worker = t // (ncores * nsub)
    nchunk = per_worker // chunk
    mesh = plsc.VectorSubcoreMesh(core_axis_name="c", subcore_axis_name="s")

    @functools.partial(
        pl.kernel, mesh=mesh,
        out_type=jax.ShapeDtypeStruct((rows, width), xp.dtype),
        scratch_types=[pltpu.VMEM((TOP_K, chunk), jnp.int32), pltpu.VMEM((chunk, width), xp.dtype), pltpu.SemaphoreType.DMA],
    )
    def scatter(xp_hbm, dest_hbm, out_hbm, idx_v, rows_v, sem):
        base = (lax.axis_index("s") * ncores + lax.axis_index("c")) * per_worker

        @pl.loop(0, nchunk)
        def _(i):
            off = pl.multiple_of(base + i * chunk, chunk)
            pltpu.sync_copy(dest_hbm.at[:, pl.ds(off, chunk)], idx_v)
            pltpu.sync_copy(xp_hbm.at[pl.ds(off, chunk)], rows_v)
            copies = [pltpu.async_copy(rows_v, out_hbm.at[idx_v.at[s]], sem) for s in range(TOP_K)]
            for cp in copies:
                cp.wait()

    return scatter(xp, dest)


def _experts_kernel(be_ref, nu_ref, nv_ref, first_ref, slot_ref, nxt_ref, xs_ref, wg_hbm, wu_hbm, wd_hbm, ys_ref,
                    wgf_ref, wuf_ref, wdf_ref, wgb_ref, wub_ref, wdb_ref, sem, *, layer):
    i = pl.program_id(0)

    def fetch(e, s):
        return [pltpu.make_async_copy(wg_hbm.at[layer, e], wgf_ref.at[s], sem.at[s]),
                pltpu.make_async_copy(wu_hbm.at[layer, e], wuf_ref.at[s], sem.at[s]),
                pltpu.make_async_copy(wd_hbm.at[layer, e], wdf_ref.at[s], sem.at[s])]

    @pl.when(i == 0)
    def _():
        for cp in fetch(be_ref[0], 0):
            cp.start()

    @pl.when(jnp.logical_and(first_ref[i] == 1, i < nu_ref[0]))
    def _():
        s = slot_ref[i]
        for cp in fetch(be_ref[i], s):
            cp.wait()
        wgb_ref[...] = wgf_ref[s].astype(BF16)
        wub_ref[...] = wuf_ref[s].astype(BF16)
        wdb_ref[...] = wdf_ref[s].astype(BF16)

        @pl.when(nxt_ref[i] >= 0)
        def _():
            for cp in fetch(nxt_ref[i], 1 - s):
                cp.start()

    @pl.when(i < nu_ref[0])
    def _():
        sub = xs_ref.shape[0] // EXPERT_SUBBLOCKS
        acts = []
        for r in range(EXPERT_SUBBLOCKS):
            rows = pl.ds(r * sub, sub)
            live = (_iota2((sub, 1), 0) + r * sub) < nv_ref[i]
            xa, xb = _unpack_pairs(jnp.where(live, xs_ref[rows, :], jnp.uint32(0)))
            x = jnp.concatenate([xa.astype(BF16), xb.astype(BF16)], axis=1)
            acts.append((_dot(x, wgb_ref[...]), _dot(x, wub_ref[...])))
        outs = [_dot((_silu(gate) * up).astype(BF16), wdb_ref[...]) for gate, up in acts]
        for r, y in enumerate(outs):
            ys_ref[pl.ds(r * sub, sub), :] = _pack_pairs(y)


def _experts(block_e, n_used, n_valid, xs, wg, wu, wd, layer, block):
    rows, half = xs.shape
    d = 2 * half
    nb = rows // block
    pos = jnp.arange(nb, dtype=jnp.int32)
    first = jnp.concatenate([jnp.ones((1,), jnp.int32), (block_e[1:] != block_e[:-1]).astype(jnp.int32)])
    slot = (jnp.cumsum(first) - 1) % 2
    later = (pos[None, :] > pos[:, None]) & (block_e[None, :] != block_e[:, None]) & (pos[None, :] < n_used[0])
    nxt_pos = jnp.min(jnp.where(later, pos[None, :], nb), axis=1)
    nxt = jnp.where(nxt_pos < nb, block_e[jnp.minimum(nxt_pos, nb - 1)], -1)
    blk = lambda i, be, nu, *rest: (jnp.minimum(i, nu[0] - 1), 0)
    hbm = pl.BlockSpec(memory_space=pl.ANY)
    return pl.pallas_call(
        functools.partial(_experts_kernel, layer=layer),
        grid_spec=pltpu.PrefetchScalarGridSpec(
            num_scalar_prefetch=6,
            grid=(nb,),
            in_specs=[pl.BlockSpec((block, half), blk), hbm, hbm, hbm],
            out_specs=pl.BlockSpec((block, half), blk),
            scratch_shapes=[pltpu.VMEM((2, d, D_EXPERT), F32), pltpu.VMEM((2, d, D_EXPERT), F32),
                            pltpu.VMEM((2, D_EXPERT, d), F32),
                            pltpu.VMEM((d, D_EXPERT), BF16), pltpu.VMEM((d, D_EXPERT), BF16),
                            pltpu.VMEM((D_EXPERT, d), BF16), pltpu.SemaphoreType.DMA((2,))],
        ),
        out_shape=jax.ShapeDtypeStruct((rows, half), jnp.uint32),
        compiler_params=_cparams(("arbitrary",)),
        name="moe_experts",
    )(block_e, n_used, n_valid, first, slot.astype(jnp.int32), nxt.astype(jnp.int32), xs, wg, wu, wd)


def _sc_gather_rows(table, idx, chunk=SC_CHUNK):
    n = idx.shape[0]
    width = table.shape[1]
    info = plsc.get_sparse_core_info()
    ncores, nsub = info.num_cores, info.num_subcores
    per_worker = n // (ncores * nsub)
    nchunk = per_worker // chunk
    mesh = plsc.VectorSubcoreMesh(core_axis_name="c", subcore_axis_name="s")

    @functools.partial(
        pl.kernel, mesh=mesh,
        out_type=jax.ShapeDtypeStruct((n, width), table.dtype),
        scratch_types=[pltpu.VMEM((nchunk, chunk), jnp.int32), pltpu.VMEM((2, chunk, width), table.dtype),
                       pltpu.SemaphoreType.DMA((2,)), pltpu.SemaphoreType.DMA((2,))],
    )
    def gather(table_hbm, idx_hbm, out_hbm, idx_v, rows_v, gsem, wsem):
        wid = lax.axis_index("s") * ncores + lax.axis_index("c")
        base = wid * per_worker
        pltpu.sync_copy(idx_hbm.at[pl.ds(wid * nchunk, nchunk)], idx_v)

        def fetch(j, b):
            return pltpu.make_async_copy(table_hbm.at[idx_v.at[j]], rows_v.at[b], gsem.at[b])

        def flush(j, b):
            off = pl.multiple_of(base + j * chunk, chunk)
            return pltpu.make_async_copy(rows_v.at[b], out_hbm.at[pl.ds(off, chunk)], wsem.at[b])

        fetch(0, 0).start()

        @pl.loop(0, nchunk, step=2)
        def _(i):
            for b in range(2):
                j = i + b
                fetch(j, b).wait()

                @pl.when(j + 1 < nchunk)
                def _():
                    @pl.when(j >= 1)
                    def _():
                        flush(j - 1, 1 - b).wait()

                    fetch(j + 1, 1 - b).start()

                flush(j, b).start()

        flush(nchunk - 2, 0).wait()
        flush(nchunk - 1, 1).wait()

    return gather(table, idx.reshape(n // chunk, chunk))


def _shared_kernel(xp_ref, sg_ref, su_ref, sd_ref, o_ref):
    xa, xb = _unpack_pairs(xp_ref[...])
    x = jnp.concatenate([xa.astype(BF16), xb.astype(BF16)], axis=1)
    hs = _silu(_dot(x, sg_ref[...])) * _dot(x, su_ref[...])
    o_ref[...] = _pack_pairs(_dot(hs.astype(BF16), sd_ref[...]))


def _shared_expert(xp, sg, su, sd, tm=512):
    t, half = xp.shape
    row = lambda i: (i, 0)
    fix = lambda i: (0, 0)
    return pl.pallas_call(
        _shared_kernel,
        grid=(t // tm,),
        in_specs=[pl.BlockSpec((tm, half), row), pl.BlockSpec(sg.shape, fix), pl.BlockSpec(su.shape, fix),
                  pl.BlockSpec(sd.shape, fix)],
        out_specs=pl.BlockSpec((tm, half), row),
        out_shape=jax.ShapeDtypeStruct((t, half), jnp.uint32),
        compiler_params=_cparams(("arbitrary",)),
        name="moe_shared",
    )(xp, sg, su, sd)


def _combine_kernel(x_ref, gate_ref, rows_ref, sh_ref, g_ref, b_ref, o_ref):
    o_ref[...] = _combined(x_ref, gate_ref, rows_ref, sh_ref, g_ref, b_ref)


def _combine(x, gate_t, rows, shared, g, b, tm=512):
    t, d = x.shape
    row = lambda i: (i, 0)
    return pl.pallas_call(
        _combine_kernel,
        grid=(t // tm,),
        in_specs=_stream_specs((x, gate_t, rows, shared, g, b), tm),
        out_specs=pl.BlockSpec((tm, d), row),
        out_shape=jax.ShapeDtypeStruct((t, d), F32),
        compiler_params=_cparams(("arbitrary",)),
        name="moe_combine",
    )(x, gate_t, rows, shared, g, b)


def _take_cols(w, idx):
    idx = np.asarray(idx)
    runs, start = [], 0
    for pos in range(1, len(idx) + 1):
        run_ends = pos == len(idx) or (idx[pos] != idx[pos - 1] + 1 if idx[pos - 1] >= 0 else idx[pos] >= 0)
        if run_ends:
            runs.append((start, int(idx[start]), pos - start))
            start = pos

    def body(w_ref, o_ref):
        for dst, src, width in runs:
            if src < 0:
                o_ref[:, dst:dst + width] = jnp.zeros((o_ref.shape[0], width), o_ref.dtype)
            else:
                o_ref[:, dst:dst + width] = w_ref[:, src:src + width].astype(o_ref.dtype)

    rows = w.shape[0]
    tr = min(rows, 256)
    return pl.pallas_call(
        body,
        grid=(rows // tr,),
        in_specs=[pl.BlockSpec((tr, w.shape[1]), lambda i: (i, 0))],
        out_specs=pl.BlockSpec((tr, len(idx)), lambda i: (i, 0)),
        out_shape=jax.ShapeDtypeStruct((rows, len(idx)), BF16),
        compiler_params=_cparams(("arbitrary",)),
        name="weight_cols",
    )(w)


def _pad_lane_row(v, first_lane, width=LANES):
    out = jnp.zeros((1, width), F32)
    return lax.dynamic_update_slice(out, v.reshape(1, -1).astype(F32), (0, first_lane))


def _even_in_cols():
    z = lambda n: -np.ones(n, int)
    kr0 = Q_LORA + KV_LORA
    cols = [np.arange(0, Q_LORA), np.arange(Q_LORA, Q_LORA + KV_LORA),
            z(MLA_NOPE), np.arange(kr0, kr0 + MLA_ROPE), z(LANES - MLA_NOPE - MLA_ROPE)]
    g0 = kr0 + MLA_ROPE
    nqk = GDN_H * GDN_DK
    cols.append(np.arange(g0, g0 + 3 * nqk))
    zoff = g0 + 3 * nqk + 2 * GDN_H
    cols.append(np.arange(zoff, zoff + GDN_H * GDN_DV))
    cols += [np.arange(g0 + 3 * nqk, g0 + 3 * nqk + 2 * GDN_H), z(LANES - 2 * GDN_H)]
    return np.concatenate(cols)


EV_WIDTHS = (Q_LORA + KV_LORA + LANES, 3 * GDN_H * GDN_DK, GDN_H * GDN_DV, LANES)


def _mla_q_cols():
    per = MLA_NOPE + MLA_ROPE
    half = MLA_ROPE // 2
    main, sw = [], []
    for h in range(MLA_H):
        b = h * per
        main += [np.arange(b, b + per), -np.ones(LANES - per, int)]
        sw += [-np.ones(MLA_NOPE, int), np.arange(b + MLA_NOPE + half, b + per), np.arange(b + MLA_NOPE, b + MLA_NOPE + half),
               -np.ones(LANES - per, int)]
    return np.concatenate(main + sw)


def _mla_kv_cols():
    per = MLA_NOPE + MLA_V
    kc, vc = [], []
    for h in range(MLA_H):
        b = h * per
        kc += [np.arange(b, b + MLA_NOPE), -np.ones(LANES - MLA_NOPE, int)]
        vv = np.arange(b + MLA_NOPE, b + per)
        pad = -np.ones(LANES - MLA_V, int)
        vc += [vv, pad] if h % 2 == 0 else [pad, vv]
    return np.concatenate(kc + vc)


def _odd_in_cols():
    z = lambda n: -np.ones(n, int)
    o = 0
    cols = []
    mq0, mk0 = 0, ML_H * ML_DK
    for base in (mq0, mk0):
        for h in range(ML_H):
            cols += [np.arange(base + h * ML_DK, base + (h + 1) * ML_DK), z(LANES - ML_DK)]
    mv0 = 2 * ML_H * ML_DK
    cols.append(np.arange(mv0, mv0 + ML_H * ML_DV))
    mi0 = mv0 + ML_H * ML_DV
    mo0 = mi0 + 2 * ML_H
    cols.append(np.arange(mo0, mo0 + ML_H * ML_DV))
    cols += [np.arange(mi0, mi0 + 2 * ML_H), z(LANES - 2 * ML_H)]
    sq0 = mo0 + ML_H * ML_DV
    sk0 = sq0 + SWA_H * SWA_D
    sv0 = sk0 + SWA_KV * SWA_D
    half = SWA_D // 2

    cols.append(np.arange(sq0, sq0 + SWA_H * SWA_D))
    for g in range(SWA_KV):
        cols += [np.arange(sk0 + g * SWA_D, sk0 + (g + 1) * SWA_D)] * 2
    for g in range(SWA_KV):
        vv = np.arange(sv0 + g * SWA_D, sv0 + (g + 1) * SWA_D)
        cols += [vv, z(LANES - SWA_D), z(LANES - SWA_D), vv]
    return np.concatenate(cols)


def _even_weights(w_in, w_qb, w_kvb):
    return (_take_cols(w_in, _even_in_cols()), _take_cols(w_qb, _mla_q_cols()), _take_cols(w_kvb, _mla_kv_cols()))


def _even_mixer(stream, tabs, weights, q_norm, kv_norm, conv_w, a_log, dt_bias, o_norm, batch, seq):
    ctab, stab = tabs
    w, wq2, wkv2 = weights
    x, mla_in, act, z, gates = _proj_even(stream, w, conv_w, seq)
    q, k, v = _mla_prep(mla_in, ctab, stab, q_norm.reshape(1, -1), kv_norm.reshape(1, -1), wq2, wkv2)
    o_a = _mla_attn(q, k, v, batch, seq)
    o_b = _gdn(act, gates, z, _pad_lane_row(a_log, GDN_H), _pad_lane_row(dt_bias, GDN_H),
               o_norm.reshape(1, -1), batch, seq)
    return x, o_a, o_b


def _odd_mixer(stream, tabs, w, b_i, b_f, ml_norm, sinks, batch, seq):
    ctab, stab = tabs
    x, mq, mk, mv, mo, mg, sq, sk, sv = _proj_odd(stream, w, ctab, stab)
    bias_row = _pad_lane_row(jnp.concatenate([b_i, b_f]), 0)
    o_c = _mlstm(mq, mk, mv, mo, mg, bias_row, ml_norm.reshape(1, -1), batch, seq)
    o_d = _swa(sq, sk, sv, _pad_lane_row(sinks, 0), batch, seq)
    return x, o_c, o_d


def _moe(x, xp, routing, w_gate, w_up, w_down, layer, s_gate, s_up, s_down, ln_g, ln_b):
    t, d = x.shape
    idx, gate, rank, cnt = routing
    counts = cnt[:, 0].astype(jnp.int32)
    block = int(min(max(pl.next_power_of_2(t * TOP_K // N_EXPERTS) // 2, EXPERT_BLOCK_MIN), EXPERT_BLOCK_MAX))
    padded = (counts + block - 1) // block * block
    pad_end = jnp.cumsum(padded)
    pad_start = pad_end - padded
    start_col = jnp.broadcast_to(pad_start.astype(F32).reshape(-1, 1), (N_EXPERTS, LANES))
    dest = _dest_rows(idx, rank, start_col)
    n_blocks = t * TOP_K // block + N_EXPERTS
    rows = n_blocks * block
    block_row = jnp.arange(n_blocks, dtype=jnp.int32) * block
    block_e = jnp.minimum(jnp.sum((pad_end[None, :] <= block_row[:, None]).astype(jnp.int32), axis=1), N_EXPERTS - 1)
    n_used = (pad_end[-1:] // block).astype(jnp.int32)
    live_end = jnp.sum(jnp.where(block_e[:, None] == jnp.arange(N_EXPERTS, dtype=jnp.int32)[None, :],
                                 (pad_start + counts)[None, :], 0), axis=1)
    n_valid = jnp.clip(live_end - block_row, 0, block).astype(jnp.int32)
    xs = _sc_scatter_rows(xp, dest, rows)
    ys = _experts(block_e, n_used, n_valid, xs, w_gate, w_up, w_down, layer, block)
    picked = _sc_gather_rows(ys, dest.reshape(-1)).reshape(TOP_K, t, d // 2)
    shared = _shared_expert(xp, s_gate.astype(BF16), s_up.astype(BF16), s_down.astype(BF16))
    return (x, gate.T, picked, shared, ln_g.reshape(1, -1), ln_b.reshape(1, -1))


def kernel(x, positions, ev_w_in, mla_q_norm, mla_w_qb, mla_kv_norm, mla_w_kvb, gdn_conv, gdn_a_log, gdn_dt_bias, gdn_norm, ev_w_out, od_w_in, mlstm_b_i, mlstm_b_f, mlstm_norm, swa_sinks, od_w_out, ln1_g, ln1_b, router_w, router_b, moe_w_gate, moe_w_up, moe_w_down, shared_w_gate, shared_w_up, shared_w_down, ln2_g, ln2_b):
    batch, seq, d = x.shape
    streams = STREAMS if batch % STREAMS == 0 else 1
    sb = batch // streams
    ts = sb * seq
    hs, tabs_m, tabs_s = [], [], []
    for s in range(streams):
        pos = positions[s * sb:(s + 1) * sb].reshape(ts, 1).astype(F32)
        tm_, ts_ = _rope_tables(pos)
        tabs_m.append(tm_)
        tabs_s.append(ts_)
        hs.append((x[s * sb:(s + 1) * sb].reshape(ts, d),))
    for layer in range(DEPTH):
        j = layer // 2
        if layer % 2 == 0:
            weights = _even_weights(ev_w_in[j], mla_w_qb[j], mla_w_kvb[j])
            w_out = ev_w_out[j].astype(BF16)
        else:
            weights = _take_cols(od_w_in[j], _odd_in_cols())
            w_out = od_w_out[j].astype(BF16)
        for s in range(streams):
            if layer % 2 == 0:
                h, a1, a2 = _even_mixer(hs[s], tabs_m[s], weights, mla_q_norm[j], mla_kv_norm[j], gdn_conv[j], gdn_a_log[j],
                                        gdn_dt_bias[j], gdn_norm[j], sb, seq)
            else:
                h, a1, a2 = _odd_mixer(hs[s], tabs_s[s], weights, mlstm_b_i[j], mlstm_b_f[j], mlstm_norm[j], swa_sinks[j],
                                       sb, seq)
            bias_col = jnp.broadcast_to(router_b[layer].reshape(-1, 1).astype(F32), (N_EXPERTS, LANES))
            h, hp, *routing = _outproj_ln_route(h, a1, a2, w_out, ln1_g[layer].reshape(1, -1), ln1_b[layer].reshape(1, -1),
                                                router_w[layer].T, bias_col)
            hs[s] = _moe(h, hp, routing, moe_w_gate, moe_w_up, moe_w_down, layer,
                         shared_w_gate[layer], shared_w_up[layer], shared_w_down[layer], ln2_g[layer], ln2_b[layer])
    return jnp.concatenate([_combine(*h).reshape(sb, seq, d) for h in hs], axis=0)
```

```python
import functools
import math

import numpy as np
import jax
import jax.numpy as jnp
from jax import lax
from jax.experimental import pallas as pl
from jax.experimental.pallas import tpu as pltpu
from jax.experimental.pallas import tpu_sc as plsc

F32 = jnp.float32
BF16 = jnp.bfloat16

D_MODEL = 1024
DEPTH = 4
ROPE_THETA = 10000.0
EPS = 1e-6
LN_EPS = 1e-5
MLA_H, MLA_NOPE, MLA_ROPE, MLA_V = 8, 64, 32, 64
Q_LORA, KV_LORA = 256, 128
GDN_H, GDN_DK, GDN_DV, CONV_W, GDN_CHUNK = 4, 128, 128, 4, 64
ML_H, ML_DK, ML_DV, ML_CHUNK = 4, 64, 128, 64
SWA_H, SWA_KV, SWA_D, WINDOW = 8, 2, 64, 128
N_EXPERTS, N_GROUPS, TOPK_GROUPS, TOP_K = 64, 8, 4, 8
D_EXPERT, D_SHARED = 256, 256
ROUTED_SCALE = 2.5
DN_ALPHA = (2 * DEPTH) ** 0.25

LANES = 128
SUBLANES = 8
V7X_VMEM_BYTES = 64 * 1024 * 1024
VMEM_LIMIT = V7X_VMEM_BYTES * 3 // 4

EXPERT_BLOCK_MIN = 256
EXPERT_BLOCK_MAX = 1024
STREAMS = 1
EXPERT_SUBBLOCKS = 4
SWA_SEQS_PER_STEP = 8
MLSTM_SEQS_PER_STEP = 2
GDN_SEQS_PER_STEP = 8
SC_CHUNK = 64


def _cparams(sem, vmem=VMEM_LIMIT):
    return pltpu.CompilerParams(dimension_semantics=sem, vmem_limit_bytes=vmem)


def _dot(a, b):
    return jnp.dot(a, b, preferred_element_type=F32)


def _dot_nt(a, b):
    return lax.dot_general(a, b, (((1,), (1,)), ((), ())), preferred_element_type=F32)


def _dot_tn(a, b):
    return lax.dot_general(a, b, (((0,), (0,)), ((), ())), preferred_element_type=F32)


def _split2(a):
    hi = a.astype(BF16)
    lo = (a - hi.astype(F32)).astype(BF16)
    return hi, lo


def _split3(a):
    p1 = a.astype(BF16)
    r = a - p1.astype(F32)
    p2 = r.astype(BF16)
    p3 = (r - p2.astype(F32)).astype(BF16)
    return p1, p2, p3


def _dot3(a, b, dot=_dot):
    ah, al = _split2(a)
    bh, bl = _split2(b)
    return dot(ah, bh) + (dot(ah, bl) + dot(al, bh))


def _dot_sel(sel, b, dot=_dot):
    sel = sel.astype(BF16)
    p1, p2, p3 = _split3(b)
    return dot(sel, p1) + (dot(sel, p2) + dot(sel, p3))


def _sigmoid(x):
    return 1.0 / (1.0 + jnp.exp(-x))


def _softplus(x):
    return jnp.maximum(x, 0.0) + jnp.log(1.0 + jnp.exp(-jnp.abs(x)))


def _silu(x):
    return x * _sigmoid(x)


def _lane_bcast(x, c):
    return jnp.broadcast_to(x[:, c:c + 1], x.shape)


def _iota2(shape, dim):
    return lax.broadcasted_iota(jnp.int32, shape, dim)


def _rope_kernel(pos_ref, rows_ref, sel_ref, cm_ref, sm_ref, cs_ref, ss_ref):
    ang = pos_ref[...] * rows_ref[0:1, :]
    cos_parts = _split3(jnp.cos(ang))
    sin_parts = _split3(jnp.sin(ang))

    def place(parts, k):
        return _dot(parts[0], sel_ref[k]) + (_dot(parts[1], sel_ref[k]) + _dot(parts[2], sel_ref[k]))

    cm_ref[...] = place(cos_parts, 0) + rows_ref[1:2, :]
    sm_ref[...] = place(sin_parts, 1)
    cs_ref[...] = place(cos_parts, 2)
    ss_ref[...] = place(sin_parts, 3)


def _rope_consts():
    hm, hs = MLA_ROPE // 2, SWA_D // 2
    rows = np.zeros((8, LANES), np.float32)
    rows[0, :hm] = ROPE_THETA ** (-(np.arange(0, MLA_ROPE, 2, dtype=np.float32) / MLA_ROPE))
    rows[0, hm:hm + hs] = ROPE_THETA ** (-(np.arange(0, SWA_D, 2, dtype=np.float32) / SWA_D))
    rows[1, :MLA_NOPE] = 1.0
    sel = np.zeros((4, LANES, LANES), np.float32)
    for j in range(hm):
        sel[0, j, MLA_NOPE + j] = sel[0, j, MLA_NOPE + hm + j] = 1.0
        sel[1, j, MLA_NOPE + j] = -1.0
        sel[1, j, MLA_NOPE + hm + j] = 1.0
    for h in range(LANES // SWA_D):
        for j in range(hs):
            sel[2, hm + j, h * SWA_D + j] = sel[2, hm + j, h * SWA_D + hs + j] = 1.0
            sel[3, hm + j, h * SWA_D + j] = -1.0
            sel[3, hm + j, h * SWA_D + hs + j] = 1.0
    return jnp.asarray(rows), jnp.asarray(sel, BF16)


def _rope_tables(pos, tm=512):
    t = pos.shape[0]
    tm = min(tm, t)
    rows, sel = _rope_consts()
    cm, sm, cs, ss = pl.pallas_call(
        _rope_kernel,
        grid=(t // tm,),
        in_specs=[pl.BlockSpec((tm, 1), lambda i: (i, 0)), pl.BlockSpec((8, LANES), lambda i: (0, 0)),
                  pl.BlockSpec((4, LANES, LANES), lambda i: (0, 0, 0))],
        out_specs=[pl.BlockSpec((tm, LANES), lambda i: (i, 0))] * 4,
        out_shape=[jax.ShapeDtypeStruct((t, LANES), F32)] * 4,
        compiler_params=_cparams(("arbitrary",)),
        name="rope_tables",
    )(pos, rows, sel)
    return (cm, sm), (cs, ss)


N_COMBINE_IN = 6
FUSED_TM = 256


def _combined(x_ref, gate_ref, rows_ref, sh_ref, g_ref, b_ref):
    gate = gate_ref[...]
    ya, yb = _unpack_pairs(sh_ref[...])
    for s in range(TOP_K):
        a, b = _unpack_pairs(rows_ref[s])
        ya = ya + gate[:, s:s + 1] * a
        yb = yb + gate[:, s:s + 1] * b
    ff = jnp.concatenate([ya, yb], axis=1)
    return _layer_norm(DN_ALPHA * x_ref[...] + ff, g_ref[...], b_ref[...])


def _stream_specs(stream, tm):
    d = stream[0].shape[1]
    row = lambda i: (i, 0)
    fix = lambda i: (0, 0)
    specs = [pl.BlockSpec((tm, d), row)]
    if len(stream) > 1:
        specs += [pl.BlockSpec((tm, TOP_K), row), pl.BlockSpec((TOP_K, tm, d // 2), lambda i: (0, i, 0)),
                  pl.BlockSpec((tm, d // 2), row), pl.BlockSpec((1, d), fix), pl.BlockSpec((1, d), fix)]
    return specs


def _stream_tile(stream_refs, h_ref):
    if h_ref is None:
        return stream_refs[0][...].astype(BF16)
    h = _combined(*stream_refs)
    h_ref[...] = h
    return h.astype(BF16)


def _proj_even_kernel(*refs, tiles_per_seq, fused):
    n_in = N_COMBINE_IN if fused else 1
    stream_refs = refs[:n_in]
    w_ref, cw_ref, c_ref, s_ref, qn_ref, kvn_ref, wq_ref, wkv_ref = refs[n_in:n_in + 8]
    q_ref, k_ref, v_ref, act_ref, z_ref, g_ref = refs[n_in + 8:n_in + 14]
    h_ref = refs[n_in + 14] if fused else None
    ext_ref, mla_ref = refs[-2:]
    tm = act_ref.shape[0]
    o = np.concatenate([[0], np.cumsum(EV_WIDTHS)]).tolist()
    halo = SUBLANES
    tap0 = halo - (CONV_W - 1)

    @pl.when(pl.program_id(0) % tiles_per_seq == 0)
    def _():
        ext_ref[0:halo, :] = jnp.zeros((halo, ext_ref.shape[1]), F32)

    xb = _stream_tile(stream_refs, h_ref)
    nchunk = 3
    cw = EV_WIDTHS[1] // nchunk

    def project(ci):
        ext_ref[halo:halo + tm, ci * cw:(ci + 1) * cw] = _dot(xb, w_ref[:, o[1] + ci * cw:o[1] + (ci + 1) * cw])

    project(0)
    for ci in range(nchunk):
        if ci + 1 < nchunk:
            project(ci + 1)
        else:
            mla_ref[...] = _dot(xb, w_ref[:, o[0]:o[1]])
            z_ref[...] = _dot(xb, w_ref[:, o[2]:o[3]]).astype(z_ref.dtype)
            g_ref[...] = _dot(xb, w_ref[:, o[3]:o[4]])
        cols = slice(ci * cw, (ci + 1) * cw)
        conv = cw_ref[0:1, cols] * ext_ref[tap0:tap0 + tm, cols]
        for j in range(1, CONV_W):
            conv = conv + cw_ref[j:j + 1, cols] * ext_ref[tap0 + j:tap0 + j + tm, cols]
        act_ref[:, cols] = _silu(conv).astype(act_ref.dtype)
    ext_ref[0:halo, :] = ext_ref[tm:tm + halo, :]
    _mla_prep_tile(mla_ref, c_ref, s_ref, qn_ref, kvn_ref, wq_ref, wkv_ref, q_ref, k_ref, v_ref)


def _proj_even(stream, w, conv_w, ctab, stab, qn, kvn, wq2, wkv2, seq, tm=512):
    t, k = stream[0].shape
    fused = len(stream) > 1
    tm = min(FUSED_TM if fused else tm, seq)
    row = lambda i: (i, 0)
    fix = lambda i: (0, 0)
    hw = MLA_H * LANES
    widths = (hw, hw, hw) + EV_WIDTHS[1:] + ((k,) if fused else ())
    dtypes = (BF16, BF16, BF16) + (F32,) * (len(widths) - 3)
    outs = pl.pallas_call(
        functools.partial(_proj_even_kernel, tiles_per_seq=seq // tm, fused=fused),
        grid=(t // tm,),
        in_specs=_stream_specs(stream, tm) + [pl.BlockSpec(w.shape, fix), pl.BlockSpec(conv_w.shape, fix),
                                              pl.BlockSpec((tm, LANES), row), pl.BlockSpec((tm, LANES), row),
                                              pl.BlockSpec(qn.shape, fix), pl.BlockSpec(kvn.shape, fix),
                                              pl.BlockSpec(wq2.shape, fix), pl.BlockSpec(wkv2.shape, fix)],
        out_specs=[pl.BlockSpec((tm, n), row) for n in widths],
        out_shape=[jax.ShapeDtypeStruct((t, n), dt) for n, dt in zip(widths, dtypes)],
        scratch_shapes=[pltpu.VMEM((tm + SUBLANES, EV_WIDTHS[1]), F32), pltpu.VMEM((tm, EV_WIDTHS[0]), F32)],
        compiler_params=_cparams(("arbitrary",)),
        name="combine_in_proj" if fused else "in_proj",
    )(*stream, w, conv_w, ctab, stab, qn, kvn, wq2, wkv2)
    return (outs[-1] if fused else stream[0],) + tuple(outs[:6])


OD_SEG = dict(mq=(0, 512), mk=(512, 1024), mv=(1024, 1536), mo=(1536, 2048), gates=(2048, 2176),
              sq=(2176, 2688), sk=(2688, 2944), sv=(2944, 3456))
OD_COLS = 3456


def _proj_odd_kernel(*refs, fused):
    n_in = N_COMBINE_IN if fused else 1
    stream_refs = refs[:n_in]
    w_ref, c_ref, s_ref, mq_ref, mk_ref, mv_ref, mo_ref, mg_ref, sq_ref, sk_ref, sv_ref = refs[n_in:n_in + 11]
    xb = _stream_tile(stream_refs, refs[n_in + 11] if fused else None)

    def seg(name):
        a, b = OD_SEG[name]
        return _dot(xb, w_ref[:, a:b])

    mq_ref[...] = seg("mq").astype(mq_ref.dtype)
    mk_ref[...] = seg("mk").astype(mk_ref.dtype)
    mv_ref[...] = seg("mv").astype(mv_ref.dtype)
    mo_ref[...] = seg("mo").astype(mo_ref.dtype)
    mg_ref[...] = seg("gates")
    c = c_ref[...]
    s = s_ref[...]
    def swap_halves(t):
        half = SWA_D // 2
        first_half = (_iota2(t.shape, 1) % SWA_D) < half
        return jnp.where(first_half, pltpu.roll(t, t.shape[1] - half, 1), pltpu.roll(t, half, 1))

    c8 = jnp.concatenate([c] * (SWA_H // 2), axis=1)
    s8 = jnp.concatenate([s] * (SWA_H // 2), axis=1)
    q = seg("sq")
    sq_ref[...] = (q * c8 + swap_halves(q) * s8).astype(sq_ref.dtype)
    c2 = jnp.concatenate([c] * SWA_KV, axis=1)
    s2 = jnp.concatenate([s] * SWA_KV, axis=1)
    k = seg("sk")
    sk_ref[...] = (k * c2 + swap_halves(k) * s2).astype(sk_ref.dtype)
    sv_ref[...] = seg("sv").astype(sv_ref.dtype)


def _proj_odd(stream, w, ctab, stab, tm=512):
    t, k = stream[0].shape
    fused = len(stream) > 1
    tm = min(FUSED_TM if fused else tm, t)
    widths = (512, 512, 512, 512, 128, SWA_H * SWA_D, SWA_KV * LANES, 2 * SWA_KV * LANES)
    dtypes = (F32, F32, F32, F32, F32, BF16, BF16, BF16)
    n_out = len(widths)
    if fused:
        widths, dtypes = widths + (k,), dtypes + (F32,)
    outs = pl.pallas_call(
        functools.partial(_proj_odd_kernel, fused=fused),
        grid=(t // tm,),
        in_specs=_stream_specs(stream, tm) + [pl.BlockSpec(w.shape, lambda i: (0, 0)),
                                              pl.BlockSpec((tm, LANES), lambda i: (i, 0)),
                                              pl.BlockSpec((tm, LANES), lambda i: (i, 0))],
        out_specs=[pl.BlockSpec((tm, n), lambda i: (i, 0)) for n in widths],
        out_shape=[jax.ShapeDtypeStruct((t, n), dt) for n, dt in zip(widths, dtypes)],
        compiler_params=_cparams(("arbitrary",)),
        name="combine_in_proj_odd" if fused else "in_proj_odd",
    )(*stream, w, ctab, stab)
    return (outs[-1] if fused else stream[0],) + tuple(outs[:n_out])


def _rms(x, g):
    return x * lax.rsqrt(jnp.mean(x * x, axis=-1, keepdims=True) + EPS) * g


def _mla_prep_tile(in_ref, c_ref, s_ref, qn_ref, kvn_ref, wq_ref, wkv_ref, q_ref, k_ref, v_ref):
    hw = MLA_H * LANES
    c = c_ref[...]
    s = s_ref[...]
    c8 = jnp.concatenate([c] * MLA_H, axis=1)
    s8 = jnp.concatenate([s] * MLA_H, axis=1)
    def swap_halves(t):
        half = MLA_ROPE // 2
        first_half = (_iota2(t.shape, 1) % LANES) < MLA_NOPE + half
        return jnp.where(first_half, pltpu.roll(t, t.shape[1] - half, 1), pltpu.roll(t, half, 1))

    cqn = _rms(in_ref[:, 0:Q_LORA], qn_ref[...]).astype(BF16)
    qq = _dot(cqn, wq_ref[...])
    scale = (MLA_NOPE + MLA_ROPE) ** -0.5
    q_ref[...] = ((qq[:, :hw] * c8 + qq[:, hw:] * s8) * scale).astype(q_ref.dtype)
    ckvn = _rms(in_ref[:, Q_LORA:Q_LORA + KV_LORA], kvn_ref[...]).astype(BF16)
    kv = _dot(ckvn, wkv_ref[...])
    o = Q_LORA + KV_LORA
    kr = in_ref[:, o:o + LANES]
    krr = kr * c + swap_halves(kr) * s
    k_ref[...] = (kv[:, :hw] + jnp.concatenate([krr] * MLA_H, axis=1)).astype(k_ref.dtype)
    v_ref[...] = kv[:, hw:].astype(v_ref.dtype)


def _mla_attn_kernel(q_ref, k_ref, v_ref, o_ref, *, tq):
    i = pl.program_id(2)
    neg = -1e30
    lane = _iota2((tq, LANES), 1)
    ones_lane = (MLA_V, 0)

    def chunk(j, carry, masked):
        start = pl.multiple_of(j * tq, tq)
        out = []
        for hh in range(2):
            m, acc = carry[hh]
            q = q_ref[:, hh * LANES:(hh + 1) * LANES]
            kc = k_ref[pl.ds(start, tq), hh * LANES:(hh + 1) * LANES]
            vc = v_ref[pl.ds(start, tq), hh * LANES:(hh + 1) * LANES]
            vc = jnp.where(lane == ones_lane[hh], jnp.ones_like(vc), vc)
            s = _dot_nt(q, kc)
            if masked:
                s = jnp.where(_iota2(s.shape, 0) >= _iota2(s.shape, 1), s, neg)
            m_new = jnp.maximum(m, jnp.max(s, axis=-1, keepdims=True))
            alpha = jnp.exp(m - m_new)
            p = jnp.exp((s - m_new).astype(BF16))
            acc = alpha * acc + _dot(p, vc)
            out.append((m_new, acc))
        return tuple(out)

    one = (jnp.full((tq, 1), neg, F32), jnp.zeros((tq, LANES), F32))
    carry = lax.fori_loop(0, i, lambda j, c: chunk(j, c, False), (one, one))
    (_, acc0), (_, acc1) = chunk(i, carry, True)
    o0 = acc0 / _lane_bcast(acc0, ones_lane[0])
    o1 = acc1 / _lane_bcast(acc1, ones_lane[1])
    o_ref[...] = jnp.where(lane < MLA_V, o0, o1).astype(o_ref.dtype)


def _mla_attn(q, k, v, batch, seq, tq=512):
    tq = min(tq, seq)
    nq = seq // tq
    pairs = MLA_H // 2
    return pl.pallas_call(
        functools.partial(_mla_attn_kernel, tq=tq),
        grid=(batch, pairs, nq),
        in_specs=[pl.BlockSpec((tq, 2 * LANES), lambda b, p, i: (b * nq + i, p)),
                  pl.BlockSpec((seq, 2 * LANES), lambda b, p, i: (b, p)),
                  pl.BlockSpec((seq, 2 * LANES), lambda b, p, i: (b, p))],
        out_specs=pl.BlockSpec((tq, LANES), lambda b, p, i: (b * nq + i, p)),
        out_shape=jax.ShapeDtypeStruct((batch * seq, pairs * LANES), BF16),
        compiler_params=_cparams(("arbitrary", "arbitrary", "arbitrary")),
        name="mla_attn",
    )(q, k, v)


def _unit_lower_inverse_many(ns):
    c = ns[0].shape[0]
    eye = (_iota2((c, c), 0) == _iota2((c, c), 1)).astype(F32)
    xs = [-n for n in ns]
    ps = [eye + x for x in xs]
    xb = [x.astype(BF16) for x in xs]
    for _ in range(int(math.log2(c)) - 1):
        xs = [_dot(b, b) for b in xb]
        xb = [x.astype(BF16) for x in xs]
        ps = [p + _dot(p.astype(BF16), b) for p, b in zip(ps, xb)]
    return ps


def _gdn_kernel(act_ref, g_ref, z_ref, al_ref, dt_ref, on_ref, o_ref, st_ref):
    c = GDN_CHUNK
    hd = GDN_DK
    nqk = GDN_H * GDN_DK

    @pl.when(pl.program_id(1) == 0)
    def _():
        st_ref[...] = jnp.zeros(st_ref.shape, F32)

    tri = (_iota2((c, c), 0) >= _iota2((c, c), 1)).astype(F32)
    row_ge = _iota2((c, c), 0) >= _iota2((c, c), 1)
    row_gt = _iota2((c, c), 0) > _iota2((c, c), 1)
    lane = _iota2((c, LANES), 1)

    seqs = []
    for bb in range(act_ref.shape[0]):
        gates = g_ref[bb]
        g_all = -jnp.exp(al_ref[...]) * _softplus(gates + dt_ref[...])
        gc_all = _dot_sel(tri, g_all)
        seqs.append(dict(beta_all=_sigmoid(gates), gc_all=gc_all, gc_parts=_split3(gc_all)))
    units = []
    for bb, sq in enumerate(seqs):
        for h in range(GDN_H):
            q = act_ref[bb, :, h * hd:(h + 1) * hd].astype(F32)
            k = act_ref[bb, :, nqk + h * hd:nqk + (h + 1) * hd].astype(F32)
            v = act_ref[bb, :, 2 * nqk + h * GDN_DV:2 * nqk + (h + 1) * GDN_DV].astype(F32)
            q = q * lax.rsqrt(jnp.sum(q * q, axis=-1, keepdims=True) + EPS) * (GDN_DK ** -0.5)
            k = k * lax.rsqrt(jnp.sum(k * k, axis=-1, keepdims=True) + EPS)
            beta = _lane_bcast(sq["beta_all"], h)
            gcol = _lane_bcast(sq["gc_all"], GDN_H + h)
            units.append(dict(bb=bb, h=h, q=q, k=k, v=v, beta=beta, gcol=gcol, kb=k * beta, parts=sq["gc_parts"]))
    for u in units:
        pick = (lane == GDN_H + u["h"]).astype(BF16)
        p0, p1, p2 = u["parts"]
        u["grow"] = _dot_nt(pick, p0) + (_dot_nt(pick, p1) + _dot_nt(pick, p2))
        u["kk"] = _dot3(u["kb"], u["k"], _dot_nt)
        u["qk"] = _dot_nt(u["q"].astype(BF16), u["k"].astype(BF16))
    for u in units:
        gcol = u["gcol"]
        decay = jnp.exp(jnp.where(row_ge, gcol[:, :c] - u["grow"], -jnp.inf))
        eg = jnp.exp(gcol)
        glast = gcol[c - 1:c, :]
        u["lower"] = jnp.where(row_gt, u["kk"] * decay, 0.0)
        u["rhs"] = jnp.concatenate([u["v"] * u["beta"], u["kb"] * eg], axis=1)
        u["attn"] = u["qk"] * decay
        u["qg"] = (u["q"] * eg).astype(BF16)
        u["kg"] = (u["k"] * jnp.exp(glast - gcol)).astype(BF16)
        u["gl"] = jnp.exp(glast)

    tinvs = _unit_lower_inverse_many([u["lower"] for u in units])
    uws = []
    for u, tinv in zip(units, tinvs):
        uws.append(_dot(tinv.astype(BF16), u["rhs"].astype(BF16)))
    states = [st_ref[u["bb"], u["h"]] for u in units]
    sbs = [s.astype(BF16) for s in states]
    vnews = [(uw[:, :GDN_DV] - _dot(uw[:, GDN_DV:].astype(BF16), sb)).astype(BF16) for uw, sb in zip(uws, sbs)]
    for u, state, sb, vnb in zip(units, states, sbs, vnews):
        bb, h = u["bb"], u["h"]
        o = _dot(u["qg"], sb) + _dot(u["attn"].astype(BF16), vnb)
        st_ref[bb, h] = state * u["gl"] + _dot_tn(u["kg"], vnb)
        o = _rms(o, on_ref[...]) * _silu(z_ref[bb, :, h * GDN_DV:(h + 1) * GDN_DV].astype(F32))
        o_ref[bb, :, h * GDN_DV:(h + 1) * GDN_DV] = o.astype(o_ref.dtype)


def _gdn(act, gates, z, a_row, dt_row, o_norm, batch, seq):
    c = GDN_CHUNK
    nc = seq // c
    w3 = act.shape[1]
    wo = GDN_H * GDN_DV
    nb = min(GDN_SEQS_PER_STEP, batch)
    row = lambda b, i: (b, i, 0)
    fix = lambda b, i: (0, 0)
    out = pl.pallas_call(
        _gdn_kernel,
        grid=(batch // nb, nc),
        in_specs=[pl.BlockSpec((nb, c, w3), row), pl.BlockSpec((nb, c, LANES), row), pl.BlockSpec((nb, c, wo), row),
                  pl.BlockSpec((1, LANES), fix), pl.BlockSpec((1, LANES), fix), pl.BlockSpec((1, GDN_DV), fix)],
        out_specs=pl.BlockSpec((nb, c, wo), row),
        out_shape=jax.ShapeDtypeStruct((batch, seq, wo), BF16),
        scratch_shapes=[pltpu.VMEM((nb, GDN_H, GDN_DK, GDN_DV), F32)],
        compiler_params=_cparams(("arbitrary", "arbitrary")),
        name="gdn",
    )(act.reshape(batch, seq, w3), gates.reshape(batch, seq, LANES), z.reshape(batch, seq, wo), a_row, dt_row, o_norm)
    return out.reshape(batch * seq, wo)


def _mlstm_kernel(q_ref, k_ref, v_ref, og_ref, g_ref, bias_ref, nrm_ref, o_ref, c_ref, n_ref, m_ref):
    @pl.when(pl.program_id(1) == 0)
    def _():
        c_ref[...] = jnp.zeros(c_ref.shape, F32)
        n_ref[...] = jnp.zeros(n_ref.shape, F32)
        m_ref[...] = jnp.zeros(m_ref.shape, F32)

    c = ML_CHUNK
    tri = (_iota2((c, c), 0) >= _iota2((c, c), 1)).astype(F32)
    row_ge = _iota2((c, c), 0) >= _iota2((c, c), 1)
    ones = jnp.ones((c, LANES), F32)
    lane = _iota2((c, LANES), 1)

    units = []
    for bb in range(q_ref.shape[0]):
        pre = g_ref[bb] + bias_ref[...]
        logf = jnp.minimum(pre, 0.0) - jnp.log(1.0 + jnp.exp(-jnp.abs(pre)))
        bcum_all = _dot_sel(tri, logf)
        for h in range(ML_H):
            q = q_ref[bb, :, h * LANES:(h + 1) * LANES].astype(F32)
            k = k_ref[bb, :, h * LANES:(h + 1) * LANES].astype(F32) * (ML_DK ** -0.5)
            units.append(dict(bb=bb, h=h, q=q, k=k, qb=q.astype(BF16), vb=v_ref[bb, :, h * ML_DV:(h + 1) * ML_DV].astype(BF16),
                              bcol=_lane_bcast(bcum_all, ML_H + h),
                              icol=_lane_bcast(pre, h),
                              col=jnp.where(lane == h, pre, 0.0) - jnp.where(lane == ML_H + h, bcum_all, 0.0),
                              m_st=m_ref[bb, h], cst=c_ref[bb, h], nst=n_ref[bb, h]))
    for u in units:
        u["row"] = _dot_sel(ones, u["col"], _dot_nt)
        u["qk"] = _dot_nt(u["qb"], u["k"].astype(BF16))
        u["qc"] = _dot(u["qb"], u["cst"].astype(BF16))
    for u in units:
        u["d"] = jnp.where(row_ge, u["bcol"][:, :c] + u["row"], -jnp.inf)
        u["inter"] = u["bcol"] + u["m_st"]
        u["m_t"] = jnp.maximum(u["inter"], jnp.max(u["d"], axis=-1, keepdims=True))
        u["b_end"] = u["bcol"][c - 1:c, :]
        u["a"] = u["b_end"] - u["bcol"] + u["icol"]
        u["m_new"] = jnp.maximum(u["b_end"] + u["m_st"], jnp.max(u["a"], axis=0, keepdims=True))
    for u in units:
        u["w_inter"] = jnp.exp(u["inter"] - u["m_t"])
        u["p"] = jnp.exp(u["d"] - u["m_t"][:, :c]) * u["qk"]
        u["keep"] = jnp.exp(u["b_end"] + u["m_st"] - u["m_new"])
        u["ks"] = u["k"] * jnp.exp(u["a"] - u["m_new"])
    for u in units:
        u["pv"] = _dot(u["p"].astype(BF16), u["vb"])
        u["kv"] = _dot_tn(u["ks"].astype(BF16), u["vb"])
    for u in units:
        u["den"] = (u["w_inter"] * jnp.sum(u["q"] * u["nst"], axis=-1, keepdims=True)
                    + jnp.sum(u["p"], axis=-1, keepdims=True))
    for u in units:
        bb, h = u["bb"], u["h"]
        num = u["w_inter"] * u["qc"] + u["pv"]
        hc = num / jnp.maximum(jnp.abs(u["den"]), jnp.exp(-u["m_t"]))
        c_ref[bb, h] = u["cst"] * u["keep"] + u["kv"]
        n_ref[bb, h] = u["nst"] * u["keep"] + jnp.sum(u["ks"], axis=0, keepdims=True)
        m_ref[bb, h] = u["m_new"]
        hn = (_rms(hc, nrm_ref[:, h * ML_DV:(h + 1) * ML_DV])
              * _sigmoid(og_ref[bb, :, h * ML_DV:(h + 1) * ML_DV].astype(F32)))
        o_ref[bb, :, h * ML_DV:(h + 1) * ML_DV] = hn.astype(o_ref.dtype)


def _mlstm(mq, mk, mv, mo, gates, bias_row, norm_row, batch, seq):
    c = ML_CHUNK
    nc = seq // c
    nb = min(MLSTM_SEQS_PER_STEP, batch)
    row = lambda b, i: (b, i, 0)
    fix = lambda b, i: (0, 0)
    wide = ML_H * LANES
    r3 = lambda a: a.reshape(batch, seq, a.shape[-1])
    out = pl.pallas_call(
        _mlstm_kernel,
        grid=(batch // nb, nc),
        in_specs=[pl.BlockSpec((nb, c, wide), row), pl.BlockSpec((nb, c, wide), row), pl.BlockSpec((nb, c, wide), row),
                  pl.BlockSpec((nb, c, wide), row), pl.BlockSpec((nb, c, LANES), row),
                  pl.BlockSpec((1, LANES), fix), pl.BlockSpec((1, wide), fix)],
        out_specs=pl.BlockSpec((nb, c, wide), row),
        out_shape=jax.ShapeDtypeStruct((batch, seq, wide), BF16),
        scratch_shapes=[pltpu.VMEM((nb, ML_H, LANES, ML_DV), F32), pltpu.VMEM((nb, ML_H, 1, LANES), F32),
                        pltpu.VMEM((nb, ML_H, 1, LANES), F32)],
        compiler_params=_cparams(("arbitrary", "arbitrary")),
        name="mlstm",
    )(r3(mq), r3(mk), r3(mv), r3(mo), r3(gates), bias_row, norm_row)
    return out.reshape(batch * seq, wide)


def _swa_kernel(q_ref, kc_ref, kp_ref, vc_ref, vp_ref, sink_ref, o_ref):
    w = WINDOW
    n = pl.program_id(1)
    scale = SWA_D ** -0.5
    qi = _iota2((w, w), 0)
    kj = _iota2((w, w), 1)
    mask_c = kj <= qi
    mask_p = jnp.logical_and(kj > qi, n > 0)
    grp = SWA_H // SWA_KV
    neg = -1e30
    units = [(bb, h) for bb in range(q_ref.shape[0]) for h in range(SWA_H)]
    scores = []
    half_of_lane = _iota2((w, LANES), 1) // SWA_D
    for bb, h in units:
        g = h // grp
        pair = q_ref[bb, :, (h // 2) * LANES:(h // 2 + 1) * LANES]
        q = jnp.where(half_of_lane == h % 2, pair, jnp.zeros_like(pair))
        scores.append((_dot_nt(q, kc_ref[bb, :, g * LANES:(g + 1) * LANES]),
                       _dot_nt(q, kp_ref[bb, :, g * LANES:(g + 1) * LANES])))
    masked, tops, exps, dens, probs = [], [], [], [], {}
    for sc, sp in scores:
        masked.append((jnp.where(mask_c, sc * scale, neg), jnp.where(mask_p, sp * scale, neg)))
    for (bb, h), (s_c, s_p) in zip(units, masked):
        tops.append(jnp.maximum(jnp.max(jnp.maximum(s_c, s_p), axis=-1, keepdims=True), sink_ref[:, h:h + 1]))
    for (s_c, s_p), m in zip(masked, tops):
        exps.append((jnp.where(mask_c, jnp.exp(s_c - m), 0.0), jnp.where(mask_p, jnp.exp(s_p - m), 0.0)))
    ones_b = jnp.ones((w, LANES), BF16)
    for (bb, h), (p_c, p_p), m in zip(units, exps, tops):
        p_c, p_p = p_c.astype(BF16), p_p.astype(BF16)
        probs[bb, h] = (p_c, p_p)
        dens.append(_dot(p_c, ones_b) + _dot(p_p, ones_b) + jnp.exp(sink_ref[:, h:h + 1] - m))
    inv = {u: 1.0 / den for u, den in zip(units, dens)}
    for bb in range(q_ref.shape[0]):
        for pair in range(SWA_H // 2):
            acc = None
            for sub in range(2):
                h = 2 * pair + sub
                vcol = (2 * (h // grp) + sub) * LANES
                p_c, p_p = probs[bb, h]
                part = (_dot(p_c, vc_ref[bb, :, vcol:vcol + LANES]) + _dot(p_p, vp_ref[bb, :, vcol:vcol + LANES])) * inv[bb, h]
                acc = part if acc is None else acc + part
            o_ref[bb, :, pair * LANES:(pair + 1) * LANES] = acc.astype(o_ref.dtype)


def _swa(sq, sk, sv, sinks_row, batch, seq):
    w = WINDOW
    nb = seq // w
    ns = min(SWA_SEQS_PER_STEP, batch)
    wo = SWA_H * SWA_D
    cur = lambda b, n: (b, n, 0)
    prev = lambda b, n: (b, jnp.maximum(n - 1, 0), 0)
    r3 = lambda a: a.reshape(batch, seq, a.shape[-1])
    q3, k3, v3 = r3(sq), r3(sk), r3(sv)
    out = pl.pallas_call(
        _swa_kernel,
        grid=(batch // ns, nb),
        in_specs=[pl.BlockSpec((ns, w, sq.shape[1]), cur),
                  pl.BlockSpec((ns, w, sk.shape[1]), cur), pl.BlockSpec((ns, w, sk.shape[1]), prev),
                  pl.BlockSpec((ns, w, sv.shape[1]), cur), pl.BlockSpec((ns, w, sv.shape[1]), prev),
                  pl.BlockSpec((1, LANES), lambda b, n: (0, 0))],
        out_specs=pl.BlockSpec((ns, w, wo), cur),
        out_shape=jax.ShapeDtypeStruct((batch, seq, wo), BF16),
        compiler_params=_cparams(("arbitrary", "arbitrary")),
        name="swa",
    )(q3, k3, k3, v3, v3, sinks_row)
    return out.reshape(batch * seq, wo)


def _layer_norm(h, g, b):
    mu = jnp.mean(h, axis=-1, keepdims=True)
    d = h - mu
    var = jnp.mean(d * d, axis=-1, keepdims=True)
    return d * lax.rsqrt(var + LN_EPS) * g + b


ROUTE_ROWS = 512


def _outproj_kernel(x_ref, a1_ref, a2_ref, w_ref, g_ref, b_ref, wt_ref, bias_ref,
                    o_ref, op_ref, idx_ref, gate_ref, rank_ref, cnt_ref, carry_ref):
    tm = x_ref.shape[0]
    k1 = a1_ref.shape[1]
    rp = min(ROUTE_ROWS, tm)

    @pl.when(pl.program_id(0) == 0)
    def _():
        carry_ref[...] = jnp.zeros(carry_ref.shape, F32)

    for p in range(tm // rp):
        r = slice(p * rp, (p + 1) * rp)
        y = _dot(a1_ref[r, :].astype(BF16), w_ref[0:k1, :]) + _dot(a2_ref[r, :].astype(BF16), w_ref[k1:, :])
        h = _layer_norm(DN_ALPHA * x_ref[r, :] + y, g_ref[...], b_ref[...])
        o_ref[r, :] = h
        op_ref[r, :] = _pack_pairs(h)
    for p in range(tm // rp):
        r = slice(p * rp, (p + 1) * rp)
        idx, gate, rank = _route_tile(o_ref[r, :], wt_ref, bias_ref, carry_ref)
        idx_ref[:, r] = idx
        gate_ref[:, r] = gate
        rank_ref[:, r] = rank
    cnt_ref[...] = carry_ref[...]


def _outproj_ln_route(x, a1, a2, w, g, b, wt, bias_col, tm=1024):
    t, d = x.shape
    tm = min(tm, t)
    row = lambda i: (i, 0)
    col = lambda i: (0, i)
    fix = lambda i: (0, 0)
    return pl.pallas_call(
        _outproj_kernel,
        grid=(t // tm,),
        in_specs=[pl.BlockSpec((tm, d), row), pl.BlockSpec((tm, a1.shape[1]), row), pl.BlockSpec((tm, a2.shape[1]), row),
                  pl.BlockSpec(w.shape, fix), pl.BlockSpec((1, d), fix), pl.BlockSpec((1, d), fix),
                  pl.BlockSpec(wt.shape, fix), pl.BlockSpec((N_EXPERTS, LANES), fix)],
        out_specs=[pl.BlockSpec((tm, d), row), pl.BlockSpec((tm, d // 2), row),
                   pl.BlockSpec((TOP_K, tm), col), pl.BlockSpec((TOP_K, tm), col), pl.BlockSpec((TOP_K, tm), col),
                   pl.BlockSpec((N_EXPERTS, LANES), fix)],
        out_shape=[jax.ShapeDtypeStruct((t, d), F32), jax.ShapeDtypeStruct((t, d // 2), jnp.uint32),
                   jax.ShapeDtypeStruct((TOP_K, t), jnp.int32), jax.ShapeDtypeStruct((TOP_K, t), F32),
                   jax.ShapeDtypeStruct((TOP_K, t), jnp.int32), jax.ShapeDtypeStruct((N_EXPERTS, LANES), F32)],
        scratch_shapes=[pltpu.VMEM((N_EXPERTS, LANES), F32)],
        compiler_params=_cparams(("arbitrary",)),
        name="outproj_ln_route",
    )(x, a1, a2, w, g, b, wt, bias_col)


def _first_index(x, m, iota_f, sentinel):
    return jnp.min(jnp.where(x == m, iota_f, sentinel), axis=0, keepdims=True)


def _route_tile(x, wt_ref, bias_ref, carry_ref):
    tm = x.shape[0]
    e = N_EXPERTS
    gs = e // N_GROUPS
    ninf = -jnp.inf

    logits = _dot3(wt_ref[...], x, _dot_nt)
    scores = _sigmoid(logits)
    sel = scores + bias_ref[:, 0:1]

    sub_f = _iota2((gs, tm), 0).astype(F32)
    gscore = []
    for g in range(N_GROUPS):
        blk = sel[g * gs:(g + 1) * gs, :]
        m1 = jnp.max(blk, axis=0, keepdims=True)
        i1 = _first_index(blk, m1, sub_f, float(gs))
        m2 = jnp.max(jnp.where(sub_f == i1, ninf, blk), axis=0, keepdims=True)
        gscore.append(m1 + m2)
    gsc = jnp.concatenate(gscore, axis=0)
    grp_f = _iota2((N_GROUPS, tm), 0).astype(F32)
    gmask = jnp.zeros((N_GROUPS, tm), F32)
    for _ in range(TOPK_GROUPS):
        m = jnp.max(gsc, axis=0, keepdims=True)
        gi = _first_index(gsc, m, grp_f, float(N_GROUPS))
        hit = grp_f == gi
        gmask = jnp.where(hit, 1.0, gmask)
        gsc = jnp.where(hit, ninf, gsc)
    masked = jnp.concatenate(
        [jnp.where(gmask[g:g + 1, :] > 0.0, sel[g * gs:(g + 1) * gs, :], ninf) for g in range(N_GROUPS)], axis=0)

    exp_f = _iota2((e, tm), 0).astype(F32)
    chosen = jnp.zeros((e, tm), F32)
    idxs, gates = [], []
    for _ in range(TOP_K):
        m = jnp.max(masked, axis=0, keepdims=True)
        ei = _first_index(masked, m, exp_f, float(e))
        hit = exp_f == ei
        idxs.append(ei)
        gates.append(jnp.sum(jnp.where(hit, scores, 0.0), axis=0, keepdims=True))
        chosen = jnp.where(hit, 1.0, chosen)
        masked = jnp.where(hit, ninf, masked)
    gate = jnp.concatenate(gates, axis=0)
    gate = gate / jnp.sum(gate, axis=0, keepdims=True) * ROUTED_SCALE
    idx_f = jnp.concatenate(idxs, axis=0)

    upper = (_iota2((tm, tm), 0) < _iota2((tm, tm), 1)).astype(BF16)
    before = _dot(chosen.astype(BF16), upper) + carry_ref[...][:, 0:1]
    ranks = [jnp.sum(jnp.where(exp_f == idxs[k], before, 0.0), axis=0, keepdims=True) for k in range(TOP_K)]
    carry_ref[...] = carry_ref[...] + jnp.sum(chosen, axis=1, keepdims=True)
    return idx_f.astype(jnp.int32), gate, jnp.concatenate(ranks, axis=0).astype(jnp.int32)


def _dest_kernel(idx_ref, rank_ref, start_ref, dest_ref):
    tm = idx_ref.shape[1]
    exp_i = _iota2((N_EXPERTS, tm), 0)
    start = start_ref[:, 0:1]
    rows = [jnp.sum(jnp.where(exp_i == idx_ref[s:s + 1, :], start, 0.0), axis=0, keepdims=True) for s in range(TOP_K)]
    dest_ref[...] = jnp.concatenate(rows, axis=0).astype(jnp.int32) + rank_ref[...]


def _dest_rows(idx, rank, start_col, tm=2048):
    t = idx.shape[1]
    tm = min(tm, t)
    col = lambda i: (0, i)
    return pl.pallas_call(
        _dest_kernel,
        grid=(t // tm,),
        in_specs=[pl.BlockSpec((TOP_K, tm), col), pl.BlockSpec((TOP_K, tm), col),
                  pl.BlockSpec((N_EXPERTS, LANES), lambda i: (0, 0))],
        out_specs=pl.BlockSpec((TOP_K, tm), col),
        out_shape=jax.ShapeDtypeStruct((TOP_K, t), jnp.int32),
        compiler_params=_cparams(("arbitrary",)),
        name="moe_dest",
    )(idx, rank, start_col)


def _pack_pairs(x):
    n = x.shape[1] // 2
    hi = lax.bitcast_convert_type(x[:, :n].astype(BF16).astype(F32), jnp.uint32)
    lo = lax.bitcast_convert_type(x[:, n:].astype(BF16).astype(F32), jnp.uint32)
    return hi | (lo >> 16)


def _unpack_pairs(w):
    hi = lax.bitcast_convert_type(w & jnp.uint32(0xFFFF0000), F32)
    lo = lax.bitcast_convert_type(w << 16, F32)
    return hi, lo


def _sc_scatter_rows(xp, dest, rows, chunk=LANES):
    t, width = xp.shape
    info = plsc.get_sparse_core_info()
    ncores, nsub = info.num_cores, info.num_subcores
    per_worker = t // (ncores * nsub)
    nchunk = per_worker // chunk
    mesh = plsc.VectorSubcoreMesh(core_axis_name="c", subcore_axis_name="s")

    @functools.partial(
        pl.kernel, mesh=mesh,
        out_type=jax.ShapeDtypeStruct((rows, width), xp.dtype),
        scratch_types=[pltpu.VMEM((TOP_K, chunk), jnp.int32), pltpu.VMEM((chunk, width), xp.dtype), pltpu.SemaphoreType.DMA],
    )
    def scatter(xp_hbm, dest_hbm, out_hbm, idx_v, rows_v, sem):
        base = (lax.axis_index("s") * ncores + lax.axis_index("c")) * per_worker

        @pl.loop(0, nchunk)
        def _(i):
            off = pl.multiple_of(base + i * chunk, chunk)
            pltpu.sync_copy(dest_hbm.at[:, pl.ds(off, chunk)], idx_v)
            pltpu.sync_copy(xp_hbm.at[pl.ds(off, chunk)], rows_v)
            copies = [pltpu.async_copy(rows_v, out_hbm.at[idx_v.at[s]], sem) for s in range(TOP_K)]
            for cp in copies:
                cp.wait()

    return scatter(xp, dest)


def _experts_kernel(be_ref, nu_ref, nv_ref, first_ref, slot_ref, nxt_ref, xs_ref, wg_hbm, wu_hbm, wd_hbm, ys_ref,
                    wgf_ref, wuf_ref, wdf_ref, wgb_ref, wub_ref, wdb_ref, sem, *, layer):
    i = pl.program_id(0)

    def fetch(e, s):
        return [pltpu.make_async_copy(wg_hbm.at[layer, e], wgf_ref.at[s], sem.at[s]),
                pltpu.make_async_copy(wu_hbm.at[layer, e], wuf_ref.at[s], sem.at[s]),
                pltpu.make_async_copy(wd_hbm.at[layer, e], wdf_ref.at[s], sem.at[s])]

    @pl.when(i == 0)
    def _():
        for cp in fetch(be_ref[0], 0):
            cp.start()

    @pl.when(jnp.logical_and(first_ref[i] == 1, i < nu_ref[0]))
    def _():
        s = slot_ref[i]
        for cp in fetch(be_ref[i], s):
            cp.wait()
        wgb_ref[...] = wgf_ref[s].astype(BF16)
        wub_ref[...] = wuf_ref[s].astype(BF16)
        wdb_ref[...] = wdf_ref[s].astype(BF16)

        @pl.when(nxt_ref[i] >= 0)
        def _():
            for cp in fetch(nxt_ref[i], 1 - s):
                cp.start()

    @pl.when(i < nu_ref[0])
    def _():
        sub = xs_ref.shape[0] // EXPERT_SUBBLOCKS
        acts = []
        for r in range(EXPERT_SUBBLOCKS):
            rows = pl.ds(r * sub, sub)
            live = (_iota2((sub, 1), 0) + r * sub) < nv_ref[i]
            xa, xb = _unpack_pairs(jnp.where(live, xs_ref[rows, :], jnp.uint32(0)))
            x = jnp.concatenate([xa.astype(BF16), xb.astype(BF16)], axis=1)
            acts.append((_dot(x, wgb_ref[...]), _dot(x, wub_ref[...])))
        outs = [_dot((_silu(gate) * up).astype(BF16), wdb_ref[...]) for gate, up in acts]
        for r, y in enumerate(outs):
            ys_ref[pl.ds(r * sub, sub), :] = _pack_pairs(y)


def _experts(block_e, n_used, n_valid, xs, wg, wu, wd, layer, block):
    rows, half = xs.shape
    d = 2 * half
    nb = rows // block
    pos = jnp.arange(nb, dtype=jnp.int32)
    first = jnp.concatenate([jnp.ones((1,), jnp.int32), (block_e[1:] != block_e[:-1]).astype(jnp.int32)])
    slot = (jnp.cumsum(first) - 1) % 2
    later = (pos[None, :] > pos[:, None]) & (block_e[None, :] != block_e[:, None]) & (pos[None, :] < n_used[0])
    nxt_pos = jnp.min(jnp.where(later, pos[None, :], nb), axis=1)
    nxt = jnp.where(nxt_pos < nb, block_e[jnp.minimum(nxt_pos, nb - 1)], -1)
    blk = lambda i, be, nu, *rest: (jnp.minimum(i, nu[0] - 1), 0)
    hbm = pl.BlockSpec(memory_space=pl.ANY)
    return pl.pallas_call(
        functools.partial(_experts_kernel, layer=layer),
        grid_spec=pltpu.PrefetchScalarGridSpec(
            num_scalar_prefetch=6,
            grid=(nb,),
            in_specs=[pl.BlockSpec((block, half), blk), hbm, hbm, hbm],
            out_specs=pl.BlockSpec((block, half), blk),
            scratch_shapes=[pltpu.VMEM((2, d, D_EXPERT), F32), pltpu.VMEM((2, d, D_EXPERT), F32),
                            pltpu.VMEM((2, D_EXPERT, d), F32),
                            pltpu.VMEM((d, D_EXPERT), BF16), pltpu.VMEM((d, D_EXPERT), BF16),
                            pltpu.VMEM((D_EXPERT, d), BF16), pltpu.SemaphoreType.DMA((2,))],
        ),
        out_shape=jax.ShapeDtypeStruct((rows, half), jnp.uint32),
        compiler_params=_cparams(("arbitrary",)),
        name="moe_experts",
    )(block_e, n_used, n_valid, first, slot.astype(jnp.int32), nxt.astype(jnp.int32), xs, wg, wu, wd)


def _sc_gather_rows(table, idx, chunk=SC_CHUNK):
    n = idx.shape[0]
    width = table.shape[1]
    info = plsc.get_sparse_core_info()
    ncores, nsub = info.num_cores, info.num_subcores
    per_worker = n // (ncores * nsub)
    nchunk = per_worker // chunk
    mesh = plsc.VectorSubcoreMesh(core_axis_name="c", subcore_axis_name="s")

    @functools.partial(
        pl.kernel, mesh=mesh,
        out_type=jax.ShapeDtypeStruct((n, width), table.dtype),
        scratch_types=[pltpu.VMEM((nchunk, chunk), jnp.int32), pltpu.VMEM((2, chunk, width), table.dtype),
                       pltpu.SemaphoreType.DMA((2,)), pltpu.SemaphoreType.DMA((2,))],
    )
    def gather(table_hbm, idx_hbm, out_hbm, idx_v, rows_v, gsem, wsem):
        wid = lax.axis_index("s") * ncores + lax.axis_index("c")
        base = wid * per_worker
        pltpu.sync_copy(idx_hbm.at[pl.ds(wid * nchunk, nchunk)], idx_v)

        def fetch(j, b):
            return pltpu.make_async_copy(table_hbm.at[idx_v.at[j]], rows_v.at[b], gsem.at[b])

        def flush(j, b):
            off = pl.multiple_of(base + j * chunk, chunk)
            return pltpu.make_async_copy(rows_v.at[b], out_hbm.at[pl.ds(off, chunk)], wsem.at[b])

        fetch(0, 0).start()

        @pl.loop(0, nchunk, step=2)
        def _(i):
            for b in range(2):
                j = i + b
                fetch(j, b).wait()

                @pl.when(j + 1 < nchunk)
                def _():
                    @pl.when(j >= 1)
                    def _():
                        flush(j - 1, 1 - b).wait()

                    fetch(j + 1, 1 - b).start()

                flush(j, b).start()

        flush(nchunk - 2, 0).wait()
        flush(nchunk - 1, 1).wait()

    return gather(table, idx.reshape(n // chunk, chunk))


def _shared_kernel(xp_ref, sg_ref, su_ref, sd_ref, o_ref):
    xa, xb = _unpack_pairs(xp_ref[...])
    x = jnp.concatenate([xa.astype(BF16), xb.astype(BF16)], axis=1)
    hs = _silu(_dot(x, sg_ref[...])) * _dot(x, su_ref[...])
    o_ref[...] = _pack_pairs(_dot(hs.astype(BF16), sd_ref[...]))


def _shared_expert(xp, sg, su, sd, tm=512):
    t, half = xp.shape
    row = lambda i: (i, 0)
    fix = lambda i: (0, 0)
    return pl.pallas_call(
        _shared_kernel,
        grid=(t // tm,),
        in_specs=[pl.BlockSpec((tm, half), row), pl.BlockSpec(sg.shape, fix), pl.BlockSpec(su.shape, fix),
                  pl.BlockSpec(sd.shape, fix)],
        out_specs=pl.BlockSpec((tm, half), row),
        out_shape=jax.ShapeDtypeStruct((t, half), jnp.uint32),
        compiler_params=_cparams(("arbitrary",)),
        name="moe_shared",
    )(xp, sg, su, sd)


def _combine_kernel(x_ref, gate_ref, rows_ref, sh_ref, g_ref, b_ref, o_ref):
    o_ref[...] = _combined(x_ref, gate_ref, rows_ref, sh_ref, g_ref, b_ref)


def _combine(x, gate_t, rows, shared, g, b, tm=512):
    t, d = x.shape
    row = lambda i: (i, 0)
    return pl.pallas_call(
        _combine_kernel,
        grid=(t // tm,),
        in_specs=_stream_specs((x, gate_t, rows, shared, g, b), tm),
        out_specs=pl.BlockSpec((tm, d), row),
        out_shape=jax.ShapeDtypeStruct((t, d), F32),
        compiler_params=_cparams(("arbitrary",)),
        name="moe_combine",
    )(x, gate_t, rows, shared, g, b)


def _take_cols(w, idx):
    idx = np.asarray(idx)
    runs, start = [], 0
    for pos in range(1, len(idx) + 1):
        run_ends = pos == len(idx) or (idx[pos] != idx[pos - 1] + 1 if idx[pos - 1] >= 0 else idx[pos] >= 0)
        if run_ends:
            runs.append((start, int(idx[start]), pos - start))
            start = pos

    def body(w_ref, o_ref):
        for dst, src, width in runs:
            if src < 0:
                o_ref[:, dst:dst + width] = jnp.zeros((o_ref.shape[0], width), o_ref.dtype)
            else:
                o_ref[:, dst:dst + width] = w_ref[:, src:src + width].astype(o_ref.dtype)

    rows = w.shape[0]
    tr = min(rows, 256)
    return pl.pallas_call(
        body,
        grid=(rows // tr,),
        in_specs=[pl.BlockSpec((tr, w.shape[1]), lambda i: (i, 0))],
        out_specs=pl.BlockSpec((tr, len(idx)), lambda i: (i, 0)),
        out_shape=jax.ShapeDtypeStruct((rows, len(idx)), BF16),
        compiler_params=_cparams(("arbitrary",)),
        name="weight_cols",
    )(w)


def _pad_lane_row(v, first_lane, width=LANES):
    out = jnp.zeros((1, width), F32)
    return lax.dynamic_update_slice(out, v.reshape(1, -1).astype(F32), (0, first_lane))


def _even_in_cols():
    z = lambda n: -np.ones(n, int)
    kr0 = Q_LORA + KV_LORA
    cols = [np.arange(0, Q_LORA), np.arange(Q_LORA, Q_LORA + KV_LORA),
            z(MLA_NOPE), np.arange(kr0, kr0 + MLA_ROPE), z(LANES - MLA_NOPE - MLA_ROPE)]
    g0 = kr0 + MLA_ROPE
    nqk = GDN_H * GDN_DK
    cols.append(np.arange(g0, g0 + 3 * nqk))
    zoff = g0 + 3 * nqk + 2 * GDN_H
    cols.append(np.arange(zoff, zoff + GDN_H * GDN_DV))
    cols += [np.arange(g0 + 3 * nqk, g0 + 3 * nqk + 2 * GDN_H), z(LANES - 2 * GDN_H)]
    return np.concatenate(cols)


EV_WIDTHS = (Q_LORA + KV_LORA + LANES, 3 * GDN_H * GDN_DK, GDN_H * GDN_DV, LANES)


def _mla_q_cols():
    per = MLA_NOPE + MLA_ROPE
    half = MLA_ROPE // 2
    main, sw = [], []
    for h in range(MLA_H):
        b = h * per
        main += [np.arange(b, b + per), -np.ones(LANES - per, int)]
        sw += [-np.ones(MLA_NOPE, int), np.arange(b + MLA_NOPE + half, b + per), np.arange(b + MLA_NOPE, b + MLA_NOPE + half),
               -np.ones(LANES - per, int)]
    return np.concatenate(main + sw)


def _mla_kv_cols():
    per = MLA_NOPE + MLA_V
    kc, vc = [], []
    for h in range(MLA_H):
        b = h * per
        kc += [np.arange(b, b + MLA_NOPE), -np.ones(LANES - MLA_NOPE, int)]
        vv = np.arange(b + MLA_NOPE, b + per)
        pad = -np.ones(LANES - MLA_V, int)
        vc += [vv, pad] if h % 2 == 0 else [pad, vv]
    return np.concatenate(kc + vc)


def _odd_in_cols():
    z = lambda n: -np.ones(n, int)
    o = 0
    cols = []
    mq0, mk0 = 0, ML_H * ML_DK
    for base in (mq0, mk0):
        for h in range(ML_H):
            cols += [np.arange(base + h * ML_DK, base + (h + 1) * ML_DK), z(LANES - ML_DK)]
    mv0 = 2 * ML_H * ML_DK
    cols.append(np.arange(mv0, mv0 + ML_H * ML_DV))
    mi0 = mv0 + ML_H * ML_DV
    mo0 = mi0 + 2 * ML_H
    cols.append(np.arange(mo0, mo0 + ML_H * ML_DV))
    cols += [np.arange(mi0, mi0 + 2 * ML_H), z(LANES - 2 * ML_H)]
    sq0 = mo0 + ML_H * ML_DV
    sk0 = sq0 + SWA_H * SWA_D
    sv0 = sk0 + SWA_KV * SWA_D
    half = SWA_D // 2

    cols.append(np.arange(sq0, sq0 + SWA_H * SWA_D))
    for g in range(SWA_KV):
        cols += [np.arange(sk0 + g * SWA_D, sk0 + (g + 1) * SWA_D)] * 2
    for g in range(SWA_KV):
        vv = np.arange(sv0 + g * SWA_D, sv0 + (g + 1) * SWA_D)
        cols += [vv, z(LANES - SWA_D), z(LANES - SWA_D), vv]
    return np.concatenate(cols)


def _even_weights(w_in, w_qb, w_kvb):
    return (_take_cols(w_in, _even_in_cols()), _take_cols(w_qb, _mla_q_cols()), _take_cols(w_kvb, _mla_kv_cols()))


def _even_mixer(stream, tabs, weights, q_norm, kv_norm, conv_w, a_log, dt_bias, o_norm, batch, seq):
    ctab, stab = tabs
    w, wq2, wkv2 = weights
    x, q, k, v, act, z, gates = _proj_even(stream, w, conv_w, ctab, stab, q_norm.reshape(1, -1), kv_norm.reshape(1, -1),
                                           wq2, wkv2, seq)
    o_a = _mla_attn(q, k, v, batch, seq)
    o_b = _gdn(act, gates, z, _pad_lane_row(a_log, GDN_H), _pad_lane_row(dt_bias, GDN_H),
               o_norm.reshape(1, -1), batch, seq)
    return x, o_a, o_b


def _odd_mixer(stream, tabs, w, b_i, b_f, ml_norm, sinks, batch, seq):
    ctab, stab = tabs
    x, mq, mk, mv, mo, mg, sq, sk, sv = _proj_odd(stream, w, ctab, stab)
    bias_row = _pad_lane_row(jnp.concatenate([b_i, b_f]), 0)
    o_c = _mlstm(mq, mk, mv, mo, mg, bias_row, ml_norm.reshape(1, -1), batch, seq)
    o_d = _swa(sq, sk, sv, _pad_lane_row(sinks, 0), batch, seq)
    return x, o_c, o_d


def _moe(x, xp, routing, w_gate, w_up, w_down, layer, s_gate, s_up, s_down, ln_g, ln_b):
    t, d = x.shape
    idx, gate, rank, cnt = routing
    counts = cnt[:, 0].astype(jnp.int32)
    block = int(min(max(pl.next_power_of_2(t * TOP_K // N_EXPERTS) // 2, EXPERT_BLOCK_MIN), EXPERT_BLOCK_MAX))
    padded = (counts + block - 1) // block * block
    pad_end = jnp.cumsum(padded)
    pad_start = pad_end - padded
    start_col = jnp.broadcast_to(pad_start.astype(F32).reshape(-1, 1), (N_EXPERTS, LANES))
    dest = _dest_rows(idx, rank, start_col)
    n_blocks = t * TOP_K // block + N_EXPERTS
    rows = n_blocks * block
    block_row = jnp.arange(n_blocks, dtype=jnp.int32) * block
    block_e = jnp.minimum(jnp.sum((pad_end[None, :] <= block_row[:, None]).astype(jnp.int32), axis=1), N_EXPERTS - 1)
    n_used = (pad_end[-1:] // block).astype(jnp.int32)
    live_end = jnp.sum(jnp.where(block_e[:, None] == jnp.arange(N_EXPERTS, dtype=jnp.int32)[None, :],
                                 (pad_start + counts)[None, :], 0), axis=1)
    n_valid = jnp.clip(live_end - block_row, 0, block).astype(jnp.int32)
    xs = _sc_scatter_rows(xp, dest, rows)
    ys = _experts(block_e, n_used, n_valid, xs, w_gate, w_up, w_down, layer, block)
    picked = _sc_gather_rows(ys, dest.reshape(-1)).reshape(TOP_K, t, d // 2)
    shared = _shared_expert(xp, s_gate.astype(BF16), s_up.astype(BF16), s_down.astype(BF16))
    return (x, gate.T, picked, shared, ln_g.reshape(1, -1), ln_b.reshape(1, -1))


def kernel(x, positions, ev_w_in, mla_q_norm, mla_w_qb, mla_kv_norm, mla_w_kvb, gdn_conv, gdn_a_log, gdn_dt_bias, gdn_norm, ev_w_out, od_w_in, mlstm_b_i, mlstm_b_f, mlstm_norm, swa_sinks, od_w_out, ln1_g, ln1_b, router_w, router_b, moe_w_gate, moe_w_up, moe_w_down, shared_w_gate, shared_w_up, shared_w_down, ln2_g, ln2_b):
    batch, seq, d = x.shape
    streams = STREAMS if batch % STREAMS == 0 else 1
    sb = batch // streams
    ts = sb * seq
    hs, tabs_m, tabs_s = [], [], []
    for s in range(streams):
        pos = positions[s * sb:(s + 1) * sb].reshape(ts, 1).astype(F32)
        tm_, ts_ = _rope_tables(pos)
        tabs_m.append(tm_)
        tabs_s.append(ts_)
        hs.append((x[s * sb:(s + 1) * sb].reshape(ts, d),))
    for layer in range(DEPTH):
        j = layer // 2
        if layer % 2 == 0:
            weights = _even_weights(ev_w_in[j], mla_w_qb[j], mla_w_kvb[j])
            w_out = ev_w_out[j].astype(BF16)
        else:
            weights = _take_cols(od_w_in[j], _odd_in_cols())
            w_out = od_w_out[j].astype(BF16)
        for s in range(streams):
            if layer % 2 == 0:
                h, a1, a2 = _even_mixer(hs[s], tabs_m[s], weights, mla_q_norm[j], mla_kv_norm[j], gdn_conv[j], gdn_a_log[j],
                                        gdn_dt_bias[j], gdn_norm[j], sb, seq)
            else:
                h, a1, a2 = _odd_mixer(hs[s], tabs_s[s], weights, mlstm_b_i[j], mlstm_b_f[j], mlstm_norm[j], swa_sinks[j],
                                       sb, seq)
            bias_col = jnp.broadcast_to(router_b[layer].reshape(-1, 1).astype(F32), (N_EXPERTS, LANES))
            h, hp, *routing = _outproj_ln_route(h, a1, a2, w_out, ln1_g[layer].reshape(1, -1), ln1_b[layer].reshape(1, -1),
                                                router_w[layer].T, bias_col)
            hs[s] = _moe(h, hp, routing, moe_w_gate, moe_w_up, moe_w_down, layer,
                         shared_w_gate[layer], shared_w_up[layer], shared_w_down[layer], ln2_g[layer], ln2_b[layer])
    return jnp.concatenate([_combine(*h).reshape(sb, seq, d) for h in hs], axis=0)
```

```python
import functools
import math

import numpy as np
import jax
import jax.numpy as jnp
from jax import lax
from jax.experimental import pallas as pl
from jax.experimental.pallas import tpu as pltpu
from jax.experimental.pallas import tpu_sc as plsc

F32 = jnp.float32
BF16 = jnp.bfloat16

D_MODEL = 1024
DEPTH = 4
ROPE_THETA = 10000.0
EPS = 1e-6
LN_EPS = 1e-5
MLA_H, MLA_NOPE, MLA_ROPE, MLA_V = 8, 64, 32, 64
Q_LORA, KV_LORA = 256, 128
GDN_H, GDN_DK, GDN_DV, CONV_W, GDN_CHUNK = 4, 128, 128, 4, 64
ML_H, ML_DK, ML_DV, ML_CHUNK = 4, 64, 128, 64
SWA_H, SWA_KV, SWA_D, WINDOW = 8, 2, 64, 128
N_EXPERTS, N_GROUPS, TOPK_GROUPS, TOP_K = 64, 8, 4, 8
D_EXPERT, D_SHARED = 256, 256
ROUTED_SCALE = 2.5
DN_ALPHA = (2 * DEPTH) ** 0.25

LANES = 128
SUBLANES = 8
V7X_VMEM_BYTES = 64 * 1024 * 1024
VMEM_LIMIT = V7X_VMEM_BYTES * 3 // 4

EXPERT_BLOCK_MIN = 256
EXPERT_BLOCK_MAX = 1024
STREAMS = 1
EXPERT_SUBBLOCKS = 4
SWA_SEQS_PER_STEP = 8
MLSTM_SEQS_PER_STEP = 2
GDN_SEQS_PER_STEP = 8
SC_CHUNK = 64


def _cparams(sem, vmem=VMEM_LIMIT):
    return pltpu.CompilerParams(dimension_semantics=sem, vmem_limit_bytes=vmem)


def _dot(a, b):
    return jnp.dot(a, b, preferred_element_type=F32)


def _dot_nt(a, b):
    return lax.dot_general(a, b, (((1,), (1,)), ((), ())), preferred_element_type=F32)


def _dot_tn(a, b):
    return lax.dot_general(a, b, (((0,), (0,)), ((), ())), preferred_element_type=F32)


def _split2(a):
    hi = a.astype(BF16)
    lo = (a - hi.astype(F32)).astype(BF16)
    return hi, lo


def _split3(a):
    p1 = a.astype(BF16)
    r = a - p1.astype(F32)
    p2 = r.astype(BF16)
    p3 = (r - p2.astype(F32)).astype(BF16)
    return p1, p2, p3


def _dot3(a, b, dot=_dot):
    ah, al = _split2(a)
    bh, bl = _split2(b)
    return dot(ah, bh) + (dot(ah, bl) + dot(al, bh))


def _dot_sel(sel, b, dot=_dot):
    sel = sel.astype(BF16)
    p1, p2, p3 = _split3(b)
    return dot(sel, p1) + (dot(sel, p2) + dot(sel, p3))


def _sigmoid(x):
    return 1.0 / (1.0 + jnp.exp(-x))


def _softplus(x):
    return jnp.maximum(x, 0.0) + jnp.log(1.0 + jnp.exp(-jnp.abs(x)))


def _silu(x):
    return x * _sigmoid(x)


def _lane_bcast(x, c):
    return jnp.broadcast_to(x[:, c:c + 1], x.shape)


def _iota2(shape, dim):
    return lax.broadcasted_iota(jnp.int32, shape, dim)


def _rope_kernel(pos_ref, rows_ref, sel_ref, cm_ref, sm_ref, cs_ref, ss_ref):
    ang = pos_ref[...] * rows_ref[0:1, :]
    cos_parts = _split3(jnp.cos(ang))
    sin_parts = _split3(jnp.sin(ang))

    def place(parts, k):
        return _dot(parts[0], sel_ref[k]) + (_dot(parts[1], sel_ref[k]) + _dot(parts[2], sel_ref[k]))

    cm_ref[...] = place(cos_parts, 0) + rows_ref[1:2, :]
    sm_ref[...] = place(sin_parts, 1)
    cs_ref[...] = place(cos_parts, 2)
    ss_ref[...] = place(sin_parts, 3)


def _rope_consts():
    hm, hs = MLA_ROPE // 2, SWA_D // 2
    rows = np.zeros((8, LANES), np.float32)
    rows[0, :hm] = ROPE_THETA ** (-(np.arange(0, MLA_ROPE, 2, dtype=np.float32) / MLA_ROPE))
    rows[0, hm:hm + hs] = ROPE_THETA ** (-(np.arange(0, SWA_D, 2, dtype=np.float32) / SWA_D))
    rows[1, :MLA_NOPE] = 1.0
    sel = np.zeros((4, LANES, LANES), np.float32)
    for j in range(hm):
        sel[0, j, MLA_NOPE + j] = sel[0, j, MLA_NOPE + hm + j] = 1.0
        sel[1, j, MLA_NOPE + j] = -1.0
        sel[1, j, MLA_NOPE + hm + j] = 1.0
    for h in range(LANES // SWA_D):
        for j in range(hs):
            sel[2, hm + j, h * SWA_D + j] = sel[2, hm + j, h * SWA_D + hs + j] = 1.0
            sel[3, hm + j, h * SWA_D + j] = -1.0
            sel[3, hm + j, h * SWA_D + hs + j] = 1.0
    return jnp.asarray(rows), jnp.asarray(sel, BF16)


def _rope_tables(pos, tm=512):
    t = pos.shape[0]
    tm = min(tm, t)
    rows, sel = _rope_consts()
    cm, sm, cs, ss = pl.pallas_call(
        _rope_kernel,
        grid=(t // tm,),
        in_specs=[pl.BlockSpec((tm, 1), lambda i: (i, 0)), pl.BlockSpec((8, LANES), lambda i: (0, 0)),
                  pl.BlockSpec((4, LANES, LANES), lambda i: (0, 0, 0))],
        out_specs=[pl.BlockSpec((tm, LANES), lambda i: (i, 0))] * 4,
        out_shape=[jax.ShapeDtypeStruct((t, LANES), F32)] * 4,
        compiler_params=_cparams(("arbitrary",)),
        name="rope_tables",
    )(pos, rows, sel)
    return (cm, sm), (cs, ss)


N_COMBINE_IN = 6
FUSED_TM = 256


def _combined(x_ref, gate_ref, rows_ref, sh_ref, g_ref, b_ref):
    gate = gate_ref[...]
    ya, yb = _unpack_pairs(sh_ref[...])
    for s in range(TOP_K):
        a, b = _unpack_pairs(rows_ref[s])
        ya = ya + gate[:, s:s + 1] * a
        yb = yb + gate[:, s:s + 1] * b
    ff = jnp.concatenate([ya, yb], axis=1)
    return _layer_norm(DN_ALPHA * x_ref[...] + ff, g_ref[...], b_ref[...])


def _stream_specs(stream, tm):
    d = stream[0].shape[1]
    row = lambda i: (i, 0)
    fix = lambda i: (0, 0)
    specs = [pl.BlockSpec((tm, d), row)]
    if len(stream) > 1:
        specs += [pl.BlockSpec((tm, TOP_K), row), pl.BlockSpec((TOP_K, tm, d // 2), lambda i: (0, i, 0)),
                  pl.BlockSpec((tm, d // 2), row), pl.BlockSpec((1, d), fix), pl.BlockSpec((1, d), fix)]
    return specs


def _stream_tile(stream_refs, h_ref):
    if h_ref is None:
        return stream_refs[0][...].astype(BF16)
    h = _combined(*stream_refs)
    h_ref[...] = h
    return h.astype(BF16)


def _proj_even_kernel(*refs, tiles_per_seq, fused):
    n_in = N_COMBINE_IN if fused else 1
    stream_refs = refs[:n_in]
    w_ref, cw_ref, c_ref, s_ref, qn_ref, kvn_ref, wq_ref, wkv_ref = refs[n_in:n_in + 8]
    q_ref, k_ref, v_ref, act_ref, z_ref, g_ref = refs[n_in + 8:n_in + 14]
    h_ref = refs[n_in + 14] if fused else None
    ext_ref, mla_ref = refs[-2:]
    tm = act_ref.shape[0]
    o = np.concatenate([[0], np.cumsum(EV_WIDTHS)]).tolist()
    halo = SUBLANES
    tap0 = halo - (CONV_W - 1)

    @pl.when(pl.program_id(0) % tiles_per_seq == 0)
    def _():
        ext_ref[0:halo, :] = jnp.zeros((halo, ext_ref.shape[1]), F32)

    xb = _stream_tile(stream_refs, h_ref)
    nchunk = 3
    cw = EV_WIDTHS[1] // nchunk

    def project(ci):
        ext_ref[halo:halo + tm, ci * cw:(ci + 1) * cw] = _dot(xb, w_ref[:, o[1] + ci * cw:o[1] + (ci + 1) * cw])

    project(0)
    for ci in range(nchunk):
        if ci + 1 < nchunk:
            project(ci + 1)
        else:
            mla_ref[...] = _dot(xb, w_ref[:, o[0]:o[1]])
            z_ref[...] = _dot(xb, w_ref[:, o[2]:o[3]]).astype(z_ref.dtype)
            g_ref[...] = _dot(xb, w_ref[:, o[3]:o[4]])
        cols = slice(ci * cw, (ci + 1) * cw)
        conv = cw_ref[0:1, cols] * ext_ref[tap0:tap0 + tm, cols]
        for j in range(1, CONV_W):
            conv = conv + cw_ref[j:j + 1, cols] * ext_ref[tap0 + j:tap0 + j + tm, cols]
        act_ref[:, cols] = _silu(conv).astype(act_ref.dtype)
    ext_ref[0:halo, :] = ext_ref[tm:tm + halo, :]
    _mla_prep_tile(mla_ref, c_ref, s_ref, qn_ref, kvn_ref, wq_ref, wkv_ref, q_ref, k_ref, v_ref)


def _proj_even(stream, w, conv_w, ctab, stab, qn, kvn, wq2, wkv2, seq, tm=512):
    t, k = stream[0].shape
    fused = len(stream) > 1
    tm = min(FUSED_TM if fused else tm, seq)
    row = lambda i: (i, 0)
    fix = lambda i: (0, 0)
    hw = MLA_H * LANES
    widths = (hw, hw, hw) + EV_WIDTHS[1:] + ((k,) if fused else ())
    dtypes = (BF16, BF16, BF16) + (F32,) * (len(widths) - 3)
    outs = pl.pallas_call(
        functools.partial(_proj_even_kernel, tiles_per_seq=seq // tm, fused=fused),
        grid=(t // tm,),
        in_specs=_stream_specs(stream, tm) + [pl.BlockSpec(w.shape, fix), pl.BlockSpec(conv_w.shape, fix),
                                              pl.BlockSpec((tm, LANES), row), pl.BlockSpec((tm, LANES), row),
                                              pl.BlockSpec(qn.shape, fix), pl.BlockSpec(kvn.shape, fix),
                                              pl.BlockSpec(wq2.shape, fix), pl.BlockSpec(wkv2.shape, fix)],
        out_specs=[pl.BlockSpec((tm, n), row) for n in widths],
        out_shape=[jax.ShapeDtypeStruct((t, n), dt) for n, dt in zip(widths, dtypes)],
        scratch_shapes=[pltpu.VMEM((tm + SUBLANES, EV_WIDTHS[1]), F32), pltpu.VMEM((tm, EV_WIDTHS[0]), F32)],
        compiler_params=_cparams(("arbitrary",)),
        name="combine_in_proj" if fused else "in_proj",
    )(*stream, w, conv_w, ctab, stab, qn, kvn, wq2, wkv2)
    return (outs[-1] if fused else stream[0],) + tuple(outs[:6])


OD_SEG = dict(mq=(0, 512), mk=(512, 1024), mv=(1024, 1536), mo=(1536, 2048), gates=(2048, 2176),
              sq=(2176, 2688), sk=(2688, 2944), sv=(2944, 3456))
OD_COLS = 3456


def _proj_odd_kernel(*refs, fused):
    n_in = N_COMBINE_IN if fused else 1
    stream_refs = refs[:n_in]
    w_ref, c_ref, s_ref, mq_ref, mk_ref, mv_ref, mo_ref, mg_ref, sq_ref, sk_ref, sv_ref = refs[n_in:n_in + 11]
    xb = _stream_tile(stream_refs, refs[n_in + 11] if fused else None)

    def seg(name):
        a, b = OD_SEG[name]
        return _dot(xb, w_ref[:, a:b])

    mq_ref[...] = seg("mq").astype(mq_ref.dtype)
    mk_ref[...] = seg("mk").astype(mk_ref.dtype)
    mv_ref[...] = seg("mv").astype(mv_ref.dtype)
    mo_ref[...] = seg("mo").astype(mo_ref.dtype)
    mg_ref[...] = seg("gates")
    c = c_ref[...]
    s = s_ref[...]
    def swap_halves(t):
        half = SWA_D // 2
        first_half = (_iota2(t.shape, 1) % SWA_D) < half
        return jnp.where(first_half, pltpu.roll(t, t.shape[1] - half, 1), pltpu.roll(t, half, 1))

    c8 = jnp.concatenate([c] * (SWA_H // 2), axis=1)
    s8 = jnp.concatenate([s] * (SWA_H // 2), axis=1)
    q = seg("sq")
    sq_ref[...] = (q * c8 + swap_halves(q) * s8).astype(sq_ref.dtype)
    c2 = jnp.concatenate([c] * SWA_KV, axis=1)
    s2 = jnp.concatenate([s] * SWA_KV, axis=1)
    k = seg("sk")
    sk_ref[...] = (k * c2 + swap_halves(k) * s2).astype(sk_ref.dtype)
    sv_ref[...] = seg("sv").astype(sv_ref.dtype)


def _proj_odd(stream, w, ctab, stab, tm=512):
    t, k = stream[0].shape
    fused = len(stream) > 1
    tm = min(FUSED_TM if fused else tm, t)
    widths = (512, 512, 512, 512, 128, SWA_H * SWA_D, SWA_KV * LANES, 2 * SWA_KV * LANES)
    dtypes = (F32, F32, F32, F32, F32, BF16, BF16, BF16)
    n_out = len(widths)
    if fused:
        widths, dtypes = widths + (k,), dtypes + (F32,)
    outs = pl.pallas_call(
        functools.partial(_proj_odd_kernel, fused=fused),
        grid=(t // tm,),
        in_specs=_stream_specs(stream, tm) + [pl.BlockSpec(w.shape, lambda i: (0, 0)),
                                              pl.BlockSpec((tm, LANES), lambda i: (i, 0)),
                                              pl.BlockSpec((tm, LANES), lambda i: (i, 0))],
        out_specs=[pl.BlockSpec((tm, n), lambda i: (i, 0)) for n in widths],
        out_shape=[jax.ShapeDtypeStruct((t, n), dt) for n, dt in zip(widths, dtypes)],
        compiler_params=_cparams(("arbitrary",)),
        name="combine_in_proj_odd" if fused else "in_proj_odd",
    )(*stream, w, ctab, stab)
    return (outs[-1] if fused else stream[0],) + tuple(outs[:n_out])


def _rms(x, g):
    return x * lax.rsqrt(jnp.mean(x * x, axis=-1, keepdims=True) + EPS) * g


def _mla_prep_tile(in_ref, c_ref, s_ref, qn_ref, kvn_ref, wq_ref, wkv_ref, q_ref, k_ref, v_ref):
    hw = MLA_H * LANES
    c = c_ref[...]
    s = s_ref[...]
    c8 = jnp.concatenate([c] * MLA_H, axis=1)
    s8 = jnp.concatenate([s] * MLA_H, axis=1)
    def swap_halves(t):
        half = MLA_ROPE // 2
        first_half = (_iota2(t.shape, 1) % LANES) < MLA_NOPE + half
        return jnp.where(first_half, pltpu.roll(t, t.shape[1] - half, 1), pltpu.roll(t, half, 1))

    cqn = _rms(in_ref[:, 0:Q_LORA], qn_ref[...]).astype(BF16)
    qq = _dot(cqn, wq_ref[...])
    scale = (MLA_NOPE + MLA_ROPE) ** -0.5
    q_ref[...] = ((qq[:, :hw] * c8 + qq[:, hw:] * s8) * scale).astype(q_ref.dtype)
    ckvn = _rms(in_ref[:, Q_LORA:Q_LORA + KV_LORA], kvn_ref[...]).astype(BF16)
    kv = _dot(ckvn, wkv_ref[...])
    o = Q_LORA + KV_LORA
    kr = in_ref[:, o:o + LANES]
    krr = kr * c + swap_halves(kr) * s
    k_ref[...] = (kv[:, :hw] + jnp.concatenate([krr] * MLA_H, axis=1)).astype(k_ref.dtype)
    v_ref[...] = kv[:, hw:].astype(v_ref.dtype)


def _mla_attn_kernel(q_ref, k_ref, v_ref, o_ref, *, tq):
    i = pl.program_id(2)
    neg = -1e30
    lane = _iota2((tq, LANES), 1)
    ones_lane = (MLA_V, 0)

    def chunk(j, carry, masked):
        start = pl.multiple_of(j * tq, tq)
        out = []
        for hh in range(2):
            m, acc = carry[hh]
            q = q_ref[:, hh * LANES:(hh + 1) * LANES]
            kc = k_ref[pl.ds(start, tq), hh * LANES:(hh + 1) * LANES]
            vc = v_ref[pl.ds(start, tq), hh * LANES:(hh + 1) * LANES]
            vc = jnp.where(lane == ones_lane[hh], jnp.ones_like(vc), vc)
            s = _dot_nt(q, kc)
            if masked:
                s = jnp.where(_iota2(s.shape, 0) >= _iota2(s.shape, 1), s, neg)
            m_new = jnp.maximum(m, jnp.max(s, axis=-1, keepdims=True))
            alpha = jnp.exp(m - m_new)
            p = jnp.exp((s - m_new).astype(BF16))
            acc = alpha * acc + _dot(p, vc)
            out.append((m_new, acc))
        return tuple(out)

    one = (jnp.full((tq, 1), neg, F32), jnp.zeros((tq, LANES), F32))
    carry = lax.fori_loop(0, i, lambda j, c: chunk(j, c, False), (one, one))
    (_, acc0), (_, acc1) = chunk(i, carry, True)
    o0 = acc0 / _lane_bcast(acc0, ones_lane[0])
    o1 = acc1 / _lane_bcast(acc1, ones_lane[1])
    o_ref[...] = jnp.where(lane < MLA_V, o0, o1).astype(o_ref.dtype)


def _mla_attn(q, k, v, batch, seq, tq=512):
    tq = min(tq, seq)
    nq = seq // tq
    pairs = MLA_H // 2
    return pl.pallas_call(
        functools.partial(_mla_attn_kernel, tq=tq),
        grid=(batch, pairs, nq),
        in_specs=[pl.BlockSpec((tq, 2 * LANES), lambda b, p, i: (b * nq + i, p)),
                  pl.BlockSpec((seq, 2 * LANES), lambda b, p, i: (b, p)),
                  pl.BlockSpec((seq, 2 * LANES), lambda b, p, i: (b, p))],
        out_specs=pl.BlockSpec((tq, LANES), lambda b, p, i: (b * nq + i, p)),
        out_shape=jax.ShapeDtypeStruct((batch * seq, pairs * LANES), BF16),
        compiler_params=_cparams(("arbitrary", "arbitrary", "arbitrary")),
        name="mla_attn",
    )(q, k, v)


def _unit_lower_inverse_many(ns):
    c = ns[0].shape[0]
    eye = (_iota2((c, c), 0) == _iota2((c, c), 1)).astype(F32)
    xs = [-n for n in ns]
    ps = [eye + x for x in xs]
    xb = [x.astype(BF16) for x in xs]
    for _ in range(int(math.log2(c)) - 1):
        xs = [_dot(b, b) for b in xb]
        xb = [x.astype(BF16) for x in xs]
        ps = [p + _dot(p.astype(BF16), b) for p, b in zip(ps, xb)]
    return ps


def _gdn_kernel(act_ref, g_ref, z_ref, al_ref, dt_ref, on_ref, o_ref, st_ref):
    c = GDN_CHUNK
    hd = GDN_DK
    nqk = GDN_H * GDN_DK

    @pl.when(pl.program_id(1) == 0)
    def _():
        st_ref[...] = jnp.zeros(st_ref.shape, F32)

    tri = (_iota2((c, c), 0) >= _iota2((c, c), 1)).astype(F32)
    row_ge = _iota2((c, c), 0) >= _iota2((c, c), 1)
    row_gt = _iota2((c, c), 0) > _iota2((c, c), 1)
    lane = _iota2((c, LANES), 1)

    seqs = []
    for bb in range(act_ref.shape[0]):
        gates = g_ref[bb]
        g_all = -jnp.exp(al_ref[...]) * _softplus(gates + dt_ref[...])
        gc_all = _dot_sel(tri, g_all)
        seqs.append(dict(beta_all=_sigmoid(gates), gc_all=gc_all, gc_parts=_split3(gc_all)))
    units = []
    for bb, sq in enumerate(seqs):
        for h in range(GDN_H):
            q = act_ref[bb, :, h * hd:(h + 1) * hd].astype(F32)
            k = act_ref[bb, :, nqk + h * hd:nqk + (h + 1) * hd].astype(F32)
            v = act_ref[bb, :, 2 * nqk + h * GDN_DV:2 * nqk + (h + 1) * GDN_DV].astype(F32)
            q = q * lax.rsqrt(jnp.sum(q * q, axis=-1, keepdims=True) + EPS) * (GDN_DK ** -0.5)
            k = k * lax.rsqrt(jnp.sum(k * k, axis=-1, keepdims=True) + EPS)
            beta = _lane_bcast(sq["beta_all"], h)
            gcol = _lane_bcast(sq["gc_all"], GDN_H + h)
            units.append(dict(bb=bb, h=h, q=q, k=k, v=v, beta=beta, gcol=gcol, kb=k * beta, parts=sq["gc_parts"]))
    for u in units:
        pick = (lane == GDN_H + u["h"]).astype(BF16)
        p0, p1, p2 = u["parts"]
        u["grow"] = _dot_nt(pick, p0) + (_dot_nt(pick, p1) + _dot_nt(pick, p2))
        u["kk"] = _dot3(u["kb"], u["k"], _dot_nt)
        u["qk"] = _dot_nt(u["q"].astype(BF16), u["k"].astype(BF16))
    for u in units:
        gcol = u["gcol"]
        decay = jnp.exp(jnp.where(row_ge, gcol[:, :c] - u["grow"], -jnp.inf))
        eg = jnp.exp(gcol)
        glast = gcol[c - 1:c, :]
        u["lower"] = jnp.where(row_gt, u["kk"] * decay, 0.0)
        u["rhs"] = jnp.concatenate([u["v"] * u["beta"], u["kb"] * eg], axis=1)
        u["attn"] = u["qk"] * decay
        u["qg"] = (u["q"] * eg).astype(BF16)
        u["kg"] = (u["k"] * jnp.exp(glast - gcol)).astype(BF16)
        u["gl"] = jnp.exp(glast)

    tinvs = _unit_lower_inverse_many([u["lower"] for u in units])
    uws = []
    for u, tinv in zip(units, tinvs):
        uws.append(_dot(tinv.astype(BF16), u["rhs"].astype(BF16)))
    states = [st_ref[u["bb"], u["h"]] for u in units]
    sbs = [s.astype(BF16) for s in states]
    vnews = [(uw[:, :GDN_DV] - _dot(uw[:, GDN_DV:].astype(BF16), sb)).astype(BF16) for uw, sb in zip(uws, sbs)]
    for u, state, sb, vnb in zip(units, states, sbs, vnews):
        bb, h = u["bb"], u["h"]
        o = _dot(u["qg"], sb) + _dot(u["attn"].astype(BF16), vnb)
        st_ref[bb, h] = state * u["gl"] + _dot_tn(u["kg"], vnb)
        o = _rms(o, on_ref[...]) * _silu(z_ref[bb, :, h * GDN_DV:(h + 1) * GDN_DV].astype(F32))
        o_ref[bb, :, h * GDN_DV:(h + 1) * GDN_DV] = o.astype(o_ref.dtype)


def _gdn(act, gates, z, a_row, dt_row, o_norm, batch, seq):
    c = GDN_CHUNK
    nc = seq // c
    w3 = act.shape[1]
    wo = GDN_H * GDN_DV
    nb = min(GDN_SEQS_PER_STEP, batch)
    row = lambda b, i: (b, i, 0)
    fix = lambda b, i: (0, 0)
    out = pl.pallas_call(
        _gdn_kernel,
        grid=(batch // nb, nc),
        in_specs=[pl.BlockSpec((nb, c, w3), row), pl.BlockSpec((nb, c, LANES), row), pl.BlockSpec((nb, c, wo), row),
                  pl.BlockSpec((1, LANES), fix), pl.BlockSpec((1, LANES), fix), pl.BlockSpec((1, GDN_DV), fix)],
        out_specs=pl.BlockSpec((nb, c, wo), row),
        out_shape=jax.ShapeDtypeStruct((batch, seq, wo), BF16),
        scratch_shapes=[pltpu.VMEM((nb, GDN_H, GDN_DK, GDN_DV), F32)],
        compiler_params=_cparams(("arbitrary", "arbitrary")),
        name="gdn",
    )(act.reshape(batch, seq, w3), gates.reshape(batch, seq, LANES), z.reshape(batch, seq, wo), a_row, dt_row, o_norm)
    return out.reshape(batch * seq, wo)


def _mlstm_kernel(q_ref, k_ref, v_ref, og_ref, g_ref, bias_ref, nrm_ref, o_ref, c_ref, n_ref, m_ref):
    @pl.when(pl.program_id(1) == 0)
    def _():
        c_ref[...] = jnp.zeros(c_ref.shape, F32)
        n_ref[...] = jnp.zeros(n_ref.shape, F32)
        m_ref[...] = jnp.zeros(m_ref.shape, F32)

    c = ML_CHUNK
    tri = (_iota2((c, c), 0) >= _iota2((c, c), 1)).astype(F32)
    row_ge = _iota2((c, c), 0) >= _iota2((c, c), 1)
    ones = jnp.ones((c, LANES), F32)
    lane = _iota2((c, LANES), 1)

    units = []
    for bb in range(q_ref.shape[0]):
        pre = g_ref[bb] + bias_ref[...]
        logf = jnp.minimum(pre, 0.0) - jnp.log(1.0 + jnp.exp(-jnp.abs(pre)))
        bcum_all = _dot_sel(tri, logf)
        for h in range(ML_H):
            q = q_ref[bb, :, h * LANES:(h + 1) * LANES].astype(F32)
            k = k_ref[bb, :, h * LANES:(h + 1) * LANES].astype(F32) * (ML_DK ** -0.5)
            units.append(dict(bb=bb, h=h, q=q, k=k, qb=q.astype(BF16), vb=v_ref[bb, :, h * ML_DV:(h + 1) * ML_DV].astype(BF16),
                              bcol=_lane_bcast(bcum_all, ML_H + h),
                              icol=_lane_bcast(pre, h),
                              col=jnp.where(lane == h, pre, 0.0) - jnp.where(lane == ML_H + h, bcum_all, 0.0),
                              m_st=m_ref[bb, h], cst=c_ref[bb, h], nst=n_ref[bb, h]))
    for u in units:
        u["row"] = _dot_sel(ones, u["col"], _dot_nt)
        u["qk"] = _dot_nt(u["qb"], u["k"].astype(BF16))
        u["qc"] = _dot(u["qb"], u["cst"].astype(BF16))
    for u in units:
        u["d"] = jnp.where(row_ge, u["bcol"][:, :c] + u["row"], -jnp.inf)
        u["inter"] = u["bcol"] + u["m_st"]
        u["m_t"] = jnp.maximum(u["inter"], jnp.max(u["d"], axis=-1, keepdims=True))
        u["b_end"] = u["bcol"][c - 1:c, :]
        u["a"] = u["b_end"] - u["bcol"] + u["icol"]
        u["m_new"] = jnp.maximum(u["b_end"] + u["m_st"], jnp.max(u["a"], axis=0, keepdims=True))
    for u in units:
        u["w_inter"] = jnp.exp(u["inter"] - u["m_t"])
        u["p"] = jnp.exp(u["d"] - u["m_t"][:, :c]) * u["qk"]
        u["keep"] = jnp.exp(u["b_end"] + u["m_st"] - u["m_new"])
        u["ks"] = u["k"] * jnp.exp(u["a"] - u["m_new"])
    for u in units:
        u["pv"] = _dot(u["p"].astype(BF16), u["vb"])
        u["kv"] = _dot_tn(u["ks"].astype(BF16), u["vb"])
    for u in units:
        u["den"] = (u["w_inter"] * jnp.sum(u["q"] * u["nst"], axis=-1, keepdims=True)
                    + jnp.sum(u["p"], axis=-1, keepdims=True))
    for u in units:
        bb, h = u["bb"], u["h"]
        num = u["w_inter"] * u["qc"] + u["pv"]
        hc = num / jnp.maximum(jnp.abs(u["den"]), jnp.exp(-u["m_t"]))
        c_ref[bb, h] = u["cst"] * u["keep"] + u["kv"]
        n_ref[bb, h] = u["nst"] * u["keep"] + jnp.sum(u["ks"], axis=0, keepdims=True)
        m_ref[bb, h] = u["m_new"]
        hn = (_rms(hc, nrm_ref[:, h * ML_DV:(h + 1) * ML_DV])
              * _sigmoid(og_ref[bb, :, h * ML_DV:(h + 1) * ML_DV].astype(F32)))
        o_ref[bb, :, h * ML_DV:(h + 1) * ML_DV] = hn.astype(o_ref.dtype)


def _mlstm(mq, mk, mv, mo, gates, bias_row, norm_row, batch, seq):
    c = ML_CHUNK
    nc = seq // c
    nb = min(MLSTM_SEQS_PER_STEP, batch)
    row = lambda b, i: (b, i, 0)
    fix = lambda b, i: (0, 0)
    wide = ML_H * LANES
    r3 = lambda a: a.reshape(batch, seq, a.shape[-1])
    out = pl.pallas_call(
        _mlstm_kernel,
        grid=(batch // nb, nc),
        in_specs=[pl.BlockSpec((nb, c, wide), row), pl.BlockSpec((nb, c, wide), row), pl.BlockSpec((nb, c, wide), row),
                  pl.BlockSpec((nb, c, wide), row), pl.BlockSpec((nb, c, LANES), row),
                  pl.BlockSpec((1, LANES), fix), pl.BlockSpec((1, wide), fix)],
        out_specs=pl.BlockSpec((nb, c, wide), row),
        out_shape=jax.ShapeDtypeStruct((batch, seq, wide), BF16),
        scratch_shapes=[pltpu.VMEM((nb, ML_H, LANES, ML_DV), F32), pltpu.VMEM((nb, ML_H, 1, LANES), F32),
                        pltpu.VMEM((nb, ML_H, 1, LANES), F32)],
        compiler_params=_cparams(("arbitrary", "arbitrary")),
        name="mlstm",
    )(r3(mq), r3(mk), r3(mv), r3(mo), r3(gates), bias_row, norm_row)
    return out.reshape(batch * seq, wide)


def _swa_kernel(q_ref, kc_ref, kp_ref, vc_ref, vp_ref, sink_ref, o_ref):
    w = WINDOW
    n = pl.program_id(1)
    scale = SWA_D ** -0.5
    qi = _iota2((w, w), 0)
    kj = _iota2((w, w), 1)
    mask_c = kj <= qi
    mask_p = jnp.logical_and(kj > qi, n > 0)
    grp = SWA_H // SWA_KV
    neg = -1e30
    units = [(bb, h) for bb in range(q_ref.shape[0]) for h in range(SWA_H)]
    scores = []
    half_of_lane = _iota2((w, LANES), 1) // SWA_D
    for bb, h in units:
        g = h // grp
        pair = q_ref[bb, :, (h // 2) * LANES:(h // 2 + 1) * LANES]
        q = jnp.where(half_of_lane == h % 2, pair, jnp.zeros_like(pair))
        scores.append((_dot_nt(q, kc_ref[bb, :, g * LANES:(g + 1) * LANES]),
                       _dot_nt(q, kp_ref[bb, :, g * LANES:(g + 1) * LANES])))
    masked, tops, exps, dens, probs = [], [], [], [], {}
    for sc, sp in scores:
        masked.append((jnp.where(mask_c, sc * scale, neg), jnp.where(mask_p, sp * scale, neg)))
    for (bb, h), (s_c, s_p) in zip(units, masked):
        tops.append(jnp.maximum(jnp.max(jnp.maximum(s_c, s_p), axis=-1, keepdims=True), sink_ref[:, h:h + 1]))
    for (s_c, s_p), m in zip(masked, tops):
        exps.append((jnp.where(mask_c, jnp.exp(s_c - m), 0.0), jnp.where(mask_p, jnp.exp(s_p - m), 0.0)))
    ones_b = jnp.ones((w, LANES), BF16)
    for (bb, h), (p_c, p_p), m in zip(units, exps, tops):
        p_c, p_p = p_c.astype(BF16), p_p.astype(BF16)
        probs[bb, h] = (p_c, p_p)
        dens.append(_dot(p_c, ones_b) + _dot(p_p, ones_b) + jnp.exp(sink_ref[:, h:h + 1] - m))
    inv = {u: 1.0 / den for u, den in zip(units, dens)}
    for bb in range(q_ref.shape[0]):
        for pair in range(SWA_H // 2):
            acc = None
            for sub in range(2):
                h = 2 * pair + sub
                vcol = (2 * (h // grp) + sub) * LANES
                p_c, p_p = probs[bb, h]
                part = (_dot(p_c, vc_ref[bb, :, vcol:vcol + LANES]) + _dot(p_p, vp_ref[bb, :, vcol:vcol + LANES])) * inv[bb, h]
                acc = part if acc is None else acc + part
            o_ref[bb, :, pair * LANES:(pair + 1) * LANES] = acc.astype(o_ref.dtype)


def _swa(sq, sk, sv, sinks_row, batch, seq):
    w = WINDOW
    nb = seq // w
    ns = min(SWA_SEQS_PER_STEP, batch)
    wo = SWA_H * SWA_D
    cur = lambda b, n: (b, n, 0)
    prev = lambda b, n: (b, jnp.maximum(n - 1, 0), 0)
    r3 = lambda a: a.reshape(batch, seq, a.shape[-1])
    q3, k3, v3 = r3(sq), r3(sk), r3(sv)
    out = pl.pallas_call(
        _swa_kernel,
        grid=(batch // ns, nb),
        in_specs=[pl.BlockSpec((ns, w, sq.shape[1]), cur),
                  pl.BlockSpec((ns, w, sk.shape[1]), cur), pl.BlockSpec((ns, w, sk.shape[1]), prev),
                  pl.BlockSpec((ns, w, sv.shape[1]), cur), pl.BlockSpec((ns, w, sv.shape[1]), prev),
                  pl.BlockSpec((1, LANES), lambda b, n: (0, 0))],
        out_specs=pl.BlockSpec((ns, w, wo), cur),
        out_shape=jax.ShapeDtypeStruct((batch, seq, wo), BF16),
        compiler_params=_cparams(("arbitrary", "arbitrary")),
        name="swa",
    )(q3, k3, k3, v3, v3, sinks_row)
    return out.reshape(batch * seq, wo)


def _layer_norm(h, g, b):
    mu = jnp.mean(h, axis=-1, keepdims=True)
    d = h - mu
    var = jnp.mean(d * d, axis=-1, keepdims=True)
    return d * lax.rsqrt(var + LN_EPS) * g + b


ROUTE_ROWS = 512


def _outproj_kernel(x_ref, a1_ref, a2_ref, w_ref, g_ref, b_ref, wt_ref, bias_ref,
                    o_ref, op_ref, idx_ref, gate_ref, rank_ref, cnt_ref, carry_ref):
    tm = x_ref.shape[0]
    k1 = a1_ref.shape[1]
    rp = min(ROUTE_ROWS, tm)

    @pl.when(pl.program_id(0) == 0)
    def _():
        carry_ref[...] = jnp.zeros(carry_ref.shape, F32)

    for p in range(tm // rp):
        r = slice(p * rp, (p + 1) * rp)
        y = _dot(a1_ref[r, :].astype(BF16), w_ref[0:k1, :]) + _dot(a2_ref[r, :].astype(BF16), w_ref[k1:, :])
        h = _layer_norm(DN_ALPHA * x_ref[r, :] + y, g_ref[...], b_ref[...])
        o_ref[r, :] = h
        op_ref[r, :] = _pack_pairs(h)
    for p in range(tm // rp):
        r = slice(p * rp, (p + 1) * rp)
        idx, gate, rank = _route_tile(o_ref[r, :], wt_ref, bias_ref, carry_ref)
        idx_ref[:, r] = idx
        eye = (_iota2((TOP_K, LANES), 0) == _iota2((TOP_K, LANES), 1)).astype(F32)
        gate_ref[r, :] = _dot_sel(eye, gate, dot=lambda s, p: _dot_tn(p, s))[:, 0:TOP_K]
        rank_ref[:, r] = rank
    cnt_ref[...] = carry_ref[...]


def _outproj_ln_route(x, a1, a2, w, g, b, wt, bias_col, tm=1024):
    t, d = x.shape
    tm = min(tm, t)
    row = lambda i: (i, 0)
    col = lambda i: (0, i)
    fix = lambda i: (0, 0)
    return pl.pallas_call(
        _outproj_kernel,
        grid=(t // tm,),
        in_specs=[pl.BlockSpec((tm, d), row), pl.BlockSpec((tm, a1.shape[1]), row), pl.BlockSpec((tm, a2.shape[1]), row),
                  pl.BlockSpec(w.shape, fix), pl.BlockSpec((1, d), fix), pl.BlockSpec((1, d), fix),
                  pl.BlockSpec(wt.shape, fix), pl.BlockSpec((N_EXPERTS, LANES), fix)],
        out_specs=[pl.BlockSpec((tm, d), row), pl.BlockSpec((tm, d // 2), row),
                   pl.BlockSpec((TOP_K, tm), col), pl.BlockSpec((tm, TOP_K), row), pl.BlockSpec((TOP_K, tm), col),
                   pl.BlockSpec((N_EXPERTS, LANES), fix)],
        out_shape=[jax.ShapeDtypeStruct((t, d), F32), jax.ShapeDtypeStruct((t, d // 2), jnp.uint32),
                   jax.ShapeDtypeStruct((TOP_K, t), jnp.int32), jax.ShapeDtypeStruct((t, TOP_K), F32),
                   jax.ShapeDtypeStruct((TOP_K, t), jnp.int32), jax.ShapeDtypeStruct((N_EXPERTS, LANES), F32)],
        scratch_shapes=[pltpu.VMEM((N_EXPERTS, LANES), F32)],
        compiler_params=_cparams(("arbitrary",)),
        name="outproj_ln_route",
    )(x, a1, a2, w, g, b, wt, bias_col)


def _first_index(x, m, iota_f, sentinel):
    return jnp.min(jnp.where(x == m, iota_f, sentinel), axis=0, keepdims=True)


def _route_tile(x, wt_ref, bias_ref, carry_ref):
    tm = x.shape[0]
    e = N_EXPERTS
    gs = e // N_GROUPS
    ninf = -jnp.inf

    logits = _dot3(wt_ref[...], x, _dot_nt)
    scores = _sigmoid(logits)
    sel = scores + bias_ref[:, 0:1]

    sub_f = _iota2((gs, tm), 0).astype(F32)
    gscore = []
    for g in range(N_GROUPS):
        blk = sel[g * gs:(g + 1) * gs, :]
        m1 = jnp.max(blk, axis=0, keepdims=True)
        i1 = _first_index(blk, m1, sub_f, float(gs))
        m2 = jnp.max(jnp.where(sub_f == i1, ninf, blk), axis=0, keepdims=True)
        gscore.append(m1 + m2)
    gsc = jnp.concatenate(gscore, axis=0)
    grp_f = _iota2((N_GROUPS, tm), 0).astype(F32)
    gmask = jnp.zeros((N_GROUPS, tm), F32)
    for _ in range(TOPK_GROUPS):
        m = jnp.max(gsc, axis=0, keepdims=True)
        gi = _first_index(gsc, m, grp_f, float(N_GROUPS))
        hit = grp_f == gi
        gmask = jnp.where(hit, 1.0, gmask)
        gsc = jnp.where(hit, ninf, gsc)
    masked = jnp.concatenate(
        [jnp.where(gmask[g:g + 1, :] > 0.0, sel[g * gs:(g + 1) * gs, :], ninf) for g in range(N_GROUPS)], axis=0)

    exp_f = _iota2((e, tm), 0).astype(F32)
    chosen = jnp.zeros((e, tm), F32)
    idxs, gates = [], []
    for _ in range(TOP_K):
        m = jnp.max(masked, axis=0, keepdims=True)
        ei = _first_index(masked, m, exp_f, float(e))
        hit = exp_f == ei
        idxs.append(ei)
        gates.append(jnp.sum(jnp.where(hit, scores, 0.0), axis=0, keepdims=True))
        chosen = jnp.where(hit, 1.0, chosen)
        masked = jnp.where(hit, ninf, masked)
    gate = jnp.concatenate(gates, axis=0)
    gate = gate / jnp.sum(gate, axis=0, keepdims=True) * ROUTED_SCALE
    idx_f = jnp.concatenate(idxs, axis=0)

    upper = (_iota2((tm, tm), 0) < _iota2((tm, tm), 1)).astype(BF16)
    before = _dot(chosen.astype(BF16), upper) + carry_ref[...][:, 0:1]
    ranks = [jnp.sum(jnp.where(exp_f == idxs[k], before, 0.0), axis=0, keepdims=True) for k in range(TOP_K)]
    carry_ref[...] = carry_ref[...] + jnp.sum(chosen, axis=1, keepdims=True)
    return idx_f.astype(jnp.int32), gate, jnp.concatenate(ranks, axis=0).astype(jnp.int32)


def _dest_kernel(idx_ref, rank_ref, start_ref, dest_ref):
    tm = idx_ref.shape[1]
    exp_i = _iota2((N_EXPERTS, tm), 0)
    start = start_ref[:, 0:1]
    rows = [jnp.sum(jnp.where(exp_i == idx_ref[s:s + 1, :], start, 0.0), axis=0, keepdims=True) for s in range(TOP_K)]
    dest_ref[...] = jnp.concatenate(rows, axis=0).astype(jnp.int32) + rank_ref[...]


def _dest_rows(idx, rank, start_col, tm=2048):
    t = idx.shape[1]
    tm = min(tm, t)
    col = lambda i: (0, i)
    return pl.pallas_call(
        _dest_kernel,
        grid=(t // tm,),
        in_specs=[pl.BlockSpec((TOP_K, tm), col), pl.BlockSpec((TOP_K, tm), col),
                  pl.BlockSpec((N_EXPERTS, LANES), lambda i: (0, 0))],
        out_specs=pl.BlockSpec((TOP_K, tm), col),
        out_shape=jax.ShapeDtypeStruct((TOP_K, t), jnp.int32),
        compiler_params=_cparams(("arbitrary",)),
        name="moe_dest",
    )(idx, rank, start_col)


def _pack_pairs(x):
    n = x.shape[1] // 2
    hi = lax.bitcast_convert_type(x[:, :n].astype(BF16).astype(F32), jnp.uint32)
    lo = lax.bitcast_convert_type(x[:, n:].astype(BF16).astype(F32), jnp.uint32)
    return hi | (lo >> 16)


def _unpack_pairs(w):
    hi = lax.bitcast_convert_type(w & jnp.uint32(0xFFFF0000), F32)
    lo = lax.bitcast_convert_type(w << 16, F32)
    return hi, lo


def _sc_scatter_rows(xp, dest, rows, chunk=LANES):
    t, width = xp.shape
    info = plsc.get_sparse_core_info()
    ncores, nsub = info.num_cores, info.num_subcores
    per_worker = t // (ncores * nsub)
    nchunk = per_worker // chunk
    mesh = plsc.VectorSubcoreMesh(core_axis_name="c", subcore_axis_name="s")

    @functools.partial(
        pl.kernel, mesh=mesh,
        out_type=jax.ShapeDtypeStruct((rows, width), xp.dtype),
        scratch_types=[pltpu.VMEM((TOP_K, chunk), jnp.int32), pltpu.VMEM((chunk, width), xp.dtype), pltpu.SemaphoreType.DMA],
    )
    def scatter(xp_hbm, dest_hbm, out_hbm, idx_v, rows_v, sem):
        base = (lax.axis_index("s") * ncores + lax.axis_index("c")) * per_worker

        @pl.loop(0, nchunk)
        def _(i):
            off = pl.multiple_of(base + i * chunk, chunk)
            pltpu.sync_copy(dest_hbm.at[:, pl.ds(off, chunk)], idx_v)
            pltpu.sync_copy(xp_hbm.at[pl.ds(off, chunk)], rows_v)
            copies = [pltpu.async_copy(rows_v, out_hbm.at[idx_v.at[s]], sem) for s in range(TOP_K)]
            for cp in copies:
                cp.wait()

    return scatter(xp, dest)


def _experts_kernel(be_ref, nu_ref, nv_ref, first_ref, slot_ref, nxt_ref, xs_ref, wg_hbm, wu_hbm, wd_hbm, ys_ref,
                    wgf_ref, wuf_ref, wdf_ref, wgb_ref, wub_ref, wdb_ref, sem, *, layer):
    i = pl.program_id(0)

    def fetch(e, s):
        return [pltpu.make_async_copy(wg_hbm.at[layer, e], wgf_ref.at[s], sem.at[s]),
                pltpu.make_async_copy(wu_hbm.at[layer, e], wuf_ref.at[s], sem.at[s]),
                pltpu.make_async_copy(wd_hbm.at[layer, e], wdf_ref.at[s], sem.at[s])]

    @pl.when(i == 0)
    def _():
        for cp in fetch(be_ref[0], 0):
            cp.start()

    @pl.when(jnp.logical_and(first_ref[i] == 1, i < nu_ref[0]))
    def _():
        s = slot_ref[i]
        for cp in fetch(be_ref[i], s):
            cp.wait()
        wgb_ref[...] = wgf_ref[s].astype(BF16)
        wub_ref[...] = wuf_ref[s].astype(BF16)
        wdb_ref[...] = wdf_ref[s].astype(BF16)

        @pl.when(nxt_ref[i] >= 0)
        def _():
            for cp in fetch(nxt_ref[i], 1 - s):
                cp.start()

    @pl.when(i < nu_ref[0])
    def _():
        sub = xs_ref.shape[0] // EXPERT_SUBBLOCKS
        acts = []
        for r in range(EXPERT_SUBBLOCKS):
            rows = pl.ds(r * sub, sub)
            live = (_iota2((sub, 1), 0) + r * sub) < nv_ref[i]
            xa, xb = _unpack_pairs(jnp.where(live, xs_ref[rows, :], jnp.uint32(0)))
            x = jnp.concatenate([xa.astype(BF16), xb.astype(BF16)], axis=1)
            acts.append((_dot(x, wgb_ref[...]), _dot(x, wub_ref[...])))
        outs = [_dot((_silu(gate) * up).astype(BF16), wdb_ref[...]) for gate, up in acts]
        for r, y in enumerate(outs):
            ys_ref[pl.ds(r * sub, sub), :] = _pack_pairs(y)


def _experts(block_e, n_used, n_valid, xs, wg, wu, wd, layer, block):
    rows, half = xs.shape
    d = 2 * half
    nb = rows // block
    pos = jnp.arange(nb, dtype=jnp.int32)
    first = jnp.concatenate([jnp.ones((1,), jnp.int32), (block_e[1:] != block_e[:-1]).astype(jnp.int32)])
    slot = (jnp.cumsum(first) - 1) % 2
    later = (pos[None, :] > pos[:, None]) & (block_e[None, :] != block_e[:, None]) & (pos[None, :] < n_used[0])
    nxt_pos = jnp.min(jnp.where(later, pos[None, :], nb), axis=1)
    nxt = jnp.where(nxt_pos < nb, block_e[jnp.minimum(nxt_pos, nb - 1)], -1)
    blk = lambda i, be, nu, *rest: (jnp.minimum(i, nu[0] - 1), 0)
    hbm = pl.BlockSpec(memory_space=pl.ANY)
    return pl.pallas_call(
        functools.partial(_experts_kernel, layer=layer),
        grid_spec=pltpu.PrefetchScalarGridSpec(
            num_scalar_prefetch=6,
            grid=(nb,),
            in_specs=[pl.BlockSpec((block, half), blk), hbm, hbm, hbm],
            out_specs=pl.BlockSpec((block, half), blk),
            scratch_shapes=[pltpu.VMEM((2, d, D_EXPERT), F32), pltpu.VMEM((2, d, D_EXPERT), F32),
                            pltpu.VMEM((2, D_EXPERT, d), F32),
                            pltpu.VMEM((d, D_EXPERT), BF16), pltpu.VMEM((d, D_EXPERT), BF16),
                            pltpu.VMEM((D_EXPERT, d), BF16), pltpu.SemaphoreType.DMA((2,))],
        ),
        out_shape=jax.ShapeDtypeStruct((rows, half), jnp.uint32),
        compiler_params=_cparams(("arbitrary",)),
        name="moe_experts",
    )(block_e, n_used, n_valid, first, slot.astype(jnp.int32), nxt.astype(jnp.int32), xs, wg, wu, wd)


def _sc_gather_rows(table, idx, chunk=SC_CHUNK):
    n = idx.shape[0]
    width = table.shape[1]
    info = plsc.get_sparse_core_info()
    ncores, nsub = info.num_cores, info.num_subcores
    per_worker = n // (ncores * nsub)
    nchunk = per_worker // chunk
    mesh = plsc.VectorSubcoreMesh(core_axis_name="c", subcore_axis_name="s")

    @functools.partial(
        pl.kernel, mesh=mesh,
        out_type=jax.ShapeDtypeStruct((n, width), table.dtype),
        scratch_types=[pltpu.VMEM((nchunk, chunk), jnp.int32), pltpu.VMEM((2, chunk, width), table.dtype),
                       pltpu.SemaphoreType.DMA((2,)), pltpu.SemaphoreType.DMA((2,))],
    )
    def gather(table_hbm, idx_hbm, out_hbm, idx_v, rows_v, gsem, wsem):
        wid = lax.axis_index("s") * ncores + lax.axis_index("c")
        base = wid * per_worker
        pltpu.sync_copy(idx_hbm.at[pl.ds(wid * nchunk, nchunk)], idx_v)

        def fetch(j, b):
            return pltpu.make_async_copy(table_hbm.at[idx_v.at[j]], rows_v.at[b], gsem.at[b])

        def flush(j, b):
            off = pl.multiple_of(base + j * chunk, chunk)
            return pltpu.make_async_copy(rows_v.at[b], out_hbm.at[pl.ds(off, chunk)], wsem.at[b])

        fetch(0, 0).start()

        @pl.loop(0, nchunk, step=2)
        def _(i):
            for b in range(2):
                j = i + b
                fetch(j, b).wait()

                @pl.when(j + 1 < nchunk)
                def _():
                    @pl.when(j >= 1)
                    def _():
                        flush(j - 1, 1 - b).wait()

                    fetch(j + 1, 1 - b).start()

                flush(j, b).start()

        flush(nchunk - 2, 0).wait()
        flush(nchunk - 1, 1).wait()

    return gather(table, idx.reshape(n // chunk, chunk))


def _shared_kernel(xp_ref, sg_ref, su_ref, sd_ref, o_ref):
    xa, xb = _unpack_pairs(xp_ref[...])
    x = jnp.concatenate([xa.astype(BF16), xb.astype(BF16)], axis=1)
    hs = _silu(_dot(x, sg_ref[...])) * _dot(x, su_ref[...])
    o_ref[...] = _pack_pairs(_dot(hs.astype(BF16), sd_ref[...]))


def _shared_expert(xp, sg, su, sd, tm=512):
    t, half = xp.shape
    row = lambda i: (i, 0)
    fix = lambda i: (0, 0)
    return pl.pallas_call(
        _shared_kernel,
        grid=(t // tm,),
        in_specs=[pl.BlockSpec((tm, half), row), pl.BlockSpec(sg.shape, fix), pl.BlockSpec(su.shape, fix),
                  pl.BlockSpec(sd.shape, fix)],
        out_specs=pl.BlockSpec((tm, half), row),
        out_shape=jax.ShapeDtypeStruct((t, half), jnp.uint32),
        compiler_params=_cparams(("arbitrary",)),
        name="moe_shared",
    )(xp, sg, su, sd)


def _combine_kernel(x_ref, gate_ref, rows_ref, sh_ref, g_ref, b_ref, o_ref):
    o_ref[...] = _combined(x_ref, gate_ref, rows_ref, sh_ref, g_ref, b_ref)


def _combine(x, gate_t, rows, shared, g, b, tm=512):
    t, d = x.shape
    row = lambda i: (i, 0)
    return pl.pallas_call(
        _combine_kernel,
        grid=(t // tm,),
        in_specs=_stream_specs((x, gate_t, rows, shared, g, b), tm),
        out_specs=pl.BlockSpec((tm, d), row),
        out_shape=jax.ShapeDtypeStruct((t, d), F32),
        compiler_params=_cparams(("arbitrary",)),
        name="moe_combine",
    )(x, gate_t, rows, shared, g, b)


def _take_cols(w, idx):
    idx = np.asarray(idx)
    runs, start = [], 0
    for pos in range(1, len(idx) + 1):
        run_ends = pos == len(idx) or (idx[pos] != idx[pos - 1] + 1 if idx[pos - 1] >= 0 else idx[pos] >= 0)
        if run_ends:
            runs.append((start, int(idx[start]), pos - start))
            start = pos

    def body(w_ref, o_ref):
        for dst, src, width in runs:
            if src < 0:
                o_ref[:, dst:dst + width] = jnp.zeros((o_ref.shape[0], width), o_ref.dtype)
            else:
                o_ref[:, dst:dst + width] = w_ref[:, src:src + width].astype(o_ref.dtype)

    rows = w.shape[0]
    tr = min(rows, 256)
    return pl.pallas_call(
        body,
        grid=(rows // tr,),
        in_specs=[pl.BlockSpec((tr, w.shape[1]), lambda i: (i, 0))],
        out_specs=pl.BlockSpec((tr, len(idx)), lambda i: (i, 0)),
        out_shape=jax.ShapeDtypeStruct((rows, len(idx)), BF16),
        compiler_params=_cparams(("arbitrary",)),
        name="weight_cols",
    )(w)


def _pad_lane_row(v, first_lane, width=LANES):
    out = jnp.zeros((1, width), F32)
    return lax.dynamic_update_slice(out, v.reshape(1, -1).astype(F32), (0, first_lane))


def _even_in_cols():
    z = lambda n: -np.ones(n, int)
    kr0 = Q_LORA + KV_LORA
    cols = [np.arange(0, Q_LORA), np.arange(Q_LORA, Q_LORA + KV_LORA),
            z(MLA_NOPE), np.arange(kr0, kr0 + MLA_ROPE), z(LANES - MLA_NOPE - MLA_ROPE)]
    g0 = kr0 + MLA_ROPE
    nqk = GDN_H * GDN_DK
    cols.append(np.arange(g0, g0 + 3 * nqk))
    zoff = g0 + 3 * nqk + 2 * GDN_H
    cols.append(np.arange(zoff, zoff + GDN_H * GDN_DV))
    cols += [np.arange(g0 + 3 * nqk, g0 + 3 * nqk + 2 * GDN_H), z(LANES - 2 * GDN_H)]
    return np.concatenate(cols)


EV_WIDTHS = (Q_LORA + KV_LORA + LANES, 3 * GDN_H * GDN_DK, GDN_H * GDN_DV, LANES)


def _mla_q_cols():
    per = MLA_NOPE + MLA_ROPE
    half = MLA_ROPE // 2
    main, sw = [], []
    for h in range(MLA_H):
        b = h * per
        main += [np.arange(b, b + per), -np.ones(LANES - per, int)]
        sw += [-np.ones(MLA_NOPE, int), np.arange(b + MLA_NOPE + half, b + per), np.arange(b + MLA_NOPE, b + MLA_NOPE + half),
               -np.ones(LANES - per, int)]
    return np.concatenate(main + sw)


def _mla_kv_cols():
    per = MLA_NOPE + MLA_V
    kc, vc = [], []
    for h in range(MLA_H):
        b = h * per
        kc += [np.arange(b, b + MLA_NOPE), -np.ones(LANES - MLA_NOPE, int)]
        vv = np.arange(b + MLA_NOPE, b + per)
        pad = -np.ones(LANES - MLA_V, int)
        vc += [vv, pad] if h % 2 == 0 else [pad, vv]
    return np.concatenate(kc + vc)


def _odd_in_cols():
    z = lambda n: -np.ones(n, int)
    o = 0
    cols = []
    mq0, mk0 = 0, ML_H * ML_DK
    for base in (mq0, mk0):
        for h in range(ML_H):
            cols += [np.arange(base + h * ML_DK, base + (h + 1) * ML_DK), z(LANES - ML_DK)]
    mv0 = 2 * ML_H * ML_DK
    cols.append(np.arange(mv0, mv0 + ML_H * ML_DV))
    mi0 = mv0 + ML_H * ML_DV
    mo0 = mi0 + 2 * ML_H
    cols.append(np.arange(mo0, mo0 + ML_H * ML_DV))
    cols += [np.arange(mi0, mi0 + 2 * ML_H), z(LANES - 2 * ML_H)]
    sq0 = mo0 + ML_H * ML_DV
    sk0 = sq0 + SWA_H * SWA_D
    sv0 = sk0 + SWA_KV * SWA_D
    half = SWA_D // 2

    cols.append(np.arange(sq0, sq0 + SWA_H * SWA_D))
    for g in range(SWA_KV):
        cols += [np.arange(sk0 + g * SWA_D, sk0 + (g + 1) * SWA_D)] * 2
    for g in range(SWA_KV):
        vv = np.arange(sv0 + g * SWA_D, sv0 + (g + 1) * SWA_D)
        cols += [vv, z(LANES - SWA_D), z(LANES - SWA_D), vv]
    return np.concatenate(cols)


def _even_weights(w_in, w_qb, w_kvb):
    return (_take_cols(w_in, _even_in_cols()), _take_cols(w_qb, _mla_q_cols()), _take_cols(w_kvb, _mla_kv_cols()))


def _even_mixer(stream, tabs, weights, q_norm, kv_norm, conv_w, a_log, dt_bias, o_norm, batch, seq):
    ctab, stab = tabs
    w, wq2, wkv2 = weights
    x, q, k, v, act, z, gates = _proj_even(stream, w, conv_w, ctab, stab, q_norm.reshape(1, -1), kv_norm.reshape(1, -1),
                                           wq2, wkv2, seq)
    o_a = _mla_attn(q, k, v, batch, seq)
    o_b = _gdn(act, gates, z, _pad_lane_row(a_log, GDN_H), _pad_lane_row(dt_bias, GDN_H),
               o_norm.reshape(1, -1), batch, seq)
    return x, o_a, o_b


def _odd_mixer(stream, tabs, w, b_i, b_f, ml_norm, sinks, batch, seq):
    ctab, stab = tabs
    x, mq, mk, mv, mo, mg, sq, sk, sv = _proj_odd(stream, w, ctab, stab)
    bias_row = _pad_lane_row(jnp.concatenate([b_i, b_f]), 0)
    o_c = _mlstm(mq, mk, mv, mo, mg, bias_row, ml_norm.reshape(1, -1), batch, seq)
    o_d = _swa(sq, sk, sv, _pad_lane_row(sinks, 0), batch, seq)
    return x, o_c, o_d


def _moe(x, xp, routing, w_gate, w_up, w_down, layer, s_gate, s_up, s_down, ln_g, ln_b):
    t, d = x.shape
    idx, gate, rank, cnt = routing
    counts = cnt[:, 0].astype(jnp.int32)
    block = int(min(max(pl.next_power_of_2(t * TOP_K // N_EXPERTS) // 2, EXPERT_BLOCK_MIN), EXPERT_BLOCK_MAX))
    padded = (counts + block - 1) // block * block
    pad_end = jnp.cumsum(padded)
    pad_start = pad_end - padded
    start_col = jnp.broadcast_to(pad_start.astype(F32).reshape(-1, 1), (N_EXPERTS, LANES))
    dest = _dest_rows(idx, rank, start_col)
    n_blocks = t * TOP_K // block + N_EXPERTS
    rows = n_blocks * block
    block_row = jnp.arange(n_blocks, dtype=jnp.int32) * block
    block_e = jnp.minimum(jnp.sum((pad_end[None, :] <= block_row[:, None]).astype(jnp.int32), axis=1), N_EXPERTS - 1)
    n_used = (pad_end[-1:] // block).astype(jnp.int32)
    live_end = jnp.sum(jnp.where(block_e[:, None] == jnp.arange(N_EXPERTS, dtype=jnp.int32)[None, :],
                                 (pad_start + counts)[None, :], 0), axis=1)
    n_valid = jnp.clip(live_end - block_row, 0, block).astype(jnp.int32)
    xs = _sc_scatter_rows(xp, dest, rows)
    ys = _experts(block_e, n_used, n_valid, xs, w_gate, w_up, w_down, layer, block)
    picked = _sc_gather_rows(ys, dest.reshape(-1)).reshape(TOP_K, t, d // 2)
    shared = _shared_expert(xp, s_gate.astype(BF16), s_up.astype(BF16), s_down.astype(BF16))
    return (x, gate, picked, shared, ln_g.reshape(1, -1), ln_b.reshape(1, -1))


def kernel(x, positions, ev_w_in, mla_q_norm, mla_w_qb, mla_kv_norm, mla_w_kvb, gdn_conv, gdn_a_log, gdn_dt_bias, gdn_norm, ev_w_out, od_w_in, mlstm_b_i, mlstm_b_f, mlstm_norm, swa_sinks, od_w_out, ln1_g, ln1_b, router_w, router_b, moe_w_gate, moe_w_up, moe_w_down, shared_w_gate, shared_w_up, shared_w_down, ln2_g, ln2_b):
    batch, seq, d = x.shape
    streams = STREAMS if batch % STREAMS == 0 else 1
    sb = batch // streams
    ts = sb * seq
    hs, tabs_m, tabs_s = [], [], []
    for s in range(streams):
        pos = positions[s * sb:(s + 1) * sb].reshape(ts, 1).astype(F32)
        tm_, ts_ = _rope_tables(pos)
        tabs_m.append(tm_)
        tabs_s.append(ts_)
        hs.append((x[s * sb:(s + 1) * sb].reshape(ts, d),))
    for layer in range(DEPTH):
        j = layer // 2
        if layer % 2 == 0:
            weights = _even_weights(ev_w_in[j], mla_w_qb[j], mla_w_kvb[j])
            w_out = ev_w_out[j].astype(BF16)
        else:
            weights = _take_cols(od_w_in[j], _odd_in_cols())
            w_out = od_w_out[j].astype(BF16)
        for s in range(streams):
            if layer % 2 == 0:
                h, a1, a2 = _even_mixer(hs[s], tabs_m[s], weights, mla_q_norm[j], mla_kv_norm[j], gdn_conv[j], gdn_a_log[j],
                                        gdn_dt_bias[j], gdn_norm[j], sb, seq)
            else:
                h, a1, a2 = _odd_mixer(hs[s], tabs_s[s], weights, mlstm_b_i[j], mlstm_b_f[j], mlstm_norm[j], swa_sinks[j],
                                       sb, seq)
            bias_col = jnp.broadcast_to(router_b[layer].reshape(-1, 1).astype(F32), (N_EXPERTS, LANES))
            h, hp, *routing = _outproj_ln_route(h, a1, a2, w_out, ln1_g[layer].reshape(1, -1), ln1_b[layer].reshape(1, -1),
                                                router_w[layer].T, bias_col)
            hs[s] = _moe(h, hp, routing, moe_w_gate, moe_w_up, moe_w_down, layer,
                         shared_w_gate[layer], shared_w_up[layer], shared_w_down[layer], ln2_g[layer], ln2_b[layer])
    return jnp.concatenate([_combine(*h).reshape(sb, seq, d) for h in hs], axis=0)
```

```python
import functools
import math

import numpy as np
import jax
import jax.numpy as jnp
from jax import lax
from jax.experimental import pallas as pl
from jax.experimental.pallas import tpu as pltpu
from jax.experimental.pallas import tpu_sc as plsc

F32 = jnp.float32
BF16 = jnp.bfloat16

D_MODEL = 1024
DEPTH = 4
ROPE_THETA = 10000.0
EPS = 1e-6
LN_EPS = 1e-5
MLA_H, MLA_NOPE, MLA_ROPE, MLA_V = 8, 64, 32, 64
Q_LORA, KV_LORA = 256, 128
GDN_H, GDN_DK, GDN_DV, CONV_W, GDN_CHUNK = 4, 128, 128, 4, 64
ML_H, ML_DK, ML_DV, ML_CHUNK = 4, 64, 128, 64
SWA_H, SWA_KV, SWA_D, WINDOW = 8, 2, 64, 128
N_EXPERTS, N_GROUPS, TOPK_GROUPS, TOP_K = 64, 8, 4, 8
D_EXPERT, D_SHARED = 256, 256
ROUTED_SCALE = 2.5
DN_ALPHA = (2 * DEPTH) ** 0.25

LANES = 128
SUBLANES = 8
V7X_VMEM_BYTES = 64 * 1024 * 1024
VMEM_LIMIT = V7X_VMEM_BYTES * 3 // 4

EXPERT_BLOCK_MIN = 256
EXPERT_BLOCK_MAX = 1024
STREAMS = 1
EXPERT_SUBBLOCKS = 4
SWA_SEQS_PER_STEP = 8
MLSTM_SEQS_PER_STEP = 2
GDN_SEQS_PER_STEP = 8
SC_CHUNK = 64


def _cparams(sem, vmem=VMEM_LIMIT):
    return pltpu.CompilerParams(dimension_semantics=sem, vmem_limit_bytes=vmem)


def _dot(a, b):
    return jnp.dot(a, b, preferred_element_type=F32)


def _dot_nt(a, b):
    return lax.dot_general(a, b, (((1,), (1,)), ((), ())), preferred_element_type=F32)


def _dot_tn(a, b):
    return lax.dot_general(a, b, (((0,), (0,)), ((), ())), preferred_element_type=F32)


def _split2(a):
    hi = a.astype(BF16)
    lo = (a - hi.astype(F32)).astype(BF16)
    return hi, lo


def _split3(a):
    p1 = a.astype(BF16)
    r = a - p1.astype(F32)
    p2 = r.astype(BF16)
    p3 = (r - p2.astype(F32)).astype(BF16)
    return p1, p2, p3


def _dot3(a, b, dot=_dot):
    ah, al = _split2(a)
    bh, bl = _split2(b)
    return dot(ah, bh) + (dot(ah, bl) + dot(al, bh))


def _dot_sel(sel, b, dot=_dot):
    sel = sel.astype(BF16)
    p1, p2, p3 = _split3(b)
    return dot(sel, p1) + (dot(sel, p2) + dot(sel, p3))


def _sigmoid(x):
    return 1.0 / (1.0 + jnp.exp(-x))


def _softplus(x):
    return jnp.maximum(x, 0.0) + jnp.log(1.0 + jnp.exp(-jnp.abs(x)))


def _silu(x):
    return x * _sigmoid(x)


def _lane_bcast(x, c):
    return jnp.broadcast_to(x[:, c:c + 1], x.shape)


def _iota2(shape, dim):
    return lax.broadcasted_iota(jnp.int32, shape, dim)


def _rope_kernel(pos_ref, rows_ref, sel_ref, cm_ref, sm_ref, cs_ref, ss_ref):
    ang = pos_ref[...] * rows_ref[0:1, :]
    cos_parts = _split3(jnp.cos(ang))
    sin_parts = _split3(jnp.sin(ang))

    def place(parts, k):
        return _dot(parts[0], sel_ref[k]) + (_dot(parts[1], sel_ref[k]) + _dot(parts[2], sel_ref[k]))

    cm_ref[...] = place(cos_parts, 0) + rows_ref[1:2, :]
    sm_ref[...] = place(sin_parts, 1)
    cs_ref[...] = place(cos_parts, 2)
    ss_ref[...] = place(sin_parts, 3)


def _rope_consts():
    hm, hs = MLA_ROPE // 2, SWA_D // 2
    rows = np.zeros((8, LANES), np.float32)
    rows[0, :hm] = ROPE_THETA ** (-(np.arange(0, MLA_ROPE, 2, dtype=np.float32) / MLA_ROPE))
    rows[0, hm:hm + hs] = ROPE_THETA ** (-(np.arange(0, SWA_D, 2, dtype=np.float32) / SWA_D))
    rows[1, :MLA_NOPE] = 1.0
    sel = np.zeros((4, LANES, LANES), np.float32)
    for j in range(hm):
        sel[0, j, MLA_NOPE + j] = sel[0, j, MLA_NOPE + hm + j] = 1.0
        sel[1, j, MLA_NOPE + j] = -1.0
        sel[1, j, MLA_NOPE + hm + j] = 1.0
    for h in range(LANES // SWA_D):
        for j in range(hs):
            sel[2, hm + j, h * SWA_D + j] = sel[2, hm + j, h * SWA_D + hs + j] = 1.0
            sel[3, hm + j, h * SWA_D + j] = -1.0
            sel[3, hm + j, h * SWA_D + hs + j] = 1.0
    return jnp.asarray(rows), jnp.asarray(sel, BF16)


def _rope_tables(pos, tm=512):
    t = pos.shape[0]
    tm = min(tm, t)
    rows, sel = _rope_consts()
    cm, sm, cs, ss = pl.pallas_call(
        _rope_kernel,
        grid=(t // tm,),
        in_specs=[pl.BlockSpec((tm, 1), lambda i: (i, 0)), pl.BlockSpec((8, LANES), lambda i: (0, 0)),
                  pl.BlockSpec((4, LANES, LANES), lambda i: (0, 0, 0))],
        out_specs=[pl.BlockSpec((tm, LANES), lambda i: (i, 0))] * 4,
        out_shape=[jax.ShapeDtypeStruct((t, LANES), F32)] * 4,
        compiler_params=_cparams(("arbitrary",)),
        name="rope_tables",
    )(pos, rows, sel)
    return (cm, sm), (cs, ss)


N_COMBINE_IN = 6
FUSED_TM = 256


def _combined(x_ref, gate_ref, rows_ref, sh_ref, g_ref, b_ref):
    gate = gate_ref[...]
    ya, yb = _unpack_pairs(sh_ref[...])
    for s in range(TOP_K):
        a, b = _unpack_pairs(rows_ref[s])
        ya = ya + gate[:, s:s + 1] * a
        yb = yb + gate[:, s:s + 1] * b
    ff = jnp.concatenate([ya, yb], axis=1)
    return _layer_norm(DN_ALPHA * x_ref[...] + ff, g_ref[...], b_ref[...])


def _stream_specs(stream, tm):
    d = stream[0].shape[1]
    row = lambda i: (i, 0)
    fix = lambda i: (0, 0)
    specs = [pl.BlockSpec((tm, d), row)]
    if len(stream) > 1:
        specs += [pl.BlockSpec((tm, TOP_K), row), pl.BlockSpec((TOP_K, tm, d // 2), lambda i: (0, i, 0)),
                  pl.BlockSpec((tm, d // 2), row), pl.BlockSpec((1, d), fix), pl.BlockSpec((1, d), fix)]
    return specs


def _stream_tile(stream_refs, h_ref):
    if h_ref is None:
        return stream_refs[0][...].astype(BF16)
    h = _combined(*stream_refs)
    h_ref[...] = h
    return h.astype(BF16)


def _proj_even_kernel(*refs, tiles_per_seq, fused):
    n_in = N_COMBINE_IN if fused else 1
    stream_refs = refs[:n_in]
    w_ref, cw_ref, c_ref, s_ref, qn_ref, kvn_ref, wq_ref, wkv_ref = refs[n_in:n_in + 8]
    q_ref, k_ref, v_ref, act_ref, z_ref, g_ref = refs[n_in + 8:n_in + 14]
    h_ref = refs[n_in + 14] if fused else None
    ext_ref, mla_ref = refs[-2:]
    tm = act_ref.shape[0]
    o = np.concatenate([[0], np.cumsum(EV_WIDTHS)]).tolist()
    halo = SUBLANES
    tap0 = halo - (CONV_W - 1)

    @pl.when(pl.program_id(0) % tiles_per_seq == 0)
    def _():
        ext_ref[0:halo, :] = jnp.zeros((halo, ext_ref.shape[1]), F32)

    xb = _stream_tile(stream_refs, h_ref)
    nchunk = 3
    cw = EV_WIDTHS[1] // nchunk

    def project(ci):
        ext_ref[halo:halo + tm, ci * cw:(ci + 1) * cw] = _dot(xb, w_ref[:, o[1] + ci * cw:o[1] + (ci + 1) * cw])

    project(0)
    for ci in range(nchunk):
        if ci + 1 < nchunk:
            project(ci + 1)
        else:
            mla_ref[...] = _dot(xb, w_ref[:, o[0]:o[1]])
            z_ref[...] = _dot(xb, w_ref[:, o[2]:o[3]]).astype(z_ref.dtype)
            g_ref[...] = _dot(xb, w_ref[:, o[3]:o[4]])
        cols = slice(ci * cw, (ci + 1) * cw)
        conv = cw_ref[0:1, cols] * ext_ref[tap0:tap0 + tm, cols]
        for j in range(1, CONV_W):
            conv = conv + cw_ref[j:j + 1, cols] * ext_ref[tap0 + j:tap0 + j + tm, cols]
        act_ref[:, cols] = _silu(conv).astype(act_ref.dtype)
    ext_ref[0:halo, :] = ext_ref[tm:tm + halo, :]
    _mla_prep_tile(mla_ref, c_ref, s_ref, qn_ref, kvn_ref, wq_ref, wkv_ref, q_ref, k_ref, v_ref)


def _proj_even(stream, w, conv_w, ctab, stab, qn, kvn, wq2, wkv2, seq, tm=512):
    t, k = stream[0].shape
    fused = len(stream) > 1
    tm = min(FUSED_TM if fused else tm, seq)
    row = lambda i: (i, 0)
    fix = lambda i: (0, 0)
    hw = MLA_H * LANES
    widths = (hw, hw, hw) + EV_WIDTHS[1:] + ((k,) if fused else ())
    dtypes = (BF16, BF16, BF16) + (F32,) * (len(widths) - 3)
    outs = pl.pallas_call(
        functools.partial(_proj_even_kernel, tiles_per_seq=seq // tm, fused=fused),
        grid=(t // tm,),
        in_specs=_stream_specs(stream, tm) + [pl.BlockSpec(w.shape, fix), pl.BlockSpec(conv_w.shape, fix),
                                              pl.BlockSpec((tm, LANES), row), pl.BlockSpec((tm, LANES), row),
                                              pl.BlockSpec(qn.shape, fix), pl.BlockSpec(kvn.shape, fix),
                                              pl.BlockSpec(wq2.shape, fix), pl.BlockSpec(wkv2.shape, fix)],
        out_specs=[pl.BlockSpec((tm, n), row) for n in widths],
        out_shape=[jax.ShapeDtypeStruct((t, n), dt) for n, dt in zip(widths, dtypes)],
        scratch_shapes=[pltpu.VMEM((tm + SUBLANES, EV_WIDTHS[1]), F32), pltpu.VMEM((tm, EV_WIDTHS[0]), F32)],
        compiler_params=_cparams(("arbitrary",)),
        name="combine_in_proj" if fused else "in_proj",
    )(*stream, w, conv_w, ctab, stab, qn, kvn, wq2, wkv2)
    return (outs[-1] if fused else stream[0],) + tuple(outs[:6])


OD_SEG = dict(mq=(0, 512), mk=(512, 1024), mv=(1024, 1536), mo=(1536, 2048), gates=(2048, 2176),
              sq=(2176, 2688), sk=(2688, 2944), sv=(2944, 3456))
OD_COLS = 3456


def _proj_odd_kernel(*refs, fused):
    n_in = N_COMBINE_IN if fused else 1
    stream_refs = refs[:n_in]
    w_ref, c_ref, s_ref, mq_ref, mk_ref, mv_ref, mo_ref, mg_ref, sq_ref, sk_ref, sv_ref = refs[n_in:n_in + 11]
    xb = _stream_tile(stream_refs, refs[n_in + 11] if fused else None)

    def seg(name):
        a, b = OD_SEG[name]
        return _dot(xb, w_ref[:, a:b])

    mq_ref[...] = seg("mq").astype(mq_ref.dtype)
    mk_ref[...] = seg("mk").astype(mk_ref.dtype)
    mv_ref[...] = seg("mv").astype(mv_ref.dtype)
    mo_ref[...] = seg("mo").astype(mo_ref.dtype)
    mg_ref[...] = seg("gates")
    c = c_ref[...]
    s = s_ref[...]
    def swap_halves(t):
        half = SWA_D // 2
        first_half = (_iota2(t.shape, 1) % SWA_D) < half
        return jnp.where(first_half, pltpu.roll(t, t.shape[1] - half, 1), pltpu.roll(t, half, 1))

    c8 = jnp.concatenate([c] * (SWA_H // 2), axis=1)
    s8 = jnp.concatenate([s] * (SWA_H // 2), axis=1)
    q = seg("sq")
    sq_ref[...] = (q * c8 + swap_halves(q) * s8).astype(sq_ref.dtype)
    c2 = jnp.concatenate([c] * SWA_KV, axis=1)
    s2 = jnp.concatenate([s] * SWA_KV, axis=1)
    k = seg("sk")
    sk_ref[...] = (k * c2 + swap_halves(k) * s2).astype(sk_ref.dtype)
    sv_ref[...] = seg("sv").astype(sv_ref.dtype)


def _proj_odd(stream, w, ctab, stab, tm=512):
    t, k = stream[0].shape
    fused = len(stream) > 1
    tm = min(FUSED_TM if fused else tm, t)
    widths = (512, 512, 512, 512, 128, SWA_H * SWA_D, SWA_KV * LANES, 2 * SWA_KV * LANES)
    dtypes = (F32, F32, F32, F32, F32, BF16, BF16, BF16)
    n_out = len(widths)
    if fused:
        widths, dtypes = widths + (k,), dtypes + (F32,)
    outs = pl.pallas_call(
        functools.partial(_proj_odd_kernel, fused=fused),
        grid=(t // tm,),
        in_specs=_stream_specs(stream, tm) + [pl.BlockSpec(w.shape, lambda i: (0, 0)),
                                              pl.BlockSpec((tm, LANES), lambda i: (i, 0)),
                                              pl.BlockSpec((tm, LANES), lambda i: (i, 0))],
        out_specs=[pl.BlockSpec((tm, n), lambda i: (i, 0)) for n in widths],
        out_shape=[jax.ShapeDtypeStruct((t, n), dt) for n, dt in zip(widths, dtypes)],
        compiler_params=_cparams(("arbitrary",)),
        name="combine_in_proj_odd" if fused else "in_proj_odd",
    )(*stream, w, ctab, stab)
    return (outs[-1] if fused else stream[0],) + tuple(outs[:n_out])


def _rms(x, g):
    return x * lax.rsqrt(jnp.mean(x * x, axis=-1, keepdims=True) + EPS) * g


def _mla_prep_tile(in_ref, c_ref, s_ref, qn_ref, kvn_ref, wq_ref, wkv_ref, q_ref, k_ref, v_ref):
    hw = MLA_H * LANES
    c = c_ref[...]
    s = s_ref[...]
    c8 = jnp.concatenate([c] * MLA_H, axis=1)
    s8 = jnp.concatenate([s] * MLA_H, axis=1)
    def swap_halves(t):
        half = MLA_ROPE // 2
        first_half = (_iota2(t.shape, 1) % LANES) < MLA_NOPE + half
        return jnp.where(first_half, pltpu.roll(t, t.shape[1] - half, 1), pltpu.roll(t, half, 1))

    cqn = _rms(in_ref[:, 0:Q_LORA], qn_ref[...]).astype(BF16)
    qq = _dot(cqn, wq_ref[...])
    scale = (MLA_NOPE + MLA_ROPE) ** -0.5
    q_ref[...] = ((qq[:, :hw] * c8 + qq[:, hw:] * s8) * scale).astype(q_ref.dtype)
    ckvn = _rms(in_ref[:, Q_LORA:Q_LORA + KV_LORA], kvn_ref[...]).astype(BF16)
    kv = _dot(ckvn, wkv_ref[...])
    o = Q_LORA + KV_LORA
    kr = in_ref[:, o:o + LANES]
    krr = kr * c + swap_halves(kr) * s
    k_ref[...] = (kv[:, :hw] + jnp.concatenate([krr] * MLA_H, axis=1)).astype(k_ref.dtype)
    v_ref[...] = kv[:, hw:].astype(v_ref.dtype)


def _mla_attn_kernel(q_ref, k_ref, v_ref, o_ref, *, tq):
    i = pl.program_id(2)
    neg = -1e30
    lane = _iota2((tq, LANES), 1)
    ones_lane = (MLA_V, 0)

    def chunk(j, carry, masked):
        start = pl.multiple_of(j * tq, tq)
        out = []
        for hh in range(2):
            m, acc = carry[hh]
            q = q_ref[:, hh * LANES:(hh + 1) * LANES]
            kc = k_ref[pl.ds(start, tq), hh * LANES:(hh + 1) * LANES]
            vc = v_ref[pl.ds(start, tq), hh * LANES:(hh + 1) * LANES]
            vc = jnp.where(lane == ones_lane[hh], jnp.ones_like(vc), vc)
            s = _dot_nt(q, kc)
            if masked:
                s = jnp.where(_iota2(s.shape, 0) >= _iota2(s.shape, 1), s, neg)
            m_new = jnp.maximum(m, jnp.max(s, axis=-1, keepdims=True))
            alpha = jnp.exp(m - m_new)
            p = jnp.exp((s - m_new).astype(BF16))
            acc = alpha * acc + _dot(p, vc)
            out.append((m_new, acc))
        return tuple(out)

    one = (jnp.full((tq, 1), neg, F32), jnp.zeros((tq, LANES), F32))
    carry = lax.fori_loop(0, i, lambda j, c: chunk(j, c, False), (one, one))
    (_, acc0), (_, acc1) = chunk(i, carry, True)
    o0 = acc0 / _lane_bcast(acc0, ones_lane[0])
    o1 = acc1 / _lane_bcast(acc1, ones_lane[1])
    o_ref[...] = jnp.where(lane < MLA_V, o0, o1).astype(o_ref.dtype)


def _mla_attn(q, k, v, batch, seq, tq=512):
    tq = min(tq, seq)
    nq = seq // tq
    pairs = MLA_H // 2
    return pl.pallas_call(
        functools.partial(_mla_attn_kernel, tq=tq),
        grid=(batch, pairs, nq),
        in_specs=[pl.BlockSpec((tq, 2 * LANES), lambda b, p, i: (b * nq + i, p)),
                  pl.BlockSpec((seq, 2 * LANES), lambda b, p, i: (b, p)),
                  pl.BlockSpec((seq, 2 * LANES), lambda b, p, i: (b, p))],
        out_specs=pl.BlockSpec((tq, LANES), lambda b, p, i: (b * nq + i, p)),
        out_shape=jax.ShapeDtypeStruct((batch * seq, pairs * LANES), BF16),
        compiler_params=_cparams(("arbitrary", "arbitrary", "arbitrary")),
        name="mla_attn",
    )(q, k, v)


def _unit_lower_inverse_many(ns):
    c = ns[0].shape[0]
    eye = (_iota2((c, c), 0) == _iota2((c, c), 1)).astype(F32)
    xs = [-n for n in ns]
    ps = [eye + x for x in xs]
    xb = [x.astype(BF16) for x in xs]
    for _ in range(int(math.log2(c)) - 1):
        xs = [_dot(b, b) for b in xb]
        xb = [x.astype(BF16) for x in xs]
        ps = [p + _dot(p.astype(BF16), b) for p, b in zip(ps, xb)]
    return ps


def _gdn_kernel(act_ref, g_ref, z_ref, al_ref, dt_ref, on_ref, o_ref, st_ref):
    c = GDN_CHUNK
    hd = GDN_DK
    nqk = GDN_H * GDN_DK

    @pl.when(pl.program_id(1) == 0)
    def _():
        st_ref[...] = jnp.zeros(st_ref.shape, F32)

    tri = (_iota2((c, c), 0) >= _iota2((c, c), 1)).astype(F32)
    row_ge = _iota2((c, c), 0) >= _iota2((c, c), 1)
    row_gt = _iota2((c, c), 0) > _iota2((c, c), 1)
    lane = _iota2((c, LANES), 1)

    seqs = []
    for bb in range(act_ref.shape[0]):
        gates = g_ref[bb]
        g_all = -jnp.exp(al_ref[...]) * _softplus(gates + dt_ref[...])
        gc_all = _dot_sel(tri, g_all)
        seqs.append(dict(beta_all=_sigmoid(gates), gc_all=gc_all, gc_parts=_split3(gc_all)))
    units = []
    for bb, sq in enumerate(seqs):
        for h in range(GDN_H):
            q = act_ref[bb, :, h * hd:(h + 1) * hd].astype(F32)
            k = act_ref[bb, :, nqk + h * hd:nqk + (h + 1) * hd].astype(F32)
            v = act_ref[bb, :, 2 * nqk + h * GDN_DV:2 * nqk + (h + 1) * GDN_DV].astype(F32)
            q = q * lax.rsqrt(jnp.sum(q * q, axis=-1, keepdims=True) + EPS) * (GDN_DK ** -0.5)
            k = k * lax.rsqrt(jnp.sum(k * k, axis=-1, keepdims=True) + EPS)
            beta = _lane_bcast(sq["beta_all"], h)
            gcol = _lane_bcast(sq["gc_all"], GDN_H + h)
            units.append(dict(bb=bb, h=h, q=q, k=k, v=v, beta=beta, gcol=gcol, kb=k * beta, parts=sq["gc_parts"]))
    for u in units:
        pick = (lane == GDN_H + u["h"]).astype(BF16)
        p0, p1, p2 = u["parts"]
        u["grow"] = _dot_nt(pick, p0) + (_dot_nt(pick, p1) + _dot_nt(pick, p2))
        u["kk"] = _dot3(u["kb"], u["k"], _dot_nt)
        u["qk"] = _dot_nt(u["q"].astype(BF16), u["k"].astype(BF16))
    for u in units:
        gcol = u["gcol"]
        decay = jnp.exp(jnp.where(row_ge, gcol[:, :c] - u["grow"], -jnp.inf))
        eg = jnp.exp(gcol)
        glast = gcol[c - 1:c, :]
        u["lower"] = jnp.where(row_gt, u["kk"] * decay, 0.0)
        u["rhs"] = jnp.concatenate([u["v"] * u["beta"], u["kb"] * eg], axis=1)
        u["attn"] = u["qk"] * decay
        u["qg"] = (u["q"] * eg).astype(BF16)
        u["kg"] = (u["k"] * jnp.exp(glast - gcol)).astype(BF16)
        u["gl"] = jnp.exp(glast)

    tinvs = _unit_lower_inverse_many([u["lower"] for u in units])
    uws = []
    for u, tinv in zip(units, tinvs):
        uws.append(_dot(tinv.astype(BF16), u["rhs"].astype(BF16)))
    states = [st_ref[u["bb"], u["h"]] for u in units]
    sbs = [s.astype(BF16) for s in states]
    vnews = [(uw[:, :GDN_DV] - _dot(uw[:, GDN_DV:].astype(BF16), sb)).astype(BF16) for uw, sb in zip(uws, sbs)]
    for u, state, sb, vnb in zip(units, states, sbs, vnews):
        bb, h = u["bb"], u["h"]
        o = _dot(u["qg"], sb) + _dot(u["attn"].astype(BF16), vnb)
        st_ref[bb, h] = state * u["gl"] + _dot_tn(u["kg"], vnb)
        o = _rms(o, on_ref[...]) * _silu(z_ref[bb, :, h * GDN_DV:(h + 1) * GDN_DV].astype(F32))
        o_ref[bb, :, h * GDN_DV:(h + 1) * GDN_DV] = o.astype(o_ref.dtype)


def _gdn(act, gates, z, a_row, dt_row, o_norm, batch, seq):
    c = GDN_CHUNK
    nc = seq // c
    w3 = act.shape[1]
    wo = GDN_H * GDN_DV
    nb = min(GDN_SEQS_PER_STEP, batch)
    row = lambda b, i: (b, i, 0)
    fix = lambda b, i: (0, 0)
    out = pl.pallas_call(
        _gdn_kernel,
        grid=(batch // nb, nc),
        in_specs=[pl.BlockSpec((nb, c, w3), row), pl.BlockSpec((nb, c, LANES), row), pl.BlockSpec((nb, c, wo), row),
                  pl.BlockSpec((1, LANES), fix), pl.BlockSpec((1, LANES), fix), pl.BlockSpec((1, GDN_DV), fix)],
        out_specs=pl.BlockSpec((nb, c, wo), row),
        out_shape=jax.ShapeDtypeStruct((batch, seq, wo), BF16),
        scratch_shapes=[pltpu.VMEM((nb, GDN_H, GDN_DK, GDN_DV), F32)],
        compiler_params=_cparams(("arbitrary", "arbitrary")),
        name="gdn",
    )(act.reshape(batch, seq, w3), gates.reshape(batch, seq, LANES), z.reshape(batch, seq, wo), a_row, dt_row, o_norm)
    return out.reshape(batch * seq, wo)


def _mlstm_kernel(q_ref, k_ref, v_ref, og_ref, g_ref, bias_ref, nrm_ref, o_ref, c_ref, n_ref, m_ref):
    @pl.when(pl.program_id(1) == 0)
    def _():
        c_ref[...] = jnp.zeros(c_ref.shape, F32)
        n_ref[...] = jnp.zeros(n_ref.shape, F32)
        m_ref[...] = jnp.zeros(m_ref.shape, F32)

    c = ML_CHUNK
    tri = (_iota2((c, c), 0) >= _iota2((c, c), 1)).astype(F32)
    row_ge = _iota2((c, c), 0) >= _iota2((c, c), 1)
    ones = jnp.ones((c, LANES), F32)
    lane = _iota2((c, LANES), 1)

    units = []
    for bb in range(q_ref.shape[0]):
        pre = g_ref[bb] + bias_ref[...]
        logf = jnp.minimum(pre, 0.0) - jnp.log(1.0 + jnp.exp(-jnp.abs(pre)))
        bcum_all = _dot_sel(tri, logf)
        for h in range(ML_H):
            q = q_ref[bb, :, h * LANES:(h + 1) * LANES].astype(F32)
            k = k_ref[bb, :, h * LANES:(h + 1) * LANES].astype(F32) * (ML_DK ** -0.5)
            units.append(dict(bb=bb, h=h, q=q, k=k, qb=q.astype(BF16), vb=v_ref[bb, :, h * ML_DV:(h + 1) * ML_DV].astype(BF16),
                              bcol=_lane_bcast(bcum_all, ML_H + h),
                              icol=_lane_bcast(pre, h),
                              col=jnp.where(lane == h, pre, 0.0) - jnp.where(lane == ML_H + h, bcum_all, 0.0),
                              m_st=m_ref[bb, h], cst=c_ref[bb, h], nst=n_ref[bb, h]))
    for u in units:
        u["row"] = _dot_sel(ones, u["col"], _dot_nt)
        u["qk"] = _dot_nt(u["qb"], u["k"].astype(BF16))
        u["qc"] = _dot(u["qb"], u["cst"].astype(BF16))
    for u in units:
        u["d"] = jnp.where(row_ge, u["bcol"][:, :c] + u["row"], -jnp.inf)
        u["inter"] = u["bcol"] + u["m_st"]
        u["m_t"] = jnp.maximum(u["inter"], jnp.max(u["d"], axis=-1, keepdims=True))
        u["b_end"] = u["bcol"][c - 1:c, :]
        u["a"] = u["b_end"] - u["bcol"] + u["icol"]
        u["m_new"] = jnp.maximum(u["b_end"] + u["m_st"], jnp.max(u["a"], axis=0, keepdims=True))
    for u in units:
        u["w_inter"] = jnp.exp(u["inter"] - u["m_t"])
        u["p"] = jnp.exp(u["d"] - u["m_t"][:, :c]) * u["qk"]
        u["keep"] = jnp.exp(u["b_end"] + u["m_st"] - u["m_new"])
        u["ks"] = u["k"] * jnp.exp(u["a"] - u["m_new"])
    for u in units:
        u["pv"] = _dot(u["p"].astype(BF16), u["vb"])
        u["kv"] = _dot_tn(u["ks"].astype(BF16), u["vb"])
    for u in units:
        u["den"] = (u["w_inter"] * jnp.sum(u["q"] * u["nst"], axis=-1, keepdims=True)
                    + jnp.sum(u["p"], axis=-1, keepdims=True))
    for u in units:
        bb, h = u["bb"], u["h"]
        num = u["w_inter"] * u["qc"] + u["pv"]
        hc = num / jnp.maximum(jnp.abs(u["den"]), jnp.exp(-u["m_t"]))
        c_ref[bb, h] = u["cst"] * u["keep"] + u["kv"]
        n_ref[bb, h] = u["nst"] * u["keep"] + jnp.sum(u["ks"], axis=0, keepdims=True)
        m_ref[bb, h] = u["m_new"]
        hn = (_rms(hc, nrm_ref[:, h * ML_DV:(h + 1) * ML_DV])
              * _sigmoid(og_ref[bb, :, h * ML_DV:(h + 1) * ML_DV].astype(F32)))
        o_ref[bb, :, h * ML_DV:(h + 1) * ML_DV] = hn.astype(o_ref.dtype)


def _mlstm(mq, mk, mv, mo, gates, bias_row, norm_row, batch, seq):
    c = ML_CHUNK
    nc = seq // c
    nb = min(MLSTM_SEQS_PER_STEP, batch)
    row = lambda b, i: (b, i, 0)
    fix = lambda b, i: (0, 0)
    wide = ML_H * LANES
    r3 = lambda a: a.reshape(batch, seq, a.shape[-1])
    out = pl.pallas_call(
        _mlstm_kernel,
        grid=(batch // nb, nc),
        in_specs=[pl.BlockSpec((nb, c, wide), row), pl.BlockSpec((nb, c, wide), row), pl.BlockSpec((nb, c, wide), row),
                  pl.BlockSpec((nb, c, wide), row), pl.BlockSpec((nb, c, LANES), row),
                  pl.BlockSpec((1, LANES), fix), pl.BlockSpec((1, wide), fix)],
        out_specs=pl.BlockSpec((nb, c, wide), row),
        out_shape=jax.ShapeDtypeStruct((batch, seq, wide), BF16),
        scratch_shapes=[pltpu.VMEM((nb, ML_H, LANES, ML_DV), F32), pltpu.VMEM((nb, ML_H, 1, LANES), F32),
                        pltpu.VMEM((nb, ML_H, 1, LANES), F32)],
        compiler_params=_cparams(("arbitrary", "arbitrary")),
        name="mlstm",
    )(r3(mq), r3(mk), r3(mv), r3(mo), r3(gates), bias_row, norm_row)
    return out.reshape(batch * seq, wide)


def _swa_kernel(q_ref, kc_ref, kp_ref, vc_ref, vp_ref, sink_ref, o_ref):
    w = WINDOW
    n = pl.program_id(1)
    scale = SWA_D ** -0.5
    qi = _iota2((w, w), 0)
    kj = _iota2((w, w), 1)
    mask_c = kj <= qi
    mask_p = jnp.logical_and(kj > qi, n > 0)
    grp = SWA_H // SWA_KV
    neg = -1e30
    units = [(bb, h) for bb in range(q_ref.shape[0]) for h in range(SWA_H)]
    scores = []
    half_of_lane = _iota2((w, LANES), 1) // SWA_D
    for bb, h in units:
        g = h // grp
        pair = q_ref[bb, :, (h // 2) * LANES:(h // 2 + 1) * LANES]
        q = jnp.where(half_of_lane == h % 2, pair, jnp.zeros_like(pair))
        scores.append((_dot_nt(q, kc_ref[bb, :, g * LANES:(g + 1) * LANES]),
                       _dot_nt(q, kp_ref[bb, :, g * LANES:(g + 1) * LANES])))
    masked, tops, exps, dens, probs = [], [], [], [], {}
    for sc, sp in scores:
        masked.append((jnp.where(mask_c, sc * scale, neg), jnp.where(mask_p, sp * scale, neg)))
    for (bb, h), (s_c, s_p) in zip(units, masked):
        tops.append(jnp.maximum(jnp.max(jnp.maximum(s_c, s_p), axis=-1, keepdims=True), sink_ref[:, h:h + 1]))
    for (s_c, s_p), m in zip(masked, tops):
        exps.append((jnp.where(mask_c, jnp.exp(s_c - m), 0.0), jnp.where(mask_p, jnp.exp(s_p - m), 0.0)))
    ones_b = jnp.ones((w, LANES), BF16)
    for (bb, h), (p_c, p_p), m in zip(units, exps, tops):
        p_c, p_p = p_c.astype(BF16), p_p.astype(BF16)
        probs[bb, h] = (p_c, p_p)
        dens.append(_dot(p_c, ones_b) + _dot(p_p, ones_b) + jnp.exp(sink_ref[:, h:h + 1] - m))
    inv = {u: 1.0 / den for u, den in zip(units, dens)}
    for bb in range(q_ref.shape[0]):
        for pair in range(SWA_H // 2):
            acc = None
            for sub in range(2):
                h = 2 * pair + sub
                vcol = (2 * (h // grp) + sub) * LANES
                p_c, p_p = probs[bb, h]
                part = (_dot(p_c, vc_ref[bb, :, vcol:vcol + LANES]) + _dot(p_p, vp_ref[bb, :, vcol:vcol + LANES])) * inv[bb, h]
                acc = part if acc is None else acc + part
            o_ref[bb, :, pair * LANES:(pair + 1) * LANES] = acc.astype(o_ref.dtype)


def _swa(sq, sk, sv, sinks_row, batch, seq):
    w = WINDOW
    nb = seq // w
    ns = min(SWA_SEQS_PER_STEP, batch)
    wo = SWA_H * SWA_D
    cur = lambda b, n: (b, n, 0)
    prev = lambda b, n: (b, jnp.maximum(n - 1, 0), 0)
    r3 = lambda a: a.reshape(batch, seq, a.shape[-1])
    q3, k3, v3 = r3(sq), r3(sk), r3(sv)
    out = pl.pallas_call(
        _swa_kernel,
        grid=(batch // ns, nb),
        in_specs=[pl.BlockSpec((ns, w, sq.shape[1]), cur),
                  pl.BlockSpec((ns, w, sk.shape[1]), cur), pl.BlockSpec((ns, w, sk.shape[1]), prev),
                  pl.BlockSpec((ns, w, sv.shape[1]), cur), pl.BlockSpec((ns, w, sv.shape[1]), prev),
                  pl.BlockSpec((1, LANES), lambda b, n: (0, 0))],
        out_specs=pl.BlockSpec((ns, w, wo), cur),
        out_shape=jax.ShapeDtypeStruct((batch, seq, wo), BF16),
        compiler_params=_cparams(("arbitrary", "arbitrary")),
        name="swa",
    )(q3, k3, k3, v3, v3, sinks_row)
    return out.reshape(batch * seq, wo)


def _layer_norm(h, g, b):
    mu = jnp.mean(h, axis=-1, keepdims=True)
    d = h - mu
    var = jnp.mean(d * d, axis=-1, keepdims=True)
    return d * lax.rsqrt(var + LN_EPS) * g + b


ROUTE_ROWS = 256


def _outproj_kernel(x_ref, a1_ref, a2_ref, w_ref, g_ref, b_ref, wt_ref, bias_ref,
                    o_ref, op_ref, idx_ref, gate_ref, rank_ref, cnt_ref, carry_ref):
    tm = x_ref.shape[0]
    k1 = a1_ref.shape[1]
    rp = min(ROUTE_ROWS, tm)

    @pl.when(pl.program_id(0) == 0)
    def _():
        carry_ref[...] = jnp.zeros(carry_ref.shape, F32)

    for p in range(tm // rp):
        r = slice(p * rp, (p + 1) * rp)
        y = _dot(a1_ref[r, :].astype(BF16), w_ref[0:k1, :]) + _dot(a2_ref[r, :].astype(BF16), w_ref[k1:, :])
        h = _layer_norm(DN_ALPHA * x_ref[r, :] + y, g_ref[...], b_ref[...])
        o_ref[r, :] = h
        op_ref[r, :] = _pack_pairs(h)
    for p in range(tm // rp):
        r = slice(p * rp, (p + 1) * rp)
        idx, gate, rank = _route_tile(o_ref[r, :], wt_ref, bias_ref, carry_ref)
        idx_ref[:, r] = idx
        gate_ref[:, r] = gate
        rank_ref[:, r] = rank
    cnt_ref[...] = carry_ref[...]


def _outproj_ln_route(x, a1, a2, w, g, b, wt, bias_col, tm=1024):
    t, d = x.shape
    tm = min(tm, t)
    row = lambda i: (i, 0)
    col = lambda i: (0, i)
    fix = lambda i: (0, 0)
    return pl.pallas_call(
        _outproj_kernel,
        grid=(t // tm,),
        in_specs=[pl.BlockSpec((tm, d), row), pl.BlockSpec((tm, a1.shape[1]), row), pl.BlockSpec((tm, a2.shape[1]), row),
                  pl.BlockSpec(w.shape, fix), pl.BlockSpec((1, d), fix), pl.BlockSpec((1, d), fix),
                  pl.BlockSpec(wt.shape, fix), pl.BlockSpec((N_EXPERTS, LANES), fix)],
        out_specs=[pl.BlockSpec((tm, d), row), pl.BlockSpec((tm, d // 2), row),
                   pl.BlockSpec((TOP_K, tm), col), pl.BlockSpec((TOP_K, tm), col), pl.BlockSpec((TOP_K, tm), col),
                   pl.BlockSpec((N_EXPERTS, LANES), fix)],
        out_shape=[jax.ShapeDtypeStruct((t, d), F32), jax.ShapeDtypeStruct((t, d // 2), jnp.uint32),
                   jax.ShapeDtypeStruct((TOP_K, t), jnp.int32), jax.ShapeDtypeStruct((TOP_K, t), F32),
                   jax.ShapeDtypeStruct((TOP_K, t), jnp.int32), jax.ShapeDtypeStruct((N_EXPERTS, LANES), F32)],
        scratch_shapes=[pltpu.VMEM((N_EXPERTS, LANES), F32)],
        compiler_params=_cparams(("arbitrary",)),
        name="outproj_ln_route",
    )(x, a1, a2, w, g, b, wt, bias_col)


def _first_index(x, m, iota_f, sentinel):
    return jnp.min(jnp.where(x == m, iota_f, sentinel), axis=0, keepdims=True)


def _route_tile(x, wt_ref, bias_ref, carry_ref):
    tm = x.shape[0]
    e = N_EXPERTS
    gs = e // N_GROUPS
    ninf = -jnp.inf

    logits = _dot3(wt_ref[...], x, _dot_nt)
    scores = _sigmoid(logits)
    sel = scores + bias_ref[:, 0:1]

    sub_f = _iota2((gs, tm), 0).astype(F32)
    gscore = []
    for g in range(N_GROUPS):
        blk = sel[g * gs:(g + 1) * gs, :]
        m1 = jnp.max(blk, axis=0, keepdims=True)
        i1 = _first_index(blk, m1, sub_f, float(gs))
        m2 = jnp.max(jnp.where(sub_f == i1, ninf, blk), axis=0, keepdims=True)
        gscore.append(m1 + m2)
    gsc = jnp.concatenate(gscore, axis=0)
    grp_f = _iota2((N_GROUPS, tm), 0).astype(F32)
    gmask = jnp.zeros((N_GROUPS, tm), F32)
    for _ in range(TOPK_GROUPS):
        m = jnp.max(gsc, axis=0, keepdims=True)
        gi = _first_index(gsc, m, grp_f, float(N_GROUPS))
        hit = grp_f == gi
        gmask = jnp.where(hit, 1.0, gmask)
        gsc = jnp.where(hit, ninf, gsc)
    masked = jnp.concatenate(
        [jnp.where(gmask[g:g + 1, :] > 0.0, sel[g * gs:(g + 1) * gs, :], ninf) for g in range(N_GROUPS)], axis=0)

    exp_f = _iota2((e, tm), 0).astype(F32)
    chosen = jnp.zeros((e, tm), F32)
    idxs, gates = [], []
    for _ in range(TOP_K):
        m = jnp.max(masked, axis=0, keepdims=True)
        ei = _first_index(masked, m, exp_f, float(e))
        hit = exp_f == ei
        idxs.append(ei)
        gates.append(jnp.sum(jnp.where(hit, scores, 0.0), axis=0, keepdims=True))
        chosen = jnp.where(hit, 1.0, chosen)
        masked = jnp.where(hit, ninf, masked)
    gate = jnp.concatenate(gates, axis=0)
    gate = gate / jnp.sum(gate, axis=0, keepdims=True) * ROUTED_SCALE
    idx_f = jnp.concatenate(idxs, axis=0)

    upper = (_iota2((tm, tm), 0) < _iota2((tm, tm), 1)).astype(BF16)
    before = _dot(chosen.astype(BF16), upper) + carry_ref[...][:, 0:1]
    ranks = [jnp.sum(jnp.where(exp_f == idxs[k], before, 0.0), axis=0, keepdims=True) for k in range(TOP_K)]
    carry_ref[...] = carry_ref[...] + jnp.sum(chosen, axis=1, keepdims=True)
    return idx_f.astype(jnp.int32), gate, jnp.concatenate(ranks, axis=0).astype(jnp.int32)


def _dest_kernel(idx_ref, rank_ref, start_ref, dest_ref):
    tm = idx_ref.shape[1]
    exp_i = _iota2((N_EXPERTS, tm), 0)
    start = start_ref[:, 0:1]
    rows = [jnp.sum(jnp.where(exp_i == idx_ref[s:s + 1, :], start, 0.0), axis=0, keepdims=True) for s in range(TOP_K)]
    dest_ref[...] = jnp.concatenate(rows, axis=0).astype(jnp.int32) + rank_ref[...]


def _dest_rows(idx, rank, start_col, tm=2048):
    t = idx.shape[1]
    tm = min(tm, t)
    col = lambda i: (0, i)
    return pl.pallas_call(
        _dest_kernel,
        grid=(t // tm,),
        in_specs=[pl.BlockSpec((TOP_K, tm), col), pl.BlockSpec((TOP_K, tm), col),
                  pl.BlockSpec((N_EXPERTS, LANES), lambda i: (0, 0))],
        out_specs=pl.BlockSpec((TOP_K, tm), col),
        out_shape=jax.ShapeDtypeStruct((TOP_K, t), jnp.int32),
        compiler_params=_cparams(("arbitrary",)),
        name="moe_dest",
    )(idx, rank, start_col)


def _pack_pairs(x):
    n = x.shape[1] // 2
    hi = lax.bitcast_convert_type(x[:, :n].astype(BF16).astype(F32), jnp.uint32)
    lo = lax.bitcast_convert_type(x[:, n:].astype(BF16).astype(F32), jnp.uint32)
    return hi | (lo >> 16)


def _unpack_pairs(w):
    hi = lax.bitcast_convert_type(w & jnp.uint32(0xFFFF0000), F32)
    lo = lax.bitcast_convert_type(w << 16, F32)
    return hi, lo


def _sc_scatter_rows(xp, dest, rows, chunk=LANES):
    t, width = xp.shape
    info = plsc.get_sparse_core_info()
    ncores, nsub = info.num_cores, info.num_subcores
    per_worker = t // (ncores * nsub)
    nchunk = per_worker // chunk
    mesh = plsc.VectorSubcoreMesh(core_axis_name="c", subcore_axis_name="s")

    @functools.partial(
        pl.kernel, mesh=mesh,
        out_type=jax.ShapeDtypeStruct((rows, width), xp.dtype),
        scratch_types=[pltpu.VMEM((TOP_K, chunk), jnp.int32), pltpu.VMEM((chunk, width), xp.dtype), pltpu.SemaphoreType.DMA],
    )
    def scatter(xp_hbm, dest_hbm, out_hbm, idx_v, rows_v, sem):
        base = (lax.axis_index("s") * ncores + lax.axis_index("c")) * per_worker

        @pl.loop(0, nchunk)
        def _(i):
            off = pl.multiple_of(base + i * chunk, chunk)
            pltpu.sync_copy(dest_hbm.at[:, pl.ds(off, chunk)], idx_v)
            pltpu.sync_copy(xp_hbm.at[pl.ds(off, chunk)], rows_v)
            copies = [pltpu.async_copy(rows_v, out_hbm.at[idx_v.at[s]], sem) for s in range(TOP_K)]
            for cp in copies:
                cp.wait()

    return scatter(xp, dest)


def _experts_kernel(be_ref, nu_ref, nv_ref, first_ref, slot_ref, nxt_ref, xs_ref, wg_hbm, wu_hbm, wd_hbm, ys_ref,
                    wgf_ref, wuf_ref, wdf_ref, wgb_ref, wub_ref, wdb_ref, sem, *, layer):
    i = pl.program_id(0)

    def fetch(e, s):
        return [pltpu.make_async_copy(wg_hbm.at[layer, e], wgf_ref.at[s], sem.at[s]),
                pltpu.make_async_copy(wu_hbm.at[layer, e], wuf_ref.at[s], sem.at[s]),
                pltpu.make_async_copy(wd_hbm.at[layer, e], wdf_ref.at[s], sem.at[s])]

    @pl.when(i == 0)
    def _():
        for cp in fetch(be_ref[0], 0):
            cp.start()

    @pl.when(jnp.logical_and(first_ref[i] == 1, i < nu_ref[0]))
    def _():
        s = slot_ref[i]
        for cp in fetch(be_ref[i], s):
            cp.wait()
        wgb_ref[...] = wgf_ref[s].astype(BF16)
        wub_ref[...] = wuf_ref[s].astype(BF16)
        wdb_ref[...] = wdf_ref[s].astype(BF16)

        @pl.when(nxt_ref[i] >= 0)
        def _():
            for cp in fetch(nxt_ref[i], 1 - s):
                cp.start()

    @pl.when(i < nu_ref[0])
    def _():
        sub = xs_ref.shape[0] // EXPERT_SUBBLOCKS
        acts = []
        for r in range(EXPERT_SUBBLOCKS):
            rows = pl.ds(r * sub, sub)
            live = (_iota2((sub, 1), 0) + r * sub) < nv_ref[i]
            xa, xb = _unpack_pairs(jnp.where(live, xs_ref[rows, :], jnp.uint32(0)))
            x = jnp.concatenate([xa.astype(BF16), xb.astype(BF16)], axis=1)
            acts.append((_dot(x, wgb_ref[...]), _dot(x, wub_ref[...])))
        outs = [_dot((_silu(gate) * up).astype(BF16), wdb_ref[...]) for gate, up in acts]
        for r, y in enumerate(outs):
            ys_ref[pl.ds(r * sub, sub), :] = _pack_pairs(y)


def _experts(block_e, n_used, n_valid, xs, wg, wu, wd, layer, block):
    rows, half = xs.shape
    d = 2 * half
    nb = rows // block
    pos = jnp.arange(nb, dtype=jnp.int32)
    first = jnp.concatenate([jnp.ones((1,), jnp.int32), (block_e[1:] != block_e[:-1]).astype(jnp.int32)])
    slot = (jnp.cumsum(first) - 1) % 2
    later = (pos[None, :] > pos[:, None]) & (block_e[None, :] != block_e[:, None]) & (pos[None, :] < n_used[0])
    nxt_pos = jnp.min(jnp.where(later, pos[None, :], nb), axis=1)
    nxt = jnp.where(nxt_pos < nb, block_e[jnp.minimum(nxt_pos, nb - 1)], -1)
    blk = lambda i, be, nu, *rest: (jnp.minimum(i, nu[0] - 1), 0)
    hbm = pl.BlockSpec(memory_space=pl.ANY)
    return pl.pallas_call(
        functools.partial(_experts_kernel, layer=layer),
        grid_spec=pltpu.PrefetchScalarGridSpec(
            num_scalar_prefetch=6,
            grid=(nb,),
            in_specs=[pl.BlockSpec((block, half), blk), hbm, hbm, hbm],
            out_specs=pl.BlockSpec((block, half), blk),
            scratch_shapes=[pltpu.VMEM((2, d, D_EXPERT), F32), pltpu.VMEM((2, d, D_EXPERT), F32),
                            pltpu.VMEM((2, D_EXPERT, d), F32),
                            pltpu.VMEM((d, D_EXPERT), BF16), pltpu.VMEM((d, D_EXPERT), BF16),
                            pltpu.VMEM((D_EXPERT, d), BF16), pltpu.SemaphoreType.DMA((2,))],
        ),
        out_shape=jax.ShapeDtypeStruct((rows, half), jnp.uint32),
        compiler_params=_cparams(("arbitrary",)),
        name="moe_experts",
    )(block_e, n_used, n_valid, first, slot.astype(jnp.int32), nxt.astype(jnp.int32), xs, wg, wu, wd)


def _sc_gather_rows(table, idx, chunk=SC_CHUNK):
    n = idx.shape[0]
    width = table.shape[1]
    info = plsc.get_sparse_core_info()
    ncores, nsub = info.num_cores, info.num_subcores
    per_worker = n // (ncores * nsub)
    nchunk = per_worker // chunk
    mesh = plsc.VectorSubcoreMesh(core_axis_name="c", subcore_axis_name="s")

    @functools.partial(
        pl.kernel, mesh=mesh,
        out_type=jax.ShapeDtypeStruct((n, width), table.dtype),
        scratch_types=[pltpu.VMEM((nchunk, chunk), jnp.int32), pltpu.VMEM((2, chunk, width), table.dtype),
                       pltpu.SemaphoreType.DMA((2,)), pltpu.SemaphoreType.DMA((2,))],
    )
    def gather(table_hbm, idx_hbm, out_hbm, idx_v, rows_v, gsem, wsem):
        wid = lax.axis_index("s") * ncores + lax.axis_index("c")
        base = wid * per_worker
        pltpu.sync_copy(idx_hbm.at[pl.ds(wid * nchunk, nchunk)], idx_v)

        def fetch(j, b):
            return pltpu.make_async_copy(table_hbm.at[idx_v.at[j]], rows_v.at[b], gsem.at[b])

        def flush(j, b):
            off = pl.multiple_of(base + j * chunk, chunk)
            return pltpu.make_async_copy(rows_v.at[b], out_hbm.at[pl.ds(off, chunk)], wsem.at[b])

        fetch(0, 0).start()

        @pl.loop(0, nchunk, step=2)
        def _(i):
            for b in range(2):
                j = i + b
                fetch(j, b).wait()

                @pl.when(j + 1 < nchunk)
                def _():
                    @pl.when(j >= 1)
                    def _():
                        flush(j - 1, 1 - b).wait()

                    fetch(j + 1, 1 - b).start()

                flush(j, b).start()

        flush(nchunk - 2, 0).wait()
        flush(nchunk - 1, 1).wait()

    return gather(table, idx.reshape(n // chunk, chunk))


def _shared_kernel(xp_ref, sg_ref, su_ref, sd_ref, o_ref):
    xa, xb = _unpack_pairs(xp_ref[...])
    x = jnp.concatenate([xa.astype(BF16), xb.astype(BF16)], axis=1)
    hs = _silu(_dot(x, sg_ref[...])) * _dot(x, su_ref[...])
    o_ref[...] = _pack_pairs(_dot(hs.astype(BF16), sd_ref[...]))


def _shared_expert(xp, sg, su, sd, tm=512):
    t, half = xp.shape
    row = lambda i: (i, 0)
    fix = lambda i: (0, 0)
    return pl.pallas_call(
        _shared_kernel,
        grid=(t // tm,),
        in_specs=[pl.BlockSpec((tm, half), row), pl.BlockSpec(sg.shape, fix), pl.BlockSpec(su.shape, fix),
                  pl.BlockSpec(sd.shape, fix)],
        out_specs=pl.BlockSpec((tm, half), row),
        out_shape=jax.ShapeDtypeStruct((t, half), jnp.uint32),
        compiler_params=_cparams(("arbitrary",)),
        name="moe_shared",
    )(xp, sg, su, sd)


def _combine_kernel(x_ref, gate_ref, rows_ref, sh_ref, g_ref, b_ref, o_ref):
    o_ref[...] = _combined(x_ref, gate_ref, rows_ref, sh_ref, g_ref, b_ref)


def _combine(x, gate_t, rows, shared, g, b, tm=512):
    t, d = x.shape
    row = lambda i: (i, 0)
    return pl.pallas_call(
        _combine_kernel,
        grid=(t // tm,),
        in_specs=_stream_specs((x, gate_t, rows, shared, g, b), tm),
        out_specs=pl.BlockSpec((tm, d), row),
        out_shape=jax.ShapeDtypeStruct((t, d), F32),
        compiler_params=_cparams(("arbitrary",)),
        name="moe_combine",
    )(x, gate_t, rows, shared, g, b)


def _take_cols(w, idx):
    idx = np.asarray(idx)
    runs, start = [], 0
    for pos in range(1, len(idx) + 1):
        run_ends = pos == len(idx) or (idx[pos] != idx[pos - 1] + 1 if idx[pos - 1] >= 0 else idx[pos] >= 0)
        if run_ends:
            runs.append((start, int(idx[start]), pos - start))
            start = pos

    def body(w_ref, o_ref):
        for dst, src, width in runs:
            if src < 0:
                o_ref[:, dst:dst + width] = jnp.zeros((o_ref.shape[0], width), o_ref.dtype)
            else:
                o_ref[:, dst:dst + width] = w_ref[:, src:src + width].astype(o_ref.dtype)

    rows = w.shape[0]
    tr = min(rows, 256)
    return pl.pallas_call(
        body,
        grid=(rows // tr,),
        in_specs=[pl.BlockSpec((tr, w.shape[1]), lambda i: (i, 0))],
        out_specs=pl.BlockSpec((tr, len(idx)), lambda i: (i, 0)),
        out_shape=jax.ShapeDtypeStruct((rows, len(idx)), BF16),
        compiler_params=_cparams(("arbitrary",)),
        name="weight_cols",
    )(w)


def _pad_lane_row(v, first_lane, width=LANES):
    out = jnp.zeros((1, width), F32)
    return lax.dynamic_update_slice(out, v.reshape(1, -1).astype(F32), (0, first_lane))


def _even_in_cols():
    z = lambda n: -np.ones(n, int)
    kr0 = Q_LORA + KV_LORA
    cols = [np.arange(0, Q_LORA), np.arange(Q_LORA, Q_LORA + KV_LORA),
            z(MLA_NOPE), np.arange(kr0, kr0 + MLA_ROPE), z(LANES - MLA_NOPE - MLA_ROPE)]
    g0 = kr0 + MLA_ROPE
    nqk = GDN_H * GDN_DK
    cols.append(np.arange(g0, g0 + 3 * nqk))
    zoff = g0 + 3 * nqk + 2 * GDN_H
    cols.append(np.arange(zoff, zoff + GDN_H * GDN_DV))
    cols += [np.arange(g0 + 3 * nqk, g0 + 3 * nqk + 2 * GDN_H), z(LANES - 2 * GDN_H)]
    return np.concatenate(cols)


EV_WIDTHS = (Q_LORA + KV_LORA + LANES, 3 * GDN_H * GDN_DK, GDN_H * GDN_DV, LANES)


def _mla_q_cols():
    per = MLA_NOPE + MLA_ROPE
    half = MLA_ROPE // 2
    main, sw = [], []
    for h in range(MLA_H):
        b = h * per
        main += [np.arange(b, b + per), -np.ones(LANES - per, int)]
        sw += [-np.ones(MLA_NOPE, int), np.arange(b + MLA_NOPE + half, b + per), np.arange(b + MLA_NOPE, b + MLA_NOPE + half),
               -np.ones(LANES - per, int)]
    return np.concatenate(main + sw)


def _mla_kv_cols():
    per = MLA_NOPE + MLA_V
    kc, vc = [], []
    for h in range(MLA_H):
        b = h * per
        kc += [np.arange(b, b + MLA_NOPE), -np.ones(LANES - MLA_NOPE, int)]
        vv = np.arange(b + MLA_NOPE, b + per)
        pad = -np.ones(LANES - MLA_V, int)
        vc += [vv, pad] if h % 2 == 0 else [pad, vv]
    return np.concatenate(kc + vc)


def _odd_in_cols():
    z = lambda n: -np.ones(n, int)
    o = 0
    cols = []
    mq0, mk0 = 0, ML_H * ML_DK
    for base in (mq0, mk0):
        for h in range(ML_H):
            cols += [np.arange(base + h * ML_DK, base + (h + 1) * ML_DK), z(LANES - ML_DK)]
    mv0 = 2 * ML_H * ML_DK
    cols.append(np.arange(mv0, mv0 + ML_H * ML_DV))
    mi0 = mv0 + ML_H * ML_DV
    mo0 = mi0 + 2 * ML_H
    cols.append(np.arange(mo0, mo0 + ML_H * ML_DV))
    cols += [np.arange(mi0, mi0 + 2 * ML_H), z(LANES - 2 * ML_H)]
    sq0 = mo0 + ML_H * ML_DV
    sk0 = sq0 + SWA_H * SWA_D
    sv0 = sk0 + SWA_KV * SWA_D
    half = SWA_D // 2

    cols.append(np.arange(sq0, sq0 + SWA_H * SWA_D))
    for g in range(SWA_KV):
        cols += [np.arange(sk0 + g * SWA_D, sk0 + (g + 1) * SWA_D)] * 2
    for g in range(SWA_KV):
        vv = np.arange(sv0 + g * SWA_D, sv0 + (g + 1) * SWA_D)
        cols += [vv, z(LANES - SWA_D), z(LANES - SWA_D), vv]
    return np.concatenate(cols)


def _even_weights(w_in, w_qb, w_kvb):
    return (_take_cols(w_in, _even_in_cols()), _take_cols(w_qb, _mla_q_cols()), _take_cols(w_kvb, _mla_kv_cols()))


def _even_mixer(stream, tabs, weights, q_norm, kv_norm, conv_w, a_log, dt_bias, o_norm, batch, seq):
    ctab, stab = tabs
    w, wq2, wkv2 = weights
    x, q, k, v, act, z, gates = _proj_even(stream, w, conv_w, ctab, stab, q_norm.reshape(1, -1), kv_norm.reshape(1, -1),
                                           wq2, wkv2, seq)
    o_a = _mla_attn(q, k, v, batch, seq)
    o_b = _gdn(act, gates, z, _pad_lane_row(a_log, GDN_H), _pad_lane_row(dt_bias, GDN_H),
               o_norm.reshape(1, -1), batch, seq)
    return x, o_a, o_b


def _odd_mixer(stream, tabs, w, b_i, b_f, ml_norm, sinks, batch, seq):
    ctab, stab = tabs
    x, mq, mk, mv, mo, mg, sq, sk, sv = _proj_odd(stream, w, ctab, stab)
    bias_row = _pad_lane_row(jnp.concatenate([b_i, b_f]), 0)
    o_c = _mlstm(mq, mk, mv, mo, mg, bias_row, ml_norm.reshape(1, -1), batch, seq)
    o_d = _swa(sq, sk, sv, _pad_lane_row(sinks, 0), batch, seq)
    return x, o_c, o_d


def _moe(x, xp, routing, w_gate, w_up, w_down, layer, s_gate, s_up, s_down, ln_g, ln_b):
    t, d = x.shape
    idx, gate, rank, cnt = routing
    counts = cnt[:, 0].astype(jnp.int32)
    block = int(min(max(pl.next_power_of_2(t * TOP_K // N_EXPERTS) // 2, EXPERT_BLOCK_MIN), EXPERT_BLOCK_MAX))
    padded = (counts + block - 1) // block * block
    pad_end = jnp.cumsum(padded)
    pad_start = pad_end - padded
    start_col = jnp.broadcast_to(pad_start.astype(F32).reshape(-1, 1), (N_EXPERTS, LANES))
    dest = _dest_rows(idx, rank, start_col)
    n_blocks = t * TOP_K // block + N_EXPERTS
    rows = n_blocks * block
    block_row = jnp.arange(n_blocks, dtype=jnp.int32) * block
    block_e = jnp.minimum(jnp.sum((pad_end[None, :] <= block_row[:, None]).astype(jnp.int32), axis=1), N_EXPERTS - 1)
    n_used = (pad_end[-1:] // block).astype(jnp.int32)
    live_end = jnp.sum(jnp.where(block_e[:, None] == jnp.arange(N_EXPERTS, dtype=jnp.int32)[None, :],
                                 (pad_start + counts)[None, :], 0), axis=1)
    n_valid = jnp.clip(live_end - block_row, 0, block).astype(jnp.int32)
    xs = _sc_scatter_rows(xp, dest, rows)
    ys = _experts(block_e, n_used, n_valid, xs, w_gate, w_up, w_down, layer, block)
    picked = _sc_gather_rows(ys, dest.reshape(-1)).reshape(TOP_K, t, d // 2)
    shared = _shared_expert(xp, s_gate.astype(BF16), s_up.astype(BF16), s_down.astype(BF16))
    return (x, gate.T, picked, shared, ln_g.reshape(1, -1), ln_b.reshape(1, -1))


def kernel(x, positions, ev_w_in, mla_q_norm, mla_w_qb, mla_kv_norm, mla_w_kvb, gdn_conv, gdn_a_log, gdn_dt_bias, gdn_norm, ev_w_out, od_w_in, mlstm_b_i, mlstm_b_f, mlstm_norm, swa_sinks, od_w_out, ln1_g, ln1_b, router_w, router_b, moe_w_gate, moe_w_up, moe_w_down, shared_w_gate, shared_w_up, shared_w_down, ln2_g, ln2_b):
    batch, seq, d = x.shape
    streams = STREAMS if batch % STREAMS == 0 else 1
    sb = batch // streams
    ts = sb * seq
    hs, tabs_m, tabs_s = [], [], []
    for s in range(streams):
        pos = positions[s * sb:(s + 1) * sb].reshape(ts, 1).astype(F32)
        tm_, ts_ = _rope_tables(pos)
        tabs_m.append(tm_)
        tabs_s.append(ts_)
        hs.append((x[s * sb:(s + 1) * sb].reshape(ts, d),))
    for layer in range(DEPTH):
        j = layer // 2
        if layer % 2 == 0:
            weights = _even_weights(ev_w_in[j], mla_w_qb[j], mla_w_kvb[j])
            w_out = ev_w_out[j].astype(BF16)
        else:
            weights = _take_cols(od_w_in[j], _odd_in_cols())
            w_out = od_w_out[j].astype(BF16)
        for s in range(streams):
            if layer % 2 == 0:
                h, a1, a2 = _even_mixer(hs[s], tabs_m[s], weights, mla_q_norm[j], mla_kv_norm[j], gdn_conv[j], gdn_a_log[j],
                                        gdn_dt_bias[j], gdn_norm[j], sb, seq)
            else:
                h, a1, a2 = _odd_mixer(hs[s], tabs_s[s], weights, mlstm_b_i[j], mlstm_b_f[j], mlstm_norm[j], swa_sinks[j],
                                       sb, seq)
            bias_col = jnp.broadcast_to(router_b[layer].reshape(-1, 1).astype(F32), (N_EXPERTS, LANES))
            h, hp, *routing = _outproj_ln_route(h, a1, a2, w_out, ln1_g[layer].reshape(1, -1), ln1_b[layer].reshape(1, -1),
                                                router_w[layer].T, bias_col)
            hs[s] = _moe(h, hp, routing, moe_w_gate, moe_w_up, moe_w_down, layer,
                         shared_w_gate[layer], shared_w_up[layer], shared_w_down[layer], ln2_g[layer], ln2_b[layer])
    return jnp.concatenate([_combine(*h).reshape(sb, seq, d) for h in hs], axis=0)
```
